```python
import jax, jax.numpy as jnp
from jax import lax
import numpy as np

D_MODEL = 2048
BATCH = 8
SEQ = 4096
DEPTH = 1

CONV_DIM = 1024
CONV_GROUPS = 8
CONV_WIDTH = 3
N_HEADS = 16
QK_NOPE_DIM = 128
QK_ROPE_DIM = 64
QK_HEAD_DIM = QK_NOPE_DIM + QK_ROPE_DIM
V_HEAD_DIM = 128
Q_LORA_RANK = 768
KV_LORA_RANK = 512
ROPE_THETA = 10000.0
Q_BLOCK = 128
D_FF = 5632
NORM_EPS = 1e-6
NEG_INF = -1e30

IN_SIZES = (CONV_DIM, CONV_DIM, CONV_DIM, Q_LORA_RANK, KV_LORA_RANK, QK_ROPE_DIM, 2 * D_MODEL)
IN_WIDTH = sum(IN_SIZES)
IN_SPLIT_IDX = tuple(int(i) for i in np.cumsum(IN_SIZES)[:-1])

kernel_name = "hybrid_gated_conv_mla_convffn_block"


def rmsnorm(x, g):
    xf = x.astype(jnp.float32)
    r = lax.rsqrt(jnp.mean(xf * xf, axis=-1, keepdims=True) + NORM_EPS)
    return (xf * r).astype(x.dtype) * g


def causal_dwconv3(u, w):
    s = u.shape[1]
    up = jnp.pad(u, ((0, 0), (CONV_WIDTH - 1, 0), (0, 0)))
    y = w[0] * up[:, 0:s]
    for k in range(1, CONV_WIDTH):
        y = y + w[k] * up[:, k:k + s]
    return y


def rope_tables(positions, dtype):
    inv_freq = ROPE_THETA ** (-jnp.arange(0, QK_ROPE_DIM, 2, dtype=jnp.float32) / QK_ROPE_DIM)
    ang = positions.astype(jnp.float32)[..., None] * inv_freq
    cos = jnp.cos(ang)[:, :, None, :].astype(dtype)
    sin = jnp.sin(ang)[:, :, None, :].astype(dtype)
    return cos, sin


def apply_rope_tail(x, cos, sin):
    x_nope = x[..., :QK_NOPE_DIM]
    x_r = x[..., QK_NOPE_DIM:]
    x1, x2 = jnp.split(x_r, 2, axis=-1)
    rot = jnp.concatenate([x1 * cos - x2 * sin, x2 * cos + x1 * sin], axis=-1)
    return jnp.concatenate([x_nope, rot], axis=-1)


def causal_block_attention(q, k, v):
    b, s, h, dk = q.shape
    dv = v.shape[-1]
    nb = s // Q_BLOCK
    scale = dk ** -0.5
    qb = q.reshape(b, nb, Q_BLOCK, h, dk).transpose(1, 0, 3, 2, 4)
    kh = k.transpose(0, 2, 1, 3)
    vh = v.transpose(0, 2, 1, 3)
    kpos = jnp.arange(s)

    def one_block(args):
        q_blk, blk = args
        sc = jnp.einsum('bhqd,bhkd->bhqk', q_blk, kh).astype(jnp.float32) * scale
        qpos = blk * Q_BLOCK + jnp.arange(Q_BLOCK)
        mask = kpos[None, :] <= qpos[:, None]
        sc = jnp.where(mask, sc, NEG_INF)
        p = jax.nn.softmax(sc, axis=-1).astype(vh.dtype)
        return jnp.einsum('bhqk,bhkd->bhqd', p, vh)

    o = lax.map(one_block, (qb, jnp.arange(nb)))
    return o.transpose(1, 0, 3, 2, 4).reshape(b, s, h * dv)


def _fwd_setup_inputs(seed: int = 0) -> dict:
    key = jax.random.key(seed)
    ks = jax.random.split(key, 20)

    def dense(k, fan_in, fan_out):
        return jax.random.normal(k, (DEPTH, fan_in, fan_out), jnp.float32) * fan_in ** -0.5

    def gain(k, n):
        return 1.0 + 0.02 * jax.random.normal(k, (DEPTH, n), jnp.float32)

    x = jax.random.normal(ks[0], (BATCH, SEQ, D_MODEL), jnp.float32)
    positions = jnp.broadcast_to(jnp.arange(SEQ, dtype=jnp.int32), (BATCH, SEQ))
    return {
        "x": x,
        "positions": positions,
        "ln1_g": gain(ks[1], D_MODEL),
        "w_in": dense(ks[2], D_MODEL, IN_WIDTH),
        "b_gate": 0.01 * jax.random.normal(ks[3], (DEPTH, 2 * D_MODEL), jnp.float32),
        "conv_w": jax.random.normal(ks[4], (DEPTH, CONV_WIDTH, CONV_DIM), jnp.float32) * CONV_WIDTH ** -0.5,
        "w_conv_out": dense(ks[5], CONV_DIM, D_MODEL),
        "q_a_g": gain(ks[6], Q_LORA_RANK),
        "w_q_b": dense(ks[7], Q_LORA_RANK, N_HEADS * QK_HEAD_DIM),
        "kv_a_g": gain(ks[8], KV_LORA_RANK),
        "w_kv_b": dense(ks[9], KV_LORA_RANK, N_HEADS * (QK_NOPE_DIM + V_HEAD_DIM)),
        "q_norm_g": gain(ks[10], QK_HEAD_DIM),
        "k_norm_g": gain(ks[11], QK_HEAD_DIM),
        "w_mla_out": dense(ks[12], N_HEADS * V_HEAD_DIM, D_MODEL),
        "w_o": dense(ks[13], D_MODEL, D_MODEL),
        "ln2_g": gain(ks[14], D_MODEL),
        "w_ffn_up": dense(ks[15], D_MODEL, 2 * D_FF),
        "ffn_conv_w": jax.random.normal(ks[16], (DEPTH, CONV_WIDTH, 2 * D_FF), jnp.float32) * CONV_WIDTH ** -0.5,
        "ffn_conv_b": 0.01 * jax.random.normal(ks[17], (DEPTH, 2 * D_FF), jnp.float32),
        "w_ffn_down": dense(ks[18], D_FF, D_MODEL),
    }


def _fwd_reference(x, positions, ln1_g, w_in, b_gate, conv_w, w_conv_out, q_a_g, w_q_b, kv_a_g, w_kv_b,
              q_norm_g, k_norm_g, w_mla_out, w_o, ln2_g, w_ffn_up, ffn_conv_w, ffn_conv_b, w_ffn_down):
    b, s, _ = x.shape
    cos, sin = rope_tables(positions, x.dtype)
    h = x
    for l in range(DEPTH):
        u = rmsnorm(h, ln1_g[l])
        z = u @ w_in[l]
        zb, zc, zv, q_lat, kv_lat, k_rope, gates = jnp.split(z, IN_SPLIT_IDX, axis=-1)
        gates = jax.nn.sigmoid(gates + b_gate[l])
        gate_a, gate_b = jnp.split(gates, 2, axis=-1)

        y_conv = (zb * causal_dwconv3(zc * zv, conv_w[l])) @ w_conv_out[l]

        q = (rmsnorm(q_lat, q_a_g[l]) @ w_q_b[l]).reshape(b, s, N_HEADS, QK_HEAD_DIM)
        kv = (rmsnorm(kv_lat, kv_a_g[l]) @ w_kv_b[l]).reshape(b, s, N_HEADS, QK_NOPE_DIM + V_HEAD_DIM)
        k_nope, v = jnp.split(kv, [QK_NOPE_DIM], axis=-1)
        k = jnp.concatenate(
            [k_nope, jnp.broadcast_to(k_rope[:, :, None, :], (b, s, N_HEADS, QK_ROPE_DIM))], axis=-1)
        q = apply_rope_tail(rmsnorm(q, q_norm_g[l]), cos, sin)
        k = apply_rope_tail(rmsnorm(k, k_norm_g[l]), cos, sin)
        y_mla = causal_block_attention(q, k, v) @ w_mla_out[l]

        h = h + (gate_a * y_conv + gate_b * y_mla) @ w_o[l]

        u = rmsnorm(h, ln2_g[l])
        a = causal_dwconv3(u @ w_ffn_up[l], ffn_conv_w[l]) + ffn_conv_b[l]
        a_gate, a_up = jnp.split(a, 2, axis=-1)
        h = h + (jax.nn.silu(a_gate) * a_up) @ w_ffn_down[l]
    return h


import jax as _jax
import jax.numpy as _jnp

TWIN_FORMAT = 'train_step'
FWD_PARAMS = ['x', 'positions', 'ln1_g', 'w_in', 'b_gate', 'conv_w', 'w_conv_out', 'q_a_g', 'w_q_b', 'kv_a_g', 'w_kv_b', 'q_norm_g', 'k_norm_g', 'w_mla_out', 'w_o', 'ln2_g', 'w_ffn_up', 'ffn_conv_w', 'ffn_conv_b', 'w_ffn_down']
TWIN_WEIGHTS = ['ln1_g', 'w_in', 'b_gate', 'conv_w', 'w_conv_out', 'q_a_g', 'w_q_b', 'kv_a_g', 'w_kv_b', 'q_norm_g', 'k_norm_g', 'w_mla_out', 'w_o', 'ln2_g', 'w_ffn_up', 'ffn_conv_w', 'ffn_conv_b', 'w_ffn_down']
TWIN_DIFF_INPUT = 'x'
TWIN_INPUTS = ['x', 'positions', 'ln1_g', 'w_in', 'b_gate', 'conv_w', 'w_conv_out', 'q_a_g', 'w_q_b', 'kv_a_g', 'w_kv_b', 'q_norm_g', 'k_norm_g', 'w_mla_out', 'w_o', 'ln2_g', 'w_ffn_up', 'ffn_conv_w', 'ffn_conv_b', 'w_ffn_down', 'loss_target', 'm_ln1_g', 'm_w_in', 'm_b_gate', 'm_conv_w', 'm_w_conv_out', 'm_q_a_g', 'm_w_q_b', 'm_kv_a_g', 'm_w_kv_b', 'm_q_norm_g', 'm_k_norm_g', 'm_w_mla_out', 'm_w_o', 'm_ln2_g', 'm_w_ffn_up', 'm_ffn_conv_w', 'm_ffn_conv_b', 'm_w_ffn_down', 'v_ln1_g', 'v_w_in', 'v_b_gate', 'v_conv_w', 'v_w_conv_out', 'v_q_a_g', 'v_w_q_b', 'v_kv_a_g', 'v_w_kv_b', 'v_q_norm_g', 'v_k_norm_g', 'v_w_mla_out', 'v_w_o', 'v_ln2_g', 'v_w_ffn_up', 'v_ffn_conv_w', 'v_ffn_conv_b', 'v_w_ffn_down']
TWIN_OUTPUTS = ['loss', 'grad_x', 'grad_ln1_g', 'grad_w_in', 'grad_b_gate', 'grad_conv_w', 'grad_w_conv_out', 'grad_q_a_g', 'grad_w_q_b', 'grad_kv_a_g', 'grad_w_kv_b', 'grad_q_norm_g', 'grad_k_norm_g', 'grad_w_mla_out', 'grad_w_o', 'grad_ln2_g', 'grad_w_ffn_up', 'grad_ffn_conv_w', 'grad_ffn_conv_b', 'grad_w_ffn_down', 'delta_ln1_g', 'delta_w_in', 'delta_b_gate', 'delta_conv_w', 'delta_w_conv_out', 'delta_q_a_g', 'delta_w_q_b', 'delta_kv_a_g', 'delta_w_kv_b', 'delta_q_norm_g', 'delta_k_norm_g', 'delta_w_mla_out', 'delta_w_o', 'delta_ln2_g', 'delta_w_ffn_up', 'delta_ffn_conv_w', 'delta_ffn_conv_b', 'delta_w_ffn_down', 'new_m_ln1_g', 'new_m_w_in', 'new_m_b_gate', 'new_m_conv_w', 'new_m_w_conv_out', 'new_m_q_a_g', 'new_m_w_q_b', 'new_m_kv_a_g', 'new_m_w_kv_b', 'new_m_q_norm_g', 'new_m_k_norm_g', 'new_m_w_mla_out', 'new_m_w_o', 'new_m_ln2_g', 'new_m_w_ffn_up', 'new_m_ffn_conv_w', 'new_m_ffn_conv_b', 'new_m_w_ffn_down', 'new_v_ln1_g', 'new_v_w_in', 'new_v_b_gate', 'new_v_conv_w', 'new_v_w_conv_out', 'new_v_q_a_g', 'new_v_w_q_b', 'new_v_kv_a_g', 'new_v_w_kv_b', 'new_v_q_norm_g', 'new_v_k_norm_g', 'new_v_w_mla_out', 'new_v_w_o', 'new_v_ln2_g', 'new_v_w_ffn_up', 'new_v_ffn_conv_w', 'new_v_ffn_conv_b', 'new_v_w_ffn_down']
TWIN_LEAF_KINDS = {'loss': 'loss', 'grad_x': 'grad_x', 'grad_ln1_g': 'grad_w', 'grad_w_in': 'grad_w', 'grad_b_gate': 'grad_w', 'grad_conv_w': 'grad_w', 'grad_w_conv_out': 'grad_w', 'grad_q_a_g': 'grad_w', 'grad_w_q_b': 'grad_w', 'grad_kv_a_g': 'grad_w', 'grad_w_kv_b': 'grad_w', 'grad_q_norm_g': 'grad_w', 'grad_k_norm_g': 'grad_w', 'grad_w_mla_out': 'grad_w', 'grad_w_o': 'grad_w', 'grad_ln2_g': 'grad_w', 'grad_w_ffn_up': 'grad_w', 'grad_ffn_conv_w': 'grad_w', 'grad_ffn_conv_b': 'grad_w', 'grad_w_ffn_down': 'grad_w', 'delta_ln1_g': 'delta_w', 'delta_w_in': 'delta_w', 'delta_b_gate': 'delta_w', 'delta_conv_w': 'delta_w', 'delta_w_conv_out': 'delta_w', 'delta_q_a_g': 'delta_w', 'delta_w_q_b': 'delta_w', 'delta_kv_a_g': 'delta_w', 'delta_w_kv_b': 'delta_w', 'delta_q_norm_g': 'delta_w', 'delta_k_norm_g': 'delta_w', 'delta_w_mla_out': 'delta_w', 'delta_w_o': 'delta_w', 'delta_ln2_g': 'delta_w', 'delta_w_ffn_up': 'delta_w', 'delta_ffn_conv_w': 'delta_w', 'delta_ffn_conv_b': 'delta_w', 'delta_w_ffn_down': 'delta_w', 'new_m_ln1_g': 'new_m', 'new_m_w_in': 'new_m', 'new_m_b_gate': 'new_m', 'new_m_conv_w': 'new_m', 'new_m_w_conv_out': 'new_m', 'new_m_q_a_g': 'new_m', 'new_m_w_q_b': 'new_m', 'new_m_kv_a_g': 'new_m', 'new_m_w_kv_b': 'new_m', 'new_m_q_norm_g': 'new_m', 'new_m_k_norm_g': 'new_m', 'new_m_w_mla_out': 'new_m', 'new_m_w_o': 'new_m', 'new_m_ln2_g': 'new_m', 'new_m_w_ffn_up': 'new_m', 'new_m_ffn_conv_w': 'new_m', 'new_m_ffn_conv_b': 'new_m', 'new_m_w_ffn_down': 'new_m', 'new_v_ln1_g': 'new_v', 'new_v_w_in': 'new_v', 'new_v_b_gate': 'new_v', 'new_v_conv_w': 'new_v', 'new_v_w_conv_out': 'new_v', 'new_v_q_a_g': 'new_v', 'new_v_w_q_b': 'new_v', 'new_v_kv_a_g': 'new_v', 'new_v_w_kv_b': 'new_v', 'new_v_q_norm_g': 'new_v', 'new_v_k_norm_g': 'new_v', 'new_v_w_mla_out': 'new_v', 'new_v_w_o': 'new_v', 'new_v_ln2_g': 'new_v', 'new_v_w_ffn_up': 'new_v', 'new_v_ffn_conv_w': 'new_v', 'new_v_ffn_conv_b': 'new_v', 'new_v_w_ffn_down': 'new_v'}


def _forward(args):
    return _fwd_reference(*[args[k] for k in FWD_PARAMS])


def _output_shape():
    def fwd():
        inp = _fwd_setup_inputs(0)
        return _fwd_reference(*[inp[k] for k in FWD_PARAMS])
    out = _jax.eval_shape(fwd)
    return out.shape, out.dtype

N_MICROBATCH = 1
ADAM_LR = 0.001
ADAM_B1 = 0.9
ADAM_B2 = 0.999
ADAM_EPS = 1e-08
ADAM_WD = 0.01
ADAM_STEP = 10
PER_EXAMPLE_BATCH_AXIS = {'x': 0, 'positions': 0, 'loss_target': 0}
SHARED_INPUTS = []
_WEIGHT_DTYPES = {'ln1_g': _jnp.float32, 'w_in': _jnp.float32, 'b_gate': _jnp.float32, 'conv_w': _jnp.float32, 'w_conv_out': _jnp.float32, 'q_a_g': _jnp.float32, 'w_q_b': _jnp.float32, 'kv_a_g': _jnp.float32, 'w_kv_b': _jnp.float32, 'q_norm_g': _jnp.float32, 'k_norm_g': _jnp.float32, 'w_mla_out': _jnp.float32, 'w_o': _jnp.float32, 'ln2_g': _jnp.float32, 'w_ffn_up': _jnp.float32, 'ffn_conv_w': _jnp.float32, 'ffn_conv_b': _jnp.float32, 'w_ffn_down': _jnp.float32}
MOMENT_SCALE = {'ln1_g': 1.464334e+01, 'w_in': 1.911171e-01, 'b_gate': 1.174789e+00, 'conv_w': 5.411871e+00, 'w_conv_out': 2.345159e-01, 'q_a_g': 3.744741e-02, 'w_q_b': 1.872490e-02, 'kv_a_g': 2.388859e-01, 'w_kv_b': 2.737885e-02, 'q_norm_g': 3.323786e-01, 'k_norm_g': 3.323685e-01, 'w_mla_out': 3.310502e-02, 'w_o': 1.988099e-01, 'ln2_g': 1.294799e+01, 'w_ffn_up': 1.108174e-01, 'ffn_conv_w': 1.815631e+00, 'ffn_conv_b': 1.591354e+00, 'w_ffn_down': 1.445486e-01}


def _to_microbatches(a, axis):
    t = _jnp.moveaxis(a, axis, 0)
    t = t.reshape((N_MICROBATCH, t.shape[0] // N_MICROBATCH) + t.shape[1:])
    return _jnp.moveaxis(t, 1, axis + 1)


def setup_inputs(seed: int = 0) -> dict:
    inp = _fwd_setup_inputs(seed)
    key = _jax.random.fold_in(_jax.random.key(seed), 7919)
    shape, _ = _output_shape()
    out = dict(inp)
    out["loss_target"] = _jax.random.normal(_jax.random.fold_in(key, 0), shape, _jnp.float32)
    for i, name in enumerate(TWIN_WEIGHTS):
        w = inp[name].astype(_jnp.float32)
        if MOMENT_SCALE is None:
            s = _jnp.sqrt(_jnp.mean(_jnp.square(w)) + 1e-30)
        else:
            s = MOMENT_SCALE[name]
        km, kv = _jax.random.split(_jax.random.fold_in(key, i + 1))
        out[name] = w
        out["m_" + name] = s * _jax.random.normal(km, w.shape, _jnp.float32)
        out["v_" + name] = (s * s) * _jax.random.uniform(kv, w.shape, _jnp.float32, 0.5, 1.5)
    if N_MICROBATCH > 1:
        for name, axis in PER_EXAMPLE_BATCH_AXIS.items():
            out[name] = _to_microbatches(out[name], axis)
    return {'x': out['x'], 'positions': out['positions'], 'ln1_g': out['ln1_g'], 'w_in': out['w_in'], 'b_gate': out['b_gate'], 'conv_w': out['conv_w'], 'w_conv_out': out['w_conv_out'], 'q_a_g': out['q_a_g'], 'w_q_b': out['w_q_b'], 'kv_a_g': out['kv_a_g'], 'w_kv_b': out['w_kv_b'], 'q_norm_g': out['q_norm_g'], 'k_norm_g': out['k_norm_g'], 'w_mla_out': out['w_mla_out'], 'w_o': out['w_o'], 'ln2_g': out['ln2_g'], 'w_ffn_up': out['w_ffn_up'], 'ffn_conv_w': out['ffn_conv_w'], 'ffn_conv_b': out['ffn_conv_b'], 'w_ffn_down': out['w_ffn_down'], 'loss_target': out['loss_target'], 'm_ln1_g': out['m_ln1_g'], 'm_w_in': out['m_w_in'], 'm_b_gate': out['m_b_gate'], 'm_conv_w': out['m_conv_w'], 'm_w_conv_out': out['m_w_conv_out'], 'm_q_a_g': out['m_q_a_g'], 'm_w_q_b': out['m_w_q_b'], 'm_kv_a_g': out['m_kv_a_g'], 'm_w_kv_b': out['m_w_kv_b'], 'm_q_norm_g': out['m_q_norm_g'], 'm_k_norm_g': out['m_k_norm_g'], 'm_w_mla_out': out['m_w_mla_out'], 'm_w_o': out['m_w_o'], 'm_ln2_g': out['m_ln2_g'], 'm_w_ffn_up': out['m_w_ffn_up'], 'm_ffn_conv_w': out['m_ffn_conv_w'], 'm_ffn_conv_b': out['m_ffn_conv_b'], 'm_w_ffn_down': out['m_w_ffn_down'], 'v_ln1_g': out['v_ln1_g'], 'v_w_in': out['v_w_in'], 'v_b_gate': out['v_b_gate'], 'v_conv_w': out['v_conv_w'], 'v_w_conv_out': out['v_w_conv_out'], 'v_q_a_g': out['v_q_a_g'], 'v_w_q_b': out['v_w_q_b'], 'v_kv_a_g': out['v_kv_a_g'], 'v_w_kv_b': out['v_w_kv_b'], 'v_q_norm_g': out['v_q_norm_g'], 'v_k_norm_g': out['v_k_norm_g'], 'v_w_mla_out': out['v_w_mla_out'], 'v_w_o': out['v_w_o'], 'v_ln2_g': out['v_ln2_g'], 'v_w_ffn_up': out['v_w_ffn_up'], 'v_ffn_conv_w': out['v_ffn_conv_w'], 'v_ffn_conv_b': out['v_ffn_conv_b'], 'v_w_ffn_down': out['v_w_ffn_down']}


def _loss(weights, diff, rest, loss_target):
    with _jax.named_scope("forward"):
        args = {**rest, TWIN_DIFF_INPUT: diff, **{k: w.astype(_WEIGHT_DTYPES[k]) for k, w in weights.items()}}
        y = _forward(args)
    with _jax.named_scope("loss_head"):
        err = _jnp.square(y.astype(_jnp.float32) - loss_target)
        return 0.5 * _jnp.sum(_jnp.mean(err, axis=-1)) if err.ndim else 0.5 * err


def _adamw(w, g, m, v):
    m = ADAM_B1 * m + (1.0 - ADAM_B1) * g
    v = ADAM_B2 * v + (1.0 - ADAM_B2) * _jnp.square(g)
    m_hat = m / (1.0 - ADAM_B1 ** ADAM_STEP)
    v_hat = v / (1.0 - ADAM_B2 ** ADAM_STEP)
    delta = -ADAM_LR * (m_hat / (_jnp.sqrt(v_hat) + ADAM_EPS) + ADAM_WD * w)
    return delta, m, v


def reference(x, positions, ln1_g, w_in, b_gate, conv_w, w_conv_out, q_a_g, w_q_b, kv_a_g, w_kv_b, q_norm_g, k_norm_g, w_mla_out, w_o, ln2_g, w_ffn_up, ffn_conv_w, ffn_conv_b, w_ffn_down, loss_target, m_ln1_g, m_w_in, m_b_gate, m_conv_w, m_w_conv_out, m_q_a_g, m_w_q_b, m_kv_a_g, m_w_kv_b, m_q_norm_g, m_k_norm_g, m_w_mla_out, m_w_o, m_ln2_g, m_w_ffn_up, m_ffn_conv_w, m_ffn_conv_b, m_w_ffn_down, v_ln1_g, v_w_in, v_b_gate, v_conv_w, v_w_conv_out, v_q_a_g, v_w_q_b, v_kv_a_g, v_w_kv_b, v_q_norm_g, v_k_norm_g, v_w_mla_out, v_w_o, v_ln2_g, v_w_ffn_up, v_ffn_conv_w, v_ffn_conv_b, v_w_ffn_down):
    given = dict(x=x, positions=positions, ln1_g=ln1_g, w_in=w_in, b_gate=b_gate, conv_w=conv_w, w_conv_out=w_conv_out, q_a_g=q_a_g, w_q_b=w_q_b, kv_a_g=kv_a_g, w_kv_b=w_kv_b, q_norm_g=q_norm_g, k_norm_g=k_norm_g, w_mla_out=w_mla_out, w_o=w_o, ln2_g=ln2_g, w_ffn_up=w_ffn_up, ffn_conv_w=ffn_conv_w, ffn_conv_b=ffn_conv_b, w_ffn_down=w_ffn_down, loss_target=loss_target, m_ln1_g=m_ln1_g, m_w_in=m_w_in, m_b_gate=m_b_gate, m_conv_w=m_conv_w, m_w_conv_out=m_w_conv_out, m_q_a_g=m_q_a_g, m_w_q_b=m_w_q_b, m_kv_a_g=m_kv_a_g, m_w_kv_b=m_w_kv_b, m_q_norm_g=m_q_norm_g, m_k_norm_g=m_k_norm_g, m_w_mla_out=m_w_mla_out, m_w_o=m_w_o, m_ln2_g=m_ln2_g, m_w_ffn_up=m_w_ffn_up, m_ffn_conv_w=m_ffn_conv_w, m_ffn_conv_b=m_ffn_conv_b, m_w_ffn_down=m_w_ffn_down, v_ln1_g=v_ln1_g, v_w_in=v_w_in, v_b_gate=v_b_gate, v_conv_w=v_conv_w, v_w_conv_out=v_w_conv_out, v_q_a_g=v_q_a_g, v_w_q_b=v_w_q_b, v_kv_a_g=v_kv_a_g, v_w_kv_b=v_w_kv_b, v_q_norm_g=v_q_norm_g, v_k_norm_g=v_k_norm_g, v_w_mla_out=v_w_mla_out, v_w_o=v_w_o, v_ln2_g=v_ln2_g, v_w_ffn_up=v_w_ffn_up, v_ffn_conv_w=v_ffn_conv_w, v_ffn_conv_b=v_ffn_conv_b, v_w_ffn_down=v_w_ffn_down)
    weights = {n: given[n] for n in TWIN_WEIGHTS}
    shared = {n: given[n] for n in SHARED_INPUTS}
    per_example = {n: given[n] for n in ['x', 'positions']}
    grad_fn = _jax.value_and_grad(_loss, argnums=(0, 1))

    def one_microbatch(ex, loss_target):
        ex = dict(ex)
        diff = ex.pop(TWIN_DIFF_INPUT)
        return grad_fn(weights, diff, {**shared, **ex}, loss_target)

    if N_MICROBATCH == 1:
        loss, (grad_w, grad_x) = one_microbatch(per_example, given["loss_target"])
    else:
        def body(carry, xs):
            loss_sum, grad_sum = carry
            l_k, (gw_k, gx_k) = one_microbatch(xs[0], xs[1])
            with _jax.named_scope("update"):
                return (loss_sum + l_k, _jax.tree.map(_jnp.add, grad_sum, gw_k)), gx_k

        init = (_jnp.zeros((), _jnp.float32), _jax.tree.map(_jnp.zeros_like, weights))
        (loss, grad_w), grad_x = _jax.lax.scan(body, init, (per_example, given["loss_target"]))
    with _jax.named_scope("update"):
        delta_w, new_m, new_v = {}, {}, {}
        for n in TWIN_WEIGHTS:
            delta_w[n], new_m[n], new_v[n] = _adamw(weights[n], grad_w[n], given["m_" + n], given["v_" + n])
    return (loss, grad_x, *[grad_w[n] for n in TWIN_WEIGHTS], *[delta_w[n] for n in TWIN_WEIGHTS],
            *[new_m[n] for n in TWIN_WEIGHTS], *[new_v[n] for n in TWIN_WEIGHTS])
```

```python
import functools

import jax
import jax.numpy as jnp
from jax import lax
from jax.experimental import pallas as pl
from jax.experimental.pallas import tpu as pltpu

BF = jnp.bfloat16
F32 = jnp.float32
MESH = pl.DeviceIdType.MESH
N_DEV = 8
N_CHIP = 4

NOPE = 128
ROPE = 64
HALF = ROPE // 2
HEAD_QK = NOPE + ROPE
HEAD_V = 128
LANE = 128
SUB = 8
NORM_EPS = 1e-6
NEG_INF = -1e30
ROPE_THETA = 10000.0
ADAM_LR = 0.001
ADAM_B1 = 0.9
ADAM_B2 = 0.999
ADAM_EPS = 1e-08
ADAM_WD = 0.01
ADAM_STEP = 10

VMEM_LIMIT = 52 * 1024 * 1024
MM_TM, MM_TN, MM_TK, MM_TS = 1024, 1536, 1024, 512
ROW_TILE, ROW_TILE_BWD = 512, 256
HEAD_ROW_TILE, HEAD_ROW_TILE_BWD = 256, 128
COL_TILE = 512
ATTN_TILE = 512
ANY = pl.BlockSpec(memory_space=pl.ANY)


def _pick(n, target, mult):
    t = (min(n, target) // mult) * mult
    while t > 0:
        if n % t == 0:
            return t
        t -= mult
    raise ValueError(f"no tile for {n} (target {target}, multiple {mult})")


def _cp(*sem):
    return pltpu.CompilerParams(dimension_semantics=sem, vmem_limit_bytes=VMEM_LIMIT)


def _mm_nn(a, b3, name, add=None, out_dtype=F32, blk0=0, nblk=None):
    m, k = a.shape
    nb_all, k2, nbw = b3.shape
    assert k == k2
    nblk = nb_all - blk0 if nblk is None else nblk
    n = nblk * nbw
    tm = _pick(m, MM_TM, 16)
    tn = _pick(nbw, MM_TN, LANE)
    tk = _pick(k, MM_TK, LANE)
    per = nbw // tn
    nk = k // tk

    def body(*refs):
        if add is None:
            a_ref, b_ref, o_ref, acc = refs
        else:
            a_ref, b_ref, c_ref, o_ref, acc = refs
        kk = pl.program_id(2)

        @pl.when(kk == 0)
        def _():
            acc[...] = jnp.zeros_like(acc)

        acc[...] += jnp.dot(a_ref[...].astype(BF), b_ref[...].astype(BF), preferred_element_type=F32)

        @pl.when(kk == nk - 1)
        def _():
            r = acc[...]
            if add is not None:
                r = r + c_ref[...]
            o_ref[...] = r.astype(out_dtype)

    in_specs = [pl.BlockSpec((tm, tk), lambda i, j, kk: (i, kk)),
                pl.BlockSpec((None, tk, tn), lambda i, j, kk: (blk0 + j // per, kk, j % per))]
    args = [a, b3]
    if add is not None:
        in_specs.append(pl.BlockSpec((tm, tn), lambda i, j, kk: (i, j)))
        args.append(add)
    return pl.pallas_call(
        body, name=name, grid=(m // tm, n // tn, nk),
        in_specs=in_specs, out_specs=pl.BlockSpec((tm, tn), lambda i, j, kk: (i, j)),
        out_shape=jax.ShapeDtypeStruct((m, n), out_dtype),
        scratch_shapes=[pltpu.VMEM((tm, tn), F32)],
        compiler_params=_cp("parallel", "parallel", "arbitrary"),
    )(*args)


def _mm_nt(a, b3, name, add=None, out_dtype=F32, blk0=0, nblk=None):
    m, n = a.shape
    nb_all, k, nbw = b3.shape
    nblk = nb_all - blk0 if nblk is None else nblk
    assert n == nblk * nbw
    tm = _pick(m, MM_TM, 16)
    tn = _pick(k, MM_TK, LANE)
    tk = _pick(nbw, MM_TN, LANE)
    per = nbw // tk
    nk = n // tk

    def body(*refs):
        if add is None:
            a_ref, b_ref, o_ref, acc = refs
        else:
            a_ref, b_ref, c_ref, o_ref, acc = refs
        kk = pl.program_id(2)

        @pl.when(kk == 0)
        def _():
            acc[...] = jnp.zeros_like(acc)

        acc[...] += lax.dot_general(a_ref[...].astype(BF), b_ref[...].astype(BF),
                                    (((1,), (1,)), ((), ())), preferred_element_type=F32)

        @pl.when(kk == nk - 1)
        def _():
            r = acc[...]
            if add is not None:
                r = r + c_ref[...]
            o_ref[...] = r.astype(out_dtype)

    in_specs = [pl.BlockSpec((tm, tk), lambda i, j, kk: (i, kk)),
                pl.BlockSpec((None, tn, tk), lambda i, j, kk: (blk0 + kk // per, j, kk % per))]
    args = [a, b3]
    if add is not None:
        in_specs.append(pl.BlockSpec((tm, tn), lambda i, j, kk: (i, j)))
        args.append(add)
    return pl.pallas_call(
        body, name=name, grid=(m // tm, k // tn, nk),
        in_specs=in_specs, out_specs=pl.BlockSpec((tm, tn), lambda i, j, kk: (i, j)),
        out_shape=jax.ShapeDtypeStruct((m, k), out_dtype),
        scratch_shapes=[pltpu.VMEM((tm, tn), F32)],
        compiler_params=_cp("parallel", "parallel", "arbitrary"),
    )(*args)


def _mm_tn(a, b, nblk, name, out_dtype=BF):
    s, m = a.shape
    s2, n = b.shape
    assert s == s2 and n % nblk == 0
    nbw = n // nblk
    tm = _pick(m, MM_TM, LANE)
    tn = _pick(nbw, MM_TN, LANE)
    ts = _pick(s, MM_TS, LANE)
    per = nbw // tn
    ns = s // ts

    def body(a_ref, b_ref, o_ref, acc):
        ss = pl.program_id(2)

        @pl.when(ss == 0)
        def _():
            acc[...] = jnp.zeros_like(acc)

        acc[...] += lax.dot_general(a_ref[...].astype(BF), b_ref[...].astype(BF),
                                    (((0,), (0,)), ((), ())), preferred_element_type=F32)

        @pl.when(ss == ns - 1)
        def _():
            o_ref[...] = acc[...].astype(out_dtype)

    return pl.pallas_call(
        body, name=name, grid=(m // tm, n // tn, ns),
        in_specs=[pl.BlockSpec((ts, tm), lambda i, j, ss: (ss, i)),
                  pl.BlockSpec((ts, tn), lambda i, j, ss: (ss, j))],
        out_specs=pl.BlockSpec((None, tm, tn), lambda i, j, ss: (j // per, i, j % per)),
        out_shape=jax.ShapeDtypeStruct((nblk, m, nbw), out_dtype),
        scratch_shapes=[pltpu.VMEM((tm, tn), F32)],
        compiler_params=_cp("parallel", "parallel", "arbitrary"),
    )(a, b)


def _rows8(rows, width):
    idx = lax.broadcasted_iota(jnp.int32, (SUB, width), 0)
    out = jnp.zeros((SUB, width), F32)
    for r, v in enumerate(rows):
        out = jnp.where(idx == r, v, out)
    return out


def _rms_fwd(x, g, width, col_blk, name):
    s = x.shape[0]
    tr = _pick(s, ROW_TILE, 16)

    def body(x_ref, g_ref, u_ref):
        xv = x_ref[...]
        r = lax.rsqrt(jnp.mean(xv * xv, axis=-1, keepdims=True) + NORM_EPS)
        u_ref[...] = ((xv * r) * g_ref[...]).astype(BF)

    return pl.pallas_call(
        body, name=name, grid=(s // tr,),
        in_specs=[pl.BlockSpec((tr, width), lambda i: (i, col_blk)),
                  pl.BlockSpec((1, width), lambda i: (0, 0))],
        out_specs=pl.BlockSpec((tr, width), lambda i: (i, 0)),
        out_shape=jax.ShapeDtypeStruct((s, width), BF),
        compiler_params=_cp("parallel"),
    )(x, g)


def _rms_bwd(x, du, g, width, col_blk, name, extra=None, out_dtype=F32):
    s = x.shape[0]
    tr = _pick(s, ROW_TILE_BWD, 16)

    def body(*refs):
        if extra is None:
            x_ref, du_ref, g_ref, dx_ref, dg_ref = refs
        else:
            x_ref, du_ref, g_ref, e_ref, dx_ref, dg_ref = refs
        i = pl.program_id(0)
        xv = x_ref[...]
        duv = du_ref[...].astype(F32)
        r = lax.rsqrt(jnp.mean(xv * xv, axis=-1, keepdims=True) + NORM_EPS)
        nv = xv * r
        dn = duv * g_ref[...]
        dx = r * (dn - nv * jnp.mean(dn * nv, axis=-1, keepdims=True))
        if extra is not None:
            dx = dx + e_ref[...]
        dx_ref[...] = dx.astype(out_dtype)

        @pl.when(i == 0)
        def _():
            dg_ref[...] = jnp.zeros_like(dg_ref)

        dg_ref[...] += _rows8([jnp.sum(duv * nv, axis=0, keepdims=True)], width)

    in_specs = [pl.BlockSpec((tr, width), lambda i: (i, col_blk)),
                pl.BlockSpec((tr, width), lambda i: (i, 0)),
                pl.BlockSpec((1, width), lambda i: (0, 0))]
    args = [x, du, g]
    if extra is not None:
        in_specs.append(pl.BlockSpec((tr, width), lambda i: (i, 0)))
        args.append(extra)
    return pl.pallas_call(
        body, name=name, grid=(s // tr,),
        in_specs=in_specs,
        out_specs=[pl.BlockSpec((tr, width), lambda i: (i, 0)),
                   pl.BlockSpec((SUB, width), lambda i: (0, 0))],
        out_shape=[jax.ShapeDtypeStruct((s, width), out_dtype), jax.ShapeDtypeStruct((SUB, width), F32)],
        compiler_params=_cp("arbitrary"),
    )(*args)


def _down(cur, prev8, k):
    ext = jnp.concatenate([prev8, cur], axis=0)
    return pltpu.roll(ext, k, axis=0)[SUB:]


def _up(cur, next8, k):
    ext = jnp.concatenate([cur, next8], axis=0)
    return pltpu.roll(ext, ext.shape[0] - k, axis=0)[:cur.shape[0]]


def _conv3(w_ref, cur, prev8):
    return w_ref[0:1, :] * _down(cur, prev8, 2) + w_ref[1:2, :] * _down(cur, prev8, 1) + w_ref[2:3, :] * cur


def _conv3_t(w_ref, cur, next8):
    return w_ref[2:3, :] * cur + w_ref[1:2, :] * _up(cur, next8, 1) + w_ref[0:1, :] * _up(cur, next8, 2)


def _spec_cur(tr, tc, c0):
    return pl.BlockSpec((tr, tc), lambda j, i: (i, c0 + j))


def _spec_prev(tr, tc, c0):
    return pl.BlockSpec((SUB, tc), lambda j, i: (jnp.maximum(i * (tr // SUB) - 1, 0), c0 + j))


def _spec_next(tr, tc, c0, s):
    return pl.BlockSpec((SUB, tc), lambda j, i: (jnp.minimum((i + 1) * (tr // SUB), s // SUB - 1), c0 + j))


def _spec_w(tc, c0):
    return pl.BlockSpec((SUB, tc), lambda j, i: (0, c0 + j))


def _pad8(w):
    return jnp.pad(w, ((0, SUB - w.shape[0]), (0, 0)))


def _conv_mix_fwd(z_a, cw8, conv):
    s = z_a.shape[0]
    tr = _pick(s, ROW_TILE, 16)
    tc = _pick(conv, COL_TILE, LANE)
    nc = conv // tc

    def body(zb_ref, zc_ref, zv_ref, zcp_ref, zvp_ref, w_ref, p_ref):
        i = pl.program_id(1)
        cv = zc_ref[...] * zv_ref[...]
        cvp = jnp.where(i > 0, zcp_ref[...] * zvp_ref[...], 0.0)
        p_ref[...] = (zb_ref[...] * _conv3(w_ref, cv, cvp)).astype(BF)

    return pl.pallas_call(
        body, name="conv_mix_fwd", grid=(nc, s // tr),
        in_specs=[_spec_cur(tr, tc, 0), _spec_cur(tr, tc, nc), _spec_cur(tr, tc, 2 * nc),
                  _spec_prev(tr, tc, nc), _spec_prev(tr, tc, 2 * nc), _spec_w(tc, 0)],
        out_specs=_spec_cur(tr, tc, 0),
        out_shape=jax.ShapeDtypeStruct((s, conv), BF),
        compiler_params=_cp("parallel", "parallel"),
    )(z_a, z_a, z_a, z_a, z_a, cw8)


def _conv_mix_bwd(z_a, d_p, cw8, conv):
    s = z_a.shape[0]
    tr = _pick(s, ROW_TILE_BWD, 16)
    tc = _pick(conv, COL_TILE, LANE)
    nc = conv // tc
    nr = s // tr

    def body(zb_ref, zbn_ref, zc_ref, zcp_ref, zv_ref, zvp_ref, dp_ref, dpn_ref, w_ref,
             dzb_ref, dzc_ref, dzv_ref, dw_ref):
        i = pl.program_id(1)
        zc = zc_ref[...]
        zv = zv_ref[...]
        cv = zc * zv
        cvp = jnp.where(i > 0, zcp_ref[...] * zvp_ref[...], 0.0)
        dpv = dp_ref[...]
        dzb_ref[...] = (dpv * _conv3(w_ref, cv, cvp)).astype(BF)
        dcc = dpv * zb_ref[...]
        dccn = jnp.where(i < nr - 1, dpn_ref[...] * zbn_ref[...], 0.0)
        dcv = _conv3_t(w_ref, dcc, dccn)
        dzc_ref[...] = (dcv * zv).astype(BF)
        dzv_ref[...] = (dcv * zc).astype(BF)

        @pl.when(i == 0)
        def _():
            dw_ref[...] = jnp.zeros_like(dw_ref)

        dw_ref[...] += _rows8([jnp.sum(dcc * _down(cv, cvp, 2), axis=0, keepdims=True),
                               jnp.sum(dcc * _down(cv, cvp, 1), axis=0, keepdims=True),
                               jnp.sum(dcc * cv, axis=0, keepdims=True)], tc)

    out = jax.ShapeDtypeStruct((s, conv), BF)
    return pl.pallas_call(
        body, name="conv_mix_bwd", grid=(nc, nr),
        in_specs=[_spec_cur(tr, tc, 0), _spec_next(tr, tc, 0, s),
                  _spec_cur(tr, tc, nc), _spec_prev(tr, tc, nc),
                  _spec_cur(tr, tc, 2 * nc), _spec_prev(tr, tc, 2 * nc),
                  _spec_cur(tr, tc, 0), _spec_next(tr, tc, 0, s), _spec_w(tc, 0)],
        out_specs=[_spec_cur(tr, tc, 0), _spec_cur(tr, tc, 0), _spec_cur(tr, tc, 0), _spec_w(tc, 0)],
        out_shape=[out, out, out, jax.ShapeDtypeStruct((SUB, conv), F32)],
        compiler_params=_cp("parallel", "arbitrary"),
    )(z_a, z_a, z_a, z_a, z_a, z_a, d_p, d_p, cw8)


def _silu_parts(ag):
    sg = jax.nn.sigmoid(ag)
    return ag * sg, sg


def _ffn_act_fwd(a_pre, cw8, cb, dff):
    s = a_pre.shape[0]
    tr = _pick(s, ROW_TILE, 16)
    tc = _pick(dff, COL_TILE, LANE)
    nc = dff // tc

    def body(xg_ref, xgp_ref, xu_ref, xup_ref, wg_ref, wu_ref, bg_ref, bu_ref, f_ref):
        i = pl.program_id(1)
        xgp = jnp.where(i > 0, xgp_ref[...], 0.0)
        xup = jnp.where(i > 0, xup_ref[...], 0.0)
        ag = _conv3(wg_ref, xg_ref[...], xgp) + bg_ref[...]
        au = _conv3(wu_ref, xu_ref[...], xup) + bu_ref[...]
        f_ref[...] = (_silu_parts(ag)[0] * au).astype(BF)

    return pl.pallas_call(
        body, name="ffn_act_fwd", grid=(nc, s // tr),
        in_specs=[_spec_cur(tr, tc, 0), _spec_prev(tr, tc, 0), _spec_cur(tr, tc, nc), _spec_prev(tr, tc, nc),
                  _spec_w(tc, 0), _spec_w(tc, nc),
                  pl.BlockSpec((1, tc), lambda j, i: (0, j)), pl.BlockSpec((1, tc), lambda j, i: (0, nc + j))],
        out_specs=_spec_cur(tr, tc, 0),
        out_shape=jax.ShapeDtypeStruct((s, dff), BF),
        compiler_params=_cp("parallel", "parallel"),
    )(a_pre, a_pre, a_pre, a_pre, cw8, cw8, cb, cb)


def _ffn_act_bwd(a_pre, d_f, cw8, cb, dff):
    s = a_pre.shape[0]
    tr = _pick(s, ROW_TILE_BWD, 16)
    tc = _pick(dff, COL_TILE, LANE)
    nc = dff // tc
    nr = s // tr

    def body(xg_ref, xgp_ref, xgn_ref, xu_ref, xup_ref, xun_ref, df_ref, dfn_ref,
             wg_ref, wu_ref, bg_ref, bu_ref, dxg_ref, dxu_ref, dwg_ref, dwu_ref):
        i = pl.program_id(1)
        xg = xg_ref[...]
        xu = xu_ref[...]
        xgp = jnp.where(i > 0, xgp_ref[...], 0.0)
        xup = jnp.where(i > 0, xup_ref[...], 0.0)

        def d_act(xg_t, xgp_t, xu_t, xup_t, df_t):
            ag = _conv3(wg_ref, xg_t, xgp_t) + bg_ref[...]
            au = _conv3(wu_ref, xu_t, xup_t) + bu_ref[...]
            sil, sg = _silu_parts(ag)
            return df_t * au * (sg * (1.0 + ag * (1.0 - sg))), df_t * sil

        dag, dau = d_act(xg, xgp, xu, xup, df_ref[...])
        dfn = jnp.where(i < nr - 1, dfn_ref[...], 0.0)
        dagn, daun = d_act(xgn_ref[...], xg[tr - SUB:], xun_ref[...], xu[tr - SUB:], dfn)
        dxg_ref[...] = _conv3_t(wg_ref, dag, dagn).astype(BF)
        dxu_ref[...] = _conv3_t(wu_ref, dau, daun).astype(BF)

        @pl.when(i == 0)
        def _():
            dwg_ref[...] = jnp.zeros_like(dwg_ref)
            dwu_ref[...] = jnp.zeros_like(dwu_ref)

        def wgrad(da, x, xp):
            return _rows8([jnp.sum(da * _down(x, xp, 2), axis=0, keepdims=True),
                           jnp.sum(da * _down(x, xp, 1), axis=0, keepdims=True),
                           jnp.sum(da * x, axis=0, keepdims=True),
                           jnp.sum(da, axis=0, keepdims=True)], tc)

        dwg_ref[...] += wgrad(dag, xg, xgp)
        dwu_ref[...] += wgrad(dau, xu, xup)

    half = jax.ShapeDtypeStruct((s, dff), BF)
    wsh = jax.ShapeDtypeStruct((SUB, dff), F32)
    return pl.pallas_call(
        body, name="ffn_act_bwd", grid=(nc, nr),
        in_specs=[_spec_cur(tr, tc, 0), _spec_prev(tr, tc, 0), _spec_next(tr, tc, 0, s),
                  _spec_cur(tr, tc, nc), _spec_prev(tr, tc, nc), _spec_next(tr, tc, nc, s),
                  _spec_cur(tr, tc, 0), _spec_next(tr, tc, 0, s),
                  _spec_w(tc, 0), _spec_w(tc, nc),
                  pl.BlockSpec((1, tc), lambda j, i: (0, j)), pl.BlockSpec((1, tc), lambda j, i: (0, nc + j))],
        out_specs=[_spec_cur(tr, tc, 0), _spec_cur(tr, tc, 0), _spec_w(tc, 0), _spec_w(tc, 0)],
        out_shape=[half, half, wsh, wsh],
        compiler_params=_cp("parallel", "arbitrary"),
    )(a_pre, a_pre, a_pre, a_pre, a_pre, a_pre, d_f, d_f, cw8, cw8, cb, cb)


def _gate_fwd(z_g, b_gate, yc, ym, d):
    s = z_g.shape[0]
    tr = _pick(s, ROW_TILE, 16)
    tc = _pick(d, COL_TILE, LANE)
    nc = d // tc

    def body(za_ref, zb_ref, ba_ref, bb_ref, yc_ref, ym_ref, o_ref):
        ga = jax.nn.sigmoid(za_ref[...] + ba_ref[...])
        gb = jax.nn.sigmoid(zb_ref[...] + bb_ref[...])
        o_ref[...] = (ga * yc_ref[...] + gb * ym_ref[...]).astype(BF)

    return pl.pallas_call(
        body, name="gate_fwd", grid=(nc, s // tr),
        in_specs=[_spec_cur(tr, tc, 0), _spec_cur(tr, tc, nc),
                  pl.BlockSpec((1, tc), lambda j, i: (0, j)), pl.BlockSpec((1, tc), lambda j, i: (0, nc + j)),
                  _spec_cur(tr, tc, 0), _spec_cur(tr, tc, 0)],
        out_specs=_spec_cur(tr, tc, 0),
        out_shape=jax.ShapeDtypeStruct((s, d), BF),
        compiler_params=_cp("parallel", "parallel"),
    )(z_g, z_g, b_gate, b_gate, yc, ym)


def _gate_bwd(d_mix, z_g, b_gate, yc, ym, d):
    s = z_g.shape[0]
    tr = _pick(s, ROW_TILE, 16)
    tc = _pick(d, COL_TILE, LANE)
    nc = d // tc

    def body(dm_ref, za_ref, zb_ref, ba_ref, bb_ref, yc_ref, ym_ref,
             dza_ref, dzb_ref, dyc_ref, dym_ref, dba_ref, dbb_ref):
        i = pl.program_id(1)
        dm = dm_ref[...]
        ga = jax.nn.sigmoid(za_ref[...] + ba_ref[...])
        gb = jax.nn.sigmoid(zb_ref[...] + bb_ref[...])
        dza = dm * yc_ref[...] * (ga * (1.0 - ga))
        dzb = dm * ym_ref[...] * (gb * (1.0 - gb))
        dza_ref[...] = dza.astype(BF)
        dzb_ref[...] = dzb.astype(BF)
        dyc_ref[...] = (dm * ga).astype(BF)
        dym_ref[...] = (dm * gb).astype(BF)

        @pl.when(i == 0)
        def _():
            dba_ref[...] = jnp.zeros_like(dba_ref)
            dbb_ref[...] = jnp.zeros_like(dbb_ref)

        dba_ref[...] += _rows8([jnp.sum(dza, axis=0, keepdims=True)], tc)
        dbb_ref[...] += _rows8([jnp.sum(dzb, axis=0, keepdims=True)], tc)

    act = jax.ShapeDtypeStruct((s, d), BF)
    bsh = jax.ShapeDtypeStruct((SUB, d), F32)
    return pl.pallas_call(
        body, name="gate_bwd", grid=(nc, s // tr),
        in_specs=[_spec_cur(tr, tc, 0), _spec_cur(tr, tc, 0), _spec_cur(tr, tc, nc),
                  pl.BlockSpec((1, tc), lambda j, i: (0, j)), pl.BlockSpec((1, tc), lambda j, i: (0, nc + j)),
                  _spec_cur(tr, tc, 0), _spec_cur(tr, tc, 0)],
        out_specs=[_spec_cur(tr, tc, 0)] * 4 + [_spec_w(tc, 0)] * 2,
        out_shape=[act, act, act, act, bsh, bsh],
        compiler_params=_cp("parallel", "arbitrary"),
    )(d_mix, z_g, z_g, b_gate, b_gate, yc, ym)


def _lay(v):
    z = jnp.zeros(v.shape[:-1] + (HALF,), v.dtype)
    return jnp.concatenate([v[..., :HALF], z, v[..., HALF:], z], axis=-1)


def _unlay(v):
    return jnp.concatenate([v[..., :HALF], v[..., 2 * HALF:3 * HALF]], axis=-1)


def _rope_tables(positions):
    s = positions.shape[0]
    tr = _pick(s, ROW_TILE, 8)
    inv_freq = ROPE_THETA ** (-jnp.arange(0, ROPE, 2, dtype=F32) / ROPE)
    consts = jnp.stack([_lay(jnp.concatenate([inv_freq, inv_freq])),
                        _lay(jnp.ones((ROPE,), F32)),
                        _lay(jnp.concatenate([-jnp.ones((HALF,), F32), jnp.ones((HALF,), F32)]))])
    consts = _pad8(consts)

    def body(p_ref, c_ref, cos_ref, sin_ref):
        ang = p_ref[...].astype(F32) * c_ref[0:1, :]
        cos_ref[...] = jnp.cos(ang) * c_ref[1:2, :]
        sin_ref[...] = jnp.sin(ang) * c_ref[2:3, :]

    tab = jax.ShapeDtypeStruct((s, LANE), F32)
    return pl.pallas_call(
        body, name="rope_tables", grid=(s // tr,),
        in_specs=[pl.BlockSpec((tr, 1), lambda i: (i, 0)), pl.BlockSpec((SUB, LANE), lambda i: (0, 0))],
        out_specs=[pl.BlockSpec((tr, LANE), lambda i: (i, 0))] * 2,
        out_shape=[tab, tab],
        compiler_params=_cp("parallel"),
    )(positions, consts)


def _rope(t, cos, sin):
    return t * cos + pltpu.roll(t, 2 * HALF, axis=1) * sin


def _rope_t(d, cos, sin):
    return d * cos + pltpu.roll(d * sin, 2 * HALF, axis=1)


def _head_fwd(q_raw, kv_raw, z_a, kr_blk, cos, sin, gains, heads):
    s = q_raw.shape[0]
    tr = _pick(s, HEAD_ROW_TILE, 16)
    hw = heads * LANE

    def body(q_ref, kv_ref, kr_ref, cos_ref, sin_ref, g_ref, qo_ref, ko_ref, vo_ref):
        cosv = cos_ref[...]
        sinv = sin_ref[...]
        krv = kr_ref[...]
        kr_ss = jnp.sum(krv * krv, axis=-1, keepdims=True)
        for h in range(heads):
            lo = h * LANE
            qn = q_ref[:, lo:lo + LANE]
            qr = q_ref[:, hw + lo:hw + lo + LANE]
            ss = jnp.sum(qn * qn, axis=-1, keepdims=True) + jnp.sum(qr * qr, axis=-1, keepdims=True)
            r = lax.rsqrt(ss / HEAD_QK + NORM_EPS)
            qo_ref[:, 2 * lo:2 * lo + LANE] = ((qn * r) * g_ref[0:1, :]).astype(BF)
            qo_ref[:, 2 * lo + LANE:2 * lo + 2 * LANE] = _rope((qr * r) * g_ref[1:2, :], cosv, sinv).astype(BF)
            kn = kv_ref[:, 2 * lo:2 * lo + LANE]
            ss = jnp.sum(kn * kn, axis=-1, keepdims=True) + kr_ss
            r = lax.rsqrt(ss / HEAD_QK + NORM_EPS)
            ko_ref[:, 2 * lo:2 * lo + LANE] = ((kn * r) * g_ref[2:3, :]).astype(BF)
            ko_ref[:, 2 * lo + LANE:2 * lo + 2 * LANE] = _rope((krv * r) * g_ref[3:4, :], cosv, sinv).astype(BF)
            vo_ref[:, lo:lo + LANE] = kv_ref[:, 2 * lo + LANE:2 * lo + 2 * LANE].astype(BF)

    row = lambda w: pl.BlockSpec((tr, w), lambda i: (i, 0))
    return pl.pallas_call(
        body, name="head_fwd", grid=(s // tr,),
        in_specs=[row(2 * hw), row(2 * hw), pl.BlockSpec((tr, LANE), lambda i: (i, kr_blk)),
                  row(LANE), row(LANE), pl.BlockSpec((SUB, LANE), lambda i: (0, 0))],
        out_specs=[row(2 * hw), row(2 * hw), row(hw)],
        out_shape=[jax.ShapeDtypeStruct((s, 2 * hw), BF), jax.ShapeDtypeStruct((s, 2 * hw), BF),
                   jax.ShapeDtypeStruct((s, hw), BF)],
        compiler_params=_cp("parallel"),
    )(q_raw, kv_raw, z_a, cos, sin, gains)


def _head_bwd(q_raw, kv_raw, z_a, kr_blk, cos, sin, gains, dq_att, dk_att, dv, heads):
    s = q_raw.shape[0]
    tr = _pick(s, HEAD_ROW_TILE_BWD, 16)
    hw = heads * LANE

    def body(q_ref, kv_ref, kr_ref, cos_ref, sin_ref, g_ref, dq_ref, dk_ref, dv_ref,
             dqr_ref, dkv_ref, dkr_ref, dg_ref):
        i = pl.program_id(0)
        cosv = cos_ref[...]
        sinv = sin_ref[...]
        krv = kr_ref[...]
        kr_ss = jnp.sum(krv * krv, axis=-1, keepdims=True)
        dkr = jnp.zeros((tr, LANE), F32)
        dgs = [jnp.zeros((1, LANE), F32) for _ in range(4)]

        def norm_bwd(xn, xr, ss, dn_out, dr_out, gn, gr):
            r = lax.rsqrt(ss / HEAD_QK + NORM_EPS)
            nn = xn * r
            nr = xr * r
            dt = _rope_t(dr_out, cosv, sinv)
            dnn = dn_out * gn
            dnr = dt * gr
            mean = (jnp.sum(dnn * nn, axis=-1, keepdims=True) + jnp.sum(dnr * nr, axis=-1, keepdims=True)) / HEAD_QK
            return (r * (dnn - nn * mean), r * (dnr - nr * mean),
                    jnp.sum(dn_out * nn, axis=0, keepdims=True), jnp.sum(dt * nr, axis=0, keepdims=True))

        for h in range(heads):
            lo = h * LANE
            qn = q_ref[:, lo:lo + LANE]
            qr = q_ref[:, hw + lo:hw + lo + LANE]
            ss = jnp.sum(qn * qn, axis=-1, keepdims=True) + jnp.sum(qr * qr, axis=-1, keepdims=True)
            dxn, dxr, g0, g1 = norm_bwd(qn, qr, ss, dq_ref[:, 2 * lo:2 * lo + LANE],
                                        dq_ref[:, 2 * lo + LANE:2 * lo + 2 * LANE], g_ref[0:1, :], g_ref[1:2, :])
            dqr_ref[:, lo:lo + LANE] = dxn.astype(BF)
            dqr_ref[:, hw + lo:hw + lo + LANE] = dxr.astype(BF)
            kn = kv_ref[:, 2 * lo:2 * lo + LANE]
            ss = jnp.sum(kn * kn, axis=-1, keepdims=True) + kr_ss
            dxn, dxr, g2, g3 = norm_bwd(kn, krv, ss, dk_ref[:, 2 * lo:2 * lo + LANE],
                                        dk_ref[:, 2 * lo + LANE:2 * lo + 2 * LANE], g_ref[2:3, :], g_ref[3:4, :])
            dkv_ref[:, 2 * lo:2 * lo + LANE] = dxn.astype(BF)
            dkv_ref[:, 2 * lo + LANE:2 * lo + 2 * LANE] = dv_ref[:, lo:lo + LANE].astype(BF)
            dkr = dkr + dxr
            dgs = [a + b for a, b in zip(dgs, (g0, g1, g2, g3))]
        dkr_ref[...] = dkr

        @pl.when(i == 0)
        def _():
            dg_ref[...] = jnp.zeros_like(dg_ref)

        dg_ref[...] += _rows8(dgs, LANE)

    row = lambda w: pl.BlockSpec((tr, w), lambda i: (i, 0))
    return pl.pallas_call(
        body, name="head_bwd", grid=(s // tr,),
        in_specs=[row(2 * hw), row(2 * hw), pl.BlockSpec((tr, LANE), lambda i: (i, kr_blk)),
                  row(LANE), row(LANE), pl.BlockSpec((SUB, LANE), lambda i: (0, 0)),
                  row(2 * hw), row(2 * hw), row(hw)],
        out_specs=[row(2 * hw), row(2 * hw), row(LANE), pl.BlockSpec((SUB, LANE), lambda i: (0, 0))],
        out_shape=[jax.ShapeDtypeStruct((s, 2 * hw), BF), jax.ShapeDtypeStruct((s, 2 * hw), BF),
                   jax.ShapeDtypeStruct((s, LANE), F32), jax.ShapeDtypeStruct((SUB, LANE), F32)],
        compiler_params=_cp("arbitrary"),
    )(q_raw, kv_raw, z_a, cos, sin, gains, dq_att, dk_att, dv)


def _causal_mask(t):
    rows = lax.broadcasted_iota(jnp.int32, (t, t), 0)
    cols = lax.broadcasted_iota(jnp.int32, (t, t), 1)
    return cols <= rows


def _attn_fwd(q_att, k_att, v, heads):
    s = q_att.shape[0]
    t = _pick(s, ATTN_TILE, LANE)
    nt = s // t
    scale = HEAD_QK ** -0.5

    def body(q_ref, k_ref, v_ref, o_ref, lse_ref, m_s, l_s, acc_s):
        i = pl.program_id(1)
        j = pl.program_id(2)

        @pl.when(j == 0)
        def _():
            m_s[...] = jnp.full_like(m_s, NEG_INF)
            l_s[...] = jnp.zeros_like(l_s)
            acc_s[...] = jnp.zeros_like(acc_s)

        def step(masked):
            sc = lax.dot_general(q_ref[...], k_ref[...], (((1,), (1,)), ((), ())),
                                 preferred_element_type=F32) * scale
            if masked:
                sc = jnp.where(_causal_mask(t), sc, NEG_INF)
            m_prev = m_s[...]
            m_new = jnp.maximum(m_prev, jnp.max(sc, axis=-1, keepdims=True))
            alpha = jnp.exp(m_prev - m_new)
            p = jnp.exp(sc - m_new)
            l_s[...] = alpha * l_s[...] + jnp.sum(p, axis=-1, keepdims=True)
            acc_s[...] = alpha * acc_s[...] + jnp.dot(p.astype(BF), v_ref[...], preferred_element_type=F32)
            m_s[...] = m_new

        @pl.when(j < i)
        def _():
            step(False)

        @pl.when(j == i)
        def _():
            step(True)
            o_ref[...] = acc_s[...] / l_s[...]
            lse_ref[...] = m_s[...] + jnp.log(l_s[...])

    kv_idx = lambda h, i, j: (jnp.minimum(j, i), h)
    return pl.pallas_call(
        body, name="attn_fwd", grid=(heads, nt, nt),
        in_specs=[pl.BlockSpec((t, 2 * LANE), lambda h, i, j: (i, h)),
                  pl.BlockSpec((t, 2 * LANE), kv_idx),
                  pl.BlockSpec((t, LANE), kv_idx)],
        out_specs=[pl.BlockSpec((t, LANE), lambda h, i, j: (i, h)),
                   pl.BlockSpec((None, t, 1), lambda h, i, j: (h, i, 0))],
        out_shape=[jax.ShapeDtypeStruct((s, heads * LANE), F32), jax.ShapeDtypeStruct((heads, s, 1), F32)],
        scratch_shapes=[pltpu.VMEM((t, 1), F32), pltpu.VMEM((t, 1), F32), pltpu.VMEM((t, LANE), F32)],
        compiler_params=_cp("parallel", "parallel", "arbitrary"),
    )(q_att, k_att, v)


def _attn_bwd(q_att, k_att, v, o, lse, d_o, heads):
    s = q_att.shape[0]
    t = _pick(s, ATTN_TILE, LANE)
    nt = s // t
    scale = HEAD_QK ** -0.5

    def body(q_ref, k_ref, v_ref, do_ref, o_ref, lse_ref, dq_ref, dk_ref, dv_ref, dk_s, dv_s):
        j = pl.program_id(1)
        i = pl.program_id(2)

        @pl.when((j == 0) & (i == 0))
        def _():
            dq_ref[...] = jnp.zeros_like(dq_ref)

        @pl.when(i == j)
        def _():
            dk_s[...] = jnp.zeros_like(dk_s)
            dv_s[...] = jnp.zeros_like(dv_s)

        def step(masked):
            q = q_ref[...]
            k = k_ref[...]
            do = do_ref[...]
            sc = lax.dot_general(q, k, (((1,), (1,)), ((), ())), preferred_element_type=F32) * scale
            if masked:
                sc = jnp.where(_causal_mask(t), sc, NEG_INF)
            p = jnp.exp(sc - lse_ref[...])
            dp = lax.dot_general(do, v_ref[...], (((1,), (1,)), ((), ())), preferred_element_type=F32)
            delta = jnp.sum(do.astype(F32) * o_ref[...], axis=-1, keepdims=True)
            ds = (p * (dp - delta) * scale).astype(BF)
            dv_s[...] += lax.dot_general(p.astype(BF), do, (((0,), (0,)), ((), ())), preferred_element_type=F32)
            dk_s[...] += lax.dot_general(ds, q, (((0,), (0,)), ((), ())), preferred_element_type=F32)
            rows = pl.ds(pl.multiple_of(i * t, t), t)
            dq_ref[rows, :] += jnp.dot(ds, k, preferred_element_type=F32)

        @pl.when(i > j)
        def _():
            step(False)

        @pl.when(i == j)
        def _():
            step(True)

        @pl.when(i == nt - 1)
        def _():
            dk_ref[...] = dk_s[...]
            dv_ref[...] = dv_s[...]

    q_idx = lambda h, j, i: (jnp.maximum(i, j), h)
    kv_idx = lambda h, j, i: (j, h)
    return pl.pallas_call(
        body, name="attn_bwd", grid=(heads, nt, nt),
        in_specs=[pl.BlockSpec((t, 2 * LANE), q_idx), pl.BlockSpec((t, 2 * LANE), kv_idx),
                  pl.BlockSpec((t, LANE), kv_idx), pl.BlockSpec((t, LANE), q_idx), pl.BlockSpec((t, LANE), q_idx),
                  pl.BlockSpec((None, t, 1), lambda h, j, i: (h, jnp.maximum(i, j), 0))],
        out_specs=[pl.BlockSpec((s, 2 * LANE), lambda h, j, i: (0, h)),
                   pl.BlockSpec((t, 2 * LANE), kv_idx), pl.BlockSpec((t, LANE), kv_idx)],
        out_shape=[jax.ShapeDtypeStruct((s, heads * 2 * LANE), F32),
                   jax.ShapeDtypeStruct((s, heads * 2 * LANE), F32),
                   jax.ShapeDtypeStruct((s, heads * LANE), F32)],
        scratch_shapes=[pltpu.VMEM((t, 2 * LANE), F32), pltpu.VMEM((t, LANE), F32)],
        compiler_params=_cp("parallel", "arbitrary", "arbitrary"),
    )(q_att, k_att, v, d_o, o, lse)


def _loss_head(y, target):
    s, d = y.shape
    tr = _pick(s, ROW_TILE, 8)

    def body(y_ref, t_ref, dy_ref, l_ref):
        i = pl.program_id(0)
        e = y_ref[...] - t_ref[...]
        dy_ref[...] = e / d

        @pl.when(i == 0)
        def _():
            l_ref[...] = jnp.zeros_like(l_ref)

        l_ref[...] += 0.5 * jnp.sum(jnp.mean(e * e, axis=-1, keepdims=True), axis=0, keepdims=True)

    return pl.pallas_call(
        body, name="loss_head", grid=(s // tr,),
        in_specs=[pl.BlockSpec((tr, d), lambda i: (i, 0))] * 2,
        out_specs=[pl.BlockSpec((tr, d), lambda i: (i, 0)), pl.BlockSpec((SUB, LANE), lambda i: (0, 0))],
        out_shape=[jax.ShapeDtypeStruct((s, d), F32), jax.ShapeDtypeStruct((SUB, LANE), F32)],
        compiler_params=_cp("arbitrary"),
    )(y, target)


def _add_pairs(a, b, name):
    n, r, c = a.shape
    tr = _pick(r, 512, 16)

    def body(a_ref, b_ref, o_ref):
        o_ref[...] = (a_ref[...].astype(F32) + b_ref[...].astype(F32)).astype(BF)

    spec = pl.BlockSpec((None, tr, c), lambda k, i: (k, i, 0))
    return pl.pallas_call(
        body, name=name, grid=(n, r // tr), in_specs=[spec, spec], out_specs=spec,
        out_shape=jax.ShapeDtypeStruct((n, r, c), BF),
        compiler_params=_cp("parallel", "parallel"),
    )(a, b)


def _sum_parts(parts, name):
    n, r, c = parts.shape
    tr = _pick(r, 512, 8)

    def body(p_ref, o_ref):
        g = p_ref[0].astype(F32)
        for k in range(1, n):
            g = g + p_ref[k].astype(F32)
        o_ref[...] = g

    return pl.pallas_call(
        body, name=name, grid=(r // tr,),
        in_specs=[pl.BlockSpec((n, tr, c), lambda i: (0, i, 0))],
        out_specs=pl.BlockSpec((tr, c), lambda i: (i, 0)),
        out_shape=jax.ShapeDtypeStruct((r, c), F32),
        compiler_params=_cp("parallel"),
    )(parts)


def _adamw(parts, w, m, v, name):
    n, r, c = parts.shape
    tr = _pick(r, 256, 8)

    def body(p_ref, w_ref, m_ref, v_ref, g_ref, d_ref, mo_ref, vo_ref):
        g = p_ref[0].astype(F32)
        for k in range(1, n):
            g = g + p_ref[k].astype(F32)
        m_new = ADAM_B1 * m_ref[...] + (1.0 - ADAM_B1) * g
        v_new = ADAM_B2 * v_ref[...] + (1.0 - ADAM_B2) * jnp.square(g)
        m_hat = m_new / (1.0 - ADAM_B1 ** ADAM_STEP)
        v_hat = v_new / (1.0 - ADAM_B2 ** ADAM_STEP)
        g_ref[...] = g
        d_ref[...] = -ADAM_LR * (m_hat / (jnp.sqrt(v_hat) + ADAM_EPS) + ADAM_WD * w_ref[...])
        mo_ref[...] = m_new
        vo_ref[...] = v_new

    spec = pl.BlockSpec((tr, c), lambda i: (i, 0))
    sh = jax.ShapeDtypeStruct((r, c), F32)
    return pl.pallas_call(
        body, name=name, grid=(r // tr,),
        in_specs=[pl.BlockSpec((n, tr, c), lambda i: (0, i, 0)), spec, spec, spec],
        out_specs=[spec] * 4, out_shape=[sh] * 4,
        compiler_params=_cp("parallel"),
    )(parts, w, m, v)


def _place():
    x, y, c = lax.axis_index("x"), lax.axis_index("y"), lax.axis_index("c")
    chips = [(1 - x, y), (x, 1 - y), (1 - x, 1 - y)]
    return x, y, c, chips


def _all_gather(shards, name):
    n = len(shards)

    def body(*refs):
        ins, outs = refs[:n], refs[n:2 * n]
        send_sems, recv_sems, local_sems = refs[2 * n:]
        x, y, c, chips = _place()
        me, sibling = (x, y, c), (x, y, 1 - c)

        def slot(w, p):
            return outs[w].at[4 * p[0] + 2 * p[1] + p[2]]

        def copy(w, k, block, to, src=None):
            return pltpu.make_async_remote_copy(
                src_ref=slot(w, block) if src is None else src, dst_ref=slot(w, block),
                send_sem=send_sems.at[w, k], recv_sem=recv_sems.at[w, k], device_id=to, device_id_type=MESH)

        first = []
        for w in range(n):
            first += [copy(w, 1 + j, me, (*chip, c), src=ins[w]) for j, chip in enumerate(chips)]
            first.append(copy(w, 0, me, sibling, src=ins[w]))
        for cp in first:
            cp.start()
        mine = [pltpu.make_async_copy(ins[w], slot(w, me), local_sems.at[w]) for w in range(n)]
        for cp in mine:
            cp.start()
        passed = []
        for w in range(n):
            for j, chip in enumerate(chips):
                copy(w, 1 + j, (*chip, c), me).wait_recv()
                cp = copy(w, 4 + j, (*chip, c), sibling)
                cp.start()
                passed.append(cp)
        for w in range(n):
            copy(w, 0, sibling, me).wait_recv()
            for j, chip in enumerate(chips):
                copy(w, 4 + j, (*chip, 1 - c), me).wait_recv()
        for cp in first + passed:
            cp.wait_send()
        for cp in mine:
            cp.wait()

    return pl.pallas_call(
        body, name=name,
        in_specs=[ANY] * n, out_specs=[ANY] * n,
        out_shape=[jax.ShapeDtypeStruct((N_DEV,) + a.shape, a.dtype) for a in shards],
        scratch_shapes=[pltpu.SemaphoreType.DMA((n, 7)), pltpu.SemaphoreType.DMA((n, 7)),
                        pltpu.SemaphoreType.DMA((n,))],
    )(*shards)


def _sibling_exchange(grads, name):
    n = len(grads)

    def body(*refs):
        ins, owns, lands = refs[:n], refs[n:2 * n], refs[2 * n:3 * n]
        send_sems, recv_sems, local_sems = refs[3 * n:]
        x, y, c, _ = _place()
        sends, keeps = [], []
        for w in range(n):
            for k in range(N_CHIP):
                sends.append(pltpu.make_async_remote_copy(
                    src_ref=ins[w].at[2 * k + (1 - c)], dst_ref=lands[w].at[k],
                    send_sem=send_sems.at[w, k], recv_sem=recv_sems.at[w, k],
                    device_id=(x, y, 1 - c), device_id_type=MESH))
                keeps.append(pltpu.make_async_copy(ins[w].at[2 * k + c], owns[w].at[k], local_sems.at[w, k]))
        for cp in sends + keeps:
            cp.start()
        for cp in sends + keeps:
            cp.wait()

    shapes = [jax.ShapeDtypeStruct((N_CHIP,) + g.shape[1:], g.dtype) for g in grads]
    res = pl.pallas_call(
        body, name=name,
        in_specs=[ANY] * n, out_specs=[ANY] * (2 * n), out_shape=shapes + shapes,
        scratch_shapes=[pltpu.SemaphoreType.DMA((n, N_CHIP)), pltpu.SemaphoreType.DMA((n, N_CHIP)),
                        pltpu.SemaphoreType.DMA((n, N_CHIP))],
    )(*grads)
    return res[:n], res[n:]


def _chip_exchange(parts, name):
    n = len(parts)

    def body(*refs):
        ins, outs = refs[:n], refs[n:2 * n]
        send_sems, recv_sems, local_sems = refs[2 * n:]
        x, y, c, chips = _place()
        mine = 2 * x + y
        cps = []
        for w in range(n):
            for j, chip in enumerate(chips):
                cps.append(pltpu.make_async_remote_copy(
                    src_ref=ins[w].at[2 * chip[0] + chip[1]], dst_ref=outs[w].at[mine],
                    send_sem=send_sems.at[w, j], recv_sem=recv_sems.at[w, j],
                    device_id=(*chip, c), device_id_type=MESH))
            cps.append(pltpu.make_async_copy(ins[w].at[mine], outs[w].at[mine], local_sems.at[w]))
        for cp in cps:
            cp.start()
        for cp in cps:
            cp.wait()

    return pl.pallas_call(
        body, name=name,
        in_specs=[ANY] * n, out_specs=[ANY] * n,
        out_shape=[jax.ShapeDtypeStruct(p.shape, p.dtype) for p in parts],
        scratch_shapes=[pltpu.SemaphoreType.DMA((n, 3)), pltpu.SemaphoreType.DMA((n, 3)),
                        pltpu.SemaphoreType.DMA((n,))],
    )(*parts)


def _unblock(w3):
    nb, k, nbw = w3.shape
    return w3.transpose(1, 0, 2).reshape(k, nb * nbw)


def _block(w, nb):
    k, n = w.shape
    return w.reshape(k, nb, n // nb).transpose(1, 0, 2)


def kernel(x, positions, ln1_g, w_in, b_gate, conv_w, w_conv_out, q_a_g, w_q_b, kv_a_g, w_kv_b, q_norm_g, k_norm_g, w_mla_out, w_o, ln2_g, w_ffn_up, ffn_conv_w, ffn_conv_b, w_ffn_down, loss_target, m_ln1_g, m_w_in, m_b_gate, m_conv_w, m_w_conv_out, m_q_a_g, m_w_q_b, m_kv_a_g, m_w_kv_b, m_q_norm_g, m_k_norm_g, m_w_mla_out, m_w_o, m_ln2_g, m_w_ffn_up, m_ffn_conv_w, m_ffn_conv_b, m_w_ffn_down, v_ln1_g, v_w_in, v_b_gate, v_conv_w, v_w_conv_out, v_q_a_g, v_w_q_b, v_kv_a_g, v_w_kv_b, v_q_norm_g, v_k_norm_g, v_w_mla_out, v_w_o, v_ln2_g, v_w_ffn_up, v_ffn_conv_w, v_ffn_conv_b, v_w_ffn_down):
    s, d = x.shape[1], x.shape[2]
    conv = conv_w.shape[2] * N_DEV
    ql, kvl = q_a_g.shape[1], kv_a_g.shape[1]
    heads = w_q_b.shape[2] * N_DEV // HEAD_QK
    dff = w_ffn_down.shape[1] * N_DEV
    hw = heads * LANE
    conv3 = 3 * conv
    kr_off = conv3 + ql
    kv_off = -(-(kr_off + LANE) // kvl) * kvl
    wa = kv_off + kvl
    assert conv3 % ql == 0 and kr_off % LANE == 0
    xs = x[0]
    tgt = loss_target[0]
    pos = positions.reshape(s, 1)

    big = dict(w_in=w_in[0], w_conv_out=w_conv_out[0], w_q_b=w_q_b[0], w_kv_b=w_kv_b[0],
               w_mla_out=w_mla_out[0], w_o=w_o[0], w_ffn_up=w_ffn_up[0], w_ffn_down=w_ffn_down[0])
    names = list(big)
    allg = _all_gather([big[k].astype(BF) for k in names] + [_pad8(conv_w[0]), _pad8(ffn_conv_w[0])],
                       "gather_weights")
    gathered = dict(zip(names, allg[:len(names)]))
    cw8 = _unblock(allg[len(names)])
    fcw8 = _unblock(allg[len(names) + 1])

    w_in_full = _unblock(gathered["w_in"])
    zpad = jnp.zeros((d, kv_off - kr_off - LANE), BF)
    w_a = jnp.concatenate([w_in_full[:, :kr_off], _lay(w_in_full[:, kr_off + kvl:kr_off + kvl + ROPE]), zpad,
                           w_in_full[:, kr_off:kr_off + kvl]], axis=1)[None]
    g_off = kr_off + kvl + ROPE
    w_g = _block(w_in_full[:, g_off:], 2)
    wq_full = _unblock(gathered["w_q_b"]).reshape(ql, heads, HEAD_QK)
    w_q = jnp.concatenate([wq_full[:, :, :NOPE].reshape(ql, hw), _lay(wq_full[:, :, NOPE:]).reshape(ql, hw)],
                          axis=1)[None]
    w_co = gathered["w_conv_out"]
    w_kv = gathered["w_kv_b"]
    w_mo = gathered["w_mla_out"].reshape(1, hw, d)
    w_oo = gathered["w_o"].reshape(1, d, d)
    w_up = gathered["w_ffn_up"]
    w_dn = gathered["w_ffn_down"].reshape(1, dff, d)
    gains = _pad8(jnp.concatenate([q_norm_g[:, :NOPE], _lay(q_norm_g[:, NOPE:]),
                                   k_norm_g[:, :NOPE], _lay(k_norm_g[:, NOPE:])], axis=0))
    kr_blk = kr_off // LANE

    cos, sin = _rope_tables(pos)
    u1 = _rms_fwd(xs, ln1_g, d, 0, "rms1_fwd")
    z_a = _mm_nn(u1, w_a, "mm_z_a")
    z_g = _mm_nn(u1, w_g, "mm_z_g")
    p = _conv_mix_fwd(z_a, cw8, conv)
    yc = _mm_nn(p, w_co, "mm_y_conv")
    qn = _rms_fwd(z_a, q_a_g, ql, conv3 // ql, "rms_q_fwd")
    kvn = _rms_fwd(z_a, kv_a_g, kvl, kv_off // kvl, "rms_kv_fwd")
    q_raw = _mm_nn(qn, w_q, "mm_q")
    kv_raw = _mm_nn(kvn, w_kv, "mm_kv")
    q_att, k_att, v_bf = _head_fwd(q_raw, kv_raw, z_a, kr_blk, cos, sin, gains, heads)
    o, lse = _attn_fwd(q_att, k_att, v_bf, heads)
    ym = _mm_nn(o, w_mo, "mm_y_mla")
    mix = _gate_fwd(z_g, b_gate, yc, ym, d)
    h1 = _mm_nn(mix, w_oo, "mm_h1", add=xs)
    u2 = _rms_fwd(h1, ln2_g, d, 0, "rms2_fwd")
    a_pre = _mm_nn(u2, w_up, "mm_ffn_up")
    f = _ffn_act_fwd(a_pre, fcw8, ffn_conv_b, dff)
    y = _mm_nn(f, w_dn, "mm_ffn_down", add=h1)
    dy, loss_part = _loss_head(y, tgt)

    d_f = _mm_nt(dy, w_dn, "mm_d_f")
    g_dn = _mm_tn(f, dy, 1, "mm_g_ffn_down")
    d_xg, d_xu, dfw_g, dfw_u = _ffn_act_bwd(a_pre, d_f, fcw8, ffn_conv_b, dff)
    half = N_DEV // 2
    d_u2 = _mm_nt(d_xg, w_up, "mm_d_u2_gate", blk0=0, nblk=half)
    d_u2 = _mm_nt(d_xu, w_up, "mm_d_u2_up", blk0=half, nblk=half, add=d_u2)
    g_up = jnp.concatenate([_mm_tn(u2, d_xg, half, "mm_g_ffn_up_gate"), _mm_tn(u2, d_xu, half, "mm_g_ffn_up_up")], axis=0)
    d_h1, dg_ln2 = _rms_bwd(h1, d_u2, ln2_g, d, 0, "rms2_bwd", extra=dy)
    d_mix = _mm_nt(d_h1, w_oo, "mm_d_mix")
    g_oo = _mm_tn(mix, d_h1, 1, "mm_g_w_o")
    d_zga, d_zgb, d_yc, d_ym, dba, dbb = _gate_bwd(d_mix, z_g, b_gate, yc, ym, d)
    d_p = _mm_nt(d_yc, w_co, "mm_d_p")
    g_co = _mm_tn(p, d_yc, N_DEV, "mm_g_conv_out")
    d_o = _mm_nt(d_ym, w_mo, "mm_d_o", out_dtype=BF)
    g_mo = _mm_tn(o, d_ym, 1, "mm_g_mla_out")
    d_zb, d_zc, d_zv, dcw = _conv_mix_bwd(z_a, d_p, cw8, conv)
    dq_att, dk_att, dv = _attn_bwd(q_att, k_att, v_bf, o, lse, d_o, heads)
    d_q_raw, d_kv_raw, d_kr, dgains = _head_bwd(q_raw, kv_raw, z_a, kr_blk, cos, sin, gains, dq_att, dk_att, dv, heads)
    d_qn = _mm_nt(d_q_raw, w_q, "mm_d_qn")
    g_q = _mm_tn(qn, d_q_raw, 1, "mm_g_q")
    d_kvn = _mm_nt(d_kv_raw, w_kv, "mm_d_kvn")
    g_kv = _mm_tn(kvn, d_kv_raw, N_DEV, "mm_g_kv")
    d_ql, dg_qa = _rms_bwd(z_a, d_qn, q_a_g, ql, conv3 // ql, "rms_q_bwd", out_dtype=BF)
    d_kvl, dg_kva = _rms_bwd(z_a, d_kvn, kv_a_g, kvl, kv_off // kvl, "rms_kv_bwd", out_dtype=BF)
    d_z_a = jnp.concatenate([d_zb, d_zc, d_zv, d_ql, d_kr.astype(BF), jnp.zeros((s, kv_off - kr_off - LANE), BF),
                             d_kvl], axis=1)
    d_u1 = _mm_nt(d_z_a, w_a, "mm_d_u1_a")
    d_u1 = _mm_nt(d_zga, w_g, "mm_d_u1_ga", blk0=0, nblk=1, add=d_u1)
    d_u1 = _mm_nt(d_zgb, w_g, "mm_d_u1_gb", blk0=1, nblk=1, add=d_u1)
    g_a = _mm_tn(u1, d_z_a, 1, "mm_g_w_a")[0]
    g_ga = _mm_tn(u1, d_zga, 1, "mm_g_w_ga")[0]
    g_gb = _mm_tn(u1, d_zgb, 1, "mm_g_w_gb")[0]
    grad_x, dg_ln1 = _rms_bwd(xs, d_u1, ln1_g, d, 0, "rms1_bwd", extra=d_h1)

    g_in = _block(jnp.concatenate([g_a[:, :kr_off], g_a[:, kv_off:kv_off + kvl],
                                   _unlay(g_a[:, kr_off:kr_off + LANE]), g_ga, g_gb], axis=1), N_DEV)
    g_q2 = g_q[0]
    g_qb = _block(jnp.concatenate([g_q2[:, :hw].reshape(ql, heads, NOPE),
                                   _unlay(g_q2[:, hw:].reshape(ql, heads, LANE))], axis=2).reshape(ql, heads * HEAD_QK), N_DEV)
    part = dict(w_in=g_in, w_conv_out=g_co, w_q_b=g_qb, w_kv_b=g_kv,
                w_mla_out=g_mo.reshape(N_DEV, hw // N_DEV, d), w_o=g_oo.reshape(N_DEV, d // N_DEV, d),
                w_ffn_up=g_up, w_ffn_down=g_dn.reshape(N_DEV, dff // N_DEV, d))
    owns, lands = _sibling_exchange([part[k] for k in names], "reduce_sibling")
    chip_parts = [_add_pairs(a, b, "reduce_add_" + k) for k, a, b in zip(names, owns, lands)]
    summed = dict(zip(names, _chip_exchange(chip_parts, "reduce_chips")))

    small = dict(ln1_g=dg_ln1[0:1], b_gate=jnp.concatenate([dba[0:1], dbb[0:1]], axis=1), q_a_g=dg_qa[0:1],
                 kv_a_g=dg_kva[0:1],
                 q_norm_g=jnp.concatenate([dgains[0:1], _unlay(dgains[1:2])], axis=1),
                 k_norm_g=jnp.concatenate([dgains[2:3], _unlay(dgains[3:4])], axis=1),
                 ln2_g=dg_ln2[0:1], ffn_conv_b=jnp.concatenate([dfw_g[3:4], dfw_u[3:4]], axis=1))
    small_names = list(small)
    extra = [dcw[0:3].reshape(1, -1), jnp.concatenate([dfw_g[0:3], dfw_u[0:3]], axis=1).reshape(1, -1),
             loss_part[0:1, 0:1]]
    flat = jnp.concatenate([small[k] for k in small_names] + extra, axis=1)
    n_flat = flat.shape[1]
    rows = -(-n_flat // (SUB * LANE)) * SUB
    flat = jnp.pad(flat, ((0, 0), (0, rows * LANE - n_flat))).reshape(rows, LANE)
    total = _sum_parts(_all_gather([flat], "gather_small")[0], "sum_small").reshape(1, rows * LANE)
    off = 0
    small_g = {}
    for k in small_names:
        small_g[k] = total[:, off:off + small[k].shape[1]]
        off += small[k].shape[1]
    me = 4 * lax.axis_index("x") + 2 * lax.axis_index("y") + lax.axis_index("c")
    cwn, fcwn = conv // N_DEV, 2 * dff // N_DEV
    g_cw = lax.dynamic_slice_in_dim(total[:, off:off + 3 * conv].reshape(3, conv), me * cwn, cwn, axis=1)
    off += 3 * conv
    g_fcw = lax.dynamic_slice_in_dim(total[:, off:off + 6 * dff].reshape(3, 2 * dff), me * fcwn, fcwn, axis=1)
    off += 6 * dff
    loss = total[0, off]

    loc = locals()
    out = {}
    for k in names:
        out[k] = _adamw(summed[k], big[k], loc["m_" + k][0], loc["v_" + k][0], "adamw_" + k)
    small_w = dict(ln1_g=ln1_g, b_gate=b_gate, q_a_g=q_a_g, kv_a_g=kv_a_g, q_norm_g=q_norm_g, k_norm_g=k_norm_g,
                   ln2_g=ln2_g, ffn_conv_b=ffn_conv_b, conv_w=conv_w[0].reshape(1, -1),
                   ffn_conv_w=ffn_conv_w[0].reshape(1, -1))
    small_g["conv_w"] = g_cw.reshape(1, -1)
    small_g["ffn_conv_w"] = g_fcw.reshape(1, -1)
    packed_names = list(small_w)

    def pack(get):
        vflat = jnp.concatenate([get(k).reshape(1, -1) for k in packed_names], axis=1)
        nr = -(-vflat.shape[1] // (SUB * LANE)) * SUB
        return jnp.pad(vflat, ((0, 0), (0, nr * LANE - vflat.shape[1])), constant_values=1.0).reshape(nr, LANE)

    res = _adamw(pack(lambda k: small_g[k])[None], pack(lambda k: small_w[k]), pack(lambda k: loc["m_" + k]),
                 pack(lambda k: loc["v_" + k]), "adamw_small")
    res = [r.reshape(1, -1) for r in res]
    off = 0
    for k in packed_names:
        shape = loc[k].shape
        size = small_w[k].shape[1]
        out[k] = [r[:, off:off + size].reshape(shape) for r in res]
        off += size
    for k in names:
        out[k] = [r[None] for r in out[k]]

    order = ["ln1_g", "w_in", "b_gate", "conv_w", "w_conv_out", "q_a_g", "w_q_b", "kv_a_g", "w_kv_b", "q_norm_g",
             "k_norm_g", "w_mla_out", "w_o", "ln2_g", "w_ffn_up", "ffn_conv_w", "ffn_conv_b", "w_ffn_down"]
    return (loss, grad_x[None], *[out[k][0] for k in order], *[out[k][1] for k in order],
            *[out[k][2] for k in order], *[out[k][3] for k in order])
```

```python
import functools

import jax
import jax.numpy as jnp
from jax import lax
from jax.experimental import pallas as pl
from jax.experimental.pallas import tpu as pltpu

BF = jnp.bfloat16
F32 = jnp.float32
MESH = pl.DeviceIdType.MESH
N_DEV = 8

NOPE = 128
ROPE = 64
HALF = ROPE // 2
HEAD_QK = NOPE + ROPE
HEAD_V = 128
LANE = 128
SUB = 8
NORM_EPS = 1e-6
NEG_INF = -1e30
ROPE_THETA = 10000.0
ADAM_LR = 0.001
ADAM_B1 = 0.9
ADAM_B2 = 0.999
ADAM_EPS = 1e-08
ADAM_WD = 0.01
ADAM_STEP = 10

VMEM_LIMIT = 52 * 1024 * 1024
MM_TM, MM_TN, MM_TK, MM_TS = 1024, 1536, 1024, 512
ROW_TILE, ROW_TILE_BWD = 512, 256
HEAD_ROW_TILE, HEAD_ROW_TILE_BWD = 256, 128
COL_TILE = 512
ATTN_TILE = 512
ANY = pl.BlockSpec(memory_space=pl.ANY)


def _pick(n, target, mult):
    t = (min(n, target) // mult) * mult
    while t > 0:
        if n % t == 0:
            return t
        t -= mult
    raise ValueError(f"no tile for {n} (target {target}, multiple {mult})")


def _cp(*sem):
    return pltpu.CompilerParams(dimension_semantics=sem, vmem_limit_bytes=VMEM_LIMIT)


def _mm_nn(a, b3, name, add=None, out_dtype=F32, blk0=0, nblk=None):
    m, k = a.shape
    nb_all, k2, nbw = b3.shape
    assert k == k2
    nblk = nb_all - blk0 if nblk is None else nblk
    n = nblk * nbw
    tm = _pick(m, MM_TM, 16)
    tn = _pick(nbw, MM_TN, LANE)
    tk = _pick(k, MM_TK, LANE)
    per = nbw // tn
    nk = k // tk

    def body(*refs):
        if add is None:
            a_ref, b_ref, o_ref, acc = refs
        else:
            a_ref, b_ref, c_ref, o_ref, acc = refs
        kk = pl.program_id(2)

        @pl.when(kk == 0)
        def _():
            acc[...] = jnp.zeros_like(acc)

        acc[...] += jnp.dot(a_ref[...].astype(BF), b_ref[...].astype(BF), preferred_element_type=F32)

        @pl.when(kk == nk - 1)
        def _():
            r = acc[...]
            if add is not None:
                r = r + c_ref[...]
            o_ref[...] = r.astype(out_dtype)

    in_specs = [pl.BlockSpec((tm, tk), lambda i, j, kk: (i, kk)),
                pl.BlockSpec((None, tk, tn), lambda i, j, kk: (blk0 + j // per, kk, j % per))]
    args = [a, b3]
    if add is not None:
        in_specs.append(pl.BlockSpec((tm, tn), lambda i, j, kk: (i, j)))
        args.append(add)
    return pl.pallas_call(
        body, name=name, grid=(m // tm, n // tn, nk),
        in_specs=in_specs, out_specs=pl.BlockSpec((tm, tn), lambda i, j, kk: (i, j)),
        out_shape=jax.ShapeDtypeStruct((m, n), out_dtype),
        scratch_shapes=[pltpu.VMEM((tm, tn), F32)],
        compiler_params=_cp("parallel", "parallel", "arbitrary"),
    )(*args)


def _mm_nt(a, b3, name, add=None, out_dtype=F32, blk0=0, nblk=None):
    m, n = a.shape
    nb_all, k, nbw = b3.shape
    nblk = nb_all - blk0 if nblk is None else nblk
    assert n == nblk * nbw
    tm = _pick(m, MM_TM, 16)
    tn = _pick(k, MM_TK, LANE)
    tk = _pick(nbw, MM_TN, LANE)
    per = nbw // tk
    nk = n // tk

    def body(*refs):
        if add is None:
            a_ref, b_ref, o_ref, acc = refs
        else:
            a_ref, b_ref, c_ref, o_ref, acc = refs
        kk = pl.program_id(2)

        @pl.when(kk == 0)
        def _():
            acc[...] = jnp.zeros_like(acc)

        acc[...] += lax.dot_general(a_ref[...].astype(BF), b_ref[...].astype(BF),
                                    (((1,), (1,)), ((), ())), preferred_element_type=F32)

        @pl.when(kk == nk - 1)
        def _():
            r = acc[...]
            if add is not None:
                r = r + c_ref[...]
            o_ref[...] = r.astype(out_dtype)

    in_specs = [pl.BlockSpec((tm, tk), lambda i, j, kk: (i, kk)),
                pl.BlockSpec((None, tn, tk), lambda i, j, kk: (blk0 + kk // per, j, kk % per))]
    args = [a, b3]
    if add is not None:
        in_specs.append(pl.BlockSpec((tm, tn), lambda i, j, kk: (i, j)))
        args.append(add)
    return pl.pallas_call(
        body, name=name, grid=(m // tm, k // tn, nk),
        in_specs=in_specs, out_specs=pl.BlockSpec((tm, tn), lambda i, j, kk: (i, j)),
        out_shape=jax.ShapeDtypeStruct((m, k), out_dtype),
        scratch_shapes=[pltpu.VMEM((tm, tn), F32)],
        compiler_params=_cp("parallel", "parallel", "arbitrary"),
    )(*args)


def _mm_tn(a, b, nblk, name, out_dtype=BF):
    s, m = a.shape
    s2, n = b.shape
    assert s == s2 and n % nblk == 0
    nbw = n // nblk
    tm = _pick(m, MM_TM, LANE)
    tn = _pick(nbw, MM_TN, LANE)
    ts = _pick(s, MM_TS, LANE)
    per = nbw // tn
    ns = s // ts

    def body(a_ref, b_ref, o_ref, acc):
        ss = pl.program_id(2)

        @pl.when(ss == 0)
        def _():
            acc[...] = jnp.zeros_like(acc)

        acc[...] += lax.dot_general(a_ref[...].astype(BF), b_ref[...].astype(BF),
                                    (((0,), (0,)), ((), ())), preferred_element_type=F32)

        @pl.when(ss == ns - 1)
        def _():
            o_ref[...] = acc[...].astype(out_dtype)

    return pl.pallas_call(
        body, name=name, grid=(m // tm, n // tn, ns),
        in_specs=[pl.BlockSpec((ts, tm), lambda i, j, ss: (ss, i)),
                  pl.BlockSpec((ts, tn), lambda i, j, ss: (ss, j))],
        out_specs=pl.BlockSpec((None, tm, tn), lambda i, j, ss: (j // per, i, j % per)),
        out_shape=jax.ShapeDtypeStruct((nblk, m, nbw), out_dtype),
        scratch_shapes=[pltpu.VMEM((tm, tn), F32)],
        compiler_params=_cp("parallel", "parallel", "arbitrary"),
    )(a, b)


def _rows8(rows, width):
    idx = lax.broadcasted_iota(jnp.int32, (SUB, width), 0)
    out = jnp.zeros((SUB, width), F32)
    for r, v in enumerate(rows):
        out = jnp.where(idx == r, v, out)
    return out


def _rms_fwd(x, g, width, col_blk, name):
    s = x.shape[0]
    tr = _pick(s, ROW_TILE, 16)

    def body(x_ref, g_ref, u_ref):
        xv = x_ref[...]
        r = lax.rsqrt(jnp.mean(xv * xv, axis=-1, keepdims=True) + NORM_EPS)
        u_ref[...] = ((xv * r) * g_ref[...]).astype(BF)

    return pl.pallas_call(
        body, name=name, grid=(s // tr,),
        in_specs=[pl.BlockSpec((tr, width), lambda i: (i, col_blk)),
                  pl.BlockSpec((1, width), lambda i: (0, 0))],
        out_specs=pl.BlockSpec((tr, width), lambda i: (i, 0)),
        out_shape=jax.ShapeDtypeStruct((s, width), BF),
        compiler_params=_cp("parallel"),
    )(x, g)


def _rms_bwd(x, du, g, width, col_blk, name, extra=None, out_dtype=F32):
    s = x.shape[0]
    tr = _pick(s, ROW_TILE_BWD, 16)

    def body(*refs):
        if extra is None:
            x_ref, du_ref, g_ref, dx_ref, dg_ref = refs
        else:
            x_ref, du_ref, g_ref, e_ref, dx_ref, dg_ref = refs
        i = pl.program_id(0)
        xv = x_ref[...]
        duv = du_ref[...].astype(F32)
        r = lax.rsqrt(jnp.mean(xv * xv, axis=-1, keepdims=True) + NORM_EPS)
        nv = xv * r
        dn = duv * g_ref[...]
        dx = r * (dn - nv * jnp.mean(dn * nv, axis=-1, keepdims=True))
        if extra is not None:
            dx = dx + e_ref[...]
        dx_ref[...] = dx.astype(out_dtype)

        @pl.when(i == 0)
        def _():
            dg_ref[...] = jnp.zeros_like(dg_ref)

        dg_ref[...] += _rows8([jnp.sum(duv * nv, axis=0, keepdims=True)], width)

    in_specs = [pl.BlockSpec((tr, width), lambda i: (i, col_blk)),
                pl.BlockSpec((tr, width), lambda i: (i, 0)),
                pl.BlockSpec((1, width), lambda i: (0, 0))]
    args = [x, du, g]
    if extra is not None:
        in_specs.append(pl.BlockSpec((tr, width), lambda i: (i, 0)))
        args.append(extra)
    return pl.pallas_call(
        body, name=name, grid=(s // tr,),
        in_specs=in_specs,
        out_specs=[pl.BlockSpec((tr, width), lambda i: (i, 0)),
                   pl.BlockSpec((SUB, width), lambda i: (0, 0))],
        out_shape=[jax.ShapeDtypeStruct((s, width), out_dtype), jax.ShapeDtypeStruct((SUB, width), F32)],
        compiler_params=_cp("arbitrary"),
    )(*args)


def _down(cur, prev8, k):
    ext = jnp.concatenate([prev8, cur], axis=0)
    return pltpu.roll(ext, k, axis=0)[SUB:]


def _up(cur, next8, k):
    ext = jnp.concatenate([cur, next8], axis=0)
    return pltpu.roll(ext, ext.shape[0] - k, axis=0)[:cur.shape[0]]


def _conv3(w_ref, cur, prev8):
    return w_ref[0:1, :] * _down(cur, prev8, 2) + w_ref[1:2, :] * _down(cur, prev8, 1) + w_ref[2:3, :] * cur


def _conv3_t(w_ref, cur, next8):
    return w_ref[2:3, :] * cur + w_ref[1:2, :] * _up(cur, next8, 1) + w_ref[0:1, :] * _up(cur, next8, 2)


def _spec_cur(tr, tc, c0):
    return pl.BlockSpec((tr, tc), lambda j, i: (i, c0 + j))


def _spec_prev(tr, tc, c0):
    return pl.BlockSpec((SUB, tc), lambda j, i: (jnp.maximum(i * (tr // SUB) - 1, 0), c0 + j))


def _spec_next(tr, tc, c0, s):
    return pl.BlockSpec((SUB, tc), lambda j, i: (jnp.minimum((i + 1) * (tr // SUB), s // SUB - 1), c0 + j))


def _spec_w(tc, c0):
    return pl.BlockSpec((SUB, tc), lambda j, i: (0, c0 + j))


def _pad8(w):
    return jnp.pad(w, ((0, SUB - w.shape[0]), (0, 0)))


def _conv_mix_fwd(z_a, cw8, conv):
    s = z_a.shape[0]
    tr = _pick(s, ROW_TILE, 16)
    tc = _pick(conv, COL_TILE, LANE)
    nc = conv // tc

    def body(zb_ref, zc_ref, zv_ref, zcp_ref, zvp_ref, w_ref, p_ref):
        i = pl.program_id(1)
        cv = zc_ref[...] * zv_ref[...]
        cvp = jnp.where(i > 0, zcp_ref[...] * zvp_ref[...], 0.0)
        p_ref[...] = (zb_ref[...] * _conv3(w_ref, cv, cvp)).astype(BF)

    return pl.pallas_call(
        body, name="conv_mix_fwd", grid=(nc, s // tr),
        in_specs=[_spec_cur(tr, tc, 0), _spec_cur(tr, tc, nc), _spec_cur(tr, tc, 2 * nc),
                  _spec_prev(tr, tc, nc), _spec_prev(tr, tc, 2 * nc), _spec_w(tc, 0)],
        out_specs=_spec_cur(tr, tc, 0),
        out_shape=jax.ShapeDtypeStruct((s, conv), BF),
        compiler_params=_cp("parallel", "parallel"),
    )(z_a, z_a, z_a, z_a, z_a, cw8)


def _conv_mix_bwd(z_a, d_p, cw8, conv):
    s = z_a.shape[0]
    tr = _pick(s, ROW_TILE_BWD, 16)
    tc = _pick(conv, COL_TILE, LANE)
    nc = conv // tc
    nr = s // tr

    def body(zb_ref, zbn_ref, zc_ref, zcp_ref, zv_ref, zvp_ref, dp_ref, dpn_ref, w_ref,
             dzb_ref, dzc_ref, dzv_ref, dw_ref):
        i = pl.program_id(1)
        zc = zc_ref[...]
        zv = zv_ref[...]
        cv = zc * zv
        cvp = jnp.where(i > 0, zcp_ref[...] * zvp_ref[...], 0.0)
        dpv = dp_ref[...]
        dzb_ref[...] = (dpv * _conv3(w_ref, cv, cvp)).astype(BF)
        dcc = dpv * zb_ref[...]
        dccn = jnp.where(i < nr - 1, dpn_ref[...] * zbn_ref[...], 0.0)
        dcv = _conv3_t(w_ref, dcc, dccn)
        dzc_ref[...] = (dcv * zv).astype(BF)
        dzv_ref[...] = (dcv * zc).astype(BF)

        @pl.when(i == 0)
        def _():
            dw_ref[...] = jnp.zeros_like(dw_ref)

        dw_ref[...] += _rows8([jnp.sum(dcc * _down(cv, cvp, 2), axis=0, keepdims=True),
                               jnp.sum(dcc * _down(cv, cvp, 1), axis=0, keepdims=True),
                               jnp.sum(dcc * cv, axis=0, keepdims=True)], tc)

    out = jax.ShapeDtypeStruct((s, conv), BF)
    return pl.pallas_call(
        body, name="conv_mix_bwd", grid=(nc, nr),
        in_specs=[_spec_cur(tr, tc, 0), _spec_next(tr, tc, 0, s),
                  _spec_cur(tr, tc, nc), _spec_prev(tr, tc, nc),
                  _spec_cur(tr, tc, 2 * nc), _spec_prev(tr, tc, 2 * nc),
                  _spec_cur(tr, tc, 0), _spec_next(tr, tc, 0, s), _spec_w(tc, 0)],
        out_specs=[_spec_cur(tr, tc, 0), _spec_cur(tr, tc, 0), _spec_cur(tr, tc, 0), _spec_w(tc, 0)],
        out_shape=[out, out, out, jax.ShapeDtypeStruct((SUB, conv), F32)],
        compiler_params=_cp("parallel", "arbitrary"),
    )(z_a, z_a, z_a, z_a, z_a, z_a, d_p, d_p, cw8)


def _silu_parts(ag):
    sg = jax.nn.sigmoid(ag)
    return ag * sg, sg


def _ffn_act_fwd(a_pre, cw8, cb, dff):
    s = a_pre.shape[0]
    tr = _pick(s, ROW_TILE, 16)
    tc = _pick(dff, COL_TILE, LANE)
    nc = dff // tc

    def body(xg_ref, xgp_ref, xu_ref, xup_ref, wg_ref, wu_ref, bg_ref, bu_ref, f_ref):
        i = pl.program_id(1)
        xgp = jnp.where(i > 0, xgp_ref[...], 0.0)
        xup = jnp.where(i > 0, xup_ref[...], 0.0)
        ag = _conv3(wg_ref, xg_ref[...], xgp) + bg_ref[...]
        au = _conv3(wu_ref, xu_ref[...], xup) + bu_ref[...]
        f_ref[...] = (_silu_parts(ag)[0] * au).astype(BF)

    return pl.pallas_call(
        body, name="ffn_act_fwd", grid=(nc, s // tr),
        in_specs=[_spec_cur(tr, tc, 0), _spec_prev(tr, tc, 0), _spec_cur(tr, tc, nc), _spec_prev(tr, tc, nc),
                  _spec_w(tc, 0), _spec_w(tc, nc),
                  pl.BlockSpec((1, tc), lambda j, i: (0, j)), pl.BlockSpec((1, tc), lambda j, i: (0, nc + j))],
        out_specs=_spec_cur(tr, tc, 0),
        out_shape=jax.ShapeDtypeStruct((s, dff), BF),
        compiler_params=_cp("parallel", "parallel"),
    )(a_pre, a_pre, a_pre, a_pre, cw8, cw8, cb, cb)


def _ffn_act_bwd(a_pre, d_f, cw8, cb, dff):
    s = a_pre.shape[0]
    tr = _pick(s, ROW_TILE_BWD, 16)
    tc = _pick(dff, COL_TILE, LANE)
    nc = dff // tc
    nr = s // tr

    def body(xg_ref, xgp_ref, xgn_ref, xu_ref, xup_ref, xun_ref, df_ref, dfn_ref,
             wg_ref, wu_ref, bg_ref, bu_ref, dxg_ref, dxu_ref, dwg_ref, dwu_ref):
        i = pl.program_id(1)
        xg = xg_ref[...]
        xu = xu_ref[...]
        xgp = jnp.where(i > 0, xgp_ref[...], 0.0)
        xup = jnp.where(i > 0, xup_ref[...], 0.0)

        def d_act(xg_t, xgp_t, xu_t, xup_t, df_t):
            ag = _conv3(wg_ref, xg_t, xgp_t) + bg_ref[...]
            au = _conv3(wu_ref, xu_t, xup_t) + bu_ref[...]
            sil, sg = _silu_parts(ag)
            return df_t * au * (sg * (1.0 + ag * (1.0 - sg))), df_t * sil

        dag, dau = d_act(xg, xgp, xu, xup, df_ref[...])
        dfn = jnp.where(i < nr - 1, dfn_ref[...], 0.0)
        dagn, daun = d_act(xgn_ref[...], xg[tr - SUB:], xun_ref[...], xu[tr - SUB:], dfn)
        dxg_ref[...] = _conv3_t(wg_ref, dag, dagn).astype(BF)
        dxu_ref[...] = _conv3_t(wu_ref, dau, daun).astype(BF)

        @pl.when(i == 0)
        def _():
            dwg_ref[...] = jnp.zeros_like(dwg_ref)
            dwu_ref[...] = jnp.zeros_like(dwu_ref)

        def wgrad(da, x, xp):
            return _rows8([jnp.sum(da * _down(x, xp, 2), axis=0, keepdims=True),
                           jnp.sum(da * _down(x, xp, 1), axis=0, keepdims=True),
                           jnp.sum(da * x, axis=0, keepdims=True),
                           jnp.sum(da, axis=0, keepdims=True)], tc)

        dwg_ref[...] += wgrad(dag, xg, xgp)
        dwu_ref[...] += wgrad(dau, xu, xup)

    half = jax.ShapeDtypeStruct((s, dff), BF)
    wsh = jax.ShapeDtypeStruct((SUB, dff), F32)
    return pl.pallas_call(
        body, name="ffn_act_bwd", grid=(nc, nr),
        in_specs=[_spec_cur(tr, tc, 0), _spec_prev(tr, tc, 0), _spec_next(tr, tc, 0, s),
                  _spec_cur(tr, tc, nc), _spec_prev(tr, tc, nc), _spec_next(tr, tc, nc, s),
                  _spec_cur(tr, tc, 0), _spec_next(tr, tc, 0, s),
                  _spec_w(tc, 0), _spec_w(tc, nc),
                  pl.BlockSpec((1, tc), lambda j, i: (0, j)), pl.BlockSpec((1, tc), lambda j, i: (0, nc + j))],
        out_specs=[_spec_cur(tr, tc, 0), _spec_cur(tr, tc, 0), _spec_w(tc, 0), _spec_w(tc, 0)],
        out_shape=[half, half, wsh, wsh],
        compiler_params=_cp("parallel", "arbitrary"),
    )(a_pre, a_pre, a_pre, a_pre, a_pre, a_pre, d_f, d_f, cw8, cw8, cb, cb)


def _gate_fwd(z_g, b_gate, yc, ym, d):
    s = z_g.shape[0]
    tr = _pick(s, ROW_TILE, 16)
    tc = _pick(d, COL_TILE, LANE)
    nc = d // tc

    def body(za_ref, zb_ref, ba_ref, bb_ref, yc_ref, ym_ref, o_ref):
        ga = jax.nn.sigmoid(za_ref[...] + ba_ref[...])
        gb = jax.nn.sigmoid(zb_ref[...] + bb_ref[...])
        o_ref[...] = (ga * yc_ref[...] + gb * ym_ref[...]).astype(BF)

    return pl.pallas_call(
        body, name="gate_fwd", grid=(nc, s // tr),
        in_specs=[_spec_cur(tr, tc, 0), _spec_cur(tr, tc, nc),
                  pl.BlockSpec((1, tc), lambda j, i: (0, j)), pl.BlockSpec((1, tc), lambda j, i: (0, nc + j)),
                  _spec_cur(tr, tc, 0), _spec_cur(tr, tc, 0)],
        out_specs=_spec_cur(tr, tc, 0),
        out_shape=jax.ShapeDtypeStruct((s, d), BF),
        compiler_params=_cp("parallel", "parallel"),
    )(z_g, z_g, b_gate, b_gate, yc, ym)


def _gate_bwd(d_mix, z_g, b_gate, yc, ym, d):
    s = z_g.shape[0]
    tr = _pick(s, ROW_TILE, 16)
    tc = _pick(d, COL_TILE, LANE)
    nc = d // tc

    def body(dm_ref, za_ref, zb_ref, ba_ref, bb_ref, yc_ref, ym_ref,
             dza_ref, dzb_ref, dyc_ref, dym_ref, dba_ref, dbb_ref):
        i = pl.program_id(1)
        dm = dm_ref[...]
        ga = jax.nn.sigmoid(za_ref[...] + ba_ref[...])
        gb = jax.nn.sigmoid(zb_ref[...] + bb_ref[...])
        dza = dm * yc_ref[...] * (ga * (1.0 - ga))
        dzb = dm * ym_ref[...] * (gb * (1.0 - gb))
        dza_ref[...] = dza.astype(BF)
        dzb_ref[...] = dzb.astype(BF)
        dyc_ref[...] = (dm * ga).astype(BF)
        dym_ref[...] = (dm * gb).astype(BF)

        @pl.when(i == 0)
        def _():
            dba_ref[...] = jnp.zeros_like(dba_ref)
            dbb_ref[...] = jnp.zeros_like(dbb_ref)

        dba_ref[...] += _rows8([jnp.sum(dza, axis=0, keepdims=True)], tc)
        dbb_ref[...] += _rows8([jnp.sum(dzb, axis=0, keepdims=True)], tc)

    act = jax.ShapeDtypeStruct((s, d), BF)
    bsh = jax.ShapeDtypeStruct((SUB, d), F32)
    return pl.pallas_call(
        body, name="gate_bwd", grid=(nc, s // tr),
        in_specs=[_spec_cur(tr, tc, 0), _spec_cur(tr, tc, 0), _spec_cur(tr, tc, nc),
                  pl.BlockSpec((1, tc), lambda j, i: (0, j)), pl.BlockSpec((1, tc), lambda j, i: (0, nc + j)),
                  _spec_cur(tr, tc, 0), _spec_cur(tr, tc, 0)],
        out_specs=[_spec_cur(tr, tc, 0)] * 4 + [_spec_w(tc, 0)] * 2,
        out_shape=[act, act, act, act, bsh, bsh],
        compiler_params=_cp("parallel", "arbitrary"),
    )(d_mix, z_g, z_g, b_gate, b_gate, yc, ym)


def _lay(v):
    z = jnp.zeros(v.shape[:-1] + (HALF,), v.dtype)
    return jnp.concatenate([v[..., :HALF], z, v[..., HALF:], z], axis=-1)


def _unlay(v):
    return jnp.concatenate([v[..., :HALF], v[..., 2 * HALF:3 * HALF]], axis=-1)


def _rope_tables(positions):
    s = positions.shape[0]
    tr = _pick(s, ROW_TILE, 8)
    inv_freq = ROPE_THETA ** (-jnp.arange(0, ROPE, 2, dtype=F32) / ROPE)
    consts = jnp.stack([_lay(jnp.concatenate([inv_freq, inv_freq])),
                        _lay(jnp.ones((ROPE,), F32)),
                        _lay(jnp.concatenate([-jnp.ones((HALF,), F32), jnp.ones((HALF,), F32)]))])
    consts = _pad8(consts)

    def body(p_ref, c_ref, cos_ref, sin_ref):
        ang = p_ref[...].astype(F32) * c_ref[0:1, :]
        cos_ref[...] = jnp.cos(ang) * c_ref[1:2, :]
        sin_ref[...] = jnp.sin(ang) * c_ref[2:3, :]

    tab = jax.ShapeDtypeStruct((s, LANE), F32)
    return pl.pallas_call(
        body, name="rope_tables", grid=(s // tr,),
        in_specs=[pl.BlockSpec((tr, 1), lambda i: (i, 0)), pl.BlockSpec((SUB, LANE), lambda i: (0, 0))],
        out_specs=[pl.BlockSpec((tr, LANE), lambda i: (i, 0))] * 2,
        out_shape=[tab, tab],
        compiler_params=_cp("parallel"),
    )(positions, consts)


def _rope(t, cos, sin):
    return t * cos + pltpu.roll(t, 2 * HALF, axis=1) * sin


def _rope_t(d, cos, sin):
    return d * cos + pltpu.roll(d * sin, 2 * HALF, axis=1)


def _head_fwd(q_raw, kv_raw, z_a, kr_blk, cos, sin, gains, heads):
    s = q_raw.shape[0]
    tr = _pick(s, HEAD_ROW_TILE, 16)
    hw = heads * LANE

    def body(q_ref, kv_ref, kr_ref, cos_ref, sin_ref, g_ref, qo_ref, ko_ref, vo_ref):
        cosv = cos_ref[...]
        sinv = sin_ref[...]
        krv = kr_ref[...]
        kr_ss = jnp.sum(krv * krv, axis=-1, keepdims=True)
        for h in range(heads):
            lo = h * LANE
            qn = q_ref[:, lo:lo + LANE]
            qr = q_ref[:, hw + lo:hw + lo + LANE]
            ss = jnp.sum(qn * qn, axis=-1, keepdims=True) + jnp.sum(qr * qr, axis=-1, keepdims=True)
            r = lax.rsqrt(ss / HEAD_QK + NORM_EPS)
            qo_ref[:, 2 * lo:2 * lo + LANE] = ((qn * r) * g_ref[0:1, :]).astype(BF)
            qo_ref[:, 2 * lo + LANE:2 * lo + 2 * LANE] = _rope((qr * r) * g_ref[1:2, :], cosv, sinv).astype(BF)
            kn = kv_ref[:, 2 * lo:2 * lo + LANE]
            ss = jnp.sum(kn * kn, axis=-1, keepdims=True) + kr_ss
            r = lax.rsqrt(ss / HEAD_QK + NORM_EPS)
            ko_ref[:, 2 * lo:2 * lo + LANE] = ((kn * r) * g_ref[2:3, :]).astype(BF)
            ko_ref[:, 2 * lo + LANE:2 * lo + 2 * LANE] = _rope((krv * r) * g_ref[3:4, :], cosv, sinv).astype(BF)
            vo_ref[:, lo:lo + LANE] = kv_ref[:, 2 * lo + LANE:2 * lo + 2 * LANE].astype(BF)

    row = lambda w: pl.BlockSpec((tr, w), lambda i: (i, 0))
    return pl.pallas_call(
        body, name="head_fwd", grid=(s // tr,),
        in_specs=[row(2 * hw), row(2 * hw), pl.BlockSpec((tr, LANE), lambda i: (i, kr_blk)),
                  row(LANE), row(LANE), pl.BlockSpec((SUB, LANE), lambda i: (0, 0))],
        out_specs=[row(2 * hw), row(2 * hw), row(hw)],
        out_shape=[jax.ShapeDtypeStruct((s, 2 * hw), BF), jax.ShapeDtypeStruct((s, 2 * hw), BF),
                   jax.ShapeDtypeStruct((s, hw), BF)],
        compiler_params=_cp("parallel"),
    )(q_raw, kv_raw, z_a, cos, sin, gains)


def _head_bwd(q_raw, kv_raw, z_a, kr_blk, cos, sin, gains, dq_att, dk_att, dv, heads):
    s = q_raw.shape[0]
    tr = _pick(s, HEAD_ROW_TILE_BWD, 16)
    hw = heads * LANE

    def body(q_ref, kv_ref, kr_ref, cos_ref, sin_ref, g_ref, dq_ref, dk_ref, dv_ref,
             dqr_ref, dkv_ref, dkr_ref, dg_ref):
        i = pl.program_id(0)
        cosv = cos_ref[...]
        sinv = sin_ref[...]
        krv = kr_ref[...]
        kr_ss = jnp.sum(krv * krv, axis=-1, keepdims=True)
        dkr = jnp.zeros((tr, LANE), F32)
        dgs = [jnp.zeros((1, LANE), F32) for _ in range(4)]

        def norm_bwd(xn, xr, ss, dn_out, dr_out, gn, gr):
            r = lax.rsqrt(ss / HEAD_QK + NORM_EPS)
            nn = xn * r
            nr = xr * r
            dt = _rope_t(dr_out, cosv, sinv)
            dnn = dn_out * gn
            dnr = dt * gr
            mean = (jnp.sum(dnn * nn, axis=-1, keepdims=True) + jnp.sum(dnr * nr, axis=-1, keepdims=True)) / HEAD_QK
            return (r * (dnn - nn * mean), r * (dnr - nr * mean),
                    jnp.sum(dn_out * nn, axis=0, keepdims=True), jnp.sum(dt * nr, axis=0, keepdims=True))

        for h in range(heads):
            lo = h * LANE
            qn = q_ref[:, lo:lo + LANE]
            qr = q_ref[:, hw + lo:hw + lo + LANE]
            ss = jnp.sum(qn * qn, axis=-1, keepdims=True) + jnp.sum(qr * qr, axis=-1, keepdims=True)
            dxn, dxr, g0, g1 = norm_bwd(qn, qr, ss, dq_ref[:, 2 * lo:2 * lo + LANE],
                                        dq_ref[:, 2 * lo + LANE:2 * lo + 2 * LANE], g_ref[0:1, :], g_ref[1:2, :])
            dqr_ref[:, lo:lo + LANE] = dxn.astype(BF)
            dqr_ref[:, hw + lo:hw + lo + LANE] = dxr.astype(BF)
            kn = kv_ref[:, 2 * lo:2 * lo + LANE]
            ss = jnp.sum(kn * kn, axis=-1, keepdims=True) + kr_ss
            dxn, dxr, g2, g3 = norm_bwd(kn, krv, ss, dk_ref[:, 2 * lo:2 * lo + LANE],
                                        dk_ref[:, 2 * lo + LANE:2 * lo + 2 * LANE], g_ref[2:3, :], g_ref[3:4, :])
            dkv_ref[:, 2 * lo:2 * lo + LANE] = dxn.astype(BF)
            dkv_ref[:, 2 * lo + LANE:2 * lo + 2 * LANE] = dv_ref[:, lo:lo + LANE].astype(BF)
            dkr = dkr + dxr
            dgs = [a + b for a, b in zip(dgs, (g0, g1, g2, g3))]
        dkr_ref[...] = dkr

        @pl.when(i == 0)
        def _():
            dg_ref[...] = jnp.zeros_like(dg_ref)

        dg_ref[...] += _rows8(dgs, LANE)

    row = lambda w: pl.BlockSpec((tr, w), lambda i: (i, 0))
    return pl.pallas_call(
        body, name="head_bwd", grid=(s // tr,),
        in_specs=[row(2 * hw), row(2 * hw), pl.BlockSpec((tr, LANE), lambda i: (i, kr_blk)),
                  row(LANE), row(LANE), pl.BlockSpec((SUB, LANE), lambda i: (0, 0)),
                  row(2 * hw), row(2 * hw), row(hw)],
        out_specs=[row(2 * hw), row(2 * hw), row(LANE), pl.BlockSpec((SUB, LANE), lambda i: (0, 0))],
        out_shape=[jax.ShapeDtypeStruct((s, 2 * hw), BF), jax.ShapeDtypeStruct((s, 2 * hw), BF),
                   jax.ShapeDtypeStruct((s, LANE), F32), jax.ShapeDtypeStruct((SUB, LANE), F32)],
        compiler_params=_cp("arbitrary"),
    )(q_raw, kv_raw, z_a, cos, sin, gains, dq_att, dk_att, dv)


def _causal_mask(nrows, ncols, row0):
    rows = lax.broadcasted_iota(jnp.int32, (nrows, ncols), 0) + row0
    cols = lax.broadcasted_iota(jnp.int32, (nrows, ncols), 1)
    return cols <= rows


def _attn_fwd(q_att, k_att, v, heads):
    s = q_att.shape[0]
    t = _pick(s, ATTN_TILE, LANE)
    nt = s // t
    th = t // 2
    scale = HEAD_QK ** -0.5

    def body(q_ref, k_ref, v_ref, o_ref, lse_ref, m_s, l_s, acc_s):
        i = pl.program_id(1)
        j = pl.program_id(2)

        @pl.when(j == 0)
        def _():
            m_s[...] = jnp.full_like(m_s, NEG_INF)
            l_s[...] = jnp.zeros_like(l_s)
            acc_s[...] = jnp.zeros_like(acc_s)

        def step(masked):
            for r0 in range(0, t, th):
                rows = slice(r0, r0 + th)
                sc = lax.dot_general(q_ref[rows, :], k_ref[...], (((1,), (1,)), ((), ())),
                                     preferred_element_type=F32) * scale
                if masked:
                    sc = jnp.where(_causal_mask(th, t, r0), sc, NEG_INF)
                m_prev = m_s[rows, :]
                m_new = jnp.maximum(m_prev, jnp.max(sc, axis=-1, keepdims=True))
                alpha = jnp.exp(m_prev - m_new)
                p = jnp.exp(sc - m_new)
                l_s[rows, :] = alpha * l_s[rows, :] + jnp.sum(p, axis=-1, keepdims=True)
                acc_s[rows, :] = alpha * acc_s[rows, :] + jnp.dot(p.astype(BF), v_ref[...],
                                                                  preferred_element_type=F32)
                m_s[rows, :] = m_new

        @pl.when(j < i)
        def _():
            step(False)

        @pl.when(j == i)
        def _():
            step(True)
            o_ref[...] = acc_s[...] / l_s[...]
            lse_ref[...] = m_s[...] + jnp.log(l_s[...])

    kv_idx = lambda h, i, j: (jnp.minimum(j, i), h)
    return pl.pallas_call(
        body, name="attn_fwd", grid=(heads, nt, nt),
        in_specs=[pl.BlockSpec((t, 2 * LANE), lambda h, i, j: (i, h)),
                  pl.BlockSpec((t, 2 * LANE), kv_idx),
                  pl.BlockSpec((t, LANE), kv_idx)],
        out_specs=[pl.BlockSpec((t, LANE), lambda h, i, j: (i, h)),
                   pl.BlockSpec((None, t, 1), lambda h, i, j: (h, i, 0))],
        out_shape=[jax.ShapeDtypeStruct((s, heads * LANE), F32), jax.ShapeDtypeStruct((heads, s, 1), F32)],
        scratch_shapes=[pltpu.VMEM((t, 1), F32), pltpu.VMEM((t, 1), F32), pltpu.VMEM((t, LANE), F32)],
        compiler_params=_cp("parallel", "parallel", "arbitrary"),
    )(q_att, k_att, v)


def _attn_bwd(q_att, k_att, v, o, lse, d_o, heads):
    s = q_att.shape[0]
    t = _pick(s, ATTN_TILE, LANE)
    nt = s // t
    scale = HEAD_QK ** -0.5

    def body(q_ref, k_ref, v_ref, do_ref, o_ref, lse_ref, dq_ref, dk_ref, dv_ref, dk_s, dv_s):
        j = pl.program_id(1)
        i = pl.program_id(2)

        @pl.when((j == 0) & (i == 0))
        def _():
            dq_ref[...] = jnp.zeros_like(dq_ref)

        @pl.when(i == j)
        def _():
            dk_s[...] = jnp.zeros_like(dk_s)
            dv_s[...] = jnp.zeros_like(dv_s)

        def step(masked):
            q = q_ref[...]
            k = k_ref[...]
            do = do_ref[...]
            sc = lax.dot_general(q, k, (((1,), (1,)), ((), ())), preferred_element_type=F32) * scale
            if masked:
                sc = jnp.where(_causal_mask(t, t, 0), sc, NEG_INF)
            p = jnp.exp(sc - lse_ref[...])
            dp = lax.dot_general(do, v_ref[...], (((1,), (1,)), ((), ())), preferred_element_type=F32)
            delta = jnp.sum(do.astype(F32) * o_ref[...], axis=-1, keepdims=True)
            ds = (p * (dp - delta) * scale).astype(BF)
            dv_s[...] += lax.dot_general(p.astype(BF), do, (((0,), (0,)), ((), ())), preferred_element_type=F32)
            dk_s[...] += lax.dot_general(ds, q, (((0,), (0,)), ((), ())), preferred_element_type=F32)
            rows = pl.ds(pl.multiple_of(i * t, t), t)
            dq_ref[rows, :] += jnp.dot(ds, k, preferred_element_type=F32)

        @pl.when(i > j)
        def _():
            step(False)

        @pl.when(i == j)
        def _():
            step(True)

        @pl.when(i == nt - 1)
        def _():
            dk_ref[...] = dk_s[...]
            dv_ref[...] = dv_s[...]

    q_idx = lambda h, j, i: (jnp.maximum(i, j), h)
    kv_idx = lambda h, j, i: (j, h)
    return pl.pallas_call(
        body, name="attn_bwd", grid=(heads, nt, nt),
        in_specs=[pl.BlockSpec((t, 2 * LANE), q_idx), pl.BlockSpec((t, 2 * LANE), kv_idx),
                  pl.BlockSpec((t, LANE), kv_idx), pl.BlockSpec((t, LANE), q_idx), pl.BlockSpec((t, LANE), q_idx),
                  pl.BlockSpec((None, t, 1), lambda h, j, i: (h, jnp.maximum(i, j), 0))],
        out_specs=[pl.BlockSpec((s, 2 * LANE), lambda h, j, i: (0, h)),
                   pl.BlockSpec((t, 2 * LANE), kv_idx), pl.BlockSpec((t, LANE), kv_idx)],
        out_shape=[jax.ShapeDtypeStruct((s, heads * 2 * LANE), F32),
                   jax.ShapeDtypeStruct((s, heads * 2 * LANE), F32),
                   jax.ShapeDtypeStruct((s, heads * LANE), F32)],
        scratch_shapes=[pltpu.VMEM((t, 2 * LANE), F32), pltpu.VMEM((t, LANE), F32)],
        compiler_params=_cp("parallel", "arbitrary", "arbitrary"),
    )(q_att, k_att, v, d_o, o, lse)


def _loss_head(y, target):
    s, d = y.shape
    tr = _pick(s, ROW_TILE, 8)

    def body(y_ref, t_ref, dy_ref, l_ref):
        i = pl.program_id(0)
        e = y_ref[...] - t_ref[...]
        dy_ref[...] = e / d

        @pl.when(i == 0)
        def _():
            l_ref[...] = jnp.zeros_like(l_ref)

        l_ref[...] += 0.5 * jnp.sum(jnp.mean(e * e, axis=-1, keepdims=True), axis=0, keepdims=True)

    return pl.pallas_call(
        body, name="loss_head", grid=(s // tr,),
        in_specs=[pl.BlockSpec((tr, d), lambda i: (i, 0))] * 2,
        out_specs=[pl.BlockSpec((tr, d), lambda i: (i, 0)), pl.BlockSpec((SUB, LANE), lambda i: (0, 0))],
        out_shape=[jax.ShapeDtypeStruct((s, d), F32), jax.ShapeDtypeStruct((SUB, LANE), F32)],
        compiler_params=_cp("arbitrary"),
    )(y, target)


def _sum_parts(parts, name):
    n, r, c = parts.shape
    tr = _pick(r, 512, 8)

    def body(p_ref, o_ref):
        g = p_ref[0].astype(F32)
        for k in range(1, n):
            g = g + p_ref[k].astype(F32)
        o_ref[...] = g

    return pl.pallas_call(
        body, name=name, grid=(r // tr,),
        in_specs=[pl.BlockSpec((n, tr, c), lambda i: (0, i, 0))],
        out_specs=pl.BlockSpec((tr, c), lambda i: (i, 0)),
        out_shape=jax.ShapeDtypeStruct((r, c), F32),
        compiler_params=_cp("parallel"),
    )(parts)


def _adamw(parts, w, m, v, name):
    n, r, c = parts.shape
    tr = _pick(r, 256, 8)

    def body(p_ref, w_ref, m_ref, v_ref, g_ref, d_ref, mo_ref, vo_ref):
        g = p_ref[0].astype(F32)
        for k in range(1, n):
            g = g + p_ref[k].astype(F32)
        m_new = ADAM_B1 * m_ref[...] + (1.0 - ADAM_B1) * g
        v_new = ADAM_B2 * v_ref[...] + (1.0 - ADAM_B2) * jnp.square(g)
        m_hat = m_new / (1.0 - ADAM_B1 ** ADAM_STEP)
        v_hat = v_new / (1.0 - ADAM_B2 ** ADAM_STEP)
        g_ref[...] = g
        d_ref[...] = -ADAM_LR * (m_hat / (jnp.sqrt(v_hat) + ADAM_EPS) + ADAM_WD * w_ref[...])
        mo_ref[...] = m_new
        vo_ref[...] = v_new

    spec = pl.BlockSpec((tr, c), lambda i: (i, 0))
    sh = jax.ShapeDtypeStruct((r, c), F32)
    return pl.pallas_call(
        body, name=name, grid=(r // tr,),
        in_specs=[pl.BlockSpec((n, tr, c), lambda i: (0, i, 0)), spec, spec, spec],
        out_specs=[spec] * 4, out_shape=[sh] * 4,
        compiler_params=_cp("parallel"),
    )(parts, w, m, v)


def _place():
    x, y, c = lax.axis_index("x"), lax.axis_index("y"), lax.axis_index("c")
    chips = [(1 - x, y), (x, 1 - y), (1 - x, 1 - y)]
    return x, y, c, chips


def _all_gather(shards, name):
    n = len(shards)

    def body(*refs):
        ins, outs = refs[:n], refs[n:2 * n]
        send_sems, recv_sems, local_sems = refs[2 * n:]
        x, y, c, chips = _place()
        me, sibling = (x, y, c), (x, y, 1 - c)

        def slot(w, p):
            return outs[w].at[4 * p[0] + 2 * p[1] + p[2]]

        def copy(w, k, block, to, src=None):
            return pltpu.make_async_remote_copy(
                src_ref=slot(w, block) if src is None else src, dst_ref=slot(w, block),
                send_sem=send_sems.at[w, k], recv_sem=recv_sems.at[w, k], device_id=to, device_id_type=MESH)

        first = []
        for w in range(n):
            first += [copy(w, 1 + j, me, (*chip, c), src=ins[w]) for j, chip in enumerate(chips)]
            first.append(copy(w, 0, me, sibling, src=ins[w]))
        for cp in first:
            cp.start()
        mine = [pltpu.make_async_copy(ins[w], slot(w, me), local_sems.at[w]) for w in range(n)]
        for cp in mine:
            cp.start()
        passed = []
        for w in range(n):
            for j, chip in enumerate(chips):
                copy(w, 1 + j, (*chip, c), me).wait_recv()
                cp = copy(w, 4 + j, (*chip, c), sibling)
                cp.start()
                passed.append(cp)
        for w in range(n):
            copy(w, 0, sibling, me).wait_recv()
            for j, chip in enumerate(chips):
                copy(w, 4 + j, (*chip, 1 - c), me).wait_recv()
        for cp in first + passed:
            cp.wait_send()
        for cp in mine:
            cp.wait()

    return pl.pallas_call(
        body, name=name,
        in_specs=[ANY] * n, out_specs=[ANY] * n,
        out_shape=[jax.ShapeDtypeStruct((N_DEV,) + a.shape, a.dtype) for a in shards],
        scratch_shapes=[pltpu.SemaphoreType.DMA((n, 7)), pltpu.SemaphoreType.DMA((n, 7)),
                        pltpu.SemaphoreType.DMA((n,))],
    )(*shards)


HBM = pl.BlockSpec(memory_space=pltpu.HBM)
SEM = pl.BlockSpec(memory_space=pltpu.SEMAPHORE)
EFFECT = pltpu.SideEffectType.DATAFLOW_SIDE_EFFECTING
PEERS = [(dx, dy, dc) for dx in (1, 0) for dy in (1, 0) for dc in (0, 1) if (dx, dy, dc) != (0, 0, 0)]


def _peer(x, y, c, flip):
    dx, dy, dc = flip
    return (1 - x if dx else x, 1 - y if dy else y, 1 - c if dc else c)


def _exchange_copies(srcs, lands, send, recv, loc, gather):
    x, y, c, _ = _place()
    me = 4 * x + 2 * y + c
    remote, local = [], []
    for w in range(len(srcs)):
        for k, flip in enumerate(PEERS):
            px, py, pc = _peer(x, y, c, flip)
            src = srcs[w] if gather else srcs[w].at[4 * px + 2 * py + pc]
            remote.append(pltpu.make_async_remote_copy(
                src_ref=src, dst_ref=lands[w].at[me], send_sem=send[w].at[k], recv_sem=recv[w].at[k],
                device_id=(px, py, pc), device_id_type=MESH))
        local.append(pltpu.make_async_copy(srcs[w] if gather else srcs[w].at[me], lands[w].at[me], loc[w]))
    return remote, local


class _Exchange:
    def __init__(self, srcs, lands, send, recv, loc, token, gather):
        self.srcs, self.lands, self.send, self.recv, self.loc = srcs, lands, send, recv, loc
        self.token, self.gather = token, gather


def _exchange_start(srcs, gather, name):
    n = len(srcs)
    land_shapes = [((N_DEV,) + a.shape) if gather else a.shape for a in srcs]
    lands = [pltpu.with_memory_space_constraint(lax.empty(sh, a.dtype), pltpu.HBM) for sh, a in zip(land_shapes, srcs)]
    srcs = [pltpu.with_memory_space_constraint(a, pltpu.HBM) for a in srcs]

    def body(*refs):
        src_refs, land_refs = refs[:n], refs[n:2 * n]
        outs = refs[2 * n:]
        send, recv, loc = outs[:n], outs[n:2 * n], outs[2 * n:3 * n]
        token = outs[-1]
        remote, local = _exchange_copies(src_refs, land_refs, send, recv, loc, gather)
        for cp in remote + local:
            cp.start()
        token[...] = jnp.zeros_like(token)

    out_shape = ([pltpu.SemaphoreType.DMA((len(PEERS),))] * (2 * n) + [pltpu.SemaphoreType.DMA(())] * n
                 + [pltpu.HBM(a.shape, a.dtype) for a in srcs] + [pltpu.HBM(a.shape, a.dtype) for a in lands]
                 + [jax.ShapeDtypeStruct((SUB, LANE), F32)])
    res = pl.pallas_call(
        body, name=name, out_shape=out_shape,
        in_specs=[HBM] * (2 * n),
        out_specs=[SEM] * (3 * n) + [HBM] * (2 * n) + [pl.BlockSpec(memory_space=pltpu.VMEM)],
        input_output_aliases={i: 3 * n + i for i in range(2 * n)},
        compiler_params=pltpu.CompilerParams(has_side_effects=EFFECT),
    )(*srcs, *lands)
    return _Exchange(res[3 * n:4 * n], res[4 * n:5 * n], res[:n], res[n:2 * n], res[2 * n:3 * n], res[-1], gather)


def _exchange_wait(ex, idxs, after, name):
    n = len(idxs)
    srcs = [ex.srcs[i] for i in idxs]
    lands = [ex.lands[i] for i in idxs]
    sems = [ex.send[i] for i in idxs] + [ex.recv[i] for i in idxs] + [ex.loc[i] for i in idxs]
    gather = ex.gather

    def body(*refs):
        src_refs, land_refs = refs[:n], refs[n:2 * n]
        send, recv, loc = refs[2 * n:3 * n], refs[3 * n:4 * n], refs[4 * n:5 * n]
        remote, local = _exchange_copies(src_refs, land_refs, send, recv, loc, gather)
        for cp in remote:
            cp.wait_send()
            cp.wait_recv()
        for cp in local:
            cp.wait()

    res = pl.pallas_call(
        body, name=name,
        out_shape=[pltpu.HBM(a.shape, a.dtype) for a in srcs] + [pltpu.HBM(a.shape, a.dtype) for a in lands],
        in_specs=[HBM] * (2 * n) + [SEM] * (3 * n) + [ANY],
        out_specs=[HBM] * (2 * n),
        input_output_aliases={i: i for i in range(2 * n)},
        compiler_params=pltpu.CompilerParams(has_side_effects=EFFECT),
    )(*srcs, *lands, *sems, after)
    return res[n:]


def _after(token, a):
    return a + token[0:1, 0:1].astype(a.dtype)


def _unblock(w3):
    nb, k, nbw = w3.shape
    return w3.transpose(1, 0, 2).reshape(k, nb * nbw)


def _block(w, nb):
    k, n = w.shape
    return w.reshape(k, nb, n // nb).transpose(1, 0, 2)


def kernel(x, positions, ln1_g, w_in, b_gate, conv_w, w_conv_out, q_a_g, w_q_b, kv_a_g, w_kv_b, q_norm_g, k_norm_g, w_mla_out, w_o, ln2_g, w_ffn_up, ffn_conv_w, ffn_conv_b, w_ffn_down, loss_target, m_ln1_g, m_w_in, m_b_gate, m_conv_w, m_w_conv_out, m_q_a_g, m_w_q_b, m_kv_a_g, m_w_kv_b, m_q_norm_g, m_k_norm_g, m_w_mla_out, m_w_o, m_ln2_g, m_w_ffn_up, m_ffn_conv_w, m_ffn_conv_b, m_w_ffn_down, v_ln1_g, v_w_in, v_b_gate, v_conv_w, v_w_conv_out, v_q_a_g, v_w_q_b, v_kv_a_g, v_w_kv_b, v_q_norm_g, v_k_norm_g, v_w_mla_out, v_w_o, v_ln2_g, v_w_ffn_up, v_ffn_conv_w, v_ffn_conv_b, v_w_ffn_down):
    s, d = x.shape[1], x.shape[2]
    conv = conv_w.shape[2] * N_DEV
    ql, kvl = q_a_g.shape[1], kv_a_g.shape[1]
    heads = w_q_b.shape[2] * N_DEV // HEAD_QK
    dff = w_ffn_down.shape[1] * N_DEV
    hw = heads * LANE
    conv3 = 3 * conv
    kr_off = conv3 + ql
    kv_off = -(-(kr_off + LANE) // kvl) * kvl
    wa = kv_off + kvl
    assert conv3 % ql == 0 and kr_off % LANE == 0
    xs = x[0]
    tgt = loss_target[0]
    pos = positions.reshape(s, 1)

    big = dict(w_in=w_in[0], w_conv_out=w_conv_out[0], w_q_b=w_q_b[0], w_kv_b=w_kv_b[0],
               w_mla_out=w_mla_out[0], w_o=w_o[0], w_ffn_up=w_ffn_up[0], w_ffn_down=w_ffn_down[0])
    names = list(big)
    rest = names[1:]
    first = _all_gather([big["w_in"].astype(BF), _pad8(conv_w[0]), _pad8(ffn_conv_w[0])], "gather_w_in")
    cw8 = _unblock(first[1])
    fcw8 = _unblock(first[2])
    ag = _exchange_start([big[k].astype(BF) for k in rest], True, "gather_rest_start")

    def landed(keys, after, name):
        return _exchange_wait(ag, [rest.index(k) for k in keys], after, name)

    w_in_full = _unblock(first[0])
    zpad = jnp.zeros((d, kv_off - kr_off - LANE), BF)
    w_a = jnp.concatenate([w_in_full[:, :kr_off], _lay(w_in_full[:, kr_off + kvl:kr_off + kvl + ROPE]), zpad,
                           w_in_full[:, kr_off:kr_off + kvl]], axis=1)[None]
    g_off = kr_off + kvl + ROPE
    w_g = _block(w_in_full[:, g_off:], 2)
    gains = _pad8(jnp.concatenate([q_norm_g[:, :NOPE], _lay(q_norm_g[:, NOPE:]),
                                   k_norm_g[:, :NOPE], _lay(k_norm_g[:, NOPE:])], axis=0))
    kr_blk = kr_off // LANE

    cos, sin = _rope_tables(pos)
    u1 = _rms_fwd(xs, _after(ag.token, ln1_g), d, 0, "rms1_fwd")
    z_a = _mm_nn(u1, w_a, "mm_z_a")
    z_g = _mm_nn(u1, w_g, "mm_z_g")
    p = _conv_mix_fwd(z_a, cw8, conv)
    w_co, w_qb, w_kv = landed(["w_conv_out", "w_q_b", "w_kv_b"], p, "gather_wait_mixers")
    wq_full = _unblock(w_qb).reshape(ql, heads, HEAD_QK)
    w_q = jnp.concatenate([wq_full[:, :, :NOPE].reshape(ql, hw), _lay(wq_full[:, :, NOPE:]).reshape(ql, hw)],
                          axis=1)[None]
    yc = _mm_nn(p, w_co, "mm_y_conv")
    qn = _rms_fwd(z_a, q_a_g, ql, conv3 // ql, "rms_q_fwd")
    kvn = _rms_fwd(z_a, kv_a_g, kvl, kv_off // kvl, "rms_kv_fwd")
    q_raw = _mm_nn(qn, w_q, "mm_q")
    kv_raw = _mm_nn(kvn, w_kv, "mm_kv")
    q_att, k_att, v_bf = _head_fwd(q_raw, kv_raw, z_a, kr_blk, cos, sin, gains, heads)
    o, lse = _attn_fwd(q_att, k_att, v_bf, heads)
    w_mo, w_oo, w_up, w_dn = landed(["w_mla_out", "w_o", "w_ffn_up", "w_ffn_down"], lse, "gather_wait_outs")
    w_mo = w_mo.reshape(1, hw, d)
    w_oo = w_oo.reshape(1, d, d)
    w_dn = w_dn.reshape(1, dff, d)
    ym = _mm_nn(o, w_mo, "mm_y_mla")
    mix = _gate_fwd(z_g, b_gate, yc, ym, d)
    h1 = _mm_nn(mix, w_oo, "mm_h1", add=xs)
    u2 = _rms_fwd(h1, ln2_g, d, 0, "rms2_fwd")
    a_pre = _mm_nn(u2, w_up, "mm_ffn_up")
    f = _ffn_act_fwd(a_pre, fcw8, ffn_conv_b, dff)
    y = _mm_nn(f, w_dn, "mm_ffn_down", add=h1)
    dy, loss_part = _loss_head(y, tgt)

    d_f = _mm_nt(dy, w_dn, "mm_d_f")
    g_dn = _mm_tn(f, dy, 1, "mm_g_ffn_down").reshape(N_DEV, dff // N_DEV, d)
    rs_dn = _exchange_start([g_dn], False, "reduce_ffn_down_start")
    d_xg, d_xu, dfw_g, dfw_u = _ffn_act_bwd(a_pre, d_f, fcw8, _after(rs_dn.token, ffn_conv_b), dff)
    half = N_DEV // 2
    d_u2 = _mm_nt(d_xg, w_up, "mm_d_u2_gate", blk0=0, nblk=half)
    d_u2 = _mm_nt(d_xu, w_up, "mm_d_u2_up", blk0=half, nblk=half, add=d_u2)
    g_up = jnp.concatenate([_mm_tn(u2, d_xg, half, "mm_g_ffn_up_gate"), _mm_tn(u2, d_xu, half, "mm_g_ffn_up_up")], axis=0)
    rs_up = _exchange_start([g_up], False, "reduce_ffn_up_start")
    d_h1, dg_ln2 = _rms_bwd(h1, d_u2, _after(rs_up.token, ln2_g), d, 0, "rms2_bwd", extra=dy)
    d_mix = _mm_nt(d_h1, w_oo, "mm_d_mix")
    g_oo = _mm_tn(mix, d_h1, 1, "mm_g_w_o").reshape(N_DEV, d // N_DEV, d)
    d_zga, d_zgb, d_yc, d_ym, dba, dbb = _gate_bwd(d_mix, z_g, b_gate, yc, ym, d)
    d_p = _mm_nt(d_yc, w_co, "mm_d_p")
    g_co = _mm_tn(p, d_yc, N_DEV, "mm_g_conv_out")
    d_o = _mm_nt(d_ym, w_mo, "mm_d_o", out_dtype=BF)
    g_mo = _mm_tn(o, d_ym, 1, "mm_g_mla_out").reshape(N_DEV, hw // N_DEV, d)
    rs_mix = _exchange_start([g_oo, g_co, g_mo], False, "reduce_mixers_start")
    d_zb, d_zc, d_zv, dcw = _conv_mix_bwd(z_a, d_p, _after(rs_mix.token, cw8), conv)
    dq_att, dk_att, dv = _attn_bwd(q_att, k_att, v_bf, o, lse, d_o, heads)
    d_q_raw, d_kv_raw, d_kr, dgains = _head_bwd(q_raw, kv_raw, z_a, kr_blk, cos, sin, gains, dq_att, dk_att, dv, heads)
    d_qn = _mm_nt(d_q_raw, w_q, "mm_d_qn")
    g_q2 = _mm_tn(qn, d_q_raw, 1, "mm_g_q")[0]
    g_qb = _block(jnp.concatenate([g_q2[:, :hw].reshape(ql, heads, NOPE),
                                   _unlay(g_q2[:, hw:].reshape(ql, heads, LANE))], axis=2).reshape(ql, heads * HEAD_QK), N_DEV)
    d_kvn = _mm_nt(d_kv_raw, w_kv, "mm_d_kvn")
    g_kv = _mm_tn(kvn, d_kv_raw, N_DEV, "mm_g_kv")
    rs_qkv = _exchange_start([g_qb, g_kv], False, "reduce_qkv_start")
    d_ql, dg_qa = _rms_bwd(z_a, d_qn, _after(rs_qkv.token, q_a_g), ql, conv3 // ql, "rms_q_bwd", out_dtype=BF)
    d_kvl, dg_kva = _rms_bwd(z_a, d_kvn, kv_a_g, kvl, kv_off // kvl, "rms_kv_bwd", out_dtype=BF)
    d_z_a = jnp.concatenate([d_zb, d_zc, d_zv, d_ql, d_kr.astype(BF), jnp.zeros((s, kv_off - kr_off - LANE), BF),
                             d_kvl], axis=1)
    g_a = _mm_tn(u1, d_z_a, 1, "mm_g_w_a")[0]
    g_ga = _mm_tn(u1, d_zga, 1, "mm_g_w_ga")[0]
    g_gb = _mm_tn(u1, d_zgb, 1, "mm_g_w_gb")[0]
    g_in = _block(jnp.concatenate([g_a[:, :kr_off], g_a[:, kv_off:kv_off + kvl],
                                   _unlay(g_a[:, kr_off:kr_off + LANE]), g_ga, g_gb], axis=1), N_DEV)
    rs_in = _exchange_start([g_in], False, "reduce_w_in_start")
    d_u1 = _mm_nt(d_z_a, w_a, "mm_d_u1_a")
    d_u1 = _mm_nt(d_zga, w_g, "mm_d_u1_ga", blk0=0, nblk=1, add=d_u1)
    d_u1 = _mm_nt(d_zgb, w_g, "mm_d_u1_gb", blk0=1, nblk=1, add=d_u1)
    grad_x, dg_ln1 = _rms_bwd(xs, d_u1, _after(rs_in.token, ln1_g), d, 0, "rms1_bwd", extra=d_h1)

    small = dict(ln1_g=dg_ln1[0:1], b_gate=jnp.concatenate([dba[0:1], dbb[0:1]], axis=1), q_a_g=dg_qa[0:1],
                 kv_a_g=dg_kva[0:1],
                 q_norm_g=jnp.concatenate([dgains[0:1], _unlay(dgains[1:2])], axis=1),
                 k_norm_g=jnp.concatenate([dgains[2:3], _unlay(dgains[3:4])], axis=1),
                 ln2_g=dg_ln2[0:1], ffn_conv_b=jnp.concatenate([dfw_g[3:4], dfw_u[3:4]], axis=1))
    small_names = list(small)
    extra = [dcw[0:3].reshape(1, -1), jnp.concatenate([dfw_g[0:3], dfw_u[0:3]], axis=1).reshape(1, -1),
             loss_part[0:1, 0:1]]
    flat = jnp.concatenate([small[k] for k in small_names] + extra, axis=1)
    n_flat = flat.shape[1]
    rows = -(-n_flat // (SUB * LANE)) * SUB
    flat = jnp.pad(flat, ((0, 0), (0, rows * LANE - n_flat))).reshape(rows, LANE)
    total = _sum_parts(_all_gather([flat], "gather_small")[0], "sum_small").reshape(1, rows * LANE)
    off = 0
    small_g = {}
    for k in small_names:
        small_g[k] = total[:, off:off + small[k].shape[1]]
        off += small[k].shape[1]
    me = 4 * lax.axis_index("x") + 2 * lax.axis_index("y") + lax.axis_index("c")
    cwn, fcwn = conv // N_DEV, 2 * dff // N_DEV
    g_cw = lax.dynamic_slice_in_dim(total[:, off:off + 3 * conv].reshape(3, conv), me * cwn, cwn, axis=1)
    off += 3 * conv
    g_fcw = lax.dynamic_slice_in_dim(total[:, off:off + 6 * dff].reshape(3, 2 * dff), me * fcwn, fcwn, axis=1)
    off += 6 * dff
    loss = total[0, off]

    summed = {}
    summed["w_ffn_down"], = _exchange_wait(rs_dn, [0], total, "reduce_ffn_down_wait")
    summed["w_ffn_up"], = _exchange_wait(rs_up, [0], total, "reduce_ffn_up_wait")
    summed["w_o"], summed["w_conv_out"], summed["w_mla_out"] = _exchange_wait(rs_mix, [0, 1, 2], total, "reduce_mixers_wait")
    summed["w_q_b"], summed["w_kv_b"] = _exchange_wait(rs_qkv, [0, 1], total, "reduce_qkv_wait")

    loc = locals()
    out = {}
    for k in rest:
        out[k] = _adamw(summed[k], big[k], loc["m_" + k][0], loc["v_" + k][0], "adamw_" + k)
    summed["w_in"], = _exchange_wait(rs_in, [0], out[rest[-1]][0], "reduce_w_in_wait")
    out["w_in"] = _adamw(summed["w_in"], big["w_in"], m_w_in[0], v_w_in[0], "adamw_w_in")
    small_w = dict(ln1_g=ln1_g, b_gate=b_gate, q_a_g=q_a_g, kv_a_g=kv_a_g, q_norm_g=q_norm_g, k_norm_g=k_norm_g,
                   ln2_g=ln2_g, ffn_conv_b=ffn_conv_b, conv_w=conv_w[0].reshape(1, -1),
                   ffn_conv_w=ffn_conv_w[0].reshape(1, -1))
    small_g["conv_w"] = g_cw.reshape(1, -1)
    small_g["ffn_conv_w"] = g_fcw.reshape(1, -1)
    packed_names = list(small_w)

    def pack(get):
        vflat = jnp.concatenate([get(k).reshape(1, -1) for k in packed_names], axis=1)
        nr = -(-vflat.shape[1] // (SUB * LANE)) * SUB
        return jnp.pad(vflat, ((0, 0), (0, nr * LANE - vflat.shape[1])), constant_values=1.0).reshape(nr, LANE)

    res = _adamw(pack(lambda k: small_g[k])[None], pack(lambda k: small_w[k]), pack(lambda k: loc["m_" + k]),
                 pack(lambda k: loc["v_" + k]), "adamw_small")
    res = [r.reshape(1, -1) for r in res]
    off = 0
    for k in packed_names:
        shape = loc[k].shape
        size = small_w[k].shape[1]
        out[k] = [r[:, off:off + size].reshape(shape) for r in res]
        off += size
    for k in names:
        out[k] = [r[None] for r in out[k]]

    order = ["ln1_g", "w_in", "b_gate", "conv_w", "w_conv_out", "q_a_g", "w_q_b", "kv_a_g", "w_kv_b", "q_norm_g",
             "k_norm_g", "w_mla_out", "w_o", "ln2_g", "w_ffn_up", "ffn_conv_w", "ffn_conv_b", "w_ffn_down"]
    return (loss, grad_x[None], *[out[k][0] for k in order], *[out[k][1] for k in order],
            *[out[k][2] for k in order], *[out[k][3] for k in order])
```

```python
import functools

import jax
import jax.numpy as jnp
from jax import lax
from jax.experimental import pallas as pl
from jax.experimental.pallas import tpu as pltpu

BF = jnp.bfloat16
F32 = jnp.float32
MESH = pl.DeviceIdType.MESH
N_DEV = 8

NOPE = 128
ROPE = 64
HALF = ROPE // 2
HEAD_QK = NOPE + ROPE
HEAD_V = 128
LANE = 128
SUB = 8
NORM_EPS = 1e-6
NEG_INF = -1e30
ROPE_THETA = 10000.0
ADAM_LR = 0.001
ADAM_B1 = 0.9
ADAM_B2 = 0.999
ADAM_EPS = 1e-08
ADAM_WD = 0.01
ADAM_STEP = 10

VMEM_LIMIT = 52 * 1024 * 1024
MM_TM, MM_TN, MM_TK, MM_TS = 1024, 1536, 2048, 1024
ROW_TILE, ROW_TILE_BWD = 512, 256
HEAD_ROW_TILE, HEAD_ROW_TILE_BWD = 256, 128
COL_TILE = 512
ATTN_TILE = 512
ANY = pl.BlockSpec(memory_space=pl.ANY)


def _pick(n, target, mult):
    t = (min(n, target) // mult) * mult
    while t > 0:
        if n % t == 0:
            return t
        t -= mult
    raise ValueError(f"no tile for {n} (target {target}, multiple {mult})")


def _cp(*sem):
    return pltpu.CompilerParams(dimension_semantics=sem, vmem_limit_bytes=VMEM_LIMIT)


def _accumulate(kk, nk, acc, part, finish):
    if nk == 1:
        finish(part())
        return

    @pl.when(kk == 0)
    def _():
        acc[...] = part()

    @pl.when((kk > 0) & (kk < nk - 1))
    def _():
        acc[...] += part()

    @pl.when(kk == nk - 1)
    def _():
        finish(acc[...] + part())


def _mm_call(body, name, grid, in_specs, args, out_spec, out_shape, acc_shape, nk, dep):
    if dep is not None:
        in_specs = in_specs + [ANY]
        args = args + [dep]
    return pl.pallas_call(
        body, name=name, grid=grid, in_specs=in_specs, out_specs=out_spec, out_shape=out_shape,
        scratch_shapes=[pltpu.VMEM(acc_shape, F32)] if nk > 1 else [],
        compiler_params=_cp("parallel", "parallel", "arbitrary"),
    )(*args)


def _mm_nn(a, b3, name, add=None, out_dtype=F32, blk0=0, nblk=None, dep=None):
    m, k = a.shape
    nb_all, k2, nbw = b3.shape
    assert k == k2
    nblk = nb_all - blk0 if nblk is None else nblk
    n = nblk * nbw
    tm = _pick(m, MM_TM, 16)
    tn = _pick(nbw, MM_TN, LANE)
    tk = _pick(k, MM_TK, LANE)
    per = nbw // tn
    nk = k // tk

    def body(*refs):
        a_ref, b_ref = refs[:2]
        c_ref = refs[2] if add is not None else None
        o_ref = refs[2 + (add is not None) + (dep is not None)]
        acc = refs[-1]

        def part():
            return jnp.dot(a_ref[...].astype(BF), b_ref[...].astype(BF), preferred_element_type=F32)

        def finish(r):
            if add is not None:
                r = r + c_ref[...]
            o_ref[...] = r.astype(out_dtype)

        _accumulate(pl.program_id(2), nk, acc, part, finish)

    in_specs = [pl.BlockSpec((tm, tk), lambda i, j, kk: (i, kk)),
                pl.BlockSpec((None, tk, tn), lambda i, j, kk: (blk0 + j // per, kk, j % per))]
    args = [a, b3]
    if add is not None:
        in_specs.append(pl.BlockSpec((tm, tn), lambda i, j, kk: (i, j)))
        args.append(add)
    return _mm_call(body, name, (m // tm, n // tn, nk), in_specs, args,
                    pl.BlockSpec((tm, tn), lambda i, j, kk: (i, j)), jax.ShapeDtypeStruct((m, n), out_dtype),
                    (tm, tn), nk, dep)


def _mm_nt(a, b3, name, add=None, out_dtype=F32, blk0=0, nblk=None, dep=None):
    m, n = a.shape
    nb_all, k, nbw = b3.shape
    nblk = nb_all - blk0 if nblk is None else nblk
    assert n == nblk * nbw
    tm = _pick(m, MM_TM, 16)
    tn = _pick(k, MM_TN, LANE)
    tk = _pick(nbw, MM_TK, LANE)
    per = nbw // tk
    nk = n // tk

    def body(*refs):
        a_ref, b_ref = refs[:2]
        c_ref = refs[2] if add is not None else None
        o_ref = refs[2 + (add is not None) + (dep is not None)]
        acc = refs[-1]

        def part():
            return lax.dot_general(a_ref[...].astype(BF), b_ref[...].astype(BF),
                                   (((1,), (1,)), ((), ())), preferred_element_type=F32)

        def finish(r):
            if add is not None:
                r = r + c_ref[...]
            o_ref[...] = r.astype(out_dtype)

        _accumulate(pl.program_id(2), nk, acc, part, finish)

    in_specs = [pl.BlockSpec((tm, tk), lambda i, j, kk: (i, kk)),
                pl.BlockSpec((None, tn, tk), lambda i, j, kk: (blk0 + kk // per, j, kk % per))]
    args = [a, b3]
    if add is not None:
        in_specs.append(pl.BlockSpec((tm, tn), lambda i, j, kk: (i, j)))
        args.append(add)
    return _mm_call(body, name, (m // tm, k // tn, nk), in_specs, args,
                    pl.BlockSpec((tm, tn), lambda i, j, kk: (i, j)), jax.ShapeDtypeStruct((m, k), out_dtype),
                    (tm, tn), nk, dep)


def _mm_tn(a, b, nblk, name, out_dtype=BF, dep=None):
    s, m = a.shape
    s2, n = b.shape
    assert s == s2 and n % nblk == 0
    nbw = n // nblk
    tm = _pick(m, MM_TN, LANE)
    tn = _pick(nbw, MM_TN, LANE)
    ts = _pick(s, MM_TS, LANE)
    per = nbw // tn
    ns = s // ts

    def body(*refs):
        a_ref, b_ref = refs[:2]
        o_ref = refs[2 + (dep is not None)]
        acc = refs[-1]

        def part():
            return lax.dot_general(a_ref[...].astype(BF), b_ref[...].astype(BF),
                                   (((0,), (0,)), ((), ())), preferred_element_type=F32)

        def finish(r):
            o_ref[...] = r.astype(out_dtype)

        _accumulate(pl.program_id(2), ns, acc, part, finish)

    in_specs = [pl.BlockSpec((ts, tm), lambda i, j, ss: (ss, i)),
                pl.BlockSpec((ts, tn), lambda i, j, ss: (ss, j))]
    return _mm_call(body, name, (m // tm, n // tn, ns), in_specs, [a, b],
                    pl.BlockSpec((None, tm, tn), lambda i, j, ss: (j // per, i, j % per)),
                    jax.ShapeDtypeStruct((nblk, m, nbw), out_dtype), (tm, tn), ns, dep)


def _rows8(rows, width):
    idx = lax.broadcasted_iota(jnp.int32, (SUB, width), 0)
    out = jnp.zeros((SUB, width), F32)
    for r, v in enumerate(rows):
        out = jnp.where(idx == r, v, out)
    return out


def _rms_fwd(x, g, width, col_blk, name):
    s = x.shape[0]
    tr = _pick(s, ROW_TILE, 16)

    def body(x_ref, g_ref, u_ref):
        xv = x_ref[...]
        r = lax.rsqrt(jnp.mean(xv * xv, axis=-1, keepdims=True) + NORM_EPS)
        u_ref[...] = ((xv * r) * g_ref[...]).astype(BF)

    return pl.pallas_call(
        body, name=name, grid=(s // tr,),
        in_specs=[pl.BlockSpec((tr, width), lambda i: (i, col_blk)),
                  pl.BlockSpec((1, width), lambda i: (0, 0))],
        out_specs=pl.BlockSpec((tr, width), lambda i: (i, 0)),
        out_shape=jax.ShapeDtypeStruct((s, width), BF),
        compiler_params=_cp("parallel"),
    )(x, g)


def _rms_bwd(x, du, g, width, col_blk, name, extra=None, out_dtype=F32, also_bf16=False):
    s = x.shape[0]
    tr = _pick(s, ROW_TILE_BWD, 16)

    def body(*refs):
        x_ref, du_ref, g_ref = refs[:3]
        e_ref = refs[3] if extra is not None else None
        dx_ref = refs[3 + (extra is not None)]
        dxb_ref = refs[4 + (extra is not None)] if also_bf16 else None
        dg_ref = refs[-1]
        i = pl.program_id(0)
        xv = x_ref[...]
        duv = du_ref[...].astype(F32)
        r = lax.rsqrt(jnp.mean(xv * xv, axis=-1, keepdims=True) + NORM_EPS)
        nv = xv * r
        dn = duv * g_ref[...]
        dx = r * (dn - nv * jnp.mean(dn * nv, axis=-1, keepdims=True))
        if extra is not None:
            dx = dx + e_ref[...]
        dx_ref[...] = dx.astype(out_dtype)
        if also_bf16:
            dxb_ref[...] = dx.astype(BF)

        @pl.when(i == 0)
        def _():
            dg_ref[...] = jnp.zeros_like(dg_ref)

        dg_ref[...] += _rows8([jnp.sum(duv * nv, axis=0, keepdims=True)], width)

    in_specs = [pl.BlockSpec((tr, width), lambda i: (i, col_blk)),
                pl.BlockSpec((tr, width), lambda i: (i, 0)),
                pl.BlockSpec((1, width), lambda i: (0, 0))]
    args = [x, du, g]
    if extra is not None:
        in_specs.append(pl.BlockSpec((tr, width), lambda i: (i, 0)))
        args.append(extra)
    return pl.pallas_call(
        body, name=name, grid=(s // tr,),
        in_specs=in_specs,
        out_specs=[pl.BlockSpec((tr, width), lambda i: (i, 0))] * (1 + also_bf16)
        + [pl.BlockSpec((SUB, width), lambda i: (0, 0))],
        out_shape=[jax.ShapeDtypeStruct((s, width), out_dtype)] + [jax.ShapeDtypeStruct((s, width), BF)] * also_bf16
        + [jax.ShapeDtypeStruct((SUB, width), F32)],
        compiler_params=_cp("arbitrary"),
    )(*args)


def _down(cur, prev8, k):
    ext = jnp.concatenate([prev8, cur], axis=0)
    return pltpu.roll(ext, k, axis=0)[SUB:]


def _up(cur, next8, k):
    ext = jnp.concatenate([cur, next8], axis=0)
    return pltpu.roll(ext, ext.shape[0] - k, axis=0)[:cur.shape[0]]


def _conv3(w_ref, cur, prev8):
    return w_ref[0:1, :] * _down(cur, prev8, 2) + w_ref[1:2, :] * _down(cur, prev8, 1) + w_ref[2:3, :] * cur


def _conv3_t(w_ref, cur, next8):
    return w_ref[2:3, :] * cur + w_ref[1:2, :] * _up(cur, next8, 1) + w_ref[0:1, :] * _up(cur, next8, 2)


def _spec_cur(tr, tc, c0):
    return pl.BlockSpec((tr, tc), lambda j, i: (i, c0 + j))


def _spec_prev(tr, tc, c0):
    return pl.BlockSpec((SUB, tc), lambda j, i: (jnp.maximum(i * (tr // SUB) - 1, 0), c0 + j))


def _spec_next(tr, tc, c0, s):
    return pl.BlockSpec((SUB, tc), lambda j, i: (jnp.minimum((i + 1) * (tr // SUB), s // SUB - 1), c0 + j))


def _spec_w(tc, c0):
    return pl.BlockSpec((SUB, tc), lambda j, i: (0, c0 + j))


def _pad8(w):
    return jnp.pad(w, ((0, SUB - w.shape[0]), (0, 0)))


def _conv_mix_fwd(z_a, cw8, conv):
    s = z_a.shape[0]
    tr = _pick(s, ROW_TILE, 16)
    tc = _pick(conv, COL_TILE, LANE)
    nc = conv // tc

    def body(zb_ref, zc_ref, zv_ref, zcp_ref, zvp_ref, w_ref, p_ref):
        i = pl.program_id(1)
        cv = zc_ref[...] * zv_ref[...]
        cvp = jnp.where(i > 0, zcp_ref[...] * zvp_ref[...], 0.0)
        p_ref[...] = (zb_ref[...] * _conv3(w_ref, cv, cvp)).astype(BF)

    return pl.pallas_call(
        body, name="conv_mix_fwd", grid=(nc, s // tr),
        in_specs=[_spec_cur(tr, tc, 0), _spec_cur(tr, tc, nc), _spec_cur(tr, tc, 2 * nc),
                  _spec_prev(tr, tc, nc), _spec_prev(tr, tc, 2 * nc), _spec_w(tc, 0)],
        out_specs=_spec_cur(tr, tc, 0),
        out_shape=jax.ShapeDtypeStruct((s, conv), BF),
        compiler_params=_cp("parallel", "parallel"),
    )(z_a, z_a, z_a, z_a, z_a, cw8)


def _conv_mix_bwd(z_a, d_p, cw8, conv):
    s = z_a.shape[0]
    tr = _pick(s, ROW_TILE_BWD, 16)
    tc = _pick(conv, COL_TILE, LANE)
    nc = conv // tc
    nr = s // tr

    def body(zb_ref, zbn_ref, zc_ref, zcp_ref, zv_ref, zvp_ref, dp_ref, dpn_ref, w_ref,
             dzb_ref, dzc_ref, dzv_ref, dw_ref):
        i = pl.program_id(1)
        zc = zc_ref[...]
        zv = zv_ref[...]
        cv = zc * zv
        cvp = jnp.where(i > 0, zcp_ref[...] * zvp_ref[...], 0.0)
        dpv = dp_ref[...]
        dzb_ref[...] = (dpv * _conv3(w_ref, cv, cvp)).astype(BF)
        dcc = dpv * zb_ref[...]
        dccn = jnp.where(i < nr - 1, dpn_ref[...] * zbn_ref[...], 0.0)
        dcv = _conv3_t(w_ref, dcc, dccn)
        dzc_ref[...] = (dcv * zv).astype(BF)
        dzv_ref[...] = (dcv * zc).astype(BF)

        @pl.when(i == 0)
        def _():
            dw_ref[...] = jnp.zeros_like(dw_ref)

        dw_ref[...] += _rows8([jnp.sum(dcc * _down(cv, cvp, 2), axis=0, keepdims=True),
                               jnp.sum(dcc * _down(cv, cvp, 1), axis=0, keepdims=True),
                               jnp.sum(dcc * cv, axis=0, keepdims=True)], tc)

    out = jax.ShapeDtypeStruct((s, conv), BF)
    return pl.pallas_call(
        body, name="conv_mix_bwd", grid=(nc, nr),
        in_specs=[_spec_cur(tr, tc, 0), _spec_next(tr, tc, 0, s),
                  _spec_cur(tr, tc, nc), _spec_prev(tr, tc, nc),
                  _spec_cur(tr, tc, 2 * nc), _spec_prev(tr, tc, 2 * nc),
                  _spec_cur(tr, tc, 0), _spec_next(tr, tc, 0, s), _spec_w(tc, 0)],
        out_specs=[_spec_cur(tr, tc, 0), _spec_cur(tr, tc, 0), _spec_cur(tr, tc, 0), _spec_w(tc, 0)],
        out_shape=[out, out, out, jax.ShapeDtypeStruct((SUB, conv), F32)],
        compiler_params=_cp("parallel", "arbitrary"),
    )(z_a, z_a, z_a, z_a, z_a, z_a, d_p, d_p, cw8)


def _silu_parts(ag):
    sg = jax.nn.sigmoid(ag)
    return ag * sg, sg


def _ffn_act_fwd(a_pre, cw8, cb, dff):
    s = a_pre.shape[0]
    tr = _pick(s, ROW_TILE, 16)
    tc = _pick(dff, COL_TILE, LANE)
    nc = dff // tc

    def body(xg_ref, xgp_ref, xu_ref, xup_ref, wg_ref, wu_ref, bg_ref, bu_ref, f_ref):
        i = pl.program_id(1)
        xgp = jnp.where(i > 0, xgp_ref[...], 0.0)
        xup = jnp.where(i > 0, xup_ref[...], 0.0)
        ag = _conv3(wg_ref, xg_ref[...], xgp) + bg_ref[...]
        au = _conv3(wu_ref, xu_ref[...], xup) + bu_ref[...]
        f_ref[...] = (_silu_parts(ag)[0] * au).astype(BF)

    return pl.pallas_call(
        body, name="ffn_act_fwd", grid=(nc, s // tr),
        in_specs=[_spec_cur(tr, tc, 0), _spec_prev(tr, tc, 0), _spec_cur(tr, tc, nc), _spec_prev(tr, tc, nc),
                  _spec_w(tc, 0), _spec_w(tc, nc),
                  pl.BlockSpec((1, tc), lambda j, i: (0, j)), pl.BlockSpec((1, tc), lambda j, i: (0, nc + j))],
        out_specs=_spec_cur(tr, tc, 0),
        out_shape=jax.ShapeDtypeStruct((s, dff), BF),
        compiler_params=_cp("parallel", "parallel"),
    )(a_pre, a_pre, a_pre, a_pre, cw8, cw8, cb, cb)


def _ffn_act_bwd(a_pre, d_f, cw8, cb, dff):
    s = a_pre.shape[0]
    tr = _pick(s, ROW_TILE_BWD, 16)
    tc = _pick(dff, COL_TILE, LANE)
    nc = dff // tc
    nr = s // tr

    def body(xg_ref, xgp_ref, xgn_ref, xu_ref, xup_ref, xun_ref, df_ref, dfn_ref,
             wg_ref, wu_ref, bg_ref, bu_ref, dxg_ref, dxu_ref, dwg_ref, dwu_ref):
        i = pl.program_id(1)
        xg = xg_ref[...]
        xu = xu_ref[...]
        xgp = jnp.where(i > 0, xgp_ref[...], 0.0)
        xup = jnp.where(i > 0, xup_ref[...], 0.0)

        def d_act(xg_t, xgp_t, xu_t, xup_t, df_t):
            ag = _conv3(wg_ref, xg_t, xgp_t) + bg_ref[...]
            au = _conv3(wu_ref, xu_t, xup_t) + bu_ref[...]
            sil, sg = _silu_parts(ag)
            return df_t * au * (sg * (1.0 + ag * (1.0 - sg))), df_t * sil

        dag, dau = d_act(xg, xgp, xu, xup, df_ref[...])
        dfn = jnp.where(i < nr - 1, dfn_ref[...], 0.0)
        dagn, daun = d_act(xgn_ref[...], xg[tr - SUB:], xun_ref[...], xu[tr - SUB:], dfn)
        dxg_ref[...] = _conv3_t(wg_ref, dag, dagn).astype(BF)
        dxu_ref[...] = _conv3_t(wu_ref, dau, daun).astype(BF)

        @pl.when(i == 0)
        def _():
            dwg_ref[...] = jnp.zeros_like(dwg_ref)
            dwu_ref[...] = jnp.zeros_like(dwu_ref)

        def wgrad(da, x, xp):
            return _rows8([jnp.sum(da * _down(x, xp, 2), axis=0, keepdims=True),
                           jnp.sum(da * _down(x, xp, 1), axis=0, keepdims=True),
                           jnp.sum(da * x, axis=0, keepdims=True),
                           jnp.sum(da, axis=0, keepdims=True)], tc)

        dwg_ref[...] += wgrad(dag, xg, xgp)
        dwu_ref[...] += wgrad(dau, xu, xup)

    half = jax.ShapeDtypeStruct((s, dff), BF)
    wsh = jax.ShapeDtypeStruct((SUB, dff), F32)
    return pl.pallas_call(
        body, name="ffn_act_bwd", grid=(nc, nr),
        in_specs=[_spec_cur(tr, tc, 0), _spec_prev(tr, tc, 0), _spec_next(tr, tc, 0, s),
                  _spec_cur(tr, tc, nc), _spec_prev(tr, tc, nc), _spec_next(tr, tc, nc, s),
                  _spec_cur(tr, tc, 0), _spec_next(tr, tc, 0, s),
                  _spec_w(tc, 0), _spec_w(tc, nc),
                  pl.BlockSpec((1, tc), lambda j, i: (0, j)), pl.BlockSpec((1, tc), lambda j, i: (0, nc + j))],
        out_specs=[_spec_cur(tr, tc, 0), _spec_cur(tr, tc, 0), _spec_w(tc, 0), _spec_w(tc, 0)],
        out_shape=[half, half, wsh, wsh],
        compiler_params=_cp("parallel", "arbitrary"),
    )(a_pre, a_pre, a_pre, a_pre, a_pre, a_pre, d_f, d_f, cw8, cw8, cb, cb)


def _gate_fwd(z_g, b_gate, yc, ym, d):
    s = z_g.shape[0]
    tr = _pick(s, ROW_TILE, 16)
    tc = _pick(d, COL_TILE, LANE)
    nc = d // tc

    def body(za_ref, zb_ref, ba_ref, bb_ref, yc_ref, ym_ref, o_ref):
        ga = jax.nn.sigmoid(za_ref[...] + ba_ref[...])
        gb = jax.nn.sigmoid(zb_ref[...] + bb_ref[...])
        o_ref[...] = (ga * yc_ref[...] + gb * ym_ref[...]).astype(BF)

    return pl.pallas_call(
        body, name="gate_fwd", grid=(nc, s // tr),
        in_specs=[_spec_cur(tr, tc, 0), _spec_cur(tr, tc, nc),
                  pl.BlockSpec((1, tc), lambda j, i: (0, j)), pl.BlockSpec((1, tc), lambda j, i: (0, nc + j)),
                  _spec_cur(tr, tc, 0), _spec_cur(tr, tc, 0)],
        out_specs=_spec_cur(tr, tc, 0),
        out_shape=jax.ShapeDtypeStruct((s, d), BF),
        compiler_params=_cp("parallel", "parallel"),
    )(z_g, z_g, b_gate, b_gate, yc, ym)


def _gate_bwd(d_mix, z_g, b_gate, yc, ym, d):
    s = z_g.shape[0]
    tr = _pick(s, ROW_TILE, 16)
    tc = _pick(d, COL_TILE, LANE)
    nc = d // tc

    def body(dm_ref, za_ref, zb_ref, ba_ref, bb_ref, yc_ref, ym_ref,
             dza_ref, dzb_ref, dyc_ref, dym_ref, dba_ref, dbb_ref):
        i = pl.program_id(1)
        dm = dm_ref[...]
        ga = jax.nn.sigmoid(za_ref[...] + ba_ref[...])
        gb = jax.nn.sigmoid(zb_ref[...] + bb_ref[...])
        dza = dm * yc_ref[...] * (ga * (1.0 - ga))
        dzb = dm * ym_ref[...] * (gb * (1.0 - gb))
        dza_ref[...] = dza.astype(BF)
        dzb_ref[...] = dzb.astype(BF)
        dyc_ref[...] = (dm * ga).astype(BF)
        dym_ref[...] = (dm * gb).astype(BF)

        @pl.when(i == 0)
        def _():
            dba_ref[...] = jnp.zeros_like(dba_ref)
            dbb_ref[...] = jnp.zeros_like(dbb_ref)

        dba_ref[...] += _rows8([jnp.sum(dza, axis=0, keepdims=True)], tc)
        dbb_ref[...] += _rows8([jnp.sum(dzb, axis=0, keepdims=True)], tc)

    act = jax.ShapeDtypeStruct((s, d), BF)
    bsh = jax.ShapeDtypeStruct((SUB, d), F32)
    return pl.pallas_call(
        body, name="gate_bwd", grid=(nc, s // tr),
        in_specs=[_spec_cur(tr, tc, 0), _spec_cur(tr, tc, 0), _spec_cur(tr, tc, nc),
                  pl.BlockSpec((1, tc), lambda j, i: (0, j)), pl.BlockSpec((1, tc), lambda j, i: (0, nc + j)),
                  _spec_cur(tr, tc, 0), _spec_cur(tr, tc, 0)],
        out_specs=[_spec_cur(tr, tc, 0)] * 4 + [_spec_w(tc, 0)] * 2,
        out_shape=[act, act, act, act, bsh, bsh],
        compiler_params=_cp("parallel", "arbitrary"),
    )(d_mix, z_g, z_g, b_gate, b_gate, yc, ym)


def _lay(v):
    z = jnp.zeros(v.shape[:-1] + (HALF,), v.dtype)
    return jnp.concatenate([v[..., :HALF], z, v[..., HALF:], z], axis=-1)


def _unlay(v):
    return jnp.concatenate([v[..., :HALF], v[..., 2 * HALF:3 * HALF]], axis=-1)


def _rope_tables(positions):
    s = positions.shape[0]
    tr = _pick(s, ROW_TILE, 8)
    inv_freq = ROPE_THETA ** (-jnp.arange(0, ROPE, 2, dtype=F32) / ROPE)
    consts = jnp.stack([_lay(jnp.concatenate([inv_freq, inv_freq])),
                        _lay(jnp.ones((ROPE,), F32)),
                        _lay(jnp.concatenate([-jnp.ones((HALF,), F32), jnp.ones((HALF,), F32)]))])
    consts = _pad8(consts)

    def body(p_ref, c_ref, cos_ref, sin_ref):
        ang = p_ref[...].astype(F32) * c_ref[0:1, :]
        cos_ref[...] = jnp.cos(ang) * c_ref[1:2, :]
        sin_ref[...] = jnp.sin(ang) * c_ref[2:3, :]

    tab = jax.ShapeDtypeStruct((s, LANE), F32)
    return pl.pallas_call(
        body, name="rope_tables", grid=(s // tr,),
        in_specs=[pl.BlockSpec((tr, 1), lambda i: (i, 0)), pl.BlockSpec((SUB, LANE), lambda i: (0, 0))],
        out_specs=[pl.BlockSpec((tr, LANE), lambda i: (i, 0))] * 2,
        out_shape=[tab, tab],
        compiler_params=_cp("parallel"),
    )(positions, consts)


def _rope(t, cos, sin):
    return t * cos + pltpu.roll(t, 2 * HALF, axis=1) * sin


def _rope_t(d, cos, sin):
    return d * cos + pltpu.roll(d * sin, 2 * HALF, axis=1)


def _head_fwd(q_raw, kv_raw, z_a, kr_blk, cos, sin, gains, heads):
    s = q_raw.shape[0]
    tr = _pick(s, HEAD_ROW_TILE, 16)
    hw = heads * LANE

    def body(q_ref, kv_ref, kr_ref, cos_ref, sin_ref, g_ref, qo_ref, ko_ref, vo_ref):
        cosv = cos_ref[...]
        sinv = sin_ref[...]
        krv = kr_ref[...]
        kr_ss = jnp.sum(krv * krv, axis=-1, keepdims=True)
        for h in range(heads):
            lo = h * LANE
            qn = q_ref[:, lo:lo + LANE]
            qr = q_ref[:, hw + lo:hw + lo + LANE]
            ss = jnp.sum(qn * qn, axis=-1, keepdims=True) + jnp.sum(qr * qr, axis=-1, keepdims=True)
            r = lax.rsqrt(ss / HEAD_QK + NORM_EPS)
            qo_ref[:, 2 * lo:2 * lo + LANE] = ((qn * r) * g_ref[0:1, :]).astype(BF)
            qo_ref[:, 2 * lo + LANE:2 * lo + 2 * LANE] = _rope((qr * r) * g_ref[1:2, :], cosv, sinv).astype(BF)
            kn = kv_ref[:, 2 * lo:2 * lo + LANE]
            ss = jnp.sum(kn * kn, axis=-1, keepdims=True) + kr_ss
            r = lax.rsqrt(ss / HEAD_QK + NORM_EPS)
            ko_ref[:, 2 * lo:2 * lo + LANE] = ((kn * r) * g_ref[2:3, :]).astype(BF)
            ko_ref[:, 2 * lo + LANE:2 * lo + 2 * LANE] = _rope((krv * r) * g_ref[3:4, :], cosv, sinv).astype(BF)
            vo_ref[:, lo:lo + LANE] = kv_ref[:, 2 * lo + LANE:2 * lo + 2 * LANE].astype(BF)

    row = lambda w: pl.BlockSpec((tr, w), lambda i: (i, 0))
    return pl.pallas_call(
        body, name="head_fwd", grid=(s // tr,),
        in_specs=[row(2 * hw), row(2 * hw), pl.BlockSpec((tr, LANE), lambda i: (i, kr_blk)),
                  row(LANE), row(LANE), pl.BlockSpec((SUB, LANE), lambda i: (0, 0))],
        out_specs=[row(2 * hw), row(2 * hw), row(hw)],
        out_shape=[jax.ShapeDtypeStruct((s, 2 * hw), BF), jax.ShapeDtypeStruct((s, 2 * hw), BF),
                   jax.ShapeDtypeStruct((s, hw), BF)],
        compiler_params=_cp("parallel"),
    )(q_raw, kv_raw, z_a, cos, sin, gains)


def _head_bwd(q_raw, kv_raw, z_a, kr_blk, cos, sin, gains, dq_att, dk_att, dv, heads):
    s = q_raw.shape[0]
    tr = _pick(s, HEAD_ROW_TILE_BWD, 16)
    hw = heads * LANE

    def body(q_ref, kv_ref, kr_ref, cos_ref, sin_ref, g_ref, dq_ref, dk_ref, dv_ref,
             dqr_ref, dkv_ref, dkr_ref, dg_ref):
        i = pl.program_id(0)
        cosv = cos_ref[...]
        sinv = sin_ref[...]
        krv = kr_ref[...]
        kr_ss = jnp.sum(krv * krv, axis=-1, keepdims=True)
        dkr = jnp.zeros((tr, LANE), F32)
        dgs = [jnp.zeros((1, LANE), F32) for _ in range(4)]

        def norm_bwd(xn, xr, ss, dn_out, dr_out, gn, gr):
            r = lax.rsqrt(ss / HEAD_QK + NORM_EPS)
            nn = xn * r
            nr = xr * r
            dt = _rope_t(dr_out, cosv, sinv)
            dnn = dn_out * gn
            dnr = dt * gr
            mean = (jnp.sum(dnn * nn, axis=-1, keepdims=True) + jnp.sum(dnr * nr, axis=-1, keepdims=True)) / HEAD_QK
            return (r * (dnn - nn * mean), r * (dnr - nr * mean),
                    jnp.sum(dn_out * nn, axis=0, keepdims=True), jnp.sum(dt * nr, axis=0, keepdims=True))

        for h in range(heads):
            lo = h * LANE
            qn = q_ref[:, lo:lo + LANE]
            qr = q_ref[:, hw + lo:hw + lo + LANE]
            ss = jnp.sum(qn * qn, axis=-1, keepdims=True) + jnp.sum(qr * qr, axis=-1, keepdims=True)
            dxn, dxr, g0, g1 = norm_bwd(qn, qr, ss, dq_ref[:, 2 * lo:2 * lo + LANE],
                                        dq_ref[:, 2 * lo + LANE:2 * lo + 2 * LANE], g_ref[0:1, :], g_ref[1:2, :])
            dqr_ref[:, lo:lo + LANE] = dxn.astype(BF)
            dqr_ref[:, hw + lo:hw + lo + LANE] = dxr.astype(BF)
            kn = kv_ref[:, 2 * lo:2 * lo + LANE]
            ss = jnp.sum(kn * kn, axis=-1, keepdims=True) + kr_ss
            dxn, dxr, g2, g3 = norm_bwd(kn, krv, ss, dk_ref[:, 2 * lo:2 * lo + LANE],
                                        dk_ref[:, 2 * lo + LANE:2 * lo + 2 * LANE], g_ref[2:3, :], g_ref[3:4, :])
            dkv_ref[:, 2 * lo:2 * lo + LANE] = dxn.astype(BF)
            dkv_ref[:, 2 * lo + LANE:2 * lo + 2 * LANE] = dv_ref[:, lo:lo + LANE].astype(BF)
            dkr = dkr + dxr
            dgs = [a + b for a, b in zip(dgs, (g0, g1, g2, g3))]
        dkr_ref[...] = dkr

        @pl.when(i == 0)
        def _():
            dg_ref[...] = jnp.zeros_like(dg_ref)

        dg_ref[...] += _rows8(dgs, LANE)

    row = lambda w: pl.BlockSpec((tr, w), lambda i: (i, 0))
    return pl.pallas_call(
        body, name="head_bwd", grid=(s // tr,),
        in_specs=[row(2 * hw), row(2 * hw), pl.BlockSpec((tr, LANE), lambda i: (i, kr_blk)),
                  row(LANE), row(LANE), pl.BlockSpec((SUB, LANE), lambda i: (0, 0)),
                  row(2 * hw), row(2 * hw), row(hw)],
        out_specs=[row(2 * hw), row(2 * hw), row(LANE), pl.BlockSpec((SUB, LANE), lambda i: (0, 0))],
        out_shape=[jax.ShapeDtypeStruct((s, 2 * hw), BF), jax.ShapeDtypeStruct((s, 2 * hw), BF),
                   jax.ShapeDtypeStruct((s, LANE), F32), jax.ShapeDtypeStruct((SUB, LANE), F32)],
        compiler_params=_cp("arbitrary"),
    )(q_raw, kv_raw, z_a, cos, sin, gains, dq_att, dk_att, dv)


def _causal_mask(nrows, ncols, row0):
    rows = lax.broadcasted_iota(jnp.int32, (nrows, ncols), 0) + row0
    cols = lax.broadcasted_iota(jnp.int32, (nrows, ncols), 1)
    return cols <= rows


def _causal_steps(nt, q_major):
    pairs = ([(i, j) for i in range(nt) for j in range(i + 1)] if q_major
             else [(i, j) for j in range(nt) for i in range(j, nt)])
    return (jnp.array([p[0] for p in pairs], jnp.int32), jnp.array([p[1] for p in pairs], jnp.int32))


def _attn_fwd(q_att, k_att, v, heads):
    s = q_att.shape[0]
    t = _pick(s, ATTN_TILE, LANE)
    nt = s // t
    th = t // 2
    scale = HEAD_QK ** -0.5
    qi, kj = _causal_steps(nt, True)

    def body(qi_ref, kj_ref, q_ref, k_ref, v_ref, o_ref, ob_ref, lse_ref, m_s, l_s, acc_s):
        st = pl.program_id(1)
        i = qi_ref[st]
        j = kj_ref[st]

        @pl.when(j == 0)
        def _():
            m_s[...] = jnp.full_like(m_s, NEG_INF)
            l_s[...] = jnp.zeros_like(l_s)
            acc_s[...] = jnp.zeros_like(acc_s)

        def step(masked):
            for r0 in range(0, t, th):
                rows = slice(r0, r0 + th)
                sc = lax.dot_general(q_ref[rows, :], k_ref[...], (((1,), (1,)), ((), ())),
                                     preferred_element_type=F32) * scale
                if masked:
                    sc = jnp.where(_causal_mask(th, t, r0), sc, NEG_INF)
                m_prev = m_s[rows, :]
                m_new = jnp.maximum(m_prev, jnp.max(sc, axis=-1, keepdims=True))
                alpha = jnp.exp(m_prev - m_new)
                p = jnp.exp(sc - m_new)
                l_s[rows, :] = alpha * l_s[rows, :] + jnp.sum(p, axis=-1, keepdims=True)
                acc_s[rows, :] = alpha * acc_s[rows, :] + jnp.dot(p.astype(BF), v_ref[...],
                                                                  preferred_element_type=F32)
                m_s[rows, :] = m_new

        @pl.when(j < i)
        def _():
            step(False)

        @pl.when(j == i)
        def _():
            step(True)
            o = acc_s[...] / l_s[...]
            o_ref[...] = o
            ob_ref[...] = o.astype(BF)
            lse_ref[...] = m_s[...] + jnp.log(l_s[...])

    q_idx = lambda h, st, qi_r, kj_r: (qi_r[st], h)
    kv_idx = lambda h, st, qi_r, kj_r: (kj_r[st], h)
    return pl.pallas_call(
        body, name="attn_fwd",
        grid_spec=pltpu.PrefetchScalarGridSpec(
            num_scalar_prefetch=2, grid=(heads, qi.shape[0]),
            in_specs=[pl.BlockSpec((t, 2 * LANE), q_idx), pl.BlockSpec((t, 2 * LANE), kv_idx),
                      pl.BlockSpec((t, LANE), kv_idx)],
            out_specs=[pl.BlockSpec((t, LANE), q_idx), pl.BlockSpec((t, LANE), q_idx),
                       pl.BlockSpec((None, t, 1), lambda h, st, qi_r, kj_r: (h, qi_r[st], 0))],
            scratch_shapes=[pltpu.VMEM((t, 1), F32), pltpu.VMEM((t, 1), F32), pltpu.VMEM((t, LANE), F32)]),
        out_shape=[jax.ShapeDtypeStruct((s, heads * LANE), F32), jax.ShapeDtypeStruct((s, heads * LANE), BF),
                   jax.ShapeDtypeStruct((heads, s, 1), F32)],
        compiler_params=_cp("parallel", "arbitrary"),
    )(qi, kj, q_att, k_att, v)


def _attn_bwd(q_att, k_att, v, o, lse, d_o, heads, dep=None):
    s = q_att.shape[0]
    t = _pick(s, ATTN_TILE, LANE)
    nt = s // t
    scale = HEAD_QK ** -0.5
    qi, kj = _causal_steps(nt, False)

    def body(qi_ref, kj_ref, q_ref, k_ref, v_ref, do_ref, o_ref, lse_ref, *rest):
        dq_ref, dk_ref, dv_ref, dk_s, dv_s = rest[-5:]
        st = pl.program_id(1)
        i = qi_ref[st]
        j = kj_ref[st]

        @pl.when(st == 0)
        def _():
            dq_ref[...] = jnp.zeros_like(dq_ref)

        @pl.when(i == j)
        def _():
            dk_s[...] = jnp.zeros_like(dk_s)
            dv_s[...] = jnp.zeros_like(dv_s)

        def step(masked):
            q = q_ref[...]
            k = k_ref[...]
            do = do_ref[...]
            sc = lax.dot_general(q, k, (((1,), (1,)), ((), ())), preferred_element_type=F32) * scale
            if masked:
                sc = jnp.where(_causal_mask(t, t, 0), sc, NEG_INF)
            p = jnp.exp(sc - lse_ref[...])
            dp = lax.dot_general(do, v_ref[...], (((1,), (1,)), ((), ())), preferred_element_type=F32)
            delta = jnp.sum(do.astype(F32) * o_ref[...], axis=-1, keepdims=True)
            ds = (p * (dp - delta) * scale).astype(BF)
            dv_s[...] += lax.dot_general(p.astype(BF), do, (((0,), (0,)), ((), ())), preferred_element_type=F32)
            dk_s[...] += lax.dot_general(ds, q, (((0,), (0,)), ((), ())), preferred_element_type=F32)
            rows = pl.ds(pl.multiple_of(i * t, t), t)
            dq_ref[rows, :] += jnp.dot(ds, k, preferred_element_type=F32)

        @pl.when(i > j)
        def _():
            step(False)

        @pl.when(i == j)
        def _():
            step(True)

        @pl.when(i == nt - 1)
        def _():
            dk_ref[...] = dk_s[...]
            dv_ref[...] = dv_s[...]

    q_idx = lambda h, st, qi_r, kj_r: (qi_r[st], h)
    kv_idx = lambda h, st, qi_r, kj_r: (kj_r[st], h)
    in_specs = [pl.BlockSpec((t, 2 * LANE), q_idx), pl.BlockSpec((t, 2 * LANE), kv_idx),
                pl.BlockSpec((t, LANE), kv_idx), pl.BlockSpec((t, LANE), q_idx), pl.BlockSpec((t, LANE), q_idx),
                pl.BlockSpec((None, t, 1), lambda h, st, qi_r, kj_r: (h, qi_r[st], 0))]
    args = [q_att, k_att, v, d_o, o, lse]
    if dep is not None:
        in_specs.append(ANY)
        args.append(dep)
    return pl.pallas_call(
        body, name="attn_bwd",
        grid_spec=pltpu.PrefetchScalarGridSpec(
            num_scalar_prefetch=2, grid=(heads, qi.shape[0]),
            in_specs=in_specs,
            out_specs=[pl.BlockSpec((s, 2 * LANE), lambda h, st, qi_r, kj_r: (0, h)),
                       pl.BlockSpec((t, 2 * LANE), kv_idx), pl.BlockSpec((t, LANE), kv_idx)],
            scratch_shapes=[pltpu.VMEM((t, 2 * LANE), F32), pltpu.VMEM((t, LANE), F32)]),
        out_shape=[jax.ShapeDtypeStruct((s, heads * 2 * LANE), F32),
                   jax.ShapeDtypeStruct((s, heads * 2 * LANE), F32),
                   jax.ShapeDtypeStruct((s, heads * LANE), F32)],
        compiler_params=_cp("parallel", "arbitrary"),
    )(qi, kj, *args)


def _loss_head(y, target):
    s, d = y.shape
    tr = _pick(s, ROW_TILE, 8)

    def body(y_ref, t_ref, dy_ref, dyb_ref, l_ref):
        i = pl.program_id(0)
        e = y_ref[...] - t_ref[...]
        dy_ref[...] = e / d
        dyb_ref[...] = (e / d).astype(BF)

        @pl.when(i == 0)
        def _():
            l_ref[...] = jnp.zeros_like(l_ref)

        l_ref[...] += 0.5 * jnp.sum(jnp.mean(e * e, axis=-1, keepdims=True), axis=0, keepdims=True)

    return pl.pallas_call(
        body, name="loss_head", grid=(s // tr,),
        in_specs=[pl.BlockSpec((tr, d), lambda i: (i, 0))] * 2,
        out_specs=[pl.BlockSpec((tr, d), lambda i: (i, 0)), pl.BlockSpec((tr, d), lambda i: (i, 0)),
                   pl.BlockSpec((SUB, LANE), lambda i: (0, 0))],
        out_shape=[jax.ShapeDtypeStruct((s, d), F32), jax.ShapeDtypeStruct((s, d), BF),
                   jax.ShapeDtypeStruct((SUB, LANE), F32)],
        compiler_params=_cp("arbitrary"),
    )(y, target)


def _sum_parts(parts, name):
    n, r, c = parts.shape
    tr = _pick(r, 512, 8)

    def body(p_ref, o_ref):
        g = p_ref[0].astype(F32)
        for k in range(1, n):
            g = g + p_ref[k].astype(F32)
        o_ref[...] = g

    return pl.pallas_call(
        body, name=name, grid=(r // tr,),
        in_specs=[pl.BlockSpec((n, tr, c), lambda i: (0, i, 0))],
        out_specs=pl.BlockSpec((tr, c), lambda i: (i, 0)),
        out_shape=jax.ShapeDtypeStruct((r, c), F32),
        compiler_params=_cp("parallel"),
    )(parts)


def _adamw(parts, w, m, v, name):
    n, r, c = parts.shape
    tr = _pick(r, 256, 8)

    def body(p_ref, w_ref, m_ref, v_ref, g_ref, d_ref, mo_ref, vo_ref):
        g = p_ref[0].astype(F32)
        for k in range(1, n):
            g = g + p_ref[k].astype(F32)
        m_new = ADAM_B1 * m_ref[...] + (1.0 - ADAM_B1) * g
        v_new = ADAM_B2 * v_ref[...] + (1.0 - ADAM_B2) * jnp.square(g)
        m_hat = m_new / (1.0 - ADAM_B1 ** ADAM_STEP)
        v_hat = v_new / (1.0 - ADAM_B2 ** ADAM_STEP)
        g_ref[...] = g
        d_ref[...] = -ADAM_LR * (m_hat / (jnp.sqrt(v_hat) + ADAM_EPS) + ADAM_WD * w_ref[...])
        mo_ref[...] = m_new
        vo_ref[...] = v_new

    spec = pl.BlockSpec((tr, c), lambda i: (i, 0))
    sh = jax.ShapeDtypeStruct((r, c), F32)
    return pl.pallas_call(
        body, name=name, grid=(r // tr,),
        in_specs=[pl.BlockSpec((n, tr, c), lambda i: (0, i, 0)), spec, spec, spec],
        out_specs=[spec] * 4, out_shape=[sh] * 4,
        compiler_params=_cp("parallel"),
    )(parts, w, m, v)


def _place():
    x, y, c = lax.axis_index("x"), lax.axis_index("y"), lax.axis_index("c")
    chips = [(1 - x, y), (x, 1 - y), (1 - x, 1 - y)]
    return x, y, c, chips


def _all_gather(shards, name):
    n = len(shards)

    def body(*refs):
        ins, outs = refs[:n], refs[n:2 * n]
        send_sems, recv_sems, local_sems = refs[2 * n:]
        x, y, c, chips = _place()
        me, sibling = (x, y, c), (x, y, 1 - c)

        def slot(w, p):
            return outs[w].at[4 * p[0] + 2 * p[1] + p[2]]

        def copy(w, k, block, to, src=None):
            return pltpu.make_async_remote_copy(
                src_ref=slot(w, block) if src is None else src, dst_ref=slot(w, block),
                send_sem=send_sems.at[w, k], recv_sem=recv_sems.at[w, k], device_id=to, device_id_type=MESH)

        first = []
        for w in range(n):
            first += [copy(w, 1 + j, me, (*chip, c), src=ins[w]) for j, chip in enumerate(chips)]
            first.append(copy(w, 0, me, sibling, src=ins[w]))
        for cp in first:
            cp.start()
        mine = [pltpu.make_async_copy(ins[w], slot(w, me), local_sems.at[w]) for w in range(n)]
        for cp in mine:
            cp.start()
        passed = []
        for w in range(n):
            for j, chip in enumerate(chips):
                copy(w, 1 + j, (*chip, c), me).wait_recv()
                cp = copy(w, 4 + j, (*chip, c), sibling)
                cp.start()
                passed.append(cp)
        for w in range(n):
            copy(w, 0, sibling, me).wait_recv()
            for j, chip in enumerate(chips):
                copy(w, 4 + j, (*chip, 1 - c), me).wait_recv()
        for cp in first + passed:
            cp.wait_send()
        for cp in mine:
            cp.wait()

    return pl.pallas_call(
        body, name=name,
        in_specs=[ANY] * n, out_specs=[ANY] * n,
        out_shape=[jax.ShapeDtypeStruct((N_DEV,) + a.shape, a.dtype) for a in shards],
        scratch_shapes=[pltpu.SemaphoreType.DMA((n, 7)), pltpu.SemaphoreType.DMA((n, 7)),
                        pltpu.SemaphoreType.DMA((n,))],
    )(*shards)


HBM = pl.BlockSpec(memory_space=pltpu.HBM)
SEM = pl.BlockSpec(memory_space=pltpu.SEMAPHORE)
EFFECT = pltpu.SideEffectType.DATAFLOW_SIDE_EFFECTING
PEERS = [(dx, dy, dc) for dx in (1, 0) for dy in (1, 0) for dc in (0, 1) if (dx, dy, dc) != (0, 0, 0)]


def _peer(x, y, c, flip):
    dx, dy, dc = flip
    return (1 - x if dx else x, 1 - y if dy else y, 1 - c if dc else c)


def _exchange_copies(srcs, lands, send, recv, loc, gather):
    x, y, c, _ = _place()
    me = 4 * x + 2 * y + c
    remote, local = [], []
    for w in range(len(srcs)):
        for k, flip in enumerate(PEERS):
            px, py, pc = _peer(x, y, c, flip)
            src = srcs[w] if gather else srcs[w].at[4 * px + 2 * py + pc]
            remote.append(pltpu.make_async_remote_copy(
                src_ref=src, dst_ref=lands[w].at[me], send_sem=send[w].at[k], recv_sem=recv[w].at[k],
                device_id=(px, py, pc), device_id_type=MESH))
        local.append(pltpu.make_async_copy(srcs[w] if gather else srcs[w].at[me], lands[w].at[me], loc[w]))
    return remote, local


class _Exchange:
    def __init__(self, srcs, lands, send, recv, loc, token, gather):
        self.srcs, self.lands, self.send, self.recv, self.loc = srcs, lands, send, recv, loc
        self.token, self.gather = token, gather


def _exchange_start(srcs, gather, name):
    n = len(srcs)
    land_shapes = [((N_DEV,) + a.shape) if gather else a.shape for a in srcs]
    lands = [pltpu.with_memory_space_constraint(lax.empty(sh, a.dtype), pltpu.HBM) for sh, a in zip(land_shapes, srcs)]
    srcs = [pltpu.with_memory_space_constraint(a, pltpu.HBM) for a in srcs]

    def body(*refs):
        src_refs, land_refs = refs[:n], refs[n:2 * n]
        outs = refs[2 * n:]
        send, recv, loc = outs[:n], outs[n:2 * n], outs[2 * n:3 * n]
        token = outs[-1]
        remote, local = _exchange_copies(src_refs, land_refs, send, recv, loc, gather)
        for cp in remote + local:
            cp.start()
        token[...] = jnp.zeros_like(token)

    out_shape = ([pltpu.SemaphoreType.DMA((len(PEERS),))] * (2 * n) + [pltpu.SemaphoreType.DMA(())] * n
                 + [pltpu.HBM(a.shape, a.dtype) for a in srcs] + [pltpu.HBM(a.shape, a.dtype) for a in lands]
                 + [jax.ShapeDtypeStruct((SUB, LANE), F32)])
    res = pl.pallas_call(
        body, name=name, out_shape=out_shape,
        in_specs=[HBM] * (2 * n),
        out_specs=[SEM] * (3 * n) + [HBM] * (2 * n) + [pl.BlockSpec(memory_space=pltpu.VMEM)],
        input_output_aliases={i: 3 * n + i for i in range(2 * n)},
        compiler_params=pltpu.CompilerParams(has_side_effects=EFFECT),
    )(*srcs, *lands)
    return _Exchange(res[3 * n:4 * n], res[4 * n:5 * n], res[:n], res[n:2 * n], res[2 * n:3 * n], res[-1], gather)


def _exchange_wait(ex, idxs, after, name):
    n = len(idxs)
    srcs = [ex.srcs[i] for i in idxs]
    lands = [ex.lands[i] for i in idxs]
    sems = [ex.send[i] for i in idxs] + [ex.recv[i] for i in idxs] + [ex.loc[i] for i in idxs]
    gather = ex.gather

    def body(*refs):
        src_refs, land_refs = refs[:n], refs[n:2 * n]
        send, recv, loc = refs[2 * n:3 * n], refs[3 * n:4 * n], refs[4 * n:5 * n]
        remote, local = _exchange_copies(src_refs, land_refs, send, recv, loc, gather)
        for cp in remote:
            cp.wait_send()
            cp.wait_recv()
        for cp in local:
            cp.wait()

    res = pl.pallas_call(
        body, name=name,
        out_shape=[pltpu.HBM(a.shape, a.dtype) for a in srcs] + [pltpu.HBM(a.shape, a.dtype) for a in lands],
        in_specs=[HBM] * (2 * n) + [SEM] * (3 * n) + [ANY],
        out_specs=[HBM] * (2 * n),
        input_output_aliases={i: i for i in range(2 * n)},
        compiler_params=pltpu.CompilerParams(has_side_effects=EFFECT),
    )(*srcs, *lands, *sems, after)
    return res[n:]


def _after(token, a):
    return a + token[0:1, 0:1].astype(a.dtype)


def _unblock(w3):
    nb, k, nbw = w3.shape
    return w3.transpose(1, 0, 2).reshape(k, nb * nbw)


def _block(w, nb):
    k, n = w.shape
    return w.reshape(k, nb, n // nb).transpose(1, 0, 2)


def kernel(x, positions, ln1_g, w_in, b_gate, conv_w, w_conv_out, q_a_g, w_q_b, kv_a_g, w_kv_b, q_norm_g, k_norm_g, w_mla_out, w_o, ln2_g, w_ffn_up, ffn_conv_w, ffn_conv_b, w_ffn_down, loss_target, m_ln1_g, m_w_in, m_b_gate, m_conv_w, m_w_conv_out, m_q_a_g, m_w_q_b, m_kv_a_g, m_w_kv_b, m_q_norm_g, m_k_norm_g, m_w_mla_out, m_w_o, m_ln2_g, m_w_ffn_up, m_ffn_conv_w, m_ffn_conv_b, m_w_ffn_down, v_ln1_g, v_w_in, v_b_gate, v_conv_w, v_w_conv_out, v_q_a_g, v_w_q_b, v_kv_a_g, v_w_kv_b, v_q_norm_g, v_k_norm_g, v_w_mla_out, v_w_o, v_ln2_g, v_w_ffn_up, v_ffn_conv_w, v_ffn_conv_b, v_w_ffn_down):
    s, d = x.shape[1], x.shape[2]
    conv = conv_w.shape[2] * N_DEV
    ql, kvl = q_a_g.shape[1], kv_a_g.shape[1]
    heads = w_q_b.shape[2] * N_DEV // HEAD_QK
    dff = w_ffn_down.shape[1] * N_DEV
    hw = heads * LANE
    conv3 = 3 * conv
    kr_off = conv3 + ql
    kv_off = -(-(kr_off + LANE) // kvl) * kvl
    wa = kv_off + kvl
    assert conv3 % ql == 0 and kr_off % LANE == 0
    xs = x[0]
    tgt = loss_target[0]
    pos = positions.reshape(s, 1)

    big = dict(w_in=w_in[0], w_conv_out=w_conv_out[0], w_q_b=w_q_b[0], w_kv_b=w_kv_b[0],
               w_mla_out=w_mla_out[0], w_o=w_o[0], w_ffn_up=w_ffn_up[0], w_ffn_down=w_ffn_down[0])
    names = list(big)
    rest = names[1:]
    first = _all_gather([big["w_in"].astype(BF), _pad8(conv_w[0]), _pad8(ffn_conv_w[0])], "gather_w_in")
    cw8 = _unblock(first[1])
    fcw8 = _unblock(first[2])
    ag = _exchange_start([big[k].astype(BF) for k in rest], True, "gather_rest_start")

    def landed(keys, after, name):
        return _exchange_wait(ag, [rest.index(k) for k in keys], after, name)

    w_in_full = _unblock(first[0])
    zpad = jnp.zeros((d, kv_off - kr_off - LANE), BF)
    w_a = jnp.concatenate([w_in_full[:, :kr_off], _lay(w_in_full[:, kr_off + kvl:kr_off + kvl + ROPE]), zpad,
                           w_in_full[:, kr_off:kr_off + kvl]], axis=1)[None]
    g_off = kr_off + kvl + ROPE
    w_g = _block(w_in_full[:, g_off:], 2)
    gains = _pad8(jnp.concatenate([q_norm_g[:, :NOPE], _lay(q_norm_g[:, NOPE:]),
                                   k_norm_g[:, :NOPE], _lay(k_norm_g[:, NOPE:])], axis=0))
    kr_blk = kr_off // LANE

    cos, sin = _rope_tables(pos)
    u1 = _rms_fwd(xs, _after(ag.token, ln1_g), d, 0, "rms1_fwd")
    z_a = _mm_nn(u1, w_a, "mm_z_a")
    z_g = _mm_nn(u1, w_g, "mm_z_g")
    p = _conv_mix_fwd(z_a, cw8, conv)
    w_co, w_qb, w_kv = landed(["w_conv_out", "w_q_b", "w_kv_b"], p, "gather_wait_mixers")
    wq_full = _unblock(w_qb).reshape(ql, heads, HEAD_QK)
    w_q = jnp.concatenate([wq_full[:, :, :NOPE].reshape(ql, hw), _lay(wq_full[:, :, NOPE:]).reshape(ql, hw)],
                          axis=1)[None]
    yc = _mm_nn(p, w_co, "mm_y_conv")
    qn = _rms_fwd(z_a, q_a_g, ql, conv3 // ql, "rms_q_fwd")
    kvn = _rms_fwd(z_a, kv_a_g, kvl, kv_off // kvl, "rms_kv_fwd")
    q_raw = _mm_nn(qn, w_q, "mm_q")
    kv_raw = _mm_nn(kvn, w_kv, "mm_kv")
    q_att, k_att, v_bf = _head_fwd(q_raw, kv_raw, z_a, kr_blk, cos, sin, gains, heads)
    o, o_bf, lse = _attn_fwd(q_att, k_att, v_bf, heads)
    w_mo, w_oo, w_up, w_dn = landed(["w_mla_out", "w_o", "w_ffn_up", "w_ffn_down"], lse, "gather_wait_outs")
    w_mo = w_mo.reshape(1, hw, d)
    w_oo = w_oo.reshape(1, d, d)
    w_dn = w_dn.reshape(1, dff, d)
    ym = _mm_nn(o_bf, w_mo, "mm_y_mla")
    mix = _gate_fwd(z_g, b_gate, yc, ym, d)
    h1 = _mm_nn(mix, w_oo, "mm_h1", add=xs)
    u2 = _rms_fwd(h1, ln2_g, d, 0, "rms2_fwd")
    a_pre = _mm_nn(u2, w_up, "mm_ffn_up")
    f = _ffn_act_fwd(a_pre, fcw8, ffn_conv_b, dff)
    y = _mm_nn(f, w_dn, "mm_ffn_down", add=h1)
    dy, dy_bf, loss_part = _loss_head(y, tgt)

    g_dn = _mm_tn(f, dy_bf, 1, "mm_g_ffn_down").reshape(N_DEV, dff // N_DEV, d)
    rs_dn = _exchange_start([g_dn], False, "reduce_ffn_down_start")
    d_f = _mm_nt(dy_bf, w_dn, "mm_d_f", dep=rs_dn.token)
    d_xg, d_xu, dfw_g, dfw_u = _ffn_act_bwd(a_pre, d_f, fcw8, ffn_conv_b, dff)
    half = N_DEV // 2
    g_up = jnp.concatenate([_mm_tn(u2, d_xg, half, "mm_g_ffn_up_gate"), _mm_tn(u2, d_xu, half, "mm_g_ffn_up_up")], axis=0)
    rs_up = _exchange_start([g_up], False, "reduce_ffn_up_start")
    d_u2 = _mm_nt(d_xg, w_up, "mm_d_u2_gate", blk0=0, nblk=half, dep=rs_up.token)
    d_u2 = _mm_nt(d_xu, w_up, "mm_d_u2_up", blk0=half, nblk=half, add=d_u2)
    d_h1, d_h1_bf, dg_ln2 = _rms_bwd(h1, d_u2, ln2_g, d, 0, "rms2_bwd", extra=dy, also_bf16=True)
    g_oo = _mm_tn(mix, d_h1_bf, 1, "mm_g_w_o").reshape(N_DEV, d // N_DEV, d)
    d_mix = _mm_nt(d_h1_bf, w_oo, "mm_d_mix")
    d_zga, d_zgb, d_yc, d_ym, dba, dbb = _gate_bwd(d_mix, z_g, b_gate, yc, ym, d)
    g_co = _mm_tn(p, d_yc, N_DEV, "mm_g_conv_out")
    g_mo = _mm_tn(o_bf, d_ym, 1, "mm_g_mla_out").reshape(N_DEV, hw // N_DEV, d)
    rs_mix = _exchange_start([g_oo, g_co, g_mo], False, "reduce_mixers_start")
    d_p = _mm_nt(d_yc, w_co, "mm_d_p", dep=rs_mix.token)
    d_o = _mm_nt(d_ym, w_mo, "mm_d_o", out_dtype=BF)
    d_zb, d_zc, d_zv, dcw = _conv_mix_bwd(z_a, d_p, cw8, conv)
    dq_att, dk_att, dv = _attn_bwd(q_att, k_att, v_bf, o, lse, d_o, heads, dep=rs_mix.token)
    d_q_raw, d_kv_raw, d_kr, dgains = _head_bwd(q_raw, kv_raw, z_a, kr_blk, cos, sin, gains, dq_att, dk_att, dv, heads)
    g_q2 = _mm_tn(qn, d_q_raw, 1, "mm_g_q")[0]
    g_qb = _block(jnp.concatenate([g_q2[:, :hw].reshape(ql, heads, NOPE),
                                   _unlay(g_q2[:, hw:].reshape(ql, heads, LANE))], axis=2).reshape(ql, heads * HEAD_QK), N_DEV)
    g_kv = _mm_tn(kvn, d_kv_raw, N_DEV, "mm_g_kv")
    rs_qkv = _exchange_start([g_qb, g_kv], False, "reduce_qkv_start")
    d_qn = _mm_nt(d_q_raw, w_q, "mm_d_qn", dep=rs_qkv.token)
    d_kvn = _mm_nt(d_kv_raw, w_kv, "mm_d_kvn")
    d_ql, dg_qa = _rms_bwd(z_a, d_qn, q_a_g, ql, conv3 // ql, "rms_q_bwd", out_dtype=BF)
    d_kvl, dg_kva = _rms_bwd(z_a, d_kvn, kv_a_g, kvl, kv_off // kvl, "rms_kv_bwd", out_dtype=BF)
    d_z_a = jnp.concatenate([d_zb, d_zc, d_zv, d_ql, d_kr.astype(BF), jnp.zeros((s, kv_off - kr_off - LANE), BF),
                             d_kvl], axis=1)
    g_a = _mm_tn(u1, d_z_a, 1, "mm_g_w_a")[0]
    g_ga = _mm_tn(u1, d_zga, 1, "mm_g_w_ga")[0]
    g_gb = _mm_tn(u1, d_zgb, 1, "mm_g_w_gb")[0]
    g_in = _block(jnp.concatenate([g_a[:, :kr_off], g_a[:, kv_off:kv_off + kvl],
                                   _unlay(g_a[:, kr_off:kr_off + LANE]), g_ga, g_gb], axis=1), N_DEV)
    rs_in = _exchange_start([g_in], False, "reduce_w_in_start")
    d_u1 = _mm_nt(d_z_a, w_a, "mm_d_u1_a", dep=rs_in.token)
    d_u1 = _mm_nt(d_zga, w_g, "mm_d_u1_ga", blk0=0, nblk=1, add=d_u1)
    d_u1 = _mm_nt(d_zgb, w_g, "mm_d_u1_gb", blk0=1, nblk=1, add=d_u1)
    grad_x, dg_ln1 = _rms_bwd(xs, d_u1, ln1_g, d, 0, "rms1_bwd", extra=d_h1)

    small = dict(ln1_g=dg_ln1[0:1], b_gate=jnp.concatenate([dba[0:1], dbb[0:1]], axis=1), q_a_g=dg_qa[0:1],
                 kv_a_g=dg_kva[0:1],
                 q_norm_g=jnp.concatenate([dgains[0:1], _unlay(dgains[1:2])], axis=1),
                 k_norm_g=jnp.concatenate([dgains[2:3], _unlay(dgains[3:4])], axis=1),
                 ln2_g=dg_ln2[0:1], ffn_conv_b=jnp.concatenate([dfw_g[3:4], dfw_u[3:4]], axis=1))
    small_names = list(small)
    extra = [dcw[0:3].reshape(1, -1), jnp.concatenate([dfw_g[0:3], dfw_u[0:3]], axis=1).reshape(1, -1),
             loss_part[0:1, 0:1]]
    flat = jnp.concatenate([small[k] for k in small_names] + extra, axis=1)
    n_flat = flat.shape[1]
    rows = -(-n_flat // (SUB * LANE)) * SUB
    flat = jnp.pad(flat, ((0, 0), (0, rows * LANE - n_flat))).reshape(rows, LANE)
    total = _sum_parts(_all_gather([flat], "gather_small")[0], "sum_small").reshape(1, rows * LANE)
    off = 0
    small_g = {}
    for k in small_names:
        small_g[k] = total[:, off:off + small[k].shape[1]]
        off += small[k].shape[1]
    me = 4 * lax.axis_index("x") + 2 * lax.axis_index("y") + lax.axis_index("c")
    cwn, fcwn = conv // N_DEV, 2 * dff // N_DEV
    g_cw = lax.dynamic_slice_in_dim(total[:, off:off + 3 * conv].reshape(3, conv), me * cwn, cwn, axis=1)
    off += 3 * conv
    g_fcw = lax.dynamic_slice_in_dim(total[:, off:off + 6 * dff].reshape(3, 2 * dff), me * fcwn, fcwn, axis=1)
    off += 6 * dff
    loss = total[0, off]

    summed = {}
    summed["w_ffn_down"], = _exchange_wait(rs_dn, [0], total, "reduce_ffn_down_wait")
    summed["w_ffn_up"], = _exchange_wait(rs_up, [0], total, "reduce_ffn_up_wait")
    summed["w_o"], summed["w_conv_out"], summed["w_mla_out"] = _exchange_wait(rs_mix, [0, 1, 2], total, "reduce_mixers_wait")
    summed["w_q_b"], summed["w_kv_b"] = _exchange_wait(rs_qkv, [0, 1], total, "reduce_qkv_wait")

    loc = locals()
    out = {}
    for k in rest:
        out[k] = _adamw(summed[k], big[k], loc["m_" + k][0], loc["v_" + k][0], "adamw_" + k)
    summed["w_in"], = _exchange_wait(rs_in, [0], out[rest[-1]][0], "reduce_w_in_wait")
    out["w_in"] = _adamw(summed["w_in"], big["w_in"], m_w_in[0], v_w_in[0], "adamw_w_in")
    small_w = dict(ln1_g=ln1_g, b_gate=b_gate, q_a_g=q_a_g, kv_a_g=kv_a_g, q_norm_g=q_norm_g, k_norm_g=k_norm_g,
                   ln2_g=ln2_g, ffn_conv_b=ffn_conv_b, conv_w=conv_w[0].reshape(1, -1),
                   ffn_conv_w=ffn_conv_w[0].reshape(1, -1))
    small_g["conv_w"] = g_cw.reshape(1, -1)
    small_g["ffn_conv_w"] = g_fcw.reshape(1, -1)
    packed_names = list(small_w)

    def pack(get):
        vflat = jnp.concatenate([get(k).reshape(1, -1) for k in packed_names], axis=1)
        nr = -(-vflat.shape[1] // (SUB * LANE)) * SUB
        return jnp.pad(vflat, ((0, 0), (0, nr * LANE - vflat.shape[1])), constant_values=1.0).reshape(nr, LANE)

    res = _adamw(pack(lambda k: small_g[k])[None], pack(lambda k: small_w[k]), pack(lambda k: loc["m_" + k]),
                 pack(lambda k: loc["v_" + k]), "adamw_small")
    res = [r.reshape(1, -1) for r in res]
    off = 0
    for k in packed_names:
        shape = loc[k].shape
        size = small_w[k].shape[1]
        out[k] = [r[:, off:off + size].reshape(shape) for r in res]
        off += size
    for k in names:
        out[k] = [r[None] for r in out[k]]

    order = ["ln1_g", "w_in", "b_gate", "conv_w", "w_conv_out", "q_a_g", "w_q_b", "kv_a_g", "w_kv_b", "q_norm_g",
             "k_norm_g", "w_mla_out", "w_o", "ln2_g", "w_ffn_up", "ffn_conv_w", "ffn_conv_b", "w_ffn_down"]
    return (loss, grad_x[None], *[out[k][0] for k in order], *[out[k][1] for k in order],
            *[out[k][2] for k in order], *[out[k][3] for k in order])
```

```python
import functools

import jax
import jax.numpy as jnp
from jax import lax
from jax.experimental import pallas as pl
from jax.experimental.pallas import tpu as pltpu

BF = jnp.bfloat16
F32 = jnp.float32
MESH = pl.DeviceIdType.MESH
N_DEV = 8

NOPE = 128
ROPE = 64
HALF = ROPE // 2
HEAD_QK = NOPE + ROPE
HEAD_V = 128
LANE = 128
SUB = 8
NORM_EPS = 1e-6
NEG_INF = -1e30
ROPE_THETA = 10000.0
ADAM_LR = 0.001
ADAM_B1 = 0.9
ADAM_B2 = 0.999
ADAM_EPS = 1e-08
ADAM_WD = 0.01
ADAM_STEP = 10

VMEM_LIMIT = 52 * 1024 * 1024
MM_TM, MM_TN, MM_TK, MM_TS = 1024, 1536, 2048, 1024
ROW_TILE, ROW_TILE_BWD = 512, 256
HEAD_ROW_TILE, HEAD_ROW_TILE_BWD = 256, 128
COL_TILE = 512
ATTN_TILE = 1024
ATTN_TILE_FWD = 1024
ANY = pl.BlockSpec(memory_space=pl.ANY)


def _pick(n, target, mult):
    t = (min(n, target) // mult) * mult
    while t > 0:
        if n % t == 0:
            return t
        t -= mult
    raise ValueError(f"no tile for {n} (target {target}, multiple {mult})")


def _cp(*sem):
    return pltpu.CompilerParams(dimension_semantics=sem, vmem_limit_bytes=VMEM_LIMIT)


def _accumulate(kk, nk, acc, part, finish):
    if nk == 1:
        finish(part())
        return

    @pl.when(kk == 0)
    def _():
        acc[...] = part()

    @pl.when((kk > 0) & (kk < nk - 1))
    def _():
        acc[...] += part()

    @pl.when(kk == nk - 1)
    def _():
        finish(acc[...] + part())


def _mm_call(body, name, grid, in_specs, args, out_spec, out_shape, acc_shape, nk, dep):
    if dep is not None:
        in_specs = in_specs + [ANY]
        args = args + [dep]
    return pl.pallas_call(
        body, name=name, grid=grid, in_specs=in_specs, out_specs=out_spec, out_shape=out_shape,
        scratch_shapes=[pltpu.VMEM(acc_shape, F32)] if nk > 1 else [],
        compiler_params=_cp("parallel", "parallel", "arbitrary"),
    )(*args)


def _mm_nn(a, b3, name, add=None, out_dtype=F32, blk0=0, nblk=None, dep=None):
    m, k = a.shape
    nb_all, k2, nbw = b3.shape
    assert k == k2
    nblk = nb_all - blk0 if nblk is None else nblk
    n = nblk * nbw
    tm = _pick(m, MM_TM, 16)
    tn = _pick(nbw, MM_TN, LANE)
    tk = _pick(k, MM_TK, LANE)
    per = nbw // tn
    nk = k // tk

    def body(*refs):
        a_ref, b_ref = refs[:2]
        c_ref = refs[2] if add is not None else None
        o_ref = refs[2 + (add is not None) + (dep is not None)]
        acc = refs[-1]

        def part():
            return jnp.dot(a_ref[...].astype(BF), b_ref[...].astype(BF), preferred_element_type=F32)

        def finish(r):
            if add is not None:
                r = r + c_ref[...]
            o_ref[...] = r.astype(out_dtype)

        _accumulate(pl.program_id(2), nk, acc, part, finish)

    in_specs = [pl.BlockSpec((tm, tk), lambda i, j, kk: (i, kk)),
                pl.BlockSpec((None, tk, tn), lambda i, j, kk: (blk0 + j // per, kk, j % per))]
    args = [a, b3]
    if add is not None:
        in_specs.append(pl.BlockSpec((tm, tn), lambda i, j, kk: (i, j)))
        args.append(add)
    return _mm_call(body, name, (m // tm, n // tn, nk), in_specs, args,
                    pl.BlockSpec((tm, tn), lambda i, j, kk: (i, j)), jax.ShapeDtypeStruct((m, n), out_dtype),
                    (tm, tn), nk, dep)


def _mm_nt(a, b3, name, add=None, out_dtype=F32, blk0=0, nblk=None, dep=None):
    m, n = a.shape
    nb_all, k, nbw = b3.shape
    nblk = nb_all - blk0 if nblk is None else nblk
    assert n == nblk * nbw
    tm = _pick(m, MM_TM, 16)
    tn = _pick(k, MM_TN, LANE)
    tk = _pick(nbw, MM_TK, LANE)
    per = nbw // tk
    nk = n // tk

    def body(*refs):
        a_ref, b_ref = refs[:2]
        c_ref = refs[2] if add is not None else None
        o_ref = refs[2 + (add is not None) + (dep is not None)]
        acc = refs[-1]

        def part():
            return lax.dot_general(a_ref[...].astype(BF), b_ref[...].astype(BF),
                                   (((1,), (1,)), ((), ())), preferred_element_type=F32)

        def finish(r):
            if add is not None:
                r = r + c_ref[...]
            o_ref[...] = r.astype(out_dtype)

        _accumulate(pl.program_id(2), nk, acc, part, finish)

    in_specs = [pl.BlockSpec((tm, tk), lambda i, j, kk: (i, kk)),
                pl.BlockSpec((None, tn, tk), lambda i, j, kk: (blk0 + kk // per, j, kk % per))]
    args = [a, b3]
    if add is not None:
        in_specs.append(pl.BlockSpec((tm, tn), lambda i, j, kk: (i, j)))
        args.append(add)
    return _mm_call(body, name, (m // tm, k // tn, nk), in_specs, args,
                    pl.BlockSpec((tm, tn), lambda i, j, kk: (i, j)), jax.ShapeDtypeStruct((m, k), out_dtype),
                    (tm, tn), nk, dep)


def _mm_tn(a, b, nblk, name, out_dtype=BF, dep=None):
    s, m = a.shape
    s2, n = b.shape
    assert s == s2 and n % nblk == 0
    nbw = n // nblk
    tm = _pick(m, MM_TN, LANE)
    tn = _pick(nbw, MM_TN, LANE)
    ts = _pick(s, MM_TS, LANE)
    per = nbw // tn
    ns = s // ts

    def body(*refs):
        a_ref, b_ref = refs[:2]
        o_ref = refs[2 + (dep is not None)]
        acc = refs[-1]

        def part():
            return lax.dot_general(a_ref[...].astype(BF), b_ref[...].astype(BF),
                                   (((0,), (0,)), ((), ())), preferred_element_type=F32)

        def finish(r):
            o_ref[...] = r.astype(out_dtype)

        _accumulate(pl.program_id(2), ns, acc, part, finish)

    in_specs = [pl.BlockSpec((ts, tm), lambda i, j, ss: (ss, i)),
                pl.BlockSpec((ts, tn), lambda i, j, ss: (ss, j))]
    return _mm_call(body, name, (m // tm, n // tn, ns), in_specs, [a, b],
                    pl.BlockSpec((None, tm, tn), lambda i, j, ss: (j // per, i, j % per)),
                    jax.ShapeDtypeStruct((nblk, m, nbw), out_dtype), (tm, tn), ns, dep)


def _rows8(rows, width):
    idx = lax.broadcasted_iota(jnp.int32, (SUB, width), 0)
    out = jnp.zeros((SUB, width), F32)
    for r, v in enumerate(rows):
        out = jnp.where(idx == r, v, out)
    return out


def _rms_fwd(x, g, width, col_blk, name):
    s = x.shape[0]
    tr = _pick(s, ROW_TILE, 16)

    def body(x_ref, g_ref, u_ref):
        xv = x_ref[...]
        r = lax.rsqrt(jnp.mean(xv * xv, axis=-1, keepdims=True) + NORM_EPS)
        u_ref[...] = ((xv * r) * g_ref[...]).astype(BF)

    return pl.pallas_call(
        body, name=name, grid=(s // tr,),
        in_specs=[pl.BlockSpec((tr, width), lambda i: (i, col_blk)),
                  pl.BlockSpec((1, width), lambda i: (0, 0))],
        out_specs=pl.BlockSpec((tr, width), lambda i: (i, 0)),
        out_shape=jax.ShapeDtypeStruct((s, width), BF),
        compiler_params=_cp("parallel"),
    )(x, g)


def _rms_bwd(x, du, g, width, col_blk, name, extra=None, out_dtype=F32, also_bf16=False):
    s = x.shape[0]
    tr = _pick(s, ROW_TILE_BWD, 16)

    def body(*refs):
        x_ref, du_ref, g_ref = refs[:3]
        e_ref = refs[3] if extra is not None else None
        dx_ref = refs[3 + (extra is not None)]
        dxb_ref = refs[4 + (extra is not None)] if also_bf16 else None
        dg_ref = refs[-1]
        i = pl.program_id(0)
        xv = x_ref[...]
        duv = du_ref[...].astype(F32)
        r = lax.rsqrt(jnp.mean(xv * xv, axis=-1, keepdims=True) + NORM_EPS)
        nv = xv * r
        dn = duv * g_ref[...]
        dx = r * (dn - nv * jnp.mean(dn * nv, axis=-1, keepdims=True))
        if extra is not None:
            dx = dx + e_ref[...]
        dx_ref[...] = dx.astype(out_dtype)
        if also_bf16:
            dxb_ref[...] = dx.astype(BF)

        @pl.when(i == 0)
        def _():
            dg_ref[...] = jnp.zeros_like(dg_ref)

        dg_ref[...] += _rows8([jnp.sum(duv * nv, axis=0, keepdims=True)], width)

    in_specs = [pl.BlockSpec((tr, width), lambda i: (i, col_blk)),
                pl.BlockSpec((tr, width), lambda i: (i, 0)),
                pl.BlockSpec((1, width), lambda i: (0, 0))]
    args = [x, du, g]
    if extra is not None:
        in_specs.append(pl.BlockSpec((tr, width), lambda i: (i, 0)))
        args.append(extra)
    return pl.pallas_call(
        body, name=name, grid=(s // tr,),
        in_specs=in_specs,
        out_specs=[pl.BlockSpec((tr, width), lambda i: (i, 0))] * (1 + also_bf16)
        + [pl.BlockSpec((SUB, width), lambda i: (0, 0))],
        out_shape=[jax.ShapeDtypeStruct((s, width), out_dtype)] + [jax.ShapeDtypeStruct((s, width), BF)] * also_bf16
        + [jax.ShapeDtypeStruct((SUB, width), F32)],
        compiler_params=_cp("arbitrary"),
    )(*args)


def _down(cur, prev8, k):
    ext = jnp.concatenate([prev8, cur], axis=0)
    return pltpu.roll(ext, k, axis=0)[SUB:]


def _up(cur, next8, k):
    ext = jnp.concatenate([cur, next8], axis=0)
    return pltpu.roll(ext, ext.shape[0] - k, axis=0)[:cur.shape[0]]


def _lags(cur, prev8):
    return _down(cur, prev8, 1), _down(cur, prev8, 2)


def _conv3(w_ref, cur, prev8, lags=None):
    lag1, lag2 = _lags(cur, prev8) if lags is None else lags
    return w_ref[0:1, :] * lag2 + w_ref[1:2, :] * lag1 + w_ref[2:3, :] * cur


def _conv3_t(w_ref, cur, next8):
    return w_ref[2:3, :] * cur + w_ref[1:2, :] * _up(cur, next8, 1) + w_ref[0:1, :] * _up(cur, next8, 2)


def _spec_cur(tr, tc, c0):
    return pl.BlockSpec((tr, tc), lambda j, i: (i, c0 + j))


def _spec_prev(tr, tc, c0):
    return pl.BlockSpec((SUB, tc), lambda j, i: (jnp.maximum(i * (tr // SUB) - 1, 0), c0 + j))


def _spec_next(tr, tc, c0, s):
    return pl.BlockSpec((SUB, tc), lambda j, i: (jnp.minimum((i + 1) * (tr // SUB), s // SUB - 1), c0 + j))


def _spec_w(tc, c0):
    return pl.BlockSpec((SUB, tc), lambda j, i: (0, c0 + j))


def _pad8(w):
    return jnp.pad(w, ((0, SUB - w.shape[0]), (0, 0)))


def _conv_mix_fwd(z_a, cw8, conv):
    s = z_a.shape[0]
    tr = _pick(s, ROW_TILE, 16)
    tc = _pick(conv, COL_TILE, LANE)
    nc = conv // tc

    def body(zb_ref, zc_ref, zv_ref, zcp_ref, zvp_ref, w_ref, p_ref):
        i = pl.program_id(1)
        cv = zc_ref[...] * zv_ref[...]
        cvp = jnp.where(i > 0, zcp_ref[...] * zvp_ref[...], 0.0)
        p_ref[...] = (zb_ref[...] * _conv3(w_ref, cv, cvp)).astype(BF)

    return pl.pallas_call(
        body, name="conv_mix_fwd", grid=(nc, s // tr),
        in_specs=[_spec_cur(tr, tc, 0), _spec_cur(tr, tc, nc), _spec_cur(tr, tc, 2 * nc),
                  _spec_prev(tr, tc, nc), _spec_prev(tr, tc, 2 * nc), _spec_w(tc, 0)],
        out_specs=_spec_cur(tr, tc, 0),
        out_shape=jax.ShapeDtypeStruct((s, conv), BF),
        compiler_params=_cp("parallel", "parallel"),
    )(z_a, z_a, z_a, z_a, z_a, cw8)


def _conv_mix_bwd(z_a, d_p, cw8, conv):
    s = z_a.shape[0]
    tr = _pick(s, ROW_TILE_BWD, 16)
    tc = _pick(conv, COL_TILE, LANE)
    nc = conv // tc
    nr = s // tr

    def body(zb_ref, zbn_ref, zc_ref, zcp_ref, zv_ref, zvp_ref, dp_ref, dpn_ref, w_ref,
             dzb_ref, dzc_ref, dzv_ref, dw_ref):
        i = pl.program_id(1)
        zc = zc_ref[...]
        zv = zv_ref[...]
        cv = zc * zv
        cvp = jnp.where(i > 0, zcp_ref[...] * zvp_ref[...], 0.0)
        cv1, cv2 = _lags(cv, cvp)
        dpv = dp_ref[...]
        dzb_ref[...] = (dpv * _conv3(w_ref, cv, cvp, (cv1, cv2))).astype(BF)
        dcc = dpv * zb_ref[...]
        dccn = jnp.where(i < nr - 1, dpn_ref[...] * zbn_ref[...], 0.0)
        dcv = _conv3_t(w_ref, dcc, dccn)
        dzc_ref[...] = (dcv * zv).astype(BF)
        dzv_ref[...] = (dcv * zc).astype(BF)

        @pl.when(i == 0)
        def _():
            dw_ref[...] = jnp.zeros_like(dw_ref)

        dw_ref[...] += _rows8([jnp.sum(dcc * cv2, axis=0, keepdims=True),
                               jnp.sum(dcc * cv1, axis=0, keepdims=True),
                               jnp.sum(dcc * cv, axis=0, keepdims=True)], tc)

    out = jax.ShapeDtypeStruct((s, conv), BF)
    return pl.pallas_call(
        body, name="conv_mix_bwd", grid=(nc, nr),
        in_specs=[_spec_cur(tr, tc, 0), _spec_next(tr, tc, 0, s),
                  _spec_cur(tr, tc, nc), _spec_prev(tr, tc, nc),
                  _spec_cur(tr, tc, 2 * nc), _spec_prev(tr, tc, 2 * nc),
                  _spec_cur(tr, tc, 0), _spec_next(tr, tc, 0, s), _spec_w(tc, 0)],
        out_specs=[_spec_cur(tr, tc, 0), _spec_cur(tr, tc, 0), _spec_cur(tr, tc, 0), _spec_w(tc, 0)],
        out_shape=[out, out, out, jax.ShapeDtypeStruct((SUB, conv), F32)],
        compiler_params=_cp("parallel", "arbitrary"),
    )(z_a, z_a, z_a, z_a, z_a, z_a, d_p, d_p, cw8)


def _silu_parts(ag):
    sg = jax.nn.sigmoid(ag)
    return ag * sg, sg


def _ffn_act_fwd(a_pre, cw8, cb, dff):
    s = a_pre.shape[0]
    tr = _pick(s, ROW_TILE, 16)
    tc = _pick(dff, COL_TILE, LANE)
    nc = dff // tc

    def body(xg_ref, xgp_ref, xu_ref, xup_ref, wg_ref, wu_ref, bg_ref, bu_ref, f_ref):
        i = pl.program_id(1)
        xgp = jnp.where(i > 0, xgp_ref[...], 0.0)
        xup = jnp.where(i > 0, xup_ref[...], 0.0)
        ag = _conv3(wg_ref, xg_ref[...], xgp) + bg_ref[...]
        au = _conv3(wu_ref, xu_ref[...], xup) + bu_ref[...]
        f_ref[...] = (_silu_parts(ag)[0] * au).astype(BF)

    return pl.pallas_call(
        body, name="ffn_act_fwd", grid=(nc, s // tr),
        in_specs=[_spec_cur(tr, tc, 0), _spec_prev(tr, tc, 0), _spec_cur(tr, tc, nc), _spec_prev(tr, tc, nc),
                  _spec_w(tc, 0), _spec_w(tc, nc),
                  pl.BlockSpec((1, tc), lambda j, i: (0, j)), pl.BlockSpec((1, tc), lambda j, i: (0, nc + j))],
        out_specs=_spec_cur(tr, tc, 0),
        out_shape=jax.ShapeDtypeStruct((s, dff), BF),
        compiler_params=_cp("parallel", "parallel"),
    )(a_pre, a_pre, a_pre, a_pre, cw8, cw8, cb, cb)


def _ffn_act_bwd(a_pre, d_f, cw8, cb, dff):
    s = a_pre.shape[0]
    tr = _pick(s, ROW_TILE_BWD, 16)
    tc = _pick(dff, COL_TILE, LANE)
    nc = dff // tc
    nr = s // tr

    def body(xg_ref, xgp_ref, xgn_ref, xu_ref, xup_ref, xun_ref, df_ref, dfn_ref,
             wg_ref, wu_ref, bg_ref, bu_ref, dxg_ref, dxu_ref, dwg_ref, dwu_ref):
        i = pl.program_id(1)
        xg = xg_ref[...]
        xu = xu_ref[...]
        xgp = jnp.where(i > 0, xgp_ref[...], 0.0)
        xup = jnp.where(i > 0, xup_ref[...], 0.0)

        def d_act(xg_t, xgp_t, xu_t, xup_t, df_t, lags_g=None, lags_u=None):
            ag = _conv3(wg_ref, xg_t, xgp_t, lags_g) + bg_ref[...]
            au = _conv3(wu_ref, xu_t, xup_t, lags_u) + bu_ref[...]
            sil, sg = _silu_parts(ag)
            return df_t * au * (sg * (1.0 + ag * (1.0 - sg))), df_t * sil

        lags_g = _lags(xg, xgp)
        lags_u = _lags(xu, xup)
        dag, dau = d_act(xg, xgp, xu, xup, df_ref[...], lags_g, lags_u)
        dfn = jnp.where(i < nr - 1, dfn_ref[...], 0.0)
        dagn, daun = d_act(xgn_ref[...], xg[tr - SUB:], xun_ref[...], xu[tr - SUB:], dfn)
        dxg_ref[...] = _conv3_t(wg_ref, dag, dagn).astype(BF)
        dxu_ref[...] = _conv3_t(wu_ref, dau, daun).astype(BF)

        @pl.when(i == 0)
        def _():
            dwg_ref[...] = jnp.zeros_like(dwg_ref)
            dwu_ref[...] = jnp.zeros_like(dwu_ref)

        def wgrad(da, x, lags):
            return _rows8([jnp.sum(da * lags[1], axis=0, keepdims=True),
                           jnp.sum(da * lags[0], axis=0, keepdims=True),
                           jnp.sum(da * x, axis=0, keepdims=True),
                           jnp.sum(da, axis=0, keepdims=True)], tc)

        dwg_ref[...] += wgrad(dag, xg, lags_g)
        dwu_ref[...] += wgrad(dau, xu, lags_u)

    half = jax.ShapeDtypeStruct((s, dff), BF)
    wsh = jax.ShapeDtypeStruct((SUB, dff), F32)
    return pl.pallas_call(
        body, name="ffn_act_bwd", grid=(nc, nr),
        in_specs=[_spec_cur(tr, tc, 0), _spec_prev(tr, tc, 0), _spec_next(tr, tc, 0, s),
                  _spec_cur(tr, tc, nc), _spec_prev(tr, tc, nc), _spec_next(tr, tc, nc, s),
                  _spec_cur(tr, tc, 0), _spec_next(tr, tc, 0, s),
                  _spec_w(tc, 0), _spec_w(tc, nc),
                  pl.BlockSpec((1, tc), lambda j, i: (0, j)), pl.BlockSpec((1, tc), lambda j, i: (0, nc + j))],
        out_specs=[_spec_cur(tr, tc, 0), _spec_cur(tr, tc, 0), _spec_w(tc, 0), _spec_w(tc, 0)],
        out_shape=[half, half, wsh, wsh],
        compiler_params=_cp("parallel", "arbitrary"),
    )(a_pre, a_pre, a_pre, a_pre, a_pre, a_pre, d_f, d_f, cw8, cw8, cb, cb)


def _gate_fwd(z_g, b_gate, yc, ym, d):
    s = z_g.shape[0]
    tr = _pick(s, ROW_TILE, 16)
    tc = _pick(d, COL_TILE, LANE)
    nc = d // tc

    def body(za_ref, zb_ref, ba_ref, bb_ref, yc_ref, ym_ref, o_ref):
        ga = jax.nn.sigmoid(za_ref[...] + ba_ref[...])
        gb = jax.nn.sigmoid(zb_ref[...] + bb_ref[...])
        o_ref[...] = (ga * yc_ref[...] + gb * ym_ref[...]).astype(BF)

    return pl.pallas_call(
        body, name="gate_fwd", grid=(nc, s // tr),
        in_specs=[_spec_cur(tr, tc, 0), _spec_cur(tr, tc, nc),
                  pl.BlockSpec((1, tc), lambda j, i: (0, j)), pl.BlockSpec((1, tc), lambda j, i: (0, nc + j)),
                  _spec_cur(tr, tc, 0), _spec_cur(tr, tc, 0)],
        out_specs=_spec_cur(tr, tc, 0),
        out_shape=jax.ShapeDtypeStruct((s, d), BF),
        compiler_params=_cp("parallel", "parallel"),
    )(z_g, z_g, b_gate, b_gate, yc, ym)


def _gate_bwd(d_mix, z_g, b_gate, yc, ym, d):
    s = z_g.shape[0]
    tr = _pick(s, ROW_TILE, 16)
    tc = _pick(d, COL_TILE, LANE)
    nc = d // tc

    def body(dm_ref, za_ref, zb_ref, ba_ref, bb_ref, yc_ref, ym_ref,
             dza_ref, dzb_ref, dyc_ref, dym_ref, dba_ref, dbb_ref):
        i = pl.program_id(1)
        dm = dm_ref[...]
        ga = jax.nn.sigmoid(za_ref[...] + ba_ref[...])
        gb = jax.nn.sigmoid(zb_ref[...] + bb_ref[...])
        dza = dm * yc_ref[...] * (ga * (1.0 - ga))
        dzb = dm * ym_ref[...] * (gb * (1.0 - gb))
        dza_ref[...] = dza.astype(BF)
        dzb_ref[...] = dzb.astype(BF)
        dyc_ref[...] = (dm * ga).astype(BF)
        dym_ref[...] = (dm * gb).astype(BF)

        @pl.when(i == 0)
        def _():
            dba_ref[...] = jnp.zeros_like(dba_ref)
            dbb_ref[...] = jnp.zeros_like(dbb_ref)

        dba_ref[...] += _rows8([jnp.sum(dza, axis=0, keepdims=True)], tc)
        dbb_ref[...] += _rows8([jnp.sum(dzb, axis=0, keepdims=True)], tc)

    act = jax.ShapeDtypeStruct((s, d), BF)
    bsh = jax.ShapeDtypeStruct((SUB, d), F32)
    return pl.pallas_call(
        body, name="gate_bwd", grid=(nc, s // tr),
        in_specs=[_spec_cur(tr, tc, 0), _spec_cur(tr, tc, 0), _spec_cur(tr, tc, nc),
                  pl.BlockSpec((1, tc), lambda j, i: (0, j)), pl.BlockSpec((1, tc), lambda j, i: (0, nc + j)),
                  _spec_cur(tr, tc, 0), _spec_cur(tr, tc, 0)],
        out_specs=[_spec_cur(tr, tc, 0)] * 4 + [_spec_w(tc, 0)] * 2,
        out_shape=[act, act, act, act, bsh, bsh],
        compiler_params=_cp("parallel", "arbitrary"),
    )(d_mix, z_g, z_g, b_gate, b_gate, yc, ym)


def _lay(v):
    z = jnp.zeros(v.shape[:-1] + (HALF,), v.dtype)
    return jnp.concatenate([v[..., :HALF], z, v[..., HALF:], z], axis=-1)


def _unlay(v):
    return jnp.concatenate([v[..., :HALF], v[..., 2 * HALF:3 * HALF]], axis=-1)


def _rope_tables(positions):
    s = positions.shape[0]
    tr = _pick(s, ROW_TILE, 8)
    inv_freq = ROPE_THETA ** (-jnp.arange(0, ROPE, 2, dtype=F32) / ROPE)
    consts = jnp.stack([_lay(jnp.concatenate([inv_freq, inv_freq])),
                        _lay(jnp.ones((ROPE,), F32)),
                        _lay(jnp.concatenate([-jnp.ones((HALF,), F32), jnp.ones((HALF,), F32)]))])
    consts = _pad8(consts)

    def body(p_ref, c_ref, cos_ref, sin_ref):
        ang = p_ref[...].astype(F32) * c_ref[0:1, :]
        cos_ref[...] = jnp.cos(ang) * c_ref[1:2, :]
        sin_ref[...] = jnp.sin(ang) * c_ref[2:3, :]

    tab = jax.ShapeDtypeStruct((s, LANE), F32)
    return pl.pallas_call(
        body, name="rope_tables", grid=(s // tr,),
        in_specs=[pl.BlockSpec((tr, 1), lambda i: (i, 0)), pl.BlockSpec((SUB, LANE), lambda i: (0, 0))],
        out_specs=[pl.BlockSpec((tr, LANE), lambda i: (i, 0))] * 2,
        out_shape=[tab, tab],
        compiler_params=_cp("parallel"),
    )(positions, consts)


def _rope(t, cos, sin):
    return t * cos + pltpu.roll(t, 2 * HALF, axis=1) * sin


def _rope_t(d, cos, sin):
    return d * cos + pltpu.roll(d * sin, 2 * HALF, axis=1)


def _head_fwd(q_raw, kv_raw, z_a, kr_blk, cos, sin, gains, heads):
    s = q_raw.shape[0]
    tr = _pick(s, HEAD_ROW_TILE, 16)
    hw = heads * LANE

    def body(q_ref, kv_ref, kr_ref, cos_ref, sin_ref, g_ref, qo_ref, ko_ref, vo_ref):
        cosv = cos_ref[...]
        sinv = sin_ref[...]
        krv = kr_ref[...]
        kr_ss = jnp.sum(krv * krv, axis=-1, keepdims=True)
        for h in range(heads):
            lo = h * LANE
            qn = q_ref[:, lo:lo + LANE]
            qr = q_ref[:, hw + lo:hw + lo + LANE]
            ss = jnp.sum(qn * qn, axis=-1, keepdims=True) + jnp.sum(qr * qr, axis=-1, keepdims=True)
            r = lax.rsqrt(ss / HEAD_QK + NORM_EPS)
            qo_ref[:, 2 * lo:2 * lo + LANE] = ((qn * r) * g_ref[0:1, :]).astype(BF)
            qo_ref[:, 2 * lo + LANE:2 * lo + 2 * LANE] = _rope((qr * r) * g_ref[1:2, :], cosv, sinv).astype(BF)
            kn = kv_ref[:, 2 * lo:2 * lo + LANE]
            ss = jnp.sum(kn * kn, axis=-1, keepdims=True) + kr_ss
            r = lax.rsqrt(ss / HEAD_QK + NORM_EPS)
            ko_ref[:, 2 * lo:2 * lo + LANE] = ((kn * r) * g_ref[2:3, :]).astype(BF)
            ko_ref[:, 2 * lo + LANE:2 * lo + 2 * LANE] = _rope((krv * r) * g_ref[3:4, :], cosv, sinv).astype(BF)
            vo_ref[:, lo:lo + LANE] = kv_ref[:, 2 * lo + LANE:2 * lo + 2 * LANE].astype(BF)

    row = lambda w: pl.BlockSpec((tr, w), lambda i: (i, 0))
    return pl.pallas_call(
        body, name="head_fwd", grid=(s // tr,),
        in_specs=[row(2 * hw), row(2 * hw), pl.BlockSpec((tr, LANE), lambda i: (i, kr_blk)),
                  row(LANE), row(LANE), pl.BlockSpec((SUB, LANE), lambda i: (0, 0))],
        out_specs=[row(2 * hw), row(2 * hw), row(hw)],
        out_shape=[jax.ShapeDtypeStruct((s, 2 * hw), BF), jax.ShapeDtypeStruct((s, 2 * hw), BF),
                   jax.ShapeDtypeStruct((s, hw), BF)],
        compiler_params=_cp("parallel"),
    )(q_raw, kv_raw, z_a, cos, sin, gains)


def _head_bwd(q_raw, kv_raw, z_a, kr_blk, cos, sin, gains, dq_att, dk_att, dv, heads):
    s = q_raw.shape[0]
    tr = _pick(s, HEAD_ROW_TILE_BWD, 16)
    hw = heads * LANE

    def body(q_ref, kv_ref, kr_ref, cos_ref, sin_ref, g_ref, dq_ref, dk_ref, dv_ref,
             dqr_ref, dkv_ref, dkr_ref, dg_ref):
        i = pl.program_id(0)
        cosv = cos_ref[...]
        sinv = sin_ref[...]
        krv = kr_ref[...]
        kr_ss = jnp.sum(krv * krv, axis=-1, keepdims=True)
        dkr = jnp.zeros((tr, LANE), F32)
        dgs = [jnp.zeros((1, LANE), F32) for _ in range(4)]

        def norm_bwd(xn, xr, ss, dn_out, dr_out, gn, gr):
            r = lax.rsqrt(ss / HEAD_QK + NORM_EPS)
            nn = xn * r
            nr = xr * r
            dt = _rope_t(dr_out, cosv, sinv)
            dnn = dn_out * gn
            dnr = dt * gr
            mean = (jnp.sum(dnn * nn, axis=-1, keepdims=True) + jnp.sum(dnr * nr, axis=-1, keepdims=True)) / HEAD_QK
            return (r * (dnn - nn * mean), r * (dnr - nr * mean),
                    jnp.sum(dn_out * nn, axis=0, keepdims=True), jnp.sum(dt * nr, axis=0, keepdims=True))

        for h in range(heads):
            lo = h * LANE
            qn = q_ref[:, lo:lo + LANE]
            qr = q_ref[:, hw + lo:hw + lo + LANE]
            ss = jnp.sum(qn * qn, axis=-1, keepdims=True) + jnp.sum(qr * qr, axis=-1, keepdims=True)
            dxn, dxr, g0, g1 = norm_bwd(qn, qr, ss, dq_ref[:, 2 * lo:2 * lo + LANE],
                                        dq_ref[:, 2 * lo + LANE:2 * lo + 2 * LANE], g_ref[0:1, :], g_ref[1:2, :])
            dqr_ref[:, lo:lo + LANE] = dxn.astype(BF)
            dqr_ref[:, hw + lo:hw + lo + LANE] = dxr.astype(BF)
            kn = kv_ref[:, 2 * lo:2 * lo + LANE]
            ss = jnp.sum(kn * kn, axis=-1, keepdims=True) + kr_ss
            dxn, dxr, g2, g3 = norm_bwd(kn, krv, ss, dk_ref[:, 2 * lo:2 * lo + LANE],
                                        dk_ref[:, 2 * lo + LANE:2 * lo + 2 * LANE], g_ref[2:3, :], g_ref[3:4, :])
            dkv_ref[:, 2 * lo:2 * lo + LANE] = dxn.astype(BF)
            dkv_ref[:, 2 * lo + LANE:2 * lo + 2 * LANE] = dv_ref[:, lo:lo + LANE].astype(BF)
            dkr = dkr + dxr
            dgs = [a + b for a, b in zip(dgs, (g0, g1, g2, g3))]
        dkr_ref[...] = dkr

        @pl.when(i == 0)
        def _():
            dg_ref[...] = jnp.zeros_like(dg_ref)

        dg_ref[...] += _rows8(dgs, LANE)

    row = lambda w: pl.BlockSpec((tr, w), lambda i: (i, 0))
    return pl.pallas_call(
        body, name="head_bwd", grid=(s // tr,),
        in_specs=[row(2 * hw), row(2 * hw), pl.BlockSpec((tr, LANE), lambda i: (i, kr_blk)),
                  row(LANE), row(LANE), pl.BlockSpec((SUB, LANE), lambda i: (0, 0)),
                  row(2 * hw), row(2 * hw), row(hw)],
        out_specs=[row(2 * hw), row(2 * hw), row(LANE), pl.BlockSpec((SUB, LANE), lambda i: (0, 0))],
        out_shape=[jax.ShapeDtypeStruct((s, 2 * hw), BF), jax.ShapeDtypeStruct((s, 2 * hw), BF),
                   jax.ShapeDtypeStruct((s, LANE), F32), jax.ShapeDtypeStruct((SUB, LANE), F32)],
        compiler_params=_cp("arbitrary"),
    )(q_raw, kv_raw, z_a, cos, sin, gains, dq_att, dk_att, dv)


def _causal_mask(nrows, ncols, row0):
    rows = lax.broadcasted_iota(jnp.int32, (nrows, ncols), 0) + row0
    cols = lax.broadcasted_iota(jnp.int32, (nrows, ncols), 1)
    return cols <= rows


def _causal_steps(nt, q_major):
    pairs = ([(i, j) for i in range(nt) for j in range(i + 1)] if q_major
             else [(i, j) for j in range(nt) for i in range(j, nt)])
    return (jnp.array([p[0] for p in pairs], jnp.int32), jnp.array([p[1] for p in pairs], jnp.int32))


def _attn_fwd(q_att, k_att, v, heads):
    s = q_att.shape[0]
    t = _pick(s, ATTN_TILE_FWD, LANE)
    nt = s // t
    th = t
    scale = HEAD_QK ** -0.5
    qi, kj = _causal_steps(nt, True)

    def body(qi_ref, kj_ref, q_ref, k_ref, v_ref, o_ref, ob_ref, lse_ref, m_s, l_s, acc_s):
        st = pl.program_id(1)
        i = qi_ref[st]
        j = kj_ref[st]

        @pl.when(j == 0)
        def _():
            m_s[...] = jnp.full_like(m_s, NEG_INF)
            l_s[...] = jnp.zeros_like(l_s)
            acc_s[...] = jnp.zeros_like(acc_s)

        def step(masked):
            for r0 in range(0, t, th):
                rows = slice(r0, r0 + th)
                sc = lax.dot_general(q_ref[rows, :], k_ref[...], (((1,), (1,)), ((), ())),
                                     preferred_element_type=F32) * scale
                if masked:
                    sc = jnp.where(_causal_mask(th, t, r0), sc, NEG_INF)
                m_prev = m_s[rows, :]
                m_new = jnp.maximum(m_prev, jnp.max(sc, axis=-1, keepdims=True))
                alpha = jnp.exp(m_prev - m_new)
                p = jnp.exp(sc - jnp.tile(m_new, (1, t // LANE)))
                l_s[rows, :] = alpha * l_s[rows, :] + jnp.sum(p, axis=-1, keepdims=True)
                acc_s[rows, :] = alpha * acc_s[rows, :] + jnp.dot(p.astype(BF), v_ref[...],
                                                                  preferred_element_type=F32)
                m_s[rows, :] = m_new

        @pl.when(j < i)
        def _():
            step(False)

        @pl.when(j == i)
        def _():
            step(True)
            o = acc_s[...] / l_s[...]
            o_ref[...] = o
            ob_ref[...] = o.astype(BF)
            lse_ref[...] = (m_s[...] + jnp.log(l_s[...]))[:, 0:1]

    q_idx = lambda h, st, qi_r, kj_r: (qi_r[st], h)
    kv_idx = lambda h, st, qi_r, kj_r: (kj_r[st], h)
    return pl.pallas_call(
        body, name="attn_fwd",
        grid_spec=pltpu.PrefetchScalarGridSpec(
            num_scalar_prefetch=2, grid=(heads, qi.shape[0]),
            in_specs=[pl.BlockSpec((t, 2 * LANE), q_idx), pl.BlockSpec((t, 2 * LANE), kv_idx),
                      pl.BlockSpec((t, LANE), kv_idx)],
            out_specs=[pl.BlockSpec((t, LANE), q_idx), pl.BlockSpec((t, LANE), q_idx),
                       pl.BlockSpec((None, t, 1), lambda h, st, qi_r, kj_r: (h, qi_r[st], 0))],
            scratch_shapes=[pltpu.VMEM((t, LANE), F32), pltpu.VMEM((t, LANE), F32), pltpu.VMEM((t, LANE), F32)]),
        out_shape=[jax.ShapeDtypeStruct((s, heads * LANE), F32), jax.ShapeDtypeStruct((s, heads * LANE), BF),
                   jax.ShapeDtypeStruct((heads, s, 1), F32)],
        compiler_params=_cp("parallel", "arbitrary"),
    )(qi, kj, q_att, k_att, v)


def _attn_bwd(q_att, k_att, v, o, lse, d_o, heads, dep=None):
    s = q_att.shape[0]
    t = _pick(s, ATTN_TILE, LANE)
    nt = s // t
    scale = HEAD_QK ** -0.5
    qi, kj = _causal_steps(nt, False)

    def body(qi_ref, kj_ref, q_ref, k_ref, v_ref, do_ref, o_ref, lse_ref, *rest):
        dq_ref, dk_ref, dv_ref, dk_s, dv_s = rest[-5:]
        st = pl.program_id(1)
        i = qi_ref[st]
        j = kj_ref[st]

        @pl.when(st == 0)
        def _():
            dq_ref[...] = jnp.zeros_like(dq_ref)

        @pl.when(i == j)
        def _():
            dk_s[...] = jnp.zeros_like(dk_s)
            dv_s[...] = jnp.zeros_like(dv_s)

        def step(masked):
            q = q_ref[...]
            k = k_ref[...]
            do = do_ref[...]
            sc = lax.dot_general(q, k, (((1,), (1,)), ((), ())), preferred_element_type=F32) * scale
            if masked:
                sc = jnp.where(_causal_mask(t, t, 0), sc, NEG_INF)
            p = jnp.exp(sc - lse_ref[...])
            dp = lax.dot_general(do, v_ref[...], (((1,), (1,)), ((), ())), preferred_element_type=F32)
            delta = jnp.sum(do.astype(F32) * o_ref[...], axis=-1, keepdims=True)
            ds = (p * (dp - delta) * scale).astype(BF)
            dv_s[...] += lax.dot_general(p.astype(BF), do, (((0,), (0,)), ((), ())), preferred_element_type=F32)
            dk_s[...] += lax.dot_general(ds, q, (((0,), (0,)), ((), ())), preferred_element_type=F32)
            rows = pl.ds(pl.multiple_of(i * t, t), t)
            dq_ref[rows, :] += jnp.dot(ds, k, preferred_element_type=F32)

        @pl.when(i > j)
        def _():
            step(False)

        @pl.when(i == j)
        def _():
            step(True)

        @pl.when(i == nt - 1)
        def _():
            dk_ref[...] = dk_s[...]
            dv_ref[...] = dv_s[...]

    q_idx = lambda h, st, qi_r, kj_r: (qi_r[st], h)
    kv_idx = lambda h, st, qi_r, kj_r: (kj_r[st], h)
    in_specs = [pl.BlockSpec((t, 2 * LANE), q_idx), pl.BlockSpec((t, 2 * LANE), kv_idx),
                pl.BlockSpec((t, LANE), kv_idx), pl.BlockSpec((t, LANE), q_idx), pl.BlockSpec((t, LANE), q_idx),
                pl.BlockSpec((None, t, 1), lambda h, st, qi_r, kj_r: (h, qi_r[st], 0))]
    args = [q_att, k_att, v, d_o, o, lse]
    if dep is not None:
        in_specs.append(ANY)
        args.append(dep)
    return pl.pallas_call(
        body, name="attn_bwd",
        grid_spec=pltpu.PrefetchScalarGridSpec(
            num_scalar_prefetch=2, grid=(heads, qi.shape[0]),
            in_specs=in_specs,
            out_specs=[pl.BlockSpec((s, 2 * LANE), lambda h, st, qi_r, kj_r: (0, h)),
                       pl.BlockSpec((t, 2 * LANE), kv_idx), pl.BlockSpec((t, LANE), kv_idx)],
            scratch_shapes=[pltpu.VMEM((t, 2 * LANE), F32), pltpu.VMEM((t, LANE), F32)]),
        out_shape=[jax.ShapeDtypeStruct((s, heads * 2 * LANE), F32),
                   jax.ShapeDtypeStruct((s, heads * 2 * LANE), F32),
                   jax.ShapeDtypeStruct((s, heads * LANE), F32)],
        compiler_params=_cp("parallel", "arbitrary"),
    )(qi, kj, *args)


def _loss_head(y, target):
    s, d = y.shape
    tr = _pick(s, ROW_TILE, 8)

    def body(y_ref, t_ref, dy_ref, dyb_ref, l_ref):
        i = pl.program_id(0)
        e = y_ref[...] - t_ref[...]
        dy_ref[...] = e / d
        dyb_ref[...] = (e / d).astype(BF)

        @pl.when(i == 0)
        def _():
            l_ref[...] = jnp.zeros_like(l_ref)

        l_ref[...] += 0.5 * jnp.sum(jnp.mean(e * e, axis=-1, keepdims=True), axis=0, keepdims=True)

    return pl.pallas_call(
        body, name="loss_head", grid=(s // tr,),
        in_specs=[pl.BlockSpec((tr, d), lambda i: (i, 0))] * 2,
        out_specs=[pl.BlockSpec((tr, d), lambda i: (i, 0)), pl.BlockSpec((tr, d), lambda i: (i, 0)),
                   pl.BlockSpec((SUB, LANE), lambda i: (0, 0))],
        out_shape=[jax.ShapeDtypeStruct((s, d), F32), jax.ShapeDtypeStruct((s, d), BF),
                   jax.ShapeDtypeStruct((SUB, LANE), F32)],
        compiler_params=_cp("arbitrary"),
    )(y, target)


def _sum_parts(parts, name):
    n, r, c = parts.shape
    tr = _pick(r, 512, 8)

    def body(p_ref, o_ref):
        g = p_ref[0].astype(F32)
        for k in range(1, n):
            g = g + p_ref[k].astype(F32)
        o_ref[...] = g

    return pl.pallas_call(
        body, name=name, grid=(r // tr,),
        in_specs=[pl.BlockSpec((n, tr, c), lambda i: (0, i, 0))],
        out_specs=pl.BlockSpec((tr, c), lambda i: (i, 0)),
        out_shape=jax.ShapeDtypeStruct((r, c), F32),
        compiler_params=_cp("parallel"),
    )(parts)


def _adamw(parts, w, m, v, name):
    n, r, c = parts.shape
    tr = _pick(r, 256, 8)

    def body(p_ref, w_ref, m_ref, v_ref, g_ref, d_ref, mo_ref, vo_ref):
        g = p_ref[0].astype(F32)
        for k in range(1, n):
            g = g + p_ref[k].astype(F32)
        m_new = ADAM_B1 * m_ref[...] + (1.0 - ADAM_B1) * g
        v_new = ADAM_B2 * v_ref[...] + (1.0 - ADAM_B2) * jnp.square(g)
        m_hat = m_new / (1.0 - ADAM_B1 ** ADAM_STEP)
        v_hat = v_new / (1.0 - ADAM_B2 ** ADAM_STEP)
        g_ref[...] = g
        d_ref[...] = -ADAM_LR * (m_hat / (jnp.sqrt(v_hat) + ADAM_EPS) + ADAM_WD * w_ref[...])
        mo_ref[...] = m_new
        vo_ref[...] = v_new

    spec = pl.BlockSpec((tr, c), lambda i: (i, 0))
    sh = jax.ShapeDtypeStruct((r, c), F32)
    return pl.pallas_call(
        body, name=name, grid=(r // tr,),
        in_specs=[pl.BlockSpec((n, tr, c), lambda i: (0, i, 0)), spec, spec, spec],
        out_specs=[spec] * 4, out_shape=[sh] * 4,
        compiler_params=_cp("parallel"),
    )(parts, w, m, v)


def _place():
    x, y, c = lax.axis_index("x"), lax.axis_index("y"), lax.axis_index("c")
    chips = [(1 - x, y), (x, 1 - y), (1 - x, 1 - y)]
    return x, y, c, chips


def _all_gather(shards, name, dep=None):
    n = len(shards)
    deps = [] if dep is None else list(dep)

    def body(*refs):
        ins, outs = refs[:n], refs[n + len(deps):2 * n + len(deps)]
        send_sems, recv_sems, local_sems = refs[2 * n + len(deps):]
        x, y, c, chips = _place()
        me, sibling = (x, y, c), (x, y, 1 - c)

        def slot(w, p):
            return outs[w].at[4 * p[0] + 2 * p[1] + p[2]]

        def copy(w, k, block, to, src=None):
            return pltpu.make_async_remote_copy(
                src_ref=slot(w, block) if src is None else src, dst_ref=slot(w, block),
                send_sem=send_sems.at[w, k], recv_sem=recv_sems.at[w, k], device_id=to, device_id_type=MESH)

        first = []
        for w in range(n):
            first += [copy(w, 1 + j, me, (*chip, c), src=ins[w]) for j, chip in enumerate(chips)]
            first.append(copy(w, 0, me, sibling, src=ins[w]))
        for cp in first:
            cp.start()
        mine = [pltpu.make_async_copy(ins[w], slot(w, me), local_sems.at[w]) for w in range(n)]
        for cp in mine:
            cp.start()
        passed = []
        for w in range(n):
            for j, chip in enumerate(chips):
                copy(w, 1 + j, (*chip, c), me).wait_recv()
                cp = copy(w, 4 + j, (*chip, c), sibling)
                cp.start()
                passed.append(cp)
        for w in range(n):
            copy(w, 0, sibling, me).wait_recv()
            for j, chip in enumerate(chips):
                copy(w, 4 + j, (*chip, 1 - c), me).wait_recv()
        for cp in first + passed:
            cp.wait_send()
        for cp in mine:
            cp.wait()

    return pl.pallas_call(
        body, name=name,
        in_specs=[ANY] * (n + len(deps)), out_specs=[ANY] * n,
        out_shape=[jax.ShapeDtypeStruct((N_DEV,) + a.shape, a.dtype) for a in shards],
        scratch_shapes=[pltpu.SemaphoreType.DMA((n, 7)), pltpu.SemaphoreType.DMA((n, 7)),
                        pltpu.SemaphoreType.DMA((n,))],
    )(*shards, *deps)


HBM = pl.BlockSpec(memory_space=pltpu.HBM)
SEM = pl.BlockSpec(memory_space=pltpu.SEMAPHORE)
EFFECT = pltpu.SideEffectType.DATAFLOW_SIDE_EFFECTING
PEERS = [(dx, dy, dc) for dx in (1, 0) for dy in (1, 0) for dc in (0, 1) if (dx, dy, dc) != (0, 0, 0)]


def _peer(x, y, c, flip):
    dx, dy, dc = flip
    return (1 - x if dx else x, 1 - y if dy else y, 1 - c if dc else c)


def _exchange_copies(srcs, lands, send, recv, loc, gather):
    x, y, c, _ = _place()
    me = 4 * x + 2 * y + c
    remote, local = [], []
    for w in range(len(srcs)):
        for k, flip in enumerate(PEERS):
            px, py, pc = _peer(x, y, c, flip)
            src = srcs[w] if gather else srcs[w].at[4 * px + 2 * py + pc]
            remote.append(pltpu.make_async_remote_copy(
                src_ref=src, dst_ref=lands[w].at[me], send_sem=send[w].at[k], recv_sem=recv[w].at[k],
                device_id=(px, py, pc), device_id_type=MESH))
        local.append(pltpu.make_async_copy(srcs[w] if gather else srcs[w].at[me], lands[w].at[me], loc[w]))
    return remote, local


class _Exchange:
    def __init__(self, srcs, lands, send, recv, loc, token, gather):
        self.srcs, self.lands, self.send, self.recv, self.loc = srcs, lands, send, recv, loc
        self.token, self.gather = token, gather


def _exchange_start(srcs, gather, name):
    n = len(srcs)
    land_shapes = [((N_DEV,) + a.shape) if gather else a.shape for a in srcs]
    lands = [pltpu.with_memory_space_constraint(lax.empty(sh, a.dtype), pltpu.HBM) for sh, a in zip(land_shapes, srcs)]
    srcs = [pltpu.with_memory_space_constraint(a, pltpu.HBM) for a in srcs]

    def body(*refs):
        src_refs, land_refs = refs[:n], refs[n:2 * n]
        outs = refs[2 * n:]
        send, recv, loc = outs[:n], outs[n:2 * n], outs[2 * n:3 * n]
        token = outs[-1]
        remote, local = _exchange_copies(src_refs, land_refs, send, recv, loc, gather)
        for cp in remote + local:
            cp.start()
        token[...] = jnp.zeros_like(token)

    out_shape = ([pltpu.SemaphoreType.DMA((len(PEERS),))] * (2 * n) + [pltpu.SemaphoreType.DMA(())] * n
                 + [pltpu.HBM(a.shape, a.dtype) for a in srcs] + [pltpu.HBM(a.shape, a.dtype) for a in lands]
                 + [jax.ShapeDtypeStruct((SUB, LANE), F32)])
    res = pl.pallas_call(
        body, name=name, out_shape=out_shape,
        in_specs=[HBM] * (2 * n),
        out_specs=[SEM] * (3 * n) + [HBM] * (2 * n) + [pl.BlockSpec(memory_space=pltpu.VMEM)],
        input_output_aliases={i: 3 * n + i for i in range(2 * n)},
        compiler_params=pltpu.CompilerParams(has_side_effects=EFFECT),
    )(*srcs, *lands)
    return _Exchange(res[3 * n:4 * n], res[4 * n:5 * n], res[:n], res[n:2 * n], res[2 * n:3 * n], res[-1], gather)


def _exchange_wait(ex, idxs, after, name):
    n = len(idxs)
    srcs = [ex.srcs[i] for i in idxs]
    lands = [ex.lands[i] for i in idxs]
    sems = [ex.send[i] for i in idxs] + [ex.recv[i] for i in idxs] + [ex.loc[i] for i in idxs]
    gather = ex.gather

    def body(*refs):
        src_refs, land_refs = refs[:n], refs[n:2 * n]
        send, recv, loc = refs[2 * n:3 * n], refs[3 * n:4 * n], refs[4 * n:5 * n]
        remote, local = _exchange_copies(src_refs, land_refs, send, recv, loc, gather)
        for cp in remote:
            cp.wait_send()
            cp.wait_recv()
        for cp in local:
            cp.wait()

    res = pl.pallas_call(
        body, name=name,
        out_shape=[pltpu.HBM(a.shape, a.dtype) for a in srcs] + [pltpu.HBM(a.shape, a.dtype) for a in lands],
        in_specs=[HBM] * (2 * n) + [SEM] * (3 * n) + [ANY],
        out_specs=[HBM] * (2 * n),
        input_output_aliases={i: i for i in range(2 * n)},
        compiler_params=pltpu.CompilerParams(has_side_effects=EFFECT),
    )(*srcs, *lands, *sems, after)
    return res[n:]


def _after(token, a):
    return a + token[0:1, 0:1].astype(a.dtype)


def _unblock(w3):
    nb, k, nbw = w3.shape
    return w3.transpose(1, 0, 2).reshape(k, nb * nbw)


def _block(w, nb):
    k, n = w.shape
    return w.reshape(k, nb, n // nb).transpose(1, 0, 2)


def kernel(x, positions, ln1_g, w_in, b_gate, conv_w, w_conv_out, q_a_g, w_q_b, kv_a_g, w_kv_b, q_norm_g, k_norm_g, w_mla_out, w_o, ln2_g, w_ffn_up, ffn_conv_w, ffn_conv_b, w_ffn_down, loss_target, m_ln1_g, m_w_in, m_b_gate, m_conv_w, m_w_conv_out, m_q_a_g, m_w_q_b, m_kv_a_g, m_w_kv_b, m_q_norm_g, m_k_norm_g, m_w_mla_out, m_w_o, m_ln2_g, m_w_ffn_up, m_ffn_conv_w, m_ffn_conv_b, m_w_ffn_down, v_ln1_g, v_w_in, v_b_gate, v_conv_w, v_w_conv_out, v_q_a_g, v_w_q_b, v_kv_a_g, v_w_kv_b, v_q_norm_g, v_k_norm_g, v_w_mla_out, v_w_o, v_ln2_g, v_w_ffn_up, v_ffn_conv_w, v_ffn_conv_b, v_w_ffn_down):
    s, d = x.shape[1], x.shape[2]
    conv = conv_w.shape[2] * N_DEV
    ql, kvl = q_a_g.shape[1], kv_a_g.shape[1]
    heads = w_q_b.shape[2] * N_DEV // HEAD_QK
    dff = w_ffn_down.shape[1] * N_DEV
    hw = heads * LANE
    conv3 = 3 * conv
    kr_off = conv3 + ql
    kv_off = -(-(kr_off + LANE) // kvl) * kvl
    wa = kv_off + kvl
    assert conv3 % ql == 0 and kr_off % LANE == 0
    xs = x[0]
    tgt = loss_target[0]
    pos = positions.reshape(s, 1)

    big = dict(w_in=w_in[0], w_conv_out=w_conv_out[0], w_q_b=w_q_b[0], w_kv_b=w_kv_b[0],
               w_mla_out=w_mla_out[0], w_o=w_o[0], w_ffn_up=w_ffn_up[0], w_ffn_down=w_ffn_down[0])
    names = list(big)
    rest = names[1:]
    first = _all_gather([big["w_in"].astype(BF), _pad8(conv_w[0]), _pad8(ffn_conv_w[0])], "gather_w_in")
    cw8 = _unblock(first[1])
    fcw8 = _unblock(first[2])
    ag = _exchange_start([big[k].astype(BF) for k in rest], True, "gather_rest_start")

    def landed(keys, after, name):
        return _exchange_wait(ag, [rest.index(k) for k in keys], after, name)

    w_in_full = _unblock(first[0])
    zpad = jnp.zeros((d, kv_off - kr_off - LANE), BF)
    w_a = jnp.concatenate([w_in_full[:, :kr_off], _lay(w_in_full[:, kr_off + kvl:kr_off + kvl + ROPE]), zpad,
                           w_in_full[:, kr_off:kr_off + kvl]], axis=1)[None]
    g_off = kr_off + kvl + ROPE
    w_g = _block(w_in_full[:, g_off:], 2)
    gains = _pad8(jnp.concatenate([q_norm_g[:, :NOPE], _lay(q_norm_g[:, NOPE:]),
                                   k_norm_g[:, :NOPE], _lay(k_norm_g[:, NOPE:])], axis=0))
    kr_blk = kr_off // LANE

    cos, sin = _rope_tables(pos)
    u1 = _rms_fwd(xs, _after(ag.token, ln1_g), d, 0, "rms1_fwd")
    z_a = _mm_nn(u1, w_a, "mm_z_a")
    z_g = _mm_nn(u1, w_g, "mm_z_g")
    p = _conv_mix_fwd(z_a, cw8, conv)
    w_co, w_qb, w_kv = landed(["w_conv_out", "w_q_b", "w_kv_b"], p, "gather_wait_mixers")
    wq_full = _unblock(w_qb).reshape(ql, heads, HEAD_QK)
    w_q = jnp.concatenate([wq_full[:, :, :NOPE].reshape(ql, hw), _lay(wq_full[:, :, NOPE:]).reshape(ql, hw)],
                          axis=1)[None]
    yc = _mm_nn(p, w_co, "mm_y_conv")
    qn = _rms_fwd(z_a, q_a_g, ql, conv3 // ql, "rms_q_fwd")
    kvn = _rms_fwd(z_a, kv_a_g, kvl, kv_off // kvl, "rms_kv_fwd")
    q_raw = _mm_nn(qn, w_q, "mm_q")
    kv_raw = _mm_nn(kvn, w_kv, "mm_kv")
    q_att, k_att, v_bf = _head_fwd(q_raw, kv_raw, z_a, kr_blk, cos, sin, gains, heads)
    o, o_bf, lse = _attn_fwd(q_att, k_att, v_bf, heads)
    w_mo, w_oo, w_up, w_dn = landed(["w_mla_out", "w_o", "w_ffn_up", "w_ffn_down"], lse, "gather_wait_outs")
    w_mo = w_mo.reshape(1, hw, d)
    w_oo = w_oo.reshape(1, d, d)
    w_dn = w_dn.reshape(1, dff, d)
    ym = _mm_nn(o_bf, w_mo, "mm_y_mla")
    mix = _gate_fwd(z_g, b_gate, yc, ym, d)
    h1 = _mm_nn(mix, w_oo, "mm_h1", add=xs)
    u2 = _rms_fwd(h1, ln2_g, d, 0, "rms2_fwd")
    a_pre = _mm_nn(u2, w_up, "mm_ffn_up")
    f = _ffn_act_fwd(a_pre, fcw8, ffn_conv_b, dff)
    y = _mm_nn(f, w_dn, "mm_ffn_down", add=h1)
    dy, dy_bf, loss_part = _loss_head(y, tgt)

    g_dn = _mm_tn(f, dy_bf, 1, "mm_g_ffn_down").reshape(N_DEV, dff // N_DEV, d)
    rs_dn = _exchange_start([g_dn], False, "reduce_ffn_down_start")
    d_f = _mm_nt(dy_bf, w_dn, "mm_d_f", dep=rs_dn.token)
    d_xg, d_xu, dfw_g, dfw_u = _ffn_act_bwd(a_pre, d_f, fcw8, ffn_conv_b, dff)
    half = N_DEV // 2
    g_up = jnp.concatenate([_mm_tn(u2, d_xg, half, "mm_g_ffn_up_gate"), _mm_tn(u2, d_xu, half, "mm_g_ffn_up_up")], axis=0)
    rs_up = _exchange_start([g_up], False, "reduce_ffn_up_start")
    d_u2 = _mm_nt(d_xg, w_up, "mm_d_u2_gate", blk0=0, nblk=half, dep=rs_up.token)
    d_u2 = _mm_nt(d_xu, w_up, "mm_d_u2_up", blk0=half, nblk=half, add=d_u2)
    d_h1, d_h1_bf, dg_ln2 = _rms_bwd(h1, d_u2, ln2_g, d, 0, "rms2_bwd", extra=dy, also_bf16=True)
    g_oo = _mm_tn(mix, d_h1_bf, 1, "mm_g_w_o").reshape(N_DEV, d // N_DEV, d)
    d_mix = _mm_nt(d_h1_bf, w_oo, "mm_d_mix")
    d_zga, d_zgb, d_yc, d_ym, dba, dbb = _gate_bwd(d_mix, z_g, b_gate, yc, ym, d)
    g_co = _mm_tn(p, d_yc, N_DEV, "mm_g_conv_out")
    g_mo = _mm_tn(o_bf, d_ym, 1, "mm_g_mla_out").reshape(N_DEV, hw // N_DEV, d)
    rs_mix = _exchange_start([g_oo, g_co, g_mo], False, "reduce_mixers_start")
    d_p = _mm_nt(d_yc, w_co, "mm_d_p", dep=rs_mix.token)
    d_o = _mm_nt(d_ym, w_mo, "mm_d_o", out_dtype=BF)
    d_zb, d_zc, d_zv, dcw = _conv_mix_bwd(z_a, d_p, cw8, conv)
    dq_att, dk_att, dv = _attn_bwd(q_att, k_att, v_bf, o, lse, d_o, heads, dep=rs_mix.token)
    d_q_raw, d_kv_raw, d_kr, dgains = _head_bwd(q_raw, kv_raw, z_a, kr_blk, cos, sin, gains, dq_att, dk_att, dv, heads)
    g_q2 = _mm_tn(qn, d_q_raw, 1, "mm_g_q")[0]
    g_qb = _block(jnp.concatenate([g_q2[:, :hw].reshape(ql, heads, NOPE),
                                   _unlay(g_q2[:, hw:].reshape(ql, heads, LANE))], axis=2).reshape(ql, heads * HEAD_QK), N_DEV)
    g_kv = _mm_tn(kvn, d_kv_raw, N_DEV, "mm_g_kv")
    rs_qkv = _exchange_start([g_qb, g_kv], False, "reduce_qkv_start")
    d_qn = _mm_nt(d_q_raw, w_q, "mm_d_qn", dep=rs_qkv.token)
    d_kvn = _mm_nt(d_kv_raw, w_kv, "mm_d_kvn")
    d_ql, dg_qa = _rms_bwd(z_a, d_qn, q_a_g, ql, conv3 // ql, "rms_q_bwd", out_dtype=BF)
    d_kvl, dg_kva = _rms_bwd(z_a, d_kvn, kv_a_g, kvl, kv_off // kvl, "rms_kv_bwd", out_dtype=BF)
    d_z_a = jnp.concatenate([d_zb, d_zc, d_zv, d_ql, d_kr.astype(BF), jnp.zeros((s, kv_off - kr_off - LANE), BF),
                             d_kvl], axis=1)
    g_a = _mm_tn(u1, d_z_a, 1, "mm_g_w_a")[0]
    g_ga = _mm_tn(u1, d_zga, 1, "mm_g_w_ga")[0]
    g_gb = _mm_tn(u1, d_zgb, 1, "mm_g_w_gb")[0]
    g_in = _block(jnp.concatenate([g_a[:, :kr_off], g_a[:, kv_off:kv_off + kvl],
                                   _unlay(g_a[:, kr_off:kr_off + LANE]), g_ga, g_gb], axis=1), N_DEV)
    rs_in = _exchange_start([g_in], False, "reduce_w_in_start")
    d_u1 = _mm_nt(d_z_a, w_a, "mm_d_u1_a", dep=rs_in.token)
    d_u1 = _mm_nt(d_zga, w_g, "mm_d_u1_ga", blk0=0, nblk=1, add=d_u1)
    d_u1 = _mm_nt(d_zgb, w_g, "mm_d_u1_gb", blk0=1, nblk=1, add=d_u1)
    grad_x, dg_ln1 = _rms_bwd(xs, d_u1, ln1_g, d, 0, "rms1_bwd", extra=d_h1)

    summed = {}
    summed["w_ffn_down"], = _exchange_wait(rs_dn, [0], grad_x, "reduce_ffn_down_wait")
    summed["w_ffn_up"], = _exchange_wait(rs_up, [0], grad_x, "reduce_ffn_up_wait")
    summed["w_o"], summed["w_conv_out"], summed["w_mla_out"] = _exchange_wait(rs_mix, [0, 1, 2], grad_x, "reduce_mixers_wait")
    summed["w_q_b"], summed["w_kv_b"] = _exchange_wait(rs_qkv, [0, 1], grad_x, "reduce_qkv_wait")
    loc = locals()
    out = {}
    for k in rest:
        out[k] = _adamw(summed[k], big[k], loc["m_" + k][0], loc["v_" + k][0], "adamw_" + k)

    small = dict(ln1_g=dg_ln1[0:1], b_gate=jnp.concatenate([dba[0:1], dbb[0:1]], axis=1), q_a_g=dg_qa[0:1],
                 kv_a_g=dg_kva[0:1],
                 q_norm_g=jnp.concatenate([dgains[0:1], _unlay(dgains[1:2])], axis=1),
                 k_norm_g=jnp.concatenate([dgains[2:3], _unlay(dgains[3:4])], axis=1),
                 ln2_g=dg_ln2[0:1], ffn_conv_b=jnp.concatenate([dfw_g[3:4], dfw_u[3:4]], axis=1))
    small_names = list(small)
    extra = [dcw[0:3].reshape(1, -1), jnp.concatenate([dfw_g[0:3], dfw_u[0:3]], axis=1).reshape(1, -1),
             loss_part[0:1, 0:1]]
    flat = jnp.concatenate([small[k] for k in small_names] + extra, axis=1)
    n_flat = flat.shape[1]
    rows = -(-n_flat // (SUB * LANE)) * SUB
    flat = jnp.pad(flat, ((0, 0), (0, rows * LANE - n_flat))).reshape(rows, LANE)
    total = _sum_parts(_all_gather([flat], "gather_small", dep=[out[k][0] for k in rest])[0], "sum_small").reshape(1, rows * LANE)
    off = 0
    small_g = {}
    for k in small_names:
        small_g[k] = total[:, off:off + small[k].shape[1]]
        off += small[k].shape[1]
    me = 4 * lax.axis_index("x") + 2 * lax.axis_index("y") + lax.axis_index("c")
    cwn, fcwn = conv // N_DEV, 2 * dff // N_DEV
    g_cw = lax.dynamic_slice_in_dim(total[:, off:off + 3 * conv].reshape(3, conv), me * cwn, cwn, axis=1)
    off += 3 * conv
    g_fcw = lax.dynamic_slice_in_dim(total[:, off:off + 6 * dff].reshape(3, 2 * dff), me * fcwn, fcwn, axis=1)
    off += 6 * dff
    loss = total[0, off]

    summed["w_in"], = _exchange_wait(rs_in, [0], total, "reduce_w_in_wait")
    out["w_in"] = _adamw(summed["w_in"], big["w_in"], m_w_in[0], v_w_in[0], "adamw_w_in")
    small_w = dict(ln1_g=ln1_g, b_gate=b_gate, q_a_g=q_a_g, kv_a_g=kv_a_g, q_norm_g=q_norm_g, k_norm_g=k_norm_g,
                   ln2_g=ln2_g, ffn_conv_b=ffn_conv_b, conv_w=conv_w[0].reshape(1, -1),
                   ffn_conv_w=ffn_conv_w[0].reshape(1, -1))
    small_g["conv_w"] = g_cw.reshape(1, -1)
    small_g["ffn_conv_w"] = g_fcw.reshape(1, -1)
    packed_names = list(small_w)

    def pack(get):
        vflat = jnp.concatenate([get(k).reshape(1, -1) for k in packed_names], axis=1)
        nr = -(-vflat.shape[1] // (SUB * LANE)) * SUB
        return jnp.pad(vflat, ((0, 0), (0, nr * LANE - vflat.shape[1])), constant_values=1.0).reshape(nr, LANE)

    res = _adamw(pack(lambda k: small_g[k])[None], pack(lambda k: small_w[k]), pack(lambda k: loc["m_" + k]),
                 pack(lambda k: loc["v_" + k]), "adamw_small")
    res = [r.reshape(1, -1) for r in res]
    off = 0
    for k in packed_names:
        shape = loc[k].shape
        size = small_w[k].shape[1]
        out[k] = [r[:, off:off + size].reshape(shape) for r in res]
        off += size
    for k in names:
        out[k] = [r[None] for r in out[k]]

    order = ["ln1_g", "w_in", "b_gate", "conv_w", "w_conv_out", "q_a_g", "w_q_b", "kv_a_g", "w_kv_b", "q_norm_g",
             "k_norm_g", "w_mla_out", "w_o", "ln2_g", "w_ffn_up", "ffn_conv_w", "ffn_conv_b", "w_ffn_down"]
    return (loss, grad_x[None], *[out[k][0] for k in order], *[out[k][1] for k in order],
            *[out[k][2] for k in order], *[out[k][3] for k in order])
```

```python
import functools

import jax
import jax.numpy as jnp
from jax import lax
from jax.experimental import pallas as pl
from jax.experimental.pallas import tpu as pltpu

BF = jnp.bfloat16
F32 = jnp.float32
MESH = pl.DeviceIdType.MESH
N_DEV = 8

NOPE = 128
ROPE = 64
HALF = ROPE // 2
HEAD_QK = NOPE + ROPE
HEAD_V = 128
LANE = 128
SUB = 8
NORM_EPS = 1e-6
NEG_INF = -1e30
ROPE_THETA = 10000.0
ADAM_LR = 0.001
ADAM_B1 = 0.9
ADAM_B2 = 0.999
ADAM_EPS = 1e-08
ADAM_WD = 0.01
ADAM_STEP = 10

VMEM_LIMIT = 52 * 1024 * 1024
MM_TM, MM_TN, MM_TK, MM_TS = 1024, 1536, 2048, 1024
ROW_TILE, ROW_TILE_BWD = 512, 256
HEAD_ROW_TILE, HEAD_ROW_TILE_BWD = 256, 128
COL_TILE = 512
ATTN_TILE = 1024
ATTN_TILE_FWD = 1024
ANY = pl.BlockSpec(memory_space=pl.ANY)


def _pick(n, target, mult):
    t = (min(n, target) // mult) * mult
    while t > 0:
        if n % t == 0:
            return t
        t -= mult
    raise ValueError(f"no tile for {n} (target {target}, multiple {mult})")


def _cp(*sem):
    return pltpu.CompilerParams(dimension_semantics=sem, vmem_limit_bytes=VMEM_LIMIT)


def _accumulate(kk, nk, acc, part, finish):
    if nk == 1:
        finish(part())
        return

    @pl.when(kk == 0)
    def _():
        acc[...] = part()

    @pl.when((kk > 0) & (kk < nk - 1))
    def _():
        acc[...] += part()

    @pl.when(kk == nk - 1)
    def _():
        finish(acc[...] + part())


def _mm_call(body, name, grid, in_specs, args, out_spec, out_shape, acc_shape, nk, dep):
    if dep is not None:
        in_specs = in_specs + [ANY]
        args = args + [dep]
    return pl.pallas_call(
        body, name=name, grid=grid, in_specs=in_specs, out_specs=out_spec, out_shape=out_shape,
        scratch_shapes=[pltpu.VMEM(acc_shape, F32)] if nk > 1 else [],
        compiler_params=_cp("parallel", "parallel", "arbitrary"),
    )(*args)


def _mm_nn(a, b3, name, add=None, out_dtype=F32, blk0=0, nblk=None, dep=None):
    m, k = a.shape
    nb_all, k2, nbw = b3.shape
    assert k == k2
    nblk = nb_all - blk0 if nblk is None else nblk
    n = nblk * nbw
    tm = _pick(m, MM_TM, 16)
    tn = _pick(nbw, MM_TN, LANE)
    tk = _pick(k, MM_TK, LANE)
    per = nbw // tn
    nk = k // tk

    def body(*refs):
        a_ref, b_ref = refs[:2]
        c_ref = refs[2] if add is not None else None
        o_ref = refs[2 + (add is not None) + (dep is not None)]
        acc = refs[-1]

        def part():
            return jnp.dot(a_ref[...].astype(BF), b_ref[...].astype(BF), preferred_element_type=F32)

        def finish(r):
            if add is not None:
                r = r + c_ref[...]
            o_ref[...] = r.astype(out_dtype)

        _accumulate(pl.program_id(2), nk, acc, part, finish)

    in_specs = [pl.BlockSpec((tm, tk), lambda i, j, kk: (i, kk)),
                pl.BlockSpec((None, tk, tn), lambda i, j, kk: (blk0 + j // per, kk, j % per))]
    args = [a, b3]
    if add is not None:
        in_specs.append(pl.BlockSpec((tm, tn), lambda i, j, kk: (i, j)))
        args.append(add)
    return _mm_call(body, name, (m // tm, n // tn, nk), in_specs, args,
                    pl.BlockSpec((tm, tn), lambda i, j, kk: (i, j)), jax.ShapeDtypeStruct((m, n), out_dtype),
                    (tm, tn), nk, dep)


def _mm_nt(a, b3, name, add=None, out_dtype=F32, blk0=0, nblk=None, dep=None):
    m, n = a.shape
    nb_all, k, nbw = b3.shape
    nblk = nb_all - blk0 if nblk is None else nblk
    assert n == nblk * nbw
    tm = _pick(m, MM_TM, 16)
    tn = _pick(k, MM_TN, LANE)
    tk = _pick(nbw, MM_TK, LANE)
    per = nbw // tk
    nk = n // tk

    def body(*refs):
        a_ref, b_ref = refs[:2]
        c_ref = refs[2] if add is not None else None
        o_ref = refs[2 + (add is not None) + (dep is not None)]
        acc = refs[-1]

        def part():
            return lax.dot_general(a_ref[...].astype(BF), b_ref[...].astype(BF),
                                   (((1,), (1,)), ((), ())), preferred_element_type=F32)

        def finish(r):
            if add is not None:
                r = r + c_ref[...]
            o_ref[...] = r.astype(out_dtype)

        _accumulate(pl.program_id(2), nk, acc, part, finish)

    in_specs = [pl.BlockSpec((tm, tk), lambda i, j, kk: (i, kk)),
                pl.BlockSpec((None, tn, tk), lambda i, j, kk: (blk0 + kk // per, j, kk % per))]
    args = [a, b3]
    if add is not None:
        in_specs.append(pl.BlockSpec((tm, tn), lambda i, j, kk: (i, j)))
        args.append(add)
    return _mm_call(body, name, (m // tm, k // tn, nk), in_specs, args,
                    pl.BlockSpec((tm, tn), lambda i, j, kk: (i, j)), jax.ShapeDtypeStruct((m, k), out_dtype),
                    (tm, tn), nk, dep)


def _mm_tn(a, b, nblk, name, out_dtype=BF, dep=None, into=None, blk0=0):
    s, m = a.shape
    s2, n = b.shape
    assert s == s2 and n % nblk == 0 and (dep is None or into is None)
    nbw = n // nblk
    tm = _pick(m, MM_TN, LANE)
    tn = _pick(nbw, MM_TN, LANE)
    ts = _pick(s, MM_TS, LANE)
    per = nbw // tn
    ns = s // ts

    def body(*refs):
        a_ref, b_ref = refs[:2]
        o_ref = refs[2 + (dep is not None or into is not None)]
        acc = refs[-1]

        def part():
            return lax.dot_general(a_ref[...].astype(BF), b_ref[...].astype(BF),
                                   (((0,), (0,)), ((), ())), preferred_element_type=F32)

        def finish(r):
            o_ref[...] = r.astype(out_dtype)

        _accumulate(pl.program_id(2), ns, acc, part, finish)

    in_specs = [pl.BlockSpec((ts, tm), lambda i, j, ss: (ss, i)),
                pl.BlockSpec((ts, tn), lambda i, j, ss: (ss, j))]
    out_spec = pl.BlockSpec((None, tm, tn), lambda i, j, ss: (blk0 + j // per, i, j % per))
    if into is None:
        return _mm_call(body, name, (m // tm, n // tn, ns), in_specs, [a, b], out_spec,
                        jax.ShapeDtypeStruct((nblk, m, nbw), out_dtype), (tm, tn), ns, dep)
    assert into.shape[1:] == (m, nbw) and into.dtype == out_dtype
    return pl.pallas_call(
        body, name=name, grid=(m // tm, n // tn, ns), in_specs=in_specs + [ANY], out_specs=out_spec,
        out_shape=jax.ShapeDtypeStruct(into.shape, out_dtype), input_output_aliases={2: 0},
        scratch_shapes=[pltpu.VMEM((tm, tn), F32)] if ns > 1 else [],
        compiler_params=_cp("parallel", "parallel", "arbitrary"),
    )(a, b, into)


def _rows8(rows, width):
    idx = lax.broadcasted_iota(jnp.int32, (SUB, width), 0)
    out = jnp.zeros((SUB, width), F32)
    for r, v in enumerate(rows):
        out = jnp.where(idx == r, v, out)
    return out


def _rms_fwd(x, g, width, col_blk, name):
    s = x.shape[0]
    tr = _pick(s, ROW_TILE, 16)

    def body(x_ref, g_ref, u_ref):
        xv = x_ref[...]
        r = lax.rsqrt(jnp.mean(xv * xv, axis=-1, keepdims=True) + NORM_EPS)
        u_ref[...] = ((xv * r) * g_ref[...]).astype(BF)

    return pl.pallas_call(
        body, name=name, grid=(s // tr,),
        in_specs=[pl.BlockSpec((tr, width), lambda i: (i, col_blk)),
                  pl.BlockSpec((1, width), lambda i: (0, 0))],
        out_specs=pl.BlockSpec((tr, width), lambda i: (i, 0)),
        out_shape=jax.ShapeDtypeStruct((s, width), BF),
        compiler_params=_cp("parallel"),
    )(x, g)


def _rms_bwd(x, du, g, width, col_blk, name, extra=None, out_dtype=F32, also_bf16=False):
    s = x.shape[0]
    tr = _pick(s, ROW_TILE_BWD, 16)

    def body(*refs):
        x_ref, du_ref, g_ref = refs[:3]
        e_ref = refs[3] if extra is not None else None
        dx_ref = refs[3 + (extra is not None)]
        dxb_ref = refs[4 + (extra is not None)] if also_bf16 else None
        dg_ref = refs[-1]
        i = pl.program_id(0)
        xv = x_ref[...]
        duv = du_ref[...].astype(F32)
        r = lax.rsqrt(jnp.mean(xv * xv, axis=-1, keepdims=True) + NORM_EPS)
        nv = xv * r
        dn = duv * g_ref[...]
        dx = r * (dn - nv * jnp.mean(dn * nv, axis=-1, keepdims=True))
        if extra is not None:
            dx = dx + e_ref[...]
        dx_ref[...] = dx.astype(out_dtype)
        if also_bf16:
            dxb_ref[...] = dx.astype(BF)

        @pl.when(i == 0)
        def _():
            dg_ref[...] = jnp.zeros_like(dg_ref)

        dg_ref[...] += _rows8([jnp.sum(duv * nv, axis=0, keepdims=True)], width)

    in_specs = [pl.BlockSpec((tr, width), lambda i: (i, col_blk)),
                pl.BlockSpec((tr, width), lambda i: (i, 0)),
                pl.BlockSpec((1, width), lambda i: (0, 0))]
    args = [x, du, g]
    if extra is not None:
        in_specs.append(pl.BlockSpec((tr, width), lambda i: (i, 0)))
        args.append(extra)
    return pl.pallas_call(
        body, name=name, grid=(s // tr,),
        in_specs=in_specs,
        out_specs=[pl.BlockSpec((tr, width), lambda i: (i, 0))] * (1 + also_bf16)
        + [pl.BlockSpec((SUB, width), lambda i: (0, 0))],
        out_shape=[jax.ShapeDtypeStruct((s, width), out_dtype)] + [jax.ShapeDtypeStruct((s, width), BF)] * also_bf16
        + [jax.ShapeDtypeStruct((SUB, width), F32)],
        compiler_params=_cp("arbitrary"),
    )(*args)


def _down(cur, prev8, k):
    ext = jnp.concatenate([prev8, cur], axis=0)
    return pltpu.roll(ext, k, axis=0)[SUB:]


def _up(cur, next8, k):
    ext = jnp.concatenate([cur, next8], axis=0)
    return pltpu.roll(ext, ext.shape[0] - k, axis=0)[:cur.shape[0]]


def _lags(cur, prev8):
    return _down(cur, prev8, 1), _down(cur, prev8, 2)


def _conv3(w_ref, cur, prev8, lags=None):
    lag1, lag2 = _lags(cur, prev8) if lags is None else lags
    return w_ref[0:1, :] * lag2 + w_ref[1:2, :] * lag1 + w_ref[2:3, :] * cur


def _conv3_t(w_ref, cur, next8):
    return w_ref[2:3, :] * cur + w_ref[1:2, :] * _up(cur, next8, 1) + w_ref[0:1, :] * _up(cur, next8, 2)


def _spec_cur(tr, tc, c0):
    return pl.BlockSpec((tr, tc), lambda j, i: (i, c0 + j))


def _spec_prev(tr, tc, c0):
    return pl.BlockSpec((SUB, tc), lambda j, i: (jnp.maximum(i * (tr // SUB) - 1, 0), c0 + j))


def _spec_next(tr, tc, c0, s):
    return pl.BlockSpec((SUB, tc), lambda j, i: (jnp.minimum((i + 1) * (tr // SUB), s // SUB - 1), c0 + j))


def _spec_w(tc, c0):
    return pl.BlockSpec((SUB, tc), lambda j, i: (0, c0 + j))


def _pad8(w):
    return jnp.pad(w, ((0, SUB - w.shape[0]), (0, 0)))


def _conv_mix_fwd(z_a, cw8, conv):
    s = z_a.shape[0]
    tr = _pick(s, ROW_TILE, 16)
    tc = _pick(conv, COL_TILE, LANE)
    nc = conv // tc

    def body(zb_ref, zc_ref, zv_ref, zcp_ref, zvp_ref, w_ref, p_ref):
        i = pl.program_id(1)
        cv = zc_ref[...] * zv_ref[...]
        cvp = jnp.where(i > 0, zcp_ref[...] * zvp_ref[...], 0.0)
        p_ref[...] = (zb_ref[...] * _conv3(w_ref, cv, cvp)).astype(BF)

    return pl.pallas_call(
        body, name="conv_mix_fwd", grid=(nc, s // tr),
        in_specs=[_spec_cur(tr, tc, 0), _spec_cur(tr, tc, nc), _spec_cur(tr, tc, 2 * nc),
                  _spec_prev(tr, tc, nc), _spec_prev(tr, tc, 2 * nc), _spec_w(tc, 0)],
        out_specs=_spec_cur(tr, tc, 0),
        out_shape=jax.ShapeDtypeStruct((s, conv), BF),
        compiler_params=_cp("parallel", "parallel"),
    )(z_a, z_a, z_a, z_a, z_a, cw8)


def _conv_mix_bwd(z_a, d_p, cw8, conv):
    s = z_a.shape[0]
    tr = _pick(s, ROW_TILE_BWD, 16)
    tc = _pick(conv, COL_TILE, LANE)
    nc = conv // tc
    nr = s // tr

    def body(zb_ref, zbn_ref, zc_ref, zcp_ref, zv_ref, zvp_ref, dp_ref, dpn_ref, w_ref,
             dzb_ref, dzc_ref, dzv_ref, dw_ref):
        i = pl.program_id(1)
        zc = zc_ref[...]
        zv = zv_ref[...]
        cv = zc * zv
        cvp = jnp.where(i > 0, zcp_ref[...] * zvp_ref[...], 0.0)
        cv1, cv2 = _lags(cv, cvp)
        dpv = dp_ref[...]
        dzb_ref[...] = (dpv * _conv3(w_ref, cv, cvp, (cv1, cv2))).astype(BF)
        dcc = dpv * zb_ref[...]
        dccn = jnp.where(i < nr - 1, dpn_ref[...] * zbn_ref[...], 0.0)
        dcv = _conv3_t(w_ref, dcc, dccn)
        dzc_ref[...] = (dcv * zv).astype(BF)
        dzv_ref[...] = (dcv * zc).astype(BF)

        @pl.when(i == 0)
        def _():
            dw_ref[...] = jnp.zeros_like(dw_ref)

        dw_ref[...] += _rows8([jnp.sum(dcc * cv2, axis=0, keepdims=True),
                               jnp.sum(dcc * cv1, axis=0, keepdims=True),
                               jnp.sum(dcc * cv, axis=0, keepdims=True)], tc)

    out = jax.ShapeDtypeStruct((s, conv), BF)
    return pl.pallas_call(
        body, name="conv_mix_bwd", grid=(nc, nr),
        in_specs=[_spec_cur(tr, tc, 0), _spec_next(tr, tc, 0, s),
                  _spec_cur(tr, tc, nc), _spec_prev(tr, tc, nc),
                  _spec_cur(tr, tc, 2 * nc), _spec_prev(tr, tc, 2 * nc),
                  _spec_cur(tr, tc, 0), _spec_next(tr, tc, 0, s), _spec_w(tc, 0)],
        out_specs=[_spec_cur(tr, tc, 0), _spec_cur(tr, tc, 0), _spec_cur(tr, tc, 0), _spec_w(tc, 0)],
        out_shape=[out, out, out, jax.ShapeDtypeStruct((SUB, conv), F32)],
        compiler_params=_cp("parallel", "arbitrary"),
    )(z_a, z_a, z_a, z_a, z_a, z_a, d_p, d_p, cw8)


def _silu_parts(ag):
    sg = jax.nn.sigmoid(ag)
    return ag * sg, sg


def _ffn_act_fwd(a_pre, cw8, cb, dff):
    s = a_pre.shape[0]
    tr = _pick(s, ROW_TILE, 16)
    tc = _pick(dff, COL_TILE, LANE)
    nc = dff // tc

    def body(xg_ref, xgp_ref, xu_ref, xup_ref, wg_ref, wu_ref, bg_ref, bu_ref, f_ref):
        i = pl.program_id(1)
        xgp = jnp.where(i > 0, xgp_ref[...], 0.0)
        xup = jnp.where(i > 0, xup_ref[...], 0.0)
        ag = _conv3(wg_ref, xg_ref[...], xgp) + bg_ref[...]
        au = _conv3(wu_ref, xu_ref[...], xup) + bu_ref[...]
        f_ref[...] = (_silu_parts(ag)[0] * au).astype(BF)

    return pl.pallas_call(
        body, name="ffn_act_fwd", grid=(nc, s // tr),
        in_specs=[_spec_cur(tr, tc, 0), _spec_prev(tr, tc, 0), _spec_cur(tr, tc, nc), _spec_prev(tr, tc, nc),
                  _spec_w(tc, 0), _spec_w(tc, nc),
                  pl.BlockSpec((1, tc), lambda j, i: (0, j)), pl.BlockSpec((1, tc), lambda j, i: (0, nc + j))],
        out_specs=_spec_cur(tr, tc, 0),
        out_shape=jax.ShapeDtypeStruct((s, dff), BF),
        compiler_params=_cp("parallel", "parallel"),
    )(a_pre, a_pre, a_pre, a_pre, cw8, cw8, cb, cb)


def _ffn_act_bwd(a_pre, d_f, cw8, cb, dff):
    s = a_pre.shape[0]
    tr = _pick(s, ROW_TILE_BWD, 16)
    tc = _pick(dff, COL_TILE, LANE)
    nc = dff // tc
    nr = s // tr

    def body(xg_ref, xgp_ref, xgn_ref, xu_ref, xup_ref, xun_ref, df_ref, dfn_ref,
             wg_ref, wu_ref, bg_ref, bu_ref, dxg_ref, dxu_ref, dwg_ref, dwu_ref):
        i = pl.program_id(1)
        xg = xg_ref[...]
        xu = xu_ref[...]
        xgp = jnp.where(i > 0, xgp_ref[...], 0.0)
        xup = jnp.where(i > 0, xup_ref[...], 0.0)

        def d_act(xg_t, xgp_t, xu_t, xup_t, df_t, lags_g=None, lags_u=None):
            ag = _conv3(wg_ref, xg_t, xgp_t, lags_g) + bg_ref[...]
            au = _conv3(wu_ref, xu_t, xup_t, lags_u) + bu_ref[...]
            sil, sg = _silu_parts(ag)
            return df_t * au * (sg * (1.0 + ag * (1.0 - sg))), df_t * sil

        lags_g = _lags(xg, xgp)
        lags_u = _lags(xu, xup)
        dag, dau = d_act(xg, xgp, xu, xup, df_ref[...], lags_g, lags_u)
        dfn = jnp.where(i < nr - 1, dfn_ref[...], 0.0)
        dagn, daun = d_act(xgn_ref[...], xg[tr - SUB:], xun_ref[...], xu[tr - SUB:], dfn)
        dxg_ref[...] = _conv3_t(wg_ref, dag, dagn).astype(BF)
        dxu_ref[...] = _conv3_t(wu_ref, dau, daun).astype(BF)

        @pl.when(i == 0)
        def _():
            dwg_ref[...] = jnp.zeros_like(dwg_ref)
            dwu_ref[...] = jnp.zeros_like(dwu_ref)

        def wgrad(da, x, lags):
            return _rows8([jnp.sum(da * lags[1], axis=0, keepdims=True),
                           jnp.sum(da * lags[0], axis=0, keepdims=True),
                           jnp.sum(da * x, axis=0, keepdims=True),
                           jnp.sum(da, axis=0, keepdims=True)], tc)

        dwg_ref[...] += wgrad(dag, xg, lags_g)
        dwu_ref[...] += wgrad(dau, xu, lags_u)

    half = jax.ShapeDtypeStruct((s, dff), BF)
    wsh = jax.ShapeDtypeStruct((SUB, dff), F32)
    return pl.pallas_call(
        body, name="ffn_act_bwd", grid=(nc, nr),
        in_specs=[_spec_cur(tr, tc, 0), _spec_prev(tr, tc, 0), _spec_next(tr, tc, 0, s),
                  _spec_cur(tr, tc, nc), _spec_prev(tr, tc, nc), _spec_next(tr, tc, nc, s),
                  _spec_cur(tr, tc, 0), _spec_next(tr, tc, 0, s),
                  _spec_w(tc, 0), _spec_w(tc, nc),
                  pl.BlockSpec((1, tc), lambda j, i: (0, j)), pl.BlockSpec((1, tc), lambda j, i: (0, nc + j))],
        out_specs=[_spec_cur(tr, tc, 0), _spec_cur(tr, tc, 0), _spec_w(tc, 0), _spec_w(tc, 0)],
        out_shape=[half, half, wsh, wsh],
        compiler_params=_cp("parallel", "arbitrary"),
    )(a_pre, a_pre, a_pre, a_pre, a_pre, a_pre, d_f, d_f, cw8, cw8, cb, cb)


def _gate_fwd(z_g, b_gate, yc, ym, d):
    s = z_g.shape[0]
    tr = _pick(s, ROW_TILE, 16)
    tc = _pick(d, COL_TILE, LANE)
    nc = d // tc

    def body(za_ref, zb_ref, ba_ref, bb_ref, yc_ref, ym_ref, o_ref):
        ga = jax.nn.sigmoid(za_ref[...] + ba_ref[...])
        gb = jax.nn.sigmoid(zb_ref[...] + bb_ref[...])
        o_ref[...] = (ga * yc_ref[...] + gb * ym_ref[...]).astype(BF)

    return pl.pallas_call(
        body, name="gate_fwd", grid=(nc, s // tr),
        in_specs=[_spec_cur(tr, tc, 0), _spec_cur(tr, tc, nc),
                  pl.BlockSpec((1, tc), lambda j, i: (0, j)), pl.BlockSpec((1, tc), lambda j, i: (0, nc + j)),
                  _spec_cur(tr, tc, 0), _spec_cur(tr, tc, 0)],
        out_specs=_spec_cur(tr, tc, 0),
        out_shape=jax.ShapeDtypeStruct((s, d), BF),
        compiler_params=_cp("parallel", "parallel"),
    )(z_g, z_g, b_gate, b_gate, yc, ym)


def _gate_bwd(d_mix, z_g, b_gate, yc, ym, d):
    s = z_g.shape[0]
    tr = _pick(s, ROW_TILE, 16)
    tc = _pick(d, COL_TILE, LANE)
    nc = d // tc

    def body(dm_ref, za_ref, zb_ref, ba_ref, bb_ref, yc_ref, ym_ref,
             dza_ref, dzb_ref, dyc_ref, dym_ref, dba_ref, dbb_ref):
        i = pl.program_id(1)
        dm = dm_ref[...]
        ga = jax.nn.sigmoid(za_ref[...] + ba_ref[...])
        gb = jax.nn.sigmoid(zb_ref[...] + bb_ref[...])
        dza = dm * yc_ref[...] * (ga * (1.0 - ga))
        dzb = dm * ym_ref[...] * (gb * (1.0 - gb))
        dza_ref[...] = dza.astype(BF)
        dzb_ref[...] = dzb.astype(BF)
        dyc_ref[...] = (dm * ga).astype(BF)
        dym_ref[...] = (dm * gb).astype(BF)

        @pl.when(i == 0)
        def _():
            dba_ref[...] = jnp.zeros_like(dba_ref)
            dbb_ref[...] = jnp.zeros_like(dbb_ref)

        dba_ref[...] += _rows8([jnp.sum(dza, axis=0, keepdims=True)], tc)
        dbb_ref[...] += _rows8([jnp.sum(dzb, axis=0, keepdims=True)], tc)

    act = jax.ShapeDtypeStruct((s, d), BF)
    bsh = jax.ShapeDtypeStruct((SUB, d), F32)
    return pl.pallas_call(
        body, name="gate_bwd", grid=(nc, s // tr),
        in_specs=[_spec_cur(tr, tc, 0), _spec_cur(tr, tc, 0), _spec_cur(tr, tc, nc),
                  pl.BlockSpec((1, tc), lambda j, i: (0, j)), pl.BlockSpec((1, tc), lambda j, i: (0, nc + j)),
                  _spec_cur(tr, tc, 0), _spec_cur(tr, tc, 0)],
        out_specs=[_spec_cur(tr, tc, 0)] * 4 + [_spec_w(tc, 0)] * 2,
        out_shape=[act, act, act, act, bsh, bsh],
        compiler_params=_cp("parallel", "arbitrary"),
    )(d_mix, z_g, z_g, b_gate, b_gate, yc, ym)


def _lay(v):
    z = jnp.zeros(v.shape[:-1] + (HALF,), v.dtype)
    return jnp.concatenate([v[..., :HALF], z, v[..., HALF:], z], axis=-1)


def _unlay(v):
    return jnp.concatenate([v[..., :HALF], v[..., 2 * HALF:3 * HALF]], axis=-1)


def _rope_tables(positions):
    s = positions.shape[0]
    tr = _pick(s, ROW_TILE, 8)
    inv_freq = ROPE_THETA ** (-jnp.arange(0, ROPE, 2, dtype=F32) / ROPE)
    consts = jnp.stack([_lay(jnp.concatenate([inv_freq, inv_freq])),
                        _lay(jnp.ones((ROPE,), F32)),
                        _lay(jnp.concatenate([-jnp.ones((HALF,), F32), jnp.ones((HALF,), F32)]))])
    consts = _pad8(consts)

    def body(p_ref, c_ref, cos_ref, sin_ref):
        ang = p_ref[...].astype(F32) * c_ref[0:1, :]
        cos_ref[...] = jnp.cos(ang) * c_ref[1:2, :]
        sin_ref[...] = jnp.sin(ang) * c_ref[2:3, :]

    tab = jax.ShapeDtypeStruct((s, LANE), F32)
    return pl.pallas_call(
        body, name="rope_tables", grid=(s // tr,),
        in_specs=[pl.BlockSpec((tr, 1), lambda i: (i, 0)), pl.BlockSpec((SUB, LANE), lambda i: (0, 0))],
        out_specs=[pl.BlockSpec((tr, LANE), lambda i: (i, 0))] * 2,
        out_shape=[tab, tab],
        compiler_params=_cp("parallel"),
    )(positions, consts)


def _rope(t, cos, sin):
    return t * cos + pltpu.roll(t, 2 * HALF, axis=1) * sin


def _rope_t(d, cos, sin):
    return d * cos + pltpu.roll(d * sin, 2 * HALF, axis=1)


def _head_fwd(q_raw, kv_raw, z_a, kr_blk, cos, sin, gains, heads):
    s = q_raw.shape[0]
    tr = _pick(s, HEAD_ROW_TILE, 16)
    hw = heads * LANE

    def body(q_ref, kv_ref, kr_ref, cos_ref, sin_ref, g_ref, qo_ref, ko_ref, vo_ref):
        cosv = cos_ref[...]
        sinv = sin_ref[...]
        krv = kr_ref[...]
        kr_ss = jnp.sum(krv * krv, axis=-1, keepdims=True)
        for h in range(heads):
            lo = h * LANE
            qn = q_ref[:, lo:lo + LANE]
            qr = q_ref[:, hw + lo:hw + lo + LANE]
            ss = jnp.sum(qn * qn, axis=-1, keepdims=True) + jnp.sum(qr * qr, axis=-1, keepdims=True)
            r = lax.rsqrt(ss / HEAD_QK + NORM_EPS)
            qo_ref[:, 2 * lo:2 * lo + LANE] = ((qn * r) * g_ref[0:1, :]).astype(BF)
            qo_ref[:, 2 * lo + LANE:2 * lo + 2 * LANE] = _rope((qr * r) * g_ref[1:2, :], cosv, sinv).astype(BF)
            kn = kv_ref[:, 2 * lo:2 * lo + LANE]
            ss = jnp.sum(kn * kn, axis=-1, keepdims=True) + kr_ss
            r = lax.rsqrt(ss / HEAD_QK + NORM_EPS)
            ko_ref[:, 2 * lo:2 * lo + LANE] = ((kn * r) * g_ref[2:3, :]).astype(BF)
            ko_ref[:, 2 * lo + LANE:2 * lo + 2 * LANE] = _rope((krv * r) * g_ref[3:4, :], cosv, sinv).astype(BF)
            vo_ref[:, lo:lo + LANE] = kv_ref[:, 2 * lo + LANE:2 * lo + 2 * LANE].astype(BF)

    row = lambda w: pl.BlockSpec((tr, w), lambda i: (i, 0))
    return pl.pallas_call(
        body, name="head_fwd", grid=(s // tr,),
        in_specs=[row(2 * hw), row(2 * hw), pl.BlockSpec((tr, LANE), lambda i: (i, kr_blk)),
                  row(LANE), row(LANE), pl.BlockSpec((SUB, LANE), lambda i: (0, 0))],
        out_specs=[row(2 * hw), row(2 * hw), row(hw)],
        out_shape=[jax.ShapeDtypeStruct((s, 2 * hw), BF), jax.ShapeDtypeStruct((s, 2 * hw), BF),
                   jax.ShapeDtypeStruct((s, hw), BF)],
        compiler_params=_cp("parallel"),
    )(q_raw, kv_raw, z_a, cos, sin, gains)


def _head_bwd(q_raw, kv_raw, z_a, kr_blk, cos, sin, gains, dq_att, dk_att, dv, heads):
    s = q_raw.shape[0]
    tr = _pick(s, HEAD_ROW_TILE_BWD, 16)
    hw = heads * LANE

    def body(q_ref, kv_ref, kr_ref, cos_ref, sin_ref, g_ref, dq_ref, dk_ref, dv_ref,
             dqr_ref, dkv_ref, dkr_ref, dg_ref):
        i = pl.program_id(0)
        cosv = cos_ref[...]
        sinv = sin_ref[...]
        krv = kr_ref[...]
        kr_ss = jnp.sum(krv * krv, axis=-1, keepdims=True)
        dkr = jnp.zeros((tr, LANE), F32)
        dgs = [jnp.zeros((1, LANE), F32) for _ in range(4)]

        def norm_bwd(xn, xr, ss, dn_out, dr_out, gn, gr):
            r = lax.rsqrt(ss / HEAD_QK + NORM_EPS)
            nn = xn * r
            nr = xr * r
            dt = _rope_t(dr_out, cosv, sinv)
            dnn = dn_out * gn
            dnr = dt * gr
            mean = (jnp.sum(dnn * nn, axis=-1, keepdims=True) + jnp.sum(dnr * nr, axis=-1, keepdims=True)) / HEAD_QK
            return (r * (dnn - nn * mean), r * (dnr - nr * mean),
                    jnp.sum(dn_out * nn, axis=0, keepdims=True), jnp.sum(dt * nr, axis=0, keepdims=True))

        for h in range(heads):
            lo = h * LANE
            qn = q_ref[:, lo:lo + LANE]
            qr = q_ref[:, hw + lo:hw + lo + LANE]
            ss = jnp.sum(qn * qn, axis=-1, keepdims=True) + jnp.sum(qr * qr, axis=-1, keepdims=True)
            dxn, dxr, g0, g1 = norm_bwd(qn, qr, ss, dq_ref[:, 2 * lo:2 * lo + LANE],
                                        dq_ref[:, 2 * lo + LANE:2 * lo + 2 * LANE], g_ref[0:1, :], g_ref[1:2, :])
            dqr_ref[:, lo:lo + LANE] = dxn.astype(BF)
            dqr_ref[:, hw + lo:hw + lo + LANE] = dxr.astype(BF)
            kn = kv_ref[:, 2 * lo:2 * lo + LANE]
            ss = jnp.sum(kn * kn, axis=-1, keepdims=True) + kr_ss
            dxn, dxr, g2, g3 = norm_bwd(kn, krv, ss, dk_ref[:, 2 * lo:2 * lo + LANE],
                                        dk_ref[:, 2 * lo + LANE:2 * lo + 2 * LANE], g_ref[2:3, :], g_ref[3:4, :])
            dkv_ref[:, 2 * lo:2 * lo + LANE] = dxn.astype(BF)
            dkv_ref[:, 2 * lo + LANE:2 * lo + 2 * LANE] = dv_ref[:, lo:lo + LANE].astype(BF)
            dkr = dkr + dxr
            dgs = [a + b for a, b in zip(dgs, (g0, g1, g2, g3))]
        dkr_ref[...] = dkr

        @pl.when(i == 0)
        def _():
            dg_ref[...] = jnp.zeros_like(dg_ref)

        dg_ref[...] += _rows8(dgs, LANE)

    row = lambda w: pl.BlockSpec((tr, w), lambda i: (i, 0))
    return pl.pallas_call(
        body, name="head_bwd", grid=(s // tr,),
        in_specs=[row(2 * hw), row(2 * hw), pl.BlockSpec((tr, LANE), lambda i: (i, kr_blk)),
                  row(LANE), row(LANE), pl.BlockSpec((SUB, LANE), lambda i: (0, 0)),
                  row(2 * hw), row(2 * hw), row(hw)],
        out_specs=[row(2 * hw), row(2 * hw), row(LANE), pl.BlockSpec((SUB, LANE), lambda i: (0, 0))],
        out_shape=[jax.ShapeDtypeStruct((s, 2 * hw), BF), jax.ShapeDtypeStruct((s, 2 * hw), BF),
                   jax.ShapeDtypeStruct((s, LANE), F32), jax.ShapeDtypeStruct((SUB, LANE), F32)],
        compiler_params=_cp("arbitrary"),
    )(q_raw, kv_raw, z_a, cos, sin, gains, dq_att, dk_att, dv)


def _causal_mask(nrows, ncols, row0):
    rows = lax.broadcasted_iota(jnp.int32, (nrows, ncols), 0) + row0
    cols = lax.broadcasted_iota(jnp.int32, (nrows, ncols), 1)
    return cols <= rows


def _causal_steps(nt, q_major):
    pairs = ([(i, j) for i in range(nt) for j in range(i + 1)] if q_major
             else [(i, j) for j in range(nt) for i in range(j, nt)])
    return (jnp.array([p[0] for p in pairs], jnp.int32), jnp.array([p[1] for p in pairs], jnp.int32))


def _attn_fwd(q_att, k_att, v, heads):
    s = q_att.shape[0]
    t = _pick(s, ATTN_TILE_FWD, LANE)
    nt = s // t
    th = t
    scale = HEAD_QK ** -0.5
    qi, kj = _causal_steps(nt, True)

    def body(qi_ref, kj_ref, q_ref, k_ref, v_ref, o_ref, ob_ref, lse_ref, m_s, l_s, acc_s):
        st = pl.program_id(1)
        i = qi_ref[st]
        j = kj_ref[st]

        @pl.when(j == 0)
        def _():
            m_s[...] = jnp.full_like(m_s, NEG_INF)
            l_s[...] = jnp.zeros_like(l_s)
            acc_s[...] = jnp.zeros_like(acc_s)

        def step(masked):
            for r0 in range(0, t, th):
                rows = slice(r0, r0 + th)
                sc = lax.dot_general(q_ref[rows, :], k_ref[...], (((1,), (1,)), ((), ())),
                                     preferred_element_type=F32) * scale
                if masked:
                    sc = jnp.where(_causal_mask(th, t, r0), sc, NEG_INF)
                m_prev = m_s[rows, :]
                m_new = jnp.maximum(m_prev, jnp.max(sc, axis=-1, keepdims=True))
                alpha = jnp.exp(m_prev - m_new)
                p = jnp.exp(sc - jnp.tile(m_new, (1, t // LANE)))
                l_s[rows, :] = alpha * l_s[rows, :] + jnp.sum(p, axis=-1, keepdims=True)
                acc_s[rows, :] = alpha * acc_s[rows, :] + jnp.dot(p.astype(BF), v_ref[...],
                                                                  preferred_element_type=F32)
                m_s[rows, :] = m_new

        @pl.when(j < i)
        def _():
            step(False)

        @pl.when(j == i)
        def _():
            step(True)
            o = acc_s[...] / l_s[...]
            o_ref[...] = o
            ob_ref[...] = o.astype(BF)
            lse_ref[...] = (m_s[...] + jnp.log(l_s[...]))[:, 0:1]

    q_idx = lambda h, st, qi_r, kj_r: (qi_r[st], h)
    kv_idx = lambda h, st, qi_r, kj_r: (kj_r[st], h)
    return pl.pallas_call(
        body, name="attn_fwd",
        grid_spec=pltpu.PrefetchScalarGridSpec(
            num_scalar_prefetch=2, grid=(heads, qi.shape[0]),
            in_specs=[pl.BlockSpec((t, 2 * LANE), q_idx), pl.BlockSpec((t, 2 * LANE), kv_idx),
                      pl.BlockSpec((t, LANE), kv_idx)],
            out_specs=[pl.BlockSpec((t, LANE), q_idx), pl.BlockSpec((t, LANE), q_idx),
                       pl.BlockSpec((None, t, 1), lambda h, st, qi_r, kj_r: (h, qi_r[st], 0))],
            scratch_shapes=[pltpu.VMEM((t, LANE), F32), pltpu.VMEM((t, LANE), F32), pltpu.VMEM((t, LANE), F32)]),
        out_shape=[jax.ShapeDtypeStruct((s, heads * LANE), F32), jax.ShapeDtypeStruct((s, heads * LANE), BF),
                   jax.ShapeDtypeStruct((heads, s, 1), F32)],
        compiler_params=_cp("parallel", "arbitrary"),
    )(qi, kj, q_att, k_att, v)


def _attn_bwd(q_att, k_att, v, o, lse, d_o, heads, dep=None):
    s = q_att.shape[0]
    t = _pick(s, ATTN_TILE, LANE)
    nt = s // t
    scale = HEAD_QK ** -0.5
    qi, kj = _causal_steps(nt, False)

    def body(qi_ref, kj_ref, q_ref, k_ref, v_ref, do_ref, o_ref, lse_ref, *rest):
        dq_ref, dk_ref, dv_ref, dk_s, dv_s = rest[-5:]
        st = pl.program_id(1)
        i = qi_ref[st]
        j = kj_ref[st]

        @pl.when(st == 0)
        def _():
            dq_ref[...] = jnp.zeros_like(dq_ref)

        @pl.when(i == j)
        def _():
            dk_s[...] = jnp.zeros_like(dk_s)
            dv_s[...] = jnp.zeros_like(dv_s)

        def step(masked):
            q = q_ref[...]
            k = k_ref[...]
            do = do_ref[...]
            sc = lax.dot_general(q, k, (((1,), (1,)), ((), ())), preferred_element_type=F32) * scale
            if masked:
                sc = jnp.where(_causal_mask(t, t, 0), sc, NEG_INF)
            p = jnp.exp(sc - lse_ref[...])
            dp = lax.dot_general(do, v_ref[...], (((1,), (1,)), ((), ())), preferred_element_type=F32)
            delta = jnp.sum(do.astype(F32) * o_ref[...], axis=-1, keepdims=True)
            ds = (p * (dp - delta) * scale).astype(BF)
            dv_s[...] += lax.dot_general(p.astype(BF), do, (((0,), (0,)), ((), ())), preferred_element_type=F32)
            dk_s[...] += lax.dot_general(ds, q, (((0,), (0,)), ((), ())), preferred_element_type=F32)
            rows = pl.ds(pl.multiple_of(i * t, t), t)
            dq_ref[rows, :] += jnp.dot(ds, k, preferred_element_type=F32)

        @pl.when(i > j)
        def _():
            step(False)

        @pl.when(i == j)
        def _():
            step(True)

        @pl.when(i == nt - 1)
        def _():
            dk_ref[...] = dk_s[...]
            dv_ref[...] = dv_s[...]

    q_idx = lambda h, st, qi_r, kj_r: (qi_r[st], h)
    kv_idx = lambda h, st, qi_r, kj_r: (kj_r[st], h)
    in_specs = [pl.BlockSpec((t, 2 * LANE), q_idx), pl.BlockSpec((t, 2 * LANE), kv_idx),
                pl.BlockSpec((t, LANE), kv_idx), pl.BlockSpec((t, LANE), q_idx), pl.BlockSpec((t, LANE), q_idx),
                pl.BlockSpec((None, t, 1), lambda h, st, qi_r, kj_r: (h, qi_r[st], 0))]
    args = [q_att, k_att, v, d_o, o, lse]
    if dep is not None:
        in_specs.append(ANY)
        args.append(dep)
    return pl.pallas_call(
        body, name="attn_bwd",
        grid_spec=pltpu.PrefetchScalarGridSpec(
            num_scalar_prefetch=2, grid=(heads, qi.shape[0]),
            in_specs=in_specs,
            out_specs=[pl.BlockSpec((s, 2 * LANE), lambda h, st, qi_r, kj_r: (0, h)),
                       pl.BlockSpec((t, 2 * LANE), kv_idx), pl.BlockSpec((t, LANE), kv_idx)],
            scratch_shapes=[pltpu.VMEM((t, 2 * LANE), F32), pltpu.VMEM((t, LANE), F32)]),
        out_shape=[jax.ShapeDtypeStruct((s, heads * 2 * LANE), F32),
                   jax.ShapeDtypeStruct((s, heads * 2 * LANE), F32),
                   jax.ShapeDtypeStruct((s, heads * LANE), F32)],
        compiler_params=_cp("parallel", "arbitrary"),
    )(qi, kj, *args)


def _loss_head(y, target):
    s, d = y.shape
    tr = _pick(s, ROW_TILE, 8)

    def body(y_ref, t_ref, dy_ref, dyb_ref, l_ref):
        i = pl.program_id(0)
        e = y_ref[...] - t_ref[...]
        dy_ref[...] = e / d
        dyb_ref[...] = (e / d).astype(BF)

        @pl.when(i == 0)
        def _():
            l_ref[...] = jnp.zeros_like(l_ref)

        l_ref[...] += 0.5 * jnp.sum(jnp.mean(e * e, axis=-1, keepdims=True), axis=0, keepdims=True)

    return pl.pallas_call(
        body, name="loss_head", grid=(s // tr,),
        in_specs=[pl.BlockSpec((tr, d), lambda i: (i, 0))] * 2,
        out_specs=[pl.BlockSpec((tr, d), lambda i: (i, 0)), pl.BlockSpec((tr, d), lambda i: (i, 0)),
                   pl.BlockSpec((SUB, LANE), lambda i: (0, 0))],
        out_shape=[jax.ShapeDtypeStruct((s, d), F32), jax.ShapeDtypeStruct((s, d), BF),
                   jax.ShapeDtypeStruct((SUB, LANE), F32)],
        compiler_params=_cp("arbitrary"),
    )(y, target)


def _sum_parts(parts, name):
    n, r, c = parts.shape
    tr = _pick(r, 512, 8)

    def body(p_ref, o_ref):
        g = p_ref[0].astype(F32)
        for k in range(1, n):
            g = g + p_ref[k].astype(F32)
        o_ref[...] = g

    return pl.pallas_call(
        body, name=name, grid=(r // tr,),
        in_specs=[pl.BlockSpec((n, tr, c), lambda i: (0, i, 0))],
        out_specs=pl.BlockSpec((tr, c), lambda i: (i, 0)),
        out_shape=jax.ShapeDtypeStruct((r, c), F32),
        compiler_params=_cp("parallel"),
    )(parts)


def _adamw(parts, w, m, v, name):
    n, r, c = parts.shape
    tr = _pick(r, 256, 8)

    def body(p_ref, w_ref, m_ref, v_ref, g_ref, d_ref, mo_ref, vo_ref):
        g = p_ref[0].astype(F32)
        for k in range(1, n):
            g = g + p_ref[k].astype(F32)
        m_new = ADAM_B1 * m_ref[...] + (1.0 - ADAM_B1) * g
        v_new = ADAM_B2 * v_ref[...] + (1.0 - ADAM_B2) * jnp.square(g)
        m_hat = m_new / (1.0 - ADAM_B1 ** ADAM_STEP)
        v_hat = v_new / (1.0 - ADAM_B2 ** ADAM_STEP)
        g_ref[...] = g
        d_ref[...] = -ADAM_LR * (m_hat / (jnp.sqrt(v_hat) + ADAM_EPS) + ADAM_WD * w_ref[...])
        mo_ref[...] = m_new
        vo_ref[...] = v_new

    spec = pl.BlockSpec((tr, c), lambda i: (i, 0))
    sh = jax.ShapeDtypeStruct((r, c), F32)
    return pl.pallas_call(
        body, name=name, grid=(r // tr,),
        in_specs=[pl.BlockSpec((n, tr, c), lambda i: (0, i, 0)), spec, spec, spec],
        out_specs=[spec] * 4, out_shape=[sh] * 4,
        compiler_params=_cp("parallel"),
    )(parts, w, m, v)


def _place():
    x, y, c = lax.axis_index("x"), lax.axis_index("y"), lax.axis_index("c")
    chips = [(1 - x, y), (x, 1 - y), (1 - x, 1 - y)]
    return x, y, c, chips


def _all_gather(shards, name, dep=None):
    n = len(shards)
    deps = [] if dep is None else list(dep)

    def body(*refs):
        ins, outs = refs[:n], refs[n + len(deps):2 * n + len(deps)]
        send_sems, recv_sems, local_sems = refs[2 * n + len(deps):]
        x, y, c, chips = _place()
        me, sibling = (x, y, c), (x, y, 1 - c)

        def slot(w, p):
            return outs[w].at[4 * p[0] + 2 * p[1] + p[2]]

        def copy(w, k, block, to, src=None):
            return pltpu.make_async_remote_copy(
                src_ref=slot(w, block) if src is None else src, dst_ref=slot(w, block),
                send_sem=send_sems.at[w, k], recv_sem=recv_sems.at[w, k], device_id=to, device_id_type=MESH)

        first = []
        for w in range(n):
            first += [copy(w, 1 + j, me, (*chip, c), src=ins[w]) for j, chip in enumerate(chips)]
            first.append(copy(w, 0, me, sibling, src=ins[w]))
        for cp in first:
            cp.start()
        mine = [pltpu.make_async_copy(ins[w], slot(w, me), local_sems.at[w]) for w in range(n)]
        for cp in mine:
            cp.start()
        passed = []
        for w in range(n):
            for j, chip in enumerate(chips):
                copy(w, 1 + j, (*chip, c), me).wait_recv()
                cp = copy(w, 4 + j, (*chip, c), sibling)
                cp.start()
                passed.append(cp)
        for w in range(n):
            copy(w, 0, sibling, me).wait_recv()
            for j, chip in enumerate(chips):
                copy(w, 4 + j, (*chip, 1 - c), me).wait_recv()
        for cp in first + passed:
            cp.wait_send()
        for cp in mine:
            cp.wait()

    return pl.pallas_call(
        body, name=name,
        in_specs=[ANY] * (n + len(deps)), out_specs=[ANY] * n,
        out_shape=[jax.ShapeDtypeStruct((N_DEV,) + a.shape, a.dtype) for a in shards],
        scratch_shapes=[pltpu.SemaphoreType.DMA((n, 7)), pltpu.SemaphoreType.DMA((n, 7)),
                        pltpu.SemaphoreType.DMA((n,))],
    )(*shards, *deps)


HBM = pl.BlockSpec(memory_space=pltpu.HBM)
SEM = pl.BlockSpec(memory_space=pltpu.SEMAPHORE)
EFFECT = pltpu.SideEffectType.DATAFLOW_SIDE_EFFECTING
PEERS = [(dx, dy, dc) for dx in (1, 0) for dy in (1, 0) for dc in (0, 1) if (dx, dy, dc) != (0, 0, 0)]


def _peer(x, y, c, flip):
    dx, dy, dc = flip
    return (1 - x if dx else x, 1 - y if dy else y, 1 - c if dc else c)


def _exchange_copies(srcs, lands, send, recv, loc, gather):
    x, y, c, _ = _place()
    me = 4 * x + 2 * y + c
    remote, local = [], []
    for w in range(len(srcs)):
        for k, flip in enumerate(PEERS):
            px, py, pc = _peer(x, y, c, flip)
            src = srcs[w] if gather else srcs[w].at[4 * px + 2 * py + pc]
            remote.append(pltpu.make_async_remote_copy(
                src_ref=src, dst_ref=lands[w].at[me], send_sem=send[w].at[k], recv_sem=recv[w].at[k],
                device_id=(px, py, pc), device_id_type=MESH))
        local.append(pltpu.make_async_copy(srcs[w] if gather else srcs[w].at[me], lands[w].at[me], loc[w]))
    return remote, local


class _Exchange:
    def __init__(self, srcs, lands, send, recv, loc, token, gather):
        self.srcs, self.lands, self.send, self.recv, self.loc = srcs, lands, send, recv, loc
        self.token, self.gather = token, gather


def _exchange_start(srcs, gather, name):
    n = len(srcs)
    land_shapes = [((N_DEV,) + a.shape) if gather else a.shape for a in srcs]
    lands = [pltpu.with_memory_space_constraint(lax.empty(sh, a.dtype), pltpu.HBM) for sh, a in zip(land_shapes, srcs)]
    srcs = [pltpu.with_memory_space_constraint(a, pltpu.HBM) for a in srcs]

    def body(*refs):
        src_refs, land_refs = refs[:n], refs[n:2 * n]
        outs = refs[2 * n:]
        send, recv, loc = outs[:n], outs[n:2 * n], outs[2 * n:3 * n]
        token = outs[-1]
        remote, local = _exchange_copies(src_refs, land_refs, send, recv, loc, gather)
        for cp in remote + local:
            cp.start()
        token[...] = jnp.zeros_like(token)

    out_shape = ([pltpu.SemaphoreType.DMA((len(PEERS),))] * (2 * n) + [pltpu.SemaphoreType.DMA(())] * n
                 + [pltpu.HBM(a.shape, a.dtype) for a in srcs] + [pltpu.HBM(a.shape, a.dtype) for a in lands]
                 + [jax.ShapeDtypeStruct((SUB, LANE), F32)])
    res = pl.pallas_call(
        body, name=name, out_shape=out_shape,
        in_specs=[HBM] * (2 * n),
        out_specs=[SEM] * (3 * n) + [HBM] * (2 * n) + [pl.BlockSpec(memory_space=pltpu.VMEM)],
        input_output_aliases={i: 3 * n + i for i in range(2 * n)},
        compiler_params=pltpu.CompilerParams(has_side_effects=EFFECT),
    )(*srcs, *lands)
    return _Exchange(res[3 * n:4 * n], res[4 * n:5 * n], res[:n], res[n:2 * n], res[2 * n:3 * n], res[-1], gather)


def _exchange_wait(ex, idxs, after, name):
    n = len(idxs)
    srcs = [ex.srcs[i] for i in idxs]
    lands = [ex.lands[i] for i in idxs]
    sems = [ex.send[i] for i in idxs] + [ex.recv[i] for i in idxs] + [ex.loc[i] for i in idxs]
    gather = ex.gather

    def body(*refs):
        src_refs, land_refs = refs[:n], refs[n:2 * n]
        send, recv, loc = refs[2 * n:3 * n], refs[3 * n:4 * n], refs[4 * n:5 * n]
        remote, local = _exchange_copies(src_refs, land_refs, send, recv, loc, gather)
        for cp in remote:
            cp.wait_send()
            cp.wait_recv()
        for cp in local:
            cp.wait()

    res = pl.pallas_call(
        body, name=name,
        out_shape=[pltpu.HBM(a.shape, a.dtype) for a in srcs] + [pltpu.HBM(a.shape, a.dtype) for a in lands],
        in_specs=[HBM] * (2 * n) + [SEM] * (3 * n) + [ANY],
        out_specs=[HBM] * (2 * n),
        input_output_aliases={i: i for i in range(2 * n)},
        compiler_params=pltpu.CompilerParams(has_side_effects=EFFECT),
    )(*srcs, *lands, *sems, after)
    return res[n:]


def _after(token, a):
    return a + token[0:1, 0:1].astype(a.dtype)


def _unblock(w3):
    nb, k, nbw = w3.shape
    return w3.transpose(1, 0, 2).reshape(k, nb * nbw)


def _block(w, nb):
    k, n = w.shape
    return w.reshape(k, nb, n // nb).transpose(1, 0, 2)


def kernel(x, positions, ln1_g, w_in, b_gate, conv_w, w_conv_out, q_a_g, w_q_b, kv_a_g, w_kv_b, q_norm_g, k_norm_g, w_mla_out, w_o, ln2_g, w_ffn_up, ffn_conv_w, ffn_conv_b, w_ffn_down, loss_target, m_ln1_g, m_w_in, m_b_gate, m_conv_w, m_w_conv_out, m_q_a_g, m_w_q_b, m_kv_a_g, m_w_kv_b, m_q_norm_g, m_k_norm_g, m_w_mla_out, m_w_o, m_ln2_g, m_w_ffn_up, m_ffn_conv_w, m_ffn_conv_b, m_w_ffn_down, v_ln1_g, v_w_in, v_b_gate, v_conv_w, v_w_conv_out, v_q_a_g, v_w_q_b, v_kv_a_g, v_w_kv_b, v_q_norm_g, v_k_norm_g, v_w_mla_out, v_w_o, v_ln2_g, v_w_ffn_up, v_ffn_conv_w, v_ffn_conv_b, v_w_ffn_down):
    s, d = x.shape[1], x.shape[2]
    conv = conv_w.shape[2] * N_DEV
    ql, kvl = q_a_g.shape[1], kv_a_g.shape[1]
    heads = w_q_b.shape[2] * N_DEV // HEAD_QK
    dff = w_ffn_down.shape[1] * N_DEV
    hw = heads * LANE
    conv3 = 3 * conv
    kr_off = conv3 + ql
    kv_off = -(-(kr_off + LANE) // kvl) * kvl
    wa = kv_off + kvl
    assert conv3 % ql == 0 and kr_off % LANE == 0
    xs = x[0]
    tgt = loss_target[0]
    pos = positions.reshape(s, 1)

    big = dict(w_in=w_in[0], w_conv_out=w_conv_out[0], w_q_b=w_q_b[0], w_kv_b=w_kv_b[0],
               w_mla_out=w_mla_out[0], w_o=w_o[0], w_ffn_up=w_ffn_up[0], w_ffn_down=w_ffn_down[0])
    names = list(big)
    rest = names[1:]
    first = _all_gather([big["w_in"].astype(BF), _pad8(conv_w[0]), _pad8(ffn_conv_w[0])], "gather_w_in")
    cw8 = _unblock(first[1])
    fcw8 = _unblock(first[2])
    ag = _exchange_start([big[k].astype(BF) for k in rest], True, "gather_rest_start")

    def landed(keys, after, name):
        return _exchange_wait(ag, [rest.index(k) for k in keys], after, name)

    w_in_full = _unblock(first[0])
    zpad = jnp.zeros((d, kv_off - kr_off - LANE), BF)
    w_a = jnp.concatenate([w_in_full[:, :kr_off], _lay(w_in_full[:, kr_off + kvl:kr_off + kvl + ROPE]), zpad,
                           w_in_full[:, kr_off:kr_off + kvl]], axis=1)[None]
    g_off = kr_off + kvl + ROPE
    w_g = _block(w_in_full[:, g_off:], 2)
    gains = _pad8(jnp.concatenate([q_norm_g[:, :NOPE], _lay(q_norm_g[:, NOPE:]),
                                   k_norm_g[:, :NOPE], _lay(k_norm_g[:, NOPE:])], axis=0))
    kr_blk = kr_off // LANE

    cos, sin = _rope_tables(pos)
    u1 = _rms_fwd(xs, _after(ag.token, ln1_g), d, 0, "rms1_fwd")
    z_a = _mm_nn(u1, w_a, "mm_z_a")
    z_g = _mm_nn(u1, w_g, "mm_z_g")
    p = _conv_mix_fwd(z_a, cw8, conv)
    w_co, w_qb, w_kv = landed(["w_conv_out", "w_q_b", "w_kv_b"], p, "gather_wait_mixers")
    wq_full = _unblock(w_qb).reshape(ql, heads, HEAD_QK)
    w_q = jnp.concatenate([wq_full[:, :, :NOPE].reshape(ql, hw), _lay(wq_full[:, :, NOPE:]).reshape(ql, hw)],
                          axis=1)[None]
    yc = _mm_nn(p, w_co, "mm_y_conv")
    qn = _rms_fwd(z_a, q_a_g, ql, conv3 // ql, "rms_q_fwd")
    kvn = _rms_fwd(z_a, kv_a_g, kvl, kv_off // kvl, "rms_kv_fwd")
    q_raw = _mm_nn(qn, w_q, "mm_q")
    kv_raw = _mm_nn(kvn, w_kv, "mm_kv")
    q_att, k_att, v_bf = _head_fwd(q_raw, kv_raw, z_a, kr_blk, cos, sin, gains, heads)
    o, o_bf, lse = _attn_fwd(q_att, k_att, v_bf, heads)
    w_mo, w_oo = landed(["w_mla_out", "w_o"], lse, "gather_wait_outs")
    w_mo = w_mo.reshape(1, hw, d)
    w_oo = w_oo.reshape(1, d, d)
    ym = _mm_nn(o_bf, w_mo, "mm_y_mla")
    mix = _gate_fwd(z_g, b_gate, yc, ym, d)
    h1 = _mm_nn(mix, w_oo, "mm_h1", add=xs)
    u2 = _rms_fwd(h1, ln2_g, d, 0, "rms2_fwd")
    w_up, = landed(["w_ffn_up"], u2, "gather_wait_ffn_up")
    a_pre = _mm_nn(u2, w_up, "mm_ffn_up")
    f = _ffn_act_fwd(a_pre, fcw8, ffn_conv_b, dff)
    w_dn, = landed(["w_ffn_down"], f, "gather_wait_ffn_down")
    w_dn = w_dn.reshape(1, dff, d)
    y = _mm_nn(f, w_dn, "mm_ffn_down", add=h1)
    dy, dy_bf, loss_part = _loss_head(y, tgt)

    g_dn = _mm_tn(f, dy_bf, 1, "mm_g_ffn_down").reshape(N_DEV, dff // N_DEV, d)
    rs_dn = _exchange_start([g_dn], False, "reduce_ffn_down_start")
    d_f = _mm_nt(dy_bf, w_dn, "mm_d_f", dep=rs_dn.token)
    d_xg, d_xu, dfw_g, dfw_u = _ffn_act_bwd(a_pre, d_f, fcw8, ffn_conv_b, dff)
    half = N_DEV // 2
    g_up = _mm_tn(u2, d_xg, half, "mm_g_ffn_up_gate", into=lax.empty((N_DEV, d, 2 * dff // N_DEV), BF))
    g_up = _mm_tn(u2, d_xu, half, "mm_g_ffn_up_up", into=g_up, blk0=half)
    rs_up = _exchange_start([g_up], False, "reduce_ffn_up_start")
    d_u2 = _mm_nt(d_xg, w_up, "mm_d_u2_gate", blk0=0, nblk=half, dep=rs_up.token)
    d_u2 = _mm_nt(d_xu, w_up, "mm_d_u2_up", blk0=half, nblk=half, add=d_u2)
    d_h1, d_h1_bf, dg_ln2 = _rms_bwd(h1, d_u2, ln2_g, d, 0, "rms2_bwd", extra=dy, also_bf16=True)
    g_oo = _mm_tn(mix, d_h1_bf, 1, "mm_g_w_o").reshape(N_DEV, d // N_DEV, d)
    d_mix = _mm_nt(d_h1_bf, w_oo, "mm_d_mix")
    d_zga, d_zgb, d_yc, d_ym, dba, dbb = _gate_bwd(d_mix, z_g, b_gate, yc, ym, d)
    g_co = _mm_tn(p, d_yc, N_DEV, "mm_g_conv_out")
    g_mo = _mm_tn(o_bf, d_ym, 1, "mm_g_mla_out").reshape(N_DEV, hw // N_DEV, d)
    rs_mix = _exchange_start([g_oo, g_co, g_mo], False, "reduce_mixers_start")
    d_p = _mm_nt(d_yc, w_co, "mm_d_p", dep=rs_mix.token)
    d_o = _mm_nt(d_ym, w_mo, "mm_d_o", out_dtype=BF)
    d_zb, d_zc, d_zv, dcw = _conv_mix_bwd(z_a, d_p, cw8, conv)
    dq_att, dk_att, dv = _attn_bwd(q_att, k_att, v_bf, o, lse, d_o, heads, dep=rs_mix.token)
    d_q_raw, d_kv_raw, d_kr, dgains = _head_bwd(q_raw, kv_raw, z_a, kr_blk, cos, sin, gains, dq_att, dk_att, dv, heads)
    g_q2 = _mm_tn(qn, d_q_raw, 1, "mm_g_q")[0]
    g_qb = _block(jnp.concatenate([g_q2[:, :hw].reshape(ql, heads, NOPE),
                                   _unlay(g_q2[:, hw:].reshape(ql, heads, LANE))], axis=2).reshape(ql, heads * HEAD_QK), N_DEV)
    g_kv = _mm_tn(kvn, d_kv_raw, N_DEV, "mm_g_kv")
    rs_qkv = _exchange_start([g_qb, g_kv], False, "reduce_qkv_start")
    d_qn = _mm_nt(d_q_raw, w_q, "mm_d_qn", dep=rs_qkv.token)
    d_kvn = _mm_nt(d_kv_raw, w_kv, "mm_d_kvn")
    d_ql, dg_qa = _rms_bwd(z_a, d_qn, q_a_g, ql, conv3 // ql, "rms_q_bwd", out_dtype=BF)
    d_kvl, dg_kva = _rms_bwd(z_a, d_kvn, kv_a_g, kvl, kv_off // kvl, "rms_kv_bwd", out_dtype=BF)
    d_z_a = jnp.concatenate([d_zb, d_zc, d_zv, d_ql, d_kr.astype(BF), jnp.zeros((s, kv_off - kr_off - LANE), BF),
                             d_kvl], axis=1)
    g_a = _mm_tn(u1, d_z_a, 1, "mm_g_w_a")[0]
    g_ga = _mm_tn(u1, d_zga, 1, "mm_g_w_ga")[0]
    g_gb = _mm_tn(u1, d_zgb, 1, "mm_g_w_gb")[0]
    g_in = _block(jnp.concatenate([g_a[:, :kr_off], g_a[:, kv_off:kv_off + kvl],
                                   _unlay(g_a[:, kr_off:kr_off + LANE]), g_ga, g_gb], axis=1), N_DEV)
    rs_in = _exchange_start([g_in], False, "reduce_w_in_start")
    d_u1 = _mm_nt(d_z_a, w_a, "mm_d_u1_a", dep=rs_in.token)
    d_u1 = _mm_nt(d_zga, w_g, "mm_d_u1_ga", blk0=0, nblk=1, add=d_u1)
    d_u1 = _mm_nt(d_zgb, w_g, "mm_d_u1_gb", blk0=1, nblk=1, add=d_u1)
    grad_x, dg_ln1 = _rms_bwd(xs, d_u1, ln1_g, d, 0, "rms1_bwd", extra=d_h1)

    summed = {}
    summed["w_ffn_down"], = _exchange_wait(rs_dn, [0], grad_x, "reduce_ffn_down_wait")
    summed["w_ffn_up"], = _exchange_wait(rs_up, [0], grad_x, "reduce_ffn_up_wait")
    summed["w_o"], summed["w_conv_out"], summed["w_mla_out"] = _exchange_wait(rs_mix, [0, 1, 2], grad_x, "reduce_mixers_wait")
    summed["w_q_b"], summed["w_kv_b"] = _exchange_wait(rs_qkv, [0, 1], grad_x, "reduce_qkv_wait")
    loc = locals()
    out = {}
    for k in rest:
        out[k] = _adamw(summed[k], big[k], loc["m_" + k][0], loc["v_" + k][0], "adamw_" + k)

    small = dict(ln1_g=dg_ln1[0:1], b_gate=jnp.concatenate([dba[0:1], dbb[0:1]], axis=1), q_a_g=dg_qa[0:1],
                 kv_a_g=dg_kva[0:1],
                 q_norm_g=jnp.concatenate([dgains[0:1], _unlay(dgains[1:2])], axis=1),
                 k_norm_g=jnp.concatenate([dgains[2:3], _unlay(dgains[3:4])], axis=1),
                 ln2_g=dg_ln2[0:1], ffn_conv_b=jnp.concatenate([dfw_g[3:4], dfw_u[3:4]], axis=1))
    small_names = list(small)
    extra = [dcw[0:3].reshape(1, -1), jnp.concatenate([dfw_g[0:3], dfw_u[0:3]], axis=1).reshape(1, -1),
             loss_part[0:1, 0:1]]
    flat = jnp.concatenate([small[k] for k in small_names] + extra, axis=1)
    n_flat = flat.shape[1]
    rows = -(-n_flat // (SUB * LANE)) * SUB
    flat = jnp.pad(flat, ((0, 0), (0, rows * LANE - n_flat))).reshape(rows, LANE)
    total = _sum_parts(_all_gather([flat], "gather_small", dep=[out[k][0] for k in rest])[0], "sum_small").reshape(1, rows * LANE)
    off = 0
    small_g = {}
    for k in small_names:
        small_g[k] = total[:, off:off + small[k].shape[1]]
        off += small[k].shape[1]
    me = 4 * lax.axis_index("x") + 2 * lax.axis_index("y") + lax.axis_index("c")
    cwn, fcwn = conv // N_DEV, 2 * dff // N_DEV
    g_cw = lax.dynamic_slice_in_dim(total[:, off:off + 3 * conv].reshape(3, conv), me * cwn, cwn, axis=1)
    off += 3 * conv
    g_fcw = lax.dynamic_slice_in_dim(total[:, off:off + 6 * dff].reshape(3, 2 * dff), me * fcwn, fcwn, axis=1)
    off += 6 * dff
    loss = total[0, off]

    summed["w_in"], = _exchange_wait(rs_in, [0], total, "reduce_w_in_wait")
    out["w_in"] = _adamw(summed["w_in"], big["w_in"], m_w_in[0], v_w_in[0], "adamw_w_in")
    small_w = dict(ln1_g=ln1_g, b_gate=b_gate, q_a_g=q_a_g, kv_a_g=kv_a_g, q_norm_g=q_norm_g, k_norm_g=k_norm_g,
                   ln2_g=ln2_g, ffn_conv_b=ffn_conv_b, conv_w=conv_w[0].reshape(1, -1),
                   ffn_conv_w=ffn_conv_w[0].reshape(1, -1))
    small_g["conv_w"] = g_cw.reshape(1, -1)
    small_g["ffn_conv_w"] = g_fcw.reshape(1, -1)
    packed_names = list(small_w)

    def pack(get):
        vflat = jnp.concatenate([get(k).reshape(1, -1) for k in packed_names], axis=1)
        nr = -(-vflat.shape[1] // (SUB * LANE)) * SUB
        return jnp.pad(vflat, ((0, 0), (0, nr * LANE - vflat.shape[1])), constant_values=1.0).reshape(nr, LANE)

    res = _adamw(pack(lambda k: small_g[k])[None], pack(lambda k: small_w[k]), pack(lambda k: loc["m_" + k]),
                 pack(lambda k: loc["v_" + k]), "adamw_small")
    res = [r.reshape(1, -1) for r in res]
    off = 0
    for k in packed_names:
        shape = loc[k].shape
        size = small_w[k].shape[1]
        out[k] = [r[:, off:off + size].reshape(shape) for r in res]
        off += size
    for k in names:
        out[k] = [r[None] for r in out[k]]

    order = ["ln1_g", "w_in", "b_gate", "conv_w", "w_conv_out", "q_a_g", "w_q_b", "kv_a_g", "w_kv_b", "q_norm_g",
             "k_norm_g", "w_mla_out", "w_o", "ln2_g", "w_ffn_up", "ffn_conv_w", "ffn_conv_b", "w_ffn_down"]
    return (loss, grad_x[None], *[out[k][0] for k in order], *[out[k][1] for k in order],
            *[out[k][2] for k in order], *[out[k][3] for k in order])
```

```python
import functools

import jax
import jax.numpy as jnp
from jax import lax
from jax.experimental import pallas as pl
from jax.experimental.pallas import tpu as pltpu

BF = jnp.bfloat16
F32 = jnp.float32
MESH = pl.DeviceIdType.MESH
N_DEV = 8

NOPE = 128
ROPE = 64
HALF = ROPE // 2
HEAD_QK = NOPE + ROPE
HEAD_V = 128
LANE = 128
SUB = 8
NORM_EPS = 1e-6
NEG_INF = -1e30
ROPE_THETA = 10000.0
ADAM_LR = 0.001
ADAM_B1 = 0.9
ADAM_B2 = 0.999
ADAM_EPS = 1e-08
ADAM_WD = 0.01
ADAM_STEP = 10

VMEM_LIMIT = 52 * 1024 * 1024
MM_TM, MM_TN, MM_TK, MM_TS = 1024, 1536, 2048, 1024
ROW_TILE, ROW_TILE_BWD = 512, 256
HEAD_ROW_TILE, HEAD_ROW_TILE_BWD = 256, 128
COL_TILE = 512
ATTN_TILE = 1024
ATTN_TILE_FWD = 1024
ANY = pl.BlockSpec(memory_space=pl.ANY)


def _pick(n, target, mult):
    t = (min(n, target) // mult) * mult
    while t > 0:
        if n % t == 0:
            return t
        t -= mult
    raise ValueError(f"no tile for {n} (target {target}, multiple {mult})")


def _cp(*sem):
    return pltpu.CompilerParams(dimension_semantics=sem, vmem_limit_bytes=VMEM_LIMIT)


def _accumulate(kk, nk, acc, part, finish):
    if nk == 1:
        finish(part())
        return

    @pl.when(kk == 0)
    def _():
        acc[...] = part()

    @pl.when((kk > 0) & (kk < nk - 1))
    def _():
        acc[...] += part()

    @pl.when(kk == nk - 1)
    def _():
        finish(acc[...] + part())


def _mm_call(body, name, grid, in_specs, args, out_spec, out_shape, acc_shape, nk, dep):
    if dep is not None:
        in_specs = in_specs + [ANY]
        args = args + [dep]
    return pl.pallas_call(
        body, name=name, grid=grid, in_specs=in_specs, out_specs=out_spec, out_shape=out_shape,
        scratch_shapes=[pltpu.VMEM(acc_shape, F32)] if nk > 1 else [],
        compiler_params=_cp("parallel", "parallel", "arbitrary"),
    )(*args)


def _mm_nn(a, b3, name, add=None, out_dtype=F32, blk0=0, nblk=None, dep=None):
    m, k = a.shape
    nb_all, k2, nbw = b3.shape
    assert k == k2
    nblk = nb_all - blk0 if nblk is None else nblk
    n = nblk * nbw
    tm = _pick(m, MM_TM, 16)
    tn = _pick(nbw, MM_TN, LANE)
    tk = _pick(k, MM_TK, LANE)
    per = nbw // tn
    nk = k // tk

    def body(*refs):
        a_ref, b_ref = refs[:2]
        c_ref = refs[2] if add is not None else None
        o_ref = refs[2 + (add is not None) + (dep is not None)]
        acc = refs[-1]

        def part():
            return jnp.dot(a_ref[...].astype(BF), b_ref[...].astype(BF), preferred_element_type=F32)

        def finish(r):
            if add is not None:
                r = r + c_ref[...]
            o_ref[...] = r.astype(out_dtype)

        _accumulate(pl.program_id(2), nk, acc, part, finish)

    in_specs = [pl.BlockSpec((tm, tk), lambda i, j, kk: (i, kk)),
                pl.BlockSpec((None, tk, tn), lambda i, j, kk: (blk0 + j // per, kk, j % per))]
    args = [a, b3]
    if add is not None:
        in_specs.append(pl.BlockSpec((tm, tn), lambda i, j, kk: (i, j)))
        args.append(add)
    return _mm_call(body, name, (m // tm, n // tn, nk), in_specs, args,
                    pl.BlockSpec((tm, tn), lambda i, j, kk: (i, j)), jax.ShapeDtypeStruct((m, n), out_dtype),
                    (tm, tn), nk, dep)


def _mm_nt(a, b3, name, add=None, out_dtype=F32, blk0=0, nblk=None, dep=None):
    m, n = a.shape
    nb_all, k, nbw = b3.shape
    nblk = nb_all - blk0 if nblk is None else nblk
    assert n == nblk * nbw
    tm = _pick(m, MM_TM, 16)
    tn = _pick(k, MM_TN, LANE)
    tk = _pick(nbw, MM_TK, LANE)
    per = nbw // tk
    nk = n // tk

    def body(*refs):
        a_ref, b_ref = refs[:2]
        c_ref = refs[2] if add is not None else None
        o_ref = refs[2 + (add is not None) + (dep is not None)]
        acc = refs[-1]

        def part():
            return lax.dot_general(a_ref[...].astype(BF), b_ref[...].astype(BF),
                                   (((1,), (1,)), ((), ())), preferred_element_type=F32)

        def finish(r):
            if add is not None:
                r = r + c_ref[...]
            o_ref[...] = r.astype(out_dtype)

        _accumulate(pl.program_id(2), nk, acc, part, finish)

    in_specs = [pl.BlockSpec((tm, tk), lambda i, j, kk: (i, kk)),
                pl.BlockSpec((None, tn, tk), lambda i, j, kk: (blk0 + kk // per, j, kk % per))]
    args = [a, b3]
    if add is not None:
        in_specs.append(pl.BlockSpec((tm, tn), lambda i, j, kk: (i, j)))
        args.append(add)
    return _mm_call(body, name, (m // tm, k // tn, nk), in_specs, args,
                    pl.BlockSpec((tm, tn), lambda i, j, kk: (i, j)), jax.ShapeDtypeStruct((m, k), out_dtype),
                    (tm, tn), nk, dep)


def _mm_tn(a, b, nblk, name, out_dtype=BF, dep=None, into=None, blk0=0):
    s, m = a.shape
    s2, n = b.shape
    assert s == s2 and n % nblk == 0 and (dep is None or into is None)
    nbw = n // nblk
    tm = _pick(m, MM_TN, LANE)
    tn = _pick(nbw, MM_TN, LANE)
    ts = _pick(s, MM_TS, LANE)
    per = nbw // tn
    ns = s // ts

    def body(*refs):
        a_ref, b_ref = refs[:2]
        o_ref = refs[2 + (dep is not None or into is not None)]
        acc = refs[-1]

        def part():
            return lax.dot_general(a_ref[...].astype(BF), b_ref[...].astype(BF),
                                   (((0,), (0,)), ((), ())), preferred_element_type=F32)

        def finish(r):
            o_ref[...] = r.astype(out_dtype)

        _accumulate(pl.program_id(2), ns, acc, part, finish)

    in_specs = [pl.BlockSpec((ts, tm), lambda i, j, ss: (ss, i)),
                pl.BlockSpec((ts, tn), lambda i, j, ss: (ss, j))]
    out_spec = pl.BlockSpec((None, tm, tn), lambda i, j, ss: (blk0 + j // per, i, j % per))
    if into is None:
        return _mm_call(body, name, (m // tm, n // tn, ns), in_specs, [a, b], out_spec,
                        jax.ShapeDtypeStruct((nblk, m, nbw), out_dtype), (tm, tn), ns, dep)
    assert into.shape[1:] == (m, nbw) and into.dtype == out_dtype
    return pl.pallas_call(
        body, name=name, grid=(m // tm, n // tn, ns), in_specs=in_specs + [ANY], out_specs=out_spec,
        out_shape=jax.ShapeDtypeStruct(into.shape, out_dtype), input_output_aliases={2: 0},
        scratch_shapes=[pltpu.VMEM((tm, tn), F32)] if ns > 1 else [],
        compiler_params=_cp("parallel", "parallel", "arbitrary"),
    )(a, b, into)


def _rows8(rows, width):
    idx = lax.broadcasted_iota(jnp.int32, (SUB, width), 0)
    out = jnp.zeros((SUB, width), F32)
    for r, v in enumerate(rows):
        out = jnp.where(idx == r, v, out)
    return out


def _rms_fwd(x, g, width, col_blk, name):
    s = x.shape[0]
    tr = _pick(s, ROW_TILE, 16)

    def body(x_ref, g_ref, u_ref):
        xv = x_ref[...]
        r = lax.rsqrt(jnp.mean(xv * xv, axis=-1, keepdims=True) + NORM_EPS)
        u_ref[...] = ((xv * r) * g_ref[...]).astype(BF)

    return pl.pallas_call(
        body, name=name, grid=(s // tr,),
        in_specs=[pl.BlockSpec((tr, width), lambda i: (i, col_blk)),
                  pl.BlockSpec((1, width), lambda i: (0, 0))],
        out_specs=pl.BlockSpec((tr, width), lambda i: (i, 0)),
        out_shape=jax.ShapeDtypeStruct((s, width), BF),
        compiler_params=_cp("parallel"),
    )(x, g)


def _rms_bwd(x, du, g, width, col_blk, name, extra=None, out_dtype=F32, also_bf16=False):
    s = x.shape[0]
    tr = _pick(s, ROW_TILE_BWD, 16)

    def body(*refs):
        x_ref, du_ref, g_ref = refs[:3]
        e_ref = refs[3] if extra is not None else None
        dx_ref = refs[3 + (extra is not None)]
        dxb_ref = refs[4 + (extra is not None)] if also_bf16 else None
        dg_ref = refs[-1]
        i = pl.program_id(0)
        xv = x_ref[...]
        duv = du_ref[...].astype(F32)
        r = lax.rsqrt(jnp.mean(xv * xv, axis=-1, keepdims=True) + NORM_EPS)
        nv = xv * r
        dn = duv * g_ref[...]
        dx = r * (dn - nv * jnp.mean(dn * nv, axis=-1, keepdims=True))
        if extra is not None:
            dx = dx + e_ref[...]
        dx_ref[...] = dx.astype(out_dtype)
        if also_bf16:
            dxb_ref[...] = dx.astype(BF)

        @pl.when(i == 0)
        def _():
            dg_ref[...] = jnp.zeros_like(dg_ref)

        dg_ref[...] += _rows8([jnp.sum(duv * nv, axis=0, keepdims=True)], width)

    in_specs = [pl.BlockSpec((tr, width), lambda i: (i, col_blk)),
                pl.BlockSpec((tr, width), lambda i: (i, 0)),
                pl.BlockSpec((1, width), lambda i: (0, 0))]
    args = [x, du, g]
    if extra is not None:
        in_specs.append(pl.BlockSpec((tr, width), lambda i: (i, 0)))
        args.append(extra)
    return pl.pallas_call(
        body, name=name, grid=(s // tr,),
        in_specs=in_specs,
        out_specs=[pl.BlockSpec((tr, width), lambda i: (i, 0))] * (1 + also_bf16)
        + [pl.BlockSpec((SUB, width), lambda i: (0, 0))],
        out_shape=[jax.ShapeDtypeStruct((s, width), out_dtype)] + [jax.ShapeDtypeStruct((s, width), BF)] * also_bf16
        + [jax.ShapeDtypeStruct((SUB, width), F32)],
        compiler_params=_cp("arbitrary"),
    )(*args)


def _down(cur, prev8, k):
    ext = jnp.concatenate([prev8, cur], axis=0)
    return pltpu.roll(ext, k, axis=0)[SUB:]


def _up(cur, next8, k):
    ext = jnp.concatenate([cur, next8], axis=0)
    return pltpu.roll(ext, ext.shape[0] - k, axis=0)[:cur.shape[0]]


def _lags(cur, prev8):
    return _down(cur, prev8, 1), _down(cur, prev8, 2)


def _conv3(w_ref, cur, prev8, lags=None):
    lag1, lag2 = _lags(cur, prev8) if lags is None else lags
    return w_ref[0:1, :] * lag2 + w_ref[1:2, :] * lag1 + w_ref[2:3, :] * cur


def _conv3_t(w_ref, cur, next8):
    return w_ref[2:3, :] * cur + w_ref[1:2, :] * _up(cur, next8, 1) + w_ref[0:1, :] * _up(cur, next8, 2)


def _spec_cur(tr, tc, c0):
    return pl.BlockSpec((tr, tc), lambda j, i: (i, c0 + j))


def _spec_prev(tr, tc, c0):
    return pl.BlockSpec((SUB, tc), lambda j, i: (jnp.maximum(i * (tr // SUB) - 1, 0), c0 + j))


def _spec_next(tr, tc, c0, s):
    return pl.BlockSpec((SUB, tc), lambda j, i: (jnp.minimum((i + 1) * (tr // SUB), s // SUB - 1), c0 + j))


def _spec_w(tc, c0):
    return pl.BlockSpec((SUB, tc), lambda j, i: (0, c0 + j))


def _pad8(w):
    return jnp.pad(w, ((0, SUB - w.shape[0]), (0, 0)))


def _conv_mix_fwd(z_a, cw8, conv):
    s = z_a.shape[0]
    tr = _pick(s, ROW_TILE, 16)
    tc = _pick(conv, COL_TILE, LANE)
    nc = conv // tc

    def body(zb_ref, zc_ref, zv_ref, zcp_ref, zvp_ref, w_ref, p_ref):
        i = pl.program_id(1)
        cv = zc_ref[...] * zv_ref[...]
        cvp = jnp.where(i > 0, zcp_ref[...] * zvp_ref[...], 0.0)
        p_ref[...] = (zb_ref[...] * _conv3(w_ref, cv, cvp)).astype(BF)

    return pl.pallas_call(
        body, name="conv_mix_fwd", grid=(nc, s // tr),
        in_specs=[_spec_cur(tr, tc, 0), _spec_cur(tr, tc, nc), _spec_cur(tr, tc, 2 * nc),
                  _spec_prev(tr, tc, nc), _spec_prev(tr, tc, 2 * nc), _spec_w(tc, 0)],
        out_specs=_spec_cur(tr, tc, 0),
        out_shape=jax.ShapeDtypeStruct((s, conv), BF),
        compiler_params=_cp("parallel", "parallel"),
    )(z_a, z_a, z_a, z_a, z_a, cw8)


def _conv_mix_bwd(z_a, d_p, cw8, conv):
    s = z_a.shape[0]
    tr = _pick(s, ROW_TILE_BWD, 16)
    tc = _pick(conv, COL_TILE, LANE)
    nc = conv // tc
    nr = s // tr

    def body(zb_ref, zbn_ref, zc_ref, zcp_ref, zv_ref, zvp_ref, dp_ref, dpn_ref, w_ref,
             dzb_ref, dzc_ref, dzv_ref, dw_ref):
        i = pl.program_id(1)
        zc = zc_ref[...]
        zv = zv_ref[...]
        cv = zc * zv
        cvp = jnp.where(i > 0, zcp_ref[...] * zvp_ref[...], 0.0)
        cv1, cv2 = _lags(cv, cvp)
        dpv = dp_ref[...]
        dzb_ref[...] = (dpv * _conv3(w_ref, cv, cvp, (cv1, cv2))).astype(BF)
        dcc = dpv * zb_ref[...]
        dccn = jnp.where(i < nr - 1, dpn_ref[...] * zbn_ref[...], 0.0)
        dcv = _conv3_t(w_ref, dcc, dccn)
        dzc_ref[...] = (dcv * zv).astype(BF)
        dzv_ref[...] = (dcv * zc).astype(BF)

        @pl.when(i == 0)
        def _():
            dw_ref[...] = jnp.zeros_like(dw_ref)

        dw_ref[...] += _rows8([jnp.sum(dcc * cv2, axis=0, keepdims=True),
                               jnp.sum(dcc * cv1, axis=0, keepdims=True),
                               jnp.sum(dcc * cv, axis=0, keepdims=True)], tc)

    out = jax.ShapeDtypeStruct((s, conv), BF)
    return pl.pallas_call(
        body, name="conv_mix_bwd", grid=(nc, nr),
        in_specs=[_spec_cur(tr, tc, 0), _spec_next(tr, tc, 0, s),
                  _spec_cur(tr, tc, nc), _spec_prev(tr, tc, nc),
                  _spec_cur(tr, tc, 2 * nc), _spec_prev(tr, tc, 2 * nc),
                  _spec_cur(tr, tc, 0), _spec_next(tr, tc, 0, s), _spec_w(tc, 0)],
        out_specs=[_spec_cur(tr, tc, 0), _spec_cur(tr, tc, 0), _spec_cur(tr, tc, 0), _spec_w(tc, 0)],
        out_shape=[out, out, out, jax.ShapeDtypeStruct((SUB, conv), F32)],
        compiler_params=_cp("parallel", "arbitrary"),
    )(z_a, z_a, z_a, z_a, z_a, z_a, d_p, d_p, cw8)


def _silu_parts(ag):
    sg = jax.nn.sigmoid(ag)
    return ag * sg, sg


def _ffn_act_fwd(a_pre, cw8, cb, dff):
    s = a_pre.shape[0]
    tr = _pick(s, ROW_TILE, 16)
    tc = _pick(dff, COL_TILE, LANE)
    nc = dff // tc

    def body(xg_ref, xgp_ref, xu_ref, xup_ref, wg_ref, wu_ref, bg_ref, bu_ref, f_ref):
        i = pl.program_id(1)
        xgp = jnp.where(i > 0, xgp_ref[...], 0.0)
        xup = jnp.where(i > 0, xup_ref[...], 0.0)
        ag = _conv3(wg_ref, xg_ref[...], xgp) + bg_ref[...]
        au = _conv3(wu_ref, xu_ref[...], xup) + bu_ref[...]
        f_ref[...] = (_silu_parts(ag)[0] * au).astype(BF)

    return pl.pallas_call(
        body, name="ffn_act_fwd", grid=(nc, s // tr),
        in_specs=[_spec_cur(tr, tc, 0), _spec_prev(tr, tc, 0), _spec_cur(tr, tc, nc), _spec_prev(tr, tc, nc),
                  _spec_w(tc, 0), _spec_w(tc, nc),
                  pl.BlockSpec((1, tc), lambda j, i: (0, j)), pl.BlockSpec((1, tc), lambda j, i: (0, nc + j))],
        out_specs=_spec_cur(tr, tc, 0),
        out_shape=jax.ShapeDtypeStruct((s, dff), BF),
        compiler_params=_cp("parallel", "parallel"),
    )(a_pre, a_pre, a_pre, a_pre, cw8, cw8, cb, cb)


def _ffn_act_bwd(a_pre, d_f, cw8, cb, dff):
    s = a_pre.shape[0]
    tr = _pick(s, ROW_TILE_BWD, 16)
    tc = _pick(dff, COL_TILE, LANE)
    nc = dff // tc
    nr = s // tr

    def body(xg_ref, xgp_ref, xgn_ref, xu_ref, xup_ref, xun_ref, df_ref, dfn_ref,
             wg_ref, wu_ref, bg_ref, bu_ref, dxg_ref, dxu_ref, dwg_ref, dwu_ref):
        i = pl.program_id(1)
        xg = xg_ref[...]
        xu = xu_ref[...]
        xgp = jnp.where(i > 0, xgp_ref[...], 0.0)
        xup = jnp.where(i > 0, xup_ref[...], 0.0)

        def d_act(xg_t, xgp_t, xu_t, xup_t, df_t, lags_g=None, lags_u=None):
            ag = _conv3(wg_ref, xg_t, xgp_t, lags_g) + bg_ref[...]
            au = _conv3(wu_ref, xu_t, xup_t, lags_u) + bu_ref[...]
            sil, sg = _silu_parts(ag)
            return df_t * au * (sg * (1.0 + ag * (1.0 - sg))), df_t * sil

        lags_g = _lags(xg, xgp)
        lags_u = _lags(xu, xup)
        dag, dau = d_act(xg, xgp, xu, xup, df_ref[...], lags_g, lags_u)
        dfn = jnp.where(i < nr - 1, dfn_ref[...], 0.0)
        dagn, daun = d_act(xgn_ref[...], xg[tr - SUB:], xun_ref[...], xu[tr - SUB:], dfn)
        dxg_ref[...] = _conv3_t(wg_ref, dag, dagn).astype(BF)
        dxu_ref[...] = _conv3_t(wu_ref, dau, daun).astype(BF)

        @pl.when(i == 0)
        def _():
            dwg_ref[...] = jnp.zeros_like(dwg_ref)
            dwu_ref[...] = jnp.zeros_like(dwu_ref)

        def wgrad(da, x, lags):
            return _rows8([jnp.sum(da * lags[1], axis=0, keepdims=True),
                           jnp.sum(da * lags[0], axis=0, keepdims=True),
                           jnp.sum(da * x, axis=0, keepdims=True),
                           jnp.sum(da, axis=0, keepdims=True)], tc)

        dwg_ref[...] += wgrad(dag, xg, lags_g)
        dwu_ref[...] += wgrad(dau, xu, lags_u)

    half = jax.ShapeDtypeStruct((s, dff), BF)
    wsh = jax.ShapeDtypeStruct((SUB, dff), F32)
    return pl.pallas_call(
        body, name="ffn_act_bwd", grid=(nc, nr),
        in_specs=[_spec_cur(tr, tc, 0), _spec_prev(tr, tc, 0), _spec_next(tr, tc, 0, s),
                  _spec_cur(tr, tc, nc), _spec_prev(tr, tc, nc), _spec_next(tr, tc, nc, s),
                  _spec_cur(tr, tc, 0), _spec_next(tr, tc, 0, s),
                  _spec_w(tc, 0), _spec_w(tc, nc),
                  pl.BlockSpec((1, tc), lambda j, i: (0, j)), pl.BlockSpec((1, tc), lambda j, i: (0, nc + j))],
        out_specs=[_spec_cur(tr, tc, 0), _spec_cur(tr, tc, 0), _spec_w(tc, 0), _spec_w(tc, 0)],
        out_shape=[half, half, wsh, wsh],
        compiler_params=_cp("parallel", "arbitrary"),
    )(a_pre, a_pre, a_pre, a_pre, a_pre, a_pre, d_f, d_f, cw8, cw8, cb, cb)


def _gate_fwd(z_ga, z_gb, b_gate, yc, ym, d):
    s = z_ga.shape[0]
    tr = _pick(s, ROW_TILE, 16)
    tc = _pick(d, COL_TILE, LANE)
    nc = d // tc

    def body(za_ref, zb_ref, ba_ref, bb_ref, yc_ref, ym_ref, o_ref):
        ga = jax.nn.sigmoid(za_ref[...] + ba_ref[...])
        gb = jax.nn.sigmoid(zb_ref[...] + bb_ref[...])
        o_ref[...] = (ga * yc_ref[...] + gb * ym_ref[...]).astype(BF)

    return pl.pallas_call(
        body, name="gate_fwd", grid=(nc, s // tr),
        in_specs=[_spec_cur(tr, tc, 0), _spec_cur(tr, tc, 0),
                  pl.BlockSpec((1, tc), lambda j, i: (0, j)), pl.BlockSpec((1, tc), lambda j, i: (0, nc + j)),
                  _spec_cur(tr, tc, 0), _spec_cur(tr, tc, 0)],
        out_specs=_spec_cur(tr, tc, 0),
        out_shape=jax.ShapeDtypeStruct((s, d), BF),
        compiler_params=_cp("parallel", "parallel"),
    )(z_ga, z_gb, b_gate, b_gate, yc, ym)


def _gate_bwd(d_mix, z_ga, z_gb, b_gate, yc, ym, d):
    s = z_ga.shape[0]
    tr = _pick(s, ROW_TILE, 16)
    tc = _pick(d, COL_TILE, LANE)
    nc = d // tc

    def body(dm_ref, za_ref, zb_ref, ba_ref, bb_ref, yc_ref, ym_ref,
             dza_ref, dzb_ref, dyc_ref, dym_ref, dba_ref, dbb_ref):
        i = pl.program_id(1)
        dm = dm_ref[...]
        ga = jax.nn.sigmoid(za_ref[...] + ba_ref[...])
        gb = jax.nn.sigmoid(zb_ref[...] + bb_ref[...])
        dza = dm * yc_ref[...] * (ga * (1.0 - ga))
        dzb = dm * ym_ref[...] * (gb * (1.0 - gb))
        dza_ref[...] = dza.astype(BF)
        dzb_ref[...] = dzb.astype(BF)
        dyc_ref[...] = (dm * ga).astype(BF)
        dym_ref[...] = (dm * gb).astype(BF)

        @pl.when(i == 0)
        def _():
            dba_ref[...] = jnp.zeros_like(dba_ref)
            dbb_ref[...] = jnp.zeros_like(dbb_ref)

        dba_ref[...] += _rows8([jnp.sum(dza, axis=0, keepdims=True)], tc)
        dbb_ref[...] += _rows8([jnp.sum(dzb, axis=0, keepdims=True)], tc)

    act = jax.ShapeDtypeStruct((s, d), BF)
    bsh = jax.ShapeDtypeStruct((SUB, d), F32)
    return pl.pallas_call(
        body, name="gate_bwd", grid=(nc, s // tr),
        in_specs=[_spec_cur(tr, tc, 0), _spec_cur(tr, tc, 0), _spec_cur(tr, tc, 0),
                  pl.BlockSpec((1, tc), lambda j, i: (0, j)), pl.BlockSpec((1, tc), lambda j, i: (0, nc + j)),
                  _spec_cur(tr, tc, 0), _spec_cur(tr, tc, 0)],
        out_specs=[_spec_cur(tr, tc, 0)] * 4 + [_spec_w(tc, 0)] * 2,
        out_shape=[act, act, act, act, bsh, bsh],
        compiler_params=_cp("parallel", "arbitrary"),
    )(d_mix, z_ga, z_gb, b_gate, b_gate, yc, ym)


def _lay(v):
    z = jnp.zeros(v.shape[:-1] + (HALF,), v.dtype)
    return jnp.concatenate([v[..., :HALF], z, v[..., HALF:], z], axis=-1)


def _unlay(v):
    return jnp.concatenate([v[..., :HALF], v[..., 2 * HALF:3 * HALF]], axis=-1)


def _lay_rows(v):
    z = jnp.zeros((HALF,) + v.shape[1:], v.dtype)
    return jnp.concatenate([v[:HALF], z, v[HALF:], z], axis=0)


def _rope_tables(positions):
    s = positions.shape[0]
    tr = _pick(s, ROW_TILE, 8)
    inv_freq = ROPE_THETA ** (-jnp.arange(0, ROPE, 2, dtype=F32) / ROPE)
    consts = jnp.stack([_lay(jnp.concatenate([inv_freq, inv_freq])),
                        _lay(jnp.ones((ROPE,), F32)),
                        _lay(jnp.concatenate([-jnp.ones((HALF,), F32), jnp.ones((HALF,), F32)]))])
    consts = _pad8(consts)

    def body(p_ref, c_ref, cos_ref, sin_ref):
        ang = p_ref[...].astype(F32) * c_ref[0:1, :]
        cos_ref[...] = jnp.cos(ang) * c_ref[1:2, :]
        sin_ref[...] = jnp.sin(ang) * c_ref[2:3, :]

    tab = jax.ShapeDtypeStruct((s, LANE), F32)
    return pl.pallas_call(
        body, name="rope_tables", grid=(s // tr,),
        in_specs=[pl.BlockSpec((tr, 1), lambda i: (i, 0)), pl.BlockSpec((SUB, LANE), lambda i: (0, 0))],
        out_specs=[pl.BlockSpec((tr, LANE), lambda i: (i, 0))] * 2,
        out_shape=[tab, tab],
        compiler_params=_cp("parallel"),
    )(positions, consts)


def _rope(t, cos, sin):
    return t * cos + pltpu.roll(t, 2 * HALF, axis=1) * sin


def _rope_t(d, cos, sin):
    return d * cos + pltpu.roll(d * sin, 2 * HALF, axis=1)


def _head_fwd(q_raw, kv_raw, z_a, kr_blk, cos, sin, gains, heads):
    s = q_raw.shape[0]
    tr = _pick(s, HEAD_ROW_TILE, 16)
    hw = heads * LANE

    def body(q_ref, kv_ref, kr_ref, cos_ref, sin_ref, g_ref, qo_ref, ko_ref, vo_ref):
        cosv = cos_ref[...]
        sinv = sin_ref[...]
        krv = kr_ref[...]
        kr_ss = jnp.sum(krv * krv, axis=-1, keepdims=True)
        for h in range(heads):
            lo = h * LANE
            qn = q_ref[:, lo:lo + LANE]
            qr = q_ref[:, hw + lo:hw + lo + LANE]
            ss = jnp.sum(qn * qn, axis=-1, keepdims=True) + jnp.sum(qr * qr, axis=-1, keepdims=True)
            r = lax.rsqrt(ss / HEAD_QK + NORM_EPS)
            qo_ref[:, 2 * lo:2 * lo + LANE] = ((qn * r) * g_ref[0:1, :]).astype(BF)
            qo_ref[:, 2 * lo + LANE:2 * lo + 2 * LANE] = _rope((qr * r) * g_ref[1:2, :], cosv, sinv).astype(BF)
            kn = kv_ref[:, 2 * lo:2 * lo + LANE]
            ss = jnp.sum(kn * kn, axis=-1, keepdims=True) + kr_ss
            r = lax.rsqrt(ss / HEAD_QK + NORM_EPS)
            ko_ref[:, 2 * lo:2 * lo + LANE] = ((kn * r) * g_ref[2:3, :]).astype(BF)
            ko_ref[:, 2 * lo + LANE:2 * lo + 2 * LANE] = _rope((krv * r) * g_ref[3:4, :], cosv, sinv).astype(BF)
            vo_ref[:, lo:lo + LANE] = kv_ref[:, 2 * lo + LANE:2 * lo + 2 * LANE].astype(BF)

    row = lambda w: pl.BlockSpec((tr, w), lambda i: (i, 0))
    return pl.pallas_call(
        body, name="head_fwd", grid=(s // tr,),
        in_specs=[row(2 * hw), row(2 * hw), pl.BlockSpec((tr, LANE), lambda i: (i, kr_blk)),
                  row(LANE), row(LANE), pl.BlockSpec((SUB, LANE), lambda i: (0, 0))],
        out_specs=[row(2 * hw), row(2 * hw), row(hw)],
        out_shape=[jax.ShapeDtypeStruct((s, 2 * hw), BF), jax.ShapeDtypeStruct((s, 2 * hw), BF),
                   jax.ShapeDtypeStruct((s, hw), BF)],
        compiler_params=_cp("parallel"),
    )(q_raw, kv_raw, z_a, cos, sin, gains)


def _head_bwd(q_raw, kv_raw, z_a, kr_blk, cos, sin, gains, dq_att, dk_att, dv, heads):
    s = q_raw.shape[0]
    tr = _pick(s, HEAD_ROW_TILE_BWD, 16)
    hw = heads * LANE

    def body(q_ref, kv_ref, kr_ref, cos_ref, sin_ref, g_ref, dq_ref, dk_ref, dv_ref,
             dqr_ref, dkv_ref, dkr_ref, dg_ref):
        i = pl.program_id(0)
        cosv = cos_ref[...]
        sinv = sin_ref[...]
        krv = kr_ref[...]
        kr_ss = jnp.sum(krv * krv, axis=-1, keepdims=True)
        dkr = jnp.zeros((tr, LANE), F32)
        dgs = [jnp.zeros((1, LANE), F32) for _ in range(4)]

        def norm_bwd(xn, xr, ss, dn_out, dr_out, gn, gr):
            r = lax.rsqrt(ss / HEAD_QK + NORM_EPS)
            nn = xn * r
            nr = xr * r
            dt = _rope_t(dr_out, cosv, sinv)
            dnn = dn_out * gn
            dnr = dt * gr
            mean = (jnp.sum(dnn * nn, axis=-1, keepdims=True) + jnp.sum(dnr * nr, axis=-1, keepdims=True)) / HEAD_QK
            return (r * (dnn - nn * mean), r * (dnr - nr * mean),
                    jnp.sum(dn_out * nn, axis=0, keepdims=True), jnp.sum(dt * nr, axis=0, keepdims=True))

        for h in range(heads):
            lo = h * LANE
            qn = q_ref[:, lo:lo + LANE]
            qr = q_ref[:, hw + lo:hw + lo + LANE]
            ss = jnp.sum(qn * qn, axis=-1, keepdims=True) + jnp.sum(qr * qr, axis=-1, keepdims=True)
            dxn, dxr, g0, g1 = norm_bwd(qn, qr, ss, dq_ref[:, 2 * lo:2 * lo + LANE],
                                        dq_ref[:, 2 * lo + LANE:2 * lo + 2 * LANE], g_ref[0:1, :], g_ref[1:2, :])
            dqr_ref[:, lo:lo + LANE] = dxn.astype(BF)
            dqr_ref[:, hw + lo:hw + lo + LANE] = dxr.astype(BF)
            kn = kv_ref[:, 2 * lo:2 * lo + LANE]
            ss = jnp.sum(kn * kn, axis=-1, keepdims=True) + kr_ss
            dxn, dxr, g2, g3 = norm_bwd(kn, krv, ss, dk_ref[:, 2 * lo:2 * lo + LANE],
                                        dk_ref[:, 2 * lo + LANE:2 * lo + 2 * LANE], g_ref[2:3, :], g_ref[3:4, :])
            dkv_ref[:, 2 * lo:2 * lo + LANE] = dxn.astype(BF)
            dkv_ref[:, 2 * lo + LANE:2 * lo + 2 * LANE] = dv_ref[:, lo:lo + LANE].astype(BF)
            dkr = dkr + dxr
            dgs = [a + b for a, b in zip(dgs, (g0, g1, g2, g3))]
        dkr_ref[...] = dkr

        @pl.when(i == 0)
        def _():
            dg_ref[...] = jnp.zeros_like(dg_ref)

        dg_ref[...] += _rows8(dgs, LANE)

    row = lambda w: pl.BlockSpec((tr, w), lambda i: (i, 0))
    return pl.pallas_call(
        body, name="head_bwd", grid=(s // tr,),
        in_specs=[row(2 * hw), row(2 * hw), pl.BlockSpec((tr, LANE), lambda i: (i, kr_blk)),
                  row(LANE), row(LANE), pl.BlockSpec((SUB, LANE), lambda i: (0, 0)),
                  row(2 * hw), row(2 * hw), row(hw)],
        out_specs=[row(2 * hw), row(2 * hw), row(LANE), pl.BlockSpec((SUB, LANE), lambda i: (0, 0))],
        out_shape=[jax.ShapeDtypeStruct((s, 2 * hw), BF), jax.ShapeDtypeStruct((s, 2 * hw), BF),
                   jax.ShapeDtypeStruct((s, LANE), F32), jax.ShapeDtypeStruct((SUB, LANE), F32)],
        compiler_params=_cp("arbitrary"),
    )(q_raw, kv_raw, z_a, cos, sin, gains, dq_att, dk_att, dv)


def _causal_mask(nrows, ncols, row0):
    rows = lax.broadcasted_iota(jnp.int32, (nrows, ncols), 0) + row0
    cols = lax.broadcasted_iota(jnp.int32, (nrows, ncols), 1)
    return cols <= rows


def _causal_steps(nt, q_major):
    pairs = ([(i, j) for i in range(nt) for j in range(i + 1)] if q_major
             else [(i, j) for j in range(nt) for i in range(j, nt)])
    return (jnp.array([p[0] for p in pairs], jnp.int32), jnp.array([p[1] for p in pairs], jnp.int32))


def _attn_fwd(q_att, k_att, v, heads):
    s = q_att.shape[0]
    t = _pick(s, ATTN_TILE_FWD, LANE)
    nt = s // t
    th = t
    scale = HEAD_QK ** -0.5
    qi, kj = _causal_steps(nt, True)

    def body(qi_ref, kj_ref, q_ref, k_ref, v_ref, o_ref, ob_ref, lse_ref, m_s, l_s, acc_s):
        st = pl.program_id(1)
        i = qi_ref[st]
        j = kj_ref[st]

        @pl.when(j == 0)
        def _():
            m_s[...] = jnp.full_like(m_s, NEG_INF)
            l_s[...] = jnp.zeros_like(l_s)
            acc_s[...] = jnp.zeros_like(acc_s)

        def step(masked):
            for r0 in range(0, t, th):
                rows = slice(r0, r0 + th)
                sc = lax.dot_general(q_ref[rows, :], k_ref[...], (((1,), (1,)), ((), ())),
                                     preferred_element_type=F32) * scale
                if masked:
                    sc = jnp.where(_causal_mask(th, t, r0), sc, NEG_INF)
                m_prev = m_s[rows, :]
                m_new = jnp.maximum(m_prev, jnp.max(sc, axis=-1, keepdims=True))
                alpha = jnp.exp(m_prev - m_new)
                p = jnp.exp(sc - jnp.tile(m_new, (1, t // LANE)))
                l_s[rows, :] = alpha * l_s[rows, :] + jnp.sum(p, axis=-1, keepdims=True)
                acc_s[rows, :] = alpha * acc_s[rows, :] + jnp.dot(p.astype(BF), v_ref[...],
                                                                  preferred_element_type=F32)
                m_s[rows, :] = m_new

        @pl.when(j < i)
        def _():
            step(False)

        @pl.when(j == i)
        def _():
            step(True)
            o = acc_s[...] / l_s[...]
            o_ref[...] = o
            ob_ref[...] = o.astype(BF)
            lse_ref[...] = (m_s[...] + jnp.log(l_s[...]))[:, 0:1]

    q_idx = lambda h, st, qi_r, kj_r: (qi_r[st], h)
    kv_idx = lambda h, st, qi_r, kj_r: (kj_r[st], h)
    return pl.pallas_call(
        body, name="attn_fwd",
        grid_spec=pltpu.PrefetchScalarGridSpec(
            num_scalar_prefetch=2, grid=(heads, qi.shape[0]),
            in_specs=[pl.BlockSpec((t, 2 * LANE), q_idx), pl.BlockSpec((t, 2 * LANE), kv_idx),
                      pl.BlockSpec((t, LANE), kv_idx)],
            out_specs=[pl.BlockSpec((t, LANE), q_idx), pl.BlockSpec((t, LANE), q_idx),
                       pl.BlockSpec((None, t, 1), lambda h, st, qi_r, kj_r: (h, qi_r[st], 0))],
            scratch_shapes=[pltpu.VMEM((t, LANE), F32), pltpu.VMEM((t, LANE), F32), pltpu.VMEM((t, LANE), F32)]),
        out_shape=[jax.ShapeDtypeStruct((s, heads * LANE), F32), jax.ShapeDtypeStruct((s, heads * LANE), BF),
                   jax.ShapeDtypeStruct((heads, s, 1), F32)],
        compiler_params=_cp("parallel", "arbitrary"),
    )(qi, kj, q_att, k_att, v)


def _attn_bwd(q_att, k_att, v, o, lse, d_o, heads, dep=None):
    s = q_att.shape[0]
    t = _pick(s, ATTN_TILE, LANE)
    nt = s // t
    scale = HEAD_QK ** -0.5
    qi, kj = _causal_steps(nt, False)

    def body(qi_ref, kj_ref, q_ref, k_ref, v_ref, do_ref, o_ref, lse_ref, *rest):
        dq_ref, dk_ref, dv_ref, dk_s, dv_s = rest[-5:]
        st = pl.program_id(1)
        i = qi_ref[st]
        j = kj_ref[st]

        @pl.when(st == 0)
        def _():
            dq_ref[...] = jnp.zeros_like(dq_ref)

        @pl.when(i == j)
        def _():
            dk_s[...] = jnp.zeros_like(dk_s)
            dv_s[...] = jnp.zeros_like(dv_s)

        def step(masked):
            q = q_ref[...]
            k = k_ref[...]
            do = do_ref[...]
            sc = lax.dot_general(q, k, (((1,), (1,)), ((), ())), preferred_element_type=F32) * scale
            if masked:
                sc = jnp.where(_causal_mask(t, t, 0), sc, NEG_INF)
            p = jnp.exp(sc - lse_ref[...])
            dp = lax.dot_general(do, v_ref[...], (((1,), (1,)), ((), ())), preferred_element_type=F32)
            delta = jnp.sum(do.astype(F32) * o_ref[...], axis=-1, keepdims=True)
            ds = (p * (dp - delta) * scale).astype(BF)
            dv_s[...] += lax.dot_general(p.astype(BF), do, (((0,), (0,)), ((), ())), preferred_element_type=F32)
            dk_s[...] += lax.dot_general(ds, q, (((0,), (0,)), ((), ())), preferred_element_type=F32)
            rows = pl.ds(pl.multiple_of(i * t, t), t)
            dq_ref[rows, :] += jnp.dot(ds, k, preferred_element_type=F32)

        @pl.when(i > j)
        def _():
            step(False)

        @pl.when(i == j)
        def _():
            step(True)

        @pl.when(i == nt - 1)
        def _():
            dk_ref[...] = dk_s[...]
            dv_ref[...] = dv_s[...]

    q_idx = lambda h, st, qi_r, kj_r: (qi_r[st], h)
    kv_idx = lambda h, st, qi_r, kj_r: (kj_r[st], h)
    in_specs = [pl.BlockSpec((t, 2 * LANE), q_idx), pl.BlockSpec((t, 2 * LANE), kv_idx),
                pl.BlockSpec((t, LANE), kv_idx), pl.BlockSpec((t, LANE), q_idx), pl.BlockSpec((t, LANE), q_idx),
                pl.BlockSpec((None, t, 1), lambda h, st, qi_r, kj_r: (h, qi_r[st], 0))]
    args = [q_att, k_att, v, d_o, o, lse]
    if dep is not None:
        in_specs.append(ANY)
        args.append(dep)
    return pl.pallas_call(
        body, name="attn_bwd",
        grid_spec=pltpu.PrefetchScalarGridSpec(
            num_scalar_prefetch=2, grid=(heads, qi.shape[0]),
            in_specs=in_specs,
            out_specs=[pl.BlockSpec((s, 2 * LANE), lambda h, st, qi_r, kj_r: (0, h)),
                       pl.BlockSpec((t, 2 * LANE), kv_idx), pl.BlockSpec((t, LANE), kv_idx)],
            scratch_shapes=[pltpu.VMEM((t, 2 * LANE), F32), pltpu.VMEM((t, LANE), F32)]),
        out_shape=[jax.ShapeDtypeStruct((s, heads * 2 * LANE), F32),
                   jax.ShapeDtypeStruct((s, heads * 2 * LANE), F32),
                   jax.ShapeDtypeStruct((s, heads * LANE), F32)],
        compiler_params=_cp("parallel", "arbitrary"),
    )(qi, kj, *args)


def _loss_head(y, target):
    s, d = y.shape
    tr = _pick(s, ROW_TILE, 8)

    def body(y_ref, t_ref, dy_ref, dyb_ref, l_ref):
        i = pl.program_id(0)
        e = y_ref[...] - t_ref[...]
        dy_ref[...] = e / d
        dyb_ref[...] = (e / d).astype(BF)

        @pl.when(i == 0)
        def _():
            l_ref[...] = jnp.zeros_like(l_ref)

        l_ref[...] += 0.5 * jnp.sum(jnp.mean(e * e, axis=-1, keepdims=True), axis=0, keepdims=True)

    return pl.pallas_call(
        body, name="loss_head", grid=(s // tr,),
        in_specs=[pl.BlockSpec((tr, d), lambda i: (i, 0))] * 2,
        out_specs=[pl.BlockSpec((tr, d), lambda i: (i, 0)), pl.BlockSpec((tr, d), lambda i: (i, 0)),
                   pl.BlockSpec((SUB, LANE), lambda i: (0, 0))],
        out_shape=[jax.ShapeDtypeStruct((s, d), F32), jax.ShapeDtypeStruct((s, d), BF),
                   jax.ShapeDtypeStruct((SUB, LANE), F32)],
        compiler_params=_cp("arbitrary"),
    )(y, target)


def _sum_parts(parts, name):
    n, r, c = parts.shape
    tr = _pick(r, 512, 8)

    def body(p_ref, o_ref):
        g = p_ref[0].astype(F32)
        for k in range(1, n):
            g = g + p_ref[k].astype(F32)
        o_ref[...] = g

    return pl.pallas_call(
        body, name=name, grid=(r // tr,),
        in_specs=[pl.BlockSpec((n, tr, c), lambda i: (0, i, 0))],
        out_specs=pl.BlockSpec((tr, c), lambda i: (i, 0)),
        out_shape=jax.ShapeDtypeStruct((r, c), F32),
        compiler_params=_cp("parallel"),
    )(parts)


def _adamw(parts, w, m, v, name, by_cols=False):
    n, r, c = parts.shape
    tr, tc = (r, _pick(c, 256, LANE)) if by_cols else (_pick(r, 256, 16 if r % 16 == 0 else 8), c)

    def body(p_ref, w_ref, m_ref, v_ref, g_ref, d_ref, mo_ref, vo_ref):
        g = p_ref[0].astype(F32)
        for k in range(1, n):
            g = g + p_ref[k].astype(F32)
        m_new = ADAM_B1 * m_ref[...] + (1.0 - ADAM_B1) * g
        v_new = ADAM_B2 * v_ref[...] + (1.0 - ADAM_B2) * jnp.square(g)
        m_hat = m_new / (1.0 - ADAM_B1 ** ADAM_STEP)
        v_hat = v_new / (1.0 - ADAM_B2 ** ADAM_STEP)
        g_ref[...] = g
        d_ref[...] = -ADAM_LR * (m_hat / (jnp.sqrt(v_hat) + ADAM_EPS) + ADAM_WD * w_ref[...])
        mo_ref[...] = m_new
        vo_ref[...] = v_new

    idx = (lambda i: (0, i)) if by_cols else (lambda i: (i, 0))
    spec = pl.BlockSpec((tr, tc), idx)
    sh = jax.ShapeDtypeStruct((r, c), F32)
    return pl.pallas_call(
        body, name=name, grid=(c // tc if by_cols else r // tr,),
        in_specs=[pl.BlockSpec((n, tr, tc), lambda i: (0,) + idx(i)), spec, spec, spec],
        out_specs=[spec] * 4, out_shape=[sh] * 4,
        compiler_params=_cp("parallel"),
    )(parts, w, m, v)


def _place():
    x, y, c = lax.axis_index("x"), lax.axis_index("y"), lax.axis_index("c")
    chips = [(1 - x, y), (x, 1 - y), (1 - x, 1 - y)]
    return x, y, c, chips


def _all_gather(shards, name, dep=None):
    n = len(shards)
    deps = [] if dep is None else list(dep)

    def body(*refs):
        ins, outs = refs[:n], refs[n + len(deps):2 * n + len(deps)]
        send_sems, recv_sems, local_sems = refs[2 * n + len(deps):]
        x, y, c, chips = _place()
        me, sibling = (x, y, c), (x, y, 1 - c)

        def slot(w, p):
            return outs[w].at[4 * p[0] + 2 * p[1] + p[2]]

        def copy(w, k, block, to, src=None):
            return pltpu.make_async_remote_copy(
                src_ref=slot(w, block) if src is None else src, dst_ref=slot(w, block),
                send_sem=send_sems.at[w, k], recv_sem=recv_sems.at[w, k], device_id=to, device_id_type=MESH)

        first = []
        for w in range(n):
            first += [copy(w, 1 + j, me, (*chip, c), src=ins[w]) for j, chip in enumerate(chips)]
            first.append(copy(w, 0, me, sibling, src=ins[w]))
        for cp in first:
            cp.start()
        mine = [pltpu.make_async_copy(ins[w], slot(w, me), local_sems.at[w]) for w in range(n)]
        for cp in mine:
            cp.start()
        passed = []
        for w in range(n):
            for j, chip in enumerate(chips):
                copy(w, 1 + j, (*chip, c), me).wait_recv()
                cp = copy(w, 4 + j, (*chip, c), sibling)
                cp.start()
                passed.append(cp)
        for w in range(n):
            copy(w, 0, sibling, me).wait_recv()
            for j, chip in enumerate(chips):
                copy(w, 4 + j, (*chip, 1 - c), me).wait_recv()
        for cp in first + passed:
            cp.wait_send()
        for cp in mine:
            cp.wait()

    return pl.pallas_call(
        body, name=name,
        in_specs=[ANY] * (n + len(deps)), out_specs=[ANY] * n,
        out_shape=[jax.ShapeDtypeStruct((N_DEV,) + a.shape, a.dtype) for a in shards],
        scratch_shapes=[pltpu.SemaphoreType.DMA((n, 7)), pltpu.SemaphoreType.DMA((n, 7)),
                        pltpu.SemaphoreType.DMA((n,))],
    )(*shards, *deps)


HBM = pl.BlockSpec(memory_space=pltpu.HBM)
SEM = pl.BlockSpec(memory_space=pltpu.SEMAPHORE)
EFFECT = pltpu.SideEffectType.DATAFLOW_SIDE_EFFECTING
PEERS = [(dx, dy, dc) for dx in (1, 0) for dy in (1, 0) for dc in (0, 1) if (dx, dy, dc) != (0, 0, 0)]


def _peer(x, y, c, flip):
    dx, dy, dc = flip
    return (1 - x if dx else x, 1 - y if dy else y, 1 - c if dc else c)


def _exchange_copies(srcs, lands, send, recv, loc, gather):
    x, y, c, _ = _place()
    me = 4 * x + 2 * y + c
    remote, local = [], []
    for w in range(len(srcs)):
        for k, flip in enumerate(PEERS):
            px, py, pc = _peer(x, y, c, flip)
            src = srcs[w] if gather else srcs[w].at[4 * px + 2 * py + pc]
            remote.append(pltpu.make_async_remote_copy(
                src_ref=src, dst_ref=lands[w].at[me], send_sem=send[w].at[k], recv_sem=recv[w].at[k],
                device_id=(px, py, pc), device_id_type=MESH))
        local.append(pltpu.make_async_copy(srcs[w] if gather else srcs[w].at[me], lands[w].at[me], loc[w]))
    return remote, local


class _Exchange:
    def __init__(self, srcs, lands, send, recv, loc, token, gather):
        self.srcs, self.lands, self.send, self.recv, self.loc = srcs, lands, send, recv, loc
        self.token, self.gather = token, gather


def _exchange_start(srcs, gather, name):
    n = len(srcs)
    land_shapes = [((N_DEV,) + a.shape) if gather else a.shape for a in srcs]
    lands = [pltpu.with_memory_space_constraint(lax.empty(sh, a.dtype), pltpu.HBM) for sh, a in zip(land_shapes, srcs)]
    srcs = [pltpu.with_memory_space_constraint(a, pltpu.HBM) for a in srcs]

    def body(*refs):
        src_refs, land_refs = refs[:n], refs[n:2 * n]
        outs = refs[2 * n:]
        send, recv, loc = outs[:n], outs[n:2 * n], outs[2 * n:3 * n]
        token = outs[-1]
        remote, local = _exchange_copies(src_refs, land_refs, send, recv, loc, gather)
        for cp in remote + local:
            cp.start()
        token[...] = jnp.zeros_like(token)

    out_shape = ([pltpu.SemaphoreType.DMA((len(PEERS),))] * (2 * n) + [pltpu.SemaphoreType.DMA(())] * n
                 + [pltpu.HBM(a.shape, a.dtype) for a in srcs] + [pltpu.HBM(a.shape, a.dtype) for a in lands]
                 + [jax.ShapeDtypeStruct((SUB, LANE), F32)])
    res = pl.pallas_call(
        body, name=name, out_shape=out_shape,
        in_specs=[HBM] * (2 * n),
        out_specs=[SEM] * (3 * n) + [HBM] * (2 * n) + [pl.BlockSpec(memory_space=pltpu.VMEM)],
        input_output_aliases={i: 3 * n + i for i in range(2 * n)},
        compiler_params=pltpu.CompilerParams(has_side_effects=EFFECT),
    )(*srcs, *lands)
    return _Exchange(res[3 * n:4 * n], res[4 * n:5 * n], res[:n], res[n:2 * n], res[2 * n:3 * n], res[-1], gather)


def _exchange_wait(ex, idxs, after, name):
    n = len(idxs)
    srcs = [ex.srcs[i] for i in idxs]
    lands = [ex.lands[i] for i in idxs]
    sems = [ex.send[i] for i in idxs] + [ex.recv[i] for i in idxs] + [ex.loc[i] for i in idxs]
    gather = ex.gather

    def body(*refs):
        src_refs, land_refs = refs[:n], refs[n:2 * n]
        send, recv, loc = refs[2 * n:3 * n], refs[3 * n:4 * n], refs[4 * n:5 * n]
        remote, local = _exchange_copies(src_refs, land_refs, send, recv, loc, gather)
        for cp in remote:
            cp.wait_send()
            cp.wait_recv()
        for cp in local:
            cp.wait()

    res = pl.pallas_call(
        body, name=name,
        out_shape=[pltpu.HBM(a.shape, a.dtype) for a in srcs] + [pltpu.HBM(a.shape, a.dtype) for a in lands],
        in_specs=[HBM] * (2 * n) + [SEM] * (3 * n) + [ANY],
        out_specs=[HBM] * (2 * n),
        input_output_aliases={i: i for i in range(2 * n)},
        compiler_params=pltpu.CompilerParams(has_side_effects=EFFECT),
    )(*srcs, *lands, *sems, after)
    return res[n:]


def _after(token, a):
    return a + token[0:1, 0:1].astype(a.dtype)


def _unblock(w3):
    nb, k, nbw = w3.shape
    return w3.transpose(1, 0, 2).reshape(k, nb * nbw)


def _block(w, nb):
    k, n = w.shape
    return w.reshape(k, nb, n // nb).transpose(1, 0, 2)


def kernel(x, positions, ln1_g, w_in, b_gate, conv_w, w_conv_out, q_a_g, w_q_b, kv_a_g, w_kv_b, q_norm_g, k_norm_g, w_mla_out, w_o, ln2_g, w_ffn_up, ffn_conv_w, ffn_conv_b, w_ffn_down, loss_target, m_ln1_g, m_w_in, m_b_gate, m_conv_w, m_w_conv_out, m_q_a_g, m_w_q_b, m_kv_a_g, m_w_kv_b, m_q_norm_g, m_k_norm_g, m_w_mla_out, m_w_o, m_ln2_g, m_w_ffn_up, m_ffn_conv_w, m_ffn_conv_b, m_w_ffn_down, v_ln1_g, v_w_in, v_b_gate, v_conv_w, v_w_conv_out, v_q_a_g, v_w_q_b, v_kv_a_g, v_w_kv_b, v_q_norm_g, v_k_norm_g, v_w_mla_out, v_w_o, v_ln2_g, v_w_ffn_up, v_ffn_conv_w, v_ffn_conv_b, v_w_ffn_down):
    s, d = x.shape[1], x.shape[2]
    conv = conv_w.shape[2] * N_DEV
    ql, kvl = q_a_g.shape[1], kv_a_g.shape[1]
    heads = w_q_b.shape[2] * N_DEV // HEAD_QK
    dff = w_ffn_down.shape[1] * N_DEV
    hw = heads * LANE
    conv3 = 3 * conv
    kr_off = conv3 + ql
    kv_off = -(-(kr_off + LANE) // kvl) * kvl
    wa = kv_off + kvl
    assert conv3 % ql == 0 and kr_off % LANE == 0
    xs = x[0]
    tgt = loss_target[0]
    pos = positions.reshape(s, 1)

    nin = w_in.shape[2]
    big = dict(w_in=w_in[0].T, w_conv_out=w_conv_out[0], w_q_b=w_q_b[0], w_kv_b=w_kv_b[0],
               w_mla_out=w_mla_out[0], w_o=w_o[0], w_ffn_up=w_ffn_up[0], w_ffn_down=w_ffn_down[0])
    names = list(big)
    rest = names[1:]
    first = _all_gather([big["w_in"].astype(BF), _pad8(conv_w[0]), _pad8(ffn_conv_w[0])], "gather_w_in")
    cw8 = _unblock(first[1])
    fcw8 = _unblock(first[2])
    ag = _exchange_start([big[k].astype(BF) for k in rest], True, "gather_rest_start")

    def landed(keys, after, name):
        return _exchange_wait(ag, [rest.index(k) for k in keys], after, name)

    w_in_t = first[0].reshape(N_DEV * nin, d)
    g_off = kr_off + kvl + ROPE
    w_a_t = jnp.concatenate([w_in_t[:kr_off], _lay_rows(w_in_t[kr_off + kvl:g_off]),
                             jnp.zeros((kv_off - kr_off - LANE, d), BF), w_in_t[kr_off:kr_off + kvl]], axis=0)[None]
    w_ga_t = w_in_t[g_off:g_off + d][None]
    w_gb_t = w_in_t[g_off + d:g_off + 2 * d][None]
    gains = _pad8(jnp.concatenate([q_norm_g[:, :NOPE], _lay(q_norm_g[:, NOPE:]),
                                   k_norm_g[:, :NOPE], _lay(k_norm_g[:, NOPE:])], axis=0))
    kr_blk = kr_off // LANE

    cos, sin = _rope_tables(pos)
    u1 = _rms_fwd(xs, _after(ag.token, ln1_g), d, 0, "rms1_fwd")
    z_a = _mm_nt(u1, w_a_t, "mm_z_a")
    z_ga = _mm_nt(u1, w_ga_t, "mm_z_ga")
    z_gb = _mm_nt(u1, w_gb_t, "mm_z_gb")
    p = _conv_mix_fwd(z_a, cw8, conv)
    w_co, w_qb, w_kv = landed(["w_conv_out", "w_q_b", "w_kv_b"], p, "gather_wait_mixers")
    wq_full = _unblock(w_qb).reshape(ql, heads, HEAD_QK)
    w_q = jnp.concatenate([wq_full[:, :, :NOPE].reshape(ql, hw), _lay(wq_full[:, :, NOPE:]).reshape(ql, hw)],
                          axis=1)[None]
    yc = _mm_nn(p, w_co, "mm_y_conv")
    qn = _rms_fwd(z_a, q_a_g, ql, conv3 // ql, "rms_q_fwd")
    kvn = _rms_fwd(z_a, kv_a_g, kvl, kv_off // kvl, "rms_kv_fwd")
    q_raw = _mm_nn(qn, w_q, "mm_q")
    kv_raw = _mm_nn(kvn, w_kv, "mm_kv")
    q_att, k_att, v_bf = _head_fwd(q_raw, kv_raw, z_a, kr_blk, cos, sin, gains, heads)
    o, o_bf, lse = _attn_fwd(q_att, k_att, v_bf, heads)
    w_mo, w_oo = landed(["w_mla_out", "w_o"], lse, "gather_wait_outs")
    w_mo = w_mo.reshape(1, hw, d)
    w_oo = w_oo.reshape(1, d, d)
    ym = _mm_nn(o_bf, w_mo, "mm_y_mla")
    mix = _gate_fwd(z_ga, z_gb, b_gate, yc, ym, d)
    h1 = _mm_nn(mix, w_oo, "mm_h1", add=xs)
    u2 = _rms_fwd(h1, ln2_g, d, 0, "rms2_fwd")
    w_up, = landed(["w_ffn_up"], u2, "gather_wait_ffn_up")
    a_pre = _mm_nn(u2, w_up, "mm_ffn_up")
    f = _ffn_act_fwd(a_pre, fcw8, ffn_conv_b, dff)
    w_dn, = landed(["w_ffn_down"], f, "gather_wait_ffn_down")
    w_dn = w_dn.reshape(1, dff, d)
    y = _mm_nn(f, w_dn, "mm_ffn_down", add=h1)
    dy, dy_bf, loss_part = _loss_head(y, tgt)

    g_dn = _mm_tn(f, dy_bf, 1, "mm_g_ffn_down").reshape(N_DEV, dff // N_DEV, d)
    rs_dn = _exchange_start([g_dn], False, "reduce_ffn_down_start")
    d_f = _mm_nt(dy_bf, w_dn, "mm_d_f", dep=rs_dn.token)
    d_xg, d_xu, dfw_g, dfw_u = _ffn_act_bwd(a_pre, d_f, fcw8, ffn_conv_b, dff)
    half = N_DEV // 2
    g_up = _mm_tn(u2, d_xg, half, "mm_g_ffn_up_gate", into=lax.empty((N_DEV, d, 2 * dff // N_DEV), BF))
    g_up = _mm_tn(u2, d_xu, half, "mm_g_ffn_up_up", into=g_up, blk0=half)
    rs_up = _exchange_start([g_up], False, "reduce_ffn_up_start")
    d_u2 = _mm_nt(d_xg, w_up, "mm_d_u2_gate", blk0=0, nblk=half, dep=rs_up.token)
    d_u2 = _mm_nt(d_xu, w_up, "mm_d_u2_up", blk0=half, nblk=half, add=d_u2)
    d_h1, d_h1_bf, dg_ln2 = _rms_bwd(h1, d_u2, ln2_g, d, 0, "rms2_bwd", extra=dy, also_bf16=True)
    g_oo = _mm_tn(mix, d_h1_bf, 1, "mm_g_w_o").reshape(N_DEV, d // N_DEV, d)
    d_mix = _mm_nt(d_h1_bf, w_oo, "mm_d_mix")
    d_zga, d_zgb, d_yc, d_ym, dba, dbb = _gate_bwd(d_mix, z_ga, z_gb, b_gate, yc, ym, d)
    g_co = _mm_tn(p, d_yc, N_DEV, "mm_g_conv_out")
    g_mo = _mm_tn(o_bf, d_ym, 1, "mm_g_mla_out").reshape(N_DEV, hw // N_DEV, d)
    rs_mix = _exchange_start([g_oo, g_co, g_mo], False, "reduce_mixers_start")
    d_p = _mm_nt(d_yc, w_co, "mm_d_p", dep=rs_mix.token)
    d_o = _mm_nt(d_ym, w_mo, "mm_d_o", out_dtype=BF)
    d_zb, d_zc, d_zv, dcw = _conv_mix_bwd(z_a, d_p, cw8, conv)
    dq_att, dk_att, dv = _attn_bwd(q_att, k_att, v_bf, o, lse, d_o, heads, dep=rs_mix.token)
    d_q_raw, d_kv_raw, d_kr, dgains = _head_bwd(q_raw, kv_raw, z_a, kr_blk, cos, sin, gains, dq_att, dk_att, dv, heads)
    g_q2 = _mm_tn(qn, d_q_raw, 1, "mm_g_q")[0]
    g_qb = _block(jnp.concatenate([g_q2[:, :hw].reshape(ql, heads, NOPE),
                                   _unlay(g_q2[:, hw:].reshape(ql, heads, LANE))], axis=2).reshape(ql, heads * HEAD_QK), N_DEV)
    g_kv = _mm_tn(kvn, d_kv_raw, N_DEV, "mm_g_kv")
    rs_qkv = _exchange_start([g_qb, g_kv], False, "reduce_qkv_start")
    d_qn = _mm_nt(d_q_raw, w_q, "mm_d_qn", dep=rs_qkv.token)
    d_kvn = _mm_nt(d_kv_raw, w_kv, "mm_d_kvn")
    d_ql, dg_qa = _rms_bwd(z_a, d_qn, q_a_g, ql, conv3 // ql, "rms_q_bwd", out_dtype=BF)
    d_kvl, dg_kva = _rms_bwd(z_a, d_kvn, kv_a_g, kvl, kv_off // kvl, "rms_kv_bwd", out_dtype=BF)
    d_z_a = jnp.concatenate([d_zb, d_zc, d_zv, d_ql, d_kr.astype(BF), jnp.zeros((s, kv_off - kr_off - LANE), BF),
                             d_kvl], axis=1)
    g_a = _mm_tn(d_z_a, u1, 1, "mm_g_w_a")[0]
    g_ga = _mm_tn(d_zga, u1, 1, "mm_g_w_ga")[0]
    g_gb = _mm_tn(d_zgb, u1, 1, "mm_g_w_gb")[0]
    g_in = jnp.concatenate([g_a[:kr_off], g_a[kv_off:kv_off + kvl], g_a[kr_off:kr_off + HALF],
                            g_a[kr_off + 2 * HALF:kr_off + 3 * HALF], g_ga, g_gb], axis=0).reshape(N_DEV, nin, d)
    rs_in = _exchange_start([g_in], False, "reduce_w_in_start")
    d_u1 = _mm_nn(d_z_a, w_a_t, "mm_d_u1_a", dep=rs_in.token)
    d_u1 = _mm_nn(d_zga, w_ga_t, "mm_d_u1_ga", add=d_u1)
    d_u1 = _mm_nn(d_zgb, w_gb_t, "mm_d_u1_gb", add=d_u1)
    grad_x, dg_ln1 = _rms_bwd(xs, d_u1, ln1_g, d, 0, "rms1_bwd", extra=d_h1)

    summed = {}
    summed["w_ffn_down"], = _exchange_wait(rs_dn, [0], grad_x, "reduce_ffn_down_wait")
    summed["w_ffn_up"], = _exchange_wait(rs_up, [0], grad_x, "reduce_ffn_up_wait")
    summed["w_o"], summed["w_conv_out"], summed["w_mla_out"] = _exchange_wait(rs_mix, [0, 1, 2], grad_x, "reduce_mixers_wait")
    summed["w_q_b"], summed["w_kv_b"] = _exchange_wait(rs_qkv, [0, 1], grad_x, "reduce_qkv_wait")
    loc = locals()
    out = {}
    for k in rest:
        out[k] = _adamw(summed[k], big[k], loc["m_" + k][0], loc["v_" + k][0], "adamw_" + k)

    small = dict(ln1_g=dg_ln1[0:1], b_gate=jnp.concatenate([dba[0:1], dbb[0:1]], axis=1), q_a_g=dg_qa[0:1],
                 kv_a_g=dg_kva[0:1],
                 q_norm_g=jnp.concatenate([dgains[0:1], _unlay(dgains[1:2])], axis=1),
                 k_norm_g=jnp.concatenate([dgains[2:3], _unlay(dgains[3:4])], axis=1),
                 ln2_g=dg_ln2[0:1], ffn_conv_b=jnp.concatenate([dfw_g[3:4], dfw_u[3:4]], axis=1))
    small_names = list(small)
    extra = [dcw[0:3].reshape(1, -1), jnp.concatenate([dfw_g[0:3], dfw_u[0:3]], axis=1).reshape(1, -1),
             loss_part[0:1, 0:1]]
    flat = jnp.concatenate([small[k] for k in small_names] + extra, axis=1)
    n_flat = flat.shape[1]
    rows = -(-n_flat // (SUB * LANE)) * SUB
    flat = jnp.pad(flat, ((0, 0), (0, rows * LANE - n_flat))).reshape(rows, LANE)
    total = _sum_parts(_all_gather([flat], "gather_small", dep=[out[k][0] for k in rest])[0], "sum_small").reshape(1, rows * LANE)
    off = 0
    small_g = {}
    for k in small_names:
        small_g[k] = total[:, off:off + small[k].shape[1]]
        off += small[k].shape[1]
    me = 4 * lax.axis_index("x") + 2 * lax.axis_index("y") + lax.axis_index("c")
    cwn, fcwn = conv // N_DEV, 2 * dff // N_DEV
    g_cw = lax.dynamic_slice_in_dim(total[:, off:off + 3 * conv].reshape(3, conv), me * cwn, cwn, axis=1)
    off += 3 * conv
    g_fcw = lax.dynamic_slice_in_dim(total[:, off:off + 6 * dff].reshape(3, 2 * dff), me * fcwn, fcwn, axis=1)
    off += 6 * dff
    loss = total[0, off]

    summed["w_in"], = _exchange_wait(rs_in, [0], total, "reduce_w_in_wait")
    out["w_in"] = [r.T for r in _adamw(summed["w_in"], big["w_in"], m_w_in[0].T, v_w_in[0].T, "adamw_w_in",
                                       by_cols=True)]
    small_w = dict(ln1_g=ln1_g, b_gate=b_gate, q_a_g=q_a_g, kv_a_g=kv_a_g, q_norm_g=q_norm_g, k_norm_g=k_norm_g,
                   ln2_g=ln2_g, ffn_conv_b=ffn_conv_b, conv_w=conv_w[0].reshape(1, -1),
                   ffn_conv_w=ffn_conv_w[0].reshape(1, -1))
    small_g["conv_w"] = g_cw.reshape(1, -1)
    small_g["ffn_conv_w"] = g_fcw.reshape(1, -1)
    packed_names = list(small_w)

    def pack(get):
        vflat = jnp.concatenate([get(k).reshape(1, -1) for k in packed_names], axis=1)
        nr = -(-vflat.shape[1] // (SUB * LANE)) * SUB
        return jnp.pad(vflat, ((0, 0), (0, nr * LANE - vflat.shape[1])), constant_values=1.0).reshape(nr, LANE)

    res = _adamw(pack(lambda k: small_g[k])[None], pack(lambda k: small_w[k]), pack(lambda k: loc["m_" + k]),
                 pack(lambda k: loc["v_" + k]), "adamw_small")
    res = [r.reshape(1, -1) for r in res]
    off = 0
    for k in packed_names:
        shape = loc[k].shape
        size = small_w[k].shape[1]
        out[k] = [r[:, off:off + size].reshape(shape) for r in res]
        off += size
    for k in names:
        out[k] = [r[None] for r in out[k]]

    order = ["ln1_g", "w_in", "b_gate", "conv_w", "w_conv_out", "q_a_g", "w_q_b", "kv_a_g", "w_kv_b", "q_norm_g",
             "k_norm_g", "w_mla_out", "w_o", "ln2_g", "w_ffn_up", "ffn_conv_w", "ffn_conv_b", "w_ffn_down"]
    return (loss, grad_x[None], *[out[k][0] for k in order], *[out[k][1] for k in order],
            *[out[k][2] for k in order], *[out[k][3] for k in order])
```

```python
import functools

import jax
import jax.numpy as jnp
from jax import lax
from jax.experimental import pallas as pl
from jax.experimental.pallas import tpu as pltpu

BF = jnp.bfloat16
F32 = jnp.float32
MESH = pl.DeviceIdType.MESH
N_DEV = 8

NOPE = 128
ROPE = 64
HALF = ROPE // 2
HEAD_QK = NOPE + ROPE
HEAD_V = 128
LANE = 128
SUB = 8
NORM_EPS = 1e-6
NEG_INF = -1e30
ROPE_THETA = 10000.0
ADAM_LR = 0.001
ADAM_B1 = 0.9
ADAM_B2 = 0.999
ADAM_EPS = 1e-08
ADAM_WD = 0.01
ADAM_STEP = 10

VMEM_LIMIT = 52 * 1024 * 1024
MM_TM, MM_TN, MM_TK, MM_TS = 1024, 1536, 2048, 1024
ROW_TILE, ROW_TILE_BWD = 512, 256
HEAD_ROW_TILE, HEAD_ROW_TILE_BWD = 256, 128
COL_TILE = 512
ATTN_TILE = 1024
ATTN_TILE_FWD = 1024
ANY = pl.BlockSpec(memory_space=pl.ANY)


def _pick(n, target, mult):
    t = (min(n, target) // mult) * mult
    while t > 0:
        if n % t == 0:
            return t
        t -= mult
    raise ValueError(f"no tile for {n} (target {target}, multiple {mult})")


def _cp(*sem):
    return pltpu.CompilerParams(dimension_semantics=sem, vmem_limit_bytes=VMEM_LIMIT)


def _accumulate(kk, nk, acc, part, finish):
    if nk == 1:
        finish(part())
        return

    @pl.when(kk == 0)
    def _():
        acc[...] = part()

    @pl.when((kk > 0) & (kk < nk - 1))
    def _():
        acc[...] += part()

    @pl.when(kk == nk - 1)
    def _():
        finish(acc[...] + part())


def _mm_call(body, name, grid, in_specs, args, out_spec, out_shape, acc_shape, nk, dep):
    if dep is not None:
        in_specs = in_specs + [ANY]
        args = args + [dep]
    return pl.pallas_call(
        body, name=name, grid=grid, in_specs=in_specs, out_specs=out_spec, out_shape=out_shape,
        scratch_shapes=[pltpu.VMEM(acc_shape, F32)] if nk > 1 else [],
        compiler_params=_cp("parallel", "parallel", "arbitrary"),
    )(*args)


def _mm_nn(a, b3, name, add=None, out_dtype=F32, blk0=0, nblk=None, dep=None):
    m, k = a.shape
    nb_all, k2, nbw = b3.shape
    assert k == k2
    nblk = nb_all - blk0 if nblk is None else nblk
    n = nblk * nbw
    tm = _pick(m, MM_TM, 16)
    tn = _pick(nbw, MM_TN, LANE)
    tk = _pick(k, MM_TK, LANE)
    per = nbw // tn
    nk = k // tk

    def body(*refs):
        a_ref, b_ref = refs[:2]
        c_ref = refs[2] if add is not None else None
        o_ref = refs[2 + (add is not None) + (dep is not None)]
        acc = refs[-1]

        def part():
            return jnp.dot(a_ref[...].astype(BF), b_ref[...].astype(BF), preferred_element_type=F32)

        def finish(r):
            if add is not None:
                r = r + c_ref[...]
            o_ref[...] = r.astype(out_dtype)

        _accumulate(pl.program_id(2), nk, acc, part, finish)

    in_specs = [pl.BlockSpec((tm, tk), lambda i, j, kk: (i, kk)),
                pl.BlockSpec((None, tk, tn), lambda i, j, kk: (blk0 + j // per, kk, j % per))]
    args = [a, b3]
    if add is not None:
        in_specs.append(pl.BlockSpec((tm, tn), lambda i, j, kk: (i, j)))
        args.append(add)
    return _mm_call(body, name, (m // tm, n // tn, nk), in_specs, args,
                    pl.BlockSpec((tm, tn), lambda i, j, kk: (i, j)), jax.ShapeDtypeStruct((m, n), out_dtype),
                    (tm, tn), nk, dep)


def _mm_nt(a, b3, name, add=None, out_dtype=F32, blk0=0, nblk=None, dep=None):
    m, n = a.shape
    nb_all, k, nbw = b3.shape
    nblk = nb_all - blk0 if nblk is None else nblk
    assert n == nblk * nbw
    tm = _pick(m, MM_TM, 16)
    tn = _pick(k, MM_TN, LANE)
    tk = _pick(nbw, MM_TK, LANE)
    per = nbw // tk
    nk = n // tk

    def body(*refs):
        a_ref, b_ref = refs[:2]
        c_ref = refs[2] if add is not None else None
        o_ref = refs[2 + (add is not None) + (dep is not None)]
        acc = refs[-1]

        def part():
            return lax.dot_general(a_ref[...].astype(BF), b_ref[...].astype(BF),
                                   (((1,), (1,)), ((), ())), preferred_element_type=F32)

        def finish(r):
            if add is not None:
                r = r + c_ref[...]
            o_ref[...] = r.astype(out_dtype)

        _accumulate(pl.program_id(2), nk, acc, part, finish)

    in_specs = [pl.BlockSpec((tm, tk), lambda i, j, kk: (i, kk)),
                pl.BlockSpec((None, tn, tk), lambda i, j, kk: (blk0 + kk // per, j, kk % per))]
    args = [a, b3]
    if add is not None:
        in_specs.append(pl.BlockSpec((tm, tn), lambda i, j, kk: (i, j)))
        args.append(add)
    return _mm_call(body, name, (m // tm, k // tn, nk), in_specs, args,
                    pl.BlockSpec((tm, tn), lambda i, j, kk: (i, j)), jax.ShapeDtypeStruct((m, k), out_dtype),
                    (tm, tn), nk, dep)


def _mm_tn(a, b, nblk, name, out_dtype=BF, dep=None, into=None, blk0=0):
    s, m = a.shape
    s2, n = b.shape
    assert s == s2 and n % nblk == 0 and (dep is None or into is None)
    nbw = n // nblk
    tm = _pick(m, MM_TN, LANE)
    tn = _pick(nbw, MM_TN, LANE)
    ts = _pick(s, MM_TS, LANE)
    per = nbw // tn
    ns = s // ts

    def body(*refs):
        a_ref, b_ref = refs[:2]
        o_ref = refs[2 + (dep is not None or into is not None)]
        acc = refs[-1]

        def part():
            return lax.dot_general(a_ref[...].astype(BF), b_ref[...].astype(BF),
                                   (((0,), (0,)), ((), ())), preferred_element_type=F32)

        def finish(r):
            o_ref[...] = r.astype(out_dtype)

        _accumulate(pl.program_id(2), ns, acc, part, finish)

    in_specs = [pl.BlockSpec((ts, tm), lambda i, j, ss: (ss, i)),
                pl.BlockSpec((ts, tn), lambda i, j, ss: (ss, j))]
    out_spec = pl.BlockSpec((None, tm, tn), lambda i, j, ss: (blk0 + j // per, i, j % per))
    if into is None:
        return _mm_call(body, name, (m // tm, n // tn, ns), in_specs, [a, b], out_spec,
                        jax.ShapeDtypeStruct((nblk, m, nbw), out_dtype), (tm, tn), ns, dep)
    assert into.shape[1:] == (m, nbw) and into.dtype == out_dtype
    return pl.pallas_call(
        body, name=name, grid=(m // tm, n // tn, ns), in_specs=in_specs + [ANY], out_specs=out_spec,
        out_shape=jax.ShapeDtypeStruct(into.shape, out_dtype), input_output_aliases={2: 0},
        scratch_shapes=[pltpu.VMEM((tm, tn), F32)] if ns > 1 else [],
        compiler_params=_cp("parallel", "parallel", "arbitrary"),
    )(a, b, into)


def _rows8(rows, width):
    idx = lax.broadcasted_iota(jnp.int32, (SUB, width), 0)
    out = jnp.zeros((SUB, width), F32)
    for r, v in enumerate(rows):
        out = jnp.where(idx == r, v, out)
    return out


def _rms_fwd(x, g, width, col_blk, name):
    s = x.shape[0]
    tr = _pick(s, ROW_TILE, 16)

    def body(x_ref, g_ref, u_ref):
        xv = x_ref[...]
        r = lax.rsqrt(jnp.mean(xv * xv, axis=-1, keepdims=True) + NORM_EPS)
        u_ref[...] = ((xv * r) * g_ref[...]).astype(BF)

    return pl.pallas_call(
        body, name=name, grid=(s // tr,),
        in_specs=[pl.BlockSpec((tr, width), lambda i: (i, col_blk)),
                  pl.BlockSpec((1, width), lambda i: (0, 0))],
        out_specs=pl.BlockSpec((tr, width), lambda i: (i, 0)),
        out_shape=jax.ShapeDtypeStruct((s, width), BF),
        compiler_params=_cp("parallel"),
    )(x, g)


def _rms_bwd(x, du, g, width, col_blk, name, extra=None, out_dtype=F32, also_bf16=False):
    s = x.shape[0]
    tr = _pick(s, ROW_TILE_BWD, 16)

    def body(*refs):
        x_ref, du_ref, g_ref = refs[:3]
        e_ref = refs[3] if extra is not None else None
        dx_ref = refs[3 + (extra is not None)]
        dxb_ref = refs[4 + (extra is not None)] if also_bf16 else None
        dg_ref = refs[-1]
        i = pl.program_id(0)
        xv = x_ref[...]
        duv = du_ref[...].astype(F32)
        r = lax.rsqrt(jnp.mean(xv * xv, axis=-1, keepdims=True) + NORM_EPS)
        nv = xv * r
        dn = duv * g_ref[...]
        dx = r * (dn - nv * jnp.mean(dn * nv, axis=-1, keepdims=True))
        if extra is not None:
            dx = dx + e_ref[...]
        dx_ref[...] = dx.astype(out_dtype)
        if also_bf16:
            dxb_ref[...] = dx.astype(BF)

        @pl.when(i == 0)
        def _():
            dg_ref[...] = jnp.zeros_like(dg_ref)

        dg_ref[...] += _rows8([jnp.sum(duv * nv, axis=0, keepdims=True)], width)

    in_specs = [pl.BlockSpec((tr, width), lambda i: (i, col_blk)),
                pl.BlockSpec((tr, width), lambda i: (i, 0)),
                pl.BlockSpec((1, width), lambda i: (0, 0))]
    args = [x, du, g]
    if extra is not None:
        in_specs.append(pl.BlockSpec((tr, width), lambda i: (i, 0)))
        args.append(extra)
    return pl.pallas_call(
        body, name=name, grid=(s // tr,),
        in_specs=in_specs,
        out_specs=[pl.BlockSpec((tr, width), lambda i: (i, 0))] * (1 + also_bf16)
        + [pl.BlockSpec((SUB, width), lambda i: (0, 0))],
        out_shape=[jax.ShapeDtypeStruct((s, width), out_dtype)] + [jax.ShapeDtypeStruct((s, width), BF)] * also_bf16
        + [jax.ShapeDtypeStruct((SUB, width), F32)],
        compiler_params=_cp("arbitrary"),
    )(*args)


def _down(cur, prev8, k):
    ext = jnp.concatenate([prev8, cur], axis=0)
    return pltpu.roll(ext, k, axis=0)[SUB:]


def _up(cur, next8, k):
    ext = jnp.concatenate([cur, next8], axis=0)
    return pltpu.roll(ext, ext.shape[0] - k, axis=0)[:cur.shape[0]]


def _lags(cur, prev8):
    return _down(cur, prev8, 1), _down(cur, prev8, 2)


def _conv3(w_ref, cur, prev8, lags=None):
    lag1, lag2 = _lags(cur, prev8) if lags is None else lags
    return w_ref[0:1, :] * lag2 + w_ref[1:2, :] * lag1 + w_ref[2:3, :] * cur


def _conv3_t(w_ref, cur, next8):
    return w_ref[2:3, :] * cur + w_ref[1:2, :] * _up(cur, next8, 1) + w_ref[0:1, :] * _up(cur, next8, 2)


def _spec_cur(tr, tc, c0):
    return pl.BlockSpec((tr, tc), lambda j, i: (i, c0 + j))


def _spec_prev(tr, tc, c0):
    return pl.BlockSpec((SUB, tc), lambda j, i: (jnp.maximum(i * (tr // SUB) - 1, 0), c0 + j))


def _spec_next(tr, tc, c0, s):
    return pl.BlockSpec((SUB, tc), lambda j, i: (jnp.minimum((i + 1) * (tr // SUB), s // SUB - 1), c0 + j))


def _spec_w(tc, c0):
    return pl.BlockSpec((SUB, tc), lambda j, i: (0, c0 + j))


def _pad8(w):
    return jnp.pad(w, ((0, SUB - w.shape[0]), (0, 0)))


def _conv_mix_fwd(z_a, cw8, conv):
    s = z_a.shape[0]
    tr = _pick(s, ROW_TILE, 16)
    tc = _pick(conv, COL_TILE, LANE)
    nc = conv // tc

    def body(zb_ref, zc_ref, zv_ref, zcp_ref, zvp_ref, w_ref, p_ref):
        i = pl.program_id(1)
        cv = zc_ref[...] * zv_ref[...]
        cvp = jnp.where(i > 0, zcp_ref[...] * zvp_ref[...], 0.0)
        p_ref[...] = (zb_ref[...] * _conv3(w_ref, cv, cvp)).astype(BF)

    return pl.pallas_call(
        body, name="conv_mix_fwd", grid=(nc, s // tr),
        in_specs=[_spec_cur(tr, tc, 0), _spec_cur(tr, tc, nc), _spec_cur(tr, tc, 2 * nc),
                  _spec_prev(tr, tc, nc), _spec_prev(tr, tc, 2 * nc), _spec_w(tc, 0)],
        out_specs=_spec_cur(tr, tc, 0),
        out_shape=jax.ShapeDtypeStruct((s, conv), BF),
        compiler_params=_cp("parallel", "parallel"),
    )(z_a, z_a, z_a, z_a, z_a, cw8)


def _conv_mix_bwd(z_a, d_p, cw8, conv):
    s = z_a.shape[0]
    tr = _pick(s, ROW_TILE_BWD, 16)
    tc = _pick(conv, COL_TILE, LANE)
    nc = conv // tc
    nr = s // tr

    def body(zb_ref, zbn_ref, zc_ref, zcp_ref, zv_ref, zvp_ref, dp_ref, dpn_ref, w_ref,
             dzb_ref, dzc_ref, dzv_ref, dw_ref):
        i = pl.program_id(1)
        zc = zc_ref[...]
        zv = zv_ref[...]
        cv = zc * zv
        cvp = jnp.where(i > 0, zcp_ref[...] * zvp_ref[...], 0.0)
        cv1, cv2 = _lags(cv, cvp)
        dpv = dp_ref[...]
        dzb_ref[...] = (dpv * _conv3(w_ref, cv, cvp, (cv1, cv2))).astype(BF)
        dcc = dpv * zb_ref[...]
        dccn = jnp.where(i < nr - 1, dpn_ref[...] * zbn_ref[...], 0.0)
        dcv = _conv3_t(w_ref, dcc, dccn)
        dzc_ref[...] = (dcv * zv).astype(BF)
        dzv_ref[...] = (dcv * zc).astype(BF)

        @pl.when(i == 0)
        def _():
            dw_ref[...] = jnp.zeros_like(dw_ref)

        dw_ref[...] += _rows8([jnp.sum(dcc * cv2, axis=0, keepdims=True),
                               jnp.sum(dcc * cv1, axis=0, keepdims=True),
                               jnp.sum(dcc * cv, axis=0, keepdims=True)], tc)

    out = jax.ShapeDtypeStruct((s, conv), BF)
    return pl.pallas_call(
        body, name="conv_mix_bwd", grid=(nc, nr),
        in_specs=[_spec_cur(tr, tc, 0), _spec_next(tr, tc, 0, s),
                  _spec_cur(tr, tc, nc), _spec_prev(tr, tc, nc),
                  _spec_cur(tr, tc, 2 * nc), _spec_prev(tr, tc, 2 * nc),
                  _spec_cur(tr, tc, 0), _spec_next(tr, tc, 0, s), _spec_w(tc, 0)],
        out_specs=[_spec_cur(tr, tc, 0), _spec_cur(tr, tc, 0), _spec_cur(tr, tc, 0), _spec_w(tc, 0)],
        out_shape=[out, out, out, jax.ShapeDtypeStruct((SUB, conv), F32)],
        compiler_params=_cp("parallel", "arbitrary"),
    )(z_a, z_a, z_a, z_a, z_a, z_a, d_p, d_p, cw8)


def _silu_parts(ag):
    sg = jax.nn.sigmoid(ag)
    return ag * sg, sg


def _ffn_act_fwd(a_pre, cw8, cb, dff):
    s = a_pre.shape[0]
    tr = _pick(s, ROW_TILE, 16)
    tc = _pick(dff, COL_TILE, LANE)
    nc = dff // tc

    def body(xg_ref, xgp_ref, xu_ref, xup_ref, wg_ref, wu_ref, bg_ref, bu_ref, f_ref):
        i = pl.program_id(1)
        xgp = jnp.where(i > 0, xgp_ref[...], 0.0)
        xup = jnp.where(i > 0, xup_ref[...], 0.0)
        ag = _conv3(wg_ref, xg_ref[...], xgp) + bg_ref[...]
        au = _conv3(wu_ref, xu_ref[...], xup) + bu_ref[...]
        f_ref[...] = (_silu_parts(ag)[0] * au).astype(BF)

    return pl.pallas_call(
        body, name="ffn_act_fwd", grid=(nc, s // tr),
        in_specs=[_spec_cur(tr, tc, 0), _spec_prev(tr, tc, 0), _spec_cur(tr, tc, nc), _spec_prev(tr, tc, nc),
                  _spec_w(tc, 0), _spec_w(tc, nc),
                  pl.BlockSpec((1, tc), lambda j, i: (0, j)), pl.BlockSpec((1, tc), lambda j, i: (0, nc + j))],
        out_specs=_spec_cur(tr, tc, 0),
        out_shape=jax.ShapeDtypeStruct((s, dff), BF),
        compiler_params=_cp("parallel", "parallel"),
    )(a_pre, a_pre, a_pre, a_pre, cw8, cw8, cb, cb)


def _ffn_act_bwd(a_pre, d_f, cw8, cb, dff):
    s = a_pre.shape[0]
    tr = _pick(s, ROW_TILE_BWD, 16)
    tc = _pick(dff, COL_TILE, LANE)
    nc = dff // tc
    nr = s // tr

    def body(xg_ref, xgp_ref, xgn_ref, xu_ref, xup_ref, xun_ref, df_ref, dfn_ref,
             wg_ref, wu_ref, bg_ref, bu_ref, dxg_ref, dxu_ref, dwg_ref, dwu_ref):
        i = pl.program_id(1)
        xg = xg_ref[...]
        xu = xu_ref[...]
        xgp = jnp.where(i > 0, xgp_ref[...], 0.0)
        xup = jnp.where(i > 0, xup_ref[...], 0.0)

        def d_act(xg_t, xgp_t, xu_t, xup_t, df_t, lags_g=None, lags_u=None):
            ag = _conv3(wg_ref, xg_t, xgp_t, lags_g) + bg_ref[...]
            au = _conv3(wu_ref, xu_t, xup_t, lags_u) + bu_ref[...]
            sil, sg = _silu_parts(ag)
            return df_t * au * (sg * (1.0 + ag * (1.0 - sg))), df_t * sil

        lags_g = _lags(xg, xgp)
        lags_u = _lags(xu, xup)
        dag, dau = d_act(xg, xgp, xu, xup, df_ref[...], lags_g, lags_u)
        dfn = jnp.where(i < nr - 1, dfn_ref[...], 0.0)
        dagn, daun = d_act(xgn_ref[...], xg[tr - SUB:], xun_ref[...], xu[tr - SUB:], dfn)
        dxg_ref[...] = _conv3_t(wg_ref, dag, dagn).astype(BF)
        dxu_ref[...] = _conv3_t(wu_ref, dau, daun).astype(BF)

        @pl.when(i == 0)
        def _():
            dwg_ref[...] = jnp.zeros_like(dwg_ref)
            dwu_ref[...] = jnp.zeros_like(dwu_ref)

        def wgrad(da, x, lags):
            return _rows8([jnp.sum(da * lags[1], axis=0, keepdims=True),
                           jnp.sum(da * lags[0], axis=0, keepdims=True),
                           jnp.sum(da * x, axis=0, keepdims=True),
                           jnp.sum(da, axis=0, keepdims=True)], tc)

        dwg_ref[...] += wgrad(dag, xg, lags_g)
        dwu_ref[...] += wgrad(dau, xu, lags_u)

    half = jax.ShapeDtypeStruct((s, dff), BF)
    wsh = jax.ShapeDtypeStruct((SUB, dff), F32)
    return pl.pallas_call(
        body, name="ffn_act_bwd", grid=(nc, nr),
        in_specs=[_spec_cur(tr, tc, 0), _spec_prev(tr, tc, 0), _spec_next(tr, tc, 0, s),
                  _spec_cur(tr, tc, nc), _spec_prev(tr, tc, nc), _spec_next(tr, tc, nc, s),
                  _spec_cur(tr, tc, 0), _spec_next(tr, tc, 0, s),
                  _spec_w(tc, 0), _spec_w(tc, nc),
                  pl.BlockSpec((1, tc), lambda j, i: (0, j)), pl.BlockSpec((1, tc), lambda j, i: (0, nc + j))],
        out_specs=[_spec_cur(tr, tc, 0), _spec_cur(tr, tc, 0), _spec_w(tc, 0), _spec_w(tc, 0)],
        out_shape=[half, half, wsh, wsh],
        compiler_params=_cp("parallel", "arbitrary"),
    )(a_pre, a_pre, a_pre, a_pre, a_pre, a_pre, d_f, d_f, cw8, cw8, cb, cb)


def _gate_fwd(z_ga, z_gb, b_gate, yc, ym, d):
    s = z_ga.shape[0]
    tr = _pick(s, ROW_TILE, 16)
    tc = _pick(d, COL_TILE, LANE)
    nc = d // tc

    def body(za_ref, zb_ref, ba_ref, bb_ref, yc_ref, ym_ref, o_ref):
        ga = jax.nn.sigmoid(za_ref[...] + ba_ref[...])
        gb = jax.nn.sigmoid(zb_ref[...] + bb_ref[...])
        o_ref[...] = (ga * yc_ref[...] + gb * ym_ref[...]).astype(BF)

    return pl.pallas_call(
        body, name="gate_fwd", grid=(nc, s // tr),
        in_specs=[_spec_cur(tr, tc, 0), _spec_cur(tr, tc, 0),
                  pl.BlockSpec((1, tc), lambda j, i: (0, j)), pl.BlockSpec((1, tc), lambda j, i: (0, nc + j)),
                  _spec_cur(tr, tc, 0), _spec_cur(tr, tc, 0)],
        out_specs=_spec_cur(tr, tc, 0),
        out_shape=jax.ShapeDtypeStruct((s, d), BF),
        compiler_params=_cp("parallel", "parallel"),
    )(z_ga, z_gb, b_gate, b_gate, yc, ym)


def _gate_bwd(d_mix, z_ga, z_gb, b_gate, yc, ym, d):
    s = z_ga.shape[0]
    tr = _pick(s, ROW_TILE, 16)
    tc = _pick(d, COL_TILE, LANE)
    nc = d // tc

    def body(dm_ref, za_ref, zb_ref, ba_ref, bb_ref, yc_ref, ym_ref,
             dza_ref, dzb_ref, dyc_ref, dym_ref, dba_ref, dbb_ref):
        i = pl.program_id(1)
        dm = dm_ref[...]
        ga = jax.nn.sigmoid(za_ref[...] + ba_ref[...])
        gb = jax.nn.sigmoid(zb_ref[...] + bb_ref[...])
        dza = dm * yc_ref[...] * (ga * (1.0 - ga))
        dzb = dm * ym_ref[...] * (gb * (1.0 - gb))
        dza_ref[...] = dza.astype(BF)
        dzb_ref[...] = dzb.astype(BF)
        dyc_ref[...] = (dm * ga).astype(BF)
        dym_ref[...] = (dm * gb).astype(BF)

        @pl.when(i == 0)
        def _():
            dba_ref[...] = jnp.zeros_like(dba_ref)
            dbb_ref[...] = jnp.zeros_like(dbb_ref)

        dba_ref[...] += _rows8([jnp.sum(dza, axis=0, keepdims=True)], tc)
        dbb_ref[...] += _rows8([jnp.sum(dzb, axis=0, keepdims=True)], tc)

    act = jax.ShapeDtypeStruct((s, d), BF)
    bsh = jax.ShapeDtypeStruct((SUB, d), F32)
    return pl.pallas_call(
        body, name="gate_bwd", grid=(nc, s // tr),
        in_specs=[_spec_cur(tr, tc, 0), _spec_cur(tr, tc, 0), _spec_cur(tr, tc, 0),
                  pl.BlockSpec((1, tc), lambda j, i: (0, j)), pl.BlockSpec((1, tc), lambda j, i: (0, nc + j)),
                  _spec_cur(tr, tc, 0), _spec_cur(tr, tc, 0)],
        out_specs=[_spec_cur(tr, tc, 0)] * 4 + [_spec_w(tc, 0)] * 2,
        out_shape=[act, act, act, act, bsh, bsh],
        compiler_params=_cp("parallel", "arbitrary"),
    )(d_mix, z_ga, z_gb, b_gate, b_gate, yc, ym)


def _lay(v):
    z = jnp.zeros(v.shape[:-1] + (HALF,), v.dtype)
    return jnp.concatenate([v[..., :HALF], z, v[..., HALF:], z], axis=-1)


def _unlay(v):
    return jnp.concatenate([v[..., :HALF], v[..., 2 * HALF:3 * HALF]], axis=-1)


def _lay_rows(v):
    z = jnp.zeros((HALF,) + v.shape[1:], v.dtype)
    return jnp.concatenate([v[:HALF], z, v[HALF:], z], axis=0)


def _rope_tables(positions):
    s = positions.shape[0]
    tr = _pick(s, ROW_TILE, 8)
    inv_freq = ROPE_THETA ** (-jnp.arange(0, ROPE, 2, dtype=F32) / ROPE)
    consts = jnp.stack([_lay(jnp.concatenate([inv_freq, inv_freq])),
                        _lay(jnp.ones((ROPE,), F32)),
                        _lay(jnp.concatenate([-jnp.ones((HALF,), F32), jnp.ones((HALF,), F32)]))])
    consts = _pad8(consts)

    def body(p_ref, c_ref, cos_ref, sin_ref):
        ang = p_ref[...].astype(F32) * c_ref[0:1, :]
        cos_ref[...] = jnp.cos(ang) * c_ref[1:2, :]
        sin_ref[...] = jnp.sin(ang) * c_ref[2:3, :]

    tab = jax.ShapeDtypeStruct((s, LANE), F32)
    return pl.pallas_call(
        body, name="rope_tables", grid=(s // tr,),
        in_specs=[pl.BlockSpec((tr, 1), lambda i: (i, 0)), pl.BlockSpec((SUB, LANE), lambda i: (0, 0))],
        out_specs=[pl.BlockSpec((tr, LANE), lambda i: (i, 0))] * 2,
        out_shape=[tab, tab],
        compiler_params=_cp("parallel"),
    )(positions, consts)


def _rope(t, cos, sin):
    return t * cos + pltpu.roll(t, 2 * HALF, axis=1) * sin


def _rope_t(d, cos, sin):
    return d * cos + pltpu.roll(d * sin, 2 * HALF, axis=1)


def _head_fwd(q_raw, kv_raw, z_a, kr_blk, cos, sin, gains, heads):
    s = q_raw.shape[0]
    tr = _pick(s, HEAD_ROW_TILE, 16)
    hw = heads * LANE

    def body(q_ref, kv_ref, kr_ref, cos_ref, sin_ref, g_ref, qo_ref, ko_ref, vo_ref):
        cosv = cos_ref[...]
        sinv = sin_ref[...]
        krv = kr_ref[...]
        kr_ss = jnp.sum(krv * krv, axis=-1, keepdims=True)
        for h in range(heads):
            lo = h * LANE
            qn = q_ref[:, lo:lo + LANE]
            qr = q_ref[:, hw + lo:hw + lo + LANE]
            ss = jnp.sum(qn * qn, axis=-1, keepdims=True) + jnp.sum(qr * qr, axis=-1, keepdims=True)
            r = lax.rsqrt(ss / HEAD_QK + NORM_EPS)
            qo_ref[:, 2 * lo:2 * lo + LANE] = ((qn * r) * g_ref[0:1, :]).astype(BF)
            qo_ref[:, 2 * lo + LANE:2 * lo + 2 * LANE] = _rope((qr * r) * g_ref[1:2, :], cosv, sinv).astype(BF)
            kn = kv_ref[:, 2 * lo:2 * lo + LANE]
            ss = jnp.sum(kn * kn, axis=-1, keepdims=True) + kr_ss
            r = lax.rsqrt(ss / HEAD_QK + NORM_EPS)
            ko_ref[:, 2 * lo:2 * lo + LANE] = ((kn * r) * g_ref[2:3, :]).astype(BF)
            ko_ref[:, 2 * lo + LANE:2 * lo + 2 * LANE] = _rope((krv * r) * g_ref[3:4, :], cosv, sinv).astype(BF)
            vo_ref[:, lo:lo + LANE] = kv_ref[:, 2 * lo + LANE:2 * lo + 2 * LANE].astype(BF)

    row = lambda w: pl.BlockSpec((tr, w), lambda i: (i, 0))
    return pl.pallas_call(
        body, name="head_fwd", grid=(s // tr,),
        in_specs=[row(2 * hw), row(2 * hw), pl.BlockSpec((tr, LANE), lambda i: (i, kr_blk)),
                  row(LANE), row(LANE), pl.BlockSpec((SUB, LANE), lambda i: (0, 0))],
        out_specs=[row(2 * hw), row(2 * hw), row(hw)],
        out_shape=[jax.ShapeDtypeStruct((s, 2 * hw), BF), jax.ShapeDtypeStruct((s, 2 * hw), BF),
                   jax.ShapeDtypeStruct((s, hw), BF)],
        compiler_params=_cp("parallel"),
    )(q_raw, kv_raw, z_a, cos, sin, gains)


def _head_bwd(q_raw, kv_raw, z_a, kr_blk, cos, sin, gains, dq_att, dk_att, dv, heads):
    s = q_raw.shape[0]
    tr = _pick(s, HEAD_ROW_TILE_BWD, 16)
    hw = heads * LANE

    def body(q_ref, kv_ref, kr_ref, cos_ref, sin_ref, g_ref, dq_ref, dk_ref, dv_ref,
             dqr_ref, dkv_ref, dkr_ref, dg_ref):
        i = pl.program_id(0)
        cosv = cos_ref[...]
        sinv = sin_ref[...]
        krv = kr_ref[...]
        kr_ss = jnp.sum(krv * krv, axis=-1, keepdims=True)
        dkr = jnp.zeros((tr, LANE), F32)
        dgs = [jnp.zeros((1, LANE), F32) for _ in range(4)]

        def norm_bwd(xn, xr, ss, dn_out, dr_out, gn, gr):
            r = lax.rsqrt(ss / HEAD_QK + NORM_EPS)
            nn = xn * r
            nr = xr * r
            dt = _rope_t(dr_out, cosv, sinv)
            dnn = dn_out * gn
            dnr = dt * gr
            mean = (jnp.sum(dnn * nn, axis=-1, keepdims=True) + jnp.sum(dnr * nr, axis=-1, keepdims=True)) / HEAD_QK
            return (r * (dnn - nn * mean), r * (dnr - nr * mean),
                    jnp.sum(dn_out * nn, axis=0, keepdims=True), jnp.sum(dt * nr, axis=0, keepdims=True))

        for h in range(heads):
            lo = h * LANE
            qn = q_ref[:, lo:lo + LANE]
            qr = q_ref[:, hw + lo:hw + lo + LANE]
            ss = jnp.sum(qn * qn, axis=-1, keepdims=True) + jnp.sum(qr * qr, axis=-1, keepdims=True)
            dxn, dxr, g0, g1 = norm_bwd(qn, qr, ss, dq_ref[:, 2 * lo:2 * lo + LANE],
                                        dq_ref[:, 2 * lo + LANE:2 * lo + 2 * LANE], g_ref[0:1, :], g_ref[1:2, :])
            dqr_ref[:, lo:lo + LANE] = dxn.astype(BF)
            dqr_ref[:, hw + lo:hw + lo + LANE] = dxr.astype(BF)
            kn = kv_ref[:, 2 * lo:2 * lo + LANE]
            ss = jnp.sum(kn * kn, axis=-1, keepdims=True) + kr_ss
            dxn, dxr, g2, g3 = norm_bwd(kn, krv, ss, dk_ref[:, 2 * lo:2 * lo + LANE],
                                        dk_ref[:, 2 * lo + LANE:2 * lo + 2 * LANE], g_ref[2:3, :], g_ref[3:4, :])
            dkv_ref[:, 2 * lo:2 * lo + LANE] = dxn.astype(BF)
            dkv_ref[:, 2 * lo + LANE:2 * lo + 2 * LANE] = dv_ref[:, lo:lo + LANE].astype(BF)
            dkr = dkr + dxr
            dgs = [a + b for a, b in zip(dgs, (g0, g1, g2, g3))]
        dkr_ref[...] = dkr

        @pl.when(i == 0)
        def _():
            dg_ref[...] = jnp.zeros_like(dg_ref)

        dg_ref[...] += _rows8(dgs, LANE)

    row = lambda w: pl.BlockSpec((tr, w), lambda i: (i, 0))
    return pl.pallas_call(
        body, name="head_bwd", grid=(s // tr,),
        in_specs=[row(2 * hw), row(2 * hw), pl.BlockSpec((tr, LANE), lambda i: (i, kr_blk)),
                  row(LANE), row(LANE), pl.BlockSpec((SUB, LANE), lambda i: (0, 0)),
                  row(2 * hw), row(2 * hw), row(hw)],
        out_specs=[row(2 * hw), row(2 * hw), row(LANE), pl.BlockSpec((SUB, LANE), lambda i: (0, 0))],
        out_shape=[jax.ShapeDtypeStruct((s, 2 * hw), BF), jax.ShapeDtypeStruct((s, 2 * hw), BF),
                   jax.ShapeDtypeStruct((s, LANE), F32), jax.ShapeDtypeStruct((SUB, LANE), F32)],
        compiler_params=_cp("arbitrary"),
    )(q_raw, kv_raw, z_a, cos, sin, gains, dq_att, dk_att, dv)


def _causal_mask(nrows, ncols, row0):
    rows = lax.broadcasted_iota(jnp.int32, (nrows, ncols), 0) + row0
    cols = lax.broadcasted_iota(jnp.int32, (nrows, ncols), 1)
    return cols <= rows


def _causal_steps(nt, q_major):
    pairs = ([(i, j) for i in range(nt) for j in range(i + 1)] if q_major
             else [(i, j) for j in range(nt) for i in range(j, nt)])
    return (jnp.array([p[0] for p in pairs], jnp.int32), jnp.array([p[1] for p in pairs], jnp.int32))


def _attn_fwd(q_att, k_att, v, heads):
    s = q_att.shape[0]
    t = _pick(s, ATTN_TILE_FWD, LANE)
    nt = s // t
    th = t
    scale = HEAD_QK ** -0.5
    qi, kj = _causal_steps(nt, True)

    def body(qi_ref, kj_ref, q_ref, k_ref, v_ref, o_ref, ob_ref, lse_ref, m_s, l_s, acc_s):
        st = pl.program_id(1)
        i = qi_ref[st]
        j = kj_ref[st]

        @pl.when(j == 0)
        def _():
            m_s[...] = jnp.full_like(m_s, NEG_INF)
            l_s[...] = jnp.zeros_like(l_s)
            acc_s[...] = jnp.zeros_like(acc_s)

        def step(masked):
            for r0 in range(0, t, th):
                rows = slice(r0, r0 + th)
                sc = lax.dot_general(q_ref[rows, :], k_ref[...], (((1,), (1,)), ((), ())),
                                     preferred_element_type=F32) * scale
                if masked:
                    sc = jnp.where(_causal_mask(th, t, r0), sc, NEG_INF)
                m_prev = m_s[rows, :]
                m_new = jnp.maximum(m_prev, jnp.max(sc, axis=-1, keepdims=True))
                alpha = jnp.exp(m_prev - m_new)
                p = jnp.exp(sc - jnp.tile(m_new, (1, t // LANE)))
                l_s[rows, :] = alpha * l_s[rows, :] + jnp.sum(p, axis=-1, keepdims=True)
                acc_s[rows, :] = alpha * acc_s[rows, :] + jnp.dot(p.astype(BF), v_ref[...],
                                                                  preferred_element_type=F32)
                m_s[rows, :] = m_new

        @pl.when(j < i)
        def _():
            step(False)

        @pl.when(j == i)
        def _():
            step(True)
            o = acc_s[...] / l_s[...]
            o_ref[...] = o
            ob_ref[...] = o.astype(BF)
            lse_ref[...] = (m_s[...] + jnp.log(l_s[...]))[:, 0:1]

    q_idx = lambda h, st, qi_r, kj_r: (qi_r[st], h)
    kv_idx = lambda h, st, qi_r, kj_r: (kj_r[st], h)
    return pl.pallas_call(
        body, name="attn_fwd",
        grid_spec=pltpu.PrefetchScalarGridSpec(
            num_scalar_prefetch=2, grid=(heads, qi.shape[0]),
            in_specs=[pl.BlockSpec((t, 2 * LANE), q_idx), pl.BlockSpec((t, 2 * LANE), kv_idx),
                      pl.BlockSpec((t, LANE), kv_idx)],
            out_specs=[pl.BlockSpec((t, LANE), q_idx), pl.BlockSpec((t, LANE), q_idx),
                       pl.BlockSpec((None, t, 1), lambda h, st, qi_r, kj_r: (h, qi_r[st], 0))],
            scratch_shapes=[pltpu.VMEM((t, LANE), F32), pltpu.VMEM((t, LANE), F32), pltpu.VMEM((t, LANE), F32)]),
        out_shape=[jax.ShapeDtypeStruct((s, heads * LANE), F32), jax.ShapeDtypeStruct((s, heads * LANE), BF),
                   jax.ShapeDtypeStruct((heads, s, 1), F32)],
        compiler_params=_cp("parallel", "arbitrary"),
    )(qi, kj, q_att, k_att, v)


def _attn_bwd(q_att, k_att, v, o, lse, d_o, heads, dep=None):
    s = q_att.shape[0]
    t = _pick(s, ATTN_TILE, LANE)
    nt = s // t
    scale = HEAD_QK ** -0.5
    qi, kj = _causal_steps(nt, False)

    def body(qi_ref, kj_ref, q_ref, k_ref, v_ref, do_ref, o_ref, lse_ref, *rest):
        dq_ref, dk_ref, dv_ref, dk_s, dv_s = rest[-5:]
        st = pl.program_id(1)
        i = qi_ref[st]
        j = kj_ref[st]

        @pl.when(st == 0)
        def _():
            dq_ref[...] = jnp.zeros_like(dq_ref)

        @pl.when(i == j)
        def _():
            dk_s[...] = jnp.zeros_like(dk_s)
            dv_s[...] = jnp.zeros_like(dv_s)

        def step(masked):
            q = q_ref[...]
            k = k_ref[...]
            do = do_ref[...]
            sc = lax.dot_general(q, k, (((1,), (1,)), ((), ())), preferred_element_type=F32) * scale
            if masked:
                sc = jnp.where(_causal_mask(t, t, 0), sc, NEG_INF)
            p = jnp.exp(sc - lse_ref[...])
            dp = lax.dot_general(do, v_ref[...], (((1,), (1,)), ((), ())), preferred_element_type=F32)
            delta = jnp.sum(do.astype(F32) * o_ref[...], axis=-1, keepdims=True)
            ds = (p * (dp - delta) * scale).astype(BF)
            dv_s[...] += lax.dot_general(p.astype(BF), do, (((0,), (0,)), ((), ())), preferred_element_type=F32)
            dk_s[...] += lax.dot_general(ds, q, (((0,), (0,)), ((), ())), preferred_element_type=F32)
            rows = pl.ds(pl.multiple_of(i * t, t), t)
            dq_ref[rows, :] += jnp.dot(ds, k, preferred_element_type=F32)

        @pl.when(i > j)
        def _():
            step(False)

        @pl.when(i == j)
        def _():
            step(True)

        @pl.when(i == nt - 1)
        def _():
            dk_ref[...] = dk_s[...]
            dv_ref[...] = dv_s[...]

    q_idx = lambda h, st, qi_r, kj_r: (qi_r[st], h)
    kv_idx = lambda h, st, qi_r, kj_r: (kj_r[st], h)
    in_specs = [pl.BlockSpec((t, 2 * LANE), q_idx), pl.BlockSpec((t, 2 * LANE), kv_idx),
                pl.BlockSpec((t, LANE), kv_idx), pl.BlockSpec((t, LANE), q_idx), pl.BlockSpec((t, LANE), q_idx),
                pl.BlockSpec((None, t, 1), lambda h, st, qi_r, kj_r: (h, qi_r[st], 0))]
    args = [q_att, k_att, v, d_o, o, lse]
    if dep is not None:
        in_specs.append(ANY)
        args.append(dep)
    return pl.pallas_call(
        body, name="attn_bwd",
        grid_spec=pltpu.PrefetchScalarGridSpec(
            num_scalar_prefetch=2, grid=(heads, qi.shape[0]),
            in_specs=in_specs,
            out_specs=[pl.BlockSpec((s, 2 * LANE), lambda h, st, qi_r, kj_r: (0, h)),
                       pl.BlockSpec((t, 2 * LANE), kv_idx), pl.BlockSpec((t, LANE), kv_idx)],
            scratch_shapes=[pltpu.VMEM((t, 2 * LANE), F32), pltpu.VMEM((t, LANE), F32)]),
        out_shape=[jax.ShapeDtypeStruct((s, heads * 2 * LANE), F32),
                   jax.ShapeDtypeStruct((s, heads * 2 * LANE), F32),
                   jax.ShapeDtypeStruct((s, heads * LANE), F32)],
        compiler_params=_cp("parallel", "arbitrary"),
    )(qi, kj, *args)


def _loss_head(y, target):
    s, d = y.shape
    tr = _pick(s, ROW_TILE, 8)

    def body(y_ref, t_ref, dy_ref, dyb_ref, l_ref):
        i = pl.program_id(0)
        e = y_ref[...] - t_ref[...]
        dy_ref[...] = e / d
        dyb_ref[...] = (e / d).astype(BF)

        @pl.when(i == 0)
        def _():
            l_ref[...] = jnp.zeros_like(l_ref)

        l_ref[...] += 0.5 * jnp.sum(jnp.mean(e * e, axis=-1, keepdims=True), axis=0, keepdims=True)

    return pl.pallas_call(
        body, name="loss_head", grid=(s // tr,),
        in_specs=[pl.BlockSpec((tr, d), lambda i: (i, 0))] * 2,
        out_specs=[pl.BlockSpec((tr, d), lambda i: (i, 0)), pl.BlockSpec((tr, d), lambda i: (i, 0)),
                   pl.BlockSpec((SUB, LANE), lambda i: (0, 0))],
        out_shape=[jax.ShapeDtypeStruct((s, d), F32), jax.ShapeDtypeStruct((s, d), BF),
                   jax.ShapeDtypeStruct((SUB, LANE), F32)],
        compiler_params=_cp("arbitrary"),
    )(y, target)


def _sum_parts(parts, name):
    n, r, c = parts.shape
    tr = _pick(r, 512, 8)

    def body(p_ref, o_ref):
        g = p_ref[0].astype(F32)
        for k in range(1, n):
            g = g + p_ref[k].astype(F32)
        o_ref[...] = g

    return pl.pallas_call(
        body, name=name, grid=(r // tr,),
        in_specs=[pl.BlockSpec((n, tr, c), lambda i: (0, i, 0))],
        out_specs=pl.BlockSpec((tr, c), lambda i: (i, 0)),
        out_shape=jax.ShapeDtypeStruct((r, c), F32),
        compiler_params=_cp("parallel"),
    )(parts)


def _adamw(parts, w, m, v, name, by_cols=False):
    n, rp, c = parts.shape
    r = w.shape[0]
    assert by_cols or rp == r
    tr, tc = (r, _pick(c, 256, LANE)) if by_cols else (_pick(r, 256, 16 if r % 16 == 0 else 8), c)

    def body(p_ref, w_ref, m_ref, v_ref, g_ref, d_ref, mo_ref, vo_ref):
        g = p_ref[0].astype(F32)
        for k in range(1, n):
            g = g + p_ref[k].astype(F32)
        g = g[:r] if by_cols else g
        m_new = ADAM_B1 * m_ref[...] + (1.0 - ADAM_B1) * g
        v_new = ADAM_B2 * v_ref[...] + (1.0 - ADAM_B2) * jnp.square(g)
        m_hat = m_new / (1.0 - ADAM_B1 ** ADAM_STEP)
        v_hat = v_new / (1.0 - ADAM_B2 ** ADAM_STEP)
        g_ref[...] = g
        d_ref[...] = -ADAM_LR * (m_hat / (jnp.sqrt(v_hat) + ADAM_EPS) + ADAM_WD * w_ref[...])
        mo_ref[...] = m_new
        vo_ref[...] = v_new

    idx = (lambda i: (0, i)) if by_cols else (lambda i: (i, 0))
    spec = pl.BlockSpec((tr, tc), idx)
    sh = jax.ShapeDtypeStruct((r, c), F32)
    return pl.pallas_call(
        body, name=name, grid=(c // tc if by_cols else r // tr,),
        in_specs=[pl.BlockSpec((n, rp if by_cols else tr, tc), lambda i: (0,) + idx(i)), spec, spec, spec],
        out_specs=[spec] * 4, out_shape=[sh] * 4,
        compiler_params=_cp("parallel"),
    )(parts, w, m, v)


def _place():
    x, y, c = lax.axis_index("x"), lax.axis_index("y"), lax.axis_index("c")
    chips = [(1 - x, y), (x, 1 - y), (1 - x, 1 - y)]
    return x, y, c, chips


def _all_gather(shards, name, dep=None):
    n = len(shards)
    deps = [] if dep is None else list(dep)

    def body(*refs):
        ins, outs = refs[:n], refs[n + len(deps):2 * n + len(deps)]
        send_sems, recv_sems, local_sems = refs[2 * n + len(deps):]
        x, y, c, chips = _place()
        me, sibling = (x, y, c), (x, y, 1 - c)

        def slot(w, p):
            return outs[w].at[4 * p[0] + 2 * p[1] + p[2]]

        def copy(w, k, block, to, src=None):
            return pltpu.make_async_remote_copy(
                src_ref=slot(w, block) if src is None else src, dst_ref=slot(w, block),
                send_sem=send_sems.at[w, k], recv_sem=recv_sems.at[w, k], device_id=to, device_id_type=MESH)

        first = []
        for w in range(n):
            first += [copy(w, 1 + j, me, (*chip, c), src=ins[w]) for j, chip in enumerate(chips)]
            first.append(copy(w, 0, me, sibling, src=ins[w]))
        for cp in first:
            cp.start()
        mine = [pltpu.make_async_copy(ins[w], slot(w, me), local_sems.at[w]) for w in range(n)]
        for cp in mine:
            cp.start()
        passed = []
        for w in range(n):
            for j, chip in enumerate(chips):
                copy(w, 1 + j, (*chip, c), me).wait_recv()
                cp = copy(w, 4 + j, (*chip, c), sibling)
                cp.start()
                passed.append(cp)
        for w in range(n):
            copy(w, 0, sibling, me).wait_recv()
            for j, chip in enumerate(chips):
                copy(w, 4 + j, (*chip, 1 - c), me).wait_recv()
        for cp in first + passed:
            cp.wait_send()
        for cp in mine:
            cp.wait()

    return pl.pallas_call(
        body, name=name,
        in_specs=[ANY] * (n + len(deps)), out_specs=[ANY] * n,
        out_shape=[jax.ShapeDtypeStruct((N_DEV,) + a.shape, a.dtype) for a in shards],
        scratch_shapes=[pltpu.SemaphoreType.DMA((n, 7)), pltpu.SemaphoreType.DMA((n, 7)),
                        pltpu.SemaphoreType.DMA((n,))],
    )(*shards, *deps)


HBM = pl.BlockSpec(memory_space=pltpu.HBM)
SEM = pl.BlockSpec(memory_space=pltpu.SEMAPHORE)
EFFECT = pltpu.SideEffectType.DATAFLOW_SIDE_EFFECTING
PEERS = [(dx, dy, dc) for dx in (1, 0) for dy in (1, 0) for dc in (0, 1) if (dx, dy, dc) != (0, 0, 0)]


def _peer(x, y, c, flip):
    dx, dy, dc = flip
    return (1 - x if dx else x, 1 - y if dy else y, 1 - c if dc else c)


def _exchange_copies(srcs, lands, send, recv, loc, gather):
    x, y, c, _ = _place()
    me = 4 * x + 2 * y + c
    remote, local = [], []
    for w in range(len(srcs)):
        for k, flip in enumerate(PEERS):
            px, py, pc = _peer(x, y, c, flip)
            src = srcs[w] if gather else srcs[w].at[4 * px + 2 * py + pc]
            remote.append(pltpu.make_async_remote_copy(
                src_ref=src, dst_ref=lands[w].at[me], send_sem=send[w].at[k], recv_sem=recv[w].at[k],
                device_id=(px, py, pc), device_id_type=MESH))
        local.append(pltpu.make_async_copy(srcs[w] if gather else srcs[w].at[me], lands[w].at[me], loc[w]))
    return remote, local


class _Exchange:
    def __init__(self, srcs, lands, send, recv, loc, token, gather):
        self.srcs, self.lands, self.send, self.recv, self.loc = srcs, lands, send, recv, loc
        self.token, self.gather = token, gather


def _exchange_start(srcs, gather, name):
    n = len(srcs)
    land_shapes = [((N_DEV,) + a.shape) if gather else a.shape for a in srcs]
    lands = [pltpu.with_memory_space_constraint(lax.empty(sh, a.dtype), pltpu.HBM) for sh, a in zip(land_shapes, srcs)]
    srcs = [pltpu.with_memory_space_constraint(a, pltpu.HBM) for a in srcs]

    def body(*refs):
        src_refs, land_refs = refs[:n], refs[n:2 * n]
        outs = refs[2 * n:]
        send, recv, loc = outs[:n], outs[n:2 * n], outs[2 * n:3 * n]
        token = outs[-1]
        remote, local = _exchange_copies(src_refs, land_refs, send, recv, loc, gather)
        for cp in remote + local:
            cp.start()
        token[...] = jnp.zeros_like(token)

    out_shape = ([pltpu.SemaphoreType.DMA((len(PEERS),))] * (2 * n) + [pltpu.SemaphoreType.DMA(())] * n
                 + [pltpu.HBM(a.shape, a.dtype) for a in srcs] + [pltpu.HBM(a.shape, a.dtype) for a in lands]
                 + [jax.ShapeDtypeStruct((SUB, LANE), F32)])
    res = pl.pallas_call(
        body, name=name, out_shape=out_shape,
        in_specs=[HBM] * (2 * n),
        out_specs=[SEM] * (3 * n) + [HBM] * (2 * n) + [pl.BlockSpec(memory_space=pltpu.VMEM)],
        input_output_aliases={i: 3 * n + i for i in range(2 * n)},
        compiler_params=pltpu.CompilerParams(has_side_effects=EFFECT),
    )(*srcs, *lands)
    return _Exchange(res[3 * n:4 * n], res[4 * n:5 * n], res[:n], res[n:2 * n], res[2 * n:3 * n], res[-1], gather)


def _exchange_wait(ex, idxs, after, name):
    n = len(idxs)
    srcs = [ex.srcs[i] for i in idxs]
    lands = [ex.lands[i] for i in idxs]
    sems = [ex.send[i] for i in idxs] + [ex.recv[i] for i in idxs] + [ex.loc[i] for i in idxs]
    gather = ex.gather

    def body(*refs):
        src_refs, land_refs = refs[:n], refs[n:2 * n]
        send, recv, loc = refs[2 * n:3 * n], refs[3 * n:4 * n], refs[4 * n:5 * n]
        remote, local = _exchange_copies(src_refs, land_refs, send, recv, loc, gather)
        for cp in remote:
            cp.wait_send()
            cp.wait_recv()
        for cp in local:
            cp.wait()

    res = pl.pallas_call(
        body, name=name,
        out_shape=[pltpu.HBM(a.shape, a.dtype) for a in srcs] + [pltpu.HBM(a.shape, a.dtype) for a in lands],
        in_specs=[HBM] * (2 * n) + [SEM] * (3 * n) + [ANY],
        out_specs=[HBM] * (2 * n),
        input_output_aliases={i: i for i in range(2 * n)},
        compiler_params=pltpu.CompilerParams(has_side_effects=EFFECT),
    )(*srcs, *lands, *sems, after)
    return res[n:]


def _after(token, a):
    return a + token[0:1, 0:1].astype(a.dtype)


def _unblock(w3):
    nb, k, nbw = w3.shape
    return w3.transpose(1, 0, 2).reshape(k, nb * nbw)


def _block(w, nb):
    k, n = w.shape
    return w.reshape(k, nb, n // nb).transpose(1, 0, 2)


def kernel(x, positions, ln1_g, w_in, b_gate, conv_w, w_conv_out, q_a_g, w_q_b, kv_a_g, w_kv_b, q_norm_g, k_norm_g, w_mla_out, w_o, ln2_g, w_ffn_up, ffn_conv_w, ffn_conv_b, w_ffn_down, loss_target, m_ln1_g, m_w_in, m_b_gate, m_conv_w, m_w_conv_out, m_q_a_g, m_w_q_b, m_kv_a_g, m_w_kv_b, m_q_norm_g, m_k_norm_g, m_w_mla_out, m_w_o, m_ln2_g, m_w_ffn_up, m_ffn_conv_w, m_ffn_conv_b, m_w_ffn_down, v_ln1_g, v_w_in, v_b_gate, v_conv_w, v_w_conv_out, v_q_a_g, v_w_q_b, v_kv_a_g, v_w_kv_b, v_q_norm_g, v_k_norm_g, v_w_mla_out, v_w_o, v_ln2_g, v_w_ffn_up, v_ffn_conv_w, v_ffn_conv_b, v_w_ffn_down):
    s, d = x.shape[1], x.shape[2]
    conv = conv_w.shape[2] * N_DEV
    ql, kvl = q_a_g.shape[1], kv_a_g.shape[1]
    heads = w_q_b.shape[2] * N_DEV // HEAD_QK
    dff = w_ffn_down.shape[1] * N_DEV
    hw = heads * LANE
    conv3 = 3 * conv
    kr_off = conv3 + ql
    kv_off = -(-(kr_off + LANE) // kvl) * kvl
    wa = kv_off + kvl
    assert conv3 % ql == 0 and kr_off % LANE == 0
    xs = x[0]
    tgt = loss_target[0]
    pos = positions.reshape(s, 1)

    nin = w_in.shape[2]
    big = dict(w_in=w_in[0].T, w_conv_out=w_conv_out[0], w_q_b=w_q_b[0], w_kv_b=w_kv_b[0],
               w_mla_out=w_mla_out[0], w_o=w_o[0], w_ffn_up=w_ffn_up[0], w_ffn_down=w_ffn_down[0])
    names = list(big)
    rest = names[1:]
    nin_p = -(-nin // 16) * 16
    first = _all_gather([jnp.pad(big["w_in"].astype(BF), ((0, nin_p - nin), (0, 0))), _pad8(conv_w[0]),
                         _pad8(ffn_conv_w[0])], "gather_w_in")
    cw8 = _unblock(first[1])
    fcw8 = _unblock(first[2])
    ag = _exchange_start([big[k].astype(BF) for k in rest], True, "gather_rest_start")

    def landed(keys, after, name):
        return _exchange_wait(ag, [rest.index(k) for k in keys], after, name)

    w_in_t = first[0][:, :nin].reshape(N_DEV * nin, d)
    g_off = kr_off + kvl + ROPE
    w_a_t = jnp.concatenate([w_in_t[:kr_off], _lay_rows(w_in_t[kr_off + kvl:g_off]),
                             jnp.zeros((kv_off - kr_off - LANE, d), BF), w_in_t[kr_off:kr_off + kvl]], axis=0)[None]
    w_ga_t = w_in_t[g_off:g_off + d][None]
    w_gb_t = w_in_t[g_off + d:g_off + 2 * d][None]
    gains = _pad8(jnp.concatenate([q_norm_g[:, :NOPE], _lay(q_norm_g[:, NOPE:]),
                                   k_norm_g[:, :NOPE], _lay(k_norm_g[:, NOPE:])], axis=0))
    kr_blk = kr_off // LANE

    cos, sin = _rope_tables(pos)
    u1 = _rms_fwd(xs, _after(ag.token, ln1_g), d, 0, "rms1_fwd")
    z_a = _mm_nt(u1, w_a_t, "mm_z_a")
    z_ga = _mm_nt(u1, w_ga_t, "mm_z_ga")
    z_gb = _mm_nt(u1, w_gb_t, "mm_z_gb")
    p = _conv_mix_fwd(z_a, cw8, conv)
    w_co, w_qb, w_kv = landed(["w_conv_out", "w_q_b", "w_kv_b"], p, "gather_wait_mixers")
    wq_full = _unblock(w_qb).reshape(ql, heads, HEAD_QK)
    w_q = jnp.concatenate([wq_full[:, :, :NOPE].reshape(ql, hw), _lay(wq_full[:, :, NOPE:]).reshape(ql, hw)],
                          axis=1)[None]
    yc = _mm_nn(p, w_co, "mm_y_conv")
    qn = _rms_fwd(z_a, q_a_g, ql, conv3 // ql, "rms_q_fwd")
    kvn = _rms_fwd(z_a, kv_a_g, kvl, kv_off // kvl, "rms_kv_fwd")
    q_raw = _mm_nn(qn, w_q, "mm_q")
    kv_raw = _mm_nn(kvn, w_kv, "mm_kv")
    q_att, k_att, v_bf = _head_fwd(q_raw, kv_raw, z_a, kr_blk, cos, sin, gains, heads)
    o, o_bf, lse = _attn_fwd(q_att, k_att, v_bf, heads)
    w_mo, w_oo = landed(["w_mla_out", "w_o"], lse, "gather_wait_outs")
    w_mo = w_mo.reshape(1, hw, d)
    w_oo = w_oo.reshape(1, d, d)
    ym = _mm_nn(o_bf, w_mo, "mm_y_mla")
    mix = _gate_fwd(z_ga, z_gb, b_gate, yc, ym, d)
    h1 = _mm_nn(mix, w_oo, "mm_h1", add=xs)
    u2 = _rms_fwd(h1, ln2_g, d, 0, "rms2_fwd")
    w_up, = landed(["w_ffn_up"], u2, "gather_wait_ffn_up")
    a_pre = _mm_nn(u2, w_up, "mm_ffn_up")
    f = _ffn_act_fwd(a_pre, fcw8, ffn_conv_b, dff)
    w_dn, = landed(["w_ffn_down"], f, "gather_wait_ffn_down")
    w_dn = w_dn.reshape(1, dff, d)
    y = _mm_nn(f, w_dn, "mm_ffn_down", add=h1)
    dy, dy_bf, loss_part = _loss_head(y, tgt)

    g_dn = _mm_tn(f, dy_bf, 1, "mm_g_ffn_down").reshape(N_DEV, dff // N_DEV, d)
    rs_dn = _exchange_start([g_dn], False, "reduce_ffn_down_start")
    d_f = _mm_nt(dy_bf, w_dn, "mm_d_f", dep=rs_dn.token)
    d_xg, d_xu, dfw_g, dfw_u = _ffn_act_bwd(a_pre, d_f, fcw8, ffn_conv_b, dff)
    half = N_DEV // 2
    g_up = _mm_tn(u2, d_xg, half, "mm_g_ffn_up_gate", into=lax.empty((N_DEV, d, 2 * dff // N_DEV), BF))
    g_up = _mm_tn(u2, d_xu, half, "mm_g_ffn_up_up", into=g_up, blk0=half)
    rs_up = _exchange_start([g_up], False, "reduce_ffn_up_start")
    d_u2 = _mm_nt(d_xg, w_up, "mm_d_u2_gate", blk0=0, nblk=half, dep=rs_up.token)
    d_u2 = _mm_nt(d_xu, w_up, "mm_d_u2_up", blk0=half, nblk=half, add=d_u2)
    d_h1, d_h1_bf, dg_ln2 = _rms_bwd(h1, d_u2, ln2_g, d, 0, "rms2_bwd", extra=dy, also_bf16=True)
    g_oo = _mm_tn(mix, d_h1_bf, 1, "mm_g_w_o").reshape(N_DEV, d // N_DEV, d)
    d_mix = _mm_nt(d_h1_bf, w_oo, "mm_d_mix")
    d_zga, d_zgb, d_yc, d_ym, dba, dbb = _gate_bwd(d_mix, z_ga, z_gb, b_gate, yc, ym, d)
    g_co = _mm_tn(p, d_yc, N_DEV, "mm_g_conv_out")
    g_mo = _mm_tn(o_bf, d_ym, 1, "mm_g_mla_out").reshape(N_DEV, hw // N_DEV, d)
    rs_mix = _exchange_start([g_oo, g_co, g_mo], False, "reduce_mixers_start")
    d_p = _mm_nt(d_yc, w_co, "mm_d_p", dep=rs_mix.token)
    d_o = _mm_nt(d_ym, w_mo, "mm_d_o", out_dtype=BF)
    d_zb, d_zc, d_zv, dcw = _conv_mix_bwd(z_a, d_p, cw8, conv)
    dq_att, dk_att, dv = _attn_bwd(q_att, k_att, v_bf, o, lse, d_o, heads, dep=rs_mix.token)
    d_q_raw, d_kv_raw, d_kr, dgains = _head_bwd(q_raw, kv_raw, z_a, kr_blk, cos, sin, gains, dq_att, dk_att, dv, heads)
    g_q2 = _mm_tn(qn, d_q_raw, 1, "mm_g_q")[0]
    g_qb = _block(jnp.concatenate([g_q2[:, :hw].reshape(ql, heads, NOPE),
                                   _unlay(g_q2[:, hw:].reshape(ql, heads, LANE))], axis=2).reshape(ql, heads * HEAD_QK), N_DEV)
    g_kv = _mm_tn(kvn, d_kv_raw, N_DEV, "mm_g_kv")
    rs_qkv = _exchange_start([g_qb, g_kv], False, "reduce_qkv_start")
    d_qn = _mm_nt(d_q_raw, w_q, "mm_d_qn", dep=rs_qkv.token)
    d_kvn = _mm_nt(d_kv_raw, w_kv, "mm_d_kvn")
    d_ql, dg_qa = _rms_bwd(z_a, d_qn, q_a_g, ql, conv3 // ql, "rms_q_bwd", out_dtype=BF)
    d_kvl, dg_kva = _rms_bwd(z_a, d_kvn, kv_a_g, kvl, kv_off // kvl, "rms_kv_bwd", out_dtype=BF)
    d_z_a = jnp.concatenate([d_zb, d_zc, d_zv, d_ql, d_kr.astype(BF), jnp.zeros((s, kv_off - kr_off - LANE), BF),
                             d_kvl], axis=1)
    g_a = _mm_tn(d_z_a, u1, 1, "mm_g_w_a")[0]
    g_ga = _mm_tn(d_zga, u1, 1, "mm_g_w_ga")[0]
    g_gb = _mm_tn(d_zgb, u1, 1, "mm_g_w_gb")[0]
    g_in = jnp.concatenate([g_a[:kr_off], g_a[kv_off:kv_off + kvl], g_a[kr_off:kr_off + HALF],
                            g_a[kr_off + 2 * HALF:kr_off + 3 * HALF], g_ga, g_gb], axis=0).reshape(N_DEV, nin, d)
    g_in = jnp.pad(g_in, ((0, 0), (0, nin_p - nin), (0, 0)))
    rs_in = _exchange_start([g_in], False, "reduce_w_in_start")
    d_u1 = _mm_nn(d_z_a, w_a_t, "mm_d_u1_a", dep=rs_in.token)
    d_u1 = _mm_nn(d_zga, w_ga_t, "mm_d_u1_ga", add=d_u1)
    d_u1 = _mm_nn(d_zgb, w_gb_t, "mm_d_u1_gb", add=d_u1)
    grad_x, dg_ln1 = _rms_bwd(xs, d_u1, ln1_g, d, 0, "rms1_bwd", extra=d_h1)

    summed = {}
    summed["w_ffn_down"], = _exchange_wait(rs_dn, [0], grad_x, "reduce_ffn_down_wait")
    summed["w_ffn_up"], = _exchange_wait(rs_up, [0], grad_x, "reduce_ffn_up_wait")
    summed["w_o"], summed["w_conv_out"], summed["w_mla_out"] = _exchange_wait(rs_mix, [0, 1, 2], grad_x, "reduce_mixers_wait")
    summed["w_q_b"], summed["w_kv_b"] = _exchange_wait(rs_qkv, [0, 1], grad_x, "reduce_qkv_wait")
    loc = locals()
    out = {}
    for k in rest:
        out[k] = _adamw(summed[k], big[k], loc["m_" + k][0], loc["v_" + k][0], "adamw_" + k)

    small = dict(ln1_g=dg_ln1[0:1], b_gate=jnp.concatenate([dba[0:1], dbb[0:1]], axis=1), q_a_g=dg_qa[0:1],
                 kv_a_g=dg_kva[0:1],
                 q_norm_g=jnp.concatenate([dgains[0:1], _unlay(dgains[1:2])], axis=1),
                 k_norm_g=jnp.concatenate([dgains[2:3], _unlay(dgains[3:4])], axis=1),
                 ln2_g=dg_ln2[0:1], ffn_conv_b=jnp.concatenate([dfw_g[3:4], dfw_u[3:4]], axis=1))
    small_names = list(small)
    extra = [dcw[0:3].reshape(1, -1), jnp.concatenate([dfw_g[0:3], dfw_u[0:3]], axis=1).reshape(1, -1),
             loss_part[0:1, 0:1]]
    flat = jnp.concatenate([small[k] for k in small_names] + extra, axis=1)
    n_flat = flat.shape[1]
    rows = -(-n_flat // (SUB * LANE)) * SUB
    flat = jnp.pad(flat, ((0, 0), (0, rows * LANE - n_flat))).reshape(rows, LANE)
    total = _sum_parts(_all_gather([flat], "gather_small", dep=[out[k][0] for k in rest])[0], "sum_small").reshape(1, rows * LANE)
    off = 0
    small_g = {}
    for k in small_names:
        small_g[k] = total[:, off:off + small[k].shape[1]]
        off += small[k].shape[1]
    me = 4 * lax.axis_index("x") + 2 * lax.axis_index("y") + lax.axis_index("c")
    cwn, fcwn = conv // N_DEV, 2 * dff // N_DEV
    g_cw = lax.dynamic_slice_in_dim(total[:, off:off + 3 * conv].reshape(3, conv), me * cwn, cwn, axis=1)
    off += 3 * conv
    g_fcw = lax.dynamic_slice_in_dim(total[:, off:off + 6 * dff].reshape(3, 2 * dff), me * fcwn, fcwn, axis=1)
    off += 6 * dff
    loss = total[0, off]

    summed["w_in"], = _exchange_wait(rs_in, [0], total, "reduce_w_in_wait")
    out["w_in"] = [r.T for r in _adamw(summed["w_in"], big["w_in"], m_w_in[0].T, v_w_in[0].T, "adamw_w_in",
                                       by_cols=True)]
    small_w = dict(ln1_g=ln1_g, b_gate=b_gate, q_a_g=q_a_g, kv_a_g=kv_a_g, q_norm_g=q_norm_g, k_norm_g=k_norm_g,
                   ln2_g=ln2_g, ffn_conv_b=ffn_conv_b, conv_w=conv_w[0].reshape(1, -1),
                   ffn_conv_w=ffn_conv_w[0].reshape(1, -1))
    small_g["conv_w"] = g_cw.reshape(1, -1)
    small_g["ffn_conv_w"] = g_fcw.reshape(1, -1)
    packed_names = list(small_w)

    def pack(get):
        vflat = jnp.concatenate([get(k).reshape(1, -1) for k in packed_names], axis=1)
        nr = -(-vflat.shape[1] // (SUB * LANE)) * SUB
        return jnp.pad(vflat, ((0, 0), (0, nr * LANE - vflat.shape[1])), constant_values=1.0).reshape(nr, LANE)

    res = _adamw(pack(lambda k: small_g[k])[None], pack(lambda k: small_w[k]), pack(lambda k: loc["m_" + k]),
                 pack(lambda k: loc["v_" + k]), "adamw_small")
    res = [r.reshape(1, -1) for r in res]
    off = 0
    for k in packed_names:
        shape = loc[k].shape
        size = small_w[k].shape[1]
        out[k] = [r[:, off:off + size].reshape(shape) for r in res]
        off += size
    for k in names:
        out[k] = [r[None] for r in out[k]]

    order = ["ln1_g", "w_in", "b_gate", "conv_w", "w_conv_out", "q_a_g", "w_q_b", "kv_a_g", "w_kv_b", "q_norm_g",
             "k_norm_g", "w_mla_out", "w_o", "ln2_g", "w_ffn_up", "ffn_conv_w", "ffn_conv_b", "w_ffn_down"]
    return (loss, grad_x[None], *[out[k][0] for k in order], *[out[k][1] for k in order],
            *[out[k][2] for k in order], *[out[k][3] for k in order])
```

```python
import functools

import jax
import jax.numpy as jnp
from jax import lax
from jax.experimental import pallas as pl
from jax.experimental.pallas import tpu as pltpu

BF = jnp.bfloat16
F32 = jnp.float32
MESH = pl.DeviceIdType.MESH
N_DEV = 8

NOPE = 128
ROPE = 64
HALF = ROPE // 2
HEAD_QK = NOPE + ROPE
HEAD_V = 128
LANE = 128
SUB = 8
NORM_EPS = 1e-6
NEG_INF = -1e30
ROPE_THETA = 10000.0
ADAM_LR = 0.001
ADAM_B1 = 0.9
ADAM_B2 = 0.999
ADAM_EPS = 1e-08
ADAM_WD = 0.01
ADAM_STEP = 10

VMEM_LIMIT = 52 * 1024 * 1024
MM_TM, MM_TN, MM_TK, MM_TS = 1024, 1536, 2048, 1024
ROW_TILE, ROW_TILE_BWD = 512, 256
HEAD_ROW_TILE, HEAD_ROW_TILE_BWD = 256, 128
COL_TILE = 512
ATTN_TILE = 1024
ATTN_TILE_FWD = 1024
ANY = pl.BlockSpec(memory_space=pl.ANY)


def _pick(n, target, mult):
    t = (min(n, target) // mult) * mult
    while t > 0:
        if n % t == 0:
            return t
        t -= mult
    raise ValueError(f"no tile for {n} (target {target}, multiple {mult})")


def _cp(*sem):
    return pltpu.CompilerParams(dimension_semantics=sem, vmem_limit_bytes=VMEM_LIMIT)


def _accumulate(kk, nk, acc, part, finish):
    if nk == 1:
        finish(part())
        return

    @pl.when(kk == 0)
    def _():
        acc[...] = part()

    @pl.when((kk > 0) & (kk < nk - 1))
    def _():
        acc[...] += part()

    @pl.when(kk == nk - 1)
    def _():
        finish(acc[...] + part())


def _mm_call(body, name, grid, in_specs, args, out_spec, out_shape, acc_shape, nk, dep):
    if dep is not None:
        in_specs = in_specs + [ANY]
        args = args + [dep]
    return pl.pallas_call(
        body, name=name, grid=grid, in_specs=in_specs, out_specs=out_spec, out_shape=out_shape,
        scratch_shapes=[pltpu.VMEM(acc_shape, F32)] if nk > 1 else [],
        compiler_params=_cp("parallel", "parallel", "arbitrary"),
    )(*args)


def _mm_nn(a, b3, name, add=None, out_dtype=F32, blk0=0, nblk=None, dep=None):
    m, k = a.shape
    nb_all, k2, nbw = b3.shape
    assert k == k2
    nblk = nb_all - blk0 if nblk is None else nblk
    n = nblk * nbw
    tm = _pick(m, MM_TM, 16)
    tn = _pick(nbw, MM_TN, LANE)
    tk = _pick(k, MM_TK, LANE)
    per = nbw // tn
    nk = k // tk

    def body(*refs):
        a_ref, b_ref = refs[:2]
        c_ref = refs[2] if add is not None else None
        o_ref = refs[2 + (add is not None) + (dep is not None)]
        acc = refs[-1]

        def part():
            return jnp.dot(a_ref[...].astype(BF), b_ref[...].astype(BF), preferred_element_type=F32)

        def finish(r):
            if add is not None:
                r = r + c_ref[...]
            o_ref[...] = r.astype(out_dtype)

        _accumulate(pl.program_id(2), nk, acc, part, finish)

    in_specs = [pl.BlockSpec((tm, tk), lambda i, j, kk: (i, kk)),
                pl.BlockSpec((None, tk, tn), lambda i, j, kk: (blk0 + j // per, kk, j % per))]
    args = [a, b3]
    if add is not None:
        in_specs.append(pl.BlockSpec((tm, tn), lambda i, j, kk: (i, j)))
        args.append(add)
    return _mm_call(body, name, (m // tm, n // tn, nk), in_specs, args,
                    pl.BlockSpec((tm, tn), lambda i, j, kk: (i, j)), jax.ShapeDtypeStruct((m, n), out_dtype),
                    (tm, tn), nk, dep)


def _mm_nt(a, b3, name, add=None, out_dtype=F32, blk0=0, nblk=None, dep=None):
    m, n = a.shape
    nb_all, k, nbw = b3.shape
    nblk = nb_all - blk0 if nblk is None else nblk
    assert n == nblk * nbw
    tm = _pick(m, MM_TM, 16)
    tn = _pick(k, MM_TN, LANE)
    tk = _pick(nbw, MM_TK, LANE)
    per = nbw // tk
    nk = n // tk

    def body(*refs):
        a_ref, b_ref = refs[:2]
        c_ref = refs[2] if add is not None else None
        o_ref = refs[2 + (add is not None) + (dep is not None)]
        acc = refs[-1]

        def part():
            return lax.dot_general(a_ref[...].astype(BF), b_ref[...].astype(BF),
                                   (((1,), (1,)), ((), ())), preferred_element_type=F32)

        def finish(r):
            if add is not None:
                r = r + c_ref[...]
            o_ref[...] = r.astype(out_dtype)

        _accumulate(pl.program_id(2), nk, acc, part, finish)

    in_specs = [pl.BlockSpec((tm, tk), lambda i, j, kk: (i, kk)),
                pl.BlockSpec((None, tn, tk), lambda i, j, kk: (blk0 + kk // per, j, kk % per))]
    args = [a, b3]
    if add is not None:
        in_specs.append(pl.BlockSpec((tm, tn), lambda i, j, kk: (i, j)))
        args.append(add)
    return _mm_call(body, name, (m // tm, k // tn, nk), in_specs, args,
                    pl.BlockSpec((tm, tn), lambda i, j, kk: (i, j)), jax.ShapeDtypeStruct((m, k), out_dtype),
                    (tm, tn), nk, dep)


def _mm_tn(a, b, nblk, name, out_dtype=BF, dep=None, into=None, blk0=0):
    s, m = a.shape
    s2, n = b.shape
    assert s == s2 and n % nblk == 0 and (dep is None or into is None)
    nbw = n // nblk
    tm = _pick(m, MM_TN, LANE)
    tn = _pick(nbw, MM_TN, LANE)
    ts = _pick(s, MM_TS, LANE)
    per = nbw // tn
    ns = s // ts

    def body(*refs):
        a_ref, b_ref = refs[:2]
        o_ref = refs[2 + (dep is not None or into is not None)]
        acc = refs[-1]

        def part():
            return lax.dot_general(a_ref[...].astype(BF), b_ref[...].astype(BF),
                                   (((0,), (0,)), ((), ())), preferred_element_type=F32)

        def finish(r):
            o_ref[...] = r.astype(out_dtype)

        _accumulate(pl.program_id(2), ns, acc, part, finish)

    in_specs = [pl.BlockSpec((ts, tm), lambda i, j, ss: (ss, i)),
                pl.BlockSpec((ts, tn), lambda i, j, ss: (ss, j))]
    out_spec = pl.BlockSpec((None, tm, tn), lambda i, j, ss: (blk0 + j // per, i, j % per))
    if into is None:
        return _mm_call(body, name, (m // tm, n // tn, ns), in_specs, [a, b], out_spec,
                        jax.ShapeDtypeStruct((nblk, m, nbw), out_dtype), (tm, tn), ns, dep)
    assert into.shape[1:] == (m, nbw) and into.dtype == out_dtype
    return pl.pallas_call(
        body, name=name, grid=(m // tm, n // tn, ns), in_specs=in_specs + [ANY], out_specs=out_spec,
        out_shape=jax.ShapeDtypeStruct(into.shape, out_dtype), input_output_aliases={2: 0},
        scratch_shapes=[pltpu.VMEM((tm, tn), F32)] if ns > 1 else [],
        compiler_params=_cp("parallel", "parallel", "arbitrary"),
    )(a, b, into)


def _rows8(rows, width):
    idx = lax.broadcasted_iota(jnp.int32, (SUB, width), 0)
    out = jnp.zeros((SUB, width), F32)
    for r, v in enumerate(rows):
        out = jnp.where(idx == r, v, out)
    return out


def _rms_fwd(x, g, width, col_blk, name):
    s = x.shape[0]
    tr = _pick(s, ROW_TILE, 16)

    def body(x_ref, g_ref, u_ref):
        xv = x_ref[...]
        r = lax.rsqrt(jnp.mean(xv * xv, axis=-1, keepdims=True) + NORM_EPS)
        u_ref[...] = ((xv * r) * g_ref[...]).astype(BF)

    return pl.pallas_call(
        body, name=name, grid=(s // tr,),
        in_specs=[pl.BlockSpec((tr, width), lambda i: (i, col_blk)),
                  pl.BlockSpec((1, width), lambda i: (0, 0))],
        out_specs=pl.BlockSpec((tr, width), lambda i: (i, 0)),
        out_shape=jax.ShapeDtypeStruct((s, width), BF),
        compiler_params=_cp("parallel"),
    )(x, g)


def _rms_bwd(x, du, g, width, col_blk, name, extra=None, out_dtype=F32, also_bf16=False):
    s = x.shape[0]
    tr = _pick(s, ROW_TILE_BWD, 16)

    def body(*refs):
        x_ref, du_ref, g_ref = refs[:3]
        e_ref = refs[3] if extra is not None else None
        dx_ref = refs[3 + (extra is not None)]
        dxb_ref = refs[4 + (extra is not None)] if also_bf16 else None
        dg_ref = refs[-1]
        i = pl.program_id(0)
        xv = x_ref[...]
        duv = du_ref[...].astype(F32)
        r = lax.rsqrt(jnp.mean(xv * xv, axis=-1, keepdims=True) + NORM_EPS)
        nv = xv * r
        dn = duv * g_ref[...]
        dx = r * (dn - nv * jnp.mean(dn * nv, axis=-1, keepdims=True))
        if extra is not None:
            dx = dx + e_ref[...]
        dx_ref[...] = dx.astype(out_dtype)
        if also_bf16:
            dxb_ref[...] = dx.astype(BF)

        @pl.when(i == 0)
        def _():
            dg_ref[...] = jnp.zeros_like(dg_ref)

        dg_ref[...] += _rows8([jnp.sum(duv * nv, axis=0, keepdims=True)], width)

    in_specs = [pl.BlockSpec((tr, width), lambda i: (i, col_blk)),
                pl.BlockSpec((tr, width), lambda i: (i, 0)),
                pl.BlockSpec((1, width), lambda i: (0, 0))]
    args = [x, du, g]
    if extra is not None:
        in_specs.append(pl.BlockSpec((tr, width), lambda i: (i, 0)))
        args.append(extra)
    return pl.pallas_call(
        body, name=name, grid=(s // tr,),
        in_specs=in_specs,
        out_specs=[pl.BlockSpec((tr, width), lambda i: (i, 0))] * (1 + also_bf16)
        + [pl.BlockSpec((SUB, width), lambda i: (0, 0))],
        out_shape=[jax.ShapeDtypeStruct((s, width), out_dtype)] + [jax.ShapeDtypeStruct((s, width), BF)] * also_bf16
        + [jax.ShapeDtypeStruct((SUB, width), F32)],
        compiler_params=_cp("arbitrary"),
    )(*args)


def _down(cur, prev8, k):
    ext = jnp.concatenate([prev8, cur], axis=0)
    return pltpu.roll(ext, k, axis=0)[SUB:]


def _up(cur, next8, k):
    ext = jnp.concatenate([cur, next8], axis=0)
    return pltpu.roll(ext, ext.shape[0] - k, axis=0)[:cur.shape[0]]


def _lags(cur, prev8):
    return _down(cur, prev8, 1), _down(cur, prev8, 2)


def _conv3(w_ref, cur, prev8, lags=None):
    lag1, lag2 = _lags(cur, prev8) if lags is None else lags
    return w_ref[0:1, :] * lag2 + w_ref[1:2, :] * lag1 + w_ref[2:3, :] * cur


def _conv3_t(w_ref, cur, next8):
    return w_ref[2:3, :] * cur + w_ref[1:2, :] * _up(cur, next8, 1) + w_ref[0:1, :] * _up(cur, next8, 2)


def _spec_cur(tr, tc, c0):
    return pl.BlockSpec((tr, tc), lambda j, i: (i, c0 + j))


def _spec_prev(tr, tc, c0):
    return pl.BlockSpec((SUB, tc), lambda j, i: (jnp.maximum(i * (tr // SUB) - 1, 0), c0 + j))


def _spec_next(tr, tc, c0, s):
    return pl.BlockSpec((SUB, tc), lambda j, i: (jnp.minimum((i + 1) * (tr // SUB), s // SUB - 1), c0 + j))


def _spec_w(tc, c0):
    return pl.BlockSpec((SUB, tc), lambda j, i: (0, c0 + j))


def _pad8(w):
    return jnp.pad(w, ((0, SUB - w.shape[0]), (0, 0)))


def _conv_mix_fwd(z_a, cw8, conv):
    s = z_a.shape[0]
    tr = _pick(s, ROW_TILE, 16)
    tc = _pick(conv, COL_TILE, LANE)
    nc = conv // tc

    def body(zb_ref, zc_ref, zv_ref, zcp_ref, zvp_ref, w_ref, p_ref):
        i = pl.program_id(1)
        cv = zc_ref[...] * zv_ref[...]
        cvp = jnp.where(i > 0, zcp_ref[...] * zvp_ref[...], 0.0)
        p_ref[...] = (zb_ref[...] * _conv3(w_ref, cv, cvp)).astype(BF)

    return pl.pallas_call(
        body, name="conv_mix_fwd", grid=(nc, s // tr),
        in_specs=[_spec_cur(tr, tc, 0), _spec_cur(tr, tc, nc), _spec_cur(tr, tc, 2 * nc),
                  _spec_prev(tr, tc, nc), _spec_prev(tr, tc, 2 * nc), _spec_w(tc, 0)],
        out_specs=_spec_cur(tr, tc, 0),
        out_shape=jax.ShapeDtypeStruct((s, conv), BF),
        compiler_params=_cp("parallel", "parallel"),
    )(z_a, z_a, z_a, z_a, z_a, cw8)


def _conv_mix_bwd(z_a, d_p, cw8, conv):
    s = z_a.shape[0]
    tr = _pick(s, ROW_TILE_BWD, 16)
    tc = _pick(conv, COL_TILE, LANE)
    nc = conv // tc
    nr = s // tr

    def body(zb_ref, zbn_ref, zc_ref, zcp_ref, zv_ref, zvp_ref, dp_ref, dpn_ref, w_ref,
             dzb_ref, dzc_ref, dzv_ref, dw_ref):
        i = pl.program_id(1)
        zc = zc_ref[...]
        zv = zv_ref[...]
        cv = zc * zv
        cvp = jnp.where(i > 0, zcp_ref[...] * zvp_ref[...], 0.0)
        cv1, cv2 = _lags(cv, cvp)
        dpv = dp_ref[...]
        dzb_ref[...] = (dpv * _conv3(w_ref, cv, cvp, (cv1, cv2))).astype(BF)
        dcc = dpv * zb_ref[...]
        dccn = jnp.where(i < nr - 1, dpn_ref[...] * zbn_ref[...], 0.0)
        dcv = _conv3_t(w_ref, dcc, dccn)
        dzc_ref[...] = (dcv * zv).astype(BF)
        dzv_ref[...] = (dcv * zc).astype(BF)

        @pl.when(i == 0)
        def _():
            dw_ref[...] = jnp.zeros_like(dw_ref)

        dw_ref[...] += _rows8([jnp.sum(dcc * cv2, axis=0, keepdims=True),
                               jnp.sum(dcc * cv1, axis=0, keepdims=True),
                               jnp.sum(dcc * cv, axis=0, keepdims=True)], tc)

    out = jax.ShapeDtypeStruct((s, conv), BF)
    return pl.pallas_call(
        body, name="conv_mix_bwd", grid=(nc, nr),
        in_specs=[_spec_cur(tr, tc, 0), _spec_next(tr, tc, 0, s),
                  _spec_cur(tr, tc, nc), _spec_prev(tr, tc, nc),
                  _spec_cur(tr, tc, 2 * nc), _spec_prev(tr, tc, 2 * nc),
                  _spec_cur(tr, tc, 0), _spec_next(tr, tc, 0, s), _spec_w(tc, 0)],
        out_specs=[_spec_cur(tr, tc, 0), _spec_cur(tr, tc, 0), _spec_cur(tr, tc, 0), _spec_w(tc, 0)],
        out_shape=[out, out, out, jax.ShapeDtypeStruct((SUB, conv), F32)],
        compiler_params=_cp("parallel", "arbitrary"),
    )(z_a, z_a, z_a, z_a, z_a, z_a, d_p, d_p, cw8)


def _silu_parts(ag):
    sg = jax.nn.sigmoid(ag)
    return ag * sg, sg


def _ffn_act_fwd(a_pre, cw8, cb, dff):
    s = a_pre.shape[0]
    tr = _pick(s, ROW_TILE, 16)
    tc = _pick(dff, COL_TILE, LANE)
    nc = dff // tc

    def body(xg_ref, xgp_ref, xu_ref, xup_ref, wg_ref, wu_ref, bg_ref, bu_ref, f_ref):
        i = pl.program_id(1)
        xgp = jnp.where(i > 0, xgp_ref[...], 0.0)
        xup = jnp.where(i > 0, xup_ref[...], 0.0)
        ag = _conv3(wg_ref, xg_ref[...], xgp) + bg_ref[...]
        au = _conv3(wu_ref, xu_ref[...], xup) + bu_ref[...]
        f_ref[...] = (_silu_parts(ag)[0] * au).astype(BF)

    return pl.pallas_call(
        body, name="ffn_act_fwd", grid=(nc, s // tr),
        in_specs=[_spec_cur(tr, tc, 0), _spec_prev(tr, tc, 0), _spec_cur(tr, tc, nc), _spec_prev(tr, tc, nc),
                  _spec_w(tc, 0), _spec_w(tc, nc),
                  pl.BlockSpec((1, tc), lambda j, i: (0, j)), pl.BlockSpec((1, tc), lambda j, i: (0, nc + j))],
        out_specs=_spec_cur(tr, tc, 0),
        out_shape=jax.ShapeDtypeStruct((s, dff), BF),
        compiler_params=_cp("parallel", "parallel"),
    )(a_pre, a_pre, a_pre, a_pre, cw8, cw8, cb, cb)


def _ffn_act_bwd(a_pre, d_f, cw8, cb, dff):
    s = a_pre.shape[0]
    tr = _pick(s, ROW_TILE_BWD, 16)
    tc = _pick(dff, COL_TILE, LANE)
    nc = dff // tc
    nr = s // tr

    def body(xg_ref, xgp_ref, xgn_ref, xu_ref, xup_ref, xun_ref, df_ref, dfn_ref,
             wg_ref, wu_ref, bg_ref, bu_ref, dxg_ref, dxu_ref, dwg_ref, dwu_ref):
        i = pl.program_id(1)
        xg = xg_ref[...]
        xu = xu_ref[...]
        xgp = jnp.where(i > 0, xgp_ref[...], 0.0)
        xup = jnp.where(i > 0, xup_ref[...], 0.0)

        def d_act(xg_t, xgp_t, xu_t, xup_t, df_t, lags_g=None, lags_u=None):
            ag = _conv3(wg_ref, xg_t, xgp_t, lags_g) + bg_ref[...]
            au = _conv3(wu_ref, xu_t, xup_t, lags_u) + bu_ref[...]
            sil, sg = _silu_parts(ag)
            return df_t * au * (sg * (1.0 + ag * (1.0 - sg))), df_t * sil

        lags_g = _lags(xg, xgp)
        lags_u = _lags(xu, xup)
        dag, dau = d_act(xg, xgp, xu, xup, df_ref[...], lags_g, lags_u)
        dfn = jnp.where(i < nr - 1, dfn_ref[...], 0.0)
        dagn, daun = d_act(xgn_ref[...], xg[tr - SUB:], xun_ref[...], xu[tr - SUB:], dfn)
        dxg_ref[...] = _conv3_t(wg_ref, dag, dagn).astype(BF)
        dxu_ref[...] = _conv3_t(wu_ref, dau, daun).astype(BF)

        @pl.when(i == 0)
        def _():
            dwg_ref[...] = jnp.zeros_like(dwg_ref)
            dwu_ref[...] = jnp.zeros_like(dwu_ref)

        def wgrad(da, x, lags):
            return _rows8([jnp.sum(da * lags[1], axis=0, keepdims=True),
                           jnp.sum(da * lags[0], axis=0, keepdims=True),
                           jnp.sum(da * x, axis=0, keepdims=True),
                           jnp.sum(da, axis=0, keepdims=True)], tc)

        dwg_ref[...] += wgrad(dag, xg, lags_g)
        dwu_ref[...] += wgrad(dau, xu, lags_u)

    half = jax.ShapeDtypeStruct((s, dff), BF)
    wsh = jax.ShapeDtypeStruct((SUB, dff), F32)
    return pl.pallas_call(
        body, name="ffn_act_bwd", grid=(nc, nr),
        in_specs=[_spec_cur(tr, tc, 0), _spec_prev(tr, tc, 0), _spec_next(tr, tc, 0, s),
                  _spec_cur(tr, tc, nc), _spec_prev(tr, tc, nc), _spec_next(tr, tc, nc, s),
                  _spec_cur(tr, tc, 0), _spec_next(tr, tc, 0, s),
                  _spec_w(tc, 0), _spec_w(tc, nc),
                  pl.BlockSpec((1, tc), lambda j, i: (0, j)), pl.BlockSpec((1, tc), lambda j, i: (0, nc + j))],
        out_specs=[_spec_cur(tr, tc, 0), _spec_cur(tr, tc, 0), _spec_w(tc, 0), _spec_w(tc, 0)],
        out_shape=[half, half, wsh, wsh],
        compiler_params=_cp("parallel", "arbitrary"),
    )(a_pre, a_pre, a_pre, a_pre, a_pre, a_pre, d_f, d_f, cw8, cw8, cb, cb)


def _gate_fwd(z_ga, z_gb, b_gate, yc, ym, d):
    s = z_ga.shape[0]
    tr = _pick(s, ROW_TILE, 16)
    tc = _pick(d, COL_TILE, LANE)
    nc = d // tc

    def body(za_ref, zb_ref, ba_ref, bb_ref, yc_ref, ym_ref, o_ref):
        ga = jax.nn.sigmoid(za_ref[...] + ba_ref[...])
        gb = jax.nn.sigmoid(zb_ref[...] + bb_ref[...])
        o_ref[...] = (ga * yc_ref[...] + gb * ym_ref[...]).astype(BF)

    return pl.pallas_call(
        body, name="gate_fwd", grid=(nc, s // tr),
        in_specs=[_spec_cur(tr, tc, 0), _spec_cur(tr, tc, 0),
                  pl.BlockSpec((1, tc), lambda j, i: (0, j)), pl.BlockSpec((1, tc), lambda j, i: (0, nc + j)),
                  _spec_cur(tr, tc, 0), _spec_cur(tr, tc, 0)],
        out_specs=_spec_cur(tr, tc, 0),
        out_shape=jax.ShapeDtypeStruct((s, d), BF),
        compiler_params=_cp("parallel", "parallel"),
    )(z_ga, z_gb, b_gate, b_gate, yc, ym)


def _gate_bwd(d_mix, z_ga, z_gb, b_gate, yc, ym, d):
    s = z_ga.shape[0]
    tr = _pick(s, ROW_TILE, 16)
    tc = _pick(d, COL_TILE, LANE)
    nc = d // tc

    def body(dm_ref, za_ref, zb_ref, ba_ref, bb_ref, yc_ref, ym_ref,
             dza_ref, dzb_ref, dyc_ref, dym_ref, dba_ref, dbb_ref):
        i = pl.program_id(1)
        dm = dm_ref[...]
        ga = jax.nn.sigmoid(za_ref[...] + ba_ref[...])
        gb = jax.nn.sigmoid(zb_ref[...] + bb_ref[...])
        dza = dm * yc_ref[...] * (ga * (1.0 - ga))
        dzb = dm * ym_ref[...] * (gb * (1.0 - gb))
        dza_ref[...] = dza.astype(BF)
        dzb_ref[...] = dzb.astype(BF)
        dyc_ref[...] = (dm * ga).astype(BF)
        dym_ref[...] = (dm * gb).astype(BF)

        @pl.when(i == 0)
        def _():
            dba_ref[...] = jnp.zeros_like(dba_ref)
            dbb_ref[...] = jnp.zeros_like(dbb_ref)

        dba_ref[...] += _rows8([jnp.sum(dza, axis=0, keepdims=True)], tc)
        dbb_ref[...] += _rows8([jnp.sum(dzb, axis=0, keepdims=True)], tc)

    act = jax.ShapeDtypeStruct((s, d), BF)
    bsh = jax.ShapeDtypeStruct((SUB, d), F32)
    return pl.pallas_call(
        body, name="gate_bwd", grid=(nc, s // tr),
        in_specs=[_spec_cur(tr, tc, 0), _spec_cur(tr, tc, 0), _spec_cur(tr, tc, 0),
                  pl.BlockSpec((1, tc), lambda j, i: (0, j)), pl.BlockSpec((1, tc), lambda j, i: (0, nc + j)),
                  _spec_cur(tr, tc, 0), _spec_cur(tr, tc, 0)],
        out_specs=[_spec_cur(tr, tc, 0)] * 4 + [_spec_w(tc, 0)] * 2,
        out_shape=[act, act, act, act, bsh, bsh],
        compiler_params=_cp("parallel", "arbitrary"),
    )(d_mix, z_ga, z_gb, b_gate, b_gate, yc, ym)


def _lay(v):
    z = jnp.zeros(v.shape[:-1] + (HALF,), v.dtype)
    return jnp.concatenate([v[..., :HALF], z, v[..., HALF:], z], axis=-1)


def _unlay(v):
    return jnp.concatenate([v[..., :HALF], v[..., 2 * HALF:3 * HALF]], axis=-1)


def _lay_rows(v):
    z = jnp.zeros((HALF,) + v.shape[1:], v.dtype)
    return jnp.concatenate([v[:HALF], z, v[HALF:], z], axis=0)


def _rope_tables(positions):
    s = positions.shape[0]
    tr = _pick(s, ROW_TILE, 8)
    inv_freq = ROPE_THETA ** (-jnp.arange(0, ROPE, 2, dtype=F32) / ROPE)
    consts = jnp.stack([_lay(jnp.concatenate([inv_freq, inv_freq])),
                        _lay(jnp.ones((ROPE,), F32)),
                        _lay(jnp.concatenate([-jnp.ones((HALF,), F32), jnp.ones((HALF,), F32)]))])
    consts = _pad8(consts)

    def body(p_ref, c_ref, cos_ref, sin_ref):
        ang = p_ref[...].astype(F32) * c_ref[0:1, :]
        cos_ref[...] = jnp.cos(ang) * c_ref[1:2, :]
        sin_ref[...] = jnp.sin(ang) * c_ref[2:3, :]

    tab = jax.ShapeDtypeStruct((s, LANE), F32)
    return pl.pallas_call(
        body, name="rope_tables", grid=(s // tr,),
        in_specs=[pl.BlockSpec((tr, 1), lambda i: (i, 0)), pl.BlockSpec((SUB, LANE), lambda i: (0, 0))],
        out_specs=[pl.BlockSpec((tr, LANE), lambda i: (i, 0))] * 2,
        out_shape=[tab, tab],
        compiler_params=_cp("parallel"),
    )(positions, consts)


def _rope(t, cos, sin):
    return t * cos + pltpu.roll(t, 2 * HALF, axis=1) * sin


def _rope_t(d, cos, sin):
    return d * cos + pltpu.roll(d * sin, 2 * HALF, axis=1)


def _head_fwd(q_raw, kv_raw, z_a, kr_blk, cos, sin, gains, heads):
    s = q_raw.shape[0]
    tr = _pick(s, HEAD_ROW_TILE, 16)
    hw = heads * LANE

    def body(q_ref, kv_ref, kr_ref, cos_ref, sin_ref, g_ref, qo_ref, ko_ref, vo_ref):
        cosv = cos_ref[...]
        sinv = sin_ref[...]
        krv = kr_ref[...]
        kr_ss = jnp.sum(krv * krv, axis=-1, keepdims=True)
        for h in range(heads):
            lo = h * LANE
            qn = q_ref[:, lo:lo + LANE]
            qr = q_ref[:, hw + lo:hw + lo + LANE]
            ss = jnp.sum(qn * qn, axis=-1, keepdims=True) + jnp.sum(qr * qr, axis=-1, keepdims=True)
            r = lax.rsqrt(ss / HEAD_QK + NORM_EPS)
            qo_ref[:, 2 * lo:2 * lo + LANE] = ((qn * r) * g_ref[0:1, :]).astype(BF)
            qo_ref[:, 2 * lo + LANE:2 * lo + 2 * LANE] = _rope((qr * r) * g_ref[1:2, :], cosv, sinv).astype(BF)
            kn = kv_ref[:, 2 * lo:2 * lo + LANE]
            ss = jnp.sum(kn * kn, axis=-1, keepdims=True) + kr_ss
            r = lax.rsqrt(ss / HEAD_QK + NORM_EPS)
            ko_ref[:, 2 * lo:2 * lo + LANE] = ((kn * r) * g_ref[2:3, :]).astype(BF)
            ko_ref[:, 2 * lo + LANE:2 * lo + 2 * LANE] = _rope((krv * r) * g_ref[3:4, :], cosv, sinv).astype(BF)
            vo_ref[:, lo:lo + LANE] = kv_ref[:, 2 * lo + LANE:2 * lo + 2 * LANE].astype(BF)

    row = lambda w: pl.BlockSpec((tr, w), lambda i: (i, 0))
    return pl.pallas_call(
        body, name="head_fwd", grid=(s // tr,),
        in_specs=[row(2 * hw), row(2 * hw), pl.BlockSpec((tr, LANE), lambda i: (i, kr_blk)),
                  row(LANE), row(LANE), pl.BlockSpec((SUB, LANE), lambda i: (0, 0))],
        out_specs=[row(2 * hw), row(2 * hw), row(hw)],
        out_shape=[jax.ShapeDtypeStruct((s, 2 * hw), BF), jax.ShapeDtypeStruct((s, 2 * hw), BF),
                   jax.ShapeDtypeStruct((s, hw), BF)],
        compiler_params=_cp("parallel"),
    )(q_raw, kv_raw, z_a, cos, sin, gains)


def _head_bwd(q_raw, kv_raw, z_a, kr_blk, cos, sin, gains, dq_att, dk_att, dv, heads):
    s = q_raw.shape[0]
    tr = _pick(s, HEAD_ROW_TILE_BWD, 16)
    hw = heads * LANE

    def body(q_ref, kv_ref, kr_ref, cos_ref, sin_ref, g_ref, dq_ref, dk_ref, dv_ref,
             dqr_ref, dkv_ref, dkr_ref, dg_ref):
        i = pl.program_id(0)
        cosv = cos_ref[...]
        sinv = sin_ref[...]
        krv = kr_ref[...]
        kr_ss = jnp.sum(krv * krv, axis=-1, keepdims=True)
        dkr = jnp.zeros((tr, LANE), F32)
        dgs = [jnp.zeros((1, LANE), F32) for _ in range(4)]

        def norm_bwd(xn, xr, ss, dn_out, dr_out, gn, gr):
            r = lax.rsqrt(ss / HEAD_QK + NORM_EPS)
            nn = xn * r
            nr = xr * r
            dt = _rope_t(dr_out, cosv, sinv)
            dnn = dn_out * gn
            dnr = dt * gr
            mean = (jnp.sum(dnn * nn, axis=-1, keepdims=True) + jnp.sum(dnr * nr, axis=-1, keepdims=True)) / HEAD_QK
            return (r * (dnn - nn * mean), r * (dnr - nr * mean),
                    jnp.sum(dn_out * nn, axis=0, keepdims=True), jnp.sum(dt * nr, axis=0, keepdims=True))

        for h in range(heads):
            lo = h * LANE
            qn = q_ref[:, lo:lo + LANE]
            qr = q_ref[:, hw + lo:hw + lo + LANE]
            ss = jnp.sum(qn * qn, axis=-1, keepdims=True) + jnp.sum(qr * qr, axis=-1, keepdims=True)
            dxn, dxr, g0, g1 = norm_bwd(qn, qr, ss, dq_ref[:, 2 * lo:2 * lo + LANE],
                                        dq_ref[:, 2 * lo + LANE:2 * lo + 2 * LANE], g_ref[0:1, :], g_ref[1:2, :])
            dqr_ref[:, lo:lo + LANE] = dxn.astype(BF)
            dqr_ref[:, hw + lo:hw + lo + LANE] = dxr.astype(BF)
            kn = kv_ref[:, 2 * lo:2 * lo + LANE]
            ss = jnp.sum(kn * kn, axis=-1, keepdims=True) + kr_ss
            dxn, dxr, g2, g3 = norm_bwd(kn, krv, ss, dk_ref[:, 2 * lo:2 * lo + LANE],
                                        dk_ref[:, 2 * lo + LANE:2 * lo + 2 * LANE], g_ref[2:3, :], g_ref[3:4, :])
            dkv_ref[:, 2 * lo:2 * lo + LANE] = dxn.astype(BF)
            dkv_ref[:, 2 * lo + LANE:2 * lo + 2 * LANE] = dv_ref[:, lo:lo + LANE].astype(BF)
            dkr = dkr + dxr
            dgs = [a + b for a, b in zip(dgs, (g0, g1, g2, g3))]
        dkr_ref[...] = dkr

        @pl.when(i == 0)
        def _():
            dg_ref[...] = jnp.zeros_like(dg_ref)

        dg_ref[...] += _rows8(dgs, LANE)

    row = lambda w: pl.BlockSpec((tr, w), lambda i: (i, 0))
    return pl.pallas_call(
        body, name="head_bwd", grid=(s // tr,),
        in_specs=[row(2 * hw), row(2 * hw), pl.BlockSpec((tr, LANE), lambda i: (i, kr_blk)),
                  row(LANE), row(LANE), pl.BlockSpec((SUB, LANE), lambda i: (0, 0)),
                  row(2 * hw), row(2 * hw), row(hw)],
        out_specs=[row(2 * hw), row(2 * hw), row(LANE), pl.BlockSpec((SUB, LANE), lambda i: (0, 0))],
        out_shape=[jax.ShapeDtypeStruct((s, 2 * hw), BF), jax.ShapeDtypeStruct((s, 2 * hw), BF),
                   jax.ShapeDtypeStruct((s, LANE), F32), jax.ShapeDtypeStruct((SUB, LANE), F32)],
        compiler_params=_cp("arbitrary"),
    )(q_raw, kv_raw, z_a, cos, sin, gains, dq_att, dk_att, dv)


def _causal_mask(nrows, ncols, row0):
    rows = lax.broadcasted_iota(jnp.int32, (nrows, ncols), 0) + row0
    cols = lax.broadcasted_iota(jnp.int32, (nrows, ncols), 1)
    return cols <= rows


def _causal_steps(nt, q_major):
    pairs = ([(i, j) for i in range(nt) for j in range(i + 1)] if q_major
             else [(i, j) for j in range(nt) for i in range(j, nt)])
    return (jnp.array([p[0] for p in pairs], jnp.int32), jnp.array([p[1] for p in pairs], jnp.int32))


def _attn_fwd(q_att, k_att, v, heads):
    s = q_att.shape[0]
    t = _pick(s, ATTN_TILE_FWD, LANE)
    nt = s // t
    th = t
    scale = HEAD_QK ** -0.5
    qi, kj = _causal_steps(nt, True)

    def body(qi_ref, kj_ref, q_ref, k_ref, v_ref, o_ref, ob_ref, lse_ref, m_s, l_s, acc_s):
        st = pl.program_id(1)
        i = qi_ref[st]
        j = kj_ref[st]

        @pl.when(j == 0)
        def _():
            m_s[...] = jnp.full_like(m_s, NEG_INF)
            l_s[...] = jnp.zeros_like(l_s)
            acc_s[...] = jnp.zeros_like(acc_s)

        def step(masked):
            for r0 in range(0, t, th):
                rows = slice(r0, r0 + th)
                sc = lax.dot_general(q_ref[rows, :], k_ref[...], (((1,), (1,)), ((), ())),
                                     preferred_element_type=F32) * scale
                if masked:
                    sc = jnp.where(_causal_mask(th, t, r0), sc, NEG_INF)
                m_prev = m_s[rows, :]
                m_new = jnp.maximum(m_prev, jnp.max(sc, axis=-1, keepdims=True))
                alpha = jnp.exp(m_prev - m_new)
                p = jnp.exp(sc - jnp.tile(m_new, (1, t // LANE)))
                l_s[rows, :] = alpha * l_s[rows, :] + jnp.sum(p, axis=-1, keepdims=True)
                acc_s[rows, :] = alpha * acc_s[rows, :] + jnp.dot(p.astype(BF), v_ref[...],
                                                                  preferred_element_type=F32)
                m_s[rows, :] = m_new

        @pl.when(j < i)
        def _():
            step(False)

        @pl.when(j == i)
        def _():
            step(True)
            o = acc_s[...] / l_s[...]
            o_ref[...] = o
            ob_ref[...] = o.astype(BF)
            lse_ref[...] = (m_s[...] + jnp.log(l_s[...]))[:, 0:1]

    q_idx = lambda h, st, qi_r, kj_r: (qi_r[st], h)
    kv_idx = lambda h, st, qi_r, kj_r: (kj_r[st], h)
    return pl.pallas_call(
        body, name="attn_fwd",
        grid_spec=pltpu.PrefetchScalarGridSpec(
            num_scalar_prefetch=2, grid=(heads, qi.shape[0]),
            in_specs=[pl.BlockSpec((t, 2 * LANE), q_idx), pl.BlockSpec((t, 2 * LANE), kv_idx),
                      pl.BlockSpec((t, LANE), kv_idx)],
            out_specs=[pl.BlockSpec((t, LANE), q_idx), pl.BlockSpec((t, LANE), q_idx),
                       pl.BlockSpec((None, t, 1), lambda h, st, qi_r, kj_r: (h, qi_r[st], 0))],
            scratch_shapes=[pltpu.VMEM((t, LANE), F32), pltpu.VMEM((t, LANE), F32), pltpu.VMEM((t, LANE), F32)]),
        out_shape=[jax.ShapeDtypeStruct((s, heads * LANE), F32), jax.ShapeDtypeStruct((s, heads * LANE), BF),
                   jax.ShapeDtypeStruct((heads, s, 1), F32)],
        compiler_params=_cp("parallel", "arbitrary"),
    )(qi, kj, q_att, k_att, v)


def _attn_bwd(q_att, k_att, v, o, lse, d_o, heads, dep=None):
    s = q_att.shape[0]
    t = _pick(s, ATTN_TILE, LANE)
    nt = s // t
    scale = HEAD_QK ** -0.5
    qi, kj = _causal_steps(nt, False)

    def body(qi_ref, kj_ref, q_ref, k_ref, v_ref, do_ref, o_ref, lse_ref, *rest):
        dq_ref, dk_ref, dv_ref, dk_s, dv_s = rest[-5:]
        st = pl.program_id(1)
        i = qi_ref[st]
        j = kj_ref[st]

        @pl.when(st == 0)
        def _():
            dq_ref[...] = jnp.zeros_like(dq_ref)

        @pl.when(i == j)
        def _():
            dk_s[...] = jnp.zeros_like(dk_s)
            dv_s[...] = jnp.zeros_like(dv_s)

        def step(masked):
            q = q_ref[...]
            k = k_ref[...]
            do = do_ref[...]
            sc = lax.dot_general(q, k, (((1,), (1,)), ((), ())), preferred_element_type=F32) * scale
            if masked:
                sc = jnp.where(_causal_mask(t, t, 0), sc, NEG_INF)
            p = jnp.exp(sc - lse_ref[...])
            dp = lax.dot_general(do, v_ref[...], (((1,), (1,)), ((), ())), preferred_element_type=F32)
            delta = jnp.sum(do.astype(F32) * o_ref[...], axis=-1, keepdims=True)
            ds = (p * (dp - delta) * scale).astype(BF)
            dv_s[...] += lax.dot_general(p.astype(BF), do, (((0,), (0,)), ((), ())), preferred_element_type=F32)
            dk_s[...] += lax.dot_general(ds, q, (((0,), (0,)), ((), ())), preferred_element_type=F32)
            rows = pl.ds(pl.multiple_of(i * t, t), t)
            dq_ref[rows, :] += jnp.dot(ds, k, preferred_element_type=F32)

        @pl.when(i > j)
        def _():
            step(False)

        @pl.when(i == j)
        def _():
            step(True)

        @pl.when(i == nt - 1)
        def _():
            dk_ref[...] = dk_s[...]
            dv_ref[...] = dv_s[...]

    q_idx = lambda h, st, qi_r, kj_r: (qi_r[st], h)
    kv_idx = lambda h, st, qi_r, kj_r: (kj_r[st], h)
    in_specs = [pl.BlockSpec((t, 2 * LANE), q_idx), pl.BlockSpec((t, 2 * LANE), kv_idx),
                pl.BlockSpec((t, LANE), kv_idx), pl.BlockSpec((t, LANE), q_idx), pl.BlockSpec((t, LANE), q_idx),
                pl.BlockSpec((None, t, 1), lambda h, st, qi_r, kj_r: (h, qi_r[st], 0))]
    args = [q_att, k_att, v, d_o, o, lse]
    if dep is not None:
        in_specs.append(ANY)
        args.append(dep)
    return pl.pallas_call(
        body, name="attn_bwd",
        grid_spec=pltpu.PrefetchScalarGridSpec(
            num_scalar_prefetch=2, grid=(heads, qi.shape[0]),
            in_specs=in_specs,
            out_specs=[pl.BlockSpec((s, 2 * LANE), lambda h, st, qi_r, kj_r: (0, h)),
                       pl.BlockSpec((t, 2 * LANE), kv_idx), pl.BlockSpec((t, LANE), kv_idx)],
            scratch_shapes=[pltpu.VMEM((t, 2 * LANE), F32), pltpu.VMEM((t, LANE), F32)]),
        out_shape=[jax.ShapeDtypeStruct((s, heads * 2 * LANE), F32),
                   jax.ShapeDtypeStruct((s, heads * 2 * LANE), F32),
                   jax.ShapeDtypeStruct((s, heads * LANE), F32)],
        compiler_params=_cp("parallel", "arbitrary"),
    )(qi, kj, *args)


def _loss_head(y, target):
    s, d = y.shape
    tr = _pick(s, ROW_TILE, 8)

    def body(y_ref, t_ref, dy_ref, dyb_ref, l_ref):
        i = pl.program_id(0)
        e = y_ref[...] - t_ref[...]
        dy_ref[...] = e / d
        dyb_ref[...] = (e / d).astype(BF)

        @pl.when(i == 0)
        def _():
            l_ref[...] = jnp.zeros_like(l_ref)

        l_ref[...] += 0.5 * jnp.sum(jnp.mean(e * e, axis=-1, keepdims=True), axis=0, keepdims=True)

    return pl.pallas_call(
        body, name="loss_head", grid=(s // tr,),
        in_specs=[pl.BlockSpec((tr, d), lambda i: (i, 0))] * 2,
        out_specs=[pl.BlockSpec((tr, d), lambda i: (i, 0)), pl.BlockSpec((tr, d), lambda i: (i, 0)),
                   pl.BlockSpec((SUB, LANE), lambda i: (0, 0))],
        out_shape=[jax.ShapeDtypeStruct((s, d), F32), jax.ShapeDtypeStruct((s, d), BF),
                   jax.ShapeDtypeStruct((SUB, LANE), F32)],
        compiler_params=_cp("arbitrary"),
    )(y, target)


def _sum_parts(parts, name):
    n, r, c = parts.shape
    tr = _pick(r, 512, 8)

    def body(p_ref, o_ref):
        g = p_ref[0].astype(F32)
        for k in range(1, n):
            g = g + p_ref[k].astype(F32)
        o_ref[...] = g

    return pl.pallas_call(
        body, name=name, grid=(r // tr,),
        in_specs=[pl.BlockSpec((n, tr, c), lambda i: (0, i, 0))],
        out_specs=pl.BlockSpec((tr, c), lambda i: (i, 0)),
        out_shape=jax.ShapeDtypeStruct((r, c), F32),
        compiler_params=_cp("parallel"),
    )(parts)


def _adamw(parts, w, m, v, name, by_cols=False):
    n, rp, c = parts.shape
    r = w.shape[0]
    assert by_cols or rp == r
    tr, tc = (r, _pick(c, 256, LANE)) if by_cols else (_pick(r, 256, 16 if r % 16 == 0 else 8), c)

    def body(p_ref, w_ref, m_ref, v_ref, g_ref, d_ref, mo_ref, vo_ref):
        g = p_ref[0].astype(F32)
        for k in range(1, n):
            g = g + p_ref[k].astype(F32)
        g = g[:r] if by_cols else g
        m_new = ADAM_B1 * m_ref[...] + (1.0 - ADAM_B1) * g
        v_new = ADAM_B2 * v_ref[...] + (1.0 - ADAM_B2) * jnp.square(g)
        m_hat = m_new / (1.0 - ADAM_B1 ** ADAM_STEP)
        v_hat = v_new / (1.0 - ADAM_B2 ** ADAM_STEP)
        g_ref[...] = g
        d_ref[...] = -ADAM_LR * (m_hat / (jnp.sqrt(v_hat) + ADAM_EPS) + ADAM_WD * w_ref[...])
        mo_ref[...] = m_new
        vo_ref[...] = v_new

    idx = (lambda i: (0, i)) if by_cols else (lambda i: (i, 0))
    spec = pl.BlockSpec((tr, tc), idx)
    sh = jax.ShapeDtypeStruct((r, c), F32)
    return pl.pallas_call(
        body, name=name, grid=(c // tc if by_cols else r // tr,),
        in_specs=[pl.BlockSpec((n, rp if by_cols else tr, tc), lambda i: (0,) + idx(i)), spec, spec, spec],
        out_specs=[spec] * 4, out_shape=[sh] * 4,
        compiler_params=_cp("parallel"),
    )(parts, w, m, v)


def _place():
    x, y, c = lax.axis_index("x"), lax.axis_index("y"), lax.axis_index("c")
    chips = [(1 - x, y), (x, 1 - y), (1 - x, 1 - y)]
    return x, y, c, chips


def _all_gather(shards, name, dep=None):
    n = len(shards)
    deps = [] if dep is None else list(dep)

    def body(*refs):
        ins, outs = refs[:n], refs[n + len(deps):2 * n + len(deps)]
        send_sems, recv_sems, local_sems = refs[2 * n + len(deps):]
        x, y, c, chips = _place()
        me, sibling = (x, y, c), (x, y, 1 - c)

        def slot(w, p):
            return outs[w].at[4 * p[0] + 2 * p[1] + p[2]]

        def copy(w, k, block, to, src=None):
            return pltpu.make_async_remote_copy(
                src_ref=slot(w, block) if src is None else src, dst_ref=slot(w, block),
                send_sem=send_sems.at[w, k], recv_sem=recv_sems.at[w, k], device_id=to, device_id_type=MESH)

        first = []
        for w in range(n):
            first += [copy(w, 1 + j, me, (*chip, c), src=ins[w]) for j, chip in enumerate(chips)]
            first.append(copy(w, 0, me, sibling, src=ins[w]))
        for cp in first:
            cp.start()
        mine = [pltpu.make_async_copy(ins[w], slot(w, me), local_sems.at[w]) for w in range(n)]
        for cp in mine:
            cp.start()
        passed = []
        for w in range(n):
            for j, chip in enumerate(chips):
                copy(w, 1 + j, (*chip, c), me).wait_recv()
                cp = copy(w, 4 + j, (*chip, c), sibling)
                cp.start()
                passed.append(cp)
        for w in range(n):
            copy(w, 0, sibling, me).wait_recv()
            for j, chip in enumerate(chips):
                copy(w, 4 + j, (*chip, 1 - c), me).wait_recv()
        for cp in first + passed:
            cp.wait_send()
        for cp in mine:
            cp.wait()

    return pl.pallas_call(
        body, name=name,
        in_specs=[ANY] * (n + len(deps)), out_specs=[ANY] * n,
        out_shape=[jax.ShapeDtypeStruct((N_DEV,) + a.shape, a.dtype) for a in shards],
        scratch_shapes=[pltpu.SemaphoreType.DMA((n, 7)), pltpu.SemaphoreType.DMA((n, 7)),
                        pltpu.SemaphoreType.DMA((n,))],
    )(*shards, *deps)


HBM = pl.BlockSpec(memory_space=pltpu.HBM)
SEM = pl.BlockSpec(memory_space=pltpu.SEMAPHORE)
EFFECT = pltpu.SideEffectType.DATAFLOW_SIDE_EFFECTING
PEERS = [(dx, dy, dc) for dx in (1, 0) for dy in (1, 0) for dc in (0, 1) if (dx, dy, dc) != (0, 0, 0)]


def _peer(x, y, c, flip):
    dx, dy, dc = flip
    return (1 - x if dx else x, 1 - y if dy else y, 1 - c if dc else c)


def _exchange_copies(srcs, lands, send, recv, loc, gather):
    x, y, c, _ = _place()
    me = 4 * x + 2 * y + c
    remote, local = [], []
    for w in range(len(srcs)):
        for k, flip in enumerate(PEERS):
            px, py, pc = _peer(x, y, c, flip)
            src = srcs[w] if gather else srcs[w].at[4 * px + 2 * py + pc]
            remote.append(pltpu.make_async_remote_copy(
                src_ref=src, dst_ref=lands[w].at[me], send_sem=send[w].at[k], recv_sem=recv[w].at[k],
                device_id=(px, py, pc), device_id_type=MESH))
        local.append(pltpu.make_async_copy(srcs[w] if gather else srcs[w].at[me], lands[w].at[me], loc[w]))
    return remote, local


class _Exchange:
    def __init__(self, srcs, lands, send, recv, loc, token, gather):
        self.srcs, self.lands, self.send, self.recv, self.loc = srcs, lands, send, recv, loc
        self.token, self.gather = token, gather


def _exchange_start(srcs, gather, name, dep=None):
    n = len(srcs)
    deps = [] if dep is None else [dep]
    land_shapes = [((N_DEV,) + a.shape) if gather else a.shape for a in srcs]
    lands = [pltpu.with_memory_space_constraint(lax.empty(sh, a.dtype), pltpu.HBM) for sh, a in zip(land_shapes, srcs)]
    srcs = [pltpu.with_memory_space_constraint(a, pltpu.HBM) for a in srcs]

    def body(*refs):
        src_refs, land_refs = refs[:n], refs[n:2 * n]
        outs = refs[2 * n + len(deps):]
        send, recv, loc = outs[:n], outs[n:2 * n], outs[2 * n:3 * n]
        token = outs[-1]
        remote, local = _exchange_copies(src_refs, land_refs, send, recv, loc, gather)
        for cp in remote + local:
            cp.start()
        token[...] = jnp.zeros_like(token)

    out_shape = ([pltpu.SemaphoreType.DMA((len(PEERS),))] * (2 * n) + [pltpu.SemaphoreType.DMA(())] * n
                 + [pltpu.HBM(a.shape, a.dtype) for a in srcs] + [pltpu.HBM(a.shape, a.dtype) for a in lands]
                 + [jax.ShapeDtypeStruct((SUB, LANE), F32)])
    res = pl.pallas_call(
        body, name=name, out_shape=out_shape,
        in_specs=[HBM] * (2 * n) + [ANY] * len(deps),
        out_specs=[SEM] * (3 * n) + [HBM] * (2 * n) + [pl.BlockSpec(memory_space=pltpu.VMEM)],
        input_output_aliases={i: 3 * n + i for i in range(2 * n)},
        compiler_params=pltpu.CompilerParams(has_side_effects=EFFECT),
    )(*srcs, *lands, *deps)
    return _Exchange(res[3 * n:4 * n], res[4 * n:5 * n], res[:n], res[n:2 * n], res[2 * n:3 * n], res[-1], gather)


def _exchange_wait(ex, idxs, after, name):
    n = len(idxs)
    srcs = [ex.srcs[i] for i in idxs]
    lands = [ex.lands[i] for i in idxs]
    sems = [ex.send[i] for i in idxs] + [ex.recv[i] for i in idxs] + [ex.loc[i] for i in idxs]
    gather = ex.gather

    def body(*refs):
        src_refs, land_refs = refs[:n], refs[n:2 * n]
        send, recv, loc = refs[2 * n:3 * n], refs[3 * n:4 * n], refs[4 * n:5 * n]
        remote, local = _exchange_copies(src_refs, land_refs, send, recv, loc, gather)
        for cp in remote:
            cp.wait_send()
            cp.wait_recv()
        for cp in local:
            cp.wait()

    res = pl.pallas_call(
        body, name=name,
        out_shape=[pltpu.HBM(a.shape, a.dtype) for a in srcs] + [pltpu.HBM(a.shape, a.dtype) for a in lands],
        in_specs=[HBM] * (2 * n) + [SEM] * (3 * n) + [ANY],
        out_specs=[HBM] * (2 * n),
        input_output_aliases={i: i for i in range(2 * n)},
        compiler_params=pltpu.CompilerParams(has_side_effects=EFFECT),
    )(*srcs, *lands, *sems, after)
    return res[n:]


def _after(token, a):
    return a + token[0:1, 0:1].astype(a.dtype)


def _unblock(w3):
    nb, k, nbw = w3.shape
    return w3.transpose(1, 0, 2).reshape(k, nb * nbw)


def _block(w, nb):
    k, n = w.shape
    return w.reshape(k, nb, n // nb).transpose(1, 0, 2)


def kernel(x, positions, ln1_g, w_in, b_gate, conv_w, w_conv_out, q_a_g, w_q_b, kv_a_g, w_kv_b, q_norm_g, k_norm_g, w_mla_out, w_o, ln2_g, w_ffn_up, ffn_conv_w, ffn_conv_b, w_ffn_down, loss_target, m_ln1_g, m_w_in, m_b_gate, m_conv_w, m_w_conv_out, m_q_a_g, m_w_q_b, m_kv_a_g, m_w_kv_b, m_q_norm_g, m_k_norm_g, m_w_mla_out, m_w_o, m_ln2_g, m_w_ffn_up, m_ffn_conv_w, m_ffn_conv_b, m_w_ffn_down, v_ln1_g, v_w_in, v_b_gate, v_conv_w, v_w_conv_out, v_q_a_g, v_w_q_b, v_kv_a_g, v_w_kv_b, v_q_norm_g, v_k_norm_g, v_w_mla_out, v_w_o, v_ln2_g, v_w_ffn_up, v_ffn_conv_w, v_ffn_conv_b, v_w_ffn_down):
    s, d = x.shape[1], x.shape[2]
    conv = conv_w.shape[2] * N_DEV
    ql, kvl = q_a_g.shape[1], kv_a_g.shape[1]
    heads = w_q_b.shape[2] * N_DEV // HEAD_QK
    dff = w_ffn_down.shape[1] * N_DEV
    hw = heads * LANE
    conv3 = 3 * conv
    kr_off = conv3 + ql
    kv_off = -(-(kr_off + LANE) // kvl) * kvl
    wa = kv_off + kvl
    assert conv3 % ql == 0 and kr_off % LANE == 0
    xs = x[0]
    tgt = loss_target[0]
    pos = positions.reshape(s, 1)

    nin = w_in.shape[2]
    big = dict(w_in=w_in[0].T, w_conv_out=w_conv_out[0], w_q_b=w_q_b[0], w_kv_b=w_kv_b[0],
               w_mla_out=w_mla_out[0], w_o=w_o[0], w_ffn_up=w_ffn_up[0], w_ffn_down=w_ffn_down[0])
    names = list(big)
    rest = names[1:]
    nin_p = -(-nin // 16) * 16
    first = _all_gather([jnp.pad(big["w_in"].astype(BF), ((0, nin_p - nin), (0, 0))), _pad8(conv_w[0]),
                         _pad8(ffn_conv_w[0])], "gather_w_in")
    cw8 = _unblock(first[1])
    fcw8 = _unblock(first[2])
    ag = _exchange_start([big[k].astype(BF) for k in rest], True, "gather_rest_start", dep=first[1])

    def landed(keys, after, name):
        return _exchange_wait(ag, [rest.index(k) for k in keys], after, name)

    w_in_t = first[0][:, :nin].reshape(N_DEV * nin, d)
    g_off = kr_off + kvl + ROPE
    w_a_t = jnp.concatenate([w_in_t[:kr_off], _lay_rows(w_in_t[kr_off + kvl:g_off]),
                             jnp.zeros((kv_off - kr_off - LANE, d), BF), w_in_t[kr_off:kr_off + kvl]], axis=0)[None]
    w_ga_t = w_in_t[g_off:g_off + d][None]
    w_gb_t = w_in_t[g_off + d:g_off + 2 * d][None]
    gains = _pad8(jnp.concatenate([q_norm_g[:, :NOPE], _lay(q_norm_g[:, NOPE:]),
                                   k_norm_g[:, :NOPE], _lay(k_norm_g[:, NOPE:])], axis=0))
    kr_blk = kr_off // LANE

    cos, sin = _rope_tables(pos)
    u1 = _rms_fwd(xs, _after(ag.token, ln1_g), d, 0, "rms1_fwd")
    z_a = _mm_nt(u1, w_a_t, "mm_z_a")
    z_ga = _mm_nt(u1, w_ga_t, "mm_z_ga")
    z_gb = _mm_nt(u1, w_gb_t, "mm_z_gb")
    p = _conv_mix_fwd(z_a, cw8, conv)
    w_co, w_qb, w_kv = landed(["w_conv_out", "w_q_b", "w_kv_b"], p, "gather_wait_mixers")
    wq_full = _unblock(w_qb).reshape(ql, heads, HEAD_QK)
    w_q = jnp.concatenate([wq_full[:, :, :NOPE].reshape(ql, hw), _lay(wq_full[:, :, NOPE:]).reshape(ql, hw)],
                          axis=1)[None]
    yc = _mm_nn(p, w_co, "mm_y_conv")
    qn = _rms_fwd(z_a, q_a_g, ql, conv3 // ql, "rms_q_fwd")
    kvn = _rms_fwd(z_a, kv_a_g, kvl, kv_off // kvl, "rms_kv_fwd")
    q_raw = _mm_nn(qn, w_q, "mm_q")
    kv_raw = _mm_nn(kvn, w_kv, "mm_kv")
    q_att, k_att, v_bf = _head_fwd(q_raw, kv_raw, z_a, kr_blk, cos, sin, gains, heads)
    o, o_bf, lse = _attn_fwd(q_att, k_att, v_bf, heads)
    w_mo, w_oo = landed(["w_mla_out", "w_o"], lse, "gather_wait_outs")
    w_mo = w_mo.reshape(1, hw, d)
    w_oo = w_oo.reshape(1, d, d)
    ym = _mm_nn(o_bf, w_mo, "mm_y_mla")
    mix = _gate_fwd(z_ga, z_gb, b_gate, yc, ym, d)
    h1 = _mm_nn(mix, w_oo, "mm_h1", add=xs)
    u2 = _rms_fwd(h1, ln2_g, d, 0, "rms2_fwd")
    w_up, = landed(["w_ffn_up"], u2, "gather_wait_ffn_up")
    a_pre = _mm_nn(u2, w_up, "mm_ffn_up")
    f = _ffn_act_fwd(a_pre, fcw8, ffn_conv_b, dff)
    w_dn, = landed(["w_ffn_down"], f, "gather_wait_ffn_down")
    w_dn = w_dn.reshape(1, dff, d)
    y = _mm_nn(f, w_dn, "mm_ffn_down", add=h1)
    dy, dy_bf, loss_part = _loss_head(y, tgt)

    g_dn = _mm_tn(f, dy_bf, 1, "mm_g_ffn_down").reshape(N_DEV, dff // N_DEV, d)
    rs_dn = _exchange_start([g_dn], False, "reduce_ffn_down_start")
    d_f = _mm_nt(dy_bf, w_dn, "mm_d_f", dep=rs_dn.token)
    d_xg, d_xu, dfw_g, dfw_u = _ffn_act_bwd(a_pre, d_f, fcw8, ffn_conv_b, dff)
    half = N_DEV // 2
    g_up = _mm_tn(u2, d_xg, half, "mm_g_ffn_up_gate", into=lax.empty((N_DEV, d, 2 * dff // N_DEV), BF))
    g_up = _mm_tn(u2, d_xu, half, "mm_g_ffn_up_up", into=g_up, blk0=half)
    rs_up = _exchange_start([g_up], False, "reduce_ffn_up_start")
    d_u2 = _mm_nt(d_xg, w_up, "mm_d_u2_gate", blk0=0, nblk=half, dep=rs_up.token)
    d_u2 = _mm_nt(d_xu, w_up, "mm_d_u2_up", blk0=half, nblk=half, add=d_u2)
    d_h1, d_h1_bf, dg_ln2 = _rms_bwd(h1, d_u2, ln2_g, d, 0, "rms2_bwd", extra=dy, also_bf16=True)
    g_oo = _mm_tn(mix, d_h1_bf, 1, "mm_g_w_o").reshape(N_DEV, d // N_DEV, d)
    d_mix = _mm_nt(d_h1_bf, w_oo, "mm_d_mix")
    d_zga, d_zgb, d_yc, d_ym, dba, dbb = _gate_bwd(d_mix, z_ga, z_gb, b_gate, yc, ym, d)
    g_co = _mm_tn(p, d_yc, N_DEV, "mm_g_conv_out")
    g_mo = _mm_tn(o_bf, d_ym, 1, "mm_g_mla_out").reshape(N_DEV, hw // N_DEV, d)
    rs_mix = _exchange_start([g_oo, g_co, g_mo], False, "reduce_mixers_start")
    d_p = _mm_nt(d_yc, w_co, "mm_d_p", dep=rs_mix.token)
    d_o = _mm_nt(d_ym, w_mo, "mm_d_o", out_dtype=BF)
    d_zb, d_zc, d_zv, dcw = _conv_mix_bwd(z_a, d_p, cw8, conv)
    dq_att, dk_att, dv = _attn_bwd(q_att, k_att, v_bf, o, lse, d_o, heads, dep=rs_mix.token)
    d_q_raw, d_kv_raw, d_kr, dgains = _head_bwd(q_raw, kv_raw, z_a, kr_blk, cos, sin, gains, dq_att, dk_att, dv, heads)
    g_q2 = _mm_tn(qn, d_q_raw, 1, "mm_g_q")[0]
    g_qb = _block(jnp.concatenate([g_q2[:, :hw].reshape(ql, heads, NOPE),
                                   _unlay(g_q2[:, hw:].reshape(ql, heads, LANE))], axis=2).reshape(ql, heads * HEAD_QK), N_DEV)
    g_kv = _mm_tn(kvn, d_kv_raw, N_DEV, "mm_g_kv")
    rs_qkv = _exchange_start([g_qb, g_kv], False, "reduce_qkv_start")
    d_qn = _mm_nt(d_q_raw, w_q, "mm_d_qn", dep=rs_qkv.token)
    d_kvn = _mm_nt(d_kv_raw, w_kv, "mm_d_kvn")
    d_ql, dg_qa = _rms_bwd(z_a, d_qn, q_a_g, ql, conv3 // ql, "rms_q_bwd", out_dtype=BF)
    d_kvl, dg_kva = _rms_bwd(z_a, d_kvn, kv_a_g, kvl, kv_off // kvl, "rms_kv_bwd", out_dtype=BF)
    d_z_a = jnp.concatenate([d_zb, d_zc, d_zv, d_ql, d_kr.astype(BF), jnp.zeros((s, kv_off - kr_off - LANE), BF),
                             d_kvl], axis=1)
    g_a = _mm_tn(d_z_a, u1, 1, "mm_g_w_a")[0]
    g_ga = _mm_tn(d_zga, u1, 1, "mm_g_w_ga")[0]
    g_gb = _mm_tn(d_zgb, u1, 1, "mm_g_w_gb")[0]
    g_in = jnp.concatenate([g_a[:kr_off], g_a[kv_off:kv_off + kvl], g_a[kr_off:kr_off + HALF],
                            g_a[kr_off + 2 * HALF:kr_off + 3 * HALF], g_ga, g_gb], axis=0).reshape(N_DEV, nin, d)
    g_in = jnp.pad(g_in, ((0, 0), (0, nin_p - nin), (0, 0)))
    rs_in = _exchange_start([g_in], False, "reduce_w_in_start")
    d_u1 = _mm_nn(d_z_a, w_a_t, "mm_d_u1_a", dep=rs_in.token)
    d_u1 = _mm_nn(d_zga, w_ga_t, "mm_d_u1_ga", add=d_u1)
    d_u1 = _mm_nn(d_zgb, w_gb_t, "mm_d_u1_gb", add=d_u1)
    grad_x, dg_ln1 = _rms_bwd(xs, d_u1, ln1_g, d, 0, "rms1_bwd", extra=d_h1)

    summed = {}
    summed["w_ffn_down"], = _exchange_wait(rs_dn, [0], grad_x, "reduce_ffn_down_wait")
    summed["w_ffn_up"], = _exchange_wait(rs_up, [0], grad_x, "reduce_ffn_up_wait")
    summed["w_o"], summed["w_conv_out"], summed["w_mla_out"] = _exchange_wait(rs_mix, [0, 1, 2], grad_x, "reduce_mixers_wait")
    summed["w_q_b"], summed["w_kv_b"] = _exchange_wait(rs_qkv, [0, 1], grad_x, "reduce_qkv_wait")
    loc = locals()
    out = {}
    for k in rest:
        out[k] = _adamw(summed[k], big[k], loc["m_" + k][0], loc["v_" + k][0], "adamw_" + k)

    small = dict(ln1_g=dg_ln1[0:1], b_gate=jnp.concatenate([dba[0:1], dbb[0:1]], axis=1), q_a_g=dg_qa[0:1],
                 kv_a_g=dg_kva[0:1],
                 q_norm_g=jnp.concatenate([dgains[0:1], _unlay(dgains[1:2])], axis=1),
                 k_norm_g=jnp.concatenate([dgains[2:3], _unlay(dgains[3:4])], axis=1),
                 ln2_g=dg_ln2[0:1], ffn_conv_b=jnp.concatenate([dfw_g[3:4], dfw_u[3:4]], axis=1))
    small_names = list(small)
    extra = [dcw[0:3].reshape(1, -1), jnp.concatenate([dfw_g[0:3], dfw_u[0:3]], axis=1).reshape(1, -1),
             loss_part[0:1, 0:1]]
    flat = jnp.concatenate([small[k] for k in small_names] + extra, axis=1)
    n_flat = flat.shape[1]
    rows = -(-n_flat // (SUB * LANE)) * SUB
    flat = jnp.pad(flat, ((0, 0), (0, rows * LANE - n_flat))).reshape(rows, LANE)
    total = _sum_parts(_all_gather([flat], "gather_small", dep=[out[k][0] for k in rest])[0], "sum_small").reshape(1, rows * LANE)
    off = 0
    small_g = {}
    for k in small_names:
        small_g[k] = total[:, off:off + small[k].shape[1]]
        off += small[k].shape[1]
    me = 4 * lax.axis_index("x") + 2 * lax.axis_index("y") + lax.axis_index("c")
    cwn, fcwn = conv // N_DEV, 2 * dff // N_DEV
    g_cw = lax.dynamic_slice_in_dim(total[:, off:off + 3 * conv].reshape(3, conv), me * cwn, cwn, axis=1)
    off += 3 * conv
    g_fcw = lax.dynamic_slice_in_dim(total[:, off:off + 6 * dff].reshape(3, 2 * dff), me * fcwn, fcwn, axis=1)
    off += 6 * dff
    loss = total[0, off]

    summed["w_in"], = _exchange_wait(rs_in, [0], total, "reduce_w_in_wait")
    out["w_in"] = [r.T for r in _adamw(summed["w_in"], big["w_in"], m_w_in[0].T, v_w_in[0].T, "adamw_w_in",
                                       by_cols=True)]
    small_w = dict(ln1_g=ln1_g, b_gate=b_gate, q_a_g=q_a_g, kv_a_g=kv_a_g, q_norm_g=q_norm_g, k_norm_g=k_norm_g,
                   ln2_g=ln2_g, ffn_conv_b=ffn_conv_b, conv_w=conv_w[0].reshape(1, -1),
                   ffn_conv_w=ffn_conv_w[0].reshape(1, -1))
    small_g["conv_w"] = g_cw.reshape(1, -1)
    small_g["ffn_conv_w"] = g_fcw.reshape(1, -1)
    packed_names = list(small_w)

    def pack(get):
        vflat = jnp.concatenate([get(k).reshape(1, -1) for k in packed_names], axis=1)
        nr = -(-vflat.shape[1] // (SUB * LANE)) * SUB
        return jnp.pad(vflat, ((0, 0), (0, nr * LANE - vflat.shape[1])), constant_values=1.0).reshape(nr, LANE)

    res = _adamw(pack(lambda k: small_g[k])[None], pack(lambda k: small_w[k]), pack(lambda k: loc["m_" + k]),
                 pack(lambda k: loc["v_" + k]), "adamw_small")
    res = [r.reshape(1, -1) for r in res]
    off = 0
    for k in packed_names:
        shape = loc[k].shape
        size = small_w[k].shape[1]
        out[k] = [r[:, off:off + size].reshape(shape) for r in res]
        off += size
    for k in names:
        out[k] = [r[None] for r in out[k]]

    order = ["ln1_g", "w_in", "b_gate", "conv_w", "w_conv_out", "q_a_g", "w_q_b", "kv_a_g", "w_kv_b", "q_norm_g",
             "k_norm_g", "w_mla_out", "w_o", "ln2_g", "w_ffn_up", "ffn_conv_w", "ffn_conv_b", "w_ffn_down"]
    return (loss, grad_x[None], *[out[k][0] for k in order], *[out[k][1] for k in order],
            *[out[k][2] for k in order], *[out[k][3] for k in order])
```

```python
import functools

import jax
import jax.numpy as jnp
from jax import lax
from jax.experimental import pallas as pl
from jax.experimental.pallas import tpu as pltpu

BF = jnp.bfloat16
F32 = jnp.float32
MESH = pl.DeviceIdType.MESH
N_DEV = 8

NOPE = 128
ROPE = 64
HALF = ROPE // 2
HEAD_QK = NOPE + ROPE
HEAD_V = 128
LANE = 128
SUB = 8
NORM_EPS = 1e-6
NEG_INF = -1e30
ROPE_THETA = 10000.0
ADAM_LR = 0.001
ADAM_B1 = 0.9
ADAM_B2 = 0.999
ADAM_EPS = 1e-08
ADAM_WD = 0.01
ADAM_STEP = 10

VMEM_LIMIT = 52 * 1024 * 1024
MM_TM, MM_TN, MM_TK, MM_TS = 1024, 1536, 2048, 1024
ROW_TILE, ROW_TILE_BWD = 512, 256
HEAD_ROW_TILE, HEAD_ROW_TILE_BWD = 256, 128
COL_TILE = 512
ATTN_TILE = 1024
ATTN_TILE_FWD = 1024
ANY = pl.BlockSpec(memory_space=pl.ANY)


def _pick(n, target, mult):
    t = (min(n, target) // mult) * mult
    while t > 0:
        if n % t == 0:
            return t
        t -= mult
    raise ValueError(f"no tile for {n} (target {target}, multiple {mult})")


def _cp(*sem):
    return pltpu.CompilerParams(dimension_semantics=sem, vmem_limit_bytes=VMEM_LIMIT)


def _accumulate(kk, nk, acc, part, finish):
    if nk == 1:
        finish(part())
        return

    @pl.when(kk == 0)
    def _():
        acc[...] = part()

    @pl.when((kk > 0) & (kk < nk - 1))
    def _():
        acc[...] += part()

    @pl.when(kk == nk - 1)
    def _():
        finish(acc[...] + part())


def _mm_call(body, name, grid, in_specs, args, out_spec, out_shape, acc_shape, nk, dep):
    if dep is not None:
        in_specs = in_specs + [ANY]
        args = args + [dep]
    return pl.pallas_call(
        body, name=name, grid=grid, in_specs=in_specs, out_specs=out_spec, out_shape=out_shape,
        scratch_shapes=[pltpu.VMEM(acc_shape, F32)] if nk > 1 else [],
        compiler_params=_cp("parallel", "parallel", "arbitrary"),
    )(*args)


def _mm_nn_loss(a, b3, add, target, name):
    m, k = a.shape
    _, k2, n = b3.shape
    assert k == k2 and b3.shape[0] == 1
    tm = _pick(m, MM_TM, 16)
    tn = _pick(n, MM_TN, LANE)
    tk = _pick(k, MM_TK, LANE)
    nk = k // tk

    def body(a_ref, b_ref, c_ref, t_ref, dy_ref, dyb_ref, l_ref, acc):
        kk = pl.program_id(2)

        @pl.when((pl.program_id(0) == 0) & (pl.program_id(1) == 0) & (kk == 0))
        def _():
            l_ref[...] = jnp.zeros_like(l_ref)

        def part():
            return jnp.dot(a_ref[...].astype(BF), b_ref[0].astype(BF), preferred_element_type=F32)

        def finish(r):
            e = r + c_ref[...] - t_ref[...]
            dy_ref[...] = e / n
            dyb_ref[...] = (e / n).astype(BF)
            l_ref[...] += 0.5 * jnp.sum(jnp.sum(e * e, axis=-1, keepdims=True), axis=0, keepdims=True) / n

        _accumulate(kk, nk, acc, part, finish)

    tile = pl.BlockSpec((tm, tn), lambda i, j, kk: (i, j))
    return pl.pallas_call(
        body, name=name, grid=(m // tm, n // tn, nk),
        in_specs=[pl.BlockSpec((tm, tk), lambda i, j, kk: (i, kk)),
                  pl.BlockSpec((1, tk, tn), lambda i, j, kk: (0, kk, j)), tile, tile],
        out_specs=[tile, tile, pl.BlockSpec((SUB, LANE), lambda i, j, kk: (0, 0))],
        out_shape=[jax.ShapeDtypeStruct((m, n), F32), jax.ShapeDtypeStruct((m, n), BF),
                   jax.ShapeDtypeStruct((SUB, LANE), F32)],
        scratch_shapes=[pltpu.VMEM((tm, tn), F32)],
        compiler_params=_cp("arbitrary", "arbitrary", "arbitrary"),
    )(a, b3, add, target)


def _mm_nn(a, b3, name, add=None, out_dtype=F32, blk0=0, nblk=None, dep=None):
    m, k = a.shape
    nb_all, k2, nbw = b3.shape
    assert k == k2
    nblk = nb_all - blk0 if nblk is None else nblk
    n = nblk * nbw
    tm = _pick(m, MM_TM, 16)
    tn = _pick(nbw, MM_TN, LANE)
    tk = _pick(k, MM_TK, LANE)
    per = nbw // tn
    nk = k // tk

    def body(*refs):
        a_ref, b_ref = refs[:2]
        c_ref = refs[2] if add is not None else None
        o_ref = refs[2 + (add is not None) + (dep is not None)]
        acc = refs[-1]

        def part():
            return jnp.dot(a_ref[...].astype(BF), b_ref[...].astype(BF), preferred_element_type=F32)

        def finish(r):
            if add is not None:
                r = r + c_ref[...]
            o_ref[...] = r.astype(out_dtype)

        _accumulate(pl.program_id(2), nk, acc, part, finish)

    in_specs = [pl.BlockSpec((tm, tk), lambda i, j, kk: (i, kk)),
                pl.BlockSpec((None, tk, tn), lambda i, j, kk: (blk0 + j // per, kk, j % per))]
    args = [a, b3]
    if add is not None:
        in_specs.append(pl.BlockSpec((tm, tn), lambda i, j, kk: (i, j)))
        args.append(add)
    return _mm_call(body, name, (m // tm, n // tn, nk), in_specs, args,
                    pl.BlockSpec((tm, tn), lambda i, j, kk: (i, j)), jax.ShapeDtypeStruct((m, n), out_dtype),
                    (tm, tn), nk, dep)


def _mm_nt(a, b3, name, add=None, out_dtype=F32, blk0=0, nblk=None, dep=None):
    m, n = a.shape
    nb_all, k, nbw = b3.shape
    nblk = nb_all - blk0 if nblk is None else nblk
    assert n == nblk * nbw
    tm = _pick(m, MM_TM, 16)
    tn = _pick(k, MM_TN, LANE)
    tk = _pick(nbw, MM_TK, LANE)
    per = nbw // tk
    nk = n // tk

    def body(*refs):
        a_ref, b_ref = refs[:2]
        c_ref = refs[2] if add is not None else None
        o_ref = refs[2 + (add is not None) + (dep is not None)]
        acc = refs[-1]

        def part():
            return lax.dot_general(a_ref[...].astype(BF), b_ref[...].astype(BF),
                                   (((1,), (1,)), ((), ())), preferred_element_type=F32)

        def finish(r):
            if add is not None:
                r = r + c_ref[...]
            o_ref[...] = r.astype(out_dtype)

        _accumulate(pl.program_id(2), nk, acc, part, finish)

    in_specs = [pl.BlockSpec((tm, tk), lambda i, j, kk: (i, kk)),
                pl.BlockSpec((None, tn, tk), lambda i, j, kk: (blk0 + kk // per, j, kk % per))]
    args = [a, b3]
    if add is not None:
        in_specs.append(pl.BlockSpec((tm, tn), lambda i, j, kk: (i, j)))
        args.append(add)
    return _mm_call(body, name, (m // tm, k // tn, nk), in_specs, args,
                    pl.BlockSpec((tm, tn), lambda i, j, kk: (i, j)), jax.ShapeDtypeStruct((m, k), out_dtype),
                    (tm, tn), nk, dep)


def _mm_tn(a, b, nblk, name, out_dtype=BF, dep=None, into=None, blk0=0):
    s, m = a.shape
    s2, n = b.shape
    assert s == s2 and n % nblk == 0 and (dep is None or into is None)
    nbw = n // nblk
    tm = _pick(m, MM_TN, LANE)
    tn = _pick(nbw, MM_TN, LANE)
    ts = _pick(s, MM_TS, LANE)
    per = nbw // tn
    ns = s // ts

    def body(*refs):
        a_ref, b_ref = refs[:2]
        o_ref = refs[2 + (dep is not None or into is not None)]
        acc = refs[-1]

        def part():
            return lax.dot_general(a_ref[...].astype(BF), b_ref[...].astype(BF),
                                   (((0,), (0,)), ((), ())), preferred_element_type=F32)

        def finish(r):
            o_ref[...] = r.astype(out_dtype)

        _accumulate(pl.program_id(2), ns, acc, part, finish)

    in_specs = [pl.BlockSpec((ts, tm), lambda i, j, ss: (ss, i)),
                pl.BlockSpec((ts, tn), lambda i, j, ss: (ss, j))]
    out_spec = pl.BlockSpec((None, tm, tn), lambda i, j, ss: (blk0 + j // per, i, j % per))
    if into is None:
        return _mm_call(body, name, (m // tm, n // tn, ns), in_specs, [a, b], out_spec,
                        jax.ShapeDtypeStruct((nblk, m, nbw), out_dtype), (tm, tn), ns, dep)
    assert into.shape[1:] == (m, nbw) and into.dtype == out_dtype
    return pl.pallas_call(
        body, name=name, grid=(m // tm, n // tn, ns), in_specs=in_specs + [ANY], out_specs=out_spec,
        out_shape=jax.ShapeDtypeStruct(into.shape, out_dtype), input_output_aliases={2: 0},
        scratch_shapes=[pltpu.VMEM((tm, tn), F32)] if ns > 1 else [],
        compiler_params=_cp("parallel", "parallel", "arbitrary"),
    )(a, b, into)


def _rows8(rows, width):
    idx = lax.broadcasted_iota(jnp.int32, (SUB, width), 0)
    out = jnp.zeros((SUB, width), F32)
    for r, v in enumerate(rows):
        out = jnp.where(idx == r, v, out)
    return out


def _rms_fwd(x, g, width, col_blk, name):
    s = x.shape[0]
    tr = _pick(s, ROW_TILE, 16)

    def body(x_ref, g_ref, u_ref):
        xv = x_ref[...]
        r = lax.rsqrt(jnp.mean(xv * xv, axis=-1, keepdims=True) + NORM_EPS)
        u_ref[...] = ((xv * r) * g_ref[...]).astype(BF)

    return pl.pallas_call(
        body, name=name, grid=(s // tr,),
        in_specs=[pl.BlockSpec((tr, width), lambda i: (i, col_blk)),
                  pl.BlockSpec((1, width), lambda i: (0, 0))],
        out_specs=pl.BlockSpec((tr, width), lambda i: (i, 0)),
        out_shape=jax.ShapeDtypeStruct((s, width), BF),
        compiler_params=_cp("parallel"),
    )(x, g)


def _rms_bwd(x, du, g, width, col_blk, name, extra=None, out_dtype=F32, also_bf16=False):
    s = x.shape[0]
    tr = _pick(s, ROW_TILE_BWD, 16)

    def body(*refs):
        x_ref, du_ref, g_ref = refs[:3]
        e_ref = refs[3] if extra is not None else None
        dx_ref = refs[3 + (extra is not None)]
        dxb_ref = refs[4 + (extra is not None)] if also_bf16 else None
        dg_ref = refs[-1]
        i = pl.program_id(0)
        xv = x_ref[...]
        duv = du_ref[...].astype(F32)
        r = lax.rsqrt(jnp.mean(xv * xv, axis=-1, keepdims=True) + NORM_EPS)
        nv = xv * r
        dn = duv * g_ref[...]
        dx = r * (dn - nv * jnp.mean(dn * nv, axis=-1, keepdims=True))
        if extra is not None:
            dx = dx + e_ref[...]
        dx_ref[...] = dx.astype(out_dtype)
        if also_bf16:
            dxb_ref[...] = dx.astype(BF)

        @pl.when(i == 0)
        def _():
            dg_ref[...] = jnp.zeros_like(dg_ref)

        dg_ref[...] += _rows8([jnp.sum(duv * nv, axis=0, keepdims=True)], width)

    in_specs = [pl.BlockSpec((tr, width), lambda i: (i, col_blk)),
                pl.BlockSpec((tr, width), lambda i: (i, 0)),
                pl.BlockSpec((1, width), lambda i: (0, 0))]
    args = [x, du, g]
    if extra is not None:
        in_specs.append(pl.BlockSpec((tr, width), lambda i: (i, 0)))
        args.append(extra)
    return pl.pallas_call(
        body, name=name, grid=(s // tr,),
        in_specs=in_specs,
        out_specs=[pl.BlockSpec((tr, width), lambda i: (i, 0))] * (1 + also_bf16)
        + [pl.BlockSpec((SUB, width), lambda i: (0, 0))],
        out_shape=[jax.ShapeDtypeStruct((s, width), out_dtype)] + [jax.ShapeDtypeStruct((s, width), BF)] * also_bf16
        + [jax.ShapeDtypeStruct((SUB, width), F32)],
        compiler_params=_cp("arbitrary"),
    )(*args)


def _down(cur, prev8, k):
    ext = jnp.concatenate([prev8, cur], axis=0)
    return pltpu.roll(ext, k, axis=0)[SUB:]


def _up(cur, next8, k):
    ext = jnp.concatenate([cur, next8], axis=0)
    return pltpu.roll(ext, ext.shape[0] - k, axis=0)[:cur.shape[0]]


def _lags(cur, prev8):
    return _down(cur, prev8, 1), _down(cur, prev8, 2)


def _conv3(w_ref, cur, prev8, lags=None):
    lag1, lag2 = _lags(cur, prev8) if lags is None else lags
    return w_ref[0:1, :] * lag2 + w_ref[1:2, :] * lag1 + w_ref[2:3, :] * cur


def _conv3_t(w_ref, cur, next8):
    return w_ref[2:3, :] * cur + w_ref[1:2, :] * _up(cur, next8, 1) + w_ref[0:1, :] * _up(cur, next8, 2)


def _spec_cur(tr, tc, c0):
    return pl.BlockSpec((tr, tc), lambda j, i: (i, c0 + j))


def _spec_prev(tr, tc, c0):
    return pl.BlockSpec((SUB, tc), lambda j, i: (jnp.maximum(i * (tr // SUB) - 1, 0), c0 + j))


def _spec_next(tr, tc, c0, s):
    return pl.BlockSpec((SUB, tc), lambda j, i: (jnp.minimum((i + 1) * (tr // SUB), s // SUB - 1), c0 + j))


def _spec_w(tc, c0):
    return pl.BlockSpec((SUB, tc), lambda j, i: (0, c0 + j))


def _pad8(w):
    return jnp.pad(w, ((0, SUB - w.shape[0]), (0, 0)))


def _conv_mix_fwd(z_a, cw8, conv):
    s = z_a.shape[0]
    tr = _pick(s, ROW_TILE, 16)
    tc = _pick(conv, COL_TILE, LANE)
    nc = conv // tc

    def body(zb_ref, zc_ref, zv_ref, zcp_ref, zvp_ref, w_ref, p_ref):
        i = pl.program_id(1)
        cv = zc_ref[...] * zv_ref[...]
        cvp = jnp.where(i > 0, zcp_ref[...] * zvp_ref[...], 0.0)
        p_ref[...] = (zb_ref[...] * _conv3(w_ref, cv, cvp)).astype(BF)

    return pl.pallas_call(
        body, name="conv_mix_fwd", grid=(nc, s // tr),
        in_specs=[_spec_cur(tr, tc, 0), _spec_cur(tr, tc, nc), _spec_cur(tr, tc, 2 * nc),
                  _spec_prev(tr, tc, nc), _spec_prev(tr, tc, 2 * nc), _spec_w(tc, 0)],
        out_specs=_spec_cur(tr, tc, 0),
        out_shape=jax.ShapeDtypeStruct((s, conv), BF),
        compiler_params=_cp("parallel", "parallel"),
    )(z_a, z_a, z_a, z_a, z_a, cw8)


def _conv_mix_bwd(z_a, d_p, cw8, conv):
    s = z_a.shape[0]
    tr = _pick(s, ROW_TILE_BWD, 16)
    tc = _pick(conv, COL_TILE, LANE)
    nc = conv // tc
    nr = s // tr

    def body(zb_ref, zbn_ref, zc_ref, zcp_ref, zv_ref, zvp_ref, dp_ref, dpn_ref, w_ref,
             dzb_ref, dzc_ref, dzv_ref, dw_ref):
        i = pl.program_id(1)
        zc = zc_ref[...]
        zv = zv_ref[...]
        cv = zc * zv
        cvp = jnp.where(i > 0, zcp_ref[...] * zvp_ref[...], 0.0)
        cv1, cv2 = _lags(cv, cvp)
        dpv = dp_ref[...]
        dzb_ref[...] = (dpv * _conv3(w_ref, cv, cvp, (cv1, cv2))).astype(BF)
        dcc = dpv * zb_ref[...]
        dccn = jnp.where(i < nr - 1, dpn_ref[...] * zbn_ref[...], 0.0)
        dcv = _conv3_t(w_ref, dcc, dccn)
        dzc_ref[...] = (dcv * zv).astype(BF)
        dzv_ref[...] = (dcv * zc).astype(BF)

        @pl.when(i == 0)
        def _():
            dw_ref[...] = jnp.zeros_like(dw_ref)

        dw_ref[...] += _rows8([jnp.sum(dcc * cv2, axis=0, keepdims=True),
                               jnp.sum(dcc * cv1, axis=0, keepdims=True),
                               jnp.sum(dcc * cv, axis=0, keepdims=True)], tc)

    out = jax.ShapeDtypeStruct((s, conv), BF)
    return pl.pallas_call(
        body, name="conv_mix_bwd", grid=(nc, nr),
        in_specs=[_spec_cur(tr, tc, 0), _spec_next(tr, tc, 0, s),
                  _spec_cur(tr, tc, nc), _spec_prev(tr, tc, nc),
                  _spec_cur(tr, tc, 2 * nc), _spec_prev(tr, tc, 2 * nc),
                  _spec_cur(tr, tc, 0), _spec_next(tr, tc, 0, s), _spec_w(tc, 0)],
        out_specs=[_spec_cur(tr, tc, 0), _spec_cur(tr, tc, 0), _spec_cur(tr, tc, 0), _spec_w(tc, 0)],
        out_shape=[out, out, out, jax.ShapeDtypeStruct((SUB, conv), F32)],
        compiler_params=_cp("parallel", "arbitrary"),
    )(z_a, z_a, z_a, z_a, z_a, z_a, d_p, d_p, cw8)


def _silu_parts(ag):
    sg = jax.nn.sigmoid(ag)
    return ag * sg, sg


def _ffn_act_fwd(a_pre, cw8, cb, dff):
    s = a_pre.shape[0]
    tr = _pick(s, ROW_TILE, 16)
    tc = _pick(dff, COL_TILE, LANE)
    nc = dff // tc

    def body(xg_ref, xgp_ref, xu_ref, xup_ref, wg_ref, wu_ref, bg_ref, bu_ref, f_ref):
        i = pl.program_id(1)
        xgp = jnp.where(i > 0, xgp_ref[...], 0.0)
        xup = jnp.where(i > 0, xup_ref[...], 0.0)
        ag = _conv3(wg_ref, xg_ref[...], xgp) + bg_ref[...]
        au = _conv3(wu_ref, xu_ref[...], xup) + bu_ref[...]
        f_ref[...] = (_silu_parts(ag)[0] * au).astype(BF)

    return pl.pallas_call(
        body, name="ffn_act_fwd", grid=(nc, s // tr),
        in_specs=[_spec_cur(tr, tc, 0), _spec_prev(tr, tc, 0), _spec_cur(tr, tc, nc), _spec_prev(tr, tc, nc),
                  _spec_w(tc, 0), _spec_w(tc, nc),
                  pl.BlockSpec((1, tc), lambda j, i: (0, j)), pl.BlockSpec((1, tc), lambda j, i: (0, nc + j))],
        out_specs=_spec_cur(tr, tc, 0),
        out_shape=jax.ShapeDtypeStruct((s, dff), BF),
        compiler_params=_cp("parallel", "parallel"),
    )(a_pre, a_pre, a_pre, a_pre, cw8, cw8, cb, cb)


def _ffn_act_bwd(a_pre, d_f, cw8, cb, dff):
    s = a_pre.shape[0]
    tr = _pick(s, ROW_TILE_BWD, 16)
    tc = _pick(dff, COL_TILE, LANE)
    nc = dff // tc
    nr = s // tr

    def body(xg_ref, xgp_ref, xgn_ref, xu_ref, xup_ref, xun_ref, df_ref, dfn_ref,
             wg_ref, wu_ref, bg_ref, bu_ref, dxg_ref, dxu_ref, dwg_ref, dwu_ref):
        i = pl.program_id(1)
        xg = xg_ref[...]
        xu = xu_ref[...]
        xgp = jnp.where(i > 0, xgp_ref[...], 0.0)
        xup = jnp.where(i > 0, xup_ref[...], 0.0)

        def d_act(xg_t, xgp_t, xu_t, xup_t, df_t, lags_g=None, lags_u=None):
            ag = _conv3(wg_ref, xg_t, xgp_t, lags_g) + bg_ref[...]
            au = _conv3(wu_ref, xu_t, xup_t, lags_u) + bu_ref[...]
            sil, sg = _silu_parts(ag)
            return df_t * au * (sg * (1.0 + ag * (1.0 - sg))), df_t * sil

        lags_g = _lags(xg, xgp)
        lags_u = _lags(xu, xup)
        dag, dau = d_act(xg, xgp, xu, xup, df_ref[...], lags_g, lags_u)
        dfn = jnp.where(i < nr - 1, dfn_ref[...], 0.0)
        dagn, daun = d_act(xgn_ref[...], xg[tr - SUB:], xun_ref[...], xu[tr - SUB:], dfn)
        dxg_ref[...] = _conv3_t(wg_ref, dag, dagn).astype(BF)
        dxu_ref[...] = _conv3_t(wu_ref, dau, daun).astype(BF)

        @pl.when(i == 0)
        def _():
            dwg_ref[...] = jnp.zeros_like(dwg_ref)
            dwu_ref[...] = jnp.zeros_like(dwu_ref)

        def wgrad(da, x, lags):
            return _rows8([jnp.sum(da * lags[1], axis=0, keepdims=True),
                           jnp.sum(da * lags[0], axis=0, keepdims=True),
                           jnp.sum(da * x, axis=0, keepdims=True),
                           jnp.sum(da, axis=0, keepdims=True)], tc)

        dwg_ref[...] += wgrad(dag, xg, lags_g)
        dwu_ref[...] += wgrad(dau, xu, lags_u)

    half = jax.ShapeDtypeStruct((s, dff), BF)
    wsh = jax.ShapeDtypeStruct((SUB, dff), F32)
    return pl.pallas_call(
        body, name="ffn_act_bwd", grid=(nc, nr),
        in_specs=[_spec_cur(tr, tc, 0), _spec_prev(tr, tc, 0), _spec_next(tr, tc, 0, s),
                  _spec_cur(tr, tc, nc), _spec_prev(tr, tc, nc), _spec_next(tr, tc, nc, s),
                  _spec_cur(tr, tc, 0), _spec_next(tr, tc, 0, s),
                  _spec_w(tc, 0), _spec_w(tc, nc),
                  pl.BlockSpec((1, tc), lambda j, i: (0, j)), pl.BlockSpec((1, tc), lambda j, i: (0, nc + j))],
        out_specs=[_spec_cur(tr, tc, 0), _spec_cur(tr, tc, 0), _spec_w(tc, 0), _spec_w(tc, 0)],
        out_shape=[half, half, wsh, wsh],
        compiler_params=_cp("parallel", "arbitrary"),
    )(a_pre, a_pre, a_pre, a_pre, a_pre, a_pre, d_f, d_f, cw8, cw8, cb, cb)


def _gate_fwd(z_ga, z_gb, b_gate, yc, ym, d):
    s = z_ga.shape[0]
    tr = _pick(s, ROW_TILE, 16)
    tc = _pick(d, COL_TILE, LANE)
    nc = d // tc

    def body(za_ref, zb_ref, ba_ref, bb_ref, yc_ref, ym_ref, o_ref):
        ga = jax.nn.sigmoid(za_ref[...] + ba_ref[...])
        gb = jax.nn.sigmoid(zb_ref[...] + bb_ref[...])
        o_ref[...] = (ga * yc_ref[...] + gb * ym_ref[...]).astype(BF)

    return pl.pallas_call(
        body, name="gate_fwd", grid=(nc, s // tr),
        in_specs=[_spec_cur(tr, tc, 0), _spec_cur(tr, tc, 0),
                  pl.BlockSpec((1, tc), lambda j, i: (0, j)), pl.BlockSpec((1, tc), lambda j, i: (0, nc + j)),
                  _spec_cur(tr, tc, 0), _spec_cur(tr, tc, 0)],
        out_specs=_spec_cur(tr, tc, 0),
        out_shape=jax.ShapeDtypeStruct((s, d), BF),
        compiler_params=_cp("parallel", "parallel"),
    )(z_ga, z_gb, b_gate, b_gate, yc, ym)


def _gate_bwd(d_mix, z_ga, z_gb, b_gate, yc, ym, d):
    s = z_ga.shape[0]
    tr = _pick(s, ROW_TILE, 16)
    tc = _pick(d, COL_TILE, LANE)
    nc = d // tc

    def body(dm_ref, za_ref, zb_ref, ba_ref, bb_ref, yc_ref, ym_ref,
             dza_ref, dzb_ref, dyc_ref, dym_ref, dba_ref, dbb_ref):
        i = pl.program_id(1)
        dm = dm_ref[...]
        ga = jax.nn.sigmoid(za_ref[...] + ba_ref[...])
        gb = jax.nn.sigmoid(zb_ref[...] + bb_ref[...])
        dza = dm * yc_ref[...] * (ga * (1.0 - ga))
        dzb = dm * ym_ref[...] * (gb * (1.0 - gb))
        dza_ref[...] = dza.astype(BF)
        dzb_ref[...] = dzb.astype(BF)
        dyc_ref[...] = (dm * ga).astype(BF)
        dym_ref[...] = (dm * gb).astype(BF)

        @pl.when(i == 0)
        def _():
            dba_ref[...] = jnp.zeros_like(dba_ref)
            dbb_ref[...] = jnp.zeros_like(dbb_ref)

        dba_ref[...] += _rows8([jnp.sum(dza, axis=0, keepdims=True)], tc)
        dbb_ref[...] += _rows8([jnp.sum(dzb, axis=0, keepdims=True)], tc)

    act = jax.ShapeDtypeStruct((s, d), BF)
    bsh = jax.ShapeDtypeStruct((SUB, d), F32)
    return pl.pallas_call(
        body, name="gate_bwd", grid=(nc, s // tr),
        in_specs=[_spec_cur(tr, tc, 0), _spec_cur(tr, tc, 0), _spec_cur(tr, tc, 0),
                  pl.BlockSpec((1, tc), lambda j, i: (0, j)), pl.BlockSpec((1, tc), lambda j, i: (0, nc + j)),
                  _spec_cur(tr, tc, 0), _spec_cur(tr, tc, 0)],
        out_specs=[_spec_cur(tr, tc, 0)] * 4 + [_spec_w(tc, 0)] * 2,
        out_shape=[act, act, act, act, bsh, bsh],
        compiler_params=_cp("parallel", "arbitrary"),
    )(d_mix, z_ga, z_gb, b_gate, b_gate, yc, ym)


def _lay(v):
    z = jnp.zeros(v.shape[:-1] + (HALF,), v.dtype)
    return jnp.concatenate([v[..., :HALF], z, v[..., HALF:], z], axis=-1)


def _unlay(v):
    return jnp.concatenate([v[..., :HALF], v[..., 2 * HALF:3 * HALF]], axis=-1)


def _lay_rows(v):
    z = jnp.zeros((HALF,) + v.shape[1:], v.dtype)
    return jnp.concatenate([v[:HALF], z, v[HALF:], z], axis=0)


def _rope_tables(positions):
    s = positions.shape[0]
    tr = _pick(s, ROW_TILE, 8)
    inv_freq = ROPE_THETA ** (-jnp.arange(0, ROPE, 2, dtype=F32) / ROPE)
    consts = jnp.stack([_lay(jnp.concatenate([inv_freq, inv_freq])),
                        _lay(jnp.ones((ROPE,), F32)),
                        _lay(jnp.concatenate([-jnp.ones((HALF,), F32), jnp.ones((HALF,), F32)]))])
    consts = _pad8(consts)

    def body(p_ref, c_ref, cos_ref, sin_ref):
        ang = p_ref[...].astype(F32) * c_ref[0:1, :]
        cos_ref[...] = jnp.cos(ang) * c_ref[1:2, :]
        sin_ref[...] = jnp.sin(ang) * c_ref[2:3, :]

    tab = jax.ShapeDtypeStruct((s, LANE), F32)
    return pl.pallas_call(
        body, name="rope_tables", grid=(s // tr,),
        in_specs=[pl.BlockSpec((tr, 1), lambda i: (i, 0)), pl.BlockSpec((SUB, LANE), lambda i: (0, 0))],
        out_specs=[pl.BlockSpec((tr, LANE), lambda i: (i, 0))] * 2,
        out_shape=[tab, tab],
        compiler_params=_cp("parallel"),
    )(positions, consts)


def _lane_sum(p):
    return jnp.sum(p, axis=-1, keepdims=True)


def _rope(t, cos, sin):
    return t * cos + pltpu.roll(t, 2 * HALF, axis=1) * sin


def _rope_t(d, cos, sin):
    return d * cos + pltpu.roll(d * sin, 2 * HALF, axis=1)


def _head_fwd(q_raw, kv_raw, z_a, kr_blk, cos, sin, gains, heads):
    s = q_raw.shape[0]
    tr = _pick(s, HEAD_ROW_TILE, 16)
    hw = heads * LANE

    def body(q_ref, kv_ref, kr_ref, cos_ref, sin_ref, g_ref, qo_ref, ko_ref, vo_ref):
        cosv = cos_ref[...]
        sinv = sin_ref[...]
        krv = kr_ref[...]
        kr_sq = krv * krv
        for h in range(heads):
            lo = h * LANE
            qn = q_ref[:, lo:lo + LANE]
            qr = q_ref[:, hw + lo:hw + lo + LANE]
            r = lax.rsqrt(_lane_sum(qn * qn + qr * qr) / HEAD_QK + NORM_EPS)
            qo_ref[:, 2 * lo:2 * lo + LANE] = ((qn * r) * g_ref[0:1, :]).astype(BF)
            qo_ref[:, 2 * lo + LANE:2 * lo + 2 * LANE] = _rope((qr * r) * g_ref[1:2, :], cosv, sinv).astype(BF)
            kn = kv_ref[:, 2 * lo:2 * lo + LANE]
            r = lax.rsqrt(_lane_sum(kn * kn + kr_sq) / HEAD_QK + NORM_EPS)
            ko_ref[:, 2 * lo:2 * lo + LANE] = ((kn * r) * g_ref[2:3, :]).astype(BF)
            ko_ref[:, 2 * lo + LANE:2 * lo + 2 * LANE] = _rope((krv * r) * g_ref[3:4, :], cosv, sinv).astype(BF)
            vo_ref[:, lo:lo + LANE] = kv_ref[:, 2 * lo + LANE:2 * lo + 2 * LANE].astype(BF)

    row = lambda w: pl.BlockSpec((tr, w), lambda i: (i, 0))
    return pl.pallas_call(
        body, name="head_fwd", grid=(s // tr,),
        in_specs=[row(2 * hw), row(2 * hw), pl.BlockSpec((tr, LANE), lambda i: (i, kr_blk)),
                  row(LANE), row(LANE), pl.BlockSpec((SUB, LANE), lambda i: (0, 0))],
        out_specs=[row(2 * hw), row(2 * hw), row(hw)],
        out_shape=[jax.ShapeDtypeStruct((s, 2 * hw), BF), jax.ShapeDtypeStruct((s, 2 * hw), BF),
                   jax.ShapeDtypeStruct((s, hw), BF)],
        compiler_params=_cp("parallel"),
    )(q_raw, kv_raw, z_a, cos, sin, gains)


def _head_bwd(q_raw, kv_raw, z_a, kr_blk, cos, sin, gains, dq_att, dk_att, dv, heads):
    s = q_raw.shape[0]
    tr = _pick(s, HEAD_ROW_TILE_BWD, 16)
    hw = heads * LANE

    def body(q_ref, kv_ref, kr_ref, cos_ref, sin_ref, g_ref, dq_ref, dk_ref, dv_ref,
             dqr_ref, dkv_ref, dkr_ref, dg_ref):
        i = pl.program_id(0)
        cosv = cos_ref[...]
        sinv = sin_ref[...]
        krv = kr_ref[...]
        kr_sq = krv * krv
        dkr = jnp.zeros((tr, LANE), F32)
        dgs = [jnp.zeros((1, LANE), F32) for _ in range(4)]

        def norm_bwd(xn, xr, sq, dn_out, dr_out, gn, gr):
            r = lax.rsqrt(_lane_sum(sq) / HEAD_QK + NORM_EPS)
            nn = xn * r
            nr = xr * r
            dt = _rope_t(dr_out, cosv, sinv)
            dnn = dn_out * gn
            dnr = dt * gr
            mean = _lane_sum(dnn * nn + dnr * nr) / HEAD_QK
            return (r * (dnn - nn * mean), r * (dnr - nr * mean),
                    jnp.sum(dn_out * nn, axis=0, keepdims=True), jnp.sum(dt * nr, axis=0, keepdims=True))

        for h in range(heads):
            lo = h * LANE
            qn = q_ref[:, lo:lo + LANE]
            qr = q_ref[:, hw + lo:hw + lo + LANE]
            dxn, dxr, g0, g1 = norm_bwd(qn, qr, qn * qn + qr * qr, dq_ref[:, 2 * lo:2 * lo + LANE],
                                        dq_ref[:, 2 * lo + LANE:2 * lo + 2 * LANE], g_ref[0:1, :], g_ref[1:2, :])
            dqr_ref[:, lo:lo + LANE] = dxn.astype(BF)
            dqr_ref[:, hw + lo:hw + lo + LANE] = dxr.astype(BF)
            kn = kv_ref[:, 2 * lo:2 * lo + LANE]
            dxn, dxr, g2, g3 = norm_bwd(kn, krv, kn * kn + kr_sq, dk_ref[:, 2 * lo:2 * lo + LANE],
                                        dk_ref[:, 2 * lo + LANE:2 * lo + 2 * LANE], g_ref[2:3, :], g_ref[3:4, :])
            dkv_ref[:, 2 * lo:2 * lo + LANE] = dxn.astype(BF)
            dkv_ref[:, 2 * lo + LANE:2 * lo + 2 * LANE] = dv_ref[:, lo:lo + LANE].astype(BF)
            dkr = dkr + dxr
            dgs = [a + b for a, b in zip(dgs, (g0, g1, g2, g3))]
        dkr_ref[...] = dkr

        @pl.when(i == 0)
        def _():
            dg_ref[...] = jnp.zeros_like(dg_ref)

        dg_ref[...] += _rows8(dgs, LANE)

    row = lambda w: pl.BlockSpec((tr, w), lambda i: (i, 0))
    return pl.pallas_call(
        body, name="head_bwd", grid=(s // tr,),
        in_specs=[row(2 * hw), row(2 * hw), pl.BlockSpec((tr, LANE), lambda i: (i, kr_blk)),
                  row(LANE), row(LANE), pl.BlockSpec((SUB, LANE), lambda i: (0, 0)),
                  row(2 * hw), row(2 * hw), row(hw)],
        out_specs=[row(2 * hw), row(2 * hw), row(LANE), pl.BlockSpec((SUB, LANE), lambda i: (0, 0))],
        out_shape=[jax.ShapeDtypeStruct((s, 2 * hw), BF), jax.ShapeDtypeStruct((s, 2 * hw), BF),
                   jax.ShapeDtypeStruct((s, LANE), F32), jax.ShapeDtypeStruct((SUB, LANE), F32)],
        compiler_params=_cp("arbitrary"),
    )(q_raw, kv_raw, z_a, cos, sin, gains, dq_att, dk_att, dv)


def _causal_mask(nrows, ncols, row0):
    rows = lax.broadcasted_iota(jnp.int32, (nrows, ncols), 0) + row0
    cols = lax.broadcasted_iota(jnp.int32, (nrows, ncols), 1)
    return cols <= rows


def _causal_steps(nt, q_major):
    pairs = ([(i, j) for i in range(nt) for j in range(i + 1)] if q_major
             else [(i, j) for j in range(nt) for i in range(j, nt)])
    return (jnp.array([p[0] for p in pairs], jnp.int32), jnp.array([p[1] for p in pairs], jnp.int32))


def _attn_fwd(q_att, k_att, v, heads):
    s = q_att.shape[0]
    t = _pick(s, ATTN_TILE_FWD, LANE)
    nt = s // t
    th = t // 2
    scale = HEAD_QK ** -0.5
    qi, kj = _causal_steps(nt, True)

    def body(qi_ref, kj_ref, q_ref, k_ref, v_ref, o_ref, ob_ref, lse_ref, m_s, l_s, acc_s):
        st = pl.program_id(1)
        i = qi_ref[st]
        j = kj_ref[st]

        @pl.when(j == 0)
        def _():
            m_s[...] = jnp.full_like(m_s, NEG_INF)
            l_s[...] = jnp.zeros_like(l_s)
            acc_s[...] = jnp.zeros_like(acc_s)

        def update(rows, ncol, masked):
            sc = lax.dot_general(q_ref[rows, :], k_ref[0:ncol, :], (((1,), (1,)), ((), ())),
                                 preferred_element_type=F32) * scale
            if masked:
                sc = jnp.where(_causal_mask(rows.stop - rows.start, ncol, rows.start), sc, NEG_INF)
            m_prev = m_s[rows, :]
            m_new = jnp.maximum(m_prev, jnp.max(sc, axis=-1, keepdims=True))
            alpha = jnp.exp(m_prev - m_new)
            p = jnp.exp(sc - jnp.tile(m_new, (1, ncol // LANE)))
            l_s[rows, :] = alpha * l_s[rows, :] + jnp.sum(p, axis=-1, keepdims=True)
            acc_s[rows, :] = alpha * acc_s[rows, :] + jnp.dot(p.astype(BF), v_ref[0:ncol, :],
                                                              preferred_element_type=F32)
            m_s[rows, :] = m_new

        @pl.when(j < i)
        def _():
            update(slice(0, t), t, False)

        @pl.when(j == i)
        def _():
            update(slice(0, th), th, True)
            update(slice(th, t), t, True)
            o = acc_s[...] / l_s[...]
            o_ref[...] = o
            ob_ref[...] = o.astype(BF)
            lse_ref[...] = (m_s[...] + jnp.log(l_s[...]))[:, 0:1]

    q_idx = lambda h, st, qi_r, kj_r: (qi_r[st], h)
    kv_idx = lambda h, st, qi_r, kj_r: (kj_r[st], h)
    return pl.pallas_call(
        body, name="attn_fwd",
        grid_spec=pltpu.PrefetchScalarGridSpec(
            num_scalar_prefetch=2, grid=(heads, qi.shape[0]),
            in_specs=[pl.BlockSpec((t, 2 * LANE), q_idx), pl.BlockSpec((t, 2 * LANE), kv_idx),
                      pl.BlockSpec((t, LANE), kv_idx)],
            out_specs=[pl.BlockSpec((t, LANE), q_idx), pl.BlockSpec((t, LANE), q_idx),
                       pl.BlockSpec((None, t, 1), lambda h, st, qi_r, kj_r: (h, qi_r[st], 0))],
            scratch_shapes=[pltpu.VMEM((t, LANE), F32), pltpu.VMEM((t, LANE), F32), pltpu.VMEM((t, LANE), F32)]),
        out_shape=[jax.ShapeDtypeStruct((s, heads * LANE), F32), jax.ShapeDtypeStruct((s, heads * LANE), BF),
                   jax.ShapeDtypeStruct((heads, s, 1), F32)],
        compiler_params=_cp("parallel", "arbitrary"),
    )(qi, kj, q_att, k_att, v)


def _attn_bwd(q_att, k_att, v, o, lse, d_o, heads, dep=None):
    s = q_att.shape[0]
    t = _pick(s, ATTN_TILE, LANE)
    nt = s // t
    th = t // 2
    scale = HEAD_QK ** -0.5
    qi, kj = _causal_steps(nt, False)

    def body(qi_ref, kj_ref, q_ref, k_ref, v_ref, do_ref, o_ref, lse_ref, *rest):
        dq_ref, dk_ref, dv_ref, dk_s, dv_s = rest[-5:]
        st = pl.program_id(1)
        i = qi_ref[st]
        j = kj_ref[st]

        @pl.when(st == 0)
        def _():
            dq_ref[...] = jnp.zeros_like(dq_ref)

        @pl.when(i == j)
        def _():
            dk_s[...] = jnp.zeros_like(dk_s)
            dv_s[...] = jnp.zeros_like(dv_s)

        def update(rows, ncol, masked):
            nrow = rows.stop - rows.start
            q = q_ref[rows, :]
            k = k_ref[0:ncol, :]
            do = do_ref[rows, :]
            sc = lax.dot_general(q, k, (((1,), (1,)), ((), ())), preferred_element_type=F32) * scale
            if masked:
                sc = jnp.where(_causal_mask(nrow, ncol, rows.start), sc, NEG_INF)
            p = jnp.exp(sc - lse_ref[rows, :])
            dp = lax.dot_general(do, v_ref[0:ncol, :], (((1,), (1,)), ((), ())), preferred_element_type=F32)
            delta = jnp.sum(do.astype(F32) * o_ref[rows, :], axis=-1, keepdims=True)
            ds = (p * (dp - delta) * scale).astype(BF)
            dv_s[0:ncol, :] += lax.dot_general(p.astype(BF), do, (((0,), (0,)), ((), ())),
                                               preferred_element_type=F32)
            dk_s[0:ncol, :] += lax.dot_general(ds, q, (((0,), (0,)), ((), ())), preferred_element_type=F32)
            out_rows = pl.ds(pl.multiple_of(i * t + rows.start, nrow), nrow)
            dq_ref[out_rows, :] += jnp.dot(ds, k, preferred_element_type=F32)

        @pl.when(i > j)
        def _():
            update(slice(0, t), t, False)

        @pl.when(i == j)
        def _():
            update(slice(0, th), th, True)
            update(slice(th, t), t, True)

        @pl.when(i == nt - 1)
        def _():
            dk_ref[...] = dk_s[...]
            dv_ref[...] = dv_s[...]

    q_idx = lambda h, st, qi_r, kj_r: (qi_r[st], h)
    kv_idx = lambda h, st, qi_r, kj_r: (kj_r[st], h)
    in_specs = [pl.BlockSpec((t, 2 * LANE), q_idx), pl.BlockSpec((t, 2 * LANE), kv_idx),
                pl.BlockSpec((t, LANE), kv_idx), pl.BlockSpec((t, LANE), q_idx), pl.BlockSpec((t, LANE), q_idx),
                pl.BlockSpec((None, t, 1), lambda h, st, qi_r, kj_r: (h, qi_r[st], 0))]
    args = [q_att, k_att, v, d_o, o, lse]
    if dep is not None:
        in_specs.append(ANY)
        args.append(dep)
    return pl.pallas_call(
        body, name="attn_bwd",
        grid_spec=pltpu.PrefetchScalarGridSpec(
            num_scalar_prefetch=2, grid=(heads, qi.shape[0]),
            in_specs=in_specs,
            out_specs=[pl.BlockSpec((s, 2 * LANE), lambda h, st, qi_r, kj_r: (0, h)),
                       pl.BlockSpec((t, 2 * LANE), kv_idx), pl.BlockSpec((t, LANE), kv_idx)],
            scratch_shapes=[pltpu.VMEM((t, 2 * LANE), F32), pltpu.VMEM((t, LANE), F32)]),
        out_shape=[jax.ShapeDtypeStruct((s, heads * 2 * LANE), F32),
                   jax.ShapeDtypeStruct((s, heads * 2 * LANE), F32),
                   jax.ShapeDtypeStruct((s, heads * LANE), F32)],
        compiler_params=_cp("parallel", "arbitrary"),
    )(qi, kj, *args)


def _sum_parts(parts, name):
    n, r, c = parts.shape
    tr = _pick(r, 512, 8)

    def body(p_ref, o_ref):
        g = p_ref[0].astype(F32)
        for k in range(1, n):
            g = g + p_ref[k].astype(F32)
        o_ref[...] = g

    return pl.pallas_call(
        body, name=name, grid=(r // tr,),
        in_specs=[pl.BlockSpec((n, tr, c), lambda i: (0, i, 0))],
        out_specs=pl.BlockSpec((tr, c), lambda i: (i, 0)),
        out_shape=jax.ShapeDtypeStruct((r, c), F32),
        compiler_params=_cp("parallel"),
    )(parts)


def _adamw(parts, w, m, v, name, by_cols=False):
    n, rp, c = parts.shape
    r = w.shape[0]
    assert by_cols or rp == r
    tr, tc = (r, _pick(c, 256, LANE)) if by_cols else (_pick(r, 256, 16 if r % 16 == 0 else 8), c)

    def body(p_ref, w_ref, m_ref, v_ref, g_ref, d_ref, mo_ref, vo_ref):
        g = p_ref[0].astype(F32)
        for k in range(1, n):
            g = g + p_ref[k].astype(F32)
        g = g[:r] if by_cols else g
        m_new = ADAM_B1 * m_ref[...] + (1.0 - ADAM_B1) * g
        v_new = ADAM_B2 * v_ref[...] + (1.0 - ADAM_B2) * jnp.square(g)
        m_hat = m_new / (1.0 - ADAM_B1 ** ADAM_STEP)
        v_hat = v_new / (1.0 - ADAM_B2 ** ADAM_STEP)
        g_ref[...] = g
        d_ref[...] = -ADAM_LR * (m_hat / (jnp.sqrt(v_hat) + ADAM_EPS) + ADAM_WD * w_ref[...])
        mo_ref[...] = m_new
        vo_ref[...] = v_new

    idx = (lambda i: (0, i)) if by_cols else (lambda i: (i, 0))
    spec = pl.BlockSpec((tr, tc), idx)
    sh = jax.ShapeDtypeStruct((r, c), F32)
    return pl.pallas_call(
        body, name=name, grid=(c // tc if by_cols else r // tr,),
        in_specs=[pl.BlockSpec((n, rp if by_cols else tr, tc), lambda i: (0,) + idx(i)), spec, spec, spec],
        out_specs=[spec] * 4, out_shape=[sh] * 4,
        compiler_params=_cp("parallel"),
    )(parts, w, m, v)


def _place():
    x, y, c = lax.axis_index("x"), lax.axis_index("y"), lax.axis_index("c")
    chips = [(1 - x, y), (x, 1 - y), (1 - x, 1 - y)]
    return x, y, c, chips


def _all_gather(shards, name, dep=None):
    n = len(shards)
    deps = [] if dep is None else list(dep)

    def body(*refs):
        ins, outs = refs[:n], refs[n + len(deps):2 * n + len(deps)]
        send_sems, recv_sems, local_sems = refs[2 * n + len(deps):]
        x, y, c, chips = _place()
        me, sibling = (x, y, c), (x, y, 1 - c)

        def slot(w, p):
            return outs[w].at[4 * p[0] + 2 * p[1] + p[2]]

        def copy(w, k, block, to, src=None):
            return pltpu.make_async_remote_copy(
                src_ref=slot(w, block) if src is None else src, dst_ref=slot(w, block),
                send_sem=send_sems.at[w, k], recv_sem=recv_sems.at[w, k], device_id=to, device_id_type=MESH)

        first = []
        for w in range(n):
            first += [copy(w, 1 + j, me, (*chip, c), src=ins[w]) for j, chip in enumerate(chips)]
            first.append(copy(w, 0, me, sibling, src=ins[w]))
        for cp in first:
            cp.start()
        mine = [pltpu.make_async_copy(ins[w], slot(w, me), local_sems.at[w]) for w in range(n)]
        for cp in mine:
            cp.start()
        passed = []
        for w in range(n):
            for j, chip in enumerate(chips):
                copy(w, 1 + j, (*chip, c), me).wait_recv()
                cp = copy(w, 4 + j, (*chip, c), sibling)
                cp.start()
                passed.append(cp)
        for w in range(n):
            copy(w, 0, sibling, me).wait_recv()
            for j, chip in enumerate(chips):
                copy(w, 4 + j, (*chip, 1 - c), me).wait_recv()
        for cp in first + passed:
            cp.wait_send()
        for cp in mine:
            cp.wait()

    return pl.pallas_call(
        body, name=name,
        in_specs=[ANY] * (n + len(deps)), out_specs=[ANY] * n,
        out_shape=[jax.ShapeDtypeStruct((N_DEV,) + a.shape, a.dtype) for a in shards],
        scratch_shapes=[pltpu.SemaphoreType.DMA((n, 7)), pltpu.SemaphoreType.DMA((n, 7)),
                        pltpu.SemaphoreType.DMA((n,))],
    )(*shards, *deps)


HBM = pl.BlockSpec(memory_space=pltpu.HBM)
SEM = pl.BlockSpec(memory_space=pltpu.SEMAPHORE)
EFFECT = pltpu.SideEffectType.DATAFLOW_SIDE_EFFECTING
PEERS = [(dx, dy, dc) for dx in (1, 0) for dy in (1, 0) for dc in (0, 1) if (dx, dy, dc) != (0, 0, 0)]


def _peer(x, y, c, flip):
    dx, dy, dc = flip
    return (1 - x if dx else x, 1 - y if dy else y, 1 - c if dc else c)


def _exchange_copies(srcs, lands, send, recv, loc, gather):
    x, y, c, _ = _place()
    me = 4 * x + 2 * y + c
    remote, local = [], []
    for w in range(len(srcs)):
        for k, flip in enumerate(PEERS):
            px, py, pc = _peer(x, y, c, flip)
            src = srcs[w] if gather else srcs[w].at[4 * px + 2 * py + pc]
            remote.append(pltpu.make_async_remote_copy(
                src_ref=src, dst_ref=lands[w].at[me], send_sem=send[w].at[k], recv_sem=recv[w].at[k],
                device_id=(px, py, pc), device_id_type=MESH))
        local.append(pltpu.make_async_copy(srcs[w] if gather else srcs[w].at[me], lands[w].at[me], loc[w]))
    return remote, local


class _Exchange:
    def __init__(self, srcs, lands, send, recv, loc, token, gather):
        self.srcs, self.lands, self.send, self.recv, self.loc = srcs, lands, send, recv, loc
        self.token, self.gather = token, gather


def _exchange_start(srcs, gather, name, dep=None):
    n = len(srcs)
    deps = [] if dep is None else [dep]
    land_shapes = [((N_DEV,) + a.shape) if gather else a.shape for a in srcs]
    lands = [pltpu.with_memory_space_constraint(lax.empty(sh, a.dtype), pltpu.HBM) for sh, a in zip(land_shapes, srcs)]
    srcs = [pltpu.with_memory_space_constraint(a, pltpu.HBM) for a in srcs]

    def body(*refs):
        src_refs, land_refs = refs[:n], refs[n:2 * n]
        outs = refs[2 * n + len(deps):]
        send, recv, loc = outs[:n], outs[n:2 * n], outs[2 * n:3 * n]
        token = outs[-1]
        remote, local = _exchange_copies(src_refs, land_refs, send, recv, loc, gather)
        for cp in remote + local:
            cp.start()
        token[...] = jnp.zeros_like(token)

    out_shape = ([pltpu.SemaphoreType.DMA((len(PEERS),))] * (2 * n) + [pltpu.SemaphoreType.DMA(())] * n
                 + [pltpu.HBM(a.shape, a.dtype) for a in srcs] + [pltpu.HBM(a.shape, a.dtype) for a in lands]
                 + [jax.ShapeDtypeStruct((SUB, LANE), F32)])
    res = pl.pallas_call(
        body, name=name, out_shape=out_shape,
        in_specs=[HBM] * (2 * n) + [ANY] * len(deps),
        out_specs=[SEM] * (3 * n) + [HBM] * (2 * n) + [pl.BlockSpec(memory_space=pltpu.VMEM)],
        input_output_aliases={i: 3 * n + i for i in range(2 * n)},
        compiler_params=pltpu.CompilerParams(has_side_effects=EFFECT),
    )(*srcs, *lands, *deps)
    return _Exchange(res[3 * n:4 * n], res[4 * n:5 * n], res[:n], res[n:2 * n], res[2 * n:3 * n], res[-1], gather)


def _exchange_wait(ex, idxs, after, name):
    n = len(idxs)
    srcs = [ex.srcs[i] for i in idxs]
    lands = [ex.lands[i] for i in idxs]
    sems = [ex.send[i] for i in idxs] + [ex.recv[i] for i in idxs] + [ex.loc[i] for i in idxs]
    gather = ex.gather

    def body(*refs):
        src_refs, land_refs = refs[:n], refs[n:2 * n]
        send, recv, loc = refs[2 * n:3 * n], refs[3 * n:4 * n], refs[4 * n:5 * n]
        remote, local = _exchange_copies(src_refs, land_refs, send, recv, loc, gather)
        for cp in remote:
            cp.wait_send()
            cp.wait_recv()
        for cp in local:
            cp.wait()

    res = pl.pallas_call(
        body, name=name,
        out_shape=[pltpu.HBM(a.shape, a.dtype) for a in srcs] + [pltpu.HBM(a.shape, a.dtype) for a in lands],
        in_specs=[HBM] * (2 * n) + [SEM] * (3 * n) + [ANY],
        out_specs=[HBM] * (2 * n),
        input_output_aliases={i: i for i in range(2 * n)},
        compiler_params=pltpu.CompilerParams(has_side_effects=EFFECT),
    )(*srcs, *lands, *sems, after)
    return res[n:]


def _after(token, a):
    return a + token[0:1, 0:1].astype(a.dtype)


def _unblock(w3):
    nb, k, nbw = w3.shape
    return w3.transpose(1, 0, 2).reshape(k, nb * nbw)


def _block(w, nb):
    k, n = w.shape
    return w.reshape(k, nb, n // nb).transpose(1, 0, 2)


def kernel(x, positions, ln1_g, w_in, b_gate, conv_w, w_conv_out, q_a_g, w_q_b, kv_a_g, w_kv_b, q_norm_g, k_norm_g, w_mla_out, w_o, ln2_g, w_ffn_up, ffn_conv_w, ffn_conv_b, w_ffn_down, loss_target, m_ln1_g, m_w_in, m_b_gate, m_conv_w, m_w_conv_out, m_q_a_g, m_w_q_b, m_kv_a_g, m_w_kv_b, m_q_norm_g, m_k_norm_g, m_w_mla_out, m_w_o, m_ln2_g, m_w_ffn_up, m_ffn_conv_w, m_ffn_conv_b, m_w_ffn_down, v_ln1_g, v_w_in, v_b_gate, v_conv_w, v_w_conv_out, v_q_a_g, v_w_q_b, v_kv_a_g, v_w_kv_b, v_q_norm_g, v_k_norm_g, v_w_mla_out, v_w_o, v_ln2_g, v_w_ffn_up, v_ffn_conv_w, v_ffn_conv_b, v_w_ffn_down):
    s, d = x.shape[1], x.shape[2]
    conv = conv_w.shape[2] * N_DEV
    ql, kvl = q_a_g.shape[1], kv_a_g.shape[1]
    heads = w_q_b.shape[2] * N_DEV // HEAD_QK
    dff = w_ffn_down.shape[1] * N_DEV
    hw = heads * LANE
    conv3 = 3 * conv
    kr_off = conv3 + ql
    kv_off = -(-(kr_off + LANE) // kvl) * kvl
    wa = kv_off + kvl
    assert conv3 % ql == 0 and kr_off % LANE == 0
    xs = x[0]
    tgt = loss_target[0]
    pos = positions.reshape(s, 1)

    nin = w_in.shape[2]
    big = dict(w_in=w_in[0].T, w_conv_out=w_conv_out[0], w_q_b=w_q_b[0], w_kv_b=w_kv_b[0],
               w_mla_out=w_mla_out[0], w_o=w_o[0], w_ffn_up=w_ffn_up[0], w_ffn_down=w_ffn_down[0])
    names = list(big)
    rest = names[1:]
    first = _all_gather([big["w_in"].astype(BF), _pad8(conv_w[0]), _pad8(ffn_conv_w[0])], "gather_w_in")
    cw8 = _unblock(first[1])
    fcw8 = _unblock(first[2])
    ag = _exchange_start([big[k].astype(BF) for k in rest], True, "gather_rest_start", dep=first[1])

    def landed(keys, after, name):
        return _exchange_wait(ag, [rest.index(k) for k in keys], after, name)

    w_in_t = first[0].reshape(N_DEV * nin, d)
    g_off = kr_off + kvl + ROPE
    w_a_t = jnp.concatenate([w_in_t[:kr_off], _lay_rows(w_in_t[kr_off + kvl:g_off]),
                             jnp.zeros((kv_off - kr_off - LANE, d), BF), w_in_t[kr_off:kr_off + kvl]], axis=0)[None]
    w_ga_t = w_in_t[g_off:g_off + d][None]
    w_gb_t = w_in_t[g_off + d:g_off + 2 * d][None]
    gains = _pad8(jnp.concatenate([q_norm_g[:, :NOPE], _lay(q_norm_g[:, NOPE:]),
                                   k_norm_g[:, :NOPE], _lay(k_norm_g[:, NOPE:])], axis=0))
    kr_blk = kr_off // LANE

    cos, sin = _rope_tables(pos)
    u1 = _rms_fwd(xs, _after(ag.token, ln1_g), d, 0, "rms1_fwd")
    z_a = _mm_nt(u1, w_a_t, "mm_z_a")
    z_ga = _mm_nt(u1, w_ga_t, "mm_z_ga")
    z_gb = _mm_nt(u1, w_gb_t, "mm_z_gb")
    p = _conv_mix_fwd(z_a, cw8, conv)
    w_co, w_qb, w_kv = landed(["w_conv_out", "w_q_b", "w_kv_b"], p, "gather_wait_mixers")
    wq_full = _unblock(w_qb).reshape(ql, heads, HEAD_QK)
    w_q = jnp.concatenate([wq_full[:, :, :NOPE].reshape(ql, hw), _lay(wq_full[:, :, NOPE:]).reshape(ql, hw)],
                          axis=1)[None]
    yc = _mm_nn(p, w_co, "mm_y_conv")
    qn = _rms_fwd(z_a, q_a_g, ql, conv3 // ql, "rms_q_fwd")
    kvn = _rms_fwd(z_a, kv_a_g, kvl, kv_off // kvl, "rms_kv_fwd")
    q_raw = _mm_nn(qn, w_q, "mm_q")
    kv_raw = _mm_nn(kvn, w_kv, "mm_kv")
    q_att, k_att, v_bf = _head_fwd(q_raw, kv_raw, z_a, kr_blk, cos, sin, gains, heads)
    o, o_bf, lse = _attn_fwd(q_att, k_att, v_bf, heads)
    w_mo, w_oo = landed(["w_mla_out", "w_o"], lse, "gather_wait_outs")
    w_mo = w_mo.reshape(1, hw, d)
    w_oo = w_oo.reshape(1, d, d)
    ym = _mm_nn(o_bf, w_mo, "mm_y_mla")
    mix = _gate_fwd(z_ga, z_gb, b_gate, yc, ym, d)
    h1 = _mm_nn(mix, w_oo, "mm_h1", add=xs)
    u2 = _rms_fwd(h1, ln2_g, d, 0, "rms2_fwd")
    w_up, = landed(["w_ffn_up"], u2, "gather_wait_ffn_up")
    a_pre = _mm_nn(u2, w_up, "mm_ffn_up")
    f = _ffn_act_fwd(a_pre, fcw8, ffn_conv_b, dff)
    w_dn, = landed(["w_ffn_down"], f, "gather_wait_ffn_down")
    w_dn = w_dn.reshape(1, dff, d)
    dy, dy_bf, loss_part = _mm_nn_loss(f, w_dn, h1, tgt, "mm_ffn_down_loss")

    g_dn = _mm_tn(f, dy_bf, 1, "mm_g_ffn_down").reshape(N_DEV, dff // N_DEV, d)
    rs_dn = _exchange_start([g_dn], False, "reduce_ffn_down_start")
    d_f = _mm_nt(dy_bf, w_dn, "mm_d_f", dep=rs_dn.token)
    d_xg, d_xu, dfw_g, dfw_u = _ffn_act_bwd(a_pre, d_f, fcw8, ffn_conv_b, dff)
    half = N_DEV // 2
    g_up = _mm_tn(u2, d_xg, half, "mm_g_ffn_up_gate", into=lax.empty((N_DEV, d, 2 * dff // N_DEV), BF))
    g_up = _mm_tn(u2, d_xu, half, "mm_g_ffn_up_up", into=g_up, blk0=half)
    rs_up = _exchange_start([g_up], False, "reduce_ffn_up_start")
    d_u2 = _mm_nt(d_xg, w_up, "mm_d_u2_gate", blk0=0, nblk=half, dep=rs_up.token)
    d_u2 = _mm_nt(d_xu, w_up, "mm_d_u2_up", blk0=half, nblk=half, add=d_u2)
    d_h1, d_h1_bf, dg_ln2 = _rms_bwd(h1, d_u2, ln2_g, d, 0, "rms2_bwd", extra=dy, also_bf16=True)
    g_oo = _mm_tn(mix, d_h1_bf, 1, "mm_g_w_o").reshape(N_DEV, d // N_DEV, d)
    d_mix = _mm_nt(d_h1_bf, w_oo, "mm_d_mix")
    d_zga, d_zgb, d_yc, d_ym, dba, dbb = _gate_bwd(d_mix, z_ga, z_gb, b_gate, yc, ym, d)
    g_co = _mm_tn(p, d_yc, N_DEV, "mm_g_conv_out")
    g_mo = _mm_tn(o_bf, d_ym, 1, "mm_g_mla_out").reshape(N_DEV, hw // N_DEV, d)
    rs_mix = _exchange_start([g_oo, g_co, g_mo], False, "reduce_mixers_start")
    d_p = _mm_nt(d_yc, w_co, "mm_d_p", dep=rs_mix.token)
    d_o = _mm_nt(d_ym, w_mo, "mm_d_o", out_dtype=BF)
    d_zb, d_zc, d_zv, dcw = _conv_mix_bwd(z_a, d_p, cw8, conv)
    dq_att, dk_att, dv = _attn_bwd(q_att, k_att, v_bf, o, lse, d_o, heads, dep=rs_mix.token)
    d_q_raw, d_kv_raw, d_kr, dgains = _head_bwd(q_raw, kv_raw, z_a, kr_blk, cos, sin, gains, dq_att, dk_att, dv, heads)
    g_q2 = _mm_tn(qn, d_q_raw, 1, "mm_g_q")[0]
    g_qb = _block(jnp.concatenate([g_q2[:, :hw].reshape(ql, heads, NOPE),
                                   _unlay(g_q2[:, hw:].reshape(ql, heads, LANE))], axis=2).reshape(ql, heads * HEAD_QK), N_DEV)
    g_kv = _mm_tn(kvn, d_kv_raw, N_DEV, "mm_g_kv")
    rs_qkv = _exchange_start([g_qb, g_kv], False, "reduce_qkv_start")
    d_qn = _mm_nt(d_q_raw, w_q, "mm_d_qn", dep=rs_qkv.token)
    d_kvn = _mm_nt(d_kv_raw, w_kv, "mm_d_kvn")
    d_ql, dg_qa = _rms_bwd(z_a, d_qn, q_a_g, ql, conv3 // ql, "rms_q_bwd", out_dtype=BF)
    d_kvl, dg_kva = _rms_bwd(z_a, d_kvn, kv_a_g, kvl, kv_off // kvl, "rms_kv_bwd", out_dtype=BF)
    d_z_a = jnp.concatenate([d_zb, d_zc, d_zv, d_ql, d_kr.astype(BF), jnp.zeros((s, kv_off - kr_off - LANE), BF),
                             d_kvl], axis=1)
    g_a = _mm_tn(d_z_a, u1, 1, "mm_g_w_a")[0]
    g_ga = _mm_tn(d_zga, u1, 1, "mm_g_w_ga")[0]
    g_gb = _mm_tn(d_zgb, u1, 1, "mm_g_w_gb")[0]
    g_in = jnp.concatenate([g_a[:kr_off], g_a[kv_off:kv_off + kvl], g_a[kr_off:kr_off + HALF],
                            g_a[kr_off + 2 * HALF:kr_off + 3 * HALF], g_ga, g_gb], axis=0).reshape(N_DEV, nin, d)
    rs_in = _exchange_start([g_in], False, "reduce_w_in_start")
    d_u1 = _mm_nn(d_z_a, w_a_t, "mm_d_u1_a", dep=rs_in.token)
    d_u1 = _mm_nn(d_zga, w_ga_t, "mm_d_u1_ga", add=d_u1)
    d_u1 = _mm_nn(d_zgb, w_gb_t, "mm_d_u1_gb", add=d_u1)
    grad_x, dg_ln1 = _rms_bwd(xs, d_u1, ln1_g, d, 0, "rms1_bwd", extra=d_h1)

    summed = {}
    summed["w_ffn_down"], = _exchange_wait(rs_dn, [0], grad_x, "reduce_ffn_down_wait")
    summed["w_ffn_up"], = _exchange_wait(rs_up, [0], grad_x, "reduce_ffn_up_wait")
    summed["w_o"], summed["w_conv_out"], summed["w_mla_out"] = _exchange_wait(rs_mix, [0, 1, 2], grad_x, "reduce_mixers_wait")
    summed["w_q_b"], summed["w_kv_b"] = _exchange_wait(rs_qkv, [0, 1], grad_x, "reduce_qkv_wait")
    loc = locals()
    out = {}
    for k in rest:
        out[k] = _adamw(summed[k], big[k], loc["m_" + k][0], loc["v_" + k][0], "adamw_" + k)

    small = dict(ln1_g=dg_ln1[0:1], b_gate=jnp.concatenate([dba[0:1], dbb[0:1]], axis=1), q_a_g=dg_qa[0:1],
                 kv_a_g=dg_kva[0:1],
                 q_norm_g=jnp.concatenate([dgains[0:1], _unlay(dgains[1:2])], axis=1),
                 k_norm_g=jnp.concatenate([dgains[2:3], _unlay(dgains[3:4])], axis=1),
                 ln2_g=dg_ln2[0:1], ffn_conv_b=jnp.concatenate([dfw_g[3:4], dfw_u[3:4]], axis=1))
    small_names = list(small)
    extra = [dcw[0:3].reshape(1, -1), jnp.concatenate([dfw_g[0:3], dfw_u[0:3]], axis=1).reshape(1, -1),
             loss_part[0:1, 0:1]]
    flat = jnp.concatenate([small[k] for k in small_names] + extra, axis=1)
    n_flat = flat.shape[1]
    rows = -(-n_flat // (SUB * LANE)) * SUB
    flat = jnp.pad(flat, ((0, 0), (0, rows * LANE - n_flat))).reshape(rows, LANE)
    total = _sum_parts(_all_gather([flat], "gather_small", dep=[out[k][0] for k in rest])[0], "sum_small").reshape(1, rows * LANE)
    off = 0
    small_g = {}
    for k in small_names:
        small_g[k] = total[:, off:off + small[k].shape[1]]
        off += small[k].shape[1]
    me = 4 * lax.axis_index("x") + 2 * lax.axis_index("y") + lax.axis_index("c")
    cwn, fcwn = conv // N_DEV, 2 * dff // N_DEV
    g_cw = lax.dynamic_slice_in_dim(total[:, off:off + 3 * conv].reshape(3, conv), me * cwn, cwn, axis=1)
    off += 3 * conv
    g_fcw = lax.dynamic_slice_in_dim(total[:, off:off + 6 * dff].reshape(3, 2 * dff), me * fcwn, fcwn, axis=1)
    off += 6 * dff
    loss = total[0, off]

    summed["w_in"], = _exchange_wait(rs_in, [0], total, "reduce_w_in_wait")
    out["w_in"] = [r.T for r in _adamw(summed["w_in"], big["w_in"], m_w_in[0].T, v_w_in[0].T, "adamw_w_in",
                                       by_cols=True)]
    small_w = dict(ln1_g=ln1_g, b_gate=b_gate, q_a_g=q_a_g, kv_a_g=kv_a_g, q_norm_g=q_norm_g, k_norm_g=k_norm_g,
                   ln2_g=ln2_g, ffn_conv_b=ffn_conv_b, conv_w=conv_w[0].reshape(1, -1),
                   ffn_conv_w=ffn_conv_w[0].reshape(1, -1))
    small_g["conv_w"] = g_cw.reshape(1, -1)
    small_g["ffn_conv_w"] = g_fcw.reshape(1, -1)
    packed_names = list(small_w)

    def pack(get):
        vflat = jnp.concatenate([get(k).reshape(1, -1) for k in packed_names], axis=1)
        nr = -(-vflat.shape[1] // (SUB * LANE)) * SUB
        return jnp.pad(vflat, ((0, 0), (0, nr * LANE - vflat.shape[1])), constant_values=1.0).reshape(nr, LANE)

    res = _adamw(pack(lambda k: small_g[k])[None], pack(lambda k: small_w[k]), pack(lambda k: loc["m_" + k]),
                 pack(lambda k: loc["v_" + k]), "adamw_small")
    res = [r.reshape(1, -1) for r in res]
    off = 0
    for k in packed_names:
        shape = loc[k].shape
        size = small_w[k].shape[1]
        out[k] = [r[:, off:off + size].reshape(shape) for r in res]
        off += size
    for k in names:
        out[k] = [r[None] for r in out[k]]

    order = ["ln1_g", "w_in", "b_gate", "conv_w", "w_conv_out", "q_a_g", "w_q_b", "kv_a_g", "w_kv_b", "q_norm_g",
             "k_norm_g", "w_mla_out", "w_o", "ln2_g", "w_ffn_up", "ffn_conv_w", "ffn_conv_b", "w_ffn_down"]
    return (loss, grad_x[None], *[out[k][0] for k in order], *[out[k][1] for k in order],
            *[out[k][2] for k in order], *[out[k][3] for k in order])
```

```python
import functools

import jax
import jax.numpy as jnp
from jax import lax
from jax.experimental import pallas as pl
from jax.experimental.pallas import tpu as pltpu

BF = jnp.bfloat16
F32 = jnp.float32
MESH = pl.DeviceIdType.MESH
N_DEV = 8

NOPE = 128
ROPE = 64
HALF = ROPE // 2
HEAD_QK = NOPE + ROPE
HEAD_V = 128
LANE = 128
SUB = 8
NORM_EPS = 1e-6
NEG_INF = -1e30
ROPE_THETA = 10000.0
ADAM_LR = 0.001
ADAM_B1 = 0.9
ADAM_B2 = 0.999
ADAM_EPS = 1e-08
ADAM_WD = 0.01
ADAM_STEP = 10

VMEM_LIMIT = 52 * 1024 * 1024
MM_TM, MM_TN, MM_TK, MM_TS = 1024, 1536, 2048, 1024
ROW_TILE, ROW_TILE_BWD = 512, 256
HEAD_ROW_TILE, HEAD_ROW_TILE_BWD = 256, 128
COL_TILE = 512
ATTN_TILE = 1024
ATTN_TILE_FWD = 1024
ANY = pl.BlockSpec(memory_space=pl.ANY)


def _pick(n, target, mult):
    t = (min(n, target) // mult) * mult
    while t > 0:
        if n % t == 0:
            return t
        t -= mult
    raise ValueError(f"no tile for {n} (target {target}, multiple {mult})")


def _cp(*sem):
    return pltpu.CompilerParams(dimension_semantics=sem, vmem_limit_bytes=VMEM_LIMIT)


def _accumulate(kk, nk, acc, part, finish):
    if nk == 1:
        finish(part())
        return

    @pl.when(kk == 0)
    def _():
        acc[...] = part()

    @pl.when((kk > 0) & (kk < nk - 1))
    def _():
        acc[...] += part()

    @pl.when(kk == nk - 1)
    def _():
        finish(acc[...] + part())


def _mm_call(body, name, grid, in_specs, args, out_spec, out_shape, acc_shape, nk, dep):
    if dep is not None:
        in_specs = in_specs + [ANY]
        args = args + [dep]
    return pl.pallas_call(
        body, name=name, grid=grid, in_specs=in_specs, out_specs=out_spec, out_shape=out_shape,
        scratch_shapes=[pltpu.VMEM(acc_shape, F32)] if nk > 1 else [],
        compiler_params=_cp("parallel", "parallel", "arbitrary"),
    )(*args)


def _mm_nn_loss(a, b3, add, target, name):
    m, k = a.shape
    _, k2, n = b3.shape
    assert k == k2 and b3.shape[0] == 1
    tm = _pick(m, MM_TM, 16)
    tn = _pick(n, MM_TN, LANE)
    tk = _pick(k, MM_TK, LANE)
    nk = k // tk

    def body(a_ref, b_ref, c_ref, t_ref, dy_ref, dyb_ref, l_ref, acc):
        kk = pl.program_id(2)

        @pl.when((pl.program_id(0) == 0) & (pl.program_id(1) == 0) & (kk == 0))
        def _():
            l_ref[...] = jnp.zeros_like(l_ref)

        def part():
            return jnp.dot(a_ref[...].astype(BF), b_ref[0].astype(BF), preferred_element_type=F32)

        def finish(r):
            e = r + c_ref[...] - t_ref[...]
            dy_ref[...] = e / n
            dyb_ref[...] = (e / n).astype(BF)
            l_ref[...] += 0.5 * jnp.sum(jnp.sum(e * e, axis=-1, keepdims=True), axis=0, keepdims=True) / n

        _accumulate(kk, nk, acc, part, finish)

    tile = pl.BlockSpec((tm, tn), lambda i, j, kk: (i, j))
    return pl.pallas_call(
        body, name=name, grid=(m // tm, n // tn, nk),
        in_specs=[pl.BlockSpec((tm, tk), lambda i, j, kk: (i, kk)),
                  pl.BlockSpec((1, tk, tn), lambda i, j, kk: (0, kk, j)), tile, tile],
        out_specs=[tile, tile, pl.BlockSpec((SUB, LANE), lambda i, j, kk: (0, 0))],
        out_shape=[jax.ShapeDtypeStruct((m, n), F32), jax.ShapeDtypeStruct((m, n), BF),
                   jax.ShapeDtypeStruct((SUB, LANE), F32)],
        scratch_shapes=[pltpu.VMEM((tm, tn), F32)],
        compiler_params=_cp("arbitrary", "arbitrary", "arbitrary"),
    )(a, b3, add, target)


def _mm_nn(a, b3, name, add=None, out_dtype=F32, blk0=0, nblk=None, dep=None):
    pair = isinstance(a, (list, tuple))
    a_list = list(a) if pair else [a]
    m, ka = a_list[0].shape
    k = ka * len(a_list)
    nb_all, k2, nbw = b3.shape
    assert k == k2
    nblk = nb_all - blk0 if nblk is None else nblk
    n = nblk * nbw
    tm = _pick(m, MM_TM, 16)
    tn = _pick(nbw, MM_TN, LANE)
    tk = _pick(ka, MM_TK, LANE)
    per = nbw // tn
    nk = k // tk
    nka = ka // tk
    na_ops = len(a_list)

    def body(*refs):
        a_refs, b_ref = refs[:na_ops], refs[na_ops]
        c_ref = refs[na_ops + 1] if add is not None else None
        o_ref = refs[na_ops + 1 + (add is not None) + (dep is not None)]
        acc = refs[-1]
        kk = pl.program_id(2)

        def part():
            av = a_refs[0][...] if not pair else jnp.where(kk < nka, a_refs[0][...], a_refs[1][...])
            return jnp.dot(av.astype(BF), b_ref[...].astype(BF), preferred_element_type=F32)

        def finish(r):
            if add is not None:
                r = r + c_ref[...]
            o_ref[...] = r.astype(out_dtype)

        _accumulate(kk, nk, acc, part, finish)

    if pair:
        in_specs = [pl.BlockSpec((tm, tk), lambda i, j, kk: (i, jnp.minimum(kk, nka - 1))),
                    pl.BlockSpec((tm, tk), lambda i, j, kk: (i, jnp.maximum(kk - nka, 0)))]
    else:
        in_specs = [pl.BlockSpec((tm, tk), lambda i, j, kk: (i, kk))]
    in_specs.append(pl.BlockSpec((None, tk, tn), lambda i, j, kk: (blk0 + j // per, kk, j % per)))
    args = a_list + [b3]
    if add is not None:
        in_specs.append(pl.BlockSpec((tm, tn), lambda i, j, kk: (i, j)))
        args.append(add)
    return _mm_call(body, name, (m // tm, n // tn, nk), in_specs, args,
                    pl.BlockSpec((tm, tn), lambda i, j, kk: (i, j)), jax.ShapeDtypeStruct((m, n), out_dtype),
                    (tm, tn), nk, dep)


def _mm_nt(a, b3, name, add=None, out_dtype=F32, blk0=0, nblk=None, dep=None):
    pair = isinstance(a, (list, tuple))
    a_list = list(a) if pair else [a]
    m, na = a_list[0].shape
    n = na * len(a_list)
    nb_all, k, nbw = b3.shape
    nblk = nb_all - blk0 if nblk is None else nblk
    assert n == nblk * nbw and na % nbw == 0
    tm = _pick(m, MM_TM, 16)
    tn = _pick(k, MM_TN, LANE)
    tk = _pick(nbw, MM_TK, LANE)
    per = nbw // tk
    nk = n // tk
    nka = na // tk
    na_ops = len(a_list)

    def body(*refs):
        a_refs, b_ref = refs[:na_ops], refs[na_ops]
        c_ref = refs[na_ops + 1] if add is not None else None
        o_ref = refs[na_ops + 1 + (add is not None) + (dep is not None)]
        acc = refs[-1]
        kk = pl.program_id(2)

        def part():
            av = a_refs[0][...] if not pair else jnp.where(kk < nka, a_refs[0][...], a_refs[1][...])
            return lax.dot_general(av.astype(BF), b_ref[...].astype(BF),
                                   (((1,), (1,)), ((), ())), preferred_element_type=F32)

        def finish(r):
            if add is not None:
                r = r + c_ref[...]
            o_ref[...] = r.astype(out_dtype)

        _accumulate(kk, nk, acc, part, finish)

    if pair:
        in_specs = [pl.BlockSpec((tm, tk), lambda i, j, kk: (i, jnp.minimum(kk, nka - 1))),
                    pl.BlockSpec((tm, tk), lambda i, j, kk: (i, jnp.maximum(kk - nka, 0)))]
    else:
        in_specs = [pl.BlockSpec((tm, tk), lambda i, j, kk: (i, kk))]
    in_specs.append(pl.BlockSpec((None, tn, tk), lambda i, j, kk: (blk0 + kk // per, j, kk % per)))
    args = a_list + [b3]
    if add is not None:
        in_specs.append(pl.BlockSpec((tm, tn), lambda i, j, kk: (i, j)))
        args.append(add)
    return _mm_call(body, name, (m // tm, k // tn, nk), in_specs, args,
                    pl.BlockSpec((tm, tn), lambda i, j, kk: (i, j)), jax.ShapeDtypeStruct((m, k), out_dtype),
                    (tm, tn), nk, dep)


def _mm_tn(a, b, nblk, name, out_dtype=BF, dep=None, into=None, blk0=0):
    s, m = a.shape
    s2, n = b.shape
    assert s == s2 and n % nblk == 0 and (dep is None or into is None)
    nbw = n // nblk
    tm = _pick(m, MM_TN, LANE)
    tn = _pick(nbw, MM_TN, LANE)
    ts = _pick(s, MM_TS, LANE)
    per = nbw // tn
    ns = s // ts

    def body(*refs):
        a_ref, b_ref = refs[:2]
        o_ref = refs[2 + (dep is not None or into is not None)]
        acc = refs[-1]

        def part():
            return lax.dot_general(a_ref[...].astype(BF), b_ref[...].astype(BF),
                                   (((0,), (0,)), ((), ())), preferred_element_type=F32)

        def finish(r):
            o_ref[...] = r.astype(out_dtype)

        _accumulate(pl.program_id(2), ns, acc, part, finish)

    in_specs = [pl.BlockSpec((ts, tm), lambda i, j, ss: (ss, i)),
                pl.BlockSpec((ts, tn), lambda i, j, ss: (ss, j))]
    out_spec = pl.BlockSpec((None, tm, tn), lambda i, j, ss: (blk0 + j // per, i, j % per))
    if into is None:
        return _mm_call(body, name, (m // tm, n // tn, ns), in_specs, [a, b], out_spec,
                        jax.ShapeDtypeStruct((nblk, m, nbw), out_dtype), (tm, tn), ns, dep)
    assert into.shape[1:] == (m, nbw) and into.dtype == out_dtype
    return pl.pallas_call(
        body, name=name, grid=(m // tm, n // tn, ns), in_specs=in_specs + [ANY], out_specs=out_spec,
        out_shape=jax.ShapeDtypeStruct(into.shape, out_dtype), input_output_aliases={2: 0},
        scratch_shapes=[pltpu.VMEM((tm, tn), F32)] if ns > 1 else [],
        compiler_params=_cp("parallel", "parallel", "arbitrary"),
    )(a, b, into)


def _rows8(rows, width):
    idx = lax.broadcasted_iota(jnp.int32, (SUB, width), 0)
    out = jnp.zeros((SUB, width), F32)
    for r, v in enumerate(rows):
        out = jnp.where(idx == r, v, out)
    return out


def _rms_fwd(x, g, width, col_blk, name):
    s = x.shape[0]
    tr = _pick(s, ROW_TILE, 16)

    def body(x_ref, g_ref, u_ref):
        xv = x_ref[...]
        r = lax.rsqrt(jnp.mean(xv * xv, axis=-1, keepdims=True) + NORM_EPS)
        u_ref[...] = ((xv * r) * g_ref[...]).astype(BF)

    return pl.pallas_call(
        body, name=name, grid=(s // tr,),
        in_specs=[pl.BlockSpec((tr, width), lambda i: (i, col_blk)),
                  pl.BlockSpec((1, width), lambda i: (0, 0))],
        out_specs=pl.BlockSpec((tr, width), lambda i: (i, 0)),
        out_shape=jax.ShapeDtypeStruct((s, width), BF),
        compiler_params=_cp("parallel"),
    )(x, g)


def _rms_bwd(x, du, g, width, col_blk, name, extra=None, out_dtype=F32, also_bf16=False):
    s = x.shape[0]
    tr = _pick(s, ROW_TILE_BWD, 16)

    def body(*refs):
        x_ref, du_ref, g_ref = refs[:3]
        e_ref = refs[3] if extra is not None else None
        dx_ref = refs[3 + (extra is not None)]
        dxb_ref = refs[4 + (extra is not None)] if also_bf16 else None
        dg_ref = refs[-1]
        i = pl.program_id(0)
        xv = x_ref[...]
        duv = du_ref[...].astype(F32)
        r = lax.rsqrt(jnp.mean(xv * xv, axis=-1, keepdims=True) + NORM_EPS)
        nv = xv * r
        dn = duv * g_ref[...]
        dx = r * (dn - nv * jnp.mean(dn * nv, axis=-1, keepdims=True))
        if extra is not None:
            dx = dx + e_ref[...]
        dx_ref[...] = dx.astype(out_dtype)
        if also_bf16:
            dxb_ref[...] = dx.astype(BF)

        @pl.when(i == 0)
        def _():
            dg_ref[...] = jnp.zeros_like(dg_ref)

        dg_ref[...] += _rows8([jnp.sum(duv * nv, axis=0, keepdims=True)], width)

    in_specs = [pl.BlockSpec((tr, width), lambda i: (i, col_blk)),
                pl.BlockSpec((tr, width), lambda i: (i, 0)),
                pl.BlockSpec((1, width), lambda i: (0, 0))]
    args = [x, du, g]
    if extra is not None:
        in_specs.append(pl.BlockSpec((tr, width), lambda i: (i, 0)))
        args.append(extra)
    return pl.pallas_call(
        body, name=name, grid=(s // tr,),
        in_specs=in_specs,
        out_specs=[pl.BlockSpec((tr, width), lambda i: (i, 0))] * (1 + also_bf16)
        + [pl.BlockSpec((SUB, width), lambda i: (0, 0))],
        out_shape=[jax.ShapeDtypeStruct((s, width), out_dtype)] + [jax.ShapeDtypeStruct((s, width), BF)] * also_bf16
        + [jax.ShapeDtypeStruct((SUB, width), F32)],
        compiler_params=_cp("arbitrary"),
    )(*args)


def _down(cur, prev8, k):
    ext = jnp.concatenate([prev8, cur], axis=0)
    return pltpu.roll(ext, k, axis=0)[SUB:]


def _up(cur, next8, k):
    ext = jnp.concatenate([cur, next8], axis=0)
    return pltpu.roll(ext, ext.shape[0] - k, axis=0)[:cur.shape[0]]


def _lags(cur, prev8):
    return _down(cur, prev8, 1), _down(cur, prev8, 2)


def _conv3(w_ref, cur, prev8, lags=None):
    lag1, lag2 = _lags(cur, prev8) if lags is None else lags
    return w_ref[0:1, :] * lag2 + w_ref[1:2, :] * lag1 + w_ref[2:3, :] * cur


def _conv3_t(w_ref, cur, next8):
    return w_ref[2:3, :] * cur + w_ref[1:2, :] * _up(cur, next8, 1) + w_ref[0:1, :] * _up(cur, next8, 2)


def _spec_cur(tr, tc, c0):
    return pl.BlockSpec((tr, tc), lambda j, i: (i, c0 + j))


def _spec_prev(tr, tc, c0):
    return pl.BlockSpec((SUB, tc), lambda j, i: (jnp.maximum(i * (tr // SUB) - 1, 0), c0 + j))


def _spec_next(tr, tc, c0, s):
    return pl.BlockSpec((SUB, tc), lambda j, i: (jnp.minimum((i + 1) * (tr // SUB), s // SUB - 1), c0 + j))


def _spec_w(tc, c0):
    return pl.BlockSpec((SUB, tc), lambda j, i: (0, c0 + j))


def _pad8(w):
    return jnp.pad(w, ((0, SUB - w.shape[0]), (0, 0)))


def _conv_mix_fwd(z_a, cw8, conv):
    s = z_a.shape[0]
    tr = _pick(s, ROW_TILE, 16)
    tc = _pick(conv, COL_TILE, LANE)
    nc = conv // tc

    def body(zb_ref, zc_ref, zv_ref, zcp_ref, zvp_ref, w_ref, p_ref):
        i = pl.program_id(1)
        cv = zc_ref[...] * zv_ref[...]
        cvp = jnp.where(i > 0, zcp_ref[...] * zvp_ref[...], 0.0)
        p_ref[...] = (zb_ref[...] * _conv3(w_ref, cv, cvp)).astype(BF)

    return pl.pallas_call(
        body, name="conv_mix_fwd", grid=(nc, s // tr),
        in_specs=[_spec_cur(tr, tc, 0), _spec_cur(tr, tc, nc), _spec_cur(tr, tc, 2 * nc),
                  _spec_prev(tr, tc, nc), _spec_prev(tr, tc, 2 * nc), _spec_w(tc, 0)],
        out_specs=_spec_cur(tr, tc, 0),
        out_shape=jax.ShapeDtypeStruct((s, conv), BF),
        compiler_params=_cp("parallel", "parallel"),
    )(z_a, z_a, z_a, z_a, z_a, cw8)


def _conv_mix_bwd(z_a, d_p, cw8, conv):
    s = z_a.shape[0]
    tr = _pick(s, ROW_TILE_BWD, 16)
    tc = _pick(conv, COL_TILE, LANE)
    nc = conv // tc
    nr = s // tr

    def body(zb_ref, zbn_ref, zc_ref, zcp_ref, zv_ref, zvp_ref, dp_ref, dpn_ref, w_ref,
             dzb_ref, dzc_ref, dzv_ref, dw_ref):
        i = pl.program_id(1)
        zc = zc_ref[...]
        zv = zv_ref[...]
        cv = zc * zv
        cvp = jnp.where(i > 0, zcp_ref[...] * zvp_ref[...], 0.0)
        cv1, cv2 = _lags(cv, cvp)
        dpv = dp_ref[...]
        dzb_ref[...] = (dpv * _conv3(w_ref, cv, cvp, (cv1, cv2))).astype(BF)
        dcc = dpv * zb_ref[...]
        dccn = jnp.where(i < nr - 1, dpn_ref[...] * zbn_ref[...], 0.0)
        dcv = _conv3_t(w_ref, dcc, dccn)
        dzc_ref[...] = (dcv * zv).astype(BF)
        dzv_ref[...] = (dcv * zc).astype(BF)

        @pl.when(i == 0)
        def _():
            dw_ref[...] = jnp.zeros_like(dw_ref)

        dw_ref[...] += _rows8([jnp.sum(dcc * cv2, axis=0, keepdims=True),
                               jnp.sum(dcc * cv1, axis=0, keepdims=True),
                               jnp.sum(dcc * cv, axis=0, keepdims=True)], tc)

    out = jax.ShapeDtypeStruct((s, conv), BF)
    return pl.pallas_call(
        body, name="conv_mix_bwd", grid=(nc, nr),
        in_specs=[_spec_cur(tr, tc, 0), _spec_next(tr, tc, 0, s),
                  _spec_cur(tr, tc, nc), _spec_prev(tr, tc, nc),
                  _spec_cur(tr, tc, 2 * nc), _spec_prev(tr, tc, 2 * nc),
                  _spec_cur(tr, tc, 0), _spec_next(tr, tc, 0, s), _spec_w(tc, 0)],
        out_specs=[_spec_cur(tr, tc, 0), _spec_cur(tr, tc, 0), _spec_cur(tr, tc, 0), _spec_w(tc, 0)],
        out_shape=[out, out, out, jax.ShapeDtypeStruct((SUB, conv), F32)],
        compiler_params=_cp("parallel", "arbitrary"),
    )(z_a, z_a, z_a, z_a, z_a, z_a, d_p, d_p, cw8)


def _silu_parts(ag):
    sg = jax.nn.sigmoid(ag)
    return ag * sg, sg


def _ffn_act_fwd(a_pre, cw8, cb, dff):
    s = a_pre.shape[0]
    tr = _pick(s, ROW_TILE, 16)
    tc = _pick(dff, COL_TILE, LANE)
    nc = dff // tc

    def body(xg_ref, xgp_ref, xu_ref, xup_ref, wg_ref, wu_ref, bg_ref, bu_ref, f_ref):
        i = pl.program_id(1)
        xgp = jnp.where(i > 0, xgp_ref[...], 0.0)
        xup = jnp.where(i > 0, xup_ref[...], 0.0)
        ag = _conv3(wg_ref, xg_ref[...], xgp) + bg_ref[...]
        au = _conv3(wu_ref, xu_ref[...], xup) + bu_ref[...]
        f_ref[...] = (_silu_parts(ag)[0] * au).astype(BF)

    return pl.pallas_call(
        body, name="ffn_act_fwd", grid=(nc, s // tr),
        in_specs=[_spec_cur(tr, tc, 0), _spec_prev(tr, tc, 0), _spec_cur(tr, tc, nc), _spec_prev(tr, tc, nc),
                  _spec_w(tc, 0), _spec_w(tc, nc),
                  pl.BlockSpec((1, tc), lambda j, i: (0, j)), pl.BlockSpec((1, tc), lambda j, i: (0, nc + j))],
        out_specs=_spec_cur(tr, tc, 0),
        out_shape=jax.ShapeDtypeStruct((s, dff), BF),
        compiler_params=_cp("parallel", "parallel"),
    )(a_pre, a_pre, a_pre, a_pre, cw8, cw8, cb, cb)


def _ffn_act_bwd(a_pre, d_f, cw8, cb, dff):
    s = a_pre.shape[0]
    tr = _pick(s, ROW_TILE_BWD, 16)
    tc = _pick(dff, COL_TILE, LANE)
    nc = dff // tc
    nr = s // tr

    def body(xg_ref, xgp_ref, xgn_ref, xu_ref, xup_ref, xun_ref, df_ref, dfn_ref,
             wg_ref, wu_ref, bg_ref, bu_ref, dxg_ref, dxu_ref, dwg_ref, dwu_ref):
        i = pl.program_id(1)
        xg = xg_ref[...]
        xu = xu_ref[...]
        xgp = jnp.where(i > 0, xgp_ref[...], 0.0)
        xup = jnp.where(i > 0, xup_ref[...], 0.0)

        def d_act(xg_t, xgp_t, xu_t, xup_t, df_t, lags_g=None, lags_u=None):
            ag = _conv3(wg_ref, xg_t, xgp_t, lags_g) + bg_ref[...]
            au = _conv3(wu_ref, xu_t, xup_t, lags_u) + bu_ref[...]
            sil, sg = _silu_parts(ag)
            return df_t * au * (sg * (1.0 + ag * (1.0 - sg))), df_t * sil

        lags_g = _lags(xg, xgp)
        lags_u = _lags(xu, xup)
        dag, dau = d_act(xg, xgp, xu, xup, df_ref[...], lags_g, lags_u)
        dfn = jnp.where(i < nr - 1, dfn_ref[...], 0.0)
        dagn, daun = d_act(xgn_ref[...], xg[tr - SUB:], xun_ref[...], xu[tr - SUB:], dfn)
        dxg_ref[...] = _conv3_t(wg_ref, dag, dagn).astype(BF)
        dxu_ref[...] = _conv3_t(wu_ref, dau, daun).astype(BF)

        @pl.when(i == 0)
        def _():
            dwg_ref[...] = jnp.zeros_like(dwg_ref)
            dwu_ref[...] = jnp.zeros_like(dwu_ref)

        def wgrad(da, x, lags):
            return _rows8([jnp.sum(da * lags[1], axis=0, keepdims=True),
                           jnp.sum(da * lags[0], axis=0, keepdims=True),
                           jnp.sum(da * x, axis=0, keepdims=True),
                           jnp.sum(da, axis=0, keepdims=True)], tc)

        dwg_ref[...] += wgrad(dag, xg, lags_g)
        dwu_ref[...] += wgrad(dau, xu, lags_u)

    half = jax.ShapeDtypeStruct((s, dff), BF)
    wsh = jax.ShapeDtypeStruct((SUB, dff), F32)
    return pl.pallas_call(
        body, name="ffn_act_bwd", grid=(nc, nr),
        in_specs=[_spec_cur(tr, tc, 0), _spec_prev(tr, tc, 0), _spec_next(tr, tc, 0, s),
                  _spec_cur(tr, tc, nc), _spec_prev(tr, tc, nc), _spec_next(tr, tc, nc, s),
                  _spec_cur(tr, tc, 0), _spec_next(tr, tc, 0, s),
                  _spec_w(tc, 0), _spec_w(tc, nc),
                  pl.BlockSpec((1, tc), lambda j, i: (0, j)), pl.BlockSpec((1, tc), lambda j, i: (0, nc + j))],
        out_specs=[_spec_cur(tr, tc, 0), _spec_cur(tr, tc, 0), _spec_w(tc, 0), _spec_w(tc, 0)],
        out_shape=[half, half, wsh, wsh],
        compiler_params=_cp("parallel", "arbitrary"),
    )(a_pre, a_pre, a_pre, a_pre, a_pre, a_pre, d_f, d_f, cw8, cw8, cb, cb)


def _gate_fwd(z_g, b_gate, yc, ym, d):
    s = z_g.shape[0]
    tr = _pick(s, ROW_TILE, 16)
    tc = _pick(d, COL_TILE, LANE)
    nc = d // tc

    def body(za_ref, zb_ref, ba_ref, bb_ref, yc_ref, ym_ref, o_ref):
        ga = jax.nn.sigmoid(za_ref[...] + ba_ref[...])
        gb = jax.nn.sigmoid(zb_ref[...] + bb_ref[...])
        o_ref[...] = (ga * yc_ref[...] + gb * ym_ref[...]).astype(BF)

    return pl.pallas_call(
        body, name="gate_fwd", grid=(nc, s // tr),
        in_specs=[_spec_cur(tr, tc, 0), _spec_cur(tr, tc, nc),
                  pl.BlockSpec((1, tc), lambda j, i: (0, j)), pl.BlockSpec((1, tc), lambda j, i: (0, nc + j)),
                  _spec_cur(tr, tc, 0), _spec_cur(tr, tc, 0)],
        out_specs=_spec_cur(tr, tc, 0),
        out_shape=jax.ShapeDtypeStruct((s, d), BF),
        compiler_params=_cp("parallel", "parallel"),
    )(z_g, z_g, b_gate, b_gate, yc, ym)


def _gate_bwd(d_mix, z_g, b_gate, yc, ym, d):
    s = z_g.shape[0]
    tr = _pick(s, ROW_TILE, 16)
    tc = _pick(d, COL_TILE, LANE)
    nc = d // tc

    def body(dm_ref, za_ref, zb_ref, ba_ref, bb_ref, yc_ref, ym_ref,
             dza_ref, dzb_ref, dyc_ref, dym_ref, dba_ref, dbb_ref):
        i = pl.program_id(1)
        dm = dm_ref[...].astype(F32)
        ga = jax.nn.sigmoid(za_ref[...] + ba_ref[...])
        gb = jax.nn.sigmoid(zb_ref[...] + bb_ref[...])
        dza = dm * yc_ref[...] * (ga * (1.0 - ga))
        dzb = dm * ym_ref[...] * (gb * (1.0 - gb))
        dza_ref[...] = dza.astype(BF)
        dzb_ref[...] = dzb.astype(BF)
        dyc_ref[...] = (dm * ga).astype(BF)
        dym_ref[...] = (dm * gb).astype(BF)

        @pl.when(i == 0)
        def _():
            dba_ref[...] = jnp.zeros_like(dba_ref)
            dbb_ref[...] = jnp.zeros_like(dbb_ref)

        dba_ref[...] += _rows8([jnp.sum(dza, axis=0, keepdims=True)], tc)
        dbb_ref[...] += _rows8([jnp.sum(dzb, axis=0, keepdims=True)], tc)

    act = jax.ShapeDtypeStruct((s, d), BF)
    bsh = jax.ShapeDtypeStruct((SUB, d), F32)
    return pl.pallas_call(
        body, name="gate_bwd", grid=(nc, s // tr),
        in_specs=[_spec_cur(tr, tc, 0), _spec_cur(tr, tc, 0), _spec_cur(tr, tc, nc),
                  pl.BlockSpec((1, tc), lambda j, i: (0, j)), pl.BlockSpec((1, tc), lambda j, i: (0, nc + j)),
                  _spec_cur(tr, tc, 0), _spec_cur(tr, tc, 0)],
        out_specs=[_spec_cur(tr, tc, 0)] * 4 + [_spec_w(tc, 0)] * 2,
        out_shape=[act, act, act, act, bsh, bsh],
        compiler_params=_cp("parallel", "arbitrary"),
    )(d_mix, z_g, z_g, b_gate, b_gate, yc, ym)


def _lay(v):
    z = jnp.zeros(v.shape[:-1] + (HALF,), v.dtype)
    return jnp.concatenate([v[..., :HALF], z, v[..., HALF:], z], axis=-1)


def _unlay(v):
    return jnp.concatenate([v[..., :HALF], v[..., 2 * HALF:3 * HALF]], axis=-1)


def _lay_rows(v):
    z = jnp.zeros((HALF,) + v.shape[1:], v.dtype)
    return jnp.concatenate([v[:HALF], z, v[HALF:], z], axis=0)


def _rope_tables(positions):
    s = positions.shape[0]
    tr = _pick(s, ROW_TILE, 8)
    inv_freq = ROPE_THETA ** (-jnp.arange(0, ROPE, 2, dtype=F32) / ROPE)
    consts = jnp.stack([_lay(jnp.concatenate([inv_freq, inv_freq])),
                        _lay(jnp.ones((ROPE,), F32)),
                        _lay(jnp.concatenate([-jnp.ones((HALF,), F32), jnp.ones((HALF,), F32)]))])
    consts = _pad8(consts)

    def body(p_ref, c_ref, cos_ref, sin_ref):
        ang = p_ref[...].astype(F32) * c_ref[0:1, :]
        cos_ref[...] = jnp.cos(ang) * c_ref[1:2, :]
        sin_ref[...] = jnp.sin(ang) * c_ref[2:3, :]

    tab = jax.ShapeDtypeStruct((s, LANE), F32)
    return pl.pallas_call(
        body, name="rope_tables", grid=(s // tr,),
        in_specs=[pl.BlockSpec((tr, 1), lambda i: (i, 0)), pl.BlockSpec((SUB, LANE), lambda i: (0, 0))],
        out_specs=[pl.BlockSpec((tr, LANE), lambda i: (i, 0))] * 2,
        out_shape=[tab, tab],
        compiler_params=_cp("parallel"),
    )(positions, consts)


def _lane_sum(p):
    return jnp.sum(p, axis=-1, keepdims=True)


def _rope(t, cos, sin):
    return t * cos + pltpu.roll(t, 2 * HALF, axis=1) * sin


def _rope_t(d, cos, sin):
    return d * cos + pltpu.roll(d * sin, 2 * HALF, axis=1)


def _head_fwd(q_raw, kv_raw, z_a, kr_blk, cos, sin, gains, heads):
    s = q_raw.shape[0]
    tr = _pick(s, HEAD_ROW_TILE, 16)
    hw = heads * LANE

    def body(q_ref, kv_ref, kr_ref, cos_ref, sin_ref, g_ref, qo_ref, ko_ref, vo_ref):
        cosv = cos_ref[...]
        sinv = sin_ref[...]
        krv = kr_ref[...]
        kr_sq = krv * krv
        for h in range(heads):
            lo = h * LANE
            qn = q_ref[:, lo:lo + LANE]
            qr = q_ref[:, hw + lo:hw + lo + LANE]
            r = lax.rsqrt(_lane_sum(qn * qn + qr * qr) / HEAD_QK + NORM_EPS)
            qo_ref[:, 2 * lo:2 * lo + LANE] = ((qn * r) * g_ref[0:1, :]).astype(BF)
            qo_ref[:, 2 * lo + LANE:2 * lo + 2 * LANE] = _rope((qr * r) * g_ref[1:2, :], cosv, sinv).astype(BF)
            kn = kv_ref[:, 2 * lo:2 * lo + LANE]
            r = lax.rsqrt(_lane_sum(kn * kn + kr_sq) / HEAD_QK + NORM_EPS)
            ko_ref[:, 2 * lo:2 * lo + LANE] = ((kn * r) * g_ref[2:3, :]).astype(BF)
            ko_ref[:, 2 * lo + LANE:2 * lo + 2 * LANE] = _rope((krv * r) * g_ref[3:4, :], cosv, sinv).astype(BF)
            vo_ref[:, lo:lo + LANE] = kv_ref[:, 2 * lo + LANE:2 * lo + 2 * LANE].astype(BF)

    row = lambda w: pl.BlockSpec((tr, w), lambda i: (i, 0))
    return pl.pallas_call(
        body, name="head_fwd", grid=(s // tr,),
        in_specs=[row(2 * hw), row(2 * hw), pl.BlockSpec((tr, LANE), lambda i: (i, kr_blk)),
                  row(LANE), row(LANE), pl.BlockSpec((SUB, LANE), lambda i: (0, 0))],
        out_specs=[row(2 * hw), row(2 * hw), row(hw)],
        out_shape=[jax.ShapeDtypeStruct((s, 2 * hw), BF), jax.ShapeDtypeStruct((s, 2 * hw), BF),
                   jax.ShapeDtypeStruct((s, hw), BF)],
        compiler_params=_cp("parallel"),
    )(q_raw, kv_raw, z_a, cos, sin, gains)


def _head_bwd(q_raw, kv_raw, z_a, kr_blk, cos, sin, gains, dq_att, dk_att, dv, heads):
    s = q_raw.shape[0]
    tr = _pick(s, HEAD_ROW_TILE_BWD, 16)
    hw = heads * LANE

    def body(q_ref, kv_ref, kr_ref, cos_ref, sin_ref, g_ref, dq_ref, dk_ref, dv_ref,
             dqr_ref, dkv_ref, dkr_ref, dg_ref):
        i = pl.program_id(0)
        cosv = cos_ref[...]
        sinv = sin_ref[...]
        krv = kr_ref[...]
        kr_sq = krv * krv
        dkr = jnp.zeros((tr, LANE), F32)
        dgs = [jnp.zeros((1, LANE), F32) for _ in range(4)]

        def norm_bwd(xn, xr, sq, dn_out, dr_out, gn, gr):
            r = lax.rsqrt(_lane_sum(sq) / HEAD_QK + NORM_EPS)
            nn = xn * r
            nr = xr * r
            dt = _rope_t(dr_out, cosv, sinv)
            dnn = dn_out * gn
            dnr = dt * gr
            mean = _lane_sum(dnn * nn + dnr * nr) / HEAD_QK
            return (r * (dnn - nn * mean), r * (dnr - nr * mean),
                    jnp.sum(dn_out * nn, axis=0, keepdims=True), jnp.sum(dt * nr, axis=0, keepdims=True))

        for h in range(heads):
            lo = h * LANE
            qn = q_ref[:, lo:lo + LANE]
            qr = q_ref[:, hw + lo:hw + lo + LANE]
            dxn, dxr, g0, g1 = norm_bwd(qn, qr, qn * qn + qr * qr, dq_ref[:, 2 * lo:2 * lo + LANE],
                                        dq_ref[:, 2 * lo + LANE:2 * lo + 2 * LANE], g_ref[0:1, :], g_ref[1:2, :])
            dqr_ref[:, lo:lo + LANE] = dxn.astype(BF)
            dqr_ref[:, hw + lo:hw + lo + LANE] = dxr.astype(BF)
            kn = kv_ref[:, 2 * lo:2 * lo + LANE]
            dxn, dxr, g2, g3 = norm_bwd(kn, krv, kn * kn + kr_sq, dk_ref[:, 2 * lo:2 * lo + LANE],
                                        dk_ref[:, 2 * lo + LANE:2 * lo + 2 * LANE], g_ref[2:3, :], g_ref[3:4, :])
            dkv_ref[:, 2 * lo:2 * lo + LANE] = dxn.astype(BF)
            dkv_ref[:, 2 * lo + LANE:2 * lo + 2 * LANE] = dv_ref[:, lo:lo + LANE].astype(BF)
            dkr = dkr + dxr
            dgs = [a + b for a, b in zip(dgs, (g0, g1, g2, g3))]
        dkr_ref[...] = dkr

        @pl.when(i == 0)
        def _():
            dg_ref[...] = jnp.zeros_like(dg_ref)

        dg_ref[...] += _rows8(dgs, LANE)

    row = lambda w: pl.BlockSpec((tr, w), lambda i: (i, 0))
    return pl.pallas_call(
        body, name="head_bwd", grid=(s // tr,),
        in_specs=[row(2 * hw), row(2 * hw), pl.BlockSpec((tr, LANE), lambda i: (i, kr_blk)),
                  row(LANE), row(LANE), pl.BlockSpec((SUB, LANE), lambda i: (0, 0)),
                  row(2 * hw), row(2 * hw), row(hw)],
        out_specs=[row(2 * hw), row(2 * hw), row(LANE), pl.BlockSpec((SUB, LANE), lambda i: (0, 0))],
        out_shape=[jax.ShapeDtypeStruct((s, 2 * hw), BF), jax.ShapeDtypeStruct((s, 2 * hw), BF),
                   jax.ShapeDtypeStruct((s, LANE), F32), jax.ShapeDtypeStruct((SUB, LANE), F32)],
        compiler_params=_cp("arbitrary"),
    )(q_raw, kv_raw, z_a, cos, sin, gains, dq_att, dk_att, dv)


def _causal_mask(nrows, ncols, row0):
    rows = lax.broadcasted_iota(jnp.int32, (nrows, ncols), 0) + row0
    cols = lax.broadcasted_iota(jnp.int32, (nrows, ncols), 1)
    return cols <= rows


def _causal_steps(nt, q_major):
    pairs = ([(i, j) for i in range(nt) for j in range(i + 1)] if q_major
             else [(i, j) for j in range(nt) for i in range(j, nt)])
    return (jnp.array([p[0] for p in pairs], jnp.int32), jnp.array([p[1] for p in pairs], jnp.int32))


def _attn_fwd(q_att, k_att, v, heads):
    s = q_att.shape[0]
    t = _pick(s, ATTN_TILE_FWD, LANE)
    nt = s // t
    th = t // 2
    scale = HEAD_QK ** -0.5
    qi, kj = _causal_steps(nt, True)

    def body(qi_ref, kj_ref, q_ref, k_ref, v_ref, o_ref, ob_ref, lse_ref, m_s, l_s, acc_s):
        st = pl.program_id(1)
        i = qi_ref[st]
        j = kj_ref[st]

        @pl.when(j == 0)
        def _():
            m_s[...] = jnp.full_like(m_s, NEG_INF)
            l_s[...] = jnp.zeros_like(l_s)
            acc_s[...] = jnp.zeros_like(acc_s)

        def update(rows, ncol, masked):
            sc = lax.dot_general(q_ref[rows, :], k_ref[0:ncol, :], (((1,), (1,)), ((), ())),
                                 preferred_element_type=F32) * scale
            if masked:
                sc = jnp.where(_causal_mask(rows.stop - rows.start, ncol, rows.start), sc, NEG_INF)
            m_prev = m_s[rows, :]
            m_new = jnp.maximum(m_prev, jnp.max(sc, axis=-1, keepdims=True))
            alpha = jnp.exp(m_prev - m_new)
            p = jnp.exp(sc - jnp.tile(m_new, (1, ncol // LANE)))
            l_s[rows, :] = alpha * l_s[rows, :] + jnp.sum(p, axis=-1, keepdims=True)
            acc_s[rows, :] = alpha * acc_s[rows, :] + jnp.dot(p.astype(BF), v_ref[0:ncol, :],
                                                              preferred_element_type=F32)
            m_s[rows, :] = m_new

        @pl.when(j < i)
        def _():
            update(slice(0, t), t, False)

        @pl.when(j == i)
        def _():
            update(slice(0, th), th, True)
            update(slice(th, t), t, True)
            o = acc_s[...] / l_s[...]
            o_ref[...] = o
            ob_ref[...] = o.astype(BF)
            lse_ref[...] = (m_s[...] + jnp.log(l_s[...]))[:, 0:1]

    q_idx = lambda h, st, qi_r, kj_r: (qi_r[st], h)
    kv_idx = lambda h, st, qi_r, kj_r: (kj_r[st], h)
    return pl.pallas_call(
        body, name="attn_fwd",
        grid_spec=pltpu.PrefetchScalarGridSpec(
            num_scalar_prefetch=2, grid=(heads, qi.shape[0]),
            in_specs=[pl.BlockSpec((t, 2 * LANE), q_idx), pl.BlockSpec((t, 2 * LANE), kv_idx),
                      pl.BlockSpec((t, LANE), kv_idx)],
            out_specs=[pl.BlockSpec((t, LANE), q_idx), pl.BlockSpec((t, LANE), q_idx),
                       pl.BlockSpec((None, t, 1), lambda h, st, qi_r, kj_r: (h, qi_r[st], 0))],
            scratch_shapes=[pltpu.VMEM((t, LANE), F32), pltpu.VMEM((t, LANE), F32), pltpu.VMEM((t, LANE), F32)]),
        out_shape=[jax.ShapeDtypeStruct((s, heads * LANE), F32), jax.ShapeDtypeStruct((s, heads * LANE), BF),
                   jax.ShapeDtypeStruct((heads, s, 1), F32)],
        compiler_params=_cp("parallel", "arbitrary"),
    )(qi, kj, q_att, k_att, v)


def _attn_bwd(q_att, k_att, v, o, lse, d_o, heads, dep=None):
    s = q_att.shape[0]
    t = _pick(s, ATTN_TILE, LANE)
    nt = s // t
    th = t // 2
    scale = HEAD_QK ** -0.5
    qi, kj = _causal_steps(nt, False)

    def body(qi_ref, kj_ref, q_ref, k_ref, v_ref, do_ref, o_ref, lse_ref, *rest):
        dq_ref, dk_ref, dv_ref, dk_s, dv_s = rest[-5:]
        st = pl.program_id(1)
        i = qi_ref[st]
        j = kj_ref[st]

        @pl.when(st == 0)
        def _():
            dq_ref[...] = jnp.zeros_like(dq_ref)

        @pl.when(i == j)
        def _():
            dk_s[...] = jnp.zeros_like(dk_s)
            dv_s[...] = jnp.zeros_like(dv_s)

        def update(rows, ncol, masked):
            nrow = rows.stop - rows.start
            q = q_ref[rows, :]
            k = k_ref[0:ncol, :]
            do = do_ref[rows, :]
            sc = lax.dot_general(q, k, (((1,), (1,)), ((), ())), preferred_element_type=F32) * scale
            if masked:
                sc = jnp.where(_causal_mask(nrow, ncol, rows.start), sc, NEG_INF)
            p = jnp.exp(sc - lse_ref[rows, :])
            dp = lax.dot_general(do, v_ref[0:ncol, :], (((1,), (1,)), ((), ())), preferred_element_type=F32)
            delta = jnp.sum(do.astype(F32) * o_ref[rows, :], axis=-1, keepdims=True)
            ds = (p * (dp - delta) * scale).astype(BF)
            dv_s[0:ncol, :] += lax.dot_general(p.astype(BF), do, (((0,), (0,)), ((), ())),
                                               preferred_element_type=F32)
            dk_s[0:ncol, :] += lax.dot_general(ds, q, (((0,), (0,)), ((), ())), preferred_element_type=F32)
            out_rows = pl.ds(pl.multiple_of(i * t + rows.start, nrow), nrow)
            dq_ref[out_rows, :] += jnp.dot(ds, k, preferred_element_type=F32)

        @pl.when(i > j)
        def _():
            update(slice(0, t), t, False)

        @pl.when(i == j)
        def _():
            update(slice(0, th), th, True)
            update(slice(th, t), t, True)

        @pl.when(i == nt - 1)
        def _():
            dk_ref[...] = dk_s[...].astype(BF)
            dv_ref[...] = dv_s[...].astype(BF)

    q_idx = lambda h, st, qi_r, kj_r: (qi_r[st], h)
    kv_idx = lambda h, st, qi_r, kj_r: (kj_r[st], h)
    in_specs = [pl.BlockSpec((t, 2 * LANE), q_idx), pl.BlockSpec((t, 2 * LANE), kv_idx),
                pl.BlockSpec((t, LANE), kv_idx), pl.BlockSpec((t, LANE), q_idx), pl.BlockSpec((t, LANE), q_idx),
                pl.BlockSpec((None, t, 1), lambda h, st, qi_r, kj_r: (h, qi_r[st], 0))]
    args = [q_att, k_att, v, d_o, o, lse]
    if dep is not None:
        in_specs.append(ANY)
        args.append(dep)
    return pl.pallas_call(
        body, name="attn_bwd",
        grid_spec=pltpu.PrefetchScalarGridSpec(
            num_scalar_prefetch=2, grid=(heads, qi.shape[0]),
            in_specs=in_specs,
            out_specs=[pl.BlockSpec((s, 2 * LANE), lambda h, st, qi_r, kj_r: (0, h)),
                       pl.BlockSpec((t, 2 * LANE), kv_idx), pl.BlockSpec((t, LANE), kv_idx)],
            scratch_shapes=[pltpu.VMEM((t, 2 * LANE), F32), pltpu.VMEM((t, LANE), F32)]),
        out_shape=[jax.ShapeDtypeStruct((s, heads * 2 * LANE), F32),
                   jax.ShapeDtypeStruct((s, heads * 2 * LANE), BF),
                   jax.ShapeDtypeStruct((s, heads * LANE), BF)],
        compiler_params=_cp("parallel", "arbitrary"),
    )(qi, kj, *args)


def _sum_parts(parts, name):
    n, r, c = parts.shape
    tr = _pick(r, 512, 8)

    def body(p_ref, o_ref):
        g = p_ref[0].astype(F32)
        for k in range(1, n):
            g = g + p_ref[k].astype(F32)
        o_ref[...] = g

    return pl.pallas_call(
        body, name=name, grid=(r // tr,),
        in_specs=[pl.BlockSpec((n, tr, c), lambda i: (0, i, 0))],
        out_specs=pl.BlockSpec((tr, c), lambda i: (i, 0)),
        out_shape=jax.ShapeDtypeStruct((r, c), F32),
        compiler_params=_cp("parallel"),
    )(parts)


def _adamw(parts, w, m, v, name, by_cols=False):
    n, rp, c = parts.shape
    r = w.shape[0]
    assert by_cols or rp == r
    tr, tc = (r, _pick(c, 256, LANE)) if by_cols else (_pick(r, 256, 16 if r % 16 == 0 else 8), c)

    def body(p_ref, w_ref, m_ref, v_ref, g_ref, d_ref, mo_ref, vo_ref):
        g = p_ref[0].astype(F32)
        for k in range(1, n):
            g = g + p_ref[k].astype(F32)
        g = g[:r] if by_cols else g
        m_new = ADAM_B1 * m_ref[...] + (1.0 - ADAM_B1) * g
        v_new = ADAM_B2 * v_ref[...] + (1.0 - ADAM_B2) * jnp.square(g)
        m_hat = m_new / (1.0 - ADAM_B1 ** ADAM_STEP)
        v_hat = v_new / (1.0 - ADAM_B2 ** ADAM_STEP)
        g_ref[...] = g
        d_ref[...] = -ADAM_LR * (m_hat / (jnp.sqrt(v_hat) + ADAM_EPS) + ADAM_WD * w_ref[...])
        mo_ref[...] = m_new
        vo_ref[...] = v_new

    idx = (lambda i: (0, i)) if by_cols else (lambda i: (i, 0))
    spec = pl.BlockSpec((tr, tc), idx)
    sh = jax.ShapeDtypeStruct((r, c), F32)
    return pl.pallas_call(
        body, name=name, grid=(c // tc if by_cols else r // tr,),
        in_specs=[pl.BlockSpec((n, rp if by_cols else tr, tc), lambda i: (0,) + idx(i)), spec, spec, spec],
        out_specs=[spec] * 4, out_shape=[sh] * 4,
        compiler_params=_cp("parallel"),
    )(parts, w, m, v)


def _place():
    x, y, c = lax.axis_index("x"), lax.axis_index("y"), lax.axis_index("c")
    chips = [(1 - x, y), (x, 1 - y), (1 - x, 1 - y)]
    return x, y, c, chips


def _all_gather(shards, name, dep=None):
    n = len(shards)
    deps = [] if dep is None else list(dep)

    def body(*refs):
        ins, outs = refs[:n], refs[n + len(deps):2 * n + len(deps)]
        send_sems, recv_sems, local_sems = refs[2 * n + len(deps):]
        x, y, c, chips = _place()
        me, sibling = (x, y, c), (x, y, 1 - c)

        def slot(w, p):
            return outs[w].at[4 * p[0] + 2 * p[1] + p[2]]

        def copy(w, k, block, to, src=None):
            return pltpu.make_async_remote_copy(
                src_ref=slot(w, block) if src is None else src, dst_ref=slot(w, block),
                send_sem=send_sems.at[w, k], recv_sem=recv_sems.at[w, k], device_id=to, device_id_type=MESH)

        first = []
        for w in range(n):
            first += [copy(w, 1 + j, me, (*chip, c), src=ins[w]) for j, chip in enumerate(chips)]
            first.append(copy(w, 0, me, sibling, src=ins[w]))
        for cp in first:
            cp.start()
        mine = [pltpu.make_async_copy(ins[w], slot(w, me), local_sems.at[w]) for w in range(n)]
        for cp in mine:
            cp.start()
        passed = []
        for w in range(n):
            for j, chip in enumerate(chips):
                copy(w, 1 + j, (*chip, c), me).wait_recv()
                cp = copy(w, 4 + j, (*chip, c), sibling)
                cp.start()
                passed.append(cp)
        for w in range(n):
            copy(w, 0, sibling, me).wait_recv()
            for j, chip in enumerate(chips):
                copy(w, 4 + j, (*chip, 1 - c), me).wait_recv()
        for cp in first + passed:
            cp.wait_send()
        for cp in mine:
            cp.wait()

    return pl.pallas_call(
        body, name=name,
        in_specs=[ANY] * (n + len(deps)), out_specs=[ANY] * n,
        out_shape=[jax.ShapeDtypeStruct((N_DEV,) + a.shape, a.dtype) for a in shards],
        scratch_shapes=[pltpu.SemaphoreType.DMA((n, 7)), pltpu.SemaphoreType.DMA((n, 7)),
                        pltpu.SemaphoreType.DMA((n,))],
    )(*shards, *deps)


HBM = pl.BlockSpec(memory_space=pltpu.HBM)
SEM = pl.BlockSpec(memory_space=pltpu.SEMAPHORE)
EFFECT = pltpu.SideEffectType.DATAFLOW_SIDE_EFFECTING
PEERS = [(dx, dy, dc) for dx in (1, 0) for dy in (1, 0) for dc in (0, 1) if (dx, dy, dc) != (0, 0, 0)]


def _peer(x, y, c, flip):
    dx, dy, dc = flip
    return (1 - x if dx else x, 1 - y if dy else y, 1 - c if dc else c)


def _exchange_copies(srcs, lands, send, recv, loc, gather):
    x, y, c, _ = _place()
    me = 4 * x + 2 * y + c
    remote, local = [], []
    for w in range(len(srcs)):
        for k, flip in enumerate(PEERS):
            px, py, pc = _peer(x, y, c, flip)
            src = srcs[w] if gather else srcs[w].at[4 * px + 2 * py + pc]
            remote.append(pltpu.make_async_remote_copy(
                src_ref=src, dst_ref=lands[w].at[me], send_sem=send[w].at[k], recv_sem=recv[w].at[k],
                device_id=(px, py, pc), device_id_type=MESH))
        local.append(pltpu.make_async_copy(srcs[w] if gather else srcs[w].at[me], lands[w].at[me], loc[w]))
    return remote, local


class _Exchange:
    def __init__(self, srcs, lands, send, recv, loc, token, gather):
        self.srcs, self.lands, self.send, self.recv, self.loc = srcs, lands, send, recv, loc
        self.token, self.gather = token, gather


def _exchange_start(srcs, gather, name, dep=None):
    n = len(srcs)
    deps = [] if dep is None else [dep]
    land_shapes = [((N_DEV,) + a.shape) if gather else a.shape for a in srcs]
    lands = [pltpu.with_memory_space_constraint(lax.empty(sh, a.dtype), pltpu.HBM) for sh, a in zip(land_shapes, srcs)]
    srcs = [pltpu.with_memory_space_constraint(a, pltpu.HBM) for a in srcs]

    def body(*refs):
        src_refs, land_refs = refs[:n], refs[n:2 * n]
        outs = refs[2 * n + len(deps):]
        send, recv, loc = outs[:n], outs[n:2 * n], outs[2 * n:3 * n]
        token = outs[-1]
        remote, local = _exchange_copies(src_refs, land_refs, send, recv, loc, gather)
        for cp in remote + local:
            cp.start()
        token[...] = jnp.zeros_like(token)

    out_shape = ([pltpu.SemaphoreType.DMA((len(PEERS),))] * (2 * n) + [pltpu.SemaphoreType.DMA(())] * n
                 + [pltpu.HBM(a.shape, a.dtype) for a in srcs] + [pltpu.HBM(a.shape, a.dtype) for a in lands]
                 + [jax.ShapeDtypeStruct((SUB, LANE), F32)])
    res = pl.pallas_call(
        body, name=name, out_shape=out_shape,
        in_specs=[HBM] * (2 * n) + [ANY] * len(deps),
        out_specs=[SEM] * (3 * n) + [HBM] * (2 * n) + [pl.BlockSpec(memory_space=pltpu.VMEM)],
        input_output_aliases={i: 3 * n + i for i in range(2 * n)},
        compiler_params=pltpu.CompilerParams(has_side_effects=EFFECT),
    )(*srcs, *lands, *deps)
    return _Exchange(res[3 * n:4 * n], res[4 * n:5 * n], res[:n], res[n:2 * n], res[2 * n:3 * n], res[-1], gather)


def _exchange_wait(ex, idxs, after, name):
    n = len(idxs)
    srcs = [ex.srcs[i] for i in idxs]
    lands = [ex.lands[i] for i in idxs]
    sems = [ex.send[i] for i in idxs] + [ex.recv[i] for i in idxs] + [ex.loc[i] for i in idxs]
    gather = ex.gather

    def body(*refs):
        src_refs, land_refs = refs[:n], refs[n:2 * n]
        send, recv, loc = refs[2 * n:3 * n], refs[3 * n:4 * n], refs[4 * n:5 * n]
        remote, local = _exchange_copies(src_refs, land_refs, send, recv, loc, gather)
        for cp in remote:
            cp.wait_send()
            cp.wait_recv()
        for cp in local:
            cp.wait()

    res = pl.pallas_call(
        body, name=name,
        out_shape=[pltpu.HBM(a.shape, a.dtype) for a in srcs] + [pltpu.HBM(a.shape, a.dtype) for a in lands],
        in_specs=[HBM] * (2 * n) + [SEM] * (3 * n) + [ANY],
        out_specs=[HBM] * (2 * n),
        input_output_aliases={i: i for i in range(2 * n)},
        compiler_params=pltpu.CompilerParams(has_side_effects=EFFECT),
    )(*srcs, *lands, *sems, after)
    return res[n:]


def _after(token, a):
    return a + token[0:1, 0:1].astype(a.dtype)


def _unblock(w3):
    nb, k, nbw = w3.shape
    return w3.transpose(1, 0, 2).reshape(k, nb * nbw)


def _block(w, nb):
    k, n = w.shape
    return w.reshape(k, nb, n // nb).transpose(1, 0, 2)


def kernel(x, positions, ln1_g, w_in, b_gate, conv_w, w_conv_out, q_a_g, w_q_b, kv_a_g, w_kv_b, q_norm_g, k_norm_g, w_mla_out, w_o, ln2_g, w_ffn_up, ffn_conv_w, ffn_conv_b, w_ffn_down, loss_target, m_ln1_g, m_w_in, m_b_gate, m_conv_w, m_w_conv_out, m_q_a_g, m_w_q_b, m_kv_a_g, m_w_kv_b, m_q_norm_g, m_k_norm_g, m_w_mla_out, m_w_o, m_ln2_g, m_w_ffn_up, m_ffn_conv_w, m_ffn_conv_b, m_w_ffn_down, v_ln1_g, v_w_in, v_b_gate, v_conv_w, v_w_conv_out, v_q_a_g, v_w_q_b, v_kv_a_g, v_w_kv_b, v_q_norm_g, v_k_norm_g, v_w_mla_out, v_w_o, v_ln2_g, v_w_ffn_up, v_ffn_conv_w, v_ffn_conv_b, v_w_ffn_down):
    s, d = x.shape[1], x.shape[2]
    conv = conv_w.shape[2] * N_DEV
    ql, kvl = q_a_g.shape[1], kv_a_g.shape[1]
    heads = w_q_b.shape[2] * N_DEV // HEAD_QK
    dff = w_ffn_down.shape[1] * N_DEV
    hw = heads * LANE
    conv3 = 3 * conv
    kr_off = conv3 + ql
    kv_off = -(-(kr_off + LANE) // kvl) * kvl
    wa = kv_off + kvl
    assert conv3 % ql == 0 and kr_off % LANE == 0
    xs = x[0]
    tgt = loss_target[0]
    pos = positions.reshape(s, 1)

    nin = w_in.shape[2]
    big = dict(w_in=w_in[0].T, w_conv_out=w_conv_out[0], w_q_b=w_q_b[0], w_kv_b=w_kv_b[0],
               w_mla_out=w_mla_out[0], w_o=w_o[0], w_ffn_up=w_ffn_up[0], w_ffn_down=w_ffn_down[0])
    names = list(big)
    rest = names[1:]
    first = _all_gather([big["w_in"].astype(BF), _pad8(conv_w[0]), _pad8(ffn_conv_w[0])], "gather_w_in")
    cw8 = _unblock(first[1])
    fcw8 = _unblock(first[2])
    ag = _exchange_start([big[k].astype(BF) for k in rest], True, "gather_rest_start", dep=first[1])

    def landed(keys, after, name):
        return _exchange_wait(ag, [rest.index(k) for k in keys], after, name)

    w_in_t = first[0].reshape(N_DEV * nin, d)
    g_off = kr_off + kvl + ROPE
    w_a_t = jnp.concatenate([w_in_t[:kr_off], _lay_rows(w_in_t[kr_off + kvl:g_off]),
                             jnp.zeros((kv_off - kr_off - LANE, d), BF), w_in_t[kr_off:kr_off + kvl]], axis=0)[None]
    w_g_t = w_in_t[g_off:][None]
    gains = _pad8(jnp.concatenate([q_norm_g[:, :NOPE], _lay(q_norm_g[:, NOPE:]),
                                   k_norm_g[:, :NOPE], _lay(k_norm_g[:, NOPE:])], axis=0))
    kr_blk = kr_off // LANE

    cos, sin = _rope_tables(pos)
    u1 = _rms_fwd(xs, _after(ag.token, ln1_g), d, 0, "rms1_fwd")
    z_a = _mm_nt(u1, w_a_t, "mm_z_a")
    z_g = _mm_nt(u1, w_g_t, "mm_z_g", out_dtype=BF)
    p = _conv_mix_fwd(z_a, cw8, conv)
    w_co, w_qb, w_kv = landed(["w_conv_out", "w_q_b", "w_kv_b"], p, "gather_wait_mixers")
    wq_full = _unblock(w_qb).reshape(ql, heads, HEAD_QK)
    w_q = jnp.concatenate([wq_full[:, :, :NOPE].reshape(ql, hw), _lay(wq_full[:, :, NOPE:]).reshape(ql, hw)],
                          axis=1)[None]
    yc = _mm_nn(p, w_co, "mm_y_conv", out_dtype=BF)
    qn = _rms_fwd(z_a, q_a_g, ql, conv3 // ql, "rms_q_fwd")
    kvn = _rms_fwd(z_a, kv_a_g, kvl, kv_off // kvl, "rms_kv_fwd")
    q_raw = _mm_nn(qn, w_q, "mm_q")
    kv_raw = _mm_nn(kvn, w_kv, "mm_kv")
    q_att, k_att, v_bf = _head_fwd(q_raw, kv_raw, z_a, kr_blk, cos, sin, gains, heads)
    o, o_bf, lse = _attn_fwd(q_att, k_att, v_bf, heads)
    w_mo, w_oo = landed(["w_mla_out", "w_o"], lse, "gather_wait_outs")
    w_mo = w_mo.reshape(1, hw, d)
    w_oo = w_oo.reshape(1, d, d)
    ym = _mm_nn(o_bf, w_mo, "mm_y_mla", out_dtype=BF)
    mix = _gate_fwd(z_g, b_gate, yc, ym, d)
    h1 = _mm_nn(mix, w_oo, "mm_h1", add=xs)
    u2 = _rms_fwd(h1, ln2_g, d, 0, "rms2_fwd")
    w_up, = landed(["w_ffn_up"], u2, "gather_wait_ffn_up")
    a_pre = _mm_nn(u2, w_up, "mm_ffn_up")
    f = _ffn_act_fwd(a_pre, fcw8, ffn_conv_b, dff)
    w_dn, = landed(["w_ffn_down"], f, "gather_wait_ffn_down")
    w_dn = w_dn.reshape(1, dff, d)
    dy, dy_bf, loss_part = _mm_nn_loss(f, w_dn, h1, tgt, "mm_ffn_down_loss")

    g_dn = _mm_tn(f, dy_bf, 1, "mm_g_ffn_down").reshape(N_DEV, dff // N_DEV, d)
    rs_dn = _exchange_start([g_dn], False, "reduce_ffn_down_start")
    d_f = _mm_nt(dy_bf, w_dn, "mm_d_f", dep=rs_dn.token)
    d_xg, d_xu, dfw_g, dfw_u = _ffn_act_bwd(a_pre, d_f, fcw8, ffn_conv_b, dff)
    half = N_DEV // 2
    g_up = _mm_tn(u2, d_xg, half, "mm_g_ffn_up_gate", into=lax.empty((N_DEV, d, 2 * dff // N_DEV), BF))
    g_up = _mm_tn(u2, d_xu, half, "mm_g_ffn_up_up", into=g_up, blk0=half)
    rs_up = _exchange_start([g_up], False, "reduce_ffn_up_start")
    d_u2 = _mm_nt([d_xg, d_xu], w_up, "mm_d_u2", out_dtype=BF, dep=rs_up.token)
    d_h1, d_h1_bf, dg_ln2 = _rms_bwd(h1, d_u2, ln2_g, d, 0, "rms2_bwd", extra=dy, also_bf16=True)
    g_oo = _mm_tn(mix, d_h1_bf, 1, "mm_g_w_o").reshape(N_DEV, d // N_DEV, d)
    d_mix = _mm_nt(d_h1_bf, w_oo, "mm_d_mix", out_dtype=BF)
    d_zga, d_zgb, d_yc, d_ym, dba, dbb = _gate_bwd(d_mix, z_g, b_gate, yc, ym, d)
    g_co = _mm_tn(p, d_yc, N_DEV, "mm_g_conv_out")
    g_mo = _mm_tn(o_bf, d_ym, 1, "mm_g_mla_out").reshape(N_DEV, hw // N_DEV, d)
    rs_mix = _exchange_start([g_oo, g_co, g_mo], False, "reduce_mixers_start")
    d_p = _mm_nt(d_yc, w_co, "mm_d_p", dep=rs_mix.token)
    d_o = _mm_nt(d_ym, w_mo, "mm_d_o", out_dtype=BF)
    d_zb, d_zc, d_zv, dcw = _conv_mix_bwd(z_a, d_p, cw8, conv)
    dq_att, dk_att, dv = _attn_bwd(q_att, k_att, v_bf, o, lse, d_o, heads, dep=rs_mix.token)
    d_q_raw, d_kv_raw, d_kr, dgains = _head_bwd(q_raw, kv_raw, z_a, kr_blk, cos, sin, gains, dq_att, dk_att, dv, heads)
    g_q2 = _mm_tn(qn, d_q_raw, 1, "mm_g_q")[0]
    g_qb = _block(jnp.concatenate([g_q2[:, :hw].reshape(ql, heads, NOPE),
                                   _unlay(g_q2[:, hw:].reshape(ql, heads, LANE))], axis=2).reshape(ql, heads * HEAD_QK), N_DEV)
    g_kv = _mm_tn(kvn, d_kv_raw, N_DEV, "mm_g_kv")
    rs_qkv = _exchange_start([g_qb, g_kv], False, "reduce_qkv_start")
    d_qn = _mm_nt(d_q_raw, w_q, "mm_d_qn", dep=rs_qkv.token)
    d_kvn = _mm_nt(d_kv_raw, w_kv, "mm_d_kvn")
    d_ql, dg_qa = _rms_bwd(z_a, d_qn, q_a_g, ql, conv3 // ql, "rms_q_bwd", out_dtype=BF)
    d_kvl, dg_kva = _rms_bwd(z_a, d_kvn, kv_a_g, kvl, kv_off // kvl, "rms_kv_bwd", out_dtype=BF)
    d_z_a = jnp.concatenate([d_zb, d_zc, d_zv, d_ql, d_kr.astype(BF), jnp.zeros((s, kv_off - kr_off - LANE), BF),
                             d_kvl], axis=1)
    g_a = _mm_tn(d_z_a, u1, 1, "mm_g_w_a")[0]
    g_ga = _mm_tn(d_zga, u1, 1, "mm_g_w_ga")[0]
    g_gb = _mm_tn(d_zgb, u1, 1, "mm_g_w_gb")[0]
    g_in = jnp.concatenate([g_a[:kr_off], g_a[kv_off:kv_off + kvl], g_a[kr_off:kr_off + HALF],
                            g_a[kr_off + 2 * HALF:kr_off + 3 * HALF], g_ga, g_gb], axis=0).reshape(N_DEV, nin, d)
    rs_in = _exchange_start([g_in], False, "reduce_w_in_start")
    d_u1 = _mm_nn(d_z_a, w_a_t, "mm_d_u1_a", dep=rs_in.token)
    d_u1 = _mm_nn([d_zga, d_zgb], w_g_t, "mm_d_u1_g", add=d_u1)
    grad_x, dg_ln1 = _rms_bwd(xs, d_u1, ln1_g, d, 0, "rms1_bwd", extra=d_h1)

    summed = {}
    summed["w_ffn_down"], = _exchange_wait(rs_dn, [0], grad_x, "reduce_ffn_down_wait")
    summed["w_ffn_up"], = _exchange_wait(rs_up, [0], grad_x, "reduce_ffn_up_wait")
    summed["w_o"], summed["w_conv_out"], summed["w_mla_out"] = _exchange_wait(rs_mix, [0, 1, 2], grad_x, "reduce_mixers_wait")
    summed["w_q_b"], summed["w_kv_b"] = _exchange_wait(rs_qkv, [0, 1], grad_x, "reduce_qkv_wait")
    loc = locals()
    out = {}
    for k in rest:
        out[k] = _adamw(summed[k], big[k], loc["m_" + k][0], loc["v_" + k][0], "adamw_" + k)

    small = dict(ln1_g=dg_ln1[0:1], b_gate=jnp.concatenate([dba[0:1], dbb[0:1]], axis=1), q_a_g=dg_qa[0:1],
                 kv_a_g=dg_kva[0:1],
                 q_norm_g=jnp.concatenate([dgains[0:1], _unlay(dgains[1:2])], axis=1),
                 k_norm_g=jnp.concatenate([dgains[2:3], _unlay(dgains[3:4])], axis=1),
                 ln2_g=dg_ln2[0:1], ffn_conv_b=jnp.concatenate([dfw_g[3:4], dfw_u[3:4]], axis=1))
    small_names = list(small)
    extra = [dcw[0:3].reshape(1, -1), jnp.concatenate([dfw_g[0:3], dfw_u[0:3]], axis=1).reshape(1, -1),
             loss_part[0:1, 0:1]]
    flat = jnp.concatenate([small[k] for k in small_names] + extra, axis=1)
    n_flat = flat.shape[1]
    rows = -(-n_flat // (SUB * LANE)) * SUB
    flat = jnp.pad(flat, ((0, 0), (0, rows * LANE - n_flat))).reshape(rows, LANE)
    total = _sum_parts(_all_gather([flat], "gather_small", dep=[out[k][0] for k in rest])[0], "sum_small").reshape(1, rows * LANE)
    off = 0
    small_g = {}
    for k in small_names:
        small_g[k] = total[:, off:off + small[k].shape[1]]
        off += small[k].shape[1]
    me = 4 * lax.axis_index("x") + 2 * lax.axis_index("y") + lax.axis_index("c")
    cwn, fcwn = conv // N_DEV, 2 * dff // N_DEV
    g_cw = lax.dynamic_slice_in_dim(total[:, off:off + 3 * conv].reshape(3, conv), me * cwn, cwn, axis=1)
    off += 3 * conv
    g_fcw = lax.dynamic_slice_in_dim(total[:, off:off + 6 * dff].reshape(3, 2 * dff), me * fcwn, fcwn, axis=1)
    off += 6 * dff
    loss = total[0, off]

    summed["w_in"], = _exchange_wait(rs_in, [0], total, "reduce_w_in_wait")
    out["w_in"] = [r.T for r in _adamw(summed["w_in"], big["w_in"], m_w_in[0].T, v_w_in[0].T, "adamw_w_in",
                                       by_cols=True)]
    small_w = dict(ln1_g=ln1_g, b_gate=b_gate, q_a_g=q_a_g, kv_a_g=kv_a_g, q_norm_g=q_norm_g, k_norm_g=k_norm_g,
                   ln2_g=ln2_g, ffn_conv_b=ffn_conv_b, conv_w=conv_w[0].reshape(1, -1),
                   ffn_conv_w=ffn_conv_w[0].reshape(1, -1))
    small_g["conv_w"] = g_cw.reshape(1, -1)
    small_g["ffn_conv_w"] = g_fcw.reshape(1, -1)
    packed_names = list(small_w)

    def pack(get):
        vflat = jnp.concatenate([get(k).reshape(1, -1) for k in packed_names], axis=1)
        nr = -(-vflat.shape[1] // (SUB * LANE)) * SUB
        return jnp.pad(vflat, ((0, 0), (0, nr * LANE - vflat.shape[1])), constant_values=1.0).reshape(nr, LANE)

    res = _adamw(pack(lambda k: small_g[k])[None], pack(lambda k: small_w[k]), pack(lambda k: loc["m_" + k]),
                 pack(lambda k: loc["v_" + k]), "adamw_small")
    res = [r.reshape(1, -1) for r in res]
    off = 0
    for k in packed_names:
        shape = loc[k].shape
        size = small_w[k].shape[1]
        out[k] = [r[:, off:off + size].reshape(shape) for r in res]
        off += size
    for k in names:
        out[k] = [r[None] for r in out[k]]

    order = ["ln1_g", "w_in", "b_gate", "conv_w", "w_conv_out", "q_a_g", "w_q_b", "kv_a_g", "w_kv_b", "q_norm_g",
             "k_norm_g", "w_mla_out", "w_o", "ln2_g", "w_ffn_up", "ffn_conv_w", "ffn_conv_b", "w_ffn_down"]
    return (loss, grad_x[None], *[out[k][0] for k in order], *[out[k][1] for k in order],
            *[out[k][2] for k in order], *[out[k][3] for k in order])
```

```python
import functools

import jax
import jax.numpy as jnp
from jax import lax
from jax.experimental import pallas as pl
from jax.experimental.pallas import tpu as pltpu

BF = jnp.bfloat16
F32 = jnp.float32
MESH = pl.DeviceIdType.MESH
N_DEV = 8

NOPE = 128
ROPE = 64
HALF = ROPE // 2
HEAD_QK = NOPE + ROPE
HEAD_V = 128
LANE = 128
SUB = 8
NORM_EPS = 1e-6
NEG_INF = -1e30
ROPE_THETA = 10000.0
ADAM_LR = 0.001
ADAM_B1 = 0.9
ADAM_B2 = 0.999
ADAM_EPS = 1e-08
ADAM_WD = 0.01
ADAM_STEP = 10

VMEM_LIMIT = 52 * 1024 * 1024
MM_TM, MM_TN, MM_TK, MM_TS = 1024, 1536, 2048, 1024
ROW_TILE, ROW_TILE_BWD = 512, 256
HEAD_ROW_TILE, HEAD_ROW_TILE_BWD = 256, 128
COL_TILE = 512
ATTN_TILE = 1024
ATTN_TILE_FWD = 1024
ANY = pl.BlockSpec(memory_space=pl.ANY)


def _pick(n, target, mult):
    t = (min(n, target) // mult) * mult
    while t > 0:
        if n % t == 0:
            return t
        t -= mult
    raise ValueError(f"no tile for {n} (target {target}, multiple {mult})")


def _cp(*sem):
    return pltpu.CompilerParams(dimension_semantics=sem, vmem_limit_bytes=VMEM_LIMIT)


def _accumulate(kk, nk, acc, part, finish):
    if nk == 1:
        finish(part())
        return

    @pl.when(kk == 0)
    def _():
        acc[...] = part()

    @pl.when((kk > 0) & (kk < nk - 1))
    def _():
        acc[...] += part()

    @pl.when(kk == nk - 1)
    def _():
        finish(acc[...] + part())


def _mm_call(body, name, grid, in_specs, args, out_spec, out_shape, acc_shape, nk, dep):
    if dep is not None:
        in_specs = in_specs + [ANY]
        args = args + [dep]
    return pl.pallas_call(
        body, name=name, grid=grid, in_specs=in_specs, out_specs=out_spec, out_shape=out_shape,
        scratch_shapes=[pltpu.VMEM(acc_shape, F32)] if nk > 1 else [],
        compiler_params=_cp("parallel", "parallel", "arbitrary"),
    )(*args)


def _mm_nn_loss(a, b3, add, target, name):
    m, k = a.shape
    _, k2, n = b3.shape
    assert k == k2 and b3.shape[0] == 1
    tm = _pick(m, MM_TM, 16)
    tn = _pick(n, MM_TN, LANE)
    tk = _pick(k, MM_TK, LANE)
    nk = k // tk

    def body(a_ref, b_ref, c_ref, t_ref, dy_ref, dyb_ref, l_ref, acc):
        kk = pl.program_id(2)

        @pl.when((pl.program_id(0) == 0) & (pl.program_id(1) == 0) & (kk == 0))
        def _():
            l_ref[...] = jnp.zeros_like(l_ref)

        def part():
            return jnp.dot(a_ref[...].astype(BF), b_ref[0].astype(BF), preferred_element_type=F32)

        def finish(r):
            e = r + c_ref[...] - t_ref[...]
            dy_ref[...] = e / n
            dyb_ref[...] = (e / n).astype(BF)
            l_ref[...] += 0.5 * jnp.sum(jnp.sum(e * e, axis=-1, keepdims=True), axis=0, keepdims=True) / n

        _accumulate(kk, nk, acc, part, finish)

    tile = pl.BlockSpec((tm, tn), lambda i, j, kk: (i, j))
    return pl.pallas_call(
        body, name=name, grid=(m // tm, n // tn, nk),
        in_specs=[pl.BlockSpec((tm, tk), lambda i, j, kk: (i, kk)),
                  pl.BlockSpec((1, tk, tn), lambda i, j, kk: (0, kk, j)), tile, tile],
        out_specs=[tile, tile, pl.BlockSpec((SUB, LANE), lambda i, j, kk: (0, 0))],
        out_shape=[jax.ShapeDtypeStruct((m, n), F32), jax.ShapeDtypeStruct((m, n), BF),
                   jax.ShapeDtypeStruct((SUB, LANE), F32)],
        scratch_shapes=[pltpu.VMEM((tm, tn), F32)],
        compiler_params=_cp("arbitrary", "arbitrary", "arbitrary"),
    )(a, b3, add, target)


def _mm_nn(a, b3, name, add=None, out_dtype=F32, blk0=0, nblk=None, dep=None):
    pair = isinstance(a, (list, tuple))
    a_list = list(a) if pair else [a]
    m, ka = a_list[0].shape
    k = ka * len(a_list)
    nb_all, k2, nbw = b3.shape
    assert k == k2
    nblk = nb_all - blk0 if nblk is None else nblk
    n = nblk * nbw
    tm = _pick(m, MM_TM if k > MM_TM else 2 * MM_TM, 16)
    tn = _pick(nbw, MM_TN, LANE)
    tk = _pick(ka, MM_TK, LANE)
    per = nbw // tn
    nk = k // tk
    nka = ka // tk
    na_ops = len(a_list)

    def body(*refs):
        a_refs, b_ref = refs[:na_ops], refs[na_ops]
        c_ref = refs[na_ops + 1] if add is not None else None
        o_ref = refs[na_ops + 1 + (add is not None) + (dep is not None)]
        acc = refs[-1]
        kk = pl.program_id(2)

        def part():
            av = a_refs[0][...] if not pair else jnp.where(kk < nka, a_refs[0][...], a_refs[1][...])
            return jnp.dot(av.astype(BF), b_ref[...].astype(BF), preferred_element_type=F32)

        def finish(r):
            if add is not None:
                r = r + c_ref[...]
            o_ref[...] = r.astype(out_dtype)

        _accumulate(kk, nk, acc, part, finish)

    if pair:
        in_specs = [pl.BlockSpec((tm, tk), lambda i, j, kk: (i, jnp.minimum(kk, nka - 1))),
                    pl.BlockSpec((tm, tk), lambda i, j, kk: (i, jnp.maximum(kk - nka, 0)))]
    else:
        in_specs = [pl.BlockSpec((tm, tk), lambda i, j, kk: (i, kk))]
    in_specs.append(pl.BlockSpec((None, tk, tn), lambda i, j, kk: (blk0 + j // per, kk, j % per)))
    args = a_list + [b3]
    if add is not None:
        in_specs.append(pl.BlockSpec((tm, tn), lambda i, j, kk: (i, j)))
        args.append(add)
    return _mm_call(body, name, (m // tm, n // tn, nk), in_specs, args,
                    pl.BlockSpec((tm, tn), lambda i, j, kk: (i, j)), jax.ShapeDtypeStruct((m, n), out_dtype),
                    (tm, tn), nk, dep)


def _mm_nt(a, b3, name, add=None, out_dtype=F32, blk0=0, nblk=None, dep=None):
    pair = isinstance(a, (list, tuple))
    a_list = list(a) if pair else [a]
    m, na = a_list[0].shape
    n = na * len(a_list)
    nb_all, k, nbw = b3.shape
    nblk = nb_all - blk0 if nblk is None else nblk
    assert n == nblk * nbw and na % nbw == 0
    tm = _pick(m, 2 * MM_TM if k <= MM_TM and n <= MM_TK else MM_TM, 16)
    tn = _pick(k, MM_TN, LANE)
    tk = _pick(nbw, MM_TK, LANE)
    per = nbw // tk
    nk = n // tk
    nka = na // tk
    na_ops = len(a_list)

    def body(*refs):
        a_refs, b_ref = refs[:na_ops], refs[na_ops]
        c_ref = refs[na_ops + 1] if add is not None else None
        o_ref = refs[na_ops + 1 + (add is not None) + (dep is not None)]
        acc = refs[-1]
        kk = pl.program_id(2)

        def part():
            av = a_refs[0][...] if not pair else jnp.where(kk < nka, a_refs[0][...], a_refs[1][...])
            return lax.dot_general(av.astype(BF), b_ref[...].astype(BF),
                                   (((1,), (1,)), ((), ())), preferred_element_type=F32)

        def finish(r):
            if add is not None:
                r = r + c_ref[...]
            o_ref[...] = r.astype(out_dtype)

        _accumulate(kk, nk, acc, part, finish)

    if pair:
        in_specs = [pl.BlockSpec((tm, tk), lambda i, j, kk: (i, jnp.minimum(kk, nka - 1))),
                    pl.BlockSpec((tm, tk), lambda i, j, kk: (i, jnp.maximum(kk - nka, 0)))]
    else:
        in_specs = [pl.BlockSpec((tm, tk), lambda i, j, kk: (i, kk))]
    in_specs.append(pl.BlockSpec((None, tn, tk), lambda i, j, kk: (blk0 + kk // per, j, kk % per)))
    args = a_list + [b3]
    if add is not None:
        in_specs.append(pl.BlockSpec((tm, tn), lambda i, j, kk: (i, j)))
        args.append(add)
    return _mm_call(body, name, (m // tm, k // tn, nk), in_specs, args,
                    pl.BlockSpec((tm, tn), lambda i, j, kk: (i, j)), jax.ShapeDtypeStruct((m, k), out_dtype),
                    (tm, tn), nk, dep)


def _mm_tn(a, b, nblk, name, out_dtype=BF, dep=None, into=None, blk0=0):
    s, m = a.shape
    s2, n = b.shape
    assert s == s2 and n % nblk == 0 and (dep is None or into is None)
    nbw = n // nblk
    tm = _pick(m, MM_TN, LANE)
    tn = _pick(nbw, MM_TN, LANE)
    ts = _pick(s, MM_TS, LANE)
    per = nbw // tn
    ns = s // ts

    def body(*refs):
        a_ref, b_ref = refs[:2]
        o_ref = refs[2 + (dep is not None or into is not None)]
        acc = refs[-1]

        def part():
            return lax.dot_general(a_ref[...].astype(BF), b_ref[...].astype(BF),
                                   (((0,), (0,)), ((), ())), preferred_element_type=F32)

        def finish(r):
            o_ref[...] = r.astype(out_dtype)

        _accumulate(pl.program_id(2), ns, acc, part, finish)

    in_specs = [pl.BlockSpec((ts, tm), lambda i, j, ss: (ss, i)),
                pl.BlockSpec((ts, tn), lambda i, j, ss: (ss, j))]
    out_spec = pl.BlockSpec((None, tm, tn), lambda i, j, ss: (blk0 + j // per, i, j % per))
    if into is None:
        return _mm_call(body, name, (m // tm, n // tn, ns), in_specs, [a, b], out_spec,
                        jax.ShapeDtypeStruct((nblk, m, nbw), out_dtype), (tm, tn), ns, dep)
    assert into.shape[1:] == (m, nbw) and into.dtype == out_dtype
    return pl.pallas_call(
        body, name=name, grid=(m // tm, n // tn, ns), in_specs=in_specs + [ANY], out_specs=out_spec,
        out_shape=jax.ShapeDtypeStruct(into.shape, out_dtype), input_output_aliases={2: 0},
        scratch_shapes=[pltpu.VMEM((tm, tn), F32)] if ns > 1 else [],
        compiler_params=_cp("parallel", "parallel", "arbitrary"),
    )(a, b, into)


def _rows8(rows, width):
    idx = lax.broadcasted_iota(jnp.int32, (SUB, width), 0)
    out = jnp.zeros((SUB, width), F32)
    for r, v in enumerate(rows):
        out = jnp.where(idx == r, v, out)
    return out


def _rms_fwd(x, g, width, col_blk, name):
    s = x.shape[0]
    tr = _pick(s, ROW_TILE, 16)

    def body(x_ref, g_ref, u_ref):
        xv = x_ref[...]
        r = lax.rsqrt(jnp.mean(xv * xv, axis=-1, keepdims=True) + NORM_EPS)
        u_ref[...] = ((xv * r) * g_ref[...]).astype(BF)

    return pl.pallas_call(
        body, name=name, grid=(s // tr,),
        in_specs=[pl.BlockSpec((tr, width), lambda i: (i, col_blk)),
                  pl.BlockSpec((1, width), lambda i: (0, 0))],
        out_specs=pl.BlockSpec((tr, width), lambda i: (i, 0)),
        out_shape=jax.ShapeDtypeStruct((s, width), BF),
        compiler_params=_cp("parallel"),
    )(x, g)


def _rms_bwd(x, du, g, width, col_blk, name, extra=None, out_dtype=F32, also_bf16=False):
    s = x.shape[0]
    tr = _pick(s, ROW_TILE_BWD, 16)

    def body(*refs):
        x_ref, du_ref, g_ref = refs[:3]
        e_ref = refs[3] if extra is not None else None
        dx_ref = refs[3 + (extra is not None)]
        dxb_ref = refs[4 + (extra is not None)] if also_bf16 else None
        dg_ref = refs[-1]
        i = pl.program_id(0)
        xv = x_ref[...]
        duv = du_ref[...].astype(F32)
        r = lax.rsqrt(jnp.mean(xv * xv, axis=-1, keepdims=True) + NORM_EPS)
        nv = xv * r
        dn = duv * g_ref[...]
        dx = r * (dn - nv * jnp.mean(dn * nv, axis=-1, keepdims=True))
        if extra is not None:
            dx = dx + e_ref[...]
        dx_ref[...] = dx.astype(out_dtype)
        if also_bf16:
            dxb_ref[...] = dx.astype(BF)

        @pl.when(i == 0)
        def _():
            dg_ref[...] = jnp.zeros_like(dg_ref)

        dg_ref[...] += _rows8([jnp.sum(duv * nv, axis=0, keepdims=True)], width)

    in_specs = [pl.BlockSpec((tr, width), lambda i: (i, col_blk)),
                pl.BlockSpec((tr, width), lambda i: (i, 0)),
                pl.BlockSpec((1, width), lambda i: (0, 0))]
    args = [x, du, g]
    if extra is not None:
        in_specs.append(pl.BlockSpec((tr, width), lambda i: (i, 0)))
        args.append(extra)
    return pl.pallas_call(
        body, name=name, grid=(s // tr,),
        in_specs=in_specs,
        out_specs=[pl.BlockSpec((tr, width), lambda i: (i, 0))] * (1 + also_bf16)
        + [pl.BlockSpec((SUB, width), lambda i: (0, 0))],
        out_shape=[jax.ShapeDtypeStruct((s, width), out_dtype)] + [jax.ShapeDtypeStruct((s, width), BF)] * also_bf16
        + [jax.ShapeDtypeStruct((SUB, width), F32)],
        compiler_params=_cp("arbitrary"),
    )(*args)


def _down(cur, prev8, k):
    ext = jnp.concatenate([prev8, cur], axis=0)
    return pltpu.roll(ext, k, axis=0)[SUB:]


def _up(cur, next8, k):
    ext = jnp.concatenate([cur, next8], axis=0)
    return pltpu.roll(ext, ext.shape[0] - k, axis=0)[:cur.shape[0]]


def _lags(cur, prev8):
    return _down(cur, prev8, 1), _down(cur, prev8, 2)


def _conv3(w_ref, cur, prev8, lags=None):
    lag1, lag2 = _lags(cur, prev8) if lags is None else lags
    return w_ref[0:1, :] * lag2 + w_ref[1:2, :] * lag1 + w_ref[2:3, :] * cur


def _conv3_t(w_ref, cur, next8):
    return w_ref[2:3, :] * cur + w_ref[1:2, :] * _up(cur, next8, 1) + w_ref[0:1, :] * _up(cur, next8, 2)


def _spec_cur(tr, tc, c0):
    return pl.BlockSpec((tr, tc), lambda j, i: (i, c0 + j))


def _spec_prev(tr, tc, c0):
    return pl.BlockSpec((SUB, tc), lambda j, i: (jnp.maximum(i * (tr // SUB) - 1, 0), c0 + j))


def _spec_next(tr, tc, c0, s):
    return pl.BlockSpec((SUB, tc), lambda j, i: (jnp.minimum((i + 1) * (tr // SUB), s // SUB - 1), c0 + j))


def _spec_w(tc, c0):
    return pl.BlockSpec((SUB, tc), lambda j, i: (0, c0 + j))


def _pad8(w):
    return jnp.pad(w, ((0, SUB - w.shape[0]), (0, 0)))


def _conv_mix_fwd(z_a, cw8, conv):
    s = z_a.shape[0]
    tr = _pick(s, ROW_TILE, 16)
    tc = _pick(conv, COL_TILE, LANE)
    nc = conv // tc

    def body(zb_ref, zc_ref, zv_ref, zcp_ref, zvp_ref, w_ref, p_ref):
        i = pl.program_id(1)
        cv = zc_ref[...] * zv_ref[...]
        cvp = jnp.where(i > 0, zcp_ref[...] * zvp_ref[...], 0.0)
        p_ref[...] = (zb_ref[...] * _conv3(w_ref, cv, cvp)).astype(BF)

    return pl.pallas_call(
        body, name="conv_mix_fwd", grid=(nc, s // tr),
        in_specs=[_spec_cur(tr, tc, 0), _spec_cur(tr, tc, nc), _spec_cur(tr, tc, 2 * nc),
                  _spec_prev(tr, tc, nc), _spec_prev(tr, tc, 2 * nc), _spec_w(tc, 0)],
        out_specs=_spec_cur(tr, tc, 0),
        out_shape=jax.ShapeDtypeStruct((s, conv), BF),
        compiler_params=_cp("parallel", "parallel"),
    )(z_a, z_a, z_a, z_a, z_a, cw8)


def _conv_mix_bwd(z_a, d_p, cw8, conv):
    s = z_a.shape[0]
    tr = _pick(s, ROW_TILE_BWD, 16)
    tc = _pick(conv, COL_TILE, LANE)
    nc = conv // tc
    nr = s // tr

    def body(zb_ref, zbn_ref, zc_ref, zcp_ref, zv_ref, zvp_ref, dp_ref, dpn_ref, w_ref,
             dzb_ref, dzc_ref, dzv_ref, dw_ref):
        i = pl.program_id(1)
        zc = zc_ref[...]
        zv = zv_ref[...]
        cv = zc * zv
        cvp = jnp.where(i > 0, zcp_ref[...] * zvp_ref[...], 0.0)
        cv1, cv2 = _lags(cv, cvp)
        dpv = dp_ref[...]
        dzb_ref[...] = (dpv * _conv3(w_ref, cv, cvp, (cv1, cv2))).astype(BF)
        dcc = dpv * zb_ref[...]
        dccn = jnp.where(i < nr - 1, dpn_ref[...] * zbn_ref[...], 0.0)
        dcv = _conv3_t(w_ref, dcc, dccn)
        dzc_ref[...] = (dcv * zv).astype(BF)
        dzv_ref[...] = (dcv * zc).astype(BF)

        @pl.when(i == 0)
        def _():
            dw_ref[...] = jnp.zeros_like(dw_ref)

        dw_ref[...] += _rows8([jnp.sum(dcc * cv2, axis=0, keepdims=True),
                               jnp.sum(dcc * cv1, axis=0, keepdims=True),
                               jnp.sum(dcc * cv, axis=0, keepdims=True)], tc)

    out = jax.ShapeDtypeStruct((s, conv), BF)
    return pl.pallas_call(
        body, name="conv_mix_bwd", grid=(nc, nr),
        in_specs=[_spec_cur(tr, tc, 0), _spec_next(tr, tc, 0, s),
                  _spec_cur(tr, tc, nc), _spec_prev(tr, tc, nc),
                  _spec_cur(tr, tc, 2 * nc), _spec_prev(tr, tc, 2 * nc),
                  _spec_cur(tr, tc, 0), _spec_next(tr, tc, 0, s), _spec_w(tc, 0)],
        out_specs=[_spec_cur(tr, tc, 0), _spec_cur(tr, tc, 0), _spec_cur(tr, tc, 0), _spec_w(tc, 0)],
        out_shape=[out, out, out, jax.ShapeDtypeStruct((SUB, conv), F32)],
        compiler_params=_cp("parallel", "arbitrary"),
    )(z_a, z_a, z_a, z_a, z_a, z_a, d_p, d_p, cw8)


def _silu_parts(ag):
    sg = jax.nn.sigmoid(ag)
    return ag * sg, sg


def _ffn_act_fwd(a_pre, cw8, cb, dff):
    s = a_pre.shape[0]
    tr = _pick(s, ROW_TILE, 16)
    tc = _pick(dff, COL_TILE, LANE)
    nc = dff // tc

    def body(xg_ref, xgp_ref, xu_ref, xup_ref, wg_ref, wu_ref, bg_ref, bu_ref, f_ref):
        i = pl.program_id(1)
        xgp = jnp.where(i > 0, xgp_ref[...], 0.0)
        xup = jnp.where(i > 0, xup_ref[...], 0.0)
        ag = _conv3(wg_ref, xg_ref[...], xgp) + bg_ref[...]
        au = _conv3(wu_ref, xu_ref[...], xup) + bu_ref[...]
        f_ref[...] = (_silu_parts(ag)[0] * au).astype(BF)

    return pl.pallas_call(
        body, name="ffn_act_fwd", grid=(nc, s // tr),
        in_specs=[_spec_cur(tr, tc, 0), _spec_prev(tr, tc, 0), _spec_cur(tr, tc, nc), _spec_prev(tr, tc, nc),
                  _spec_w(tc, 0), _spec_w(tc, nc),
                  pl.BlockSpec((1, tc), lambda j, i: (0, j)), pl.BlockSpec((1, tc), lambda j, i: (0, nc + j))],
        out_specs=_spec_cur(tr, tc, 0),
        out_shape=jax.ShapeDtypeStruct((s, dff), BF),
        compiler_params=_cp("parallel", "parallel"),
    )(a_pre, a_pre, a_pre, a_pre, cw8, cw8, cb, cb)


def _ffn_act_bwd(a_pre, d_f, cw8, cb, dff):
    s = a_pre.shape[0]
    tr = _pick(s, ROW_TILE_BWD, 16)
    tc = _pick(dff, COL_TILE, LANE)
    nc = dff // tc
    nr = s // tr

    def body(xg_ref, xgp_ref, xgn_ref, xu_ref, xup_ref, xun_ref, df_ref, dfn_ref,
             wg_ref, wu_ref, bg_ref, bu_ref, dxg_ref, dxu_ref, dwg_ref, dwu_ref):
        i = pl.program_id(1)
        xg = xg_ref[...]
        xu = xu_ref[...]
        xgp = jnp.where(i > 0, xgp_ref[...], 0.0)
        xup = jnp.where(i > 0, xup_ref[...], 0.0)

        def d_act(xg_t, xgp_t, xu_t, xup_t, df_t, lags_g=None, lags_u=None):
            ag = _conv3(wg_ref, xg_t, xgp_t, lags_g) + bg_ref[...]
            au = _conv3(wu_ref, xu_t, xup_t, lags_u) + bu_ref[...]
            sil, sg = _silu_parts(ag)
            return df_t * au * (sg * (1.0 + ag * (1.0 - sg))), df_t * sil

        lags_g = _lags(xg, xgp)
        lags_u = _lags(xu, xup)
        dag, dau = d_act(xg, xgp, xu, xup, df_ref[...], lags_g, lags_u)
        dfn = jnp.where(i < nr - 1, dfn_ref[...], 0.0)
        dagn, daun = d_act(xgn_ref[...], xg[tr - SUB:], xun_ref[...], xu[tr - SUB:], dfn)
        dxg_ref[...] = _conv3_t(wg_ref, dag, dagn).astype(BF)
        dxu_ref[...] = _conv3_t(wu_ref, dau, daun).astype(BF)

        @pl.when(i == 0)
        def _():
            dwg_ref[...] = jnp.zeros_like(dwg_ref)
            dwu_ref[...] = jnp.zeros_like(dwu_ref)

        def wgrad(da, x, lags):
            return _rows8([jnp.sum(da * lags[1], axis=0, keepdims=True),
                           jnp.sum(da * lags[0], axis=0, keepdims=True),
                           jnp.sum(da * x, axis=0, keepdims=True),
                           jnp.sum(da, axis=0, keepdims=True)], tc)

        dwg_ref[...] += wgrad(dag, xg, lags_g)
        dwu_ref[...] += wgrad(dau, xu, lags_u)

    half = jax.ShapeDtypeStruct((s, dff), BF)
    wsh = jax.ShapeDtypeStruct((SUB, dff), F32)
    return pl.pallas_call(
        body, name="ffn_act_bwd", grid=(nc, nr),
        in_specs=[_spec_cur(tr, tc, 0), _spec_prev(tr, tc, 0), _spec_next(tr, tc, 0, s),
                  _spec_cur(tr, tc, nc), _spec_prev(tr, tc, nc), _spec_next(tr, tc, nc, s),
                  _spec_cur(tr, tc, 0), _spec_next(tr, tc, 0, s),
                  _spec_w(tc, 0), _spec_w(tc, nc),
                  pl.BlockSpec((1, tc), lambda j, i: (0, j)), pl.BlockSpec((1, tc), lambda j, i: (0, nc + j))],
        out_specs=[_spec_cur(tr, tc, 0), _spec_cur(tr, tc, 0), _spec_w(tc, 0), _spec_w(tc, 0)],
        out_shape=[half, half, wsh, wsh],
        compiler_params=_cp("parallel", "arbitrary"),
    )(a_pre, a_pre, a_pre, a_pre, a_pre, a_pre, d_f, d_f, cw8, cw8, cb, cb)


def _gate_fwd(z_g, b_gate, yc, ym, d):
    s = z_g.shape[0]
    tr = _pick(s, ROW_TILE, 16)
    tc = _pick(d, COL_TILE, LANE)
    nc = d // tc

    def body(za_ref, zb_ref, ba_ref, bb_ref, yc_ref, ym_ref, o_ref):
        ga = jax.nn.sigmoid(za_ref[...] + ba_ref[...])
        gb = jax.nn.sigmoid(zb_ref[...] + bb_ref[...])
        o_ref[...] = (ga * yc_ref[...] + gb * ym_ref[...]).astype(BF)

    return pl.pallas_call(
        body, name="gate_fwd", grid=(nc, s // tr),
        in_specs=[_spec_cur(tr, tc, 0), _spec_cur(tr, tc, nc),
                  pl.BlockSpec((1, tc), lambda j, i: (0, j)), pl.BlockSpec((1, tc), lambda j, i: (0, nc + j)),
                  _spec_cur(tr, tc, 0), _spec_cur(tr, tc, 0)],
        out_specs=_spec_cur(tr, tc, 0),
        out_shape=jax.ShapeDtypeStruct((s, d), BF),
        compiler_params=_cp("parallel", "parallel"),
    )(z_g, z_g, b_gate, b_gate, yc, ym)


def _gate_bwd(d_mix, z_g, b_gate, yc, ym, d):
    s = z_g.shape[0]
    tr = _pick(s, ROW_TILE, 16)
    tc = _pick(d, COL_TILE, LANE)
    nc = d // tc

    def body(dm_ref, za_ref, zb_ref, ba_ref, bb_ref, yc_ref, ym_ref,
             dza_ref, dzb_ref, dyc_ref, dym_ref, dba_ref, dbb_ref):
        i = pl.program_id(1)
        dm = dm_ref[...].astype(F32)
        ga = jax.nn.sigmoid(za_ref[...] + ba_ref[...])
        gb = jax.nn.sigmoid(zb_ref[...] + bb_ref[...])
        dza = dm * yc_ref[...] * (ga * (1.0 - ga))
        dzb = dm * ym_ref[...] * (gb * (1.0 - gb))
        dza_ref[...] = dza.astype(BF)
        dzb_ref[...] = dzb.astype(BF)
        dyc_ref[...] = (dm * ga).astype(BF)
        dym_ref[...] = (dm * gb).astype(BF)

        @pl.when(i == 0)
        def _():
            dba_ref[...] = jnp.zeros_like(dba_ref)
            dbb_ref[...] = jnp.zeros_like(dbb_ref)

        dba_ref[...] += _rows8([jnp.sum(dza, axis=0, keepdims=True)], tc)
        dbb_ref[...] += _rows8([jnp.sum(dzb, axis=0, keepdims=True)], tc)

    act = jax.ShapeDtypeStruct((s, d), BF)
    bsh = jax.ShapeDtypeStruct((SUB, d), F32)
    return pl.pallas_call(
        body, name="gate_bwd", grid=(nc, s // tr),
        in_specs=[_spec_cur(tr, tc, 0), _spec_cur(tr, tc, 0), _spec_cur(tr, tc, nc),
                  pl.BlockSpec((1, tc), lambda j, i: (0, j)), pl.BlockSpec((1, tc), lambda j, i: (0, nc + j)),
                  _spec_cur(tr, tc, 0), _spec_cur(tr, tc, 0)],
        out_specs=[_spec_cur(tr, tc, 0)] * 4 + [_spec_w(tc, 0)] * 2,
        out_shape=[act, act, act, act, bsh, bsh],
        compiler_params=_cp("parallel", "arbitrary"),
    )(d_mix, z_g, z_g, b_gate, b_gate, yc, ym)


def _lay(v):
    z = jnp.zeros(v.shape[:-1] + (HALF,), v.dtype)
    return jnp.concatenate([v[..., :HALF], z, v[..., HALF:], z], axis=-1)


def _unlay(v):
    return jnp.concatenate([v[..., :HALF], v[..., 2 * HALF:3 * HALF]], axis=-1)


def _lay_rows(v):
    z = jnp.zeros((HALF,) + v.shape[1:], v.dtype)
    return jnp.concatenate([v[:HALF], z, v[HALF:], z], axis=0)


def _rope_tables(positions):
    s = positions.shape[0]
    tr = _pick(s, ROW_TILE, 8)
    inv_freq = ROPE_THETA ** (-jnp.arange(0, ROPE, 2, dtype=F32) / ROPE)
    consts = jnp.stack([_lay(jnp.concatenate([inv_freq, inv_freq])),
                        _lay(jnp.ones((ROPE,), F32)),
                        _lay(jnp.concatenate([-jnp.ones((HALF,), F32), jnp.ones((HALF,), F32)]))])
    consts = _pad8(consts)

    def body(p_ref, c_ref, cos_ref, sin_ref):
        ang = p_ref[...].astype(F32) * c_ref[0:1, :]
        cos_ref[...] = jnp.cos(ang) * c_ref[1:2, :]
        sin_ref[...] = jnp.sin(ang) * c_ref[2:3, :]

    tab = jax.ShapeDtypeStruct((s, LANE), F32)
    return pl.pallas_call(
        body, name="rope_tables", grid=(s // tr,),
        in_specs=[pl.BlockSpec((tr, 1), lambda i: (i, 0)), pl.BlockSpec((SUB, LANE), lambda i: (0, 0))],
        out_specs=[pl.BlockSpec((tr, LANE), lambda i: (i, 0))] * 2,
        out_shape=[tab, tab],
        compiler_params=_cp("parallel"),
    )(positions, consts)


def _lane_sum(p):
    return jnp.sum(p, axis=-1, keepdims=True)


def _rope(t, cos, sin):
    return t * cos + pltpu.roll(t, 2 * HALF, axis=1) * sin


def _rope_t(d, cos, sin):
    return d * cos + pltpu.roll(d * sin, 2 * HALF, axis=1)


def _head_fwd(q_raw, kv_raw, z_a, kr_blk, cos, sin, gains, heads):
    s = q_raw.shape[0]
    tr = _pick(s, HEAD_ROW_TILE, 16)
    hw = heads * LANE

    def body(q_ref, kv_ref, kr_ref, cos_ref, sin_ref, g_ref, qo_ref, ko_ref, vo_ref):
        cosv = cos_ref[...]
        sinv = sin_ref[...]
        krv = kr_ref[...]
        kr_sq = krv * krv
        for h in range(heads):
            lo = h * LANE
            qn = q_ref[:, lo:lo + LANE]
            qr = q_ref[:, hw + lo:hw + lo + LANE]
            r = lax.rsqrt(_lane_sum(qn * qn + qr * qr) / HEAD_QK + NORM_EPS)
            qo_ref[:, 2 * lo:2 * lo + LANE] = ((qn * r) * g_ref[0:1, :]).astype(BF)
            qo_ref[:, 2 * lo + LANE:2 * lo + 2 * LANE] = _rope((qr * r) * g_ref[1:2, :], cosv, sinv).astype(BF)
            kn = kv_ref[:, 2 * lo:2 * lo + LANE]
            r = lax.rsqrt(_lane_sum(kn * kn + kr_sq) / HEAD_QK + NORM_EPS)
            ko_ref[:, 2 * lo:2 * lo + LANE] = ((kn * r) * g_ref[2:3, :]).astype(BF)
            ko_ref[:, 2 * lo + LANE:2 * lo + 2 * LANE] = _rope((krv * r) * g_ref[3:4, :], cosv, sinv).astype(BF)
            vo_ref[:, lo:lo + LANE] = kv_ref[:, 2 * lo + LANE:2 * lo + 2 * LANE].astype(BF)

    row = lambda w: pl.BlockSpec((tr, w), lambda i: (i, 0))
    return pl.pallas_call(
        body, name="head_fwd", grid=(s // tr,),
        in_specs=[row(2 * hw), row(2 * hw), pl.BlockSpec((tr, LANE), lambda i: (i, kr_blk)),
                  row(LANE), row(LANE), pl.BlockSpec((SUB, LANE), lambda i: (0, 0))],
        out_specs=[row(2 * hw), row(2 * hw), row(hw)],
        out_shape=[jax.ShapeDtypeStruct((s, 2 * hw), BF), jax.ShapeDtypeStruct((s, 2 * hw), BF),
                   jax.ShapeDtypeStruct((s, hw), BF)],
        compiler_params=_cp("parallel"),
    )(q_raw, kv_raw, z_a, cos, sin, gains)


def _head_bwd(q_raw, kv_raw, z_a, kr_blk, cos, sin, gains, dq_att, dk_att, dv, heads):
    s = q_raw.shape[0]
    tr = _pick(s, HEAD_ROW_TILE_BWD, 16)
    hw = heads * LANE

    def body(q_ref, kv_ref, kr_ref, cos_ref, sin_ref, g_ref, dq_ref, dk_ref, dv_ref,
             dqr_ref, dkv_ref, dkr_ref, dg_ref):
        i = pl.program_id(0)
        cosv = cos_ref[...]
        sinv = sin_ref[...]
        krv = kr_ref[...]
        kr_sq = krv * krv
        dkr = jnp.zeros((tr, LANE), F32)
        dgs = [jnp.zeros((1, LANE), F32) for _ in range(4)]

        def norm_bwd(xn, xr, sq, dn_out, dr_out, gn, gr):
            r = lax.rsqrt(_lane_sum(sq) / HEAD_QK + NORM_EPS)
            nn = xn * r
            nr = xr * r
            dt = _rope_t(dr_out, cosv, sinv)
            dnn = dn_out * gn
            dnr = dt * gr
            mean = _lane_sum(dnn * nn + dnr * nr) / HEAD_QK
            return (r * (dnn - nn * mean), r * (dnr - nr * mean),
                    jnp.sum(dn_out * nn, axis=0, keepdims=True), jnp.sum(dt * nr, axis=0, keepdims=True))

        for h in range(heads):
            lo = h * LANE
            qn = q_ref[:, lo:lo + LANE]
            qr = q_ref[:, hw + lo:hw + lo + LANE]
            dxn, dxr, g0, g1 = norm_bwd(qn, qr, qn * qn + qr * qr, dq_ref[:, 2 * lo:2 * lo + LANE],
                                        dq_ref[:, 2 * lo + LANE:2 * lo + 2 * LANE], g_ref[0:1, :], g_ref[1:2, :])
            dqr_ref[:, lo:lo + LANE] = dxn.astype(BF)
            dqr_ref[:, hw + lo:hw + lo + LANE] = dxr.astype(BF)
            kn = kv_ref[:, 2 * lo:2 * lo + LANE]
            dxn, dxr, g2, g3 = norm_bwd(kn, krv, kn * kn + kr_sq, dk_ref[:, 2 * lo:2 * lo + LANE],
                                        dk_ref[:, 2 * lo + LANE:2 * lo + 2 * LANE], g_ref[2:3, :], g_ref[3:4, :])
            dkv_ref[:, 2 * lo:2 * lo + LANE] = dxn.astype(BF)
            dkv_ref[:, 2 * lo + LANE:2 * lo + 2 * LANE] = dv_ref[:, lo:lo + LANE].astype(BF)
            dkr = dkr + dxr
            dgs = [a + b for a, b in zip(dgs, (g0, g1, g2, g3))]
        dkr_ref[...] = dkr

        @pl.when(i == 0)
        def _():
            dg_ref[...] = jnp.zeros_like(dg_ref)

        dg_ref[...] += _rows8(dgs, LANE)

    row = lambda w: pl.BlockSpec((tr, w), lambda i: (i, 0))
    return pl.pallas_call(
        body, name="head_bwd", grid=(s // tr,),
        in_specs=[row(2 * hw), row(2 * hw), pl.BlockSpec((tr, LANE), lambda i: (i, kr_blk)),
                  row(LANE), row(LANE), pl.BlockSpec((SUB, LANE), lambda i: (0, 0)),
                  row(2 * hw), row(2 * hw), row(hw)],
        out_specs=[row(2 * hw), row(2 * hw), row(LANE), pl.BlockSpec((SUB, LANE), lambda i: (0, 0))],
        out_shape=[jax.ShapeDtypeStruct((s, 2 * hw), BF), jax.ShapeDtypeStruct((s, 2 * hw), BF),
                   jax.ShapeDtypeStruct((s, LANE), F32), jax.ShapeDtypeStruct((SUB, LANE), F32)],
        compiler_params=_cp("arbitrary"),
    )(q_raw, kv_raw, z_a, cos, sin, gains, dq_att, dk_att, dv)


def _causal_mask(nrows, ncols, row0):
    rows = lax.broadcasted_iota(jnp.int32, (nrows, ncols), 0) + row0
    cols = lax.broadcasted_iota(jnp.int32, (nrows, ncols), 1)
    return cols <= rows


def _causal_steps(nt, q_major):
    pairs = ([(i, j) for i in range(nt) for j in range(i + 1)] if q_major
             else [(i, j) for j in range(nt) for i in range(j, nt)])
    return (jnp.array([p[0] for p in pairs], jnp.int32), jnp.array([p[1] for p in pairs], jnp.int32))


def _attn_fwd(q_att, k_att, v, heads):
    s = q_att.shape[0]
    t = _pick(s, ATTN_TILE_FWD, LANE)
    nt = s // t
    th = t // 2
    scale = HEAD_QK ** -0.5
    qi, kj = _causal_steps(nt, True)

    def body(qi_ref, kj_ref, q_ref, k_ref, v_ref, o_ref, ob_ref, lse_ref, m_s, l_s, acc_s):
        st = pl.program_id(1)
        i = qi_ref[st]
        j = kj_ref[st]

        @pl.when(j == 0)
        def _():
            m_s[...] = jnp.full_like(m_s, NEG_INF)
            l_s[...] = jnp.zeros_like(l_s)
            acc_s[...] = jnp.zeros_like(acc_s)

        def update(rows, ncol, masked):
            sc = lax.dot_general(q_ref[rows, :], k_ref[0:ncol, :], (((1,), (1,)), ((), ())),
                                 preferred_element_type=F32) * scale
            if masked:
                sc = jnp.where(_causal_mask(rows.stop - rows.start, ncol, rows.start), sc, NEG_INF)
            m_prev = m_s[rows, :]
            m_new = jnp.maximum(m_prev, jnp.max(sc, axis=-1, keepdims=True))
            alpha = jnp.exp(m_prev - m_new)
            p = jnp.exp(sc - jnp.tile(m_new, (1, ncol // LANE)))
            l_s[rows, :] = alpha * l_s[rows, :] + jnp.sum(p, axis=-1, keepdims=True)
            acc_s[rows, :] = alpha * acc_s[rows, :] + jnp.dot(p.astype(BF), v_ref[0:ncol, :],
                                                              preferred_element_type=F32)
            m_s[rows, :] = m_new

        @pl.when(j < i)
        def _():
            update(slice(0, t), t, False)

        @pl.when(j == i)
        def _():
            update(slice(0, th), th, True)
            update(slice(th, t), t, True)
            o = acc_s[...] / l_s[...]
            o_ref[...] = o
            ob_ref[...] = o.astype(BF)
            lse_ref[...] = (m_s[...] + jnp.log(l_s[...]))[:, 0:1]

    q_idx = lambda h, st, qi_r, kj_r: (qi_r[st], h)
    kv_idx = lambda h, st, qi_r, kj_r: (kj_r[st], h)
    return pl.pallas_call(
        body, name="attn_fwd",
        grid_spec=pltpu.PrefetchScalarGridSpec(
            num_scalar_prefetch=2, grid=(heads, qi.shape[0]),
            in_specs=[pl.BlockSpec((t, 2 * LANE), q_idx), pl.BlockSpec((t, 2 * LANE), kv_idx),
                      pl.BlockSpec((t, LANE), kv_idx)],
            out_specs=[pl.BlockSpec((t, LANE), q_idx), pl.BlockSpec((t, LANE), q_idx),
                       pl.BlockSpec((None, t, 1), lambda h, st, qi_r, kj_r: (h, qi_r[st], 0))],
            scratch_shapes=[pltpu.VMEM((t, LANE), F32), pltpu.VMEM((t, LANE), F32), pltpu.VMEM((t, LANE), F32)]),
        out_shape=[jax.ShapeDtypeStruct((s, heads * LANE), F32), jax.ShapeDtypeStruct((s, heads * LANE), BF),
                   jax.ShapeDtypeStruct((heads, s, 1), F32)],
        compiler_params=_cp("parallel", "arbitrary"),
    )(qi, kj, q_att, k_att, v)


def _attn_bwd(q_att, k_att, v, o, lse, d_o, heads, dep=None):
    s = q_att.shape[0]
    t = _pick(s, ATTN_TILE, LANE)
    nt = s // t
    th = t // 2
    scale = HEAD_QK ** -0.5
    qi, kj = _causal_steps(nt, False)

    def body(qi_ref, kj_ref, q_ref, k_ref, v_ref, do_ref, o_ref, lse_ref, *rest):
        dq_ref, dk_ref, dv_ref, dk_s, dv_s = rest[-5:]
        st = pl.program_id(1)
        i = qi_ref[st]
        j = kj_ref[st]

        @pl.when(st == 0)
        def _():
            dq_ref[...] = jnp.zeros_like(dq_ref)

        @pl.when(i == j)
        def _():
            dk_s[...] = jnp.zeros_like(dk_s)
            dv_s[...] = jnp.zeros_like(dv_s)

        def update(rows, ncol, masked):
            nrow = rows.stop - rows.start
            q = q_ref[rows, :]
            k = k_ref[0:ncol, :]
            do = do_ref[rows, :]
            sc = lax.dot_general(q, k, (((1,), (1,)), ((), ())), preferred_element_type=F32) * scale
            if masked:
                sc = jnp.where(_causal_mask(nrow, ncol, rows.start), sc, NEG_INF)
            p = jnp.exp(sc - lse_ref[rows, :])
            dp = lax.dot_general(do, v_ref[0:ncol, :], (((1,), (1,)), ((), ())), preferred_element_type=F32)
            delta = jnp.sum(do.astype(F32) * o_ref[rows, :], axis=-1, keepdims=True)
            ds = (p * (dp - delta) * scale).astype(BF)
            dv_s[0:ncol, :] += lax.dot_general(p.astype(BF), do, (((0,), (0,)), ((), ())),
                                               preferred_element_type=F32)
            dk_s[0:ncol, :] += lax.dot_general(ds, q, (((0,), (0,)), ((), ())), preferred_element_type=F32)
            out_rows = pl.ds(pl.multiple_of(i * t + rows.start, nrow), nrow)
            dq_ref[out_rows, :] += jnp.dot(ds, k, preferred_element_type=F32)

        @pl.when(i > j)
        def _():
            update(slice(0, t), t, False)

        @pl.when(i == j)
        def _():
            update(slice(0, th), th, True)
            update(slice(th, t), t, True)

        @pl.when(i == nt - 1)
        def _():
            dk_ref[...] = dk_s[...].astype(BF)
            dv_ref[...] = dv_s[...].astype(BF)

    q_idx = lambda h, st, qi_r, kj_r: (qi_r[st], h)
    kv_idx = lambda h, st, qi_r, kj_r: (kj_r[st], h)
    in_specs = [pl.BlockSpec((t, 2 * LANE), q_idx), pl.BlockSpec((t, 2 * LANE), kv_idx),
                pl.BlockSpec((t, LANE), kv_idx), pl.BlockSpec((t, LANE), q_idx), pl.BlockSpec((t, LANE), q_idx),
                pl.BlockSpec((None, t, 1), lambda h, st, qi_r, kj_r: (h, qi_r[st], 0))]
    args = [q_att, k_att, v, d_o, o, lse]
    if dep is not None:
        in_specs.append(ANY)
        args.append(dep)
    return pl.pallas_call(
        body, name="attn_bwd",
        grid_spec=pltpu.PrefetchScalarGridSpec(
            num_scalar_prefetch=2, grid=(heads, qi.shape[0]),
            in_specs=in_specs,
            out_specs=[pl.BlockSpec((s, 2 * LANE), lambda h, st, qi_r, kj_r: (0, h)),
                       pl.BlockSpec((t, 2 * LANE), kv_idx), pl.BlockSpec((t, LANE), kv_idx)],
            scratch_shapes=[pltpu.VMEM((t, 2 * LANE), F32), pltpu.VMEM((t, LANE), F32)]),
        out_shape=[jax.ShapeDtypeStruct((s, heads * 2 * LANE), F32),
                   jax.ShapeDtypeStruct((s, heads * 2 * LANE), BF),
                   jax.ShapeDtypeStruct((s, heads * LANE), BF)],
        compiler_params=_cp("parallel", "arbitrary"),
    )(qi, kj, *args)


def _sum_parts(parts, name):
    n, r, c = parts.shape
    tr = _pick(r, 512, 8)

    def body(p_ref, o_ref):
        g = p_ref[0].astype(F32)
        for k in range(1, n):
            g = g + p_ref[k].astype(F32)
        o_ref[...] = g

    return pl.pallas_call(
        body, name=name, grid=(r // tr,),
        in_specs=[pl.BlockSpec((n, tr, c), lambda i: (0, i, 0))],
        out_specs=pl.BlockSpec((tr, c), lambda i: (i, 0)),
        out_shape=jax.ShapeDtypeStruct((r, c), F32),
        compiler_params=_cp("parallel"),
    )(parts)


def _adamw(parts, w, m, v, name, by_cols=False):
    n, rp, c = parts.shape
    r = w.shape[0]
    assert by_cols or rp == r
    tr, tc = (r, _pick(c, 256, LANE)) if by_cols else (_pick(r, 256, 16 if r % 16 == 0 else 8), c)

    def body(p_ref, w_ref, m_ref, v_ref, g_ref, d_ref, mo_ref, vo_ref):
        g = p_ref[0].astype(F32)
        for k in range(1, n):
            g = g + p_ref[k].astype(F32)
        g = g[:r] if by_cols else g
        m_new = ADAM_B1 * m_ref[...] + (1.0 - ADAM_B1) * g
        v_new = ADAM_B2 * v_ref[...] + (1.0 - ADAM_B2) * jnp.square(g)
        m_hat = m_new / (1.0 - ADAM_B1 ** ADAM_STEP)
        v_hat = v_new / (1.0 - ADAM_B2 ** ADAM_STEP)
        g_ref[...] = g
        d_ref[...] = -ADAM_LR * (m_hat / (jnp.sqrt(v_hat) + ADAM_EPS) + ADAM_WD * w_ref[...])
        mo_ref[...] = m_new
        vo_ref[...] = v_new

    idx = (lambda i: (0, i)) if by_cols else (lambda i: (i, 0))
    spec = pl.BlockSpec((tr, tc), idx)
    sh = jax.ShapeDtypeStruct((r, c), F32)
    return pl.pallas_call(
        body, name=name, grid=(c // tc if by_cols else r // tr,),
        in_specs=[pl.BlockSpec((n, rp if by_cols else tr, tc), lambda i: (0,) + idx(i)), spec, spec, spec],
        out_specs=[spec] * 4, out_shape=[sh] * 4,
        compiler_params=_cp("parallel"),
    )(parts, w, m, v)


def _place():
    x, y, c = lax.axis_index("x"), lax.axis_index("y"), lax.axis_index("c")
    chips = [(1 - x, y), (x, 1 - y), (1 - x, 1 - y)]
    return x, y, c, chips


def _all_gather(shards, name, dep=None):
    n = len(shards)
    deps = [] if dep is None else list(dep)

    def body(*refs):
        ins, outs = refs[:n], refs[n + len(deps):2 * n + len(deps)]
        send_sems, recv_sems, local_sems = refs[2 * n + len(deps):]
        x, y, c, chips = _place()
        me, sibling = (x, y, c), (x, y, 1 - c)

        def slot(w, p):
            return outs[w].at[4 * p[0] + 2 * p[1] + p[2]]

        def copy(w, k, block, to, src=None):
            return pltpu.make_async_remote_copy(
                src_ref=slot(w, block) if src is None else src, dst_ref=slot(w, block),
                send_sem=send_sems.at[w, k], recv_sem=recv_sems.at[w, k], device_id=to, device_id_type=MESH)

        first = []
        for w in range(n):
            first += [copy(w, 1 + j, me, (*chip, c), src=ins[w]) for j, chip in enumerate(chips)]
            first.append(copy(w, 0, me, sibling, src=ins[w]))
        for cp in first:
            cp.start()
        mine = [pltpu.make_async_copy(ins[w], slot(w, me), local_sems.at[w]) for w in range(n)]
        for cp in mine:
            cp.start()
        passed = []
        for w in range(n):
            for j, chip in enumerate(chips):
                copy(w, 1 + j, (*chip, c), me).wait_recv()
                cp = copy(w, 4 + j, (*chip, c), sibling)
                cp.start()
                passed.append(cp)
        for w in range(n):
            copy(w, 0, sibling, me).wait_recv()
            for j, chip in enumerate(chips):
                copy(w, 4 + j, (*chip, 1 - c), me).wait_recv()
        for cp in first + passed:
            cp.wait_send()
        for cp in mine:
            cp.wait()

    return pl.pallas_call(
        body, name=name,
        in_specs=[ANY] * (n + len(deps)), out_specs=[ANY] * n,
        out_shape=[jax.ShapeDtypeStruct((N_DEV,) + a.shape, a.dtype) for a in shards],
        scratch_shapes=[pltpu.SemaphoreType.DMA((n, 7)), pltpu.SemaphoreType.DMA((n, 7)),
                        pltpu.SemaphoreType.DMA((n,))],
    )(*shards, *deps)


HBM = pl.BlockSpec(memory_space=pltpu.HBM)
SEM = pl.BlockSpec(memory_space=pltpu.SEMAPHORE)
EFFECT = pltpu.SideEffectType.DATAFLOW_SIDE_EFFECTING
PEERS = [(dx, dy, dc) for dx in (1, 0) for dy in (1, 0) for dc in (0, 1) if (dx, dy, dc) != (0, 0, 0)]


def _peer(x, y, c, flip):
    dx, dy, dc = flip
    return (1 - x if dx else x, 1 - y if dy else y, 1 - c if dc else c)


def _exchange_copies(srcs, lands, send, recv, loc, gather):
    x, y, c, _ = _place()
    me = 4 * x + 2 * y + c
    remote, local = [], []
    for w in range(len(srcs)):
        for k, flip in enumerate(PEERS):
            px, py, pc = _peer(x, y, c, flip)
            src = srcs[w] if gather else srcs[w].at[4 * px + 2 * py + pc]
            remote.append(pltpu.make_async_remote_copy(
                src_ref=src, dst_ref=lands[w].at[me], send_sem=send[w].at[k], recv_sem=recv[w].at[k],
                device_id=(px, py, pc), device_id_type=MESH))
        local.append(pltpu.make_async_copy(srcs[w] if gather else srcs[w].at[me], lands[w].at[me], loc[w]))
    return remote, local


class _Exchange:
    def __init__(self, srcs, lands, send, recv, loc, token, gather):
        self.srcs, self.lands, self.send, self.recv, self.loc = srcs, lands, send, recv, loc
        self.token, self.gather = token, gather


def _exchange_start(srcs, gather, name, dep=None):
    n = len(srcs)
    deps = [] if dep is None else [dep]
    land_shapes = [((N_DEV,) + a.shape) if gather else a.shape for a in srcs]
    lands = [pltpu.with_memory_space_constraint(lax.empty(sh, a.dtype), pltpu.HBM) for sh, a in zip(land_shapes, srcs)]
    srcs = [pltpu.with_memory_space_constraint(a, pltpu.HBM) for a in srcs]

    def body(*refs):
        src_refs, land_refs = refs[:n], refs[n:2 * n]
        outs = refs[2 * n + len(deps):]
        send, recv, loc = outs[:n], outs[n:2 * n], outs[2 * n:3 * n]
        token = outs[-1]
        remote, local = _exchange_copies(src_refs, land_refs, send, recv, loc, gather)
        for cp in remote + local:
            cp.start()
        token[...] = jnp.zeros_like(token)

    out_shape = ([pltpu.SemaphoreType.DMA((len(PEERS),))] * (2 * n) + [pltpu.SemaphoreType.DMA(())] * n
                 + [pltpu.HBM(a.shape, a.dtype) for a in srcs] + [pltpu.HBM(a.shape, a.dtype) for a in lands]
                 + [jax.ShapeDtypeStruct((SUB, LANE), F32)])
    res = pl.pallas_call(
        body, name=name, out_shape=out_shape,
        in_specs=[HBM] * (2 * n) + [ANY] * len(deps),
        out_specs=[SEM] * (3 * n) + [HBM] * (2 * n) + [pl.BlockSpec(memory_space=pltpu.VMEM)],
        input_output_aliases={i: 3 * n + i for i in range(2 * n)},
        compiler_params=pltpu.CompilerParams(has_side_effects=EFFECT),
    )(*srcs, *lands, *deps)
    return _Exchange(res[3 * n:4 * n], res[4 * n:5 * n], res[:n], res[n:2 * n], res[2 * n:3 * n], res[-1], gather)


def _exchange_wait(ex, idxs, after, name):
    n = len(idxs)
    srcs = [ex.srcs[i] for i in idxs]
    lands = [ex.lands[i] for i in idxs]
    sems = [ex.send[i] for i in idxs] + [ex.recv[i] for i in idxs] + [ex.loc[i] for i in idxs]
    gather = ex.gather

    def body(*refs):
        src_refs, land_refs = refs[:n], refs[n:2 * n]
        send, recv, loc = refs[2 * n:3 * n], refs[3 * n:4 * n], refs[4 * n:5 * n]
        remote, local = _exchange_copies(src_refs, land_refs, send, recv, loc, gather)
        for cp in remote:
            cp.wait_send()
            cp.wait_recv()
        for cp in local:
            cp.wait()

    res = pl.pallas_call(
        body, name=name,
        out_shape=[pltpu.HBM(a.shape, a.dtype) for a in srcs] + [pltpu.HBM(a.shape, a.dtype) for a in lands],
        in_specs=[HBM] * (2 * n) + [SEM] * (3 * n) + [ANY],
        out_specs=[HBM] * (2 * n),
        input_output_aliases={i: i for i in range(2 * n)},
        compiler_params=pltpu.CompilerParams(has_side_effects=EFFECT),
    )(*srcs, *lands, *sems, after)
    return res[n:]


def _after(token, a):
    return a + token[0:1, 0:1].astype(a.dtype)


def _unblock(w3):
    nb, k, nbw = w3.shape
    return w3.transpose(1, 0, 2).reshape(k, nb * nbw)


def _block(w, nb):
    k, n = w.shape
    return w.reshape(k, nb, n // nb).transpose(1, 0, 2)


def kernel(x, positions, ln1_g, w_in, b_gate, conv_w, w_conv_out, q_a_g, w_q_b, kv_a_g, w_kv_b, q_norm_g, k_norm_g, w_mla_out, w_o, ln2_g, w_ffn_up, ffn_conv_w, ffn_conv_b, w_ffn_down, loss_target, m_ln1_g, m_w_in, m_b_gate, m_conv_w, m_w_conv_out, m_q_a_g, m_w_q_b, m_kv_a_g, m_w_kv_b, m_q_norm_g, m_k_norm_g, m_w_mla_out, m_w_o, m_ln2_g, m_w_ffn_up, m_ffn_conv_w, m_ffn_conv_b, m_w_ffn_down, v_ln1_g, v_w_in, v_b_gate, v_conv_w, v_w_conv_out, v_q_a_g, v_w_q_b, v_kv_a_g, v_w_kv_b, v_q_norm_g, v_k_norm_g, v_w_mla_out, v_w_o, v_ln2_g, v_w_ffn_up, v_ffn_conv_w, v_ffn_conv_b, v_w_ffn_down):
    s, d = x.shape[1], x.shape[2]
    conv = conv_w.shape[2] * N_DEV
    ql, kvl = q_a_g.shape[1], kv_a_g.shape[1]
    heads = w_q_b.shape[2] * N_DEV // HEAD_QK
    dff = w_ffn_down.shape[1] * N_DEV
    hw = heads * LANE
    conv3 = 3 * conv
    kr_off = conv3 + ql
    kv_off = -(-(kr_off + LANE) // kvl) * kvl
    wa = kv_off + kvl
    assert conv3 % ql == 0 and kr_off % LANE == 0
    xs = x[0]
    tgt = loss_target[0]
    pos = positions.reshape(s, 1)

    nin = w_in.shape[2]
    big = dict(w_in=w_in[0].T, w_conv_out=w_conv_out[0], w_q_b=w_q_b[0], w_kv_b=w_kv_b[0],
               w_mla_out=w_mla_out[0], w_o=w_o[0], w_ffn_up=w_ffn_up[0], w_ffn_down=w_ffn_down[0])
    names = list(big)
    rest = names[1:]
    first = _all_gather([big["w_in"].astype(BF), _pad8(conv_w[0]), _pad8(ffn_conv_w[0])], "gather_w_in")
    cw8 = _unblock(first[1])
    fcw8 = _unblock(first[2])
    ag = _exchange_start([big[k].astype(BF) for k in rest], True, "gather_rest_start", dep=first[1])

    def landed(keys, after, name):
        return _exchange_wait(ag, [rest.index(k) for k in keys], after, name)

    w_in_t = first[0].reshape(N_DEV * nin, d)
    g_off = kr_off + kvl + ROPE
    w_a_t = jnp.concatenate([w_in_t[:kr_off], _lay_rows(w_in_t[kr_off + kvl:g_off]),
                             jnp.zeros((kv_off - kr_off - LANE, d), BF), w_in_t[kr_off:kr_off + kvl]], axis=0)[None]
    w_g_t = w_in_t[g_off:][None]
    gains = _pad8(jnp.concatenate([q_norm_g[:, :NOPE], _lay(q_norm_g[:, NOPE:]),
                                   k_norm_g[:, :NOPE], _lay(k_norm_g[:, NOPE:])], axis=0))
    kr_blk = kr_off // LANE

    cos, sin = _rope_tables(pos)
    u1 = _rms_fwd(xs, _after(ag.token, ln1_g), d, 0, "rms1_fwd")
    z_a = _mm_nt(u1, w_a_t, "mm_z_a")
    z_g = _mm_nt(u1, w_g_t, "mm_z_g", out_dtype=BF)
    p = _conv_mix_fwd(z_a, cw8, conv)
    w_co, w_qb, w_kv = landed(["w_conv_out", "w_q_b", "w_kv_b"], p, "gather_wait_mixers")
    w_co = _unblock(w_co)[None]
    w_kv = _unblock(w_kv)[None]
    wq_full = _unblock(w_qb).reshape(ql, heads, HEAD_QK)
    w_q = jnp.concatenate([wq_full[:, :, :NOPE].reshape(ql, hw), _lay(wq_full[:, :, NOPE:]).reshape(ql, hw)],
                          axis=1)[None]
    yc = _mm_nn(p, w_co, "mm_y_conv", out_dtype=BF)
    qn = _rms_fwd(z_a, q_a_g, ql, conv3 // ql, "rms_q_fwd")
    kvn = _rms_fwd(z_a, kv_a_g, kvl, kv_off // kvl, "rms_kv_fwd")
    q_raw = _mm_nn(qn, w_q, "mm_q")
    kv_raw = _mm_nn(kvn, w_kv, "mm_kv")
    q_att, k_att, v_bf = _head_fwd(q_raw, kv_raw, z_a, kr_blk, cos, sin, gains, heads)
    o, o_bf, lse = _attn_fwd(q_att, k_att, v_bf, heads)
    w_mo, w_oo = landed(["w_mla_out", "w_o"], lse, "gather_wait_outs")
    w_mo = w_mo.reshape(1, hw, d)
    w_oo = w_oo.reshape(1, d, d)
    ym = _mm_nn(o_bf, w_mo, "mm_y_mla", out_dtype=BF)
    mix = _gate_fwd(z_g, b_gate, yc, ym, d)
    h1 = _mm_nn(mix, w_oo, "mm_h1", add=xs)
    u2 = _rms_fwd(h1, ln2_g, d, 0, "rms2_fwd")
    w_up, = landed(["w_ffn_up"], u2, "gather_wait_ffn_up")
    a_pre = _mm_nn(u2, w_up, "mm_ffn_up")
    f = _ffn_act_fwd(a_pre, fcw8, ffn_conv_b, dff)
    w_dn, = landed(["w_ffn_down"], f, "gather_wait_ffn_down")
    w_dn = w_dn.reshape(1, dff, d)
    dy, dy_bf, loss_part = _mm_nn_loss(f, w_dn, h1, tgt, "mm_ffn_down_loss")

    g_dn = _mm_tn(f, dy_bf, 1, "mm_g_ffn_down").reshape(N_DEV, dff // N_DEV, d)
    rs_dn = _exchange_start([g_dn], False, "reduce_ffn_down_start")
    d_f = _mm_nt(dy_bf, w_dn, "mm_d_f", dep=rs_dn.token)
    d_xg, d_xu, dfw_g, dfw_u = _ffn_act_bwd(a_pre, d_f, fcw8, ffn_conv_b, dff)
    half = N_DEV // 2
    g_up = _mm_tn(u2, d_xg, half, "mm_g_ffn_up_gate", into=lax.empty((N_DEV, d, 2 * dff // N_DEV), BF))
    g_up = _mm_tn(u2, d_xu, half, "mm_g_ffn_up_up", into=g_up, blk0=half)
    rs_up = _exchange_start([g_up], False, "reduce_ffn_up_start")
    d_u2 = _mm_nt([d_xg, d_xu], w_up, "mm_d_u2", out_dtype=BF, dep=rs_up.token)
    d_h1, d_h1_bf, dg_ln2 = _rms_bwd(h1, d_u2, ln2_g, d, 0, "rms2_bwd", extra=dy, also_bf16=True)
    g_oo = _mm_tn(mix, d_h1_bf, 1, "mm_g_w_o").reshape(N_DEV, d // N_DEV, d)
    d_mix = _mm_nt(d_h1_bf, w_oo, "mm_d_mix", out_dtype=BF)
    d_zga, d_zgb, d_yc, d_ym, dba, dbb = _gate_bwd(d_mix, z_g, b_gate, yc, ym, d)
    g_co = _block(_mm_tn(p, d_yc, 1, "mm_g_conv_out")[0], N_DEV)
    g_mo = _mm_tn(o_bf, d_ym, 1, "mm_g_mla_out").reshape(N_DEV, hw // N_DEV, d)
    rs_mix = _exchange_start([g_oo, g_co, g_mo], False, "reduce_mixers_start")
    d_p = _mm_nt(d_yc, w_co, "mm_d_p", dep=rs_mix.token)
    d_o = _mm_nt(d_ym, w_mo, "mm_d_o", out_dtype=BF)
    d_zb, d_zc, d_zv, dcw = _conv_mix_bwd(z_a, d_p, cw8, conv)
    dq_att, dk_att, dv = _attn_bwd(q_att, k_att, v_bf, o, lse, d_o, heads, dep=rs_mix.token)
    d_q_raw, d_kv_raw, d_kr, dgains = _head_bwd(q_raw, kv_raw, z_a, kr_blk, cos, sin, gains, dq_att, dk_att, dv, heads)
    g_q2 = _mm_tn(qn, d_q_raw, 1, "mm_g_q")[0]
    g_qb = _block(jnp.concatenate([g_q2[:, :hw].reshape(ql, heads, NOPE),
                                   _unlay(g_q2[:, hw:].reshape(ql, heads, LANE))], axis=2).reshape(ql, heads * HEAD_QK), N_DEV)
    g_kv = _block(_mm_tn(kvn, d_kv_raw, 1, "mm_g_kv")[0], N_DEV)
    rs_qkv = _exchange_start([g_qb, g_kv], False, "reduce_qkv_start")
    d_qn = _mm_nt(d_q_raw, w_q, "mm_d_qn", dep=rs_qkv.token)
    d_kvn = _mm_nt(d_kv_raw, w_kv, "mm_d_kvn")
    d_ql, dg_qa = _rms_bwd(z_a, d_qn, q_a_g, ql, conv3 // ql, "rms_q_bwd", out_dtype=BF)
    d_kvl, dg_kva = _rms_bwd(z_a, d_kvn, kv_a_g, kvl, kv_off // kvl, "rms_kv_bwd", out_dtype=BF)
    d_z_a = jnp.concatenate([d_zb, d_zc, d_zv, d_ql, d_kr.astype(BF), jnp.zeros((s, kv_off - kr_off - LANE), BF),
                             d_kvl], axis=1)
    g_a = _mm_tn(d_z_a, u1, 1, "mm_g_w_a")[0]
    g_ga = _mm_tn(d_zga, u1, 1, "mm_g_w_ga")[0]
    g_gb = _mm_tn(d_zgb, u1, 1, "mm_g_w_gb")[0]
    g_in = jnp.concatenate([g_a[:kr_off], g_a[kv_off:kv_off + kvl], g_a[kr_off:kr_off + HALF],
                            g_a[kr_off + 2 * HALF:kr_off + 3 * HALF], g_ga, g_gb], axis=0).reshape(N_DEV, nin, d)
    rs_in = _exchange_start([g_in], False, "reduce_w_in_start")
    d_u1 = _mm_nn(d_z_a, w_a_t, "mm_d_u1_a", dep=rs_in.token)
    d_u1 = _mm_nn([d_zga, d_zgb], w_g_t, "mm_d_u1_g", add=d_u1)
    grad_x, dg_ln1 = _rms_bwd(xs, d_u1, ln1_g, d, 0, "rms1_bwd", extra=d_h1)

    summed = {}
    summed["w_ffn_down"], = _exchange_wait(rs_dn, [0], grad_x, "reduce_ffn_down_wait")
    summed["w_ffn_up"], = _exchange_wait(rs_up, [0], grad_x, "reduce_ffn_up_wait")
    summed["w_o"], summed["w_conv_out"], summed["w_mla_out"] = _exchange_wait(rs_mix, [0, 1, 2], grad_x, "reduce_mixers_wait")
    summed["w_q_b"], summed["w_kv_b"] = _exchange_wait(rs_qkv, [0, 1], grad_x, "reduce_qkv_wait")
    loc = locals()
    out = {}
    for k in rest:
        out[k] = _adamw(summed[k], big[k], loc["m_" + k][0], loc["v_" + k][0], "adamw_" + k)

    small = dict(ln1_g=dg_ln1[0:1], b_gate=jnp.concatenate([dba[0:1], dbb[0:1]], axis=1), q_a_g=dg_qa[0:1],
                 kv_a_g=dg_kva[0:1],
                 q_norm_g=jnp.concatenate([dgains[0:1], _unlay(dgains[1:2])], axis=1),
                 k_norm_g=jnp.concatenate([dgains[2:3], _unlay(dgains[3:4])], axis=1),
                 ln2_g=dg_ln2[0:1], ffn_conv_b=jnp.concatenate([dfw_g[3:4], dfw_u[3:4]], axis=1))
    small_names = list(small)
    extra = [dcw[0:3].reshape(1, -1), jnp.concatenate([dfw_g[0:3], dfw_u[0:3]], axis=1).reshape(1, -1),
             loss_part[0:1, 0:1]]
    flat = jnp.concatenate([small[k] for k in small_names] + extra, axis=1)
    n_flat = flat.shape[1]
    rows = -(-n_flat // (SUB * LANE)) * SUB
    flat = jnp.pad(flat, ((0, 0), (0, rows * LANE - n_flat))).reshape(rows, LANE)
    total = _sum_parts(_all_gather([flat], "gather_small", dep=[out[k][0] for k in rest])[0], "sum_small").reshape(1, rows * LANE)
    off = 0
    small_g = {}
    for k in small_names:
        small_g[k] = total[:, off:off + small[k].shape[1]]
        off += small[k].shape[1]
    me = 4 * lax.axis_index("x") + 2 * lax.axis_index("y") + lax.axis_index("c")
    cwn, fcwn = conv // N_DEV, 2 * dff // N_DEV
    g_cw = lax.dynamic_slice_in_dim(total[:, off:off + 3 * conv].reshape(3, conv), me * cwn, cwn, axis=1)
    off += 3 * conv
    g_fcw = lax.dynamic_slice_in_dim(total[:, off:off + 6 * dff].reshape(3, 2 * dff), me * fcwn, fcwn, axis=1)
    off += 6 * dff
    loss = total[0, off]

    summed["w_in"], = _exchange_wait(rs_in, [0], total, "reduce_w_in_wait")
    out["w_in"] = [r.T for r in _adamw(summed["w_in"], big["w_in"], m_w_in[0].T, v_w_in[0].T, "adamw_w_in",
                                       by_cols=True)]
    small_w = dict(ln1_g=ln1_g, b_gate=b_gate, q_a_g=q_a_g, kv_a_g=kv_a_g, q_norm_g=q_norm_g, k_norm_g=k_norm_g,
                   ln2_g=ln2_g, ffn_conv_b=ffn_conv_b, conv_w=conv_w[0].reshape(1, -1),
                   ffn_conv_w=ffn_conv_w[0].reshape(1, -1))
    small_g["conv_w"] = g_cw.reshape(1, -1)
    small_g["ffn_conv_w"] = g_fcw.reshape(1, -1)
    packed_names = list(small_w)

    def pack(get):
        vflat = jnp.concatenate([get(k).reshape(1, -1) for k in packed_names], axis=1)
        nr = -(-vflat.shape[1] // (SUB * LANE)) * SUB
        return jnp.pad(vflat, ((0, 0), (0, nr * LANE - vflat.shape[1])), constant_values=1.0).reshape(nr, LANE)

    res = _adamw(pack(lambda k: small_g[k])[None], pack(lambda k: small_w[k]), pack(lambda k: loc["m_" + k]),
                 pack(lambda k: loc["v_" + k]), "adamw_small")
    res = [r.reshape(1, -1) for r in res]
    off = 0
    for k in packed_names:
        shape = loc[k].shape
        size = small_w[k].shape[1]
        out[k] = [r[:, off:off + size].reshape(shape) for r in res]
        off += size
    for k in names:
        out[k] = [r[None] for r in out[k]]

    order = ["ln1_g", "w_in", "b_gate", "conv_w", "w_conv_out", "q_a_g", "w_q_b", "kv_a_g", "w_kv_b", "q_norm_g",
             "k_norm_g", "w_mla_out", "w_o", "ln2_g", "w_ffn_up", "ffn_conv_w", "ffn_conv_b", "w_ffn_down"]
    return (loss, grad_x[None], *[out[k][0] for k in order], *[out[k][1] for k in order],
            *[out[k][2] for k in order], *[out[k][3] for k in order])
```

```python
import functools

import jax
import jax.numpy as jnp
from jax import lax
from jax.experimental import pallas as pl
from jax.experimental.pallas import tpu as pltpu

BF = jnp.bfloat16
F32 = jnp.float32
MESH = pl.DeviceIdType.MESH
N_DEV = 8

NOPE = 128
ROPE = 64
HALF = ROPE // 2
HEAD_QK = NOPE + ROPE
HEAD_V = 128
LANE = 128
SUB = 8
NORM_EPS = 1e-6
NEG_INF = -1e30
ROPE_THETA = 10000.0
ADAM_LR = 0.001
ADAM_B1 = 0.9
ADAM_B2 = 0.999
ADAM_EPS = 1e-08
ADAM_WD = 0.01
ADAM_STEP = 10

VMEM_LIMIT = 52 * 1024 * 1024
MM_TM, MM_TN, MM_TK, MM_TS = 1024, 1536, 2048, 1024
ROW_TILE, ROW_TILE_BWD = 512, 256
HEAD_ROW_TILE, HEAD_ROW_TILE_BWD = 256, 128
COL_TILE = 512
ATTN_TILE = 1024
ATTN_TILE_FWD = 1024
ANY = pl.BlockSpec(memory_space=pl.ANY)


def _pick(n, target, mult):
    t = (min(n, target) // mult) * mult
    while t > 0:
        if n % t == 0:
            return t
        t -= mult
    raise ValueError(f"no tile for {n} (target {target}, multiple {mult})")


def _cp(*sem):
    return pltpu.CompilerParams(dimension_semantics=sem, vmem_limit_bytes=VMEM_LIMIT)


def _accumulate(kk, nk, acc, part, finish):
    if nk == 1:
        finish(part())
        return

    @pl.when(kk == 0)
    def _():
        acc[...] = part()

    @pl.when((kk > 0) & (kk < nk - 1))
    def _():
        acc[...] += part()

    @pl.when(kk == nk - 1)
    def _():
        finish(acc[...] + part())


def _mm_call(body, name, grid, in_specs, args, out_spec, out_shape, acc_shape, nk, dep):
    if dep is not None:
        in_specs = in_specs + [ANY]
        args = args + [dep]
    return pl.pallas_call(
        body, name=name, grid=grid, in_specs=in_specs, out_specs=out_spec, out_shape=out_shape,
        scratch_shapes=[pltpu.VMEM(acc_shape, F32)] if nk > 1 else [],
        compiler_params=_cp("parallel", "parallel", "arbitrary"),
    )(*args)


def _mm_nn_loss(a, b3, add, target, name):
    m, k = a.shape
    _, k2, n = b3.shape
    assert k == k2 and b3.shape[0] == 1
    tm = _pick(m, MM_TM, 16)
    tn = _pick(n, MM_TN, LANE)
    tk = _pick(k, MM_TK, LANE)
    nk = k // tk

    def body(a_ref, b_ref, c_ref, t_ref, dy_ref, dyb_ref, l_ref, acc):
        kk = pl.program_id(2)

        @pl.when((pl.program_id(0) == 0) & (pl.program_id(1) == 0) & (kk == 0))
        def _():
            l_ref[...] = jnp.zeros_like(l_ref)

        def part():
            return jnp.dot(a_ref[...].astype(BF), b_ref[0].astype(BF), preferred_element_type=F32)

        def finish(r):
            e = r + c_ref[...] - t_ref[...]
            dy_ref[...] = e / n
            dyb_ref[...] = (e / n).astype(BF)
            l_ref[...] += 0.5 * jnp.sum(jnp.sum(e * e, axis=-1, keepdims=True), axis=0, keepdims=True) / n

        _accumulate(kk, nk, acc, part, finish)

    tile = pl.BlockSpec((tm, tn), lambda i, j, kk: (i, j))
    return pl.pallas_call(
        body, name=name, grid=(m // tm, n // tn, nk),
        in_specs=[pl.BlockSpec((tm, tk), lambda i, j, kk: (i, kk)),
                  pl.BlockSpec((1, tk, tn), lambda i, j, kk: (0, kk, j)), tile, tile],
        out_specs=[tile, tile, pl.BlockSpec((SUB, LANE), lambda i, j, kk: (0, 0))],
        out_shape=[jax.ShapeDtypeStruct((m, n), F32), jax.ShapeDtypeStruct((m, n), BF),
                   jax.ShapeDtypeStruct((SUB, LANE), F32)],
        scratch_shapes=[pltpu.VMEM((tm, tn), F32)],
        compiler_params=_cp("arbitrary", "arbitrary", "arbitrary"),
    )(a, b3, add, target)


def _mm_nn(a, b3, name, add=None, out_dtype=F32, blk0=0, nblk=None, dep=None):
    pair = isinstance(a, (list, tuple))
    a_list = list(a) if pair else [a]
    m, ka = a_list[0].shape
    k = ka * len(a_list)
    nb_all, k2, nbw = b3.shape
    assert k == k2
    nblk = nb_all - blk0 if nblk is None else nblk
    n = nblk * nbw
    tm = _pick(m, MM_TM if k > MM_TM else 2 * MM_TM, 16)
    tn = _pick(nbw, MM_TN, LANE)
    tk = _pick(ka, MM_TK, LANE)
    per = nbw // tn
    nk = k // tk
    nka = ka // tk
    na_ops = len(a_list)

    def body(*refs):
        a_refs, b_ref = refs[:na_ops], refs[na_ops]
        c_ref = refs[na_ops + 1] if add is not None else None
        o_ref = refs[na_ops + 1 + (add is not None) + (dep is not None)]
        acc = refs[-1]
        kk = pl.program_id(2)

        def part():
            av = a_refs[0][...] if not pair else jnp.where(kk < nka, a_refs[0][...], a_refs[1][...])
            return jnp.dot(av.astype(BF), b_ref[...].astype(BF), preferred_element_type=F32)

        def finish(r):
            if add is not None:
                r = r + c_ref[...]
            o_ref[...] = r.astype(out_dtype)

        _accumulate(kk, nk, acc, part, finish)

    if pair:
        in_specs = [pl.BlockSpec((tm, tk), lambda i, j, kk: (i, jnp.minimum(kk, nka - 1))),
                    pl.BlockSpec((tm, tk), lambda i, j, kk: (i, jnp.maximum(kk - nka, 0)))]
    else:
        in_specs = [pl.BlockSpec((tm, tk), lambda i, j, kk: (i, kk))]
    in_specs.append(pl.BlockSpec((None, tk, tn), lambda i, j, kk: (blk0 + j // per, kk, j % per)))
    args = a_list + [b3]
    if add is not None:
        in_specs.append(pl.BlockSpec((tm, tn), lambda i, j, kk: (i, j)))
        args.append(add)
    return _mm_call(body, name, (m // tm, n // tn, nk), in_specs, args,
                    pl.BlockSpec((tm, tn), lambda i, j, kk: (i, j)), jax.ShapeDtypeStruct((m, n), out_dtype),
                    (tm, tn), nk, dep)


def _mm_nt(a, b3, name, add=None, out_dtype=F32, blk0=0, nblk=None, dep=None):
    pair = isinstance(a, (list, tuple))
    a_list = list(a) if pair else [a]
    m, na = a_list[0].shape
    n = na * len(a_list)
    nb_all, k, nbw = b3.shape
    nblk = nb_all - blk0 if nblk is None else nblk
    assert n == nblk * nbw and na % nbw == 0
    tm = _pick(m, 2 * MM_TM if k <= MM_TM and n <= MM_TK else MM_TM, 16)
    tn = _pick(k, MM_TN, LANE)
    tk = _pick(nbw, MM_TK, LANE)
    per = nbw // tk
    nk = n // tk
    nka = na // tk
    na_ops = len(a_list)

    def body(*refs):
        a_refs, b_ref = refs[:na_ops], refs[na_ops]
        c_ref = refs[na_ops + 1] if add is not None else None
        o_ref = refs[na_ops + 1 + (add is not None) + (dep is not None)]
        acc = refs[-1]
        kk = pl.program_id(2)

        def part():
            av = a_refs[0][...] if not pair else jnp.where(kk < nka, a_refs[0][...], a_refs[1][...])
            return lax.dot_general(av.astype(BF), b_ref[...].astype(BF),
                                   (((1,), (1,)), ((), ())), preferred_element_type=F32)

        def finish(r):
            if add is not None:
                r = r + c_ref[...]
            o_ref[...] = r.astype(out_dtype)

        _accumulate(kk, nk, acc, part, finish)

    if pair:
        in_specs = [pl.BlockSpec((tm, tk), lambda i, j, kk: (i, jnp.minimum(kk, nka - 1))),
                    pl.BlockSpec((tm, tk), lambda i, j, kk: (i, jnp.maximum(kk - nka, 0)))]
    else:
        in_specs = [pl.BlockSpec((tm, tk), lambda i, j, kk: (i, kk))]
    in_specs.append(pl.BlockSpec((None, tn, tk), lambda i, j, kk: (blk0 + kk // per, j, kk % per)))
    args = a_list + [b3]
    if add is not None:
        in_specs.append(pl.BlockSpec((tm, tn), lambda i, j, kk: (i, j)))
        args.append(add)
    return _mm_call(body, name, (m // tm, k // tn, nk), in_specs, args,
                    pl.BlockSpec((tm, tn), lambda i, j, kk: (i, j)), jax.ShapeDtypeStruct((m, k), out_dtype),
                    (tm, tn), nk, dep)


def _mm_tn(a, b, nblk, name, out_dtype=BF, dep=None, into=None, blk0=0):
    s, m = a.shape
    s2, n = b.shape
    assert s == s2 and n % nblk == 0 and (dep is None or into is None)
    nbw = n // nblk
    tm = _pick(m, MM_TN, LANE)
    tn = _pick(nbw, MM_TN, LANE)
    ts = _pick(s, MM_TS, LANE)
    per = nbw // tn
    ns = s // ts

    def body(*refs):
        a_ref, b_ref = refs[:2]
        o_ref = refs[2 + (dep is not None or into is not None)]
        acc = refs[-1]

        def part():
            return lax.dot_general(a_ref[...].astype(BF), b_ref[...].astype(BF),
                                   (((0,), (0,)), ((), ())), preferred_element_type=F32)

        def finish(r):
            o_ref[...] = r.astype(out_dtype)

        _accumulate(pl.program_id(2), ns, acc, part, finish)

    in_specs = [pl.BlockSpec((ts, tm), lambda i, j, ss: (ss, i)),
                pl.BlockSpec((ts, tn), lambda i, j, ss: (ss, j))]
    out_spec = pl.BlockSpec((None, tm, tn), lambda i, j, ss: (blk0 + j // per, i, j % per))
    if into is None:
        return _mm_call(body, name, (m // tm, n // tn, ns), in_specs, [a, b], out_spec,
                        jax.ShapeDtypeStruct((nblk, m, nbw), out_dtype), (tm, tn), ns, dep)
    assert into.shape[1:] == (m, nbw) and into.dtype == out_dtype
    return pl.pallas_call(
        body, name=name, grid=(m // tm, n // tn, ns), in_specs=in_specs + [ANY], out_specs=out_spec,
        out_shape=jax.ShapeDtypeStruct(into.shape, out_dtype), input_output_aliases={2: 0},
        scratch_shapes=[pltpu.VMEM((tm, tn), F32)] if ns > 1 else [],
        compiler_params=_cp("parallel", "parallel", "arbitrary"),
    )(a, b, into)


def _rows8(rows, width):
    idx = lax.broadcasted_iota(jnp.int32, (SUB, width), 0)
    out = jnp.zeros((SUB, width), F32)
    for r, v in enumerate(rows):
        out = jnp.where(idx == r, v, out)
    return out


def _rms_fwd(x, g, width, col_blk, name):
    s = x.shape[0]
    tr = _pick(s, ROW_TILE, 16)

    def body(x_ref, g_ref, u_ref):
        xv = x_ref[...]
        r = lax.rsqrt(jnp.mean(xv * xv, axis=-1, keepdims=True) + NORM_EPS)
        u_ref[...] = ((xv * r) * g_ref[...]).astype(BF)

    return pl.pallas_call(
        body, name=name, grid=(s // tr,),
        in_specs=[pl.BlockSpec((tr, width), lambda i: (i, col_blk)),
                  pl.BlockSpec((1, width), lambda i: (0, 0))],
        out_specs=pl.BlockSpec((tr, width), lambda i: (i, 0)),
        out_shape=jax.ShapeDtypeStruct((s, width), BF),
        compiler_params=_cp("parallel"),
    )(x, g)


def _rms_bwd(x, du, g, width, col_blk, name, extra=None, out_dtype=F32, also_bf16=False):
    s = x.shape[0]
    tr = _pick(s, ROW_TILE_BWD, 16)

    def body(*refs):
        x_ref, du_ref, g_ref = refs[:3]
        e_ref = refs[3] if extra is not None else None
        dx_ref = refs[3 + (extra is not None)]
        dxb_ref = refs[4 + (extra is not None)] if also_bf16 else None
        dg_ref = refs[-1]
        i = pl.program_id(0)
        xv = x_ref[...]
        duv = du_ref[...].astype(F32)
        r = lax.rsqrt(jnp.mean(xv * xv, axis=-1, keepdims=True) + NORM_EPS)
        nv = xv * r
        dn = duv * g_ref[...]
        dx = r * (dn - nv * jnp.mean(dn * nv, axis=-1, keepdims=True))
        if extra is not None:
            dx = dx + e_ref[...]
        dx_ref[...] = dx.astype(out_dtype)
        if also_bf16:
            dxb_ref[...] = dx.astype(BF)

        @pl.when(i == 0)
        def _():
            dg_ref[...] = jnp.zeros_like(dg_ref)

        dg_ref[...] += _rows8([jnp.sum(duv * nv, axis=0, keepdims=True)], width)

    in_specs = [pl.BlockSpec((tr, width), lambda i: (i, col_blk)),
                pl.BlockSpec((tr, width), lambda i: (i, 0)),
                pl.BlockSpec((1, width), lambda i: (0, 0))]
    args = [x, du, g]
    if extra is not None:
        in_specs.append(pl.BlockSpec((tr, width), lambda i: (i, 0)))
        args.append(extra)
    return pl.pallas_call(
        body, name=name, grid=(s // tr,),
        in_specs=in_specs,
        out_specs=[pl.BlockSpec((tr, width), lambda i: (i, 0))] * (1 + also_bf16)
        + [pl.BlockSpec((SUB, width), lambda i: (0, 0))],
        out_shape=[jax.ShapeDtypeStruct((s, width), out_dtype)] + [jax.ShapeDtypeStruct((s, width), BF)] * also_bf16
        + [jax.ShapeDtypeStruct((SUB, width), F32)],
        compiler_params=_cp("arbitrary"),
    )(*args)


def _down(cur, prev8, k):
    ext = jnp.concatenate([prev8, cur], axis=0)
    return pltpu.roll(ext, k, axis=0)[SUB:]


def _up(cur, next8, k):
    ext = jnp.concatenate([cur, next8], axis=0)
    return pltpu.roll(ext, ext.shape[0] - k, axis=0)[:cur.shape[0]]


def _lags(cur, prev8):
    return _down(cur, prev8, 1), _down(cur, prev8, 2)


def _conv3(w_ref, cur, prev8, lags=None):
    lag1, lag2 = _lags(cur, prev8) if lags is None else lags
    return w_ref[0:1, :] * lag2 + w_ref[1:2, :] * lag1 + w_ref[2:3, :] * cur


def _conv3_t(w_ref, cur, next8):
    return w_ref[2:3, :] * cur + w_ref[1:2, :] * _up(cur, next8, 1) + w_ref[0:1, :] * _up(cur, next8, 2)


def _spec_cur(tr, tc, c0):
    return pl.BlockSpec((tr, tc), lambda j, i: (i, c0 + j))


def _spec_prev(tr, tc, c0):
    return pl.BlockSpec((SUB, tc), lambda j, i: (jnp.maximum(i * (tr // SUB) - 1, 0), c0 + j))


def _spec_next(tr, tc, c0, s):
    return pl.BlockSpec((SUB, tc), lambda j, i: (jnp.minimum((i + 1) * (tr // SUB), s // SUB - 1), c0 + j))


def _spec_w(tc, c0):
    return pl.BlockSpec((SUB, tc), lambda j, i: (0, c0 + j))


def _pad8(w):
    return jnp.pad(w, ((0, SUB - w.shape[0]), (0, 0)))


def _conv_mix_fwd(z_a, cw8, conv):
    s = z_a.shape[0]
    tr = _pick(s, ROW_TILE, 16)
    tc = _pick(conv, COL_TILE, LANE)
    nc = conv // tc

    def body(zb_ref, zc_ref, zv_ref, zcp_ref, zvp_ref, w_ref, p_ref):
        i = pl.program_id(1)
        cv = zc_ref[...] * zv_ref[...]
        cvp = jnp.where(i > 0, zcp_ref[...] * zvp_ref[...], 0.0)
        p_ref[...] = (zb_ref[...] * _conv3(w_ref, cv, cvp)).astype(BF)

    return pl.pallas_call(
        body, name="conv_mix_fwd", grid=(nc, s // tr),
        in_specs=[_spec_cur(tr, tc, 0), _spec_cur(tr, tc, nc), _spec_cur(tr, tc, 2 * nc),
                  _spec_prev(tr, tc, nc), _spec_prev(tr, tc, 2 * nc), _spec_w(tc, 0)],
        out_specs=_spec_cur(tr, tc, 0),
        out_shape=jax.ShapeDtypeStruct((s, conv), BF),
        compiler_params=_cp("parallel", "parallel"),
    )(z_a, z_a, z_a, z_a, z_a, cw8)


def _conv_mix_bwd(z_a, d_p, cw8, conv):
    s = z_a.shape[0]
    tr = _pick(s, ROW_TILE_BWD, 16)
    tc = _pick(conv, COL_TILE, LANE)
    nc = conv // tc
    nr = s // tr

    def body(zb_ref, zbn_ref, zc_ref, zcp_ref, zv_ref, zvp_ref, dp_ref, dpn_ref, w_ref,
             dzb_ref, dzc_ref, dzv_ref, dw_ref):
        i = pl.program_id(1)
        zc = zc_ref[...]
        zv = zv_ref[...]
        cv = zc * zv
        cvp = jnp.where(i > 0, zcp_ref[...] * zvp_ref[...], 0.0)
        cv1, cv2 = _lags(cv, cvp)
        dpv = dp_ref[...]
        dzb_ref[...] = (dpv * _conv3(w_ref, cv, cvp, (cv1, cv2))).astype(BF)
        dcc = dpv * zb_ref[...]
        dccn = jnp.where(i < nr - 1, dpn_ref[...] * zbn_ref[...], 0.0)
        dcv = _conv3_t(w_ref, dcc, dccn)
        dzc_ref[...] = (dcv * zv).astype(BF)
        dzv_ref[...] = (dcv * zc).astype(BF)

        @pl.when(i == 0)
        def _():
            dw_ref[...] = jnp.zeros_like(dw_ref)

        dw_ref[...] += _rows8([jnp.sum(dcc * cv2, axis=0, keepdims=True),
                               jnp.sum(dcc * cv1, axis=0, keepdims=True),
                               jnp.sum(dcc * cv, axis=0, keepdims=True)], tc)

    out = jax.ShapeDtypeStruct((s, conv), BF)
    return pl.pallas_call(
        body, name="conv_mix_bwd", grid=(nc, nr),
        in_specs=[_spec_cur(tr, tc, 0), _spec_next(tr, tc, 0, s),
                  _spec_cur(tr, tc, nc), _spec_prev(tr, tc, nc),
                  _spec_cur(tr, tc, 2 * nc), _spec_prev(tr, tc, 2 * nc),
                  _spec_cur(tr, tc, 0), _spec_next(tr, tc, 0, s), _spec_w(tc, 0)],
        out_specs=[_spec_cur(tr, tc, 0), _spec_cur(tr, tc, 0), _spec_cur(tr, tc, 0), _spec_w(tc, 0)],
        out_shape=[out, out, out, jax.ShapeDtypeStruct((SUB, conv), F32)],
        compiler_params=_cp("parallel", "arbitrary"),
    )(z_a, z_a, z_a, z_a, z_a, z_a, d_p, d_p, cw8)


def _silu_parts(ag):
    sg = jax.nn.sigmoid(ag)
    return ag * sg, sg


def _ffn_act_fwd(a_pre, cw8, cb, dff):
    s = a_pre.shape[0]
    tr = _pick(s, ROW_TILE, 16)
    tc = _pick(dff, COL_TILE, LANE)
    nc = dff // tc

    def body(xg_ref, xgp_ref, xu_ref, xup_ref, wg_ref, wu_ref, bg_ref, bu_ref, f_ref):
        i = pl.program_id(1)
        xgp = jnp.where(i > 0, xgp_ref[...], 0.0)
        xup = jnp.where(i > 0, xup_ref[...], 0.0)
        ag = _conv3(wg_ref, xg_ref[...], xgp) + bg_ref[...]
        au = _conv3(wu_ref, xu_ref[...], xup) + bu_ref[...]
        f_ref[...] = (_silu_parts(ag)[0] * au).astype(BF)

    return pl.pallas_call(
        body, name="ffn_act_fwd", grid=(nc, s // tr),
        in_specs=[_spec_cur(tr, tc, 0), _spec_prev(tr, tc, 0), _spec_cur(tr, tc, nc), _spec_prev(tr, tc, nc),
                  _spec_w(tc, 0), _spec_w(tc, nc),
                  pl.BlockSpec((1, tc), lambda j, i: (0, j)), pl.BlockSpec((1, tc), lambda j, i: (0, nc + j))],
        out_specs=_spec_cur(tr, tc, 0),
        out_shape=jax.ShapeDtypeStruct((s, dff), BF),
        compiler_params=_cp("parallel", "parallel"),
    )(a_pre, a_pre, a_pre, a_pre, cw8, cw8, cb, cb)


def _ffn_act_bwd(a_pre, d_f, cw8, cb, dff):
    s = a_pre.shape[0]
    tr = _pick(s, ROW_TILE_BWD, 16)
    tc = _pick(dff, COL_TILE, LANE)
    nc = dff // tc
    nr = s // tr

    def body(xg_ref, xgp_ref, xgn_ref, xu_ref, xup_ref, xun_ref, df_ref, dfn_ref,
             wg_ref, wu_ref, bg_ref, bu_ref, dxg_ref, dxu_ref, dwg_ref, dwu_ref):
        i = pl.program_id(1)
        xg = xg_ref[...]
        xu = xu_ref[...]
        xgp = jnp.where(i > 0, xgp_ref[...], 0.0)
        xup = jnp.where(i > 0, xup_ref[...], 0.0)

        def d_act(xg_t, xgp_t, xu_t, xup_t, df_t, lags_g=None, lags_u=None):
            ag = _conv3(wg_ref, xg_t, xgp_t, lags_g) + bg_ref[...]
            au = _conv3(wu_ref, xu_t, xup_t, lags_u) + bu_ref[...]
            sil, sg = _silu_parts(ag)
            return df_t * au * (sg * (1.0 + ag * (1.0 - sg))), df_t * sil

        lags_g = _lags(xg, xgp)
        lags_u = _lags(xu, xup)
        dag, dau = d_act(xg, xgp, xu, xup, df_ref[...], lags_g, lags_u)
        dfn = jnp.where(i < nr - 1, dfn_ref[...], 0.0)
        dagn, daun = d_act(xgn_ref[...], xg[tr - SUB:], xun_ref[...], xu[tr - SUB:], dfn)
        dxg_ref[...] = _conv3_t(wg_ref, dag, dagn).astype(BF)
        dxu_ref[...] = _conv3_t(wu_ref, dau, daun).astype(BF)

        @pl.when(i == 0)
        def _():
            dwg_ref[...] = jnp.zeros_like(dwg_ref)
            dwu_ref[...] = jnp.zeros_like(dwu_ref)

        def wgrad(da, x, lags):
            return _rows8([jnp.sum(da * lags[1], axis=0, keepdims=True),
                           jnp.sum(da * lags[0], axis=0, keepdims=True),
                           jnp.sum(da * x, axis=0, keepdims=True),
                           jnp.sum(da, axis=0, keepdims=True)], tc)

        dwg_ref[...] += wgrad(dag, xg, lags_g)
        dwu_ref[...] += wgrad(dau, xu, lags_u)

    half = jax.ShapeDtypeStruct((s, dff), BF)
    wsh = jax.ShapeDtypeStruct((SUB, dff), F32)
    return pl.pallas_call(
        body, name="ffn_act_bwd", grid=(nc, nr),
        in_specs=[_spec_cur(tr, tc, 0), _spec_prev(tr, tc, 0), _spec_next(tr, tc, 0, s),
                  _spec_cur(tr, tc, nc), _spec_prev(tr, tc, nc), _spec_next(tr, tc, nc, s),
                  _spec_cur(tr, tc, 0), _spec_next(tr, tc, 0, s),
                  _spec_w(tc, 0), _spec_w(tc, nc),
                  pl.BlockSpec((1, tc), lambda j, i: (0, j)), pl.BlockSpec((1, tc), lambda j, i: (0, nc + j))],
        out_specs=[_spec_cur(tr, tc, 0), _spec_cur(tr, tc, 0), _spec_w(tc, 0), _spec_w(tc, 0)],
        out_shape=[half, half, wsh, wsh],
        compiler_params=_cp("parallel", "arbitrary"),
    )(a_pre, a_pre, a_pre, a_pre, a_pre, a_pre, d_f, d_f, cw8, cw8, cb, cb)


def _mla_out_gate(o, w_mo, z_g, b_gate, yc):
    m, k = o.shape
    d = w_mo.shape[2]
    tm = _pick(m, MM_TM, 16)
    tn = _pick(d, MM_TM, LANE)
    nc = d // tn

    def body(a_ref, b_ref, za_ref, zb_ref, ba_ref, bb_ref, yc_ref, ym_ref, mix_ref):
        ym = jnp.dot(a_ref[...], b_ref[0], preferred_element_type=F32)
        ga = jax.nn.sigmoid(za_ref[...] + ba_ref[...])
        gb = jax.nn.sigmoid(zb_ref[...] + bb_ref[...])
        ym_ref[...] = ym.astype(BF)
        mix_ref[...] = (ga * yc_ref[...] + gb * ym).astype(BF)

    tile = pl.BlockSpec((tm, tn), lambda i, j: (i, j))
    out = jax.ShapeDtypeStruct((m, d), BF)
    return pl.pallas_call(
        body, name="mm_y_mla_gate", grid=(m // tm, nc),
        in_specs=[pl.BlockSpec((tm, k), lambda i, j: (i, 0)), pl.BlockSpec((1, k, tn), lambda i, j: (0, 0, j)),
                  tile, pl.BlockSpec((tm, tn), lambda i, j: (i, nc + j)),
                  pl.BlockSpec((1, tn), lambda i, j: (0, j)), pl.BlockSpec((1, tn), lambda i, j: (0, nc + j)), tile],
        out_specs=[tile, tile], out_shape=[out, out],
        compiler_params=_cp("parallel", "parallel"),
    )(o, w_mo, z_g, z_g, b_gate, b_gate, yc)


def _d_mix_gate(d_h1, w_oo, z_g, b_gate, yc, ym):
    m, n = d_h1.shape
    d = w_oo.shape[1]
    tm = _pick(m, MM_TM, 16)
    tn = _pick(d, COL_TILE, LANE)
    nc = d // tn

    def body(a_ref, b_ref, za_ref, zb_ref, ba_ref, bb_ref, yc_ref, ym_ref,
             dza_ref, dzb_ref, dyc_ref, dym_ref, dba_ref, dbb_ref):
        i = pl.program_id(1)
        dm = lax.dot_general(a_ref[...], b_ref[0], (((1,), (1,)), ((), ())), preferred_element_type=F32)
        ga = jax.nn.sigmoid(za_ref[...] + ba_ref[...])
        gb = jax.nn.sigmoid(zb_ref[...] + bb_ref[...])
        dza = dm * yc_ref[...] * (ga * (1.0 - ga))
        dzb = dm * ym_ref[...] * (gb * (1.0 - gb))
        dza_ref[...] = dza.astype(BF)
        dzb_ref[...] = dzb.astype(BF)
        dyc_ref[...] = (dm * ga).astype(BF)
        dym_ref[...] = (dm * gb).astype(BF)

        @pl.when(i == 0)
        def _():
            dba_ref[...] = jnp.zeros_like(dba_ref)
            dbb_ref[...] = jnp.zeros_like(dbb_ref)

        dba_ref[...] += _rows8([jnp.sum(dza, axis=0, keepdims=True)], tn)
        dbb_ref[...] += _rows8([jnp.sum(dzb, axis=0, keepdims=True)], tn)

    tile = pl.BlockSpec((tm, tn), lambda j, i: (i, j))
    act = jax.ShapeDtypeStruct((m, d), BF)
    bsh = jax.ShapeDtypeStruct((SUB, d), F32)
    return pl.pallas_call(
        body, name="mm_d_mix_gate", grid=(nc, m // tm),
        in_specs=[pl.BlockSpec((tm, n), lambda j, i: (i, 0)), pl.BlockSpec((1, tn, n), lambda j, i: (0, j, 0)),
                  tile, pl.BlockSpec((tm, tn), lambda j, i: (i, nc + j)),
                  pl.BlockSpec((1, tn), lambda j, i: (0, j)), pl.BlockSpec((1, tn), lambda j, i: (0, nc + j)),
                  tile, tile],
        out_specs=[tile] * 4 + [pl.BlockSpec((SUB, tn), lambda j, i: (0, j))] * 2,
        out_shape=[act, act, act, act, bsh, bsh],
        compiler_params=_cp("parallel", "arbitrary"),
    )(d_h1, w_oo, z_g, z_g, b_gate, b_gate, yc, ym)


def _lay(v):
    z = jnp.zeros(v.shape[:-1] + (HALF,), v.dtype)
    return jnp.concatenate([v[..., :HALF], z, v[..., HALF:], z], axis=-1)


def _unlay(v):
    return jnp.concatenate([v[..., :HALF], v[..., 2 * HALF:3 * HALF]], axis=-1)


def _lay_rows(v):
    z = jnp.zeros((HALF,) + v.shape[1:], v.dtype)
    return jnp.concatenate([v[:HALF], z, v[HALF:], z], axis=0)


def _rope_tables(positions):
    s = positions.shape[0]
    tr = _pick(s, ROW_TILE, 8)
    inv_freq = ROPE_THETA ** (-jnp.arange(0, ROPE, 2, dtype=F32) / ROPE)
    consts = jnp.stack([_lay(jnp.concatenate([inv_freq, inv_freq])),
                        _lay(jnp.ones((ROPE,), F32)),
                        _lay(jnp.concatenate([-jnp.ones((HALF,), F32), jnp.ones((HALF,), F32)]))])
    consts = _pad8(consts)

    def body(p_ref, c_ref, cos_ref, sin_ref):
        ang = p_ref[...].astype(F32) * c_ref[0:1, :]
        cos_ref[...] = jnp.cos(ang) * c_ref[1:2, :]
        sin_ref[...] = jnp.sin(ang) * c_ref[2:3, :]

    tab = jax.ShapeDtypeStruct((s, LANE), F32)
    return pl.pallas_call(
        body, name="rope_tables", grid=(s // tr,),
        in_specs=[pl.BlockSpec((tr, 1), lambda i: (i, 0)), pl.BlockSpec((SUB, LANE), lambda i: (0, 0))],
        out_specs=[pl.BlockSpec((tr, LANE), lambda i: (i, 0))] * 2,
        out_shape=[tab, tab],
        compiler_params=_cp("parallel"),
    )(positions, consts)


def _lane_sum(p):
    return jnp.sum(p, axis=-1, keepdims=True)


def _rope(t, cos, sin):
    return t * cos + pltpu.roll(t, 2 * HALF, axis=1) * sin


def _rope_t(d, cos, sin):
    return d * cos + pltpu.roll(d * sin, 2 * HALF, axis=1)


def _head_fwd(q_raw, kv_raw, z_a, kr_blk, cos, sin, gains, heads):
    s = q_raw.shape[0]
    tr = _pick(s, HEAD_ROW_TILE, 16)
    hw = heads * LANE

    def body(q_ref, kv_ref, kr_ref, cos_ref, sin_ref, g_ref, qo_ref, ko_ref, vo_ref):
        cosv = cos_ref[...]
        sinv = sin_ref[...]
        krv = kr_ref[...]
        kr_sq = krv * krv
        for h in range(heads):
            lo = h * LANE
            qn = q_ref[:, lo:lo + LANE]
            qr = q_ref[:, hw + lo:hw + lo + LANE]
            r = lax.rsqrt(_lane_sum(qn * qn + qr * qr) / HEAD_QK + NORM_EPS)
            qo_ref[:, 2 * lo:2 * lo + LANE] = ((qn * r) * g_ref[0:1, :]).astype(BF)
            qo_ref[:, 2 * lo + LANE:2 * lo + 2 * LANE] = _rope((qr * r) * g_ref[1:2, :], cosv, sinv).astype(BF)
            kn = kv_ref[:, 2 * lo:2 * lo + LANE]
            r = lax.rsqrt(_lane_sum(kn * kn + kr_sq) / HEAD_QK + NORM_EPS)
            ko_ref[:, 2 * lo:2 * lo + LANE] = ((kn * r) * g_ref[2:3, :]).astype(BF)
            ko_ref[:, 2 * lo + LANE:2 * lo + 2 * LANE] = _rope((krv * r) * g_ref[3:4, :], cosv, sinv).astype(BF)
            vo_ref[:, lo:lo + LANE] = kv_ref[:, 2 * lo + LANE:2 * lo + 2 * LANE].astype(BF)

    row = lambda w: pl.BlockSpec((tr, w), lambda i: (i, 0))
    return pl.pallas_call(
        body, name="head_fwd", grid=(s // tr,),
        in_specs=[row(2 * hw), row(2 * hw), pl.BlockSpec((tr, LANE), lambda i: (i, kr_blk)),
                  row(LANE), row(LANE), pl.BlockSpec((SUB, LANE), lambda i: (0, 0))],
        out_specs=[row(2 * hw), row(2 * hw), row(hw)],
        out_shape=[jax.ShapeDtypeStruct((s, 2 * hw), BF), jax.ShapeDtypeStruct((s, 2 * hw), BF),
                   jax.ShapeDtypeStruct((s, hw), BF)],
        compiler_params=_cp("parallel"),
    )(q_raw, kv_raw, z_a, cos, sin, gains)


def _head_bwd(q_raw, kv_raw, z_a, kr_blk, cos, sin, gains, dq_att, dk_att, dv, heads):
    s = q_raw.shape[0]
    tr = _pick(s, HEAD_ROW_TILE_BWD, 16)
    hw = heads * LANE

    def body(q_ref, kv_ref, kr_ref, cos_ref, sin_ref, g_ref, dq_ref, dk_ref, dv_ref,
             dqr_ref, dkv_ref, dkr_ref, dg_ref):
        i = pl.program_id(0)
        cosv = cos_ref[...]
        sinv = sin_ref[...]
        krv = kr_ref[...]
        kr_sq = krv * krv
        dkr = jnp.zeros((tr, LANE), F32)
        dgs = [jnp.zeros((1, LANE), F32) for _ in range(4)]

        def norm_bwd(xn, xr, sq, dn_out, dr_out, gn, gr):
            r = lax.rsqrt(_lane_sum(sq) / HEAD_QK + NORM_EPS)
            nn = xn * r
            nr = xr * r
            dt = _rope_t(dr_out, cosv, sinv)
            dnn = dn_out * gn
            dnr = dt * gr
            mean = _lane_sum(dnn * nn + dnr * nr) / HEAD_QK
            return (r * (dnn - nn * mean), r * (dnr - nr * mean),
                    jnp.sum(dn_out * nn, axis=0, keepdims=True), jnp.sum(dt * nr, axis=0, keepdims=True))

        for h in range(heads):
            lo = h * LANE
            qn = q_ref[:, lo:lo + LANE]
            qr = q_ref[:, hw + lo:hw + lo + LANE]
            dxn, dxr, g0, g1 = norm_bwd(qn, qr, qn * qn + qr * qr, dq_ref[:, 2 * lo:2 * lo + LANE],
                                        dq_ref[:, 2 * lo + LANE:2 * lo + 2 * LANE], g_ref[0:1, :], g_ref[1:2, :])
            dqr_ref[:, lo:lo + LANE] = dxn.astype(BF)
            dqr_ref[:, hw + lo:hw + lo + LANE] = dxr.astype(BF)
            kn = kv_ref[:, 2 * lo:2 * lo + LANE]
            dxn, dxr, g2, g3 = norm_bwd(kn, krv, kn * kn + kr_sq, dk_ref[:, 2 * lo:2 * lo + LANE],
                                        dk_ref[:, 2 * lo + LANE:2 * lo + 2 * LANE], g_ref[2:3, :], g_ref[3:4, :])
            dkv_ref[:, 2 * lo:2 * lo + LANE] = dxn.astype(BF)
            dkv_ref[:, 2 * lo + LANE:2 * lo + 2 * LANE] = dv_ref[:, lo:lo + LANE].astype(BF)
            dkr = dkr + dxr
            dgs = [a + b for a, b in zip(dgs, (g0, g1, g2, g3))]
        dkr_ref[...] = dkr

        @pl.when(i == 0)
        def _():
            dg_ref[...] = jnp.zeros_like(dg_ref)

        dg_ref[...] += _rows8(dgs, LANE)

    row = lambda w: pl.BlockSpec((tr, w), lambda i: (i, 0))
    return pl.pallas_call(
        body, name="head_bwd", grid=(s // tr,),
        in_specs=[row(2 * hw), row(2 * hw), pl.BlockSpec((tr, LANE), lambda i: (i, kr_blk)),
                  row(LANE), row(LANE), pl.BlockSpec((SUB, LANE), lambda i: (0, 0)),
                  row(2 * hw), row(2 * hw), row(hw)],
        out_specs=[row(2 * hw), row(2 * hw), row(LANE), pl.BlockSpec((SUB, LANE), lambda i: (0, 0))],
        out_shape=[jax.ShapeDtypeStruct((s, 2 * hw), BF), jax.ShapeDtypeStruct((s, 2 * hw), BF),
                   jax.ShapeDtypeStruct((s, LANE), F32), jax.ShapeDtypeStruct((SUB, LANE), F32)],
        compiler_params=_cp("arbitrary"),
    )(q_raw, kv_raw, z_a, cos, sin, gains, dq_att, dk_att, dv)


def _causal_mask(nrows, ncols, row0):
    rows = lax.broadcasted_iota(jnp.int32, (nrows, ncols), 0) + row0
    cols = lax.broadcasted_iota(jnp.int32, (nrows, ncols), 1)
    return cols <= rows


def _causal_steps(nt, q_major):
    pairs = ([(i, j) for i in range(nt) for j in range(i + 1)] if q_major
             else [(i, j) for j in range(nt) for i in range(j, nt)])
    return (jnp.array([p[0] for p in pairs], jnp.int32), jnp.array([p[1] for p in pairs], jnp.int32))


def _attn_fwd(q_att, k_att, v, heads):
    s = q_att.shape[0]
    t = _pick(s, ATTN_TILE_FWD, LANE)
    nt = s // t
    th = t // 2
    scale = HEAD_QK ** -0.5
    qi, kj = _causal_steps(nt, True)

    def body(qi_ref, kj_ref, q_ref, k_ref, v_ref, o_ref, ob_ref, lse_ref, m_s, l_s, acc_s):
        st = pl.program_id(1)
        i = qi_ref[st]
        j = kj_ref[st]

        @pl.when(j == 0)
        def _():
            m_s[...] = jnp.full_like(m_s, NEG_INF)
            l_s[...] = jnp.zeros_like(l_s)
            acc_s[...] = jnp.zeros_like(acc_s)

        def update(rows, ncol, masked):
            sc = lax.dot_general(q_ref[rows, :], k_ref[0:ncol, :], (((1,), (1,)), ((), ())),
                                 preferred_element_type=F32) * scale
            if masked:
                sc = jnp.where(_causal_mask(rows.stop - rows.start, ncol, rows.start), sc, NEG_INF)
            m_prev = m_s[rows, :]
            m_new = jnp.maximum(m_prev, jnp.max(sc, axis=-1, keepdims=True))
            alpha = jnp.exp(m_prev - m_new)
            p = jnp.exp(sc - jnp.tile(m_new, (1, ncol // LANE)))
            l_s[rows, :] = alpha * l_s[rows, :] + jnp.sum(p, axis=-1, keepdims=True)
            acc_s[rows, :] = alpha * acc_s[rows, :] + jnp.dot(p.astype(BF), v_ref[0:ncol, :],
                                                              preferred_element_type=F32)
            m_s[rows, :] = m_new

        @pl.when(j < i)
        def _():
            update(slice(0, t), t, False)

        @pl.when(j == i)
        def _():
            update(slice(0, th), th, True)
            update(slice(th, t), t, True)
            o = acc_s[...] / l_s[...]
            o_ref[...] = o
            ob_ref[...] = o.astype(BF)
            lse_ref[...] = (m_s[...] + jnp.log(l_s[...]))[:, 0:1]

    q_idx = lambda h, st, qi_r, kj_r: (qi_r[st], h)
    kv_idx = lambda h, st, qi_r, kj_r: (kj_r[st], h)
    return pl.pallas_call(
        body, name="attn_fwd",
        grid_spec=pltpu.PrefetchScalarGridSpec(
            num_scalar_prefetch=2, grid=(heads, qi.shape[0]),
            in_specs=[pl.BlockSpec((t, 2 * LANE), q_idx), pl.BlockSpec((t, 2 * LANE), kv_idx),
                      pl.BlockSpec((t, LANE), kv_idx)],
            out_specs=[pl.BlockSpec((t, LANE), q_idx), pl.BlockSpec((t, LANE), q_idx),
                       pl.BlockSpec((None, t, 1), lambda h, st, qi_r, kj_r: (h, qi_r[st], 0))],
            scratch_shapes=[pltpu.VMEM((t, LANE), F32), pltpu.VMEM((t, LANE), F32), pltpu.VMEM((t, LANE), F32)]),
        out_shape=[jax.ShapeDtypeStruct((s, heads * LANE), F32), jax.ShapeDtypeStruct((s, heads * LANE), BF),
                   jax.ShapeDtypeStruct((heads, s, 1), F32)],
        compiler_params=_cp("parallel", "arbitrary"),
    )(qi, kj, q_att, k_att, v)


def _attn_bwd(q_att, k_att, v, o, lse, d_o, heads, dep=None):
    s = q_att.shape[0]
    t = _pick(s, ATTN_TILE, LANE)
    nt = s // t
    th = t // 2
    scale = HEAD_QK ** -0.5
    qi, kj = _causal_steps(nt, False)

    def body(qi_ref, kj_ref, q_ref, k_ref, v_ref, do_ref, o_ref, lse_ref, *rest):
        dq_ref, dk_ref, dv_ref, dk_s, dv_s = rest[-5:]
        st = pl.program_id(1)
        i = qi_ref[st]
        j = kj_ref[st]

        @pl.when(st == 0)
        def _():
            dq_ref[...] = jnp.zeros_like(dq_ref)

        @pl.when(i == j)
        def _():
            dk_s[...] = jnp.zeros_like(dk_s)
            dv_s[...] = jnp.zeros_like(dv_s)

        def update(rows, ncol, masked):
            nrow = rows.stop - rows.start
            q = q_ref[rows, :]
            k = k_ref[0:ncol, :]
            do = do_ref[rows, :]
            sc = lax.dot_general(q, k, (((1,), (1,)), ((), ())), preferred_element_type=F32) * scale
            if masked:
                sc = jnp.where(_causal_mask(nrow, ncol, rows.start), sc, NEG_INF)
            p = jnp.exp(sc - lse_ref[rows, :])
            dp = lax.dot_general(do, v_ref[0:ncol, :], (((1,), (1,)), ((), ())), preferred_element_type=F32)
            delta = jnp.sum(do.astype(F32) * o_ref[rows, :], axis=-1, keepdims=True)
            ds = (p * (dp - delta) * scale).astype(BF)
            dv_s[0:ncol, :] += lax.dot_general(p.astype(BF), do, (((0,), (0,)), ((), ())),
                                               preferred_element_type=F32)
            dk_s[0:ncol, :] += lax.dot_general(ds, q, (((0,), (0,)), ((), ())), preferred_element_type=F32)
            out_rows = pl.ds(pl.multiple_of(i * t + rows.start, nrow), nrow)
            dq_ref[out_rows, :] += jnp.dot(ds, k, preferred_element_type=F32)

        @pl.when(i > j)
        def _():
            update(slice(0, t), t, False)

        @pl.when(i == j)
        def _():
            update(slice(0, th), th, True)
            update(slice(th, t), t, True)

        @pl.when(i == nt - 1)
        def _():
            dk_ref[...] = dk_s[...].astype(BF)
            dv_ref[...] = dv_s[...].astype(BF)

    q_idx = lambda h, st, qi_r, kj_r: (qi_r[st], h)
    kv_idx = lambda h, st, qi_r, kj_r: (kj_r[st], h)
    in_specs = [pl.BlockSpec((t, 2 * LANE), q_idx), pl.BlockSpec((t, 2 * LANE), kv_idx),
                pl.BlockSpec((t, LANE), kv_idx), pl.BlockSpec((t, LANE), q_idx), pl.BlockSpec((t, LANE), q_idx),
                pl.BlockSpec((None, t, 1), lambda h, st, qi_r, kj_r: (h, qi_r[st], 0))]
    args = [q_att, k_att, v, d_o, o, lse]
    if dep is not None:
        in_specs.append(ANY)
        args.append(dep)
    return pl.pallas_call(
        body, name="attn_bwd",
        grid_spec=pltpu.PrefetchScalarGridSpec(
            num_scalar_prefetch=2, grid=(heads, qi.shape[0]),
            in_specs=in_specs,
            out_specs=[pl.BlockSpec((s, 2 * LANE), lambda h, st, qi_r, kj_r: (0, h)),
                       pl.BlockSpec((t, 2 * LANE), kv_idx), pl.BlockSpec((t, LANE), kv_idx)],
            scratch_shapes=[pltpu.VMEM((t, 2 * LANE), F32), pltpu.VMEM((t, LANE), F32)]),
        out_shape=[jax.ShapeDtypeStruct((s, heads * 2 * LANE), F32),
                   jax.ShapeDtypeStruct((s, heads * 2 * LANE), BF),
                   jax.ShapeDtypeStruct((s, heads * LANE), BF)],
        compiler_params=_cp("parallel", "arbitrary"),
    )(qi, kj, *args)


def _sum_parts(parts, name):
    n, r, c = parts.shape
    tr = _pick(r, 512, 8)

    def body(p_ref, o_ref):
        g = p_ref[0].astype(F32)
        for k in range(1, n):
            g = g + p_ref[k].astype(F32)
        o_ref[...] = g

    return pl.pallas_call(
        body, name=name, grid=(r // tr,),
        in_specs=[pl.BlockSpec((n, tr, c), lambda i: (0, i, 0))],
        out_specs=pl.BlockSpec((tr, c), lambda i: (i, 0)),
        out_shape=jax.ShapeDtypeStruct((r, c), F32),
        compiler_params=_cp("parallel"),
    )(parts)


def _adamw(parts, w, m, v, name, by_cols=False):
    n, rp, c = parts.shape
    r = w.shape[0]
    assert by_cols or rp == r
    tr, tc = (r, _pick(c, 256, LANE)) if by_cols else (_pick(r, 256, 16 if r % 16 == 0 else 8), c)

    def body(p_ref, w_ref, m_ref, v_ref, g_ref, d_ref, mo_ref, vo_ref):
        g = p_ref[0].astype(F32)
        for k in range(1, n):
            g = g + p_ref[k].astype(F32)
        g = g[:r] if by_cols else g
        m_new = ADAM_B1 * m_ref[...] + (1.0 - ADAM_B1) * g
        v_new = ADAM_B2 * v_ref[...] + (1.0 - ADAM_B2) * jnp.square(g)
        m_hat = m_new / (1.0 - ADAM_B1 ** ADAM_STEP)
        v_hat = v_new / (1.0 - ADAM_B2 ** ADAM_STEP)
        g_ref[...] = g
        d_ref[...] = -ADAM_LR * (m_hat / (jnp.sqrt(v_hat) + ADAM_EPS) + ADAM_WD * w_ref[...])
        mo_ref[...] = m_new
        vo_ref[...] = v_new

    idx = (lambda i: (0, i)) if by_cols else (lambda i: (i, 0))
    spec = pl.BlockSpec((tr, tc), idx)
    sh = jax.ShapeDtypeStruct((r, c), F32)
    return pl.pallas_call(
        body, name=name, grid=(c // tc if by_cols else r // tr,),
        in_specs=[pl.BlockSpec((n, rp if by_cols else tr, tc), lambda i: (0,) + idx(i)), spec, spec, spec],
        out_specs=[spec] * 4, out_shape=[sh] * 4,
        compiler_params=_cp("parallel"),
    )(parts, w, m, v)


def _place():
    x, y, c = lax.axis_index("x"), lax.axis_index("y"), lax.axis_index("c")
    chips = [(1 - x, y), (x, 1 - y), (1 - x, 1 - y)]
    return x, y, c, chips


def _all_gather(shards, name, dep=None):
    n = len(shards)
    deps = [] if dep is None else list(dep)

    def body(*refs):
        ins, outs = refs[:n], refs[n + len(deps):2 * n + len(deps)]
        send_sems, recv_sems, local_sems = refs[2 * n + len(deps):]
        x, y, c, chips = _place()
        me, sibling = (x, y, c), (x, y, 1 - c)

        def slot(w, p):
            return outs[w].at[4 * p[0] + 2 * p[1] + p[2]]

        def copy(w, k, block, to, src=None):
            return pltpu.make_async_remote_copy(
                src_ref=slot(w, block) if src is None else src, dst_ref=slot(w, block),
                send_sem=send_sems.at[w, k], recv_sem=recv_sems.at[w, k], device_id=to, device_id_type=MESH)

        first = []
        for w in range(n):
            first += [copy(w, 1 + j, me, (*chip, c), src=ins[w]) for j, chip in enumerate(chips)]
            first.append(copy(w, 0, me, sibling, src=ins[w]))
        for cp in first:
            cp.start()
        mine = [pltpu.make_async_copy(ins[w], slot(w, me), local_sems.at[w]) for w in range(n)]
        for cp in mine:
            cp.start()
        passed = []
        for w in range(n):
            for j, chip in enumerate(chips):
                copy(w, 1 + j, (*chip, c), me).wait_recv()
                cp = copy(w, 4 + j, (*chip, c), sibling)
                cp.start()
                passed.append(cp)
        for w in range(n):
            copy(w, 0, sibling, me).wait_recv()
            for j, chip in enumerate(chips):
                copy(w, 4 + j, (*chip, 1 - c), me).wait_recv()
        for cp in first + passed:
            cp.wait_send()
        for cp in mine:
            cp.wait()

    return pl.pallas_call(
        body, name=name,
        in_specs=[ANY] * (n + len(deps)), out_specs=[ANY] * n,
        out_shape=[jax.ShapeDtypeStruct((N_DEV,) + a.shape, a.dtype) for a in shards],
        scratch_shapes=[pltpu.SemaphoreType.DMA((n, 7)), pltpu.SemaphoreType.DMA((n, 7)),
                        pltpu.SemaphoreType.DMA((n,))],
    )(*shards, *deps)


HBM = pl.BlockSpec(memory_space=pltpu.HBM)
SEM = pl.BlockSpec(memory_space=pltpu.SEMAPHORE)
EFFECT = pltpu.SideEffectType.DATAFLOW_SIDE_EFFECTING
PEERS = [(dx, dy, dc) for dx in (1, 0) for dy in (1, 0) for dc in (0, 1) if (dx, dy, dc) != (0, 0, 0)]


def _peer(x, y, c, flip):
    dx, dy, dc = flip
    return (1 - x if dx else x, 1 - y if dy else y, 1 - c if dc else c)


def _exchange_copies(srcs, lands, send, recv, loc, gather):
    x, y, c, _ = _place()
    me = 4 * x + 2 * y + c
    remote, local = [], []
    for w in range(len(srcs)):
        for k, flip in enumerate(PEERS):
            px, py, pc = _peer(x, y, c, flip)
            src = srcs[w] if gather else srcs[w].at[4 * px + 2 * py + pc]
            remote.append(pltpu.make_async_remote_copy(
                src_ref=src, dst_ref=lands[w].at[me], send_sem=send[w].at[k], recv_sem=recv[w].at[k],
                device_id=(px, py, pc), device_id_type=MESH))
        local.append(pltpu.make_async_copy(srcs[w] if gather else srcs[w].at[me], lands[w].at[me], loc[w]))
    return remote, local


class _Exchange:
    def __init__(self, srcs, lands, send, recv, loc, token, gather):
        self.srcs, self.lands, self.send, self.recv, self.loc = srcs, lands, send, recv, loc
        self.token, self.gather = token, gather


def _exchange_start(srcs, gather, name, dep=None):
    n = len(srcs)
    deps = [] if dep is None else [dep]
    land_shapes = [((N_DEV,) + a.shape) if gather else a.shape for a in srcs]
    lands = [pltpu.with_memory_space_constraint(lax.empty(sh, a.dtype), pltpu.HBM) for sh, a in zip(land_shapes, srcs)]
    srcs = [pltpu.with_memory_space_constraint(a, pltpu.HBM) for a in srcs]

    def body(*refs):
        src_refs, land_refs = refs[:n], refs[n:2 * n]
        outs = refs[2 * n + len(deps):]
        send, recv, loc = outs[:n], outs[n:2 * n], outs[2 * n:3 * n]
        token = outs[-1]
        remote, local = _exchange_copies(src_refs, land_refs, send, recv, loc, gather)
        for cp in remote + local:
            cp.start()
        token[...] = jnp.zeros_like(token)

    out_shape = ([pltpu.SemaphoreType.DMA((len(PEERS),))] * (2 * n) + [pltpu.SemaphoreType.DMA(())] * n
                 + [pltpu.HBM(a.shape, a.dtype) for a in srcs] + [pltpu.HBM(a.shape, a.dtype) for a in lands]
                 + [jax.ShapeDtypeStruct((SUB, LANE), F32)])
    res = pl.pallas_call(
        body, name=name, out_shape=out_shape,
        in_specs=[HBM] * (2 * n) + [ANY] * len(deps),
        out_specs=[SEM] * (3 * n) + [HBM] * (2 * n) + [pl.BlockSpec(memory_space=pltpu.VMEM)],
        input_output_aliases={i: 3 * n + i for i in range(2 * n)},
        compiler_params=pltpu.CompilerParams(has_side_effects=EFFECT),
    )(*srcs, *lands, *deps)
    return _Exchange(res[3 * n:4 * n], res[4 * n:5 * n], res[:n], res[n:2 * n], res[2 * n:3 * n], res[-1], gather)


def _exchange_wait(ex, idxs, after, name):
    n = len(idxs)
    srcs = [ex.srcs[i] for i in idxs]
    lands = [ex.lands[i] for i in idxs]
    sems = [ex.send[i] for i in idxs] + [ex.recv[i] for i in idxs] + [ex.loc[i] for i in idxs]
    gather = ex.gather

    def body(*refs):
        src_refs, land_refs = refs[:n], refs[n:2 * n]
        send, recv, loc = refs[2 * n:3 * n], refs[3 * n:4 * n], refs[4 * n:5 * n]
        remote, local = _exchange_copies(src_refs, land_refs, send, recv, loc, gather)
        for cp in remote:
            cp.wait_send()
            cp.wait_recv()
        for cp in local:
            cp.wait()

    res = pl.pallas_call(
        body, name=name,
        out_shape=[pltpu.HBM(a.shape, a.dtype) for a in srcs] + [pltpu.HBM(a.shape, a.dtype) for a in lands],
        in_specs=[HBM] * (2 * n) + [SEM] * (3 * n) + [ANY],
        out_specs=[HBM] * (2 * n),
        input_output_aliases={i: i for i in range(2 * n)},
        compiler_params=pltpu.CompilerParams(has_side_effects=EFFECT),
    )(*srcs, *lands, *sems, after)
    return res[n:]


def _after(token, a):
    return a + token[0:1, 0:1].astype(a.dtype)


def _unblock(w3):
    nb, k, nbw = w3.shape
    return w3.transpose(1, 0, 2).reshape(k, nb * nbw)


def _block(w, nb):
    k, n = w.shape
    return w.reshape(k, nb, n // nb).transpose(1, 0, 2)


def kernel(x, positions, ln1_g, w_in, b_gate, conv_w, w_conv_out, q_a_g, w_q_b, kv_a_g, w_kv_b, q_norm_g, k_norm_g, w_mla_out, w_o, ln2_g, w_ffn_up, ffn_conv_w, ffn_conv_b, w_ffn_down, loss_target, m_ln1_g, m_w_in, m_b_gate, m_conv_w, m_w_conv_out, m_q_a_g, m_w_q_b, m_kv_a_g, m_w_kv_b, m_q_norm_g, m_k_norm_g, m_w_mla_out, m_w_o, m_ln2_g, m_w_ffn_up, m_ffn_conv_w, m_ffn_conv_b, m_w_ffn_down, v_ln1_g, v_w_in, v_b_gate, v_conv_w, v_w_conv_out, v_q_a_g, v_w_q_b, v_kv_a_g, v_w_kv_b, v_q_norm_g, v_k_norm_g, v_w_mla_out, v_w_o, v_ln2_g, v_w_ffn_up, v_ffn_conv_w, v_ffn_conv_b, v_w_ffn_down):
    s, d = x.shape[1], x.shape[2]
    conv = conv_w.shape[2] * N_DEV
    ql, kvl = q_a_g.shape[1], kv_a_g.shape[1]
    heads = w_q_b.shape[2] * N_DEV // HEAD_QK
    dff = w_ffn_down.shape[1] * N_DEV
    hw = heads * LANE
    conv3 = 3 * conv
    kr_off = conv3 + ql
    kv_off = -(-(kr_off + LANE) // kvl) * kvl
    wa = kv_off + kvl
    assert conv3 % ql == 0 and kr_off % LANE == 0
    xs = x[0]
    tgt = loss_target[0]
    pos = positions.reshape(s, 1)

    nin = w_in.shape[2]
    big = dict(w_in=w_in[0].T, w_conv_out=w_conv_out[0], w_q_b=w_q_b[0], w_kv_b=w_kv_b[0],
               w_mla_out=w_mla_out[0], w_o=w_o[0], w_ffn_up=w_ffn_up[0], w_ffn_down=w_ffn_down[0])
    names = list(big)
    rest = names[1:]
    first = _all_gather([big["w_in"].astype(BF), _pad8(conv_w[0]), _pad8(ffn_conv_w[0])], "gather_w_in")
    cw8 = _unblock(first[1])
    fcw8 = _unblock(first[2])
    ag = _exchange_start([big[k].astype(BF) for k in rest], True, "gather_rest_start", dep=first[1])

    def landed(keys, after, name):
        return _exchange_wait(ag, [rest.index(k) for k in keys], after, name)

    w_in_t = first[0].reshape(N_DEV * nin, d)
    g_off = kr_off + kvl + ROPE
    w_a_t = jnp.concatenate([w_in_t[:kr_off], _lay_rows(w_in_t[kr_off + kvl:g_off]),
                             jnp.zeros((kv_off - kr_off - LANE, d), BF), w_in_t[kr_off:kr_off + kvl]], axis=0)[None]
    w_g_t = w_in_t[g_off:][None]
    gains = _pad8(jnp.concatenate([q_norm_g[:, :NOPE], _lay(q_norm_g[:, NOPE:]),
                                   k_norm_g[:, :NOPE], _lay(k_norm_g[:, NOPE:])], axis=0))
    kr_blk = kr_off // LANE

    cos, sin = _rope_tables(pos)
    u1 = _rms_fwd(xs, _after(ag.token, ln1_g), d, 0, "rms1_fwd")
    z_a = _mm_nt(u1, w_a_t, "mm_z_a")
    z_g = _mm_nt(u1, w_g_t, "mm_z_g", out_dtype=BF)
    p = _conv_mix_fwd(z_a, cw8, conv)
    w_co, w_qb, w_kv = landed(["w_conv_out", "w_q_b", "w_kv_b"], p, "gather_wait_mixers")
    w_co = _unblock(w_co)[None]
    w_kv = _unblock(w_kv)[None]
    wq_full = _unblock(w_qb).reshape(ql, heads, HEAD_QK)
    w_q = jnp.concatenate([wq_full[:, :, :NOPE].reshape(ql, hw), _lay(wq_full[:, :, NOPE:]).reshape(ql, hw)],
                          axis=1)[None]
    yc = _mm_nn(p, w_co, "mm_y_conv", out_dtype=BF)
    qn = _rms_fwd(z_a, q_a_g, ql, conv3 // ql, "rms_q_fwd")
    kvn = _rms_fwd(z_a, kv_a_g, kvl, kv_off // kvl, "rms_kv_fwd")
    q_raw = _mm_nn(qn, w_q, "mm_q")
    kv_raw = _mm_nn(kvn, w_kv, "mm_kv")
    q_att, k_att, v_bf = _head_fwd(q_raw, kv_raw, z_a, kr_blk, cos, sin, gains, heads)
    o, o_bf, lse = _attn_fwd(q_att, k_att, v_bf, heads)
    w_mo, w_oo = landed(["w_mla_out", "w_o"], lse, "gather_wait_outs")
    w_mo = w_mo.reshape(1, hw, d)
    w_oo = w_oo.reshape(1, d, d)
    ym, mix = _mla_out_gate(o_bf, w_mo, z_g, b_gate, yc)
    h1 = _mm_nn(mix, w_oo, "mm_h1", add=xs)
    u2 = _rms_fwd(h1, ln2_g, d, 0, "rms2_fwd")
    w_up, = landed(["w_ffn_up"], u2, "gather_wait_ffn_up")
    a_pre = _mm_nn(u2, w_up, "mm_ffn_up")
    f = _ffn_act_fwd(a_pre, fcw8, ffn_conv_b, dff)
    w_dn, = landed(["w_ffn_down"], f, "gather_wait_ffn_down")
    w_dn = w_dn.reshape(1, dff, d)
    dy, dy_bf, loss_part = _mm_nn_loss(f, w_dn, h1, tgt, "mm_ffn_down_loss")

    g_dn = _mm_tn(f, dy_bf, 1, "mm_g_ffn_down").reshape(N_DEV, dff // N_DEV, d)
    rs_dn = _exchange_start([g_dn], False, "reduce_ffn_down_start")
    d_f = _mm_nt(dy_bf, w_dn, "mm_d_f", dep=rs_dn.token)
    d_xg, d_xu, dfw_g, dfw_u = _ffn_act_bwd(a_pre, d_f, fcw8, ffn_conv_b, dff)
    half = N_DEV // 2
    g_up = _mm_tn(u2, d_xg, half, "mm_g_ffn_up_gate", into=lax.empty((N_DEV, d, 2 * dff // N_DEV), BF))
    g_up = _mm_tn(u2, d_xu, half, "mm_g_ffn_up_up", into=g_up, blk0=half)
    rs_up = _exchange_start([g_up], False, "reduce_ffn_up_start")
    d_u2 = _mm_nt([d_xg, d_xu], w_up, "mm_d_u2", out_dtype=BF, dep=rs_up.token)
    d_h1, d_h1_bf, dg_ln2 = _rms_bwd(h1, d_u2, ln2_g, d, 0, "rms2_bwd", extra=dy, also_bf16=True)
    g_oo = _mm_tn(mix, d_h1_bf, 1, "mm_g_w_o").reshape(N_DEV, d // N_DEV, d)
    d_zga, d_zgb, d_yc, d_ym, dba, dbb = _d_mix_gate(d_h1_bf, w_oo, z_g, b_gate, yc, ym)
    g_co = _block(_mm_tn(p, d_yc, 1, "mm_g_conv_out")[0], N_DEV)
    g_mo = _mm_tn(o_bf, d_ym, 1, "mm_g_mla_out").reshape(N_DEV, hw // N_DEV, d)
    rs_mix = _exchange_start([g_oo, g_co, g_mo], False, "reduce_mixers_start")
    d_p = _mm_nt(d_yc, w_co, "mm_d_p", dep=rs_mix.token)
    d_o = _mm_nt(d_ym, w_mo, "mm_d_o", out_dtype=BF)
    d_zb, d_zc, d_zv, dcw = _conv_mix_bwd(z_a, d_p, cw8, conv)
    dq_att, dk_att, dv = _attn_bwd(q_att, k_att, v_bf, o, lse, d_o, heads, dep=rs_mix.token)
    d_q_raw, d_kv_raw, d_kr, dgains = _head_bwd(q_raw, kv_raw, z_a, kr_blk, cos, sin, gains, dq_att, dk_att, dv, heads)
    g_q2 = _mm_tn(qn, d_q_raw, 1, "mm_g_q")[0]
    g_qb = _block(jnp.concatenate([g_q2[:, :hw].reshape(ql, heads, NOPE),
                                   _unlay(g_q2[:, hw:].reshape(ql, heads, LANE))], axis=2).reshape(ql, heads * HEAD_QK), N_DEV)
    g_kv = _block(_mm_tn(kvn, d_kv_raw, 1, "mm_g_kv")[0], N_DEV)
    rs_qkv = _exchange_start([g_qb, g_kv], False, "reduce_qkv_start")
    d_qn = _mm_nt(d_q_raw, w_q, "mm_d_qn", dep=rs_qkv.token)
    d_kvn = _mm_nt(d_kv_raw, w_kv, "mm_d_kvn")
    d_ql, dg_qa = _rms_bwd(z_a, d_qn, q_a_g, ql, conv3 // ql, "rms_q_bwd", out_dtype=BF)
    d_kvl, dg_kva = _rms_bwd(z_a, d_kvn, kv_a_g, kvl, kv_off // kvl, "rms_kv_bwd", out_dtype=BF)
    d_z_a = jnp.concatenate([d_zb, d_zc, d_zv, d_ql, d_kr.astype(BF), jnp.zeros((s, kv_off - kr_off - LANE), BF),
                             d_kvl], axis=1)
    g_a = _mm_tn(d_z_a, u1, 1, "mm_g_w_a")[0]
    g_ga = _mm_tn(d_zga, u1, 1, "mm_g_w_ga")[0]
    g_gb = _mm_tn(d_zgb, u1, 1, "mm_g_w_gb")[0]
    g_in = jnp.concatenate([g_a[:kr_off], g_a[kv_off:kv_off + kvl], g_a[kr_off:kr_off + HALF],
                            g_a[kr_off + 2 * HALF:kr_off + 3 * HALF], g_ga, g_gb], axis=0).reshape(N_DEV, nin, d)
    rs_in = _exchange_start([g_in], False, "reduce_w_in_start")
    d_u1 = _mm_nn(d_z_a, w_a_t, "mm_d_u1_a", dep=rs_in.token)
    d_u1 = _mm_nn([d_zga, d_zgb], w_g_t, "mm_d_u1_g", add=d_u1)
    grad_x, dg_ln1 = _rms_bwd(xs, d_u1, ln1_g, d, 0, "rms1_bwd", extra=d_h1)

    summed = {}
    summed["w_ffn_down"], = _exchange_wait(rs_dn, [0], grad_x, "reduce_ffn_down_wait")
    summed["w_ffn_up"], = _exchange_wait(rs_up, [0], grad_x, "reduce_ffn_up_wait")
    summed["w_o"], summed["w_conv_out"], summed["w_mla_out"] = _exchange_wait(rs_mix, [0, 1, 2], grad_x, "reduce_mixers_wait")
    summed["w_q_b"], summed["w_kv_b"] = _exchange_wait(rs_qkv, [0, 1], grad_x, "reduce_qkv_wait")
    loc = locals()
    out = {}
    for k in rest:
        out[k] = _adamw(summed[k], big[k], loc["m_" + k][0], loc["v_" + k][0], "adamw_" + k)

    small = dict(ln1_g=dg_ln1[0:1], b_gate=jnp.concatenate([dba[0:1], dbb[0:1]], axis=1), q_a_g=dg_qa[0:1],
                 kv_a_g=dg_kva[0:1],
                 q_norm_g=jnp.concatenate([dgains[0:1], _unlay(dgains[1:2])], axis=1),
                 k_norm_g=jnp.concatenate([dgains[2:3], _unlay(dgains[3:4])], axis=1),
                 ln2_g=dg_ln2[0:1], ffn_conv_b=jnp.concatenate([dfw_g[3:4], dfw_u[3:4]], axis=1))
    small_names = list(small)
    extra = [dcw[0:3].reshape(1, -1), jnp.concatenate([dfw_g[0:3], dfw_u[0:3]], axis=1).reshape(1, -1),
             loss_part[0:1, 0:1]]
    flat = jnp.concatenate([small[k] for k in small_names] + extra, axis=1)
    n_flat = flat.shape[1]
    rows = -(-n_flat // (SUB * LANE)) * SUB
    flat = jnp.pad(flat, ((0, 0), (0, rows * LANE - n_flat))).reshape(rows, LANE)
    total = _sum_parts(_all_gather([flat], "gather_small", dep=[out[k][0] for k in rest])[0], "sum_small").reshape(1, rows * LANE)
    off = 0
    small_g = {}
    for k in small_names:
        small_g[k] = total[:, off:off + small[k].shape[1]]
        off += small[k].shape[1]
    me = 4 * lax.axis_index("x") + 2 * lax.axis_index("y") + lax.axis_index("c")
    cwn, fcwn = conv // N_DEV, 2 * dff // N_DEV
    g_cw = lax.dynamic_slice_in_dim(total[:, off:off + 3 * conv].reshape(3, conv), me * cwn, cwn, axis=1)
    off += 3 * conv
    g_fcw = lax.dynamic_slice_in_dim(total[:, off:off + 6 * dff].reshape(3, 2 * dff), me * fcwn, fcwn, axis=1)
    off += 6 * dff
    loss = total[0, off]

    summed["w_in"], = _exchange_wait(rs_in, [0], total, "reduce_w_in_wait")
    out["w_in"] = [r.T for r in _adamw(summed["w_in"], big["w_in"], m_w_in[0].T, v_w_in[0].T, "adamw_w_in",
                                       by_cols=True)]
    small_w = dict(ln1_g=ln1_g, b_gate=b_gate, q_a_g=q_a_g, kv_a_g=kv_a_g, q_norm_g=q_norm_g, k_norm_g=k_norm_g,
                   ln2_g=ln2_g, ffn_conv_b=ffn_conv_b, conv_w=conv_w[0].reshape(1, -1),
                   ffn_conv_w=ffn_conv_w[0].reshape(1, -1))
    small_g["conv_w"] = g_cw.reshape(1, -1)
    small_g["ffn_conv_w"] = g_fcw.reshape(1, -1)
    packed_names = list(small_w)

    def pack(get):
        vflat = jnp.concatenate([get(k).reshape(1, -1) for k in packed_names], axis=1)
        nr = -(-vflat.shape[1] // (SUB * LANE)) * SUB
        return jnp.pad(vflat, ((0, 0), (0, nr * LANE - vflat.shape[1])), constant_values=1.0).reshape(nr, LANE)

    res = _adamw(pack(lambda k: small_g[k])[None], pack(lambda k: small_w[k]), pack(lambda k: loc["m_" + k]),
                 pack(lambda k: loc["v_" + k]), "adamw_small")
    res = [r.reshape(1, -1) for r in res]
    off = 0
    for k in packed_names:
        shape = loc[k].shape
        size = small_w[k].shape[1]
        out[k] = [r[:, off:off + size].reshape(shape) for r in res]
        off += size
    for k in names:
        out[k] = [r[None] for r in out[k]]

    order = ["ln1_g", "w_in", "b_gate", "conv_w", "w_conv_out", "q_a_g", "w_q_b", "kv_a_g", "w_kv_b", "q_norm_g",
             "k_norm_g", "w_mla_out", "w_o", "ln2_g", "w_ffn_up", "ffn_conv_w", "ffn_conv_b", "w_ffn_down"]
    return (loss, grad_x[None], *[out[k][0] for k in order], *[out[k][1] for k in order],
            *[out[k][2] for k in order], *[out[k][3] for k in order])
```

```python
import functools

import jax
import jax.numpy as jnp
from jax import lax
from jax.experimental import pallas as pl
from jax.experimental.pallas import tpu as pltpu

BF = jnp.bfloat16
F32 = jnp.float32
MESH = pl.DeviceIdType.MESH
N_DEV = 8

NOPE = 128
ROPE = 64
HALF = ROPE // 2
HEAD_QK = NOPE + ROPE
HEAD_V = 128
LANE = 128
SUB = 8
NORM_EPS = 1e-6
NEG_INF = -1e30
ROPE_THETA = 10000.0
ADAM_LR = 0.001
ADAM_B1 = 0.9
ADAM_B2 = 0.999
ADAM_EPS = 1e-08
ADAM_WD = 0.01
ADAM_STEP = 10

VMEM_LIMIT = 52 * 1024 * 1024
MM_TM, MM_TN, MM_TK, MM_TS = 1024, 1536, 2048, 2048
ROW_TILE, ROW_TILE_BWD = 512, 256
HEAD_ROW_TILE, HEAD_ROW_TILE_BWD = 256, 128
COL_TILE = 512
ATTN_TILE = 1024
ATTN_TILE_FWD = 1024
ANY = pl.BlockSpec(memory_space=pl.ANY)


def _pick(n, target, mult):
    t = (min(n, target) // mult) * mult
    while t > 0:
        if n % t == 0:
            return t
        t -= mult
    raise ValueError(f"no tile for {n} (target {target}, multiple {mult})")


def _cp(*sem):
    return pltpu.CompilerParams(dimension_semantics=sem, vmem_limit_bytes=VMEM_LIMIT)


def _accumulate(kk, nk, acc, part, finish):
    if nk == 1:
        finish(part())
        return

    @pl.when(kk == 0)
    def _():
        acc[...] = part()

    @pl.when((kk > 0) & (kk < nk - 1))
    def _():
        acc[...] += part()

    @pl.when(kk == nk - 1)
    def _():
        finish(acc[...] + part())


def _mm_call(body, name, grid, in_specs, args, out_spec, out_shape, acc_shape, nk, dep):
    if dep is not None:
        in_specs = in_specs + [ANY]
        args = args + [dep]
    return pl.pallas_call(
        body, name=name, grid=grid, in_specs=in_specs, out_specs=out_spec, out_shape=out_shape,
        scratch_shapes=[pltpu.VMEM(acc_shape, F32)] if nk > 1 else [],
        compiler_params=_cp("parallel", "parallel", "arbitrary"),
    )(*args)


def _mm_nn_loss(a, b3, add, target, name):
    m, k = a.shape
    _, k2, n = b3.shape
    assert k == k2 and b3.shape[0] == 1
    tm = _pick(m, MM_TM, 16)
    tn = _pick(n, MM_TN, LANE)
    tk = _pick(k, MM_TK, LANE)
    nk = k // tk

    def body(a_ref, b_ref, c_ref, t_ref, dy_ref, dyb_ref, l_ref, acc):
        kk = pl.program_id(2)

        @pl.when((pl.program_id(0) == 0) & (pl.program_id(1) == 0) & (kk == 0))
        def _():
            l_ref[...] = jnp.zeros_like(l_ref)

        def part():
            return jnp.dot(a_ref[...].astype(BF), b_ref[0].astype(BF), preferred_element_type=F32)

        def finish(r):
            e = r + c_ref[...] - t_ref[...]
            dy_ref[...] = e / n
            dyb_ref[...] = (e / n).astype(BF)
            l_ref[...] += 0.5 * jnp.sum(jnp.sum(e * e, axis=-1, keepdims=True), axis=0, keepdims=True) / n

        _accumulate(kk, nk, acc, part, finish)

    tile = pl.BlockSpec((tm, tn), lambda i, j, kk: (i, j))
    return pl.pallas_call(
        body, name=name, grid=(m // tm, n // tn, nk),
        in_specs=[pl.BlockSpec((tm, tk), lambda i, j, kk: (i, kk)),
                  pl.BlockSpec((1, tk, tn), lambda i, j, kk: (0, kk, j)), tile, tile],
        out_specs=[tile, tile, pl.BlockSpec((SUB, LANE), lambda i, j, kk: (0, 0))],
        out_shape=[jax.ShapeDtypeStruct((m, n), F32), jax.ShapeDtypeStruct((m, n), BF),
                   jax.ShapeDtypeStruct((SUB, LANE), F32)],
        scratch_shapes=[pltpu.VMEM((tm, tn), F32)],
        compiler_params=_cp("arbitrary", "arbitrary", "arbitrary"),
    )(a, b3, add, target)


def _mm_nn(a, b3, name, add=None, out_dtype=F32, blk0=0, nblk=None, dep=None):
    pair = isinstance(a, (list, tuple))
    a_list = list(a) if pair else [a]
    m, ka = a_list[0].shape
    k = ka * len(a_list)
    nb_all, k2, nbw = b3.shape
    assert k == k2
    nblk = nb_all - blk0 if nblk is None else nblk
    n = nblk * nbw
    tm = _pick(m, MM_TM if k > MM_TM else 2 * MM_TM, 16)
    tn = _pick(nbw, MM_TN, LANE)
    tk = _pick(ka, MM_TK, LANE)
    per = nbw // tn
    nk = k // tk
    nka = ka // tk
    na_ops = len(a_list)

    def body(*refs):
        a_refs, b_ref = refs[:na_ops], refs[na_ops]
        c_ref = refs[na_ops + 1] if add is not None else None
        o_ref = refs[na_ops + 1 + (add is not None) + (dep is not None)]
        acc = refs[-1]
        kk = pl.program_id(2)

        def part():
            av = a_refs[0][...] if not pair else jnp.where(kk < nka, a_refs[0][...], a_refs[1][...])
            return jnp.dot(av.astype(BF), b_ref[...].astype(BF), preferred_element_type=F32)

        def finish(r):
            if add is not None:
                r = r + c_ref[...]
            o_ref[...] = r.astype(out_dtype)

        _accumulate(kk, nk, acc, part, finish)

    if pair:
        in_specs = [pl.BlockSpec((tm, tk), lambda i, j, kk: (i, jnp.minimum(kk, nka - 1))),
                    pl.BlockSpec((tm, tk), lambda i, j, kk: (i, jnp.maximum(kk - nka, 0)))]
    else:
        in_specs = [pl.BlockSpec((tm, tk), lambda i, j, kk: (i, kk))]
    in_specs.append(pl.BlockSpec((None, tk, tn), lambda i, j, kk: (blk0 + j // per, kk, j % per)))
    args = a_list + [b3]
    if add is not None:
        in_specs.append(pl.BlockSpec((tm, tn), lambda i, j, kk: (i, j)))
        args.append(add)
    return _mm_call(body, name, (m // tm, n // tn, nk), in_specs, args,
                    pl.BlockSpec((tm, tn), lambda i, j, kk: (i, j)), jax.ShapeDtypeStruct((m, n), out_dtype),
                    (tm, tn), nk, dep)


def _mm_nt(a, b3, name, add=None, out_dtype=F32, blk0=0, nblk=None, dep=None):
    pair = isinstance(a, (list, tuple))
    a_list = list(a) if pair else [a]
    m, na = a_list[0].shape
    n = na * len(a_list)
    nb_all, k, nbw = b3.shape
    nblk = nb_all - blk0 if nblk is None else nblk
    assert n == nblk * nbw and na % nbw == 0
    tm = _pick(m, 2 * MM_TM if k <= MM_TM and n <= MM_TK else MM_TM, 16)
    tk = _pick(nbw, MM_TK, LANE)
    per = nbw // tk
    nk = n // tk
    tn = _pick(k, MM_TN if nk <= 2 else 2 * MM_TM, LANE)
    nka = na // tk
    na_ops = len(a_list)

    def body(*refs):
        a_refs, b_ref = refs[:na_ops], refs[na_ops]
        c_ref = refs[na_ops + 1] if add is not None else None
        o_ref = refs[na_ops + 1 + (add is not None) + (dep is not None)]
        acc = refs[-1]
        kk = pl.program_id(2)

        def part():
            av = a_refs[0][...] if not pair else jnp.where(kk < nka, a_refs[0][...], a_refs[1][...])
            return lax.dot_general(av.astype(BF), b_ref[...].astype(BF),
                                   (((1,), (1,)), ((), ())), preferred_element_type=F32)

        def finish(r):
            if add is not None:
                r = r + c_ref[...]
            o_ref[...] = r.astype(out_dtype)

        _accumulate(kk, nk, acc, part, finish)

    if pair:
        in_specs = [pl.BlockSpec((tm, tk), lambda i, j, kk: (i, jnp.minimum(kk, nka - 1))),
                    pl.BlockSpec((tm, tk), lambda i, j, kk: (i, jnp.maximum(kk - nka, 0)))]
    else:
        in_specs = [pl.BlockSpec((tm, tk), lambda i, j, kk: (i, kk))]
    in_specs.append(pl.BlockSpec((None, tn, tk), lambda i, j, kk: (blk0 + kk // per, j, kk % per)))
    args = a_list + [b3]
    if add is not None:
        in_specs.append(pl.BlockSpec((tm, tn), lambda i, j, kk: (i, j)))
        args.append(add)
    return _mm_call(body, name, (m // tm, k // tn, nk), in_specs, args,
                    pl.BlockSpec((tm, tn), lambda i, j, kk: (i, j)), jax.ShapeDtypeStruct((m, k), out_dtype),
                    (tm, tn), nk, dep)


def _mm_tn(a, b, nblk, name, out_dtype=BF, dep=None, into=None, blk0=0):
    s, m = a.shape
    s2, n = b.shape
    assert s == s2 and n % nblk == 0 and (dep is None or into is None)
    nbw = n // nblk
    tm = _pick(m, MM_TN, LANE)
    tn = _pick(nbw, MM_TN, LANE)
    ts = _pick(s, MM_TS, LANE)
    per = nbw // tn
    ns = s // ts

    def body(*refs):
        a_ref, b_ref = refs[:2]
        o_ref = refs[2 + (dep is not None or into is not None)]
        acc = refs[-1]

        def part():
            return lax.dot_general(a_ref[...].astype(BF), b_ref[...].astype(BF),
                                   (((0,), (0,)), ((), ())), preferred_element_type=F32)

        def finish(r):
            o_ref[...] = r.astype(out_dtype)

        _accumulate(pl.program_id(2), ns, acc, part, finish)

    in_specs = [pl.BlockSpec((ts, tm), lambda i, j, ss: (ss, i)),
                pl.BlockSpec((ts, tn), lambda i, j, ss: (ss, j))]
    out_spec = pl.BlockSpec((None, tm, tn), lambda i, j, ss: (blk0 + j // per, i, j % per))
    if into is None:
        return _mm_call(body, name, (m // tm, n // tn, ns), in_specs, [a, b], out_spec,
                        jax.ShapeDtypeStruct((nblk, m, nbw), out_dtype), (tm, tn), ns, dep)
    assert into.shape[1:] == (m, nbw) and into.dtype == out_dtype
    return pl.pallas_call(
        body, name=name, grid=(m // tm, n // tn, ns), in_specs=in_specs + [ANY], out_specs=out_spec,
        out_shape=jax.ShapeDtypeStruct(into.shape, out_dtype), input_output_aliases={2: 0},
        scratch_shapes=[pltpu.VMEM((tm, tn), F32)] if ns > 1 else [],
        compiler_params=_cp("parallel", "parallel", "arbitrary"),
    )(a, b, into)


def _rows8(rows, width):
    idx = lax.broadcasted_iota(jnp.int32, (SUB, width), 0)
    out = jnp.zeros((SUB, width), F32)
    for r, v in enumerate(rows):
        out = jnp.where(idx == r, v, out)
    return out


def _rms_fwd(x, g, width, col_blk, name):
    s = x.shape[0]
    tr = _pick(s, ROW_TILE, 16)

    def body(x_ref, g_ref, u_ref):
        xv = x_ref[...]
        r = lax.rsqrt(jnp.mean(xv * xv, axis=-1, keepdims=True) + NORM_EPS)
        u_ref[...] = ((xv * r) * g_ref[...]).astype(BF)

    return pl.pallas_call(
        body, name=name, grid=(s // tr,),
        in_specs=[pl.BlockSpec((tr, width), lambda i: (i, col_blk)),
                  pl.BlockSpec((1, width), lambda i: (0, 0))],
        out_specs=pl.BlockSpec((tr, width), lambda i: (i, 0)),
        out_shape=jax.ShapeDtypeStruct((s, width), BF),
        compiler_params=_cp("parallel"),
    )(x, g)


def _rms_bwd(x, du, g, width, col_blk, name, extra=None, out_dtype=F32, also_bf16=False):
    s = x.shape[0]
    tr = _pick(s, ROW_TILE_BWD, 16)

    def body(*refs):
        x_ref, du_ref, g_ref = refs[:3]
        e_ref = refs[3] if extra is not None else None
        dx_ref = refs[3 + (extra is not None)]
        dxb_ref = refs[4 + (extra is not None)] if also_bf16 else None
        dg_ref = refs[-1]
        i = pl.program_id(0)
        xv = x_ref[...]
        duv = du_ref[...].astype(F32)
        r = lax.rsqrt(jnp.mean(xv * xv, axis=-1, keepdims=True) + NORM_EPS)
        nv = xv * r
        dn = duv * g_ref[...]
        dx = r * (dn - nv * jnp.mean(dn * nv, axis=-1, keepdims=True))
        if extra is not None:
            dx = dx + e_ref[...]
        dx_ref[...] = dx.astype(out_dtype)
        if also_bf16:
            dxb_ref[...] = dx.astype(BF)

        @pl.when(i == 0)
        def _():
            dg_ref[...] = jnp.zeros_like(dg_ref)

        dg_ref[...] += _rows8([jnp.sum(duv * nv, axis=0, keepdims=True)], width)

    in_specs = [pl.BlockSpec((tr, width), lambda i: (i, col_blk)),
                pl.BlockSpec((tr, width), lambda i: (i, 0)),
                pl.BlockSpec((1, width), lambda i: (0, 0))]
    args = [x, du, g]
    if extra is not None:
        in_specs.append(pl.BlockSpec((tr, width), lambda i: (i, 0)))
        args.append(extra)
    return pl.pallas_call(
        body, name=name, grid=(s // tr,),
        in_specs=in_specs,
        out_specs=[pl.BlockSpec((tr, width), lambda i: (i, 0))] * (1 + also_bf16)
        + [pl.BlockSpec((SUB, width), lambda i: (0, 0))],
        out_shape=[jax.ShapeDtypeStruct((s, width), out_dtype)] + [jax.ShapeDtypeStruct((s, width), BF)] * also_bf16
        + [jax.ShapeDtypeStruct((SUB, width), F32)],
        compiler_params=_cp("arbitrary"),
    )(*args)


def _down(cur, prev8, k):
    ext = jnp.concatenate([prev8, cur], axis=0)
    return pltpu.roll(ext, k, axis=0)[SUB:]


def _up(cur, next8, k):
    ext = jnp.concatenate([cur, next8], axis=0)
    return pltpu.roll(ext, ext.shape[0] - k, axis=0)[:cur.shape[0]]


def _lags(cur, prev8):
    return _down(cur, prev8, 1), _down(cur, prev8, 2)


def _conv3(w_ref, cur, prev8, lags=None):
    lag1, lag2 = _lags(cur, prev8) if lags is None else lags
    return w_ref[0:1, :] * lag2 + w_ref[1:2, :] * lag1 + w_ref[2:3, :] * cur


def _conv3_t(w_ref, cur, next8):
    return w_ref[2:3, :] * cur + w_ref[1:2, :] * _up(cur, next8, 1) + w_ref[0:1, :] * _up(cur, next8, 2)


def _spec_cur(tr, tc, c0):
    return pl.BlockSpec((tr, tc), lambda j, i: (i, c0 + j))


def _spec_prev(tr, tc, c0):
    return pl.BlockSpec((SUB, tc), lambda j, i: (jnp.maximum(i * (tr // SUB) - 1, 0), c0 + j))


def _spec_next(tr, tc, c0, s):
    return pl.BlockSpec((SUB, tc), lambda j, i: (jnp.minimum((i + 1) * (tr // SUB), s // SUB - 1), c0 + j))


def _spec_w(tc, c0):
    return pl.BlockSpec((SUB, tc), lambda j, i: (0, c0 + j))


def _pad8(w):
    return jnp.pad(w, ((0, SUB - w.shape[0]), (0, 0)))


def _conv_mix_fwd(z_a, cw8, conv):
    s = z_a.shape[0]
    tr = _pick(s, ROW_TILE, 16)
    tc = _pick(conv, COL_TILE, LANE)
    nc = conv // tc

    def body(zb_ref, zc_ref, zv_ref, zcp_ref, zvp_ref, w_ref, p_ref):
        i = pl.program_id(1)
        cv = zc_ref[...] * zv_ref[...]
        cvp = jnp.where(i > 0, zcp_ref[...] * zvp_ref[...], 0.0)
        p_ref[...] = (zb_ref[...] * _conv3(w_ref, cv, cvp)).astype(BF)

    return pl.pallas_call(
        body, name="conv_mix_fwd", grid=(nc, s // tr),
        in_specs=[_spec_cur(tr, tc, 0), _spec_cur(tr, tc, nc), _spec_cur(tr, tc, 2 * nc),
                  _spec_prev(tr, tc, nc), _spec_prev(tr, tc, 2 * nc), _spec_w(tc, 0)],
        out_specs=_spec_cur(tr, tc, 0),
        out_shape=jax.ShapeDtypeStruct((s, conv), BF),
        compiler_params=_cp("parallel", "parallel"),
    )(z_a, z_a, z_a, z_a, z_a, cw8)


def _conv_mix_bwd(z_a, d_p, cw8, conv):
    s = z_a.shape[0]
    tr = _pick(s, ROW_TILE_BWD, 16)
    tc = _pick(conv, COL_TILE, LANE)
    nc = conv // tc
    nr = s // tr

    def body(zb_ref, zbn_ref, zc_ref, zcp_ref, zv_ref, zvp_ref, dp_ref, dpn_ref, w_ref,
             dzb_ref, dzc_ref, dzv_ref, dw_ref):
        i = pl.program_id(1)
        zc = zc_ref[...]
        zv = zv_ref[...]
        cv = zc * zv
        cvp = jnp.where(i > 0, zcp_ref[...] * zvp_ref[...], 0.0)
        cv1, cv2 = _lags(cv, cvp)
        dpv = dp_ref[...]
        dzb_ref[...] = (dpv * _conv3(w_ref, cv, cvp, (cv1, cv2))).astype(BF)
        dcc = dpv * zb_ref[...]
        dccn = jnp.where(i < nr - 1, dpn_ref[...] * zbn_ref[...], 0.0)
        dcv = _conv3_t(w_ref, dcc, dccn)
        dzc_ref[...] = (dcv * zv).astype(BF)
        dzv_ref[...] = (dcv * zc).astype(BF)

        @pl.when(i == 0)
        def _():
            dw_ref[...] = jnp.zeros_like(dw_ref)

        dw_ref[...] += _rows8([jnp.sum(dcc * cv2, axis=0, keepdims=True),
                               jnp.sum(dcc * cv1, axis=0, keepdims=True),
                               jnp.sum(dcc * cv, axis=0, keepdims=True)], tc)

    out = jax.ShapeDtypeStruct((s, conv), BF)
    return pl.pallas_call(
        body, name="conv_mix_bwd", grid=(nc, nr),
        in_specs=[_spec_cur(tr, tc, 0), _spec_next(tr, tc, 0, s),
                  _spec_cur(tr, tc, nc), _spec_prev(tr, tc, nc),
                  _spec_cur(tr, tc, 2 * nc), _spec_prev(tr, tc, 2 * nc),
                  _spec_cur(tr, tc, 0), _spec_next(tr, tc, 0, s), _spec_w(tc, 0)],
        out_specs=[_spec_cur(tr, tc, 0), _spec_cur(tr, tc, 0), _spec_cur(tr, tc, 0), _spec_w(tc, 0)],
        out_shape=[out, out, out, jax.ShapeDtypeStruct((SUB, conv), F32)],
        compiler_params=_cp("parallel", "arbitrary"),
    )(z_a, z_a, z_a, z_a, z_a, z_a, d_p, d_p, cw8)


def _silu_parts(ag):
    sg = jax.nn.sigmoid(ag)
    return ag * sg, sg


def _ffn_act_fwd(a_pre, cw8, cb, dff):
    s = a_pre.shape[0]
    tr = _pick(s, ROW_TILE, 16)
    tc = _pick(dff, COL_TILE, LANE)
    nc = dff // tc

    def body(xg_ref, xgp_ref, xu_ref, xup_ref, wg_ref, wu_ref, bg_ref, bu_ref, f_ref):
        i = pl.program_id(1)
        xgp = jnp.where(i > 0, xgp_ref[...], 0.0)
        xup = jnp.where(i > 0, xup_ref[...], 0.0)
        ag = _conv3(wg_ref, xg_ref[...], xgp) + bg_ref[...]
        au = _conv3(wu_ref, xu_ref[...], xup) + bu_ref[...]
        f_ref[...] = (_silu_parts(ag)[0] * au).astype(BF)

    return pl.pallas_call(
        body, name="ffn_act_fwd", grid=(nc, s // tr),
        in_specs=[_spec_cur(tr, tc, 0), _spec_prev(tr, tc, 0), _spec_cur(tr, tc, nc), _spec_prev(tr, tc, nc),
                  _spec_w(tc, 0), _spec_w(tc, nc),
                  pl.BlockSpec((1, tc), lambda j, i: (0, j)), pl.BlockSpec((1, tc), lambda j, i: (0, nc + j))],
        out_specs=_spec_cur(tr, tc, 0),
        out_shape=jax.ShapeDtypeStruct((s, dff), BF),
        compiler_params=_cp("parallel", "parallel"),
    )(a_pre, a_pre, a_pre, a_pre, cw8, cw8, cb, cb)


def _ffn_act_bwd(a_pre, d_f, cw8, cb, dff):
    s = a_pre.shape[0]
    tr = _pick(s, ROW_TILE_BWD, 16)
    tc = _pick(dff, COL_TILE, LANE)
    nc = dff // tc
    nr = s // tr

    def body(xg_ref, xgp_ref, xgn_ref, xu_ref, xup_ref, xun_ref, df_ref, dfn_ref,
             wg_ref, wu_ref, bg_ref, bu_ref, dxg_ref, dxu_ref, dwg_ref, dwu_ref):
        i = pl.program_id(1)
        xg = xg_ref[...]
        xu = xu_ref[...]
        xgp = jnp.where(i > 0, xgp_ref[...], 0.0)
        xup = jnp.where(i > 0, xup_ref[...], 0.0)

        def d_act(xg_t, xgp_t, xu_t, xup_t, df_t, lags_g=None, lags_u=None):
            ag = _conv3(wg_ref, xg_t, xgp_t, lags_g) + bg_ref[...]
            au = _conv3(wu_ref, xu_t, xup_t, lags_u) + bu_ref[...]
            sil, sg = _silu_parts(ag)
            return df_t * au * (sg * (1.0 + ag * (1.0 - sg))), df_t * sil

        lags_g = _lags(xg, xgp)
        lags_u = _lags(xu, xup)
        dag, dau = d_act(xg, xgp, xu, xup, df_ref[...], lags_g, lags_u)
        dfn = jnp.where(i < nr - 1, dfn_ref[...], 0.0)
        dagn, daun = d_act(xgn_ref[...], xg[tr - SUB:], xun_ref[...], xu[tr - SUB:], dfn)
        dxg_ref[...] = _conv3_t(wg_ref, dag, dagn).astype(BF)
        dxu_ref[...] = _conv3_t(wu_ref, dau, daun).astype(BF)

        @pl.when(i == 0)
        def _():
            dwg_ref[...] = jnp.zeros_like(dwg_ref)
            dwu_ref[...] = jnp.zeros_like(dwu_ref)

        def wgrad(da, x, lags):
            return _rows8([jnp.sum(da * lags[1], axis=0, keepdims=True),
                           jnp.sum(da * lags[0], axis=0, keepdims=True),
                           jnp.sum(da * x, axis=0, keepdims=True),
                           jnp.sum(da, axis=0, keepdims=True)], tc)

        dwg_ref[...] += wgrad(dag, xg, lags_g)
        dwu_ref[...] += wgrad(dau, xu, lags_u)

    half = jax.ShapeDtypeStruct((s, dff), BF)
    wsh = jax.ShapeDtypeStruct((SUB, dff), F32)
    return pl.pallas_call(
        body, name="ffn_act_bwd", grid=(nc, nr),
        in_specs=[_spec_cur(tr, tc, 0), _spec_prev(tr, tc, 0), _spec_next(tr, tc, 0, s),
                  _spec_cur(tr, tc, nc), _spec_prev(tr, tc, nc), _spec_next(tr, tc, nc, s),
                  _spec_cur(tr, tc, 0), _spec_next(tr, tc, 0, s),
                  _spec_w(tc, 0), _spec_w(tc, nc),
                  pl.BlockSpec((1, tc), lambda j, i: (0, j)), pl.BlockSpec((1, tc), lambda j, i: (0, nc + j))],
        out_specs=[_spec_cur(tr, tc, 0), _spec_cur(tr, tc, 0), _spec_w(tc, 0), _spec_w(tc, 0)],
        out_shape=[half, half, wsh, wsh],
        compiler_params=_cp("parallel", "arbitrary"),
    )(a_pre, a_pre, a_pre, a_pre, a_pre, a_pre, d_f, d_f, cw8, cw8, cb, cb)


def _mla_out_gate(o, w_mo, z_g, b_gate, yc):
    m, k = o.shape
    d = w_mo.shape[2]
    tm = _pick(m, MM_TM, 16)
    tn = _pick(d, MM_TM, LANE)
    nc = d // tn

    def body(a_ref, b_ref, za_ref, zb_ref, ba_ref, bb_ref, yc_ref, ym_ref, mix_ref):
        ym = jnp.dot(a_ref[...], b_ref[0], preferred_element_type=F32)
        ga = jax.nn.sigmoid(za_ref[...] + ba_ref[...])
        gb = jax.nn.sigmoid(zb_ref[...] + bb_ref[...])
        ym_ref[...] = ym.astype(BF)
        mix_ref[...] = (ga * yc_ref[...] + gb * ym).astype(BF)

    tile = pl.BlockSpec((tm, tn), lambda i, j: (i, j))
    out = jax.ShapeDtypeStruct((m, d), BF)
    return pl.pallas_call(
        body, name="mm_y_mla_gate", grid=(m // tm, nc),
        in_specs=[pl.BlockSpec((tm, k), lambda i, j: (i, 0)), pl.BlockSpec((1, k, tn), lambda i, j: (0, 0, j)),
                  tile, pl.BlockSpec((tm, tn), lambda i, j: (i, nc + j)),
                  pl.BlockSpec((1, tn), lambda i, j: (0, j)), pl.BlockSpec((1, tn), lambda i, j: (0, nc + j)), tile],
        out_specs=[tile, tile], out_shape=[out, out],
        compiler_params=_cp("parallel", "parallel"),
    )(o, w_mo, z_g, z_g, b_gate, b_gate, yc)


def _d_mix_gate(d_h1, w_oo, z_g, b_gate, yc, ym):
    m, n = d_h1.shape
    d = w_oo.shape[1]
    tm = _pick(m, MM_TM, 16)
    tn = _pick(d, COL_TILE, LANE)
    nc = d // tn

    def body(a_ref, b_ref, za_ref, zb_ref, ba_ref, bb_ref, yc_ref, ym_ref,
             dza_ref, dzb_ref, dyc_ref, dym_ref, dba_ref, dbb_ref):
        i = pl.program_id(1)
        dm = lax.dot_general(a_ref[...], b_ref[0], (((1,), (1,)), ((), ())), preferred_element_type=F32)
        ga = jax.nn.sigmoid(za_ref[...] + ba_ref[...])
        gb = jax.nn.sigmoid(zb_ref[...] + bb_ref[...])
        dza = dm * yc_ref[...] * (ga * (1.0 - ga))
        dzb = dm * ym_ref[...] * (gb * (1.0 - gb))
        dza_ref[...] = dza.astype(BF)
        dzb_ref[...] = dzb.astype(BF)
        dyc_ref[...] = (dm * ga).astype(BF)
        dym_ref[...] = (dm * gb).astype(BF)

        @pl.when(i == 0)
        def _():
            dba_ref[...] = jnp.zeros_like(dba_ref)
            dbb_ref[...] = jnp.zeros_like(dbb_ref)

        dba_ref[...] += _rows8([jnp.sum(dza, axis=0, keepdims=True)], tn)
        dbb_ref[...] += _rows8([jnp.sum(dzb, axis=0, keepdims=True)], tn)

    tile = pl.BlockSpec((tm, tn), lambda j, i: (i, j))
    act = jax.ShapeDtypeStruct((m, d), BF)
    bsh = jax.ShapeDtypeStruct((SUB, d), F32)
    return pl.pallas_call(
        body, name="mm_d_mix_gate", grid=(nc, m // tm),
        in_specs=[pl.BlockSpec((tm, n), lambda j, i: (i, 0)), pl.BlockSpec((1, tn, n), lambda j, i: (0, j, 0)),
                  tile, pl.BlockSpec((tm, tn), lambda j, i: (i, nc + j)),
                  pl.BlockSpec((1, tn), lambda j, i: (0, j)), pl.BlockSpec((1, tn), lambda j, i: (0, nc + j)),
                  tile, tile],
        out_specs=[tile] * 4 + [pl.BlockSpec((SUB, tn), lambda j, i: (0, j))] * 2,
        out_shape=[act, act, act, act, bsh, bsh],
        compiler_params=_cp("parallel", "arbitrary"),
    )(d_h1, w_oo, z_g, z_g, b_gate, b_gate, yc, ym)


def _lay(v):
    z = jnp.zeros(v.shape[:-1] + (HALF,), v.dtype)
    return jnp.concatenate([v[..., :HALF], z, v[..., HALF:], z], axis=-1)


def _unlay(v):
    return jnp.concatenate([v[..., :HALF], v[..., 2 * HALF:3 * HALF]], axis=-1)


def _lay_rows(v):
    z = jnp.zeros((HALF,) + v.shape[1:], v.dtype)
    return jnp.concatenate([v[:HALF], z, v[HALF:], z], axis=0)


def _rope_tables(positions):
    s = positions.shape[0]
    tr = _pick(s, ROW_TILE, 8)
    inv_freq = ROPE_THETA ** (-jnp.arange(0, ROPE, 2, dtype=F32) / ROPE)
    consts = jnp.stack([_lay(jnp.concatenate([inv_freq, inv_freq])),
                        _lay(jnp.ones((ROPE,), F32)),
                        _lay(jnp.concatenate([-jnp.ones((HALF,), F32), jnp.ones((HALF,), F32)]))])
    consts = _pad8(consts)

    def body(p_ref, c_ref, cos_ref, sin_ref):
        ang = p_ref[...].astype(F32) * c_ref[0:1, :]
        cos_ref[...] = jnp.cos(ang) * c_ref[1:2, :]
        sin_ref[...] = jnp.sin(ang) * c_ref[2:3, :]

    tab = jax.ShapeDtypeStruct((s, LANE), F32)
    return pl.pallas_call(
        body, name="rope_tables", grid=(s // tr,),
        in_specs=[pl.BlockSpec((tr, 1), lambda i: (i, 0)), pl.BlockSpec((SUB, LANE), lambda i: (0, 0))],
        out_specs=[pl.BlockSpec((tr, LANE), lambda i: (i, 0))] * 2,
        out_shape=[tab, tab],
        compiler_params=_cp("parallel"),
    )(positions, consts)


def _lane_sum(p):
    return jnp.sum(p, axis=-1, keepdims=True)


def _rope(t, cos, sin):
    return t * cos + pltpu.roll(t, 2 * HALF, axis=1) * sin


def _rope_t(d, cos, sin):
    return d * cos + pltpu.roll(d * sin, 2 * HALF, axis=1)


def _head_fwd(q_raw, kv_raw, z_a, kr_blk, cos, sin, gains, heads):
    s = q_raw.shape[0]
    tr = _pick(s, HEAD_ROW_TILE, 16)
    hw = heads * LANE

    def body(q_ref, kv_ref, kr_ref, cos_ref, sin_ref, g_ref, qo_ref, ko_ref, vo_ref):
        cosv = cos_ref[...]
        sinv = sin_ref[...]
        krv = kr_ref[...]
        kr_sq = krv * krv
        for h in range(heads):
            lo = h * LANE
            qn = q_ref[:, lo:lo + LANE]
            qr = q_ref[:, hw + lo:hw + lo + LANE]
            r = lax.rsqrt(_lane_sum(qn * qn + qr * qr) / HEAD_QK + NORM_EPS)
            qo_ref[:, 2 * lo:2 * lo + LANE] = ((qn * r) * g_ref[0:1, :]).astype(BF)
            qo_ref[:, 2 * lo + LANE:2 * lo + 2 * LANE] = _rope((qr * r) * g_ref[1:2, :], cosv, sinv).astype(BF)
            kn = kv_ref[:, 2 * lo:2 * lo + LANE]
            r = lax.rsqrt(_lane_sum(kn * kn + kr_sq) / HEAD_QK + NORM_EPS)
            ko_ref[:, 2 * lo:2 * lo + LANE] = ((kn * r) * g_ref[2:3, :]).astype(BF)
            ko_ref[:, 2 * lo + LANE:2 * lo + 2 * LANE] = _rope((krv * r) * g_ref[3:4, :], cosv, sinv).astype(BF)
            vo_ref[:, lo:lo + LANE] = kv_ref[:, 2 * lo + LANE:2 * lo + 2 * LANE].astype(BF)

    row = lambda w: pl.BlockSpec((tr, w), lambda i: (i, 0))
    return pl.pallas_call(
        body, name="head_fwd", grid=(s // tr,),
        in_specs=[row(2 * hw), row(2 * hw), pl.BlockSpec((tr, LANE), lambda i: (i, kr_blk)),
                  row(LANE), row(LANE), pl.BlockSpec((SUB, LANE), lambda i: (0, 0))],
        out_specs=[row(2 * hw), row(2 * hw), row(hw)],
        out_shape=[jax.ShapeDtypeStruct((s, 2 * hw), BF), jax.ShapeDtypeStruct((s, 2 * hw), BF),
                   jax.ShapeDtypeStruct((s, hw), BF)],
        compiler_params=_cp("parallel"),
    )(q_raw, kv_raw, z_a, cos, sin, gains)


def _head_bwd(q_raw, kv_raw, z_a, kr_blk, cos, sin, gains, dq_att, dk_att, dv, heads):
    s = q_raw.shape[0]
    tr = _pick(s, HEAD_ROW_TILE_BWD, 16)
    hw = heads * LANE

    def body(q_ref, kv_ref, kr_ref, cos_ref, sin_ref, g_ref, dq_ref, dk_ref, dv_ref,
             dqr_ref, dkv_ref, dkr_ref, dg_ref):
        i = pl.program_id(0)
        cosv = cos_ref[...]
        sinv = sin_ref[...]
        krv = kr_ref[...]
        kr_sq = krv * krv
        dkr = jnp.zeros((tr, LANE), F32)
        dgs = [jnp.zeros((1, LANE), F32) for _ in range(4)]

        def norm_bwd(xn, xr, sq, dn_out, dr_out, gn, gr):
            r = lax.rsqrt(_lane_sum(sq) / HEAD_QK + NORM_EPS)
            nn = xn * r
            nr = xr * r
            dt = _rope_t(dr_out, cosv, sinv)
            dnn = dn_out * gn
            dnr = dt * gr
            mean = _lane_sum(dnn * nn + dnr * nr) / HEAD_QK
            return (r * (dnn - nn * mean), r * (dnr - nr * mean),
                    jnp.sum(dn_out * nn, axis=0, keepdims=True), jnp.sum(dt * nr, axis=0, keepdims=True))

        for h in range(heads):
            lo = h * LANE
            qn = q_ref[:, lo:lo + LANE]
            qr = q_ref[:, hw + lo:hw + lo + LANE]
            dxn, dxr, g0, g1 = norm_bwd(qn, qr, qn * qn + qr * qr, dq_ref[:, 2 * lo:2 * lo + LANE],
                                        dq_ref[:, 2 * lo + LANE:2 * lo + 2 * LANE], g_ref[0:1, :], g_ref[1:2, :])
            dqr_ref[:, lo:lo + LANE] = dxn.astype(BF)
            dqr_ref[:, hw + lo:hw + lo + LANE] = dxr.astype(BF)
            kn = kv_ref[:, 2 * lo:2 * lo + LANE]
            dxn, dxr, g2, g3 = norm_bwd(kn, krv, kn * kn + kr_sq, dk_ref[:, 2 * lo:2 * lo + LANE],
                                        dk_ref[:, 2 * lo + LANE:2 * lo + 2 * LANE], g_ref[2:3, :], g_ref[3:4, :])
            dkv_ref[:, 2 * lo:2 * lo + LANE] = dxn.astype(BF)
            dkv_ref[:, 2 * lo + LANE:2 * lo + 2 * LANE] = dv_ref[:, lo:lo + LANE].astype(BF)
            dkr = dkr + dxr
            dgs = [a + b for a, b in zip(dgs, (g0, g1, g2, g3))]
        dkr_ref[...] = dkr

        @pl.when(i == 0)
        def _():
            dg_ref[...] = jnp.zeros_like(dg_ref)

        dg_ref[...] += _rows8(dgs, LANE)

    row = lambda w: pl.BlockSpec((tr, w), lambda i: (i, 0))
    return pl.pallas_call(
        body, name="head_bwd", grid=(s // tr,),
        in_specs=[row(2 * hw), row(2 * hw), pl.BlockSpec((tr, LANE), lambda i: (i, kr_blk)),
                  row(LANE), row(LANE), pl.BlockSpec((SUB, LANE), lambda i: (0, 0)),
                  row(2 * hw), row(2 * hw), row(hw)],
        out_specs=[row(2 * hw), row(2 * hw), row(LANE), pl.BlockSpec((SUB, LANE), lambda i: (0, 0))],
        out_shape=[jax.ShapeDtypeStruct((s, 2 * hw), BF), jax.ShapeDtypeStruct((s, 2 * hw), BF),
                   jax.ShapeDtypeStruct((s, LANE), F32), jax.ShapeDtypeStruct((SUB, LANE), F32)],
        compiler_params=_cp("arbitrary"),
    )(q_raw, kv_raw, z_a, cos, sin, gains, dq_att, dk_att, dv)


def _causal_mask(nrows, ncols, row0):
    rows = lax.broadcasted_iota(jnp.int32, (nrows, ncols), 0) + row0
    cols = lax.broadcasted_iota(jnp.int32, (nrows, ncols), 1)
    return cols <= rows


def _causal_steps(nt, q_major):
    pairs = ([(i, j) for i in range(nt) for j in range(i + 1)] if q_major
             else [(i, j) for j in range(nt) for i in range(j, nt)])
    return (jnp.array([p[0] for p in pairs], jnp.int32), jnp.array([p[1] for p in pairs], jnp.int32))


def _attn_fwd(q_att, k_att, v, heads):
    s = q_att.shape[0]
    t = _pick(s, ATTN_TILE_FWD, LANE)
    nt = s // t
    th = t // 2
    scale = HEAD_QK ** -0.5
    qi, kj = _causal_steps(nt, True)

    def body(qi_ref, kj_ref, q_ref, k_ref, v_ref, o_ref, ob_ref, lse_ref, m_s, l_s, acc_s):
        st = pl.program_id(1)
        i = qi_ref[st]
        j = kj_ref[st]

        @pl.when(j == 0)
        def _():
            m_s[...] = jnp.full_like(m_s, NEG_INF)
            l_s[...] = jnp.zeros_like(l_s)
            acc_s[...] = jnp.zeros_like(acc_s)

        def update(rows, ncol, masked):
            sc = lax.dot_general(q_ref[rows, :], k_ref[0:ncol, :], (((1,), (1,)), ((), ())),
                                 preferred_element_type=F32) * scale
            if masked:
                sc = jnp.where(_causal_mask(rows.stop - rows.start, ncol, rows.start), sc, NEG_INF)
            m_prev = m_s[rows, :]
            m_new = jnp.maximum(m_prev, jnp.max(sc, axis=-1, keepdims=True))
            alpha = jnp.exp(m_prev - m_new)
            p = jnp.exp(sc - jnp.tile(m_new, (1, ncol // LANE)))
            l_s[rows, :] = alpha * l_s[rows, :] + jnp.sum(p, axis=-1, keepdims=True)
            acc_s[rows, :] = alpha * acc_s[rows, :] + jnp.dot(p.astype(BF), v_ref[0:ncol, :],
                                                              preferred_element_type=F32)
            m_s[rows, :] = m_new

        @pl.when(j < i)
        def _():
            update(slice(0, t), t, False)

        @pl.when(j == i)
        def _():
            update(slice(0, th), th, True)
            update(slice(th, t), t, True)
            o = acc_s[...] / l_s[...]
            o_ref[...] = o
            ob_ref[...] = o.astype(BF)
            lse_ref[...] = (m_s[...] + jnp.log(l_s[...]))[:, 0:1]

    q_idx = lambda h, st, qi_r, kj_r: (qi_r[st], h)
    kv_idx = lambda h, st, qi_r, kj_r: (kj_r[st], h)
    return pl.pallas_call(
        body, name="attn_fwd",
        grid_spec=pltpu.PrefetchScalarGridSpec(
            num_scalar_prefetch=2, grid=(heads, qi.shape[0]),
            in_specs=[pl.BlockSpec((t, 2 * LANE), q_idx), pl.BlockSpec((t, 2 * LANE), kv_idx),
                      pl.BlockSpec((t, LANE), kv_idx)],
            out_specs=[pl.BlockSpec((t, LANE), q_idx), pl.BlockSpec((t, LANE), q_idx),
                       pl.BlockSpec((None, t, 1), lambda h, st, qi_r, kj_r: (h, qi_r[st], 0))],
            scratch_shapes=[pltpu.VMEM((t, LANE), F32), pltpu.VMEM((t, LANE), F32), pltpu.VMEM((t, LANE), F32)]),
        out_shape=[jax.ShapeDtypeStruct((s, heads * LANE), F32), jax.ShapeDtypeStruct((s, heads * LANE), BF),
                   jax.ShapeDtypeStruct((heads, s, 1), F32)],
        compiler_params=_cp("parallel", "arbitrary"),
    )(qi, kj, q_att, k_att, v)


def _attn_bwd(q_att, k_att, v, o, lse, d_o, heads, dep=None):
    s = q_att.shape[0]
    t = _pick(s, ATTN_TILE, LANE)
    nt = s // t
    th = t // 2
    scale = HEAD_QK ** -0.5
    qi, kj = _causal_steps(nt, False)

    def body(qi_ref, kj_ref, q_ref, k_ref, v_ref, do_ref, o_ref, lse_ref, *rest):
        dq_ref, dk_ref, dv_ref, dk_s, dv_s = rest[-5:]
        st = pl.program_id(1)
        i = qi_ref[st]
        j = kj_ref[st]

        @pl.when(st == 0)
        def _():
            dq_ref[...] = jnp.zeros_like(dq_ref)

        @pl.when(i == j)
        def _():
            dk_s[...] = jnp.zeros_like(dk_s)
            dv_s[...] = jnp.zeros_like(dv_s)

        def update(rows, ncol, masked):
            nrow = rows.stop - rows.start
            q = q_ref[rows, :]
            k = k_ref[0:ncol, :]
            do = do_ref[rows, :]
            sc = lax.dot_general(q, k, (((1,), (1,)), ((), ())), preferred_element_type=F32) * scale
            if masked:
                sc = jnp.where(_causal_mask(nrow, ncol, rows.start), sc, NEG_INF)
            p = jnp.exp(sc - lse_ref[rows, :])
            dp = lax.dot_general(do, v_ref[0:ncol, :], (((1,), (1,)), ((), ())), preferred_element_type=F32)
            delta = jnp.sum(do.astype(F32) * o_ref[rows, :], axis=-1, keepdims=True)
            ds = (p * (dp - delta) * scale).astype(BF)
            dv_s[0:ncol, :] += lax.dot_general(p.astype(BF), do, (((0,), (0,)), ((), ())),
                                               preferred_element_type=F32)
            dk_s[0:ncol, :] += lax.dot_general(ds, q, (((0,), (0,)), ((), ())), preferred_element_type=F32)
            out_rows = pl.ds(pl.multiple_of(i * t + rows.start, nrow), nrow)
            dq_ref[out_rows, :] += jnp.dot(ds, k, preferred_element_type=F32)

        @pl.when(i > j)
        def _():
            update(slice(0, t), t, False)

        @pl.when(i == j)
        def _():
            update(slice(0, th), th, True)
            update(slice(th, t), t, True)

        @pl.when(i == nt - 1)
        def _():
            dk_ref[...] = dk_s[...].astype(BF)
            dv_ref[...] = dv_s[...].astype(BF)

    q_idx = lambda h, st, qi_r, kj_r: (qi_r[st], h)
    kv_idx = lambda h, st, qi_r, kj_r: (kj_r[st], h)
    in_specs = [pl.BlockSpec((t, 2 * LANE), q_idx), pl.BlockSpec((t, 2 * LANE), kv_idx),
                pl.BlockSpec((t, LANE), kv_idx), pl.BlockSpec((t, LANE), q_idx), pl.BlockSpec((t, LANE), q_idx),
                pl.BlockSpec((None, t, 1), lambda h, st, qi_r, kj_r: (h, qi_r[st], 0))]
    args = [q_att, k_att, v, d_o, o, lse]
    if dep is not None:
        in_specs.append(ANY)
        args.append(dep)
    return pl.pallas_call(
        body, name="attn_bwd",
        grid_spec=pltpu.PrefetchScalarGridSpec(
            num_scalar_prefetch=2, grid=(heads, qi.shape[0]),
            in_specs=in_specs,
            out_specs=[pl.BlockSpec((s, 2 * LANE), lambda h, st, qi_r, kj_r: (0, h)),
                       pl.BlockSpec((t, 2 * LANE), kv_idx), pl.BlockSpec((t, LANE), kv_idx)],
            scratch_shapes=[pltpu.VMEM((t, 2 * LANE), F32), pltpu.VMEM((t, LANE), F32)]),
        out_shape=[jax.ShapeDtypeStruct((s, heads * 2 * LANE), F32),
                   jax.ShapeDtypeStruct((s, heads * 2 * LANE), BF),
                   jax.ShapeDtypeStruct((s, heads * LANE), BF)],
        compiler_params=_cp("parallel", "arbitrary"),
    )(qi, kj, *args)


def _sum_parts(parts, name):
    n, r, c = parts.shape
    tr = _pick(r, 512, 8)

    def body(p_ref, o_ref):
        g = p_ref[0].astype(F32)
        for k in range(1, n):
            g = g + p_ref[k].astype(F32)
        o_ref[...] = g

    return pl.pallas_call(
        body, name=name, grid=(r // tr,),
        in_specs=[pl.BlockSpec((n, tr, c), lambda i: (0, i, 0))],
        out_specs=pl.BlockSpec((tr, c), lambda i: (i, 0)),
        out_shape=jax.ShapeDtypeStruct((r, c), F32),
        compiler_params=_cp("parallel"),
    )(parts)


def _adamw(parts, w, m, v, name, by_cols=False):
    n, rp, c = parts.shape
    r = w.shape[0]
    assert by_cols or rp == r
    tr, tc = (r, _pick(c, 256, LANE)) if by_cols else (_pick(r, 256, 16 if r % 16 == 0 else 8), c)

    def body(p_ref, w_ref, m_ref, v_ref, g_ref, d_ref, mo_ref, vo_ref):
        g = p_ref[0].astype(F32)
        for k in range(1, n):
            g = g + p_ref[k].astype(F32)
        g = g[:r] if by_cols else g
        m_new = ADAM_B1 * m_ref[...] + (1.0 - ADAM_B1) * g
        v_new = ADAM_B2 * v_ref[...] + (1.0 - ADAM_B2) * jnp.square(g)
        m_hat = m_new / (1.0 - ADAM_B1 ** ADAM_STEP)
        v_hat = v_new / (1.0 - ADAM_B2 ** ADAM_STEP)
        g_ref[...] = g
        d_ref[...] = -ADAM_LR * (m_hat / (jnp.sqrt(v_hat) + ADAM_EPS) + ADAM_WD * w_ref[...])
        mo_ref[...] = m_new
        vo_ref[...] = v_new

    idx = (lambda i: (0, i)) if by_cols else (lambda i: (i, 0))
    spec = pl.BlockSpec((tr, tc), idx)
    sh = jax.ShapeDtypeStruct((r, c), F32)
    return pl.pallas_call(
        body, name=name, grid=(c // tc if by_cols else r // tr,),
        in_specs=[pl.BlockSpec((n, rp if by_cols else tr, tc), lambda i: (0,) + idx(i)), spec, spec, spec],
        out_specs=[spec] * 4, out_shape=[sh] * 4,
        compiler_params=_cp("parallel"),
    )(parts, w, m, v)


def _place():
    x, y, c = lax.axis_index("x"), lax.axis_index("y"), lax.axis_index("c")
    chips = [(1 - x, y), (x, 1 - y), (1 - x, 1 - y)]
    return x, y, c, chips


def _all_gather(shards, name, dep=None):
    n = len(shards)
    deps = [] if dep is None else list(dep)

    def body(*refs):
        ins, outs = refs[:n], refs[n + len(deps):2 * n + len(deps)]
        send_sems, recv_sems, local_sems = refs[2 * n + len(deps):]
        x, y, c, chips = _place()
        me, sibling = (x, y, c), (x, y, 1 - c)

        def slot(w, p):
            return outs[w].at[4 * p[0] + 2 * p[1] + p[2]]

        def copy(w, k, block, to, src=None):
            return pltpu.make_async_remote_copy(
                src_ref=slot(w, block) if src is None else src, dst_ref=slot(w, block),
                send_sem=send_sems.at[w, k], recv_sem=recv_sems.at[w, k], device_id=to, device_id_type=MESH)

        first = []
        for w in range(n):
            first += [copy(w, 1 + j, me, (*chip, c), src=ins[w]) for j, chip in enumerate(chips)]
            first.append(copy(w, 0, me, sibling, src=ins[w]))
        for cp in first:
            cp.start()
        mine = [pltpu.make_async_copy(ins[w], slot(w, me), local_sems.at[w]) for w in range(n)]
        for cp in mine:
            cp.start()
        passed = []
        for w in range(n):
            for j, chip in enumerate(chips):
                copy(w, 1 + j, (*chip, c), me).wait_recv()
                cp = copy(w, 4 + j, (*chip, c), sibling)
                cp.start()
                passed.append(cp)
        for w in range(n):
            copy(w, 0, sibling, me).wait_recv()
            for j, chip in enumerate(chips):
                copy(w, 4 + j, (*chip, 1 - c), me).wait_recv()
        for cp in first + passed:
            cp.wait_send()
        for cp in mine:
            cp.wait()

    return pl.pallas_call(
        body, name=name,
        in_specs=[ANY] * (n + len(deps)), out_specs=[ANY] * n,
        out_shape=[jax.ShapeDtypeStruct((N_DEV,) + a.shape, a.dtype) for a in shards],
        scratch_shapes=[pltpu.SemaphoreType.DMA((n, 7)), pltpu.SemaphoreType.DMA((n, 7)),
                        pltpu.SemaphoreType.DMA((n,))],
    )(*shards, *deps)


HBM = pl.BlockSpec(memory_space=pltpu.HBM)
SEM = pl.BlockSpec(memory_space=pltpu.SEMAPHORE)
EFFECT = pltpu.SideEffectType.DATAFLOW_SIDE_EFFECTING
PEERS = [(dx, dy, dc) for dx in (1, 0) for dy in (1, 0) for dc in (0, 1) if (dx, dy, dc) != (0, 0, 0)]


def _peer(x, y, c, flip):
    dx, dy, dc = flip
    return (1 - x if dx else x, 1 - y if dy else y, 1 - c if dc else c)


def _exchange_copies(srcs, lands, send, recv, loc, gather):
    x, y, c, _ = _place()
    me = 4 * x + 2 * y + c
    remote, local = [], []
    for w in range(len(srcs)):
        for k, flip in enumerate(PEERS):
            px, py, pc = _peer(x, y, c, flip)
            src = srcs[w] if gather else srcs[w].at[4 * px + 2 * py + pc]
            remote.append(pltpu.make_async_remote_copy(
                src_ref=src, dst_ref=lands[w].at[me], send_sem=send[w].at[k], recv_sem=recv[w].at[k],
                device_id=(px, py, pc), device_id_type=MESH))
        local.append(pltpu.make_async_copy(srcs[w] if gather else srcs[w].at[me], lands[w].at[me], loc[w]))
    return remote, local


class _Exchange:
    def __init__(self, srcs, lands, send, recv, loc, token, gather):
        self.srcs, self.lands, self.send, self.recv, self.loc = srcs, lands, send, recv, loc
        self.token, self.gather = token, gather


def _exchange_start(srcs, gather, name, dep=None):
    n = len(srcs)
    deps = [] if dep is None else [dep]
    land_shapes = [((N_DEV,) + a.shape) if gather else a.shape for a in srcs]
    lands = [pltpu.with_memory_space_constraint(lax.empty(sh, a.dtype), pltpu.HBM) for sh, a in zip(land_shapes, srcs)]
    srcs = [pltpu.with_memory_space_constraint(a, pltpu.HBM) for a in srcs]

    def body(*refs):
        src_refs, land_refs = refs[:n], refs[n:2 * n]
        outs = refs[2 * n + len(deps):]
        send, recv, loc = outs[:n], outs[n:2 * n], outs[2 * n:3 * n]
        token = outs[-1]
        remote, local = _exchange_copies(src_refs, land_refs, send, recv, loc, gather)
        for cp in remote + local:
            cp.start()
        token[...] = jnp.zeros_like(token)

    out_shape = ([pltpu.SemaphoreType.DMA((len(PEERS),))] * (2 * n) + [pltpu.SemaphoreType.DMA(())] * n
                 + [pltpu.HBM(a.shape, a.dtype) for a in srcs] + [pltpu.HBM(a.shape, a.dtype) for a in lands]
                 + [jax.ShapeDtypeStruct((SUB, LANE), F32)])
    res = pl.pallas_call(
        body, name=name, out_shape=out_shape,
        in_specs=[HBM] * (2 * n) + [ANY] * len(deps),
        out_specs=[SEM] * (3 * n) + [HBM] * (2 * n) + [pl.BlockSpec(memory_space=pltpu.VMEM)],
        input_output_aliases={i: 3 * n + i for i in range(2 * n)},
        compiler_params=pltpu.CompilerParams(has_side_effects=EFFECT),
    )(*srcs, *lands, *deps)
    return _Exchange(res[3 * n:4 * n], res[4 * n:5 * n], res[:n], res[n:2 * n], res[2 * n:3 * n], res[-1], gather)


def _exchange_wait(ex, idxs, after, name):
    n = len(idxs)
    srcs = [ex.srcs[i] for i in idxs]
    lands = [ex.lands[i] for i in idxs]
    sems = [ex.send[i] for i in idxs] + [ex.recv[i] for i in idxs] + [ex.loc[i] for i in idxs]
    gather = ex.gather

    def body(*refs):
        src_refs, land_refs = refs[:n], refs[n:2 * n]
        send, recv, loc = refs[2 * n:3 * n], refs[3 * n:4 * n], refs[4 * n:5 * n]
        remote, local = _exchange_copies(src_refs, land_refs, send, recv, loc, gather)
        for cp in remote:
            cp.wait_send()
            cp.wait_recv()
        for cp in local:
            cp.wait()

    res = pl.pallas_call(
        body, name=name,
        out_shape=[pltpu.HBM(a.shape, a.dtype) for a in srcs] + [pltpu.HBM(a.shape, a.dtype) for a in lands],
        in_specs=[HBM] * (2 * n) + [SEM] * (3 * n) + [ANY],
        out_specs=[HBM] * (2 * n),
        input_output_aliases={i: i for i in range(2 * n)},
        compiler_params=pltpu.CompilerParams(has_side_effects=EFFECT),
    )(*srcs, *lands, *sems, after)
    return res[n:]


def _after(token, a):
    return a + token[0:1, 0:1].astype(a.dtype)


def _unblock(w3):
    nb, k, nbw = w3.shape
    return w3.transpose(1, 0, 2).reshape(k, nb * nbw)


def _block(w, nb):
    k, n = w.shape
    return w.reshape(k, nb, n // nb).transpose(1, 0, 2)


def kernel(x, positions, ln1_g, w_in, b_gate, conv_w, w_conv_out, q_a_g, w_q_b, kv_a_g, w_kv_b, q_norm_g, k_norm_g, w_mla_out, w_o, ln2_g, w_ffn_up, ffn_conv_w, ffn_conv_b, w_ffn_down, loss_target, m_ln1_g, m_w_in, m_b_gate, m_conv_w, m_w_conv_out, m_q_a_g, m_w_q_b, m_kv_a_g, m_w_kv_b, m_q_norm_g, m_k_norm_g, m_w_mla_out, m_w_o, m_ln2_g, m_w_ffn_up, m_ffn_conv_w, m_ffn_conv_b, m_w_ffn_down, v_ln1_g, v_w_in, v_b_gate, v_conv_w, v_w_conv_out, v_q_a_g, v_w_q_b, v_kv_a_g, v_w_kv_b, v_q_norm_g, v_k_norm_g, v_w_mla_out, v_w_o, v_ln2_g, v_w_ffn_up, v_ffn_conv_w, v_ffn_conv_b, v_w_ffn_down):
    s, d = x.shape[1], x.shape[2]
    conv = conv_w.shape[2] * N_DEV
    ql, kvl = q_a_g.shape[1], kv_a_g.shape[1]
    heads = w_q_b.shape[2] * N_DEV // HEAD_QK
    dff = w_ffn_down.shape[1] * N_DEV
    hw = heads * LANE
    conv3 = 3 * conv
    kr_off = conv3 + ql
    kv_off = -(-(kr_off + LANE) // kvl) * kvl
    wa = kv_off + kvl
    assert conv3 % ql == 0 and kr_off % LANE == 0
    xs = x[0]
    tgt = loss_target[0]
    pos = positions.reshape(s, 1)

    nin = w_in.shape[2]
    big = dict(w_in=w_in[0].T, w_conv_out=w_conv_out[0], w_q_b=w_q_b[0], w_kv_b=w_kv_b[0],
               w_mla_out=w_mla_out[0], w_o=w_o[0], w_ffn_up=w_ffn_up[0], w_ffn_down=w_ffn_down[0])
    names = list(big)
    rest = names[1:]
    first = _all_gather([big["w_in"].astype(BF), _pad8(conv_w[0]), _pad8(ffn_conv_w[0])], "gather_w_in")
    cw8 = _unblock(first[1])
    fcw8 = _unblock(first[2])
    ag = _exchange_start([big[k].astype(BF) for k in rest], True, "gather_rest_start", dep=first[1])

    def landed(keys, after, name):
        return _exchange_wait(ag, [rest.index(k) for k in keys], after, name)

    w_in_t = first[0].reshape(N_DEV * nin, d)
    g_off = kr_off + kvl + ROPE
    w_a_t = jnp.concatenate([w_in_t[:kr_off], _lay_rows(w_in_t[kr_off + kvl:g_off]),
                             jnp.zeros((kv_off - kr_off - LANE, d), BF), w_in_t[kr_off:kr_off + kvl]], axis=0)[None]
    w_g_t = w_in_t[g_off:][None]
    gains = _pad8(jnp.concatenate([q_norm_g[:, :NOPE], _lay(q_norm_g[:, NOPE:]),
                                   k_norm_g[:, :NOPE], _lay(k_norm_g[:, NOPE:])], axis=0))
    kr_blk = kr_off // LANE

    cos, sin = _rope_tables(pos)
    u1 = _rms_fwd(xs, _after(ag.token, ln1_g), d, 0, "rms1_fwd")
    z_a = _mm_nt(u1, w_a_t, "mm_z_a")
    z_g = _mm_nt(u1, w_g_t, "mm_z_g", out_dtype=BF)
    p = _conv_mix_fwd(z_a, cw8, conv)
    w_co, w_qb, w_kv = landed(["w_conv_out", "w_q_b", "w_kv_b"], p, "gather_wait_mixers")
    w_co = _unblock(w_co)[None]
    w_kv = _unblock(w_kv)[None]
    wq_full = _unblock(w_qb).reshape(ql, heads, HEAD_QK)
    w_q = jnp.concatenate([wq_full[:, :, :NOPE].reshape(ql, hw), _lay(wq_full[:, :, NOPE:]).reshape(ql, hw)],
                          axis=1)[None]
    yc = _mm_nn(p, w_co, "mm_y_conv", out_dtype=BF)
    qn = _rms_fwd(z_a, q_a_g, ql, conv3 // ql, "rms_q_fwd")
    kvn = _rms_fwd(z_a, kv_a_g, kvl, kv_off // kvl, "rms_kv_fwd")
    q_raw = _mm_nn(qn, w_q, "mm_q")
    kv_raw = _mm_nn(kvn, w_kv, "mm_kv")
    q_att, k_att, v_bf = _head_fwd(q_raw, kv_raw, z_a, kr_blk, cos, sin, gains, heads)
    o, o_bf, lse = _attn_fwd(q_att, k_att, v_bf, heads)
    w_mo, w_oo = landed(["w_mla_out", "w_o"], lse, "gather_wait_outs")
    w_mo = w_mo.reshape(1, hw, d)
    w_oo = w_oo.reshape(1, d, d)
    ym, mix = _mla_out_gate(o_bf, w_mo, z_g, b_gate, yc)
    h1 = _mm_nn(mix, w_oo, "mm_h1", add=xs)
    u2 = _rms_fwd(h1, ln2_g, d, 0, "rms2_fwd")
    w_up, = landed(["w_ffn_up"], u2, "gather_wait_ffn_up")
    a_pre = _mm_nn(u2, w_up, "mm_ffn_up")
    f = _ffn_act_fwd(a_pre, fcw8, ffn_conv_b, dff)
    w_dn, = landed(["w_ffn_down"], f, "gather_wait_ffn_down")
    w_dn = w_dn.reshape(1, dff, d)
    dy, dy_bf, loss_part = _mm_nn_loss(f, w_dn, h1, tgt, "mm_ffn_down_loss")

    g_dn = _mm_tn(f, dy_bf, 1, "mm_g_ffn_down").reshape(N_DEV, dff // N_DEV, d)
    rs_dn = _exchange_start([g_dn], False, "reduce_ffn_down_start")
    d_f = _mm_nt(dy_bf, w_dn, "mm_d_f", dep=rs_dn.token)
    d_xg, d_xu, dfw_g, dfw_u = _ffn_act_bwd(a_pre, d_f, fcw8, ffn_conv_b, dff)
    half = N_DEV // 2
    g_up = _mm_tn(u2, d_xg, half, "mm_g_ffn_up_gate", into=lax.empty((N_DEV, d, 2 * dff // N_DEV), BF))
    g_up = _mm_tn(u2, d_xu, half, "mm_g_ffn_up_up", into=g_up, blk0=half)
    rs_up = _exchange_start([g_up], False, "reduce_ffn_up_start")
    d_u2 = _mm_nt([d_xg, d_xu], w_up, "mm_d_u2", out_dtype=BF, dep=rs_up.token)
    d_h1, d_h1_bf, dg_ln2 = _rms_bwd(h1, d_u2, ln2_g, d, 0, "rms2_bwd", extra=dy, also_bf16=True)
    g_oo = _mm_tn(mix, d_h1_bf, 1, "mm_g_w_o").reshape(N_DEV, d // N_DEV, d)
    d_zga, d_zgb, d_yc, d_ym, dba, dbb = _d_mix_gate(d_h1_bf, w_oo, z_g, b_gate, yc, ym)
    g_co = _block(_mm_tn(p, d_yc, 1, "mm_g_conv_out")[0], N_DEV)
    g_mo = _mm_tn(o_bf, d_ym, 1, "mm_g_mla_out").reshape(N_DEV, hw // N_DEV, d)
    rs_mix = _exchange_start([g_oo, g_co, g_mo], False, "reduce_mixers_start")
    d_p = _mm_nt(d_yc, w_co, "mm_d_p", dep=rs_mix.token)
    d_o = _mm_nt(d_ym, w_mo, "mm_d_o", out_dtype=BF)
    d_zb, d_zc, d_zv, dcw = _conv_mix_bwd(z_a, d_p, cw8, conv)
    dq_att, dk_att, dv = _attn_bwd(q_att, k_att, v_bf, o, lse, d_o, heads, dep=rs_mix.token)
    d_q_raw, d_kv_raw, d_kr, dgains = _head_bwd(q_raw, kv_raw, z_a, kr_blk, cos, sin, gains, dq_att, dk_att, dv, heads)
    g_q2 = _mm_tn(qn, d_q_raw, 1, "mm_g_q")[0]
    g_qb = _block(jnp.concatenate([g_q2[:, :hw].reshape(ql, heads, NOPE),
                                   _unlay(g_q2[:, hw:].reshape(ql, heads, LANE))], axis=2).reshape(ql, heads * HEAD_QK), N_DEV)
    g_kv = _block(_mm_tn(kvn, d_kv_raw, 1, "mm_g_kv")[0], N_DEV)
    rs_qkv = _exchange_start([g_qb, g_kv], False, "reduce_qkv_start")
    d_qn = _mm_nt(d_q_raw, w_q, "mm_d_qn", dep=rs_qkv.token)
    d_kvn = _mm_nt(d_kv_raw, w_kv, "mm_d_kvn")
    d_ql, dg_qa = _rms_bwd(z_a, d_qn, q_a_g, ql, conv3 // ql, "rms_q_bwd", out_dtype=BF)
    d_kvl, dg_kva = _rms_bwd(z_a, d_kvn, kv_a_g, kvl, kv_off // kvl, "rms_kv_bwd", out_dtype=BF)
    d_z_a = jnp.concatenate([d_zb, d_zc, d_zv, d_ql, d_kr.astype(BF), jnp.zeros((s, kv_off - kr_off - LANE), BF),
                             d_kvl], axis=1)
    g_a = _mm_tn(d_z_a, u1, 1, "mm_g_w_a")[0]
    g_ga = _mm_tn(d_zga, u1, 1, "mm_g_w_ga")[0]
    g_gb = _mm_tn(d_zgb, u1, 1, "mm_g_w_gb")[0]
    g_in = jnp.concatenate([g_a[:kr_off], g_a[kv_off:kv_off + kvl], g_a[kr_off:kr_off + HALF],
                            g_a[kr_off + 2 * HALF:kr_off + 3 * HALF], g_ga, g_gb], axis=0).reshape(N_DEV, nin, d)
    rs_in = _exchange_start([g_in], False, "reduce_w_in_start")
    d_u1 = _mm_nn(d_z_a, w_a_t, "mm_d_u1_a", dep=rs_in.token)
    d_u1 = _mm_nn([d_zga, d_zgb], w_g_t, "mm_d_u1_g", add=d_u1)
    grad_x, dg_ln1 = _rms_bwd(xs, d_u1, ln1_g, d, 0, "rms1_bwd", extra=d_h1)

    summed = {}
    summed["w_ffn_down"], = _exchange_wait(rs_dn, [0], grad_x, "reduce_ffn_down_wait")
    summed["w_ffn_up"], = _exchange_wait(rs_up, [0], grad_x, "reduce_ffn_up_wait")
    summed["w_o"], summed["w_conv_out"], summed["w_mla_out"] = _exchange_wait(rs_mix, [0, 1, 2], grad_x, "reduce_mixers_wait")
    summed["w_q_b"], summed["w_kv_b"] = _exchange_wait(rs_qkv, [0, 1], grad_x, "reduce_qkv_wait")
    loc = locals()
    out = {}
    for k in rest:
        out[k] = _adamw(summed[k], big[k], loc["m_" + k][0], loc["v_" + k][0], "adamw_" + k)

    small = dict(ln1_g=dg_ln1[0:1], b_gate=jnp.concatenate([dba[0:1], dbb[0:1]], axis=1), q_a_g=dg_qa[0:1],
                 kv_a_g=dg_kva[0:1],
                 q_norm_g=jnp.concatenate([dgains[0:1], _unlay(dgains[1:2])], axis=1),
                 k_norm_g=jnp.concatenate([dgains[2:3], _unlay(dgains[3:4])], axis=1),
                 ln2_g=dg_ln2[0:1], ffn_conv_b=jnp.concatenate([dfw_g[3:4], dfw_u[3:4]], axis=1))
    small_names = list(small)
    extra = [dcw[0:3].reshape(1, -1), jnp.concatenate([dfw_g[0:3], dfw_u[0:3]], axis=1).reshape(1, -1),
             loss_part[0:1, 0:1]]
    flat = jnp.concatenate([small[k] for k in small_names] + extra, axis=1)
    n_flat = flat.shape[1]
    rows = -(-n_flat // (SUB * LANE)) * SUB
    flat = jnp.pad(flat, ((0, 0), (0, rows * LANE - n_flat))).reshape(rows, LANE)
    total = _sum_parts(_all_gather([flat], "gather_small", dep=[out[k][0] for k in rest])[0], "sum_small").reshape(1, rows * LANE)
    off = 0
    small_g = {}
    for k in small_names:
        small_g[k] = total[:, off:off + small[k].shape[1]]
        off += small[k].shape[1]
    me = 4 * lax.axis_index("x") + 2 * lax.axis_index("y") + lax.axis_index("c")
    cwn, fcwn = conv // N_DEV, 2 * dff // N_DEV
    g_cw = lax.dynamic_slice_in_dim(total[:, off:off + 3 * conv].reshape(3, conv), me * cwn, cwn, axis=1)
    off += 3 * conv
    g_fcw = lax.dynamic_slice_in_dim(total[:, off:off + 6 * dff].reshape(3, 2 * dff), me * fcwn, fcwn, axis=1)
    off += 6 * dff
    loss = total[0, off]

    summed["w_in"], = _exchange_wait(rs_in, [0], total, "reduce_w_in_wait")
    out["w_in"] = [r.T for r in _adamw(summed["w_in"], big["w_in"], m_w_in[0].T, v_w_in[0].T, "adamw_w_in",
                                       by_cols=True)]
    small_w = dict(ln1_g=ln1_g, b_gate=b_gate, q_a_g=q_a_g, kv_a_g=kv_a_g, q_norm_g=q_norm_g, k_norm_g=k_norm_g,
                   ln2_g=ln2_g, ffn_conv_b=ffn_conv_b, conv_w=conv_w[0].reshape(1, -1),
                   ffn_conv_w=ffn_conv_w[0].reshape(1, -1))
    small_g["conv_w"] = g_cw.reshape(1, -1)
    small_g["ffn_conv_w"] = g_fcw.reshape(1, -1)
    packed_names = list(small_w)

    def pack(get):
        vflat = jnp.concatenate([get(k).reshape(1, -1) for k in packed_names], axis=1)
        nr = -(-vflat.shape[1] // (SUB * LANE)) * SUB
        return jnp.pad(vflat, ((0, 0), (0, nr * LANE - vflat.shape[1])), constant_values=1.0).reshape(nr, LANE)

    res = _adamw(pack(lambda k: small_g[k])[None], pack(lambda k: small_w[k]), pack(lambda k: loc["m_" + k]),
                 pack(lambda k: loc["v_" + k]), "adamw_small")
    res = [r.reshape(1, -1) for r in res]
    off = 0
    for k in packed_names:
        shape = loc[k].shape
        size = small_w[k].shape[1]
        out[k] = [r[:, off:off + size].reshape(shape) for r in res]
        off += size
    for k in names:
        out[k] = [r[None] for r in out[k]]

    order = ["ln1_g", "w_in", "b_gate", "conv_w", "w_conv_out", "q_a_g", "w_q_b", "kv_a_g", "w_kv_b", "q_norm_g",
             "k_norm_g", "w_mla_out", "w_o", "ln2_g", "w_ffn_up", "ffn_conv_w", "ffn_conv_b", "w_ffn_down"]
    return (loss, grad_x[None], *[out[k][0] for k in order], *[out[k][1] for k in order],
            *[out[k][2] for k in order], *[out[k][3] for k in order])
```

```python
import functools

import jax
import jax.numpy as jnp
from jax import lax
from jax.experimental import pallas as pl
from jax.experimental.pallas import tpu as pltpu

BF = jnp.bfloat16
F32 = jnp.float32
MESH = pl.DeviceIdType.MESH
N_DEV = 8

NOPE = 128
ROPE = 64
HALF = ROPE // 2
HEAD_QK = NOPE + ROPE
HEAD_V = 128
LANE = 128
SUB = 8
NORM_EPS = 1e-6
NEG_INF = -1e30
ROPE_THETA = 10000.0
ADAM_LR = 0.001
ADAM_B1 = 0.9
ADAM_B2 = 0.999
ADAM_EPS = 1e-08
ADAM_WD = 0.01
ADAM_STEP = 10

VMEM_LIMIT = 52 * 1024 * 1024
MM_TM, MM_TN, MM_TK, MM_TS = 1024, 1536, 2048, 2048
ROW_TILE, ROW_TILE_BWD = 512, 256
HEAD_ROW_TILE, HEAD_ROW_TILE_BWD = 256, 128
COL_TILE = 512
ATTN_TILE = 1024
ATTN_TILE_FWD = 1024
ANY = pl.BlockSpec(memory_space=pl.ANY)


def _pick(n, target, mult):
    t = (min(n, target) // mult) * mult
    while t > 0:
        if n % t == 0:
            return t
        t -= mult
    raise ValueError(f"no tile for {n} (target {target}, multiple {mult})")


def _cp(*sem):
    return pltpu.CompilerParams(dimension_semantics=sem, vmem_limit_bytes=VMEM_LIMIT)


def _accumulate(kk, nk, acc, part, finish):
    if nk == 1:
        finish(part())
        return

    @pl.when(kk == 0)
    def _():
        acc[...] = part()

    @pl.when((kk > 0) & (kk < nk - 1))
    def _():
        acc[...] += part()

    @pl.when(kk == nk - 1)
    def _():
        finish(acc[...] + part())


def _mm_call(body, name, grid, in_specs, args, out_spec, out_shape, acc_shape, nk, dep):
    if dep is not None:
        in_specs = in_specs + [ANY]
        args = args + [dep]
    return pl.pallas_call(
        body, name=name, grid=grid, in_specs=in_specs, out_specs=out_spec, out_shape=out_shape,
        scratch_shapes=[pltpu.VMEM(acc_shape, F32)] if nk > 1 else [],
        compiler_params=_cp("parallel", "parallel", "arbitrary"),
    )(*args)


def _mm_nn_loss(a, b3, add, target, name):
    m, k = a.shape
    _, k2, n = b3.shape
    assert k == k2 and b3.shape[0] == 1
    tm = _pick(m, MM_TM, 16)
    tn = _pick(n, MM_TN, LANE)
    tk = _pick(k, MM_TK, LANE)
    nk = k // tk

    def body(a_ref, b_ref, c_ref, t_ref, dy_ref, dyb_ref, l_ref, acc):
        kk = pl.program_id(2)

        @pl.when((pl.program_id(0) == 0) & (pl.program_id(1) == 0) & (kk == 0))
        def _():
            l_ref[...] = jnp.zeros_like(l_ref)

        def part():
            return jnp.dot(a_ref[...].astype(BF), b_ref[0].astype(BF), preferred_element_type=F32)

        def finish(r):
            e = r + c_ref[...] - t_ref[...]
            dy_ref[...] = e / n
            dyb_ref[...] = (e / n).astype(BF)
            l_ref[...] += 0.5 * jnp.sum(jnp.sum(e * e, axis=-1, keepdims=True), axis=0, keepdims=True) / n

        _accumulate(kk, nk, acc, part, finish)

    tile = pl.BlockSpec((tm, tn), lambda i, j, kk: (i, j))
    return pl.pallas_call(
        body, name=name, grid=(m // tm, n // tn, nk),
        in_specs=[pl.BlockSpec((tm, tk), lambda i, j, kk: (i, kk)),
                  pl.BlockSpec((1, tk, tn), lambda i, j, kk: (0, kk, j)), tile, tile],
        out_specs=[tile, tile, pl.BlockSpec((SUB, LANE), lambda i, j, kk: (0, 0))],
        out_shape=[jax.ShapeDtypeStruct((m, n), F32), jax.ShapeDtypeStruct((m, n), BF),
                   jax.ShapeDtypeStruct((SUB, LANE), F32)],
        scratch_shapes=[pltpu.VMEM((tm, tn), F32)],
        compiler_params=_cp("arbitrary", "arbitrary", "arbitrary"),
    )(a, b3, add, target)


def _mm_nn(a, b3, name, add=None, out_dtype=F32, blk0=0, nblk=None, dep=None):
    pair = isinstance(a, (list, tuple))
    a_list = list(a) if pair else [a]
    m, ka = a_list[0].shape
    k = ka * len(a_list)
    nb_all, k2, nbw = b3.shape
    assert k == k2
    nblk = nb_all - blk0 if nblk is None else nblk
    n = nblk * nbw
    tm = _pick(m, MM_TM if k > MM_TM else 2 * MM_TM, 16)
    tn = _pick(nbw, MM_TN, LANE)
    tk = _pick(ka, MM_TK, LANE)
    per = nbw // tn
    nk = k // tk
    nka = ka // tk
    na_ops = len(a_list)

    def body(*refs):
        a_refs, b_ref = refs[:na_ops], refs[na_ops]
        c_ref = refs[na_ops + 1] if add is not None else None
        o_ref = refs[na_ops + 1 + (add is not None) + (dep is not None)]
        acc = refs[-1]
        kk = pl.program_id(2)

        def part():
            av = a_refs[0][...] if not pair else jnp.where(kk < nka, a_refs[0][...], a_refs[1][...])
            return jnp.dot(av.astype(BF), b_ref[...].astype(BF), preferred_element_type=F32)

        def finish(r):
            if add is not None:
                r = r + c_ref[...]
            o_ref[...] = r.astype(out_dtype)

        _accumulate(kk, nk, acc, part, finish)

    if pair:
        in_specs = [pl.BlockSpec((tm, tk), lambda i, j, kk: (i, jnp.minimum(kk, nka - 1))),
                    pl.BlockSpec((tm, tk), lambda i, j, kk: (i, jnp.maximum(kk - nka, 0)))]
    else:
        in_specs = [pl.BlockSpec((tm, tk), lambda i, j, kk: (i, kk))]
    in_specs.append(pl.BlockSpec((None, tk, tn), lambda i, j, kk: (blk0 + j // per, kk, j % per)))
    args = a_list + [b3]
    if add is not None:
        in_specs.append(pl.BlockSpec((tm, tn), lambda i, j, kk: (i, j)))
        args.append(add)
    return _mm_call(body, name, (m // tm, n // tn, nk), in_specs, args,
                    pl.BlockSpec((tm, tn), lambda i, j, kk: (i, j)), jax.ShapeDtypeStruct((m, n), out_dtype),
                    (tm, tn), nk, dep)


def _mm_nt(a, b3, name, add=None, out_dtype=F32, blk0=0, nblk=None, dep=None):
    pair = isinstance(a, (list, tuple))
    a_list = list(a) if pair else [a]
    m, na = a_list[0].shape
    n = na * len(a_list)
    nb_all, k, nbw = b3.shape
    nblk = nb_all - blk0 if nblk is None else nblk
    assert n == nblk * nbw and na % nbw == 0
    tm = _pick(m, 2 * MM_TM if k <= MM_TM and n <= MM_TK else MM_TM, 16)
    tk = _pick(nbw, MM_TK, LANE)
    per = nbw // tk
    nk = n // tk
    tn = _pick(k, MM_TN if nk <= 2 else 2 * MM_TM, LANE)
    nka = na // tk
    na_ops = len(a_list)

    def body(*refs):
        a_refs, b_ref = refs[:na_ops], refs[na_ops]
        c_ref = refs[na_ops + 1] if add is not None else None
        o_ref = refs[na_ops + 1 + (add is not None) + (dep is not None)]
        acc = refs[-1]
        kk = pl.program_id(2)

        def part():
            av = a_refs[0][...] if not pair else jnp.where(kk < nka, a_refs[0][...], a_refs[1][...])
            return lax.dot_general(av.astype(BF), b_ref[...].astype(BF),
                                   (((1,), (1,)), ((), ())), preferred_element_type=F32)

        def finish(r):
            if add is not None:
                r = r + c_ref[...]
            o_ref[...] = r.astype(out_dtype)

        _accumulate(kk, nk, acc, part, finish)

    if pair:
        in_specs = [pl.BlockSpec((tm, tk), lambda i, j, kk: (i, jnp.minimum(kk, nka - 1))),
                    pl.BlockSpec((tm, tk), lambda i, j, kk: (i, jnp.maximum(kk - nka, 0)))]
    else:
        in_specs = [pl.BlockSpec((tm, tk), lambda i, j, kk: (i, kk))]
    in_specs.append(pl.BlockSpec((None, tn, tk), lambda i, j, kk: (blk0 + kk // per, j, kk % per)))
    args = a_list + [b3]
    if add is not None:
        in_specs.append(pl.BlockSpec((tm, tn), lambda i, j, kk: (i, j)))
        args.append(add)
    return _mm_call(body, name, (m // tm, k // tn, nk), in_specs, args,
                    pl.BlockSpec((tm, tn), lambda i, j, kk: (i, j)), jax.ShapeDtypeStruct((m, k), out_dtype),
                    (tm, tn), nk, dep)


def _mm_tn(a, b, nblk, name, out_dtype=BF, dep=None, into=None, blk0=0):
    s, m = a.shape
    s2, n = b.shape
    assert s == s2 and n % nblk == 0 and (dep is None or into is None)
    nbw = n // nblk
    tm = _pick(m, MM_TN, LANE)
    tn = _pick(nbw, MM_TN, LANE)
    ts = _pick(s, MM_TS, LANE)
    per = nbw // tn
    ns = s // ts

    def body(*refs):
        a_ref, b_ref = refs[:2]
        o_ref = refs[2 + (dep is not None or into is not None)]
        acc = refs[-1]

        def part():
            return lax.dot_general(a_ref[...].astype(BF), b_ref[...].astype(BF),
                                   (((0,), (0,)), ((), ())), preferred_element_type=F32)

        def finish(r):
            o_ref[...] = r.astype(out_dtype)

        _accumulate(pl.program_id(2), ns, acc, part, finish)

    in_specs = [pl.BlockSpec((ts, tm), lambda i, j, ss: (ss, i)),
                pl.BlockSpec((ts, tn), lambda i, j, ss: (ss, j))]
    out_spec = pl.BlockSpec((None, tm, tn), lambda i, j, ss: (blk0 + j // per, i, j % per))
    if into is None:
        return _mm_call(body, name, (m // tm, n // tn, ns), in_specs, [a, b], out_spec,
                        jax.ShapeDtypeStruct((nblk, m, nbw), out_dtype), (tm, tn), ns, dep)
    assert into.shape[1:] == (m, nbw) and into.dtype == out_dtype
    return pl.pallas_call(
        body, name=name, grid=(m // tm, n // tn, ns), in_specs=in_specs + [ANY], out_specs=out_spec,
        out_shape=jax.ShapeDtypeStruct(into.shape, out_dtype), input_output_aliases={2: 0},
        scratch_shapes=[pltpu.VMEM((tm, tn), F32)] if ns > 1 else [],
        compiler_params=_cp("parallel", "parallel", "arbitrary"),
    )(a, b, into)


def _rows8(rows, width):
    idx = lax.broadcasted_iota(jnp.int32, (SUB, width), 0)
    out = jnp.zeros((SUB, width), F32)
    for r, v in enumerate(rows):
        out = jnp.where(idx == r, v, out)
    return out


def _rms_fwd(x, g, width, col_blk, name):
    s = x.shape[0]
    tr = _pick(s, ROW_TILE, 16)

    def body(x_ref, g_ref, u_ref):
        xv = x_ref[...]
        r = lax.rsqrt(jnp.mean(xv * xv, axis=-1, keepdims=True) + NORM_EPS)
        u_ref[...] = ((xv * r) * g_ref[...]).astype(BF)

    return pl.pallas_call(
        body, name=name, grid=(s // tr,),
        in_specs=[pl.BlockSpec((tr, width), lambda i: (i, col_blk)),
                  pl.BlockSpec((1, width), lambda i: (0, 0))],
        out_specs=pl.BlockSpec((tr, width), lambda i: (i, 0)),
        out_shape=jax.ShapeDtypeStruct((s, width), BF),
        compiler_params=_cp("parallel"),
    )(x, g)


def _rms_bwd(x, du, g, width, col_blk, name, extra=None, out_dtype=F32, also_bf16=False):
    s = x.shape[0]
    tr = _pick(s, ROW_TILE_BWD, 16)

    def body(*refs):
        x_ref, du_ref, g_ref = refs[:3]
        e_ref = refs[3] if extra is not None else None
        dx_ref = refs[3 + (extra is not None)]
        dxb_ref = refs[4 + (extra is not None)] if also_bf16 else None
        dg_ref = refs[-1]
        i = pl.program_id(0)
        xv = x_ref[...]
        duv = du_ref[...].astype(F32)
        r = lax.rsqrt(jnp.mean(xv * xv, axis=-1, keepdims=True) + NORM_EPS)
        nv = xv * r
        dn = duv * g_ref[...]
        dx = r * (dn - nv * jnp.mean(dn * nv, axis=-1, keepdims=True))
        if extra is not None:
            dx = dx + e_ref[...]
        dx_ref[...] = dx.astype(out_dtype)
        if also_bf16:
            dxb_ref[...] = dx.astype(BF)

        @pl.when(i == 0)
        def _():
            dg_ref[...] = jnp.zeros_like(dg_ref)

        dg_ref[...] += _rows8([jnp.sum(duv * nv, axis=0, keepdims=True)], width)

    in_specs = [pl.BlockSpec((tr, width), lambda i: (i, col_blk)),
                pl.BlockSpec((tr, width), lambda i: (i, 0)),
                pl.BlockSpec((1, width), lambda i: (0, 0))]
    args = [x, du, g]
    if extra is not None:
        in_specs.append(pl.BlockSpec((tr, width), lambda i: (i, 0)))
        args.append(extra)
    return pl.pallas_call(
        body, name=name, grid=(s // tr,),
        in_specs=in_specs,
        out_specs=[pl.BlockSpec((tr, width), lambda i: (i, 0))] * (1 + also_bf16)
        + [pl.BlockSpec((SUB, width), lambda i: (0, 0))],
        out_shape=[jax.ShapeDtypeStruct((s, width), out_dtype)] + [jax.ShapeDtypeStruct((s, width), BF)] * also_bf16
        + [jax.ShapeDtypeStruct((SUB, width), F32)],
        compiler_params=_cp("arbitrary"),
    )(*args)


def _down(cur, prev8, k):
    ext = jnp.concatenate([prev8, cur], axis=0)
    return pltpu.roll(ext, k, axis=0)[SUB:]


def _up(cur, next8, k):
    ext = jnp.concatenate([cur, next8], axis=0)
    return pltpu.roll(ext, ext.shape[0] - k, axis=0)[:cur.shape[0]]


def _lags(cur, prev8):
    return _down(cur, prev8, 1), _down(cur, prev8, 2)


def _conv3(w_ref, cur, prev8, lags=None):
    lag1, lag2 = _lags(cur, prev8) if lags is None else lags
    return w_ref[0:1, :] * lag2 + w_ref[1:2, :] * lag1 + w_ref[2:3, :] * cur


def _conv3_t(w_ref, cur, next8):
    return w_ref[2:3, :] * cur + w_ref[1:2, :] * _up(cur, next8, 1) + w_ref[0:1, :] * _up(cur, next8, 2)


def _spec_cur(tr, tc, c0):
    return pl.BlockSpec((tr, tc), lambda j, i: (i, c0 + j))


def _spec_prev(tr, tc, c0):
    return pl.BlockSpec((SUB, tc), lambda j, i: (jnp.maximum(i * (tr // SUB) - 1, 0), c0 + j))


def _spec_next(tr, tc, c0, s):
    return pl.BlockSpec((SUB, tc), lambda j, i: (jnp.minimum((i + 1) * (tr // SUB), s // SUB - 1), c0 + j))


def _spec_w(tc, c0):
    return pl.BlockSpec((SUB, tc), lambda j, i: (0, c0 + j))


def _pad8(w):
    return jnp.pad(w, ((0, SUB - w.shape[0]), (0, 0)))


def _conv_mix_fwd(z_a, cw8, conv):
    s = z_a.shape[0]
    tr = _pick(s, ROW_TILE, 16)
    tc = _pick(conv, COL_TILE, LANE)
    nc = conv // tc

    def body(zb_ref, zc_ref, zv_ref, zcp_ref, zvp_ref, w_ref, p_ref):
        i = pl.program_id(1)
        cv = zc_ref[...] * zv_ref[...]
        cvp = jnp.where(i > 0, zcp_ref[...] * zvp_ref[...], 0.0)
        p_ref[...] = (zb_ref[...] * _conv3(w_ref, cv, cvp)).astype(BF)

    return pl.pallas_call(
        body, name="conv_mix_fwd", grid=(nc, s // tr),
        in_specs=[_spec_cur(tr, tc, 0), _spec_cur(tr, tc, nc), _spec_cur(tr, tc, 2 * nc),
                  _spec_prev(tr, tc, nc), _spec_prev(tr, tc, 2 * nc), _spec_w(tc, 0)],
        out_specs=_spec_cur(tr, tc, 0),
        out_shape=jax.ShapeDtypeStruct((s, conv), BF),
        compiler_params=_cp("parallel", "parallel"),
    )(z_a, z_a, z_a, z_a, z_a, cw8)


def _conv_mix_bwd(z_a, d_p, cw8, conv):
    s = z_a.shape[0]
    tr = _pick(s, ROW_TILE_BWD, 16)
    tc = _pick(conv, COL_TILE, LANE)
    nc = conv // tc
    nr = s // tr

    def body(zb_ref, zbn_ref, zc_ref, zcp_ref, zv_ref, zvp_ref, dp_ref, dpn_ref, w_ref,
             dzb_ref, dzc_ref, dzv_ref, dw_ref):
        i = pl.program_id(1)
        zc = zc_ref[...]
        zv = zv_ref[...]
        cv = zc * zv
        cvp = jnp.where(i > 0, zcp_ref[...] * zvp_ref[...], 0.0)
        cv1, cv2 = _lags(cv, cvp)
        dpv = dp_ref[...]
        dzb_ref[...] = (dpv * _conv3(w_ref, cv, cvp, (cv1, cv2))).astype(BF)
        dcc = dpv * zb_ref[...]
        dccn = jnp.where(i < nr - 1, dpn_ref[...] * zbn_ref[...], 0.0)
        dcv = _conv3_t(w_ref, dcc, dccn)
        dzc_ref[...] = (dcv * zv).astype(BF)
        dzv_ref[...] = (dcv * zc).astype(BF)

        @pl.when(i == 0)
        def _():
            dw_ref[...] = jnp.zeros_like(dw_ref)

        dw_ref[...] += _rows8([jnp.sum(dcc * cv2, axis=0, keepdims=True),
                               jnp.sum(dcc * cv1, axis=0, keepdims=True),
                               jnp.sum(dcc * cv, axis=0, keepdims=True)], tc)

    out = jax.ShapeDtypeStruct((s, conv), BF)
    return pl.pallas_call(
        body, name="conv_mix_bwd", grid=(nc, nr),
        in_specs=[_spec_cur(tr, tc, 0), _spec_next(tr, tc, 0, s),
                  _spec_cur(tr, tc, nc), _spec_prev(tr, tc, nc),
                  _spec_cur(tr, tc, 2 * nc), _spec_prev(tr, tc, 2 * nc),
                  _spec_cur(tr, tc, 0), _spec_next(tr, tc, 0, s), _spec_w(tc, 0)],
        out_specs=[_spec_cur(tr, tc, 0), _spec_cur(tr, tc, 0), _spec_cur(tr, tc, 0), _spec_w(tc, 0)],
        out_shape=[out, out, out, jax.ShapeDtypeStruct((SUB, conv), F32)],
        compiler_params=_cp("parallel", "arbitrary"),
    )(z_a, z_a, z_a, z_a, z_a, z_a, d_p, d_p, cw8)


def _silu_parts(ag):
    sg = jax.nn.sigmoid(ag)
    return ag * sg, sg


def _ffn_act_fwd(a_pre, cw8, cb, dff):
    s = a_pre.shape[0]
    tr = _pick(s, ROW_TILE, 16)
    tc = _pick(dff, COL_TILE, LANE)
    nc = dff // tc

    def body(xg_ref, xgp_ref, xu_ref, xup_ref, wg_ref, wu_ref, bg_ref, bu_ref, f_ref):
        i = pl.program_id(1)
        xgp = jnp.where(i > 0, xgp_ref[...], 0.0)
        xup = jnp.where(i > 0, xup_ref[...], 0.0)
        ag = _conv3(wg_ref, xg_ref[...], xgp) + bg_ref[...]
        au = _conv3(wu_ref, xu_ref[...], xup) + bu_ref[...]
        f_ref[...] = (_silu_parts(ag)[0] * au).astype(BF)

    return pl.pallas_call(
        body, name="ffn_act_fwd", grid=(nc, s // tr),
        in_specs=[_spec_cur(tr, tc, 0), _spec_prev(tr, tc, 0), _spec_cur(tr, tc, nc), _spec_prev(tr, tc, nc),
                  _spec_w(tc, 0), _spec_w(tc, nc),
                  pl.BlockSpec((1, tc), lambda j, i: (0, j)), pl.BlockSpec((1, tc), lambda j, i: (0, nc + j))],
        out_specs=_spec_cur(tr, tc, 0),
        out_shape=jax.ShapeDtypeStruct((s, dff), BF),
        compiler_params=_cp("parallel", "parallel"),
    )(a_pre, a_pre, a_pre, a_pre, cw8, cw8, cb, cb)


def _ffn_act_bwd(a_pre, d_f, cw8, cb, dff):
    s = a_pre.shape[0]
    tr = _pick(s, ROW_TILE_BWD, 16)
    tc = _pick(dff, COL_TILE, LANE)
    nc = dff // tc
    nr = s // tr

    def body(xg_ref, xgp_ref, xgn_ref, xu_ref, xup_ref, xun_ref, df_ref, dfn_ref,
             wg_ref, wu_ref, bg_ref, bu_ref, dxg_ref, dxu_ref, dwg_ref, dwu_ref):
        i = pl.program_id(1)
        xg = xg_ref[...]
        xu = xu_ref[...]
        xgp = jnp.where(i > 0, xgp_ref[...], 0.0)
        xup = jnp.where(i > 0, xup_ref[...], 0.0)

        def d_act(xg_t, xgp_t, xu_t, xup_t, df_t, lags_g=None, lags_u=None):
            ag = _conv3(wg_ref, xg_t, xgp_t, lags_g) + bg_ref[...]
            au = _conv3(wu_ref, xu_t, xup_t, lags_u) + bu_ref[...]
            sil, sg = _silu_parts(ag)
            return df_t * au * (sg * (1.0 + ag * (1.0 - sg))), df_t * sil

        lags_g = _lags(xg, xgp)
        lags_u = _lags(xu, xup)
        dag, dau = d_act(xg, xgp, xu, xup, df_ref[...], lags_g, lags_u)
        dfn = jnp.where(i < nr - 1, dfn_ref[...], 0.0)
        dagn, daun = d_act(xgn_ref[...], xg[tr - SUB:], xun_ref[...], xu[tr - SUB:], dfn)
        dxg_ref[...] = _conv3_t(wg_ref, dag, dagn).astype(BF)
        dxu_ref[...] = _conv3_t(wu_ref, dau, daun).astype(BF)

        @pl.when(i == 0)
        def _():
            dwg_ref[...] = jnp.zeros_like(dwg_ref)
            dwu_ref[...] = jnp.zeros_like(dwu_ref)

        def wgrad(da, x, lags):
            return _rows8([jnp.sum(da * lags[1], axis=0, keepdims=True),
                           jnp.sum(da * lags[0], axis=0, keepdims=True),
                           jnp.sum(da * x, axis=0, keepdims=True),
                           jnp.sum(da, axis=0, keepdims=True)], tc)

        dwg_ref[...] += wgrad(dag, xg, lags_g)
        dwu_ref[...] += wgrad(dau, xu, lags_u)

    half = jax.ShapeDtypeStruct((s, dff), BF)
    wsh = jax.ShapeDtypeStruct((SUB, dff), F32)
    return pl.pallas_call(
        body, name="ffn_act_bwd", grid=(nc, nr),
        in_specs=[_spec_cur(tr, tc, 0), _spec_prev(tr, tc, 0), _spec_next(tr, tc, 0, s),
                  _spec_cur(tr, tc, nc), _spec_prev(tr, tc, nc), _spec_next(tr, tc, nc, s),
                  _spec_cur(tr, tc, 0), _spec_next(tr, tc, 0, s),
                  _spec_w(tc, 0), _spec_w(tc, nc),
                  pl.BlockSpec((1, tc), lambda j, i: (0, j)), pl.BlockSpec((1, tc), lambda j, i: (0, nc + j))],
        out_specs=[_spec_cur(tr, tc, 0), _spec_cur(tr, tc, 0), _spec_w(tc, 0), _spec_w(tc, 0)],
        out_shape=[half, half, wsh, wsh],
        compiler_params=_cp("parallel", "arbitrary"),
    )(a_pre, a_pre, a_pre, a_pre, a_pre, a_pre, d_f, d_f, cw8, cw8, cb, cb)


def _mla_out_gate(o, w_mo, z_g, b_gate, yc):
    m, k = o.shape
    d = w_mo.shape[2]
    tm = _pick(m, MM_TM, 16)
    tn = _pick(d, MM_TM, LANE)
    nc = d // tn

    def body(a_ref, b_ref, za_ref, zb_ref, ba_ref, bb_ref, yc_ref, ym_ref, mix_ref):
        ym = jnp.dot(a_ref[...], b_ref[0], preferred_element_type=F32)
        ga = jax.nn.sigmoid(za_ref[...] + ba_ref[...])
        gb = jax.nn.sigmoid(zb_ref[...] + bb_ref[...])
        ym_ref[...] = ym.astype(BF)
        mix_ref[...] = (ga * yc_ref[...] + gb * ym).astype(BF)

    tile = pl.BlockSpec((tm, tn), lambda i, j: (i, j))
    out = jax.ShapeDtypeStruct((m, d), BF)
    return pl.pallas_call(
        body, name="mm_y_mla_gate", grid=(m // tm, nc),
        in_specs=[pl.BlockSpec((tm, k), lambda i, j: (i, 0)), pl.BlockSpec((1, k, tn), lambda i, j: (0, 0, j)),
                  tile, pl.BlockSpec((tm, tn), lambda i, j: (i, nc + j)),
                  pl.BlockSpec((1, tn), lambda i, j: (0, j)), pl.BlockSpec((1, tn), lambda i, j: (0, nc + j)), tile],
        out_specs=[tile, tile], out_shape=[out, out],
        compiler_params=_cp("parallel", "parallel"),
    )(o, w_mo, z_g, z_g, b_gate, b_gate, yc)


def _d_mix_gate(d_h1, w_oo, z_g, b_gate, yc, ym):
    m, n = d_h1.shape
    d = w_oo.shape[1]
    tm = _pick(m, MM_TM, 16)
    tn = _pick(d, COL_TILE, LANE)
    nc = d // tn

    def body(a_ref, b_ref, za_ref, zb_ref, ba_ref, bb_ref, yc_ref, ym_ref,
             dza_ref, dzb_ref, dyc_ref, dym_ref, dba_ref, dbb_ref):
        i = pl.program_id(1)
        dm = lax.dot_general(a_ref[...], b_ref[0], (((1,), (1,)), ((), ())), preferred_element_type=F32)
        ga = jax.nn.sigmoid(za_ref[...] + ba_ref[...])
        gb = jax.nn.sigmoid(zb_ref[...] + bb_ref[...])
        dza = dm * yc_ref[...] * (ga * (1.0 - ga))
        dzb = dm * ym_ref[...] * (gb * (1.0 - gb))
        dza_ref[...] = dza.astype(BF)
        dzb_ref[...] = dzb.astype(BF)
        dyc_ref[...] = (dm * ga).astype(BF)
        dym_ref[...] = (dm * gb).astype(BF)

        @pl.when(i == 0)
        def _():
            dba_ref[...] = jnp.zeros_like(dba_ref)
            dbb_ref[...] = jnp.zeros_like(dbb_ref)

        dba_ref[...] += _rows8([jnp.sum(dza, axis=0, keepdims=True)], tn)
        dbb_ref[...] += _rows8([jnp.sum(dzb, axis=0, keepdims=True)], tn)

    tile = pl.BlockSpec((tm, tn), lambda j, i: (i, j))
    act = jax.ShapeDtypeStruct((m, d), BF)
    bsh = jax.ShapeDtypeStruct((SUB, d), F32)
    return pl.pallas_call(
        body, name="mm_d_mix_gate", grid=(nc, m // tm),
        in_specs=[pl.BlockSpec((tm, n), lambda j, i: (i, 0)), pl.BlockSpec((1, tn, n), lambda j, i: (0, j, 0)),
                  tile, pl.BlockSpec((tm, tn), lambda j, i: (i, nc + j)),
                  pl.BlockSpec((1, tn), lambda j, i: (0, j)), pl.BlockSpec((1, tn), lambda j, i: (0, nc + j)),
                  tile, tile],
        out_specs=[tile] * 4 + [pl.BlockSpec((SUB, tn), lambda j, i: (0, j))] * 2,
        out_shape=[act, act, act, act, bsh, bsh],
        compiler_params=_cp("parallel", "arbitrary"),
    )(d_h1, w_oo, z_g, z_g, b_gate, b_gate, yc, ym)


def _lay(v):
    z = jnp.zeros(v.shape[:-1] + (HALF,), v.dtype)
    return jnp.concatenate([v[..., :HALF], z, v[..., HALF:], z], axis=-1)


def _unlay(v):
    return jnp.concatenate([v[..., :HALF], v[..., 2 * HALF:3 * HALF]], axis=-1)


def _lay_rows(v):
    z = jnp.zeros((HALF,) + v.shape[1:], v.dtype)
    return jnp.concatenate([v[:HALF], z, v[HALF:], z], axis=0)


def _rope_tables(positions):
    s = positions.shape[0]
    tr = _pick(s, ROW_TILE, 8)
    inv_freq = ROPE_THETA ** (-jnp.arange(0, ROPE, 2, dtype=F32) / ROPE)
    consts = jnp.stack([_lay(jnp.concatenate([inv_freq, inv_freq])),
                        _lay(jnp.ones((ROPE,), F32)),
                        _lay(jnp.concatenate([-jnp.ones((HALF,), F32), jnp.ones((HALF,), F32)]))])
    consts = _pad8(consts)

    def body(p_ref, c_ref, cos_ref, sin_ref):
        ang = p_ref[...].astype(F32) * c_ref[0:1, :]
        cos_ref[...] = jnp.cos(ang) * c_ref[1:2, :]
        sin_ref[...] = jnp.sin(ang) * c_ref[2:3, :]

    tab = jax.ShapeDtypeStruct((s, LANE), F32)
    return pl.pallas_call(
        body, name="rope_tables", grid=(s // tr,),
        in_specs=[pl.BlockSpec((tr, 1), lambda i: (i, 0)), pl.BlockSpec((SUB, LANE), lambda i: (0, 0))],
        out_specs=[pl.BlockSpec((tr, LANE), lambda i: (i, 0))] * 2,
        out_shape=[tab, tab],
        compiler_params=_cp("parallel"),
    )(positions, consts)


def _lane_sum(p):
    return jnp.sum(p, axis=-1, keepdims=True)


def _rope(t, cos, sin):
    return t * cos + pltpu.roll(t, 2 * HALF, axis=1) * sin


def _rope_t(d, cos, sin):
    return d * cos + pltpu.roll(d * sin, 2 * HALF, axis=1)


def _head_fwd(q_raw, kv_raw, z_a, kr_blk, cos, sin, gains, heads):
    s = q_raw.shape[0]
    tr = _pick(s, HEAD_ROW_TILE, 16)
    hw = heads * LANE

    def body(q_ref, kv_ref, kr_ref, cos_ref, sin_ref, g_ref, qo_ref, ko_ref, vo_ref):
        cosv = cos_ref[...]
        sinv = sin_ref[...]
        krv = kr_ref[...]
        kr_sq = krv * krv
        for h in range(heads):
            lo = h * LANE
            qn = q_ref[:, lo:lo + LANE]
            qr = q_ref[:, hw + lo:hw + lo + LANE]
            r = lax.rsqrt(_lane_sum(qn * qn + qr * qr) / HEAD_QK + NORM_EPS)
            qo_ref[:, 2 * lo:2 * lo + LANE] = ((qn * r) * g_ref[0:1, :]).astype(BF)
            qo_ref[:, 2 * lo + LANE:2 * lo + 2 * LANE] = _rope((qr * r) * g_ref[1:2, :], cosv, sinv).astype(BF)
            kn = kv_ref[:, 2 * lo:2 * lo + LANE]
            r = lax.rsqrt(_lane_sum(kn * kn + kr_sq) / HEAD_QK + NORM_EPS)
            ko_ref[:, 2 * lo:2 * lo + LANE] = ((kn * r) * g_ref[2:3, :]).astype(BF)
            ko_ref[:, 2 * lo + LANE:2 * lo + 2 * LANE] = _rope((krv * r) * g_ref[3:4, :], cosv, sinv).astype(BF)
            vo_ref[:, lo:lo + LANE] = kv_ref[:, 2 * lo + LANE:2 * lo + 2 * LANE].astype(BF)

    row = lambda w: pl.BlockSpec((tr, w), lambda i: (i, 0))
    return pl.pallas_call(
        body, name="head_fwd", grid=(s // tr,),
        in_specs=[row(2 * hw), row(2 * hw), pl.BlockSpec((tr, LANE), lambda i: (i, kr_blk)),
                  row(LANE), row(LANE), pl.BlockSpec((SUB, LANE), lambda i: (0, 0))],
        out_specs=[row(2 * hw), row(2 * hw), row(hw)],
        out_shape=[jax.ShapeDtypeStruct((s, 2 * hw), BF), jax.ShapeDtypeStruct((s, 2 * hw), BF),
                   jax.ShapeDtypeStruct((s, hw), BF)],
        compiler_params=_cp("parallel"),
    )(q_raw, kv_raw, z_a, cos, sin, gains)


def _head_bwd(q_raw, kv_raw, z_a, kr_blk, cos, sin, gains, dq_att, dk_att, dv, heads):
    s = q_raw.shape[0]
    tr = _pick(s, HEAD_ROW_TILE_BWD, 16)
    hw = heads * LANE

    def body(q_ref, kv_ref, kr_ref, cos_ref, sin_ref, g_ref, dq_ref, dk_ref, dv_ref,
             dqr_ref, dkv_ref, dkr_ref, dg_ref):
        i = pl.program_id(0)
        cosv = cos_ref[...]
        sinv = sin_ref[...]
        krv = kr_ref[...]
        kr_sq = krv * krv
        dkr = jnp.zeros((tr, LANE), F32)
        dgs = [jnp.zeros((1, LANE), F32) for _ in range(4)]

        def norm_bwd(xn, xr, sq, dn_out, dr_out, gn, gr):
            r = lax.rsqrt(_lane_sum(sq) / HEAD_QK + NORM_EPS)
            nn = xn * r
            nr = xr * r
            dt = _rope_t(dr_out, cosv, sinv)
            dnn = dn_out * gn
            dnr = dt * gr
            mean = _lane_sum(dnn * nn + dnr * nr) / HEAD_QK
            return (r * (dnn - nn * mean), r * (dnr - nr * mean),
                    jnp.sum(dn_out * nn, axis=0, keepdims=True), jnp.sum(dt * nr, axis=0, keepdims=True))

        for h in range(heads):
            lo = h * LANE
            qn = q_ref[:, lo:lo + LANE]
            qr = q_ref[:, hw + lo:hw + lo + LANE]
            dxn, dxr, g0, g1 = norm_bwd(qn, qr, qn * qn + qr * qr, dq_ref[:, 2 * lo:2 * lo + LANE],
                                        dq_ref[:, 2 * lo + LANE:2 * lo + 2 * LANE], g_ref[0:1, :], g_ref[1:2, :])
            dqr_ref[:, lo:lo + LANE] = dxn.astype(BF)
            dqr_ref[:, hw + lo:hw + lo + LANE] = dxr.astype(BF)
            kn = kv_ref[:, 2 * lo:2 * lo + LANE]
            dxn, dxr, g2, g3 = norm_bwd(kn, krv, kn * kn + kr_sq, dk_ref[:, 2 * lo:2 * lo + LANE],
                                        dk_ref[:, 2 * lo + LANE:2 * lo + 2 * LANE], g_ref[2:3, :], g_ref[3:4, :])
            dkv_ref[:, 2 * lo:2 * lo + LANE] = dxn.astype(BF)
            dkv_ref[:, 2 * lo + LANE:2 * lo + 2 * LANE] = dv_ref[:, lo:lo + LANE].astype(BF)
            dkr = dkr + dxr
            dgs = [a + b for a, b in zip(dgs, (g0, g1, g2, g3))]
        dkr_ref[...] = dkr

        @pl.when(i == 0)
        def _():
            dg_ref[...] = jnp.zeros_like(dg_ref)

        dg_ref[...] += _rows8(dgs, LANE)

    row = lambda w: pl.BlockSpec((tr, w), lambda i: (i, 0))
    return pl.pallas_call(
        body, name="head_bwd", grid=(s // tr,),
        in_specs=[row(2 * hw), row(2 * hw), pl.BlockSpec((tr, LANE), lambda i: (i, kr_blk)),
                  row(LANE), row(LANE), pl.BlockSpec((SUB, LANE), lambda i: (0, 0)),
                  row(2 * hw), row(2 * hw), row(hw)],
        out_specs=[row(2 * hw), row(2 * hw), row(LANE), pl.BlockSpec((SUB, LANE), lambda i: (0, 0))],
        out_shape=[jax.ShapeDtypeStruct((s, 2 * hw), BF), jax.ShapeDtypeStruct((s, 2 * hw), BF),
                   jax.ShapeDtypeStruct((s, LANE), F32), jax.ShapeDtypeStruct((SUB, LANE), F32)],
        compiler_params=_cp("arbitrary"),
    )(q_raw, kv_raw, z_a, cos, sin, gains, dq_att, dk_att, dv)


def _causal_mask(nrows, ncols, row0):
    rows = lax.broadcasted_iota(jnp.int32, (nrows, ncols), 0) + row0
    cols = lax.broadcasted_iota(jnp.int32, (nrows, ncols), 1)
    return cols <= rows


def _causal_steps(nt, q_major):
    pairs = ([(i, j) for i in range(nt) for j in range(i + 1)] if q_major
             else [(i, j) for j in range(nt) for i in range(j, nt)])
    return (jnp.array([p[0] for p in pairs], jnp.int32), jnp.array([p[1] for p in pairs], jnp.int32))


def _attn_fwd(q_att, k_att, v, heads):
    s = q_att.shape[0]
    t = _pick(s, ATTN_TILE_FWD, LANE)
    nt = s // t
    th = t // 2
    scale = HEAD_QK ** -0.5
    qi, kj = _causal_steps(nt, True)

    def body(qi_ref, kj_ref, q_ref, k_ref, v_ref, o_ref, ob_ref, lse_ref, m_s, l_s, acc_s):
        st = pl.program_id(1)
        i = qi_ref[st]
        j = kj_ref[st]

        @pl.when(j == 0)
        def _():
            m_s[...] = jnp.full_like(m_s, NEG_INF)
            l_s[...] = jnp.zeros_like(l_s)
            acc_s[...] = jnp.zeros_like(acc_s)

        def update(rows, ncol, masked):
            sc = lax.dot_general(q_ref[rows, :], k_ref[0:ncol, :], (((1,), (1,)), ((), ())),
                                 preferred_element_type=F32) * scale
            if masked:
                sc = jnp.where(_causal_mask(rows.stop - rows.start, ncol, rows.start), sc, NEG_INF)
            m_prev = m_s[rows, :]
            m_new = jnp.maximum(m_prev, jnp.max(sc, axis=-1, keepdims=True))
            alpha = jnp.exp(m_prev - m_new)
            p = jnp.exp(sc - jnp.tile(m_new, (1, ncol // LANE)))
            l_s[rows, :] = alpha * l_s[rows, :] + jnp.sum(p, axis=-1, keepdims=True)
            acc_s[rows, :] = alpha * acc_s[rows, :] + jnp.dot(p.astype(BF), v_ref[0:ncol, :],
                                                              preferred_element_type=F32)
            m_s[rows, :] = m_new

        @pl.when(j < i)
        def _():
            update(slice(0, t), t, False)

        @pl.when(j == i)
        def _():
            update(slice(0, th), th, True)
            update(slice(th, t), t, True)
            o = acc_s[...] / l_s[...]
            o_ref[...] = o
            ob_ref[...] = o.astype(BF)
            lse_ref[...] = (m_s[...] + jnp.log(l_s[...]))[:, 0:1]

    q_idx = lambda h, st, qi_r, kj_r: (qi_r[st], h)
    kv_idx = lambda h, st, qi_r, kj_r: (kj_r[st], h)
    return pl.pallas_call(
        body, name="attn_fwd",
        grid_spec=pltpu.PrefetchScalarGridSpec(
            num_scalar_prefetch=2, grid=(heads, qi.shape[0]),
            in_specs=[pl.BlockSpec((t, 2 * LANE), q_idx), pl.BlockSpec((t, 2 * LANE), kv_idx),
                      pl.BlockSpec((t, LANE), kv_idx)],
            out_specs=[pl.BlockSpec((t, LANE), q_idx), pl.BlockSpec((t, LANE), q_idx),
                       pl.BlockSpec((None, t, 1), lambda h, st, qi_r, kj_r: (h, qi_r[st], 0))],
            scratch_shapes=[pltpu.VMEM((t, LANE), F32), pltpu.VMEM((t, LANE), F32), pltpu.VMEM((t, LANE), F32)]),
        out_shape=[jax.ShapeDtypeStruct((s, heads * LANE), F32), jax.ShapeDtypeStruct((s, heads * LANE), BF),
                   jax.ShapeDtypeStruct((heads, s, 1), F32)],
        compiler_params=_cp("parallel", "arbitrary"),
    )(qi, kj, q_att, k_att, v)


def _attn_bwd(q_att, k_att, v, o, lse, d_o, heads, dep=None):
    s = q_att.shape[0]
    t = _pick(s, ATTN_TILE, LANE)
    nt = s // t
    th = t // 2
    scale = HEAD_QK ** -0.5
    qi, kj = _causal_steps(nt, False)

    def body(qi_ref, kj_ref, q_ref, k_ref, v_ref, do_ref, o_ref, lse_ref, *rest):
        dq_ref, dk_ref, dv_ref, dk_s, dv_s = rest[-5:]
        st = pl.program_id(1)
        i = qi_ref[st]
        j = kj_ref[st]

        @pl.when(st == 0)
        def _():
            dq_ref[...] = jnp.zeros_like(dq_ref)

        @pl.when(i == j)
        def _():
            dk_s[...] = jnp.zeros_like(dk_s)
            dv_s[...] = jnp.zeros_like(dv_s)

        def update(rows, ncol, masked):
            nrow = rows.stop - rows.start
            q = q_ref[rows, :]
            k = k_ref[0:ncol, :]
            do = do_ref[rows, :]
            sc = lax.dot_general(q, k, (((1,), (1,)), ((), ())), preferred_element_type=F32) * scale
            if masked:
                sc = jnp.where(_causal_mask(nrow, ncol, rows.start), sc, NEG_INF)
            p = jnp.exp(sc - lse_ref[rows, :])
            dp = lax.dot_general(do, v_ref[0:ncol, :], (((1,), (1,)), ((), ())), preferred_element_type=F32)
            delta = jnp.sum(do.astype(F32) * o_ref[rows, :], axis=-1, keepdims=True)
            ds = (p * (dp - delta) * scale).astype(BF)
            dv_s[0:ncol, :] += lax.dot_general(p.astype(BF), do, (((0,), (0,)), ((), ())),
                                               preferred_element_type=F32)
            dk_s[0:ncol, :] += lax.dot_general(ds, q, (((0,), (0,)), ((), ())), preferred_element_type=F32)
            out_rows = pl.ds(pl.multiple_of(i * t + rows.start, nrow), nrow)
            dq_ref[out_rows, :] += jnp.dot(ds, k, preferred_element_type=F32)

        @pl.when(i > j)
        def _():
            update(slice(0, t), t, False)

        @pl.when(i == j)
        def _():
            update(slice(0, th), th, True)
            update(slice(th, t), t, True)

        @pl.when(i == nt - 1)
        def _():
            dk_ref[...] = dk_s[...].astype(BF)
            dv_ref[...] = dv_s[...].astype(BF)

    q_idx = lambda h, st, qi_r, kj_r: (qi_r[st], h)
    kv_idx = lambda h, st, qi_r, kj_r: (kj_r[st], h)
    in_specs = [pl.BlockSpec((t, 2 * LANE), q_idx), pl.BlockSpec((t, 2 * LANE), kv_idx),
                pl.BlockSpec((t, LANE), kv_idx), pl.BlockSpec((t, LANE), q_idx), pl.BlockSpec((t, LANE), q_idx),
                pl.BlockSpec((None, t, 1), lambda h, st, qi_r, kj_r: (h, qi_r[st], 0))]
    args = [q_att, k_att, v, d_o, o, lse]
    if dep is not None:
        in_specs.append(ANY)
        args.append(dep)
    return pl.pallas_call(
        body, name="attn_bwd",
        grid_spec=pltpu.PrefetchScalarGridSpec(
            num_scalar_prefetch=2, grid=(heads, qi.shape[0]),
            in_specs=in_specs,
            out_specs=[pl.BlockSpec((s, 2 * LANE), lambda h, st, qi_r, kj_r: (0, h)),
                       pl.BlockSpec((t, 2 * LANE), kv_idx), pl.BlockSpec((t, LANE), kv_idx)],
            scratch_shapes=[pltpu.VMEM((t, 2 * LANE), F32), pltpu.VMEM((t, LANE), F32)]),
        out_shape=[jax.ShapeDtypeStruct((s, heads * 2 * LANE), F32),
                   jax.ShapeDtypeStruct((s, heads * 2 * LANE), BF),
                   jax.ShapeDtypeStruct((s, heads * LANE), BF)],
        compiler_params=_cp("parallel", "arbitrary"),
    )(qi, kj, *args)


def _sum_parts(parts, name):
    n, r, c = parts.shape
    tr = _pick(r, 512, 8)

    def body(p_ref, o_ref):
        g = p_ref[0].astype(F32)
        for k in range(1, n):
            g = g + p_ref[k].astype(F32)
        o_ref[...] = g

    return pl.pallas_call(
        body, name=name, grid=(r // tr,),
        in_specs=[pl.BlockSpec((n, tr, c), lambda i: (0, i, 0))],
        out_specs=pl.BlockSpec((tr, c), lambda i: (i, 0)),
        out_shape=jax.ShapeDtypeStruct((r, c), F32),
        compiler_params=_cp("parallel"),
    )(parts)


def _adamw(parts, w, m, v, name, by_cols=False):
    n, rp, c = parts.shape
    r = w.shape[0]
    assert by_cols or rp == r
    tr, tc = (r, _pick(c, 256, LANE)) if by_cols else (_pick(r, 256, 16 if r % 16 == 0 else 8), c)

    def body(p_ref, w_ref, m_ref, v_ref, g_ref, d_ref, mo_ref, vo_ref):
        g = p_ref[0].astype(F32)
        for k in range(1, n):
            g = g + p_ref[k].astype(F32)
        g = g[:r] if by_cols else g
        m_new = ADAM_B1 * m_ref[...] + (1.0 - ADAM_B1) * g
        v_new = ADAM_B2 * v_ref[...] + (1.0 - ADAM_B2) * jnp.square(g)
        m_hat = m_new / (1.0 - ADAM_B1 ** ADAM_STEP)
        v_hat = v_new / (1.0 - ADAM_B2 ** ADAM_STEP)
        g_ref[...] = g
        d_ref[...] = -ADAM_LR * (m_hat / (jnp.sqrt(v_hat) + ADAM_EPS) + ADAM_WD * w_ref[...])
        mo_ref[...] = m_new
        vo_ref[...] = v_new

    idx = (lambda i: (0, i)) if by_cols else (lambda i: (i, 0))
    spec = pl.BlockSpec((tr, tc), idx)
    sh = jax.ShapeDtypeStruct((r, c), F32)
    return pl.pallas_call(
        body, name=name, grid=(c // tc if by_cols else r // tr,),
        in_specs=[pl.BlockSpec((n, rp if by_cols else tr, tc), lambda i: (0,) + idx(i)), spec, spec, spec],
        out_specs=[spec] * 4, out_shape=[sh] * 4,
        compiler_params=_cp("parallel"),
    )(parts, w, m, v)


def _place():
    x, y, c = lax.axis_index("x"), lax.axis_index("y"), lax.axis_index("c")
    chips = [(1 - x, y), (x, 1 - y), (1 - x, 1 - y)]
    return x, y, c, chips


def _all_gather(shards, name, dep=None):
    n = len(shards)
    deps = [] if dep is None else list(dep)

    def body(*refs):
        ins, outs = refs[:n], refs[n + len(deps):2 * n + len(deps)]
        send_sems, recv_sems, local_sems = refs[2 * n + len(deps):]
        x, y, c, chips = _place()
        me, sibling = (x, y, c), (x, y, 1 - c)

        def slot(w, p):
            return outs[w].at[4 * p[0] + 2 * p[1] + p[2]]

        def copy(w, k, block, to, src=None):
            return pltpu.make_async_remote_copy(
                src_ref=slot(w, block) if src is None else src, dst_ref=slot(w, block),
                send_sem=send_sems.at[w, k], recv_sem=recv_sems.at[w, k], device_id=to, device_id_type=MESH)

        first = []
        for w in range(n):
            first += [copy(w, 1 + j, me, (*chip, c), src=ins[w]) for j, chip in enumerate(chips)]
            first.append(copy(w, 0, me, sibling, src=ins[w]))
        for cp in first:
            cp.start()
        mine = [pltpu.make_async_copy(ins[w], slot(w, me), local_sems.at[w]) for w in range(n)]
        for cp in mine:
            cp.start()
        passed = []
        for w in range(n):
            for j, chip in enumerate(chips):
                copy(w, 1 + j, (*chip, c), me).wait_recv()
                cp = copy(w, 4 + j, (*chip, c), sibling)
                cp.start()
                passed.append(cp)
        for w in range(n):
            copy(w, 0, sibling, me).wait_recv()
            for j, chip in enumerate(chips):
                copy(w, 4 + j, (*chip, 1 - c), me).wait_recv()
        for cp in first + passed:
            cp.wait_send()
        for cp in mine:
            cp.wait()

    return pl.pallas_call(
        body, name=name,
        in_specs=[ANY] * (n + len(deps)), out_specs=[ANY] * n,
        out_shape=[jax.ShapeDtypeStruct((N_DEV,) + a.shape, a.dtype) for a in shards],
        scratch_shapes=[pltpu.SemaphoreType.DMA((n, 7)), pltpu.SemaphoreType.DMA((n, 7)),
                        pltpu.SemaphoreType.DMA((n,))],
    )(*shards, *deps)


HBM = pl.BlockSpec(memory_space=pltpu.HBM)
SEM = pl.BlockSpec(memory_space=pltpu.SEMAPHORE)
EFFECT = pltpu.SideEffectType.DATAFLOW_SIDE_EFFECTING
PEERS = [(dx, dy, dc) for dx in (1, 0) for dy in (1, 0) for dc in (0, 1) if (dx, dy, dc) != (0, 0, 0)]


def _peer(x, y, c, flip):
    dx, dy, dc = flip
    return (1 - x if dx else x, 1 - y if dy else y, 1 - c if dc else c)


def _exchange_copies(srcs, lands, send, recv, loc, gather):
    x, y, c, _ = _place()
    me = 4 * x + 2 * y + c
    remote, local = [], []
    for w in range(len(srcs)):
        for k, flip in enumerate(PEERS):
            px, py, pc = _peer(x, y, c, flip)
            src = srcs[w] if gather else srcs[w].at[4 * px + 2 * py + pc]
            remote.append(pltpu.make_async_remote_copy(
                src_ref=src, dst_ref=lands[w].at[me], send_sem=send[w].at[k], recv_sem=recv[w].at[k],
                device_id=(px, py, pc), device_id_type=MESH))
        local.append(pltpu.make_async_copy(srcs[w] if gather else srcs[w].at[me], lands[w].at[me], loc[w]))
    return remote, local


class _Exchange:
    def __init__(self, srcs, lands, send, recv, loc, token, gather):
        self.srcs, self.lands, self.send, self.recv, self.loc = srcs, lands, send, recv, loc
        self.token, self.gather = token, gather


def _exchange_start(srcs, gather, name, dep=None):
    n = len(srcs)
    deps = [] if dep is None else [dep]
    land_shapes = [((N_DEV,) + a.shape) if gather else a.shape for a in srcs]
    lands = [pltpu.with_memory_space_constraint(lax.empty(sh, a.dtype), pltpu.HBM) for sh, a in zip(land_shapes, srcs)]
    srcs = [pltpu.with_memory_space_constraint(a, pltpu.HBM) for a in srcs]

    def body(*refs):
        src_refs, land_refs = refs[:n], refs[n:2 * n]
        outs = refs[2 * n + len(deps):]
        send, recv, loc = outs[:n], outs[n:2 * n], outs[2 * n:3 * n]
        token = outs[-1]
        remote, local = _exchange_copies(src_refs, land_refs, send, recv, loc, gather)
        for cp in remote + local:
            cp.start()
        token[...] = jnp.zeros_like(token)

    out_shape = ([pltpu.SemaphoreType.DMA((len(PEERS),))] * (2 * n) + [pltpu.SemaphoreType.DMA(())] * n
                 + [pltpu.HBM(a.shape, a.dtype) for a in srcs] + [pltpu.HBM(a.shape, a.dtype) for a in lands]
                 + [jax.ShapeDtypeStruct((SUB, LANE), F32)])
    res = pl.pallas_call(
        body, name=name, out_shape=out_shape,
        in_specs=[HBM] * (2 * n) + [ANY] * len(deps),
        out_specs=[SEM] * (3 * n) + [HBM] * (2 * n) + [pl.BlockSpec(memory_space=pltpu.VMEM)],
        input_output_aliases={i: 3 * n + i for i in range(2 * n)},
        compiler_params=pltpu.CompilerParams(has_side_effects=EFFECT),
    )(*srcs, *lands, *deps)
    return _Exchange(res[3 * n:4 * n], res[4 * n:5 * n], res[:n], res[n:2 * n], res[2 * n:3 * n], res[-1], gather)


def _exchange_wait(ex, idxs, after, name):
    n = len(idxs)
    srcs = [ex.srcs[i] for i in idxs]
    lands = [ex.lands[i] for i in idxs]
    sems = [ex.send[i] for i in idxs] + [ex.recv[i] for i in idxs] + [ex.loc[i] for i in idxs]
    gather = ex.gather

    def body(*refs):
        src_refs, land_refs = refs[:n], refs[n:2 * n]
        send, recv, loc = refs[2 * n:3 * n], refs[3 * n:4 * n], refs[4 * n:5 * n]
        remote, local = _exchange_copies(src_refs, land_refs, send, recv, loc, gather)
        for cp in remote:
            cp.wait_send()
            cp.wait_recv()
        for cp in local:
            cp.wait()

    res = pl.pallas_call(
        body, name=name,
        out_shape=[pltpu.HBM(a.shape, a.dtype) for a in srcs] + [pltpu.HBM(a.shape, a.dtype) for a in lands],
        in_specs=[HBM] * (2 * n) + [SEM] * (3 * n) + [ANY],
        out_specs=[HBM] * (2 * n),
        input_output_aliases={i: i for i in range(2 * n)},
        compiler_params=pltpu.CompilerParams(has_side_effects=EFFECT),
    )(*srcs, *lands, *sems, after)
    return res[n:]


def _gather2_copies(srcs, lands, send, recv_ici, recv_sib, loc):
    x, y, c, chips = _place()
    me = 4 * x + 2 * y + c
    remote, local = [], []
    for w in range(len(srcs)):
        remote.append(pltpu.make_async_remote_copy(
            src_ref=srcs[w], dst_ref=lands[w].at[me], send_sem=send[w].at[0], recv_sem=recv_sib[w],
            device_id=(x, y, 1 - c), device_id_type=MESH))
        for j, chip in enumerate(chips):
            remote.append(pltpu.make_async_remote_copy(
                src_ref=srcs[w], dst_ref=lands[w].at[me], send_sem=send[w].at[1 + j], recv_sem=recv_ici[w].at[j],
                device_id=(*chip, c), device_id_type=MESH))
        local.append(pltpu.make_async_copy(srcs[w], lands[w].at[me], loc[w]))
    return remote, local


def _gather2_forwards(lands, fsend, frecv, arrived=None):
    x, y, c, chips = _place()
    cps = []
    for w in range(len(lands)):
        for j, chip in enumerate(chips):
            slot = lands[w].at[4 * chip[0] + 2 * chip[1] + c]
            cp = pltpu.make_async_remote_copy(
                src_ref=slot, dst_ref=slot, send_sem=fsend[w].at[j], recv_sem=frecv[w].at[j],
                device_id=(x, y, 1 - c), device_id_type=MESH)
            if arrived is not None:
                pltpu.make_async_remote_copy(
                    src_ref=slot, dst_ref=slot, send_sem=fsend[w].at[j], recv_sem=arrived[w].at[j],
                    device_id=(x, y, 1 - c), device_id_type=MESH).wait_recv()
            cps.append(cp)
    return cps


def _gather2(shards, between, name):
    n = len(shards)
    srcs = [pltpu.with_memory_space_constraint(a, pltpu.HBM) for a in shards]
    lands = [pltpu.with_memory_space_constraint(lax.empty((N_DEV,) + a.shape, a.dtype), pltpu.HBM) for a in shards]
    hbm_like = lambda arrs: [pltpu.HBM(a.shape, a.dtype) for a in arrs]
    tok = jax.ShapeDtypeStruct((SUB, LANE), F32)
    vmem = pl.BlockSpec(memory_space=pltpu.VMEM)
    side = pltpu.CompilerParams(has_side_effects=EFFECT)

    def start(*refs):
        src_refs, land_refs = refs[:n], refs[n:2 * n]
        outs = refs[2 * n:]
        send, recv_ici, recv_sib, loc = outs[:n], outs[n:2 * n], outs[2 * n:3 * n], outs[3 * n:4 * n]
        remote, local = _gather2_copies(src_refs, land_refs, send, recv_ici, recv_sib, loc)
        for cp in remote + local:
            cp.start()
        outs[-1][...] = jnp.zeros((SUB, LANE), F32)

    res = pl.pallas_call(
        start, name=name + "_start",
        out_shape=([pltpu.SemaphoreType.DMA((4,))] * n + [pltpu.SemaphoreType.DMA((3,))] * n
                   + [pltpu.SemaphoreType.DMA(())] * (2 * n) + hbm_like(srcs) + hbm_like(lands) + [tok]),
        in_specs=[HBM] * (2 * n), out_specs=[SEM] * (4 * n) + [HBM] * (2 * n) + [vmem],
        input_output_aliases={i: 4 * n + i for i in range(2 * n)}, compiler_params=side,
    )(*srcs, *lands)
    send, recv_ici, recv_sib, loc = res[:n], res[n:2 * n], res[2 * n:3 * n], res[3 * n:4 * n]
    srcs, lands, token = res[4 * n:5 * n], res[5 * n:6 * n], res[-1]

    done = between(token)
    after = jax.tree_util.tree_leaves(done)

    def forward(*refs):
        land_refs, arrived = refs[:n], refs[n:2 * n]
        outs = refs[2 * n + len(after):]
        fsend, frecv = outs[:n], outs[n:2 * n]
        for cp in _gather2_forwards(land_refs, fsend, frecv, arrived):
            cp.start()
        outs[-1][...] = jnp.zeros((SUB, LANE), F32)

    res = pl.pallas_call(
        forward, name=name + "_forward",
        out_shape=[pltpu.SemaphoreType.DMA((3,))] * (2 * n) + hbm_like(lands) + [tok],
        in_specs=[HBM] * n + [SEM] * n + [ANY] * len(after), out_specs=[SEM] * (2 * n) + [HBM] * n + [vmem],
        input_output_aliases={i: 2 * n + i for i in range(n)}, compiler_params=side,
    )(*lands, *recv_ici, *after)
    fsend, frecv, lands, token = res[:n], res[n:2 * n], res[2 * n:3 * n], res[-1]

    def wait(*refs):
        src_refs, land_refs = refs[:n], refs[n:2 * n]
        sems = refs[2 * n:7 * n]
        send, recv_sib, loc, fsend, frecv = (sems[k * n:(k + 1) * n] for k in range(5))
        remote, local = _gather2_copies(src_refs, land_refs, send, send, recv_sib, loc)
        for w in range(n):
            for cp in remote[4 * w:4 * w + 4]:
                cp.wait_send()
            remote[4 * w].wait_recv()
        for cp in local:
            cp.wait()
        for cp in _gather2_forwards(land_refs, fsend, frecv):
            cp.wait_send()
            cp.wait_recv()

    res = pl.pallas_call(
        wait, name=name + "_wait", out_shape=hbm_like(srcs) + hbm_like(lands),
        in_specs=[HBM] * (2 * n) + [SEM] * (5 * n) + [ANY], out_specs=[HBM] * (2 * n),
        input_output_aliases={i: i for i in range(2 * n)}, compiler_params=side,
    )(*srcs, *lands, *send, *recv_sib, *loc, *fsend, *frecv, token)
    return res[n:], done


def _after(token, a):
    return a + token[0:1, 0:1].astype(a.dtype)


def _unblock(w3):
    nb, k, nbw = w3.shape
    return w3.transpose(1, 0, 2).reshape(k, nb * nbw)


def _block(w, nb):
    k, n = w.shape
    return w.reshape(k, nb, n // nb).transpose(1, 0, 2)


def kernel(x, positions, ln1_g, w_in, b_gate, conv_w, w_conv_out, q_a_g, w_q_b, kv_a_g, w_kv_b, q_norm_g, k_norm_g, w_mla_out, w_o, ln2_g, w_ffn_up, ffn_conv_w, ffn_conv_b, w_ffn_down, loss_target, m_ln1_g, m_w_in, m_b_gate, m_conv_w, m_w_conv_out, m_q_a_g, m_w_q_b, m_kv_a_g, m_w_kv_b, m_q_norm_g, m_k_norm_g, m_w_mla_out, m_w_o, m_ln2_g, m_w_ffn_up, m_ffn_conv_w, m_ffn_conv_b, m_w_ffn_down, v_ln1_g, v_w_in, v_b_gate, v_conv_w, v_w_conv_out, v_q_a_g, v_w_q_b, v_kv_a_g, v_w_kv_b, v_q_norm_g, v_k_norm_g, v_w_mla_out, v_w_o, v_ln2_g, v_w_ffn_up, v_ffn_conv_w, v_ffn_conv_b, v_w_ffn_down):
    s, d = x.shape[1], x.shape[2]
    conv = conv_w.shape[2] * N_DEV
    ql, kvl = q_a_g.shape[1], kv_a_g.shape[1]
    heads = w_q_b.shape[2] * N_DEV // HEAD_QK
    dff = w_ffn_down.shape[1] * N_DEV
    hw = heads * LANE
    conv3 = 3 * conv
    kr_off = conv3 + ql
    kv_off = -(-(kr_off + LANE) // kvl) * kvl
    wa = kv_off + kvl
    assert conv3 % ql == 0 and kr_off % LANE == 0
    xs = x[0]
    tgt = loss_target[0]
    pos = positions.reshape(s, 1)

    nin = w_in.shape[2]
    big = dict(w_in=w_in[0].T, w_conv_out=w_conv_out[0], w_q_b=w_q_b[0], w_kv_b=w_kv_b[0],
               w_mla_out=w_mla_out[0], w_o=w_o[0], w_ffn_up=w_ffn_up[0], w_ffn_down=w_ffn_down[0])
    names = list(big)
    rest = names[1:]
    early = {}

    def while_w_in_travels(token):
        early["ag"] = _exchange_start([big[k].astype(BF) for k in rest], True, "gather_rest_start", dep=token)
        cos_sin = _rope_tables(pos)
        return cos_sin, _rms_fwd(xs, _after(early["ag"].token, ln1_g), d, 0, "rms1_fwd")

    first, ((cos, sin), u1) = _gather2([big["w_in"].astype(BF), _pad8(conv_w[0]), _pad8(ffn_conv_w[0])],
                                       while_w_in_travels, "gather_w_in")
    ag = early["ag"]
    cw8 = _unblock(first[1])
    fcw8 = _unblock(first[2])

    def landed(keys, after, name):
        return _exchange_wait(ag, [rest.index(k) for k in keys], after, name)

    w_in_t = first[0].reshape(N_DEV * nin, d)
    g_off = kr_off + kvl + ROPE
    w_a_t = jnp.concatenate([w_in_t[:kr_off], _lay_rows(w_in_t[kr_off + kvl:g_off]),
                             jnp.zeros((kv_off - kr_off - LANE, d), BF), w_in_t[kr_off:kr_off + kvl]], axis=0)[None]
    w_g_t = w_in_t[g_off:][None]
    gains = _pad8(jnp.concatenate([q_norm_g[:, :NOPE], _lay(q_norm_g[:, NOPE:]),
                                   k_norm_g[:, :NOPE], _lay(k_norm_g[:, NOPE:])], axis=0))
    kr_blk = kr_off // LANE

    z_a = _mm_nt(u1, w_a_t, "mm_z_a")
    z_g = _mm_nt(u1, w_g_t, "mm_z_g", out_dtype=BF)
    p = _conv_mix_fwd(z_a, cw8, conv)
    w_co, w_qb, w_kv = landed(["w_conv_out", "w_q_b", "w_kv_b"], p, "gather_wait_mixers")
    w_co = _unblock(w_co)[None]
    w_kv = _unblock(w_kv)[None]
    wq_full = _unblock(w_qb).reshape(ql, heads, HEAD_QK)
    w_q = jnp.concatenate([wq_full[:, :, :NOPE].reshape(ql, hw), _lay(wq_full[:, :, NOPE:]).reshape(ql, hw)],
                          axis=1)[None]
    yc = _mm_nn(p, w_co, "mm_y_conv", out_dtype=BF)
    qn = _rms_fwd(z_a, q_a_g, ql, conv3 // ql, "rms_q_fwd")
    kvn = _rms_fwd(z_a, kv_a_g, kvl, kv_off // kvl, "rms_kv_fwd")
    q_raw = _mm_nn(qn, w_q, "mm_q")
    kv_raw = _mm_nn(kvn, w_kv, "mm_kv")
    q_att, k_att, v_bf = _head_fwd(q_raw, kv_raw, z_a, kr_blk, cos, sin, gains, heads)
    o, o_bf, lse = _attn_fwd(q_att, k_att, v_bf, heads)
    w_mo, w_oo = landed(["w_mla_out", "w_o"], lse, "gather_wait_outs")
    w_mo = w_mo.reshape(1, hw, d)
    w_oo = w_oo.reshape(1, d, d)
    ym, mix = _mla_out_gate(o_bf, w_mo, z_g, b_gate, yc)
    h1 = _mm_nn(mix, w_oo, "mm_h1", add=xs)
    u2 = _rms_fwd(h1, ln2_g, d, 0, "rms2_fwd")
    w_up, = landed(["w_ffn_up"], u2, "gather_wait_ffn_up")
    a_pre = _mm_nn(u2, w_up, "mm_ffn_up")
    f = _ffn_act_fwd(a_pre, fcw8, ffn_conv_b, dff)
    w_dn, = landed(["w_ffn_down"], f, "gather_wait_ffn_down")
    w_dn = w_dn.reshape(1, dff, d)
    dy, dy_bf, loss_part = _mm_nn_loss(f, w_dn, h1, tgt, "mm_ffn_down_loss")

    g_dn = _mm_tn(f, dy_bf, 1, "mm_g_ffn_down").reshape(N_DEV, dff // N_DEV, d)
    rs_dn = _exchange_start([g_dn], False, "reduce_ffn_down_start")
    d_f = _mm_nt(dy_bf, w_dn, "mm_d_f", dep=rs_dn.token)
    d_xg, d_xu, dfw_g, dfw_u = _ffn_act_bwd(a_pre, d_f, fcw8, ffn_conv_b, dff)
    half = N_DEV // 2
    g_up = _mm_tn(u2, d_xg, half, "mm_g_ffn_up_gate", into=lax.empty((N_DEV, d, 2 * dff // N_DEV), BF))
    g_up = _mm_tn(u2, d_xu, half, "mm_g_ffn_up_up", into=g_up, blk0=half)
    rs_up = _exchange_start([g_up], False, "reduce_ffn_up_start")
    d_u2 = _mm_nt([d_xg, d_xu], w_up, "mm_d_u2", out_dtype=BF, dep=rs_up.token)
    d_h1, d_h1_bf, dg_ln2 = _rms_bwd(h1, d_u2, ln2_g, d, 0, "rms2_bwd", extra=dy, also_bf16=True)
    g_oo = _mm_tn(mix, d_h1_bf, 1, "mm_g_w_o").reshape(N_DEV, d // N_DEV, d)
    d_zga, d_zgb, d_yc, d_ym, dba, dbb = _d_mix_gate(d_h1_bf, w_oo, z_g, b_gate, yc, ym)
    g_co = _block(_mm_tn(p, d_yc, 1, "mm_g_conv_out")[0], N_DEV)
    g_mo = _mm_tn(o_bf, d_ym, 1, "mm_g_mla_out").reshape(N_DEV, hw // N_DEV, d)
    rs_mix = _exchange_start([g_oo, g_co, g_mo], False, "reduce_mixers_start")
    d_p = _mm_nt(d_yc, w_co, "mm_d_p", dep=rs_mix.token)
    d_o = _mm_nt(d_ym, w_mo, "mm_d_o", out_dtype=BF)
    d_zb, d_zc, d_zv, dcw = _conv_mix_bwd(z_a, d_p, cw8, conv)
    dq_att, dk_att, dv = _attn_bwd(q_att, k_att, v_bf, o, lse, d_o, heads, dep=rs_mix.token)
    d_q_raw, d_kv_raw, d_kr, dgains = _head_bwd(q_raw, kv_raw, z_a, kr_blk, cos, sin, gains, dq_att, dk_att, dv, heads)
    g_q2 = _mm_tn(qn, d_q_raw, 1, "mm_g_q")[0]
    g_qb = _block(jnp.concatenate([g_q2[:, :hw].reshape(ql, heads, NOPE),
                                   _unlay(g_q2[:, hw:].reshape(ql, heads, LANE))], axis=2).reshape(ql, heads * HEAD_QK), N_DEV)
    g_kv = _block(_mm_tn(kvn, d_kv_raw, 1, "mm_g_kv")[0], N_DEV)
    rs_qkv = _exchange_start([g_qb, g_kv], False, "reduce_qkv_start")
    d_qn = _mm_nt(d_q_raw, w_q, "mm_d_qn", dep=rs_qkv.token)
    d_kvn = _mm_nt(d_kv_raw, w_kv, "mm_d_kvn")
    d_ql, dg_qa = _rms_bwd(z_a, d_qn, q_a_g, ql, conv3 // ql, "rms_q_bwd", out_dtype=BF)
    d_kvl, dg_kva = _rms_bwd(z_a, d_kvn, kv_a_g, kvl, kv_off // kvl, "rms_kv_bwd", out_dtype=BF)
    d_z_a = jnp.concatenate([d_zb, d_zc, d_zv, d_ql, d_kr.astype(BF), jnp.zeros((s, kv_off - kr_off - LANE), BF),
                             d_kvl], axis=1)
    g_a = _mm_tn(d_z_a, u1, 1, "mm_g_w_a")[0]
    g_ga = _mm_tn(d_zga, u1, 1, "mm_g_w_ga")[0]
    g_gb = _mm_tn(d_zgb, u1, 1, "mm_g_w_gb")[0]
    g_in = jnp.concatenate([g_a[:kr_off], g_a[kv_off:kv_off + kvl], g_a[kr_off:kr_off + HALF],
                            g_a[kr_off + 2 * HALF:kr_off + 3 * HALF], g_ga, g_gb], axis=0).reshape(N_DEV, nin, d)
    rs_in = _exchange_start([g_in], False, "reduce_w_in_start")
    d_u1 = _mm_nn(d_z_a, w_a_t, "mm_d_u1_a", dep=rs_in.token)
    d_u1 = _mm_nn([d_zga, d_zgb], w_g_t, "mm_d_u1_g", add=d_u1)
    grad_x, dg_ln1 = _rms_bwd(xs, d_u1, ln1_g, d, 0, "rms1_bwd", extra=d_h1)

    summed = {}
    summed["w_ffn_down"], = _exchange_wait(rs_dn, [0], grad_x, "reduce_ffn_down_wait")
    summed["w_ffn_up"], = _exchange_wait(rs_up, [0], grad_x, "reduce_ffn_up_wait")
    summed["w_o"], summed["w_conv_out"], summed["w_mla_out"] = _exchange_wait(rs_mix, [0, 1, 2], grad_x, "reduce_mixers_wait")
    summed["w_q_b"], summed["w_kv_b"] = _exchange_wait(rs_qkv, [0, 1], grad_x, "reduce_qkv_wait")
    loc = locals()
    out = {}
    for k in rest:
        out[k] = _adamw(summed[k], big[k], loc["m_" + k][0], loc["v_" + k][0], "adamw_" + k)

    small = dict(ln1_g=dg_ln1[0:1], b_gate=jnp.concatenate([dba[0:1], dbb[0:1]], axis=1), q_a_g=dg_qa[0:1],
                 kv_a_g=dg_kva[0:1],
                 q_norm_g=jnp.concatenate([dgains[0:1], _unlay(dgains[1:2])], axis=1),
                 k_norm_g=jnp.concatenate([dgains[2:3], _unlay(dgains[3:4])], axis=1),
                 ln2_g=dg_ln2[0:1], ffn_conv_b=jnp.concatenate([dfw_g[3:4], dfw_u[3:4]], axis=1))
    small_names = list(small)
    extra = [dcw[0:3].reshape(1, -1), jnp.concatenate([dfw_g[0:3], dfw_u[0:3]], axis=1).reshape(1, -1),
             loss_part[0:1, 0:1]]
    flat = jnp.concatenate([small[k] for k in small_names] + extra, axis=1)
    n_flat = flat.shape[1]
    rows = -(-n_flat // (SUB * LANE)) * SUB
    flat = jnp.pad(flat, ((0, 0), (0, rows * LANE - n_flat))).reshape(rows, LANE)
    total = _sum_parts(_all_gather([flat], "gather_small", dep=[out[k][0] for k in rest])[0], "sum_small").reshape(1, rows * LANE)
    off = 0
    small_g = {}
    for k in small_names:
        small_g[k] = total[:, off:off + small[k].shape[1]]
        off += small[k].shape[1]
    me = 4 * lax.axis_index("x") + 2 * lax.axis_index("y") + lax.axis_index("c")
    cwn, fcwn = conv // N_DEV, 2 * dff // N_DEV
    g_cw = lax.dynamic_slice_in_dim(total[:, off:off + 3 * conv].reshape(3, conv), me * cwn, cwn, axis=1)
    off += 3 * conv
    g_fcw = lax.dynamic_slice_in_dim(total[:, off:off + 6 * dff].reshape(3, 2 * dff), me * fcwn, fcwn, axis=1)
    off += 6 * dff
    loss = total[0, off]

    summed["w_in"], = _exchange_wait(rs_in, [0], total, "reduce_w_in_wait")
    out["w_in"] = [r.T for r in _adamw(summed["w_in"], big["w_in"], m_w_in[0].T, v_w_in[0].T, "adamw_w_in",
                                       by_cols=True)]
    small_w = dict(ln1_g=ln1_g, b_gate=b_gate, q_a_g=q_a_g, kv_a_g=kv_a_g, q_norm_g=q_norm_g, k_norm_g=k_norm_g,
                   ln2_g=ln2_g, ffn_conv_b=ffn_conv_b, conv_w=conv_w[0].reshape(1, -1),
                   ffn_conv_w=ffn_conv_w[0].reshape(1, -1))
    small_g["conv_w"] = g_cw.reshape(1, -1)
    small_g["ffn_conv_w"] = g_fcw.reshape(1, -1)
    packed_names = list(small_w)

    def pack(get):
        vflat = jnp.concatenate([get(k).reshape(1, -1) for k in packed_names], axis=1)
        nr = -(-vflat.shape[1] // (SUB * LANE)) * SUB
        return jnp.pad(vflat, ((0, 0), (0, nr * LANE - vflat.shape[1])), constant_values=1.0).reshape(nr, LANE)

    res = _adamw(pack(lambda k: small_g[k])[None], pack(lambda k: small_w[k]), pack(lambda k: loc["m_" + k]),
                 pack(lambda k: loc["v_" + k]), "adamw_small")
    res = [r.reshape(1, -1) for r in res]
    off = 0
    for k in packed_names:
        shape = loc[k].shape
        size = small_w[k].shape[1]
        out[k] = [r[:, off:off + size].reshape(shape) for r in res]
        off += size
    for k in names:
        out[k] = [r[None] for r in out[k]]

    order = ["ln1_g", "w_in", "b_gate", "conv_w", "w_conv_out", "q_a_g", "w_q_b", "kv_a_g", "w_kv_b", "q_norm_g",
             "k_norm_g", "w_mla_out", "w_o", "ln2_g", "w_ffn_up", "ffn_conv_w", "ffn_conv_b", "w_ffn_down"]
    return (loss, grad_x[None], *[out[k][0] for k in order], *[out[k][1] for k in order],
            *[out[k][2] for k in order], *[out[k][3] for k in order])
```

```python
import functools

import jax
import jax.numpy as jnp
from jax import lax
from jax.experimental import pallas as pl
from jax.experimental.pallas import tpu as pltpu

BF = jnp.bfloat16
F32 = jnp.float32
MESH = pl.DeviceIdType.MESH
N_DEV = 8

NOPE = 128
ROPE = 64
HALF = ROPE // 2
HEAD_QK = NOPE + ROPE
HEAD_V = 128
LANE = 128
SUB = 8
QK_SCALE = HEAD_QK ** -0.5
LOG2_E = 1.4426950408889634
NORM_EPS = 1e-6
NEG_INF = -1e30
ROPE_THETA = 10000.0
ADAM_LR = 0.001
ADAM_B1 = 0.9
ADAM_B2 = 0.999
ADAM_EPS = 1e-08
ADAM_WD = 0.01
ADAM_STEP = 10

VMEM_LIMIT = 52 * 1024 * 1024
MM_TM, MM_TN, MM_TK, MM_TS = 1024, 1536, 2048, 2048
ROW_TILE, ROW_TILE_BWD = 512, 256
HEAD_ROW_TILE, HEAD_ROW_TILE_BWD = 256, 128
COL_TILE = 512
ATTN_TILE = 1024
ATTN_TILE_FWD = 1024
ANY = pl.BlockSpec(memory_space=pl.ANY)


def _pick(n, target, mult):
    t = (min(n, target) // mult) * mult
    while t > 0:
        if n % t == 0:
            return t
        t -= mult
    raise ValueError(f"no tile for {n} (target {target}, multiple {mult})")


def _cp(*sem):
    return pltpu.CompilerParams(dimension_semantics=sem, vmem_limit_bytes=VMEM_LIMIT)


def _accumulate(kk, nk, acc, part, finish):
    if nk == 1:
        finish(part())
        return

    @pl.when(kk == 0)
    def _():
        acc[...] = part()

    @pl.when((kk > 0) & (kk < nk - 1))
    def _():
        acc[...] += part()

    @pl.when(kk == nk - 1)
    def _():
        finish(acc[...] + part())


def _mm_call(body, name, grid, in_specs, args, out_spec, out_shape, acc_shape, nk, dep):
    if dep is not None:
        in_specs = in_specs + [ANY]
        args = args + [dep]
    return pl.pallas_call(
        body, name=name, grid=grid, in_specs=in_specs, out_specs=out_spec, out_shape=out_shape,
        scratch_shapes=[pltpu.VMEM(acc_shape, F32)] if nk > 1 else [],
        compiler_params=_cp("parallel", "parallel", "arbitrary"),
    )(*args)


def _mm_nn_loss(a, b3, add, target, name):
    m, k = a.shape
    _, k2, n = b3.shape
    assert k == k2 and b3.shape[0] == 1
    tm = _pick(m, MM_TM, 16)
    tn = _pick(n, MM_TN, LANE)
    tk = _pick(k, MM_TK, LANE)
    nk = k // tk

    def body(a_ref, b_ref, c_ref, t_ref, dy_ref, dyb_ref, l_ref, acc):
        kk = pl.program_id(2)

        @pl.when((pl.program_id(0) == 0) & (pl.program_id(1) == 0) & (kk == 0))
        def _():
            l_ref[...] = jnp.zeros_like(l_ref)

        def part():
            return jnp.dot(a_ref[...].astype(BF), b_ref[0].astype(BF), preferred_element_type=F32)

        def finish(r):
            e = r + c_ref[...] - t_ref[...]
            dy_ref[...] = e / n
            dyb_ref[...] = (e / n).astype(BF)
            l_ref[...] += 0.5 * jnp.sum(jnp.sum(e * e, axis=-1, keepdims=True), axis=0, keepdims=True) / n

        _accumulate(kk, nk, acc, part, finish)

    tile = pl.BlockSpec((tm, tn), lambda i, j, kk: (i, j))
    return pl.pallas_call(
        body, name=name, grid=(m // tm, n // tn, nk),
        in_specs=[pl.BlockSpec((tm, tk), lambda i, j, kk: (i, kk)),
                  pl.BlockSpec((1, tk, tn), lambda i, j, kk: (0, kk, j)), tile, tile],
        out_specs=[tile, tile, pl.BlockSpec((SUB, LANE), lambda i, j, kk: (0, 0))],
        out_shape=[jax.ShapeDtypeStruct((m, n), F32), jax.ShapeDtypeStruct((m, n), BF),
                   jax.ShapeDtypeStruct((SUB, LANE), F32)],
        scratch_shapes=[pltpu.VMEM((tm, tn), F32)],
        compiler_params=_cp("arbitrary", "arbitrary", "arbitrary"),
    )(a, b3, add, target)


def _mm_nn(a, b3, name, add=None, out_dtype=F32, blk0=0, nblk=None, dep=None):
    pair = isinstance(a, (list, tuple))
    a_list = list(a) if pair else [a]
    m, ka = a_list[0].shape
    k = ka * len(a_list)
    nb_all, k2, nbw = b3.shape
    assert k == k2
    nblk = nb_all - blk0 if nblk is None else nblk
    n = nblk * nbw
    tm = _pick(m, MM_TM if k > MM_TM else 2 * MM_TM, 16)
    tn = _pick(nbw, MM_TN, LANE)
    tk = _pick(ka, MM_TK, LANE)
    per = nbw // tn
    nk = k // tk
    nka = ka // tk
    na_ops = len(a_list)

    def body(*refs):
        a_refs, b_ref = refs[:na_ops], refs[na_ops]
        c_ref = refs[na_ops + 1] if add is not None else None
        o_ref = refs[na_ops + 1 + (add is not None) + (dep is not None)]
        acc = refs[-1]
        kk = pl.program_id(2)

        def part():
            av = a_refs[0][...] if not pair else jnp.where(kk < nka, a_refs[0][...], a_refs[1][...])
            return jnp.dot(av.astype(BF), b_ref[...].astype(BF), preferred_element_type=F32)

        def finish(r):
            if add is not None:
                r = r + c_ref[...]
            o_ref[...] = r.astype(out_dtype)

        _accumulate(kk, nk, acc, part, finish)

    if pair:
        in_specs = [pl.BlockSpec((tm, tk), lambda i, j, kk: (i, jnp.minimum(kk, nka - 1))),
                    pl.BlockSpec((tm, tk), lambda i, j, kk: (i, jnp.maximum(kk - nka, 0)))]
    else:
        in_specs = [pl.BlockSpec((tm, tk), lambda i, j, kk: (i, kk))]
    in_specs.append(pl.BlockSpec((None, tk, tn), lambda i, j, kk: (blk0 + j // per, kk, j % per)))
    args = a_list + [b3]
    if add is not None:
        in_specs.append(pl.BlockSpec((tm, tn), lambda i, j, kk: (i, j)))
        args.append(add)
    return _mm_call(body, name, (m // tm, n // tn, nk), in_specs, args,
                    pl.BlockSpec((tm, tn), lambda i, j, kk: (i, j)), jax.ShapeDtypeStruct((m, n), out_dtype),
                    (tm, tn), nk, dep)


def _mm_nt(a, b3, name, add=None, out_dtype=F32, blk0=0, nblk=None, dep=None):
    pair = isinstance(a, (list, tuple))
    a_list = list(a) if pair else [a]
    m, na = a_list[0].shape
    n = na * len(a_list)
    nb_all, k, nbw = b3.shape
    nblk = nb_all - blk0 if nblk is None else nblk
    assert n == nblk * nbw and na % nbw == 0
    tm = _pick(m, 2 * MM_TM if k <= MM_TM and n <= MM_TK else MM_TM, 16)
    tk = _pick(nbw, MM_TK, LANE)
    per = nbw // tk
    nk = n // tk
    tn = _pick(k, MM_TN if nk <= 2 else 2 * MM_TM, LANE)
    nka = na // tk
    na_ops = len(a_list)

    def body(*refs):
        a_refs, b_ref = refs[:na_ops], refs[na_ops]
        c_ref = refs[na_ops + 1] if add is not None else None
        o_ref = refs[na_ops + 1 + (add is not None) + (dep is not None)]
        acc = refs[-1]
        kk = pl.program_id(2)

        def part():
            av = a_refs[0][...] if not pair else jnp.where(kk < nka, a_refs[0][...], a_refs[1][...])
            return lax.dot_general(av.astype(BF), b_ref[...].astype(BF),
                                   (((1,), (1,)), ((), ())), preferred_element_type=F32)

        def finish(r):
            if add is not None:
                r = r + c_ref[...]
            o_ref[...] = r.astype(out_dtype)

        _accumulate(kk, nk, acc, part, finish)

    if pair:
        in_specs = [pl.BlockSpec((tm, tk), lambda i, j, kk: (i, jnp.minimum(kk, nka - 1))),
                    pl.BlockSpec((tm, tk), lambda i, j, kk: (i, jnp.maximum(kk - nka, 0)))]
    else:
        in_specs = [pl.BlockSpec((tm, tk), lambda i, j, kk: (i, kk))]
    in_specs.append(pl.BlockSpec((None, tn, tk), lambda i, j, kk: (blk0 + kk // per, j, kk % per)))
    args = a_list + [b3]
    if add is not None:
        in_specs.append(pl.BlockSpec((tm, tn), lambda i, j, kk: (i, j)))
        args.append(add)
    return _mm_call(body, name, (m // tm, k // tn, nk), in_specs, args,
                    pl.BlockSpec((tm, tn), lambda i, j, kk: (i, j)), jax.ShapeDtypeStruct((m, k), out_dtype),
                    (tm, tn), nk, dep)


def _mm_tn(a, b, nblk, name, out_dtype=BF, dep=None, into=None, blk0=0):
    s, m = a.shape
    s2, n = b.shape
    assert s == s2 and n % nblk == 0 and (dep is None or into is None)
    nbw = n // nblk
    tm = _pick(m, MM_TN, LANE)
    tn = _pick(nbw, MM_TN, LANE)
    ts = _pick(s, MM_TS, LANE)
    per = nbw // tn
    ns = s // ts

    def body(*refs):
        a_ref, b_ref = refs[:2]
        o_ref = refs[2 + (dep is not None or into is not None)]
        acc = refs[-1]

        def part():
            return lax.dot_general(a_ref[...].astype(BF), b_ref[...].astype(BF),
                                   (((0,), (0,)), ((), ())), preferred_element_type=F32)

        def finish(r):
            o_ref[...] = r.astype(out_dtype)

        _accumulate(pl.program_id(2), ns, acc, part, finish)

    in_specs = [pl.BlockSpec((ts, tm), lambda i, j, ss: (ss, i)),
                pl.BlockSpec((ts, tn), lambda i, j, ss: (ss, j))]
    out_spec = pl.BlockSpec((None, tm, tn), lambda i, j, ss: (blk0 + j // per, i, j % per))
    if into is None:
        return _mm_call(body, name, (m // tm, n // tn, ns), in_specs, [a, b], out_spec,
                        jax.ShapeDtypeStruct((nblk, m, nbw), out_dtype), (tm, tn), ns, dep)
    assert into.shape[1:] == (m, nbw) and into.dtype == out_dtype
    return pl.pallas_call(
        body, name=name, grid=(m // tm, n // tn, ns), in_specs=in_specs + [ANY], out_specs=out_spec,
        out_shape=jax.ShapeDtypeStruct(into.shape, out_dtype), input_output_aliases={2: 0},
        scratch_shapes=[pltpu.VMEM((tm, tn), F32)] if ns > 1 else [],
        compiler_params=_cp("parallel", "parallel", "arbitrary"),
    )(a, b, into)


def _rows8(rows, width):
    idx = lax.broadcasted_iota(jnp.int32, (SUB, width), 0)
    out = jnp.zeros((SUB, width), F32)
    for r, v in enumerate(rows):
        out = jnp.where(idx == r, v, out)
    return out


def _rms_fwd(x, g, width, col_blk, name):
    s = x.shape[0]
    tr = _pick(s, ROW_TILE, 16)

    def body(x_ref, g_ref, u_ref):
        xv = x_ref[...]
        r = lax.rsqrt(jnp.mean(xv * xv, axis=-1, keepdims=True) + NORM_EPS)
        u_ref[...] = ((xv * r) * g_ref[...]).astype(BF)

    return pl.pallas_call(
        body, name=name, grid=(s // tr,),
        in_specs=[pl.BlockSpec((tr, width), lambda i: (i, col_blk)),
                  pl.BlockSpec((1, width), lambda i: (0, 0))],
        out_specs=pl.BlockSpec((tr, width), lambda i: (i, 0)),
        out_shape=jax.ShapeDtypeStruct((s, width), BF),
        compiler_params=_cp("parallel"),
    )(x, g)


def _rms_bwd(x, du, g, width, col_blk, name, extra=None, out_dtype=F32, also_bf16=False):
    s = x.shape[0]
    tr = _pick(s, ROW_TILE_BWD, 16)

    def body(*refs):
        x_ref, du_ref, g_ref = refs[:3]
        e_ref = refs[3] if extra is not None else None
        dx_ref = refs[3 + (extra is not None)]
        dxb_ref = refs[4 + (extra is not None)] if also_bf16 else None
        dg_ref = refs[-1]
        i = pl.program_id(0)
        xv = x_ref[...]
        duv = du_ref[...].astype(F32)
        r = lax.rsqrt(jnp.mean(xv * xv, axis=-1, keepdims=True) + NORM_EPS)
        nv = xv * r
        dn = duv * g_ref[...]
        dx = r * (dn - nv * jnp.mean(dn * nv, axis=-1, keepdims=True))
        if extra is not None:
            dx = dx + e_ref[...]
        dx_ref[...] = dx.astype(out_dtype)
        if also_bf16:
            dxb_ref[...] = dx.astype(BF)

        @pl.when(i == 0)
        def _():
            dg_ref[...] = jnp.zeros_like(dg_ref)

        dg_ref[...] += _rows8([jnp.sum(duv * nv, axis=0, keepdims=True)], width)

    in_specs = [pl.BlockSpec((tr, width), lambda i: (i, col_blk)),
                pl.BlockSpec((tr, width), lambda i: (i, 0)),
                pl.BlockSpec((1, width), lambda i: (0, 0))]
    args = [x, du, g]
    if extra is not None:
        in_specs.append(pl.BlockSpec((tr, width), lambda i: (i, 0)))
        args.append(extra)
    return pl.pallas_call(
        body, name=name, grid=(s // tr,),
        in_specs=in_specs,
        out_specs=[pl.BlockSpec((tr, width), lambda i: (i, 0))] * (1 + also_bf16)
        + [pl.BlockSpec((SUB, width), lambda i: (0, 0))],
        out_shape=[jax.ShapeDtypeStruct((s, width), out_dtype)] + [jax.ShapeDtypeStruct((s, width), BF)] * also_bf16
        + [jax.ShapeDtypeStruct((SUB, width), F32)],
        compiler_params=_cp("arbitrary"),
    )(*args)


def _down(cur, prev8, k):
    ext = jnp.concatenate([prev8, cur], axis=0)
    return pltpu.roll(ext, k, axis=0)[SUB:]


def _up(cur, next8, k):
    ext = jnp.concatenate([cur, next8], axis=0)
    return pltpu.roll(ext, ext.shape[0] - k, axis=0)[:cur.shape[0]]


def _lags(cur, prev8):
    return _down(cur, prev8, 1), _down(cur, prev8, 2)


def _conv3(w_ref, cur, prev8, lags=None):
    lag1, lag2 = _lags(cur, prev8) if lags is None else lags
    return w_ref[0:1, :] * lag2 + w_ref[1:2, :] * lag1 + w_ref[2:3, :] * cur


def _conv3_t(w_ref, cur, next8):
    return w_ref[2:3, :] * cur + w_ref[1:2, :] * _up(cur, next8, 1) + w_ref[0:1, :] * _up(cur, next8, 2)


def _spec_cur(tr, tc, c0):
    return pl.BlockSpec((tr, tc), lambda j, i: (i, c0 + j))


def _spec_prev(tr, tc, c0):
    return pl.BlockSpec((SUB, tc), lambda j, i: (jnp.maximum(i * (tr // SUB) - 1, 0), c0 + j))


def _spec_next(tr, tc, c0, s):
    return pl.BlockSpec((SUB, tc), lambda j, i: (jnp.minimum((i + 1) * (tr // SUB), s // SUB - 1), c0 + j))


def _spec_w(tc, c0):
    return pl.BlockSpec((SUB, tc), lambda j, i: (0, c0 + j))


def _pad8(w):
    return jnp.pad(w, ((0, SUB - w.shape[0]), (0, 0)))


def _conv_mix_fwd(z_a, cw8, conv):
    s = z_a.shape[0]
    tr = _pick(s, ROW_TILE, 16)
    tc = _pick(conv, COL_TILE, LANE)
    nc = conv // tc

    def body(zb_ref, zc_ref, zv_ref, zcp_ref, zvp_ref, w_ref, p_ref):
        i = pl.program_id(1)
        cv = zc_ref[...] * zv_ref[...]
        cvp = jnp.where(i > 0, zcp_ref[...] * zvp_ref[...], 0.0)
        p_ref[...] = (zb_ref[...] * _conv3(w_ref, cv, cvp)).astype(BF)

    return pl.pallas_call(
        body, name="conv_mix_fwd", grid=(nc, s // tr),
        in_specs=[_spec_cur(tr, tc, 0), _spec_cur(tr, tc, nc), _spec_cur(tr, tc, 2 * nc),
                  _spec_prev(tr, tc, nc), _spec_prev(tr, tc, 2 * nc), _spec_w(tc, 0)],
        out_specs=_spec_cur(tr, tc, 0),
        out_shape=jax.ShapeDtypeStruct((s, conv), BF),
        compiler_params=_cp("parallel", "parallel"),
    )(z_a, z_a, z_a, z_a, z_a, cw8)


def _conv_mix_bwd(z_a, d_p, cw8, conv):
    s = z_a.shape[0]
    tr = _pick(s, ROW_TILE_BWD, 16)
    tc = _pick(conv, COL_TILE, LANE)
    nc = conv // tc
    nr = s // tr

    def body(zb_ref, zbn_ref, zc_ref, zcp_ref, zv_ref, zvp_ref, dp_ref, dpn_ref, w_ref,
             dzb_ref, dzc_ref, dzv_ref, dw_ref):
        i = pl.program_id(1)
        zc = zc_ref[...]
        zv = zv_ref[...]
        cv = zc * zv
        cvp = jnp.where(i > 0, zcp_ref[...] * zvp_ref[...], 0.0)
        cv1, cv2 = _lags(cv, cvp)
        dpv = dp_ref[...]
        dzb_ref[...] = (dpv * _conv3(w_ref, cv, cvp, (cv1, cv2))).astype(BF)
        dcc = dpv * zb_ref[...]
        dccn = jnp.where(i < nr - 1, dpn_ref[...] * zbn_ref[...], 0.0)
        dcv = _conv3_t(w_ref, dcc, dccn)
        dzc_ref[...] = (dcv * zv).astype(BF)
        dzv_ref[...] = (dcv * zc).astype(BF)

        @pl.when(i == 0)
        def _():
            dw_ref[...] = jnp.zeros_like(dw_ref)

        dw_ref[...] += _rows8([jnp.sum(dcc * cv2, axis=0, keepdims=True),
                               jnp.sum(dcc * cv1, axis=0, keepdims=True),
                               jnp.sum(dcc * cv, axis=0, keepdims=True)], tc)

    out = jax.ShapeDtypeStruct((s, conv), BF)
    return pl.pallas_call(
        body, name="conv_mix_bwd", grid=(nc, nr),
        in_specs=[_spec_cur(tr, tc, 0), _spec_next(tr, tc, 0, s),
                  _spec_cur(tr, tc, nc), _spec_prev(tr, tc, nc),
                  _spec_cur(tr, tc, 2 * nc), _spec_prev(tr, tc, 2 * nc),
                  _spec_cur(tr, tc, 0), _spec_next(tr, tc, 0, s), _spec_w(tc, 0)],
        out_specs=[_spec_cur(tr, tc, 0), _spec_cur(tr, tc, 0), _spec_cur(tr, tc, 0), _spec_w(tc, 0)],
        out_shape=[out, out, out, jax.ShapeDtypeStruct((SUB, conv), F32)],
        compiler_params=_cp("parallel", "arbitrary"),
    )(z_a, z_a, z_a, z_a, z_a, z_a, d_p, d_p, cw8)


def _silu_parts(ag):
    sg = jax.nn.sigmoid(ag)
    return ag * sg, sg


def _ffn_act_fwd(a_pre, cw8, cb, dff):
    s = a_pre.shape[0]
    tr = _pick(s, ROW_TILE, 16)
    tc = _pick(dff, COL_TILE, LANE)
    nc = dff // tc

    def body(xg_ref, xgp_ref, xu_ref, xup_ref, wg_ref, wu_ref, bg_ref, bu_ref, f_ref):
        i = pl.program_id(1)
        xgp = jnp.where(i > 0, xgp_ref[...], 0.0)
        xup = jnp.where(i > 0, xup_ref[...], 0.0)
        ag = _conv3(wg_ref, xg_ref[...], xgp) + bg_ref[...]
        au = _conv3(wu_ref, xu_ref[...], xup) + bu_ref[...]
        f_ref[...] = (_silu_parts(ag)[0] * au).astype(BF)

    return pl.pallas_call(
        body, name="ffn_act_fwd", grid=(nc, s // tr),
        in_specs=[_spec_cur(tr, tc, 0), _spec_prev(tr, tc, 0), _spec_cur(tr, tc, nc), _spec_prev(tr, tc, nc),
                  _spec_w(tc, 0), _spec_w(tc, nc),
                  pl.BlockSpec((1, tc), lambda j, i: (0, j)), pl.BlockSpec((1, tc), lambda j, i: (0, nc + j))],
        out_specs=_spec_cur(tr, tc, 0),
        out_shape=jax.ShapeDtypeStruct((s, dff), BF),
        compiler_params=_cp("parallel", "parallel"),
    )(a_pre, a_pre, a_pre, a_pre, cw8, cw8, cb, cb)


def _ffn_act_bwd(a_pre, d_f, cw8, cb, dff):
    s = a_pre.shape[0]
    tr = _pick(s, ROW_TILE_BWD, 16)
    tc = _pick(dff, COL_TILE, LANE)
    nc = dff // tc
    nr = s // tr

    def body(xg_ref, xgp_ref, xgn_ref, xu_ref, xup_ref, xun_ref, df_ref, dfn_ref,
             wg_ref, wu_ref, bg_ref, bu_ref, dxg_ref, dxu_ref, dwg_ref, dwu_ref):
        i = pl.program_id(1)
        xg = xg_ref[...]
        xu = xu_ref[...]
        xgp = jnp.where(i > 0, xgp_ref[...], 0.0)
        xup = jnp.where(i > 0, xup_ref[...], 0.0)

        def d_act(xg_t, xgp_t, xu_t, xup_t, df_t, lags_g=None, lags_u=None):
            ag = _conv3(wg_ref, xg_t, xgp_t, lags_g) + bg_ref[...]
            au = _conv3(wu_ref, xu_t, xup_t, lags_u) + bu_ref[...]
            sil, sg = _silu_parts(ag)
            return df_t * au * (sg * (1.0 + ag * (1.0 - sg))), df_t * sil

        lags_g = _lags(xg, xgp)
        lags_u = _lags(xu, xup)
        dag, dau = d_act(xg, xgp, xu, xup, df_ref[...], lags_g, lags_u)
        dfn = jnp.where(i < nr - 1, dfn_ref[...], 0.0)
        dagn, daun = d_act(xgn_ref[...], xg[tr - SUB:], xun_ref[...], xu[tr - SUB:], dfn)
        dxg_ref[...] = _conv3_t(wg_ref, dag, dagn).astype(BF)
        dxu_ref[...] = _conv3_t(wu_ref, dau, daun).astype(BF)

        @pl.when(i == 0)
        def _():
            dwg_ref[...] = jnp.zeros_like(dwg_ref)
            dwu_ref[...] = jnp.zeros_like(dwu_ref)

        def wgrad(da, x, lags):
            return _rows8([jnp.sum(da * lags[1], axis=0, keepdims=True),
                           jnp.sum(da * lags[0], axis=0, keepdims=True),
                           jnp.sum(da * x, axis=0, keepdims=True),
                           jnp.sum(da, axis=0, keepdims=True)], tc)

        dwg_ref[...] += wgrad(dag, xg, lags_g)
        dwu_ref[...] += wgrad(dau, xu, lags_u)

    half = jax.ShapeDtypeStruct((s, dff), BF)
    wsh = jax.ShapeDtypeStruct((SUB, dff), F32)
    return pl.pallas_call(
        body, name="ffn_act_bwd", grid=(nc, nr),
        in_specs=[_spec_cur(tr, tc, 0), _spec_prev(tr, tc, 0), _spec_next(tr, tc, 0, s),
                  _spec_cur(tr, tc, nc), _spec_prev(tr, tc, nc), _spec_next(tr, tc, nc, s),
                  _spec_cur(tr, tc, 0), _spec_next(tr, tc, 0, s),
                  _spec_w(tc, 0), _spec_w(tc, nc),
                  pl.BlockSpec((1, tc), lambda j, i: (0, j)), pl.BlockSpec((1, tc), lambda j, i: (0, nc + j))],
        out_specs=[_spec_cur(tr, tc, 0), _spec_cur(tr, tc, 0), _spec_w(tc, 0), _spec_w(tc, 0)],
        out_shape=[half, half, wsh, wsh],
        compiler_params=_cp("parallel", "arbitrary"),
    )(a_pre, a_pre, a_pre, a_pre, a_pre, a_pre, d_f, d_f, cw8, cw8, cb, cb)


def _mla_out_gate(o, w_mo, z_g, b_gate, yc):
    m, k = o.shape
    d = w_mo.shape[2]
    tm = _pick(m, MM_TM, 16)
    tn = _pick(d, MM_TM, LANE)
    nc = d // tn

    def body(a_ref, b_ref, za_ref, zb_ref, ba_ref, bb_ref, yc_ref, ym_ref, mix_ref):
        ym = jnp.dot(a_ref[...], b_ref[0], preferred_element_type=F32)
        ga = jax.nn.sigmoid(za_ref[...] + ba_ref[...])
        gb = jax.nn.sigmoid(zb_ref[...] + bb_ref[...])
        ym_ref[...] = ym.astype(BF)
        mix_ref[...] = (ga * yc_ref[...] + gb * ym).astype(BF)

    tile = pl.BlockSpec((tm, tn), lambda i, j: (i, j))
    out = jax.ShapeDtypeStruct((m, d), BF)
    return pl.pallas_call(
        body, name="mm_y_mla_gate", grid=(m // tm, nc),
        in_specs=[pl.BlockSpec((tm, k), lambda i, j: (i, 0)), pl.BlockSpec((1, k, tn), lambda i, j: (0, 0, j)),
                  tile, pl.BlockSpec((tm, tn), lambda i, j: (i, nc + j)),
                  pl.BlockSpec((1, tn), lambda i, j: (0, j)), pl.BlockSpec((1, tn), lambda i, j: (0, nc + j)), tile],
        out_specs=[tile, tile], out_shape=[out, out],
        compiler_params=_cp("parallel", "parallel"),
    )(o, w_mo, z_g, z_g, b_gate, b_gate, yc)


def _d_mix_gate(d_h1, w_oo, z_g, b_gate, yc, ym):
    m, n = d_h1.shape
    d = w_oo.shape[1]
    tm = _pick(m, MM_TM, 16)
    tn = _pick(d, COL_TILE, LANE)
    nc = d // tn

    def body(a_ref, b_ref, za_ref, zb_ref, ba_ref, bb_ref, yc_ref, ym_ref,
             dza_ref, dzb_ref, dyc_ref, dym_ref, dba_ref, dbb_ref):
        i = pl.program_id(1)
        dm = lax.dot_general(a_ref[...], b_ref[0], (((1,), (1,)), ((), ())), preferred_element_type=F32)
        ga = jax.nn.sigmoid(za_ref[...] + ba_ref[...])
        gb = jax.nn.sigmoid(zb_ref[...] + bb_ref[...])
        dza = dm * yc_ref[...] * (ga * (1.0 - ga))
        dzb = dm * ym_ref[...] * (gb * (1.0 - gb))
        dza_ref[...] = dza.astype(BF)
        dzb_ref[...] = dzb.astype(BF)
        dyc_ref[...] = (dm * ga).astype(BF)
        dym_ref[...] = (dm * gb).astype(BF)

        @pl.when(i == 0)
        def _():
            dba_ref[...] = jnp.zeros_like(dba_ref)
            dbb_ref[...] = jnp.zeros_like(dbb_ref)

        dba_ref[...] += _rows8([jnp.sum(dza, axis=0, keepdims=True)], tn)
        dbb_ref[...] += _rows8([jnp.sum(dzb, axis=0, keepdims=True)], tn)

    tile = pl.BlockSpec((tm, tn), lambda j, i: (i, j))
    act = jax.ShapeDtypeStruct((m, d), BF)
    bsh = jax.ShapeDtypeStruct((SUB, d), F32)
    return pl.pallas_call(
        body, name="mm_d_mix_gate", grid=(nc, m // tm),
        in_specs=[pl.BlockSpec((tm, n), lambda j, i: (i, 0)), pl.BlockSpec((1, tn, n), lambda j, i: (0, j, 0)),
                  tile, pl.BlockSpec((tm, tn), lambda j, i: (i, nc + j)),
                  pl.BlockSpec((1, tn), lambda j, i: (0, j)), pl.BlockSpec((1, tn), lambda j, i: (0, nc + j)),
                  tile, tile],
        out_specs=[tile] * 4 + [pl.BlockSpec((SUB, tn), lambda j, i: (0, j))] * 2,
        out_shape=[act, act, act, act, bsh, bsh],
        compiler_params=_cp("parallel", "arbitrary"),
    )(d_h1, w_oo, z_g, z_g, b_gate, b_gate, yc, ym)


def _lay(v):
    z = jnp.zeros(v.shape[:-1] + (HALF,), v.dtype)
    return jnp.concatenate([v[..., :HALF], z, v[..., HALF:], z], axis=-1)


def _unlay(v):
    return jnp.concatenate([v[..., :HALF], v[..., 2 * HALF:3 * HALF]], axis=-1)


def _lay_rows(v):
    z = jnp.zeros((HALF,) + v.shape[1:], v.dtype)
    return jnp.concatenate([v[:HALF], z, v[HALF:], z], axis=0)


def _rope_tables(positions):
    s = positions.shape[0]
    tr = _pick(s, ROW_TILE, 8)
    inv_freq = ROPE_THETA ** (-jnp.arange(0, ROPE, 2, dtype=F32) / ROPE)
    consts = jnp.stack([_lay(jnp.concatenate([inv_freq, inv_freq])),
                        _lay(jnp.ones((ROPE,), F32)),
                        _lay(jnp.concatenate([-jnp.ones((HALF,), F32), jnp.ones((HALF,), F32)]))])
    consts = _pad8(consts)

    def body(p_ref, c_ref, cos_ref, sin_ref):
        ang = p_ref[...].astype(F32) * c_ref[0:1, :]
        cos_ref[...] = jnp.cos(ang) * c_ref[1:2, :]
        sin_ref[...] = jnp.sin(ang) * c_ref[2:3, :]

    tab = jax.ShapeDtypeStruct((s, LANE), F32)
    return pl.pallas_call(
        body, name="rope_tables", grid=(s // tr,),
        in_specs=[pl.BlockSpec((tr, 1), lambda i: (i, 0)), pl.BlockSpec((SUB, LANE), lambda i: (0, 0))],
        out_specs=[pl.BlockSpec((tr, LANE), lambda i: (i, 0))] * 2,
        out_shape=[tab, tab],
        compiler_params=_cp("parallel"),
    )(positions, consts)


def _lane_sum(p):
    return jnp.sum(p, axis=-1, keepdims=True)


def _rope(t, cos, sin):
    return t * cos + pltpu.roll(t, 2 * HALF, axis=1) * sin


def _rope_t(d, cos, sin):
    return d * cos + pltpu.roll(d * sin, 2 * HALF, axis=1)


def _head_fwd(q_raw, kv_raw, z_a, kr_blk, cos, sin, gains, heads):
    s = q_raw.shape[0]
    tr = _pick(s, HEAD_ROW_TILE, 16)
    hw = heads * LANE

    def body(q_ref, kv_ref, kr_ref, cos_ref, sin_ref, g_ref, qo_ref, ko_ref, vo_ref):
        cosv = cos_ref[...]
        sinv = sin_ref[...]
        krv = kr_ref[...]
        kr_sq = krv * krv
        for h in range(heads):
            lo = h * LANE
            qn = q_ref[:, lo:lo + LANE]
            qr = q_ref[:, hw + lo:hw + lo + LANE]
            r = lax.rsqrt(_lane_sum(qn * qn + qr * qr) / HEAD_QK + NORM_EPS)
            qo_ref[:, 2 * lo:2 * lo + LANE] = (((qn * r) * g_ref[0:1, :]) * (QK_SCALE * LOG2_E)).astype(BF)
            qo_ref[:, 2 * lo + LANE:2 * lo + 2 * LANE] = (
                _rope((qr * r) * g_ref[1:2, :], cosv, sinv) * (QK_SCALE * LOG2_E)).astype(BF)
            kn = kv_ref[:, 2 * lo:2 * lo + LANE]
            r = lax.rsqrt(_lane_sum(kn * kn + kr_sq) / HEAD_QK + NORM_EPS)
            ko_ref[:, 2 * lo:2 * lo + LANE] = ((kn * r) * g_ref[2:3, :]).astype(BF)
            ko_ref[:, 2 * lo + LANE:2 * lo + 2 * LANE] = _rope((krv * r) * g_ref[3:4, :], cosv, sinv).astype(BF)
            vo_ref[:, lo:lo + LANE] = kv_ref[:, 2 * lo + LANE:2 * lo + 2 * LANE].astype(BF)

    row = lambda w: pl.BlockSpec((tr, w), lambda i: (i, 0))
    return pl.pallas_call(
        body, name="head_fwd", grid=(s // tr,),
        in_specs=[row(2 * hw), row(2 * hw), pl.BlockSpec((tr, LANE), lambda i: (i, kr_blk)),
                  row(LANE), row(LANE), pl.BlockSpec((SUB, LANE), lambda i: (0, 0))],
        out_specs=[row(2 * hw), row(2 * hw), row(hw)],
        out_shape=[jax.ShapeDtypeStruct((s, 2 * hw), BF), jax.ShapeDtypeStruct((s, 2 * hw), BF),
                   jax.ShapeDtypeStruct((s, hw), BF)],
        compiler_params=_cp("parallel"),
    )(q_raw, kv_raw, z_a, cos, sin, gains)


def _head_bwd(q_raw, kv_raw, z_a, kr_blk, cos, sin, gains, dq_att, dk_att, dv, heads):
    s = q_raw.shape[0]
    tr = _pick(s, HEAD_ROW_TILE_BWD, 16)
    hw = heads * LANE

    def body(q_ref, kv_ref, kr_ref, cos_ref, sin_ref, g_ref, dq_ref, dk_ref, dv_ref,
             dqr_ref, dkv_ref, dkr_ref, dg_ref):
        i = pl.program_id(0)
        cosv = cos_ref[...]
        sinv = sin_ref[...]
        krv = kr_ref[...]
        kr_sq = krv * krv
        dkr = jnp.zeros((tr, LANE), F32)
        dgs = [jnp.zeros((1, LANE), F32) for _ in range(4)]

        def norm_bwd(xn, xr, sq, dn_out, dr_out, gn, gr):
            r = lax.rsqrt(_lane_sum(sq) / HEAD_QK + NORM_EPS)
            nn = xn * r
            nr = xr * r
            dt = _rope_t(dr_out, cosv, sinv)
            dnn = dn_out * gn
            dnr = dt * gr
            mean = _lane_sum(dnn * nn + dnr * nr) / HEAD_QK
            return (r * (dnn - nn * mean), r * (dnr - nr * mean),
                    jnp.sum(dn_out * nn, axis=0, keepdims=True), jnp.sum(dt * nr, axis=0, keepdims=True))

        for h in range(heads):
            lo = h * LANE
            qn = q_ref[:, lo:lo + LANE]
            qr = q_ref[:, hw + lo:hw + lo + LANE]
            dxn, dxr, g0, g1 = norm_bwd(qn, qr, qn * qn + qr * qr, dq_ref[:, 2 * lo:2 * lo + LANE] * QK_SCALE,
                                        dq_ref[:, 2 * lo + LANE:2 * lo + 2 * LANE] * QK_SCALE,
                                        g_ref[0:1, :], g_ref[1:2, :])
            dqr_ref[:, lo:lo + LANE] = dxn.astype(BF)
            dqr_ref[:, hw + lo:hw + lo + LANE] = dxr.astype(BF)
            kn = kv_ref[:, 2 * lo:2 * lo + LANE]
            dxn, dxr, g2, g3 = norm_bwd(kn, krv, kn * kn + kr_sq, dk_ref[:, 2 * lo:2 * lo + LANE],
                                        dk_ref[:, 2 * lo + LANE:2 * lo + 2 * LANE], g_ref[2:3, :], g_ref[3:4, :])
            dkv_ref[:, 2 * lo:2 * lo + LANE] = dxn.astype(BF)
            dkv_ref[:, 2 * lo + LANE:2 * lo + 2 * LANE] = dv_ref[:, lo:lo + LANE].astype(BF)
            dkr = dkr + dxr
            dgs = [a + b for a, b in zip(dgs, (g0, g1, g2, g3))]
        dkr_ref[...] = dkr

        @pl.when(i == 0)
        def _():
            dg_ref[...] = jnp.zeros_like(dg_ref)

        dg_ref[...] += _rows8(dgs, LANE)

    row = lambda w: pl.BlockSpec((tr, w), lambda i: (i, 0))
    return pl.pallas_call(
        body, name="head_bwd", grid=(s // tr,),
        in_specs=[row(2 * hw), row(2 * hw), pl.BlockSpec((tr, LANE), lambda i: (i, kr_blk)),
                  row(LANE), row(LANE), pl.BlockSpec((SUB, LANE), lambda i: (0, 0)),
                  row(2 * hw), row(2 * hw), row(hw)],
        out_specs=[row(2 * hw), row(2 * hw), row(LANE), pl.BlockSpec((SUB, LANE), lambda i: (0, 0))],
        out_shape=[jax.ShapeDtypeStruct((s, 2 * hw), BF), jax.ShapeDtypeStruct((s, 2 * hw), BF),
                   jax.ShapeDtypeStruct((s, LANE), F32), jax.ShapeDtypeStruct((SUB, LANE), F32)],
        compiler_params=_cp("arbitrary"),
    )(q_raw, kv_raw, z_a, cos, sin, gains, dq_att, dk_att, dv)


def _causal_mask(nrows, ncols, row0):
    rows = lax.broadcasted_iota(jnp.int32, (nrows, ncols), 0) + row0
    cols = lax.broadcasted_iota(jnp.int32, (nrows, ncols), 1)
    return cols <= rows


def _causal_steps(nt, q_major):
    pairs = ([(i, j) for i in range(nt) for j in range(i + 1)] if q_major
             else [(i, j) for j in range(nt) for i in range(j, nt)])
    return (jnp.array([p[0] for p in pairs], jnp.int32), jnp.array([p[1] for p in pairs], jnp.int32))


def _attn_fwd(q_att, k_att, v, heads):
    s = q_att.shape[0]
    t = _pick(s, ATTN_TILE_FWD, LANE)
    nt = s // t
    th = t // 2
    qi, kj = _causal_steps(nt, True)

    def body(qi_ref, kj_ref, q_ref, k_ref, v_ref, o_ref, ob_ref, lse_ref, m_s, l_s, acc_s):
        st = pl.program_id(1)
        i = qi_ref[st]
        j = kj_ref[st]

        @pl.when(j == 0)
        def _():
            m_s[...] = jnp.full_like(m_s, NEG_INF)
            l_s[...] = jnp.zeros_like(l_s)
            acc_s[...] = jnp.zeros_like(acc_s)

        def update(rows, ncol, masked):
            sc = lax.dot_general(q_ref[rows, :], k_ref[0:ncol, :], (((1,), (1,)), ((), ())),
                                 preferred_element_type=F32)
            if masked:
                sc = jnp.where(_causal_mask(rows.stop - rows.start, ncol, rows.start), sc, NEG_INF)
            m_prev = m_s[rows, :]
            m_new = jnp.maximum(m_prev, jnp.max(sc, axis=-1, keepdims=True))
            alpha = jnp.exp2(m_prev - m_new)
            p = jnp.exp2(sc - jnp.tile(m_new, (1, ncol // LANE)))
            l_s[rows, :] = alpha * l_s[rows, :] + jnp.sum(p, axis=-1, keepdims=True)
            acc_s[rows, :] = alpha * acc_s[rows, :] + jnp.dot(p.astype(BF), v_ref[0:ncol, :],
                                                              preferred_element_type=F32)
            m_s[rows, :] = m_new

        @pl.when(j < i)
        def _():
            update(slice(0, t), t, False)

        @pl.when(j == i)
        def _():
            update(slice(0, th), th, True)
            update(slice(th, t), t, True)
            o = acc_s[...] / l_s[...]
            o_ref[...] = o
            ob_ref[...] = o.astype(BF)
            lse_ref[...] = (m_s[...] + jnp.log2(l_s[...]))[:, 0:1]

    q_idx = lambda h, st, qi_r, kj_r: (qi_r[st], h)
    kv_idx = lambda h, st, qi_r, kj_r: (kj_r[st], h)
    return pl.pallas_call(
        body, name="attn_fwd",
        grid_spec=pltpu.PrefetchScalarGridSpec(
            num_scalar_prefetch=2, grid=(heads, qi.shape[0]),
            in_specs=[pl.BlockSpec((t, 2 * LANE), q_idx), pl.BlockSpec((t, 2 * LANE), kv_idx),
                      pl.BlockSpec((t, LANE), kv_idx)],
            out_specs=[pl.BlockSpec((t, LANE), q_idx), pl.BlockSpec((t, LANE), q_idx),
                       pl.BlockSpec((None, t, 1), lambda h, st, qi_r, kj_r: (h, qi_r[st], 0))],
            scratch_shapes=[pltpu.VMEM((t, LANE), F32), pltpu.VMEM((t, LANE), F32), pltpu.VMEM((t, LANE), F32)]),
        out_shape=[jax.ShapeDtypeStruct((s, heads * LANE), F32), jax.ShapeDtypeStruct((s, heads * LANE), BF),
                   jax.ShapeDtypeStruct((heads, s, 1), F32)],
        compiler_params=_cp("parallel", "arbitrary"),
    )(qi, kj, q_att, k_att, v)


def _attn_bwd(q_att, k_att, v, o, lse, d_o, heads, dep=None):
    s = q_att.shape[0]
    t = _pick(s, ATTN_TILE, LANE)
    nt = s // t
    th = t // 2
    qi, kj = _causal_steps(nt, False)

    def body(qi_ref, kj_ref, q_ref, k_ref, v_ref, do_ref, o_ref, lse_ref, *rest):
        dq_ref, dk_ref, dv_ref, dk_s, dv_s = rest[-5:]
        st = pl.program_id(1)
        i = qi_ref[st]
        j = kj_ref[st]

        @pl.when(st == 0)
        def _():
            dq_ref[...] = jnp.zeros_like(dq_ref)

        @pl.when(i == j)
        def _():
            dk_s[...] = jnp.zeros_like(dk_s)
            dv_s[...] = jnp.zeros_like(dv_s)

        def update(rows, ncol, masked):
            nrow = rows.stop - rows.start
            q = q_ref[rows, :]
            k = k_ref[0:ncol, :]
            do = do_ref[rows, :]
            sc = lax.dot_general(q, k, (((1,), (1,)), ((), ())), preferred_element_type=F32)
            if masked:
                sc = jnp.where(_causal_mask(nrow, ncol, rows.start), sc, NEG_INF)
            p = jnp.exp2(sc - lse_ref[rows, :])
            dp = lax.dot_general(do, v_ref[0:ncol, :], (((1,), (1,)), ((), ())), preferred_element_type=F32)
            delta = jnp.sum(do.astype(F32) * o_ref[rows, :], axis=-1, keepdims=True)
            ds = (p * (dp - delta)).astype(BF)
            dv_s[0:ncol, :] += lax.dot_general(p.astype(BF), do, (((0,), (0,)), ((), ())),
                                               preferred_element_type=F32)
            dk_s[0:ncol, :] += lax.dot_general(ds, q, (((0,), (0,)), ((), ())), preferred_element_type=F32)
            out_rows = pl.ds(pl.multiple_of(i * t + rows.start, nrow), nrow)
            dq_ref[out_rows, :] += jnp.dot(ds, k, preferred_element_type=F32)

        @pl.when(i > j)
        def _():
            update(slice(0, t), t, False)

        @pl.when(i == j)
        def _():
            update(slice(0, th), th, True)
            update(slice(th, t), t, True)

        @pl.when(i == nt - 1)
        def _():
            dk_ref[...] = (dk_s[...] * (1.0 / LOG2_E)).astype(BF)
            dv_ref[...] = dv_s[...].astype(BF)

    q_idx = lambda h, st, qi_r, kj_r: (qi_r[st], h)
    kv_idx = lambda h, st, qi_r, kj_r: (kj_r[st], h)
    in_specs = [pl.BlockSpec((t, 2 * LANE), q_idx), pl.BlockSpec((t, 2 * LANE), kv_idx),
                pl.BlockSpec((t, LANE), kv_idx), pl.BlockSpec((t, LANE), q_idx), pl.BlockSpec((t, LANE), q_idx),
                pl.BlockSpec((None, t, 1), lambda h, st, qi_r, kj_r: (h, qi_r[st], 0))]
    args = [q_att, k_att, v, d_o, o, lse]
    if dep is not None:
        in_specs.append(ANY)
        args.append(dep)
    return pl.pallas_call(
        body, name="attn_bwd",
        grid_spec=pltpu.PrefetchScalarGridSpec(
            num_scalar_prefetch=2, grid=(heads, qi.shape[0]),
            in_specs=in_specs,
            out_specs=[pl.BlockSpec((s, 2 * LANE), lambda h, st, qi_r, kj_r: (0, h)),
                       pl.BlockSpec((t, 2 * LANE), kv_idx), pl.BlockSpec((t, LANE), kv_idx)],
            scratch_shapes=[pltpu.VMEM((t, 2 * LANE), F32), pltpu.VMEM((t, LANE), F32)]),
        out_shape=[jax.ShapeDtypeStruct((s, heads * 2 * LANE), F32),
                   jax.ShapeDtypeStruct((s, heads * 2 * LANE), BF),
                   jax.ShapeDtypeStruct((s, heads * LANE), BF)],
        compiler_params=_cp("parallel", "arbitrary"),
    )(qi, kj, *args)


def _sum_parts(parts, name):
    n, r, c = parts.shape
    tr = _pick(r, 512, 8)

    def body(p_ref, o_ref):
        g = p_ref[0].astype(F32)
        for k in range(1, n):
            g = g + p_ref[k].astype(F32)
        o_ref[...] = g

    return pl.pallas_call(
        body, name=name, grid=(r // tr,),
        in_specs=[pl.BlockSpec((n, tr, c), lambda i: (0, i, 0))],
        out_specs=pl.BlockSpec((tr, c), lambda i: (i, 0)),
        out_shape=jax.ShapeDtypeStruct((r, c), F32),
        compiler_params=_cp("parallel"),
    )(parts)


def _adamw(parts, w, m, v, name, by_cols=False):
    n, rp, c = parts.shape
    r = w.shape[0]
    assert by_cols or rp == r
    tr, tc = (r, _pick(c, 256, LANE)) if by_cols else (_pick(r, 256, 16 if r % 16 == 0 else 8), c)

    def body(p_ref, w_ref, m_ref, v_ref, g_ref, d_ref, mo_ref, vo_ref):
        g = p_ref[0].astype(F32)
        for k in range(1, n):
            g = g + p_ref[k].astype(F32)
        g = g[:r] if by_cols else g
        m_new = ADAM_B1 * m_ref[...] + (1.0 - ADAM_B1) * g
        v_new = ADAM_B2 * v_ref[...] + (1.0 - ADAM_B2) * jnp.square(g)
        m_hat = m_new / (1.0 - ADAM_B1 ** ADAM_STEP)
        v_hat = v_new / (1.0 - ADAM_B2 ** ADAM_STEP)
        g_ref[...] = g
        d_ref[...] = -ADAM_LR * (m_hat / (jnp.sqrt(v_hat) + ADAM_EPS) + ADAM_WD * w_ref[...])
        mo_ref[...] = m_new
        vo_ref[...] = v_new

    idx = (lambda i: (0, i)) if by_cols else (lambda i: (i, 0))
    spec = pl.BlockSpec((tr, tc), idx)
    sh = jax.ShapeDtypeStruct((r, c), F32)
    return pl.pallas_call(
        body, name=name, grid=(c // tc if by_cols else r // tr,),
        in_specs=[pl.BlockSpec((n, rp if by_cols else tr, tc), lambda i: (0,) + idx(i)), spec, spec, spec],
        out_specs=[spec] * 4, out_shape=[sh] * 4,
        compiler_params=_cp("parallel"),
    )(parts, w, m, v)


def _place():
    x, y, c = lax.axis_index("x"), lax.axis_index("y"), lax.axis_index("c")
    chips = [(1 - x, y), (x, 1 - y), (1 - x, 1 - y)]
    return x, y, c, chips


def _all_gather(shards, name, dep=None):
    n = len(shards)
    deps = [] if dep is None else list(dep)

    def body(*refs):
        ins, outs = refs[:n], refs[n + len(deps):2 * n + len(deps)]
        send_sems, recv_sems, local_sems = refs[2 * n + len(deps):]
        x, y, c, chips = _place()
        me, sibling = (x, y, c), (x, y, 1 - c)

        def slot(w, p):
            return outs[w].at[4 * p[0] + 2 * p[1] + p[2]]

        def copy(w, k, block, to, src=None):
            return pltpu.make_async_remote_copy(
                src_ref=slot(w, block) if src is None else src, dst_ref=slot(w, block),
                send_sem=send_sems.at[w, k], recv_sem=recv_sems.at[w, k], device_id=to, device_id_type=MESH)

        first = []
        for w in range(n):
            first += [copy(w, 1 + j, me, (*chip, c), src=ins[w]) for j, chip in enumerate(chips)]
            first.append(copy(w, 0, me, sibling, src=ins[w]))
        for cp in first:
            cp.start()
        mine = [pltpu.make_async_copy(ins[w], slot(w, me), local_sems.at[w]) for w in range(n)]
        for cp in mine:
            cp.start()
        passed = []
        for w in range(n):
            for j, chip in enumerate(chips):
                copy(w, 1 + j, (*chip, c), me).wait_recv()
                cp = copy(w, 4 + j, (*chip, c), sibling)
                cp.start()
                passed.append(cp)
        for w in range(n):
            copy(w, 0, sibling, me).wait_recv()
            for j, chip in enumerate(chips):
                copy(w, 4 + j, (*chip, 1 - c), me).wait_recv()
        for cp in first + passed:
            cp.wait_send()
        for cp in mine:
            cp.wait()

    return pl.pallas_call(
        body, name=name,
        in_specs=[ANY] * (n + len(deps)), out_specs=[ANY] * n,
        out_shape=[jax.ShapeDtypeStruct((N_DEV,) + a.shape, a.dtype) for a in shards],
        scratch_shapes=[pltpu.SemaphoreType.DMA((n, 7)), pltpu.SemaphoreType.DMA((n, 7)),
                        pltpu.SemaphoreType.DMA((n,))],
    )(*shards, *deps)


HBM = pl.BlockSpec(memory_space=pltpu.HBM)
SEM = pl.BlockSpec(memory_space=pltpu.SEMAPHORE)
EFFECT = pltpu.SideEffectType.DATAFLOW_SIDE_EFFECTING
PEERS = [(dx, dy, dc) for dx in (1, 0) for dy in (1, 0) for dc in (0, 1) if (dx, dy, dc) != (0, 0, 0)]


def _peer(x, y, c, flip):
    dx, dy, dc = flip
    return (1 - x if dx else x, 1 - y if dy else y, 1 - c if dc else c)


def _exchange_copies(srcs, lands, send, recv, loc, gather):
    x, y, c, _ = _place()
    me = 4 * x + 2 * y + c
    remote, local = [], []
    for w in range(len(srcs)):
        for k, flip in enumerate(PEERS):
            px, py, pc = _peer(x, y, c, flip)
            src = srcs[w] if gather else srcs[w].at[4 * px + 2 * py + pc]
            remote.append(pltpu.make_async_remote_copy(
                src_ref=src, dst_ref=lands[w].at[me], send_sem=send[w].at[k], recv_sem=recv[w].at[k],
                device_id=(px, py, pc), device_id_type=MESH))
        local.append(pltpu.make_async_copy(srcs[w] if gather else srcs[w].at[me], lands[w].at[me], loc[w]))
    return remote, local


class _Exchange:
    def __init__(self, srcs, lands, send, recv, loc, token, gather):
        self.srcs, self.lands, self.send, self.recv, self.loc = srcs, lands, send, recv, loc
        self.token, self.gather = token, gather


def _exchange_start(srcs, gather, name, dep=None):
    n = len(srcs)
    deps = [] if dep is None else [dep]
    land_shapes = [((N_DEV,) + a.shape) if gather else a.shape for a in srcs]
    lands = [pltpu.with_memory_space_constraint(lax.empty(sh, a.dtype), pltpu.HBM) for sh, a in zip(land_shapes, srcs)]
    srcs = [pltpu.with_memory_space_constraint(a, pltpu.HBM) for a in srcs]

    def body(*refs):
        src_refs, land_refs = refs[:n], refs[n:2 * n]
        outs = refs[2 * n + len(deps):]
        send, recv, loc = outs[:n], outs[n:2 * n], outs[2 * n:3 * n]
        token = outs[-1]
        remote, local = _exchange_copies(src_refs, land_refs, send, recv, loc, gather)
        for cp in remote + local:
            cp.start()
        token[...] = jnp.zeros_like(token)

    out_shape = ([pltpu.SemaphoreType.DMA((len(PEERS),))] * (2 * n) + [pltpu.SemaphoreType.DMA(())] * n
                 + [pltpu.HBM(a.shape, a.dtype) for a in srcs] + [pltpu.HBM(a.shape, a.dtype) for a in lands]
                 + [jax.ShapeDtypeStruct((SUB, LANE), F32)])
    res = pl.pallas_call(
        body, name=name, out_shape=out_shape,
        in_specs=[HBM] * (2 * n) + [ANY] * len(deps),
        out_specs=[SEM] * (3 * n) + [HBM] * (2 * n) + [pl.BlockSpec(memory_space=pltpu.VMEM)],
        input_output_aliases={i: 3 * n + i for i in range(2 * n)},
        compiler_params=pltpu.CompilerParams(has_side_effects=EFFECT),
    )(*srcs, *lands, *deps)
    return _Exchange(res[3 * n:4 * n], res[4 * n:5 * n], res[:n], res[n:2 * n], res[2 * n:3 * n], res[-1], gather)


def _exchange_wait(ex, idxs, after, name):
    n = len(idxs)
    srcs = [ex.srcs[i] for i in idxs]
    lands = [ex.lands[i] for i in idxs]
    sems = [ex.send[i] for i in idxs] + [ex.recv[i] for i in idxs] + [ex.loc[i] for i in idxs]
    gather = ex.gather

    def body(*refs):
        src_refs, land_refs = refs[:n], refs[n:2 * n]
        send, recv, loc = refs[2 * n:3 * n], refs[3 * n:4 * n], refs[4 * n:5 * n]
        remote, local = _exchange_copies(src_refs, land_refs, send, recv, loc, gather)
        for cp in remote:
            cp.wait_send()
            cp.wait_recv()
        for cp in local:
            cp.wait()

    res = pl.pallas_call(
        body, name=name,
        out_shape=[pltpu.HBM(a.shape, a.dtype) for a in srcs] + [pltpu.HBM(a.shape, a.dtype) for a in lands],
        in_specs=[HBM] * (2 * n) + [SEM] * (3 * n) + [ANY],
        out_specs=[HBM] * (2 * n),
        input_output_aliases={i: i for i in range(2 * n)},
        compiler_params=pltpu.CompilerParams(has_side_effects=EFFECT),
    )(*srcs, *lands, *sems, after)
    return res[n:]


def _gather2_copies(srcs, lands, send, recv_ici, recv_sib, loc):
    x, y, c, chips = _place()
    me = 4 * x + 2 * y + c
    remote, local = [], []
    for w in range(len(srcs)):
        remote.append(pltpu.make_async_remote_copy(
            src_ref=srcs[w], dst_ref=lands[w].at[me], send_sem=send[w].at[0], recv_sem=recv_sib[w],
            device_id=(x, y, 1 - c), device_id_type=MESH))
        for j, chip in enumerate(chips):
            remote.append(pltpu.make_async_remote_copy(
                src_ref=srcs[w], dst_ref=lands[w].at[me], send_sem=send[w].at[1 + j], recv_sem=recv_ici[w].at[j],
                device_id=(*chip, c), device_id_type=MESH))
        local.append(pltpu.make_async_copy(srcs[w], lands[w].at[me], loc[w]))
    return remote, local


def _gather2_forwards(lands, fsend, frecv, arrived=None):
    x, y, c, chips = _place()
    cps = []
    for w in range(len(lands)):
        for j, chip in enumerate(chips):
            slot = lands[w].at[4 * chip[0] + 2 * chip[1] + c]
            cp = pltpu.make_async_remote_copy(
                src_ref=slot, dst_ref=slot, send_sem=fsend[w].at[j], recv_sem=frecv[w].at[j],
                device_id=(x, y, 1 - c), device_id_type=MESH)
            if arrived is not None:
                pltpu.make_async_remote_copy(
                    src_ref=slot, dst_ref=slot, send_sem=fsend[w].at[j], recv_sem=arrived[w].at[j],
                    device_id=(x, y, 1 - c), device_id_type=MESH).wait_recv()
            cps.append(cp)
    return cps


def _gather2(shards, between, name):
    n = len(shards)
    srcs = [pltpu.with_memory_space_constraint(a, pltpu.HBM) for a in shards]
    lands = [pltpu.with_memory_space_constraint(lax.empty((N_DEV,) + a.shape, a.dtype), pltpu.HBM) for a in shards]
    hbm_like = lambda arrs: [pltpu.HBM(a.shape, a.dtype) for a in arrs]
    tok = jax.ShapeDtypeStruct((SUB, LANE), F32)
    vmem = pl.BlockSpec(memory_space=pltpu.VMEM)
    side = pltpu.CompilerParams(has_side_effects=EFFECT)

    def start(*refs):
        src_refs, land_refs = refs[:n], refs[n:2 * n]
        outs = refs[2 * n:]
        send, recv_ici, recv_sib, loc = outs[:n], outs[n:2 * n], outs[2 * n:3 * n], outs[3 * n:4 * n]
        remote, local = _gather2_copies(src_refs, land_refs, send, recv_ici, recv_sib, loc)
        for cp in remote + local:
            cp.start()
        outs[-1][...] = jnp.zeros((SUB, LANE), F32)

    res = pl.pallas_call(
        start, name=name + "_start",
        out_shape=([pltpu.SemaphoreType.DMA((4,))] * n + [pltpu.SemaphoreType.DMA((3,))] * n
                   + [pltpu.SemaphoreType.DMA(())] * (2 * n) + hbm_like(srcs) + hbm_like(lands) + [tok]),
        in_specs=[HBM] * (2 * n), out_specs=[SEM] * (4 * n) + [HBM] * (2 * n) + [vmem],
        input_output_aliases={i: 4 * n + i for i in range(2 * n)}, compiler_params=side,
    )(*srcs, *lands)
    send, recv_ici, recv_sib, loc = res[:n], res[n:2 * n], res[2 * n:3 * n], res[3 * n:4 * n]
    srcs, lands, token = res[4 * n:5 * n], res[5 * n:6 * n], res[-1]

    done = between(token)
    after = jax.tree_util.tree_leaves(done)

    def forward(*refs):
        land_refs, arrived = refs[:n], refs[n:2 * n]
        outs = refs[2 * n + len(after):]
        fsend, frecv = outs[:n], outs[n:2 * n]
        for cp in _gather2_forwards(land_refs, fsend, frecv, arrived):
            cp.start()
        outs[-1][...] = jnp.zeros((SUB, LANE), F32)

    res = pl.pallas_call(
        forward, name=name + "_forward",
        out_shape=[pltpu.SemaphoreType.DMA((3,))] * (2 * n) + hbm_like(lands) + [tok],
        in_specs=[HBM] * n + [SEM] * n + [ANY] * len(after), out_specs=[SEM] * (2 * n) + [HBM] * n + [vmem],
        input_output_aliases={i: 2 * n + i for i in range(n)}, compiler_params=side,
    )(*lands, *recv_ici, *after)
    fsend, frecv, lands, token = res[:n], res[n:2 * n], res[2 * n:3 * n], res[-1]

    def wait(*refs):
        src_refs, land_refs = refs[:n], refs[n:2 * n]
        sems = refs[2 * n:7 * n]
        send, recv_sib, loc, fsend, frecv = (sems[k * n:(k + 1) * n] for k in range(5))
        remote, local = _gather2_copies(src_refs, land_refs, send, send, recv_sib, loc)
        for w in range(n):
            for cp in remote[4 * w:4 * w + 4]:
                cp.wait_send()
            remote[4 * w].wait_recv()
        for cp in local:
            cp.wait()
        for cp in _gather2_forwards(land_refs, fsend, frecv):
            cp.wait_send()
            cp.wait_recv()

    res = pl.pallas_call(
        wait, name=name + "_wait", out_shape=hbm_like(srcs) + hbm_like(lands),
        in_specs=[HBM] * (2 * n) + [SEM] * (5 * n) + [ANY], out_specs=[HBM] * (2 * n),
        input_output_aliases={i: i for i in range(2 * n)}, compiler_params=side,
    )(*srcs, *lands, *send, *recv_sib, *loc, *fsend, *frecv, token)
    return res[n:], done


def _after(token, a):
    return a + token[0:1, 0:1].astype(a.dtype)


def _unblock(w3):
    nb, k, nbw = w3.shape
    return w3.transpose(1, 0, 2).reshape(k, nb * nbw)


def _block(w, nb):
    k, n = w.shape
    return w.reshape(k, nb, n // nb).transpose(1, 0, 2)


def kernel(x, positions, ln1_g, w_in, b_gate, conv_w, w_conv_out, q_a_g, w_q_b, kv_a_g, w_kv_b, q_norm_g, k_norm_g, w_mla_out, w_o, ln2_g, w_ffn_up, ffn_conv_w, ffn_conv_b, w_ffn_down, loss_target, m_ln1_g, m_w_in, m_b_gate, m_conv_w, m_w_conv_out, m_q_a_g, m_w_q_b, m_kv_a_g, m_w_kv_b, m_q_norm_g, m_k_norm_g, m_w_mla_out, m_w_o, m_ln2_g, m_w_ffn_up, m_ffn_conv_w, m_ffn_conv_b, m_w_ffn_down, v_ln1_g, v_w_in, v_b_gate, v_conv_w, v_w_conv_out, v_q_a_g, v_w_q_b, v_kv_a_g, v_w_kv_b, v_q_norm_g, v_k_norm_g, v_w_mla_out, v_w_o, v_ln2_g, v_w_ffn_up, v_ffn_conv_w, v_ffn_conv_b, v_w_ffn_down):
    s, d = x.shape[1], x.shape[2]
    conv = conv_w.shape[2] * N_DEV
    ql, kvl = q_a_g.shape[1], kv_a_g.shape[1]
    heads = w_q_b.shape[2] * N_DEV // HEAD_QK
    dff = w_ffn_down.shape[1] * N_DEV
    hw = heads * LANE
    conv3 = 3 * conv
    kr_off = conv3 + ql
    kv_off = -(-(kr_off + LANE) // kvl) * kvl
    wa = kv_off + kvl
    assert conv3 % ql == 0 and kr_off % LANE == 0
    xs = x[0]
    tgt = loss_target[0]
    pos = positions.reshape(s, 1)

    nin = w_in.shape[2]
    big = dict(w_in=w_in[0].T, w_conv_out=w_conv_out[0], w_q_b=w_q_b[0], w_kv_b=w_kv_b[0],
               w_mla_out=w_mla_out[0], w_o=w_o[0], w_ffn_up=w_ffn_up[0], w_ffn_down=w_ffn_down[0])
    names = list(big)
    rest = names[1:]
    early = {}

    def while_w_in_travels(token):
        early["ag"] = _exchange_start([big[k].astype(BF) for k in rest], True, "gather_rest_start", dep=token)
        cos_sin = _rope_tables(pos)
        return cos_sin, _rms_fwd(xs, _after(early["ag"].token, ln1_g), d, 0, "rms1_fwd")

    first, ((cos, sin), u1) = _gather2([big["w_in"].astype(BF), _pad8(conv_w[0]), _pad8(ffn_conv_w[0])],
                                       while_w_in_travels, "gather_w_in")
    ag = early["ag"]
    cw8 = _unblock(first[1])
    fcw8 = _unblock(first[2])

    def landed(keys, after, name):
        return _exchange_wait(ag, [rest.index(k) for k in keys], after, name)

    w_in_t = first[0].reshape(N_DEV * nin, d)
    g_off = kr_off + kvl + ROPE
    w_a_t = jnp.concatenate([w_in_t[:kr_off], _lay_rows(w_in_t[kr_off + kvl:g_off]),
                             jnp.zeros((kv_off - kr_off - LANE, d), BF), w_in_t[kr_off:kr_off + kvl]], axis=0)[None]
    w_g_t = w_in_t[g_off:][None]
    gains = _pad8(jnp.concatenate([q_norm_g[:, :NOPE], _lay(q_norm_g[:, NOPE:]),
                                   k_norm_g[:, :NOPE], _lay(k_norm_g[:, NOPE:])], axis=0))
    kr_blk = kr_off // LANE

    z_a = _mm_nt(u1, w_a_t, "mm_z_a")
    z_g = _mm_nt(u1, w_g_t, "mm_z_g", out_dtype=BF)
    p = _conv_mix_fwd(z_a, cw8, conv)
    w_co, w_qb, w_kv = landed(["w_conv_out", "w_q_b", "w_kv_b"], p, "gather_wait_mixers")
    w_co = _unblock(w_co)[None]
    w_kv = _unblock(w_kv)[None]
    wq_full = _unblock(w_qb).reshape(ql, heads, HEAD_QK)
    w_q = jnp.concatenate([wq_full[:, :, :NOPE].reshape(ql, hw), _lay(wq_full[:, :, NOPE:]).reshape(ql, hw)],
                          axis=1)[None]
    yc = _mm_nn(p, w_co, "mm_y_conv", out_dtype=BF)
    qn = _rms_fwd(z_a, q_a_g, ql, conv3 // ql, "rms_q_fwd")
    kvn = _rms_fwd(z_a, kv_a_g, kvl, kv_off // kvl, "rms_kv_fwd")
    q_raw = _mm_nn(qn, w_q, "mm_q")
    kv_raw = _mm_nn(kvn, w_kv, "mm_kv")
    q_att, k_att, v_bf = _head_fwd(q_raw, kv_raw, z_a, kr_blk, cos, sin, gains, heads)
    o, o_bf, lse = _attn_fwd(q_att, k_att, v_bf, heads)
    w_mo, w_oo = landed(["w_mla_out", "w_o"], lse, "gather_wait_outs")
    w_mo = w_mo.reshape(1, hw, d)
    w_oo = w_oo.reshape(1, d, d)
    ym, mix = _mla_out_gate(o_bf, w_mo, z_g, b_gate, yc)
    h1 = _mm_nn(mix, w_oo, "mm_h1", add=xs)
    u2 = _rms_fwd(h1, ln2_g, d, 0, "rms2_fwd")
    w_up, = landed(["w_ffn_up"], u2, "gather_wait_ffn_up")
    a_pre = _mm_nn(u2, w_up, "mm_ffn_up")
    f = _ffn_act_fwd(a_pre, fcw8, ffn_conv_b, dff)
    w_dn, = landed(["w_ffn_down"], f, "gather_wait_ffn_down")
    w_dn = w_dn.reshape(1, dff, d)
    dy, dy_bf, loss_part = _mm_nn_loss(f, w_dn, h1, tgt, "mm_ffn_down_loss")

    g_dn = _mm_tn(f, dy_bf, 1, "mm_g_ffn_down").reshape(N_DEV, dff // N_DEV, d)
    rs_dn = _exchange_start([g_dn], False, "reduce_ffn_down_start")
    d_f = _mm_nt(dy_bf, w_dn, "mm_d_f", dep=rs_dn.token)
    d_xg, d_xu, dfw_g, dfw_u = _ffn_act_bwd(a_pre, d_f, fcw8, ffn_conv_b, dff)
    half = N_DEV // 2
    g_up = _mm_tn(u2, d_xg, half, "mm_g_ffn_up_gate", into=lax.empty((N_DEV, d, 2 * dff // N_DEV), BF))
    g_up = _mm_tn(u2, d_xu, half, "mm_g_ffn_up_up", into=g_up, blk0=half)
    rs_up = _exchange_start([g_up], False, "reduce_ffn_up_start")
    d_u2 = _mm_nt([d_xg, d_xu], w_up, "mm_d_u2", out_dtype=BF, dep=rs_up.token)
    d_h1, d_h1_bf, dg_ln2 = _rms_bwd(h1, d_u2, ln2_g, d, 0, "rms2_bwd", extra=dy, also_bf16=True)
    g_oo = _mm_tn(mix, d_h1_bf, 1, "mm_g_w_o").reshape(N_DEV, d // N_DEV, d)
    d_zga, d_zgb, d_yc, d_ym, dba, dbb = _d_mix_gate(d_h1_bf, w_oo, z_g, b_gate, yc, ym)
    g_co = _block(_mm_tn(p, d_yc, 1, "mm_g_conv_out")[0], N_DEV)
    g_mo = _mm_tn(o_bf, d_ym, 1, "mm_g_mla_out").reshape(N_DEV, hw // N_DEV, d)
    rs_mix = _exchange_start([g_oo, g_co, g_mo], False, "reduce_mixers_start")
    d_p = _mm_nt(d_yc, w_co, "mm_d_p", dep=rs_mix.token)
    d_o = _mm_nt(d_ym, w_mo, "mm_d_o", out_dtype=BF)
    d_zb, d_zc, d_zv, dcw = _conv_mix_bwd(z_a, d_p, cw8, conv)
    dq_att, dk_att, dv = _attn_bwd(q_att, k_att, v_bf, o, lse, d_o, heads, dep=rs_mix.token)
    d_q_raw, d_kv_raw, d_kr, dgains = _head_bwd(q_raw, kv_raw, z_a, kr_blk, cos, sin, gains, dq_att, dk_att, dv, heads)
    g_q2 = _mm_tn(qn, d_q_raw, 1, "mm_g_q")[0]
    g_qb = _block(jnp.concatenate([g_q2[:, :hw].reshape(ql, heads, NOPE),
                                   _unlay(g_q2[:, hw:].reshape(ql, heads, LANE))], axis=2).reshape(ql, heads * HEAD_QK), N_DEV)
    g_kv = _block(_mm_tn(kvn, d_kv_raw, 1, "mm_g_kv")[0], N_DEV)
    rs_qkv = _exchange_start([g_qb, g_kv], False, "reduce_qkv_start")
    d_qn = _mm_nt(d_q_raw, w_q, "mm_d_qn", dep=rs_qkv.token)
    d_kvn = _mm_nt(d_kv_raw, w_kv, "mm_d_kvn")
    d_ql, dg_qa = _rms_bwd(z_a, d_qn, q_a_g, ql, conv3 // ql, "rms_q_bwd", out_dtype=BF)
    d_kvl, dg_kva = _rms_bwd(z_a, d_kvn, kv_a_g, kvl, kv_off // kvl, "rms_kv_bwd", out_dtype=BF)
    d_z_a = jnp.concatenate([d_zb, d_zc, d_zv, d_ql, d_kr.astype(BF), jnp.zeros((s, kv_off - kr_off - LANE), BF),
                             d_kvl], axis=1)
    g_a = _mm_tn(d_z_a, u1, 1, "mm_g_w_a")[0]
    g_ga = _mm_tn(d_zga, u1, 1, "mm_g_w_ga")[0]
    g_gb = _mm_tn(d_zgb, u1, 1, "mm_g_w_gb")[0]
    g_in = jnp.concatenate([g_a[:kr_off], g_a[kv_off:kv_off + kvl], g_a[kr_off:kr_off + HALF],
                            g_a[kr_off + 2 * HALF:kr_off + 3 * HALF], g_ga, g_gb], axis=0).reshape(N_DEV, nin, d)
    rs_in = _exchange_start([g_in], False, "reduce_w_in_start")
    d_u1 = _mm_nn(d_z_a, w_a_t, "mm_d_u1_a", dep=rs_in.token)
    d_u1 = _mm_nn([d_zga, d_zgb], w_g_t, "mm_d_u1_g", add=d_u1)
    grad_x, dg_ln1 = _rms_bwd(xs, d_u1, ln1_g, d, 0, "rms1_bwd", extra=d_h1)

    summed = {}
    summed["w_ffn_down"], = _exchange_wait(rs_dn, [0], grad_x, "reduce_ffn_down_wait")
    summed["w_ffn_up"], = _exchange_wait(rs_up, [0], grad_x, "reduce_ffn_up_wait")
    summed["w_o"], summed["w_conv_out"], summed["w_mla_out"] = _exchange_wait(rs_mix, [0, 1, 2], grad_x, "reduce_mixers_wait")
    summed["w_q_b"], summed["w_kv_b"] = _exchange_wait(rs_qkv, [0, 1], grad_x, "reduce_qkv_wait")
    loc = locals()
    out = {}
    for k in rest:
        out[k] = _adamw(summed[k], big[k], loc["m_" + k][0], loc["v_" + k][0], "adamw_" + k)

    small = dict(ln1_g=dg_ln1[0:1], b_gate=jnp.concatenate([dba[0:1], dbb[0:1]], axis=1), q_a_g=dg_qa[0:1],
                 kv_a_g=dg_kva[0:1],
                 q_norm_g=jnp.concatenate([dgains[0:1], _unlay(dgains[1:2])], axis=1),
                 k_norm_g=jnp.concatenate([dgains[2:3], _unlay(dgains[3:4])], axis=1),
                 ln2_g=dg_ln2[0:1], ffn_conv_b=jnp.concatenate([dfw_g[3:4], dfw_u[3:4]], axis=1))
    small_names = list(small)
    extra = [dcw[0:3].reshape(1, -1), jnp.concatenate([dfw_g[0:3], dfw_u[0:3]], axis=1).reshape(1, -1),
             loss_part[0:1, 0:1]]
    flat = jnp.concatenate([small[k] for k in small_names] + extra, axis=1)
    n_flat = flat.shape[1]
    rows = -(-n_flat // (SUB * LANE)) * SUB
    flat = jnp.pad(flat, ((0, 0), (0, rows * LANE - n_flat))).reshape(rows, LANE)
    total = _sum_parts(_all_gather([flat], "gather_small", dep=[out[k][0] for k in rest])[0], "sum_small").reshape(1, rows * LANE)
    off = 0
    small_g = {}
    for k in small_names:
        small_g[k] = total[:, off:off + small[k].shape[1]]
        off += small[k].shape[1]
    me = 4 * lax.axis_index("x") + 2 * lax.axis_index("y") + lax.axis_index("c")
    cwn, fcwn = conv // N_DEV, 2 * dff // N_DEV
    g_cw = lax.dynamic_slice_in_dim(total[:, off:off + 3 * conv].reshape(3, conv), me * cwn, cwn, axis=1)
    off += 3 * conv
    g_fcw = lax.dynamic_slice_in_dim(total[:, off:off + 6 * dff].reshape(3, 2 * dff), me * fcwn, fcwn, axis=1)
    off += 6 * dff
    loss = total[0, off]

    summed["w_in"], = _exchange_wait(rs_in, [0], total, "reduce_w_in_wait")
    out["w_in"] = [r.T for r in _adamw(summed["w_in"], big["w_in"], m_w_in[0].T, v_w_in[0].T, "adamw_w_in",
                                       by_cols=True)]
    small_w = dict(ln1_g=ln1_g, b_gate=b_gate, q_a_g=q_a_g, kv_a_g=kv_a_g, q_norm_g=q_norm_g, k_norm_g=k_norm_g,
                   ln2_g=ln2_g, ffn_conv_b=ffn_conv_b, conv_w=conv_w[0].reshape(1, -1),
                   ffn_conv_w=ffn_conv_w[0].reshape(1, -1))
    small_g["conv_w"] = g_cw.reshape(1, -1)
    small_g["ffn_conv_w"] = g_fcw.reshape(1, -1)
    packed_names = list(small_w)

    def pack(get):
        vflat = jnp.concatenate([get(k).reshape(1, -1) for k in packed_names], axis=1)
        nr = -(-vflat.shape[1] // (SUB * LANE)) * SUB
        return jnp.pad(vflat, ((0, 0), (0, nr * LANE - vflat.shape[1])), constant_values=1.0).reshape(nr, LANE)

    res = _adamw(pack(lambda k: small_g[k])[None], pack(lambda k: small_w[k]), pack(lambda k: loc["m_" + k]),
                 pack(lambda k: loc["v_" + k]), "adamw_small")
    res = [r.reshape(1, -1) for r in res]
    off = 0
    for k in packed_names:
        shape = loc[k].shape
        size = small_w[k].shape[1]
        out[k] = [r[:, off:off + size].reshape(shape) for r in res]
        off += size
    for k in names:
        out[k] = [r[None] for r in out[k]]

    order = ["ln1_g", "w_in", "b_gate", "conv_w", "w_conv_out", "q_a_g", "w_q_b", "kv_a_g", "w_kv_b", "q_norm_g",
             "k_norm_g", "w_mla_out", "w_o", "ln2_g", "w_ffn_up", "ffn_conv_w", "ffn_conv_b", "w_ffn_down"]
    return (loss, grad_x[None], *[out[k][0] for k in order], *[out[k][1] for k in order],
            *[out[k][2] for k in order], *[out[k][3] for k in order])
```

```python
import functools

import jax
import jax.numpy as jnp
from jax import lax
from jax.experimental import pallas as pl
from jax.experimental.pallas import tpu as pltpu

BF = jnp.bfloat16
F32 = jnp.float32
MESH = pl.DeviceIdType.MESH
N_DEV = 8

NOPE = 128
ROPE = 64
HALF = ROPE // 2
HEAD_QK = NOPE + ROPE
HEAD_V = 128
LANE = 128
SUB = 8
QK_SCALE = HEAD_QK ** -0.5
LOG2_E = 1.4426950408889634
NORM_EPS = 1e-6
NEG_INF = -1e30
ROPE_THETA = 10000.0
ADAM_LR = 0.001
ADAM_B1 = 0.9
ADAM_B2 = 0.999
ADAM_EPS = 1e-08
ADAM_WD = 0.01
ADAM_STEP = 10

VMEM_LIMIT = 52 * 1024 * 1024
MM_TM, MM_TN, MM_TK, MM_TS = 1024, 1536, 2048, 2048
ROW_TILE, ROW_TILE_BWD = 512, 256
HEAD_ROW_TILE, HEAD_ROW_TILE_BWD = 256, 128
COL_TILE = 512
ATTN_TILE = 1024
ATTN_TILE_FWD = 1024
ANY = pl.BlockSpec(memory_space=pl.ANY)


def _pick(n, target, mult):
    t = (min(n, target) // mult) * mult
    while t > 0:
        if n % t == 0:
            return t
        t -= mult
    raise ValueError(f"no tile for {n} (target {target}, multiple {mult})")


def _cp(*sem):
    return pltpu.CompilerParams(dimension_semantics=sem, vmem_limit_bytes=VMEM_LIMIT)


def _accumulate(kk, nk, acc, part, finish):
    if nk == 1:
        finish(part())
        return

    @pl.when(kk == 0)
    def _():
        acc[...] = part()

    @pl.when((kk > 0) & (kk < nk - 1))
    def _():
        acc[...] += part()

    @pl.when(kk == nk - 1)
    def _():
        finish(acc[...] + part())


def _mm_call(body, name, grid, in_specs, args, out_spec, out_shape, acc_shape, nk, dep):
    if dep is not None:
        in_specs = in_specs + [ANY]
        args = args + [dep]
    return pl.pallas_call(
        body, name=name, grid=grid, in_specs=in_specs, out_specs=out_spec, out_shape=out_shape,
        scratch_shapes=[pltpu.VMEM(acc_shape, F32)] if nk > 1 else [],
        compiler_params=_cp("parallel", "parallel", "arbitrary"),
    )(*args)


def _mm_nn_loss(a, b3, add, target, name):
    m, k = a.shape
    _, k2, n = b3.shape
    assert k == k2 and b3.shape[0] == 1
    tm = _pick(m, MM_TM, 16)
    tn = _pick(n, MM_TN, LANE)
    tk = _pick(k, MM_TK, LANE)
    nk = k // tk

    def body(a_ref, b_ref, c_ref, t_ref, dy_ref, dyb_ref, l_ref, acc):
        kk = pl.program_id(2)

        @pl.when((pl.program_id(0) == 0) & (pl.program_id(1) == 0) & (kk == 0))
        def _():
            l_ref[...] = jnp.zeros_like(l_ref)

        def part():
            return jnp.dot(a_ref[...].astype(BF), b_ref[0].astype(BF), preferred_element_type=F32)

        def finish(r):
            e = r + c_ref[...] - t_ref[...]
            dy_ref[...] = e / n
            dyb_ref[...] = (e / n).astype(BF)
            l_ref[...] += 0.5 * jnp.sum(jnp.sum(e * e, axis=-1, keepdims=True), axis=0, keepdims=True) / n

        _accumulate(kk, nk, acc, part, finish)

    tile = pl.BlockSpec((tm, tn), lambda i, j, kk: (i, j))
    return pl.pallas_call(
        body, name=name, grid=(m // tm, n // tn, nk),
        in_specs=[pl.BlockSpec((tm, tk), lambda i, j, kk: (i, kk)),
                  pl.BlockSpec((1, tk, tn), lambda i, j, kk: (0, kk, j)), tile, tile],
        out_specs=[tile, tile, pl.BlockSpec((SUB, LANE), lambda i, j, kk: (0, 0))],
        out_shape=[jax.ShapeDtypeStruct((m, n), F32), jax.ShapeDtypeStruct((m, n), BF),
                   jax.ShapeDtypeStruct((SUB, LANE), F32)],
        scratch_shapes=[pltpu.VMEM((tm, tn), F32)],
        compiler_params=_cp("arbitrary", "arbitrary", "arbitrary"),
    )(a, b3, add, target)


def _mm_nn(a, b3, name, add=None, out_dtype=F32, blk0=0, nblk=None, dep=None):
    pair = isinstance(a, (list, tuple))
    a_list = list(a) if pair else [a]
    m, ka = a_list[0].shape
    k = ka * len(a_list)
    nb_all, k2, nbw = b3.shape
    assert k == k2
    nblk = nb_all - blk0 if nblk is None else nblk
    n = nblk * nbw
    tm = _pick(m, MM_TM if k > MM_TM else 2 * MM_TM, 16)
    tn = _pick(nbw, MM_TN, LANE)
    tk = _pick(ka, MM_TK, LANE)
    per = nbw // tn
    nk = k // tk
    nka = ka // tk
    na_ops = len(a_list)

    def body(*refs):
        a_refs, b_ref = refs[:na_ops], refs[na_ops]
        c_ref = refs[na_ops + 1] if add is not None else None
        o_ref = refs[na_ops + 1 + (add is not None) + (dep is not None)]
        acc = refs[-1]
        kk = pl.program_id(2)

        def part():
            av = a_refs[0][...] if not pair else jnp.where(kk < nka, a_refs[0][...], a_refs[1][...])
            return jnp.dot(av.astype(BF), b_ref[...].astype(BF), preferred_element_type=F32)

        def finish(r):
            if add is not None:
                r = r + c_ref[...]
            o_ref[...] = r.astype(out_dtype)

        _accumulate(kk, nk, acc, part, finish)

    if pair:
        in_specs = [pl.BlockSpec((tm, tk), lambda i, j, kk: (i, jnp.minimum(kk, nka - 1))),
                    pl.BlockSpec((tm, tk), lambda i, j, kk: (i, jnp.maximum(kk - nka, 0)))]
    else:
        in_specs = [pl.BlockSpec((tm, tk), lambda i, j, kk: (i, kk))]
    in_specs.append(pl.BlockSpec((None, tk, tn), lambda i, j, kk: (blk0 + j // per, kk, j % per)))
    args = a_list + [b3]
    if add is not None:
        in_specs.append(pl.BlockSpec((tm, tn), lambda i, j, kk: (i, j)))
        args.append(add)
    return _mm_call(body, name, (m // tm, n // tn, nk), in_specs, args,
                    pl.BlockSpec((tm, tn), lambda i, j, kk: (i, j)), jax.ShapeDtypeStruct((m, n), out_dtype),
                    (tm, tn), nk, dep)


def _mm_nt(a, b3, name, add=None, out_dtype=F32, blk0=0, nblk=None, dep=None):
    pair = isinstance(a, (list, tuple))
    a_list = list(a) if pair else [a]
    m, na = a_list[0].shape
    n = na * len(a_list)
    nb_all, k, nbw = b3.shape
    nblk = nb_all - blk0 if nblk is None else nblk
    assert n == nblk * nbw and na % nbw == 0
    tm = _pick(m, 2 * MM_TM if k <= MM_TM and n <= MM_TK else MM_TM, 16)
    tk = _pick(nbw, MM_TK, LANE)
    per = nbw // tk
    nk = n // tk
    tn = _pick(k, MM_TN if nk <= 2 else 2 * MM_TM, LANE)
    nka = na // tk
    na_ops = len(a_list)

    def body(*refs):
        a_refs, b_ref = refs[:na_ops], refs[na_ops]
        c_ref = refs[na_ops + 1] if add is not None else None
        o_ref = refs[na_ops + 1 + (add is not None) + (dep is not None)]
        acc = refs[-1]
        kk = pl.program_id(2)

        def part():
            av = a_refs[0][...] if not pair else jnp.where(kk < nka, a_refs[0][...], a_refs[1][...])
            return lax.dot_general(av.astype(BF), b_ref[...].astype(BF),
                                   (((1,), (1,)), ((), ())), preferred_element_type=F32)

        def finish(r):
            if add is not None:
                r = r + c_ref[...]
            o_ref[...] = r.astype(out_dtype)

        _accumulate(kk, nk, acc, part, finish)

    if pair:
        in_specs = [pl.BlockSpec((tm, tk), lambda i, j, kk: (i, jnp.minimum(kk, nka - 1))),
                    pl.BlockSpec((tm, tk), lambda i, j, kk: (i, jnp.maximum(kk - nka, 0)))]
    else:
        in_specs = [pl.BlockSpec((tm, tk), lambda i, j, kk: (i, kk))]
    in_specs.append(pl.BlockSpec((None, tn, tk), lambda i, j, kk: (blk0 + kk // per, j, kk % per)))
    args = a_list + [b3]
    if add is not None:
        in_specs.append(pl.BlockSpec((tm, tn), lambda i, j, kk: (i, j)))
        args.append(add)
    return _mm_call(body, name, (m // tm, k // tn, nk), in_specs, args,
                    pl.BlockSpec((tm, tn), lambda i, j, kk: (i, j)), jax.ShapeDtypeStruct((m, k), out_dtype),
                    (tm, tn), nk, dep)


def _mm_tn(a, b, nblk, name, out_dtype=BF, dep=None, into=None, blk0=0):
    s, m = a.shape
    s2, n = b.shape
    assert s == s2 and n % nblk == 0 and (dep is None or into is None)
    nbw = n // nblk
    tm = _pick(m, MM_TN, LANE)
    tn = _pick(nbw, MM_TN, LANE)
    ts = _pick(s, MM_TS, LANE)
    per = nbw // tn
    ns = s // ts

    def body(*refs):
        a_ref, b_ref = refs[:2]
        o_ref = refs[2 + (dep is not None or into is not None)]
        acc = refs[-1]

        def part():
            return lax.dot_general(a_ref[...].astype(BF), b_ref[...].astype(BF),
                                   (((0,), (0,)), ((), ())), preferred_element_type=F32)

        def finish(r):
            o_ref[...] = r.astype(out_dtype)

        _accumulate(pl.program_id(2), ns, acc, part, finish)

    in_specs = [pl.BlockSpec((ts, tm), lambda i, j, ss: (ss, i)),
                pl.BlockSpec((ts, tn), lambda i, j, ss: (ss, j))]
    out_spec = pl.BlockSpec((None, tm, tn), lambda i, j, ss: (blk0 + j // per, i, j % per))
    if into is None:
        return _mm_call(body, name, (m // tm, n // tn, ns), in_specs, [a, b], out_spec,
                        jax.ShapeDtypeStruct((nblk, m, nbw), out_dtype), (tm, tn), ns, dep)
    assert into.shape[1:] == (m, nbw) and into.dtype == out_dtype
    return pl.pallas_call(
        body, name=name, grid=(m // tm, n // tn, ns), in_specs=in_specs + [ANY], out_specs=out_spec,
        out_shape=jax.ShapeDtypeStruct(into.shape, out_dtype), input_output_aliases={2: 0},
        scratch_shapes=[pltpu.VMEM((tm, tn), F32)] if ns > 1 else [],
        compiler_params=_cp("parallel", "parallel", "arbitrary"),
    )(a, b, into)


def _rows8(rows, width):
    idx = lax.broadcasted_iota(jnp.int32, (SUB, width), 0)
    out = jnp.zeros((SUB, width), F32)
    for r, v in enumerate(rows):
        out = jnp.where(idx == r, v, out)
    return out


def _rms_fwd(x, g, width, col_blk, name):
    s = x.shape[0]
    tr = _pick(s, ROW_TILE, 16)

    def body(x_ref, g_ref, u_ref):
        xv = x_ref[...]
        r = lax.rsqrt(jnp.mean(xv * xv, axis=-1, keepdims=True) + NORM_EPS)
        u_ref[...] = ((xv * r) * g_ref[...]).astype(BF)

    return pl.pallas_call(
        body, name=name, grid=(s // tr,),
        in_specs=[pl.BlockSpec((tr, width), lambda i: (i, col_blk)),
                  pl.BlockSpec((1, width), lambda i: (0, 0))],
        out_specs=pl.BlockSpec((tr, width), lambda i: (i, 0)),
        out_shape=jax.ShapeDtypeStruct((s, width), BF),
        compiler_params=_cp("parallel"),
    )(x, g)


def _rms_bwd(x, du, g, width, col_blk, name, extra=None, out_dtype=F32, also_bf16=False):
    s = x.shape[0]
    tr = _pick(s, ROW_TILE_BWD, 16)

    def body(*refs):
        x_ref, du_ref, g_ref = refs[:3]
        e_ref = refs[3] if extra is not None else None
        dx_ref = refs[3 + (extra is not None)]
        dxb_ref = refs[4 + (extra is not None)] if also_bf16 else None
        dg_ref = refs[-1]
        i = pl.program_id(0)
        xv = x_ref[...]
        duv = du_ref[...].astype(F32)
        r = lax.rsqrt(jnp.mean(xv * xv, axis=-1, keepdims=True) + NORM_EPS)
        nv = xv * r
        dn = duv * g_ref[...]
        dx = r * (dn - nv * jnp.mean(dn * nv, axis=-1, keepdims=True))
        if extra is not None:
            dx = dx + e_ref[...]
        dx_ref[...] = dx.astype(out_dtype)
        if also_bf16:
            dxb_ref[...] = dx.astype(BF)

        @pl.when(i == 0)
        def _():
            dg_ref[...] = jnp.zeros_like(dg_ref)

        dg_ref[...] += _rows8([jnp.sum(duv * nv, axis=0, keepdims=True)], width)

    in_specs = [pl.BlockSpec((tr, width), lambda i: (i, col_blk)),
                pl.BlockSpec((tr, width), lambda i: (i, 0)),
                pl.BlockSpec((1, width), lambda i: (0, 0))]
    args = [x, du, g]
    if extra is not None:
        in_specs.append(pl.BlockSpec((tr, width), lambda i: (i, 0)))
        args.append(extra)
    return pl.pallas_call(
        body, name=name, grid=(s // tr,),
        in_specs=in_specs,
        out_specs=[pl.BlockSpec((tr, width), lambda i: (i, 0))] * (1 + also_bf16)
        + [pl.BlockSpec((SUB, width), lambda i: (0, 0))],
        out_shape=[jax.ShapeDtypeStruct((s, width), out_dtype)] + [jax.ShapeDtypeStruct((s, width), BF)] * also_bf16
        + [jax.ShapeDtypeStruct((SUB, width), F32)],
        compiler_params=_cp("arbitrary"),
    )(*args)


def _down(cur, prev8, k):
    ext = jnp.concatenate([prev8, cur], axis=0)
    return pltpu.roll(ext, k, axis=0)[SUB:]


def _up(cur, next8, k):
    ext = jnp.concatenate([cur, next8], axis=0)
    return pltpu.roll(ext, ext.shape[0] - k, axis=0)[:cur.shape[0]]


def _lags(cur, prev8):
    return _down(cur, prev8, 1), _down(cur, prev8, 2)


def _conv3(w_ref, cur, prev8, lags=None):
    lag1, lag2 = _lags(cur, prev8) if lags is None else lags
    return w_ref[0:1, :] * lag2 + w_ref[1:2, :] * lag1 + w_ref[2:3, :] * cur


def _conv3_t(w_ref, cur, next8):
    return w_ref[2:3, :] * cur + w_ref[1:2, :] * _up(cur, next8, 1) + w_ref[0:1, :] * _up(cur, next8, 2)


def _spec_cur(tr, tc, c0):
    return pl.BlockSpec((tr, tc), lambda j, i: (i, c0 + j))


def _spec_prev(tr, tc, c0):
    return pl.BlockSpec((SUB, tc), lambda j, i: (jnp.maximum(i * (tr // SUB) - 1, 0), c0 + j))


def _spec_next(tr, tc, c0, s):
    return pl.BlockSpec((SUB, tc), lambda j, i: (jnp.minimum((i + 1) * (tr // SUB), s // SUB - 1), c0 + j))


def _spec_w(tc, c0):
    return pl.BlockSpec((SUB, tc), lambda j, i: (0, c0 + j))


def _pad8(w):
    return jnp.pad(w, ((0, SUB - w.shape[0]), (0, 0)))


def _conv_mix_fwd(z_a, cw8, conv):
    s = z_a.shape[0]
    tr = _pick(s, ROW_TILE, 16)
    tc = _pick(conv, COL_TILE, LANE)
    nc = conv // tc

    def body(zb_ref, zc_ref, zv_ref, zcp_ref, zvp_ref, w_ref, p_ref):
        i = pl.program_id(1)
        cv = zc_ref[...] * zv_ref[...]
        cvp = jnp.where(i > 0, zcp_ref[...] * zvp_ref[...], 0.0)
        p_ref[...] = (zb_ref[...] * _conv3(w_ref, cv, cvp)).astype(BF)

    return pl.pallas_call(
        body, name="conv_mix_fwd", grid=(nc, s // tr),
        in_specs=[_spec_cur(tr, tc, 0), _spec_cur(tr, tc, nc), _spec_cur(tr, tc, 2 * nc),
                  _spec_prev(tr, tc, nc), _spec_prev(tr, tc, 2 * nc), _spec_w(tc, 0)],
        out_specs=_spec_cur(tr, tc, 0),
        out_shape=jax.ShapeDtypeStruct((s, conv), BF),
        compiler_params=_cp("parallel", "parallel"),
    )(z_a, z_a, z_a, z_a, z_a, cw8)


def _conv_mix_bwd(z_a, d_p, cw8, conv):
    s = z_a.shape[0]
    tr = _pick(s, ROW_TILE_BWD, 16)
    tc = _pick(conv, COL_TILE, LANE)
    nc = conv // tc
    nr = s // tr

    def body(zb_ref, zbn_ref, zc_ref, zcp_ref, zv_ref, zvp_ref, dp_ref, dpn_ref, w_ref,
             dzb_ref, dzc_ref, dzv_ref, dw_ref):
        i = pl.program_id(1)
        zc = zc_ref[...]
        zv = zv_ref[...]
        cv = zc * zv
        cvp = jnp.where(i > 0, zcp_ref[...] * zvp_ref[...], 0.0)
        cv1, cv2 = _lags(cv, cvp)
        dpv = dp_ref[...]
        dzb_ref[...] = (dpv * _conv3(w_ref, cv, cvp, (cv1, cv2))).astype(BF)
        dcc = dpv * zb_ref[...]
        dccn = jnp.where(i < nr - 1, dpn_ref[...] * zbn_ref[...], 0.0)
        dcv = _conv3_t(w_ref, dcc, dccn)
        dzc_ref[...] = (dcv * zv).astype(BF)
        dzv_ref[...] = (dcv * zc).astype(BF)

        @pl.when(i == 0)
        def _():
            dw_ref[...] = jnp.zeros_like(dw_ref)

        dw_ref[...] += _rows8([jnp.sum(dcc * cv2, axis=0, keepdims=True),
                               jnp.sum(dcc * cv1, axis=0, keepdims=True),
                               jnp.sum(dcc * cv, axis=0, keepdims=True)], tc)

    out = jax.ShapeDtypeStruct((s, conv), BF)
    return pl.pallas_call(
        body, name="conv_mix_bwd", grid=(nc, nr),
        in_specs=[_spec_cur(tr, tc, 0), _spec_next(tr, tc, 0, s),
                  _spec_cur(tr, tc, nc), _spec_prev(tr, tc, nc),
                  _spec_cur(tr, tc, 2 * nc), _spec_prev(tr, tc, 2 * nc),
                  _spec_cur(tr, tc, 0), _spec_next(tr, tc, 0, s), _spec_w(tc, 0)],
        out_specs=[_spec_cur(tr, tc, 0), _spec_cur(tr, tc, 0), _spec_cur(tr, tc, 0), _spec_w(tc, 0)],
        out_shape=[out, out, out, jax.ShapeDtypeStruct((SUB, conv), F32)],
        compiler_params=_cp("parallel", "arbitrary"),
    )(z_a, z_a, z_a, z_a, z_a, z_a, d_p, d_p, cw8)


def _silu_parts(ag):
    sg = jax.nn.sigmoid(ag)
    return ag * sg, sg


def _ffn_up_act(u2, w_up, cw8, cb, dff):
    s, d = u2.shape
    nb, _, nbw = w_up.shape
    half = nb // 2
    assert half * nbw == dff
    tm = _pick(s, ROW_TILE, 16)

    def body(u_ref, wg_ref, wu_ref, cg_ref, cu_ref, bg_ref, bu_ref, ag_ref, au_ref, f_ref, hist_g, hist_u):
        i = pl.program_id(1)

        @pl.when(i == 0)
        def _():
            hist_g[...] = jnp.zeros_like(hist_g)
            hist_u[...] = jnp.zeros_like(hist_u)

        u = u_ref[...]
        xg = jnp.dot(u, wg_ref[...], preferred_element_type=F32)
        xu = jnp.dot(u, wu_ref[...], preferred_element_type=F32)
        ag_ref[...] = xg
        au_ref[...] = xu
        ag = _conv3(cg_ref, xg, hist_g[...]) + bg_ref[...]
        au = _conv3(cu_ref, xu, hist_u[...]) + bu_ref[...]
        f_ref[...] = (_silu_parts(ag)[0] * au).astype(BF)
        hist_g[...] = xg[tm - SUB:]
        hist_u[...] = xu[tm - SUB:]

    once = pl.Buffered(1)
    tile = pl.BlockSpec((tm, nbw), lambda j, i: (i, j))
    return pl.pallas_call(
        body, name="mm_ffn_up_act", grid=(half, s // tm),
        in_specs=[pl.BlockSpec((tm, d), lambda j, i: (i, 0)),
                  pl.BlockSpec((None, d, nbw), lambda j, i: (j, 0, 0), pipeline_mode=once),
                  pl.BlockSpec((None, d, nbw), lambda j, i: (half + j, 0, 0), pipeline_mode=once),
                  pl.BlockSpec((SUB, nbw), lambda j, i: (0, j)), pl.BlockSpec((SUB, nbw), lambda j, i: (0, half + j)),
                  pl.BlockSpec((1, nbw), lambda j, i: (0, j)), pl.BlockSpec((1, nbw), lambda j, i: (0, half + j))],
        out_specs=[tile, tile, tile],
        out_shape=[jax.ShapeDtypeStruct((s, dff), F32), jax.ShapeDtypeStruct((s, dff), F32),
                   jax.ShapeDtypeStruct((s, dff), BF)],
        scratch_shapes=[pltpu.VMEM((SUB, nbw), F32), pltpu.VMEM((SUB, nbw), F32)],
        compiler_params=_cp("arbitrary", "arbitrary"),
    )(u2, w_up, w_up, cw8, cw8, cb, cb)


def _ffn_act_bwd(a_g, a_u, d_f, cw8, cb, dff):
    s = a_g.shape[0]
    tr = _pick(s, ROW_TILE_BWD, 16)
    tc = _pick(dff, COL_TILE, LANE)
    nc = dff // tc
    nr = s // tr

    def body(xg_ref, xgp_ref, xgn_ref, xu_ref, xup_ref, xun_ref, df_ref, dfn_ref,
             wg_ref, wu_ref, bg_ref, bu_ref, dxg_ref, dxu_ref, dwg_ref, dwu_ref):
        i = pl.program_id(1)
        xg = xg_ref[...]
        xu = xu_ref[...]
        xgp = jnp.where(i > 0, xgp_ref[...], 0.0)
        xup = jnp.where(i > 0, xup_ref[...], 0.0)

        def d_act(xg_t, xgp_t, xu_t, xup_t, df_t, lags_g=None, lags_u=None):
            ag = _conv3(wg_ref, xg_t, xgp_t, lags_g) + bg_ref[...]
            au = _conv3(wu_ref, xu_t, xup_t, lags_u) + bu_ref[...]
            sil, sg = _silu_parts(ag)
            return df_t * au * (sg * (1.0 + ag * (1.0 - sg))), df_t * sil

        lags_g = _lags(xg, xgp)
        lags_u = _lags(xu, xup)
        dag, dau = d_act(xg, xgp, xu, xup, df_ref[...], lags_g, lags_u)
        dfn = jnp.where(i < nr - 1, dfn_ref[...], 0.0)
        dagn, daun = d_act(xgn_ref[...], xg[tr - SUB:], xun_ref[...], xu[tr - SUB:], dfn)
        dxg_ref[...] = _conv3_t(wg_ref, dag, dagn).astype(BF)
        dxu_ref[...] = _conv3_t(wu_ref, dau, daun).astype(BF)

        @pl.when(i == 0)
        def _():
            dwg_ref[...] = jnp.zeros_like(dwg_ref)
            dwu_ref[...] = jnp.zeros_like(dwu_ref)

        def wgrad(da, x, lags):
            return _rows8([jnp.sum(da * lags[1], axis=0, keepdims=True),
                           jnp.sum(da * lags[0], axis=0, keepdims=True),
                           jnp.sum(da * x, axis=0, keepdims=True),
                           jnp.sum(da, axis=0, keepdims=True)], tc)

        dwg_ref[...] += wgrad(dag, xg, lags_g)
        dwu_ref[...] += wgrad(dau, xu, lags_u)

    half = jax.ShapeDtypeStruct((s, dff), BF)
    wsh = jax.ShapeDtypeStruct((SUB, dff), F32)
    return pl.pallas_call(
        body, name="ffn_act_bwd", grid=(nc, nr),
        in_specs=[_spec_cur(tr, tc, 0), _spec_prev(tr, tc, 0), _spec_next(tr, tc, 0, s),
                  _spec_cur(tr, tc, 0), _spec_prev(tr, tc, 0), _spec_next(tr, tc, 0, s),
                  _spec_cur(tr, tc, 0), _spec_next(tr, tc, 0, s),
                  _spec_w(tc, 0), _spec_w(tc, nc),
                  pl.BlockSpec((1, tc), lambda j, i: (0, j)), pl.BlockSpec((1, tc), lambda j, i: (0, nc + j))],
        out_specs=[_spec_cur(tr, tc, 0), _spec_cur(tr, tc, 0), _spec_w(tc, 0), _spec_w(tc, 0)],
        out_shape=[half, half, wsh, wsh],
        compiler_params=_cp("parallel", "arbitrary"),
    )(a_g, a_g, a_g, a_u, a_u, a_u, d_f, d_f, cw8, cw8, cb, cb)


def _mla_out_gate(o, w_mo, z_g, b_gate, yc):
    m, k = o.shape
    d = w_mo.shape[2]
    tm = _pick(m, MM_TM, 16)
    tn = _pick(d, MM_TM, LANE)
    nc = d // tn

    def body(a_ref, b_ref, za_ref, zb_ref, ba_ref, bb_ref, yc_ref, ym_ref, mix_ref):
        ym = jnp.dot(a_ref[...], b_ref[0], preferred_element_type=F32)
        ga = jax.nn.sigmoid(za_ref[...] + ba_ref[...])
        gb = jax.nn.sigmoid(zb_ref[...] + bb_ref[...])
        ym_ref[...] = ym.astype(BF)
        mix_ref[...] = (ga * yc_ref[...] + gb * ym).astype(BF)

    tile = pl.BlockSpec((tm, tn), lambda i, j: (i, j))
    out = jax.ShapeDtypeStruct((m, d), BF)
    return pl.pallas_call(
        body, name="mm_y_mla_gate", grid=(m // tm, nc),
        in_specs=[pl.BlockSpec((tm, k), lambda i, j: (i, 0)), pl.BlockSpec((1, k, tn), lambda i, j: (0, 0, j)),
                  tile, pl.BlockSpec((tm, tn), lambda i, j: (i, nc + j)),
                  pl.BlockSpec((1, tn), lambda i, j: (0, j)), pl.BlockSpec((1, tn), lambda i, j: (0, nc + j)), tile],
        out_specs=[tile, tile], out_shape=[out, out],
        compiler_params=_cp("parallel", "parallel"),
    )(o, w_mo, z_g, z_g, b_gate, b_gate, yc)


def _d_mix_gate(d_h1, w_oo, z_g, b_gate, yc, ym):
    m, n = d_h1.shape
    d = w_oo.shape[1]
    tm = _pick(m, MM_TM, 16)
    tn = _pick(d, COL_TILE, LANE)
    nc = d // tn

    def body(a_ref, b_ref, za_ref, zb_ref, ba_ref, bb_ref, yc_ref, ym_ref,
             dza_ref, dzb_ref, dyc_ref, dym_ref, dba_ref, dbb_ref):
        i = pl.program_id(1)
        dm = lax.dot_general(a_ref[...], b_ref[0], (((1,), (1,)), ((), ())), preferred_element_type=F32)
        ga = jax.nn.sigmoid(za_ref[...] + ba_ref[...])
        gb = jax.nn.sigmoid(zb_ref[...] + bb_ref[...])
        dza = dm * yc_ref[...] * (ga * (1.0 - ga))
        dzb = dm * ym_ref[...] * (gb * (1.0 - gb))
        dza_ref[...] = dza.astype(BF)
        dzb_ref[...] = dzb.astype(BF)
        dyc_ref[...] = (dm * ga).astype(BF)
        dym_ref[...] = (dm * gb).astype(BF)

        @pl.when(i == 0)
        def _():
            dba_ref[...] = jnp.zeros_like(dba_ref)
            dbb_ref[...] = jnp.zeros_like(dbb_ref)

        dba_ref[...] += _rows8([jnp.sum(dza, axis=0, keepdims=True)], tn)
        dbb_ref[...] += _rows8([jnp.sum(dzb, axis=0, keepdims=True)], tn)

    tile = pl.BlockSpec((tm, tn), lambda j, i: (i, j))
    act = jax.ShapeDtypeStruct((m, d), BF)
    bsh = jax.ShapeDtypeStruct((SUB, d), F32)
    return pl.pallas_call(
        body, name="mm_d_mix_gate", grid=(nc, m // tm),
        in_specs=[pl.BlockSpec((tm, n), lambda j, i: (i, 0)), pl.BlockSpec((1, tn, n), lambda j, i: (0, j, 0)),
                  tile, pl.BlockSpec((tm, tn), lambda j, i: (i, nc + j)),
                  pl.BlockSpec((1, tn), lambda j, i: (0, j)), pl.BlockSpec((1, tn), lambda j, i: (0, nc + j)),
                  tile, tile],
        out_specs=[tile] * 4 + [pl.BlockSpec((SUB, tn), lambda j, i: (0, j))] * 2,
        out_shape=[act, act, act, act, bsh, bsh],
        compiler_params=_cp("parallel", "arbitrary"),
    )(d_h1, w_oo, z_g, z_g, b_gate, b_gate, yc, ym)


def _lay(v):
    z = jnp.zeros(v.shape[:-1] + (HALF,), v.dtype)
    return jnp.concatenate([v[..., :HALF], z, v[..., HALF:], z], axis=-1)


def _unlay(v):
    return jnp.concatenate([v[..., :HALF], v[..., 2 * HALF:3 * HALF]], axis=-1)


def _lay_rows(v):
    z = jnp.zeros((HALF,) + v.shape[1:], v.dtype)
    return jnp.concatenate([v[:HALF], z, v[HALF:], z], axis=0)


def _rope_tables(positions):
    s = positions.shape[0]
    tr = _pick(s, ROW_TILE, 8)
    inv_freq = ROPE_THETA ** (-jnp.arange(0, ROPE, 2, dtype=F32) / ROPE)
    consts = jnp.stack([_lay(jnp.concatenate([inv_freq, inv_freq])),
                        _lay(jnp.ones((ROPE,), F32)),
                        _lay(jnp.concatenate([-jnp.ones((HALF,), F32), jnp.ones((HALF,), F32)]))])
    consts = _pad8(consts)

    def body(p_ref, c_ref, cos_ref, sin_ref):
        ang = p_ref[...].astype(F32) * c_ref[0:1, :]
        cos_ref[...] = jnp.cos(ang) * c_ref[1:2, :]
        sin_ref[...] = jnp.sin(ang) * c_ref[2:3, :]

    tab = jax.ShapeDtypeStruct((s, LANE), F32)
    return pl.pallas_call(
        body, name="rope_tables", grid=(s // tr,),
        in_specs=[pl.BlockSpec((tr, 1), lambda i: (i, 0)), pl.BlockSpec((SUB, LANE), lambda i: (0, 0))],
        out_specs=[pl.BlockSpec((tr, LANE), lambda i: (i, 0))] * 2,
        out_shape=[tab, tab],
        compiler_params=_cp("parallel"),
    )(positions, consts)


def _lane_sum(p):
    return jnp.sum(p, axis=-1, keepdims=True)


def _rope(t, cos, sin):
    return t * cos + pltpu.roll(t, 2 * HALF, axis=1) * sin


def _rope_t(d, cos, sin):
    return d * cos + pltpu.roll(d * sin, 2 * HALF, axis=1)


def _head_fwd(q_raw, kv_raw, z_a, kr_blk, cos, sin, gains, heads):
    s = q_raw.shape[0]
    tr = _pick(s, HEAD_ROW_TILE, 16)
    hw = heads * LANE

    def body(q_ref, kv_ref, kr_ref, cos_ref, sin_ref, g_ref, qo_ref, ko_ref, vo_ref):
        cosv = cos_ref[...]
        sinv = sin_ref[...]
        krv = kr_ref[...]
        kr_sq = krv * krv
        for h in range(heads):
            lo = h * LANE
            qn = q_ref[:, lo:lo + LANE]
            qr = q_ref[:, hw + lo:hw + lo + LANE]
            r = lax.rsqrt(_lane_sum(qn * qn + qr * qr) / HEAD_QK + NORM_EPS)
            qo_ref[:, 2 * lo:2 * lo + LANE] = (((qn * r) * g_ref[0:1, :]) * (QK_SCALE * LOG2_E)).astype(BF)
            qo_ref[:, 2 * lo + LANE:2 * lo + 2 * LANE] = (
                _rope((qr * r) * g_ref[1:2, :], cosv, sinv) * (QK_SCALE * LOG2_E)).astype(BF)
            kn = kv_ref[:, 2 * lo:2 * lo + LANE]
            r = lax.rsqrt(_lane_sum(kn * kn + kr_sq) / HEAD_QK + NORM_EPS)
            ko_ref[:, 2 * lo:2 * lo + LANE] = ((kn * r) * g_ref[2:3, :]).astype(BF)
            ko_ref[:, 2 * lo + LANE:2 * lo + 2 * LANE] = _rope((krv * r) * g_ref[3:4, :], cosv, sinv).astype(BF)
            vo_ref[:, lo:lo + LANE] = kv_ref[:, 2 * lo + LANE:2 * lo + 2 * LANE].astype(BF)

    row = lambda w: pl.BlockSpec((tr, w), lambda i: (i, 0))
    return pl.pallas_call(
        body, name="head_fwd", grid=(s // tr,),
        in_specs=[row(2 * hw), row(2 * hw), pl.BlockSpec((tr, LANE), lambda i: (i, kr_blk)),
                  row(LANE), row(LANE), pl.BlockSpec((SUB, LANE), lambda i: (0, 0))],
        out_specs=[row(2 * hw), row(2 * hw), row(hw)],
        out_shape=[jax.ShapeDtypeStruct((s, 2 * hw), BF), jax.ShapeDtypeStruct((s, 2 * hw), BF),
                   jax.ShapeDtypeStruct((s, hw), BF)],
        compiler_params=_cp("parallel"),
    )(q_raw, kv_raw, z_a, cos, sin, gains)


def _head_bwd(q_raw, kv_raw, z_a, kr_blk, cos, sin, gains, dq_att, dk_att, dv, heads):
    s = q_raw.shape[0]
    tr = _pick(s, HEAD_ROW_TILE_BWD, 16)
    hw = heads * LANE

    def body(q_ref, kv_ref, kr_ref, cos_ref, sin_ref, g_ref, dq_ref, dk_ref, dv_ref,
             dqr_ref, dkv_ref, dkr_ref, dg_ref):
        i = pl.program_id(0)
        cosv = cos_ref[...]
        sinv = sin_ref[...]
        krv = kr_ref[...]
        kr_sq = krv * krv
        dkr = jnp.zeros((tr, LANE), F32)
        dgs = [jnp.zeros((1, LANE), F32) for _ in range(4)]

        def norm_bwd(xn, xr, sq, dn_out, dr_out, gn, gr):
            r = lax.rsqrt(_lane_sum(sq) / HEAD_QK + NORM_EPS)
            nn = xn * r
            nr = xr * r
            dt = _rope_t(dr_out, cosv, sinv)
            dnn = dn_out * gn
            dnr = dt * gr
            mean = _lane_sum(dnn * nn + dnr * nr) / HEAD_QK
            return (r * (dnn - nn * mean), r * (dnr - nr * mean),
                    jnp.sum(dn_out * nn, axis=0, keepdims=True), jnp.sum(dt * nr, axis=0, keepdims=True))

        for h in range(heads):
            lo = h * LANE
            qn = q_ref[:, lo:lo + LANE]
            qr = q_ref[:, hw + lo:hw + lo + LANE]
            dxn, dxr, g0, g1 = norm_bwd(qn, qr, qn * qn + qr * qr, dq_ref[:, 2 * lo:2 * lo + LANE] * QK_SCALE,
                                        dq_ref[:, 2 * lo + LANE:2 * lo + 2 * LANE] * QK_SCALE,
                                        g_ref[0:1, :], g_ref[1:2, :])
            dqr_ref[:, lo:lo + LANE] = dxn.astype(BF)
            dqr_ref[:, hw + lo:hw + lo + LANE] = dxr.astype(BF)
            kn = kv_ref[:, 2 * lo:2 * lo + LANE]
            dxn, dxr, g2, g3 = norm_bwd(kn, krv, kn * kn + kr_sq, dk_ref[:, 2 * lo:2 * lo + LANE],
                                        dk_ref[:, 2 * lo + LANE:2 * lo + 2 * LANE], g_ref[2:3, :], g_ref[3:4, :])
            dkv_ref[:, 2 * lo:2 * lo + LANE] = dxn.astype(BF)
            dkv_ref[:, 2 * lo + LANE:2 * lo + 2 * LANE] = dv_ref[:, lo:lo + LANE].astype(BF)
            dkr = dkr + dxr
            dgs = [a + b for a, b in zip(dgs, (g0, g1, g2, g3))]
        dkr_ref[...] = dkr

        @pl.when(i == 0)
        def _():
            dg_ref[...] = jnp.zeros_like(dg_ref)

        dg_ref[...] += _rows8(dgs, LANE)

    row = lambda w: pl.BlockSpec((tr, w), lambda i: (i, 0))
    return pl.pallas_call(
        body, name="head_bwd", grid=(s // tr,),
        in_specs=[row(2 * hw), row(2 * hw), pl.BlockSpec((tr, LANE), lambda i: (i, kr_blk)),
                  row(LANE), row(LANE), pl.BlockSpec((SUB, LANE), lambda i: (0, 0)),
                  row(2 * hw), row(2 * hw), row(hw)],
        out_specs=[row(2 * hw), row(2 * hw), row(LANE), pl.BlockSpec((SUB, LANE), lambda i: (0, 0))],
        out_shape=[jax.ShapeDtypeStruct((s, 2 * hw), BF), jax.ShapeDtypeStruct((s, 2 * hw), BF),
                   jax.ShapeDtypeStruct((s, LANE), F32), jax.ShapeDtypeStruct((SUB, LANE), F32)],
        compiler_params=_cp("arbitrary"),
    )(q_raw, kv_raw, z_a, cos, sin, gains, dq_att, dk_att, dv)


def _causal_mask(nrows, ncols, row0):
    rows = lax.broadcasted_iota(jnp.int32, (nrows, ncols), 0) + row0
    cols = lax.broadcasted_iota(jnp.int32, (nrows, ncols), 1)
    return cols <= rows


def _causal_steps(nt, q_major):
    pairs = ([(i, j) for i in range(nt) for j in range(i + 1)] if q_major
             else [(i, j) for j in range(nt) for i in range(j, nt)])
    return (jnp.array([p[0] for p in pairs], jnp.int32), jnp.array([p[1] for p in pairs], jnp.int32))


def _attn_fwd(q_att, k_att, v, heads):
    s = q_att.shape[0]
    t = _pick(s, ATTN_TILE_FWD, LANE)
    nt = s // t
    th = t // 2
    qi, kj = _causal_steps(nt, True)

    def body(qi_ref, kj_ref, q_ref, k_ref, v_ref, o_ref, ob_ref, lse_ref, m_s, l_s, acc_s):
        st = pl.program_id(1)
        i = qi_ref[st]
        j = kj_ref[st]

        @pl.when(j == 0)
        def _():
            m_s[...] = jnp.full_like(m_s, NEG_INF)
            l_s[...] = jnp.zeros_like(l_s)
            acc_s[...] = jnp.zeros_like(acc_s)

        def update(rows, ncol, masked):
            sc = lax.dot_general(q_ref[rows, :], k_ref[0:ncol, :], (((1,), (1,)), ((), ())),
                                 preferred_element_type=F32)
            if masked:
                sc = jnp.where(_causal_mask(rows.stop - rows.start, ncol, rows.start), sc, NEG_INF)
            m_prev = m_s[rows, :]
            m_new = jnp.maximum(m_prev, jnp.max(sc, axis=-1, keepdims=True))
            alpha = jnp.exp2(m_prev - m_new)
            p = jnp.exp2(sc - jnp.tile(m_new, (1, ncol // LANE)))
            l_s[rows, :] = alpha * l_s[rows, :] + jnp.sum(p, axis=-1, keepdims=True)
            acc_s[rows, :] = alpha * acc_s[rows, :] + jnp.dot(p.astype(BF), v_ref[0:ncol, :],
                                                              preferred_element_type=F32)
            m_s[rows, :] = m_new

        @pl.when(j < i)
        def _():
            update(slice(0, t), t, False)

        @pl.when(j == i)
        def _():
            update(slice(0, th), th, True)
            update(slice(th, t), t, True)
            o = acc_s[...] / l_s[...]
            o_ref[...] = o
            ob_ref[...] = o.astype(BF)
            lse_ref[...] = (m_s[...] + jnp.log2(l_s[...]))[:, 0:1]

    q_idx = lambda h, st, qi_r, kj_r: (qi_r[st], h)
    kv_idx = lambda h, st, qi_r, kj_r: (kj_r[st], h)
    return pl.pallas_call(
        body, name="attn_fwd",
        grid_spec=pltpu.PrefetchScalarGridSpec(
            num_scalar_prefetch=2, grid=(heads, qi.shape[0]),
            in_specs=[pl.BlockSpec((t, 2 * LANE), q_idx), pl.BlockSpec((t, 2 * LANE), kv_idx),
                      pl.BlockSpec((t, LANE), kv_idx)],
            out_specs=[pl.BlockSpec((t, LANE), q_idx), pl.BlockSpec((t, LANE), q_idx),
                       pl.BlockSpec((None, t, 1), lambda h, st, qi_r, kj_r: (h, qi_r[st], 0))],
            scratch_shapes=[pltpu.VMEM((t, LANE), F32), pltpu.VMEM((t, LANE), F32), pltpu.VMEM((t, LANE), F32)]),
        out_shape=[jax.ShapeDtypeStruct((s, heads * LANE), F32), jax.ShapeDtypeStruct((s, heads * LANE), BF),
                   jax.ShapeDtypeStruct((heads, s, 1), F32)],
        compiler_params=_cp("parallel", "arbitrary"),
    )(qi, kj, q_att, k_att, v)


def _attn_bwd(q_att, k_att, v, o, lse, d_o, heads, dep=None):
    s = q_att.shape[0]
    t = _pick(s, ATTN_TILE, LANE)
    nt = s // t
    th = t // 2
    qi, kj = _causal_steps(nt, False)

    def body(qi_ref, kj_ref, q_ref, k_ref, v_ref, do_ref, o_ref, lse_ref, *rest):
        dq_ref, dk_ref, dv_ref, dk_s, dv_s = rest[-5:]
        st = pl.program_id(1)
        i = qi_ref[st]
        j = kj_ref[st]

        @pl.when(st == 0)
        def _():
            dq_ref[...] = jnp.zeros_like(dq_ref)

        @pl.when(i == j)
        def _():
            dk_s[...] = jnp.zeros_like(dk_s)
            dv_s[...] = jnp.zeros_like(dv_s)

        def update(rows, ncol, masked):
            nrow = rows.stop - rows.start
            q = q_ref[rows, :]
            k = k_ref[0:ncol, :]
            do = do_ref[rows, :]
            sc = lax.dot_general(q, k, (((1,), (1,)), ((), ())), preferred_element_type=F32)
            if masked:
                sc = jnp.where(_causal_mask(nrow, ncol, rows.start), sc, NEG_INF)
            p = jnp.exp2(sc - lse_ref[rows, :])
            dp = lax.dot_general(do, v_ref[0:ncol, :], (((1,), (1,)), ((), ())), preferred_element_type=F32)
            delta = jnp.sum(do.astype(F32) * o_ref[rows, :], axis=-1, keepdims=True)
            ds = (p * (dp - delta)).astype(BF)
            dv_s[0:ncol, :] += lax.dot_general(p.astype(BF), do, (((0,), (0,)), ((), ())),
                                               preferred_element_type=F32)
            dk_s[0:ncol, :] += lax.dot_general(ds, q, (((0,), (0,)), ((), ())), preferred_element_type=F32)
            out_rows = pl.ds(pl.multiple_of(i * t + rows.start, nrow), nrow)
            dq_ref[out_rows, :] += jnp.dot(ds, k, preferred_element_type=F32)

        @pl.when(i > j)
        def _():
            update(slice(0, t), t, False)

        @pl.when(i == j)
        def _():
            update(slice(0, th), th, True)
            update(slice(th, t), t, True)

        @pl.when(i == nt - 1)
        def _():
            dk_ref[...] = (dk_s[...] * (1.0 / LOG2_E)).astype(BF)
            dv_ref[...] = dv_s[...].astype(BF)

    q_idx = lambda h, st, qi_r, kj_r: (qi_r[st], h)
    kv_idx = lambda h, st, qi_r, kj_r: (kj_r[st], h)
    in_specs = [pl.BlockSpec((t, 2 * LANE), q_idx), pl.BlockSpec((t, 2 * LANE), kv_idx),
                pl.BlockSpec((t, LANE), kv_idx), pl.BlockSpec((t, LANE), q_idx), pl.BlockSpec((t, LANE), q_idx),
                pl.BlockSpec((None, t, 1), lambda h, st, qi_r, kj_r: (h, qi_r[st], 0))]
    args = [q_att, k_att, v, d_o, o, lse]
    if dep is not None:
        in_specs.append(ANY)
        args.append(dep)
    return pl.pallas_call(
        body, name="attn_bwd",
        grid_spec=pltpu.PrefetchScalarGridSpec(
            num_scalar_prefetch=2, grid=(heads, qi.shape[0]),
            in_specs=in_specs,
            out_specs=[pl.BlockSpec((s, 2 * LANE), lambda h, st, qi_r, kj_r: (0, h)),
                       pl.BlockSpec((t, 2 * LANE), kv_idx), pl.BlockSpec((t, LANE), kv_idx)],
            scratch_shapes=[pltpu.VMEM((t, 2 * LANE), F32), pltpu.VMEM((t, LANE), F32)]),
        out_shape=[jax.ShapeDtypeStruct((s, heads * 2 * LANE), F32),
                   jax.ShapeDtypeStruct((s, heads * 2 * LANE), BF),
                   jax.ShapeDtypeStruct((s, heads * LANE), BF)],
        compiler_params=_cp("parallel", "arbitrary"),
    )(qi, kj, *args)


def _sum_parts(parts, name):
    n, r, c = parts.shape
    tr = _pick(r, 512, 8)

    def body(p_ref, o_ref):
        g = p_ref[0].astype(F32)
        for k in range(1, n):
            g = g + p_ref[k].astype(F32)
        o_ref[...] = g

    return pl.pallas_call(
        body, name=name, grid=(r // tr,),
        in_specs=[pl.BlockSpec((n, tr, c), lambda i: (0, i, 0))],
        out_specs=pl.BlockSpec((tr, c), lambda i: (i, 0)),
        out_shape=jax.ShapeDtypeStruct((r, c), F32),
        compiler_params=_cp("parallel"),
    )(parts)


def _adamw(parts, w, m, v, name, by_cols=False):
    n, rp, c = parts.shape
    r = w.shape[0]
    assert by_cols or rp == r
    tr, tc = (r, _pick(c, 256, LANE)) if by_cols else (_pick(r, 256, 16 if r % 16 == 0 else 8), c)

    def body(p_ref, w_ref, m_ref, v_ref, g_ref, d_ref, mo_ref, vo_ref):
        g = p_ref[0].astype(F32)
        for k in range(1, n):
            g = g + p_ref[k].astype(F32)
        g = g[:r] if by_cols else g
        m_new = ADAM_B1 * m_ref[...] + (1.0 - ADAM_B1) * g
        v_new = ADAM_B2 * v_ref[...] + (1.0 - ADAM_B2) * jnp.square(g)
        m_hat = m_new / (1.0 - ADAM_B1 ** ADAM_STEP)
        v_hat = v_new / (1.0 - ADAM_B2 ** ADAM_STEP)
        g_ref[...] = g
        d_ref[...] = -ADAM_LR * (m_hat / (jnp.sqrt(v_hat) + ADAM_EPS) + ADAM_WD * w_ref[...])
        mo_ref[...] = m_new
        vo_ref[...] = v_new

    idx = (lambda i: (0, i)) if by_cols else (lambda i: (i, 0))
    spec = pl.BlockSpec((tr, tc), idx)
    sh = jax.ShapeDtypeStruct((r, c), F32)
    return pl.pallas_call(
        body, name=name, grid=(c // tc if by_cols else r // tr,),
        in_specs=[pl.BlockSpec((n, rp if by_cols else tr, tc), lambda i: (0,) + idx(i)), spec, spec, spec],
        out_specs=[spec] * 4, out_shape=[sh] * 4,
        compiler_params=_cp("parallel"),
    )(parts, w, m, v)


def _place():
    x, y, c = lax.axis_index("x"), lax.axis_index("y"), lax.axis_index("c")
    chips = [(1 - x, y), (x, 1 - y), (1 - x, 1 - y)]
    return x, y, c, chips


def _all_gather(shards, name, dep=None):
    n = len(shards)
    deps = [] if dep is None else list(dep)

    def body(*refs):
        ins, outs = refs[:n], refs[n + len(deps):2 * n + len(deps)]
        send_sems, recv_sems, local_sems = refs[2 * n + len(deps):]
        x, y, c, chips = _place()
        me, sibling = (x, y, c), (x, y, 1 - c)

        def slot(w, p):
            return outs[w].at[4 * p[0] + 2 * p[1] + p[2]]

        def copy(w, k, block, to, src=None):
            return pltpu.make_async_remote_copy(
                src_ref=slot(w, block) if src is None else src, dst_ref=slot(w, block),
                send_sem=send_sems.at[w, k], recv_sem=recv_sems.at[w, k], device_id=to, device_id_type=MESH)

        first = []
        for w in range(n):
            first += [copy(w, 1 + j, me, (*chip, c), src=ins[w]) for j, chip in enumerate(chips)]
            first.append(copy(w, 0, me, sibling, src=ins[w]))
        for cp in first:
            cp.start()
        mine = [pltpu.make_async_copy(ins[w], slot(w, me), local_sems.at[w]) for w in range(n)]
        for cp in mine:
            cp.start()
        passed = []
        for w in range(n):
            for j, chip in enumerate(chips):
                copy(w, 1 + j, (*chip, c), me).wait_recv()
                cp = copy(w, 4 + j, (*chip, c), sibling)
                cp.start()
                passed.append(cp)
        for w in range(n):
            copy(w, 0, sibling, me).wait_recv()
            for j, chip in enumerate(chips):
                copy(w, 4 + j, (*chip, 1 - c), me).wait_recv()
        for cp in first + passed:
            cp.wait_send()
        for cp in mine:
            cp.wait()

    return pl.pallas_call(
        body, name=name,
        in_specs=[ANY] * (n + len(deps)), out_specs=[ANY] * n,
        out_shape=[jax.ShapeDtypeStruct((N_DEV,) + a.shape, a.dtype) for a in shards],
        scratch_shapes=[pltpu.SemaphoreType.DMA((n, 7)), pltpu.SemaphoreType.DMA((n, 7)),
                        pltpu.SemaphoreType.DMA((n,))],
    )(*shards, *deps)


HBM = pl.BlockSpec(memory_space=pltpu.HBM)
SEM = pl.BlockSpec(memory_space=pltpu.SEMAPHORE)
EFFECT = pltpu.SideEffectType.DATAFLOW_SIDE_EFFECTING
PEERS = [(dx, dy, dc) for dx in (1, 0) for dy in (1, 0) for dc in (0, 1) if (dx, dy, dc) != (0, 0, 0)]


def _peer(x, y, c, flip):
    dx, dy, dc = flip
    return (1 - x if dx else x, 1 - y if dy else y, 1 - c if dc else c)


def _exchange_copies(srcs, lands, send, recv, loc, gather):
    x, y, c, _ = _place()
    me = 4 * x + 2 * y + c
    remote, local = [], []
    for w in range(len(srcs)):
        for k, flip in enumerate(PEERS):
            px, py, pc = _peer(x, y, c, flip)
            src = srcs[w] if gather else srcs[w].at[4 * px + 2 * py + pc]
            remote.append(pltpu.make_async_remote_copy(
                src_ref=src, dst_ref=lands[w].at[me], send_sem=send[w].at[k], recv_sem=recv[w].at[k],
                device_id=(px, py, pc), device_id_type=MESH))
        local.append(pltpu.make_async_copy(srcs[w] if gather else srcs[w].at[me], lands[w].at[me], loc[w]))
    return remote, local


class _Exchange:
    def __init__(self, srcs, lands, send, recv, loc, token, gather):
        self.srcs, self.lands, self.send, self.recv, self.loc = srcs, lands, send, recv, loc
        self.token, self.gather = token, gather


def _exchange_start(srcs, gather, name, dep=None):
    n = len(srcs)
    deps = [] if dep is None else [dep]
    land_shapes = [((N_DEV,) + a.shape) if gather else a.shape for a in srcs]
    lands = [pltpu.with_memory_space_constraint(lax.empty(sh, a.dtype), pltpu.HBM) for sh, a in zip(land_shapes, srcs)]
    srcs = [pltpu.with_memory_space_constraint(a, pltpu.HBM) for a in srcs]

    def body(*refs):
        src_refs, land_refs = refs[:n], refs[n:2 * n]
        outs = refs[2 * n + len(deps):]
        send, recv, loc = outs[:n], outs[n:2 * n], outs[2 * n:3 * n]
        token = outs[-1]
        remote, local = _exchange_copies(src_refs, land_refs, send, recv, loc, gather)
        for cp in remote + local:
            cp.start()
        token[...] = jnp.zeros_like(token)

    out_shape = ([pltpu.SemaphoreType.DMA((len(PEERS),))] * (2 * n) + [pltpu.SemaphoreType.DMA(())] * n
                 + [pltpu.HBM(a.shape, a.dtype) for a in srcs] + [pltpu.HBM(a.shape, a.dtype) for a in lands]
                 + [jax.ShapeDtypeStruct((SUB, LANE), F32)])
    res = pl.pallas_call(
        body, name=name, out_shape=out_shape,
        in_specs=[HBM] * (2 * n) + [ANY] * len(deps),
        out_specs=[SEM] * (3 * n) + [HBM] * (2 * n) + [pl.BlockSpec(memory_space=pltpu.VMEM)],
        input_output_aliases={i: 3 * n + i for i in range(2 * n)},
        compiler_params=pltpu.CompilerParams(has_side_effects=EFFECT),
    )(*srcs, *lands, *deps)
    return _Exchange(res[3 * n:4 * n], res[4 * n:5 * n], res[:n], res[n:2 * n], res[2 * n:3 * n], res[-1], gather)


def _exchange_wait(ex, idxs, after, name):
    n = len(idxs)
    srcs = [ex.srcs[i] for i in idxs]
    lands = [ex.lands[i] for i in idxs]
    sems = [ex.send[i] for i in idxs] + [ex.recv[i] for i in idxs] + [ex.loc[i] for i in idxs]
    gather = ex.gather

    def body(*refs):
        src_refs, land_refs = refs[:n], refs[n:2 * n]
        send, recv, loc = refs[2 * n:3 * n], refs[3 * n:4 * n], refs[4 * n:5 * n]
        remote, local = _exchange_copies(src_refs, land_refs, send, recv, loc, gather)
        for cp in remote:
            cp.wait_send()
            cp.wait_recv()
        for cp in local:
            cp.wait()

    res = pl.pallas_call(
        body, name=name,
        out_shape=[pltpu.HBM(a.shape, a.dtype) for a in srcs] + [pltpu.HBM(a.shape, a.dtype) for a in lands],
        in_specs=[HBM] * (2 * n) + [SEM] * (3 * n) + [ANY],
        out_specs=[HBM] * (2 * n),
        input_output_aliases={i: i for i in range(2 * n)},
        compiler_params=pltpu.CompilerParams(has_side_effects=EFFECT),
    )(*srcs, *lands, *sems, after)
    return res[n:]


def _gather2_copies(srcs, lands, send, recv_ici, recv_sib, loc):
    x, y, c, chips = _place()
    me = 4 * x + 2 * y + c
    remote, local = [], []
    for w in range(len(srcs)):
        remote.append(pltpu.make_async_remote_copy(
            src_ref=srcs[w], dst_ref=lands[w].at[me], send_sem=send[w].at[0], recv_sem=recv_sib[w],
            device_id=(x, y, 1 - c), device_id_type=MESH))
        for j, chip in enumerate(chips):
            remote.append(pltpu.make_async_remote_copy(
                src_ref=srcs[w], dst_ref=lands[w].at[me], send_sem=send[w].at[1 + j], recv_sem=recv_ici[w].at[j],
                device_id=(*chip, c), device_id_type=MESH))
        local.append(pltpu.make_async_copy(srcs[w], lands[w].at[me], loc[w]))
    return remote, local


def _gather2_forwards(lands, fsend, frecv, arrived=None):
    x, y, c, chips = _place()
    cps = []
    for w in range(len(lands)):
        for j, chip in enumerate(chips):
            slot = lands[w].at[4 * chip[0] + 2 * chip[1] + c]
            cp = pltpu.make_async_remote_copy(
                src_ref=slot, dst_ref=slot, send_sem=fsend[w].at[j], recv_sem=frecv[w].at[j],
                device_id=(x, y, 1 - c), device_id_type=MESH)
            if arrived is not None:
                pltpu.make_async_remote_copy(
                    src_ref=slot, dst_ref=slot, send_sem=fsend[w].at[j], recv_sem=arrived[w].at[j],
                    device_id=(x, y, 1 - c), device_id_type=MESH).wait_recv()
            cps.append(cp)
    return cps


def _gather2(shards, between, name):
    n = len(shards)
    srcs = [pltpu.with_memory_space_constraint(a, pltpu.HBM) for a in shards]
    lands = [pltpu.with_memory_space_constraint(lax.empty((N_DEV,) + a.shape, a.dtype), pltpu.HBM) for a in shards]
    hbm_like = lambda arrs: [pltpu.HBM(a.shape, a.dtype) for a in arrs]
    tok = jax.ShapeDtypeStruct((SUB, LANE), F32)
    vmem = pl.BlockSpec(memory_space=pltpu.VMEM)
    side = pltpu.CompilerParams(has_side_effects=EFFECT)

    def start(*refs):
        src_refs, land_refs = refs[:n], refs[n:2 * n]
        outs = refs[2 * n:]
        send, recv_ici, recv_sib, loc = outs[:n], outs[n:2 * n], outs[2 * n:3 * n], outs[3 * n:4 * n]
        remote, local = _gather2_copies(src_refs, land_refs, send, recv_ici, recv_sib, loc)
        for cp in remote + local:
            cp.start()
        outs[-1][...] = jnp.zeros((SUB, LANE), F32)

    res = pl.pallas_call(
        start, name=name + "_start",
        out_shape=([pltpu.SemaphoreType.DMA((4,))] * n + [pltpu.SemaphoreType.DMA((3,))] * n
                   + [pltpu.SemaphoreType.DMA(())] * (2 * n) + hbm_like(srcs) + hbm_like(lands) + [tok]),
        in_specs=[HBM] * (2 * n), out_specs=[SEM] * (4 * n) + [HBM] * (2 * n) + [vmem],
        input_output_aliases={i: 4 * n + i for i in range(2 * n)}, compiler_params=side,
    )(*srcs, *lands)
    send, recv_ici, recv_sib, loc = res[:n], res[n:2 * n], res[2 * n:3 * n], res[3 * n:4 * n]
    srcs, lands, token = res[4 * n:5 * n], res[5 * n:6 * n], res[-1]

    done = between(token)
    after = jax.tree_util.tree_leaves(done)

    def forward(*refs):
        land_refs, arrived = refs[:n], refs[n:2 * n]
        outs = refs[2 * n + len(after):]
        fsend, frecv = outs[:n], outs[n:2 * n]
        for cp in _gather2_forwards(land_refs, fsend, frecv, arrived):
            cp.start()
        outs[-1][...] = jnp.zeros((SUB, LANE), F32)

    res = pl.pallas_call(
        forward, name=name + "_forward",
        out_shape=[pltpu.SemaphoreType.DMA((3,))] * (2 * n) + hbm_like(lands) + [tok],
        in_specs=[HBM] * n + [SEM] * n + [ANY] * len(after), out_specs=[SEM] * (2 * n) + [HBM] * n + [vmem],
        input_output_aliases={i: 2 * n + i for i in range(n)}, compiler_params=side,
    )(*lands, *recv_ici, *after)
    fsend, frecv, lands, token = res[:n], res[n:2 * n], res[2 * n:3 * n], res[-1]

    def wait(*refs):
        src_refs, land_refs = refs[:n], refs[n:2 * n]
        sems = refs[2 * n:7 * n]
        send, recv_sib, loc, fsend, frecv = (sems[k * n:(k + 1) * n] for k in range(5))
        remote, local = _gather2_copies(src_refs, land_refs, send, send, recv_sib, loc)
        for w in range(n):
            for cp in remote[4 * w:4 * w + 4]:
                cp.wait_send()
            remote[4 * w].wait_recv()
        for cp in local:
            cp.wait()
        for cp in _gather2_forwards(land_refs, fsend, frecv):
            cp.wait_send()
            cp.wait_recv()

    res = pl.pallas_call(
        wait, name=name + "_wait", out_shape=hbm_like(srcs) + hbm_like(lands),
        in_specs=[HBM] * (2 * n) + [SEM] * (5 * n) + [ANY], out_specs=[HBM] * (2 * n),
        input_output_aliases={i: i for i in range(2 * n)}, compiler_params=side,
    )(*srcs, *lands, *send, *recv_sib, *loc, *fsend, *frecv, token)
    return res[n:], done


def _after(token, a):
    return a + token[0:1, 0:1].astype(a.dtype)


def _unblock(w3):
    nb, k, nbw = w3.shape
    return w3.transpose(1, 0, 2).reshape(k, nb * nbw)


def _block(w, nb):
    k, n = w.shape
    return w.reshape(k, nb, n // nb).transpose(1, 0, 2)


def kernel(x, positions, ln1_g, w_in, b_gate, conv_w, w_conv_out, q_a_g, w_q_b, kv_a_g, w_kv_b, q_norm_g, k_norm_g, w_mla_out, w_o, ln2_g, w_ffn_up, ffn_conv_w, ffn_conv_b, w_ffn_down, loss_target, m_ln1_g, m_w_in, m_b_gate, m_conv_w, m_w_conv_out, m_q_a_g, m_w_q_b, m_kv_a_g, m_w_kv_b, m_q_norm_g, m_k_norm_g, m_w_mla_out, m_w_o, m_ln2_g, m_w_ffn_up, m_ffn_conv_w, m_ffn_conv_b, m_w_ffn_down, v_ln1_g, v_w_in, v_b_gate, v_conv_w, v_w_conv_out, v_q_a_g, v_w_q_b, v_kv_a_g, v_w_kv_b, v_q_norm_g, v_k_norm_g, v_w_mla_out, v_w_o, v_ln2_g, v_w_ffn_up, v_ffn_conv_w, v_ffn_conv_b, v_w_ffn_down):
    s, d = x.shape[1], x.shape[2]
    conv = conv_w.shape[2] * N_DEV
    ql, kvl = q_a_g.shape[1], kv_a_g.shape[1]
    heads = w_q_b.shape[2] * N_DEV // HEAD_QK
    dff = w_ffn_down.shape[1] * N_DEV
    hw = heads * LANE
    conv3 = 3 * conv
    kr_off = conv3 + ql
    kv_off = -(-(kr_off + LANE) // kvl) * kvl
    wa = kv_off + kvl
    assert conv3 % ql == 0 and kr_off % LANE == 0
    xs = x[0]
    tgt = loss_target[0]
    pos = positions.reshape(s, 1)

    nin = w_in.shape[2]
    big = dict(w_in=w_in[0].T, w_conv_out=w_conv_out[0], w_q_b=w_q_b[0], w_kv_b=w_kv_b[0],
               w_mla_out=w_mla_out[0], w_o=w_o[0], w_ffn_up=w_ffn_up[0], w_ffn_down=w_ffn_down[0])
    names = list(big)
    rest = names[1:]
    early = {}

    def while_w_in_travels(token):
        early["ag"] = _exchange_start([big[k].astype(BF) for k in rest], True, "gather_rest_start", dep=token)
        cos_sin = _rope_tables(pos)
        return cos_sin, _rms_fwd(xs, _after(early["ag"].token, ln1_g), d, 0, "rms1_fwd")

    first, ((cos, sin), u1) = _gather2([big["w_in"].astype(BF), _pad8(conv_w[0]), _pad8(ffn_conv_w[0])],
                                       while_w_in_travels, "gather_w_in")
    ag = early["ag"]
    cw8 = _unblock(first[1])
    fcw8 = _unblock(first[2])

    def landed(keys, after, name):
        return _exchange_wait(ag, [rest.index(k) for k in keys], after, name)

    w_in_t = first[0].reshape(N_DEV * nin, d)
    g_off = kr_off + kvl + ROPE
    w_a_t = jnp.concatenate([w_in_t[:kr_off], _lay_rows(w_in_t[kr_off + kvl:g_off]),
                             jnp.zeros((kv_off - kr_off - LANE, d), BF), w_in_t[kr_off:kr_off + kvl]], axis=0)[None]
    w_g_t = w_in_t[g_off:][None]
    gains = _pad8(jnp.concatenate([q_norm_g[:, :NOPE], _lay(q_norm_g[:, NOPE:]),
                                   k_norm_g[:, :NOPE], _lay(k_norm_g[:, NOPE:])], axis=0))
    kr_blk = kr_off // LANE

    z_a = _mm_nt(u1, w_a_t, "mm_z_a")
    z_g = _mm_nt(u1, w_g_t, "mm_z_g", out_dtype=BF)
    p = _conv_mix_fwd(z_a, cw8, conv)
    w_co, w_qb, w_kv = landed(["w_conv_out", "w_q_b", "w_kv_b"], p, "gather_wait_mixers")
    w_co = _unblock(w_co)[None]
    w_kv = _unblock(w_kv)[None]
    wq_full = _unblock(w_qb).reshape(ql, heads, HEAD_QK)
    w_q = jnp.concatenate([wq_full[:, :, :NOPE].reshape(ql, hw), _lay(wq_full[:, :, NOPE:]).reshape(ql, hw)],
                          axis=1)[None]
    yc = _mm_nn(p, w_co, "mm_y_conv", out_dtype=BF)
    qn = _rms_fwd(z_a, q_a_g, ql, conv3 // ql, "rms_q_fwd")
    kvn = _rms_fwd(z_a, kv_a_g, kvl, kv_off // kvl, "rms_kv_fwd")
    q_raw = _mm_nn(qn, w_q, "mm_q")
    kv_raw = _mm_nn(kvn, w_kv, "mm_kv")
    q_att, k_att, v_bf = _head_fwd(q_raw, kv_raw, z_a, kr_blk, cos, sin, gains, heads)
    o, o_bf, lse = _attn_fwd(q_att, k_att, v_bf, heads)
    w_mo, w_oo = landed(["w_mla_out", "w_o"], lse, "gather_wait_outs")
    w_mo = w_mo.reshape(1, hw, d)
    w_oo = w_oo.reshape(1, d, d)
    ym, mix = _mla_out_gate(o_bf, w_mo, z_g, b_gate, yc)
    h1 = _mm_nn(mix, w_oo, "mm_h1", add=xs)
    u2 = _rms_fwd(h1, ln2_g, d, 0, "rms2_fwd")
    w_up, = landed(["w_ffn_up"], u2, "gather_wait_ffn_up")
    a_g, a_u, f = _ffn_up_act(u2, w_up, fcw8, ffn_conv_b, dff)
    w_dn, = landed(["w_ffn_down"], f, "gather_wait_ffn_down")
    w_dn = w_dn.reshape(1, dff, d)
    dy, dy_bf, loss_part = _mm_nn_loss(f, w_dn, h1, tgt, "mm_ffn_down_loss")

    g_dn = _mm_tn(f, dy_bf, 1, "mm_g_ffn_down").reshape(N_DEV, dff // N_DEV, d)
    rs_dn = _exchange_start([g_dn], False, "reduce_ffn_down_start")
    d_f = _mm_nt(dy_bf, w_dn, "mm_d_f", dep=rs_dn.token)
    d_xg, d_xu, dfw_g, dfw_u = _ffn_act_bwd(a_g, a_u, d_f, fcw8, ffn_conv_b, dff)
    half = N_DEV // 2
    g_up = _mm_tn(u2, d_xg, half, "mm_g_ffn_up_gate", into=lax.empty((N_DEV, d, 2 * dff // N_DEV), BF))
    g_up = _mm_tn(u2, d_xu, half, "mm_g_ffn_up_up", into=g_up, blk0=half)
    rs_up = _exchange_start([g_up], False, "reduce_ffn_up_start")
    d_u2 = _mm_nt([d_xg, d_xu], w_up, "mm_d_u2", out_dtype=BF, dep=rs_up.token)
    d_h1, d_h1_bf, dg_ln2 = _rms_bwd(h1, d_u2, ln2_g, d, 0, "rms2_bwd", extra=dy, also_bf16=True)
    g_oo = _mm_tn(mix, d_h1_bf, 1, "mm_g_w_o").reshape(N_DEV, d // N_DEV, d)
    d_zga, d_zgb, d_yc, d_ym, dba, dbb = _d_mix_gate(d_h1_bf, w_oo, z_g, b_gate, yc, ym)
    g_co = _block(_mm_tn(p, d_yc, 1, "mm_g_conv_out")[0], N_DEV)
    g_mo = _mm_tn(o_bf, d_ym, 1, "mm_g_mla_out").reshape(N_DEV, hw // N_DEV, d)
    rs_mix = _exchange_start([g_oo, g_co, g_mo], False, "reduce_mixers_start")
    d_p = _mm_nt(d_yc, w_co, "mm_d_p", dep=rs_mix.token)
    d_o = _mm_nt(d_ym, w_mo, "mm_d_o", out_dtype=BF)
    d_zb, d_zc, d_zv, dcw = _conv_mix_bwd(z_a, d_p, cw8, conv)
    dq_att, dk_att, dv = _attn_bwd(q_att, k_att, v_bf, o, lse, d_o, heads, dep=rs_mix.token)
    d_q_raw, d_kv_raw, d_kr, dgains = _head_bwd(q_raw, kv_raw, z_a, kr_blk, cos, sin, gains, dq_att, dk_att, dv, heads)
    g_q2 = _mm_tn(qn, d_q_raw, 1, "mm_g_q")[0]
    g_qb = _block(jnp.concatenate([g_q2[:, :hw].reshape(ql, heads, NOPE),
                                   _unlay(g_q2[:, hw:].reshape(ql, heads, LANE))], axis=2).reshape(ql, heads * HEAD_QK), N_DEV)
    g_kv = _block(_mm_tn(kvn, d_kv_raw, 1, "mm_g_kv")[0], N_DEV)
    rs_qkv = _exchange_start([g_qb, g_kv], False, "reduce_qkv_start")
    d_qn = _mm_nt(d_q_raw, w_q, "mm_d_qn", dep=rs_qkv.token)
    d_kvn = _mm_nt(d_kv_raw, w_kv, "mm_d_kvn")
    d_ql, dg_qa = _rms_bwd(z_a, d_qn, q_a_g, ql, conv3 // ql, "rms_q_bwd", out_dtype=BF)
    d_kvl, dg_kva = _rms_bwd(z_a, d_kvn, kv_a_g, kvl, kv_off // kvl, "rms_kv_bwd", out_dtype=BF)
    d_z_a = jnp.concatenate([d_zb, d_zc, d_zv, d_ql, d_kr.astype(BF), jnp.zeros((s, kv_off - kr_off - LANE), BF),
                             d_kvl], axis=1)
    g_a = _mm_tn(d_z_a, u1, 1, "mm_g_w_a")[0]
    g_ga = _mm_tn(d_zga, u1, 1, "mm_g_w_ga")[0]
    g_gb = _mm_tn(d_zgb, u1, 1, "mm_g_w_gb")[0]
    g_in = jnp.concatenate([g_a[:kr_off], g_a[kv_off:kv_off + kvl], g_a[kr_off:kr_off + HALF],
                            g_a[kr_off + 2 * HALF:kr_off + 3 * HALF], g_ga, g_gb], axis=0).reshape(N_DEV, nin, d)
    rs_in = _exchange_start([g_in], False, "reduce_w_in_start")
    d_u1 = _mm_nn(d_z_a, w_a_t, "mm_d_u1_a", dep=rs_in.token)
    d_u1 = _mm_nn([d_zga, d_zgb], w_g_t, "mm_d_u1_g", add=d_u1)
    grad_x, dg_ln1 = _rms_bwd(xs, d_u1, ln1_g, d, 0, "rms1_bwd", extra=d_h1)

    summed = {}
    summed["w_ffn_down"], = _exchange_wait(rs_dn, [0], grad_x, "reduce_ffn_down_wait")
    summed["w_ffn_up"], = _exchange_wait(rs_up, [0], grad_x, "reduce_ffn_up_wait")
    summed["w_o"], summed["w_conv_out"], summed["w_mla_out"] = _exchange_wait(rs_mix, [0, 1, 2], grad_x, "reduce_mixers_wait")
    summed["w_q_b"], summed["w_kv_b"] = _exchange_wait(rs_qkv, [0, 1], grad_x, "reduce_qkv_wait")
    loc = locals()
    out = {}
    for k in rest:
        out[k] = _adamw(summed[k], big[k], loc["m_" + k][0], loc["v_" + k][0], "adamw_" + k)

    small = dict(ln1_g=dg_ln1[0:1], b_gate=jnp.concatenate([dba[0:1], dbb[0:1]], axis=1), q_a_g=dg_qa[0:1],
                 kv_a_g=dg_kva[0:1],
                 q_norm_g=jnp.concatenate([dgains[0:1], _unlay(dgains[1:2])], axis=1),
                 k_norm_g=jnp.concatenate([dgains[2:3], _unlay(dgains[3:4])], axis=1),
                 ln2_g=dg_ln2[0:1], ffn_conv_b=jnp.concatenate([dfw_g[3:4], dfw_u[3:4]], axis=1))
    small_names = list(small)
    extra = [dcw[0:3].reshape(1, -1), jnp.concatenate([dfw_g[0:3], dfw_u[0:3]], axis=1).reshape(1, -1),
             loss_part[0:1, 0:1]]
    flat = jnp.concatenate([small[k] for k in small_names] + extra, axis=1)
    n_flat = flat.shape[1]
    rows = -(-n_flat // (SUB * LANE)) * SUB
    flat = jnp.pad(flat, ((0, 0), (0, rows * LANE - n_flat))).reshape(rows, LANE)
    total = _sum_parts(_all_gather([flat], "gather_small", dep=[out[k][0] for k in rest])[0], "sum_small").reshape(1, rows * LANE)
    off = 0
    small_g = {}
    for k in small_names:
        small_g[k] = total[:, off:off + small[k].shape[1]]
        off += small[k].shape[1]
    me = 4 * lax.axis_index("x") + 2 * lax.axis_index("y") + lax.axis_index("c")
    cwn, fcwn = conv // N_DEV, 2 * dff // N_DEV
    g_cw = lax.dynamic_slice_in_dim(total[:, off:off + 3 * conv].reshape(3, conv), me * cwn, cwn, axis=1)
    off += 3 * conv
    g_fcw = lax.dynamic_slice_in_dim(total[:, off:off + 6 * dff].reshape(3, 2 * dff), me * fcwn, fcwn, axis=1)
    off += 6 * dff
    loss = total[0, off]

    summed["w_in"], = _exchange_wait(rs_in, [0], total, "reduce_w_in_wait")
    out["w_in"] = [r.T for r in _adamw(summed["w_in"], big["w_in"], m_w_in[0].T, v_w_in[0].T, "adamw_w_in",
                                       by_cols=True)]
    small_w = dict(ln1_g=ln1_g, b_gate=b_gate, q_a_g=q_a_g, kv_a_g=kv_a_g, q_norm_g=q_norm_g, k_norm_g=k_norm_g,
                   ln2_g=ln2_g, ffn_conv_b=ffn_conv_b, conv_w=conv_w[0].reshape(1, -1),
                   ffn_conv_w=ffn_conv_w[0].reshape(1, -1))
    small_g["conv_w"] = g_cw.reshape(1, -1)
    small_g["ffn_conv_w"] = g_fcw.reshape(1, -1)
    packed_names = list(small_w)

    def pack(get):
        vflat = jnp.concatenate([get(k).reshape(1, -1) for k in packed_names], axis=1)
        nr = -(-vflat.shape[1] // (SUB * LANE)) * SUB
        return jnp.pad(vflat, ((0, 0), (0, nr * LANE - vflat.shape[1])), constant_values=1.0).reshape(nr, LANE)

    res = _adamw(pack(lambda k: small_g[k])[None], pack(lambda k: small_w[k]), pack(lambda k: loc["m_" + k]),
                 pack(lambda k: loc["v_" + k]), "adamw_small")
    res = [r.reshape(1, -1) for r in res]
    off = 0
    for k in packed_names:
        shape = loc[k].shape
        size = small_w[k].shape[1]
        out[k] = [r[:, off:off + size].reshape(shape) for r in res]
        off += size
    for k in names:
        out[k] = [r[None] for r in out[k]]

    order = ["ln1_g", "w_in", "b_gate", "conv_w", "w_conv_out", "q_a_g", "w_q_b", "kv_a_g", "w_kv_b", "q_norm_g",
             "k_norm_g", "w_mla_out", "w_o", "ln2_g", "w_ffn_up", "ffn_conv_w", "ffn_conv_b", "w_ffn_down"]
    return (loss, grad_x[None], *[out[k][0] for k in order], *[out[k][1] for k in order],
            *[out[k][2] for k in order], *[out[k][3] for k in order])
```

```python
import functools

import jax
import jax.numpy as jnp
from jax import lax
from jax.experimental import pallas as pl
from jax.experimental.pallas import tpu as pltpu

BF = jnp.bfloat16
F32 = jnp.float32
MESH = pl.DeviceIdType.MESH
N_DEV = 8

NOPE = 128
ROPE = 64
HALF = ROPE // 2
HEAD_QK = NOPE + ROPE
HEAD_V = 128
LANE = 128
SUB = 8
QK_SCALE = HEAD_QK ** -0.5
LOG2_E = 1.4426950408889634
NORM_EPS = 1e-6
NEG_INF = -1e30
ROPE_THETA = 10000.0
ADAM_LR = 0.001
ADAM_B1 = 0.9
ADAM_B2 = 0.999
ADAM_EPS = 1e-08
ADAM_WD = 0.01
ADAM_STEP = 10

VMEM_LIMIT = 52 * 1024 * 1024
MM_TM, MM_TN, MM_TK, MM_TS = 1024, 1536, 2048, 2048
ROW_TILE, ROW_TILE_BWD = 512, 256
HEAD_ROW_TILE, HEAD_ROW_TILE_BWD = 256, 256
COL_TILE = 512
ATTN_TILE = 1024
ATTN_TILE_FWD = 1024
ANY = pl.BlockSpec(memory_space=pl.ANY)


def _pick(n, target, mult):
    t = (min(n, target) // mult) * mult
    while t > 0:
        if n % t == 0:
            return t
        t -= mult
    raise ValueError(f"no tile for {n} (target {target}, multiple {mult})")


def _cp(*sem):
    return pltpu.CompilerParams(dimension_semantics=sem, vmem_limit_bytes=VMEM_LIMIT)


def _accumulate(kk, nk, acc, part, finish):
    if nk == 1:
        finish(part())
        return

    @pl.when(kk == 0)
    def _():
        acc[...] = part()

    @pl.when((kk > 0) & (kk < nk - 1))
    def _():
        acc[...] += part()

    @pl.when(kk == nk - 1)
    def _():
        finish(acc[...] + part())


def _mm_call(body, name, grid, in_specs, args, out_spec, out_shape, acc_shape, nk, dep):
    if dep is not None:
        in_specs = in_specs + [ANY]
        args = args + [dep]
    return pl.pallas_call(
        body, name=name, grid=grid, in_specs=in_specs, out_specs=out_spec, out_shape=out_shape,
        scratch_shapes=[pltpu.VMEM(acc_shape, F32)] if nk > 1 else [],
        compiler_params=_cp("parallel", "parallel", "arbitrary"),
    )(*args)


def _mm_nn_loss(a, b3, add, target, name):
    m, k = a.shape
    _, k2, n = b3.shape
    assert k == k2 and b3.shape[0] == 1
    tm = _pick(m, MM_TM, 16)
    tn = _pick(n, MM_TN, LANE)
    tk = _pick(k, MM_TK, LANE)
    nk = k // tk

    def body(a_ref, b_ref, c_ref, t_ref, dy_ref, dyb_ref, l_ref, acc):
        kk = pl.program_id(2)

        @pl.when((pl.program_id(0) == 0) & (pl.program_id(1) == 0) & (kk == 0))
        def _():
            l_ref[...] = jnp.zeros_like(l_ref)

        def part():
            return jnp.dot(a_ref[...].astype(BF), b_ref[0].astype(BF), preferred_element_type=F32)

        def finish(r):
            e = r + c_ref[...] - t_ref[...]
            dy_ref[...] = e / n
            dyb_ref[...] = (e / n).astype(BF)
            l_ref[...] += 0.5 * jnp.sum(jnp.sum(e * e, axis=-1, keepdims=True), axis=0, keepdims=True) / n

        _accumulate(kk, nk, acc, part, finish)

    tile = pl.BlockSpec((tm, tn), lambda i, j, kk: (i, j))
    return pl.pallas_call(
        body, name=name, grid=(m // tm, n // tn, nk),
        in_specs=[pl.BlockSpec((tm, tk), lambda i, j, kk: (i, kk)),
                  pl.BlockSpec((1, tk, tn), lambda i, j, kk: (0, kk, j)), tile, tile],
        out_specs=[tile, tile, pl.BlockSpec((SUB, LANE), lambda i, j, kk: (0, 0))],
        out_shape=[jax.ShapeDtypeStruct((m, n), F32), jax.ShapeDtypeStruct((m, n), BF),
                   jax.ShapeDtypeStruct((SUB, LANE), F32)],
        scratch_shapes=[pltpu.VMEM((tm, tn), F32)],
        compiler_params=_cp("arbitrary", "arbitrary", "arbitrary"),
    )(a, b3, add, target)


def _mm_nn(a, b3, name, add=None, out_dtype=F32, blk0=0, nblk=None, dep=None):
    pair = isinstance(a, (list, tuple))
    a_list = list(a) if pair else [a]
    m, ka = a_list[0].shape
    k = ka * len(a_list)
    nb_all, k2, nbw = b3.shape
    assert k == k2
    nblk = nb_all - blk0 if nblk is None else nblk
    n = nblk * nbw
    tm = _pick(m, MM_TM if k > MM_TM else 2 * MM_TM, 16)
    tn = _pick(nbw, MM_TN, LANE)
    tk = _pick(ka, MM_TK, LANE)
    per = nbw // tn
    nk = k // tk
    nka = ka // tk
    na_ops = len(a_list)

    def body(*refs):
        a_refs, b_ref = refs[:na_ops], refs[na_ops]
        c_ref = refs[na_ops + 1] if add is not None else None
        o_ref = refs[na_ops + 1 + (add is not None) + (dep is not None)]
        acc = refs[-1]
        kk = pl.program_id(2)

        def part():
            av = a_refs[0][...] if not pair else jnp.where(kk < nka, a_refs[0][...], a_refs[1][...])
            return jnp.dot(av.astype(BF), b_ref[...].astype(BF), preferred_element_type=F32)

        def finish(r):
            if add is not None:
                r = r + c_ref[...]
            o_ref[...] = r.astype(out_dtype)

        _accumulate(kk, nk, acc, part, finish)

    if pair:
        in_specs = [pl.BlockSpec((tm, tk), lambda i, j, kk: (i, jnp.minimum(kk, nka - 1))),
                    pl.BlockSpec((tm, tk), lambda i, j, kk: (i, jnp.maximum(kk - nka, 0)))]
    else:
        in_specs = [pl.BlockSpec((tm, tk), lambda i, j, kk: (i, kk))]
    in_specs.append(pl.BlockSpec((None, tk, tn), lambda i, j, kk: (blk0 + j // per, kk, j % per)))
    args = a_list + [b3]
    if add is not None:
        in_specs.append(pl.BlockSpec((tm, tn), lambda i, j, kk: (i, j)))
        args.append(add)
    return _mm_call(body, name, (m // tm, n // tn, nk), in_specs, args,
                    pl.BlockSpec((tm, tn), lambda i, j, kk: (i, j)), jax.ShapeDtypeStruct((m, n), out_dtype),
                    (tm, tn), nk, dep)


def _mm_nt(a, b3, name, add=None, out_dtype=F32, blk0=0, nblk=None, dep=None):
    pair = isinstance(a, (list, tuple))
    a_list = list(a) if pair else [a]
    m, na = a_list[0].shape
    n = na * len(a_list)
    nb_all, k, nbw = b3.shape
    nblk = nb_all - blk0 if nblk is None else nblk
    assert n == nblk * nbw and na % nbw == 0
    tm = _pick(m, 2 * MM_TM if k <= MM_TM and n <= MM_TK else MM_TM, 16)
    tk = _pick(nbw, MM_TK, LANE)
    per = nbw // tk
    nk = n // tk
    tn = _pick(k, MM_TN if nk <= 2 else 2 * MM_TM, LANE)
    nka = na // tk
    na_ops = len(a_list)

    def body(*refs):
        a_refs, b_ref = refs[:na_ops], refs[na_ops]
        c_ref = refs[na_ops + 1] if add is not None else None
        o_ref = refs[na_ops + 1 + (add is not None) + (dep is not None)]
        acc = refs[-1]
        kk = pl.program_id(2)

        def part():
            av = a_refs[0][...] if not pair else jnp.where(kk < nka, a_refs[0][...], a_refs[1][...])
            return lax.dot_general(av.astype(BF), b_ref[...].astype(BF),
                                   (((1,), (1,)), ((), ())), preferred_element_type=F32)

        def finish(r):
            if add is not None:
                r = r + c_ref[...]
            o_ref[...] = r.astype(out_dtype)

        _accumulate(kk, nk, acc, part, finish)

    if pair:
        in_specs = [pl.BlockSpec((tm, tk), lambda i, j, kk: (i, jnp.minimum(kk, nka - 1))),
                    pl.BlockSpec((tm, tk), lambda i, j, kk: (i, jnp.maximum(kk - nka, 0)))]
    else:
        in_specs = [pl.BlockSpec((tm, tk), lambda i, j, kk: (i, kk))]
    in_specs.append(pl.BlockSpec((None, tn, tk), lambda i, j, kk: (blk0 + kk // per, j, kk % per)))
    args = a_list + [b3]
    if add is not None:
        in_specs.append(pl.BlockSpec((tm, tn), lambda i, j, kk: (i, j)))
        args.append(add)
    return _mm_call(body, name, (m // tm, k // tn, nk), in_specs, args,
                    pl.BlockSpec((tm, tn), lambda i, j, kk: (i, j)), jax.ShapeDtypeStruct((m, k), out_dtype),
                    (tm, tn), nk, dep)


def _mm_tn(a, b, nblk, name, out_dtype=BF, dep=None, into=None, blk0=0):
    s, m = a.shape
    s2, n = b.shape
    assert s == s2 and n % nblk == 0 and (dep is None or into is None)
    nbw = n // nblk
    tm = _pick(m, MM_TN, LANE)
    tn = _pick(nbw, MM_TN, LANE)
    ts = _pick(s, MM_TS, LANE)
    per = nbw // tn
    ns = s // ts

    def body(*refs):
        a_ref, b_ref = refs[:2]
        o_ref = refs[2 + (dep is not None or into is not None)]
        acc = refs[-1]

        def part():
            return lax.dot_general(a_ref[...].astype(BF), b_ref[...].astype(BF),
                                   (((0,), (0,)), ((), ())), preferred_element_type=F32)

        def finish(r):
            o_ref[...] = r.astype(out_dtype)

        _accumulate(pl.program_id(2), ns, acc, part, finish)

    in_specs = [pl.BlockSpec((ts, tm), lambda i, j, ss: (ss, i)),
                pl.BlockSpec((ts, tn), lambda i, j, ss: (ss, j))]
    out_spec = pl.BlockSpec((None, tm, tn), lambda i, j, ss: (blk0 + j // per, i, j % per))
    if into is None:
        return _mm_call(body, name, (m // tm, n // tn, ns), in_specs, [a, b], out_spec,
                        jax.ShapeDtypeStruct((nblk, m, nbw), out_dtype), (tm, tn), ns, dep)
    assert into.shape[1:] == (m, nbw) and into.dtype == out_dtype
    return pl.pallas_call(
        body, name=name, grid=(m // tm, n // tn, ns), in_specs=in_specs + [ANY], out_specs=out_spec,
        out_shape=jax.ShapeDtypeStruct(into.shape, out_dtype), input_output_aliases={2: 0},
        scratch_shapes=[pltpu.VMEM((tm, tn), F32)] if ns > 1 else [],
        compiler_params=_cp("parallel", "parallel", "arbitrary"),
    )(a, b, into)


def _rows8(rows, width):
    idx = lax.broadcasted_iota(jnp.int32, (SUB, width), 0)
    out = jnp.zeros((SUB, width), F32)
    for r, v in enumerate(rows):
        out = jnp.where(idx == r, v, out)
    return out


def _rms_fwd(x, g, width, col_blk, name):
    s = x.shape[0]
    tr = _pick(s, ROW_TILE, 16)

    def body(x_ref, g_ref, u_ref):
        xv = x_ref[...]
        r = lax.rsqrt(jnp.mean(xv * xv, axis=-1, keepdims=True) + NORM_EPS)
        u_ref[...] = ((xv * r) * g_ref[...]).astype(BF)

    return pl.pallas_call(
        body, name=name, grid=(s // tr,),
        in_specs=[pl.BlockSpec((tr, width), lambda i: (i, col_blk)),
                  pl.BlockSpec((1, width), lambda i: (0, 0))],
        out_specs=pl.BlockSpec((tr, width), lambda i: (i, 0)),
        out_shape=jax.ShapeDtypeStruct((s, width), BF),
        compiler_params=_cp("parallel"),
    )(x, g)


def _rms_bwd(x, du, g, width, col_blk, name, extra=None, out_dtype=F32, also_bf16=False):
    s = x.shape[0]
    tr = _pick(s, ROW_TILE_BWD, 16)

    def body(*refs):
        x_ref, du_ref, g_ref = refs[:3]
        e_ref = refs[3] if extra is not None else None
        dx_ref = refs[3 + (extra is not None)]
        dxb_ref = refs[4 + (extra is not None)] if also_bf16 else None
        dg_ref = refs[-1]
        i = pl.program_id(0)
        xv = x_ref[...]
        duv = du_ref[...].astype(F32)
        r = lax.rsqrt(jnp.mean(xv * xv, axis=-1, keepdims=True) + NORM_EPS)
        nv = xv * r
        dn = duv * g_ref[...]
        dx = r * (dn - nv * jnp.mean(dn * nv, axis=-1, keepdims=True))
        if extra is not None:
            dx = dx + e_ref[...]
        dx_ref[...] = dx.astype(out_dtype)
        if also_bf16:
            dxb_ref[...] = dx.astype(BF)

        @pl.when(i == 0)
        def _():
            dg_ref[...] = jnp.zeros_like(dg_ref)

        dg_ref[...] += _rows8([jnp.sum(duv * nv, axis=0, keepdims=True)], width)

    in_specs = [pl.BlockSpec((tr, width), lambda i: (i, col_blk)),
                pl.BlockSpec((tr, width), lambda i: (i, 0)),
                pl.BlockSpec((1, width), lambda i: (0, 0))]
    args = [x, du, g]
    if extra is not None:
        in_specs.append(pl.BlockSpec((tr, width), lambda i: (i, 0)))
        args.append(extra)
    return pl.pallas_call(
        body, name=name, grid=(s // tr,),
        in_specs=in_specs,
        out_specs=[pl.BlockSpec((tr, width), lambda i: (i, 0))] * (1 + also_bf16)
        + [pl.BlockSpec((SUB, width), lambda i: (0, 0))],
        out_shape=[jax.ShapeDtypeStruct((s, width), out_dtype)] + [jax.ShapeDtypeStruct((s, width), BF)] * also_bf16
        + [jax.ShapeDtypeStruct((SUB, width), F32)],
        compiler_params=_cp("arbitrary"),
    )(*args)


def _down(cur, prev8, k):
    ext = jnp.concatenate([prev8, cur], axis=0)
    return pltpu.roll(ext, k, axis=0)[SUB:]


def _up(cur, next8, k):
    ext = jnp.concatenate([cur, next8], axis=0)
    return pltpu.roll(ext, ext.shape[0] - k, axis=0)[:cur.shape[0]]


def _lags(cur, prev8):
    return _down(cur, prev8, 1), _down(cur, prev8, 2)


def _conv3(w_ref, cur, prev8, lags=None):
    lag1, lag2 = _lags(cur, prev8) if lags is None else lags
    return w_ref[0:1, :] * lag2 + w_ref[1:2, :] * lag1 + w_ref[2:3, :] * cur


def _conv3_t(w_ref, cur, next8):
    return w_ref[2:3, :] * cur + w_ref[1:2, :] * _up(cur, next8, 1) + w_ref[0:1, :] * _up(cur, next8, 2)


def _spec_cur(tr, tc, c0):
    return pl.BlockSpec((tr, tc), lambda j, i: (i, c0 + j))


def _spec_prev(tr, tc, c0):
    return pl.BlockSpec((SUB, tc), lambda j, i: (jnp.maximum(i * (tr // SUB) - 1, 0), c0 + j))


def _spec_next(tr, tc, c0, s):
    return pl.BlockSpec((SUB, tc), lambda j, i: (jnp.minimum((i + 1) * (tr // SUB), s // SUB - 1), c0 + j))


def _spec_w(tc, c0):
    return pl.BlockSpec((SUB, tc), lambda j, i: (0, c0 + j))


def _pad8(w):
    return jnp.pad(w, ((0, SUB - w.shape[0]), (0, 0)))


def _conv_mix_fwd(z_a, cw8, conv):
    s = z_a.shape[0]
    tr = _pick(s, ROW_TILE, 16)
    tc = _pick(conv, COL_TILE, LANE)
    nc = conv // tc

    def body(zb_ref, zc_ref, zv_ref, zcp_ref, zvp_ref, w_ref, p_ref):
        i = pl.program_id(1)
        cv = zc_ref[...] * zv_ref[...]
        cvp = jnp.where(i > 0, zcp_ref[...] * zvp_ref[...], 0.0)
        p_ref[...] = (zb_ref[...] * _conv3(w_ref, cv, cvp)).astype(BF)

    return pl.pallas_call(
        body, name="conv_mix_fwd", grid=(nc, s // tr),
        in_specs=[_spec_cur(tr, tc, 0), _spec_cur(tr, tc, nc), _spec_cur(tr, tc, 2 * nc),
                  _spec_prev(tr, tc, nc), _spec_prev(tr, tc, 2 * nc), _spec_w(tc, 0)],
        out_specs=_spec_cur(tr, tc, 0),
        out_shape=jax.ShapeDtypeStruct((s, conv), BF),
        compiler_params=_cp("parallel", "parallel"),
    )(z_a, z_a, z_a, z_a, z_a, cw8)


def _conv_mix_bwd(z_a, d_p, cw8, conv):
    s = z_a.shape[0]
    tr = _pick(s, ROW_TILE_BWD, 16)
    tc = _pick(conv, COL_TILE, LANE)
    nc = conv // tc
    nr = s // tr

    def body(zb_ref, zbn_ref, zc_ref, zcp_ref, zv_ref, zvp_ref, dp_ref, dpn_ref, w_ref,
             dzb_ref, dzc_ref, dzv_ref, dw_ref):
        i = pl.program_id(1)
        zc = zc_ref[...]
        zv = zv_ref[...]
        cv = zc * zv
        cvp = jnp.where(i > 0, zcp_ref[...] * zvp_ref[...], 0.0)
        cv1, cv2 = _lags(cv, cvp)
        dpv = dp_ref[...]
        dzb_ref[...] = (dpv * _conv3(w_ref, cv, cvp, (cv1, cv2))).astype(BF)
        dcc = dpv * zb_ref[...]
        dccn = jnp.where(i < nr - 1, dpn_ref[...] * zbn_ref[...], 0.0)
        dcv = _conv3_t(w_ref, dcc, dccn)
        dzc_ref[...] = (dcv * zv).astype(BF)
        dzv_ref[...] = (dcv * zc).astype(BF)

        @pl.when(i == 0)
        def _():
            dw_ref[...] = jnp.zeros_like(dw_ref)

        dw_ref[...] += _rows8([jnp.sum(dcc * cv2, axis=0, keepdims=True),
                               jnp.sum(dcc * cv1, axis=0, keepdims=True),
                               jnp.sum(dcc * cv, axis=0, keepdims=True)], tc)

    out = jax.ShapeDtypeStruct((s, conv), BF)
    return pl.pallas_call(
        body, name="conv_mix_bwd", grid=(nc, nr),
        in_specs=[_spec_cur(tr, tc, 0), _spec_next(tr, tc, 0, s),
                  _spec_cur(tr, tc, nc), _spec_prev(tr, tc, nc),
                  _spec_cur(tr, tc, 2 * nc), _spec_prev(tr, tc, 2 * nc),
                  _spec_cur(tr, tc, 0), _spec_next(tr, tc, 0, s), _spec_w(tc, 0)],
        out_specs=[_spec_cur(tr, tc, 0), _spec_cur(tr, tc, 0), _spec_cur(tr, tc, 0), _spec_w(tc, 0)],
        out_shape=[out, out, out, jax.ShapeDtypeStruct((SUB, conv), F32)],
        compiler_params=_cp("parallel", "arbitrary"),
    )(z_a, z_a, z_a, z_a, z_a, z_a, d_p, d_p, cw8)


def _silu_parts(ag):
    sg = jax.nn.sigmoid(ag)
    return ag * sg, sg


def _ffn_up_act(u2, w_up, cw8, cb, dff):
    s, d = u2.shape
    nb, _, nbw = w_up.shape
    half = nb // 2
    assert half * nbw == dff
    tm = _pick(s, ROW_TILE, 16)

    def body(u_ref, wg_ref, wu_ref, cg_ref, cu_ref, bg_ref, bu_ref, ag_ref, au_ref, f_ref, hist_g, hist_u):
        i = pl.program_id(1)

        @pl.when(i == 0)
        def _():
            hist_g[...] = jnp.zeros_like(hist_g)
            hist_u[...] = jnp.zeros_like(hist_u)

        u = u_ref[...]
        xg = jnp.dot(u, wg_ref[...], preferred_element_type=F32)
        xu = jnp.dot(u, wu_ref[...], preferred_element_type=F32)
        ag_ref[...] = xg
        au_ref[...] = xu
        ag = _conv3(cg_ref, xg, hist_g[...]) + bg_ref[...]
        au = _conv3(cu_ref, xu, hist_u[...]) + bu_ref[...]
        f_ref[...] = (_silu_parts(ag)[0] * au).astype(BF)
        hist_g[...] = xg[tm - SUB:]
        hist_u[...] = xu[tm - SUB:]

    once = pl.Buffered(1)
    tile = pl.BlockSpec((tm, nbw), lambda j, i: (i, j))
    return pl.pallas_call(
        body, name="mm_ffn_up_act", grid=(half, s // tm),
        in_specs=[pl.BlockSpec((tm, d), lambda j, i: (i, 0)),
                  pl.BlockSpec((None, d, nbw), lambda j, i: (j, 0, 0), pipeline_mode=once),
                  pl.BlockSpec((None, d, nbw), lambda j, i: (half + j, 0, 0), pipeline_mode=once),
                  pl.BlockSpec((SUB, nbw), lambda j, i: (0, j)), pl.BlockSpec((SUB, nbw), lambda j, i: (0, half + j)),
                  pl.BlockSpec((1, nbw), lambda j, i: (0, j)), pl.BlockSpec((1, nbw), lambda j, i: (0, half + j))],
        out_specs=[tile, tile, tile],
        out_shape=[jax.ShapeDtypeStruct((s, dff), F32), jax.ShapeDtypeStruct((s, dff), F32),
                   jax.ShapeDtypeStruct((s, dff), BF)],
        scratch_shapes=[pltpu.VMEM((SUB, nbw), F32), pltpu.VMEM((SUB, nbw), F32)],
        compiler_params=_cp("arbitrary", "arbitrary"),
    )(u2, w_up, w_up, cw8, cw8, cb, cb)


def _ffn_act_bwd(a_g, a_u, d_f, cw8, cb, dff):
    s = a_g.shape[0]
    tr = _pick(s, ROW_TILE_BWD, 16)
    tc = _pick(dff, COL_TILE, LANE)
    nc = dff // tc
    nr = s // tr

    def body(xg_ref, xgp_ref, xgn_ref, xu_ref, xup_ref, xun_ref, df_ref, dfn_ref,
             wg_ref, wu_ref, bg_ref, bu_ref, dxg_ref, dxu_ref, dwg_ref, dwu_ref):
        i = pl.program_id(1)
        xg = xg_ref[...]
        xu = xu_ref[...]
        xgp = jnp.where(i > 0, xgp_ref[...], 0.0)
        xup = jnp.where(i > 0, xup_ref[...], 0.0)

        def d_act(xg_t, xgp_t, xu_t, xup_t, df_t, lags_g=None, lags_u=None):
            ag = _conv3(wg_ref, xg_t, xgp_t, lags_g) + bg_ref[...]
            au = _conv3(wu_ref, xu_t, xup_t, lags_u) + bu_ref[...]
            sil, sg = _silu_parts(ag)
            return df_t * au * (sg * (1.0 + ag * (1.0 - sg))), df_t * sil

        lags_g = _lags(xg, xgp)
        lags_u = _lags(xu, xup)
        dag, dau = d_act(xg, xgp, xu, xup, df_ref[...], lags_g, lags_u)
        dfn = jnp.where(i < nr - 1, dfn_ref[...], 0.0)
        dagn, daun = d_act(xgn_ref[...], xg[tr - SUB:], xun_ref[...], xu[tr - SUB:], dfn)
        dxg_ref[...] = _conv3_t(wg_ref, dag, dagn).astype(BF)
        dxu_ref[...] = _conv3_t(wu_ref, dau, daun).astype(BF)

        @pl.when(i == 0)
        def _():
            dwg_ref[...] = jnp.zeros_like(dwg_ref)
            dwu_ref[...] = jnp.zeros_like(dwu_ref)

        def wgrad(da, x, lags):
            return _rows8([jnp.sum(da * lags[1], axis=0, keepdims=True),
                           jnp.sum(da * lags[0], axis=0, keepdims=True),
                           jnp.sum(da * x, axis=0, keepdims=True),
                           jnp.sum(da, axis=0, keepdims=True)], tc)

        dwg_ref[...] += wgrad(dag, xg, lags_g)
        dwu_ref[...] += wgrad(dau, xu, lags_u)

    half = jax.ShapeDtypeStruct((s, dff), BF)
    wsh = jax.ShapeDtypeStruct((SUB, dff), F32)
    return pl.pallas_call(
        body, name="ffn_act_bwd", grid=(nc, nr),
        in_specs=[_spec_cur(tr, tc, 0), _spec_prev(tr, tc, 0), _spec_next(tr, tc, 0, s),
                  _spec_cur(tr, tc, 0), _spec_prev(tr, tc, 0), _spec_next(tr, tc, 0, s),
                  _spec_cur(tr, tc, 0), _spec_next(tr, tc, 0, s),
                  _spec_w(tc, 0), _spec_w(tc, nc),
                  pl.BlockSpec((1, tc), lambda j, i: (0, j)), pl.BlockSpec((1, tc), lambda j, i: (0, nc + j))],
        out_specs=[_spec_cur(tr, tc, 0), _spec_cur(tr, tc, 0), _spec_w(tc, 0), _spec_w(tc, 0)],
        out_shape=[half, half, wsh, wsh],
        compiler_params=_cp("parallel", "arbitrary"),
    )(a_g, a_g, a_g, a_u, a_u, a_u, d_f, d_f, cw8, cw8, cb, cb)


def _mla_out_gate(o, w_mo, z_g, b_gate, yc):
    m, k = o.shape
    d = w_mo.shape[2]
    tm = _pick(m, MM_TM, 16)
    tn = _pick(d, MM_TM, LANE)
    nc = d // tn

    def body(a_ref, b_ref, za_ref, zb_ref, ba_ref, bb_ref, yc_ref, ym_ref, mix_ref):
        ym = jnp.dot(a_ref[...], b_ref[0], preferred_element_type=F32)
        ga = jax.nn.sigmoid(za_ref[...] + ba_ref[...])
        gb = jax.nn.sigmoid(zb_ref[...] + bb_ref[...])
        ym_ref[...] = ym.astype(BF)
        mix_ref[...] = (ga * yc_ref[...] + gb * ym).astype(BF)

    tile = pl.BlockSpec((tm, tn), lambda i, j: (i, j))
    out = jax.ShapeDtypeStruct((m, d), BF)
    return pl.pallas_call(
        body, name="mm_y_mla_gate", grid=(m // tm, nc),
        in_specs=[pl.BlockSpec((tm, k), lambda i, j: (i, 0)), pl.BlockSpec((1, k, tn), lambda i, j: (0, 0, j)),
                  tile, pl.BlockSpec((tm, tn), lambda i, j: (i, nc + j)),
                  pl.BlockSpec((1, tn), lambda i, j: (0, j)), pl.BlockSpec((1, tn), lambda i, j: (0, nc + j)), tile],
        out_specs=[tile, tile], out_shape=[out, out],
        compiler_params=_cp("parallel", "parallel"),
    )(o, w_mo, z_g, z_g, b_gate, b_gate, yc)


def _d_mix_gate(d_h1, w_oo, z_g, b_gate, yc, ym):
    m, n = d_h1.shape
    d = w_oo.shape[1]
    tm = _pick(m, MM_TM, 16)
    tn = _pick(d, COL_TILE, LANE)
    nc = d // tn

    def body(a_ref, b_ref, za_ref, zb_ref, ba_ref, bb_ref, yc_ref, ym_ref,
             dza_ref, dzb_ref, dyc_ref, dym_ref, dba_ref, dbb_ref):
        i = pl.program_id(1)
        dm = lax.dot_general(a_ref[...], b_ref[0], (((1,), (1,)), ((), ())), preferred_element_type=F32)
        ga = jax.nn.sigmoid(za_ref[...] + ba_ref[...])
        gb = jax.nn.sigmoid(zb_ref[...] + bb_ref[...])
        dza = dm * yc_ref[...] * (ga * (1.0 - ga))
        dzb = dm * ym_ref[...] * (gb * (1.0 - gb))
        dza_ref[...] = dza.astype(BF)
        dzb_ref[...] = dzb.astype(BF)
        dyc_ref[...] = (dm * ga).astype(BF)
        dym_ref[...] = (dm * gb).astype(BF)

        @pl.when(i == 0)
        def _():
            dba_ref[...] = jnp.zeros_like(dba_ref)
            dbb_ref[...] = jnp.zeros_like(dbb_ref)

        dba_ref[...] += _rows8([jnp.sum(dza, axis=0, keepdims=True)], tn)
        dbb_ref[...] += _rows8([jnp.sum(dzb, axis=0, keepdims=True)], tn)

    tile = pl.BlockSpec((tm, tn), lambda j, i: (i, j))
    act = jax.ShapeDtypeStruct((m, d), BF)
    bsh = jax.ShapeDtypeStruct((SUB, d), F32)
    return pl.pallas_call(
        body, name="mm_d_mix_gate", grid=(nc, m // tm),
        in_specs=[pl.BlockSpec((tm, n), lambda j, i: (i, 0)), pl.BlockSpec((1, tn, n), lambda j, i: (0, j, 0)),
                  tile, pl.BlockSpec((tm, tn), lambda j, i: (i, nc + j)),
                  pl.BlockSpec((1, tn), lambda j, i: (0, j)), pl.BlockSpec((1, tn), lambda j, i: (0, nc + j)),
                  tile, tile],
        out_specs=[tile] * 4 + [pl.BlockSpec((SUB, tn), lambda j, i: (0, j))] * 2,
        out_shape=[act, act, act, act, bsh, bsh],
        compiler_params=_cp("parallel", "arbitrary"),
    )(d_h1, w_oo, z_g, z_g, b_gate, b_gate, yc, ym)


def _lay(v):
    z = jnp.zeros(v.shape[:-1] + (HALF,), v.dtype)
    return jnp.concatenate([v[..., :HALF], z, v[..., HALF:], z], axis=-1)


def _unlay(v):
    return jnp.concatenate([v[..., :HALF], v[..., 2 * HALF:3 * HALF]], axis=-1)


def _lay_rows(v):
    z = jnp.zeros((HALF,) + v.shape[1:], v.dtype)
    return jnp.concatenate([v[:HALF], z, v[HALF:], z], axis=0)


def _rope_tables(positions):
    s = positions.shape[0]
    tr = _pick(s, ROW_TILE, 8)
    inv_freq = ROPE_THETA ** (-jnp.arange(0, ROPE, 2, dtype=F32) / ROPE)
    consts = jnp.stack([_lay(jnp.concatenate([inv_freq, inv_freq])),
                        _lay(jnp.ones((ROPE,), F32)),
                        _lay(jnp.concatenate([-jnp.ones((HALF,), F32), jnp.ones((HALF,), F32)]))])
    consts = _pad8(consts)

    def body(p_ref, c_ref, cos_ref, sin_ref):
        ang = p_ref[...].astype(F32) * c_ref[0:1, :]
        cos_ref[...] = jnp.cos(ang) * c_ref[1:2, :]
        sin_ref[...] = jnp.sin(ang) * c_ref[2:3, :]

    tab = jax.ShapeDtypeStruct((s, LANE), F32)
    return pl.pallas_call(
        body, name="rope_tables", grid=(s // tr,),
        in_specs=[pl.BlockSpec((tr, 1), lambda i: (i, 0)), pl.BlockSpec((SUB, LANE), lambda i: (0, 0))],
        out_specs=[pl.BlockSpec((tr, LANE), lambda i: (i, 0))] * 2,
        out_shape=[tab, tab],
        compiler_params=_cp("parallel"),
    )(positions, consts)


def _lane_sum(p):
    return jnp.sum(p, axis=-1, keepdims=True)


def _rope(t, cos, sin):
    return t * cos + pltpu.roll(t, 2 * HALF, axis=1) * sin


def _rope_t(d, cos, sin):
    return d * cos + pltpu.roll(d * sin, 2 * HALF, axis=1)


def _head_fwd(q_raw, kv_raw, z_a, kr_blk, cos, sin, gains, heads):
    s = q_raw.shape[0]
    tr = _pick(s, HEAD_ROW_TILE, 16)
    hw = heads * LANE

    def body(q_ref, kv_ref, kr_ref, cos_ref, sin_ref, g_ref, qo_ref, ko_ref, vo_ref):
        cosv = cos_ref[...]
        sinv = sin_ref[...]
        krv = kr_ref[...]
        kr_sq = krv * krv
        for h in range(heads):
            lo = h * LANE
            qn = q_ref[:, lo:lo + LANE]
            qr = q_ref[:, hw + lo:hw + lo + LANE]
            r = lax.rsqrt(_lane_sum(qn * qn + qr * qr) / HEAD_QK + NORM_EPS)
            qo_ref[:, 2 * lo:2 * lo + LANE] = (((qn * r) * g_ref[0:1, :]) * (QK_SCALE * LOG2_E)).astype(BF)
            qo_ref[:, 2 * lo + LANE:2 * lo + 2 * LANE] = (
                _rope((qr * r) * g_ref[1:2, :], cosv, sinv) * (QK_SCALE * LOG2_E)).astype(BF)
            kn = kv_ref[:, 2 * lo:2 * lo + LANE]
            r = lax.rsqrt(_lane_sum(kn * kn + kr_sq) / HEAD_QK + NORM_EPS)
            ko_ref[:, 2 * lo:2 * lo + LANE] = ((kn * r) * g_ref[2:3, :]).astype(BF)
            ko_ref[:, 2 * lo + LANE:2 * lo + 2 * LANE] = _rope((krv * r) * g_ref[3:4, :], cosv, sinv).astype(BF)
            vo_ref[:, lo:lo + LANE] = kv_ref[:, 2 * lo + LANE:2 * lo + 2 * LANE].astype(BF)

    row = lambda w: pl.BlockSpec((tr, w), lambda i: (i, 0))
    return pl.pallas_call(
        body, name="head_fwd", grid=(s // tr,),
        in_specs=[row(2 * hw), row(2 * hw), pl.BlockSpec((tr, LANE), lambda i: (i, kr_blk)),
                  row(LANE), row(LANE), pl.BlockSpec((SUB, LANE), lambda i: (0, 0))],
        out_specs=[row(2 * hw), row(2 * hw), row(hw)],
        out_shape=[jax.ShapeDtypeStruct((s, 2 * hw), BF), jax.ShapeDtypeStruct((s, 2 * hw), BF),
                   jax.ShapeDtypeStruct((s, hw), BF)],
        compiler_params=_cp("parallel"),
    )(q_raw, kv_raw, z_a, cos, sin, gains)


def _head_bwd(q_raw, kv_raw, z_a, kr_blk, cos, sin, gains, dq_att, dk_att, dv, heads):
    s = q_raw.shape[0]
    tr = _pick(s, HEAD_ROW_TILE_BWD, 16)
    hw = heads * LANE

    def body(q_ref, kv_ref, kr_ref, cos_ref, sin_ref, g_ref, dq_ref, dk_ref, dv_ref,
             dqr_ref, dkv_ref, dkr_ref, dg_ref):
        i = pl.program_id(0)
        cosv = cos_ref[...]
        sinv = sin_ref[...]
        krv = kr_ref[...]
        kr_sq = krv * krv
        dkr = jnp.zeros((tr, LANE), F32)
        dgs = [jnp.zeros((1, LANE), F32) for _ in range(4)]

        def norm_bwd(xn, xr, sq, dn_out, dr_out, gn, gr):
            r = lax.rsqrt(_lane_sum(sq) / HEAD_QK + NORM_EPS)
            nn = xn * r
            nr = xr * r
            dt = _rope_t(dr_out, cosv, sinv)
            dnn = dn_out * gn
            dnr = dt * gr
            mean = _lane_sum(dnn * nn + dnr * nr) / HEAD_QK
            return (r * (dnn - nn * mean), r * (dnr - nr * mean),
                    jnp.sum(dn_out * nn, axis=0, keepdims=True), jnp.sum(dt * nr, axis=0, keepdims=True))

        for h in range(heads):
            lo = h * LANE
            qn = q_ref[:, lo:lo + LANE]
            qr = q_ref[:, hw + lo:hw + lo + LANE]
            dxn, dxr, g0, g1 = norm_bwd(qn, qr, qn * qn + qr * qr, dq_ref[:, 2 * lo:2 * lo + LANE] * QK_SCALE,
                                        dq_ref[:, 2 * lo + LANE:2 * lo + 2 * LANE] * QK_SCALE,
                                        g_ref[0:1, :], g_ref[1:2, :])
            dqr_ref[:, lo:lo + LANE] = dxn.astype(BF)
            dqr_ref[:, hw + lo:hw + lo + LANE] = dxr.astype(BF)
            kn = kv_ref[:, 2 * lo:2 * lo + LANE]
            dxn, dxr, g2, g3 = norm_bwd(kn, krv, kn * kn + kr_sq, dk_ref[:, 2 * lo:2 * lo + LANE],
                                        dk_ref[:, 2 * lo + LANE:2 * lo + 2 * LANE], g_ref[2:3, :], g_ref[3:4, :])
            dkv_ref[:, 2 * lo:2 * lo + LANE] = dxn.astype(BF)
            dkv_ref[:, 2 * lo + LANE:2 * lo + 2 * LANE] = dv_ref[:, lo:lo + LANE].astype(BF)
            dkr = dkr + dxr
            dgs = [a + b for a, b in zip(dgs, (g0, g1, g2, g3))]
        dkr_ref[...] = dkr

        @pl.when(i == 0)
        def _():
            dg_ref[...] = jnp.zeros_like(dg_ref)

        dg_ref[...] += _rows8(dgs, LANE)

    row = lambda w: pl.BlockSpec((tr, w), lambda i: (i, 0))
    return pl.pallas_call(
        body, name="head_bwd", grid=(s // tr,),
        in_specs=[row(2 * hw), row(2 * hw), pl.BlockSpec((tr, LANE), lambda i: (i, kr_blk)),
                  row(LANE), row(LANE), pl.BlockSpec((SUB, LANE), lambda i: (0, 0)),
                  row(2 * hw), row(2 * hw), row(hw)],
        out_specs=[row(2 * hw), row(2 * hw), row(LANE), pl.BlockSpec((SUB, LANE), lambda i: (0, 0))],
        out_shape=[jax.ShapeDtypeStruct((s, 2 * hw), BF), jax.ShapeDtypeStruct((s, 2 * hw), BF),
                   jax.ShapeDtypeStruct((s, LANE), F32), jax.ShapeDtypeStruct((SUB, LANE), F32)],
        compiler_params=_cp("arbitrary"),
    )(q_raw, kv_raw, z_a, cos, sin, gains, dq_att, dk_att, dv)


def _causal_mask(nrows, ncols, row0):
    rows = lax.broadcasted_iota(jnp.int32, (nrows, ncols), 0) + row0
    cols = lax.broadcasted_iota(jnp.int32, (nrows, ncols), 1)
    return cols <= rows


def _causal_steps(nt, q_major):
    pairs = ([(i, j) for i in range(nt) for j in range(i + 1)] if q_major
             else [(i, j) for j in range(nt) for i in range(j, nt)])
    return (jnp.array([p[0] for p in pairs], jnp.int32), jnp.array([p[1] for p in pairs], jnp.int32))


def _attn_fwd(q_att, k_att, v, heads):
    s = q_att.shape[0]
    t = _pick(s, ATTN_TILE_FWD, LANE)
    nt = s // t
    th = t // 2
    qi, kj = _causal_steps(nt, True)

    def body(qi_ref, kj_ref, q_ref, k_ref, v_ref, o_ref, ob_ref, lse_ref, m_s, l_s, acc_s):
        st = pl.program_id(1)
        i = qi_ref[st]
        j = kj_ref[st]

        @pl.when(j == 0)
        def _():
            m_s[...] = jnp.full_like(m_s, NEG_INF)
            l_s[...] = jnp.zeros_like(l_s)
            acc_s[...] = jnp.zeros_like(acc_s)

        def update(rows, ncol, masked):
            sc = lax.dot_general(q_ref[rows, :], k_ref[0:ncol, :], (((1,), (1,)), ((), ())),
                                 preferred_element_type=F32)
            if masked:
                sc = jnp.where(_causal_mask(rows.stop - rows.start, ncol, rows.start), sc, NEG_INF)
            m_prev = m_s[rows, :]
            m_new = jnp.maximum(m_prev, jnp.max(sc, axis=-1, keepdims=True))
            alpha = jnp.exp2(m_prev - m_new)
            p = jnp.exp2(sc - jnp.tile(m_new, (1, ncol // LANE)))
            l_s[rows, :] = alpha * l_s[rows, :] + jnp.sum(p, axis=-1, keepdims=True)
            acc_s[rows, :] = alpha * acc_s[rows, :] + jnp.dot(p.astype(BF), v_ref[0:ncol, :],
                                                              preferred_element_type=F32)
            m_s[rows, :] = m_new

        @pl.when(j < i)
        def _():
            update(slice(0, t), t, False)

        @pl.when(j == i)
        def _():
            update(slice(0, th), th, True)
            update(slice(th, t), t, True)
            o = acc_s[...] / l_s[...]
            o_ref[...] = o
            ob_ref[...] = o.astype(BF)
            lse_ref[...] = (m_s[...] + jnp.log2(l_s[...]))[:, 0:1]

    q_idx = lambda h, st, qi_r, kj_r: (qi_r[st], h)
    kv_idx = lambda h, st, qi_r, kj_r: (kj_r[st], h)
    return pl.pallas_call(
        body, name="attn_fwd",
        grid_spec=pltpu.PrefetchScalarGridSpec(
            num_scalar_prefetch=2, grid=(heads, qi.shape[0]),
            in_specs=[pl.BlockSpec((t, 2 * LANE), q_idx), pl.BlockSpec((t, 2 * LANE), kv_idx),
                      pl.BlockSpec((t, LANE), kv_idx)],
            out_specs=[pl.BlockSpec((t, LANE), q_idx), pl.BlockSpec((t, LANE), q_idx),
                       pl.BlockSpec((None, t, 1), lambda h, st, qi_r, kj_r: (h, qi_r[st], 0))],
            scratch_shapes=[pltpu.VMEM((t, LANE), F32), pltpu.VMEM((t, LANE), F32), pltpu.VMEM((t, LANE), F32)]),
        out_shape=[jax.ShapeDtypeStruct((s, heads * LANE), F32), jax.ShapeDtypeStruct((s, heads * LANE), BF),
                   jax.ShapeDtypeStruct((heads, s, 1), F32)],
        compiler_params=_cp("parallel", "arbitrary"),
    )(qi, kj, q_att, k_att, v)


def _attn_bwd(q_att, k_att, v, o, lse, d_o, heads, dep=None):
    s = q_att.shape[0]
    t = _pick(s, ATTN_TILE, LANE)
    nt = s // t
    th = t // 2
    qi, kj = _causal_steps(nt, False)

    def body(qi_ref, kj_ref, q_ref, k_ref, v_ref, do_ref, o_ref, lse_ref, *rest):
        dq_ref, dk_ref, dv_ref, dk_s, dv_s = rest[-5:]
        st = pl.program_id(1)
        i = qi_ref[st]
        j = kj_ref[st]

        @pl.when(st == 0)
        def _():
            dq_ref[...] = jnp.zeros_like(dq_ref)

        @pl.when(i == j)
        def _():
            dk_s[...] = jnp.zeros_like(dk_s)
            dv_s[...] = jnp.zeros_like(dv_s)

        def update(rows, ncol, masked):
            nrow = rows.stop - rows.start
            q = q_ref[rows, :]
            k = k_ref[0:ncol, :]
            do = do_ref[rows, :]
            sc = lax.dot_general(q, k, (((1,), (1,)), ((), ())), preferred_element_type=F32)
            if masked:
                sc = jnp.where(_causal_mask(nrow, ncol, rows.start), sc, NEG_INF)
            p = jnp.exp2(sc - lse_ref[rows, :])
            dp = lax.dot_general(do, v_ref[0:ncol, :], (((1,), (1,)), ((), ())), preferred_element_type=F32)
            delta = jnp.sum(do.astype(F32) * o_ref[rows, :], axis=-1, keepdims=True)
            ds = (p * (dp - delta)).astype(BF)
            dv_s[0:ncol, :] += lax.dot_general(p.astype(BF), do, (((0,), (0,)), ((), ())),
                                               preferred_element_type=F32)
            dk_s[0:ncol, :] += lax.dot_general(ds, q, (((0,), (0,)), ((), ())), preferred_element_type=F32)
            out_rows = pl.ds(pl.multiple_of(i * t + rows.start, nrow), nrow)
            dq_ref[out_rows, :] += jnp.dot(ds, k, preferred_element_type=F32)

        @pl.when(i > j)
        def _():
            update(slice(0, t), t, False)

        @pl.when(i == j)
        def _():
            update(slice(0, th), th, True)
            update(slice(th, t), t, True)

        @pl.when(i == nt - 1)
        def _():
            dk_ref[...] = (dk_s[...] * (1.0 / LOG2_E)).astype(BF)
            dv_ref[...] = dv_s[...].astype(BF)

    q_idx = lambda h, st, qi_r, kj_r: (qi_r[st], h)
    kv_idx = lambda h, st, qi_r, kj_r: (kj_r[st], h)
    in_specs = [pl.BlockSpec((t, 2 * LANE), q_idx), pl.BlockSpec((t, 2 * LANE), kv_idx),
                pl.BlockSpec((t, LANE), kv_idx), pl.BlockSpec((t, LANE), q_idx), pl.BlockSpec((t, LANE), q_idx),
                pl.BlockSpec((None, t, 1), lambda h, st, qi_r, kj_r: (h, qi_r[st], 0))]
    args = [q_att, k_att, v, d_o, o, lse]
    if dep is not None:
        in_specs.append(ANY)
        args.append(dep)
    return pl.pallas_call(
        body, name="attn_bwd",
        grid_spec=pltpu.PrefetchScalarGridSpec(
            num_scalar_prefetch=2, grid=(heads, qi.shape[0]),
            in_specs=in_specs,
            out_specs=[pl.BlockSpec((s, 2 * LANE), lambda h, st, qi_r, kj_r: (0, h)),
                       pl.BlockSpec((t, 2 * LANE), kv_idx), pl.BlockSpec((t, LANE), kv_idx)],
            scratch_shapes=[pltpu.VMEM((t, 2 * LANE), F32), pltpu.VMEM((t, LANE), F32)]),
        out_shape=[jax.ShapeDtypeStruct((s, heads * 2 * LANE), F32),
                   jax.ShapeDtypeStruct((s, heads * 2 * LANE), BF),
                   jax.ShapeDtypeStruct((s, heads * LANE), BF)],
        compiler_params=_cp("parallel", "arbitrary"),
    )(qi, kj, *args)


def _sum_parts(parts, name):
    n, r, c = parts.shape
    tr = _pick(r, 512, 8)

    def body(p_ref, o_ref):
        g = p_ref[0].astype(F32)
        for k in range(1, n):
            g = g + p_ref[k].astype(F32)
        o_ref[...] = g

    return pl.pallas_call(
        body, name=name, grid=(r // tr,),
        in_specs=[pl.BlockSpec((n, tr, c), lambda i: (0, i, 0))],
        out_specs=pl.BlockSpec((tr, c), lambda i: (i, 0)),
        out_shape=jax.ShapeDtypeStruct((r, c), F32),
        compiler_params=_cp("parallel"),
    )(parts)


def _adamw(parts, w, m, v, name, by_cols=False):
    n, rp, c = parts.shape
    r = w.shape[0]
    assert by_cols or rp == r
    tr, tc = (r, _pick(c, 256, LANE)) if by_cols else (_pick(r, 256, 16 if r % 16 == 0 else 8), c)

    def body(p_ref, w_ref, m_ref, v_ref, g_ref, d_ref, mo_ref, vo_ref):
        g = p_ref[0].astype(F32)
        for k in range(1, n):
            g = g + p_ref[k].astype(F32)
        g = g[:r] if by_cols else g
        m_new = ADAM_B1 * m_ref[...] + (1.0 - ADAM_B1) * g
        v_new = ADAM_B2 * v_ref[...] + (1.0 - ADAM_B2) * jnp.square(g)
        m_hat = m_new / (1.0 - ADAM_B1 ** ADAM_STEP)
        v_hat = v_new / (1.0 - ADAM_B2 ** ADAM_STEP)
        g_ref[...] = g
        d_ref[...] = -ADAM_LR * (m_hat / (jnp.sqrt(v_hat) + ADAM_EPS) + ADAM_WD * w_ref[...])
        mo_ref[...] = m_new
        vo_ref[...] = v_new

    idx = (lambda i: (0, i)) if by_cols else (lambda i: (i, 0))
    spec = pl.BlockSpec((tr, tc), idx)
    sh = jax.ShapeDtypeStruct((r, c), F32)
    return pl.pallas_call(
        body, name=name, grid=(c // tc if by_cols else r // tr,),
        in_specs=[pl.BlockSpec((n, rp if by_cols else tr, tc), lambda i: (0,) + idx(i)), spec, spec, spec],
        out_specs=[spec] * 4, out_shape=[sh] * 4,
        compiler_params=_cp("parallel"),
    )(parts, w, m, v)


def _place():
    x, y, c = lax.axis_index("x"), lax.axis_index("y"), lax.axis_index("c")
    chips = [(1 - x, y), (x, 1 - y), (1 - x, 1 - y)]
    return x, y, c, chips


def _all_gather(shards, name, dep=None):
    n = len(shards)
    deps = [] if dep is None else list(dep)

    def body(*refs):
        ins, outs = refs[:n], refs[n + len(deps):2 * n + len(deps)]
        send_sems, recv_sems, local_sems = refs[2 * n + len(deps):]
        x, y, c, chips = _place()
        me, sibling = (x, y, c), (x, y, 1 - c)

        def slot(w, p):
            return outs[w].at[4 * p[0] + 2 * p[1] + p[2]]

        def copy(w, k, block, to, src=None):
            return pltpu.make_async_remote_copy(
                src_ref=slot(w, block) if src is None else src, dst_ref=slot(w, block),
                send_sem=send_sems.at[w, k], recv_sem=recv_sems.at[w, k], device_id=to, device_id_type=MESH)

        first = []
        for w in range(n):
            first += [copy(w, 1 + j, me, (*chip, c), src=ins[w]) for j, chip in enumerate(chips)]
            first.append(copy(w, 0, me, sibling, src=ins[w]))
        for cp in first:
            cp.start()
        mine = [pltpu.make_async_copy(ins[w], slot(w, me), local_sems.at[w]) for w in range(n)]
        for cp in mine:
            cp.start()
        passed = []
        for w in range(n):
            for j, chip in enumerate(chips):
                copy(w, 1 + j, (*chip, c), me).wait_recv()
                cp = copy(w, 4 + j, (*chip, c), sibling)
                cp.start()
                passed.append(cp)
        for w in range(n):
            copy(w, 0, sibling, me).wait_recv()
            for j, chip in enumerate(chips):
                copy(w, 4 + j, (*chip, 1 - c), me).wait_recv()
        for cp in first + passed:
            cp.wait_send()
        for cp in mine:
            cp.wait()

    return pl.pallas_call(
        body, name=name,
        in_specs=[ANY] * (n + len(deps)), out_specs=[ANY] * n,
        out_shape=[jax.ShapeDtypeStruct((N_DEV,) + a.shape, a.dtype) for a in shards],
        scratch_shapes=[pltpu.SemaphoreType.DMA((n, 7)), pltpu.SemaphoreType.DMA((n, 7)),
                        pltpu.SemaphoreType.DMA((n,))],
    )(*shards, *deps)


HBM = pl.BlockSpec(memory_space=pltpu.HBM)
SEM = pl.BlockSpec(memory_space=pltpu.SEMAPHORE)
EFFECT = pltpu.SideEffectType.DATAFLOW_SIDE_EFFECTING
PEERS = [(dx, dy, dc) for dx in (1, 0) for dy in (1, 0) for dc in (0, 1) if (dx, dy, dc) != (0, 0, 0)]


def _peer(x, y, c, flip):
    dx, dy, dc = flip
    return (1 - x if dx else x, 1 - y if dy else y, 1 - c if dc else c)


def _exchange_copies(srcs, lands, send, recv, loc, gather):
    x, y, c, _ = _place()
    me = 4 * x + 2 * y + c
    remote, local = [], []
    for w in range(len(srcs)):
        for k, flip in enumerate(PEERS):
            px, py, pc = _peer(x, y, c, flip)
            src = srcs[w] if gather else srcs[w].at[4 * px + 2 * py + pc]
            remote.append(pltpu.make_async_remote_copy(
                src_ref=src, dst_ref=lands[w].at[me], send_sem=send[w].at[k], recv_sem=recv[w].at[k],
                device_id=(px, py, pc), device_id_type=MESH))
        local.append(pltpu.make_async_copy(srcs[w] if gather else srcs[w].at[me], lands[w].at[me], loc[w]))
    return remote, local


class _Exchange:
    def __init__(self, srcs, lands, send, recv, loc, token, gather):
        self.srcs, self.lands, self.send, self.recv, self.loc = srcs, lands, send, recv, loc
        self.token, self.gather = token, gather


def _exchange_start(srcs, gather, name, dep=None):
    n = len(srcs)
    deps = [] if dep is None else [dep]
    land_shapes = [((N_DEV,) + a.shape) if gather else a.shape for a in srcs]
    lands = [pltpu.with_memory_space_constraint(lax.empty(sh, a.dtype), pltpu.HBM) for sh, a in zip(land_shapes, srcs)]
    srcs = [pltpu.with_memory_space_constraint(a, pltpu.HBM) for a in srcs]

    def body(*refs):
        src_refs, land_refs = refs[:n], refs[n:2 * n]
        outs = refs[2 * n + len(deps):]
        send, recv, loc = outs[:n], outs[n:2 * n], outs[2 * n:3 * n]
        token = outs[-1]
        remote, local = _exchange_copies(src_refs, land_refs, send, recv, loc, gather)
        for cp in remote + local:
            cp.start()
        token[...] = jnp.zeros_like(token)

    out_shape = ([pltpu.SemaphoreType.DMA((len(PEERS),))] * (2 * n) + [pltpu.SemaphoreType.DMA(())] * n
                 + [pltpu.HBM(a.shape, a.dtype) for a in srcs] + [pltpu.HBM(a.shape, a.dtype) for a in lands]
                 + [jax.ShapeDtypeStruct((SUB, LANE), F32)])
    res = pl.pallas_call(
        body, name=name, out_shape=out_shape,
        in_specs=[HBM] * (2 * n) + [ANY] * len(deps),
        out_specs=[SEM] * (3 * n) + [HBM] * (2 * n) + [pl.BlockSpec(memory_space=pltpu.VMEM)],
        input_output_aliases={i: 3 * n + i for i in range(2 * n)},
        compiler_params=pltpu.CompilerParams(has_side_effects=EFFECT),
    )(*srcs, *lands, *deps)
    return _Exchange(res[3 * n:4 * n], res[4 * n:5 * n], res[:n], res[n:2 * n], res[2 * n:3 * n], res[-1], gather)


def _exchange_wait(ex, idxs, after, name):
    n = len(idxs)
    srcs = [ex.srcs[i] for i in idxs]
    lands = [ex.lands[i] for i in idxs]
    sems = [ex.send[i] for i in idxs] + [ex.recv[i] for i in idxs] + [ex.loc[i] for i in idxs]
    gather = ex.gather

    def body(*refs):
        src_refs, land_refs = refs[:n], refs[n:2 * n]
        send, recv, loc = refs[2 * n:3 * n], refs[3 * n:4 * n], refs[4 * n:5 * n]
        remote, local = _exchange_copies(src_refs, land_refs, send, recv, loc, gather)
        for cp in remote:
            cp.wait_send()
            cp.wait_recv()
        for cp in local:
            cp.wait()

    res = pl.pallas_call(
        body, name=name,
        out_shape=[pltpu.HBM(a.shape, a.dtype) for a in srcs] + [pltpu.HBM(a.shape, a.dtype) for a in lands],
        in_specs=[HBM] * (2 * n) + [SEM] * (3 * n) + [ANY],
        out_specs=[HBM] * (2 * n),
        input_output_aliases={i: i for i in range(2 * n)},
        compiler_params=pltpu.CompilerParams(has_side_effects=EFFECT),
    )(*srcs, *lands, *sems, after)
    return res[n:]


def _gather2_copies(srcs, lands, send, recv_ici, recv_sib, loc):
    x, y, c, chips = _place()
    me = 4 * x + 2 * y + c
    remote, local = [], []
    for w in range(len(srcs)):
        remote.append(pltpu.make_async_remote_copy(
            src_ref=srcs[w], dst_ref=lands[w].at[me], send_sem=send[w].at[0], recv_sem=recv_sib[w],
            device_id=(x, y, 1 - c), device_id_type=MESH))
        for j, chip in enumerate(chips):
            remote.append(pltpu.make_async_remote_copy(
                src_ref=srcs[w], dst_ref=lands[w].at[me], send_sem=send[w].at[1 + j], recv_sem=recv_ici[w].at[j],
                device_id=(*chip, c), device_id_type=MESH))
        local.append(pltpu.make_async_copy(srcs[w], lands[w].at[me], loc[w]))
    return remote, local


def _gather2_forwards(lands, fsend, frecv, arrived=None):
    x, y, c, chips = _place()
    cps = []
    for w in range(len(lands)):
        for j, chip in enumerate(chips):
            slot = lands[w].at[4 * chip[0] + 2 * chip[1] + c]
            cp = pltpu.make_async_remote_copy(
                src_ref=slot, dst_ref=slot, send_sem=fsend[w].at[j], recv_sem=frecv[w].at[j],
                device_id=(x, y, 1 - c), device_id_type=MESH)
            if arrived is not None:
                pltpu.make_async_remote_copy(
                    src_ref=slot, dst_ref=slot, send_sem=fsend[w].at[j], recv_sem=arrived[w].at[j],
                    device_id=(x, y, 1 - c), device_id_type=MESH).wait_recv()
            cps.append(cp)
    return cps


def _gather2(shards, between, name):
    n = len(shards)
    srcs = [pltpu.with_memory_space_constraint(a, pltpu.HBM) for a in shards]
    lands = [pltpu.with_memory_space_constraint(lax.empty((N_DEV,) + a.shape, a.dtype), pltpu.HBM) for a in shards]
    hbm_like = lambda arrs: [pltpu.HBM(a.shape, a.dtype) for a in arrs]
    tok = jax.ShapeDtypeStruct((SUB, LANE), F32)
    vmem = pl.BlockSpec(memory_space=pltpu.VMEM)
    side = pltpu.CompilerParams(has_side_effects=EFFECT)

    def start(*refs):
        src_refs, land_refs = refs[:n], refs[n:2 * n]
        outs = refs[2 * n:]
        send, recv_ici, recv_sib, loc = outs[:n], outs[n:2 * n], outs[2 * n:3 * n], outs[3 * n:4 * n]
        remote, local = _gather2_copies(src_refs, land_refs, send, recv_ici, recv_sib, loc)
        for cp in remote + local:
            cp.start()
        outs[-1][...] = jnp.zeros((SUB, LANE), F32)

    res = pl.pallas_call(
        start, name=name + "_start",
        out_shape=([pltpu.SemaphoreType.DMA((4,))] * n + [pltpu.SemaphoreType.DMA((3,))] * n
                   + [pltpu.SemaphoreType.DMA(())] * (2 * n) + hbm_like(srcs) + hbm_like(lands) + [tok]),
        in_specs=[HBM] * (2 * n), out_specs=[SEM] * (4 * n) + [HBM] * (2 * n) + [vmem],
        input_output_aliases={i: 4 * n + i for i in range(2 * n)}, compiler_params=side,
    )(*srcs, *lands)
    send, recv_ici, recv_sib, loc = res[:n], res[n:2 * n], res[2 * n:3 * n], res[3 * n:4 * n]
    srcs, lands, token = res[4 * n:5 * n], res[5 * n:6 * n], res[-1]

    done = between(token)
    after = jax.tree_util.tree_leaves(done)

    def forward(*refs):
        land_refs, arrived = refs[:n], refs[n:2 * n]
        outs = refs[2 * n + len(after):]
        fsend, frecv = outs[:n], outs[n:2 * n]
        for cp in _gather2_forwards(land_refs, fsend, frecv, arrived):
            cp.start()
        outs[-1][...] = jnp.zeros((SUB, LANE), F32)

    res = pl.pallas_call(
        forward, name=name + "_forward",
        out_shape=[pltpu.SemaphoreType.DMA((3,))] * (2 * n) + hbm_like(lands) + [tok],
        in_specs=[HBM] * n + [SEM] * n + [ANY] * len(after), out_specs=[SEM] * (2 * n) + [HBM] * n + [vmem],
        input_output_aliases={i: 2 * n + i for i in range(n)}, compiler_params=side,
    )(*lands, *recv_ici, *after)
    fsend, frecv, lands, token = res[:n], res[n:2 * n], res[2 * n:3 * n], res[-1]

    def wait(*refs):
        src_refs, land_refs = refs[:n], refs[n:2 * n]
        sems = refs[2 * n:7 * n]
        send, recv_sib, loc, fsend, frecv = (sems[k * n:(k + 1) * n] for k in range(5))
        remote, local = _gather2_copies(src_refs, land_refs, send, send, recv_sib, loc)
        for w in range(n):
            for cp in remote[4 * w:4 * w + 4]:
                cp.wait_send()
            remote[4 * w].wait_recv()
        for cp in local:
            cp.wait()
        for cp in _gather2_forwards(land_refs, fsend, frecv):
            cp.wait_send()
            cp.wait_recv()

    res = pl.pallas_call(
        wait, name=name + "_wait", out_shape=hbm_like(srcs) + hbm_like(lands),
        in_specs=[HBM] * (2 * n) + [SEM] * (5 * n) + [ANY], out_specs=[HBM] * (2 * n),
        input_output_aliases={i: i for i in range(2 * n)}, compiler_params=side,
    )(*srcs, *lands, *send, *recv_sib, *loc, *fsend, *frecv, token)
    return res[n:], done


def _after(token, a):
    return a + token[0:1, 0:1].astype(a.dtype)


def _unblock(w3):
    nb, k, nbw = w3.shape
    return w3.transpose(1, 0, 2).reshape(k, nb * nbw)


def _block(w, nb):
    k, n = w.shape
    return w.reshape(k, nb, n // nb).transpose(1, 0, 2)


def kernel(x, positions, ln1_g, w_in, b_gate, conv_w, w_conv_out, q_a_g, w_q_b, kv_a_g, w_kv_b, q_norm_g, k_norm_g, w_mla_out, w_o, ln2_g, w_ffn_up, ffn_conv_w, ffn_conv_b, w_ffn_down, loss_target, m_ln1_g, m_w_in, m_b_gate, m_conv_w, m_w_conv_out, m_q_a_g, m_w_q_b, m_kv_a_g, m_w_kv_b, m_q_norm_g, m_k_norm_g, m_w_mla_out, m_w_o, m_ln2_g, m_w_ffn_up, m_ffn_conv_w, m_ffn_conv_b, m_w_ffn_down, v_ln1_g, v_w_in, v_b_gate, v_conv_w, v_w_conv_out, v_q_a_g, v_w_q_b, v_kv_a_g, v_w_kv_b, v_q_norm_g, v_k_norm_g, v_w_mla_out, v_w_o, v_ln2_g, v_w_ffn_up, v_ffn_conv_w, v_ffn_conv_b, v_w_ffn_down):
    s, d = x.shape[1], x.shape[2]
    conv = conv_w.shape[2] * N_DEV
    ql, kvl = q_a_g.shape[1], kv_a_g.shape[1]
    heads = w_q_b.shape[2] * N_DEV // HEAD_QK
    dff = w_ffn_down.shape[1] * N_DEV
    hw = heads * LANE
    conv3 = 3 * conv
    kr_off = conv3 + ql
    kv_off = -(-(kr_off + LANE) // kvl) * kvl
    wa = kv_off + kvl
    assert conv3 % ql == 0 and kr_off % LANE == 0
    xs = x[0]
    tgt = loss_target[0]
    pos = positions.reshape(s, 1)

    nin = w_in.shape[2]
    big = dict(w_in=w_in[0].T, w_conv_out=w_conv_out[0], w_q_b=w_q_b[0], w_kv_b=w_kv_b[0],
               w_mla_out=w_mla_out[0], w_o=w_o[0], w_ffn_up=w_ffn_up[0], w_ffn_down=w_ffn_down[0])
    names = list(big)
    rest = names[1:]
    early = {}

    def while_w_in_travels(token):
        early["ag"] = _exchange_start([big[k].astype(BF) for k in rest], True, "gather_rest_start", dep=token)
        cos_sin = _rope_tables(pos)
        return cos_sin, _rms_fwd(xs, _after(early["ag"].token, ln1_g), d, 0, "rms1_fwd")

    first, ((cos, sin), u1) = _gather2([big["w_in"].astype(BF), _pad8(conv_w[0]), _pad8(ffn_conv_w[0])],
                                       while_w_in_travels, "gather_w_in")
    ag = early["ag"]
    cw8 = _unblock(first[1])
    fcw8 = _unblock(first[2])

    def landed(keys, after, name):
        return _exchange_wait(ag, [rest.index(k) for k in keys], after, name)

    w_in_t = first[0].reshape(N_DEV * nin, d)
    g_off = kr_off + kvl + ROPE
    w_a_t = jnp.concatenate([w_in_t[:kr_off], _lay_rows(w_in_t[kr_off + kvl:g_off]),
                             jnp.zeros((kv_off - kr_off - LANE, d), BF), w_in_t[kr_off:kr_off + kvl]], axis=0)[None]
    w_g_t = w_in_t[g_off:][None]
    gains = _pad8(jnp.concatenate([q_norm_g[:, :NOPE], _lay(q_norm_g[:, NOPE:]),
                                   k_norm_g[:, :NOPE], _lay(k_norm_g[:, NOPE:])], axis=0))
    kr_blk = kr_off // LANE

    z_a = _mm_nt(u1, w_a_t, "mm_z_a")
    z_g = _mm_nt(u1, w_g_t, "mm_z_g", out_dtype=BF)
    p = _conv_mix_fwd(z_a, cw8, conv)
    w_co, w_qb, w_kv = landed(["w_conv_out", "w_q_b", "w_kv_b"], p, "gather_wait_mixers")
    w_co = _unblock(w_co)[None]
    w_kv = _unblock(w_kv)[None]
    wq_full = _unblock(w_qb).reshape(ql, heads, HEAD_QK)
    w_q = jnp.concatenate([wq_full[:, :, :NOPE].reshape(ql, hw), _lay(wq_full[:, :, NOPE:]).reshape(ql, hw)],
                          axis=1)[None]
    yc = _mm_nn(p, w_co, "mm_y_conv", out_dtype=BF)
    qn = _rms_fwd(z_a, q_a_g, ql, conv3 // ql, "rms_q_fwd")
    kvn = _rms_fwd(z_a, kv_a_g, kvl, kv_off // kvl, "rms_kv_fwd")
    q_raw = _mm_nn(qn, w_q, "mm_q")
    kv_raw = _mm_nn(kvn, w_kv, "mm_kv")
    q_att, k_att, v_bf = _head_fwd(q_raw, kv_raw, z_a, kr_blk, cos, sin, gains, heads)
    o, o_bf, lse = _attn_fwd(q_att, k_att, v_bf, heads)
    w_mo, w_oo = landed(["w_mla_out", "w_o"], lse, "gather_wait_outs")
    w_mo = w_mo.reshape(1, hw, d)
    w_oo = w_oo.reshape(1, d, d)
    ym, mix = _mla_out_gate(o_bf, w_mo, z_g, b_gate, yc)
    h1 = _mm_nn(mix, w_oo, "mm_h1", add=xs)
    u2 = _rms_fwd(h1, ln2_g, d, 0, "rms2_fwd")
    w_up, = landed(["w_ffn_up"], u2, "gather_wait_ffn_up")
    a_g, a_u, f = _ffn_up_act(u2, w_up, fcw8, ffn_conv_b, dff)
    w_dn, = landed(["w_ffn_down"], f, "gather_wait_ffn_down")
    w_dn = w_dn.reshape(1, dff, d)
    dy, dy_bf, loss_part = _mm_nn_loss(f, w_dn, h1, tgt, "mm_ffn_down_loss")

    g_dn = _mm_tn(f, dy_bf, 1, "mm_g_ffn_down").reshape(N_DEV, dff // N_DEV, d)
    rs_dn = _exchange_start([g_dn], False, "reduce_ffn_down_start")
    d_f = _mm_nt(dy_bf, w_dn, "mm_d_f", dep=rs_dn.token)
    d_xg, d_xu, dfw_g, dfw_u = _ffn_act_bwd(a_g, a_u, d_f, fcw8, ffn_conv_b, dff)
    half = N_DEV // 2
    g_up = _mm_tn(u2, d_xg, half, "mm_g_ffn_up_gate", into=lax.empty((N_DEV, d, 2 * dff // N_DEV), BF))
    g_up = _mm_tn(u2, d_xu, half, "mm_g_ffn_up_up", into=g_up, blk0=half)
    rs_up = _exchange_start([g_up], False, "reduce_ffn_up_start")
    d_u2 = _mm_nt([d_xg, d_xu], w_up, "mm_d_u2", out_dtype=BF, dep=rs_up.token)
    d_h1, d_h1_bf, dg_ln2 = _rms_bwd(h1, d_u2, ln2_g, d, 0, "rms2_bwd", extra=dy, also_bf16=True)
    g_oo = _mm_tn(mix, d_h1_bf, 1, "mm_g_w_o").reshape(N_DEV, d // N_DEV, d)
    d_zga, d_zgb, d_yc, d_ym, dba, dbb = _d_mix_gate(d_h1_bf, w_oo, z_g, b_gate, yc, ym)
    g_co = _block(_mm_tn(p, d_yc, 1, "mm_g_conv_out")[0], N_DEV)
    g_mo = _mm_tn(o_bf, d_ym, 1, "mm_g_mla_out").reshape(N_DEV, hw // N_DEV, d)
    rs_mix = _exchange_start([g_oo, g_co, g_mo], False, "reduce_mixers_start")
    d_p = _mm_nt(d_yc, w_co, "mm_d_p", dep=rs_mix.token)
    d_o = _mm_nt(d_ym, w_mo, "mm_d_o", out_dtype=BF)
    d_zb, d_zc, d_zv, dcw = _conv_mix_bwd(z_a, d_p, cw8, conv)
    dq_att, dk_att, dv = _attn_bwd(q_att, k_att, v_bf, o, lse, d_o, heads, dep=rs_mix.token)
    d_q_raw, d_kv_raw, d_kr, dgains = _head_bwd(q_raw, kv_raw, z_a, kr_blk, cos, sin, gains, dq_att, dk_att, dv, heads)
    g_q2 = _mm_tn(qn, d_q_raw, 1, "mm_g_q")[0]
    g_qb = _block(jnp.concatenate([g_q2[:, :hw].reshape(ql, heads, NOPE),
                                   _unlay(g_q2[:, hw:].reshape(ql, heads, LANE))], axis=2).reshape(ql, heads * HEAD_QK), N_DEV)
    g_kv = _block(_mm_tn(kvn, d_kv_raw, 1, "mm_g_kv")[0], N_DEV)
    rs_qkv = _exchange_start([g_qb, g_kv], False, "reduce_qkv_start")
    d_qn = _mm_nt(d_q_raw, w_q, "mm_d_qn", dep=rs_qkv.token)
    d_kvn = _mm_nt(d_kv_raw, w_kv, "mm_d_kvn")
    d_ql, dg_qa = _rms_bwd(z_a, d_qn, q_a_g, ql, conv3 // ql, "rms_q_bwd", out_dtype=BF)
    d_kvl, dg_kva = _rms_bwd(z_a, d_kvn, kv_a_g, kvl, kv_off // kvl, "rms_kv_bwd", out_dtype=BF)
    d_z_a = jnp.concatenate([d_zb, d_zc, d_zv, d_ql, d_kr.astype(BF), jnp.zeros((s, kv_off - kr_off - LANE), BF),
                             d_kvl], axis=1)
    g_a = _mm_tn(d_z_a, u1, 1, "mm_g_w_a")[0]
    g_ga = _mm_tn(d_zga, u1, 1, "mm_g_w_ga")[0]
    g_gb = _mm_tn(d_zgb, u1, 1, "mm_g_w_gb")[0]
    g_in = jnp.concatenate([g_a[:kr_off], g_a[kv_off:kv_off + kvl], g_a[kr_off:kr_off + HALF],
                            g_a[kr_off + 2 * HALF:kr_off + 3 * HALF], g_ga, g_gb], axis=0).reshape(N_DEV, nin, d)
    rs_in = _exchange_start([g_in], False, "reduce_w_in_start")
    d_u1 = _mm_nn(d_z_a, w_a_t, "mm_d_u1_a", dep=rs_in.token)
    d_u1 = _mm_nn([d_zga, d_zgb], w_g_t, "mm_d_u1_g", add=d_u1)
    grad_x, dg_ln1 = _rms_bwd(xs, d_u1, ln1_g, d, 0, "rms1_bwd", extra=d_h1)

    summed = {}
    summed["w_ffn_down"], = _exchange_wait(rs_dn, [0], grad_x, "reduce_ffn_down_wait")
    summed["w_ffn_up"], = _exchange_wait(rs_up, [0], grad_x, "reduce_ffn_up_wait")
    summed["w_o"], summed["w_conv_out"], summed["w_mla_out"] = _exchange_wait(rs_mix, [0, 1, 2], grad_x, "reduce_mixers_wait")
    summed["w_q_b"], summed["w_kv_b"] = _exchange_wait(rs_qkv, [0, 1], grad_x, "reduce_qkv_wait")
    loc = locals()
    out = {}
    for k in rest:
        out[k] = _adamw(summed[k], big[k], loc["m_" + k][0], loc["v_" + k][0], "adamw_" + k)

    small = dict(ln1_g=dg_ln1[0:1], b_gate=jnp.concatenate([dba[0:1], dbb[0:1]], axis=1), q_a_g=dg_qa[0:1],
                 kv_a_g=dg_kva[0:1],
                 q_norm_g=jnp.concatenate([dgains[0:1], _unlay(dgains[1:2])], axis=1),
                 k_norm_g=jnp.concatenate([dgains[2:3], _unlay(dgains[3:4])], axis=1),
                 ln2_g=dg_ln2[0:1], ffn_conv_b=jnp.concatenate([dfw_g[3:4], dfw_u[3:4]], axis=1))
    small_names = list(small)
    extra = [dcw[0:3].reshape(1, -1), jnp.concatenate([dfw_g[0:3], dfw_u[0:3]], axis=1).reshape(1, -1),
             loss_part[0:1, 0:1]]
    flat = jnp.concatenate([small[k] for k in small_names] + extra, axis=1)
    n_flat = flat.shape[1]
    rows = -(-n_flat // (SUB * LANE)) * SUB
    flat = jnp.pad(flat, ((0, 0), (0, rows * LANE - n_flat))).reshape(rows, LANE)
    total = _sum_parts(_all_gather([flat], "gather_small", dep=[out[k][0] for k in rest])[0], "sum_small").reshape(1, rows * LANE)
    off = 0
    small_g = {}
    for k in small_names:
        small_g[k] = total[:, off:off + small[k].shape[1]]
        off += small[k].shape[1]
    me = 4 * lax.axis_index("x") + 2 * lax.axis_index("y") + lax.axis_index("c")
    cwn, fcwn = conv // N_DEV, 2 * dff // N_DEV
    g_cw = lax.dynamic_slice_in_dim(total[:, off:off + 3 * conv].reshape(3, conv), me * cwn, cwn, axis=1)
    off += 3 * conv
    g_fcw = lax.dynamic_slice_in_dim(total[:, off:off + 6 * dff].reshape(3, 2 * dff), me * fcwn, fcwn, axis=1)
    off += 6 * dff
    loss = total[0, off]

    summed["w_in"], = _exchange_wait(rs_in, [0], total, "reduce_w_in_wait")
    out["w_in"] = [r.T for r in _adamw(summed["w_in"], big["w_in"], m_w_in[0].T, v_w_in[0].T, "adamw_w_in",
                                       by_cols=True)]
    small_w = dict(ln1_g=ln1_g, b_gate=b_gate, q_a_g=q_a_g, kv_a_g=kv_a_g, q_norm_g=q_norm_g, k_norm_g=k_norm_g,
                   ln2_g=ln2_g, ffn_conv_b=ffn_conv_b, conv_w=conv_w[0].reshape(1, -1),
                   ffn_conv_w=ffn_conv_w[0].reshape(1, -1))
    small_g["conv_w"] = g_cw.reshape(1, -1)
    small_g["ffn_conv_w"] = g_fcw.reshape(1, -1)
    packed_names = list(small_w)

    def pack(get):
        vflat = jnp.concatenate([get(k).reshape(1, -1) for k in packed_names], axis=1)
        nr = -(-vflat.shape[1] // (SUB * LANE)) * SUB
        return jnp.pad(vflat, ((0, 0), (0, nr * LANE - vflat.shape[1])), constant_values=1.0).reshape(nr, LANE)

    res = _adamw(pack(lambda k: small_g[k])[None], pack(lambda k: small_w[k]), pack(lambda k: loc["m_" + k]),
                 pack(lambda k: loc["v_" + k]), "adamw_small")
    res = [r.reshape(1, -1) for r in res]
    off = 0
    for k in packed_names:
        shape = loc[k].shape
        size = small_w[k].shape[1]
        out[k] = [r[:, off:off + size].reshape(shape) for r in res]
        off += size
    for k in names:
        out[k] = [r[None] for r in out[k]]

    order = ["ln1_g", "w_in", "b_gate", "conv_w", "w_conv_out", "q_a_g", "w_q_b", "kv_a_g", "w_kv_b", "q_norm_g",
             "k_norm_g", "w_mla_out", "w_o", "ln2_g", "w_ffn_up", "ffn_conv_w", "ffn_conv_b", "w_ffn_down"]
    return (loss, grad_x[None], *[out[k][0] for k in order], *[out[k][1] for k in order],
            *[out[k][2] for k in order], *[out[k][3] for k in order])
```

```python
import functools

import jax
import jax.numpy as jnp
from jax import lax
from jax.experimental import pallas as pl
from jax.experimental.pallas import tpu as pltpu

BF = jnp.bfloat16
F32 = jnp.float32
MESH = pl.DeviceIdType.MESH
N_DEV = 8

NOPE = 128
ROPE = 64
HALF = ROPE // 2
HEAD_QK = NOPE + ROPE
HEAD_V = 128
LANE = 128
SUB = 8
QK_SCALE = HEAD_QK ** -0.5
LOG2_E = 1.4426950408889634
NORM_EPS = 1e-6
NEG_INF = -1e30
ROPE_THETA = 10000.0
ADAM_LR = 0.001
ADAM_B1 = 0.9
ADAM_B2 = 0.999
ADAM_EPS = 1e-08
ADAM_WD = 0.01
ADAM_STEP = 10

VMEM_LIMIT = 52 * 1024 * 1024
MM_TM, MM_TN, MM_TK, MM_TS = 1024, 1536, 2048, 2048
ROW_TILE, ROW_TILE_BWD = 512, 256
HEAD_ROW_TILE, HEAD_ROW_TILE_BWD = 256, 256
COL_TILE = 512
ATTN_TILE = 1024
ATTN_TILE_FWD = 1024
ANY = pl.BlockSpec(memory_space=pl.ANY)


def _pick(n, target, mult):
    t = (min(n, target) // mult) * mult
    while t > 0:
        if n % t == 0:
            return t
        t -= mult
    raise ValueError(f"no tile for {n} (target {target}, multiple {mult})")


def _cp(*sem):
    return pltpu.CompilerParams(dimension_semantics=sem, vmem_limit_bytes=VMEM_LIMIT)


def _accumulate(kk, nk, acc, part, finish):
    if nk == 1:
        finish(part())
        return

    @pl.when(kk == 0)
    def _():
        acc[...] = part()

    @pl.when((kk > 0) & (kk < nk - 1))
    def _():
        acc[...] += part()

    @pl.when(kk == nk - 1)
    def _():
        finish(acc[...] + part())


def _mm_call(body, name, grid, in_specs, args, out_spec, out_shape, acc_shape, nk, dep):
    if dep is not None:
        in_specs = in_specs + [ANY]
        args = args + [dep]
    return pl.pallas_call(
        body, name=name, grid=grid, in_specs=in_specs, out_specs=out_spec, out_shape=out_shape,
        scratch_shapes=[pltpu.VMEM(acc_shape, F32)] if nk > 1 else [],
        compiler_params=_cp("parallel", "parallel", "arbitrary"),
    )(*args)


def _mm_nn_loss(a, b3, add, target, name):
    m, k = a.shape
    _, k2, n = b3.shape
    assert k == k2 and b3.shape[0] == 1
    tm = _pick(m, MM_TM, 16)
    tn = _pick(n, MM_TN, LANE)
    tk = _pick(k, MM_TK, LANE)
    nk = k // tk

    def body(a_ref, b_ref, c_ref, t_ref, dy_ref, dyb_ref, l_ref, acc):
        kk = pl.program_id(2)

        @pl.when((pl.program_id(0) == 0) & (pl.program_id(1) == 0) & (kk == 0))
        def _():
            l_ref[...] = jnp.zeros_like(l_ref)

        def part():
            return jnp.dot(a_ref[...].astype(BF), b_ref[0].astype(BF), preferred_element_type=F32)

        def finish(r):
            e = r + c_ref[...] - t_ref[...]
            dy_ref[...] = e / n
            dyb_ref[...] = (e / n).astype(BF)
            l_ref[...] += 0.5 * jnp.sum(jnp.sum(e * e, axis=-1, keepdims=True), axis=0, keepdims=True) / n

        _accumulate(kk, nk, acc, part, finish)

    tile = pl.BlockSpec((tm, tn), lambda i, j, kk: (i, j))
    return pl.pallas_call(
        body, name=name, grid=(m // tm, n // tn, nk),
        in_specs=[pl.BlockSpec((tm, tk), lambda i, j, kk: (i, kk)),
                  pl.BlockSpec((1, tk, tn), lambda i, j, kk: (0, kk, j)), tile, tile],
        out_specs=[tile, tile, pl.BlockSpec((SUB, LANE), lambda i, j, kk: (0, 0))],
        out_shape=[jax.ShapeDtypeStruct((m, n), F32), jax.ShapeDtypeStruct((m, n), BF),
                   jax.ShapeDtypeStruct((SUB, LANE), F32)],
        scratch_shapes=[pltpu.VMEM((tm, tn), F32)],
        compiler_params=_cp("arbitrary", "arbitrary", "arbitrary"),
    )(a, b3, add, target)


def _mm_nn(a, b3, name, add=None, out_dtype=F32, blk0=0, nblk=None, dep=None):
    pair = isinstance(a, (list, tuple))
    a_list = list(a) if pair else [a]
    m, ka = a_list[0].shape
    k = ka * len(a_list)
    nb_all, k2, nbw = b3.shape
    assert k == k2
    nblk = nb_all - blk0 if nblk is None else nblk
    n = nblk * nbw
    tm = _pick(m, MM_TM if k > MM_TM else 2 * MM_TM, 16)
    tn = _pick(nbw, MM_TN, LANE)
    tk = _pick(ka, MM_TK, LANE)
    per = nbw // tn
    nk = k // tk
    nka = ka // tk
    na_ops = len(a_list)

    def body(*refs):
        a_refs, b_ref = refs[:na_ops], refs[na_ops]
        c_ref = refs[na_ops + 1] if add is not None else None
        o_ref = refs[na_ops + 1 + (add is not None) + (dep is not None)]
        acc = refs[-1]
        kk = pl.program_id(2)

        def part():
            av = a_refs[0][...] if not pair else jnp.where(kk < nka, a_refs[0][...], a_refs[1][...])
            return jnp.dot(av.astype(BF), b_ref[...].astype(BF), preferred_element_type=F32)

        def finish(r):
            if add is not None:
                r = r + c_ref[...]
            o_ref[...] = r.astype(out_dtype)

        _accumulate(kk, nk, acc, part, finish)

    if pair:
        in_specs = [pl.BlockSpec((tm, tk), lambda i, j, kk: (i, jnp.minimum(kk, nka - 1))),
                    pl.BlockSpec((tm, tk), lambda i, j, kk: (i, jnp.maximum(kk - nka, 0)))]
    else:
        in_specs = [pl.BlockSpec((tm, tk), lambda i, j, kk: (i, kk))]
    in_specs.append(pl.BlockSpec((None, tk, tn), lambda i, j, kk: (blk0 + j // per, kk, j % per)))
    args = a_list + [b3]
    if add is not None:
        in_specs.append(pl.BlockSpec((tm, tn), lambda i, j, kk: (i, j)))
        args.append(add)
    return _mm_call(body, name, (m // tm, n // tn, nk), in_specs, args,
                    pl.BlockSpec((tm, tn), lambda i, j, kk: (i, j)), jax.ShapeDtypeStruct((m, n), out_dtype),
                    (tm, tn), nk, dep)


def _mm_nt(a, b3, name, add=None, out_dtype=F32, blk0=0, nblk=None, dep=None):
    pair = isinstance(a, (list, tuple))
    a_list = list(a) if pair else [a]
    m, na = a_list[0].shape
    n = na * len(a_list)
    nb_all, k, nbw = b3.shape
    nblk = nb_all - blk0 if nblk is None else nblk
    assert n == nblk * nbw and na % nbw == 0
    tm = _pick(m, 2 * MM_TM if k <= MM_TM and n <= MM_TK else MM_TM, 16)
    tk = _pick(nbw, MM_TK, LANE)
    per = nbw // tk
    nk = n // tk
    tn = _pick(k, MM_TN if nk <= 2 else 2 * MM_TM, LANE)
    nka = na // tk
    na_ops = len(a_list)

    def body(*refs):
        a_refs, b_ref = refs[:na_ops], refs[na_ops]
        c_ref = refs[na_ops + 1] if add is not None else None
        o_ref = refs[na_ops + 1 + (add is not None) + (dep is not None)]
        acc = refs[-1]
        kk = pl.program_id(2)

        def part():
            av = a_refs[0][...] if not pair else jnp.where(kk < nka, a_refs[0][...], a_refs[1][...])
            return lax.dot_general(av.astype(BF), b_ref[...].astype(BF),
                                   (((1,), (1,)), ((), ())), preferred_element_type=F32)

        def finish(r):
            if add is not None:
                r = r + c_ref[...]
            o_ref[...] = r.astype(out_dtype)

        _accumulate(kk, nk, acc, part, finish)

    if pair:
        in_specs = [pl.BlockSpec((tm, tk), lambda i, j, kk: (i, jnp.minimum(kk, nka - 1))),
                    pl.BlockSpec((tm, tk), lambda i, j, kk: (i, jnp.maximum(kk - nka, 0)))]
    else:
        in_specs = [pl.BlockSpec((tm, tk), lambda i, j, kk: (i, kk))]
    in_specs.append(pl.BlockSpec((None, tn, tk), lambda i, j, kk: (blk0 + kk // per, j, kk % per)))
    args = a_list + [b3]
    if add is not None:
        in_specs.append(pl.BlockSpec((tm, tn), lambda i, j, kk: (i, j)))
        args.append(add)
    return _mm_call(body, name, (m // tm, k // tn, nk), in_specs, args,
                    pl.BlockSpec((tm, tn), lambda i, j, kk: (i, j)), jax.ShapeDtypeStruct((m, k), out_dtype),
                    (tm, tn), nk, dep)


def _mm_tn(a, b, nblk, name, out_dtype=BF, dep=None, into=None, blk0=0):
    s, m = a.shape
    s2, n = b.shape
    assert s == s2 and n % nblk == 0 and (dep is None or into is None)
    nbw = n // nblk
    tm = _pick(m, MM_TN, LANE)
    tn = _pick(nbw, MM_TN, LANE)
    ts = _pick(s, MM_TS, LANE)
    per = nbw // tn
    ns = s // ts

    def body(*refs):
        a_ref, b_ref = refs[:2]
        o_ref = refs[2 + (dep is not None or into is not None)]
        acc = refs[-1]

        def part():
            return lax.dot_general(a_ref[...].astype(BF), b_ref[...].astype(BF),
                                   (((0,), (0,)), ((), ())), preferred_element_type=F32)

        def finish(r):
            o_ref[...] = r.astype(out_dtype)

        _accumulate(pl.program_id(2), ns, acc, part, finish)

    in_specs = [pl.BlockSpec((ts, tm), lambda i, j, ss: (ss, i)),
                pl.BlockSpec((ts, tn), lambda i, j, ss: (ss, j))]
    out_spec = pl.BlockSpec((None, tm, tn), lambda i, j, ss: (blk0 + j // per, i, j % per))
    if into is None:
        return _mm_call(body, name, (m // tm, n // tn, ns), in_specs, [a, b], out_spec,
                        jax.ShapeDtypeStruct((nblk, m, nbw), out_dtype), (tm, tn), ns, dep)
    assert into.shape[1:] == (m, nbw) and into.dtype == out_dtype
    return pl.pallas_call(
        body, name=name, grid=(m // tm, n // tn, ns), in_specs=in_specs + [ANY], out_specs=out_spec,
        out_shape=jax.ShapeDtypeStruct(into.shape, out_dtype), input_output_aliases={2: 0},
        scratch_shapes=[pltpu.VMEM((tm, tn), F32)] if ns > 1 else [],
        compiler_params=_cp("parallel", "parallel", "arbitrary"),
    )(a, b, into)


def _rows8(rows, width):
    idx = lax.broadcasted_iota(jnp.int32, (SUB, width), 0)
    out = jnp.zeros((SUB, width), F32)
    for r, v in enumerate(rows):
        out = jnp.where(idx == r, v, out)
    return out


def _rms_fwd(x, g, width, col_blk, name):
    s = x.shape[0]
    tr = _pick(s, ROW_TILE, 16)

    def body(x_ref, g_ref, u_ref):
        xv = x_ref[...]
        r = lax.rsqrt(jnp.mean(xv * xv, axis=-1, keepdims=True) + NORM_EPS)
        u_ref[...] = ((xv * r) * g_ref[...]).astype(BF)

    return pl.pallas_call(
        body, name=name, grid=(s // tr,),
        in_specs=[pl.BlockSpec((tr, width), lambda i: (i, col_blk)),
                  pl.BlockSpec((1, width), lambda i: (0, 0))],
        out_specs=pl.BlockSpec((tr, width), lambda i: (i, 0)),
        out_shape=jax.ShapeDtypeStruct((s, width), BF),
        compiler_params=_cp("parallel"),
    )(x, g)


def _rms_bwd(x, du, g, width, col_blk, name, extra=None, out_dtype=F32, also_bf16=False):
    s = x.shape[0]
    tr = _pick(s, ROW_TILE_BWD, 16)

    def body(*refs):
        x_ref, du_ref, g_ref = refs[:3]
        e_ref = refs[3] if extra is not None else None
        dx_ref = refs[3 + (extra is not None)]
        dxb_ref = refs[4 + (extra is not None)] if also_bf16 else None
        dg_ref = refs[-1]
        i = pl.program_id(0)
        xv = x_ref[...]
        duv = du_ref[...].astype(F32)
        r = lax.rsqrt(jnp.mean(xv * xv, axis=-1, keepdims=True) + NORM_EPS)
        nv = xv * r
        dn = duv * g_ref[...]
        dx = r * (dn - nv * jnp.mean(dn * nv, axis=-1, keepdims=True))
        if extra is not None:
            dx = dx + e_ref[...]
        dx_ref[...] = dx.astype(out_dtype)
        if also_bf16:
            dxb_ref[...] = dx.astype(BF)

        @pl.when(i == 0)
        def _():
            dg_ref[...] = jnp.zeros_like(dg_ref)

        dg_ref[...] += _rows8([jnp.sum(duv * nv, axis=0, keepdims=True)], width)

    in_specs = [pl.BlockSpec((tr, width), lambda i: (i, col_blk)),
                pl.BlockSpec((tr, width), lambda i: (i, 0)),
                pl.BlockSpec((1, width), lambda i: (0, 0))]
    args = [x, du, g]
    if extra is not None:
        in_specs.append(pl.BlockSpec((tr, width), lambda i: (i, 0)))
        args.append(extra)
    return pl.pallas_call(
        body, name=name, grid=(s // tr,),
        in_specs=in_specs,
        out_specs=[pl.BlockSpec((tr, width), lambda i: (i, 0))] * (1 + also_bf16)
        + [pl.BlockSpec((SUB, width), lambda i: (0, 0))],
        out_shape=[jax.ShapeDtypeStruct((s, width), out_dtype)] + [jax.ShapeDtypeStruct((s, width), BF)] * also_bf16
        + [jax.ShapeDtypeStruct((SUB, width), F32)],
        compiler_params=_cp("arbitrary"),
    )(*args)


def _down(cur, prev8, k):
    ext = jnp.concatenate([prev8, cur], axis=0)
    return pltpu.roll(ext, k, axis=0)[SUB:]


def _up(cur, next8, k):
    ext = jnp.concatenate([cur, next8], axis=0)
    return pltpu.roll(ext, ext.shape[0] - k, axis=0)[:cur.shape[0]]


def _lags(cur, prev8):
    return _down(cur, prev8, 1), _down(cur, prev8, 2)


def _conv3(w_ref, cur, prev8, lags=None):
    lag1, lag2 = _lags(cur, prev8) if lags is None else lags
    return w_ref[0:1, :] * lag2 + w_ref[1:2, :] * lag1 + w_ref[2:3, :] * cur


def _conv3_t(w_ref, cur, next8):
    return w_ref[2:3, :] * cur + w_ref[1:2, :] * _up(cur, next8, 1) + w_ref[0:1, :] * _up(cur, next8, 2)


def _spec_cur(tr, tc, c0):
    return pl.BlockSpec((tr, tc), lambda j, i: (i, c0 + j))


def _spec_prev(tr, tc, c0):
    return pl.BlockSpec((SUB, tc), lambda j, i: (jnp.maximum(i * (tr // SUB) - 1, 0), c0 + j))


def _spec_next(tr, tc, c0, s):
    return pl.BlockSpec((SUB, tc), lambda j, i: (jnp.minimum((i + 1) * (tr // SUB), s // SUB - 1), c0 + j))


def _spec_w(tc, c0):
    return pl.BlockSpec((SUB, tc), lambda j, i: (0, c0 + j))


def _pad8(w):
    return jnp.pad(w, ((0, SUB - w.shape[0]), (0, 0)))


def _conv_mix_fwd(z_a, cw8, conv):
    s = z_a.shape[0]
    tr = _pick(s, ROW_TILE, 16)
    tc = _pick(conv, COL_TILE, LANE)
    nc = conv // tc

    def body(zb_ref, zc_ref, zv_ref, zcp_ref, zvp_ref, w_ref, p_ref):
        i = pl.program_id(1)
        cv = zc_ref[...] * zv_ref[...]
        cvp = jnp.where(i > 0, zcp_ref[...] * zvp_ref[...], 0.0)
        p_ref[...] = (zb_ref[...] * _conv3(w_ref, cv, cvp)).astype(BF)

    return pl.pallas_call(
        body, name="conv_mix_fwd", grid=(nc, s // tr),
        in_specs=[_spec_cur(tr, tc, 0), _spec_cur(tr, tc, nc), _spec_cur(tr, tc, 2 * nc),
                  _spec_prev(tr, tc, nc), _spec_prev(tr, tc, 2 * nc), _spec_w(tc, 0)],
        out_specs=_spec_cur(tr, tc, 0),
        out_shape=jax.ShapeDtypeStruct((s, conv), BF),
        compiler_params=_cp("parallel", "parallel"),
    )(z_a, z_a, z_a, z_a, z_a, cw8)


def _conv_mix_bwd(z_a, d_p, cw8, conv):
    s = z_a.shape[0]
    tr = _pick(s, ROW_TILE_BWD, 16)
    tc = _pick(conv, COL_TILE, LANE)
    nc = conv // tc
    nr = s // tr

    def body(zb_ref, zbn_ref, zc_ref, zcp_ref, zv_ref, zvp_ref, dp_ref, dpn_ref, w_ref,
             dzb_ref, dzc_ref, dzv_ref, dw_ref):
        i = pl.program_id(1)
        zc = zc_ref[...]
        zv = zv_ref[...]
        cv = zc * zv
        cvp = jnp.where(i > 0, zcp_ref[...] * zvp_ref[...], 0.0)
        cv1, cv2 = _lags(cv, cvp)
        dpv = dp_ref[...]
        dzb_ref[...] = (dpv * _conv3(w_ref, cv, cvp, (cv1, cv2))).astype(BF)
        dcc = dpv * zb_ref[...]
        dccn = jnp.where(i < nr - 1, dpn_ref[...] * zbn_ref[...], 0.0)
        dcv = _conv3_t(w_ref, dcc, dccn)
        dzc_ref[...] = (dcv * zv).astype(BF)
        dzv_ref[...] = (dcv * zc).astype(BF)

        @pl.when(i == 0)
        def _():
            dw_ref[...] = jnp.zeros_like(dw_ref)

        dw_ref[...] += _rows8([jnp.sum(dcc * cv2, axis=0, keepdims=True),
                               jnp.sum(dcc * cv1, axis=0, keepdims=True),
                               jnp.sum(dcc * cv, axis=0, keepdims=True)], tc)

    out = jax.ShapeDtypeStruct((s, conv), BF)
    return pl.pallas_call(
        body, name="conv_mix_bwd", grid=(nc, nr),
        in_specs=[_spec_cur(tr, tc, 0), _spec_next(tr, tc, 0, s),
                  _spec_cur(tr, tc, nc), _spec_prev(tr, tc, nc),
                  _spec_cur(tr, tc, 2 * nc), _spec_prev(tr, tc, 2 * nc),
                  _spec_cur(tr, tc, 0), _spec_next(tr, tc, 0, s), _spec_w(tc, 0)],
        out_specs=[_spec_cur(tr, tc, 0), _spec_cur(tr, tc, 0), _spec_cur(tr, tc, 0), _spec_w(tc, 0)],
        out_shape=[out, out, out, jax.ShapeDtypeStruct((SUB, conv), F32)],
        compiler_params=_cp("parallel", "arbitrary"),
    )(z_a, z_a, z_a, z_a, z_a, z_a, d_p, d_p, cw8)


def _silu_parts(ag):
    sg = jax.nn.sigmoid(ag)
    return ag * sg, sg


def _ffn_up_act(u2, w_up, cw8, cb, dff):
    s, d = u2.shape
    nb, _, nbw = w_up.shape
    half = nb // 2
    assert half * nbw == dff
    tm = _pick(s, ROW_TILE, 16)

    def body(u_ref, wg_ref, wu_ref, cg_ref, cu_ref, bg_ref, bu_ref, ag_ref, au_ref, f_ref, hist_g, hist_u):
        i = pl.program_id(1)

        @pl.when(i == 0)
        def _():
            hist_g[...] = jnp.zeros_like(hist_g)
            hist_u[...] = jnp.zeros_like(hist_u)

        u = u_ref[...]
        xg = jnp.dot(u, wg_ref[...], preferred_element_type=F32)
        xu = jnp.dot(u, wu_ref[...], preferred_element_type=F32)
        ag_ref[...] = xg
        au_ref[...] = xu
        ag = _conv3(cg_ref, xg, hist_g[...]) + bg_ref[...]
        au = _conv3(cu_ref, xu, hist_u[...]) + bu_ref[...]
        f_ref[...] = (_silu_parts(ag)[0] * au).astype(BF)
        hist_g[...] = xg[tm - SUB:]
        hist_u[...] = xu[tm - SUB:]

    once = pl.Buffered(1)
    tile = pl.BlockSpec((tm, nbw), lambda j, i: (i, j))
    return pl.pallas_call(
        body, name="mm_ffn_up_act", grid=(half, s // tm),
        in_specs=[pl.BlockSpec((tm, d), lambda j, i: (i, 0)),
                  pl.BlockSpec((None, d, nbw), lambda j, i: (j, 0, 0), pipeline_mode=once),
                  pl.BlockSpec((None, d, nbw), lambda j, i: (half + j, 0, 0), pipeline_mode=once),
                  pl.BlockSpec((SUB, nbw), lambda j, i: (0, j)), pl.BlockSpec((SUB, nbw), lambda j, i: (0, half + j)),
                  pl.BlockSpec((1, nbw), lambda j, i: (0, j)), pl.BlockSpec((1, nbw), lambda j, i: (0, half + j))],
        out_specs=[tile, tile, tile],
        out_shape=[jax.ShapeDtypeStruct((s, dff), F32), jax.ShapeDtypeStruct((s, dff), F32),
                   jax.ShapeDtypeStruct((s, dff), BF)],
        scratch_shapes=[pltpu.VMEM((SUB, nbw), F32), pltpu.VMEM((SUB, nbw), F32)],
        compiler_params=_cp("arbitrary", "arbitrary"),
    )(u2, w_up, w_up, cw8, cw8, cb, cb)


def _ffn_act_bwd(a_g, a_u, d_f, cw8, cb, dff):
    s = a_g.shape[0]
    tr = _pick(s, ROW_TILE_BWD, 16)
    tc = _pick(dff, COL_TILE, LANE)
    nc = dff // tc
    nr = s // tr

    def body(xg_ref, xgp_ref, xgn_ref, xu_ref, xup_ref, xun_ref, df_ref, dfn_ref,
             wg_ref, wu_ref, bg_ref, bu_ref, dxg_ref, dxu_ref, dwg_ref, dwu_ref):
        i = pl.program_id(1)
        xg = xg_ref[...]
        xu = xu_ref[...]
        xgp = jnp.where(i > 0, xgp_ref[...], 0.0)
        xup = jnp.where(i > 0, xup_ref[...], 0.0)

        def d_act(xg_t, xgp_t, xu_t, xup_t, df_t, lags_g=None, lags_u=None):
            ag = _conv3(wg_ref, xg_t, xgp_t, lags_g) + bg_ref[...]
            au = _conv3(wu_ref, xu_t, xup_t, lags_u) + bu_ref[...]
            sil, sg = _silu_parts(ag)
            return df_t * au * (sg * (1.0 + ag * (1.0 - sg))), df_t * sil

        lags_g = _lags(xg, xgp)
        lags_u = _lags(xu, xup)
        dag, dau = d_act(xg, xgp, xu, xup, df_ref[...], lags_g, lags_u)
        dfn = jnp.where(i < nr - 1, dfn_ref[...], 0.0)
        dagn, daun = d_act(xgn_ref[...], xg[tr - SUB:], xun_ref[...], xu[tr - SUB:], dfn)
        dxg_ref[...] = _conv3_t(wg_ref, dag, dagn).astype(BF)
        dxu_ref[...] = _conv3_t(wu_ref, dau, daun).astype(BF)

        @pl.when(i == 0)
        def _():
            dwg_ref[...] = jnp.zeros_like(dwg_ref)
            dwu_ref[...] = jnp.zeros_like(dwu_ref)

        def wgrad(da, x, lags):
            return _rows8([jnp.sum(da * lags[1], axis=0, keepdims=True),
                           jnp.sum(da * lags[0], axis=0, keepdims=True),
                           jnp.sum(da * x, axis=0, keepdims=True),
                           jnp.sum(da, axis=0, keepdims=True)], tc)

        dwg_ref[...] += wgrad(dag, xg, lags_g)
        dwu_ref[...] += wgrad(dau, xu, lags_u)

    half = jax.ShapeDtypeStruct((s, dff), BF)
    wsh = jax.ShapeDtypeStruct((SUB, dff), F32)
    return pl.pallas_call(
        body, name="ffn_act_bwd", grid=(nc, nr),
        in_specs=[_spec_cur(tr, tc, 0), _spec_prev(tr, tc, 0), _spec_next(tr, tc, 0, s),
                  _spec_cur(tr, tc, 0), _spec_prev(tr, tc, 0), _spec_next(tr, tc, 0, s),
                  _spec_cur(tr, tc, 0), _spec_next(tr, tc, 0, s),
                  _spec_w(tc, 0), _spec_w(tc, nc),
                  pl.BlockSpec((1, tc), lambda j, i: (0, j)), pl.BlockSpec((1, tc), lambda j, i: (0, nc + j))],
        out_specs=[_spec_cur(tr, tc, 0), _spec_cur(tr, tc, 0), _spec_w(tc, 0), _spec_w(tc, 0)],
        out_shape=[half, half, wsh, wsh],
        compiler_params=_cp("parallel", "arbitrary"),
    )(a_g, a_g, a_g, a_u, a_u, a_u, d_f, d_f, cw8, cw8, cb, cb)


def _residual_norm(mix, w_oo, x, g):
    m, k = mix.shape
    d = w_oo.shape[2]
    tm = _pick(m, ROW_TILE, 16)

    def body(a_ref, b_ref, x_ref, g_ref, h_ref, u_ref):
        h = jnp.dot(a_ref[...], b_ref[0], preferred_element_type=F32) + x_ref[...]
        h_ref[...] = h
        r = lax.rsqrt(jnp.mean(h * h, axis=-1, keepdims=True) + NORM_EPS)
        u_ref[...] = ((h * r) * g_ref[...]).astype(BF)

    row = pl.BlockSpec((tm, d), lambda i: (i, 0))
    return pl.pallas_call(
        body, name="mm_h1_norm", grid=(m // tm,),
        in_specs=[pl.BlockSpec((tm, k), lambda i: (i, 0)),
                  pl.BlockSpec((1, k, d), lambda i: (0, 0, 0), pipeline_mode=pl.Buffered(1)),
                  row, pl.BlockSpec((1, d), lambda i: (0, 0))],
        out_specs=[row, row],
        out_shape=[jax.ShapeDtypeStruct((m, d), F32), jax.ShapeDtypeStruct((m, d), BF)],
        compiler_params=_cp("parallel"),
    )(mix, w_oo, x, g)


def _mla_out_gate(o, w_mo, z_g, b_gate, yc):
    m, k = o.shape
    d = w_mo.shape[2]
    tm = _pick(m, MM_TM, 16)
    tn = _pick(d, MM_TM, LANE)
    nc = d // tn

    def body(a_ref, b_ref, za_ref, zb_ref, ba_ref, bb_ref, yc_ref, ym_ref, mix_ref):
        ym = jnp.dot(a_ref[...], b_ref[0], preferred_element_type=F32)
        ga = jax.nn.sigmoid(za_ref[...] + ba_ref[...])
        gb = jax.nn.sigmoid(zb_ref[...] + bb_ref[...])
        ym_ref[...] = ym.astype(BF)
        mix_ref[...] = (ga * yc_ref[...] + gb * ym).astype(BF)

    tile = pl.BlockSpec((tm, tn), lambda i, j: (i, j))
    out = jax.ShapeDtypeStruct((m, d), BF)
    return pl.pallas_call(
        body, name="mm_y_mla_gate", grid=(m // tm, nc),
        in_specs=[pl.BlockSpec((tm, k), lambda i, j: (i, 0)), pl.BlockSpec((1, k, tn), lambda i, j: (0, 0, j)),
                  tile, pl.BlockSpec((tm, tn), lambda i, j: (i, nc + j)),
                  pl.BlockSpec((1, tn), lambda i, j: (0, j)), pl.BlockSpec((1, tn), lambda i, j: (0, nc + j)), tile],
        out_specs=[tile, tile], out_shape=[out, out],
        compiler_params=_cp("parallel", "parallel"),
    )(o, w_mo, z_g, z_g, b_gate, b_gate, yc)


def _d_mix_gate(d_h1, w_oo, z_g, b_gate, yc, ym):
    m, n = d_h1.shape
    d = w_oo.shape[1]
    tm = _pick(m, MM_TM, 16)
    tn = _pick(d, COL_TILE, LANE)
    nc = d // tn

    def body(a_ref, b_ref, za_ref, zb_ref, ba_ref, bb_ref, yc_ref, ym_ref,
             dza_ref, dzb_ref, dyc_ref, dym_ref, dba_ref, dbb_ref):
        i = pl.program_id(1)
        dm = lax.dot_general(a_ref[...], b_ref[0], (((1,), (1,)), ((), ())), preferred_element_type=F32)
        ga = jax.nn.sigmoid(za_ref[...] + ba_ref[...])
        gb = jax.nn.sigmoid(zb_ref[...] + bb_ref[...])
        dza = dm * yc_ref[...] * (ga * (1.0 - ga))
        dzb = dm * ym_ref[...] * (gb * (1.0 - gb))
        dza_ref[...] = dza.astype(BF)
        dzb_ref[...] = dzb.astype(BF)
        dyc_ref[...] = (dm * ga).astype(BF)
        dym_ref[...] = (dm * gb).astype(BF)

        @pl.when(i == 0)
        def _():
            dba_ref[...] = jnp.zeros_like(dba_ref)
            dbb_ref[...] = jnp.zeros_like(dbb_ref)

        dba_ref[...] += _rows8([jnp.sum(dza, axis=0, keepdims=True)], tn)
        dbb_ref[...] += _rows8([jnp.sum(dzb, axis=0, keepdims=True)], tn)

    tile = pl.BlockSpec((tm, tn), lambda j, i: (i, j))
    act = jax.ShapeDtypeStruct((m, d), BF)
    bsh = jax.ShapeDtypeStruct((SUB, d), F32)
    return pl.pallas_call(
        body, name="mm_d_mix_gate", grid=(nc, m // tm),
        in_specs=[pl.BlockSpec((tm, n), lambda j, i: (i, 0)), pl.BlockSpec((1, tn, n), lambda j, i: (0, j, 0)),
                  tile, pl.BlockSpec((tm, tn), lambda j, i: (i, nc + j)),
                  pl.BlockSpec((1, tn), lambda j, i: (0, j)), pl.BlockSpec((1, tn), lambda j, i: (0, nc + j)),
                  tile, tile],
        out_specs=[tile] * 4 + [pl.BlockSpec((SUB, tn), lambda j, i: (0, j))] * 2,
        out_shape=[act, act, act, act, bsh, bsh],
        compiler_params=_cp("parallel", "arbitrary"),
    )(d_h1, w_oo, z_g, z_g, b_gate, b_gate, yc, ym)


def _lay(v):
    z = jnp.zeros(v.shape[:-1] + (HALF,), v.dtype)
    return jnp.concatenate([v[..., :HALF], z, v[..., HALF:], z], axis=-1)


def _unlay(v):
    return jnp.concatenate([v[..., :HALF], v[..., 2 * HALF:3 * HALF]], axis=-1)


def _lay_rows(v):
    z = jnp.zeros((HALF,) + v.shape[1:], v.dtype)
    return jnp.concatenate([v[:HALF], z, v[HALF:], z], axis=0)


def _rope_tables(positions):
    s = positions.shape[0]
    tr = _pick(s, ROW_TILE, 8)
    inv_freq = ROPE_THETA ** (-jnp.arange(0, ROPE, 2, dtype=F32) / ROPE)
    consts = jnp.stack([_lay(jnp.concatenate([inv_freq, inv_freq])),
                        _lay(jnp.ones((ROPE,), F32)),
                        _lay(jnp.concatenate([-jnp.ones((HALF,), F32), jnp.ones((HALF,), F32)]))])
    consts = _pad8(consts)

    def body(p_ref, c_ref, cos_ref, sin_ref):
        ang = p_ref[...].astype(F32) * c_ref[0:1, :]
        cos_ref[...] = jnp.cos(ang) * c_ref[1:2, :]
        sin_ref[...] = jnp.sin(ang) * c_ref[2:3, :]

    tab = jax.ShapeDtypeStruct((s, LANE), F32)
    return pl.pallas_call(
        body, name="rope_tables", grid=(s // tr,),
        in_specs=[pl.BlockSpec((tr, 1), lambda i: (i, 0)), pl.BlockSpec((SUB, LANE), lambda i: (0, 0))],
        out_specs=[pl.BlockSpec((tr, LANE), lambda i: (i, 0))] * 2,
        out_shape=[tab, tab],
        compiler_params=_cp("parallel"),
    )(positions, consts)


def _lane_sum(p):
    return jnp.sum(p, axis=-1, keepdims=True)


def _rope(t, cos, sin):
    return t * cos + pltpu.roll(t, 2 * HALF, axis=1) * sin


def _rope_t(d, cos, sin):
    return d * cos + pltpu.roll(d * sin, 2 * HALF, axis=1)


def _head_fwd(q_raw, kv_raw, z_a, kr_blk, cos, sin, gains, heads):
    s = q_raw.shape[0]
    tr = _pick(s, HEAD_ROW_TILE, 16)
    hw = heads * LANE

    def body(q_ref, kv_ref, kr_ref, cos_ref, sin_ref, g_ref, qo_ref, ko_ref, vo_ref):
        cosv = cos_ref[...]
        sinv = sin_ref[...]
        krv = kr_ref[...]
        kr_sq = krv * krv
        for h in range(heads):
            lo = h * LANE
            qn = q_ref[:, lo:lo + LANE]
            qr = q_ref[:, hw + lo:hw + lo + LANE]
            r = lax.rsqrt(_lane_sum(qn * qn + qr * qr) / HEAD_QK + NORM_EPS)
            qo_ref[:, 2 * lo:2 * lo + LANE] = (((qn * r) * g_ref[0:1, :]) * (QK_SCALE * LOG2_E)).astype(BF)
            qo_ref[:, 2 * lo + LANE:2 * lo + 2 * LANE] = (
                _rope((qr * r) * g_ref[1:2, :], cosv, sinv) * (QK_SCALE * LOG2_E)).astype(BF)
            kn = kv_ref[:, 2 * lo:2 * lo + LANE]
            r = lax.rsqrt(_lane_sum(kn * kn + kr_sq) / HEAD_QK + NORM_EPS)
            ko_ref[:, 2 * lo:2 * lo + LANE] = ((kn * r) * g_ref[2:3, :]).astype(BF)
            ko_ref[:, 2 * lo + LANE:2 * lo + 2 * LANE] = _rope((krv * r) * g_ref[3:4, :], cosv, sinv).astype(BF)
            vo_ref[:, lo:lo + LANE] = kv_ref[:, 2 * lo + LANE:2 * lo + 2 * LANE].astype(BF)

    row = lambda w: pl.BlockSpec((tr, w), lambda i: (i, 0))
    return pl.pallas_call(
        body, name="head_fwd", grid=(s // tr,),
        in_specs=[row(2 * hw), row(2 * hw), pl.BlockSpec((tr, LANE), lambda i: (i, kr_blk)),
                  row(LANE), row(LANE), pl.BlockSpec((SUB, LANE), lambda i: (0, 0))],
        out_specs=[row(2 * hw), row(2 * hw), row(hw)],
        out_shape=[jax.ShapeDtypeStruct((s, 2 * hw), BF), jax.ShapeDtypeStruct((s, 2 * hw), BF),
                   jax.ShapeDtypeStruct((s, hw), BF)],
        compiler_params=_cp("parallel"),
    )(q_raw, kv_raw, z_a, cos, sin, gains)


def _head_bwd(q_raw, kv_raw, z_a, kr_blk, cos, sin, gains, dq_att, dk_att, dv, heads):
    s = q_raw.shape[0]
    tr = _pick(s, HEAD_ROW_TILE_BWD, 16)
    hw = heads * LANE

    def body(q_ref, kv_ref, kr_ref, cos_ref, sin_ref, g_ref, dq_ref, dk_ref, dv_ref,
             dqr_ref, dkv_ref, dkr_ref, dg_ref):
        i = pl.program_id(0)
        cosv = cos_ref[...]
        sinv = sin_ref[...]
        krv = kr_ref[...]
        kr_sq = krv * krv
        dkr = jnp.zeros((tr, LANE), F32)
        dgs = [jnp.zeros((1, LANE), F32) for _ in range(4)]

        def norm_bwd(xn, xr, sq, dn_out, dr_out, gn, gr):
            r = lax.rsqrt(_lane_sum(sq) / HEAD_QK + NORM_EPS)
            nn = xn * r
            nr = xr * r
            dt = _rope_t(dr_out, cosv, sinv)
            dnn = dn_out * gn
            dnr = dt * gr
            mean = _lane_sum(dnn * nn + dnr * nr) / HEAD_QK
            return (r * (dnn - nn * mean), r * (dnr - nr * mean),
                    jnp.sum(dn_out * nn, axis=0, keepdims=True), jnp.sum(dt * nr, axis=0, keepdims=True))

        for h in range(heads):
            lo = h * LANE
            qn = q_ref[:, lo:lo + LANE]
            qr = q_ref[:, hw + lo:hw + lo + LANE]
            dxn, dxr, g0, g1 = norm_bwd(qn, qr, qn * qn + qr * qr, dq_ref[:, 2 * lo:2 * lo + LANE] * QK_SCALE,
                                        dq_ref[:, 2 * lo + LANE:2 * lo + 2 * LANE] * QK_SCALE,
                                        g_ref[0:1, :], g_ref[1:2, :])
            dqr_ref[:, lo:lo + LANE] = dxn.astype(BF)
            dqr_ref[:, hw + lo:hw + lo + LANE] = dxr.astype(BF)
            kn = kv_ref[:, 2 * lo:2 * lo + LANE]
            dxn, dxr, g2, g3 = norm_bwd(kn, krv, kn * kn + kr_sq, dk_ref[:, 2 * lo:2 * lo + LANE],
                                        dk_ref[:, 2 * lo + LANE:2 * lo + 2 * LANE], g_ref[2:3, :], g_ref[3:4, :])
            dkv_ref[:, 2 * lo:2 * lo + LANE] = dxn.astype(BF)
            dkv_ref[:, 2 * lo + LANE:2 * lo + 2 * LANE] = dv_ref[:, lo:lo + LANE].astype(BF)
            dkr = dkr + dxr
            dgs = [a + b for a, b in zip(dgs, (g0, g1, g2, g3))]
        dkr_ref[...] = dkr

        @pl.when(i == 0)
        def _():
            dg_ref[...] = jnp.zeros_like(dg_ref)

        dg_ref[...] += _rows8(dgs, LANE)

    row = lambda w: pl.BlockSpec((tr, w), lambda i: (i, 0))
    return pl.pallas_call(
        body, name="head_bwd", grid=(s // tr,),
        in_specs=[row(2 * hw), row(2 * hw), pl.BlockSpec((tr, LANE), lambda i: (i, kr_blk)),
                  row(LANE), row(LANE), pl.BlockSpec((SUB, LANE), lambda i: (0, 0)),
                  row(2 * hw), row(2 * hw), row(hw)],
        out_specs=[row(2 * hw), row(2 * hw), row(LANE), pl.BlockSpec((SUB, LANE), lambda i: (0, 0))],
        out_shape=[jax.ShapeDtypeStruct((s, 2 * hw), BF), jax.ShapeDtypeStruct((s, 2 * hw), BF),
                   jax.ShapeDtypeStruct((s, LANE), F32), jax.ShapeDtypeStruct((SUB, LANE), F32)],
        compiler_params=_cp("arbitrary"),
    )(q_raw, kv_raw, z_a, cos, sin, gains, dq_att, dk_att, dv)


def _causal_mask(nrows, ncols, row0):
    rows = lax.broadcasted_iota(jnp.int32, (nrows, ncols), 0) + row0
    cols = lax.broadcasted_iota(jnp.int32, (nrows, ncols), 1)
    return cols <= rows


def _causal_steps(nt, q_major):
    pairs = ([(i, j) for i in range(nt) for j in range(i + 1)] if q_major
             else [(i, j) for j in range(nt) for i in range(j, nt)])
    return (jnp.array([p[0] for p in pairs], jnp.int32), jnp.array([p[1] for p in pairs], jnp.int32))


def _attn_fwd(q_att, k_att, v, heads):
    s = q_att.shape[0]
    t = _pick(s, ATTN_TILE_FWD, LANE)
    nt = s // t
    th = t // 2
    qi, kj = _causal_steps(nt, True)

    def body(qi_ref, kj_ref, q_ref, k_ref, v_ref, o_ref, ob_ref, lse_ref, m_s, l_s, acc_s):
        st = pl.program_id(1)
        i = qi_ref[st]
        j = kj_ref[st]

        @pl.when(j == 0)
        def _():
            m_s[...] = jnp.full_like(m_s, NEG_INF)
            l_s[...] = jnp.zeros_like(l_s)
            acc_s[...] = jnp.zeros_like(acc_s)

        def update(rows, ncol, masked):
            sc = lax.dot_general(q_ref[rows, :], k_ref[0:ncol, :], (((1,), (1,)), ((), ())),
                                 preferred_element_type=F32)
            if masked:
                sc = jnp.where(_causal_mask(rows.stop - rows.start, ncol, rows.start), sc, NEG_INF)
            m_prev = m_s[rows, :]
            m_new = jnp.maximum(m_prev, jnp.max(sc, axis=-1, keepdims=True))
            alpha = jnp.exp2(m_prev - m_new)
            p = jnp.exp2(sc - jnp.tile(m_new, (1, ncol // LANE)))
            l_s[rows, :] = alpha * l_s[rows, :] + jnp.sum(p, axis=-1, keepdims=True)
            acc_s[rows, :] = alpha * acc_s[rows, :] + jnp.dot(p.astype(BF), v_ref[0:ncol, :],
                                                              preferred_element_type=F32)
            m_s[rows, :] = m_new

        @pl.when(j < i)
        def _():
            update(slice(0, t), t, False)

        @pl.when(j == i)
        def _():
            update(slice(0, th), th, True)
            update(slice(th, t), t, True)
            o = acc_s[...] / l_s[...]
            o_ref[...] = o
            ob_ref[...] = o.astype(BF)
            lse_ref[...] = (m_s[...] + jnp.log2(l_s[...]))[:, 0:1]

    q_idx = lambda h, st, qi_r, kj_r: (qi_r[st], h)
    kv_idx = lambda h, st, qi_r, kj_r: (kj_r[st], h)
    return pl.pallas_call(
        body, name="attn_fwd",
        grid_spec=pltpu.PrefetchScalarGridSpec(
            num_scalar_prefetch=2, grid=(heads, qi.shape[0]),
            in_specs=[pl.BlockSpec((t, 2 * LANE), q_idx), pl.BlockSpec((t, 2 * LANE), kv_idx),
                      pl.BlockSpec((t, LANE), kv_idx)],
            out_specs=[pl.BlockSpec((t, LANE), q_idx), pl.BlockSpec((t, LANE), q_idx),
                       pl.BlockSpec((None, t, 1), lambda h, st, qi_r, kj_r: (h, qi_r[st], 0))],
            scratch_shapes=[pltpu.VMEM((t, LANE), F32), pltpu.VMEM((t, LANE), F32), pltpu.VMEM((t, LANE), F32)]),
        out_shape=[jax.ShapeDtypeStruct((s, heads * LANE), F32), jax.ShapeDtypeStruct((s, heads * LANE), BF),
                   jax.ShapeDtypeStruct((heads, s, 1), F32)],
        compiler_params=_cp("parallel", "arbitrary"),
    )(qi, kj, q_att, k_att, v)


def _attn_bwd(q_att, k_att, v, o, lse, d_o, heads, dep=None):
    s = q_att.shape[0]
    t = _pick(s, ATTN_TILE, LANE)
    nt = s // t
    th = t // 2
    qi, kj = _causal_steps(nt, False)

    def body(qi_ref, kj_ref, q_ref, k_ref, v_ref, do_ref, o_ref, lse_ref, *rest):
        dq_ref, dk_ref, dv_ref, dk_s, dv_s = rest[-5:]
        st = pl.program_id(1)
        i = qi_ref[st]
        j = kj_ref[st]

        @pl.when(st == 0)
        def _():
            dq_ref[...] = jnp.zeros_like(dq_ref)

        @pl.when(i == j)
        def _():
            dk_s[...] = jnp.zeros_like(dk_s)
            dv_s[...] = jnp.zeros_like(dv_s)

        def update(rows, ncol, masked):
            nrow = rows.stop - rows.start
            q = q_ref[rows, :]
            k = k_ref[0:ncol, :]
            do = do_ref[rows, :]
            sc = lax.dot_general(q, k, (((1,), (1,)), ((), ())), preferred_element_type=F32)
            if masked:
                sc = jnp.where(_causal_mask(nrow, ncol, rows.start), sc, NEG_INF)
            p = jnp.exp2(sc - lse_ref[rows, :])
            dp = lax.dot_general(do, v_ref[0:ncol, :], (((1,), (1,)), ((), ())), preferred_element_type=F32)
            delta = jnp.sum(do.astype(F32) * o_ref[rows, :], axis=-1, keepdims=True)
            ds = (p * (dp - delta)).astype(BF)
            dv_s[0:ncol, :] += lax.dot_general(p.astype(BF), do, (((0,), (0,)), ((), ())),
                                               preferred_element_type=F32)
            dk_s[0:ncol, :] += lax.dot_general(ds, q, (((0,), (0,)), ((), ())), preferred_element_type=F32)
            out_rows = pl.ds(pl.multiple_of(i * t + rows.start, nrow), nrow)
            dq_ref[out_rows, :] += jnp.dot(ds, k, preferred_element_type=F32)

        @pl.when(i > j)
        def _():
            update(slice(0, t), t, False)

        @pl.when(i == j)
        def _():
            update(slice(0, th), th, True)
            update(slice(th, t), t, True)

        @pl.when(i == nt - 1)
        def _():
            dk_ref[...] = (dk_s[...] * (1.0 / LOG2_E)).astype(BF)
            dv_ref[...] = dv_s[...].astype(BF)

    q_idx = lambda h, st, qi_r, kj_r: (qi_r[st], h)
    kv_idx = lambda h, st, qi_r, kj_r: (kj_r[st], h)
    in_specs = [pl.BlockSpec((t, 2 * LANE), q_idx), pl.BlockSpec((t, 2 * LANE), kv_idx),
                pl.BlockSpec((t, LANE), kv_idx), pl.BlockSpec((t, LANE), q_idx), pl.BlockSpec((t, LANE), q_idx),
                pl.BlockSpec((None, t, 1), lambda h, st, qi_r, kj_r: (h, qi_r[st], 0))]
    args = [q_att, k_att, v, d_o, o, lse]
    if dep is not None:
        in_specs.append(ANY)
        args.append(dep)
    return pl.pallas_call(
        body, name="attn_bwd",
        grid_spec=pltpu.PrefetchScalarGridSpec(
            num_scalar_prefetch=2, grid=(heads, qi.shape[0]),
            in_specs=in_specs,
            out_specs=[pl.BlockSpec((s, 2 * LANE), lambda h, st, qi_r, kj_r: (0, h)),
                       pl.BlockSpec((t, 2 * LANE), kv_idx), pl.BlockSpec((t, LANE), kv_idx)],
            scratch_shapes=[pltpu.VMEM((t, 2 * LANE), F32), pltpu.VMEM((t, LANE), F32)]),
        out_shape=[jax.ShapeDtypeStruct((s, heads * 2 * LANE), F32),
                   jax.ShapeDtypeStruct((s, heads * 2 * LANE), BF),
                   jax.ShapeDtypeStruct((s, heads * LANE), BF)],
        compiler_params=_cp("parallel", "arbitrary"),
    )(qi, kj, *args)


def _sum_parts(parts, name):
    n, r, c = parts.shape
    tr = _pick(r, 512, 8)

    def body(p_ref, o_ref):
        g = p_ref[0].astype(F32)
        for k in range(1, n):
            g = g + p_ref[k].astype(F32)
        o_ref[...] = g

    return pl.pallas_call(
        body, name=name, grid=(r // tr,),
        in_specs=[pl.BlockSpec((n, tr, c), lambda i: (0, i, 0))],
        out_specs=pl.BlockSpec((tr, c), lambda i: (i, 0)),
        out_shape=jax.ShapeDtypeStruct((r, c), F32),
        compiler_params=_cp("parallel"),
    )(parts)


def _adamw(parts, w, m, v, name, by_cols=False):
    n, rp, c = parts.shape
    r = w.shape[0]
    assert by_cols or rp == r
    tr, tc = (r, _pick(c, 256, LANE)) if by_cols else (_pick(r, 256, 16 if r % 16 == 0 else 8), c)

    def body(p_ref, w_ref, m_ref, v_ref, g_ref, d_ref, mo_ref, vo_ref):
        g = p_ref[0].astype(F32)
        for k in range(1, n):
            g = g + p_ref[k].astype(F32)
        g = g[:r] if by_cols else g
        m_new = ADAM_B1 * m_ref[...] + (1.0 - ADAM_B1) * g
        v_new = ADAM_B2 * v_ref[...] + (1.0 - ADAM_B2) * jnp.square(g)
        m_hat = m_new / (1.0 - ADAM_B1 ** ADAM_STEP)
        v_hat = v_new / (1.0 - ADAM_B2 ** ADAM_STEP)
        g_ref[...] = g
        d_ref[...] = -ADAM_LR * (m_hat / (jnp.sqrt(v_hat) + ADAM_EPS) + ADAM_WD * w_ref[...])
        mo_ref[...] = m_new
        vo_ref[...] = v_new

    idx = (lambda i: (0, i)) if by_cols else (lambda i: (i, 0))
    spec = pl.BlockSpec((tr, tc), idx)
    sh = jax.ShapeDtypeStruct((r, c), F32)
    return pl.pallas_call(
        body, name=name, grid=(c // tc if by_cols else r // tr,),
        in_specs=[pl.BlockSpec((n, rp if by_cols else tr, tc), lambda i: (0,) + idx(i)), spec, spec, spec],
        out_specs=[spec] * 4, out_shape=[sh] * 4,
        compiler_params=_cp("parallel"),
    )(parts, w, m, v)


def _place():
    x, y, c = lax.axis_index("x"), lax.axis_index("y"), lax.axis_index("c")
    chips = [(1 - x, y), (x, 1 - y), (1 - x, 1 - y)]
    return x, y, c, chips


def _all_gather(shards, name, dep=None):
    n = len(shards)
    deps = [] if dep is None else list(dep)

    def body(*refs):
        ins, outs = refs[:n], refs[n + len(deps):2 * n + len(deps)]
        send_sems, recv_sems, local_sems = refs[2 * n + len(deps):]
        x, y, c, chips = _place()
        me, sibling = (x, y, c), (x, y, 1 - c)

        def slot(w, p):
            return outs[w].at[4 * p[0] + 2 * p[1] + p[2]]

        def copy(w, k, block, to, src=None):
            return pltpu.make_async_remote_copy(
                src_ref=slot(w, block) if src is None else src, dst_ref=slot(w, block),
                send_sem=send_sems.at[w, k], recv_sem=recv_sems.at[w, k], device_id=to, device_id_type=MESH)

        first = []
        for w in range(n):
            first += [copy(w, 1 + j, me, (*chip, c), src=ins[w]) for j, chip in enumerate(chips)]
            first.append(copy(w, 0, me, sibling, src=ins[w]))
        for cp in first:
            cp.start()
        mine = [pltpu.make_async_copy(ins[w], slot(w, me), local_sems.at[w]) for w in range(n)]
        for cp in mine:
            cp.start()
        passed = []
        for w in range(n):
            for j, chip in enumerate(chips):
                copy(w, 1 + j, (*chip, c), me).wait_recv()
                cp = copy(w, 4 + j, (*chip, c), sibling)
                cp.start()
                passed.append(cp)
        for w in range(n):
            copy(w, 0, sibling, me).wait_recv()
            for j, chip in enumerate(chips):
                copy(w, 4 + j, (*chip, 1 - c), me).wait_recv()
        for cp in first + passed:
            cp.wait_send()
        for cp in mine:
            cp.wait()

    return pl.pallas_call(
        body, name=name,
        in_specs=[ANY] * (n + len(deps)), out_specs=[ANY] * n,
        out_shape=[jax.ShapeDtypeStruct((N_DEV,) + a.shape, a.dtype) for a in shards],
        scratch_shapes=[pltpu.SemaphoreType.DMA((n, 7)), pltpu.SemaphoreType.DMA((n, 7)),
                        pltpu.SemaphoreType.DMA((n,))],
    )(*shards, *deps)


HBM = pl.BlockSpec(memory_space=pltpu.HBM)
SEM = pl.BlockSpec(memory_space=pltpu.SEMAPHORE)
EFFECT = pltpu.SideEffectType.DATAFLOW_SIDE_EFFECTING
PEERS = [(dx, dy, dc) for dx in (1, 0) for dy in (1, 0) for dc in (0, 1) if (dx, dy, dc) != (0, 0, 0)]


def _peer(x, y, c, flip):
    dx, dy, dc = flip
    return (1 - x if dx else x, 1 - y if dy else y, 1 - c if dc else c)


def _exchange_copies(srcs, lands, send, recv, loc, gather):
    x, y, c, _ = _place()
    me = 4 * x + 2 * y + c
    remote, local = [], []
    for w in range(len(srcs)):
        for k, flip in enumerate(PEERS):
            px, py, pc = _peer(x, y, c, flip)
            src = srcs[w] if gather else srcs[w].at[4 * px + 2 * py + pc]
            remote.append(pltpu.make_async_remote_copy(
                src_ref=src, dst_ref=lands[w].at[me], send_sem=send[w].at[k], recv_sem=recv[w].at[k],
                device_id=(px, py, pc), device_id_type=MESH))
        local.append(pltpu.make_async_copy(srcs[w] if gather else srcs[w].at[me], lands[w].at[me], loc[w]))
    return remote, local


class _Exchange:
    def __init__(self, srcs, lands, send, recv, loc, token, gather):
        self.srcs, self.lands, self.send, self.recv, self.loc = srcs, lands, send, recv, loc
        self.token, self.gather = token, gather


def _exchange_start(srcs, gather, name, dep=None):
    n = len(srcs)
    deps = [] if dep is None else [dep]
    land_shapes = [((N_DEV,) + a.shape) if gather else a.shape for a in srcs]
    lands = [pltpu.with_memory_space_constraint(lax.empty(sh, a.dtype), pltpu.HBM) for sh, a in zip(land_shapes, srcs)]
    srcs = [pltpu.with_memory_space_constraint(a, pltpu.HBM) for a in srcs]

    def body(*refs):
        src_refs, land_refs = refs[:n], refs[n:2 * n]
        outs = refs[2 * n + len(deps):]
        send, recv, loc = outs[:n], outs[n:2 * n], outs[2 * n:3 * n]
        token = outs[-1]
        remote, local = _exchange_copies(src_refs, land_refs, send, recv, loc, gather)
        for cp in remote + local:
            cp.start()
        token[...] = jnp.zeros_like(token)

    out_shape = ([pltpu.SemaphoreType.DMA((len(PEERS),))] * (2 * n) + [pltpu.SemaphoreType.DMA(())] * n
                 + [pltpu.HBM(a.shape, a.dtype) for a in srcs] + [pltpu.HBM(a.shape, a.dtype) for a in lands]
                 + [jax.ShapeDtypeStruct((SUB, LANE), F32)])
    res = pl.pallas_call(
        body, name=name, out_shape=out_shape,
        in_specs=[HBM] * (2 * n) + [ANY] * len(deps),
        out_specs=[SEM] * (3 * n) + [HBM] * (2 * n) + [pl.BlockSpec(memory_space=pltpu.VMEM)],
        input_output_aliases={i: 3 * n + i for i in range(2 * n)},
        compiler_params=pltpu.CompilerParams(has_side_effects=EFFECT),
    )(*srcs, *lands, *deps)
    return _Exchange(res[3 * n:4 * n], res[4 * n:5 * n], res[:n], res[n:2 * n], res[2 * n:3 * n], res[-1], gather)


def _exchange_wait(ex, idxs, after, name):
    n = len(idxs)
    srcs = [ex.srcs[i] for i in idxs]
    lands = [ex.lands[i] for i in idxs]
    sems = [ex.send[i] for i in idxs] + [ex.recv[i] for i in idxs] + [ex.loc[i] for i in idxs]
    gather = ex.gather

    def body(*refs):
        src_refs, land_refs = refs[:n], refs[n:2 * n]
        send, recv, loc = refs[2 * n:3 * n], refs[3 * n:4 * n], refs[4 * n:5 * n]
        remote, local = _exchange_copies(src_refs, land_refs, send, recv, loc, gather)
        for cp in remote:
            cp.wait_send()
            cp.wait_recv()
        for cp in local:
            cp.wait()

    res = pl.pallas_call(
        body, name=name,
        out_shape=[pltpu.HBM(a.shape, a.dtype) for a in srcs] + [pltpu.HBM(a.shape, a.dtype) for a in lands],
        in_specs=[HBM] * (2 * n) + [SEM] * (3 * n) + [ANY],
        out_specs=[HBM] * (2 * n),
        input_output_aliases={i: i for i in range(2 * n)},
        compiler_params=pltpu.CompilerParams(has_side_effects=EFFECT),
    )(*srcs, *lands, *sems, after)
    return res[n:]


def _gather2_copies(srcs, lands, send, recv_ici, recv_sib, loc):
    x, y, c, chips = _place()
    me = 4 * x + 2 * y + c
    remote, local = [], []
    for w in range(len(srcs)):
        remote.append(pltpu.make_async_remote_copy(
            src_ref=srcs[w], dst_ref=lands[w].at[me], send_sem=send[w].at[0], recv_sem=recv_sib[w],
            device_id=(x, y, 1 - c), device_id_type=MESH))
        for j, chip in enumerate(chips):
            remote.append(pltpu.make_async_remote_copy(
                src_ref=srcs[w], dst_ref=lands[w].at[me], send_sem=send[w].at[1 + j], recv_sem=recv_ici[w].at[j],
                device_id=(*chip, c), device_id_type=MESH))
        local.append(pltpu.make_async_copy(srcs[w], lands[w].at[me], loc[w]))
    return remote, local


def _gather2_forwards(lands, fsend, frecv, arrived=None):
    x, y, c, chips = _place()
    cps = []
    for w in range(len(lands)):
        for j, chip in enumerate(chips):
            slot = lands[w].at[4 * chip[0] + 2 * chip[1] + c]
            cp = pltpu.make_async_remote_copy(
                src_ref=slot, dst_ref=slot, send_sem=fsend[w].at[j], recv_sem=frecv[w].at[j],
                device_id=(x, y, 1 - c), device_id_type=MESH)
            if arrived is not None:
                pltpu.make_async_remote_copy(
                    src_ref=slot, dst_ref=slot, send_sem=fsend[w].at[j], recv_sem=arrived[w].at[j],
                    device_id=(x, y, 1 - c), device_id_type=MESH).wait_recv()
            cps.append(cp)
    return cps


def _gather2(shards, between, name):
    n = len(shards)
    srcs = [pltpu.with_memory_space_constraint(a, pltpu.HBM) for a in shards]
    lands = [pltpu.with_memory_space_constraint(lax.empty((N_DEV,) + a.shape, a.dtype), pltpu.HBM) for a in shards]
    hbm_like = lambda arrs: [pltpu.HBM(a.shape, a.dtype) for a in arrs]
    tok = jax.ShapeDtypeStruct((SUB, LANE), F32)
    vmem = pl.BlockSpec(memory_space=pltpu.VMEM)
    side = pltpu.CompilerParams(has_side_effects=EFFECT)

    def start(*refs):
        src_refs, land_refs = refs[:n], refs[n:2 * n]
        outs = refs[2 * n:]
        send, recv_ici, recv_sib, loc = outs[:n], outs[n:2 * n], outs[2 * n:3 * n], outs[3 * n:4 * n]
        remote, local = _gather2_copies(src_refs, land_refs, send, recv_ici, recv_sib, loc)
        for cp in remote + local:
            cp.start()
        outs[-1][...] = jnp.zeros((SUB, LANE), F32)

    res = pl.pallas_call(
        start, name=name + "_start",
        out_shape=([pltpu.SemaphoreType.DMA((4,))] * n + [pltpu.SemaphoreType.DMA((3,))] * n
                   + [pltpu.SemaphoreType.DMA(())] * (2 * n) + hbm_like(srcs) + hbm_like(lands) + [tok]),
        in_specs=[HBM] * (2 * n), out_specs=[SEM] * (4 * n) + [HBM] * (2 * n) + [vmem],
        input_output_aliases={i: 4 * n + i for i in range(2 * n)}, compiler_params=side,
    )(*srcs, *lands)
    send, recv_ici, recv_sib, loc = res[:n], res[n:2 * n], res[2 * n:3 * n], res[3 * n:4 * n]
    srcs, lands, token = res[4 * n:5 * n], res[5 * n:6 * n], res[-1]

    done = between(token)
    after = jax.tree_util.tree_leaves(done)

    def forward(*refs):
        land_refs, arrived = refs[:n], refs[n:2 * n]
        outs = refs[2 * n + len(after):]
        fsend, frecv = outs[:n], outs[n:2 * n]
        for cp in _gather2_forwards(land_refs, fsend, frecv, arrived):
            cp.start()
        outs[-1][...] = jnp.zeros((SUB, LANE), F32)

    res = pl.pallas_call(
        forward, name=name + "_forward",
        out_shape=[pltpu.SemaphoreType.DMA((3,))] * (2 * n) + hbm_like(lands) + [tok],
        in_specs=[HBM] * n + [SEM] * n + [ANY] * len(after), out_specs=[SEM] * (2 * n) + [HBM] * n + [vmem],
        input_output_aliases={i: 2 * n + i for i in range(n)}, compiler_params=side,
    )(*lands, *recv_ici, *after)
    fsend, frecv, lands, token = res[:n], res[n:2 * n], res[2 * n:3 * n], res[-1]

    def wait(*refs):
        src_refs, land_refs = refs[:n], refs[n:2 * n]
        sems = refs[2 * n:7 * n]
        send, recv_sib, loc, fsend, frecv = (sems[k * n:(k + 1) * n] for k in range(5))
        remote, local = _gather2_copies(src_refs, land_refs, send, send, recv_sib, loc)
        for w in range(n):
            for cp in remote[4 * w:4 * w + 4]:
                cp.wait_send()
            remote[4 * w].wait_recv()
        for cp in local:
            cp.wait()
        for cp in _gather2_forwards(land_refs, fsend, frecv):
            cp.wait_send()
            cp.wait_recv()

    res = pl.pallas_call(
        wait, name=name + "_wait", out_shape=hbm_like(srcs) + hbm_like(lands),
        in_specs=[HBM] * (2 * n) + [SEM] * (5 * n) + [ANY], out_specs=[HBM] * (2 * n),
        input_output_aliases={i: i for i in range(2 * n)}, compiler_params=side,
    )(*srcs, *lands, *send, *recv_sib, *loc, *fsend, *frecv, token)
    return res[n:], done


def _after(token, a):
    return a + token[0:1, 0:1].astype(a.dtype)


def _unblock(w3):
    nb, k, nbw = w3.shape
    return w3.transpose(1, 0, 2).reshape(k, nb * nbw)


def _block(w, nb):
    k, n = w.shape
    return w.reshape(k, nb, n // nb).transpose(1, 0, 2)


def kernel(x, positions, ln1_g, w_in, b_gate, conv_w, w_conv_out, q_a_g, w_q_b, kv_a_g, w_kv_b, q_norm_g, k_norm_g, w_mla_out, w_o, ln2_g, w_ffn_up, ffn_conv_w, ffn_conv_b, w_ffn_down, loss_target, m_ln1_g, m_w_in, m_b_gate, m_conv_w, m_w_conv_out, m_q_a_g, m_w_q_b, m_kv_a_g, m_w_kv_b, m_q_norm_g, m_k_norm_g, m_w_mla_out, m_w_o, m_ln2_g, m_w_ffn_up, m_ffn_conv_w, m_ffn_conv_b, m_w_ffn_down, v_ln1_g, v_w_in, v_b_gate, v_conv_w, v_w_conv_out, v_q_a_g, v_w_q_b, v_kv_a_g, v_w_kv_b, v_q_norm_g, v_k_norm_g, v_w_mla_out, v_w_o, v_ln2_g, v_w_ffn_up, v_ffn_conv_w, v_ffn_conv_b, v_w_ffn_down):
    s, d = x.shape[1], x.shape[2]
    conv = conv_w.shape[2] * N_DEV
    ql, kvl = q_a_g.shape[1], kv_a_g.shape[1]
    heads = w_q_b.shape[2] * N_DEV // HEAD_QK
    dff = w_ffn_down.shape[1] * N_DEV
    hw = heads * LANE
    conv3 = 3 * conv
    kr_off = conv3 + ql
    kv_off = -(-(kr_off + LANE) // kvl) * kvl
    wa = kv_off + kvl
    assert conv3 % ql == 0 and kr_off % LANE == 0
    xs = x[0]
    tgt = loss_target[0]
    pos = positions.reshape(s, 1)

    nin = w_in.shape[2]
    big = dict(w_in=w_in[0].T, w_conv_out=w_conv_out[0], w_q_b=w_q_b[0], w_kv_b=w_kv_b[0],
               w_mla_out=w_mla_out[0], w_o=w_o[0], w_ffn_up=w_ffn_up[0], w_ffn_down=w_ffn_down[0])
    names = list(big)
    rest = names[1:]
    early = {}

    def while_w_in_travels(token):
        early["ag"] = _exchange_start([big[k].astype(BF) for k in rest], True, "gather_rest_start", dep=token)
        cos_sin = _rope_tables(pos)
        return cos_sin, _rms_fwd(xs, _after(early["ag"].token, ln1_g), d, 0, "rms1_fwd")

    first, ((cos, sin), u1) = _gather2([big["w_in"].astype(BF), _pad8(conv_w[0]), _pad8(ffn_conv_w[0])],
                                       while_w_in_travels, "gather_w_in")
    ag = early["ag"]
    cw8 = _unblock(first[1])
    fcw8 = _unblock(first[2])

    def landed(keys, after, name):
        return _exchange_wait(ag, [rest.index(k) for k in keys], after, name)

    w_in_t = first[0].reshape(N_DEV * nin, d)
    g_off = kr_off + kvl + ROPE
    w_a_t = jnp.concatenate([w_in_t[:kr_off], _lay_rows(w_in_t[kr_off + kvl:g_off]),
                             jnp.zeros((kv_off - kr_off - LANE, d), BF), w_in_t[kr_off:kr_off + kvl]], axis=0)[None]
    w_g_t = w_in_t[g_off:][None]
    gains = _pad8(jnp.concatenate([q_norm_g[:, :NOPE], _lay(q_norm_g[:, NOPE:]),
                                   k_norm_g[:, :NOPE], _lay(k_norm_g[:, NOPE:])], axis=0))
    kr_blk = kr_off // LANE

    z_a = _mm_nt(u1, w_a_t, "mm_z_a")
    z_g = _mm_nt(u1, w_g_t, "mm_z_g", out_dtype=BF)
    p = _conv_mix_fwd(z_a, cw8, conv)
    w_co, w_qb, w_kv = landed(["w_conv_out", "w_q_b", "w_kv_b"], p, "gather_wait_mixers")
    w_co = _unblock(w_co)[None]
    w_kv = _unblock(w_kv)[None]
    wq_full = _unblock(w_qb).reshape(ql, heads, HEAD_QK)
    w_q = jnp.concatenate([wq_full[:, :, :NOPE].reshape(ql, hw), _lay(wq_full[:, :, NOPE:]).reshape(ql, hw)],
                          axis=1)[None]
    yc = _mm_nn(p, w_co, "mm_y_conv", out_dtype=BF)
    qn = _rms_fwd(z_a, q_a_g, ql, conv3 // ql, "rms_q_fwd")
    kvn = _rms_fwd(z_a, kv_a_g, kvl, kv_off // kvl, "rms_kv_fwd")
    q_raw = _mm_nn(qn, w_q, "mm_q")
    kv_raw = _mm_nn(kvn, w_kv, "mm_kv")
    q_att, k_att, v_bf = _head_fwd(q_raw, kv_raw, z_a, kr_blk, cos, sin, gains, heads)
    o, o_bf, lse = _attn_fwd(q_att, k_att, v_bf, heads)
    w_mo, w_oo = landed(["w_mla_out", "w_o"], lse, "gather_wait_outs")
    w_mo = w_mo.reshape(1, hw, d)
    w_oo = w_oo.reshape(1, d, d)
    ym, mix = _mla_out_gate(o_bf, w_mo, z_g, b_gate, yc)
    h1, u2 = _residual_norm(mix, w_oo, xs, ln2_g)
    w_up, = landed(["w_ffn_up"], u2, "gather_wait_ffn_up")
    a_g, a_u, f = _ffn_up_act(u2, w_up, fcw8, ffn_conv_b, dff)
    w_dn, = landed(["w_ffn_down"], f, "gather_wait_ffn_down")
    w_dn = w_dn.reshape(1, dff, d)
    dy, dy_bf, loss_part = _mm_nn_loss(f, w_dn, h1, tgt, "mm_ffn_down_loss")

    g_dn = _mm_tn(f, dy_bf, 1, "mm_g_ffn_down").reshape(N_DEV, dff // N_DEV, d)
    rs_dn = _exchange_start([g_dn], False, "reduce_ffn_down_start")
    d_f = _mm_nt(dy_bf, w_dn, "mm_d_f", dep=rs_dn.token)
    d_xg, d_xu, dfw_g, dfw_u = _ffn_act_bwd(a_g, a_u, d_f, fcw8, ffn_conv_b, dff)
    half = N_DEV // 2
    g_up = _mm_tn(u2, d_xg, half, "mm_g_ffn_up_gate", into=lax.empty((N_DEV, d, 2 * dff // N_DEV), BF))
    g_up = _mm_tn(u2, d_xu, half, "mm_g_ffn_up_up", into=g_up, blk0=half)
    rs_up = _exchange_start([g_up], False, "reduce_ffn_up_start")
    d_u2 = _mm_nt([d_xg, d_xu], w_up, "mm_d_u2", out_dtype=BF, dep=rs_up.token)
    d_h1, d_h1_bf, dg_ln2 = _rms_bwd(h1, d_u2, ln2_g, d, 0, "rms2_bwd", extra=dy, also_bf16=True)
    g_oo = _mm_tn(mix, d_h1_bf, 1, "mm_g_w_o").reshape(N_DEV, d // N_DEV, d)
    d_zga, d_zgb, d_yc, d_ym, dba, dbb = _d_mix_gate(d_h1_bf, w_oo, z_g, b_gate, yc, ym)
    g_co = _block(_mm_tn(p, d_yc, 1, "mm_g_conv_out")[0], N_DEV)
    g_mo = _mm_tn(o_bf, d_ym, 1, "mm_g_mla_out").reshape(N_DEV, hw // N_DEV, d)
    rs_mix = _exchange_start([g_oo, g_co, g_mo], False, "reduce_mixers_start")
    d_p = _mm_nt(d_yc, w_co, "mm_d_p", dep=rs_mix.token)
    d_o = _mm_nt(d_ym, w_mo, "mm_d_o", out_dtype=BF)
    d_zb, d_zc, d_zv, dcw = _conv_mix_bwd(z_a, d_p, cw8, conv)
    dq_att, dk_att, dv = _attn_bwd(q_att, k_att, v_bf, o, lse, d_o, heads, dep=rs_mix.token)
    d_q_raw, d_kv_raw, d_kr, dgains = _head_bwd(q_raw, kv_raw, z_a, kr_blk, cos, sin, gains, dq_att, dk_att, dv, heads)
    g_q2 = _mm_tn(qn, d_q_raw, 1, "mm_g_q")[0]
    g_qb = _block(jnp.concatenate([g_q2[:, :hw].reshape(ql, heads, NOPE),
                                   _unlay(g_q2[:, hw:].reshape(ql, heads, LANE))], axis=2).reshape(ql, heads * HEAD_QK), N_DEV)
    g_kv = _block(_mm_tn(kvn, d_kv_raw, 1, "mm_g_kv")[0], N_DEV)
    rs_qkv = _exchange_start([g_qb, g_kv], False, "reduce_qkv_start")
    d_qn = _mm_nt(d_q_raw, w_q, "mm_d_qn", dep=rs_qkv.token)
    d_kvn = _mm_nt(d_kv_raw, w_kv, "mm_d_kvn")
    d_ql, dg_qa = _rms_bwd(z_a, d_qn, q_a_g, ql, conv3 // ql, "rms_q_bwd", out_dtype=BF)
    d_kvl, dg_kva = _rms_bwd(z_a, d_kvn, kv_a_g, kvl, kv_off // kvl, "rms_kv_bwd", out_dtype=BF)
    d_z_a = jnp.concatenate([d_zb, d_zc, d_zv, d_ql, d_kr.astype(BF), jnp.zeros((s, kv_off - kr_off - LANE), BF),
                             d_kvl], axis=1)
    g_a = _mm_tn(d_z_a, u1, 1, "mm_g_w_a")[0]
    g_ga = _mm_tn(d_zga, u1, 1, "mm_g_w_ga")[0]
    g_gb = _mm_tn(d_zgb, u1, 1, "mm_g_w_gb")[0]
    g_in = jnp.concatenate([g_a[:kr_off], g_a[kv_off:kv_off + kvl], g_a[kr_off:kr_off + HALF],
                            g_a[kr_off + 2 * HALF:kr_off + 3 * HALF], g_ga, g_gb], axis=0).reshape(N_DEV, nin, d)
    rs_in = _exchange_start([g_in], False, "reduce_w_in_start")
    d_u1 = _mm_nn(d_z_a, w_a_t, "mm_d_u1_a", dep=rs_in.token)
    d_u1 = _mm_nn([d_zga, d_zgb], w_g_t, "mm_d_u1_g", add=d_u1)
    grad_x, dg_ln1 = _rms_bwd(xs, d_u1, ln1_g, d, 0, "rms1_bwd", extra=d_h1)

    summed = {}
    summed["w_ffn_down"], = _exchange_wait(rs_dn, [0], grad_x, "reduce_ffn_down_wait")
    summed["w_ffn_up"], = _exchange_wait(rs_up, [0], grad_x, "reduce_ffn_up_wait")
    summed["w_o"], summed["w_conv_out"], summed["w_mla_out"] = _exchange_wait(rs_mix, [0, 1, 2], grad_x, "reduce_mixers_wait")
    summed["w_q_b"], summed["w_kv_b"] = _exchange_wait(rs_qkv, [0, 1], grad_x, "reduce_qkv_wait")
    loc = locals()
    out = {}
    for k in rest:
        out[k] = _adamw(summed[k], big[k], loc["m_" + k][0], loc["v_" + k][0], "adamw_" + k)

    small = dict(ln1_g=dg_ln1[0:1], b_gate=jnp.concatenate([dba[0:1], dbb[0:1]], axis=1), q_a_g=dg_qa[0:1],
                 kv_a_g=dg_kva[0:1],
                 q_norm_g=jnp.concatenate([dgains[0:1], _unlay(dgains[1:2])], axis=1),
                 k_norm_g=jnp.concatenate([dgains[2:3], _unlay(dgains[3:4])], axis=1),
                 ln2_g=dg_ln2[0:1], ffn_conv_b=jnp.concatenate([dfw_g[3:4], dfw_u[3:4]], axis=1))
    small_names = list(small)
    extra = [dcw[0:3].reshape(1, -1), jnp.concatenate([dfw_g[0:3], dfw_u[0:3]], axis=1).reshape(1, -1),
             loss_part[0:1, 0:1]]
    flat = jnp.concatenate([small[k] for k in small_names] + extra, axis=1)
    n_flat = flat.shape[1]
    rows = -(-n_flat // (SUB * LANE)) * SUB
    flat = jnp.pad(flat, ((0, 0), (0, rows * LANE - n_flat))).reshape(rows, LANE)
    total = _sum_parts(_all_gather([flat], "gather_small", dep=[out[k][0] for k in rest])[0], "sum_small").reshape(1, rows * LANE)
    off = 0
    small_g = {}
    for k in small_names:
        small_g[k] = total[:, off:off + small[k].shape[1]]
        off += small[k].shape[1]
    me = 4 * lax.axis_index("x") + 2 * lax.axis_index("y") + lax.axis_index("c")
    cwn, fcwn = conv // N_DEV, 2 * dff // N_DEV
    g_cw = lax.dynamic_slice_in_dim(total[:, off:off + 3 * conv].reshape(3, conv), me * cwn, cwn, axis=1)
    off += 3 * conv
    g_fcw = lax.dynamic_slice_in_dim(total[:, off:off + 6 * dff].reshape(3, 2 * dff), me * fcwn, fcwn, axis=1)
    off += 6 * dff
    loss = total[0, off]

    summed["w_in"], = _exchange_wait(rs_in, [0], total, "reduce_w_in_wait")
    out["w_in"] = [r.T for r in _adamw(summed["w_in"], big["w_in"], m_w_in[0].T, v_w_in[0].T, "adamw_w_in",
                                       by_cols=True)]
    small_w = dict(ln1_g=ln1_g, b_gate=b_gate, q_a_g=q_a_g, kv_a_g=kv_a_g, q_norm_g=q_norm_g, k_norm_g=k_norm_g,
                   ln2_g=ln2_g, ffn_conv_b=ffn_conv_b, conv_w=conv_w[0].reshape(1, -1),
                   ffn_conv_w=ffn_conv_w[0].reshape(1, -1))
    small_g["conv_w"] = g_cw.reshape(1, -1)
    small_g["ffn_conv_w"] = g_fcw.reshape(1, -1)
    packed_names = list(small_w)

    def pack(get):
        vflat = jnp.concatenate([get(k).reshape(1, -1) for k in packed_names], axis=1)
        nr = -(-vflat.shape[1] // (SUB * LANE)) * SUB
        return jnp.pad(vflat, ((0, 0), (0, nr * LANE - vflat.shape[1])), constant_values=1.0).reshape(nr, LANE)

    res = _adamw(pack(lambda k: small_g[k])[None], pack(lambda k: small_w[k]), pack(lambda k: loc["m_" + k]),
                 pack(lambda k: loc["v_" + k]), "adamw_small")
    res = [r.reshape(1, -1) for r in res]
    off = 0
    for k in packed_names:
        shape = loc[k].shape
        size = small_w[k].shape[1]
        out[k] = [r[:, off:off + size].reshape(shape) for r in res]
        off += size
    for k in names:
        out[k] = [r[None] for r in out[k]]

    order = ["ln1_g", "w_in", "b_gate", "conv_w", "w_conv_out", "q_a_g", "w_q_b", "kv_a_g", "w_kv_b", "q_norm_g",
             "k_norm_g", "w_mla_out", "w_o", "ln2_g", "w_ffn_up", "ffn_conv_w", "ffn_conv_b", "w_ffn_down"]
    return (loss, grad_x[None], *[out[k][0] for k in order], *[out[k][1] for k in order],
            *[out[k][2] for k in order], *[out[k][3] for k in order])
```

```python
import functools

import jax
import jax.numpy as jnp
from jax import lax
from jax.experimental import pallas as pl
from jax.experimental.pallas import tpu as pltpu

BF = jnp.bfloat16
F32 = jnp.float32
MESH = pl.DeviceIdType.MESH
N_DEV = 8

NOPE = 128
ROPE = 64
HALF = ROPE // 2
HEAD_QK = NOPE + ROPE
HEAD_V = 128
LANE = 128
SUB = 8
QK_SCALE = HEAD_QK ** -0.5
LOG2_E = 1.4426950408889634
NORM_EPS = 1e-6
NEG_INF = -1e30
ROPE_THETA = 10000.0
ADAM_LR = 0.001
ADAM_B1 = 0.9
ADAM_B2 = 0.999
ADAM_EPS = 1e-08
ADAM_WD = 0.01
ADAM_STEP = 10

VMEM_LIMIT = 52 * 1024 * 1024
MM_TM, MM_TN, MM_TK, MM_TS = 1024, 1536, 2048, 2048
ROW_TILE, ROW_TILE_BWD = 512, 256
HEAD_ROW_TILE, HEAD_ROW_TILE_BWD = 256, 256
COL_TILE = 512
ATTN_TILE = 1024
ATTN_TILE_FWD = 1024
ANY = pl.BlockSpec(memory_space=pl.ANY)


def _pick(n, target, mult):
    t = (min(n, target) // mult) * mult
    while t > 0:
        if n % t == 0:
            return t
        t -= mult
    raise ValueError(f"no tile for {n} (target {target}, multiple {mult})")


def _cp(*sem):
    return pltpu.CompilerParams(dimension_semantics=sem, vmem_limit_bytes=VMEM_LIMIT)


def _accumulate(kk, nk, acc, part, finish):
    if nk == 1:
        finish(part())
        return

    @pl.when(kk == 0)
    def _():
        acc[...] = part()

    @pl.when((kk > 0) & (kk < nk - 1))
    def _():
        acc[...] += part()

    @pl.when(kk == nk - 1)
    def _():
        finish(acc[...] + part())


def _mm_call(body, name, grid, in_specs, args, out_spec, out_shape, acc_shape, nk, dep):
    if dep is not None:
        in_specs = in_specs + [ANY]
        args = args + [dep]
    return pl.pallas_call(
        body, name=name, grid=grid, in_specs=in_specs, out_specs=out_spec, out_shape=out_shape,
        scratch_shapes=[pltpu.VMEM(acc_shape, F32)] if nk > 1 else [],
        compiler_params=_cp("parallel", "parallel", "arbitrary"),
    )(*args)


def _mm_nn_loss(a, b3, add, target, name):
    m, k = a.shape
    _, k2, n = b3.shape
    assert k == k2 and b3.shape[0] == 1
    tm = _pick(m, MM_TM, 16)
    tn = _pick(n, MM_TN, LANE)
    tk = _pick(k, MM_TK, LANE)
    nk = k // tk

    def body(a_ref, b_ref, c_ref, t_ref, dy_ref, dyb_ref, l_ref, acc):
        kk = pl.program_id(2)

        @pl.when((pl.program_id(0) == 0) & (pl.program_id(1) == 0) & (kk == 0))
        def _():
            l_ref[...] = jnp.zeros_like(l_ref)

        def part():
            return jnp.dot(a_ref[...].astype(BF), b_ref[0].astype(BF), preferred_element_type=F32)

        def finish(r):
            e = r + c_ref[...] - t_ref[...]
            dy_ref[...] = e / n
            dyb_ref[...] = (e / n).astype(BF)
            l_ref[...] += 0.5 * jnp.sum(jnp.sum(e * e, axis=-1, keepdims=True), axis=0, keepdims=True) / n

        _accumulate(kk, nk, acc, part, finish)

    tile = pl.BlockSpec((tm, tn), lambda i, j, kk: (i, j))
    return pl.pallas_call(
        body, name=name, grid=(m // tm, n // tn, nk),
        in_specs=[pl.BlockSpec((tm, tk), lambda i, j, kk: (i, kk)),
                  pl.BlockSpec((1, tk, tn), lambda i, j, kk: (0, kk, j)), tile, tile],
        out_specs=[tile, tile, pl.BlockSpec((SUB, LANE), lambda i, j, kk: (0, 0))],
        out_shape=[jax.ShapeDtypeStruct((m, n), F32), jax.ShapeDtypeStruct((m, n), BF),
                   jax.ShapeDtypeStruct((SUB, LANE), F32)],
        scratch_shapes=[pltpu.VMEM((tm, tn), F32)],
        compiler_params=_cp("arbitrary", "arbitrary", "arbitrary"),
    )(a, b3, add, target)


def _mm_nn(a, b3, name, add=None, out_dtype=F32, blk0=0, nblk=None, dep=None):
    pair = isinstance(a, (list, tuple))
    a_list = list(a) if pair else [a]
    m, ka = a_list[0].shape
    k = ka * len(a_list)
    nb_all, k2, nbw = b3.shape
    assert k == k2
    nblk = nb_all - blk0 if nblk is None else nblk
    n = nblk * nbw
    tm = _pick(m, MM_TM if k > MM_TM else 2 * MM_TM, 16)
    tn = _pick(nbw, MM_TN, LANE)
    tk = _pick(ka, MM_TK, LANE)
    per = nbw // tn
    nk = k // tk
    nka = ka // tk
    na_ops = len(a_list)

    def body(*refs):
        a_refs, b_ref = refs[:na_ops], refs[na_ops]
        c_ref = refs[na_ops + 1] if add is not None else None
        o_ref = refs[na_ops + 1 + (add is not None) + (dep is not None)]
        acc = refs[-1]
        kk = pl.program_id(2)

        def part():
            av = a_refs[0][...] if not pair else jnp.where(kk < nka, a_refs[0][...], a_refs[1][...])
            return jnp.dot(av.astype(BF), b_ref[...].astype(BF), preferred_element_type=F32)

        def finish(r):
            if add is not None:
                r = r + c_ref[...]
            o_ref[...] = r.astype(out_dtype)

        _accumulate(kk, nk, acc, part, finish)

    if pair:
        in_specs = [pl.BlockSpec((tm, tk), lambda i, j, kk: (i, jnp.minimum(kk, nka - 1))),
                    pl.BlockSpec((tm, tk), lambda i, j, kk: (i, jnp.maximum(kk - nka, 0)))]
    else:
        in_specs = [pl.BlockSpec((tm, tk), lambda i, j, kk: (i, kk))]
    in_specs.append(pl.BlockSpec((None, tk, tn), lambda i, j, kk: (blk0 + j // per, kk, j % per)))
    args = a_list + [b3]
    if add is not None:
        in_specs.append(pl.BlockSpec((tm, tn), lambda i, j, kk: (i, j)))
        args.append(add)
    return _mm_call(body, name, (m // tm, n // tn, nk), in_specs, args,
                    pl.BlockSpec((tm, tn), lambda i, j, kk: (i, j)), jax.ShapeDtypeStruct((m, n), out_dtype),
                    (tm, tn), nk, dep)


def _mm_nt(a, b3, name, add=None, out_dtype=F32, blk0=0, nblk=None, dep=None):
    pair = isinstance(a, (list, tuple))
    a_list = list(a) if pair else [a]
    m, na = a_list[0].shape
    n = na * len(a_list)
    nb_all, k, nbw = b3.shape
    nblk = nb_all - blk0 if nblk is None else nblk
    assert n == nblk * nbw and na % nbw == 0
    tm = _pick(m, 2 * MM_TM if k <= MM_TM and n <= MM_TK else MM_TM, 16)
    tk = _pick(nbw, MM_TK, LANE)
    per = nbw // tk
    nk = n // tk
    tn = _pick(k, MM_TN if nk <= 2 else 2 * MM_TM, LANE)
    nka = na // tk
    na_ops = len(a_list)

    def body(*refs):
        a_refs, b_ref = refs[:na_ops], refs[na_ops]
        c_ref = refs[na_ops + 1] if add is not None else None
        o_ref = refs[na_ops + 1 + (add is not None) + (dep is not None)]
        acc = refs[-1]
        kk = pl.program_id(2)

        def part():
            av = a_refs[0][...] if not pair else jnp.where(kk < nka, a_refs[0][...], a_refs[1][...])
            return lax.dot_general(av.astype(BF), b_ref[...].astype(BF),
                                   (((1,), (1,)), ((), ())), preferred_element_type=F32)

        def finish(r):
            if add is not None:
                r = r + c_ref[...]
            o_ref[...] = r.astype(out_dtype)

        _accumulate(kk, nk, acc, part, finish)

    if pair:
        in_specs = [pl.BlockSpec((tm, tk), lambda i, j, kk: (i, jnp.minimum(kk, nka - 1))),
                    pl.BlockSpec((tm, tk), lambda i, j, kk: (i, jnp.maximum(kk - nka, 0)))]
    else:
        in_specs = [pl.BlockSpec((tm, tk), lambda i, j, kk: (i, kk))]
    in_specs.append(pl.BlockSpec((None, tn, tk), lambda i, j, kk: (blk0 + kk // per, j, kk % per)))
    args = a_list + [b3]
    if add is not None:
        in_specs.append(pl.BlockSpec((tm, tn), lambda i, j, kk: (i, j)))
        args.append(add)
    return _mm_call(body, name, (m // tm, k // tn, nk), in_specs, args,
                    pl.BlockSpec((tm, tn), lambda i, j, kk: (i, j)), jax.ShapeDtypeStruct((m, k), out_dtype),
                    (tm, tn), nk, dep)


def _rms_mm_nn(x, g, col_blk, b3, name):
    m = x.shape[0]
    _, k, n = b3.shape
    assert b3.shape[0] == 1
    tm = _pick(m, 2 * MM_TM, 16)
    tn = _pick(n, MM_TM, LANE)

    def body(x_ref, g_ref, b_ref, u_ref, o_ref):
        xv = x_ref[...]
        r = lax.rsqrt(jnp.mean(xv * xv, axis=-1, keepdims=True) + NORM_EPS)
        u = ((xv * r) * g_ref[...]).astype(BF)

        @pl.when(pl.program_id(1) == 0)
        def _():
            u_ref[...] = u

        o_ref[...] = jnp.dot(u, b_ref[0], preferred_element_type=F32)

    return pl.pallas_call(
        body, name=name, grid=(m // tm, n // tn),
        in_specs=[pl.BlockSpec((tm, k), lambda i, j: (i, col_blk)), pl.BlockSpec((1, k), lambda i, j: (0, 0)),
                  pl.BlockSpec((1, k, tn), lambda i, j: (0, 0, j))],
        out_specs=[pl.BlockSpec((tm, k), lambda i, j: (i, 0)), pl.BlockSpec((tm, tn), lambda i, j: (i, j))],
        out_shape=[jax.ShapeDtypeStruct((m, k), BF), jax.ShapeDtypeStruct((m, n), F32)],
        compiler_params=_cp("parallel", "arbitrary"),
    )(x, g, b3)


def _mm_nt_rms_bwd(a, b3, x, g, col_blk, name, dep=None):
    m, n = a.shape
    _, width, n2 = b3.shape
    assert n == n2 and b3.shape[0] == 1
    tm = _pick(m, MM_TM, 16)
    tk = _pick(n, MM_TK, LANE)
    nk = n // tk

    def body(*refs):
        a_ref, b_ref, x_ref, g_ref = refs[:4]
        dx_ref, dg_ref = refs[4 + (dep is not None):6 + (dep is not None)]
        acc = refs[-1]
        kk = pl.program_id(1)

        @pl.when((pl.program_id(0) == 0) & (kk == 0))
        def _():
            dg_ref[...] = jnp.zeros_like(dg_ref)

        def part():
            return lax.dot_general(a_ref[...], b_ref[0], (((1,), (1,)), ((), ())), preferred_element_type=F32)

        def finish(du):
            xv = x_ref[...]
            r = lax.rsqrt(jnp.mean(xv * xv, axis=-1, keepdims=True) + NORM_EPS)
            nv = xv * r
            dn = du * g_ref[...]
            dx_ref[...] = (r * (dn - nv * jnp.mean(dn * nv, axis=-1, keepdims=True))).astype(BF)
            dg_ref[...] += _rows8([jnp.sum(du * nv, axis=0, keepdims=True)], width)

        _accumulate(kk, nk, acc, part, finish)

    in_specs = [pl.BlockSpec((tm, tk), lambda i, kk: (i, kk)), pl.BlockSpec((1, width, tk), lambda i, kk: (0, 0, kk)),
                pl.BlockSpec((tm, width), lambda i, kk: (i, col_blk)), pl.BlockSpec((1, width), lambda i, kk: (0, 0))]
    args = [a, b3, x, g]
    if dep is not None:
        in_specs.append(ANY)
        args.append(dep)
    return pl.pallas_call(
        body, name=name, grid=(m // tm, nk), in_specs=in_specs,
        out_specs=[pl.BlockSpec((tm, width), lambda i, kk: (i, 0)), pl.BlockSpec((SUB, width), lambda i, kk: (0, 0))],
        out_shape=[jax.ShapeDtypeStruct((m, width), BF), jax.ShapeDtypeStruct((SUB, width), F32)],
        scratch_shapes=[pltpu.VMEM((tm, width), F32)],
        compiler_params=_cp("arbitrary", "arbitrary"),
    )(*args)


def _mm_tn(a, b, nblk, name, out_dtype=BF, dep=None, into=None, blk0=0):
    s, m = a.shape
    s2, n = b.shape
    assert s == s2 and n % nblk == 0 and (dep is None or into is None)
    nbw = n // nblk
    tm = _pick(m, MM_TN, LANE)
    tn = _pick(nbw, MM_TN, LANE)
    ts = _pick(s, MM_TS, LANE)
    per = nbw // tn
    ns = s // ts

    def body(*refs):
        a_ref, b_ref = refs[:2]
        o_ref = refs[2 + (dep is not None or into is not None)]
        acc = refs[-1]

        def part():
            return lax.dot_general(a_ref[...].astype(BF), b_ref[...].astype(BF),
                                   (((0,), (0,)), ((), ())), preferred_element_type=F32)

        def finish(r):
            o_ref[...] = r.astype(out_dtype)

        _accumulate(pl.program_id(2), ns, acc, part, finish)

    in_specs = [pl.BlockSpec((ts, tm), lambda i, j, ss: (ss, i)),
                pl.BlockSpec((ts, tn), lambda i, j, ss: (ss, j))]
    out_spec = pl.BlockSpec((None, tm, tn), lambda i, j, ss: (blk0 + j // per, i, j % per))
    if into is None:
        return _mm_call(body, name, (m // tm, n // tn, ns), in_specs, [a, b], out_spec,
                        jax.ShapeDtypeStruct((nblk, m, nbw), out_dtype), (tm, tn), ns, dep)
    assert into.shape[1:] == (m, nbw) and into.dtype == out_dtype
    return pl.pallas_call(
        body, name=name, grid=(m // tm, n // tn, ns), in_specs=in_specs + [ANY], out_specs=out_spec,
        out_shape=jax.ShapeDtypeStruct(into.shape, out_dtype), input_output_aliases={2: 0},
        scratch_shapes=[pltpu.VMEM((tm, tn), F32)] if ns > 1 else [],
        compiler_params=_cp("parallel", "parallel", "arbitrary"),
    )(a, b, into)


def _rows8(rows, width):
    idx = lax.broadcasted_iota(jnp.int32, (SUB, width), 0)
    out = jnp.zeros((SUB, width), F32)
    for r, v in enumerate(rows):
        out = jnp.where(idx == r, v, out)
    return out


def _rms_fwd(x, g, width, col_blk, name):
    s = x.shape[0]
    tr = _pick(s, ROW_TILE, 16)

    def body(x_ref, g_ref, u_ref):
        xv = x_ref[...]
        r = lax.rsqrt(jnp.mean(xv * xv, axis=-1, keepdims=True) + NORM_EPS)
        u_ref[...] = ((xv * r) * g_ref[...]).astype(BF)

    return pl.pallas_call(
        body, name=name, grid=(s // tr,),
        in_specs=[pl.BlockSpec((tr, width), lambda i: (i, col_blk)),
                  pl.BlockSpec((1, width), lambda i: (0, 0))],
        out_specs=pl.BlockSpec((tr, width), lambda i: (i, 0)),
        out_shape=jax.ShapeDtypeStruct((s, width), BF),
        compiler_params=_cp("parallel"),
    )(x, g)


def _rms_bwd(x, du, g, width, col_blk, name, extra=None, out_dtype=F32, also_bf16=False):
    s = x.shape[0]
    tr = _pick(s, ROW_TILE_BWD, 16)

    def body(*refs):
        x_ref, du_ref, g_ref = refs[:3]
        e_ref = refs[3] if extra is not None else None
        dx_ref = refs[3 + (extra is not None)]
        dxb_ref = refs[4 + (extra is not None)] if also_bf16 else None
        dg_ref = refs[-1]
        i = pl.program_id(0)
        xv = x_ref[...]
        duv = du_ref[...].astype(F32)
        r = lax.rsqrt(jnp.mean(xv * xv, axis=-1, keepdims=True) + NORM_EPS)
        nv = xv * r
        dn = duv * g_ref[...]
        dx = r * (dn - nv * jnp.mean(dn * nv, axis=-1, keepdims=True))
        if extra is not None:
            dx = dx + e_ref[...]
        dx_ref[...] = dx.astype(out_dtype)
        if also_bf16:
            dxb_ref[...] = dx.astype(BF)

        @pl.when(i == 0)
        def _():
            dg_ref[...] = jnp.zeros_like(dg_ref)

        dg_ref[...] += _rows8([jnp.sum(duv * nv, axis=0, keepdims=True)], width)

    in_specs = [pl.BlockSpec((tr, width), lambda i: (i, col_blk)),
                pl.BlockSpec((tr, width), lambda i: (i, 0)),
                pl.BlockSpec((1, width), lambda i: (0, 0))]
    args = [x, du, g]
    if extra is not None:
        in_specs.append(pl.BlockSpec((tr, width), lambda i: (i, 0)))
        args.append(extra)
    return pl.pallas_call(
        body, name=name, grid=(s // tr,),
        in_specs=in_specs,
        out_specs=[pl.BlockSpec((tr, width), lambda i: (i, 0))] * (1 + also_bf16)
        + [pl.BlockSpec((SUB, width), lambda i: (0, 0))],
        out_shape=[jax.ShapeDtypeStruct((s, width), out_dtype)] + [jax.ShapeDtypeStruct((s, width), BF)] * also_bf16
        + [jax.ShapeDtypeStruct((SUB, width), F32)],
        compiler_params=_cp("arbitrary"),
    )(*args)


def _down(cur, prev8, k):
    ext = jnp.concatenate([prev8, cur], axis=0)
    return pltpu.roll(ext, k, axis=0)[SUB:]


def _up(cur, next8, k):
    ext = jnp.concatenate([cur, next8], axis=0)
    return pltpu.roll(ext, ext.shape[0] - k, axis=0)[:cur.shape[0]]


def _lags(cur, prev8):
    return _down(cur, prev8, 1), _down(cur, prev8, 2)


def _conv3(w_ref, cur, prev8, lags=None):
    lag1, lag2 = _lags(cur, prev8) if lags is None else lags
    return w_ref[0:1, :] * lag2 + w_ref[1:2, :] * lag1 + w_ref[2:3, :] * cur


def _conv3_t(w_ref, cur, next8):
    return w_ref[2:3, :] * cur + w_ref[1:2, :] * _up(cur, next8, 1) + w_ref[0:1, :] * _up(cur, next8, 2)


def _spec_cur(tr, tc, c0):
    return pl.BlockSpec((tr, tc), lambda j, i: (i, c0 + j))


def _spec_prev(tr, tc, c0):
    return pl.BlockSpec((SUB, tc), lambda j, i: (jnp.maximum(i * (tr // SUB) - 1, 0), c0 + j))


def _spec_next(tr, tc, c0, s):
    return pl.BlockSpec((SUB, tc), lambda j, i: (jnp.minimum((i + 1) * (tr // SUB), s // SUB - 1), c0 + j))


def _spec_w(tc, c0):
    return pl.BlockSpec((SUB, tc), lambda j, i: (0, c0 + j))


def _pad8(w):
    return jnp.pad(w, ((0, SUB - w.shape[0]), (0, 0)))


def _conv_mix_fwd(z_a, cw8, conv):
    s = z_a.shape[0]
    tr = _pick(s, ROW_TILE, 16)
    tc = _pick(conv, COL_TILE, LANE)
    nc = conv // tc

    def body(zb_ref, zc_ref, zv_ref, zcp_ref, zvp_ref, w_ref, p_ref):
        i = pl.program_id(1)
        cv = zc_ref[...] * zv_ref[...]
        cvp = jnp.where(i > 0, zcp_ref[...] * zvp_ref[...], 0.0)
        p_ref[...] = (zb_ref[...] * _conv3(w_ref, cv, cvp)).astype(BF)

    return pl.pallas_call(
        body, name="conv_mix_fwd", grid=(nc, s // tr),
        in_specs=[_spec_cur(tr, tc, 0), _spec_cur(tr, tc, nc), _spec_cur(tr, tc, 2 * nc),
                  _spec_prev(tr, tc, nc), _spec_prev(tr, tc, 2 * nc), _spec_w(tc, 0)],
        out_specs=_spec_cur(tr, tc, 0),
        out_shape=jax.ShapeDtypeStruct((s, conv), BF),
        compiler_params=_cp("parallel", "parallel"),
    )(z_a, z_a, z_a, z_a, z_a, cw8)


def _conv_mix_bwd(z_a, d_p, cw8, conv):
    s = z_a.shape[0]
    tr = _pick(s, ROW_TILE_BWD, 16)
    tc = _pick(conv, COL_TILE, LANE)
    nc = conv // tc
    nr = s // tr

    def body(zb_ref, zbn_ref, zc_ref, zcp_ref, zv_ref, zvp_ref, dp_ref, dpn_ref, w_ref,
             dzb_ref, dzc_ref, dzv_ref, dw_ref):
        i = pl.program_id(1)
        zc = zc_ref[...]
        zv = zv_ref[...]
        cv = zc * zv
        cvp = jnp.where(i > 0, zcp_ref[...] * zvp_ref[...], 0.0)
        cv1, cv2 = _lags(cv, cvp)
        dpv = dp_ref[...]
        dzb_ref[...] = (dpv * _conv3(w_ref, cv, cvp, (cv1, cv2))).astype(BF)
        dcc = dpv * zb_ref[...]
        dccn = jnp.where(i < nr - 1, dpn_ref[...] * zbn_ref[...], 0.0)
        dcv = _conv3_t(w_ref, dcc, dccn)
        dzc_ref[...] = (dcv * zv).astype(BF)
        dzv_ref[...] = (dcv * zc).astype(BF)

        @pl.when(i == 0)
        def _():
            dw_ref[...] = jnp.zeros_like(dw_ref)

        dw_ref[...] += _rows8([jnp.sum(dcc * cv2, axis=0, keepdims=True),
                               jnp.sum(dcc * cv1, axis=0, keepdims=True),
                               jnp.sum(dcc * cv, axis=0, keepdims=True)], tc)

    out = jax.ShapeDtypeStruct((s, conv), BF)
    return pl.pallas_call(
        body, name="conv_mix_bwd", grid=(nc, nr),
        in_specs=[_spec_cur(tr, tc, 0), _spec_next(tr, tc, 0, s),
                  _spec_cur(tr, tc, nc), _spec_prev(tr, tc, nc),
                  _spec_cur(tr, tc, 2 * nc), _spec_prev(tr, tc, 2 * nc),
                  _spec_cur(tr, tc, 0), _spec_next(tr, tc, 0, s), _spec_w(tc, 0)],
        out_specs=[_spec_cur(tr, tc, 0), _spec_cur(tr, tc, 0), _spec_cur(tr, tc, 0), _spec_w(tc, 0)],
        out_shape=[out, out, out, jax.ShapeDtypeStruct((SUB, conv), F32)],
        compiler_params=_cp("parallel", "arbitrary"),
    )(z_a, z_a, z_a, z_a, z_a, z_a, d_p, d_p, cw8)


def _silu_parts(ag):
    sg = jax.nn.sigmoid(ag)
    return ag * sg, sg


def _ffn_up_act(u2, w_up, cw8, cb, dff):
    s, d = u2.shape
    nb, _, nbw = w_up.shape
    half = nb // 2
    assert half * nbw == dff
    tm = _pick(s, ROW_TILE, 16)

    def body(u_ref, wg_ref, wu_ref, cg_ref, cu_ref, bg_ref, bu_ref, ag_ref, au_ref, f_ref, hist_g, hist_u):
        i = pl.program_id(1)

        @pl.when(i == 0)
        def _():
            hist_g[...] = jnp.zeros_like(hist_g)
            hist_u[...] = jnp.zeros_like(hist_u)

        u = u_ref[...]
        xg = jnp.dot(u, wg_ref[...], preferred_element_type=F32)
        xu = jnp.dot(u, wu_ref[...], preferred_element_type=F32)
        ag_ref[...] = xg
        au_ref[...] = xu
        ag = _conv3(cg_ref, xg, hist_g[...]) + bg_ref[...]
        au = _conv3(cu_ref, xu, hist_u[...]) + bu_ref[...]
        f_ref[...] = (_silu_parts(ag)[0] * au).astype(BF)
        hist_g[...] = xg[tm - SUB:]
        hist_u[...] = xu[tm - SUB:]

    once = pl.Buffered(1)
    tile = pl.BlockSpec((tm, nbw), lambda j, i: (i, j))
    return pl.pallas_call(
        body, name="mm_ffn_up_act", grid=(half, s // tm),
        in_specs=[pl.BlockSpec((tm, d), lambda j, i: (i, 0)),
                  pl.BlockSpec((None, d, nbw), lambda j, i: (j, 0, 0), pipeline_mode=once),
                  pl.BlockSpec((None, d, nbw), lambda j, i: (half + j, 0, 0), pipeline_mode=once),
                  pl.BlockSpec((SUB, nbw), lambda j, i: (0, j)), pl.BlockSpec((SUB, nbw), lambda j, i: (0, half + j)),
                  pl.BlockSpec((1, nbw), lambda j, i: (0, j)), pl.BlockSpec((1, nbw), lambda j, i: (0, half + j))],
        out_specs=[tile, tile, tile],
        out_shape=[jax.ShapeDtypeStruct((s, dff), F32), jax.ShapeDtypeStruct((s, dff), F32),
                   jax.ShapeDtypeStruct((s, dff), BF)],
        scratch_shapes=[pltpu.VMEM((SUB, nbw), F32), pltpu.VMEM((SUB, nbw), F32)],
        compiler_params=_cp("arbitrary", "arbitrary"),
    )(u2, w_up, w_up, cw8, cw8, cb, cb)


def _ffn_act_bwd(a_g, a_u, d_f, cw8, cb, dff):
    s = a_g.shape[0]
    tr = _pick(s, ROW_TILE_BWD, 16)
    tc = _pick(dff, COL_TILE, LANE)
    nc = dff // tc
    nr = s // tr

    def body(xg_ref, xgp_ref, xgn_ref, xu_ref, xup_ref, xun_ref, df_ref, dfn_ref,
             wg_ref, wu_ref, bg_ref, bu_ref, dxg_ref, dxu_ref, dwg_ref, dwu_ref):
        i = pl.program_id(1)
        xg = xg_ref[...]
        xu = xu_ref[...]
        xgp = jnp.where(i > 0, xgp_ref[...], 0.0)
        xup = jnp.where(i > 0, xup_ref[...], 0.0)

        def d_act(xg_t, xgp_t, xu_t, xup_t, df_t, lags_g=None, lags_u=None):
            ag = _conv3(wg_ref, xg_t, xgp_t, lags_g) + bg_ref[...]
            au = _conv3(wu_ref, xu_t, xup_t, lags_u) + bu_ref[...]
            sil, sg = _silu_parts(ag)
            return df_t * au * (sg * (1.0 + ag * (1.0 - sg))), df_t * sil

        lags_g = _lags(xg, xgp)
        lags_u = _lags(xu, xup)
        dag, dau = d_act(xg, xgp, xu, xup, df_ref[...], lags_g, lags_u)
        dfn = jnp.where(i < nr - 1, dfn_ref[...], 0.0)
        dagn, daun = d_act(xgn_ref[...], xg[tr - SUB:], xun_ref[...], xu[tr - SUB:], dfn)
        dxg_ref[...] = _conv3_t(wg_ref, dag, dagn).astype(BF)
        dxu_ref[...] = _conv3_t(wu_ref, dau, daun).astype(BF)

        @pl.when(i == 0)
        def _():
            dwg_ref[...] = jnp.zeros_like(dwg_ref)
            dwu_ref[...] = jnp.zeros_like(dwu_ref)

        def wgrad(da, x, lags):
            return _rows8([jnp.sum(da * lags[1], axis=0, keepdims=True),
                           jnp.sum(da * lags[0], axis=0, keepdims=True),
                           jnp.sum(da * x, axis=0, keepdims=True),
                           jnp.sum(da, axis=0, keepdims=True)], tc)

        dwg_ref[...] += wgrad(dag, xg, lags_g)
        dwu_ref[...] += wgrad(dau, xu, lags_u)

    half = jax.ShapeDtypeStruct((s, dff), BF)
    wsh = jax.ShapeDtypeStruct((SUB, dff), F32)
    return pl.pallas_call(
        body, name="ffn_act_bwd", grid=(nc, nr),
        in_specs=[_spec_cur(tr, tc, 0), _spec_prev(tr, tc, 0), _spec_next(tr, tc, 0, s),
                  _spec_cur(tr, tc, 0), _spec_prev(tr, tc, 0), _spec_next(tr, tc, 0, s),
                  _spec_cur(tr, tc, 0), _spec_next(tr, tc, 0, s),
                  _spec_w(tc, 0), _spec_w(tc, nc),
                  pl.BlockSpec((1, tc), lambda j, i: (0, j)), pl.BlockSpec((1, tc), lambda j, i: (0, nc + j))],
        out_specs=[_spec_cur(tr, tc, 0), _spec_cur(tr, tc, 0), _spec_w(tc, 0), _spec_w(tc, 0)],
        out_shape=[half, half, wsh, wsh],
        compiler_params=_cp("parallel", "arbitrary"),
    )(a_g, a_g, a_g, a_u, a_u, a_u, d_f, d_f, cw8, cw8, cb, cb)


def _residual_norm(mix, w_oo, x, g):
    m, k = mix.shape
    d = w_oo.shape[2]
    tm = _pick(m, ROW_TILE, 16)

    def body(a_ref, b_ref, x_ref, g_ref, h_ref, u_ref):
        h = jnp.dot(a_ref[...], b_ref[0], preferred_element_type=F32) + x_ref[...]
        h_ref[...] = h
        r = lax.rsqrt(jnp.mean(h * h, axis=-1, keepdims=True) + NORM_EPS)
        u_ref[...] = ((h * r) * g_ref[...]).astype(BF)

    row = pl.BlockSpec((tm, d), lambda i: (i, 0))
    return pl.pallas_call(
        body, name="mm_h1_norm", grid=(m // tm,),
        in_specs=[pl.BlockSpec((tm, k), lambda i: (i, 0)),
                  pl.BlockSpec((1, k, d), lambda i: (0, 0, 0), pipeline_mode=pl.Buffered(1)),
                  row, pl.BlockSpec((1, d), lambda i: (0, 0))],
        out_specs=[row, row],
        out_shape=[jax.ShapeDtypeStruct((m, d), F32), jax.ShapeDtypeStruct((m, d), BF)],
        compiler_params=_cp("parallel"),
    )(mix, w_oo, x, g)


def _mla_out_gate(o, w_mo, z_g, b_gate, yc):
    m, k = o.shape
    d = w_mo.shape[2]
    tm = _pick(m, MM_TM, 16)
    tn = _pick(d, MM_TM, LANE)
    nc = d // tn

    def body(a_ref, b_ref, za_ref, zb_ref, ba_ref, bb_ref, yc_ref, ym_ref, mix_ref):
        ym = jnp.dot(a_ref[...], b_ref[0], preferred_element_type=F32)
        ga = jax.nn.sigmoid(za_ref[...] + ba_ref[...])
        gb = jax.nn.sigmoid(zb_ref[...] + bb_ref[...])
        ym_ref[...] = ym.astype(BF)
        mix_ref[...] = (ga * yc_ref[...] + gb * ym).astype(BF)

    tile = pl.BlockSpec((tm, tn), lambda i, j: (i, j))
    out = jax.ShapeDtypeStruct((m, d), BF)
    return pl.pallas_call(
        body, name="mm_y_mla_gate", grid=(m // tm, nc),
        in_specs=[pl.BlockSpec((tm, k), lambda i, j: (i, 0)), pl.BlockSpec((1, k, tn), lambda i, j: (0, 0, j)),
                  tile, pl.BlockSpec((tm, tn), lambda i, j: (i, nc + j)),
                  pl.BlockSpec((1, tn), lambda i, j: (0, j)), pl.BlockSpec((1, tn), lambda i, j: (0, nc + j)), tile],
        out_specs=[tile, tile], out_shape=[out, out],
        compiler_params=_cp("parallel", "parallel"),
    )(o, w_mo, z_g, z_g, b_gate, b_gate, yc)


def _d_mix_gate(d_h1, w_oo, z_g, b_gate, yc, ym):
    m, n = d_h1.shape
    d = w_oo.shape[1]
    tm = _pick(m, MM_TM, 16)
    tn = _pick(d, COL_TILE, LANE)
    nc = d // tn

    def body(a_ref, b_ref, za_ref, zb_ref, ba_ref, bb_ref, yc_ref, ym_ref,
             dza_ref, dzb_ref, dyc_ref, dym_ref, dba_ref, dbb_ref):
        i = pl.program_id(1)
        dm = lax.dot_general(a_ref[...], b_ref[0], (((1,), (1,)), ((), ())), preferred_element_type=F32)
        ga = jax.nn.sigmoid(za_ref[...] + ba_ref[...])
        gb = jax.nn.sigmoid(zb_ref[...] + bb_ref[...])
        dza = dm * yc_ref[...] * (ga * (1.0 - ga))
        dzb = dm * ym_ref[...] * (gb * (1.0 - gb))
        dza_ref[...] = dza.astype(BF)
        dzb_ref[...] = dzb.astype(BF)
        dyc_ref[...] = (dm * ga).astype(BF)
        dym_ref[...] = (dm * gb).astype(BF)

        @pl.when(i == 0)
        def _():
            dba_ref[...] = jnp.zeros_like(dba_ref)
            dbb_ref[...] = jnp.zeros_like(dbb_ref)

        dba_ref[...] += _rows8([jnp.sum(dza, axis=0, keepdims=True)], tn)
        dbb_ref[...] += _rows8([jnp.sum(dzb, axis=0, keepdims=True)], tn)

    tile = pl.BlockSpec((tm, tn), lambda j, i: (i, j))
    act = jax.ShapeDtypeStruct((m, d), BF)
    bsh = jax.ShapeDtypeStruct((SUB, d), F32)
    return pl.pallas_call(
        body, name="mm_d_mix_gate", grid=(nc, m // tm),
        in_specs=[pl.BlockSpec((tm, n), lambda j, i: (i, 0)), pl.BlockSpec((1, tn, n), lambda j, i: (0, j, 0)),
                  tile, pl.BlockSpec((tm, tn), lambda j, i: (i, nc + j)),
                  pl.BlockSpec((1, tn), lambda j, i: (0, j)), pl.BlockSpec((1, tn), lambda j, i: (0, nc + j)),
                  tile, tile],
        out_specs=[tile] * 4 + [pl.BlockSpec((SUB, tn), lambda j, i: (0, j))] * 2,
        out_shape=[act, act, act, act, bsh, bsh],
        compiler_params=_cp("parallel", "arbitrary"),
    )(d_h1, w_oo, z_g, z_g, b_gate, b_gate, yc, ym)


def _lay(v):
    z = jnp.zeros(v.shape[:-1] + (HALF,), v.dtype)
    return jnp.concatenate([v[..., :HALF], z, v[..., HALF:], z], axis=-1)


def _unlay(v):
    return jnp.concatenate([v[..., :HALF], v[..., 2 * HALF:3 * HALF]], axis=-1)


def _lay_rows(v):
    z = jnp.zeros((HALF,) + v.shape[1:], v.dtype)
    return jnp.concatenate([v[:HALF], z, v[HALF:], z], axis=0)


def _rope_tables(positions):
    s = positions.shape[0]
    tr = _pick(s, ROW_TILE, 8)
    inv_freq = ROPE_THETA ** (-jnp.arange(0, ROPE, 2, dtype=F32) / ROPE)
    consts = jnp.stack([_lay(jnp.concatenate([inv_freq, inv_freq])),
                        _lay(jnp.ones((ROPE,), F32)),
                        _lay(jnp.concatenate([-jnp.ones((HALF,), F32), jnp.ones((HALF,), F32)]))])
    consts = _pad8(consts)

    def body(p_ref, c_ref, cos_ref, sin_ref):
        ang = p_ref[...].astype(F32) * c_ref[0:1, :]
        cos_ref[...] = jnp.cos(ang) * c_ref[1:2, :]
        sin_ref[...] = jnp.sin(ang) * c_ref[2:3, :]

    tab = jax.ShapeDtypeStruct((s, LANE), F32)
    return pl.pallas_call(
        body, name="rope_tables", grid=(s // tr,),
        in_specs=[pl.BlockSpec((tr, 1), lambda i: (i, 0)), pl.BlockSpec((SUB, LANE), lambda i: (0, 0))],
        out_specs=[pl.BlockSpec((tr, LANE), lambda i: (i, 0))] * 2,
        out_shape=[tab, tab],
        compiler_params=_cp("parallel"),
    )(positions, consts)


def _lane_sum(p):
    return jnp.sum(p, axis=-1, keepdims=True)


def _rope(t, cos, sin):
    return t * cos + pltpu.roll(t, 2 * HALF, axis=1) * sin


def _rope_t(d, cos, sin):
    return d * cos + pltpu.roll(d * sin, 2 * HALF, axis=1)


def _head_fwd(q_raw, kv_raw, z_a, kr_blk, cos, sin, gains, heads):
    s = q_raw.shape[0]
    tr = _pick(s, HEAD_ROW_TILE, 16)
    hw = heads * LANE

    def body(q_ref, kv_ref, kr_ref, cos_ref, sin_ref, g_ref, qo_ref, ko_ref, vo_ref):
        cosv = cos_ref[...]
        sinv = sin_ref[...]
        krv = kr_ref[...]
        kr_sq = krv * krv
        for h in range(heads):
            lo = h * LANE
            qn = q_ref[:, lo:lo + LANE]
            qr = q_ref[:, hw + lo:hw + lo + LANE]
            r = lax.rsqrt(_lane_sum(qn * qn + qr * qr) / HEAD_QK + NORM_EPS)
            qo_ref[:, 2 * lo:2 * lo + LANE] = (((qn * r) * g_ref[0:1, :]) * (QK_SCALE * LOG2_E)).astype(BF)
            qo_ref[:, 2 * lo + LANE:2 * lo + 2 * LANE] = (
                _rope((qr * r) * g_ref[1:2, :], cosv, sinv) * (QK_SCALE * LOG2_E)).astype(BF)
            kn = kv_ref[:, 2 * lo:2 * lo + LANE]
            r = lax.rsqrt(_lane_sum(kn * kn + kr_sq) / HEAD_QK + NORM_EPS)
            ko_ref[:, 2 * lo:2 * lo + LANE] = ((kn * r) * g_ref[2:3, :]).astype(BF)
            ko_ref[:, 2 * lo + LANE:2 * lo + 2 * LANE] = _rope((krv * r) * g_ref[3:4, :], cosv, sinv).astype(BF)
            vo_ref[:, lo:lo + LANE] = kv_ref[:, 2 * lo + LANE:2 * lo + 2 * LANE].astype(BF)

    row = lambda w: pl.BlockSpec((tr, w), lambda i: (i, 0))
    return pl.pallas_call(
        body, name="head_fwd", grid=(s // tr,),
        in_specs=[row(2 * hw), row(2 * hw), pl.BlockSpec((tr, LANE), lambda i: (i, kr_blk)),
                  row(LANE), row(LANE), pl.BlockSpec((SUB, LANE), lambda i: (0, 0))],
        out_specs=[row(2 * hw), row(2 * hw), row(hw)],
        out_shape=[jax.ShapeDtypeStruct((s, 2 * hw), BF), jax.ShapeDtypeStruct((s, 2 * hw), BF),
                   jax.ShapeDtypeStruct((s, hw), BF)],
        compiler_params=_cp("parallel"),
    )(q_raw, kv_raw, z_a, cos, sin, gains)


def _head_bwd(q_raw, kv_raw, z_a, kr_blk, cos, sin, gains, dq_att, dk_att, dv, heads):
    s = q_raw.shape[0]
    tr = _pick(s, HEAD_ROW_TILE_BWD, 16)
    hw = heads * LANE

    def body(q_ref, kv_ref, kr_ref, cos_ref, sin_ref, g_ref, dq_ref, dk_ref, dv_ref,
             dqr_ref, dkv_ref, dkr_ref, dg_ref):
        i = pl.program_id(0)
        cosv = cos_ref[...]
        sinv = sin_ref[...]
        krv = kr_ref[...]
        kr_sq = krv * krv
        dkr = jnp.zeros((tr, LANE), F32)
        dgs = [jnp.zeros((1, LANE), F32) for _ in range(4)]

        def norm_bwd(xn, xr, sq, dn_out, dr_out, gn, gr):
            r = lax.rsqrt(_lane_sum(sq) / HEAD_QK + NORM_EPS)
            nn = xn * r
            nr = xr * r
            dt = _rope_t(dr_out, cosv, sinv)
            dnn = dn_out * gn
            dnr = dt * gr
            mean = _lane_sum(dnn * nn + dnr * nr) / HEAD_QK
            return (r * (dnn - nn * mean), r * (dnr - nr * mean),
                    jnp.sum(dn_out * nn, axis=0, keepdims=True), jnp.sum(dt * nr, axis=0, keepdims=True))

        for h in range(heads):
            lo = h * LANE
            qn = q_ref[:, lo:lo + LANE]
            qr = q_ref[:, hw + lo:hw + lo + LANE]
            dxn, dxr, g0, g1 = norm_bwd(qn, qr, qn * qn + qr * qr, dq_ref[:, 2 * lo:2 * lo + LANE] * QK_SCALE,
                                        dq_ref[:, 2 * lo + LANE:2 * lo + 2 * LANE] * QK_SCALE,
                                        g_ref[0:1, :], g_ref[1:2, :])
            dqr_ref[:, lo:lo + LANE] = dxn.astype(BF)
            dqr_ref[:, hw + lo:hw + lo + LANE] = dxr.astype(BF)
            kn = kv_ref[:, 2 * lo:2 * lo + LANE]
            dxn, dxr, g2, g3 = norm_bwd(kn, krv, kn * kn + kr_sq, dk_ref[:, 2 * lo:2 * lo + LANE],
                                        dk_ref[:, 2 * lo + LANE:2 * lo + 2 * LANE], g_ref[2:3, :], g_ref[3:4, :])
            dkv_ref[:, 2 * lo:2 * lo + LANE] = dxn.astype(BF)
            dkv_ref[:, 2 * lo + LANE:2 * lo + 2 * LANE] = dv_ref[:, lo:lo + LANE].astype(BF)
            dkr = dkr + dxr
            dgs = [a + b for a, b in zip(dgs, (g0, g1, g2, g3))]
        dkr_ref[...] = dkr

        @pl.when(i == 0)
        def _():
            dg_ref[...] = jnp.zeros_like(dg_ref)

        dg_ref[...] += _rows8(dgs, LANE)

    row = lambda w: pl.BlockSpec((tr, w), lambda i: (i, 0))
    return pl.pallas_call(
        body, name="head_bwd", grid=(s // tr,),
        in_specs=[row(2 * hw), row(2 * hw), pl.BlockSpec((tr, LANE), lambda i: (i, kr_blk)),
                  row(LANE), row(LANE), pl.BlockSpec((SUB, LANE), lambda i: (0, 0)),
                  row(2 * hw), row(2 * hw), row(hw)],
        out_specs=[row(2 * hw), row(2 * hw), row(LANE), pl.BlockSpec((SUB, LANE), lambda i: (0, 0))],
        out_shape=[jax.ShapeDtypeStruct((s, 2 * hw), BF), jax.ShapeDtypeStruct((s, 2 * hw), BF),
                   jax.ShapeDtypeStruct((s, LANE), F32), jax.ShapeDtypeStruct((SUB, LANE), F32)],
        compiler_params=_cp("arbitrary"),
    )(q_raw, kv_raw, z_a, cos, sin, gains, dq_att, dk_att, dv)


def _causal_mask(nrows, ncols, row0):
    rows = lax.broadcasted_iota(jnp.int32, (nrows, ncols), 0) + row0
    cols = lax.broadcasted_iota(jnp.int32, (nrows, ncols), 1)
    return cols <= rows


def _causal_steps(nt, q_major):
    pairs = ([(i, j) for i in range(nt) for j in range(i + 1)] if q_major
             else [(i, j) for j in range(nt) for i in range(j, nt)])
    return (jnp.array([p[0] for p in pairs], jnp.int32), jnp.array([p[1] for p in pairs], jnp.int32))


def _attn_fwd(q_att, k_att, v, heads):
    s = q_att.shape[0]
    t = _pick(s, ATTN_TILE_FWD, LANE)
    nt = s // t
    th = t // 2
    qi, kj = _causal_steps(nt, True)

    def body(qi_ref, kj_ref, q_ref, k_ref, v_ref, o_ref, ob_ref, lse_ref, m_s, l_s, acc_s):
        st = pl.program_id(1)
        i = qi_ref[st]
        j = kj_ref[st]

        @pl.when(j == 0)
        def _():
            m_s[...] = jnp.full_like(m_s, NEG_INF)
            l_s[...] = jnp.zeros_like(l_s)
            acc_s[...] = jnp.zeros_like(acc_s)

        def update(rows, ncol, masked):
            sc = lax.dot_general(q_ref[rows, :], k_ref[0:ncol, :], (((1,), (1,)), ((), ())),
                                 preferred_element_type=F32)
            if masked:
                sc = jnp.where(_causal_mask(rows.stop - rows.start, ncol, rows.start), sc, NEG_INF)
            m_prev = m_s[rows, :]
            m_new = jnp.maximum(m_prev, jnp.max(sc, axis=-1, keepdims=True))
            alpha = jnp.exp2(m_prev - m_new)
            p = jnp.exp2(sc - jnp.tile(m_new, (1, ncol // LANE)))
            l_s[rows, :] = alpha * l_s[rows, :] + jnp.sum(p, axis=-1, keepdims=True)
            acc_s[rows, :] = alpha * acc_s[rows, :] + jnp.dot(p.astype(BF), v_ref[0:ncol, :],
                                                              preferred_element_type=F32)
            m_s[rows, :] = m_new

        @pl.when(j < i)
        def _():
            update(slice(0, t), t, False)

        @pl.when(j == i)
        def _():
            update(slice(0, th), th, True)
            update(slice(th, t), t, True)
            o = acc_s[...] / l_s[...]
            o_ref[...] = o
            ob_ref[...] = o.astype(BF)
            lse_ref[...] = (m_s[...] + jnp.log2(l_s[...]))[:, 0:1]

    q_idx = lambda h, st, qi_r, kj_r: (qi_r[st], h)
    kv_idx = lambda h, st, qi_r, kj_r: (kj_r[st], h)
    return pl.pallas_call(
        body, name="attn_fwd",
        grid_spec=pltpu.PrefetchScalarGridSpec(
            num_scalar_prefetch=2, grid=(heads, qi.shape[0]),
            in_specs=[pl.BlockSpec((t, 2 * LANE), q_idx), pl.BlockSpec((t, 2 * LANE), kv_idx),
                      pl.BlockSpec((t, LANE), kv_idx)],
            out_specs=[pl.BlockSpec((t, LANE), q_idx), pl.BlockSpec((t, LANE), q_idx),
                       pl.BlockSpec((None, t, 1), lambda h, st, qi_r, kj_r: (h, qi_r[st], 0))],
            scratch_shapes=[pltpu.VMEM((t, LANE), F32), pltpu.VMEM((t, LANE), F32), pltpu.VMEM((t, LANE), F32)]),
        out_shape=[jax.ShapeDtypeStruct((s, heads * LANE), F32), jax.ShapeDtypeStruct((s, heads * LANE), BF),
                   jax.ShapeDtypeStruct((heads, s, 1), F32)],
        compiler_params=_cp("parallel", "arbitrary"),
    )(qi, kj, q_att, k_att, v)


def _attn_bwd(q_att, k_att, v, o, lse, d_o, heads, dep=None):
    s = q_att.shape[0]
    t = _pick(s, ATTN_TILE, LANE)
    nt = s // t
    th = t // 2
    qi, kj = _causal_steps(nt, False)

    def body(qi_ref, kj_ref, q_ref, k_ref, v_ref, do_ref, o_ref, lse_ref, *rest):
        dq_ref, dk_ref, dv_ref, dk_s, dv_s = rest[-5:]
        st = pl.program_id(1)
        i = qi_ref[st]
        j = kj_ref[st]

        @pl.when(st == 0)
        def _():
            dq_ref[...] = jnp.zeros_like(dq_ref)

        @pl.when(i == j)
        def _():
            dk_s[...] = jnp.zeros_like(dk_s)
            dv_s[...] = jnp.zeros_like(dv_s)

        def update(rows, ncol, masked):
            nrow = rows.stop - rows.start
            q = q_ref[rows, :]
            k = k_ref[0:ncol, :]
            do = do_ref[rows, :]
            sc = lax.dot_general(q, k, (((1,), (1,)), ((), ())), preferred_element_type=F32)
            if masked:
                sc = jnp.where(_causal_mask(nrow, ncol, rows.start), sc, NEG_INF)
            p = jnp.exp2(sc - lse_ref[rows, :])
            dp = lax.dot_general(do, v_ref[0:ncol, :], (((1,), (1,)), ((), ())), preferred_element_type=F32)
            delta = jnp.sum(do.astype(F32) * o_ref[rows, :], axis=-1, keepdims=True)
            ds = (p * (dp - delta)).astype(BF)
            dv_s[0:ncol, :] += lax.dot_general(p.astype(BF), do, (((0,), (0,)), ((), ())),
                                               preferred_element_type=F32)
            dk_s[0:ncol, :] += lax.dot_general(ds, q, (((0,), (0,)), ((), ())), preferred_element_type=F32)
            out_rows = pl.ds(pl.multiple_of(i * t + rows.start, nrow), nrow)
            dq_ref[out_rows, :] += jnp.dot(ds, k, preferred_element_type=F32)

        @pl.when(i > j)
        def _():
            update(slice(0, t), t, False)

        @pl.when(i == j)
        def _():
            update(slice(0, th), th, True)
            update(slice(th, t), t, True)

        @pl.when(i == nt - 1)
        def _():
            dk_ref[...] = (dk_s[...] * (1.0 / LOG2_E)).astype(BF)
            dv_ref[...] = dv_s[...].astype(BF)

    q_idx = lambda h, st, qi_r, kj_r: (qi_r[st], h)
    kv_idx = lambda h, st, qi_r, kj_r: (kj_r[st], h)
    in_specs = [pl.BlockSpec((t, 2 * LANE), q_idx), pl.BlockSpec((t, 2 * LANE), kv_idx),
                pl.BlockSpec((t, LANE), kv_idx), pl.BlockSpec((t, LANE), q_idx), pl.BlockSpec((t, LANE), q_idx),
                pl.BlockSpec((None, t, 1), lambda h, st, qi_r, kj_r: (h, qi_r[st], 0))]
    args = [q_att, k_att, v, d_o, o, lse]
    if dep is not None:
        in_specs.append(ANY)
        args.append(dep)
    return pl.pallas_call(
        body, name="attn_bwd",
        grid_spec=pltpu.PrefetchScalarGridSpec(
            num_scalar_prefetch=2, grid=(heads, qi.shape[0]),
            in_specs=in_specs,
            out_specs=[pl.BlockSpec((s, 2 * LANE), lambda h, st, qi_r, kj_r: (0, h)),
                       pl.BlockSpec((t, 2 * LANE), kv_idx), pl.BlockSpec((t, LANE), kv_idx)],
            scratch_shapes=[pltpu.VMEM((t, 2 * LANE), F32), pltpu.VMEM((t, LANE), F32)]),
        out_shape=[jax.ShapeDtypeStruct((s, heads * 2 * LANE), F32),
                   jax.ShapeDtypeStruct((s, heads * 2 * LANE), BF),
                   jax.ShapeDtypeStruct((s, heads * LANE), BF)],
        compiler_params=_cp("parallel", "arbitrary"),
    )(qi, kj, *args)


def _sum_parts(parts, name):
    n, r, c = parts.shape
    tr = _pick(r, 512, 8)

    def body(p_ref, o_ref):
        g = p_ref[0].astype(F32)
        for k in range(1, n):
            g = g + p_ref[k].astype(F32)
        o_ref[...] = g

    return pl.pallas_call(
        body, name=name, grid=(r // tr,),
        in_specs=[pl.BlockSpec((n, tr, c), lambda i: (0, i, 0))],
        out_specs=pl.BlockSpec((tr, c), lambda i: (i, 0)),
        out_shape=jax.ShapeDtypeStruct((r, c), F32),
        compiler_params=_cp("parallel"),
    )(parts)


def _adamw(parts, w, m, v, name, by_cols=False):
    n, rp, c = parts.shape
    r = w.shape[0]
    assert by_cols or rp == r
    tr, tc = (r, _pick(c, 256, LANE)) if by_cols else (_pick(r, 256, 16 if r % 16 == 0 else 8), c)

    def body(p_ref, w_ref, m_ref, v_ref, g_ref, d_ref, mo_ref, vo_ref):
        g = p_ref[0].astype(F32)
        for k in range(1, n):
            g = g + p_ref[k].astype(F32)
        g = g[:r] if by_cols else g
        m_new = ADAM_B1 * m_ref[...] + (1.0 - ADAM_B1) * g
        v_new = ADAM_B2 * v_ref[...] + (1.0 - ADAM_B2) * jnp.square(g)
        m_hat = m_new / (1.0 - ADAM_B1 ** ADAM_STEP)
        v_hat = v_new / (1.0 - ADAM_B2 ** ADAM_STEP)
        g_ref[...] = g
        d_ref[...] = -ADAM_LR * (m_hat / (jnp.sqrt(v_hat) + ADAM_EPS) + ADAM_WD * w_ref[...])
        mo_ref[...] = m_new
        vo_ref[...] = v_new

    idx = (lambda i: (0, i)) if by_cols else (lambda i: (i, 0))
    spec = pl.BlockSpec((tr, tc), idx)
    sh = jax.ShapeDtypeStruct((r, c), F32)
    return pl.pallas_call(
        body, name=name, grid=(c // tc if by_cols else r // tr,),
        in_specs=[pl.BlockSpec((n, rp if by_cols else tr, tc), lambda i: (0,) + idx(i)), spec, spec, spec],
        out_specs=[spec] * 4, out_shape=[sh] * 4,
        compiler_params=_cp("parallel"),
    )(parts, w, m, v)


def _place():
    x, y, c = lax.axis_index("x"), lax.axis_index("y"), lax.axis_index("c")
    chips = [(1 - x, y), (x, 1 - y), (1 - x, 1 - y)]
    return x, y, c, chips


def _all_gather(shards, name, dep=None):
    n = len(shards)
    deps = [] if dep is None else list(dep)

    def body(*refs):
        ins, outs = refs[:n], refs[n + len(deps):2 * n + len(deps)]
        send_sems, recv_sems, local_sems = refs[2 * n + len(deps):]
        x, y, c, chips = _place()
        me, sibling = (x, y, c), (x, y, 1 - c)

        def slot(w, p):
            return outs[w].at[4 * p[0] + 2 * p[1] + p[2]]

        def copy(w, k, block, to, src=None):
            return pltpu.make_async_remote_copy(
                src_ref=slot(w, block) if src is None else src, dst_ref=slot(w, block),
                send_sem=send_sems.at[w, k], recv_sem=recv_sems.at[w, k], device_id=to, device_id_type=MESH)

        first = []
        for w in range(n):
            first += [copy(w, 1 + j, me, (*chip, c), src=ins[w]) for j, chip in enumerate(chips)]
            first.append(copy(w, 0, me, sibling, src=ins[w]))
        for cp in first:
            cp.start()
        mine = [pltpu.make_async_copy(ins[w], slot(w, me), local_sems.at[w]) for w in range(n)]
        for cp in mine:
            cp.start()
        passed = []
        for w in range(n):
            for j, chip in enumerate(chips):
                copy(w, 1 + j, (*chip, c), me).wait_recv()
                cp = copy(w, 4 + j, (*chip, c), sibling)
                cp.start()
                passed.append(cp)
        for w in range(n):
            copy(w, 0, sibling, me).wait_recv()
            for j, chip in enumerate(chips):
                copy(w, 4 + j, (*chip, 1 - c), me).wait_recv()
        for cp in first + passed:
            cp.wait_send()
        for cp in mine:
            cp.wait()

    return pl.pallas_call(
        body, name=name,
        in_specs=[ANY] * (n + len(deps)), out_specs=[ANY] * n,
        out_shape=[jax.ShapeDtypeStruct((N_DEV,) + a.shape, a.dtype) for a in shards],
        scratch_shapes=[pltpu.SemaphoreType.DMA((n, 7)), pltpu.SemaphoreType.DMA((n, 7)),
                        pltpu.SemaphoreType.DMA((n,))],
    )(*shards, *deps)


HBM = pl.BlockSpec(memory_space=pltpu.HBM)
SEM = pl.BlockSpec(memory_space=pltpu.SEMAPHORE)
EFFECT = pltpu.SideEffectType.DATAFLOW_SIDE_EFFECTING
PEERS = [(dx, dy, dc) for dx in (1, 0) for dy in (1, 0) for dc in (0, 1) if (dx, dy, dc) != (0, 0, 0)]


def _peer(x, y, c, flip):
    dx, dy, dc = flip
    return (1 - x if dx else x, 1 - y if dy else y, 1 - c if dc else c)


def _exchange_copies(srcs, lands, send, recv, loc, gather):
    x, y, c, _ = _place()
    me = 4 * x + 2 * y + c
    remote, local = [], []
    for w in range(len(srcs)):
        for k, flip in enumerate(PEERS):
            px, py, pc = _peer(x, y, c, flip)
            src = srcs[w] if gather else srcs[w].at[4 * px + 2 * py + pc]
            remote.append(pltpu.make_async_remote_copy(
                src_ref=src, dst_ref=lands[w].at[me], send_sem=send[w].at[k], recv_sem=recv[w].at[k],
                device_id=(px, py, pc), device_id_type=MESH))
        local.append(pltpu.make_async_copy(srcs[w] if gather else srcs[w].at[me], lands[w].at[me], loc[w]))
    return remote, local


class _Exchange:
    def __init__(self, srcs, lands, send, recv, loc, token, gather):
        self.srcs, self.lands, self.send, self.recv, self.loc = srcs, lands, send, recv, loc
        self.token, self.gather = token, gather


def _exchange_start(srcs, gather, name, dep=None):
    n = len(srcs)
    deps = [] if dep is None else [dep]
    land_shapes = [((N_DEV,) + a.shape) if gather else a.shape for a in srcs]
    lands = [pltpu.with_memory_space_constraint(lax.empty(sh, a.dtype), pltpu.HBM) for sh, a in zip(land_shapes, srcs)]
    srcs = [pltpu.with_memory_space_constraint(a, pltpu.HBM) for a in srcs]

    def body(*refs):
        src_refs, land_refs = refs[:n], refs[n:2 * n]
        outs = refs[2 * n + len(deps):]
        send, recv, loc = outs[:n], outs[n:2 * n], outs[2 * n:3 * n]
        token = outs[-1]
        remote, local = _exchange_copies(src_refs, land_refs, send, recv, loc, gather)
        for cp in remote + local:
            cp.start()
        token[...] = jnp.zeros_like(token)

    out_shape = ([pltpu.SemaphoreType.DMA((len(PEERS),))] * (2 * n) + [pltpu.SemaphoreType.DMA(())] * n
                 + [pltpu.HBM(a.shape, a.dtype) for a in srcs] + [pltpu.HBM(a.shape, a.dtype) for a in lands]
                 + [jax.ShapeDtypeStruct((SUB, LANE), F32)])
    res = pl.pallas_call(
        body, name=name, out_shape=out_shape,
        in_specs=[HBM] * (2 * n) + [ANY] * len(deps),
        out_specs=[SEM] * (3 * n) + [HBM] * (2 * n) + [pl.BlockSpec(memory_space=pltpu.VMEM)],
        input_output_aliases={i: 3 * n + i for i in range(2 * n)},
        compiler_params=pltpu.CompilerParams(has_side_effects=EFFECT),
    )(*srcs, *lands, *deps)
    return _Exchange(res[3 * n:4 * n], res[4 * n:5 * n], res[:n], res[n:2 * n], res[2 * n:3 * n], res[-1], gather)


def _exchange_wait(ex, idxs, after, name):
    n = len(idxs)
    srcs = [ex.srcs[i] for i in idxs]
    lands = [ex.lands[i] for i in idxs]
    sems = [ex.send[i] for i in idxs] + [ex.recv[i] for i in idxs] + [ex.loc[i] for i in idxs]
    gather = ex.gather

    def body(*refs):
        src_refs, land_refs = refs[:n], refs[n:2 * n]
        send, recv, loc = refs[2 * n:3 * n], refs[3 * n:4 * n], refs[4 * n:5 * n]
        remote, local = _exchange_copies(src_refs, land_refs, send, recv, loc, gather)
        for cp in remote:
            cp.wait_send()
            cp.wait_recv()
        for cp in local:
            cp.wait()

    res = pl.pallas_call(
        body, name=name,
        out_shape=[pltpu.HBM(a.shape, a.dtype) for a in srcs] + [pltpu.HBM(a.shape, a.dtype) for a in lands],
        in_specs=[HBM] * (2 * n) + [SEM] * (3 * n) + [ANY],
        out_specs=[HBM] * (2 * n),
        input_output_aliases={i: i for i in range(2 * n)},
        compiler_params=pltpu.CompilerParams(has_side_effects=EFFECT),
    )(*srcs, *lands, *sems, after)
    return res[n:]


def _gather2_copies(srcs, lands, send, recv_ici, recv_sib, loc):
    x, y, c, chips = _place()
    me = 4 * x + 2 * y + c
    remote, local = [], []
    for w in range(len(srcs)):
        remote.append(pltpu.make_async_remote_copy(
            src_ref=srcs[w], dst_ref=lands[w].at[me], send_sem=send[w].at[0], recv_sem=recv_sib[w],
            device_id=(x, y, 1 - c), device_id_type=MESH))
        for j, chip in enumerate(chips):
            remote.append(pltpu.make_async_remote_copy(
                src_ref=srcs[w], dst_ref=lands[w].at[me], send_sem=send[w].at[1 + j], recv_sem=recv_ici[w].at[j],
                device_id=(*chip, c), device_id_type=MESH))
        local.append(pltpu.make_async_copy(srcs[w], lands[w].at[me], loc[w]))
    return remote, local


def _gather2_forwards(lands, fsend, frecv, arrived=None):
    x, y, c, chips = _place()
    cps = []
    for w in range(len(lands)):
        for j, chip in enumerate(chips):
            slot = lands[w].at[4 * chip[0] + 2 * chip[1] + c]
            cp = pltpu.make_async_remote_copy(
                src_ref=slot, dst_ref=slot, send_sem=fsend[w].at[j], recv_sem=frecv[w].at[j],
                device_id=(x, y, 1 - c), device_id_type=MESH)
            if arrived is not None:
                pltpu.make_async_remote_copy(
                    src_ref=slot, dst_ref=slot, send_sem=fsend[w].at[j], recv_sem=arrived[w].at[j],
                    device_id=(x, y, 1 - c), device_id_type=MESH).wait_recv()
            cps.append(cp)
    return cps


def _gather2(shards, between, name):
    n = len(shards)
    srcs = [pltpu.with_memory_space_constraint(a, pltpu.HBM) for a in shards]
    lands = [pltpu.with_memory_space_constraint(lax.empty((N_DEV,) + a.shape, a.dtype), pltpu.HBM) for a in shards]
    hbm_like = lambda arrs: [pltpu.HBM(a.shape, a.dtype) for a in arrs]
    tok = jax.ShapeDtypeStruct((SUB, LANE), F32)
    vmem = pl.BlockSpec(memory_space=pltpu.VMEM)
    side = pltpu.CompilerParams(has_side_effects=EFFECT)

    def start(*refs):
        src_refs, land_refs = refs[:n], refs[n:2 * n]
        outs = refs[2 * n:]
        send, recv_ici, recv_sib, loc = outs[:n], outs[n:2 * n], outs[2 * n:3 * n], outs[3 * n:4 * n]
        remote, local = _gather2_copies(src_refs, land_refs, send, recv_ici, recv_sib, loc)
        for cp in remote + local:
            cp.start()
        outs[-1][...] = jnp.zeros((SUB, LANE), F32)

    res = pl.pallas_call(
        start, name=name + "_start",
        out_shape=([pltpu.SemaphoreType.DMA((4,))] * n + [pltpu.SemaphoreType.DMA((3,))] * n
                   + [pltpu.SemaphoreType.DMA(())] * (2 * n) + hbm_like(srcs) + hbm_like(lands) + [tok]),
        in_specs=[HBM] * (2 * n), out_specs=[SEM] * (4 * n) + [HBM] * (2 * n) + [vmem],
        input_output_aliases={i: 4 * n + i for i in range(2 * n)}, compiler_params=side,
    )(*srcs, *lands)
    send, recv_ici, recv_sib, loc = res[:n], res[n:2 * n], res[2 * n:3 * n], res[3 * n:4 * n]
    srcs, lands, token = res[4 * n:5 * n], res[5 * n:6 * n], res[-1]

    done = between(token)
    after = jax.tree_util.tree_leaves(done)

    def forward(*refs):
        land_refs, arrived = refs[:n], refs[n:2 * n]
        outs = refs[2 * n + len(after):]
        fsend, frecv = outs[:n], outs[n:2 * n]
        for cp in _gather2_forwards(land_refs, fsend, frecv, arrived):
            cp.start()
        outs[-1][...] = jnp.zeros((SUB, LANE), F32)

    res = pl.pallas_call(
        forward, name=name + "_forward",
        out_shape=[pltpu.SemaphoreType.DMA((3,))] * (2 * n) + hbm_like(lands) + [tok],
        in_specs=[HBM] * n + [SEM] * n + [ANY] * len(after), out_specs=[SEM] * (2 * n) + [HBM] * n + [vmem],
        input_output_aliases={i: 2 * n + i for i in range(n)}, compiler_params=side,
    )(*lands, *recv_ici, *after)
    fsend, frecv, lands, token = res[:n], res[n:2 * n], res[2 * n:3 * n], res[-1]

    def wait(*refs):
        src_refs, land_refs = refs[:n], refs[n:2 * n]
        sems = refs[2 * n:7 * n]
        send, recv_sib, loc, fsend, frecv = (sems[k * n:(k + 1) * n] for k in range(5))
        remote, local = _gather2_copies(src_refs, land_refs, send, send, recv_sib, loc)
        for w in range(n):
            for cp in remote[4 * w:4 * w + 4]:
                cp.wait_send()
            remote[4 * w].wait_recv()
        for cp in local:
            cp.wait()
        for cp in _gather2_forwards(land_refs, fsend, frecv):
            cp.wait_send()
            cp.wait_recv()

    res = pl.pallas_call(
        wait, name=name + "_wait", out_shape=hbm_like(srcs) + hbm_like(lands),
        in_specs=[HBM] * (2 * n) + [SEM] * (5 * n) + [ANY], out_specs=[HBM] * (2 * n),
        input_output_aliases={i: i for i in range(2 * n)}, compiler_params=side,
    )(*srcs, *lands, *send, *recv_sib, *loc, *fsend, *frecv, token)
    return res[n:], done


def _after(token, a):
    return a + token[0:1, 0:1].astype(a.dtype)


def _unblock(w3):
    nb, k, nbw = w3.shape
    return w3.transpose(1, 0, 2).reshape(k, nb * nbw)


def _block(w, nb):
    k, n = w.shape
    return w.reshape(k, nb, n // nb).transpose(1, 0, 2)


def kernel(x, positions, ln1_g, w_in, b_gate, conv_w, w_conv_out, q_a_g, w_q_b, kv_a_g, w_kv_b, q_norm_g, k_norm_g, w_mla_out, w_o, ln2_g, w_ffn_up, ffn_conv_w, ffn_conv_b, w_ffn_down, loss_target, m_ln1_g, m_w_in, m_b_gate, m_conv_w, m_w_conv_out, m_q_a_g, m_w_q_b, m_kv_a_g, m_w_kv_b, m_q_norm_g, m_k_norm_g, m_w_mla_out, m_w_o, m_ln2_g, m_w_ffn_up, m_ffn_conv_w, m_ffn_conv_b, m_w_ffn_down, v_ln1_g, v_w_in, v_b_gate, v_conv_w, v_w_conv_out, v_q_a_g, v_w_q_b, v_kv_a_g, v_w_kv_b, v_q_norm_g, v_k_norm_g, v_w_mla_out, v_w_o, v_ln2_g, v_w_ffn_up, v_ffn_conv_w, v_ffn_conv_b, v_w_ffn_down):
    s, d = x.shape[1], x.shape[2]
    conv = conv_w.shape[2] * N_DEV
    ql, kvl = q_a_g.shape[1], kv_a_g.shape[1]
    heads = w_q_b.shape[2] * N_DEV // HEAD_QK
    dff = w_ffn_down.shape[1] * N_DEV
    hw = heads * LANE
    conv3 = 3 * conv
    kr_off = conv3 + ql
    kv_off = -(-(kr_off + LANE) // kvl) * kvl
    wa = kv_off + kvl
    assert conv3 % ql == 0 and kr_off % LANE == 0
    xs = x[0]
    tgt = loss_target[0]
    pos = positions.reshape(s, 1)

    nin = w_in.shape[2]
    big = dict(w_in=w_in[0].T, w_conv_out=w_conv_out[0], w_q_b=w_q_b[0], w_kv_b=w_kv_b[0],
               w_mla_out=w_mla_out[0], w_o=w_o[0], w_ffn_up=w_ffn_up[0], w_ffn_down=w_ffn_down[0])
    names = list(big)
    rest = names[1:]
    early = {}

    def while_w_in_travels(token):
        early["ag"] = _exchange_start([big[k].astype(BF) for k in rest], True, "gather_rest_start", dep=token)
        cos_sin = _rope_tables(pos)
        return cos_sin, _rms_fwd(xs, _after(early["ag"].token, ln1_g), d, 0, "rms1_fwd")

    first, ((cos, sin), u1) = _gather2([big["w_in"].astype(BF), _pad8(conv_w[0]), _pad8(ffn_conv_w[0])],
                                       while_w_in_travels, "gather_w_in")
    ag = early["ag"]
    cw8 = _unblock(first[1])
    fcw8 = _unblock(first[2])

    def landed(keys, after, name):
        return _exchange_wait(ag, [rest.index(k) for k in keys], after, name)

    w_in_t = first[0].reshape(N_DEV * nin, d)
    g_off = kr_off + kvl + ROPE
    w_a_t = jnp.concatenate([w_in_t[:kr_off], _lay_rows(w_in_t[kr_off + kvl:g_off]),
                             jnp.zeros((kv_off - kr_off - LANE, d), BF), w_in_t[kr_off:kr_off + kvl]], axis=0)[None]
    w_g_t = w_in_t[g_off:][None]
    gains = _pad8(jnp.concatenate([q_norm_g[:, :NOPE], _lay(q_norm_g[:, NOPE:]),
                                   k_norm_g[:, :NOPE], _lay(k_norm_g[:, NOPE:])], axis=0))
    kr_blk = kr_off // LANE

    z_a = _mm_nt(u1, w_a_t, "mm_z_a")
    z_g = _mm_nt(u1, w_g_t, "mm_z_g", out_dtype=BF)
    p = _conv_mix_fwd(z_a, cw8, conv)
    w_co, w_qb, w_kv = landed(["w_conv_out", "w_q_b", "w_kv_b"], p, "gather_wait_mixers")
    w_co = _unblock(w_co)[None]
    w_kv = _unblock(w_kv)[None]
    wq_full = _unblock(w_qb).reshape(ql, heads, HEAD_QK)
    w_q = jnp.concatenate([wq_full[:, :, :NOPE].reshape(ql, hw), _lay(wq_full[:, :, NOPE:]).reshape(ql, hw)],
                          axis=1)[None]
    yc = _mm_nn(p, w_co, "mm_y_conv", out_dtype=BF)
    qn, q_raw = _rms_mm_nn(z_a, q_a_g, conv3 // ql, w_q, "mm_q")
    kvn, kv_raw = _rms_mm_nn(z_a, kv_a_g, kv_off // kvl, w_kv, "mm_kv")
    q_att, k_att, v_bf = _head_fwd(q_raw, kv_raw, z_a, kr_blk, cos, sin, gains, heads)
    o, o_bf, lse = _attn_fwd(q_att, k_att, v_bf, heads)
    w_mo, w_oo = landed(["w_mla_out", "w_o"], lse, "gather_wait_outs")
    w_mo = w_mo.reshape(1, hw, d)
    w_oo = w_oo.reshape(1, d, d)
    ym, mix = _mla_out_gate(o_bf, w_mo, z_g, b_gate, yc)
    h1, u2 = _residual_norm(mix, w_oo, xs, ln2_g)
    w_up, = landed(["w_ffn_up"], u2, "gather_wait_ffn_up")
    a_g, a_u, f = _ffn_up_act(u2, w_up, fcw8, ffn_conv_b, dff)
    w_dn, = landed(["w_ffn_down"], f, "gather_wait_ffn_down")
    w_dn = w_dn.reshape(1, dff, d)
    dy, dy_bf, loss_part = _mm_nn_loss(f, w_dn, h1, tgt, "mm_ffn_down_loss")

    g_dn = _mm_tn(f, dy_bf, 1, "mm_g_ffn_down").reshape(N_DEV, dff // N_DEV, d)
    rs_dn = _exchange_start([g_dn], False, "reduce_ffn_down_start")
    d_f = _mm_nt(dy_bf, w_dn, "mm_d_f", dep=rs_dn.token)
    d_xg, d_xu, dfw_g, dfw_u = _ffn_act_bwd(a_g, a_u, d_f, fcw8, ffn_conv_b, dff)
    half = N_DEV // 2
    g_up = _mm_tn(u2, d_xg, half, "mm_g_ffn_up_gate", into=lax.empty((N_DEV, d, 2 * dff // N_DEV), BF))
    g_up = _mm_tn(u2, d_xu, half, "mm_g_ffn_up_up", into=g_up, blk0=half)
    rs_up = _exchange_start([g_up], False, "reduce_ffn_up_start")
    d_u2 = _mm_nt([d_xg, d_xu], w_up, "mm_d_u2", out_dtype=BF, dep=rs_up.token)
    d_h1, d_h1_bf, dg_ln2 = _rms_bwd(h1, d_u2, ln2_g, d, 0, "rms2_bwd", extra=dy, also_bf16=True)
    g_oo = _mm_tn(mix, d_h1_bf, 1, "mm_g_w_o").reshape(N_DEV, d // N_DEV, d)
    d_zga, d_zgb, d_yc, d_ym, dba, dbb = _d_mix_gate(d_h1_bf, w_oo, z_g, b_gate, yc, ym)
    g_co = _block(_mm_tn(p, d_yc, 1, "mm_g_conv_out")[0], N_DEV)
    g_mo = _mm_tn(o_bf, d_ym, 1, "mm_g_mla_out").reshape(N_DEV, hw // N_DEV, d)
    rs_mix = _exchange_start([g_oo, g_co, g_mo], False, "reduce_mixers_start")
    d_p = _mm_nt(d_yc, w_co, "mm_d_p", dep=rs_mix.token)
    d_o = _mm_nt(d_ym, w_mo, "mm_d_o", out_dtype=BF)
    d_zb, d_zc, d_zv, dcw = _conv_mix_bwd(z_a, d_p, cw8, conv)
    dq_att, dk_att, dv = _attn_bwd(q_att, k_att, v_bf, o, lse, d_o, heads, dep=rs_mix.token)
    d_q_raw, d_kv_raw, d_kr, dgains = _head_bwd(q_raw, kv_raw, z_a, kr_blk, cos, sin, gains, dq_att, dk_att, dv, heads)
    g_q2 = _mm_tn(qn, d_q_raw, 1, "mm_g_q")[0]
    g_qb = _block(jnp.concatenate([g_q2[:, :hw].reshape(ql, heads, NOPE),
                                   _unlay(g_q2[:, hw:].reshape(ql, heads, LANE))], axis=2).reshape(ql, heads * HEAD_QK), N_DEV)
    g_kv = _block(_mm_tn(kvn, d_kv_raw, 1, "mm_g_kv")[0], N_DEV)
    rs_qkv = _exchange_start([g_qb, g_kv], False, "reduce_qkv_start")
    d_ql, dg_qa = _mm_nt_rms_bwd(d_q_raw, w_q, z_a, q_a_g, conv3 // ql, "mm_d_q_lat", dep=rs_qkv.token)
    d_kvl, dg_kva = _mm_nt_rms_bwd(d_kv_raw, w_kv, z_a, kv_a_g, kv_off // kvl, "mm_d_kv_lat")
    d_z_a = jnp.concatenate([d_zb, d_zc, d_zv, d_ql, d_kr.astype(BF), jnp.zeros((s, kv_off - kr_off - LANE), BF),
                             d_kvl], axis=1)
    g_a = _mm_tn(d_z_a, u1, 1, "mm_g_w_a")[0]
    g_ga = _mm_tn(d_zga, u1, 1, "mm_g_w_ga")[0]
    g_gb = _mm_tn(d_zgb, u1, 1, "mm_g_w_gb")[0]
    g_in = jnp.concatenate([g_a[:kr_off], g_a[kv_off:kv_off + kvl], g_a[kr_off:kr_off + HALF],
                            g_a[kr_off + 2 * HALF:kr_off + 3 * HALF], g_ga, g_gb], axis=0).reshape(N_DEV, nin, d)
    rs_in = _exchange_start([g_in], False, "reduce_w_in_start")
    d_u1 = _mm_nn(d_z_a, w_a_t, "mm_d_u1_a", dep=rs_in.token)
    d_u1 = _mm_nn([d_zga, d_zgb], w_g_t, "mm_d_u1_g", add=d_u1)
    grad_x, dg_ln1 = _rms_bwd(xs, d_u1, ln1_g, d, 0, "rms1_bwd", extra=d_h1)

    summed = {}
    summed["w_ffn_down"], = _exchange_wait(rs_dn, [0], grad_x, "reduce_ffn_down_wait")
    summed["w_ffn_up"], = _exchange_wait(rs_up, [0], grad_x, "reduce_ffn_up_wait")
    summed["w_o"], summed["w_conv_out"], summed["w_mla_out"] = _exchange_wait(rs_mix, [0, 1, 2], grad_x, "reduce_mixers_wait")
    summed["w_q_b"], summed["w_kv_b"] = _exchange_wait(rs_qkv, [0, 1], grad_x, "reduce_qkv_wait")
    loc = locals()
    out = {}
    for k in rest:
        out[k] = _adamw(summed[k], big[k], loc["m_" + k][0], loc["v_" + k][0], "adamw_" + k)

    small = dict(ln1_g=dg_ln1[0:1], b_gate=jnp.concatenate([dba[0:1], dbb[0:1]], axis=1), q_a_g=dg_qa[0:1],
                 kv_a_g=dg_kva[0:1],
                 q_norm_g=jnp.concatenate([dgains[0:1], _unlay(dgains[1:2])], axis=1),
                 k_norm_g=jnp.concatenate([dgains[2:3], _unlay(dgains[3:4])], axis=1),
                 ln2_g=dg_ln2[0:1], ffn_conv_b=jnp.concatenate([dfw_g[3:4], dfw_u[3:4]], axis=1))
    small_names = list(small)
    extra = [dcw[0:3].reshape(1, -1), jnp.concatenate([dfw_g[0:3], dfw_u[0:3]], axis=1).reshape(1, -1),
             loss_part[0:1, 0:1]]
    flat = jnp.concatenate([small[k] for k in small_names] + extra, axis=1)
    n_flat = flat.shape[1]
    rows = -(-n_flat // (SUB * LANE)) * SUB
    flat = jnp.pad(flat, ((0, 0), (0, rows * LANE - n_flat))).reshape(rows, LANE)
    total = _sum_parts(_all_gather([flat], "gather_small", dep=[out[k][0] for k in rest])[0], "sum_small").reshape(1, rows * LANE)
    off = 0
    small_g = {}
    for k in small_names:
        small_g[k] = total[:, off:off + small[k].shape[1]]
        off += small[k].shape[1]
    me = 4 * lax.axis_index("x") + 2 * lax.axis_index("y") + lax.axis_index("c")
    cwn, fcwn = conv // N_DEV, 2 * dff // N_DEV
    g_cw = lax.dynamic_slice_in_dim(total[:, off:off + 3 * conv].reshape(3, conv), me * cwn, cwn, axis=1)
    off += 3 * conv
    g_fcw = lax.dynamic_slice_in_dim(total[:, off:off + 6 * dff].reshape(3, 2 * dff), me * fcwn, fcwn, axis=1)
    off += 6 * dff
    loss = total[0, off]

    summed["w_in"], = _exchange_wait(rs_in, [0], total, "reduce_w_in_wait")
    out["w_in"] = [r.T for r in _adamw(summed["w_in"], big["w_in"], m_w_in[0].T, v_w_in[0].T, "adamw_w_in",
                                       by_cols=True)]
    small_w = dict(ln1_g=ln1_g, b_gate=b_gate, q_a_g=q_a_g, kv_a_g=kv_a_g, q_norm_g=q_norm_g, k_norm_g=k_norm_g,
                   ln2_g=ln2_g, ffn_conv_b=ffn_conv_b, conv_w=conv_w[0].reshape(1, -1),
                   ffn_conv_w=ffn_conv_w[0].reshape(1, -1))
    small_g["conv_w"] = g_cw.reshape(1, -1)
    small_g["ffn_conv_w"] = g_fcw.reshape(1, -1)
    packed_names = list(small_w)

    def pack(get):
        vflat = jnp.concatenate([get(k).reshape(1, -1) for k in packed_names], axis=1)
        nr = -(-vflat.shape[1] // (SUB * LANE)) * SUB
        return jnp.pad(vflat, ((0, 0), (0, nr * LANE - vflat.shape[1])), constant_values=1.0).reshape(nr, LANE)

    res = _adamw(pack(lambda k: small_g[k])[None], pack(lambda k: small_w[k]), pack(lambda k: loc["m_" + k]),
                 pack(lambda k: loc["v_" + k]), "adamw_small")
    res = [r.reshape(1, -1) for r in res]
    off = 0
    for k in packed_names:
        shape = loc[k].shape
        size = small_w[k].shape[1]
        out[k] = [r[:, off:off + size].reshape(shape) for r in res]
        off += size
    for k in names:
        out[k] = [r[None] for r in out[k]]

    order = ["ln1_g", "w_in", "b_gate", "conv_w", "w_conv_out", "q_a_g", "w_q_b", "kv_a_g", "w_kv_b", "q_norm_g",
             "k_norm_g", "w_mla_out", "w_o", "ln2_g", "w_ffn_up", "ffn_conv_w", "ffn_conv_b", "w_ffn_down"]
    return (loss, grad_x[None], *[out[k][0] for k in order], *[out[k][1] for k in order],
            *[out[k][2] for k in order], *[out[k][3] for k in order])
```

```python
import functools

import jax
import jax.numpy as jnp
from jax import lax
from jax.experimental import pallas as pl
from jax.experimental.pallas import tpu as pltpu

BF = jnp.bfloat16
F32 = jnp.float32
MESH = pl.DeviceIdType.MESH
N_DEV = 8

NOPE = 128
ROPE = 64
HALF = ROPE // 2
HEAD_QK = NOPE + ROPE
HEAD_V = 128
LANE = 128
SUB = 8
QK_SCALE = HEAD_QK ** -0.5
LOG2_E = 1.4426950408889634
NORM_EPS = 1e-6
NEG_INF = -1e30
ROPE_THETA = 10000.0
ADAM_LR = 0.001
ADAM_B1 = 0.9
ADAM_B2 = 0.999
ADAM_EPS = 1e-08
ADAM_WD = 0.01
ADAM_STEP = 10

VMEM_LIMIT = 52 * 1024 * 1024
MM_TM, MM_TN, MM_TK, MM_TS = 1024, 1536, 2048, 2048
ROW_TILE, ROW_TILE_BWD = 512, 256
HEAD_ROW_TILE, HEAD_ROW_TILE_BWD = 256, 256
COL_TILE = 512
FFN_COL_TILE = 1408
ATTN_TILE = 1024
ATTN_TILE_FWD = 1024
ANY = pl.BlockSpec(memory_space=pl.ANY)


def _pick(n, target, mult):
    t = (min(n, target) // mult) * mult
    while t > 0:
        if n % t == 0:
            return t
        t -= mult
    raise ValueError(f"no tile for {n} (target {target}, multiple {mult})")


def _cp(*sem):
    return pltpu.CompilerParams(dimension_semantics=sem, vmem_limit_bytes=VMEM_LIMIT)


def _accumulate(kk, nk, acc, part, finish):
    if nk == 1:
        finish(part())
        return

    @pl.when(kk == 0)
    def _():
        acc[...] = part()

    @pl.when((kk > 0) & (kk < nk - 1))
    def _():
        acc[...] += part()

    @pl.when(kk == nk - 1)
    def _():
        finish(acc[...] + part())


def _mm_call(body, name, grid, in_specs, args, out_spec, out_shape, acc_shape, nk, dep):
    if dep is not None:
        in_specs = in_specs + [ANY]
        args = args + [dep]
    return pl.pallas_call(
        body, name=name, grid=grid, in_specs=in_specs, out_specs=out_spec, out_shape=out_shape,
        scratch_shapes=[pltpu.VMEM(acc_shape, F32)] if nk > 1 else [],
        compiler_params=_cp("parallel", "parallel", "arbitrary"),
    )(*args)


def _mm_nn_loss(a, b3, add, target, name):
    m, k = a.shape
    _, k2, n = b3.shape
    assert k == k2 and b3.shape[0] == 1
    tm = _pick(m, MM_TM, 16)
    tn = _pick(n, MM_TN, LANE)
    tk = _pick(k, MM_TK, LANE)
    nk = k // tk

    def body(a_ref, b_ref, c_ref, t_ref, dy_ref, dyb_ref, l_ref, acc):
        kk = pl.program_id(2)

        @pl.when((pl.program_id(0) == 0) & (pl.program_id(1) == 0) & (kk == 0))
        def _():
            l_ref[...] = jnp.zeros_like(l_ref)

        def part():
            return jnp.dot(a_ref[...].astype(BF), b_ref[0].astype(BF), preferred_element_type=F32)

        def finish(r):
            e = r + c_ref[...] - t_ref[...]
            dy_ref[...] = e / n
            dyb_ref[...] = (e / n).astype(BF)
            l_ref[...] += 0.5 * jnp.sum(jnp.sum(e * e, axis=-1, keepdims=True), axis=0, keepdims=True) / n

        _accumulate(kk, nk, acc, part, finish)

    tile = pl.BlockSpec((tm, tn), lambda i, j, kk: (i, j))
    return pl.pallas_call(
        body, name=name, grid=(m // tm, n // tn, nk),
        in_specs=[pl.BlockSpec((tm, tk), lambda i, j, kk: (i, kk)),
                  pl.BlockSpec((1, tk, tn), lambda i, j, kk: (0, kk, j)), tile, tile],
        out_specs=[tile, tile, pl.BlockSpec((SUB, LANE), lambda i, j, kk: (0, 0))],
        out_shape=[jax.ShapeDtypeStruct((m, n), F32), jax.ShapeDtypeStruct((m, n), BF),
                   jax.ShapeDtypeStruct((SUB, LANE), F32)],
        scratch_shapes=[pltpu.VMEM((tm, tn), F32)],
        compiler_params=_cp("arbitrary", "arbitrary", "arbitrary"),
    )(a, b3, add, target)


def _mm_nn(a, b3, name, add=None, out_dtype=F32, blk0=0, nblk=None, dep=None):
    pair = isinstance(a, (list, tuple))
    a_list = list(a) if pair else [a]
    m, ka = a_list[0].shape
    k = ka * len(a_list)
    nb_all, k2, nbw = b3.shape
    assert k == k2
    nblk = nb_all - blk0 if nblk is None else nblk
    n = nblk * nbw
    tm = _pick(m, MM_TM if k > MM_TM else 2 * MM_TM, 16)
    tn = _pick(nbw, MM_TN, LANE)
    tk = _pick(ka, MM_TK, LANE)
    per = nbw // tn
    nk = k // tk
    nka = ka // tk
    na_ops = len(a_list)

    def body(*refs):
        a_refs, b_ref = refs[:na_ops], refs[na_ops]
        c_ref = refs[na_ops + 1] if add is not None else None
        o_ref = refs[na_ops + 1 + (add is not None) + (dep is not None)]
        acc = refs[-1]
        kk = pl.program_id(2)

        def part():
            av = a_refs[0][...] if not pair else jnp.where(kk < nka, a_refs[0][...], a_refs[1][...])
            return jnp.dot(av.astype(BF), b_ref[...].astype(BF), preferred_element_type=F32)

        def finish(r):
            if add is not None:
                r = r + c_ref[...]
            o_ref[...] = r.astype(out_dtype)

        _accumulate(kk, nk, acc, part, finish)

    if pair:
        in_specs = [pl.BlockSpec((tm, tk), lambda i, j, kk: (i, jnp.minimum(kk, nka - 1))),
                    pl.BlockSpec((tm, tk), lambda i, j, kk: (i, jnp.maximum(kk - nka, 0)))]
    else:
        in_specs = [pl.BlockSpec((tm, tk), lambda i, j, kk: (i, kk))]
    in_specs.append(pl.BlockSpec((None, tk, tn), lambda i, j, kk: (blk0 + j // per, kk, j % per)))
    args = a_list + [b3]
    if add is not None:
        in_specs.append(pl.BlockSpec((tm, tn), lambda i, j, kk: (i, j)))
        args.append(add)
    return _mm_call(body, name, (m // tm, n // tn, nk), in_specs, args,
                    pl.BlockSpec((tm, tn), lambda i, j, kk: (i, j)), jax.ShapeDtypeStruct((m, n), out_dtype),
                    (tm, tn), nk, dep)


def _mm_nt(a, b3, name, add=None, out_dtype=F32, blk0=0, nblk=None, dep=None):
    pair = isinstance(a, (list, tuple))
    a_list = list(a) if pair else [a]
    m, na = a_list[0].shape
    n = na * len(a_list)
    nb_all, k, nbw = b3.shape
    nblk = nb_all - blk0 if nblk is None else nblk
    assert n == nblk * nbw and na % nbw == 0
    tm = _pick(m, 2 * MM_TM if k <= MM_TM and n <= MM_TK else MM_TM, 16)
    tk = _pick(nbw, MM_TK, LANE)
    per = nbw // tk
    nk = n // tk
    tn = _pick(k, MM_TN if nk <= 2 else 2 * MM_TM, LANE)
    nka = na // tk
    na_ops = len(a_list)

    def body(*refs):
        a_refs, b_ref = refs[:na_ops], refs[na_ops]
        c_ref = refs[na_ops + 1] if add is not None else None
        o_ref = refs[na_ops + 1 + (add is not None) + (dep is not None)]
        acc = refs[-1]
        kk = pl.program_id(2)

        def part():
            av = a_refs[0][...] if not pair else jnp.where(kk < nka, a_refs[0][...], a_refs[1][...])
            return lax.dot_general(av.astype(BF), b_ref[...].astype(BF),
                                   (((1,), (1,)), ((), ())), preferred_element_type=F32)

        def finish(r):
            if add is not None:
                r = r + c_ref[...]
            o_ref[...] = r.astype(out_dtype)

        _accumulate(kk, nk, acc, part, finish)

    if pair:
        in_specs = [pl.BlockSpec((tm, tk), lambda i, j, kk: (i, jnp.minimum(kk, nka - 1))),
                    pl.BlockSpec((tm, tk), lambda i, j, kk: (i, jnp.maximum(kk - nka, 0)))]
    else:
        in_specs = [pl.BlockSpec((tm, tk), lambda i, j, kk: (i, kk))]
    in_specs.append(pl.BlockSpec((None, tn, tk), lambda i, j, kk: (blk0 + kk // per, j, kk % per)))
    args = a_list + [b3]
    if add is not None:
        in_specs.append(pl.BlockSpec((tm, tn), lambda i, j, kk: (i, j)))
        args.append(add)
    return _mm_call(body, name, (m // tm, k // tn, nk), in_specs, args,
                    pl.BlockSpec((tm, tn), lambda i, j, kk: (i, j)), jax.ShapeDtypeStruct((m, k), out_dtype),
                    (tm, tn), nk, dep)


def _rms_mm_nn(x, g, col_blk, b3, name):
    m = x.shape[0]
    _, k, n = b3.shape
    assert b3.shape[0] == 1
    tm = _pick(m, 2 * MM_TM, 16)
    tn = _pick(n, MM_TM, LANE)

    def body(x_ref, g_ref, b_ref, u_ref, o_ref):
        xv = x_ref[...]
        r = lax.rsqrt(jnp.mean(xv * xv, axis=-1, keepdims=True) + NORM_EPS)
        u = ((xv * r) * g_ref[...]).astype(BF)

        @pl.when(pl.program_id(1) == 0)
        def _():
            u_ref[...] = u

        o_ref[...] = jnp.dot(u, b_ref[0], preferred_element_type=F32)

    return pl.pallas_call(
        body, name=name, grid=(m // tm, n // tn),
        in_specs=[pl.BlockSpec((tm, k), lambda i, j: (i, col_blk)), pl.BlockSpec((1, k), lambda i, j: (0, 0)),
                  pl.BlockSpec((1, k, tn), lambda i, j: (0, 0, j))],
        out_specs=[pl.BlockSpec((tm, k), lambda i, j: (i, 0)), pl.BlockSpec((tm, tn), lambda i, j: (i, j))],
        out_shape=[jax.ShapeDtypeStruct((m, k), BF), jax.ShapeDtypeStruct((m, n), F32)],
        compiler_params=_cp("parallel", "arbitrary"),
    )(x, g, b3)


def _mm_nt_rms_bwd(a, b3, x, g, col_blk, name, dep=None):
    m, n = a.shape
    _, width, n2 = b3.shape
    assert n == n2 and b3.shape[0] == 1
    tm = _pick(m, MM_TM, 16)
    tk = _pick(n, MM_TK, LANE)
    nk = n // tk

    def body(*refs):
        a_ref, b_ref, x_ref, g_ref = refs[:4]
        dx_ref, dg_ref = refs[4 + (dep is not None):6 + (dep is not None)]
        acc = refs[-1]
        kk = pl.program_id(1)

        @pl.when((pl.program_id(0) == 0) & (kk == 0))
        def _():
            dg_ref[...] = jnp.zeros_like(dg_ref)

        def part():
            return lax.dot_general(a_ref[...], b_ref[0], (((1,), (1,)), ((), ())), preferred_element_type=F32)

        def finish(du):
            xv = x_ref[...]
            r = lax.rsqrt(jnp.mean(xv * xv, axis=-1, keepdims=True) + NORM_EPS)
            nv = xv * r
            dn = du * g_ref[...]
            dx_ref[...] = (r * (dn - nv * jnp.mean(dn * nv, axis=-1, keepdims=True))).astype(BF)
            dg_ref[...] += _rows8([jnp.sum(du * nv, axis=0, keepdims=True)], width)

        _accumulate(kk, nk, acc, part, finish)

    in_specs = [pl.BlockSpec((tm, tk), lambda i, kk: (i, kk)), pl.BlockSpec((1, width, tk), lambda i, kk: (0, 0, kk)),
                pl.BlockSpec((tm, width), lambda i, kk: (i, col_blk)), pl.BlockSpec((1, width), lambda i, kk: (0, 0))]
    args = [a, b3, x, g]
    if dep is not None:
        in_specs.append(ANY)
        args.append(dep)
    return pl.pallas_call(
        body, name=name, grid=(m // tm, nk), in_specs=in_specs,
        out_specs=[pl.BlockSpec((tm, width), lambda i, kk: (i, 0)), pl.BlockSpec((SUB, width), lambda i, kk: (0, 0))],
        out_shape=[jax.ShapeDtypeStruct((m, width), BF), jax.ShapeDtypeStruct((SUB, width), F32)],
        scratch_shapes=[pltpu.VMEM((tm, width), F32)],
        compiler_params=_cp("arbitrary", "arbitrary"),
    )(*args)


def _mm_tn(a, b, nblk, name, out_dtype=BF, dep=None, into=None, blk0=0):
    s, m = a.shape
    s2, n = b.shape
    assert s == s2 and n % nblk == 0 and (dep is None or into is None)
    nbw = n // nblk
    tm = _pick(m, MM_TN, LANE)
    tn = _pick(nbw, MM_TN, LANE)
    ts = _pick(s, MM_TS, LANE)
    per = nbw // tn
    ns = s // ts

    def body(*refs):
        a_ref, b_ref = refs[:2]
        o_ref = refs[2 + (dep is not None or into is not None)]
        acc = refs[-1]

        def part():
            return lax.dot_general(a_ref[...].astype(BF), b_ref[...].astype(BF),
                                   (((0,), (0,)), ((), ())), preferred_element_type=F32)

        def finish(r):
            o_ref[...] = r.astype(out_dtype)

        _accumulate(pl.program_id(2), ns, acc, part, finish)

    in_specs = [pl.BlockSpec((ts, tm), lambda i, j, ss: (ss, i)),
                pl.BlockSpec((ts, tn), lambda i, j, ss: (ss, j))]
    out_spec = pl.BlockSpec((None, tm, tn), lambda i, j, ss: (blk0 + j // per, i, j % per))
    if into is None:
        return _mm_call(body, name, (m // tm, n // tn, ns), in_specs, [a, b], out_spec,
                        jax.ShapeDtypeStruct((nblk, m, nbw), out_dtype), (tm, tn), ns, dep)
    assert into.shape[1:] == (m, nbw) and into.dtype == out_dtype
    return pl.pallas_call(
        body, name=name, grid=(m // tm, n // tn, ns), in_specs=in_specs + [ANY], out_specs=out_spec,
        out_shape=jax.ShapeDtypeStruct(into.shape, out_dtype), input_output_aliases={2: 0},
        scratch_shapes=[pltpu.VMEM((tm, tn), F32)] if ns > 1 else [],
        compiler_params=_cp("parallel", "parallel", "arbitrary"),
    )(a, b, into)


def _rows8(rows, width):
    idx = lax.broadcasted_iota(jnp.int32, (SUB, width), 0)
    out = jnp.zeros((SUB, width), F32)
    for r, v in enumerate(rows):
        out = jnp.where(idx == r, v, out)
    return out


def _rms_fwd(x, g, width, col_blk, name):
    s = x.shape[0]
    tr = _pick(s, ROW_TILE, 16)

    def body(x_ref, g_ref, u_ref):
        xv = x_ref[...]
        r = lax.rsqrt(jnp.mean(xv * xv, axis=-1, keepdims=True) + NORM_EPS)
        u_ref[...] = ((xv * r) * g_ref[...]).astype(BF)

    return pl.pallas_call(
        body, name=name, grid=(s // tr,),
        in_specs=[pl.BlockSpec((tr, width), lambda i: (i, col_blk)),
                  pl.BlockSpec((1, width), lambda i: (0, 0))],
        out_specs=pl.BlockSpec((tr, width), lambda i: (i, 0)),
        out_shape=jax.ShapeDtypeStruct((s, width), BF),
        compiler_params=_cp("parallel"),
    )(x, g)


def _rms_bwd(x, du, g, width, col_blk, name, extra=None, out_dtype=F32, also_bf16=False):
    s = x.shape[0]
    tr = _pick(s, ROW_TILE_BWD, 16)

    def body(*refs):
        x_ref, du_ref, g_ref = refs[:3]
        e_ref = refs[3] if extra is not None else None
        dx_ref = refs[3 + (extra is not None)]
        dxb_ref = refs[4 + (extra is not None)] if also_bf16 else None
        dg_ref = refs[-1]
        i = pl.program_id(0)
        xv = x_ref[...]
        duv = du_ref[...].astype(F32)
        r = lax.rsqrt(jnp.mean(xv * xv, axis=-1, keepdims=True) + NORM_EPS)
        nv = xv * r
        dn = duv * g_ref[...]
        dx = r * (dn - nv * jnp.mean(dn * nv, axis=-1, keepdims=True))
        if extra is not None:
            dx = dx + e_ref[...]
        dx_ref[...] = dx.astype(out_dtype)
        if also_bf16:
            dxb_ref[...] = dx.astype(BF)

        @pl.when(i == 0)
        def _():
            dg_ref[...] = jnp.zeros_like(dg_ref)

        dg_ref[...] += _rows8([jnp.sum(duv * nv, axis=0, keepdims=True)], width)

    in_specs = [pl.BlockSpec((tr, width), lambda i: (i, col_blk)),
                pl.BlockSpec((tr, width), lambda i: (i, 0)),
                pl.BlockSpec((1, width), lambda i: (0, 0))]
    args = [x, du, g]
    if extra is not None:
        in_specs.append(pl.BlockSpec((tr, width), lambda i: (i, 0)))
        args.append(extra)
    return pl.pallas_call(
        body, name=name, grid=(s // tr,),
        in_specs=in_specs,
        out_specs=[pl.BlockSpec((tr, width), lambda i: (i, 0))] * (1 + also_bf16)
        + [pl.BlockSpec((SUB, width), lambda i: (0, 0))],
        out_shape=[jax.ShapeDtypeStruct((s, width), out_dtype)] + [jax.ShapeDtypeStruct((s, width), BF)] * also_bf16
        + [jax.ShapeDtypeStruct((SUB, width), F32)],
        compiler_params=_cp("arbitrary"),
    )(*args)


def _down(cur, prev8, k):
    ext = jnp.concatenate([prev8, cur], axis=0)
    return pltpu.roll(ext, k, axis=0)[SUB:]


def _up(cur, next8, k):
    ext = jnp.concatenate([cur, next8], axis=0)
    return pltpu.roll(ext, ext.shape[0] - k, axis=0)[:cur.shape[0]]


def _lags(cur, prev8):
    return _down(cur, prev8, 1), _down(cur, prev8, 2)


def _conv3(w_ref, cur, prev8, lags=None):
    lag1, lag2 = _lags(cur, prev8) if lags is None else lags
    return w_ref[0:1, :] * lag2 + w_ref[1:2, :] * lag1 + w_ref[2:3, :] * cur


def _conv3_t(w_ref, cur, next8):
    return w_ref[2:3, :] * cur + w_ref[1:2, :] * _up(cur, next8, 1) + w_ref[0:1, :] * _up(cur, next8, 2)


def _spec_cur(tr, tc, c0):
    return pl.BlockSpec((tr, tc), lambda j, i: (i, c0 + j))


def _spec_prev(tr, tc, c0):
    return pl.BlockSpec((SUB, tc), lambda j, i: (jnp.maximum(i * (tr // SUB) - 1, 0), c0 + j))


def _spec_next(tr, tc, c0, s):
    return pl.BlockSpec((SUB, tc), lambda j, i: (jnp.minimum((i + 1) * (tr // SUB), s // SUB - 1), c0 + j))


def _spec_w(tc, c0):
    return pl.BlockSpec((SUB, tc), lambda j, i: (0, c0 + j))


def _pad8(w):
    return jnp.pad(w, ((0, SUB - w.shape[0]), (0, 0)))


def _conv_mix_fwd(z_a, cw8, conv):
    s = z_a.shape[0]
    tr = _pick(s, ROW_TILE, 16)
    tc = _pick(conv, COL_TILE, LANE)
    nc = conv // tc

    def body(zb_ref, zc_ref, zv_ref, zcp_ref, zvp_ref, w_ref, p_ref):
        i = pl.program_id(1)
        cv = zc_ref[...] * zv_ref[...]
        cvp = jnp.where(i > 0, zcp_ref[...] * zvp_ref[...], 0.0)
        p_ref[...] = (zb_ref[...] * _conv3(w_ref, cv, cvp)).astype(BF)

    return pl.pallas_call(
        body, name="conv_mix_fwd", grid=(nc, s // tr),
        in_specs=[_spec_cur(tr, tc, 0), _spec_cur(tr, tc, nc), _spec_cur(tr, tc, 2 * nc),
                  _spec_prev(tr, tc, nc), _spec_prev(tr, tc, 2 * nc), _spec_w(tc, 0)],
        out_specs=_spec_cur(tr, tc, 0),
        out_shape=jax.ShapeDtypeStruct((s, conv), BF),
        compiler_params=_cp("parallel", "parallel"),
    )(z_a, z_a, z_a, z_a, z_a, cw8)


def _conv_mix_bwd(z_a, d_p, cw8, conv):
    s = z_a.shape[0]
    tr = _pick(s, ROW_TILE_BWD, 16)
    tc = _pick(conv, COL_TILE, LANE)
    nc = conv // tc
    nr = s // tr

    def body(zb_ref, zbn_ref, zc_ref, zcp_ref, zv_ref, zvp_ref, dp_ref, dpn_ref, w_ref,
             dzb_ref, dzc_ref, dzv_ref, dw_ref):
        i = pl.program_id(1)
        zc = zc_ref[...]
        zv = zv_ref[...]
        cv = zc * zv
        cvp = jnp.where(i > 0, zcp_ref[...] * zvp_ref[...], 0.0)
        cv1, cv2 = _lags(cv, cvp)
        dpv = dp_ref[...]
        dzb_ref[...] = (dpv * _conv3(w_ref, cv, cvp, (cv1, cv2))).astype(BF)
        dcc = dpv * zb_ref[...]
        dccn = jnp.where(i < nr - 1, dpn_ref[...] * zbn_ref[...], 0.0)
        dcv = _conv3_t(w_ref, dcc, dccn)
        dzc_ref[...] = (dcv * zv).astype(BF)
        dzv_ref[...] = (dcv * zc).astype(BF)

        @pl.when(i == 0)
        def _():
            dw_ref[...] = jnp.zeros_like(dw_ref)

        dw_ref[...] += _rows8([jnp.sum(dcc * cv2, axis=0, keepdims=True),
                               jnp.sum(dcc * cv1, axis=0, keepdims=True),
                               jnp.sum(dcc * cv, axis=0, keepdims=True)], tc)

    out = jax.ShapeDtypeStruct((s, conv), BF)
    return pl.pallas_call(
        body, name="conv_mix_bwd", grid=(nc, nr),
        in_specs=[_spec_cur(tr, tc, 0), _spec_next(tr, tc, 0, s),
                  _spec_cur(tr, tc, nc), _spec_prev(tr, tc, nc),
                  _spec_cur(tr, tc, 2 * nc), _spec_prev(tr, tc, 2 * nc),
                  _spec_cur(tr, tc, 0), _spec_next(tr, tc, 0, s), _spec_w(tc, 0)],
        out_specs=[_spec_cur(tr, tc, 0), _spec_cur(tr, tc, 0), _spec_cur(tr, tc, 0), _spec_w(tc, 0)],
        out_shape=[out, out, out, jax.ShapeDtypeStruct((SUB, conv), F32)],
        compiler_params=_cp("parallel", "arbitrary"),
    )(z_a, z_a, z_a, z_a, z_a, z_a, d_p, d_p, cw8)


def _silu_parts(ag):
    sg = jax.nn.sigmoid(ag)
    return ag * sg, sg


def _ffn_up_act(u2, w_up, cw8, cb, dff):
    s, d = u2.shape
    nb, _, nbw = w_up.shape
    half = nb // 2
    assert half * nbw == dff
    tm = _pick(s, ROW_TILE, 16)

    def body(u_ref, wg_ref, wu_ref, cg_ref, cu_ref, bg_ref, bu_ref, ag_ref, au_ref, f_ref, hist_g, hist_u):
        i = pl.program_id(1)

        @pl.when(i == 0)
        def _():
            hist_g[...] = jnp.zeros_like(hist_g)
            hist_u[...] = jnp.zeros_like(hist_u)

        u = u_ref[...]
        xg = jnp.dot(u, wg_ref[...], preferred_element_type=F32)
        xu = jnp.dot(u, wu_ref[...], preferred_element_type=F32)
        ag_ref[...] = xg
        au_ref[...] = xu
        ag = _conv3(cg_ref, xg, hist_g[...]) + bg_ref[...]
        au = _conv3(cu_ref, xu, hist_u[...]) + bu_ref[...]
        f_ref[...] = (_silu_parts(ag)[0] * au).astype(BF)
        hist_g[...] = xg[tm - SUB:]
        hist_u[...] = xu[tm - SUB:]

    once = pl.Buffered(1)
    tile = pl.BlockSpec((tm, nbw), lambda j, i: (i, j))
    return pl.pallas_call(
        body, name="mm_ffn_up_act", grid=(half, s // tm),
        in_specs=[pl.BlockSpec((tm, d), lambda j, i: (i, 0)),
                  pl.BlockSpec((None, d, nbw), lambda j, i: (j, 0, 0), pipeline_mode=once),
                  pl.BlockSpec((None, d, nbw), lambda j, i: (half + j, 0, 0), pipeline_mode=once),
                  pl.BlockSpec((SUB, nbw), lambda j, i: (0, j)), pl.BlockSpec((SUB, nbw), lambda j, i: (0, half + j)),
                  pl.BlockSpec((1, nbw), lambda j, i: (0, j)), pl.BlockSpec((1, nbw), lambda j, i: (0, half + j))],
        out_specs=[tile, tile, tile],
        out_shape=[jax.ShapeDtypeStruct((s, dff), F32), jax.ShapeDtypeStruct((s, dff), F32),
                   jax.ShapeDtypeStruct((s, dff), BF)],
        scratch_shapes=[pltpu.VMEM((SUB, nbw), F32), pltpu.VMEM((SUB, nbw), F32)],
        compiler_params=_cp("arbitrary", "arbitrary"),
    )(u2, w_up, w_up, cw8, cw8, cb, cb)


def _ffn_act_bwd(a_g, a_u, d_f, cw8, cb, dff):
    s = a_g.shape[0]
    tr = _pick(s, ROW_TILE_BWD // 2, 16)
    tc = _pick(dff, FFN_COL_TILE, LANE)
    nc = dff // tc
    nr = s // tr

    def body(xg_ref, xgp_ref, xgn_ref, xu_ref, xup_ref, xun_ref, df_ref, dfn_ref,
             wg_ref, wu_ref, bg_ref, bu_ref, dxg_ref, dxu_ref, dwg_ref, dwu_ref):
        i = pl.program_id(1)
        xg = xg_ref[...]
        xu = xu_ref[...]
        xgp = jnp.where(i > 0, xgp_ref[...], 0.0)
        xup = jnp.where(i > 0, xup_ref[...], 0.0)

        def d_act(xg_t, xgp_t, xu_t, xup_t, df_t, lags_g=None, lags_u=None):
            ag = _conv3(wg_ref, xg_t, xgp_t, lags_g) + bg_ref[...]
            au = _conv3(wu_ref, xu_t, xup_t, lags_u) + bu_ref[...]
            sil, sg = _silu_parts(ag)
            return df_t * au * (sg * (1.0 + ag * (1.0 - sg))), df_t * sil

        lags_g = _lags(xg, xgp)
        lags_u = _lags(xu, xup)
        dag, dau = d_act(xg, xgp, xu, xup, df_ref[...], lags_g, lags_u)
        dfn = jnp.where(i < nr - 1, dfn_ref[...], 0.0)
        dagn, daun = d_act(xgn_ref[...], xg[tr - SUB:], xun_ref[...], xu[tr - SUB:], dfn)
        dxg_ref[...] = _conv3_t(wg_ref, dag, dagn).astype(BF)
        dxu_ref[...] = _conv3_t(wu_ref, dau, daun).astype(BF)

        @pl.when(i == 0)
        def _():
            dwg_ref[...] = jnp.zeros_like(dwg_ref)
            dwu_ref[...] = jnp.zeros_like(dwu_ref)

        def wgrad(da, x, lags):
            return _rows8([jnp.sum(da * lags[1], axis=0, keepdims=True),
                           jnp.sum(da * lags[0], axis=0, keepdims=True),
                           jnp.sum(da * x, axis=0, keepdims=True),
                           jnp.sum(da, axis=0, keepdims=True)], tc)

        dwg_ref[...] += wgrad(dag, xg, lags_g)
        dwu_ref[...] += wgrad(dau, xu, lags_u)

    half = jax.ShapeDtypeStruct((s, dff), BF)
    wsh = jax.ShapeDtypeStruct((SUB, dff), F32)
    return pl.pallas_call(
        body, name="ffn_act_bwd", grid=(nc, nr),
        in_specs=[_spec_cur(tr, tc, 0), _spec_prev(tr, tc, 0), _spec_next(tr, tc, 0, s),
                  _spec_cur(tr, tc, 0), _spec_prev(tr, tc, 0), _spec_next(tr, tc, 0, s),
                  _spec_cur(tr, tc, 0), _spec_next(tr, tc, 0, s),
                  _spec_w(tc, 0), _spec_w(tc, nc),
                  pl.BlockSpec((1, tc), lambda j, i: (0, j)), pl.BlockSpec((1, tc), lambda j, i: (0, nc + j))],
        out_specs=[_spec_cur(tr, tc, 0), _spec_cur(tr, tc, 0), _spec_w(tc, 0), _spec_w(tc, 0)],
        out_shape=[half, half, wsh, wsh],
        compiler_params=_cp("parallel", "arbitrary"),
    )(a_g, a_g, a_g, a_u, a_u, a_u, d_f, d_f, cw8, cw8, cb, cb)


def _residual_norm(mix, w_oo, x, g):
    m, k = mix.shape
    d = w_oo.shape[2]
    tm = _pick(m, ROW_TILE, 16)

    def body(a_ref, b_ref, x_ref, g_ref, h_ref, u_ref):
        h = jnp.dot(a_ref[...], b_ref[0], preferred_element_type=F32) + x_ref[...]
        h_ref[...] = h
        r = lax.rsqrt(jnp.mean(h * h, axis=-1, keepdims=True) + NORM_EPS)
        u_ref[...] = ((h * r) * g_ref[...]).astype(BF)

    row = pl.BlockSpec((tm, d), lambda i: (i, 0))
    return pl.pallas_call(
        body, name="mm_h1_norm", grid=(m // tm,),
        in_specs=[pl.BlockSpec((tm, k), lambda i: (i, 0)),
                  pl.BlockSpec((1, k, d), lambda i: (0, 0, 0), pipeline_mode=pl.Buffered(1)),
                  row, pl.BlockSpec((1, d), lambda i: (0, 0))],
        out_specs=[row, row],
        out_shape=[jax.ShapeDtypeStruct((m, d), F32), jax.ShapeDtypeStruct((m, d), BF)],
        compiler_params=_cp("parallel"),
    )(mix, w_oo, x, g)


def _mla_out_gate(o, w_mo, z_g, b_gate, yc):
    m, k = o.shape
    d = w_mo.shape[2]
    tm = _pick(m, MM_TM, 16)
    tn = _pick(d, MM_TM, LANE)
    nc = d // tn

    def body(a_ref, b_ref, za_ref, zb_ref, ba_ref, bb_ref, yc_ref, ym_ref, mix_ref):
        ym = jnp.dot(a_ref[...], b_ref[0], preferred_element_type=F32)
        ga = jax.nn.sigmoid(za_ref[...] + ba_ref[...])
        gb = jax.nn.sigmoid(zb_ref[...] + bb_ref[...])
        ym_ref[...] = ym.astype(BF)
        mix_ref[...] = (ga * yc_ref[...] + gb * ym).astype(BF)

    tile = pl.BlockSpec((tm, tn), lambda i, j: (i, j))
    out = jax.ShapeDtypeStruct((m, d), BF)
    return pl.pallas_call(
        body, name="mm_y_mla_gate", grid=(m // tm, nc),
        in_specs=[pl.BlockSpec((tm, k), lambda i, j: (i, 0)), pl.BlockSpec((1, k, tn), lambda i, j: (0, 0, j)),
                  tile, pl.BlockSpec((tm, tn), lambda i, j: (i, nc + j)),
                  pl.BlockSpec((1, tn), lambda i, j: (0, j)), pl.BlockSpec((1, tn), lambda i, j: (0, nc + j)), tile],
        out_specs=[tile, tile], out_shape=[out, out],
        compiler_params=_cp("parallel", "parallel"),
    )(o, w_mo, z_g, z_g, b_gate, b_gate, yc)


def _d_mix_gate(d_h1, w_oo, z_g, b_gate, yc, ym):
    m, n = d_h1.shape
    d = w_oo.shape[1]
    tm = _pick(m, MM_TM, 16)
    tn = _pick(d, COL_TILE, LANE)
    nc = d // tn

    def body(a_ref, b_ref, za_ref, zb_ref, ba_ref, bb_ref, yc_ref, ym_ref,
             dza_ref, dzb_ref, dyc_ref, dym_ref, dba_ref, dbb_ref):
        i = pl.program_id(1)
        dm = lax.dot_general(a_ref[...], b_ref[0], (((1,), (1,)), ((), ())), preferred_element_type=F32)
        ga = jax.nn.sigmoid(za_ref[...] + ba_ref[...])
        gb = jax.nn.sigmoid(zb_ref[...] + bb_ref[...])
        dza = dm * yc_ref[...] * (ga * (1.0 - ga))
        dzb = dm * ym_ref[...] * (gb * (1.0 - gb))
        dza_ref[...] = dza.astype(BF)
        dzb_ref[...] = dzb.astype(BF)
        dyc_ref[...] = (dm * ga).astype(BF)
        dym_ref[...] = (dm * gb).astype(BF)

        @pl.when(i == 0)
        def _():
            dba_ref[...] = jnp.zeros_like(dba_ref)
            dbb_ref[...] = jnp.zeros_like(dbb_ref)

        dba_ref[...] += _rows8([jnp.sum(dza, axis=0, keepdims=True)], tn)
        dbb_ref[...] += _rows8([jnp.sum(dzb, axis=0, keepdims=True)], tn)

    tile = pl.BlockSpec((tm, tn), lambda j, i: (i, j))
    act = jax.ShapeDtypeStruct((m, d), BF)
    bsh = jax.ShapeDtypeStruct((SUB, d), F32)
    return pl.pallas_call(
        body, name="mm_d_mix_gate", grid=(nc, m // tm),
        in_specs=[pl.BlockSpec((tm, n), lambda j, i: (i, 0)), pl.BlockSpec((1, tn, n), lambda j, i: (0, j, 0)),
                  tile, pl.BlockSpec((tm, tn), lambda j, i: (i, nc + j)),
                  pl.BlockSpec((1, tn), lambda j, i: (0, j)), pl.BlockSpec((1, tn), lambda j, i: (0, nc + j)),
                  tile, tile],
        out_specs=[tile] * 4 + [pl.BlockSpec((SUB, tn), lambda j, i: (0, j))] * 2,
        out_shape=[act, act, act, act, bsh, bsh],
        compiler_params=_cp("parallel", "arbitrary"),
    )(d_h1, w_oo, z_g, z_g, b_gate, b_gate, yc, ym)


def _lay(v):
    z = jnp.zeros(v.shape[:-1] + (HALF,), v.dtype)
    return jnp.concatenate([v[..., :HALF], z, v[..., HALF:], z], axis=-1)


def _unlay(v):
    return jnp.concatenate([v[..., :HALF], v[..., 2 * HALF:3 * HALF]], axis=-1)


def _lay_rows(v):
    z = jnp.zeros((HALF,) + v.shape[1:], v.dtype)
    return jnp.concatenate([v[:HALF], z, v[HALF:], z], axis=0)


def _rope_tables(positions):
    s = positions.shape[0]
    tr = _pick(s, ROW_TILE, 8)
    inv_freq = ROPE_THETA ** (-jnp.arange(0, ROPE, 2, dtype=F32) / ROPE)
    consts = jnp.stack([_lay(jnp.concatenate([inv_freq, inv_freq])),
                        _lay(jnp.ones((ROPE,), F32)),
                        _lay(jnp.concatenate([-jnp.ones((HALF,), F32), jnp.ones((HALF,), F32)]))])
    consts = _pad8(consts)

    def body(p_ref, c_ref, cos_ref, sin_ref):
        ang = p_ref[...].astype(F32) * c_ref[0:1, :]
        cos_ref[...] = jnp.cos(ang) * c_ref[1:2, :]
        sin_ref[...] = jnp.sin(ang) * c_ref[2:3, :]

    tab = jax.ShapeDtypeStruct((s, LANE), F32)
    return pl.pallas_call(
        body, name="rope_tables", grid=(s // tr,),
        in_specs=[pl.BlockSpec((tr, 1), lambda i: (i, 0)), pl.BlockSpec((SUB, LANE), lambda i: (0, 0))],
        out_specs=[pl.BlockSpec((tr, LANE), lambda i: (i, 0))] * 2,
        out_shape=[tab, tab],
        compiler_params=_cp("parallel"),
    )(positions, consts)


def _lane_sum(p):
    return jnp.sum(p, axis=-1, keepdims=True)


def _rope(t, cos, sin):
    return t * cos + pltpu.roll(t, 2 * HALF, axis=1) * sin


def _rope_t(d, cos, sin):
    return d * cos + pltpu.roll(d * sin, 2 * HALF, axis=1)


def _head_fwd(q_raw, kv_raw, z_a, kr_blk, cos, sin, gains, heads):
    s = q_raw.shape[0]
    tr = _pick(s, HEAD_ROW_TILE, 16)
    hw = heads * LANE

    def body(q_ref, kv_ref, kr_ref, cos_ref, sin_ref, g_ref, qo_ref, ko_ref, vo_ref):
        cosv = cos_ref[...]
        sinv = sin_ref[...]
        krv = kr_ref[...]
        kr_sq = krv * krv
        for h in range(heads):
            lo = h * LANE
            qn = q_ref[:, lo:lo + LANE]
            qr = q_ref[:, hw + lo:hw + lo + LANE]
            r = lax.rsqrt(_lane_sum(qn * qn + qr * qr) / HEAD_QK + NORM_EPS)
            qo_ref[:, 2 * lo:2 * lo + LANE] = (((qn * r) * g_ref[0:1, :]) * (QK_SCALE * LOG2_E)).astype(BF)
            qo_ref[:, 2 * lo + LANE:2 * lo + 2 * LANE] = (
                _rope((qr * r) * g_ref[1:2, :], cosv, sinv) * (QK_SCALE * LOG2_E)).astype(BF)
            kn = kv_ref[:, 2 * lo:2 * lo + LANE]
            r = lax.rsqrt(_lane_sum(kn * kn + kr_sq) / HEAD_QK + NORM_EPS)
            ko_ref[:, 2 * lo:2 * lo + LANE] = ((kn * r) * g_ref[2:3, :]).astype(BF)
            ko_ref[:, 2 * lo + LANE:2 * lo + 2 * LANE] = _rope((krv * r) * g_ref[3:4, :], cosv, sinv).astype(BF)
            vo_ref[:, lo:lo + LANE] = kv_ref[:, 2 * lo + LANE:2 * lo + 2 * LANE].astype(BF)

    row = lambda w: pl.BlockSpec((tr, w), lambda i: (i, 0))
    return pl.pallas_call(
        body, name="head_fwd", grid=(s // tr,),
        in_specs=[row(2 * hw), row(2 * hw), pl.BlockSpec((tr, LANE), lambda i: (i, kr_blk)),
                  row(LANE), row(LANE), pl.BlockSpec((SUB, LANE), lambda i: (0, 0))],
        out_specs=[row(2 * hw), row(2 * hw), row(hw)],
        out_shape=[jax.ShapeDtypeStruct((s, 2 * hw), BF), jax.ShapeDtypeStruct((s, 2 * hw), BF),
                   jax.ShapeDtypeStruct((s, hw), BF)],
        compiler_params=_cp("parallel"),
    )(q_raw, kv_raw, z_a, cos, sin, gains)


def _head_bwd(q_raw, kv_raw, z_a, kr_blk, cos, sin, gains, dq_att, dk_att, dv, heads):
    s = q_raw.shape[0]
    tr = _pick(s, HEAD_ROW_TILE_BWD, 16)
    hw = heads * LANE

    def body(q_ref, kv_ref, kr_ref, cos_ref, sin_ref, g_ref, dq_ref, dk_ref, dv_ref,
             dqr_ref, dkv_ref, dkr_ref, dg_ref):
        i = pl.program_id(0)
        cosv = cos_ref[...]
        sinv = sin_ref[...]
        krv = kr_ref[...]
        kr_sq = krv * krv
        dkr = jnp.zeros((tr, LANE), F32)
        dgs = [jnp.zeros((1, LANE), F32) for _ in range(4)]

        def norm_bwd(xn, xr, sq, dn_out, dr_out, gn, gr):
            r = lax.rsqrt(_lane_sum(sq) / HEAD_QK + NORM_EPS)
            nn = xn * r
            nr = xr * r
            dt = _rope_t(dr_out, cosv, sinv)
            dnn = dn_out * gn
            dnr = dt * gr
            mean = _lane_sum(dnn * nn + dnr * nr) / HEAD_QK
            return (r * (dnn - nn * mean), r * (dnr - nr * mean),
                    jnp.sum(dn_out * nn, axis=0, keepdims=True), jnp.sum(dt * nr, axis=0, keepdims=True))

        for h in range(heads):
            lo = h * LANE
            qn = q_ref[:, lo:lo + LANE]
            qr = q_ref[:, hw + lo:hw + lo + LANE]
            dxn, dxr, g0, g1 = norm_bwd(qn, qr, qn * qn + qr * qr, dq_ref[:, 2 * lo:2 * lo + LANE] * QK_SCALE,
                                        dq_ref[:, 2 * lo + LANE:2 * lo + 2 * LANE] * QK_SCALE,
                                        g_ref[0:1, :], g_ref[1:2, :])
            dqr_ref[:, lo:lo + LANE] = dxn.astype(BF)
            dqr_ref[:, hw + lo:hw + lo + LANE] = dxr.astype(BF)
            kn = kv_ref[:, 2 * lo:2 * lo + LANE]
            dxn, dxr, g2, g3 = norm_bwd(kn, krv, kn * kn + kr_sq, dk_ref[:, 2 * lo:2 * lo + LANE],
                                        dk_ref[:, 2 * lo + LANE:2 * lo + 2 * LANE], g_ref[2:3, :], g_ref[3:4, :])
            dkv_ref[:, 2 * lo:2 * lo + LANE] = dxn.astype(BF)
            dkv_ref[:, 2 * lo + LANE:2 * lo + 2 * LANE] = dv_ref[:, lo:lo + LANE].astype(BF)
            dkr = dkr + dxr
            dgs = [a + b for a, b in zip(dgs, (g0, g1, g2, g3))]
        dkr_ref[...] = dkr

        @pl.when(i == 0)
        def _():
            dg_ref[...] = jnp.zeros_like(dg_ref)

        dg_ref[...] += _rows8(dgs, LANE)

    row = lambda w: pl.BlockSpec((tr, w), lambda i: (i, 0))
    return pl.pallas_call(
        body, name="head_bwd", grid=(s // tr,),
        in_specs=[row(2 * hw), row(2 * hw), pl.BlockSpec((tr, LANE), lambda i: (i, kr_blk)),
                  row(LANE), row(LANE), pl.BlockSpec((SUB, LANE), lambda i: (0, 0)),
                  row(2 * hw), row(2 * hw), row(hw)],
        out_specs=[row(2 * hw), row(2 * hw), row(LANE), pl.BlockSpec((SUB, LANE), lambda i: (0, 0))],
        out_shape=[jax.ShapeDtypeStruct((s, 2 * hw), BF), jax.ShapeDtypeStruct((s, 2 * hw), BF),
                   jax.ShapeDtypeStruct((s, LANE), F32), jax.ShapeDtypeStruct((SUB, LANE), F32)],
        compiler_params=_cp("arbitrary"),
    )(q_raw, kv_raw, z_a, cos, sin, gains, dq_att, dk_att, dv)


def _causal_mask(nrows, ncols, row0):
    rows = lax.broadcasted_iota(jnp.int32, (nrows, ncols), 0) + row0
    cols = lax.broadcasted_iota(jnp.int32, (nrows, ncols), 1)
    return cols <= rows


def _causal_steps(nt, q_major):
    pairs = ([(i, j) for i in range(nt) for j in range(i + 1)] if q_major
             else [(i, j) for j in range(nt) for i in range(j, nt)])
    return (jnp.array([p[0] for p in pairs], jnp.int32), jnp.array([p[1] for p in pairs], jnp.int32))


def _attn_fwd(q_att, k_att, v, heads):
    s = q_att.shape[0]
    t = _pick(s, ATTN_TILE_FWD, LANE)
    nt = s // t
    th = t // 2
    qi, kj = _causal_steps(nt, True)

    def body(qi_ref, kj_ref, q_ref, k_ref, v_ref, o_ref, ob_ref, lse_ref, m_s, l_s, acc_s):
        st = pl.program_id(1)
        i = qi_ref[st]
        j = kj_ref[st]

        @pl.when(j == 0)
        def _():
            m_s[...] = jnp.full_like(m_s, NEG_INF)
            l_s[...] = jnp.zeros_like(l_s)
            acc_s[...] = jnp.zeros_like(acc_s)

        def update(rows, ncol, masked):
            sc = lax.dot_general(q_ref[rows, :], k_ref[0:ncol, :], (((1,), (1,)), ((), ())),
                                 preferred_element_type=F32)
            if masked:
                sc = jnp.where(_causal_mask(rows.stop - rows.start, ncol, rows.start), sc, NEG_INF)
            m_prev = m_s[rows, :]
            m_new = jnp.maximum(m_prev, jnp.max(sc, axis=-1, keepdims=True))
            alpha = jnp.exp2(m_prev - m_new)
            p = jnp.exp2(sc - jnp.tile(m_new, (1, ncol // LANE)))
            l_s[rows, :] = alpha * l_s[rows, :] + jnp.sum(p, axis=-1, keepdims=True)
            acc_s[rows, :] = alpha * acc_s[rows, :] + jnp.dot(p.astype(BF), v_ref[0:ncol, :],
                                                              preferred_element_type=F32)
            m_s[rows, :] = m_new

        @pl.when(j < i)
        def _():
            update(slice(0, t), t, False)

        @pl.when(j == i)
        def _():
            update(slice(0, th), th, True)
            update(slice(th, t), t, True)
            o = acc_s[...] / l_s[...]
            o_ref[...] = o
            ob_ref[...] = o.astype(BF)
            lse_ref[...] = (m_s[...] + jnp.log2(l_s[...]))[:, 0:1]

    q_idx = lambda h, st, qi_r, kj_r: (qi_r[st], h)
    kv_idx = lambda h, st, qi_r, kj_r: (kj_r[st], h)
    return pl.pallas_call(
        body, name="attn_fwd",
        grid_spec=pltpu.PrefetchScalarGridSpec(
            num_scalar_prefetch=2, grid=(heads, qi.shape[0]),
            in_specs=[pl.BlockSpec((t, 2 * LANE), q_idx), pl.BlockSpec((t, 2 * LANE), kv_idx),
                      pl.BlockSpec((t, LANE), kv_idx)],
            out_specs=[pl.BlockSpec((t, LANE), q_idx), pl.BlockSpec((t, LANE), q_idx),
                       pl.BlockSpec((None, t, 1), lambda h, st, qi_r, kj_r: (h, qi_r[st], 0))],
            scratch_shapes=[pltpu.VMEM((t, LANE), F32), pltpu.VMEM((t, LANE), F32), pltpu.VMEM((t, LANE), F32)]),
        out_shape=[jax.ShapeDtypeStruct((s, heads * LANE), F32), jax.ShapeDtypeStruct((s, heads * LANE), BF),
                   jax.ShapeDtypeStruct((heads, s, 1), F32)],
        compiler_params=_cp("parallel", "arbitrary"),
    )(qi, kj, q_att, k_att, v)


def _attn_bwd(q_att, k_att, v, o, lse, d_o, heads, dep=None):
    s = q_att.shape[0]
    t = _pick(s, ATTN_TILE, LANE)
    nt = s // t
    th = t // 2
    qi, kj = _causal_steps(nt, False)

    def body(qi_ref, kj_ref, q_ref, k_ref, v_ref, do_ref, o_ref, lse_ref, *rest):
        dq_ref, dk_ref, dv_ref, dk_s, dv_s = rest[-5:]
        st = pl.program_id(1)
        i = qi_ref[st]
        j = kj_ref[st]

        @pl.when(st == 0)
        def _():
            dq_ref[...] = jnp.zeros_like(dq_ref)

        @pl.when(i == j)
        def _():
            dk_s[...] = jnp.zeros_like(dk_s)
            dv_s[...] = jnp.zeros_like(dv_s)

        def update(rows, ncol, masked):
            nrow = rows.stop - rows.start
            q = q_ref[rows, :]
            k = k_ref[0:ncol, :]
            do = do_ref[rows, :]
            sc = lax.dot_general(q, k, (((1,), (1,)), ((), ())), preferred_element_type=F32)
            if masked:
                sc = jnp.where(_causal_mask(nrow, ncol, rows.start), sc, NEG_INF)
            p = jnp.exp2(sc - lse_ref[rows, :])
            dp = lax.dot_general(do, v_ref[0:ncol, :], (((1,), (1,)), ((), ())), preferred_element_type=F32)
            delta = jnp.sum(do.astype(F32) * o_ref[rows, :], axis=-1, keepdims=True)
            ds = (p * (dp - delta)).astype(BF)
            dv_s[0:ncol, :] += lax.dot_general(p.astype(BF), do, (((0,), (0,)), ((), ())),
                                               preferred_element_type=F32)
            dk_s[0:ncol, :] += lax.dot_general(ds, q, (((0,), (0,)), ((), ())), preferred_element_type=F32)
            out_rows = pl.ds(pl.multiple_of(i * t + rows.start, nrow), nrow)
            dq_ref[out_rows, :] += jnp.dot(ds, k, preferred_element_type=F32)

        @pl.when(i > j)
        def _():
            update(slice(0, t), t, False)

        @pl.when(i == j)
        def _():
            update(slice(0, th), th, True)
            update(slice(th, t), t, True)

        @pl.when(i == nt - 1)
        def _():
            dk_ref[...] = (dk_s[...] * (1.0 / LOG2_E)).astype(BF)
            dv_ref[...] = dv_s[...].astype(BF)

    q_idx = lambda h, st, qi_r, kj_r: (qi_r[st], h)
    kv_idx = lambda h, st, qi_r, kj_r: (kj_r[st], h)
    in_specs = [pl.BlockSpec((t, 2 * LANE), q_idx), pl.BlockSpec((t, 2 * LANE), kv_idx),
                pl.BlockSpec((t, LANE), kv_idx), pl.BlockSpec((t, LANE), q_idx), pl.BlockSpec((t, LANE), q_idx),
                pl.BlockSpec((None, t, 1), lambda h, st, qi_r, kj_r: (h, qi_r[st], 0))]
    args = [q_att, k_att, v, d_o, o, lse]
    if dep is not None:
        in_specs.append(ANY)
        args.append(dep)
    return pl.pallas_call(
        body, name="attn_bwd",
        grid_spec=pltpu.PrefetchScalarGridSpec(
            num_scalar_prefetch=2, grid=(heads, qi.shape[0]),
            in_specs=in_specs,
            out_specs=[pl.BlockSpec((s, 2 * LANE), lambda h, st, qi_r, kj_r: (0, h)),
                       pl.BlockSpec((t, 2 * LANE), kv_idx), pl.BlockSpec((t, LANE), kv_idx)],
            scratch_shapes=[pltpu.VMEM((t, 2 * LANE), F32), pltpu.VMEM((t, LANE), F32)]),
        out_shape=[jax.ShapeDtypeStruct((s, heads * 2 * LANE), F32),
                   jax.ShapeDtypeStruct((s, heads * 2 * LANE), BF),
                   jax.ShapeDtypeStruct((s, heads * LANE), BF)],
        compiler_params=_cp("parallel", "arbitrary"),
    )(qi, kj, *args)


def _sum_parts(parts, name):
    n, r, c = parts.shape
    tr = _pick(r, 512, 8)

    def body(p_ref, o_ref):
        g = p_ref[0].astype(F32)
        for k in range(1, n):
            g = g + p_ref[k].astype(F32)
        o_ref[...] = g

    return pl.pallas_call(
        body, name=name, grid=(r // tr,),
        in_specs=[pl.BlockSpec((n, tr, c), lambda i: (0, i, 0))],
        out_specs=pl.BlockSpec((tr, c), lambda i: (i, 0)),
        out_shape=jax.ShapeDtypeStruct((r, c), F32),
        compiler_params=_cp("parallel"),
    )(parts)


def _adamw(parts, w, m, v, name, by_cols=False):
    n, rp, c = parts.shape
    r = w.shape[0]
    assert by_cols or rp == r
    tr, tc = (r, _pick(c, 256, LANE)) if by_cols else (_pick(r, 256, 16 if r % 16 == 0 else 8), c)

    def body(p_ref, w_ref, m_ref, v_ref, g_ref, d_ref, mo_ref, vo_ref):
        g = p_ref[0].astype(F32)
        for k in range(1, n):
            g = g + p_ref[k].astype(F32)
        g = g[:r] if by_cols else g
        m_new = ADAM_B1 * m_ref[...] + (1.0 - ADAM_B1) * g
        v_new = ADAM_B2 * v_ref[...] + (1.0 - ADAM_B2) * jnp.square(g)
        m_hat = m_new / (1.0 - ADAM_B1 ** ADAM_STEP)
        v_hat = v_new / (1.0 - ADAM_B2 ** ADAM_STEP)
        g_ref[...] = g
        d_ref[...] = -ADAM_LR * (m_hat / (jnp.sqrt(v_hat) + ADAM_EPS) + ADAM_WD * w_ref[...])
        mo_ref[...] = m_new
        vo_ref[...] = v_new

    idx = (lambda i: (0, i)) if by_cols else (lambda i: (i, 0))
    spec = pl.BlockSpec((tr, tc), idx)
    sh = jax.ShapeDtypeStruct((r, c), F32)
    return pl.pallas_call(
        body, name=name, grid=(c // tc if by_cols else r // tr,),
        in_specs=[pl.BlockSpec((n, rp if by_cols else tr, tc), lambda i: (0,) + idx(i)), spec, spec, spec],
        out_specs=[spec] * 4, out_shape=[sh] * 4,
        compiler_params=_cp("parallel"),
    )(parts, w, m, v)


def _place():
    x, y, c = lax.axis_index("x"), lax.axis_index("y"), lax.axis_index("c")
    chips = [(1 - x, y), (x, 1 - y), (1 - x, 1 - y)]
    return x, y, c, chips


def _all_gather(shards, name, dep=None):
    n = len(shards)
    deps = [] if dep is None else list(dep)

    def body(*refs):
        ins, outs = refs[:n], refs[n + len(deps):2 * n + len(deps)]
        send_sems, recv_sems, local_sems = refs[2 * n + len(deps):]
        x, y, c, chips = _place()
        me, sibling = (x, y, c), (x, y, 1 - c)

        def slot(w, p):
            return outs[w].at[4 * p[0] + 2 * p[1] + p[2]]

        def copy(w, k, block, to, src=None):
            return pltpu.make_async_remote_copy(
                src_ref=slot(w, block) if src is None else src, dst_ref=slot(w, block),
                send_sem=send_sems.at[w, k], recv_sem=recv_sems.at[w, k], device_id=to, device_id_type=MESH)

        first = []
        for w in range(n):
            first += [copy(w, 1 + j, me, (*chip, c), src=ins[w]) for j, chip in enumerate(chips)]
            first.append(copy(w, 0, me, sibling, src=ins[w]))
        for cp in first:
            cp.start()
        mine = [pltpu.make_async_copy(ins[w], slot(w, me), local_sems.at[w]) for w in range(n)]
        for cp in mine:
            cp.start()
        passed = []
        for w in range(n):
            for j, chip in enumerate(chips):
                copy(w, 1 + j, (*chip, c), me).wait_recv()
                cp = copy(w, 4 + j, (*chip, c), sibling)
                cp.start()
                passed.append(cp)
        for w in range(n):
            copy(w, 0, sibling, me).wait_recv()
            for j, chip in enumerate(chips):
                copy(w, 4 + j, (*chip, 1 - c), me).wait_recv()
        for cp in first + passed:
            cp.wait_send()
        for cp in mine:
            cp.wait()

    return pl.pallas_call(
        body, name=name,
        in_specs=[ANY] * (n + len(deps)), out_specs=[ANY] * n,
        out_shape=[jax.ShapeDtypeStruct((N_DEV,) + a.shape, a.dtype) for a in shards],
        scratch_shapes=[pltpu.SemaphoreType.DMA((n, 7)), pltpu.SemaphoreType.DMA((n, 7)),
                        pltpu.SemaphoreType.DMA((n,))],
    )(*shards, *deps)


HBM = pl.BlockSpec(memory_space=pltpu.HBM)
SEM = pl.BlockSpec(memory_space=pltpu.SEMAPHORE)
EFFECT = pltpu.SideEffectType.DATAFLOW_SIDE_EFFECTING
PEERS = [(dx, dy, dc) for dx in (1, 0) for dy in (1, 0) for dc in (0, 1) if (dx, dy, dc) != (0, 0, 0)]


def _peer(x, y, c, flip):
    dx, dy, dc = flip
    return (1 - x if dx else x, 1 - y if dy else y, 1 - c if dc else c)


def _exchange_copies(srcs, lands, send, recv, loc, gather):
    x, y, c, _ = _place()
    me = 4 * x + 2 * y + c
    remote, local = [], []
    for w in range(len(srcs)):
        for k, flip in enumerate(PEERS):
            px, py, pc = _peer(x, y, c, flip)
            src = srcs[w] if gather else srcs[w].at[4 * px + 2 * py + pc]
            remote.append(pltpu.make_async_remote_copy(
                src_ref=src, dst_ref=lands[w].at[me], send_sem=send[w].at[k], recv_sem=recv[w].at[k],
                device_id=(px, py, pc), device_id_type=MESH))
        local.append(pltpu.make_async_copy(srcs[w] if gather else srcs[w].at[me], lands[w].at[me], loc[w]))
    return remote, local


class _Exchange:
    def __init__(self, srcs, lands, send, recv, loc, token, gather):
        self.srcs, self.lands, self.send, self.recv, self.loc = srcs, lands, send, recv, loc
        self.token, self.gather = token, gather


def _exchange_start(srcs, gather, name, dep=None):
    n = len(srcs)
    deps = [] if dep is None else [dep]
    land_shapes = [((N_DEV,) + a.shape) if gather else a.shape for a in srcs]
    lands = [pltpu.with_memory_space_constraint(lax.empty(sh, a.dtype), pltpu.HBM) for sh, a in zip(land_shapes, srcs)]
    srcs = [pltpu.with_memory_space_constraint(a, pltpu.HBM) for a in srcs]

    def body(*refs):
        src_refs, land_refs = refs[:n], refs[n:2 * n]
        outs = refs[2 * n + len(deps):]
        send, recv, loc = outs[:n], outs[n:2 * n], outs[2 * n:3 * n]
        token = outs[-1]
        remote, local = _exchange_copies(src_refs, land_refs, send, recv, loc, gather)
        for cp in remote + local:
            cp.start()
        token[...] = jnp.zeros_like(token)

    out_shape = ([pltpu.SemaphoreType.DMA((len(PEERS),))] * (2 * n) + [pltpu.SemaphoreType.DMA(())] * n
                 + [pltpu.HBM(a.shape, a.dtype) for a in srcs] + [pltpu.HBM(a.shape, a.dtype) for a in lands]
                 + [jax.ShapeDtypeStruct((SUB, LANE), F32)])
    res = pl.pallas_call(
        body, name=name, out_shape=out_shape,
        in_specs=[HBM] * (2 * n) + [ANY] * len(deps),
        out_specs=[SEM] * (3 * n) + [HBM] * (2 * n) + [pl.BlockSpec(memory_space=pltpu.VMEM)],
        input_output_aliases={i: 3 * n + i for i in range(2 * n)},
        compiler_params=pltpu.CompilerParams(has_side_effects=EFFECT),
    )(*srcs, *lands, *deps)
    return _Exchange(res[3 * n:4 * n], res[4 * n:5 * n], res[:n], res[n:2 * n], res[2 * n:3 * n], res[-1], gather)


def _exchange_wait(ex, idxs, after, name):
    n = len(idxs)
    srcs = [ex.srcs[i] for i in idxs]
    lands = [ex.lands[i] for i in idxs]
    sems = [ex.send[i] for i in idxs] + [ex.recv[i] for i in idxs] + [ex.loc[i] for i in idxs]
    gather = ex.gather

    def body(*refs):
        src_refs, land_refs = refs[:n], refs[n:2 * n]
        send, recv, loc = refs[2 * n:3 * n], refs[3 * n:4 * n], refs[4 * n:5 * n]
        remote, local = _exchange_copies(src_refs, land_refs, send, recv, loc, gather)
        for cp in remote:
            cp.wait_send()
            cp.wait_recv()
        for cp in local:
            cp.wait()

    res = pl.pallas_call(
        body, name=name,
        out_shape=[pltpu.HBM(a.shape, a.dtype) for a in srcs] + [pltpu.HBM(a.shape, a.dtype) for a in lands],
        in_specs=[HBM] * (2 * n) + [SEM] * (3 * n) + [ANY],
        out_specs=[HBM] * (2 * n),
        input_output_aliases={i: i for i in range(2 * n)},
        compiler_params=pltpu.CompilerParams(has_side_effects=EFFECT),
    )(*srcs, *lands, *sems, after)
    return res[n:]


def _gather2_copies(srcs, lands, send, recv_ici, recv_sib, loc):
    x, y, c, chips = _place()
    me = 4 * x + 2 * y + c
    remote, local = [], []
    for w in range(len(srcs)):
        remote.append(pltpu.make_async_remote_copy(
            src_ref=srcs[w], dst_ref=lands[w].at[me], send_sem=send[w].at[0], recv_sem=recv_sib[w],
            device_id=(x, y, 1 - c), device_id_type=MESH))
        for j, chip in enumerate(chips):
            remote.append(pltpu.make_async_remote_copy(
                src_ref=srcs[w], dst_ref=lands[w].at[me], send_sem=send[w].at[1 + j], recv_sem=recv_ici[w].at[j],
                device_id=(*chip, c), device_id_type=MESH))
        local.append(pltpu.make_async_copy(srcs[w], lands[w].at[me], loc[w]))
    return remote, local


def _gather2_forwards(lands, fsend, frecv, arrived=None):
    x, y, c, chips = _place()
    cps = []
    for w in range(len(lands)):
        for j, chip in enumerate(chips):
            slot = lands[w].at[4 * chip[0] + 2 * chip[1] + c]
            cp = pltpu.make_async_remote_copy(
                src_ref=slot, dst_ref=slot, send_sem=fsend[w].at[j], recv_sem=frecv[w].at[j],
                device_id=(x, y, 1 - c), device_id_type=MESH)
            if arrived is not None:
                pltpu.make_async_remote_copy(
                    src_ref=slot, dst_ref=slot, send_sem=fsend[w].at[j], recv_sem=arrived[w].at[j],
                    device_id=(x, y, 1 - c), device_id_type=MESH).wait_recv()
            cps.append(cp)
    return cps


def _gather2(shards, between, name):
    n = len(shards)
    srcs = [pltpu.with_memory_space_constraint(a, pltpu.HBM) for a in shards]
    lands = [pltpu.with_memory_space_constraint(lax.empty((N_DEV,) + a.shape, a.dtype), pltpu.HBM) for a in shards]
    hbm_like = lambda arrs: [pltpu.HBM(a.shape, a.dtype) for a in arrs]
    tok = jax.ShapeDtypeStruct((SUB, LANE), F32)
    vmem = pl.BlockSpec(memory_space=pltpu.VMEM)
    side = pltpu.CompilerParams(has_side_effects=EFFECT)

    def start(*refs):
        src_refs, land_refs = refs[:n], refs[n:2 * n]
        outs = refs[2 * n:]
        send, recv_ici, recv_sib, loc = outs[:n], outs[n:2 * n], outs[2 * n:3 * n], outs[3 * n:4 * n]
        remote, local = _gather2_copies(src_refs, land_refs, send, recv_ici, recv_sib, loc)
        for cp in remote + local:
            cp.start()
        outs[-1][...] = jnp.zeros((SUB, LANE), F32)

    res = pl.pallas_call(
        start, name=name + "_start",
        out_shape=([pltpu.SemaphoreType.DMA((4,))] * n + [pltpu.SemaphoreType.DMA((3,))] * n
                   + [pltpu.SemaphoreType.DMA(())] * (2 * n) + hbm_like(srcs) + hbm_like(lands) + [tok]),
        in_specs=[HBM] * (2 * n), out_specs=[SEM] * (4 * n) + [HBM] * (2 * n) + [vmem],
        input_output_aliases={i: 4 * n + i for i in range(2 * n)}, compiler_params=side,
    )(*srcs, *lands)
    send, recv_ici, recv_sib, loc = res[:n], res[n:2 * n], res[2 * n:3 * n], res[3 * n:4 * n]
    srcs, lands, token = res[4 * n:5 * n], res[5 * n:6 * n], res[-1]

    done = between(token)
    after = jax.tree_util.tree_leaves(done)

    def forward(*refs):
        land_refs, arrived = refs[:n], refs[n:2 * n]
        outs = refs[2 * n + len(after):]
        fsend, frecv = outs[:n], outs[n:2 * n]
        for cp in _gather2_forwards(land_refs, fsend, frecv, arrived):
            cp.start()
        outs[-1][...] = jnp.zeros((SUB, LANE), F32)

    res = pl.pallas_call(
        forward, name=name + "_forward",
        out_shape=[pltpu.SemaphoreType.DMA((3,))] * (2 * n) + hbm_like(lands) + [tok],
        in_specs=[HBM] * n + [SEM] * n + [ANY] * len(after), out_specs=[SEM] * (2 * n) + [HBM] * n + [vmem],
        input_output_aliases={i: 2 * n + i for i in range(n)}, compiler_params=side,
    )(*lands, *recv_ici, *after)
    fsend, frecv, lands, token = res[:n], res[n:2 * n], res[2 * n:3 * n], res[-1]

    def wait(*refs):
        src_refs, land_refs = refs[:n], refs[n:2 * n]
        sems = refs[2 * n:7 * n]
        send, recv_sib, loc, fsend, frecv = (sems[k * n:(k + 1) * n] for k in range(5))
        remote, local = _gather2_copies(src_refs, land_refs, send, send, recv_sib, loc)
        for w in range(n):
            for cp in remote[4 * w:4 * w + 4]:
                cp.wait_send()
            remote[4 * w].wait_recv()
        for cp in local:
            cp.wait()
        for cp in _gather2_forwards(land_refs, fsend, frecv):
            cp.wait_send()
            cp.wait_recv()

    res = pl.pallas_call(
        wait, name=name + "_wait", out_shape=hbm_like(srcs) + hbm_like(lands),
        in_specs=[HBM] * (2 * n) + [SEM] * (5 * n) + [ANY], out_specs=[HBM] * (2 * n),
        input_output_aliases={i: i for i in range(2 * n)}, compiler_params=side,
    )(*srcs, *lands, *send, *recv_sib, *loc, *fsend, *frecv, token)
    return res[n:], done


def _after(token, a):
    return a + token[0:1, 0:1].astype(a.dtype)


def _unblock(w3):
    nb, k, nbw = w3.shape
    return w3.transpose(1, 0, 2).reshape(k, nb * nbw)


def _block(w, nb):
    k, n = w.shape
    return w.reshape(k, nb, n // nb).transpose(1, 0, 2)


def kernel(x, positions, ln1_g, w_in, b_gate, conv_w, w_conv_out, q_a_g, w_q_b, kv_a_g, w_kv_b, q_norm_g, k_norm_g, w_mla_out, w_o, ln2_g, w_ffn_up, ffn_conv_w, ffn_conv_b, w_ffn_down, loss_target, m_ln1_g, m_w_in, m_b_gate, m_conv_w, m_w_conv_out, m_q_a_g, m_w_q_b, m_kv_a_g, m_w_kv_b, m_q_norm_g, m_k_norm_g, m_w_mla_out, m_w_o, m_ln2_g, m_w_ffn_up, m_ffn_conv_w, m_ffn_conv_b, m_w_ffn_down, v_ln1_g, v_w_in, v_b_gate, v_conv_w, v_w_conv_out, v_q_a_g, v_w_q_b, v_kv_a_g, v_w_kv_b, v_q_norm_g, v_k_norm_g, v_w_mla_out, v_w_o, v_ln2_g, v_w_ffn_up, v_ffn_conv_w, v_ffn_conv_b, v_w_ffn_down):
    s, d = x.shape[1], x.shape[2]
    conv = conv_w.shape[2] * N_DEV
    ql, kvl = q_a_g.shape[1], kv_a_g.shape[1]
    heads = w_q_b.shape[2] * N_DEV // HEAD_QK
    dff = w_ffn_down.shape[1] * N_DEV
    hw = heads * LANE
    conv3 = 3 * conv
    kr_off = conv3 + ql
    kv_off = -(-(kr_off + LANE) // kvl) * kvl
    wa = kv_off + kvl
    assert conv3 % ql == 0 and kr_off % LANE == 0
    xs = x[0]
    tgt = loss_target[0]
    pos = positions.reshape(s, 1)

    nin = w_in.shape[2]
    big = dict(w_in=w_in[0].T, w_conv_out=w_conv_out[0], w_q_b=w_q_b[0], w_kv_b=w_kv_b[0],
               w_mla_out=w_mla_out[0], w_o=w_o[0], w_ffn_up=w_ffn_up[0], w_ffn_down=w_ffn_down[0])
    names = list(big)
    rest = names[1:]
    early = {}

    def while_w_in_travels(token):
        early["ag"] = _exchange_start([big[k].astype(BF) for k in rest], True, "gather_rest_start", dep=token)
        cos_sin = _rope_tables(pos)
        return cos_sin, _rms_fwd(xs, _after(early["ag"].token, ln1_g), d, 0, "rms1_fwd")

    first, ((cos, sin), u1) = _gather2([big["w_in"].astype(BF), _pad8(conv_w[0]), _pad8(ffn_conv_w[0])],
                                       while_w_in_travels, "gather_w_in")
    ag = early["ag"]
    cw8 = _unblock(first[1])
    fcw8 = _unblock(first[2])

    def landed(keys, after, name):
        return _exchange_wait(ag, [rest.index(k) for k in keys], after, name)

    w_in_t = first[0].reshape(N_DEV * nin, d)
    g_off = kr_off + kvl + ROPE
    w_a_t = jnp.concatenate([w_in_t[:kr_off], _lay_rows(w_in_t[kr_off + kvl:g_off]),
                             jnp.zeros((kv_off - kr_off - LANE, d), BF), w_in_t[kr_off:kr_off + kvl]], axis=0)[None]
    w_g_t = w_in_t[g_off:][None]
    gains = _pad8(jnp.concatenate([q_norm_g[:, :NOPE], _lay(q_norm_g[:, NOPE:]),
                                   k_norm_g[:, :NOPE], _lay(k_norm_g[:, NOPE:])], axis=0))
    kr_blk = kr_off // LANE

    z_a = _mm_nt(u1, w_a_t, "mm_z_a")
    z_g = _mm_nt(u1, w_g_t, "mm_z_g", out_dtype=BF)
    p = _conv_mix_fwd(z_a, cw8, conv)
    w_co, w_qb, w_kv = landed(["w_conv_out", "w_q_b", "w_kv_b"], p, "gather_wait_mixers")
    w_co = _unblock(w_co)[None]
    w_kv = _unblock(w_kv)[None]
    wq_full = _unblock(w_qb).reshape(ql, heads, HEAD_QK)
    w_q = jnp.concatenate([wq_full[:, :, :NOPE].reshape(ql, hw), _lay(wq_full[:, :, NOPE:]).reshape(ql, hw)],
                          axis=1)[None]
    yc = _mm_nn(p, w_co, "mm_y_conv", out_dtype=BF)
    qn, q_raw = _rms_mm_nn(z_a, q_a_g, conv3 // ql, w_q, "mm_q")
    kvn, kv_raw = _rms_mm_nn(z_a, kv_a_g, kv_off // kvl, w_kv, "mm_kv")
    q_att, k_att, v_bf = _head_fwd(q_raw, kv_raw, z_a, kr_blk, cos, sin, gains, heads)
    o, o_bf, lse = _attn_fwd(q_att, k_att, v_bf, heads)
    w_mo, w_oo = landed(["w_mla_out", "w_o"], lse, "gather_wait_outs")
    w_mo = w_mo.reshape(1, hw, d)
    w_oo = w_oo.reshape(1, d, d)
    ym, mix = _mla_out_gate(o_bf, w_mo, z_g, b_gate, yc)
    h1, u2 = _residual_norm(mix, w_oo, xs, ln2_g)
    w_up, = landed(["w_ffn_up"], u2, "gather_wait_ffn_up")
    a_g, a_u, f = _ffn_up_act(u2, w_up, fcw8, ffn_conv_b, dff)
    w_dn, = landed(["w_ffn_down"], f, "gather_wait_ffn_down")
    w_dn = w_dn.reshape(1, dff, d)
    dy, dy_bf, loss_part = _mm_nn_loss(f, w_dn, h1, tgt, "mm_ffn_down_loss")

    g_dn = _mm_tn(f, dy_bf, 1, "mm_g_ffn_down").reshape(N_DEV, dff // N_DEV, d)
    rs_dn = _exchange_start([g_dn], False, "reduce_ffn_down_start")
    d_f = _mm_nt(dy_bf, w_dn, "mm_d_f", dep=rs_dn.token)
    d_xg, d_xu, dfw_g, dfw_u = _ffn_act_bwd(a_g, a_u, d_f, fcw8, ffn_conv_b, dff)
    half = N_DEV // 2
    g_up = _mm_tn(u2, d_xg, half, "mm_g_ffn_up_gate", into=lax.empty((N_DEV, d, 2 * dff // N_DEV), BF))
    g_up = _mm_tn(u2, d_xu, half, "mm_g_ffn_up_up", into=g_up, blk0=half)
    rs_up = _exchange_start([g_up], False, "reduce_ffn_up_start")
    d_u2 = _mm_nt([d_xg, d_xu], w_up, "mm_d_u2", out_dtype=BF, dep=rs_up.token)
    d_h1, d_h1_bf, dg_ln2 = _rms_bwd(h1, d_u2, ln2_g, d, 0, "rms2_bwd", extra=dy, also_bf16=True)
    g_oo = _mm_tn(mix, d_h1_bf, 1, "mm_g_w_o").reshape(N_DEV, d // N_DEV, d)
    d_zga, d_zgb, d_yc, d_ym, dba, dbb = _d_mix_gate(d_h1_bf, w_oo, z_g, b_gate, yc, ym)
    g_co = _block(_mm_tn(p, d_yc, 1, "mm_g_conv_out")[0], N_DEV)
    g_mo = _mm_tn(o_bf, d_ym, 1, "mm_g_mla_out").reshape(N_DEV, hw // N_DEV, d)
    rs_mix = _exchange_start([g_oo, g_co, g_mo], False, "reduce_mixers_start")
    d_p = _mm_nt(d_yc, w_co, "mm_d_p", dep=rs_mix.token)
    d_o = _mm_nt(d_ym, w_mo, "mm_d_o", out_dtype=BF)
    d_zb, d_zc, d_zv, dcw = _conv_mix_bwd(z_a, d_p, cw8, conv)
    dq_att, dk_att, dv = _attn_bwd(q_att, k_att, v_bf, o, lse, d_o, heads, dep=rs_mix.token)
    d_q_raw, d_kv_raw, d_kr, dgains = _head_bwd(q_raw, kv_raw, z_a, kr_blk, cos, sin, gains, dq_att, dk_att, dv, heads)
    g_q2 = _mm_tn(qn, d_q_raw, 1, "mm_g_q")[0]
    g_qb = _block(jnp.concatenate([g_q2[:, :hw].reshape(ql, heads, NOPE),
                                   _unlay(g_q2[:, hw:].reshape(ql, heads, LANE))], axis=2).reshape(ql, heads * HEAD_QK), N_DEV)
    g_kv = _block(_mm_tn(kvn, d_kv_raw, 1, "mm_g_kv")[0], N_DEV)
    rs_qkv = _exchange_start([g_qb, g_kv], False, "reduce_qkv_start")
    d_ql, dg_qa = _mm_nt_rms_bwd(d_q_raw, w_q, z_a, q_a_g, conv3 // ql, "mm_d_q_lat", dep=rs_qkv.token)
    d_kvl, dg_kva = _mm_nt_rms_bwd(d_kv_raw, w_kv, z_a, kv_a_g, kv_off // kvl, "mm_d_kv_lat")
    d_z_a = jnp.concatenate([d_zb, d_zc, d_zv, d_ql, d_kr.astype(BF), jnp.zeros((s, kv_off - kr_off - LANE), BF),
                             d_kvl], axis=1)
    g_a = _mm_tn(d_z_a, u1, 1, "mm_g_w_a")[0]
    g_ga = _mm_tn(d_zga, u1, 1, "mm_g_w_ga")[0]
    g_gb = _mm_tn(d_zgb, u1, 1, "mm_g_w_gb")[0]
    g_in = jnp.concatenate([g_a[:kr_off], g_a[kv_off:kv_off + kvl], g_a[kr_off:kr_off + HALF],
                            g_a[kr_off + 2 * HALF:kr_off + 3 * HALF], g_ga, g_gb], axis=0).reshape(N_DEV, nin, d)
    rs_in = _exchange_start([g_in], False, "reduce_w_in_start")
    d_u1 = _mm_nn(d_z_a, w_a_t, "mm_d_u1_a", dep=rs_in.token)
    d_u1 = _mm_nn([d_zga, d_zgb], w_g_t, "mm_d_u1_g", add=d_u1)
    grad_x, dg_ln1 = _rms_bwd(xs, d_u1, ln1_g, d, 0, "rms1_bwd", extra=d_h1)

    summed = {}
    summed["w_ffn_down"], = _exchange_wait(rs_dn, [0], grad_x, "reduce_ffn_down_wait")
    summed["w_ffn_up"], = _exchange_wait(rs_up, [0], grad_x, "reduce_ffn_up_wait")
    summed["w_o"], summed["w_conv_out"], summed["w_mla_out"] = _exchange_wait(rs_mix, [0, 1, 2], grad_x, "reduce_mixers_wait")
    summed["w_q_b"], summed["w_kv_b"] = _exchange_wait(rs_qkv, [0, 1], grad_x, "reduce_qkv_wait")
    loc = locals()
    out = {}
    for k in rest:
        out[k] = _adamw(summed[k], big[k], loc["m_" + k][0], loc["v_" + k][0], "adamw_" + k)

    small = dict(ln1_g=dg_ln1[0:1], b_gate=jnp.concatenate([dba[0:1], dbb[0:1]], axis=1), q_a_g=dg_qa[0:1],
                 kv_a_g=dg_kva[0:1],
                 q_norm_g=jnp.concatenate([dgains[0:1], _unlay(dgains[1:2])], axis=1),
                 k_norm_g=jnp.concatenate([dgains[2:3], _unlay(dgains[3:4])], axis=1),
                 ln2_g=dg_ln2[0:1], ffn_conv_b=jnp.concatenate([dfw_g[3:4], dfw_u[3:4]], axis=1))
    small_names = list(small)
    extra = [dcw[0:3].reshape(1, -1), jnp.concatenate([dfw_g[0:3], dfw_u[0:3]], axis=1).reshape(1, -1),
             loss_part[0:1, 0:1]]
    flat = jnp.concatenate([small[k] for k in small_names] + extra, axis=1)
    n_flat = flat.shape[1]
    rows = -(-n_flat // (SUB * LANE)) * SUB
    flat = jnp.pad(flat, ((0, 0), (0, rows * LANE - n_flat))).reshape(rows, LANE)
    total = _sum_parts(_all_gather([flat], "gather_small", dep=[out[k][0] for k in rest])[0], "sum_small").reshape(1, rows * LANE)
    off = 0
    small_g = {}
    for k in small_names:
        small_g[k] = total[:, off:off + small[k].shape[1]]
        off += small[k].shape[1]
    me = 4 * lax.axis_index("x") + 2 * lax.axis_index("y") + lax.axis_index("c")
    cwn, fcwn = conv // N_DEV, 2 * dff // N_DEV
    g_cw = lax.dynamic_slice_in_dim(total[:, off:off + 3 * conv].reshape(3, conv), me * cwn, cwn, axis=1)
    off += 3 * conv
    g_fcw = lax.dynamic_slice_in_dim(total[:, off:off + 6 * dff].reshape(3, 2 * dff), me * fcwn, fcwn, axis=1)
    off += 6 * dff
    loss = total[0, off]

    summed["w_in"], = _exchange_wait(rs_in, [0], total, "reduce_w_in_wait")
    out["w_in"] = [r.T for r in _adamw(summed["w_in"], big["w_in"], m_w_in[0].T, v_w_in[0].T, "adamw_w_in",
                                       by_cols=True)]
    small_w = dict(ln1_g=ln1_g, b_gate=b_gate, q_a_g=q_a_g, kv_a_g=kv_a_g, q_norm_g=q_norm_g, k_norm_g=k_norm_g,
                   ln2_g=ln2_g, ffn_conv_b=ffn_conv_b, conv_w=conv_w[0].reshape(1, -1),
                   ffn_conv_w=ffn_conv_w[0].reshape(1, -1))
    small_g["conv_w"] = g_cw.reshape(1, -1)
    small_g["ffn_conv_w"] = g_fcw.reshape(1, -1)
    packed_names = list(small_w)

    def pack(get):
        vflat = jnp.concatenate([get(k).reshape(1, -1) for k in packed_names], axis=1)
        nr = -(-vflat.shape[1] // (SUB * LANE)) * SUB
        return jnp.pad(vflat, ((0, 0), (0, nr * LANE - vflat.shape[1])), constant_values=1.0).reshape(nr, LANE)

    res = _adamw(pack(lambda k: small_g[k])[None], pack(lambda k: small_w[k]), pack(lambda k: loc["m_" + k]),
                 pack(lambda k: loc["v_" + k]), "adamw_small")
    res = [r.reshape(1, -1) for r in res]
    off = 0
    for k in packed_names:
        shape = loc[k].shape
        size = small_w[k].shape[1]
        out[k] = [r[:, off:off + size].reshape(shape) for r in res]
        off += size
    for k in names:
        out[k] = [r[None] for r in out[k]]

    order = ["ln1_g", "w_in", "b_gate", "conv_w", "w_conv_out", "q_a_g", "w_q_b", "kv_a_g", "w_kv_b", "q_norm_g",
             "k_norm_g", "w_mla_out", "w_o", "ln2_g", "w_ffn_up", "ffn_conv_w", "ffn_conv_b", "w_ffn_down"]
    return (loss, grad_x[None], *[out[k][0] for k in order], *[out[k][1] for k in order],
            *[out[k][2] for k in order], *[out[k][3] for k in order])
```

```python
import jax
import jax.numpy as jnp
from jax import lax
from jax.experimental import pallas as pl
from jax.experimental.pallas import tpu as pltpu

BF = jnp.bfloat16
F32 = jnp.float32
MESH = pl.DeviceIdType.MESH
N_DEV = 8

NOPE = 128
ROPE = 64
HALF = ROPE // 2
HEAD_QK = NOPE + ROPE
LANE = 128
SUB = 8
QK_SCALE = HEAD_QK ** -0.5
LOG2_E = 1.4426950408889634
NORM_EPS = 1e-6
NEG_INF = -1e30
ROPE_THETA = 10000.0
ADAM_LR = 0.001
ADAM_B1 = 0.9
ADAM_B2 = 0.999
ADAM_EPS = 1e-08
ADAM_WD = 0.01
ADAM_STEP = 10

VMEM_LIMIT = 52 * 1024 * 1024
MM_TM, MM_TN, MM_TK, MM_TS = 1024, 1536, 2048, 2048
ROW_TILE, ROW_TILE_BWD = 512, 256
HEAD_ROW_TILE, HEAD_ROW_TILE_BWD = 256, 256
COL_TILE = 512
FFN_COL_TILE = 1408
ATTN_TILE = 1024
ATTN_TILE_FWD = 1024
ANY = pl.BlockSpec(memory_space=pl.ANY)


def _pick(n, target, mult):
    t = (min(n, target) // mult) * mult
    while t > 0:
        if n % t == 0:
            return t
        t -= mult
    raise ValueError(f"no tile for {n} (target {target}, multiple {mult})")


def _cp(*sem):
    return pltpu.CompilerParams(dimension_semantics=sem, vmem_limit_bytes=VMEM_LIMIT)


def _accumulate(kk, nk, acc, part, finish):
    if nk == 1:
        finish(part())
        return

    @pl.when(kk == 0)
    def _():
        acc[...] = part()

    @pl.when((kk > 0) & (kk < nk - 1))
    def _():
        acc[...] += part()

    @pl.when(kk == nk - 1)
    def _():
        finish(acc[...] + part())


def _mm_call(body, name, grid, in_specs, args, out_spec, out_shape, acc_shape, nk, dep):
    if dep is not None:
        in_specs = in_specs + [ANY]
        args = args + [dep]
    return pl.pallas_call(
        body, name=name, grid=grid, in_specs=in_specs, out_specs=out_spec, out_shape=out_shape,
        scratch_shapes=[pltpu.VMEM(acc_shape, F32)] if nk > 1 else [],
        compiler_params=_cp("parallel", "parallel", "arbitrary"),
    )(*args)


def _mm_nn_loss(a, b3, add, target, name):
    m, k = a.shape
    _, k2, n = b3.shape
    assert k == k2 and b3.shape[0] == 1
    tm = _pick(m, MM_TM, 16)
    tn = _pick(n, MM_TN, LANE)
    tk = _pick(k, MM_TK, LANE)
    nk = k // tk

    def body(a_ref, b_ref, c_ref, t_ref, dy_ref, dyb_ref, l_ref, acc):
        kk = pl.program_id(2)

        @pl.when((pl.program_id(0) == 0) & (pl.program_id(1) == 0) & (kk == 0))
        def _():
            l_ref[...] = jnp.zeros_like(l_ref)

        def part():
            return jnp.dot(a_ref[...].astype(BF), b_ref[0].astype(BF), preferred_element_type=F32)

        def finish(r):
            e = r + c_ref[...] - t_ref[...]
            dy_ref[...] = e / n
            dyb_ref[...] = (e / n).astype(BF)
            l_ref[...] += 0.5 * jnp.sum(jnp.sum(e * e, axis=-1, keepdims=True), axis=0, keepdims=True) / n

        _accumulate(kk, nk, acc, part, finish)

    tile = pl.BlockSpec((tm, tn), lambda i, j, kk: (i, j))
    return pl.pallas_call(
        body, name=name, grid=(m // tm, n // tn, nk),
        in_specs=[pl.BlockSpec((tm, tk), lambda i, j, kk: (i, kk)),
                  pl.BlockSpec((1, tk, tn), lambda i, j, kk: (0, kk, j)), tile, tile],
        out_specs=[tile, tile, pl.BlockSpec((SUB, LANE), lambda i, j, kk: (0, 0))],
        out_shape=[jax.ShapeDtypeStruct((m, n), F32), jax.ShapeDtypeStruct((m, n), BF),
                   jax.ShapeDtypeStruct((SUB, LANE), F32)],
        scratch_shapes=[pltpu.VMEM((tm, tn), F32)],
        compiler_params=_cp("arbitrary", "arbitrary", "arbitrary"),
    )(a, b3, add, target)


def _mm_nn(a, b3, name, add=None, out_dtype=F32, blk0=0, nblk=None, dep=None):
    pair = isinstance(a, (list, tuple))
    a_list = list(a) if pair else [a]
    m, ka = a_list[0].shape
    k = ka * len(a_list)
    nb_all, k2, nbw = b3.shape
    assert k == k2
    nblk = nb_all - blk0 if nblk is None else nblk
    n = nblk * nbw
    tm = _pick(m, MM_TM if k > MM_TM else 2 * MM_TM, 16)
    tn = _pick(nbw, MM_TN, LANE)
    tk = _pick(ka, MM_TK, LANE)
    per = nbw // tn
    nk = k // tk
    nka = ka // tk
    na_ops = len(a_list)

    def body(*refs):
        a_refs, b_ref = refs[:na_ops], refs[na_ops]
        c_ref = refs[na_ops + 1] if add is not None else None
        o_ref = refs[na_ops + 1 + (add is not None) + (dep is not None)]
        acc = refs[-1]
        kk = pl.program_id(2)

        def part():
            av = a_refs[0][...] if not pair else jnp.where(kk < nka, a_refs[0][...], a_refs[1][...])
            return jnp.dot(av.astype(BF), b_ref[...].astype(BF), preferred_element_type=F32)

        def finish(r):
            if add is not None:
                r = r + c_ref[...]
            o_ref[...] = r.astype(out_dtype)

        _accumulate(kk, nk, acc, part, finish)

    if pair:
        in_specs = [pl.BlockSpec((tm, tk), lambda i, j, kk: (i, jnp.minimum(kk, nka - 1))),
                    pl.BlockSpec((tm, tk), lambda i, j, kk: (i, jnp.maximum(kk - nka, 0)))]
    else:
        in_specs = [pl.BlockSpec((tm, tk), lambda i, j, kk: (i, kk))]
    in_specs.append(pl.BlockSpec((None, tk, tn), lambda i, j, kk: (blk0 + j // per, kk, j % per)))
    args = a_list + [b3]
    if add is not None:
        in_specs.append(pl.BlockSpec((tm, tn), lambda i, j, kk: (i, j)))
        args.append(add)
    return _mm_call(body, name, (m // tm, n // tn, nk), in_specs, args,
                    pl.BlockSpec((tm, tn), lambda i, j, kk: (i, j)), jax.ShapeDtypeStruct((m, n), out_dtype),
                    (tm, tn), nk, dep)


def _mm_nt(a, b3, name, add=None, out_dtype=F32, blk0=0, nblk=None, dep=None):
    pair = isinstance(a, (list, tuple))
    a_list = list(a) if pair else [a]
    m, na = a_list[0].shape
    n = na * len(a_list)
    nb_all, k, nbw = b3.shape
    nblk = nb_all - blk0 if nblk is None else nblk
    assert n == nblk * nbw and na % nbw == 0
    tm = _pick(m, 2 * MM_TM if k <= MM_TM and n <= MM_TK else MM_TM, 16)
    tk = _pick(nbw, MM_TK, LANE)
    per = nbw // tk
    nk = n // tk
    tn = _pick(k, MM_TN if nk <= 2 else 2 * MM_TM, LANE)
    nka = na // tk
    na_ops = len(a_list)

    def body(*refs):
        a_refs, b_ref = refs[:na_ops], refs[na_ops]
        c_ref = refs[na_ops + 1] if add is not None else None
        o_ref = refs[na_ops + 1 + (add is not None) + (dep is not None)]
        acc = refs[-1]
        kk = pl.program_id(2)

        def part():
            av = a_refs[0][...] if not pair else jnp.where(kk < nka, a_refs[0][...], a_refs[1][...])
            return lax.dot_general(av.astype(BF), b_ref[...].astype(BF),
                                   (((1,), (1,)), ((), ())), preferred_element_type=F32)

        def finish(r):
            if add is not None:
                r = r + c_ref[...]
            o_ref[...] = r.astype(out_dtype)

        _accumulate(kk, nk, acc, part, finish)

    if pair:
        in_specs = [pl.BlockSpec((tm, tk), lambda i, j, kk: (i, jnp.minimum(kk, nka - 1))),
                    pl.BlockSpec((tm, tk), lambda i, j, kk: (i, jnp.maximum(kk - nka, 0)))]
    else:
        in_specs = [pl.BlockSpec((tm, tk), lambda i, j, kk: (i, kk))]
    in_specs.append(pl.BlockSpec((None, tn, tk), lambda i, j, kk: (blk0 + kk // per, j, kk % per)))
    args = a_list + [b3]
    if add is not None:
        in_specs.append(pl.BlockSpec((tm, tn), lambda i, j, kk: (i, j)))
        args.append(add)
    return _mm_call(body, name, (m // tm, k // tn, nk), in_specs, args,
                    pl.BlockSpec((tm, tn), lambda i, j, kk: (i, j)), jax.ShapeDtypeStruct((m, k), out_dtype),
                    (tm, tn), nk, dep)


def _rms_mm_nn(x, g, col_blk, b3, name):
    m = x.shape[0]
    _, k, n = b3.shape
    assert b3.shape[0] == 1
    tm = _pick(m, 2 * MM_TM, 16)
    tn = _pick(n, MM_TM, LANE)

    def body(x_ref, g_ref, b_ref, u_ref, o_ref):
        xv = x_ref[...]
        r = lax.rsqrt(jnp.mean(xv * xv, axis=-1, keepdims=True) + NORM_EPS)
        u = ((xv * r) * g_ref[...]).astype(BF)

        @pl.when(pl.program_id(1) == 0)
        def _():
            u_ref[...] = u

        o_ref[...] = jnp.dot(u, b_ref[0], preferred_element_type=F32)

    return pl.pallas_call(
        body, name=name, grid=(m // tm, n // tn),
        in_specs=[pl.BlockSpec((tm, k), lambda i, j: (i, col_blk)), pl.BlockSpec((1, k), lambda i, j: (0, 0)),
                  pl.BlockSpec((1, k, tn), lambda i, j: (0, 0, j))],
        out_specs=[pl.BlockSpec((tm, k), lambda i, j: (i, 0)), pl.BlockSpec((tm, tn), lambda i, j: (i, j))],
        out_shape=[jax.ShapeDtypeStruct((m, k), BF), jax.ShapeDtypeStruct((m, n), F32)],
        compiler_params=_cp("parallel", "arbitrary"),
    )(x, g, b3)


def _mm_nt_rms_bwd(a, b3, x, g, col_blk, name, dep=None):
    m, n = a.shape
    _, width, n2 = b3.shape
    assert n == n2 and b3.shape[0] == 1
    tm = _pick(m, MM_TM, 16)
    tk = _pick(n, MM_TK, LANE)
    nk = n // tk

    def body(*refs):
        a_ref, b_ref, x_ref, g_ref = refs[:4]
        dx_ref, dg_ref = refs[4 + (dep is not None):6 + (dep is not None)]
        acc = refs[-1]
        kk = pl.program_id(1)

        @pl.when((pl.program_id(0) == 0) & (kk == 0))
        def _():
            dg_ref[...] = jnp.zeros_like(dg_ref)

        def part():
            return lax.dot_general(a_ref[...], b_ref[0], (((1,), (1,)), ((), ())), preferred_element_type=F32)

        def finish(du):
            xv = x_ref[...]
            r = lax.rsqrt(jnp.mean(xv * xv, axis=-1, keepdims=True) + NORM_EPS)
            nv = xv * r
            dn = du * g_ref[...]
            dx_ref[...] = (r * (dn - nv * jnp.mean(dn * nv, axis=-1, keepdims=True))).astype(BF)
            dg_ref[...] += _rows8([jnp.sum(du * nv, axis=0, keepdims=True)], width)

        _accumulate(kk, nk, acc, part, finish)

    in_specs = [pl.BlockSpec((tm, tk), lambda i, kk: (i, kk)), pl.BlockSpec((1, width, tk), lambda i, kk: (0, 0, kk)),
                pl.BlockSpec((tm, width), lambda i, kk: (i, col_blk)), pl.BlockSpec((1, width), lambda i, kk: (0, 0))]
    args = [a, b3, x, g]
    if dep is not None:
        in_specs.append(ANY)
        args.append(dep)
    return pl.pallas_call(
        body, name=name, grid=(m // tm, nk), in_specs=in_specs,
        out_specs=[pl.BlockSpec((tm, width), lambda i, kk: (i, 0)), pl.BlockSpec((SUB, width), lambda i, kk: (0, 0))],
        out_shape=[jax.ShapeDtypeStruct((m, width), BF), jax.ShapeDtypeStruct((SUB, width), F32)],
        scratch_shapes=[pltpu.VMEM((tm, width), F32)],
        compiler_params=_cp("arbitrary", "arbitrary"),
    )(*args)


def _mm_tn(a, b, nblk, name, out_dtype=BF, dep=None, into=None, blk0=0):
    s, m = a.shape
    s2, n = b.shape
    assert s == s2 and n % nblk == 0 and (dep is None or into is None)
    nbw = n // nblk
    tm = _pick(m, MM_TN, LANE)
    tn = _pick(nbw, MM_TN, LANE)
    ts = _pick(s, MM_TS, LANE)
    per = nbw // tn
    ns = s // ts

    def body(*refs):
        a_ref, b_ref = refs[:2]
        o_ref = refs[2 + (dep is not None or into is not None)]
        acc = refs[-1]

        def part():
            return lax.dot_general(a_ref[...].astype(BF), b_ref[...].astype(BF),
                                   (((0,), (0,)), ((), ())), preferred_element_type=F32)

        def finish(r):
            o_ref[...] = r.astype(out_dtype)

        _accumulate(pl.program_id(2), ns, acc, part, finish)

    in_specs = [pl.BlockSpec((ts, tm), lambda i, j, ss: (ss, i)),
                pl.BlockSpec((ts, tn), lambda i, j, ss: (ss, j))]
    out_spec = pl.BlockSpec((None, tm, tn), lambda i, j, ss: (blk0 + j // per, i, j % per))
    if into is None:
        return _mm_call(body, name, (m // tm, n // tn, ns), in_specs, [a, b], out_spec,
                        jax.ShapeDtypeStruct((nblk, m, nbw), out_dtype), (tm, tn), ns, dep)
    assert into.shape[1:] == (m, nbw) and into.dtype == out_dtype
    return pl.pallas_call(
        body, name=name, grid=(m // tm, n // tn, ns), in_specs=in_specs + [ANY], out_specs=out_spec,
        out_shape=jax.ShapeDtypeStruct(into.shape, out_dtype), input_output_aliases={2: 0},
        scratch_shapes=[pltpu.VMEM((tm, tn), F32)] if ns > 1 else [],
        compiler_params=_cp("parallel", "parallel", "arbitrary"),
    )(a, b, into)


def _rows8(rows, width):
    idx = lax.broadcasted_iota(jnp.int32, (SUB, width), 0)
    out = jnp.zeros((SUB, width), F32)
    for r, v in enumerate(rows):
        out = jnp.where(idx == r, v, out)
    return out


def _rms_fwd(x, g, width, col_blk, name):
    s = x.shape[0]
    tr = _pick(s, ROW_TILE, 16)

    def body(x_ref, g_ref, u_ref):
        xv = x_ref[...]
        r = lax.rsqrt(jnp.mean(xv * xv, axis=-1, keepdims=True) + NORM_EPS)
        u_ref[...] = ((xv * r) * g_ref[...]).astype(BF)

    return pl.pallas_call(
        body, name=name, grid=(s // tr,),
        in_specs=[pl.BlockSpec((tr, width), lambda i: (i, col_blk)),
                  pl.BlockSpec((1, width), lambda i: (0, 0))],
        out_specs=pl.BlockSpec((tr, width), lambda i: (i, 0)),
        out_shape=jax.ShapeDtypeStruct((s, width), BF),
        compiler_params=_cp("parallel"),
    )(x, g)


def _rms_bwd(x, du, g, width, col_blk, name, extra=None, out_dtype=F32, also_bf16=False):
    s = x.shape[0]
    tr = _pick(s, ROW_TILE_BWD, 16)

    def body(*refs):
        x_ref, du_ref, g_ref = refs[:3]
        e_ref = refs[3] if extra is not None else None
        dx_ref = refs[3 + (extra is not None)]
        dxb_ref = refs[4 + (extra is not None)] if also_bf16 else None
        dg_ref = refs[-1]
        i = pl.program_id(0)
        xv = x_ref[...]
        duv = du_ref[...].astype(F32)
        r = lax.rsqrt(jnp.mean(xv * xv, axis=-1, keepdims=True) + NORM_EPS)
        nv = xv * r
        dn = duv * g_ref[...]
        dx = r * (dn - nv * jnp.mean(dn * nv, axis=-1, keepdims=True))
        if extra is not None:
            dx = dx + e_ref[...]
        dx_ref[...] = dx.astype(out_dtype)
        if also_bf16:
            dxb_ref[...] = dx.astype(BF)

        @pl.when(i == 0)
        def _():
            dg_ref[...] = jnp.zeros_like(dg_ref)

        dg_ref[...] += _rows8([jnp.sum(duv * nv, axis=0, keepdims=True)], width)

    in_specs = [pl.BlockSpec((tr, width), lambda i: (i, col_blk)),
                pl.BlockSpec((tr, width), lambda i: (i, 0)),
                pl.BlockSpec((1, width), lambda i: (0, 0))]
    args = [x, du, g]
    if extra is not None:
        in_specs.append(pl.BlockSpec((tr, width), lambda i: (i, 0)))
        args.append(extra)
    return pl.pallas_call(
        body, name=name, grid=(s // tr,),
        in_specs=in_specs,
        out_specs=[pl.BlockSpec((tr, width), lambda i: (i, 0))] * (1 + also_bf16)
        + [pl.BlockSpec((SUB, width), lambda i: (0, 0))],
        out_shape=[jax.ShapeDtypeStruct((s, width), out_dtype)] + [jax.ShapeDtypeStruct((s, width), BF)] * also_bf16
        + [jax.ShapeDtypeStruct((SUB, width), F32)],
        compiler_params=_cp("arbitrary"),
    )(*args)


def _down(cur, prev8, k):
    ext = jnp.concatenate([prev8, cur], axis=0)
    return pltpu.roll(ext, k, axis=0)[SUB:]


def _up(cur, next8, k):
    ext = jnp.concatenate([cur, next8], axis=0)
    return pltpu.roll(ext, ext.shape[0] - k, axis=0)[:cur.shape[0]]


def _lags(cur, prev8):
    return _down(cur, prev8, 1), _down(cur, prev8, 2)


def _conv3(w_ref, cur, prev8, lags=None):
    lag1, lag2 = _lags(cur, prev8) if lags is None else lags
    return w_ref[0:1, :] * lag2 + w_ref[1:2, :] * lag1 + w_ref[2:3, :] * cur


def _conv3_t(w_ref, cur, next8):
    return w_ref[2:3, :] * cur + w_ref[1:2, :] * _up(cur, next8, 1) + w_ref[0:1, :] * _up(cur, next8, 2)


def _spec_cur(tr, tc, c0):
    return pl.BlockSpec((tr, tc), lambda j, i: (i, c0 + j))


def _spec_prev(tr, tc, c0):
    return pl.BlockSpec((SUB, tc), lambda j, i: (jnp.maximum(i * (tr // SUB) - 1, 0), c0 + j))


def _spec_next(tr, tc, c0, s):
    return pl.BlockSpec((SUB, tc), lambda j, i: (jnp.minimum((i + 1) * (tr // SUB), s // SUB - 1), c0 + j))


def _spec_w(tc, c0):
    return pl.BlockSpec((SUB, tc), lambda j, i: (0, c0 + j))


def _pad8(w):
    return jnp.pad(w, ((0, SUB - w.shape[0]), (0, 0)))


def _conv_mix_fwd(z_a, cw8, conv):
    s = z_a.shape[0]
    tr = _pick(s, ROW_TILE, 16)
    tc = _pick(conv, COL_TILE, LANE)
    nc = conv // tc

    def body(zb_ref, zc_ref, zv_ref, zcp_ref, zvp_ref, w_ref, p_ref):
        i = pl.program_id(1)
        cv = zc_ref[...] * zv_ref[...]
        cvp = jnp.where(i > 0, zcp_ref[...] * zvp_ref[...], 0.0)
        p_ref[...] = (zb_ref[...] * _conv3(w_ref, cv, cvp)).astype(BF)

    return pl.pallas_call(
        body, name="conv_mix_fwd", grid=(nc, s // tr),
        in_specs=[_spec_cur(tr, tc, 0), _spec_cur(tr, tc, nc), _spec_cur(tr, tc, 2 * nc),
                  _spec_prev(tr, tc, nc), _spec_prev(tr, tc, 2 * nc), _spec_w(tc, 0)],
        out_specs=_spec_cur(tr, tc, 0),
        out_shape=jax.ShapeDtypeStruct((s, conv), BF),
        compiler_params=_cp("parallel", "parallel"),
    )(z_a, z_a, z_a, z_a, z_a, cw8)


def _conv_mix_bwd(z_a, d_p, cw8, conv):
    s = z_a.shape[0]
    tr = _pick(s, ROW_TILE_BWD, 16)
    tc = _pick(conv, COL_TILE, LANE)
    nc = conv // tc
    nr = s // tr

    def body(zb_ref, zbn_ref, zc_ref, zcp_ref, zv_ref, zvp_ref, dp_ref, dpn_ref, w_ref,
             dzb_ref, dzc_ref, dzv_ref, dw_ref):
        i = pl.program_id(1)
        zc = zc_ref[...]
        zv = zv_ref[...]
        cv = zc * zv
        cvp = jnp.where(i > 0, zcp_ref[...] * zvp_ref[...], 0.0)
        cv1, cv2 = _lags(cv, cvp)
        dpv = dp_ref[...]
        dzb_ref[...] = (dpv * _conv3(w_ref, cv, cvp, (cv1, cv2))).astype(BF)
        dcc = dpv * zb_ref[...]
        dccn = jnp.where(i < nr - 1, dpn_ref[...] * zbn_ref[...], 0.0)
        dcv = _conv3_t(w_ref, dcc, dccn)
        dzc_ref[...] = (dcv * zv).astype(BF)
        dzv_ref[...] = (dcv * zc).astype(BF)

        @pl.when(i == 0)
        def _():
            dw_ref[...] = jnp.zeros_like(dw_ref)

        dw_ref[...] += _rows8([jnp.sum(dcc * cv2, axis=0, keepdims=True),
                               jnp.sum(dcc * cv1, axis=0, keepdims=True),
                               jnp.sum(dcc * cv, axis=0, keepdims=True)], tc)

    out = jax.ShapeDtypeStruct((s, conv), BF)
    return pl.pallas_call(
        body, name="conv_mix_bwd", grid=(nc, nr),
        in_specs=[_spec_cur(tr, tc, 0), _spec_next(tr, tc, 0, s),
                  _spec_cur(tr, tc, nc), _spec_prev(tr, tc, nc),
                  _spec_cur(tr, tc, 2 * nc), _spec_prev(tr, tc, 2 * nc),
                  _spec_cur(tr, tc, 0), _spec_next(tr, tc, 0, s), _spec_w(tc, 0)],
        out_specs=[_spec_cur(tr, tc, 0), _spec_cur(tr, tc, 0), _spec_cur(tr, tc, 0), _spec_w(tc, 0)],
        out_shape=[out, out, out, jax.ShapeDtypeStruct((SUB, conv), F32)],
        compiler_params=_cp("parallel", "arbitrary"),
    )(z_a, z_a, z_a, z_a, z_a, z_a, d_p, d_p, cw8)


def _silu_parts(ag):
    sg = jax.nn.sigmoid(ag)
    return ag * sg, sg


def _ffn_up_act(u2, w_up, cw8, cb, dff):
    s, d = u2.shape
    nb, _, nbw = w_up.shape
    half = nb // 2
    assert half * nbw == dff
    tm = _pick(s, ROW_TILE, 16)

    def body(u_ref, wg_ref, wu_ref, cg_ref, cu_ref, bg_ref, bu_ref, ag_ref, au_ref, f_ref, hist_g, hist_u):
        i = pl.program_id(1)

        @pl.when(i == 0)
        def _():
            hist_g[...] = jnp.zeros_like(hist_g)
            hist_u[...] = jnp.zeros_like(hist_u)

        u = u_ref[...]
        xg = jnp.dot(u, wg_ref[...], preferred_element_type=F32)
        xu = jnp.dot(u, wu_ref[...], preferred_element_type=F32)
        ag_ref[...] = xg
        au_ref[...] = xu
        ag = _conv3(cg_ref, xg, hist_g[...]) + bg_ref[...]
        au = _conv3(cu_ref, xu, hist_u[...]) + bu_ref[...]
        f_ref[...] = (_silu_parts(ag)[0] * au).astype(BF)
        hist_g[...] = xg[tm - SUB:]
        hist_u[...] = xu[tm - SUB:]

    once = pl.Buffered(1)
    tile = pl.BlockSpec((tm, nbw), lambda j, i: (i, j))
    return pl.pallas_call(
        body, name="mm_ffn_up_act", grid=(half, s // tm),
        in_specs=[pl.BlockSpec((tm, d), lambda j, i: (i, 0)),
                  pl.BlockSpec((None, d, nbw), lambda j, i: (j, 0, 0), pipeline_mode=once),
                  pl.BlockSpec((None, d, nbw), lambda j, i: (half + j, 0, 0), pipeline_mode=once),
                  pl.BlockSpec((SUB, nbw), lambda j, i: (0, j)), pl.BlockSpec((SUB, nbw), lambda j, i: (0, half + j)),
                  pl.BlockSpec((1, nbw), lambda j, i: (0, j)), pl.BlockSpec((1, nbw), lambda j, i: (0, half + j))],
        out_specs=[tile, tile, tile],
        out_shape=[jax.ShapeDtypeStruct((s, dff), F32), jax.ShapeDtypeStruct((s, dff), F32),
                   jax.ShapeDtypeStruct((s, dff), BF)],
        scratch_shapes=[pltpu.VMEM((SUB, nbw), F32), pltpu.VMEM((SUB, nbw), F32)],
        compiler_params=_cp("arbitrary", "arbitrary"),
    )(u2, w_up, w_up, cw8, cw8, cb, cb)


def _ffn_act_bwd(a_g, a_u, d_f, cw8, cb, dff):
    s = a_g.shape[0]
    tr = _pick(s, ROW_TILE_BWD // 2, 16)
    tc = _pick(dff, FFN_COL_TILE, LANE)
    nc = dff // tc
    nr = s // tr

    def body(xg_ref, xgp_ref, xgn_ref, xu_ref, xup_ref, xun_ref, df_ref, dfn_ref,
             wg_ref, wu_ref, bg_ref, bu_ref, dxg_ref, dxu_ref, dwg_ref, dwu_ref):
        i = pl.program_id(1)
        xg = xg_ref[...]
        xu = xu_ref[...]
        xgp = jnp.where(i > 0, xgp_ref[...], 0.0)
        xup = jnp.where(i > 0, xup_ref[...], 0.0)

        def d_act(xg_t, xgp_t, xu_t, xup_t, df_t, lags_g=None, lags_u=None):
            ag = _conv3(wg_ref, xg_t, xgp_t, lags_g) + bg_ref[...]
            au = _conv3(wu_ref, xu_t, xup_t, lags_u) + bu_ref[...]
            sil, sg = _silu_parts(ag)
            return df_t * au * (sg * (1.0 + ag * (1.0 - sg))), df_t * sil

        lags_g = _lags(xg, xgp)
        lags_u = _lags(xu, xup)
        dag, dau = d_act(xg, xgp, xu, xup, df_ref[...], lags_g, lags_u)
        dfn = jnp.where(i < nr - 1, dfn_ref[...], 0.0)
        dagn, daun = d_act(xgn_ref[...], xg[tr - SUB:], xun_ref[...], xu[tr - SUB:], dfn)
        dxg_ref[...] = _conv3_t(wg_ref, dag, dagn).astype(BF)
        dxu_ref[...] = _conv3_t(wu_ref, dau, daun).astype(BF)

        @pl.when(i == 0)
        def _():
            dwg_ref[...] = jnp.zeros_like(dwg_ref)
            dwu_ref[...] = jnp.zeros_like(dwu_ref)

        def wgrad(da, x, lags):
            return _rows8([jnp.sum(da * lags[1], axis=0, keepdims=True),
                           jnp.sum(da * lags[0], axis=0, keepdims=True),
                           jnp.sum(da * x, axis=0, keepdims=True),
                           jnp.sum(da, axis=0, keepdims=True)], tc)

        dwg_ref[...] += wgrad(dag, xg, lags_g)
        dwu_ref[...] += wgrad(dau, xu, lags_u)

    half = jax.ShapeDtypeStruct((s, dff), BF)
    wsh = jax.ShapeDtypeStruct((SUB, dff), F32)
    return pl.pallas_call(
        body, name="ffn_act_bwd", grid=(nc, nr),
        in_specs=[_spec_cur(tr, tc, 0), _spec_prev(tr, tc, 0), _spec_next(tr, tc, 0, s),
                  _spec_cur(tr, tc, 0), _spec_prev(tr, tc, 0), _spec_next(tr, tc, 0, s),
                  _spec_cur(tr, tc, 0), _spec_next(tr, tc, 0, s),
                  _spec_w(tc, 0), _spec_w(tc, nc),
                  pl.BlockSpec((1, tc), lambda j, i: (0, j)), pl.BlockSpec((1, tc), lambda j, i: (0, nc + j))],
        out_specs=[_spec_cur(tr, tc, 0), _spec_cur(tr, tc, 0), _spec_w(tc, 0), _spec_w(tc, 0)],
        out_shape=[half, half, wsh, wsh],
        compiler_params=_cp("parallel", "arbitrary"),
    )(a_g, a_g, a_g, a_u, a_u, a_u, d_f, d_f, cw8, cw8, cb, cb)


def _residual_norm(mix, w_oo, x, g):
    m, k = mix.shape
    d = w_oo.shape[2]
    tm = _pick(m, ROW_TILE, 16)

    def body(a_ref, b_ref, x_ref, g_ref, h_ref, u_ref):
        h = jnp.dot(a_ref[...], b_ref[0], preferred_element_type=F32) + x_ref[...]
        h_ref[...] = h
        r = lax.rsqrt(jnp.mean(h * h, axis=-1, keepdims=True) + NORM_EPS)
        u_ref[...] = ((h * r) * g_ref[...]).astype(BF)

    row = pl.BlockSpec((tm, d), lambda i: (i, 0))
    return pl.pallas_call(
        body, name="mm_h1_norm", grid=(m // tm,),
        in_specs=[pl.BlockSpec((tm, k), lambda i: (i, 0)),
                  pl.BlockSpec((1, k, d), lambda i: (0, 0, 0), pipeline_mode=pl.Buffered(1)),
                  row, pl.BlockSpec((1, d), lambda i: (0, 0))],
        out_specs=[row, row],
        out_shape=[jax.ShapeDtypeStruct((m, d), F32), jax.ShapeDtypeStruct((m, d), BF)],
        compiler_params=_cp("parallel"),
    )(mix, w_oo, x, g)


def _mla_out_gate(o, w_mo, z_g, b_gate, yc):
    m, k = o.shape
    d = w_mo.shape[2]
    tm = _pick(m, MM_TM, 16)
    tn = _pick(d, MM_TM, LANE)
    nc = d // tn

    def body(a_ref, b_ref, za_ref, zb_ref, ba_ref, bb_ref, yc_ref, ym_ref, mix_ref):
        ym = jnp.dot(a_ref[...], b_ref[0], preferred_element_type=F32)
        ga = jax.nn.sigmoid(za_ref[...] + ba_ref[...])
        gb = jax.nn.sigmoid(zb_ref[...] + bb_ref[...])
        ym_ref[...] = ym.astype(BF)
        mix_ref[...] = (ga * yc_ref[...] + gb * ym).astype(BF)

    tile = pl.BlockSpec((tm, tn), lambda i, j: (i, j))
    out = jax.ShapeDtypeStruct((m, d), BF)
    return pl.pallas_call(
        body, name="mm_y_mla_gate", grid=(m // tm, nc),
        in_specs=[pl.BlockSpec((tm, k), lambda i, j: (i, 0)), pl.BlockSpec((1, k, tn), lambda i, j: (0, 0, j)),
                  tile, pl.BlockSpec((tm, tn), lambda i, j: (i, nc + j)),
                  pl.BlockSpec((1, tn), lambda i, j: (0, j)), pl.BlockSpec((1, tn), lambda i, j: (0, nc + j)), tile],
        out_specs=[tile, tile], out_shape=[out, out],
        compiler_params=_cp("parallel", "parallel"),
    )(o, w_mo, z_g, z_g, b_gate, b_gate, yc)


def _d_mix_gate(d_h1, w_oo, z_g, b_gate, yc, ym):
    m, n = d_h1.shape
    d = w_oo.shape[1]
    tm = _pick(m, ROW_TILE, 16)
    tn = _pick(d, MM_TM, LANE)
    nc = d // tn

    def body(a_ref, b_ref, za_ref, zb_ref, ba_ref, bb_ref, yc_ref, ym_ref,
             dza_ref, dzb_ref, dyc_ref, dym_ref, dba_ref, dbb_ref):
        i = pl.program_id(1)
        dm = lax.dot_general(a_ref[...], b_ref[0], (((1,), (1,)), ((), ())), preferred_element_type=F32)
        ga = jax.nn.sigmoid(za_ref[...] + ba_ref[...])
        gb = jax.nn.sigmoid(zb_ref[...] + bb_ref[...])
        dza = dm * yc_ref[...] * (ga * (1.0 - ga))
        dzb = dm * ym_ref[...] * (gb * (1.0 - gb))
        dza_ref[...] = dza.astype(BF)
        dzb_ref[...] = dzb.astype(BF)
        dyc_ref[...] = (dm * ga).astype(BF)
        dym_ref[...] = (dm * gb).astype(BF)

        @pl.when(i == 0)
        def _():
            dba_ref[...] = jnp.zeros_like(dba_ref)
            dbb_ref[...] = jnp.zeros_like(dbb_ref)

        dba_ref[...] += _rows8([jnp.sum(dza, axis=0, keepdims=True)], tn)
        dbb_ref[...] += _rows8([jnp.sum(dzb, axis=0, keepdims=True)], tn)

    tile = pl.BlockSpec((tm, tn), lambda j, i: (i, j))
    act = jax.ShapeDtypeStruct((m, d), BF)
    bsh = jax.ShapeDtypeStruct((SUB, d), F32)
    return pl.pallas_call(
        body, name="mm_d_mix_gate", grid=(nc, m // tm),
        in_specs=[pl.BlockSpec((tm, n), lambda j, i: (i, 0)), pl.BlockSpec((1, tn, n), lambda j, i: (0, j, 0)),
                  tile, pl.BlockSpec((tm, tn), lambda j, i: (i, nc + j)),
                  pl.BlockSpec((1, tn), lambda j, i: (0, j)), pl.BlockSpec((1, tn), lambda j, i: (0, nc + j)),
                  tile, tile],
        out_specs=[tile] * 4 + [pl.BlockSpec((SUB, tn), lambda j, i: (0, j))] * 2,
        out_shape=[act, act, act, act, bsh, bsh],
        compiler_params=_cp("parallel", "arbitrary"),
    )(d_h1, w_oo, z_g, z_g, b_gate, b_gate, yc, ym)


def _lay(v):
    z = jnp.zeros(v.shape[:-1] + (HALF,), v.dtype)
    return jnp.concatenate([v[..., :HALF], z, v[..., HALF:], z], axis=-1)


def _unlay(v):
    return jnp.concatenate([v[..., :HALF], v[..., 2 * HALF:3 * HALF]], axis=-1)


def _lay_rows(v):
    z = jnp.zeros((HALF,) + v.shape[1:], v.dtype)
    return jnp.concatenate([v[:HALF], z, v[HALF:], z], axis=0)


def _rope_tables(positions):
    s = positions.shape[0]
    tr = _pick(s, ROW_TILE, 8)
    inv_freq = ROPE_THETA ** (-jnp.arange(0, ROPE, 2, dtype=F32) / ROPE)
    consts = jnp.stack([_lay(jnp.concatenate([inv_freq, inv_freq])),
                        _lay(jnp.ones((ROPE,), F32)),
                        _lay(jnp.concatenate([-jnp.ones((HALF,), F32), jnp.ones((HALF,), F32)]))])
    consts = _pad8(consts)

    def body(p_ref, c_ref, cos_ref, sin_ref):
        ang = p_ref[...].astype(F32) * c_ref[0:1, :]
        cos_ref[...] = jnp.cos(ang) * c_ref[1:2, :]
        sin_ref[...] = jnp.sin(ang) * c_ref[2:3, :]

    tab = jax.ShapeDtypeStruct((s, LANE), F32)
    return pl.pallas_call(
        body, name="rope_tables", grid=(s // tr,),
        in_specs=[pl.BlockSpec((tr, 1), lambda i: (i, 0)), pl.BlockSpec((SUB, LANE), lambda i: (0, 0))],
        out_specs=[pl.BlockSpec((tr, LANE), lambda i: (i, 0))] * 2,
        out_shape=[tab, tab],
        compiler_params=_cp("parallel"),
    )(positions, consts)


def _lane_sum(p):
    return jnp.sum(p, axis=-1, keepdims=True)


def _rope(t, cos, sin):
    return t * cos + pltpu.roll(t, 2 * HALF, axis=1) * sin


def _rope_t(d, cos, sin):
    return d * cos + pltpu.roll(d * sin, 2 * HALF, axis=1)


def _head_fwd(q_raw, kv_raw, z_a, kr_blk, cos, sin, gains, heads):
    s = q_raw.shape[0]
    tr = _pick(s, HEAD_ROW_TILE, 16)
    hw = heads * LANE

    def body(q_ref, kv_ref, kr_ref, cos_ref, sin_ref, g_ref, qo_ref, ko_ref, vo_ref):
        cosv = cos_ref[...]
        sinv = sin_ref[...]
        krv = kr_ref[...]
        kr_sq = krv * krv
        for h in range(heads):
            lo = h * LANE
            qn = q_ref[:, lo:lo + LANE]
            qr = q_ref[:, hw + lo:hw + lo + LANE]
            r = lax.rsqrt(_lane_sum(qn * qn + qr * qr) / HEAD_QK + NORM_EPS)
            qo_ref[:, 2 * lo:2 * lo + LANE] = (((qn * r) * g_ref[0:1, :]) * (QK_SCALE * LOG2_E)).astype(BF)
            qo_ref[:, 2 * lo + LANE:2 * lo + 2 * LANE] = (
                _rope((qr * r) * g_ref[1:2, :], cosv, sinv) * (QK_SCALE * LOG2_E)).astype(BF)
            kn = kv_ref[:, 2 * lo:2 * lo + LANE]
            r = lax.rsqrt(_lane_sum(kn * kn + kr_sq) / HEAD_QK + NORM_EPS)
            ko_ref[:, 2 * lo:2 * lo + LANE] = ((kn * r) * g_ref[2:3, :]).astype(BF)
            ko_ref[:, 2 * lo + LANE:2 * lo + 2 * LANE] = _rope((krv * r) * g_ref[3:4, :], cosv, sinv).astype(BF)
            vo_ref[:, lo:lo + LANE] = kv_ref[:, 2 * lo + LANE:2 * lo + 2 * LANE].astype(BF)

    row = lambda w: pl.BlockSpec((tr, w), lambda i: (i, 0))
    return pl.pallas_call(
        body, name="head_fwd", grid=(s // tr,),
        in_specs=[row(2 * hw), row(2 * hw), pl.BlockSpec((tr, LANE), lambda i: (i, kr_blk)),
                  row(LANE), row(LANE), pl.BlockSpec((SUB, LANE), lambda i: (0, 0))],
        out_specs=[row(2 * hw), row(2 * hw), row(hw)],
        out_shape=[jax.ShapeDtypeStruct((s, 2 * hw), BF), jax.ShapeDtypeStruct((s, 2 * hw), BF),
                   jax.ShapeDtypeStruct((s, hw), BF)],
        compiler_params=_cp("parallel"),
    )(q_raw, kv_raw, z_a, cos, sin, gains)


def _head_bwd(q_raw, kv_raw, z_a, kr_blk, cos, sin, gains, dq_att, dk_att, dv, heads):
    s = q_raw.shape[0]
    tr = _pick(s, HEAD_ROW_TILE_BWD, 16)
    hw = heads * LANE

    def body(q_ref, kv_ref, kr_ref, cos_ref, sin_ref, g_ref, dq_ref, dk_ref, dv_ref,
             dqr_ref, dkv_ref, dkr_ref, dg_ref):
        i = pl.program_id(0)
        cosv = cos_ref[...]
        sinv = sin_ref[...]
        krv = kr_ref[...]
        kr_sq = krv * krv
        dkr = jnp.zeros((tr, LANE), F32)
        dgs = [jnp.zeros((1, LANE), F32) for _ in range(4)]

        def norm_bwd(xn, xr, sq, dn_out, dr_out, gn, gr):
            r = lax.rsqrt(_lane_sum(sq) / HEAD_QK + NORM_EPS)
            nn = xn * r
            nr = xr * r
            dt = _rope_t(dr_out, cosv, sinv)
            dnn = dn_out * gn
            dnr = dt * gr
            mean = _lane_sum(dnn * nn + dnr * nr) / HEAD_QK
            return (r * (dnn - nn * mean), r * (dnr - nr * mean),
                    jnp.sum(dn_out * nn, axis=0, keepdims=True), jnp.sum(dt * nr, axis=0, keepdims=True))

        for h in range(heads):
            lo = h * LANE
            qn = q_ref[:, lo:lo + LANE]
            qr = q_ref[:, hw + lo:hw + lo + LANE]
            dxn, dxr, g0, g1 = norm_bwd(qn, qr, qn * qn + qr * qr, dq_ref[:, 2 * lo:2 * lo + LANE] * QK_SCALE,
                                        dq_ref[:, 2 * lo + LANE:2 * lo + 2 * LANE] * QK_SCALE,
                                        g_ref[0:1, :], g_ref[1:2, :])
            dqr_ref[:, lo:lo + LANE] = dxn.astype(BF)
            dqr_ref[:, hw + lo:hw + lo + LANE] = dxr.astype(BF)
            kn = kv_ref[:, 2 * lo:2 * lo + LANE]
            dxn, dxr, g2, g3 = norm_bwd(kn, krv, kn * kn + kr_sq, dk_ref[:, 2 * lo:2 * lo + LANE],
                                        dk_ref[:, 2 * lo + LANE:2 * lo + 2 * LANE], g_ref[2:3, :], g_ref[3:4, :])
            dkv_ref[:, 2 * lo:2 * lo + LANE] = dxn.astype(BF)
            dkv_ref[:, 2 * lo + LANE:2 * lo + 2 * LANE] = dv_ref[:, lo:lo + LANE].astype(BF)
            dkr = dkr + dxr
            dgs = [a + b for a, b in zip(dgs, (g0, g1, g2, g3))]
        dkr_ref[...] = dkr

        @pl.when(i == 0)
        def _():
            dg_ref[...] = jnp.zeros_like(dg_ref)

        dg_ref[...] += _rows8(dgs, LANE)

    row = lambda w: pl.BlockSpec((tr, w), lambda i: (i, 0))
    return pl.pallas_call(
        body, name="head_bwd", grid=(s // tr,),
        in_specs=[row(2 * hw), row(2 * hw), pl.BlockSpec((tr, LANE), lambda i: (i, kr_blk)),
                  row(LANE), row(LANE), pl.BlockSpec((SUB, LANE), lambda i: (0, 0)),
                  row(2 * hw), row(2 * hw), row(hw)],
        out_specs=[row(2 * hw), row(2 * hw), row(LANE), pl.BlockSpec((SUB, LANE), lambda i: (0, 0))],
        out_shape=[jax.ShapeDtypeStruct((s, 2 * hw), BF), jax.ShapeDtypeStruct((s, 2 * hw), BF),
                   jax.ShapeDtypeStruct((s, LANE), F32), jax.ShapeDtypeStruct((SUB, LANE), F32)],
        compiler_params=_cp("arbitrary"),
    )(q_raw, kv_raw, z_a, cos, sin, gains, dq_att, dk_att, dv)


def _causal_mask(nrows, ncols, row0):
    rows = lax.broadcasted_iota(jnp.int32, (nrows, ncols), 0) + row0
    cols = lax.broadcasted_iota(jnp.int32, (nrows, ncols), 1)
    return cols <= rows


def _causal_steps(nt, q_major):
    pairs = ([(i, j) for i in range(nt) for j in range(i + 1)] if q_major
             else [(i, j) for j in range(nt) for i in range(j, nt)])
    return (jnp.array([p[0] for p in pairs], jnp.int32), jnp.array([p[1] for p in pairs], jnp.int32))


def _attn_fwd(q_att, k_att, v, heads):
    s = q_att.shape[0]
    t = _pick(s, ATTN_TILE_FWD, LANE)
    nt = s // t
    th = t // 2
    qi, kj = _causal_steps(nt, True)

    def body(qi_ref, kj_ref, q_ref, k_ref, v_ref, o_ref, ob_ref, lse_ref, m_s, l_s, acc_s):
        st = pl.program_id(1)
        i = qi_ref[st]
        j = kj_ref[st]

        @pl.when(j == 0)
        def _():
            m_s[...] = jnp.full_like(m_s, NEG_INF)
            l_s[...] = jnp.zeros_like(l_s)
            acc_s[...] = jnp.zeros_like(acc_s)

        def update(rows, ncol, masked):
            sc = lax.dot_general(q_ref[rows, :], k_ref[0:ncol, :], (((1,), (1,)), ((), ())),
                                 preferred_element_type=F32)
            if masked:
                sc = jnp.where(_causal_mask(rows.stop - rows.start, ncol, rows.start), sc, NEG_INF)
            m_prev = m_s[rows, :]
            m_new = jnp.maximum(m_prev, jnp.max(sc, axis=-1, keepdims=True))
            alpha = jnp.exp2(m_prev - m_new)
            p = jnp.exp2(sc - jnp.tile(m_new, (1, ncol // LANE)))
            l_s[rows, :] = alpha * l_s[rows, :] + jnp.sum(p, axis=-1, keepdims=True)
            acc_s[rows, :] = alpha * acc_s[rows, :] + jnp.dot(p.astype(BF), v_ref[0:ncol, :],
                                                              preferred_element_type=F32)
            m_s[rows, :] = m_new

        @pl.when(j < i)
        def _():
            update(slice(0, t), t, False)

        @pl.when(j == i)
        def _():
            update(slice(0, th), th, True)
            update(slice(th, t), t, True)
            o = acc_s[...] / l_s[...]
            o_ref[...] = o
            ob_ref[...] = o.astype(BF)
            lse_ref[...] = (m_s[...] + jnp.log2(l_s[...]))[:, 0:1]

    q_idx = lambda h, st, qi_r, kj_r: (qi_r[st], h)
    kv_idx = lambda h, st, qi_r, kj_r: (kj_r[st], h)
    return pl.pallas_call(
        body, name="attn_fwd",
        grid_spec=pltpu.PrefetchScalarGridSpec(
            num_scalar_prefetch=2, grid=(heads, qi.shape[0]),
            in_specs=[pl.BlockSpec((t, 2 * LANE), q_idx), pl.BlockSpec((t, 2 * LANE), kv_idx),
                      pl.BlockSpec((t, LANE), kv_idx)],
            out_specs=[pl.BlockSpec((t, LANE), q_idx), pl.BlockSpec((t, LANE), q_idx),
                       pl.BlockSpec((None, t, 1), lambda h, st, qi_r, kj_r: (h, qi_r[st], 0))],
            scratch_shapes=[pltpu.VMEM((t, LANE), F32), pltpu.VMEM((t, LANE), F32), pltpu.VMEM((t, LANE), F32)]),
        out_shape=[jax.ShapeDtypeStruct((s, heads * LANE), F32), jax.ShapeDtypeStruct((s, heads * LANE), BF),
                   jax.ShapeDtypeStruct((heads, s, 1), F32)],
        compiler_params=_cp("parallel", "arbitrary"),
    )(qi, kj, q_att, k_att, v)


def _attn_bwd(q_att, k_att, v, o, lse, d_o, heads, dep=None):
    s = q_att.shape[0]
    t = _pick(s, ATTN_TILE, LANE)
    nt = s // t
    th = t // 2
    qi, kj = _causal_steps(nt, False)

    def body(qi_ref, kj_ref, q_ref, k_ref, v_ref, do_ref, o_ref, lse_ref, *rest):
        dq_ref, dk_ref, dv_ref, dk_s, dv_s = rest[-5:]
        st = pl.program_id(1)
        i = qi_ref[st]
        j = kj_ref[st]

        @pl.when(st == 0)
        def _():
            dq_ref[...] = jnp.zeros_like(dq_ref)

        @pl.when(i == j)
        def _():
            dk_s[...] = jnp.zeros_like(dk_s)
            dv_s[...] = jnp.zeros_like(dv_s)

        def update(rows, ncol, masked):
            nrow = rows.stop - rows.start
            q = q_ref[rows, :]
            k = k_ref[0:ncol, :]
            do = do_ref[rows, :]
            sc = lax.dot_general(q, k, (((1,), (1,)), ((), ())), preferred_element_type=F32)
            if masked:
                sc = jnp.where(_causal_mask(nrow, ncol, rows.start), sc, NEG_INF)
            p = jnp.exp2(sc - lse_ref[rows, :])
            dp = lax.dot_general(do, v_ref[0:ncol, :], (((1,), (1,)), ((), ())), preferred_element_type=F32)
            delta = jnp.sum(do.astype(F32) * o_ref[rows, :], axis=-1, keepdims=True)
            ds = (p * (dp - delta)).astype(BF)
            dv_s[0:ncol, :] += lax.dot_general(p.astype(BF), do, (((0,), (0,)), ((), ())),
                                               preferred_element_type=F32)
            dk_s[0:ncol, :] += lax.dot_general(ds, q, (((0,), (0,)), ((), ())), preferred_element_type=F32)
            out_rows = pl.ds(pl.multiple_of(i * t + rows.start, nrow), nrow)
            dq_ref[out_rows, :] += jnp.dot(ds, k, preferred_element_type=F32)

        @pl.when(i > j)
        def _():
            update(slice(0, t), t, False)

        @pl.when(i == j)
        def _():
            update(slice(0, th), th, True)
            update(slice(th, t), t, True)

        @pl.when(i == nt - 1)
        def _():
            dk_ref[...] = (dk_s[...] * (1.0 / LOG2_E)).astype(BF)
            dv_ref[...] = dv_s[...].astype(BF)

    q_idx = lambda h, st, qi_r, kj_r: (qi_r[st], h)
    kv_idx = lambda h, st, qi_r, kj_r: (kj_r[st], h)
    in_specs = [pl.BlockSpec((t, 2 * LANE), q_idx), pl.BlockSpec((t, 2 * LANE), kv_idx),
                pl.BlockSpec((t, LANE), kv_idx), pl.BlockSpec((t, LANE), q_idx), pl.BlockSpec((t, LANE), q_idx),
                pl.BlockSpec((None, t, 1), lambda h, st, qi_r, kj_r: (h, qi_r[st], 0))]
    args = [q_att, k_att, v, d_o, o, lse]
    if dep is not None:
        in_specs.append(ANY)
        args.append(dep)
    return pl.pallas_call(
        body, name="attn_bwd",
        grid_spec=pltpu.PrefetchScalarGridSpec(
            num_scalar_prefetch=2, grid=(heads, qi.shape[0]),
            in_specs=in_specs,
            out_specs=[pl.BlockSpec((s, 2 * LANE), lambda h, st, qi_r, kj_r: (0, h)),
                       pl.BlockSpec((t, 2 * LANE), kv_idx), pl.BlockSpec((t, LANE), kv_idx)],
            scratch_shapes=[pltpu.VMEM((t, 2 * LANE), F32), pltpu.VMEM((t, LANE), F32)]),
        out_shape=[jax.ShapeDtypeStruct((s, heads * 2 * LANE), F32),
                   jax.ShapeDtypeStruct((s, heads * 2 * LANE), BF),
                   jax.ShapeDtypeStruct((s, heads * LANE), BF)],
        compiler_params=_cp("parallel", "arbitrary"),
    )(qi, kj, *args)


def _sum_parts(parts, name):
    n, r, c = parts.shape
    tr = _pick(r, 512, 8)

    def body(p_ref, o_ref):
        g = p_ref[0].astype(F32)
        for k in range(1, n):
            g = g + p_ref[k].astype(F32)
        o_ref[...] = g

    return pl.pallas_call(
        body, name=name, grid=(r // tr,),
        in_specs=[pl.BlockSpec((n, tr, c), lambda i: (0, i, 0))],
        out_specs=pl.BlockSpec((tr, c), lambda i: (i, 0)),
        out_shape=jax.ShapeDtypeStruct((r, c), F32),
        compiler_params=_cp("parallel"),
    )(parts)


def _adamw(parts, w, m, v, name, by_cols=False):
    n, rp, c = parts.shape
    r = w.shape[0]
    assert by_cols or rp == r
    tr, tc = (r, _pick(c, 256, LANE)) if by_cols else (_pick(r, 256, 16 if r % 16 == 0 else 8), c)

    def body(p_ref, w_ref, m_ref, v_ref, g_ref, d_ref, mo_ref, vo_ref):
        g = p_ref[0].astype(F32)
        for k in range(1, n):
            g = g + p_ref[k].astype(F32)
        g = g[:r] if by_cols else g
        m_new = ADAM_B1 * m_ref[...] + (1.0 - ADAM_B1) * g
        v_new = ADAM_B2 * v_ref[...] + (1.0 - ADAM_B2) * jnp.square(g)
        m_hat = m_new / (1.0 - ADAM_B1 ** ADAM_STEP)
        v_hat = v_new / (1.0 - ADAM_B2 ** ADAM_STEP)
        g_ref[...] = g
        d_ref[...] = -ADAM_LR * (m_hat / (jnp.sqrt(v_hat) + ADAM_EPS) + ADAM_WD * w_ref[...])
        mo_ref[...] = m_new
        vo_ref[...] = v_new

    idx = (lambda i: (0, i)) if by_cols else (lambda i: (i, 0))
    spec = pl.BlockSpec((tr, tc), idx)
    sh = jax.ShapeDtypeStruct((r, c), F32)
    return pl.pallas_call(
        body, name=name, grid=(c // tc if by_cols else r // tr,),
        in_specs=[pl.BlockSpec((n, rp if by_cols else tr, tc), lambda i: (0,) + idx(i)), spec, spec, spec],
        out_specs=[spec] * 4, out_shape=[sh] * 4,
        compiler_params=_cp("parallel"),
    )(parts, w, m, v)


def _place():
    x, y, c = lax.axis_index("x"), lax.axis_index("y"), lax.axis_index("c")
    chips = [(1 - x, y), (x, 1 - y), (1 - x, 1 - y)]
    return x, y, c, chips


def _all_gather(shards, name, dep=None):
    n = len(shards)
    deps = [] if dep is None else list(dep)

    def body(*refs):
        ins, outs = refs[:n], refs[n + len(deps):2 * n + len(deps)]
        send_sems, recv_sems, local_sems = refs[2 * n + len(deps):]
        x, y, c, chips = _place()
        me, sibling = (x, y, c), (x, y, 1 - c)

        def slot(w, p):
            return outs[w].at[4 * p[0] + 2 * p[1] + p[2]]

        def copy(w, k, block, to, src=None):
            return pltpu.make_async_remote_copy(
                src_ref=slot(w, block) if src is None else src, dst_ref=slot(w, block),
                send_sem=send_sems.at[w, k], recv_sem=recv_sems.at[w, k], device_id=to, device_id_type=MESH)

        first = []
        for w in range(n):
            first += [copy(w, 1 + j, me, (*chip, c), src=ins[w]) for j, chip in enumerate(chips)]
            first.append(copy(w, 0, me, sibling, src=ins[w]))
        for cp in first:
            cp.start()
        mine = [pltpu.make_async_copy(ins[w], slot(w, me), local_sems.at[w]) for w in range(n)]
        for cp in mine:
            cp.start()
        passed = []
        for w in range(n):
            for j, chip in enumerate(chips):
                copy(w, 1 + j, (*chip, c), me).wait_recv()
                cp = copy(w, 4 + j, (*chip, c), sibling)
                cp.start()
                passed.append(cp)
        for w in range(n):
            copy(w, 0, sibling, me).wait_recv()
            for j, chip in enumerate(chips):
                copy(w, 4 + j, (*chip, 1 - c), me).wait_recv()
        for cp in first + passed:
            cp.wait_send()
        for cp in mine:
            cp.wait()

    return pl.pallas_call(
        body, name=name,
        in_specs=[ANY] * (n + len(deps)), out_specs=[ANY] * n,
        out_shape=[jax.ShapeDtypeStruct((N_DEV,) + a.shape, a.dtype) for a in shards],
        scratch_shapes=[pltpu.SemaphoreType.DMA((n, 7)), pltpu.SemaphoreType.DMA((n, 7)),
                        pltpu.SemaphoreType.DMA((n,))],
    )(*shards, *deps)


HBM = pl.BlockSpec(memory_space=pltpu.HBM)
SEM = pl.BlockSpec(memory_space=pltpu.SEMAPHORE)
EFFECT = pltpu.SideEffectType.DATAFLOW_SIDE_EFFECTING
PEERS = [(dx, dy, dc) for dx in (1, 0) for dy in (1, 0) for dc in (0, 1) if (dx, dy, dc) != (0, 0, 0)]


def _peer(x, y, c, flip):
    dx, dy, dc = flip
    return (1 - x if dx else x, 1 - y if dy else y, 1 - c if dc else c)


def _exchange_copies(srcs, lands, send, recv, loc, gather):
    x, y, c, _ = _place()
    me = 4 * x + 2 * y + c
    remote, local = [], []
    for w in range(len(srcs)):
        for k, flip in enumerate(PEERS):
            px, py, pc = _peer(x, y, c, flip)
            src = srcs[w] if gather else srcs[w].at[4 * px + 2 * py + pc]
            remote.append(pltpu.make_async_remote_copy(
                src_ref=src, dst_ref=lands[w].at[me], send_sem=send[w].at[k], recv_sem=recv[w].at[k],
                device_id=(px, py, pc), device_id_type=MESH))
        local.append(pltpu.make_async_copy(srcs[w] if gather else srcs[w].at[me], lands[w].at[me], loc[w]))
    return remote, local


class _Exchange:
    def __init__(self, srcs, lands, send, recv, loc, token, gather):
        self.srcs, self.lands, self.send, self.recv, self.loc = srcs, lands, send, recv, loc
        self.token, self.gather = token, gather


def _exchange_start(srcs, gather, name, dep=None):
    n = len(srcs)
    deps = [] if dep is None else [dep]
    land_shapes = [((N_DEV,) + a.shape) if gather else a.shape for a in srcs]
    lands = [pltpu.with_memory_space_constraint(lax.empty(sh, a.dtype), pltpu.HBM) for sh, a in zip(land_shapes, srcs)]
    srcs = [pltpu.with_memory_space_constraint(a, pltpu.HBM) for a in srcs]

    def body(*refs):
        src_refs, land_refs = refs[:n], refs[n:2 * n]
        outs = refs[2 * n + len(deps):]
        send, recv, loc = outs[:n], outs[n:2 * n], outs[2 * n:3 * n]
        token = outs[-1]
        remote, local = _exchange_copies(src_refs, land_refs, send, recv, loc, gather)
        for cp in remote + local:
            cp.start()
        token[...] = jnp.zeros_like(token)

    out_shape = ([pltpu.SemaphoreType.DMA((len(PEERS),))] * (2 * n) + [pltpu.SemaphoreType.DMA(())] * n
                 + [pltpu.HBM(a.shape, a.dtype) for a in srcs] + [pltpu.HBM(a.shape, a.dtype) for a in lands]
                 + [jax.ShapeDtypeStruct((SUB, LANE), F32)])
    res = pl.pallas_call(
        body, name=name, out_shape=out_shape,
        in_specs=[HBM] * (2 * n) + [ANY] * len(deps),
        out_specs=[SEM] * (3 * n) + [HBM] * (2 * n) + [pl.BlockSpec(memory_space=pltpu.VMEM)],
        input_output_aliases={i: 3 * n + i for i in range(2 * n)},
        compiler_params=pltpu.CompilerParams(has_side_effects=EFFECT),
    )(*srcs, *lands, *deps)
    return _Exchange(res[3 * n:4 * n], res[4 * n:5 * n], res[:n], res[n:2 * n], res[2 * n:3 * n], res[-1], gather)


def _exchange_wait(ex, idxs, after, name):
    n = len(idxs)
    srcs = [ex.srcs[i] for i in idxs]
    lands = [ex.lands[i] for i in idxs]
    sems = [ex.send[i] for i in idxs] + [ex.recv[i] for i in idxs] + [ex.loc[i] for i in idxs]
    gather = ex.gather

    def body(*refs):
        src_refs, land_refs = refs[:n], refs[n:2 * n]
        send, recv, loc = refs[2 * n:3 * n], refs[3 * n:4 * n], refs[4 * n:5 * n]
        remote, local = _exchange_copies(src_refs, land_refs, send, recv, loc, gather)
        for cp in remote:
            cp.wait_send()
            cp.wait_recv()
        for cp in local:
            cp.wait()

    res = pl.pallas_call(
        body, name=name,
        out_shape=[pltpu.HBM(a.shape, a.dtype) for a in srcs] + [pltpu.HBM(a.shape, a.dtype) for a in lands],
        in_specs=[HBM] * (2 * n) + [SEM] * (3 * n) + [ANY],
        out_specs=[HBM] * (2 * n),
        input_output_aliases={i: i for i in range(2 * n)},
        compiler_params=pltpu.CompilerParams(has_side_effects=EFFECT),
    )(*srcs, *lands, *sems, after)
    return res[n:]


def _gather2_copies(srcs, lands, send, recv_ici, recv_sib, loc):
    x, y, c, chips = _place()
    me = 4 * x + 2 * y + c
    remote, local = [], []
    for w in range(len(srcs)):
        remote.append(pltpu.make_async_remote_copy(
            src_ref=srcs[w], dst_ref=lands[w].at[me], send_sem=send[w].at[0], recv_sem=recv_sib[w],
            device_id=(x, y, 1 - c), device_id_type=MESH))
        for j, chip in enumerate(chips):
            remote.append(pltpu.make_async_remote_copy(
                src_ref=srcs[w], dst_ref=lands[w].at[me], send_sem=send[w].at[1 + j], recv_sem=recv_ici[w].at[j],
                device_id=(*chip, c), device_id_type=MESH))
        local.append(pltpu.make_async_copy(srcs[w], lands[w].at[me], loc[w]))
    return remote, local


def _gather2_forwards(lands, fsend, frecv, arrived=None):
    x, y, c, chips = _place()
    cps = []
    for w in range(len(lands)):
        for j, chip in enumerate(chips):
            slot = lands[w].at[4 * chip[0] + 2 * chip[1] + c]
            cp = pltpu.make_async_remote_copy(
                src_ref=slot, dst_ref=slot, send_sem=fsend[w].at[j], recv_sem=frecv[w].at[j],
                device_id=(x, y, 1 - c), device_id_type=MESH)
            if arrived is not None:
                pltpu.make_async_remote_copy(
                    src_ref=slot, dst_ref=slot, send_sem=fsend[w].at[j], recv_sem=arrived[w].at[j],
                    device_id=(x, y, 1 - c), device_id_type=MESH).wait_recv()
            cps.append(cp)
    return cps


def _gather2(shards, between, name):
    n = len(shards)
    srcs = [pltpu.with_memory_space_constraint(a, pltpu.HBM) for a in shards]
    lands = [pltpu.with_memory_space_constraint(lax.empty((N_DEV,) + a.shape, a.dtype), pltpu.HBM) for a in shards]
    hbm_like = lambda arrs: [pltpu.HBM(a.shape, a.dtype) for a in arrs]
    tok = jax.ShapeDtypeStruct((SUB, LANE), F32)
    vmem = pl.BlockSpec(memory_space=pltpu.VMEM)
    side = pltpu.CompilerParams(has_side_effects=EFFECT)

    def start(*refs):
        src_refs, land_refs = refs[:n], refs[n:2 * n]
        outs = refs[2 * n:]
        send, recv_ici, recv_sib, loc = outs[:n], outs[n:2 * n], outs[2 * n:3 * n], outs[3 * n:4 * n]
        remote, local = _gather2_copies(src_refs, land_refs, send, recv_ici, recv_sib, loc)
        for cp in remote + local:
            cp.start()
        outs[-1][...] = jnp.zeros((SUB, LANE), F32)

    res = pl.pallas_call(
        start, name=name + "_start",
        out_shape=([pltpu.SemaphoreType.DMA((4,))] * n + [pltpu.SemaphoreType.DMA((3,))] * n
                   + [pltpu.SemaphoreType.DMA(())] * (2 * n) + hbm_like(srcs) + hbm_like(lands) + [tok]),
        in_specs=[HBM] * (2 * n), out_specs=[SEM] * (4 * n) + [HBM] * (2 * n) + [vmem],
        input_output_aliases={i: 4 * n + i for i in range(2 * n)}, compiler_params=side,
    )(*srcs, *lands)
    send, recv_ici, recv_sib, loc = res[:n], res[n:2 * n], res[2 * n:3 * n], res[3 * n:4 * n]
    srcs, lands, token = res[4 * n:5 * n], res[5 * n:6 * n], res[-1]

    done = between(token)
    after = jax.tree_util.tree_leaves(done)

    def forward(*refs):
        land_refs, arrived = refs[:n], refs[n:2 * n]
        outs = refs[2 * n + len(after):]
        fsend, frecv = outs[:n], outs[n:2 * n]
        for cp in _gather2_forwards(land_refs, fsend, frecv, arrived):
            cp.start()
        outs[-1][...] = jnp.zeros((SUB, LANE), F32)

    res = pl.pallas_call(
        forward, name=name + "_forward",
        out_shape=[pltpu.SemaphoreType.DMA((3,))] * (2 * n) + hbm_like(lands) + [tok],
        in_specs=[HBM] * n + [SEM] * n + [ANY] * len(after), out_specs=[SEM] * (2 * n) + [HBM] * n + [vmem],
        input_output_aliases={i: 2 * n + i for i in range(n)}, compiler_params=side,
    )(*lands, *recv_ici, *after)
    fsend, frecv, lands, token = res[:n], res[n:2 * n], res[2 * n:3 * n], res[-1]

    def wait(*refs):
        src_refs, land_refs = refs[:n], refs[n:2 * n]
        sems = refs[2 * n:7 * n]
        send, recv_sib, loc, fsend, frecv = (sems[k * n:(k + 1) * n] for k in range(5))
        remote, local = _gather2_copies(src_refs, land_refs, send, send, recv_sib, loc)
        for w in range(n):
            for cp in remote[4 * w:4 * w + 4]:
                cp.wait_send()
            remote[4 * w].wait_recv()
        for cp in local:
            cp.wait()
        for cp in _gather2_forwards(land_refs, fsend, frecv):
            cp.wait_send()
            cp.wait_recv()

    res = pl.pallas_call(
        wait, name=name + "_wait", out_shape=hbm_like(srcs) + hbm_like(lands),
        in_specs=[HBM] * (2 * n) + [SEM] * (5 * n) + [ANY], out_specs=[HBM] * (2 * n),
        input_output_aliases={i: i for i in range(2 * n)}, compiler_params=side,
    )(*srcs, *lands, *send, *recv_sib, *loc, *fsend, *frecv, token)
    return res[n:], done


def _after(token, a):
    return a + token[0:1, 0:1].astype(a.dtype)


def _unblock(w3):
    nb, k, nbw = w3.shape
    return w3.transpose(1, 0, 2).reshape(k, nb * nbw)


def _block(w, nb):
    k, n = w.shape
    return w.reshape(k, nb, n // nb).transpose(1, 0, 2)


def kernel(x, positions, ln1_g, w_in, b_gate, conv_w, w_conv_out, q_a_g, w_q_b, kv_a_g, w_kv_b, q_norm_g, k_norm_g, w_mla_out, w_o, ln2_g, w_ffn_up, ffn_conv_w, ffn_conv_b, w_ffn_down, loss_target, m_ln1_g, m_w_in, m_b_gate, m_conv_w, m_w_conv_out, m_q_a_g, m_w_q_b, m_kv_a_g, m_w_kv_b, m_q_norm_g, m_k_norm_g, m_w_mla_out, m_w_o, m_ln2_g, m_w_ffn_up, m_ffn_conv_w, m_ffn_conv_b, m_w_ffn_down, v_ln1_g, v_w_in, v_b_gate, v_conv_w, v_w_conv_out, v_q_a_g, v_w_q_b, v_kv_a_g, v_w_kv_b, v_q_norm_g, v_k_norm_g, v_w_mla_out, v_w_o, v_ln2_g, v_w_ffn_up, v_ffn_conv_w, v_ffn_conv_b, v_w_ffn_down):
    s, d = x.shape[1], x.shape[2]
    conv = conv_w.shape[2] * N_DEV
    ql, kvl = q_a_g.shape[1], kv_a_g.shape[1]
    heads = w_q_b.shape[2] * N_DEV // HEAD_QK
    dff = w_ffn_down.shape[1] * N_DEV
    hw = heads * LANE
    conv3 = 3 * conv
    kr_off = conv3 + ql
    kv_off = -(-(kr_off + LANE) // kvl) * kvl
    wa = kv_off + kvl
    assert conv3 % ql == 0 and kr_off % LANE == 0
    xs = x[0]
    tgt = loss_target[0]
    pos = positions.reshape(s, 1)

    nin = w_in.shape[2]
    big = dict(w_in=w_in[0].T, w_conv_out=w_conv_out[0], w_q_b=w_q_b[0], w_kv_b=w_kv_b[0],
               w_mla_out=w_mla_out[0], w_o=w_o[0], w_ffn_up=w_ffn_up[0], w_ffn_down=w_ffn_down[0])
    names = list(big)
    rest = names[1:]
    early = {}

    def while_w_in_travels(token):
        early["ag"] = _exchange_start([big[k].astype(BF) for k in rest], True, "gather_rest_start", dep=token)
        cos_sin = _rope_tables(pos)
        return cos_sin, _rms_fwd(xs, _after(early["ag"].token, ln1_g), d, 0, "rms1_fwd")

    first, ((cos, sin), u1) = _gather2([big["w_in"].astype(BF), _pad8(conv_w[0]), _pad8(ffn_conv_w[0])],
                                       while_w_in_travels, "gather_w_in")
    ag = early["ag"]
    cw8 = _unblock(first[1])
    fcw8 = _unblock(first[2])

    def landed(keys, after, name):
        return _exchange_wait(ag, [rest.index(k) for k in keys], after, name)

    w_in_t = first[0].reshape(N_DEV * nin, d)
    g_off = kr_off + kvl + ROPE
    w_a_t = jnp.concatenate([w_in_t[:kr_off], _lay_rows(w_in_t[kr_off + kvl:g_off]),
                             jnp.zeros((kv_off - kr_off - LANE, d), BF), w_in_t[kr_off:kr_off + kvl]], axis=0)[None]
    w_g_t = w_in_t[g_off:][None]
    gains = _pad8(jnp.concatenate([q_norm_g[:, :NOPE], _lay(q_norm_g[:, NOPE:]),
                                   k_norm_g[:, :NOPE], _lay(k_norm_g[:, NOPE:])], axis=0))
    kr_blk = kr_off // LANE

    z_a = _mm_nt(u1, w_a_t, "mm_z_a")
    z_g = _mm_nt(u1, w_g_t, "mm_z_g", out_dtype=BF)
    p = _conv_mix_fwd(z_a, cw8, conv)
    w_co, w_qb, w_kv = landed(["w_conv_out", "w_q_b", "w_kv_b"], p, "gather_wait_mixers")
    w_co = _unblock(w_co)[None]
    w_kv = _unblock(w_kv)[None]
    wq_full = _unblock(w_qb).reshape(ql, heads, HEAD_QK)
    w_q = jnp.concatenate([wq_full[:, :, :NOPE].reshape(ql, hw), _lay(wq_full[:, :, NOPE:]).reshape(ql, hw)],
                          axis=1)[None]
    yc = _mm_nn(p, w_co, "mm_y_conv", out_dtype=BF)
    qn, q_raw = _rms_mm_nn(z_a, q_a_g, conv3 // ql, w_q, "mm_q")
    kvn, kv_raw = _rms_mm_nn(z_a, kv_a_g, kv_off // kvl, w_kv, "mm_kv")
    q_att, k_att, v_bf = _head_fwd(q_raw, kv_raw, z_a, kr_blk, cos, sin, gains, heads)
    o, o_bf, lse = _attn_fwd(q_att, k_att, v_bf, heads)
    w_mo, w_oo = landed(["w_mla_out", "w_o"], lse, "gather_wait_outs")
    w_mo = w_mo.reshape(1, hw, d)
    w_oo = w_oo.reshape(1, d, d)
    ym, mix = _mla_out_gate(o_bf, w_mo, z_g, b_gate, yc)
    h1, u2 = _residual_norm(mix, w_oo, xs, ln2_g)
    w_up, = landed(["w_ffn_up"], u2, "gather_wait_ffn_up")
    a_g, a_u, f = _ffn_up_act(u2, w_up, fcw8, ffn_conv_b, dff)
    w_dn, = landed(["w_ffn_down"], f, "gather_wait_ffn_down")
    w_dn = w_dn.reshape(1, dff, d)
    dy, dy_bf, loss_part = _mm_nn_loss(f, w_dn, h1, tgt, "mm_ffn_down_loss")

    g_dn = _mm_tn(f, dy_bf, 1, "mm_g_ffn_down").reshape(N_DEV, dff // N_DEV, d)
    rs_dn = _exchange_start([g_dn], False, "reduce_ffn_down_start")
    d_f = _mm_nt(dy_bf, w_dn, "mm_d_f", dep=rs_dn.token)
    d_xg, d_xu, dfw_g, dfw_u = _ffn_act_bwd(a_g, a_u, d_f, fcw8, ffn_conv_b, dff)
    half = N_DEV // 2
    g_up = _mm_tn(u2, d_xg, half, "mm_g_ffn_up_gate", into=lax.empty((N_DEV, d, 2 * dff // N_DEV), BF))
    g_up = _mm_tn(u2, d_xu, half, "mm_g_ffn_up_up", into=g_up, blk0=half)
    rs_up = _exchange_start([g_up], False, "reduce_ffn_up_start")
    d_u2 = _mm_nt([d_xg, d_xu], w_up, "mm_d_u2", out_dtype=BF, dep=rs_up.token)
    d_h1, d_h1_bf, dg_ln2 = _rms_bwd(h1, d_u2, ln2_g, d, 0, "rms2_bwd", extra=dy, also_bf16=True)
    g_oo = _mm_tn(mix, d_h1_bf, 1, "mm_g_w_o").reshape(N_DEV, d // N_DEV, d)
    d_zga, d_zgb, d_yc, d_ym, dba, dbb = _d_mix_gate(d_h1_bf, w_oo, z_g, b_gate, yc, ym)
    g_co = _block(_mm_tn(p, d_yc, 1, "mm_g_conv_out")[0], N_DEV)
    g_mo = _mm_tn(o_bf, d_ym, 1, "mm_g_mla_out").reshape(N_DEV, hw // N_DEV, d)
    rs_mix = _exchange_start([g_oo, g_co, g_mo], False, "reduce_mixers_start")
    d_p = _mm_nt(d_yc, w_co, "mm_d_p", dep=rs_mix.token)
    d_o = _mm_nt(d_ym, w_mo, "mm_d_o", out_dtype=BF)
    d_zb, d_zc, d_zv, dcw = _conv_mix_bwd(z_a, d_p, cw8, conv)
    dq_att, dk_att, dv = _attn_bwd(q_att, k_att, v_bf, o, lse, d_o, heads, dep=rs_mix.token)
    d_q_raw, d_kv_raw, d_kr, dgains = _head_bwd(q_raw, kv_raw, z_a, kr_blk, cos, sin, gains, dq_att, dk_att, dv, heads)
    g_q2 = _mm_tn(qn, d_q_raw, 1, "mm_g_q")[0]
    g_qb = _block(jnp.concatenate([g_q2[:, :hw].reshape(ql, heads, NOPE),
                                   _unlay(g_q2[:, hw:].reshape(ql, heads, LANE))], axis=2).reshape(ql, heads * HEAD_QK), N_DEV)
    g_kv = _block(_mm_tn(kvn, d_kv_raw, 1, "mm_g_kv")[0], N_DEV)
    rs_qkv = _exchange_start([g_qb, g_kv], False, "reduce_qkv_start")
    d_ql, dg_qa = _mm_nt_rms_bwd(d_q_raw, w_q, z_a, q_a_g, conv3 // ql, "mm_d_q_lat", dep=rs_qkv.token)
    d_kvl, dg_kva = _mm_nt_rms_bwd(d_kv_raw, w_kv, z_a, kv_a_g, kv_off // kvl, "mm_d_kv_lat")
    d_z_a = jnp.concatenate([d_zb, d_zc, d_zv, d_ql, d_kr.astype(BF), jnp.zeros((s, kv_off - kr_off - LANE), BF),
                             d_kvl], axis=1)
    g_a = _mm_tn(d_z_a, u1, 1, "mm_g_w_a")[0]
    g_ga = _mm_tn(d_zga, u1, 1, "mm_g_w_ga")[0]
    g_gb = _mm_tn(d_zgb, u1, 1, "mm_g_w_gb")[0]
    g_in = jnp.concatenate([g_a[:kr_off], g_a[kv_off:kv_off + kvl], g_a[kr_off:kr_off + HALF],
                            g_a[kr_off + 2 * HALF:kr_off + 3 * HALF], g_ga, g_gb], axis=0).reshape(N_DEV, nin, d)
    rs_in = _exchange_start([g_in], False, "reduce_w_in_start")
    d_u1 = _mm_nn(d_z_a, w_a_t, "mm_d_u1_a", dep=rs_in.token)
    d_u1 = _mm_nn([d_zga, d_zgb], w_g_t, "mm_d_u1_g", add=d_u1)
    grad_x, dg_ln1 = _rms_bwd(xs, d_u1, ln1_g, d, 0, "rms1_bwd", extra=d_h1)

    summed = {}
    summed["w_ffn_down"], = _exchange_wait(rs_dn, [0], grad_x, "reduce_ffn_down_wait")
    summed["w_ffn_up"], = _exchange_wait(rs_up, [0], grad_x, "reduce_ffn_up_wait")
    summed["w_o"], summed["w_conv_out"], summed["w_mla_out"] = _exchange_wait(rs_mix, [0, 1, 2], grad_x, "reduce_mixers_wait")
    summed["w_q_b"], summed["w_kv_b"] = _exchange_wait(rs_qkv, [0, 1], grad_x, "reduce_qkv_wait")
    loc = locals()
    out = {}
    for k in rest:
        out[k] = _adamw(summed[k], big[k], loc["m_" + k][0], loc["v_" + k][0], "adamw_" + k)

    small = dict(ln1_g=dg_ln1[0:1], b_gate=jnp.concatenate([dba[0:1], dbb[0:1]], axis=1), q_a_g=dg_qa[0:1],
                 kv_a_g=dg_kva[0:1],
                 q_norm_g=jnp.concatenate([dgains[0:1], _unlay(dgains[1:2])], axis=1),
                 k_norm_g=jnp.concatenate([dgains[2:3], _unlay(dgains[3:4])], axis=1),
                 ln2_g=dg_ln2[0:1], ffn_conv_b=jnp.concatenate([dfw_g[3:4], dfw_u[3:4]], axis=1))
    small_names = list(small)
    extra = [dcw[0:3].reshape(1, -1), jnp.concatenate([dfw_g[0:3], dfw_u[0:3]], axis=1).reshape(1, -1),
             loss_part[0:1, 0:1]]
    flat = jnp.concatenate([small[k] for k in small_names] + extra, axis=1)
    n_flat = flat.shape[1]
    rows = -(-n_flat // (SUB * LANE)) * SUB
    flat = jnp.pad(flat, ((0, 0), (0, rows * LANE - n_flat))).reshape(rows, LANE)
    total = _sum_parts(_all_gather([flat], "gather_small", dep=[out[k][0] for k in rest])[0], "sum_small").reshape(1, rows * LANE)
    off = 0
    small_g = {}
    for k in small_names:
        small_g[k] = total[:, off:off + small[k].shape[1]]
        off += small[k].shape[1]
    me = 4 * lax.axis_index("x") + 2 * lax.axis_index("y") + lax.axis_index("c")
    cwn, fcwn = conv // N_DEV, 2 * dff // N_DEV
    g_cw = lax.dynamic_slice_in_dim(total[:, off:off + 3 * conv].reshape(3, conv), me * cwn, cwn, axis=1)
    off += 3 * conv
    g_fcw = lax.dynamic_slice_in_dim(total[:, off:off + 6 * dff].reshape(3, 2 * dff), me * fcwn, fcwn, axis=1)
    off += 6 * dff
    loss = total[0, off]

    summed["w_in"], = _exchange_wait(rs_in, [0], total, "reduce_w_in_wait")
    out["w_in"] = [r.T for r in _adamw(summed["w_in"], big["w_in"], m_w_in[0].T, v_w_in[0].T, "adamw_w_in",
                                       by_cols=True)]
    small_w = dict(ln1_g=ln1_g, b_gate=b_gate, q_a_g=q_a_g, kv_a_g=kv_a_g, q_norm_g=q_norm_g, k_norm_g=k_norm_g,
                   ln2_g=ln2_g, ffn_conv_b=ffn_conv_b, conv_w=conv_w[0].reshape(1, -1),
                   ffn_conv_w=ffn_conv_w[0].reshape(1, -1))
    small_g["conv_w"] = g_cw.reshape(1, -1)
    small_g["ffn_conv_w"] = g_fcw.reshape(1, -1)
    packed_names = list(small_w)

    def pack(get):
        vflat = jnp.concatenate([get(k).reshape(1, -1) for k in packed_names], axis=1)
        nr = -(-vflat.shape[1] // (SUB * LANE)) * SUB
        return jnp.pad(vflat, ((0, 0), (0, nr * LANE - vflat.shape[1])), constant_values=1.0).reshape(nr, LANE)

    res = _adamw(pack(lambda k: small_g[k])[None], pack(lambda k: small_w[k]), pack(lambda k: loc["m_" + k]),
                 pack(lambda k: loc["v_" + k]), "adamw_small")
    res = [r.reshape(1, -1) for r in res]
    off = 0
    for k in packed_names:
        shape = loc[k].shape
        size = small_w[k].shape[1]
        out[k] = [r[:, off:off + size].reshape(shape) for r in res]
        off += size
    for k in names:
        out[k] = [r[None] for r in out[k]]

    order = ["ln1_g", "w_in", "b_gate", "conv_w", "w_conv_out", "q_a_g", "w_q_b", "kv_a_g", "w_kv_b", "q_norm_g",
             "k_norm_g", "w_mla_out", "w_o", "ln2_g", "w_ffn_up", "ffn_conv_w", "ffn_conv_b", "w_ffn_down"]
    return (loss, grad_x[None], *[out[k][0] for k in order], *[out[k][1] for k in order],
            *[out[k][2] for k in order], *[out[k][3] for k in order])
```

```python
import jax
import jax.numpy as jnp
from jax import lax
from jax.experimental import pallas as pl
from jax.experimental.pallas import tpu as pltpu

BF = jnp.bfloat16
F32 = jnp.float32
MESH = pl.DeviceIdType.MESH
N_DEV = 8

NOPE = 128
ROPE = 64
HALF = ROPE // 2
HEAD_QK = NOPE + ROPE
LANE = 128
SUB = 8
QK_SCALE = HEAD_QK ** -0.5
LOG2_E = 1.4426950408889634
NORM_EPS = 1e-6
NEG_INF = -1e30
ROPE_THETA = 10000.0
ADAM_LR = 0.001
ADAM_B1 = 0.9
ADAM_B2 = 0.999
ADAM_EPS = 1e-08
ADAM_WD = 0.01
ADAM_STEP = 10

VMEM_LIMIT = 52 * 1024 * 1024
MM_TM, MM_TN, MM_TK, MM_TS = 1024, 1536, 2048, 2048
ROW_TILE, ROW_TILE_BWD = 512, 256
HEAD_ROW_TILE, HEAD_ROW_TILE_BWD = 256, 256
COL_TILE = 512
FFN_COL_TILE = 1408
ATTN_TILE = 1024
ATTN_TILE_FWD = 1024
ANY = pl.BlockSpec(memory_space=pl.ANY)


def _pick(n, target, mult):
    t = (min(n, target) // mult) * mult
    while t > 0:
        if n % t == 0:
            return t
        t -= mult
    raise ValueError(f"no tile for {n} (target {target}, multiple {mult})")


def _cp(*sem):
    return pltpu.CompilerParams(dimension_semantics=sem, vmem_limit_bytes=VMEM_LIMIT)


def _accumulate(kk, nk, acc, part, finish):
    if nk == 1:
        finish(part())
        return

    @pl.when(kk == 0)
    def _():
        acc[...] = part()

    @pl.when((kk > 0) & (kk < nk - 1))
    def _():
        acc[...] += part()

    @pl.when(kk == nk - 1)
    def _():
        finish(acc[...] + part())


def _mm_call(body, name, grid, in_specs, args, out_spec, out_shape, acc_shape, nk, dep):
    if dep is not None:
        in_specs = in_specs + [ANY]
        args = args + [dep]
    return pl.pallas_call(
        body, name=name, grid=grid, in_specs=in_specs, out_specs=out_spec, out_shape=out_shape,
        scratch_shapes=[pltpu.VMEM(acc_shape, F32)] if nk > 1 else [],
        compiler_params=_cp("parallel", "parallel", "arbitrary"),
    )(*args)


def _mm_nn_loss(a, b3, add, target, name):
    m, k = a.shape
    _, k2, n = b3.shape
    assert k == k2 and b3.shape[0] == 1
    tm = _pick(m, MM_TM, 16)
    tn = _pick(n, MM_TN, LANE)
    tk = _pick(k, MM_TK, LANE)
    nk = k // tk

    def body(a_ref, b_ref, c_ref, t_ref, dy_ref, dyb_ref, l_ref, acc):
        kk = pl.program_id(2)

        @pl.when((pl.program_id(0) == 0) & (pl.program_id(1) == 0) & (kk == 0))
        def _():
            l_ref[...] = jnp.zeros_like(l_ref)

        def part():
            return jnp.dot(a_ref[...].astype(BF), b_ref[0].astype(BF), preferred_element_type=F32)

        def finish(r):
            e = r + c_ref[...] - t_ref[...]
            dy_ref[...] = e / n
            dyb_ref[...] = (e / n).astype(BF)
            l_ref[...] += 0.5 * jnp.sum(jnp.sum(e * e, axis=-1, keepdims=True), axis=0, keepdims=True) / n

        _accumulate(kk, nk, acc, part, finish)

    tile = pl.BlockSpec((tm, tn), lambda i, j, kk: (i, j))
    return pl.pallas_call(
        body, name=name, grid=(m // tm, n // tn, nk),
        in_specs=[pl.BlockSpec((tm, tk), lambda i, j, kk: (i, kk)),
                  pl.BlockSpec((1, tk, tn), lambda i, j, kk: (0, kk, j)), tile, tile],
        out_specs=[tile, tile, pl.BlockSpec((SUB, LANE), lambda i, j, kk: (0, 0))],
        out_shape=[jax.ShapeDtypeStruct((m, n), F32), jax.ShapeDtypeStruct((m, n), BF),
                   jax.ShapeDtypeStruct((SUB, LANE), F32)],
        scratch_shapes=[pltpu.VMEM((tm, tn), F32)],
        compiler_params=_cp("arbitrary", "arbitrary", "arbitrary"),
    )(a, b3, add, target)


def _mm_nn(a, b3, name, add=None, out_dtype=F32, blk0=0, nblk=None, dep=None):
    pair = isinstance(a, (list, tuple))
    a_list = list(a) if pair else [a]
    m, ka = a_list[0].shape
    k = ka * len(a_list)
    nb_all, k2, nbw = b3.shape
    assert k == k2
    nblk = nb_all - blk0 if nblk is None else nblk
    n = nblk * nbw
    tm = _pick(m, MM_TM if k > MM_TM else 2 * MM_TM, 16)
    tn = _pick(nbw, MM_TN, LANE)
    tk = _pick(ka, MM_TK, LANE)
    per = nbw // tn
    nk = k // tk
    nka = ka // tk
    na_ops = len(a_list)

    def body(*refs):
        a_refs, b_ref = refs[:na_ops], refs[na_ops]
        c_ref = refs[na_ops + 1] if add is not None else None
        o_ref = refs[na_ops + 1 + (add is not None) + (dep is not None)]
        acc = refs[-1]
        kk = pl.program_id(2)

        def part():
            av = a_refs[0][...] if not pair else jnp.where(kk < nka, a_refs[0][...], a_refs[1][...])
            return jnp.dot(av.astype(BF), b_ref[...].astype(BF), preferred_element_type=F32)

        def finish(r):
            if add is not None:
                r = r + c_ref[...]
            o_ref[...] = r.astype(out_dtype)

        _accumulate(kk, nk, acc, part, finish)

    if pair:
        in_specs = [pl.BlockSpec((tm, tk), lambda i, j, kk: (i, jnp.minimum(kk, nka - 1))),
                    pl.BlockSpec((tm, tk), lambda i, j, kk: (i, jnp.maximum(kk - nka, 0)))]
    else:
        in_specs = [pl.BlockSpec((tm, tk), lambda i, j, kk: (i, kk))]
    in_specs.append(pl.BlockSpec((None, tk, tn), lambda i, j, kk: (blk0 + j // per, kk, j % per)))
    args = a_list + [b3]
    if add is not None:
        in_specs.append(pl.BlockSpec((tm, tn), lambda i, j, kk: (i, j)))
        args.append(add)
    return _mm_call(body, name, (m // tm, n // tn, nk), in_specs, args,
                    pl.BlockSpec((tm, tn), lambda i, j, kk: (i, j)), jax.ShapeDtypeStruct((m, n), out_dtype),
                    (tm, tn), nk, dep)


def _mm_nt(a, b3, name, add=None, out_dtype=F32, blk0=0, nblk=None, dep=None):
    pair = isinstance(a, (list, tuple))
    a_list = list(a) if pair else [a]
    m, na = a_list[0].shape
    n = na * len(a_list)
    nb_all, k, nbw = b3.shape
    nblk = nb_all - blk0 if nblk is None else nblk
    assert n == nblk * nbw and na % nbw == 0
    tm = _pick(m, 2 * MM_TM if k <= MM_TM and n <= MM_TK else MM_TM, 16)
    tk = _pick(nbw, MM_TK, LANE)
    per = nbw // tk
    nk = n // tk
    tn = _pick(k, MM_TN if nk <= 2 else 2 * MM_TM, LANE)
    nka = na // tk
    na_ops = len(a_list)

    def body(*refs):
        a_refs, b_ref = refs[:na_ops], refs[na_ops]
        c_ref = refs[na_ops + 1] if add is not None else None
        o_ref = refs[na_ops + 1 + (add is not None) + (dep is not None)]
        acc = refs[-1]
        kk = pl.program_id(2)

        def part():
            av = a_refs[0][...] if not pair else jnp.where(kk < nka, a_refs[0][...], a_refs[1][...])
            return lax.dot_general(av.astype(BF), b_ref[...].astype(BF),
                                   (((1,), (1,)), ((), ())), preferred_element_type=F32)

        def finish(r):
            if add is not None:
                r = r + c_ref[...]
            o_ref[...] = r.astype(out_dtype)

        _accumulate(kk, nk, acc, part, finish)

    if pair:
        in_specs = [pl.BlockSpec((tm, tk), lambda i, j, kk: (i, jnp.minimum(kk, nka - 1))),
                    pl.BlockSpec((tm, tk), lambda i, j, kk: (i, jnp.maximum(kk - nka, 0)))]
    else:
        in_specs = [pl.BlockSpec((tm, tk), lambda i, j, kk: (i, kk))]
    in_specs.append(pl.BlockSpec((None, tn, tk), lambda i, j, kk: (blk0 + kk // per, j, kk % per)))
    args = a_list + [b3]
    if add is not None:
        in_specs.append(pl.BlockSpec((tm, tn), lambda i, j, kk: (i, j)))
        args.append(add)
    return _mm_call(body, name, (m // tm, k // tn, nk), in_specs, args,
                    pl.BlockSpec((tm, tn), lambda i, j, kk: (i, j)), jax.ShapeDtypeStruct((m, k), out_dtype),
                    (tm, tn), nk, dep)


def _rms_mm_nn(x, g, col_blk, b3, name):
    m = x.shape[0]
    _, k, n = b3.shape
    assert b3.shape[0] == 1
    tm = _pick(m, 2 * MM_TM, 16)
    tn = _pick(n, MM_TM, LANE)

    def body(x_ref, g_ref, b_ref, u_ref, o_ref):
        xv = x_ref[...]
        r = lax.rsqrt(jnp.mean(xv * xv, axis=-1, keepdims=True) + NORM_EPS)
        u = ((xv * r) * g_ref[...]).astype(BF)

        @pl.when(pl.program_id(1) == 0)
        def _():
            u_ref[...] = u

        o_ref[...] = jnp.dot(u, b_ref[0], preferred_element_type=F32).astype(BF)

    return pl.pallas_call(
        body, name=name, grid=(m // tm, n // tn),
        in_specs=[pl.BlockSpec((tm, k), lambda i, j: (i, col_blk)), pl.BlockSpec((1, k), lambda i, j: (0, 0)),
                  pl.BlockSpec((1, k, tn), lambda i, j: (0, 0, j))],
        out_specs=[pl.BlockSpec((tm, k), lambda i, j: (i, 0)), pl.BlockSpec((tm, tn), lambda i, j: (i, j))],
        out_shape=[jax.ShapeDtypeStruct((m, k), BF), jax.ShapeDtypeStruct((m, n), BF)],
        compiler_params=_cp("parallel", "arbitrary"),
    )(x, g, b3)


def _mm_nt_rms_bwd(a, b3, x, g, col_blk, name, dep=None):
    m, n = a.shape
    _, width, n2 = b3.shape
    assert n == n2 and b3.shape[0] == 1
    tm = _pick(m, MM_TM, 16)
    tk = _pick(n, MM_TK, LANE)
    nk = n // tk

    def body(*refs):
        a_ref, b_ref, x_ref, g_ref = refs[:4]
        dx_ref, dg_ref = refs[4 + (dep is not None):6 + (dep is not None)]
        acc = refs[-1]
        kk = pl.program_id(1)

        @pl.when((pl.program_id(0) == 0) & (kk == 0))
        def _():
            dg_ref[...] = jnp.zeros_like(dg_ref)

        def part():
            return lax.dot_general(a_ref[...], b_ref[0], (((1,), (1,)), ((), ())), preferred_element_type=F32)

        def finish(du):
            xv = x_ref[...]
            r = lax.rsqrt(jnp.mean(xv * xv, axis=-1, keepdims=True) + NORM_EPS)
            nv = xv * r
            dn = du * g_ref[...]
            dx_ref[...] = (r * (dn - nv * jnp.mean(dn * nv, axis=-1, keepdims=True))).astype(BF)
            dg_ref[...] += _rows8([jnp.sum(du * nv, axis=0, keepdims=True)], width)

        _accumulate(kk, nk, acc, part, finish)

    in_specs = [pl.BlockSpec((tm, tk), lambda i, kk: (i, kk)), pl.BlockSpec((1, width, tk), lambda i, kk: (0, 0, kk)),
                pl.BlockSpec((tm, width), lambda i, kk: (i, col_blk)), pl.BlockSpec((1, width), lambda i, kk: (0, 0))]
    args = [a, b3, x, g]
    if dep is not None:
        in_specs.append(ANY)
        args.append(dep)
    return pl.pallas_call(
        body, name=name, grid=(m // tm, nk), in_specs=in_specs,
        out_specs=[pl.BlockSpec((tm, width), lambda i, kk: (i, 0)), pl.BlockSpec((SUB, width), lambda i, kk: (0, 0))],
        out_shape=[jax.ShapeDtypeStruct((m, width), BF), jax.ShapeDtypeStruct((SUB, width), F32)],
        scratch_shapes=[pltpu.VMEM((tm, width), F32)],
        compiler_params=_cp("arbitrary", "arbitrary"),
    )(*args)


def _mm_tn(a, b, nblk, name, out_dtype=BF, dep=None, into=None, blk0=0):
    s, m = a.shape
    s2, n = b.shape
    assert s == s2 and n % nblk == 0 and (dep is None or into is None)
    nbw = n // nblk
    tm = _pick(m, MM_TN, LANE)
    tn = _pick(nbw, MM_TN, LANE)
    ts = _pick(s, MM_TS, LANE)
    per = nbw // tn
    ns = s // ts

    def body(*refs):
        a_ref, b_ref = refs[:2]
        o_ref = refs[2 + (dep is not None or into is not None)]
        acc = refs[-1]

        def part():
            return lax.dot_general(a_ref[...].astype(BF), b_ref[...].astype(BF),
                                   (((0,), (0,)), ((), ())), preferred_element_type=F32)

        def finish(r):
            o_ref[...] = r.astype(out_dtype)

        _accumulate(pl.program_id(2), ns, acc, part, finish)

    in_specs = [pl.BlockSpec((ts, tm), lambda i, j, ss: (ss, i)),
                pl.BlockSpec((ts, tn), lambda i, j, ss: (ss, j))]
    out_spec = pl.BlockSpec((None, tm, tn), lambda i, j, ss: (blk0 + j // per, i, j % per))
    if into is None:
        return _mm_call(body, name, (m // tm, n // tn, ns), in_specs, [a, b], out_spec,
                        jax.ShapeDtypeStruct((nblk, m, nbw), out_dtype), (tm, tn), ns, dep)
    assert into.shape[1:] == (m, nbw) and into.dtype == out_dtype
    return pl.pallas_call(
        body, name=name, grid=(m // tm, n // tn, ns), in_specs=in_specs + [ANY], out_specs=out_spec,
        out_shape=jax.ShapeDtypeStruct(into.shape, out_dtype), input_output_aliases={2: 0},
        scratch_shapes=[pltpu.VMEM((tm, tn), F32)] if ns > 1 else [],
        compiler_params=_cp("parallel", "parallel", "arbitrary"),
    )(a, b, into)


def _rows8(rows, width):
    idx = lax.broadcasted_iota(jnp.int32, (SUB, width), 0)
    out = jnp.zeros((SUB, width), F32)
    for r, v in enumerate(rows):
        out = jnp.where(idx == r, v, out)
    return out


def _rms_fwd(x, g, width, col_blk, name):
    s = x.shape[0]
    tr = _pick(s, ROW_TILE, 16)

    def body(x_ref, g_ref, u_ref):
        xv = x_ref[...]
        r = lax.rsqrt(jnp.mean(xv * xv, axis=-1, keepdims=True) + NORM_EPS)
        u_ref[...] = ((xv * r) * g_ref[...]).astype(BF)

    return pl.pallas_call(
        body, name=name, grid=(s // tr,),
        in_specs=[pl.BlockSpec((tr, width), lambda i: (i, col_blk)),
                  pl.BlockSpec((1, width), lambda i: (0, 0))],
        out_specs=pl.BlockSpec((tr, width), lambda i: (i, 0)),
        out_shape=jax.ShapeDtypeStruct((s, width), BF),
        compiler_params=_cp("parallel"),
    )(x, g)


def _rms_bwd(x, du, g, width, col_blk, name, extra=None, out_dtype=F32, also_bf16=False):
    s = x.shape[0]
    tr = _pick(s, ROW_TILE_BWD, 16)

    def body(*refs):
        x_ref, du_ref, g_ref = refs[:3]
        e_ref = refs[3] if extra is not None else None
        dx_ref = refs[3 + (extra is not None)]
        dxb_ref = refs[4 + (extra is not None)] if also_bf16 else None
        dg_ref = refs[-1]
        i = pl.program_id(0)
        xv = x_ref[...]
        duv = du_ref[...].astype(F32)
        r = lax.rsqrt(jnp.mean(xv * xv, axis=-1, keepdims=True) + NORM_EPS)
        nv = xv * r
        dn = duv * g_ref[...]
        dx = r * (dn - nv * jnp.mean(dn * nv, axis=-1, keepdims=True))
        if extra is not None:
            dx = dx + e_ref[...]
        dx_ref[...] = dx.astype(out_dtype)
        if also_bf16:
            dxb_ref[...] = dx.astype(BF)

        @pl.when(i == 0)
        def _():
            dg_ref[...] = jnp.zeros_like(dg_ref)

        dg_ref[...] += _rows8([jnp.sum(duv * nv, axis=0, keepdims=True)], width)

    in_specs = [pl.BlockSpec((tr, width), lambda i: (i, col_blk)),
                pl.BlockSpec((tr, width), lambda i: (i, 0)),
                pl.BlockSpec((1, width), lambda i: (0, 0))]
    args = [x, du, g]
    if extra is not None:
        in_specs.append(pl.BlockSpec((tr, width), lambda i: (i, 0)))
        args.append(extra)
    return pl.pallas_call(
        body, name=name, grid=(s // tr,),
        in_specs=in_specs,
        out_specs=[pl.BlockSpec((tr, width), lambda i: (i, 0))] * (1 + also_bf16)
        + [pl.BlockSpec((SUB, width), lambda i: (0, 0))],
        out_shape=[jax.ShapeDtypeStruct((s, width), out_dtype)] + [jax.ShapeDtypeStruct((s, width), BF)] * also_bf16
        + [jax.ShapeDtypeStruct((SUB, width), F32)],
        compiler_params=_cp("arbitrary"),
    )(*args)


def _down(cur, prev8, k):
    ext = jnp.concatenate([prev8, cur], axis=0)
    return pltpu.roll(ext, k, axis=0)[SUB:]


def _up(cur, next8, k):
    ext = jnp.concatenate([cur, next8], axis=0)
    return pltpu.roll(ext, ext.shape[0] - k, axis=0)[:cur.shape[0]]


def _lags(cur, prev8):
    return _down(cur, prev8, 1), _down(cur, prev8, 2)


def _conv3(w_ref, cur, prev8, lags=None):
    lag1, lag2 = _lags(cur, prev8) if lags is None else lags
    return w_ref[0:1, :] * lag2 + w_ref[1:2, :] * lag1 + w_ref[2:3, :] * cur


def _conv3_t(w_ref, cur, next8):
    return w_ref[2:3, :] * cur + w_ref[1:2, :] * _up(cur, next8, 1) + w_ref[0:1, :] * _up(cur, next8, 2)


def _spec_cur(tr, tc, c0):
    return pl.BlockSpec((tr, tc), lambda j, i: (i, c0 + j))


def _spec_prev(tr, tc, c0):
    return pl.BlockSpec((SUB, tc), lambda j, i: (jnp.maximum(i * (tr // SUB) - 1, 0), c0 + j))


def _spec_next(tr, tc, c0, s):
    return pl.BlockSpec((SUB, tc), lambda j, i: (jnp.minimum((i + 1) * (tr // SUB), s // SUB - 1), c0 + j))


def _spec_w(tc, c0):
    return pl.BlockSpec((SUB, tc), lambda j, i: (0, c0 + j))


def _pad8(w):
    return jnp.pad(w, ((0, SUB - w.shape[0]), (0, 0)))


def _conv_mix_fwd(z_a, cw8, conv):
    s = z_a.shape[0]
    tr = _pick(s, ROW_TILE, 16)
    tc = _pick(conv, COL_TILE, LANE)
    nc = conv // tc

    def body(zb_ref, zc_ref, zv_ref, zcp_ref, zvp_ref, w_ref, p_ref):
        i = pl.program_id(1)
        cv = zc_ref[...] * zv_ref[...]
        cvp = jnp.where(i > 0, zcp_ref[...] * zvp_ref[...], 0.0)
        p_ref[...] = (zb_ref[...] * _conv3(w_ref, cv, cvp)).astype(BF)

    return pl.pallas_call(
        body, name="conv_mix_fwd", grid=(nc, s // tr),
        in_specs=[_spec_cur(tr, tc, 0), _spec_cur(tr, tc, nc), _spec_cur(tr, tc, 2 * nc),
                  _spec_prev(tr, tc, nc), _spec_prev(tr, tc, 2 * nc), _spec_w(tc, 0)],
        out_specs=_spec_cur(tr, tc, 0),
        out_shape=jax.ShapeDtypeStruct((s, conv), BF),
        compiler_params=_cp("parallel", "parallel"),
    )(z_a, z_a, z_a, z_a, z_a, cw8)


def _conv_mix_bwd(z_a, d_p, cw8, conv):
    s = z_a.shape[0]
    tr = _pick(s, ROW_TILE_BWD, 16)
    tc = _pick(conv, COL_TILE, LANE)
    nc = conv // tc
    nr = s // tr

    def body(zb_ref, zbn_ref, zc_ref, zcp_ref, zv_ref, zvp_ref, dp_ref, dpn_ref, w_ref,
             dzb_ref, dzc_ref, dzv_ref, dw_ref):
        i = pl.program_id(1)
        zc = zc_ref[...]
        zv = zv_ref[...]
        cv = zc * zv
        cvp = jnp.where(i > 0, zcp_ref[...] * zvp_ref[...], 0.0)
        cv1, cv2 = _lags(cv, cvp)
        dpv = dp_ref[...]
        dzb_ref[...] = (dpv * _conv3(w_ref, cv, cvp, (cv1, cv2))).astype(BF)
        dcc = dpv * zb_ref[...]
        dccn = jnp.where(i < nr - 1, dpn_ref[...] * zbn_ref[...], 0.0)
        dcv = _conv3_t(w_ref, dcc, dccn)
        dzc_ref[...] = (dcv * zv).astype(BF)
        dzv_ref[...] = (dcv * zc).astype(BF)

        @pl.when(i == 0)
        def _():
            dw_ref[...] = jnp.zeros_like(dw_ref)

        dw_ref[...] += _rows8([jnp.sum(dcc * cv2, axis=0, keepdims=True),
                               jnp.sum(dcc * cv1, axis=0, keepdims=True),
                               jnp.sum(dcc * cv, axis=0, keepdims=True)], tc)

    out = jax.ShapeDtypeStruct((s, conv), BF)
    return pl.pallas_call(
        body, name="conv_mix_bwd", grid=(nc, nr),
        in_specs=[_spec_cur(tr, tc, 0), _spec_next(tr, tc, 0, s),
                  _spec_cur(tr, tc, nc), _spec_prev(tr, tc, nc),
                  _spec_cur(tr, tc, 2 * nc), _spec_prev(tr, tc, 2 * nc),
                  _spec_cur(tr, tc, 0), _spec_next(tr, tc, 0, s), _spec_w(tc, 0)],
        out_specs=[_spec_cur(tr, tc, 0), _spec_cur(tr, tc, 0), _spec_cur(tr, tc, 0), _spec_w(tc, 0)],
        out_shape=[out, out, out, jax.ShapeDtypeStruct((SUB, conv), F32)],
        compiler_params=_cp("parallel", "arbitrary"),
    )(z_a, z_a, z_a, z_a, z_a, z_a, d_p, d_p, cw8)


def _silu_parts(ag):
    sg = jax.nn.sigmoid(ag)
    return ag * sg, sg


def _ffn_up_act(u2, w_up, cw8, cb, dff):
    s, d = u2.shape
    nb, _, nbw = w_up.shape
    half = nb // 2
    assert half * nbw == dff
    tm = _pick(s, ROW_TILE, 16)

    def body(u_ref, wg_ref, wu_ref, cg_ref, cu_ref, bg_ref, bu_ref, ag_ref, au_ref, f_ref, hist_g, hist_u):
        i = pl.program_id(1)

        @pl.when(i == 0)
        def _():
            hist_g[...] = jnp.zeros_like(hist_g)
            hist_u[...] = jnp.zeros_like(hist_u)

        u = u_ref[...]
        xg = jnp.dot(u, wg_ref[...], preferred_element_type=F32)
        xu = jnp.dot(u, wu_ref[...], preferred_element_type=F32)
        ag_ref[...] = xg
        au_ref[...] = xu
        ag = _conv3(cg_ref, xg, hist_g[...]) + bg_ref[...]
        au = _conv3(cu_ref, xu, hist_u[...]) + bu_ref[...]
        f_ref[...] = (_silu_parts(ag)[0] * au).astype(BF)
        hist_g[...] = xg[tm - SUB:]
        hist_u[...] = xu[tm - SUB:]

    once = pl.Buffered(1)
    tile = pl.BlockSpec((tm, nbw), lambda j, i: (i, j))
    return pl.pallas_call(
        body, name="mm_ffn_up_act", grid=(half, s // tm),
        in_specs=[pl.BlockSpec((tm, d), lambda j, i: (i, 0)),
                  pl.BlockSpec((None, d, nbw), lambda j, i: (j, 0, 0), pipeline_mode=once),
                  pl.BlockSpec((None, d, nbw), lambda j, i: (half + j, 0, 0), pipeline_mode=once),
                  pl.BlockSpec((SUB, nbw), lambda j, i: (0, j)), pl.BlockSpec((SUB, nbw), lambda j, i: (0, half + j)),
                  pl.BlockSpec((1, nbw), lambda j, i: (0, j)), pl.BlockSpec((1, nbw), lambda j, i: (0, half + j))],
        out_specs=[tile, tile, tile],
        out_shape=[jax.ShapeDtypeStruct((s, dff), F32), jax.ShapeDtypeStruct((s, dff), F32),
                   jax.ShapeDtypeStruct((s, dff), BF)],
        scratch_shapes=[pltpu.VMEM((SUB, nbw), F32), pltpu.VMEM((SUB, nbw), F32)],
        compiler_params=_cp("arbitrary", "arbitrary"),
    )(u2, w_up, w_up, cw8, cw8, cb, cb)


def _ffn_act_bwd(a_g, a_u, d_f, cw8, cb, dff):
    s = a_g.shape[0]
    tr = _pick(s, ROW_TILE_BWD // 2, 16)
    tc = _pick(dff, FFN_COL_TILE, LANE)
    nc = dff // tc
    nr = s // tr

    def body(xg_ref, xgp_ref, xgn_ref, xu_ref, xup_ref, xun_ref, df_ref, dfn_ref,
             wg_ref, wu_ref, bg_ref, bu_ref, dxg_ref, dxu_ref, dwg_ref, dwu_ref):
        i = pl.program_id(1)
        xg = xg_ref[...]
        xu = xu_ref[...]
        xgp = jnp.where(i > 0, xgp_ref[...], 0.0)
        xup = jnp.where(i > 0, xup_ref[...], 0.0)

        def d_act(xg_t, xgp_t, xu_t, xup_t, df_t, lags_g=None, lags_u=None):
            ag = _conv3(wg_ref, xg_t, xgp_t, lags_g) + bg_ref[...]
            au = _conv3(wu_ref, xu_t, xup_t, lags_u) + bu_ref[...]
            sil, sg = _silu_parts(ag)
            return df_t * au * (sg * (1.0 + ag * (1.0 - sg))), df_t * sil

        lags_g = _lags(xg, xgp)
        lags_u = _lags(xu, xup)
        dag, dau = d_act(xg, xgp, xu, xup, df_ref[...], lags_g, lags_u)
        dfn = jnp.where(i < nr - 1, dfn_ref[...], 0.0)
        dagn, daun = d_act(xgn_ref[...], xg[tr - SUB:], xun_ref[...], xu[tr - SUB:], dfn)
        dxg_ref[...] = _conv3_t(wg_ref, dag, dagn).astype(BF)
        dxu_ref[...] = _conv3_t(wu_ref, dau, daun).astype(BF)

        @pl.when(i == 0)
        def _():
            dwg_ref[...] = jnp.zeros_like(dwg_ref)
            dwu_ref[...] = jnp.zeros_like(dwu_ref)

        def wgrad(da, x, lags):
            return _rows8([jnp.sum(da * lags[1], axis=0, keepdims=True),
                           jnp.sum(da * lags[0], axis=0, keepdims=True),
                           jnp.sum(da * x, axis=0, keepdims=True),
                           jnp.sum(da, axis=0, keepdims=True)], tc)

        dwg_ref[...] += wgrad(dag, xg, lags_g)
        dwu_ref[...] += wgrad(dau, xu, lags_u)

    half = jax.ShapeDtypeStruct((s, dff), BF)
    wsh = jax.ShapeDtypeStruct((SUB, dff), F32)
    return pl.pallas_call(
        body, name="ffn_act_bwd", grid=(nc, nr),
        in_specs=[_spec_cur(tr, tc, 0), _spec_prev(tr, tc, 0), _spec_next(tr, tc, 0, s),
                  _spec_cur(tr, tc, 0), _spec_prev(tr, tc, 0), _spec_next(tr, tc, 0, s),
                  _spec_cur(tr, tc, 0), _spec_next(tr, tc, 0, s),
                  _spec_w(tc, 0), _spec_w(tc, nc),
                  pl.BlockSpec((1, tc), lambda j, i: (0, j)), pl.BlockSpec((1, tc), lambda j, i: (0, nc + j))],
        out_specs=[_spec_cur(tr, tc, 0), _spec_cur(tr, tc, 0), _spec_w(tc, 0), _spec_w(tc, 0)],
        out_shape=[half, half, wsh, wsh],
        compiler_params=_cp("parallel", "arbitrary"),
    )(a_g, a_g, a_g, a_u, a_u, a_u, d_f, d_f, cw8, cw8, cb, cb)


def _residual_norm(mix, w_oo, x, g):
    m, k = mix.shape
    d = w_oo.shape[2]
    tm = _pick(m, ROW_TILE, 16)

    def body(a_ref, b_ref, x_ref, g_ref, h_ref, u_ref):
        h = jnp.dot(a_ref[...], b_ref[0], preferred_element_type=F32) + x_ref[...]
        h_ref[...] = h
        r = lax.rsqrt(jnp.mean(h * h, axis=-1, keepdims=True) + NORM_EPS)
        u_ref[...] = ((h * r) * g_ref[...]).astype(BF)

    row = pl.BlockSpec((tm, d), lambda i: (i, 0))
    return pl.pallas_call(
        body, name="mm_h1_norm", grid=(m // tm,),
        in_specs=[pl.BlockSpec((tm, k), lambda i: (i, 0)),
                  pl.BlockSpec((1, k, d), lambda i: (0, 0, 0), pipeline_mode=pl.Buffered(1)),
                  row, pl.BlockSpec((1, d), lambda i: (0, 0))],
        out_specs=[row, row],
        out_shape=[jax.ShapeDtypeStruct((m, d), F32), jax.ShapeDtypeStruct((m, d), BF)],
        compiler_params=_cp("parallel"),
    )(mix, w_oo, x, g)


def _mla_out_gate(o, w_mo, z_g, b_gate, yc):
    m, k = o.shape
    d = w_mo.shape[2]
    tm = _pick(m, MM_TM, 16)
    tn = _pick(d, MM_TM, LANE)
    nc = d // tn

    def body(a_ref, b_ref, za_ref, zb_ref, ba_ref, bb_ref, yc_ref, ym_ref, mix_ref):
        ym = jnp.dot(a_ref[...], b_ref[0], preferred_element_type=F32)
        ga = jax.nn.sigmoid(za_ref[...] + ba_ref[...])
        gb = jax.nn.sigmoid(zb_ref[...] + bb_ref[...])
        ym_ref[...] = ym.astype(BF)
        mix_ref[...] = (ga * yc_ref[...] + gb * ym).astype(BF)

    tile = pl.BlockSpec((tm, tn), lambda i, j: (i, j))
    out = jax.ShapeDtypeStruct((m, d), BF)
    return pl.pallas_call(
        body, name="mm_y_mla_gate", grid=(m // tm, nc),
        in_specs=[pl.BlockSpec((tm, k), lambda i, j: (i, 0)), pl.BlockSpec((1, k, tn), lambda i, j: (0, 0, j)),
                  tile, pl.BlockSpec((tm, tn), lambda i, j: (i, nc + j)),
                  pl.BlockSpec((1, tn), lambda i, j: (0, j)), pl.BlockSpec((1, tn), lambda i, j: (0, nc + j)), tile],
        out_specs=[tile, tile], out_shape=[out, out],
        compiler_params=_cp("parallel", "parallel"),
    )(o, w_mo, z_g, z_g, b_gate, b_gate, yc)


def _d_mix_gate(d_h1, w_oo, z_g, b_gate, yc, ym):
    m, n = d_h1.shape
    d = w_oo.shape[1]
    tm = _pick(m, ROW_TILE, 16)
    tn = _pick(d, MM_TM, LANE)
    nc = d // tn

    def body(a_ref, b_ref, za_ref, zb_ref, ba_ref, bb_ref, yc_ref, ym_ref,
             dza_ref, dzb_ref, dyc_ref, dym_ref, dba_ref, dbb_ref):
        i = pl.program_id(1)
        dm = lax.dot_general(a_ref[...], b_ref[0], (((1,), (1,)), ((), ())), preferred_element_type=F32)
        ga = jax.nn.sigmoid(za_ref[...] + ba_ref[...])
        gb = jax.nn.sigmoid(zb_ref[...] + bb_ref[...])
        dza = dm * yc_ref[...] * (ga * (1.0 - ga))
        dzb = dm * ym_ref[...] * (gb * (1.0 - gb))
        dza_ref[...] = dza.astype(BF)
        dzb_ref[...] = dzb.astype(BF)
        dyc_ref[...] = (dm * ga).astype(BF)
        dym_ref[...] = (dm * gb).astype(BF)

        @pl.when(i == 0)
        def _():
            dba_ref[...] = jnp.zeros_like(dba_ref)
            dbb_ref[...] = jnp.zeros_like(dbb_ref)

        dba_ref[...] += _rows8([jnp.sum(dza, axis=0, keepdims=True)], tn)
        dbb_ref[...] += _rows8([jnp.sum(dzb, axis=0, keepdims=True)], tn)

    tile = pl.BlockSpec((tm, tn), lambda j, i: (i, j))
    act = jax.ShapeDtypeStruct((m, d), BF)
    bsh = jax.ShapeDtypeStruct((SUB, d), F32)
    return pl.pallas_call(
        body, name="mm_d_mix_gate", grid=(nc, m // tm),
        in_specs=[pl.BlockSpec((tm, n), lambda j, i: (i, 0)), pl.BlockSpec((1, tn, n), lambda j, i: (0, j, 0)),
                  tile, pl.BlockSpec((tm, tn), lambda j, i: (i, nc + j)),
                  pl.BlockSpec((1, tn), lambda j, i: (0, j)), pl.BlockSpec((1, tn), lambda j, i: (0, nc + j)),
                  tile, tile],
        out_specs=[tile] * 4 + [pl.BlockSpec((SUB, tn), lambda j, i: (0, j))] * 2,
        out_shape=[act, act, act, act, bsh, bsh],
        compiler_params=_cp("parallel", "arbitrary"),
    )(d_h1, w_oo, z_g, z_g, b_gate, b_gate, yc, ym)


def _lay(v):
    z = jnp.zeros(v.shape[:-1] + (HALF,), v.dtype)
    return jnp.concatenate([v[..., :HALF], z, v[..., HALF:], z], axis=-1)


def _unlay(v):
    return jnp.concatenate([v[..., :HALF], v[..., 2 * HALF:3 * HALF]], axis=-1)


def _lay_rows(v):
    z = jnp.zeros((HALF,) + v.shape[1:], v.dtype)
    return jnp.concatenate([v[:HALF], z, v[HALF:], z], axis=0)


def _rope_tables(positions):
    s = positions.shape[0]
    tr = _pick(s, ROW_TILE, 8)
    inv_freq = ROPE_THETA ** (-jnp.arange(0, ROPE, 2, dtype=F32) / ROPE)
    consts = jnp.stack([_lay(jnp.concatenate([inv_freq, inv_freq])),
                        _lay(jnp.ones((ROPE,), F32)),
                        _lay(jnp.concatenate([-jnp.ones((HALF,), F32), jnp.ones((HALF,), F32)]))])
    consts = _pad8(consts)

    def body(p_ref, c_ref, cos_ref, sin_ref):
        ang = p_ref[...].astype(F32) * c_ref[0:1, :]
        cos_ref[...] = jnp.cos(ang) * c_ref[1:2, :]
        sin_ref[...] = jnp.sin(ang) * c_ref[2:3, :]

    tab = jax.ShapeDtypeStruct((s, LANE), F32)
    return pl.pallas_call(
        body, name="rope_tables", grid=(s // tr,),
        in_specs=[pl.BlockSpec((tr, 1), lambda i: (i, 0)), pl.BlockSpec((SUB, LANE), lambda i: (0, 0))],
        out_specs=[pl.BlockSpec((tr, LANE), lambda i: (i, 0))] * 2,
        out_shape=[tab, tab],
        compiler_params=_cp("parallel"),
    )(positions, consts)


def _lane_sum(p):
    return jnp.sum(p, axis=-1, keepdims=True)


def _rope(t, cos, sin):
    return t * cos + pltpu.roll(t, 2 * HALF, axis=1) * sin


def _rope_t(d, cos, sin):
    return d * cos + pltpu.roll(d * sin, 2 * HALF, axis=1)


def _head_fwd(q_raw, kv_raw, z_a, kr_blk, cos, sin, gains, heads):
    s = q_raw.shape[0]
    tr = _pick(s, HEAD_ROW_TILE, 16)
    hw = heads * LANE

    def body(q_ref, kv_ref, kr_ref, cos_ref, sin_ref, g_ref, qo_ref, ko_ref, vo_ref):
        cosv = cos_ref[...]
        sinv = sin_ref[...]
        krv = kr_ref[...]
        kr_sq = krv * krv
        for h in range(heads):
            lo = h * LANE
            qn = q_ref[:, lo:lo + LANE].astype(F32)
            qr = q_ref[:, hw + lo:hw + lo + LANE].astype(F32)
            r = lax.rsqrt(_lane_sum(qn * qn + qr * qr) / HEAD_QK + NORM_EPS)
            qo_ref[:, 2 * lo:2 * lo + LANE] = (((qn * r) * g_ref[0:1, :]) * (QK_SCALE * LOG2_E)).astype(BF)
            qo_ref[:, 2 * lo + LANE:2 * lo + 2 * LANE] = (
                _rope((qr * r) * g_ref[1:2, :], cosv, sinv) * (QK_SCALE * LOG2_E)).astype(BF)
            kn = kv_ref[:, 2 * lo:2 * lo + LANE].astype(F32)
            r = lax.rsqrt(_lane_sum(kn * kn + kr_sq) / HEAD_QK + NORM_EPS)
            ko_ref[:, 2 * lo:2 * lo + LANE] = ((kn * r) * g_ref[2:3, :]).astype(BF)
            ko_ref[:, 2 * lo + LANE:2 * lo + 2 * LANE] = _rope((krv * r) * g_ref[3:4, :], cosv, sinv).astype(BF)
            vo_ref[:, lo:lo + LANE] = kv_ref[:, 2 * lo + LANE:2 * lo + 2 * LANE].astype(BF)

    row = lambda w: pl.BlockSpec((tr, w), lambda i: (i, 0))
    return pl.pallas_call(
        body, name="head_fwd", grid=(s // tr,),
        in_specs=[row(2 * hw), row(2 * hw), pl.BlockSpec((tr, LANE), lambda i: (i, kr_blk)),
                  row(LANE), row(LANE), pl.BlockSpec((SUB, LANE), lambda i: (0, 0))],
        out_specs=[row(2 * hw), row(2 * hw), row(hw)],
        out_shape=[jax.ShapeDtypeStruct((s, 2 * hw), BF), jax.ShapeDtypeStruct((s, 2 * hw), BF),
                   jax.ShapeDtypeStruct((s, hw), BF)],
        compiler_params=_cp("parallel"),
    )(q_raw, kv_raw, z_a, cos, sin, gains)


def _head_bwd(q_raw, kv_raw, z_a, kr_blk, cos, sin, gains, dq_att, dk_att, dv, heads):
    s = q_raw.shape[0]
    tr = _pick(s, HEAD_ROW_TILE_BWD, 16)
    hw = heads * LANE

    def body(q_ref, kv_ref, kr_ref, cos_ref, sin_ref, g_ref, dq_ref, dk_ref, dv_ref,
             dqr_ref, dkv_ref, dkr_ref, dg_ref):
        i = pl.program_id(0)
        cosv = cos_ref[...]
        sinv = sin_ref[...]
        krv = kr_ref[...]
        kr_sq = krv * krv
        dkr = jnp.zeros((tr, LANE), F32)
        dgs = [jnp.zeros((1, LANE), F32) for _ in range(4)]

        def norm_bwd(xn, xr, sq, dn_out, dr_out, gn, gr):
            r = lax.rsqrt(_lane_sum(sq) / HEAD_QK + NORM_EPS)
            nn = xn * r
            nr = xr * r
            dt = _rope_t(dr_out, cosv, sinv)
            dnn = dn_out * gn
            dnr = dt * gr
            mean = _lane_sum(dnn * nn + dnr * nr) / HEAD_QK
            return (r * (dnn - nn * mean), r * (dnr - nr * mean),
                    jnp.sum(dn_out * nn, axis=0, keepdims=True), jnp.sum(dt * nr, axis=0, keepdims=True))

        for h in range(heads):
            lo = h * LANE
            qn = q_ref[:, lo:lo + LANE].astype(F32)
            qr = q_ref[:, hw + lo:hw + lo + LANE].astype(F32)
            dxn, dxr, g0, g1 = norm_bwd(qn, qr, qn * qn + qr * qr, dq_ref[:, 2 * lo:2 * lo + LANE] * QK_SCALE,
                                        dq_ref[:, 2 * lo + LANE:2 * lo + 2 * LANE] * QK_SCALE,
                                        g_ref[0:1, :], g_ref[1:2, :])
            dqr_ref[:, lo:lo + LANE] = dxn.astype(BF)
            dqr_ref[:, hw + lo:hw + lo + LANE] = dxr.astype(BF)
            kn = kv_ref[:, 2 * lo:2 * lo + LANE].astype(F32)
            dxn, dxr, g2, g3 = norm_bwd(kn, krv, kn * kn + kr_sq, dk_ref[:, 2 * lo:2 * lo + LANE],
                                        dk_ref[:, 2 * lo + LANE:2 * lo + 2 * LANE], g_ref[2:3, :], g_ref[3:4, :])
            dkv_ref[:, 2 * lo:2 * lo + LANE] = dxn.astype(BF)
            dkv_ref[:, 2 * lo + LANE:2 * lo + 2 * LANE] = dv_ref[:, lo:lo + LANE].astype(BF)
            dkr = dkr + dxr
            dgs = [a + b for a, b in zip(dgs, (g0, g1, g2, g3))]
        dkr_ref[...] = dkr

        @pl.when(i == 0)
        def _():
            dg_ref[...] = jnp.zeros_like(dg_ref)

        dg_ref[...] += _rows8(dgs, LANE)

    row = lambda w: pl.BlockSpec((tr, w), lambda i: (i, 0))
    return pl.pallas_call(
        body, name="head_bwd", grid=(s // tr,),
        in_specs=[row(2 * hw), row(2 * hw), pl.BlockSpec((tr, LANE), lambda i: (i, kr_blk)),
                  row(LANE), row(LANE), pl.BlockSpec((SUB, LANE), lambda i: (0, 0)),
                  row(2 * hw), row(2 * hw), row(hw)],
        out_specs=[row(2 * hw), row(2 * hw), row(LANE), pl.BlockSpec((SUB, LANE), lambda i: (0, 0))],
        out_shape=[jax.ShapeDtypeStruct((s, 2 * hw), BF), jax.ShapeDtypeStruct((s, 2 * hw), BF),
                   jax.ShapeDtypeStruct((s, LANE), F32), jax.ShapeDtypeStruct((SUB, LANE), F32)],
        compiler_params=_cp("arbitrary"),
    )(q_raw, kv_raw, z_a, cos, sin, gains, dq_att, dk_att, dv)


def _causal_mask(nrows, ncols, row0):
    rows = lax.broadcasted_iota(jnp.int32, (nrows, ncols), 0) + row0
    cols = lax.broadcasted_iota(jnp.int32, (nrows, ncols), 1)
    return cols <= rows


def _causal_steps(nt, q_major):
    pairs = ([(i, j) for i in range(nt) for j in range(i + 1)] if q_major
             else [(i, j) for j in range(nt) for i in range(j, nt)])
    return (jnp.array([p[0] for p in pairs], jnp.int32), jnp.array([p[1] for p in pairs], jnp.int32))


def _attn_fwd(q_att, k_att, v, heads):
    s = q_att.shape[0]
    t = _pick(s, ATTN_TILE_FWD, LANE)
    nt = s // t
    th = t // 2
    qi, kj = _causal_steps(nt, True)

    def body(qi_ref, kj_ref, q_ref, k_ref, v_ref, o_ref, ob_ref, lse_ref, m_s, l_s, acc_s):
        st = pl.program_id(1)
        i = qi_ref[st]
        j = kj_ref[st]

        @pl.when(j == 0)
        def _():
            m_s[...] = jnp.full_like(m_s, NEG_INF)
            l_s[...] = jnp.zeros_like(l_s)
            acc_s[...] = jnp.zeros_like(acc_s)

        def update(rows, ncol, masked):
            sc = lax.dot_general(q_ref[rows, :], k_ref[0:ncol, :], (((1,), (1,)), ((), ())),
                                 preferred_element_type=F32)
            if masked:
                sc = jnp.where(_causal_mask(rows.stop - rows.start, ncol, rows.start), sc, NEG_INF)
            m_prev = m_s[rows, :]
            m_new = jnp.maximum(m_prev, jnp.max(sc, axis=-1, keepdims=True))
            alpha = jnp.exp2(m_prev - m_new)
            p = jnp.exp2(sc - jnp.tile(m_new, (1, ncol // LANE)))
            l_s[rows, :] = alpha * l_s[rows, :] + jnp.sum(p, axis=-1, keepdims=True)
            acc_s[rows, :] = alpha * acc_s[rows, :] + jnp.dot(p.astype(BF), v_ref[0:ncol, :],
                                                              preferred_element_type=F32)
            m_s[rows, :] = m_new

        @pl.when(j < i)
        def _():
            update(slice(0, t), t, False)

        @pl.when(j == i)
        def _():
            update(slice(0, th), th, True)
            update(slice(th, t), t, True)
            o = acc_s[...] / l_s[...]
            o_ref[...] = o
            ob_ref[...] = o.astype(BF)
            lse_ref[...] = (m_s[...] + jnp.log2(l_s[...]))[:, 0:1]

    q_idx = lambda h, st, qi_r, kj_r: (qi_r[st], h)
    kv_idx = lambda h, st, qi_r, kj_r: (kj_r[st], h)
    return pl.pallas_call(
        body, name="attn_fwd",
        grid_spec=pltpu.PrefetchScalarGridSpec(
            num_scalar_prefetch=2, grid=(heads, qi.shape[0]),
            in_specs=[pl.BlockSpec((t, 2 * LANE), q_idx), pl.BlockSpec((t, 2 * LANE), kv_idx),
                      pl.BlockSpec((t, LANE), kv_idx)],
            out_specs=[pl.BlockSpec((t, LANE), q_idx), pl.BlockSpec((t, LANE), q_idx),
                       pl.BlockSpec((None, t, 1), lambda h, st, qi_r, kj_r: (h, qi_r[st], 0))],
            scratch_shapes=[pltpu.VMEM((t, LANE), F32), pltpu.VMEM((t, LANE), F32), pltpu.VMEM((t, LANE), F32)]),
        out_shape=[jax.ShapeDtypeStruct((s, heads * LANE), F32), jax.ShapeDtypeStruct((s, heads * LANE), BF),
                   jax.ShapeDtypeStruct((heads, s, 1), F32)],
        compiler_params=_cp("parallel", "arbitrary"),
    )(qi, kj, q_att, k_att, v)


def _attn_bwd(q_att, k_att, v, o, lse, d_o, heads, dep=None):
    s = q_att.shape[0]
    t = _pick(s, ATTN_TILE, LANE)
    nt = s // t
    th = t // 2
    qi, kj = _causal_steps(nt, False)

    def body(qi_ref, kj_ref, q_ref, k_ref, v_ref, do_ref, o_ref, lse_ref, *rest):
        dq_ref, dk_ref, dv_ref, dk_s, dv_s = rest[-5:]
        st = pl.program_id(1)
        i = qi_ref[st]
        j = kj_ref[st]

        @pl.when(st == 0)
        def _():
            dq_ref[...] = jnp.zeros_like(dq_ref)

        @pl.when(i == j)
        def _():
            dk_s[...] = jnp.zeros_like(dk_s)
            dv_s[...] = jnp.zeros_like(dv_s)

        def update(rows, ncol, masked):
            nrow = rows.stop - rows.start
            q = q_ref[rows, :]
            k = k_ref[0:ncol, :]
            do = do_ref[rows, :]
            sc = lax.dot_general(q, k, (((1,), (1,)), ((), ())), preferred_element_type=F32)
            if masked:
                sc = jnp.where(_causal_mask(nrow, ncol, rows.start), sc, NEG_INF)
            p = jnp.exp2(sc - lse_ref[rows, :])
            dp = lax.dot_general(do, v_ref[0:ncol, :], (((1,), (1,)), ((), ())), preferred_element_type=F32)
            delta = jnp.sum(do.astype(F32) * o_ref[rows, :], axis=-1, keepdims=True)
            ds = (p * (dp - delta)).astype(BF)
            dv_s[0:ncol, :] += lax.dot_general(p.astype(BF), do, (((0,), (0,)), ((), ())),
                                               preferred_element_type=F32)
            dk_s[0:ncol, :] += lax.dot_general(ds, q, (((0,), (0,)), ((), ())), preferred_element_type=F32)
            out_rows = pl.ds(pl.multiple_of(i * t + rows.start, nrow), nrow)
            dq_ref[out_rows, :] += jnp.dot(ds, k, preferred_element_type=F32)

        @pl.when(i > j)
        def _():
            update(slice(0, t), t, False)

        @pl.when(i == j)
        def _():
            update(slice(0, th), th, True)
            update(slice(th, t), t, True)

        @pl.when(i == nt - 1)
        def _():
            dk_ref[...] = (dk_s[...] * (1.0 / LOG2_E)).astype(BF)
            dv_ref[...] = dv_s[...].astype(BF)

    q_idx = lambda h, st, qi_r, kj_r: (qi_r[st], h)
    kv_idx = lambda h, st, qi_r, kj_r: (kj_r[st], h)
    in_specs = [pl.BlockSpec((t, 2 * LANE), q_idx), pl.BlockSpec((t, 2 * LANE), kv_idx),
                pl.BlockSpec((t, LANE), kv_idx), pl.BlockSpec((t, LANE), q_idx), pl.BlockSpec((t, LANE), q_idx),
                pl.BlockSpec((None, t, 1), lambda h, st, qi_r, kj_r: (h, qi_r[st], 0))]
    args = [q_att, k_att, v, d_o, o, lse]
    if dep is not None:
        in_specs.append(ANY)
        args.append(dep)
    return pl.pallas_call(
        body, name="attn_bwd",
        grid_spec=pltpu.PrefetchScalarGridSpec(
            num_scalar_prefetch=2, grid=(heads, qi.shape[0]),
            in_specs=in_specs,
            out_specs=[pl.BlockSpec((s, 2 * LANE), lambda h, st, qi_r, kj_r: (0, h)),
                       pl.BlockSpec((t, 2 * LANE), kv_idx), pl.BlockSpec((t, LANE), kv_idx)],
            scratch_shapes=[pltpu.VMEM((t, 2 * LANE), F32), pltpu.VMEM((t, LANE), F32)]),
        out_shape=[jax.ShapeDtypeStruct((s, heads * 2 * LANE), F32),
                   jax.ShapeDtypeStruct((s, heads * 2 * LANE), BF),
                   jax.ShapeDtypeStruct((s, heads * LANE), BF)],
        compiler_params=_cp("parallel", "arbitrary"),
    )(qi, kj, *args)


def _sum_parts(parts, name):
    n, r, c = parts.shape
    tr = _pick(r, 512, 8)

    def body(p_ref, o_ref):
        g = p_ref[0].astype(F32)
        for k in range(1, n):
            g = g + p_ref[k].astype(F32)
        o_ref[...] = g

    return pl.pallas_call(
        body, name=name, grid=(r // tr,),
        in_specs=[pl.BlockSpec((n, tr, c), lambda i: (0, i, 0))],
        out_specs=pl.BlockSpec((tr, c), lambda i: (i, 0)),
        out_shape=jax.ShapeDtypeStruct((r, c), F32),
        compiler_params=_cp("parallel"),
    )(parts)


def _adamw(parts, w, m, v, name, by_cols=False):
    n, rp, c = parts.shape
    r = w.shape[0]
    assert by_cols or rp == r
    tr, tc = (r, _pick(c, 256, LANE)) if by_cols else (_pick(r, 256, 16 if r % 16 == 0 else 8), c)

    def body(p_ref, w_ref, m_ref, v_ref, g_ref, d_ref, mo_ref, vo_ref):
        g = p_ref[0].astype(F32)
        for k in range(1, n):
            g = g + p_ref[k].astype(F32)
        g = g[:r] if by_cols else g
        m_new = ADAM_B1 * m_ref[...] + (1.0 - ADAM_B1) * g
        v_new = ADAM_B2 * v_ref[...] + (1.0 - ADAM_B2) * jnp.square(g)
        m_hat = m_new / (1.0 - ADAM_B1 ** ADAM_STEP)
        v_hat = v_new / (1.0 - ADAM_B2 ** ADAM_STEP)
        g_ref[...] = g
        d_ref[...] = -ADAM_LR * (m_hat / (jnp.sqrt(v_hat) + ADAM_EPS) + ADAM_WD * w_ref[...])
        mo_ref[...] = m_new
        vo_ref[...] = v_new

    idx = (lambda i: (0, i)) if by_cols else (lambda i: (i, 0))
    spec = pl.BlockSpec((tr, tc), idx)
    sh = jax.ShapeDtypeStruct((r, c), F32)
    return pl.pallas_call(
        body, name=name, grid=(c // tc if by_cols else r // tr,),
        in_specs=[pl.BlockSpec((n, rp if by_cols else tr, tc), lambda i: (0,) + idx(i)), spec, spec, spec],
        out_specs=[spec] * 4, out_shape=[sh] * 4,
        compiler_params=_cp("parallel"),
    )(parts, w, m, v)


def _place():
    x, y, c = lax.axis_index("x"), lax.axis_index("y"), lax.axis_index("c")
    chips = [(1 - x, y), (x, 1 - y), (1 - x, 1 - y)]
    return x, y, c, chips


def _all_gather(shards, name, dep=None):
    n = len(shards)
    deps = [] if dep is None else list(dep)

    def body(*refs):
        ins, outs = refs[:n], refs[n + len(deps):2 * n + len(deps)]
        send_sems, recv_sems, local_sems = refs[2 * n + len(deps):]
        x, y, c, chips = _place()
        me, sibling = (x, y, c), (x, y, 1 - c)

        def slot(w, p):
            return outs[w].at[4 * p[0] + 2 * p[1] + p[2]]

        def copy(w, k, block, to, src=None):
            return pltpu.make_async_remote_copy(
                src_ref=slot(w, block) if src is None else src, dst_ref=slot(w, block),
                send_sem=send_sems.at[w, k], recv_sem=recv_sems.at[w, k], device_id=to, device_id_type=MESH)

        first = []
        for w in range(n):
            first += [copy(w, 1 + j, me, (*chip, c), src=ins[w]) for j, chip in enumerate(chips)]
            first.append(copy(w, 0, me, sibling, src=ins[w]))
        for cp in first:
            cp.start()
        mine = [pltpu.make_async_copy(ins[w], slot(w, me), local_sems.at[w]) for w in range(n)]
        for cp in mine:
            cp.start()
        passed = []
        for w in range(n):
            for j, chip in enumerate(chips):
                copy(w, 1 + j, (*chip, c), me).wait_recv()
                cp = copy(w, 4 + j, (*chip, c), sibling)
                cp.start()
                passed.append(cp)
        for w in range(n):
            copy(w, 0, sibling, me).wait_recv()
            for j, chip in enumerate(chips):
                copy(w, 4 + j, (*chip, 1 - c), me).wait_recv()
        for cp in first + passed:
            cp.wait_send()
        for cp in mine:
            cp.wait()

    return pl.pallas_call(
        body, name=name,
        in_specs=[ANY] * (n + len(deps)), out_specs=[ANY] * n,
        out_shape=[jax.ShapeDtypeStruct((N_DEV,) + a.shape, a.dtype) for a in shards],
        scratch_shapes=[pltpu.SemaphoreType.DMA((n, 7)), pltpu.SemaphoreType.DMA((n, 7)),
                        pltpu.SemaphoreType.DMA((n,))],
    )(*shards, *deps)


HBM = pl.BlockSpec(memory_space=pltpu.HBM)
SEM = pl.BlockSpec(memory_space=pltpu.SEMAPHORE)
EFFECT = pltpu.SideEffectType.DATAFLOW_SIDE_EFFECTING
PEERS = [(dx, dy, dc) for dx in (1, 0) for dy in (1, 0) for dc in (0, 1) if (dx, dy, dc) != (0, 0, 0)]


def _peer(x, y, c, flip):
    dx, dy, dc = flip
    return (1 - x if dx else x, 1 - y if dy else y, 1 - c if dc else c)


def _exchange_copies(srcs, lands, send, recv, loc, gather):
    x, y, c, _ = _place()
    me = 4 * x + 2 * y + c
    remote, local = [], []
    for w in range(len(srcs)):
        for k, flip in enumerate(PEERS):
            px, py, pc = _peer(x, y, c, flip)
            src = srcs[w] if gather else srcs[w].at[4 * px + 2 * py + pc]
            remote.append(pltpu.make_async_remote_copy(
                src_ref=src, dst_ref=lands[w].at[me], send_sem=send[w].at[k], recv_sem=recv[w].at[k],
                device_id=(px, py, pc), device_id_type=MESH))
        local.append(pltpu.make_async_copy(srcs[w] if gather else srcs[w].at[me], lands[w].at[me], loc[w]))
    return remote, local


class _Exchange:
    def __init__(self, srcs, lands, send, recv, loc, token, gather):
        self.srcs, self.lands, self.send, self.recv, self.loc = srcs, lands, send, recv, loc
        self.token, self.gather = token, gather


def _exchange_start(srcs, gather, name, dep=None):
    n = len(srcs)
    deps = [] if dep is None else [dep]
    land_shapes = [((N_DEV,) + a.shape) if gather else a.shape for a in srcs]
    lands = [pltpu.with_memory_space_constraint(lax.empty(sh, a.dtype), pltpu.HBM) for sh, a in zip(land_shapes, srcs)]
    srcs = [pltpu.with_memory_space_constraint(a, pltpu.HBM) for a in srcs]

    def body(*refs):
        src_refs, land_refs = refs[:n], refs[n:2 * n]
        outs = refs[2 * n + len(deps):]
        send, recv, loc = outs[:n], outs[n:2 * n], outs[2 * n:3 * n]
        token = outs[-1]
        remote, local = _exchange_copies(src_refs, land_refs, send, recv, loc, gather)
        for cp in remote + local:
            cp.start()
        token[...] = jnp.zeros_like(token)

    out_shape = ([pltpu.SemaphoreType.DMA((len(PEERS),))] * (2 * n) + [pltpu.SemaphoreType.DMA(())] * n
                 + [pltpu.HBM(a.shape, a.dtype) for a in srcs] + [pltpu.HBM(a.shape, a.dtype) for a in lands]
                 + [jax.ShapeDtypeStruct((SUB, LANE), F32)])
    res = pl.pallas_call(
        body, name=name, out_shape=out_shape,
        in_specs=[HBM] * (2 * n) + [ANY] * len(deps),
        out_specs=[SEM] * (3 * n) + [HBM] * (2 * n) + [pl.BlockSpec(memory_space=pltpu.VMEM)],
        input_output_aliases={i: 3 * n + i for i in range(2 * n)},
        compiler_params=pltpu.CompilerParams(has_side_effects=EFFECT),
    )(*srcs, *lands, *deps)
    return _Exchange(res[3 * n:4 * n], res[4 * n:5 * n], res[:n], res[n:2 * n], res[2 * n:3 * n], res[-1], gather)


def _exchange_wait(ex, idxs, after, name):
    n = len(idxs)
    srcs = [ex.srcs[i] for i in idxs]
    lands = [ex.lands[i] for i in idxs]
    sems = [ex.send[i] for i in idxs] + [ex.recv[i] for i in idxs] + [ex.loc[i] for i in idxs]
    gather = ex.gather

    def body(*refs):
        src_refs, land_refs = refs[:n], refs[n:2 * n]
        send, recv, loc = refs[2 * n:3 * n], refs[3 * n:4 * n], refs[4 * n:5 * n]
        remote, local = _exchange_copies(src_refs, land_refs, send, recv, loc, gather)
        for cp in remote:
            cp.wait_send()
            cp.wait_recv()
        for cp in local:
            cp.wait()

    res = pl.pallas_call(
        body, name=name,
        out_shape=[pltpu.HBM(a.shape, a.dtype) for a in srcs] + [pltpu.HBM(a.shape, a.dtype) for a in lands],
        in_specs=[HBM] * (2 * n) + [SEM] * (3 * n) + [ANY],
        out_specs=[HBM] * (2 * n),
        input_output_aliases={i: i for i in range(2 * n)},
        compiler_params=pltpu.CompilerParams(has_side_effects=EFFECT),
    )(*srcs, *lands, *sems, after)
    return res[n:]


def _gather2_copies(srcs, lands, send, recv_ici, recv_sib, loc):
    x, y, c, chips = _place()
    me = 4 * x + 2 * y + c
    remote, local = [], []
    for w in range(len(srcs)):
        remote.append(pltpu.make_async_remote_copy(
            src_ref=srcs[w], dst_ref=lands[w].at[me], send_sem=send[w].at[0], recv_sem=recv_sib[w],
            device_id=(x, y, 1 - c), device_id_type=MESH))
        for j, chip in enumerate(chips):
            remote.append(pltpu.make_async_remote_copy(
                src_ref=srcs[w], dst_ref=lands[w].at[me], send_sem=send[w].at[1 + j], recv_sem=recv_ici[w].at[j],
                device_id=(*chip, c), device_id_type=MESH))
        local.append(pltpu.make_async_copy(srcs[w], lands[w].at[me], loc[w]))
    return remote, local


def _gather2_forwards(lands, fsend, frecv, arrived=None):
    x, y, c, chips = _place()
    cps = []
    for w in range(len(lands)):
        for j, chip in enumerate(chips):
            slot = lands[w].at[4 * chip[0] + 2 * chip[1] + c]
            cp = pltpu.make_async_remote_copy(
                src_ref=slot, dst_ref=slot, send_sem=fsend[w].at[j], recv_sem=frecv[w].at[j],
                device_id=(x, y, 1 - c), device_id_type=MESH)
            if arrived is not None:
                pltpu.make_async_remote_copy(
                    src_ref=slot, dst_ref=slot, send_sem=fsend[w].at[j], recv_sem=arrived[w].at[j],
                    device_id=(x, y, 1 - c), device_id_type=MESH).wait_recv()
            cps.append(cp)
    return cps


def _gather2(shards, between, name):
    n = len(shards)
    srcs = [pltpu.with_memory_space_constraint(a, pltpu.HBM) for a in shards]
    lands = [pltpu.with_memory_space_constraint(lax.empty((N_DEV,) + a.shape, a.dtype), pltpu.HBM) for a in shards]
    hbm_like = lambda arrs: [pltpu.HBM(a.shape, a.dtype) for a in arrs]
    tok = jax.ShapeDtypeStruct((SUB, LANE), F32)
    vmem = pl.BlockSpec(memory_space=pltpu.VMEM)
    side = pltpu.CompilerParams(has_side_effects=EFFECT)

    def start(*refs):
        src_refs, land_refs = refs[:n], refs[n:2 * n]
        outs = refs[2 * n:]
        send, recv_ici, recv_sib, loc = outs[:n], outs[n:2 * n], outs[2 * n:3 * n], outs[3 * n:4 * n]
        remote, local = _gather2_copies(src_refs, land_refs, send, recv_ici, recv_sib, loc)
        for cp in remote + local:
            cp.start()
        outs[-1][...] = jnp.zeros((SUB, LANE), F32)

    res = pl.pallas_call(
        start, name=name + "_start",
        out_shape=([pltpu.SemaphoreType.DMA((4,))] * n + [pltpu.SemaphoreType.DMA((3,))] * n
                   + [pltpu.SemaphoreType.DMA(())] * (2 * n) + hbm_like(srcs) + hbm_like(lands) + [tok]),
        in_specs=[HBM] * (2 * n), out_specs=[SEM] * (4 * n) + [HBM] * (2 * n) + [vmem],
        input_output_aliases={i: 4 * n + i for i in range(2 * n)}, compiler_params=side,
    )(*srcs, *lands)
    send, recv_ici, recv_sib, loc = res[:n], res[n:2 * n], res[2 * n:3 * n], res[3 * n:4 * n]
    srcs, lands, token = res[4 * n:5 * n], res[5 * n:6 * n], res[-1]

    done = between(token)
    after = jax.tree_util.tree_leaves(done)

    def forward(*refs):
        land_refs, arrived = refs[:n], refs[n:2 * n]
        outs = refs[2 * n + len(after):]
        fsend, frecv = outs[:n], outs[n:2 * n]
        for cp in _gather2_forwards(land_refs, fsend, frecv, arrived):
            cp.start()
        outs[-1][...] = jnp.zeros((SUB, LANE), F32)

    res = pl.pallas_call(
        forward, name=name + "_forward",
        out_shape=[pltpu.SemaphoreType.DMA((3,))] * (2 * n) + hbm_like(lands) + [tok],
        in_specs=[HBM] * n + [SEM] * n + [ANY] * len(after), out_specs=[SEM] * (2 * n) + [HBM] * n + [vmem],
        input_output_aliases={i: 2 * n + i for i in range(n)}, compiler_params=side,
    )(*lands, *recv_ici, *after)
    fsend, frecv, lands, token = res[:n], res[n:2 * n], res[2 * n:3 * n], res[-1]

    def wait(*refs):
        src_refs, land_refs = refs[:n], refs[n:2 * n]
        sems = refs[2 * n:7 * n]
        send, recv_sib, loc, fsend, frecv = (sems[k * n:(k + 1) * n] for k in range(5))
        remote, local = _gather2_copies(src_refs, land_refs, send, send, recv_sib, loc)
        for w in range(n):
            for cp in remote[4 * w:4 * w + 4]:
                cp.wait_send()
            remote[4 * w].wait_recv()
        for cp in local:
            cp.wait()
        for cp in _gather2_forwards(land_refs, fsend, frecv):
            cp.wait_send()
            cp.wait_recv()

    res = pl.pallas_call(
        wait, name=name + "_wait", out_shape=hbm_like(srcs) + hbm_like(lands),
        in_specs=[HBM] * (2 * n) + [SEM] * (5 * n) + [ANY], out_specs=[HBM] * (2 * n),
        input_output_aliases={i: i for i in range(2 * n)}, compiler_params=side,
    )(*srcs, *lands, *send, *recv_sib, *loc, *fsend, *frecv, token)
    return res[n:], done


def _after(token, a):
    return a + token[0:1, 0:1].astype(a.dtype)


def _unblock(w3):
    nb, k, nbw = w3.shape
    return w3.transpose(1, 0, 2).reshape(k, nb * nbw)


def _block(w, nb):
    k, n = w.shape
    return w.reshape(k, nb, n // nb).transpose(1, 0, 2)


def kernel(x, positions, ln1_g, w_in, b_gate, conv_w, w_conv_out, q_a_g, w_q_b, kv_a_g, w_kv_b, q_norm_g, k_norm_g, w_mla_out, w_o, ln2_g, w_ffn_up, ffn_conv_w, ffn_conv_b, w_ffn_down, loss_target, m_ln1_g, m_w_in, m_b_gate, m_conv_w, m_w_conv_out, m_q_a_g, m_w_q_b, m_kv_a_g, m_w_kv_b, m_q_norm_g, m_k_norm_g, m_w_mla_out, m_w_o, m_ln2_g, m_w_ffn_up, m_ffn_conv_w, m_ffn_conv_b, m_w_ffn_down, v_ln1_g, v_w_in, v_b_gate, v_conv_w, v_w_conv_out, v_q_a_g, v_w_q_b, v_kv_a_g, v_w_kv_b, v_q_norm_g, v_k_norm_g, v_w_mla_out, v_w_o, v_ln2_g, v_w_ffn_up, v_ffn_conv_w, v_ffn_conv_b, v_w_ffn_down):
    s, d = x.shape[1], x.shape[2]
    conv = conv_w.shape[2] * N_DEV
    ql, kvl = q_a_g.shape[1], kv_a_g.shape[1]
    heads = w_q_b.shape[2] * N_DEV // HEAD_QK
    dff = w_ffn_down.shape[1] * N_DEV
    hw = heads * LANE
    conv3 = 3 * conv
    kr_off = conv3 + ql
    kv_off = -(-(kr_off + LANE) // kvl) * kvl
    wa = kv_off + kvl
    assert conv3 % ql == 0 and kr_off % LANE == 0
    xs = x[0]
    tgt = loss_target[0]
    pos = positions.reshape(s, 1)

    nin = w_in.shape[2]
    big = dict(w_in=w_in[0].T, w_conv_out=w_conv_out[0], w_q_b=w_q_b[0], w_kv_b=w_kv_b[0],
               w_mla_out=w_mla_out[0], w_o=w_o[0], w_ffn_up=w_ffn_up[0], w_ffn_down=w_ffn_down[0])
    names = list(big)
    rest = names[1:]
    early = {}

    def while_w_in_travels(token):
        early["ag"] = _exchange_start([big[k].astype(BF) for k in rest], True, "gather_rest_start", dep=token)
        cos_sin = _rope_tables(pos)
        return cos_sin, _rms_fwd(xs, _after(early["ag"].token, ln1_g), d, 0, "rms1_fwd")

    first, ((cos, sin), u1) = _gather2([big["w_in"].astype(BF), _pad8(conv_w[0]), _pad8(ffn_conv_w[0])],
                                       while_w_in_travels, "gather_w_in")
    ag = early["ag"]
    cw8 = _unblock(first[1])
    fcw8 = _unblock(first[2])

    def landed(keys, after, name):
        return _exchange_wait(ag, [rest.index(k) for k in keys], after, name)

    w_in_t = first[0].reshape(N_DEV * nin, d)
    g_off = kr_off + kvl + ROPE
    w_a_t = jnp.concatenate([w_in_t[:kr_off], _lay_rows(w_in_t[kr_off + kvl:g_off]),
                             jnp.zeros((kv_off - kr_off - LANE, d), BF), w_in_t[kr_off:kr_off + kvl]], axis=0)[None]
    w_g_t = w_in_t[g_off:][None]
    gains = _pad8(jnp.concatenate([q_norm_g[:, :NOPE], _lay(q_norm_g[:, NOPE:]),
                                   k_norm_g[:, :NOPE], _lay(k_norm_g[:, NOPE:])], axis=0))
    kr_blk = kr_off // LANE

    z_a = _mm_nt(u1, w_a_t, "mm_z_a")
    z_g = _mm_nt(u1, w_g_t, "mm_z_g", out_dtype=BF)
    p = _conv_mix_fwd(z_a, cw8, conv)
    w_co, w_qb, w_kv = landed(["w_conv_out", "w_q_b", "w_kv_b"], p, "gather_wait_mixers")
    w_co = _unblock(w_co)[None]
    w_kv = _unblock(w_kv)[None]
    wq_full = _unblock(w_qb).reshape(ql, heads, HEAD_QK)
    w_q = jnp.concatenate([wq_full[:, :, :NOPE].reshape(ql, hw), _lay(wq_full[:, :, NOPE:]).reshape(ql, hw)],
                          axis=1)[None]
    yc = _mm_nn(p, w_co, "mm_y_conv", out_dtype=BF)
    qn, q_raw = _rms_mm_nn(z_a, q_a_g, conv3 // ql, w_q, "mm_q")
    kvn, kv_raw = _rms_mm_nn(z_a, kv_a_g, kv_off // kvl, w_kv, "mm_kv")
    q_att, k_att, v_bf = _head_fwd(q_raw, kv_raw, z_a, kr_blk, cos, sin, gains, heads)
    o, o_bf, lse = _attn_fwd(q_att, k_att, v_bf, heads)
    w_mo, w_oo = landed(["w_mla_out", "w_o"], lse, "gather_wait_outs")
    w_mo = w_mo.reshape(1, hw, d)
    w_oo = w_oo.reshape(1, d, d)
    ym, mix = _mla_out_gate(o_bf, w_mo, z_g, b_gate, yc)
    h1, u2 = _residual_norm(mix, w_oo, xs, ln2_g)
    w_up, = landed(["w_ffn_up"], u2, "gather_wait_ffn_up")
    a_g, a_u, f = _ffn_up_act(u2, w_up, fcw8, ffn_conv_b, dff)
    w_dn, = landed(["w_ffn_down"], f, "gather_wait_ffn_down")
    w_dn = w_dn.reshape(1, dff, d)
    dy, dy_bf, loss_part = _mm_nn_loss(f, w_dn, h1, tgt, "mm_ffn_down_loss")

    g_dn = _mm_tn(f, dy_bf, 1, "mm_g_ffn_down").reshape(N_DEV, dff // N_DEV, d)
    rs_dn = _exchange_start([g_dn], False, "reduce_ffn_down_start")
    d_f = _mm_nt(dy_bf, w_dn, "mm_d_f", dep=rs_dn.token)
    d_xg, d_xu, dfw_g, dfw_u = _ffn_act_bwd(a_g, a_u, d_f, fcw8, ffn_conv_b, dff)
    half = N_DEV // 2
    g_up = _mm_tn(u2, d_xg, half, "mm_g_ffn_up_gate", into=lax.empty((N_DEV, d, 2 * dff // N_DEV), BF))
    g_up = _mm_tn(u2, d_xu, half, "mm_g_ffn_up_up", into=g_up, blk0=half)
    rs_up = _exchange_start([g_up], False, "reduce_ffn_up_start")
    d_u2 = _mm_nt([d_xg, d_xu], w_up, "mm_d_u2", out_dtype=BF, dep=rs_up.token)
    d_h1, d_h1_bf, dg_ln2 = _rms_bwd(h1, d_u2, ln2_g, d, 0, "rms2_bwd", extra=dy, also_bf16=True)
    g_oo = _mm_tn(mix, d_h1_bf, 1, "mm_g_w_o").reshape(N_DEV, d // N_DEV, d)
    d_zga, d_zgb, d_yc, d_ym, dba, dbb = _d_mix_gate(d_h1_bf, w_oo, z_g, b_gate, yc, ym)
    g_co = _block(_mm_tn(p, d_yc, 1, "mm_g_conv_out")[0], N_DEV)
    g_mo = _mm_tn(o_bf, d_ym, 1, "mm_g_mla_out").reshape(N_DEV, hw // N_DEV, d)
    rs_mix = _exchange_start([g_oo, g_co, g_mo], False, "reduce_mixers_start")
    d_p = _mm_nt(d_yc, w_co, "mm_d_p", dep=rs_mix.token)
    d_o = _mm_nt(d_ym, w_mo, "mm_d_o", out_dtype=BF)
    d_zb, d_zc, d_zv, dcw = _conv_mix_bwd(z_a, d_p, cw8, conv)
    dq_att, dk_att, dv = _attn_bwd(q_att, k_att, v_bf, o, lse, d_o, heads, dep=rs_mix.token)
    d_q_raw, d_kv_raw, d_kr, dgains = _head_bwd(q_raw, kv_raw, z_a, kr_blk, cos, sin, gains, dq_att, dk_att, dv, heads)
    g_q2 = _mm_tn(qn, d_q_raw, 1, "mm_g_q")[0]
    g_qb = _block(jnp.concatenate([g_q2[:, :hw].reshape(ql, heads, NOPE),
                                   _unlay(g_q2[:, hw:].reshape(ql, heads, LANE))], axis=2).reshape(ql, heads * HEAD_QK), N_DEV)
    g_kv = _block(_mm_tn(kvn, d_kv_raw, 1, "mm_g_kv")[0], N_DEV)
    rs_qkv = _exchange_start([g_qb, g_kv], False, "reduce_qkv_start")
    d_ql, dg_qa = _mm_nt_rms_bwd(d_q_raw, w_q, z_a, q_a_g, conv3 // ql, "mm_d_q_lat", dep=rs_qkv.token)
    d_kvl, dg_kva = _mm_nt_rms_bwd(d_kv_raw, w_kv, z_a, kv_a_g, kv_off // kvl, "mm_d_kv_lat")
    d_z_a = jnp.concatenate([d_zb, d_zc, d_zv, d_ql, d_kr.astype(BF), jnp.zeros((s, kv_off - kr_off - LANE), BF),
                             d_kvl], axis=1)
    g_a = _mm_tn(d_z_a, u1, 1, "mm_g_w_a")[0]
    g_ga = _mm_tn(d_zga, u1, 1, "mm_g_w_ga")[0]
    g_gb = _mm_tn(d_zgb, u1, 1, "mm_g_w_gb")[0]
    g_in = jnp.concatenate([g_a[:kr_off], g_a[kv_off:kv_off + kvl], g_a[kr_off:kr_off + HALF],
                            g_a[kr_off + 2 * HALF:kr_off + 3 * HALF], g_ga, g_gb], axis=0).reshape(N_DEV, nin, d)
    rs_in = _exchange_start([g_in], False, "reduce_w_in_start")
    d_u1 = _mm_nn(d_z_a, w_a_t, "mm_d_u1_a", dep=rs_in.token)
    d_u1 = _mm_nn([d_zga, d_zgb], w_g_t, "mm_d_u1_g", add=d_u1)
    grad_x, dg_ln1 = _rms_bwd(xs, d_u1, ln1_g, d, 0, "rms1_bwd", extra=d_h1)

    summed = {}
    summed["w_ffn_down"], = _exchange_wait(rs_dn, [0], grad_x, "reduce_ffn_down_wait")
    summed["w_ffn_up"], = _exchange_wait(rs_up, [0], grad_x, "reduce_ffn_up_wait")
    summed["w_o"], summed["w_conv_out"], summed["w_mla_out"] = _exchange_wait(rs_mix, [0, 1, 2], grad_x, "reduce_mixers_wait")
    summed["w_q_b"], summed["w_kv_b"] = _exchange_wait(rs_qkv, [0, 1], grad_x, "reduce_qkv_wait")
    loc = locals()
    out = {}
    for k in rest:
        out[k] = _adamw(summed[k], big[k], loc["m_" + k][0], loc["v_" + k][0], "adamw_" + k)

    small = dict(ln1_g=dg_ln1[0:1], b_gate=jnp.concatenate([dba[0:1], dbb[0:1]], axis=1), q_a_g=dg_qa[0:1],
                 kv_a_g=dg_kva[0:1],
                 q_norm_g=jnp.concatenate([dgains[0:1], _unlay(dgains[1:2])], axis=1),
                 k_norm_g=jnp.concatenate([dgains[2:3], _unlay(dgains[3:4])], axis=1),
                 ln2_g=dg_ln2[0:1], ffn_conv_b=jnp.concatenate([dfw_g[3:4], dfw_u[3:4]], axis=1))
    small_names = list(small)
    extra = [dcw[0:3].reshape(1, -1), jnp.concatenate([dfw_g[0:3], dfw_u[0:3]], axis=1).reshape(1, -1),
             loss_part[0:1, 0:1]]
    flat = jnp.concatenate([small[k] for k in small_names] + extra, axis=1)
    n_flat = flat.shape[1]
    rows = -(-n_flat // (SUB * LANE)) * SUB
    flat = jnp.pad(flat, ((0, 0), (0, rows * LANE - n_flat))).reshape(rows, LANE)
    total = _sum_parts(_all_gather([flat], "gather_small", dep=[out[k][0] for k in rest])[0], "sum_small").reshape(1, rows * LANE)
    off = 0
    small_g = {}
    for k in small_names:
        small_g[k] = total[:, off:off + small[k].shape[1]]
        off += small[k].shape[1]
    me = 4 * lax.axis_index("x") + 2 * lax.axis_index("y") + lax.axis_index("c")
    cwn, fcwn = conv // N_DEV, 2 * dff // N_DEV
    g_cw = lax.dynamic_slice_in_dim(total[:, off:off + 3 * conv].reshape(3, conv), me * cwn, cwn, axis=1)
    off += 3 * conv
    g_fcw = lax.dynamic_slice_in_dim(total[:, off:off + 6 * dff].reshape(3, 2 * dff), me * fcwn, fcwn, axis=1)
    off += 6 * dff
    loss = total[0, off]

    summed["w_in"], = _exchange_wait(rs_in, [0], total, "reduce_w_in_wait")
    out["w_in"] = [r.T for r in _adamw(summed["w_in"], big["w_in"], m_w_in[0].T, v_w_in[0].T, "adamw_w_in",
                                       by_cols=True)]
    small_w = dict(ln1_g=ln1_g, b_gate=b_gate, q_a_g=q_a_g, kv_a_g=kv_a_g, q_norm_g=q_norm_g, k_norm_g=k_norm_g,
                   ln2_g=ln2_g, ffn_conv_b=ffn_conv_b, conv_w=conv_w[0].reshape(1, -1),
                   ffn_conv_w=ffn_conv_w[0].reshape(1, -1))
    small_g["conv_w"] = g_cw.reshape(1, -1)
    small_g["ffn_conv_w"] = g_fcw.reshape(1, -1)
    packed_names = list(small_w)

    def pack(get):
        vflat = jnp.concatenate([get(k).reshape(1, -1) for k in packed_names], axis=1)
        nr = -(-vflat.shape[1] // (SUB * LANE)) * SUB
        return jnp.pad(vflat, ((0, 0), (0, nr * LANE - vflat.shape[1])), constant_values=1.0).reshape(nr, LANE)

    res = _adamw(pack(lambda k: small_g[k])[None], pack(lambda k: small_w[k]), pack(lambda k: loc["m_" + k]),
                 pack(lambda k: loc["v_" + k]), "adamw_small")
    res = [r.reshape(1, -1) for r in res]
    off = 0
    for k in packed_names:
        shape = loc[k].shape
        size = small_w[k].shape[1]
        out[k] = [r[:, off:off + size].reshape(shape) for r in res]
        off += size
    for k in names:
        out[k] = [r[None] for r in out[k]]

    order = ["ln1_g", "w_in", "b_gate", "conv_w", "w_conv_out", "q_a_g", "w_q_b", "kv_a_g", "w_kv_b", "q_norm_g",
             "k_norm_g", "w_mla_out", "w_o", "ln2_g", "w_ffn_up", "ffn_conv_w", "ffn_conv_b", "w_ffn_down"]
    return (loss, grad_x[None], *[out[k][0] for k in order], *[out[k][1] for k in order],
            *[out[k][2] for k in order], *[out[k][3] for k in order])
```

```python
import jax
import jax.numpy as jnp
from jax import lax
from jax.experimental import pallas as pl
from jax.experimental.pallas import tpu as pltpu

BF = jnp.bfloat16
F32 = jnp.float32
MESH = pl.DeviceIdType.MESH
N_DEV = 8

NOPE = 128
ROPE = 64
HALF = ROPE // 2
HEAD_QK = NOPE + ROPE
LANE = 128
SUB = 8
QK_SCALE = HEAD_QK ** -0.5
LOG2_E = 1.4426950408889634
NORM_EPS = 1e-6
NEG_INF = -1e30
ROPE_THETA = 10000.0
ADAM_LR = 0.001
ADAM_B1 = 0.9
ADAM_B2 = 0.999
ADAM_EPS = 1e-08
ADAM_WD = 0.01
ADAM_STEP = 10

VMEM_LIMIT = 52 * 1024 * 1024
MM_TM, MM_TN, MM_TK, MM_TS = 1024, 1536, 2048, 2048
ROW_TILE, ROW_TILE_BWD = 512, 256
HEAD_ROW_TILE, HEAD_ROW_TILE_BWD = 512, 256
COL_TILE = 512
FFN_COL_TILE = 1408
ATTN_TILE = 1024
ATTN_TILE_FWD = 1024
ANY = pl.BlockSpec(memory_space=pl.ANY)


def _pick(n, target, mult):
    t = (min(n, target) // mult) * mult
    while t > 0:
        if n % t == 0:
            return t
        t -= mult
    raise ValueError(f"no tile for {n} (target {target}, multiple {mult})")


def _cp(*sem):
    return pltpu.CompilerParams(dimension_semantics=sem, vmem_limit_bytes=VMEM_LIMIT)


def _accumulate(kk, nk, acc, part, finish):
    if nk == 1:
        finish(part())
        return

    @pl.when(kk == 0)
    def _():
        acc[...] = part()

    @pl.when((kk > 0) & (kk < nk - 1))
    def _():
        acc[...] += part()

    @pl.when(kk == nk - 1)
    def _():
        finish(acc[...] + part())


def _mm_call(body, name, grid, in_specs, args, out_spec, out_shape, acc_shape, nk, dep):
    if dep is not None:
        in_specs = in_specs + [ANY]
        args = args + [dep]
    return pl.pallas_call(
        body, name=name, grid=grid, in_specs=in_specs, out_specs=out_spec, out_shape=out_shape,
        scratch_shapes=[pltpu.VMEM(acc_shape, F32)] if nk > 1 else [],
        compiler_params=_cp("parallel", "parallel", "arbitrary"),
    )(*args)


def _mm_nn_loss(a, b3, add, target, name):
    m, k = a.shape
    _, k2, n = b3.shape
    assert k == k2 and b3.shape[0] == 1
    tm = _pick(m, MM_TM, 16)
    tn = _pick(n, MM_TN, LANE)
    tk = _pick(k, MM_TK, LANE)
    nk = k // tk

    def body(a_ref, b_ref, c_ref, t_ref, dy_ref, dyb_ref, l_ref, acc):
        kk = pl.program_id(2)

        @pl.when((pl.program_id(0) == 0) & (pl.program_id(1) == 0) & (kk == 0))
        def _():
            l_ref[...] = jnp.zeros_like(l_ref)

        def part():
            return jnp.dot(a_ref[...].astype(BF), b_ref[0].astype(BF), preferred_element_type=F32)

        def finish(r):
            e = r + c_ref[...] - t_ref[...]
            dy_ref[...] = e / n
            dyb_ref[...] = (e / n).astype(BF)
            l_ref[...] += 0.5 * jnp.sum(jnp.sum(e * e, axis=-1, keepdims=True), axis=0, keepdims=True) / n

        _accumulate(kk, nk, acc, part, finish)

    tile = pl.BlockSpec((tm, tn), lambda i, j, kk: (i, j))
    return pl.pallas_call(
        body, name=name, grid=(m // tm, n // tn, nk),
        in_specs=[pl.BlockSpec((tm, tk), lambda i, j, kk: (i, kk)),
                  pl.BlockSpec((1, tk, tn), lambda i, j, kk: (0, kk, j)), tile, tile],
        out_specs=[tile, tile, pl.BlockSpec((SUB, LANE), lambda i, j, kk: (0, 0))],
        out_shape=[jax.ShapeDtypeStruct((m, n), F32), jax.ShapeDtypeStruct((m, n), BF),
                   jax.ShapeDtypeStruct((SUB, LANE), F32)],
        scratch_shapes=[pltpu.VMEM((tm, tn), F32)],
        compiler_params=_cp("arbitrary", "arbitrary", "arbitrary"),
    )(a, b3, add, target)


def _mm_nn(a, b3, name, add=None, out_dtype=F32, blk0=0, nblk=None, dep=None):
    pair = isinstance(a, (list, tuple))
    a_list = list(a) if pair else [a]
    m, ka = a_list[0].shape
    k = ka * len(a_list)
    nb_all, k2, nbw = b3.shape
    assert k == k2
    nblk = nb_all - blk0 if nblk is None else nblk
    n = nblk * nbw
    tm = _pick(m, MM_TM if k > MM_TM else 2 * MM_TM, 16)
    tn = _pick(nbw, MM_TN, LANE)
    tk = _pick(ka, MM_TK, LANE)
    per = nbw // tn
    nk = k // tk
    nka = ka // tk
    na_ops = len(a_list)

    def body(*refs):
        a_refs, b_ref = refs[:na_ops], refs[na_ops]
        c_ref = refs[na_ops + 1] if add is not None else None
        o_ref = refs[na_ops + 1 + (add is not None) + (dep is not None)]
        acc = refs[-1]
        kk = pl.program_id(2)

        def part():
            av = a_refs[0][...] if not pair else jnp.where(kk < nka, a_refs[0][...], a_refs[1][...])
            return jnp.dot(av.astype(BF), b_ref[...].astype(BF), preferred_element_type=F32)

        def finish(r):
            if add is not None:
                r = r + c_ref[...]
            o_ref[...] = r.astype(out_dtype)

        _accumulate(kk, nk, acc, part, finish)

    if pair:
        in_specs = [pl.BlockSpec((tm, tk), lambda i, j, kk: (i, jnp.minimum(kk, nka - 1))),
                    pl.BlockSpec((tm, tk), lambda i, j, kk: (i, jnp.maximum(kk - nka, 0)))]
    else:
        in_specs = [pl.BlockSpec((tm, tk), lambda i, j, kk: (i, kk))]
    in_specs.append(pl.BlockSpec((None, tk, tn), lambda i, j, kk: (blk0 + j // per, kk, j % per)))
    args = a_list + [b3]
    if add is not None:
        in_specs.append(pl.BlockSpec((tm, tn), lambda i, j, kk: (i, j)))
        args.append(add)
    return _mm_call(body, name, (m // tm, n // tn, nk), in_specs, args,
                    pl.BlockSpec((tm, tn), lambda i, j, kk: (i, j)), jax.ShapeDtypeStruct((m, n), out_dtype),
                    (tm, tn), nk, dep)


def _mm_nt(a, b3, name, add=None, out_dtype=F32, blk0=0, nblk=None, dep=None):
    pair = isinstance(a, (list, tuple))
    a_list = list(a) if pair else [a]
    m, na = a_list[0].shape
    n = na * len(a_list)
    nb_all, k, nbw = b3.shape
    nblk = nb_all - blk0 if nblk is None else nblk
    assert n == nblk * nbw and na % nbw == 0
    tm = _pick(m, 2 * MM_TM if k <= MM_TM and n <= MM_TK else MM_TM, 16)
    tk = _pick(nbw, MM_TK, LANE)
    per = nbw // tk
    nk = n // tk
    tn = _pick(k, MM_TN if nk <= 2 else 2 * MM_TM, LANE)
    nka = na // tk
    na_ops = len(a_list)

    def body(*refs):
        a_refs, b_ref = refs[:na_ops], refs[na_ops]
        c_ref = refs[na_ops + 1] if add is not None else None
        o_ref = refs[na_ops + 1 + (add is not None) + (dep is not None)]
        acc = refs[-1]
        kk = pl.program_id(2)

        def part():
            av = a_refs[0][...] if not pair else jnp.where(kk < nka, a_refs[0][...], a_refs[1][...])
            return lax.dot_general(av.astype(BF), b_ref[...].astype(BF),
                                   (((1,), (1,)), ((), ())), preferred_element_type=F32)

        def finish(r):
            if add is not None:
                r = r + c_ref[...]
            o_ref[...] = r.astype(out_dtype)

        _accumulate(kk, nk, acc, part, finish)

    if pair:
        in_specs = [pl.BlockSpec((tm, tk), lambda i, j, kk: (i, jnp.minimum(kk, nka - 1))),
                    pl.BlockSpec((tm, tk), lambda i, j, kk: (i, jnp.maximum(kk - nka, 0)))]
    else:
        in_specs = [pl.BlockSpec((tm, tk), lambda i, j, kk: (i, kk))]
    in_specs.append(pl.BlockSpec((None, tn, tk), lambda i, j, kk: (blk0 + kk // per, j, kk % per)))
    args = a_list + [b3]
    if add is not None:
        in_specs.append(pl.BlockSpec((tm, tn), lambda i, j, kk: (i, j)))
        args.append(add)
    return _mm_call(body, name, (m // tm, k // tn, nk), in_specs, args,
                    pl.BlockSpec((tm, tn), lambda i, j, kk: (i, j)), jax.ShapeDtypeStruct((m, k), out_dtype),
                    (tm, tn), nk, dep)


def _rms_mm_nn(x, g, col_blk, b3, name):
    m = x.shape[0]
    _, k, n = b3.shape
    assert b3.shape[0] == 1
    tm = _pick(m, 2 * MM_TM, 16)
    tn = _pick(n, MM_TM, LANE)

    def body(x_ref, g_ref, b_ref, u_ref, o_ref):
        xv = x_ref[...]
        r = lax.rsqrt(jnp.mean(xv * xv, axis=-1, keepdims=True) + NORM_EPS)
        u = ((xv * r) * g_ref[...]).astype(BF)

        @pl.when(pl.program_id(1) == 0)
        def _():
            u_ref[...] = u

        o_ref[...] = jnp.dot(u, b_ref[0], preferred_element_type=F32).astype(BF)

    return pl.pallas_call(
        body, name=name, grid=(m // tm, n // tn),
        in_specs=[pl.BlockSpec((tm, k), lambda i, j: (i, col_blk)), pl.BlockSpec((1, k), lambda i, j: (0, 0)),
                  pl.BlockSpec((1, k, tn), lambda i, j: (0, 0, j))],
        out_specs=[pl.BlockSpec((tm, k), lambda i, j: (i, 0)), pl.BlockSpec((tm, tn), lambda i, j: (i, j))],
        out_shape=[jax.ShapeDtypeStruct((m, k), BF), jax.ShapeDtypeStruct((m, n), BF)],
        compiler_params=_cp("parallel", "arbitrary"),
    )(x, g, b3)


def _mm_nt_rms_bwd(a, b3, x, g, col_blk, name, dep=None):
    m, n = a.shape
    _, width, n2 = b3.shape
    assert n == n2 and b3.shape[0] == 1
    tm = _pick(m, MM_TM, 16)
    tk = _pick(n, MM_TK, LANE)
    nk = n // tk

    def body(*refs):
        a_ref, b_ref, x_ref, g_ref = refs[:4]
        dx_ref, dg_ref = refs[4 + (dep is not None):6 + (dep is not None)]
        acc = refs[-1]
        kk = pl.program_id(1)

        @pl.when((pl.program_id(0) == 0) & (kk == 0))
        def _():
            dg_ref[...] = jnp.zeros_like(dg_ref)

        def part():
            return lax.dot_general(a_ref[...], b_ref[0], (((1,), (1,)), ((), ())), preferred_element_type=F32)

        def finish(du):
            xv = x_ref[...]
            r = lax.rsqrt(jnp.mean(xv * xv, axis=-1, keepdims=True) + NORM_EPS)
            nv = xv * r
            dn = du * g_ref[...]
            dx_ref[...] = (r * (dn - nv * jnp.mean(dn * nv, axis=-1, keepdims=True))).astype(BF)
            dg_ref[...] += _rows8([jnp.sum(du * nv, axis=0, keepdims=True)], width)

        _accumulate(kk, nk, acc, part, finish)

    in_specs = [pl.BlockSpec((tm, tk), lambda i, kk: (i, kk)), pl.BlockSpec((1, width, tk), lambda i, kk: (0, 0, kk)),
                pl.BlockSpec((tm, width), lambda i, kk: (i, col_blk)), pl.BlockSpec((1, width), lambda i, kk: (0, 0))]
    args = [a, b3, x, g]
    if dep is not None:
        in_specs.append(ANY)
        args.append(dep)
    return pl.pallas_call(
        body, name=name, grid=(m // tm, nk), in_specs=in_specs,
        out_specs=[pl.BlockSpec((tm, width), lambda i, kk: (i, 0)), pl.BlockSpec((SUB, width), lambda i, kk: (0, 0))],
        out_shape=[jax.ShapeDtypeStruct((m, width), BF), jax.ShapeDtypeStruct((SUB, width), F32)],
        scratch_shapes=[pltpu.VMEM((tm, width), F32)],
        compiler_params=_cp("arbitrary", "arbitrary"),
    )(*args)


def _mm_tn(a, b, nblk, name, out_dtype=BF, dep=None, into=None, blk0=0):
    s, m = a.shape
    s2, n = b.shape
    assert s == s2 and n % nblk == 0 and (dep is None or into is None)
    nbw = n // nblk
    tm = _pick(m, MM_TN, LANE)
    tn = _pick(nbw, MM_TN, LANE)
    ts = _pick(s, MM_TS, LANE)
    per = nbw // tn
    ns = s // ts

    def body(*refs):
        a_ref, b_ref = refs[:2]
        o_ref = refs[2 + (dep is not None or into is not None)]
        acc = refs[-1]

        def part():
            return lax.dot_general(a_ref[...].astype(BF), b_ref[...].astype(BF),
                                   (((0,), (0,)), ((), ())), preferred_element_type=F32)

        def finish(r):
            o_ref[...] = r.astype(out_dtype)

        _accumulate(pl.program_id(2), ns, acc, part, finish)

    in_specs = [pl.BlockSpec((ts, tm), lambda i, j, ss: (ss, i)),
                pl.BlockSpec((ts, tn), lambda i, j, ss: (ss, j))]
    out_spec = pl.BlockSpec((None, tm, tn), lambda i, j, ss: (blk0 + j // per, i, j % per))
    if into is None:
        return _mm_call(body, name, (m // tm, n // tn, ns), in_specs, [a, b], out_spec,
                        jax.ShapeDtypeStruct((nblk, m, nbw), out_dtype), (tm, tn), ns, dep)
    assert into.shape[1:] == (m, nbw) and into.dtype == out_dtype
    return pl.pallas_call(
        body, name=name, grid=(m // tm, n // tn, ns), in_specs=in_specs + [ANY], out_specs=out_spec,
        out_shape=jax.ShapeDtypeStruct(into.shape, out_dtype), input_output_aliases={2: 0},
        scratch_shapes=[pltpu.VMEM((tm, tn), F32)] if ns > 1 else [],
        compiler_params=_cp("parallel", "parallel", "arbitrary"),
    )(a, b, into)


def _rows8(rows, width):
    idx = lax.broadcasted_iota(jnp.int32, (SUB, width), 0)
    out = jnp.zeros((SUB, width), F32)
    for r, v in enumerate(rows):
        out = jnp.where(idx == r, v, out)
    return out


def _rms_fwd(x, g, width, col_blk, name):
    s = x.shape[0]
    tr = _pick(s, ROW_TILE, 16)

    def body(x_ref, g_ref, u_ref):
        xv = x_ref[...]
        r = lax.rsqrt(jnp.mean(xv * xv, axis=-1, keepdims=True) + NORM_EPS)
        u_ref[...] = ((xv * r) * g_ref[...]).astype(BF)

    return pl.pallas_call(
        body, name=name, grid=(s // tr,),
        in_specs=[pl.BlockSpec((tr, width), lambda i: (i, col_blk)),
                  pl.BlockSpec((1, width), lambda i: (0, 0))],
        out_specs=pl.BlockSpec((tr, width), lambda i: (i, 0)),
        out_shape=jax.ShapeDtypeStruct((s, width), BF),
        compiler_params=_cp("parallel"),
    )(x, g)


def _rms_bwd(x, du, g, width, col_blk, name, extra=None, out_dtype=F32, also_bf16=False):
    s = x.shape[0]
    tr = _pick(s, ROW_TILE_BWD, 16)

    def body(*refs):
        x_ref, du_ref, g_ref = refs[:3]
        e_ref = refs[3] if extra is not None else None
        dx_ref = refs[3 + (extra is not None)]
        dxb_ref = refs[4 + (extra is not None)] if also_bf16 else None
        dg_ref = refs[-1]
        i = pl.program_id(0)
        xv = x_ref[...]
        duv = du_ref[...].astype(F32)
        r = lax.rsqrt(jnp.mean(xv * xv, axis=-1, keepdims=True) + NORM_EPS)
        nv = xv * r
        dn = duv * g_ref[...]
        dx = r * (dn - nv * jnp.mean(dn * nv, axis=-1, keepdims=True))
        if extra is not None:
            dx = dx + e_ref[...]
        dx_ref[...] = dx.astype(out_dtype)
        if also_bf16:
            dxb_ref[...] = dx.astype(BF)

        @pl.when(i == 0)
        def _():
            dg_ref[...] = jnp.zeros_like(dg_ref)

        dg_ref[...] += _rows8([jnp.sum(duv * nv, axis=0, keepdims=True)], width)

    in_specs = [pl.BlockSpec((tr, width), lambda i: (i, col_blk)),
                pl.BlockSpec((tr, width), lambda i: (i, 0)),
                pl.BlockSpec((1, width), lambda i: (0, 0))]
    args = [x, du, g]
    if extra is not None:
        in_specs.append(pl.BlockSpec((tr, width), lambda i: (i, 0)))
        args.append(extra)
    return pl.pallas_call(
        body, name=name, grid=(s // tr,),
        in_specs=in_specs,
        out_specs=[pl.BlockSpec((tr, width), lambda i: (i, 0))] * (1 + also_bf16)
        + [pl.BlockSpec((SUB, width), lambda i: (0, 0))],
        out_shape=[jax.ShapeDtypeStruct((s, width), out_dtype)] + [jax.ShapeDtypeStruct((s, width), BF)] * also_bf16
        + [jax.ShapeDtypeStruct((SUB, width), F32)],
        compiler_params=_cp("arbitrary"),
    )(*args)


def _down(cur, prev8, k):
    ext = jnp.concatenate([prev8, cur], axis=0)
    return pltpu.roll(ext, k, axis=0)[SUB:]


def _up(cur, next8, k):
    ext = jnp.concatenate([cur, next8], axis=0)
    return pltpu.roll(ext, ext.shape[0] - k, axis=0)[:cur.shape[0]]


def _lags(cur, prev8):
    return _down(cur, prev8, 1), _down(cur, prev8, 2)


def _conv3(w_ref, cur, prev8, lags=None):
    lag1, lag2 = _lags(cur, prev8) if lags is None else lags
    return w_ref[0:1, :] * lag2 + w_ref[1:2, :] * lag1 + w_ref[2:3, :] * cur


def _conv3_t(w_ref, cur, next8):
    return w_ref[2:3, :] * cur + w_ref[1:2, :] * _up(cur, next8, 1) + w_ref[0:1, :] * _up(cur, next8, 2)


def _spec_cur(tr, tc, c0):
    return pl.BlockSpec((tr, tc), lambda j, i: (i, c0 + j))


def _spec_prev(tr, tc, c0):
    return pl.BlockSpec((SUB, tc), lambda j, i: (jnp.maximum(i * (tr // SUB) - 1, 0), c0 + j))


def _spec_next(tr, tc, c0, s):
    return pl.BlockSpec((SUB, tc), lambda j, i: (jnp.minimum((i + 1) * (tr // SUB), s // SUB - 1), c0 + j))


def _spec_w(tc, c0):
    return pl.BlockSpec((SUB, tc), lambda j, i: (0, c0 + j))


def _pad8(w):
    return jnp.pad(w, ((0, SUB - w.shape[0]), (0, 0)))


def _conv_mix_fwd(z_a, cw8, conv):
    s = z_a.shape[0]
    tr = _pick(s, ROW_TILE, 16)
    tc = _pick(conv, COL_TILE, LANE)
    nc = conv // tc

    def body(zb_ref, zc_ref, zv_ref, zcp_ref, zvp_ref, w_ref, p_ref):
        i = pl.program_id(1)
        cv = zc_ref[...] * zv_ref[...]
        cvp = jnp.where(i > 0, zcp_ref[...] * zvp_ref[...], 0.0)
        p_ref[...] = (zb_ref[...] * _conv3(w_ref, cv, cvp)).astype(BF)

    return pl.pallas_call(
        body, name="conv_mix_fwd", grid=(nc, s // tr),
        in_specs=[_spec_cur(tr, tc, 0), _spec_cur(tr, tc, nc), _spec_cur(tr, tc, 2 * nc),
                  _spec_prev(tr, tc, nc), _spec_prev(tr, tc, 2 * nc), _spec_w(tc, 0)],
        out_specs=_spec_cur(tr, tc, 0),
        out_shape=jax.ShapeDtypeStruct((s, conv), BF),
        compiler_params=_cp("parallel", "parallel"),
    )(z_a, z_a, z_a, z_a, z_a, cw8)


def _conv_mix_bwd(z_a, d_p, cw8, conv):
    s = z_a.shape[0]
    tr = _pick(s, ROW_TILE_BWD, 16)
    tc = _pick(conv, COL_TILE, LANE)
    nc = conv // tc
    nr = s // tr

    def body(zb_ref, zbn_ref, zc_ref, zcp_ref, zv_ref, zvp_ref, dp_ref, dpn_ref, w_ref,
             dzb_ref, dzc_ref, dzv_ref, dw_ref):
        i = pl.program_id(1)
        zc = zc_ref[...]
        zv = zv_ref[...]
        cv = zc * zv
        cvp = jnp.where(i > 0, zcp_ref[...] * zvp_ref[...], 0.0)
        cv1, cv2 = _lags(cv, cvp)
        dpv = dp_ref[...]
        dzb_ref[...] = (dpv * _conv3(w_ref, cv, cvp, (cv1, cv2))).astype(BF)
        dcc = dpv * zb_ref[...]
        dccn = jnp.where(i < nr - 1, dpn_ref[...] * zbn_ref[...], 0.0)
        dcv = _conv3_t(w_ref, dcc, dccn)
        dzc_ref[...] = (dcv * zv).astype(BF)
        dzv_ref[...] = (dcv * zc).astype(BF)

        @pl.when(i == 0)
        def _():
            dw_ref[...] = jnp.zeros_like(dw_ref)

        dw_ref[...] += _rows8([jnp.sum(dcc * cv2, axis=0, keepdims=True),
                               jnp.sum(dcc * cv1, axis=0, keepdims=True),
                               jnp.sum(dcc * cv, axis=0, keepdims=True)], tc)

    out = jax.ShapeDtypeStruct((s, conv), BF)
    return pl.pallas_call(
        body, name="conv_mix_bwd", grid=(nc, nr),
        in_specs=[_spec_cur(tr, tc, 0), _spec_next(tr, tc, 0, s),
                  _spec_cur(tr, tc, nc), _spec_prev(tr, tc, nc),
                  _spec_cur(tr, tc, 2 * nc), _spec_prev(tr, tc, 2 * nc),
                  _spec_cur(tr, tc, 0), _spec_next(tr, tc, 0, s), _spec_w(tc, 0)],
        out_specs=[_spec_cur(tr, tc, 0), _spec_cur(tr, tc, 0), _spec_cur(tr, tc, 0), _spec_w(tc, 0)],
        out_shape=[out, out, out, jax.ShapeDtypeStruct((SUB, conv), F32)],
        compiler_params=_cp("parallel", "arbitrary"),
    )(z_a, z_a, z_a, z_a, z_a, z_a, d_p, d_p, cw8)


def _silu_parts(ag):
    sg = jax.nn.sigmoid(ag)
    return ag * sg, sg


def _ffn_up_act(u2, w_up, cw8, cb, dff):
    s, d = u2.shape
    nb, _, nbw = w_up.shape
    half = nb // 2
    assert half * nbw == dff
    tm = _pick(s, ROW_TILE, 16)

    def body(u_ref, wg_ref, wu_ref, cg_ref, cu_ref, bg_ref, bu_ref, ag_ref, au_ref, f_ref, hist_g, hist_u):
        i = pl.program_id(1)

        @pl.when(i == 0)
        def _():
            hist_g[...] = jnp.zeros_like(hist_g)
            hist_u[...] = jnp.zeros_like(hist_u)

        u = u_ref[...]
        xg = jnp.dot(u, wg_ref[...], preferred_element_type=F32)
        xu = jnp.dot(u, wu_ref[...], preferred_element_type=F32)
        ag_ref[...] = xg
        au_ref[...] = xu
        ag = _conv3(cg_ref, xg, hist_g[...]) + bg_ref[...]
        au = _conv3(cu_ref, xu, hist_u[...]) + bu_ref[...]
        f_ref[...] = (_silu_parts(ag)[0] * au).astype(BF)
        hist_g[...] = xg[tm - SUB:]
        hist_u[...] = xu[tm - SUB:]

    once = pl.Buffered(1)
    tile = pl.BlockSpec((tm, nbw), lambda j, i: (i, j))
    return pl.pallas_call(
        body, name="mm_ffn_up_act", grid=(half, s // tm),
        in_specs=[pl.BlockSpec((tm, d), lambda j, i: (i, 0)),
                  pl.BlockSpec((None, d, nbw), lambda j, i: (j, 0, 0), pipeline_mode=once),
                  pl.BlockSpec((None, d, nbw), lambda j, i: (half + j, 0, 0), pipeline_mode=once),
                  pl.BlockSpec((SUB, nbw), lambda j, i: (0, j)), pl.BlockSpec((SUB, nbw), lambda j, i: (0, half + j)),
                  pl.BlockSpec((1, nbw), lambda j, i: (0, j)), pl.BlockSpec((1, nbw), lambda j, i: (0, half + j))],
        out_specs=[tile, tile, tile],
        out_shape=[jax.ShapeDtypeStruct((s, dff), F32), jax.ShapeDtypeStruct((s, dff), F32),
                   jax.ShapeDtypeStruct((s, dff), BF)],
        scratch_shapes=[pltpu.VMEM((SUB, nbw), F32), pltpu.VMEM((SUB, nbw), F32)],
        compiler_params=_cp("arbitrary", "arbitrary"),
    )(u2, w_up, w_up, cw8, cw8, cb, cb)


def _ffn_act_bwd(a_g, a_u, d_f, cw8, cb, dff):
    s = a_g.shape[0]
    tr = _pick(s, ROW_TILE_BWD // 2, 16)
    tc = _pick(dff, FFN_COL_TILE, LANE)
    nc = dff // tc
    nr = s // tr

    def body(xg_ref, xgp_ref, xgn_ref, xu_ref, xup_ref, xun_ref, df_ref, dfn_ref,
             wg_ref, wu_ref, bg_ref, bu_ref, dxg_ref, dxu_ref, dwg_ref, dwu_ref):
        i = pl.program_id(1)
        xg = xg_ref[...]
        xu = xu_ref[...]
        xgp = jnp.where(i > 0, xgp_ref[...], 0.0)
        xup = jnp.where(i > 0, xup_ref[...], 0.0)

        def d_act(xg_t, xgp_t, xu_t, xup_t, df_t, lags_g=None, lags_u=None):
            ag = _conv3(wg_ref, xg_t, xgp_t, lags_g) + bg_ref[...]
            au = _conv3(wu_ref, xu_t, xup_t, lags_u) + bu_ref[...]
            sil, sg = _silu_parts(ag)
            return df_t * au * (sg * (1.0 + ag * (1.0 - sg))), df_t * sil

        lags_g = _lags(xg, xgp)
        lags_u = _lags(xu, xup)
        dag, dau = d_act(xg, xgp, xu, xup, df_ref[...], lags_g, lags_u)
        dfn = jnp.where(i < nr - 1, dfn_ref[...], 0.0)
        dagn, daun = d_act(xgn_ref[...], xg[tr - SUB:], xun_ref[...], xu[tr - SUB:], dfn)
        dxg_ref[...] = _conv3_t(wg_ref, dag, dagn).astype(BF)
        dxu_ref[...] = _conv3_t(wu_ref, dau, daun).astype(BF)

        @pl.when(i == 0)
        def _():
            dwg_ref[...] = jnp.zeros_like(dwg_ref)
            dwu_ref[...] = jnp.zeros_like(dwu_ref)

        def wgrad(da, x, lags):
            return _rows8([jnp.sum(da * lags[1], axis=0, keepdims=True),
                           jnp.sum(da * lags[0], axis=0, keepdims=True),
                           jnp.sum(da * x, axis=0, keepdims=True),
                           jnp.sum(da, axis=0, keepdims=True)], tc)

        dwg_ref[...] += wgrad(dag, xg, lags_g)
        dwu_ref[...] += wgrad(dau, xu, lags_u)

    half = jax.ShapeDtypeStruct((s, dff), BF)
    wsh = jax.ShapeDtypeStruct((SUB, dff), F32)
    return pl.pallas_call(
        body, name="ffn_act_bwd", grid=(nc, nr),
        in_specs=[_spec_cur(tr, tc, 0), _spec_prev(tr, tc, 0), _spec_next(tr, tc, 0, s),
                  _spec_cur(tr, tc, 0), _spec_prev(tr, tc, 0), _spec_next(tr, tc, 0, s),
                  _spec_cur(tr, tc, 0), _spec_next(tr, tc, 0, s),
                  _spec_w(tc, 0), _spec_w(tc, nc),
                  pl.BlockSpec((1, tc), lambda j, i: (0, j)), pl.BlockSpec((1, tc), lambda j, i: (0, nc + j))],
        out_specs=[_spec_cur(tr, tc, 0), _spec_cur(tr, tc, 0), _spec_w(tc, 0), _spec_w(tc, 0)],
        out_shape=[half, half, wsh, wsh],
        compiler_params=_cp("parallel", "arbitrary"),
    )(a_g, a_g, a_g, a_u, a_u, a_u, d_f, d_f, cw8, cw8, cb, cb)


def _residual_norm(mix, w_oo, x, g):
    m, k = mix.shape
    d = w_oo.shape[2]
    tm = _pick(m, ROW_TILE, 16)

    def body(a_ref, b_ref, x_ref, g_ref, h_ref, u_ref):
        h = jnp.dot(a_ref[...], b_ref[0], preferred_element_type=F32) + x_ref[...]
        h_ref[...] = h
        r = lax.rsqrt(jnp.mean(h * h, axis=-1, keepdims=True) + NORM_EPS)
        u_ref[...] = ((h * r) * g_ref[...]).astype(BF)

    row = pl.BlockSpec((tm, d), lambda i: (i, 0))
    return pl.pallas_call(
        body, name="mm_h1_norm", grid=(m // tm,),
        in_specs=[pl.BlockSpec((tm, k), lambda i: (i, 0)),
                  pl.BlockSpec((1, k, d), lambda i: (0, 0, 0), pipeline_mode=pl.Buffered(1)),
                  row, pl.BlockSpec((1, d), lambda i: (0, 0))],
        out_specs=[row, row],
        out_shape=[jax.ShapeDtypeStruct((m, d), F32), jax.ShapeDtypeStruct((m, d), BF)],
        compiler_params=_cp("parallel"),
    )(mix, w_oo, x, g)


def _mla_out_gate(o, w_mo, z_g, b_gate, yc):
    m, k = o.shape
    d = w_mo.shape[2]
    tm = _pick(m, MM_TM, 16)
    tn = _pick(d, MM_TM, LANE)
    nc = d // tn

    def body(a_ref, b_ref, za_ref, zb_ref, ba_ref, bb_ref, yc_ref, ym_ref, mix_ref):
        ym = jnp.dot(a_ref[...], b_ref[0], preferred_element_type=F32)
        ga = jax.nn.sigmoid(za_ref[...] + ba_ref[...])
        gb = jax.nn.sigmoid(zb_ref[...] + bb_ref[...])
        ym_ref[...] = ym.astype(BF)
        mix_ref[...] = (ga * yc_ref[...] + gb * ym).astype(BF)

    tile = pl.BlockSpec((tm, tn), lambda i, j: (i, j))
    out = jax.ShapeDtypeStruct((m, d), BF)
    return pl.pallas_call(
        body, name="mm_y_mla_gate", grid=(m // tm, nc),
        in_specs=[pl.BlockSpec((tm, k), lambda i, j: (i, 0)), pl.BlockSpec((1, k, tn), lambda i, j: (0, 0, j)),
                  tile, pl.BlockSpec((tm, tn), lambda i, j: (i, nc + j)),
                  pl.BlockSpec((1, tn), lambda i, j: (0, j)), pl.BlockSpec((1, tn), lambda i, j: (0, nc + j)), tile],
        out_specs=[tile, tile], out_shape=[out, out],
        compiler_params=_cp("parallel", "parallel"),
    )(o, w_mo, z_g, z_g, b_gate, b_gate, yc)


def _d_mix_gate(d_h1, w_oo, z_g, b_gate, yc, ym):
    m, n = d_h1.shape
    d = w_oo.shape[1]
    tm = _pick(m, ROW_TILE, 16)
    tn = _pick(d, MM_TM, LANE)
    nc = d // tn

    def body(a_ref, b_ref, za_ref, zb_ref, ba_ref, bb_ref, yc_ref, ym_ref,
             dza_ref, dzb_ref, dyc_ref, dym_ref, dba_ref, dbb_ref):
        i = pl.program_id(1)
        dm = lax.dot_general(a_ref[...], b_ref[0], (((1,), (1,)), ((), ())), preferred_element_type=F32)
        ga = jax.nn.sigmoid(za_ref[...] + ba_ref[...])
        gb = jax.nn.sigmoid(zb_ref[...] + bb_ref[...])
        dza = dm * yc_ref[...] * (ga * (1.0 - ga))
        dzb = dm * ym_ref[...] * (gb * (1.0 - gb))
        dza_ref[...] = dza.astype(BF)
        dzb_ref[...] = dzb.astype(BF)
        dyc_ref[...] = (dm * ga).astype(BF)
        dym_ref[...] = (dm * gb).astype(BF)

        @pl.when(i == 0)
        def _():
            dba_ref[...] = jnp.zeros_like(dba_ref)
            dbb_ref[...] = jnp.zeros_like(dbb_ref)

        dba_ref[...] += _rows8([jnp.sum(dza, axis=0, keepdims=True)], tn)
        dbb_ref[...] += _rows8([jnp.sum(dzb, axis=0, keepdims=True)], tn)

    tile = pl.BlockSpec((tm, tn), lambda j, i: (i, j))
    act = jax.ShapeDtypeStruct((m, d), BF)
    bsh = jax.ShapeDtypeStruct((SUB, d), F32)
    return pl.pallas_call(
        body, name="mm_d_mix_gate", grid=(nc, m // tm),
        in_specs=[pl.BlockSpec((tm, n), lambda j, i: (i, 0)), pl.BlockSpec((1, tn, n), lambda j, i: (0, j, 0)),
                  tile, pl.BlockSpec((tm, tn), lambda j, i: (i, nc + j)),
                  pl.BlockSpec((1, tn), lambda j, i: (0, j)), pl.BlockSpec((1, tn), lambda j, i: (0, nc + j)),
                  tile, tile],
        out_specs=[tile] * 4 + [pl.BlockSpec((SUB, tn), lambda j, i: (0, j))] * 2,
        out_shape=[act, act, act, act, bsh, bsh],
        compiler_params=_cp("parallel", "arbitrary"),
    )(d_h1, w_oo, z_g, z_g, b_gate, b_gate, yc, ym)


def _lay(v):
    z = jnp.zeros(v.shape[:-1] + (HALF,), v.dtype)
    return jnp.concatenate([v[..., :HALF], z, v[..., HALF:], z], axis=-1)


def _unlay(v):
    return jnp.concatenate([v[..., :HALF], v[..., 2 * HALF:3 * HALF]], axis=-1)


def _lay_rows(v):
    z = jnp.zeros((HALF,) + v.shape[1:], v.dtype)
    return jnp.concatenate([v[:HALF], z, v[HALF:], z], axis=0)


def _rope_tables(positions):
    s = positions.shape[0]
    tr = _pick(s, ROW_TILE, 8)
    inv_freq = ROPE_THETA ** (-jnp.arange(0, ROPE, 2, dtype=F32) / ROPE)
    consts = jnp.stack([_lay(jnp.concatenate([inv_freq, inv_freq])),
                        _lay(jnp.ones((ROPE,), F32)),
                        _lay(jnp.concatenate([-jnp.ones((HALF,), F32), jnp.ones((HALF,), F32)]))])
    consts = _pad8(consts)

    def body(p_ref, c_ref, cos_ref, sin_ref):
        ang = p_ref[...].astype(F32) * c_ref[0:1, :]
        cos_ref[...] = jnp.cos(ang) * c_ref[1:2, :]
        sin_ref[...] = jnp.sin(ang) * c_ref[2:3, :]

    tab = jax.ShapeDtypeStruct((s, LANE), F32)
    return pl.pallas_call(
        body, name="rope_tables", grid=(s // tr,),
        in_specs=[pl.BlockSpec((tr, 1), lambda i: (i, 0)), pl.BlockSpec((SUB, LANE), lambda i: (0, 0))],
        out_specs=[pl.BlockSpec((tr, LANE), lambda i: (i, 0))] * 2,
        out_shape=[tab, tab],
        compiler_params=_cp("parallel"),
    )(positions, consts)


def _lane_sum(p):
    return jnp.sum(p, axis=-1, keepdims=True)


def _rope(t, cos, sin):
    return t * cos + pltpu.roll(t, 2 * HALF, axis=1) * sin


def _rope_t(d, cos, sin):
    return d * cos + pltpu.roll(d * sin, 2 * HALF, axis=1)


def _head_fwd(q_raw, kv_raw, z_a, kr_blk, cos, sin, gains, heads):
    s = q_raw.shape[0]
    tr = _pick(s, HEAD_ROW_TILE, 16)
    hw = heads * LANE

    def body(q_ref, kv_ref, kr_ref, cos_ref, sin_ref, g_ref, qo_ref, ko_ref, vo_ref):
        cosv = cos_ref[...]
        sinv = sin_ref[...]
        krv = kr_ref[...]
        kr_sq = krv * krv
        for h in range(heads):
            lo = h * LANE
            qn = q_ref[:, lo:lo + LANE].astype(F32)
            qr = q_ref[:, hw + lo:hw + lo + LANE].astype(F32)
            r = lax.rsqrt(_lane_sum(qn * qn + qr * qr) / HEAD_QK + NORM_EPS)
            qo_ref[:, 2 * lo:2 * lo + LANE] = (((qn * r) * g_ref[0:1, :]) * (QK_SCALE * LOG2_E)).astype(BF)
            qo_ref[:, 2 * lo + LANE:2 * lo + 2 * LANE] = (
                _rope((qr * r) * g_ref[1:2, :], cosv, sinv) * (QK_SCALE * LOG2_E)).astype(BF)
            kn = kv_ref[:, 2 * lo:2 * lo + LANE].astype(F32)
            r = lax.rsqrt(_lane_sum(kn * kn + kr_sq) / HEAD_QK + NORM_EPS)
            ko_ref[:, 2 * lo:2 * lo + LANE] = ((kn * r) * g_ref[2:3, :]).astype(BF)
            ko_ref[:, 2 * lo + LANE:2 * lo + 2 * LANE] = _rope((krv * r) * g_ref[3:4, :], cosv, sinv).astype(BF)
            vo_ref[:, lo:lo + LANE] = kv_ref[:, 2 * lo + LANE:2 * lo + 2 * LANE].astype(BF)

    row = lambda w: pl.BlockSpec((tr, w), lambda i: (i, 0))
    return pl.pallas_call(
        body, name="head_fwd", grid=(s // tr,),
        in_specs=[row(2 * hw), row(2 * hw), pl.BlockSpec((tr, LANE), lambda i: (i, kr_blk)),
                  row(LANE), row(LANE), pl.BlockSpec((SUB, LANE), lambda i: (0, 0))],
        out_specs=[row(2 * hw), row(2 * hw), row(hw)],
        out_shape=[jax.ShapeDtypeStruct((s, 2 * hw), BF), jax.ShapeDtypeStruct((s, 2 * hw), BF),
                   jax.ShapeDtypeStruct((s, hw), BF)],
        compiler_params=_cp("parallel"),
    )(q_raw, kv_raw, z_a, cos, sin, gains)


def _head_bwd(q_raw, kv_raw, z_a, kr_blk, cos, sin, gains, dq_att, dk_att, dv, heads):
    s = q_raw.shape[0]
    tr = _pick(s, HEAD_ROW_TILE_BWD, 16)
    hw = heads * LANE

    def body(q_ref, kv_ref, kr_ref, cos_ref, sin_ref, g_ref, dq_ref, dk_ref, dv_ref,
             dqr_ref, dkv_ref, dkr_ref, dg_ref):
        i = pl.program_id(0)
        cosv = cos_ref[...]
        sinv = sin_ref[...]
        krv = kr_ref[...]
        kr_sq = krv * krv
        dkr = jnp.zeros((tr, LANE), F32)
        dgs = [jnp.zeros((1, LANE), F32) for _ in range(4)]

        def norm_bwd(xn, xr, sq, dn_out, dr_out, gn, gr):
            r = lax.rsqrt(_lane_sum(sq) / HEAD_QK + NORM_EPS)
            nn = xn * r
            nr = xr * r
            dt = _rope_t(dr_out, cosv, sinv)
            dnn = dn_out * gn
            dnr = dt * gr
            mean = _lane_sum(dnn * nn + dnr * nr) / HEAD_QK
            return (r * (dnn - nn * mean), r * (dnr - nr * mean),
                    jnp.sum(dn_out * nn, axis=0, keepdims=True), jnp.sum(dt * nr, axis=0, keepdims=True))

        for h in range(heads):
            lo = h * LANE
            qn = q_ref[:, lo:lo + LANE].astype(F32)
            qr = q_ref[:, hw + lo:hw + lo + LANE].astype(F32)
            dxn, dxr, g0, g1 = norm_bwd(qn, qr, qn * qn + qr * qr, dq_ref[:, 2 * lo:2 * lo + LANE] * QK_SCALE,
                                        dq_ref[:, 2 * lo + LANE:2 * lo + 2 * LANE] * QK_SCALE,
                                        g_ref[0:1, :], g_ref[1:2, :])
            dqr_ref[:, lo:lo + LANE] = dxn.astype(BF)
            dqr_ref[:, hw + lo:hw + lo + LANE] = dxr.astype(BF)
            kn = kv_ref[:, 2 * lo:2 * lo + LANE].astype(F32)
            dxn, dxr, g2, g3 = norm_bwd(kn, krv, kn * kn + kr_sq, dk_ref[:, 2 * lo:2 * lo + LANE],
                                        dk_ref[:, 2 * lo + LANE:2 * lo + 2 * LANE], g_ref[2:3, :], g_ref[3:4, :])
            dkv_ref[:, 2 * lo:2 * lo + LANE] = dxn.astype(BF)
            dkv_ref[:, 2 * lo + LANE:2 * lo + 2 * LANE] = dv_ref[:, lo:lo + LANE].astype(BF)
            dkr = dkr + dxr
            dgs = [a + b for a, b in zip(dgs, (g0, g1, g2, g3))]
        dkr_ref[...] = dkr

        @pl.when(i == 0)
        def _():
            dg_ref[...] = jnp.zeros_like(dg_ref)

        dg_ref[...] += _rows8(dgs, LANE)

    row = lambda w: pl.BlockSpec((tr, w), lambda i: (i, 0))
    return pl.pallas_call(
        body, name="head_bwd", grid=(s // tr,),
        in_specs=[row(2 * hw), row(2 * hw), pl.BlockSpec((tr, LANE), lambda i: (i, kr_blk)),
                  row(LANE), row(LANE), pl.BlockSpec((SUB, LANE), lambda i: (0, 0)),
                  row(2 * hw), row(2 * hw), row(hw)],
        out_specs=[row(2 * hw), row(2 * hw), row(LANE), pl.BlockSpec((SUB, LANE), lambda i: (0, 0))],
        out_shape=[jax.ShapeDtypeStruct((s, 2 * hw), BF), jax.ShapeDtypeStruct((s, 2 * hw), BF),
                   jax.ShapeDtypeStruct((s, LANE), F32), jax.ShapeDtypeStruct((SUB, LANE), F32)],
        compiler_params=_cp("arbitrary"),
    )(q_raw, kv_raw, z_a, cos, sin, gains, dq_att, dk_att, dv)


def _causal_mask(nrows, ncols, row0):
    rows = lax.broadcasted_iota(jnp.int32, (nrows, ncols), 0) + row0
    cols = lax.broadcasted_iota(jnp.int32, (nrows, ncols), 1)
    return cols <= rows


def _causal_steps(nt, q_major):
    pairs = ([(i, j) for i in range(nt) for j in range(i + 1)] if q_major
             else [(i, j) for j in range(nt) for i in range(j, nt)])
    return (jnp.array([p[0] for p in pairs], jnp.int32), jnp.array([p[1] for p in pairs], jnp.int32))


def _attn_fwd(q_att, k_att, v, heads):
    s = q_att.shape[0]
    t = _pick(s, ATTN_TILE_FWD, LANE)
    nt = s // t
    th = t // 2
    qi, kj = _causal_steps(nt, True)

    def body(qi_ref, kj_ref, q_ref, k_ref, v_ref, o_ref, ob_ref, lse_ref, m_s, l_s, acc_s):
        st = pl.program_id(1)
        i = qi_ref[st]
        j = kj_ref[st]

        @pl.when(j == 0)
        def _():
            m_s[...] = jnp.full_like(m_s, NEG_INF)
            l_s[...] = jnp.zeros_like(l_s)
            acc_s[...] = jnp.zeros_like(acc_s)

        def update(rows, ncol, masked):
            sc = lax.dot_general(q_ref[rows, :], k_ref[0:ncol, :], (((1,), (1,)), ((), ())),
                                 preferred_element_type=F32)
            if masked:
                sc = jnp.where(_causal_mask(rows.stop - rows.start, ncol, rows.start), sc, NEG_INF)
            m_prev = m_s[rows, :]
            m_new = jnp.maximum(m_prev, jnp.max(sc, axis=-1, keepdims=True))
            alpha = jnp.exp2(m_prev - m_new)
            p = jnp.exp2(sc - jnp.tile(m_new, (1, ncol // LANE)))
            l_s[rows, :] = alpha * l_s[rows, :] + jnp.sum(p, axis=-1, keepdims=True)
            acc_s[rows, :] = alpha * acc_s[rows, :] + jnp.dot(p.astype(BF), v_ref[0:ncol, :],
                                                              preferred_element_type=F32)
            m_s[rows, :] = m_new

        @pl.when(j < i)
        def _():
            update(slice(0, t), t, False)

        @pl.when(j == i)
        def _():
            update(slice(0, th), th, True)
            update(slice(th, t), t, True)
            o = acc_s[...] / l_s[...]
            o_ref[...] = o
            ob_ref[...] = o.astype(BF)
            lse_ref[...] = (m_s[...] + jnp.log2(l_s[...]))[:, 0:1]

    q_idx = lambda h, st, qi_r, kj_r: (qi_r[st], h)
    kv_idx = lambda h, st, qi_r, kj_r: (kj_r[st], h)
    return pl.pallas_call(
        body, name="attn_fwd",
        grid_spec=pltpu.PrefetchScalarGridSpec(
            num_scalar_prefetch=2, grid=(heads, qi.shape[0]),
            in_specs=[pl.BlockSpec((t, 2 * LANE), q_idx), pl.BlockSpec((t, 2 * LANE), kv_idx),
                      pl.BlockSpec((t, LANE), kv_idx)],
            out_specs=[pl.BlockSpec((t, LANE), q_idx), pl.BlockSpec((t, LANE), q_idx),
                       pl.BlockSpec((None, t, 1), lambda h, st, qi_r, kj_r: (h, qi_r[st], 0))],
            scratch_shapes=[pltpu.VMEM((t, LANE), F32), pltpu.VMEM((t, LANE), F32), pltpu.VMEM((t, LANE), F32)]),
        out_shape=[jax.ShapeDtypeStruct((s, heads * LANE), F32), jax.ShapeDtypeStruct((s, heads * LANE), BF),
                   jax.ShapeDtypeStruct((heads, s, 1), F32)],
        compiler_params=_cp("parallel", "arbitrary"),
    )(qi, kj, q_att, k_att, v)


def _attn_bwd(q_att, k_att, v, o, lse, d_o, heads, dep=None):
    s = q_att.shape[0]
    t = _pick(s, ATTN_TILE, LANE)
    nt = s // t
    th = t // 2
    qi, kj = _causal_steps(nt, False)

    def body(qi_ref, kj_ref, q_ref, k_ref, v_ref, do_ref, o_ref, lse_ref, *rest):
        dq_ref, dk_ref, dv_ref, dk_s, dv_s = rest[-5:]
        st = pl.program_id(1)
        i = qi_ref[st]
        j = kj_ref[st]

        @pl.when(st == 0)
        def _():
            dq_ref[...] = jnp.zeros_like(dq_ref)

        @pl.when(i == j)
        def _():
            dk_s[...] = jnp.zeros_like(dk_s)
            dv_s[...] = jnp.zeros_like(dv_s)

        def update(rows, ncol, masked):
            nrow = rows.stop - rows.start
            q = q_ref[rows, :]
            k = k_ref[0:ncol, :]
            do = do_ref[rows, :]
            sc = lax.dot_general(q, k, (((1,), (1,)), ((), ())), preferred_element_type=F32)
            if masked:
                sc = jnp.where(_causal_mask(nrow, ncol, rows.start), sc, NEG_INF)
            p = jnp.exp2(sc - lse_ref[rows, :])
            dp = lax.dot_general(do, v_ref[0:ncol, :], (((1,), (1,)), ((), ())), preferred_element_type=F32)
            delta = jnp.sum(do.astype(F32) * o_ref[rows, :], axis=-1, keepdims=True)
            ds = (p * (dp - delta)).astype(BF)
            dv_s[0:ncol, :] += lax.dot_general(p.astype(BF), do, (((0,), (0,)), ((), ())),
                                               preferred_element_type=F32)
            dk_s[0:ncol, :] += lax.dot_general(ds, q, (((0,), (0,)), ((), ())), preferred_element_type=F32)
            out_rows = pl.ds(pl.multiple_of(i * t + rows.start, nrow), nrow)
            dq_ref[out_rows, :] += jnp.dot(ds, k, preferred_element_type=F32)

        @pl.when(i > j)
        def _():
            update(slice(0, t), t, False)

        @pl.when(i == j)
        def _():
            update(slice(0, th), th, True)
            update(slice(th, t), t, True)

        @pl.when(i == nt - 1)
        def _():
            dk_ref[...] = (dk_s[...] * (1.0 / LOG2_E)).astype(BF)
            dv_ref[...] = dv_s[...].astype(BF)

    q_idx = lambda h, st, qi_r, kj_r: (qi_r[st], h)
    kv_idx = lambda h, st, qi_r, kj_r: (kj_r[st], h)
    in_specs = [pl.BlockSpec((t, 2 * LANE), q_idx), pl.BlockSpec((t, 2 * LANE), kv_idx),
                pl.BlockSpec((t, LANE), kv_idx), pl.BlockSpec((t, LANE), q_idx), pl.BlockSpec((t, LANE), q_idx),
                pl.BlockSpec((None, t, 1), lambda h, st, qi_r, kj_r: (h, qi_r[st], 0))]
    args = [q_att, k_att, v, d_o, o, lse]
    if dep is not None:
        in_specs.append(ANY)
        args.append(dep)
    return pl.pallas_call(
        body, name="attn_bwd",
        grid_spec=pltpu.PrefetchScalarGridSpec(
            num_scalar_prefetch=2, grid=(heads, qi.shape[0]),
            in_specs=in_specs,
            out_specs=[pl.BlockSpec((s, 2 * LANE), lambda h, st, qi_r, kj_r: (0, h)),
                       pl.BlockSpec((t, 2 * LANE), kv_idx), pl.BlockSpec((t, LANE), kv_idx)],
            scratch_shapes=[pltpu.VMEM((t, 2 * LANE), F32), pltpu.VMEM((t, LANE), F32)]),
        out_shape=[jax.ShapeDtypeStruct((s, heads * 2 * LANE), F32),
                   jax.ShapeDtypeStruct((s, heads * 2 * LANE), BF),
                   jax.ShapeDtypeStruct((s, heads * LANE), BF)],
        compiler_params=_cp("parallel", "arbitrary"),
    )(qi, kj, *args)


def _sum_parts(parts, name):
    n, r, c = parts.shape
    tr = _pick(r, 512, 8)

    def body(p_ref, o_ref):
        g = p_ref[0].astype(F32)
        for k in range(1, n):
            g = g + p_ref[k].astype(F32)
        o_ref[...] = g

    return pl.pallas_call(
        body, name=name, grid=(r // tr,),
        in_specs=[pl.BlockSpec((n, tr, c), lambda i: (0, i, 0))],
        out_specs=pl.BlockSpec((tr, c), lambda i: (i, 0)),
        out_shape=jax.ShapeDtypeStruct((r, c), F32),
        compiler_params=_cp("parallel"),
    )(parts)


def _adamw(parts, w, m, v, name, by_cols=False):
    n, rp, c = parts.shape
    r = w.shape[0]
    assert by_cols or rp == r
    tr, tc = (r, _pick(c, 256, LANE)) if by_cols else (_pick(r, 256, 16 if r % 16 == 0 else 8), c)

    def body(p_ref, w_ref, m_ref, v_ref, g_ref, d_ref, mo_ref, vo_ref):
        g = p_ref[0].astype(F32)
        for k in range(1, n):
            g = g + p_ref[k].astype(F32)
        g = g[:r] if by_cols else g
        m_new = ADAM_B1 * m_ref[...] + (1.0 - ADAM_B1) * g
        v_new = ADAM_B2 * v_ref[...] + (1.0 - ADAM_B2) * jnp.square(g)
        m_hat = m_new / (1.0 - ADAM_B1 ** ADAM_STEP)
        v_hat = v_new / (1.0 - ADAM_B2 ** ADAM_STEP)
        g_ref[...] = g
        d_ref[...] = -ADAM_LR * (m_hat / (jnp.sqrt(v_hat) + ADAM_EPS) + ADAM_WD * w_ref[...])
        mo_ref[...] = m_new
        vo_ref[...] = v_new

    idx = (lambda i: (0, i)) if by_cols else (lambda i: (i, 0))
    spec = pl.BlockSpec((tr, tc), idx)
    sh = jax.ShapeDtypeStruct((r, c), F32)
    return pl.pallas_call(
        body, name=name, grid=(c // tc if by_cols else r // tr,),
        in_specs=[pl.BlockSpec((n, rp if by_cols else tr, tc), lambda i: (0,) + idx(i)), spec, spec, spec],
        out_specs=[spec] * 4, out_shape=[sh] * 4,
        compiler_params=_cp("parallel"),
    )(parts, w, m, v)


def _place():
    x, y, c = lax.axis_index("x"), lax.axis_index("y"), lax.axis_index("c")
    chips = [(1 - x, y), (x, 1 - y), (1 - x, 1 - y)]
    return x, y, c, chips


def _all_gather(shards, name, dep=None):
    n = len(shards)
    deps = [] if dep is None else list(dep)

    def body(*refs):
        ins, outs = refs[:n], refs[n + len(deps):2 * n + len(deps)]
        send_sems, recv_sems, local_sems = refs[2 * n + len(deps):]
        x, y, c, chips = _place()
        me, sibling = (x, y, c), (x, y, 1 - c)

        def slot(w, p):
            return outs[w].at[4 * p[0] + 2 * p[1] + p[2]]

        def copy(w, k, block, to, src=None):
            return pltpu.make_async_remote_copy(
                src_ref=slot(w, block) if src is None else src, dst_ref=slot(w, block),
                send_sem=send_sems.at[w, k], recv_sem=recv_sems.at[w, k], device_id=to, device_id_type=MESH)

        first = []
        for w in range(n):
            first += [copy(w, 1 + j, me, (*chip, c), src=ins[w]) for j, chip in enumerate(chips)]
            first.append(copy(w, 0, me, sibling, src=ins[w]))
        for cp in first:
            cp.start()
        mine = [pltpu.make_async_copy(ins[w], slot(w, me), local_sems.at[w]) for w in range(n)]
        for cp in mine:
            cp.start()
        passed = []
        for w in range(n):
            for j, chip in enumerate(chips):
                copy(w, 1 + j, (*chip, c), me).wait_recv()
                cp = copy(w, 4 + j, (*chip, c), sibling)
                cp.start()
                passed.append(cp)
        for w in range(n):
            copy(w, 0, sibling, me).wait_recv()
            for j, chip in enumerate(chips):
                copy(w, 4 + j, (*chip, 1 - c), me).wait_recv()
        for cp in first + passed:
            cp.wait_send()
        for cp in mine:
            cp.wait()

    return pl.pallas_call(
        body, name=name,
        in_specs=[ANY] * (n + len(deps)), out_specs=[ANY] * n,
        out_shape=[jax.ShapeDtypeStruct((N_DEV,) + a.shape, a.dtype) for a in shards],
        scratch_shapes=[pltpu.SemaphoreType.DMA((n, 7)), pltpu.SemaphoreType.DMA((n, 7)),
                        pltpu.SemaphoreType.DMA((n,))],
    )(*shards, *deps)


HBM = pl.BlockSpec(memory_space=pltpu.HBM)
SEM = pl.BlockSpec(memory_space=pltpu.SEMAPHORE)
EFFECT = pltpu.SideEffectType.DATAFLOW_SIDE_EFFECTING
PEERS = [(dx, dy, dc) for dx in (1, 0) for dy in (1, 0) for dc in (0, 1) if (dx, dy, dc) != (0, 0, 0)]


def _peer(x, y, c, flip):
    dx, dy, dc = flip
    return (1 - x if dx else x, 1 - y if dy else y, 1 - c if dc else c)


def _exchange_copies(srcs, lands, send, recv, loc, gather):
    x, y, c, _ = _place()
    me = 4 * x + 2 * y + c
    remote, local = [], []
    for w in range(len(srcs)):
        for k, flip in enumerate(PEERS):
            px, py, pc = _peer(x, y, c, flip)
            src = srcs[w] if gather else srcs[w].at[4 * px + 2 * py + pc]
            remote.append(pltpu.make_async_remote_copy(
                src_ref=src, dst_ref=lands[w].at[me], send_sem=send[w].at[k], recv_sem=recv[w].at[k],
                device_id=(px, py, pc), device_id_type=MESH))
        local.append(pltpu.make_async_copy(srcs[w] if gather else srcs[w].at[me], lands[w].at[me], loc[w]))
    return remote, local


class _Exchange:
    def __init__(self, srcs, lands, send, recv, loc, token, gather):
        self.srcs, self.lands, self.send, self.recv, self.loc = srcs, lands, send, recv, loc
        self.token, self.gather = token, gather


def _exchange_start(srcs, gather, name, dep=None):
    n = len(srcs)
    deps = [] if dep is None else [dep]
    land_shapes = [((N_DEV,) + a.shape) if gather else a.shape for a in srcs]
    lands = [pltpu.with_memory_space_constraint(lax.empty(sh, a.dtype), pltpu.HBM) for sh, a in zip(land_shapes, srcs)]
    srcs = [pltpu.with_memory_space_constraint(a, pltpu.HBM) for a in srcs]

    def body(*refs):
        src_refs, land_refs = refs[:n], refs[n:2 * n]
        outs = refs[2 * n + len(deps):]
        send, recv, loc = outs[:n], outs[n:2 * n], outs[2 * n:3 * n]
        token = outs[-1]
        remote, local = _exchange_copies(src_refs, land_refs, send, recv, loc, gather)
        for cp in remote + local:
            cp.start()
        token[...] = jnp.zeros_like(token)

    out_shape = ([pltpu.SemaphoreType.DMA((len(PEERS),))] * (2 * n) + [pltpu.SemaphoreType.DMA(())] * n
                 + [pltpu.HBM(a.shape, a.dtype) for a in srcs] + [pltpu.HBM(a.shape, a.dtype) for a in lands]
                 + [jax.ShapeDtypeStruct((SUB, LANE), F32)])
    res = pl.pallas_call(
        body, name=name, out_shape=out_shape,
        in_specs=[HBM] * (2 * n) + [ANY] * len(deps),
        out_specs=[SEM] * (3 * n) + [HBM] * (2 * n) + [pl.BlockSpec(memory_space=pltpu.VMEM)],
        input_output_aliases={i: 3 * n + i for i in range(2 * n)},
        compiler_params=pltpu.CompilerParams(has_side_effects=EFFECT),
    )(*srcs, *lands, *deps)
    return _Exchange(res[3 * n:4 * n], res[4 * n:5 * n], res[:n], res[n:2 * n], res[2 * n:3 * n], res[-1], gather)


def _exchange_wait(ex, idxs, after, name):
    n = len(idxs)
    srcs = [ex.srcs[i] for i in idxs]
    lands = [ex.lands[i] for i in idxs]
    sems = [ex.send[i] for i in idxs] + [ex.recv[i] for i in idxs] + [ex.loc[i] for i in idxs]
    gather = ex.gather

    def body(*refs):
        src_refs, land_refs = refs[:n], refs[n:2 * n]
        send, recv, loc = refs[2 * n:3 * n], refs[3 * n:4 * n], refs[4 * n:5 * n]
        remote, local = _exchange_copies(src_refs, land_refs, send, recv, loc, gather)
        for cp in remote:
            cp.wait_send()
            cp.wait_recv()
        for cp in local:
            cp.wait()

    res = pl.pallas_call(
        body, name=name,
        out_shape=[pltpu.HBM(a.shape, a.dtype) for a in srcs] + [pltpu.HBM(a.shape, a.dtype) for a in lands],
        in_specs=[HBM] * (2 * n) + [SEM] * (3 * n) + [ANY],
        out_specs=[HBM] * (2 * n),
        input_output_aliases={i: i for i in range(2 * n)},
        compiler_params=pltpu.CompilerParams(has_side_effects=EFFECT),
    )(*srcs, *lands, *sems, after)
    return res[n:]


def _gather2_copies(srcs, lands, send, recv_ici, recv_sib, loc):
    x, y, c, chips = _place()
    me = 4 * x + 2 * y + c
    remote, local = [], []
    for w in range(len(srcs)):
        remote.append(pltpu.make_async_remote_copy(
            src_ref=srcs[w], dst_ref=lands[w].at[me], send_sem=send[w].at[0], recv_sem=recv_sib[w],
            device_id=(x, y, 1 - c), device_id_type=MESH))
        for j, chip in enumerate(chips):
            remote.append(pltpu.make_async_remote_copy(
                src_ref=srcs[w], dst_ref=lands[w].at[me], send_sem=send[w].at[1 + j], recv_sem=recv_ici[w].at[j],
                device_id=(*chip, c), device_id_type=MESH))
        local.append(pltpu.make_async_copy(srcs[w], lands[w].at[me], loc[w]))
    return remote, local


def _gather2_forwards(lands, fsend, frecv, arrived=None):
    x, y, c, chips = _place()
    cps = []
    for w in range(len(lands)):
        for j, chip in enumerate(chips):
            slot = lands[w].at[4 * chip[0] + 2 * chip[1] + c]
            cp = pltpu.make_async_remote_copy(
                src_ref=slot, dst_ref=slot, send_sem=fsend[w].at[j], recv_sem=frecv[w].at[j],
                device_id=(x, y, 1 - c), device_id_type=MESH)
            if arrived is not None:
                pltpu.make_async_remote_copy(
                    src_ref=slot, dst_ref=slot, send_sem=fsend[w].at[j], recv_sem=arrived[w].at[j],
                    device_id=(x, y, 1 - c), device_id_type=MESH).wait_recv()
            cps.append(cp)
    return cps


def _gather2(shards, between, name):
    n = len(shards)
    srcs = [pltpu.with_memory_space_constraint(a, pltpu.HBM) for a in shards]
    lands = [pltpu.with_memory_space_constraint(lax.empty((N_DEV,) + a.shape, a.dtype), pltpu.HBM) for a in shards]
    hbm_like = lambda arrs: [pltpu.HBM(a.shape, a.dtype) for a in arrs]
    tok = jax.ShapeDtypeStruct((SUB, LANE), F32)
    vmem = pl.BlockSpec(memory_space=pltpu.VMEM)
    side = pltpu.CompilerParams(has_side_effects=EFFECT)

    def start(*refs):
        src_refs, land_refs = refs[:n], refs[n:2 * n]
        outs = refs[2 * n:]
        send, recv_ici, recv_sib, loc = outs[:n], outs[n:2 * n], outs[2 * n:3 * n], outs[3 * n:4 * n]
        remote, local = _gather2_copies(src_refs, land_refs, send, recv_ici, recv_sib, loc)
        for cp in remote + local:
            cp.start()
        outs[-1][...] = jnp.zeros((SUB, LANE), F32)

    res = pl.pallas_call(
        start, name=name + "_start",
        out_shape=([pltpu.SemaphoreType.DMA((4,))] * n + [pltpu.SemaphoreType.DMA((3,))] * n
                   + [pltpu.SemaphoreType.DMA(())] * (2 * n) + hbm_like(srcs) + hbm_like(lands) + [tok]),
        in_specs=[HBM] * (2 * n), out_specs=[SEM] * (4 * n) + [HBM] * (2 * n) + [vmem],
        input_output_aliases={i: 4 * n + i for i in range(2 * n)}, compiler_params=side,
    )(*srcs, *lands)
    send, recv_ici, recv_sib, loc = res[:n], res[n:2 * n], res[2 * n:3 * n], res[3 * n:4 * n]
    srcs, lands, token = res[4 * n:5 * n], res[5 * n:6 * n], res[-1]

    done = between(token)
    after = jax.tree_util.tree_leaves(done)

    def forward(*refs):
        land_refs, arrived = refs[:n], refs[n:2 * n]
        outs = refs[2 * n + len(after):]
        fsend, frecv = outs[:n], outs[n:2 * n]
        for cp in _gather2_forwards(land_refs, fsend, frecv, arrived):
            cp.start()
        outs[-1][...] = jnp.zeros((SUB, LANE), F32)

    res = pl.pallas_call(
        forward, name=name + "_forward",
        out_shape=[pltpu.SemaphoreType.DMA((3,))] * (2 * n) + hbm_like(lands) + [tok],
        in_specs=[HBM] * n + [SEM] * n + [ANY] * len(after), out_specs=[SEM] * (2 * n) + [HBM] * n + [vmem],
        input_output_aliases={i: 2 * n + i for i in range(n)}, compiler_params=side,
    )(*lands, *recv_ici, *after)
    fsend, frecv, lands, token = res[:n], res[n:2 * n], res[2 * n:3 * n], res[-1]

    def wait(*refs):
        src_refs, land_refs = refs[:n], refs[n:2 * n]
        sems = refs[2 * n:7 * n]
        send, recv_sib, loc, fsend, frecv = (sems[k * n:(k + 1) * n] for k in range(5))
        remote, local = _gather2_copies(src_refs, land_refs, send, send, recv_sib, loc)
        for w in range(n):
            for cp in remote[4 * w:4 * w + 4]:
                cp.wait_send()
            remote[4 * w].wait_recv()
        for cp in local:
            cp.wait()
        for cp in _gather2_forwards(land_refs, fsend, frecv):
            cp.wait_send()
            cp.wait_recv()

    res = pl.pallas_call(
        wait, name=name + "_wait", out_shape=hbm_like(srcs) + hbm_like(lands),
        in_specs=[HBM] * (2 * n) + [SEM] * (5 * n) + [ANY], out_specs=[HBM] * (2 * n),
        input_output_aliases={i: i for i in range(2 * n)}, compiler_params=side,
    )(*srcs, *lands, *send, *recv_sib, *loc, *fsend, *frecv, token)
    return res[n:], done


def _after(token, a):
    return a + token[0:1, 0:1].astype(a.dtype)


def _unblock(w3):
    nb, k, nbw = w3.shape
    return w3.transpose(1, 0, 2).reshape(k, nb * nbw)


def _block(w, nb):
    k, n = w.shape
    return w.reshape(k, nb, n // nb).transpose(1, 0, 2)


def kernel(x, positions, ln1_g, w_in, b_gate, conv_w, w_conv_out, q_a_g, w_q_b, kv_a_g, w_kv_b, q_norm_g, k_norm_g, w_mla_out, w_o, ln2_g, w_ffn_up, ffn_conv_w, ffn_conv_b, w_ffn_down, loss_target, m_ln1_g, m_w_in, m_b_gate, m_conv_w, m_w_conv_out, m_q_a_g, m_w_q_b, m_kv_a_g, m_w_kv_b, m_q_norm_g, m_k_norm_g, m_w_mla_out, m_w_o, m_ln2_g, m_w_ffn_up, m_ffn_conv_w, m_ffn_conv_b, m_w_ffn_down, v_ln1_g, v_w_in, v_b_gate, v_conv_w, v_w_conv_out, v_q_a_g, v_w_q_b, v_kv_a_g, v_w_kv_b, v_q_norm_g, v_k_norm_g, v_w_mla_out, v_w_o, v_ln2_g, v_w_ffn_up, v_ffn_conv_w, v_ffn_conv_b, v_w_ffn_down):
    s, d = x.shape[1], x.shape[2]
    conv = conv_w.shape[2] * N_DEV
    ql, kvl = q_a_g.shape[1], kv_a_g.shape[1]
    heads = w_q_b.shape[2] * N_DEV // HEAD_QK
    dff = w_ffn_down.shape[1] * N_DEV
    hw = heads * LANE
    conv3 = 3 * conv
    kr_off = conv3 + ql
    kv_off = -(-(kr_off + LANE) // kvl) * kvl
    wa = kv_off + kvl
    assert conv3 % ql == 0 and kr_off % LANE == 0
    xs = x[0]
    tgt = loss_target[0]
    pos = positions.reshape(s, 1)

    nin = w_in.shape[2]
    big = dict(w_in=w_in[0].T, w_conv_out=w_conv_out[0], w_q_b=w_q_b[0], w_kv_b=w_kv_b[0],
               w_mla_out=w_mla_out[0], w_o=w_o[0], w_ffn_up=w_ffn_up[0], w_ffn_down=w_ffn_down[0])
    names = list(big)
    rest = names[1:]
    early = {}

    def while_w_in_travels(token):
        early["ag"] = _exchange_start([big[k].astype(BF) for k in rest], True, "gather_rest_start", dep=token)
        cos_sin = _rope_tables(pos)
        return cos_sin, _rms_fwd(xs, _after(early["ag"].token, ln1_g), d, 0, "rms1_fwd")

    first, ((cos, sin), u1) = _gather2([big["w_in"].astype(BF), _pad8(conv_w[0]), _pad8(ffn_conv_w[0])],
                                       while_w_in_travels, "gather_w_in")
    ag = early["ag"]
    cw8 = _unblock(first[1])
    fcw8 = _unblock(first[2])

    def landed(keys, after, name):
        return _exchange_wait(ag, [rest.index(k) for k in keys], after, name)

    w_in_t = first[0].reshape(N_DEV * nin, d)
    g_off = kr_off + kvl + ROPE
    w_a_t = jnp.concatenate([w_in_t[:kr_off], _lay_rows(w_in_t[kr_off + kvl:g_off]),
                             jnp.zeros((kv_off - kr_off - LANE, d), BF), w_in_t[kr_off:kr_off + kvl]], axis=0)[None]
    w_g_t = w_in_t[g_off:][None]
    gains = _pad8(jnp.concatenate([q_norm_g[:, :NOPE], _lay(q_norm_g[:, NOPE:]),
                                   k_norm_g[:, :NOPE], _lay(k_norm_g[:, NOPE:])], axis=0))
    kr_blk = kr_off // LANE

    z_a = _mm_nt(u1, w_a_t, "mm_z_a")
    z_g = _mm_nt(u1, w_g_t, "mm_z_g", out_dtype=BF)
    p = _conv_mix_fwd(z_a, cw8, conv)
    w_co, w_qb, w_kv = landed(["w_conv_out", "w_q_b", "w_kv_b"], p, "gather_wait_mixers")
    w_co = _unblock(w_co)[None]
    w_kv = _unblock(w_kv)[None]
    wq_full = _unblock(w_qb).reshape(ql, heads, HEAD_QK)
    w_q = jnp.concatenate([wq_full[:, :, :NOPE].reshape(ql, hw), _lay(wq_full[:, :, NOPE:]).reshape(ql, hw)],
                          axis=1)[None]
    yc = _mm_nn(p, w_co, "mm_y_conv", out_dtype=BF)
    qn, q_raw = _rms_mm_nn(z_a, q_a_g, conv3 // ql, w_q, "mm_q")
    kvn, kv_raw = _rms_mm_nn(z_a, kv_a_g, kv_off // kvl, w_kv, "mm_kv")
    q_att, k_att, v_bf = _head_fwd(q_raw, kv_raw, z_a, kr_blk, cos, sin, gains, heads)
    o, o_bf, lse = _attn_fwd(q_att, k_att, v_bf, heads)
    w_mo, w_oo = landed(["w_mla_out", "w_o"], lse, "gather_wait_outs")
    w_mo = w_mo.reshape(1, hw, d)
    w_oo = w_oo.reshape(1, d, d)
    ym, mix = _mla_out_gate(o_bf, w_mo, z_g, b_gate, yc)
    h1, u2 = _residual_norm(mix, w_oo, xs, ln2_g)
    w_up, = landed(["w_ffn_up"], u2, "gather_wait_ffn_up")
    a_g, a_u, f = _ffn_up_act(u2, w_up, fcw8, ffn_conv_b, dff)
    w_dn, = landed(["w_ffn_down"], f, "gather_wait_ffn_down")
    w_dn = w_dn.reshape(1, dff, d)
    dy, dy_bf, loss_part = _mm_nn_loss(f, w_dn, h1, tgt, "mm_ffn_down_loss")

    g_dn = _mm_tn(f, dy_bf, 1, "mm_g_ffn_down").reshape(N_DEV, dff // N_DEV, d)
    rs_dn = _exchange_start([g_dn], False, "reduce_ffn_down_start")
    d_f = _mm_nt(dy_bf, w_dn, "mm_d_f", dep=rs_dn.token)
    d_xg, d_xu, dfw_g, dfw_u = _ffn_act_bwd(a_g, a_u, d_f, fcw8, ffn_conv_b, dff)
    half = N_DEV // 2
    g_up = _mm_tn(u2, d_xg, half, "mm_g_ffn_up_gate", into=lax.empty((N_DEV, d, 2 * dff // N_DEV), BF))
    g_up = _mm_tn(u2, d_xu, half, "mm_g_ffn_up_up", into=g_up, blk0=half)
    rs_up = _exchange_start([g_up], False, "reduce_ffn_up_start")
    d_u2 = _mm_nt([d_xg, d_xu], w_up, "mm_d_u2", out_dtype=BF, dep=rs_up.token)
    d_h1, d_h1_bf, dg_ln2 = _rms_bwd(h1, d_u2, ln2_g, d, 0, "rms2_bwd", extra=dy, also_bf16=True)
    g_oo = _mm_tn(mix, d_h1_bf, 1, "mm_g_w_o").reshape(N_DEV, d // N_DEV, d)
    d_zga, d_zgb, d_yc, d_ym, dba, dbb = _d_mix_gate(d_h1_bf, w_oo, z_g, b_gate, yc, ym)
    g_co = _block(_mm_tn(p, d_yc, 1, "mm_g_conv_out")[0], N_DEV)
    g_mo = _mm_tn(o_bf, d_ym, 1, "mm_g_mla_out").reshape(N_DEV, hw // N_DEV, d)
    rs_mix = _exchange_start([g_oo, g_co, g_mo], False, "reduce_mixers_start")
    d_p = _mm_nt(d_yc, w_co, "mm_d_p", dep=rs_mix.token)
    d_o = _mm_nt(d_ym, w_mo, "mm_d_o", out_dtype=BF)
    d_zb, d_zc, d_zv, dcw = _conv_mix_bwd(z_a, d_p, cw8, conv)
    dq_att, dk_att, dv = _attn_bwd(q_att, k_att, v_bf, o, lse, d_o, heads, dep=rs_mix.token)
    d_q_raw, d_kv_raw, d_kr, dgains = _head_bwd(q_raw, kv_raw, z_a, kr_blk, cos, sin, gains, dq_att, dk_att, dv, heads)
    g_q2 = _mm_tn(qn, d_q_raw, 1, "mm_g_q")[0]
    g_qb = _block(jnp.concatenate([g_q2[:, :hw].reshape(ql, heads, NOPE),
                                   _unlay(g_q2[:, hw:].reshape(ql, heads, LANE))], axis=2).reshape(ql, heads * HEAD_QK), N_DEV)
    g_kv = _block(_mm_tn(kvn, d_kv_raw, 1, "mm_g_kv")[0], N_DEV)
    rs_qkv = _exchange_start([g_qb, g_kv], False, "reduce_qkv_start")
    d_ql, dg_qa = _mm_nt_rms_bwd(d_q_raw, w_q, z_a, q_a_g, conv3 // ql, "mm_d_q_lat", dep=rs_qkv.token)
    d_kvl, dg_kva = _mm_nt_rms_bwd(d_kv_raw, w_kv, z_a, kv_a_g, kv_off // kvl, "mm_d_kv_lat")
    d_z_a = jnp.concatenate([d_zb, d_zc, d_zv, d_ql, d_kr.astype(BF), jnp.zeros((s, kv_off - kr_off - LANE), BF),
                             d_kvl], axis=1)
    g_a = _mm_tn(d_z_a, u1, 1, "mm_g_w_a")[0]
    g_ga = _mm_tn(d_zga, u1, 1, "mm_g_w_ga")[0]
    g_gb = _mm_tn(d_zgb, u1, 1, "mm_g_w_gb")[0]
    g_in = jnp.concatenate([g_a[:kr_off], g_a[kv_off:kv_off + kvl], g_a[kr_off:kr_off + HALF],
                            g_a[kr_off + 2 * HALF:kr_off + 3 * HALF], g_ga, g_gb], axis=0).reshape(N_DEV, nin, d)
    rs_in = _exchange_start([g_in], False, "reduce_w_in_start")
    d_u1 = _mm_nn(d_z_a, w_a_t, "mm_d_u1_a", dep=rs_in.token)
    d_u1 = _mm_nn([d_zga, d_zgb], w_g_t, "mm_d_u1_g", add=d_u1)
    grad_x, dg_ln1 = _rms_bwd(xs, d_u1, ln1_g, d, 0, "rms1_bwd", extra=d_h1)

    summed = {}
    summed["w_ffn_down"], = _exchange_wait(rs_dn, [0], grad_x, "reduce_ffn_down_wait")
    summed["w_ffn_up"], = _exchange_wait(rs_up, [0], grad_x, "reduce_ffn_up_wait")
    summed["w_o"], summed["w_conv_out"], summed["w_mla_out"] = _exchange_wait(rs_mix, [0, 1, 2], grad_x, "reduce_mixers_wait")
    summed["w_q_b"], summed["w_kv_b"] = _exchange_wait(rs_qkv, [0, 1], grad_x, "reduce_qkv_wait")
    loc = locals()
    out = {}
    for k in rest:
        out[k] = _adamw(summed[k], big[k], loc["m_" + k][0], loc["v_" + k][0], "adamw_" + k)

    small = dict(ln1_g=dg_ln1[0:1], b_gate=jnp.concatenate([dba[0:1], dbb[0:1]], axis=1), q_a_g=dg_qa[0:1],
                 kv_a_g=dg_kva[0:1],
                 q_norm_g=jnp.concatenate([dgains[0:1], _unlay(dgains[1:2])], axis=1),
                 k_norm_g=jnp.concatenate([dgains[2:3], _unlay(dgains[3:4])], axis=1),
                 ln2_g=dg_ln2[0:1], ffn_conv_b=jnp.concatenate([dfw_g[3:4], dfw_u[3:4]], axis=1))
    small_names = list(small)
    extra = [dcw[0:3].reshape(1, -1), jnp.concatenate([dfw_g[0:3], dfw_u[0:3]], axis=1).reshape(1, -1),
             loss_part[0:1, 0:1]]
    flat = jnp.concatenate([small[k] for k in small_names] + extra, axis=1)
    n_flat = flat.shape[1]
    rows = -(-n_flat // (SUB * LANE)) * SUB
    flat = jnp.pad(flat, ((0, 0), (0, rows * LANE - n_flat))).reshape(rows, LANE)
    total = _sum_parts(_all_gather([flat], "gather_small", dep=[out[k][0] for k in rest])[0], "sum_small").reshape(1, rows * LANE)
    off = 0
    small_g = {}
    for k in small_names:
        small_g[k] = total[:, off:off + small[k].shape[1]]
        off += small[k].shape[1]
    me = 4 * lax.axis_index("x") + 2 * lax.axis_index("y") + lax.axis_index("c")
    cwn, fcwn = conv // N_DEV, 2 * dff // N_DEV
    g_cw = lax.dynamic_slice_in_dim(total[:, off:off + 3 * conv].reshape(3, conv), me * cwn, cwn, axis=1)
    off += 3 * conv
    g_fcw = lax.dynamic_slice_in_dim(total[:, off:off + 6 * dff].reshape(3, 2 * dff), me * fcwn, fcwn, axis=1)
    off += 6 * dff
    loss = total[0, off]

    summed["w_in"], = _exchange_wait(rs_in, [0], total, "reduce_w_in_wait")
    out["w_in"] = [r.T for r in _adamw(summed["w_in"], big["w_in"], m_w_in[0].T, v_w_in[0].T, "adamw_w_in",
                                       by_cols=True)]
    small_w = dict(ln1_g=ln1_g, b_gate=b_gate, q_a_g=q_a_g, kv_a_g=kv_a_g, q_norm_g=q_norm_g, k_norm_g=k_norm_g,
                   ln2_g=ln2_g, ffn_conv_b=ffn_conv_b, conv_w=conv_w[0].reshape(1, -1),
                   ffn_conv_w=ffn_conv_w[0].reshape(1, -1))
    small_g["conv_w"] = g_cw.reshape(1, -1)
    small_g["ffn_conv_w"] = g_fcw.reshape(1, -1)
    packed_names = list(small_w)

    def pack(get):
        vflat = jnp.concatenate([get(k).reshape(1, -1) for k in packed_names], axis=1)
        nr = -(-vflat.shape[1] // (SUB * LANE)) * SUB
        return jnp.pad(vflat, ((0, 0), (0, nr * LANE - vflat.shape[1])), constant_values=1.0).reshape(nr, LANE)

    res = _adamw(pack(lambda k: small_g[k])[None], pack(lambda k: small_w[k]), pack(lambda k: loc["m_" + k]),
                 pack(lambda k: loc["v_" + k]), "adamw_small")
    res = [r.reshape(1, -1) for r in res]
    off = 0
    for k in packed_names:
        shape = loc[k].shape
        size = small_w[k].shape[1]
        out[k] = [r[:, off:off + size].reshape(shape) for r in res]
        off += size
    for k in names:
        out[k] = [r[None] for r in out[k]]

    order = ["ln1_g", "w_in", "b_gate", "conv_w", "w_conv_out", "q_a_g", "w_q_b", "kv_a_g", "w_kv_b", "q_norm_g",
             "k_norm_g", "w_mla_out", "w_o", "ln2_g", "w_ffn_up", "ffn_conv_w", "ffn_conv_b", "w_ffn_down"]
    return (loss, grad_x[None], *[out[k][0] for k in order], *[out[k][1] for k in order],
            *[out[k][2] for k in order], *[out[k][3] for k in order])
```

```python
import jax
import jax.numpy as jnp
from jax import lax
from jax.experimental import pallas as pl
from jax.experimental.pallas import tpu as pltpu

BF = jnp.bfloat16
F32 = jnp.float32
MESH = pl.DeviceIdType.MESH
N_DEV = 8

NOPE = 128
ROPE = 64
HALF = ROPE // 2
HEAD_QK = NOPE + ROPE
LANE = 128
SUB = 8
QK_SCALE = HEAD_QK ** -0.5
LOG2_E = 1.4426950408889634
NORM_EPS = 1e-6
NEG_INF = -1e30
ROPE_THETA = 10000.0
ADAM_LR = 0.001
ADAM_B1 = 0.9
ADAM_B2 = 0.999
ADAM_EPS = 1e-08
ADAM_WD = 0.01
ADAM_STEP = 10

VMEM_LIMIT = 52 * 1024 * 1024
MM_TM, MM_TN, MM_TK, MM_TS = 1024, 1536, 2048, 2048
ROW_TILE, ROW_TILE_BWD = 512, 256
HEAD_ROW_TILE, HEAD_ROW_TILE_BWD = 512, 256
COL_TILE = 512
FFN_COL_TILE = 1408
ATTN_TILE = 1024
ATTN_TILE_FWD = 1024
ANY = pl.BlockSpec(memory_space=pl.ANY)


def _pick(n, target, mult):
    t = (min(n, target) // mult) * mult
    while t > 0:
        if n % t == 0:
            return t
        t -= mult
    raise ValueError(f"no tile for {n} (target {target}, multiple {mult})")


def _cp(*sem):
    return pltpu.CompilerParams(dimension_semantics=sem, vmem_limit_bytes=VMEM_LIMIT)


def _accumulate(kk, nk, acc, part, finish):
    if nk == 1:
        finish(part())
        return

    @pl.when(kk == 0)
    def _():
        acc[...] = part()

    @pl.when((kk > 0) & (kk < nk - 1))
    def _():
        acc[...] += part()

    @pl.when(kk == nk - 1)
    def _():
        finish(acc[...] + part())


def _mm_call(body, name, grid, in_specs, args, out_spec, out_shape, acc_shape, nk, dep):
    if dep is not None:
        in_specs = in_specs + [ANY]
        args = args + [dep]
    return pl.pallas_call(
        body, name=name, grid=grid, in_specs=in_specs, out_specs=out_spec, out_shape=out_shape,
        scratch_shapes=[pltpu.VMEM(acc_shape, F32)] if nk > 1 else [],
        compiler_params=_cp("parallel", "parallel", "arbitrary"),
    )(*args)


def _mm_nn_loss(a, b3, add, target, name):
    m, k = a.shape
    _, k2, n = b3.shape
    assert k == k2 and b3.shape[0] == 1
    tm = _pick(m, MM_TM, 16)
    tn = _pick(n, MM_TN, LANE)
    tk = _pick(k, MM_TK, LANE)
    nk = k // tk

    def body(a_ref, b_ref, c_ref, t_ref, dy_ref, dyb_ref, l_ref, acc):
        kk = pl.program_id(2)

        @pl.when((pl.program_id(0) == 0) & (pl.program_id(1) == 0) & (kk == 0))
        def _():
            l_ref[...] = jnp.zeros_like(l_ref)

        def part():
            return jnp.dot(a_ref[...].astype(BF), b_ref[0].astype(BF), preferred_element_type=F32)

        def finish(r):
            e = r + c_ref[...] - t_ref[...]
            dy_ref[...] = e / n
            dyb_ref[...] = (e / n).astype(BF)
            l_ref[...] += 0.5 * jnp.sum(jnp.sum(e * e, axis=-1, keepdims=True), axis=0, keepdims=True) / n

        _accumulate(kk, nk, acc, part, finish)

    tile = pl.BlockSpec((tm, tn), lambda i, j, kk: (i, j))
    return pl.pallas_call(
        body, name=name, grid=(m // tm, n // tn, nk),
        in_specs=[pl.BlockSpec((tm, tk), lambda i, j, kk: (i, kk)),
                  pl.BlockSpec((1, tk, tn), lambda i, j, kk: (0, kk, j)), tile, tile],
        out_specs=[tile, tile, pl.BlockSpec((SUB, LANE), lambda i, j, kk: (0, 0))],
        out_shape=[jax.ShapeDtypeStruct((m, n), F32), jax.ShapeDtypeStruct((m, n), BF),
                   jax.ShapeDtypeStruct((SUB, LANE), F32)],
        scratch_shapes=[pltpu.VMEM((tm, tn), F32)],
        compiler_params=_cp("arbitrary", "arbitrary", "arbitrary"),
    )(a, b3, add, target)


def _mm_nn(a, b3, name, add=None, out_dtype=F32, blk0=0, nblk=None, dep=None):
    pair = isinstance(a, (list, tuple))
    a_list = list(a) if pair else [a]
    m, ka = a_list[0].shape
    k = ka * len(a_list)
    nb_all, k2, nbw = b3.shape
    assert k == k2
    nblk = nb_all - blk0 if nblk is None else nblk
    n = nblk * nbw
    tm = _pick(m, MM_TM if k > MM_TM else 2 * MM_TM, 16)
    tn = _pick(nbw, MM_TN, LANE)
    tk = _pick(ka, MM_TK, LANE)
    per = nbw // tn
    nk = k // tk
    nka = ka // tk
    na_ops = len(a_list)

    def body(*refs):
        a_refs, b_ref = refs[:na_ops], refs[na_ops]
        c_ref = refs[na_ops + 1] if add is not None else None
        o_ref = refs[na_ops + 1 + (add is not None) + (dep is not None)]
        acc = refs[-1]
        kk = pl.program_id(2)

        def part():
            av = a_refs[0][...] if not pair else jnp.where(kk < nka, a_refs[0][...], a_refs[1][...])
            return jnp.dot(av.astype(BF), b_ref[...].astype(BF), preferred_element_type=F32)

        def finish(r):
            if add is not None:
                r = r + c_ref[...]
            o_ref[...] = r.astype(out_dtype)

        _accumulate(kk, nk, acc, part, finish)

    if pair:
        in_specs = [pl.BlockSpec((tm, tk), lambda i, j, kk: (i, jnp.minimum(kk, nka - 1))),
                    pl.BlockSpec((tm, tk), lambda i, j, kk: (i, jnp.maximum(kk - nka, 0)))]
    else:
        in_specs = [pl.BlockSpec((tm, tk), lambda i, j, kk: (i, kk))]
    in_specs.append(pl.BlockSpec((None, tk, tn), lambda i, j, kk: (blk0 + j // per, kk, j % per)))
    args = a_list + [b3]
    if add is not None:
        in_specs.append(pl.BlockSpec((tm, tn), lambda i, j, kk: (i, j)))
        args.append(add)
    return _mm_call(body, name, (m // tm, n // tn, nk), in_specs, args,
                    pl.BlockSpec((tm, tn), lambda i, j, kk: (i, j)), jax.ShapeDtypeStruct((m, n), out_dtype),
                    (tm, tn), nk, dep)


def _mm_nt(a, b3, name, add=None, out_dtype=F32, blk0=0, nblk=None, dep=None):
    pair = isinstance(a, (list, tuple))
    a_list = list(a) if pair else [a]
    m, na = a_list[0].shape
    n = na * len(a_list)
    nb_all, k, nbw = b3.shape
    nblk = nb_all - blk0 if nblk is None else nblk
    assert n == nblk * nbw and na % nbw == 0
    tm = _pick(m, 2 * MM_TM if k <= MM_TM and n <= MM_TK else MM_TM, 16)
    tk = _pick(nbw, MM_TK, LANE)
    per = nbw // tk
    nk = n // tk
    tn = _pick(k, MM_TN if nk <= 2 else 2 * MM_TM, LANE)
    nka = na // tk
    na_ops = len(a_list)

    def body(*refs):
        a_refs, b_ref = refs[:na_ops], refs[na_ops]
        c_ref = refs[na_ops + 1] if add is not None else None
        o_ref = refs[na_ops + 1 + (add is not None) + (dep is not None)]
        acc = refs[-1]
        kk = pl.program_id(2)

        def part():
            av = a_refs[0][...] if not pair else jnp.where(kk < nka, a_refs[0][...], a_refs[1][...])
            return lax.dot_general(av.astype(BF), b_ref[...].astype(BF),
                                   (((1,), (1,)), ((), ())), preferred_element_type=F32)

        def finish(r):
            if add is not None:
                r = r + c_ref[...]
            o_ref[...] = r.astype(out_dtype)

        _accumulate(kk, nk, acc, part, finish)

    if pair:
        in_specs = [pl.BlockSpec((tm, tk), lambda i, j, kk: (i, jnp.minimum(kk, nka - 1))),
                    pl.BlockSpec((tm, tk), lambda i, j, kk: (i, jnp.maximum(kk - nka, 0)))]
    else:
        in_specs = [pl.BlockSpec((tm, tk), lambda i, j, kk: (i, kk))]
    in_specs.append(pl.BlockSpec((None, tn, tk), lambda i, j, kk: (blk0 + kk // per, j, kk % per)))
    args = a_list + [b3]
    if add is not None:
        in_specs.append(pl.BlockSpec((tm, tn), lambda i, j, kk: (i, j)))
        args.append(add)
    return _mm_call(body, name, (m // tm, k // tn, nk), in_specs, args,
                    pl.BlockSpec((tm, tn), lambda i, j, kk: (i, j)), jax.ShapeDtypeStruct((m, k), out_dtype),
                    (tm, tn), nk, dep)


def _rms_mm_nn(x, g, col_blk, b3, name):
    m = x.shape[0]
    _, k, n = b3.shape
    assert b3.shape[0] == 1
    tm = _pick(m, 2 * MM_TM, 16)
    tn = _pick(n, MM_TM, LANE)

    def body(x_ref, g_ref, b_ref, u_ref, o_ref):
        xv = x_ref[...]
        r = lax.rsqrt(jnp.mean(xv * xv, axis=-1, keepdims=True) + NORM_EPS)
        u = ((xv * r) * g_ref[...]).astype(BF)

        @pl.when(pl.program_id(1) == 0)
        def _():
            u_ref[...] = u

        o_ref[...] = jnp.dot(u, b_ref[0], preferred_element_type=F32).astype(BF)

    return pl.pallas_call(
        body, name=name, grid=(m // tm, n // tn),
        in_specs=[pl.BlockSpec((tm, k), lambda i, j: (i, col_blk)), pl.BlockSpec((1, k), lambda i, j: (0, 0)),
                  pl.BlockSpec((1, k, tn), lambda i, j: (0, 0, j))],
        out_specs=[pl.BlockSpec((tm, k), lambda i, j: (i, 0)), pl.BlockSpec((tm, tn), lambda i, j: (i, j))],
        out_shape=[jax.ShapeDtypeStruct((m, k), BF), jax.ShapeDtypeStruct((m, n), BF)],
        compiler_params=_cp("parallel", "arbitrary"),
    )(x, g, b3)


def _mm_nt_rms_bwd(a, b3, x, g, col_blk, name, dep=None):
    m, n = a.shape
    _, width, n2 = b3.shape
    assert n == n2 and b3.shape[0] == 1
    tm = _pick(m, MM_TM, 16)
    tk = _pick(n, MM_TK, LANE)
    nk = n // tk

    def body(*refs):
        a_ref, b_ref, x_ref, g_ref = refs[:4]
        dx_ref, dg_ref = refs[4 + (dep is not None):6 + (dep is not None)]
        acc = refs[-1]
        kk = pl.program_id(1)

        @pl.when((pl.program_id(0) == 0) & (kk == 0))
        def _():
            dg_ref[...] = jnp.zeros_like(dg_ref)

        def part():
            return lax.dot_general(a_ref[...], b_ref[0], (((1,), (1,)), ((), ())), preferred_element_type=F32)

        def finish(du):
            xv = x_ref[...]
            r = lax.rsqrt(jnp.mean(xv * xv, axis=-1, keepdims=True) + NORM_EPS)
            nv = xv * r
            dn = du * g_ref[...]
            dx_ref[...] = (r * (dn - nv * jnp.mean(dn * nv, axis=-1, keepdims=True))).astype(BF)
            dg_ref[...] += _rows8([jnp.sum(du * nv, axis=0, keepdims=True)], width)

        _accumulate(kk, nk, acc, part, finish)

    in_specs = [pl.BlockSpec((tm, tk), lambda i, kk: (i, kk)), pl.BlockSpec((1, width, tk), lambda i, kk: (0, 0, kk)),
                pl.BlockSpec((tm, width), lambda i, kk: (i, col_blk)), pl.BlockSpec((1, width), lambda i, kk: (0, 0))]
    args = [a, b3, x, g]
    if dep is not None:
        in_specs.append(ANY)
        args.append(dep)
    return pl.pallas_call(
        body, name=name, grid=(m // tm, nk), in_specs=in_specs,
        out_specs=[pl.BlockSpec((tm, width), lambda i, kk: (i, 0)), pl.BlockSpec((SUB, width), lambda i, kk: (0, 0))],
        out_shape=[jax.ShapeDtypeStruct((m, width), BF), jax.ShapeDtypeStruct((SUB, width), F32)],
        scratch_shapes=[pltpu.VMEM((tm, width), F32)],
        compiler_params=_cp("arbitrary", "arbitrary"),
    )(*args)


def _mm_tn(a, b, nblk, name, out_dtype=BF, dep=None, into=None, blk0=0):
    s, m = a.shape
    s2, n = b.shape
    assert s == s2 and n % nblk == 0 and (dep is None or into is None)
    nbw = n // nblk
    tm = _pick(m, MM_TN, LANE)
    tn = _pick(nbw, MM_TN, LANE)
    ts = _pick(s, MM_TS, LANE)
    per = nbw // tn
    ns = s // ts

    def body(*refs):
        a_ref, b_ref = refs[:2]
        o_ref = refs[2 + (dep is not None or into is not None)]
        acc = refs[-1]

        def part():
            return lax.dot_general(a_ref[...].astype(BF), b_ref[...].astype(BF),
                                   (((0,), (0,)), ((), ())), preferred_element_type=F32)

        def finish(r):
            o_ref[...] = r.astype(out_dtype)

        _accumulate(pl.program_id(2), ns, acc, part, finish)

    in_specs = [pl.BlockSpec((ts, tm), lambda i, j, ss: (ss, i)),
                pl.BlockSpec((ts, tn), lambda i, j, ss: (ss, j))]
    out_spec = pl.BlockSpec((None, tm, tn), lambda i, j, ss: (blk0 + j // per, i, j % per))
    if into is None:
        return _mm_call(body, name, (m // tm, n // tn, ns), in_specs, [a, b], out_spec,
                        jax.ShapeDtypeStruct((nblk, m, nbw), out_dtype), (tm, tn), ns, dep)
    assert into.shape[1:] == (m, nbw) and into.dtype == out_dtype
    return pl.pallas_call(
        body, name=name, grid=(m // tm, n // tn, ns), in_specs=in_specs + [ANY], out_specs=out_spec,
        out_shape=jax.ShapeDtypeStruct(into.shape, out_dtype), input_output_aliases={2: 0},
        scratch_shapes=[pltpu.VMEM((tm, tn), F32)] if ns > 1 else [],
        compiler_params=_cp("parallel", "parallel", "arbitrary"),
    )(a, b, into)


def _rows8(rows, width):
    idx = lax.broadcasted_iota(jnp.int32, (SUB, width), 0)
    out = jnp.zeros((SUB, width), F32)
    for r, v in enumerate(rows):
        out = jnp.where(idx == r, v, out)
    return out


def _rms_fwd(x, g, width, col_blk, name):
    s = x.shape[0]
    tr = _pick(s, ROW_TILE, 16)

    def body(x_ref, g_ref, u_ref):
        xv = x_ref[...]
        r = lax.rsqrt(jnp.mean(xv * xv, axis=-1, keepdims=True) + NORM_EPS)
        u_ref[...] = ((xv * r) * g_ref[...]).astype(BF)

    return pl.pallas_call(
        body, name=name, grid=(s // tr,),
        in_specs=[pl.BlockSpec((tr, width), lambda i: (i, col_blk)),
                  pl.BlockSpec((1, width), lambda i: (0, 0))],
        out_specs=pl.BlockSpec((tr, width), lambda i: (i, 0)),
        out_shape=jax.ShapeDtypeStruct((s, width), BF),
        compiler_params=_cp("parallel"),
    )(x, g)


def _rms_bwd(x, du, g, width, col_blk, name, extra=None, out_dtype=F32, also_bf16=False):
    s = x.shape[0]
    tr = _pick(s, ROW_TILE_BWD, 16)

    def body(*refs):
        x_ref, du_ref, g_ref = refs[:3]
        e_ref = refs[3] if extra is not None else None
        dx_ref = refs[3 + (extra is not None)]
        dxb_ref = refs[4 + (extra is not None)] if also_bf16 else None
        dg_ref = refs[-1]
        i = pl.program_id(0)
        xv = x_ref[...]
        duv = du_ref[...].astype(F32)
        r = lax.rsqrt(jnp.mean(xv * xv, axis=-1, keepdims=True) + NORM_EPS)
        nv = xv * r
        dn = duv * g_ref[...]
        dx = r * (dn - nv * jnp.mean(dn * nv, axis=-1, keepdims=True))
        if extra is not None:
            dx = dx + e_ref[...]
        dx_ref[...] = dx.astype(out_dtype)
        if also_bf16:
            dxb_ref[...] = dx.astype(BF)

        @pl.when(i == 0)
        def _():
            dg_ref[...] = jnp.zeros_like(dg_ref)

        dg_ref[...] += _rows8([jnp.sum(duv * nv, axis=0, keepdims=True)], width)

    in_specs = [pl.BlockSpec((tr, width), lambda i: (i, col_blk)),
                pl.BlockSpec((tr, width), lambda i: (i, 0)),
                pl.BlockSpec((1, width), lambda i: (0, 0))]
    args = [x, du, g]
    if extra is not None:
        in_specs.append(pl.BlockSpec((tr, width), lambda i: (i, 0)))
        args.append(extra)
    return pl.pallas_call(
        body, name=name, grid=(s // tr,),
        in_specs=in_specs,
        out_specs=[pl.BlockSpec((tr, width), lambda i: (i, 0))] * (1 + also_bf16)
        + [pl.BlockSpec((SUB, width), lambda i: (0, 0))],
        out_shape=[jax.ShapeDtypeStruct((s, width), out_dtype)] + [jax.ShapeDtypeStruct((s, width), BF)] * also_bf16
        + [jax.ShapeDtypeStruct((SUB, width), F32)],
        compiler_params=_cp("arbitrary"),
    )(*args)


def _down(cur, prev8, k):
    ext = jnp.concatenate([prev8, cur], axis=0)
    return pltpu.roll(ext, k, axis=0)[SUB:]


def _up(cur, next8, k):
    ext = jnp.concatenate([cur, next8], axis=0)
    return pltpu.roll(ext, ext.shape[0] - k, axis=0)[:cur.shape[0]]


def _lags(cur, prev8):
    return _down(cur, prev8, 1), _down(cur, prev8, 2)


def _conv3(w_ref, cur, prev8, lags=None):
    lag1, lag2 = _lags(cur, prev8) if lags is None else lags
    return w_ref[0:1, :] * lag2 + w_ref[1:2, :] * lag1 + w_ref[2:3, :] * cur


def _conv3_t(w_ref, cur, next8):
    return w_ref[2:3, :] * cur + w_ref[1:2, :] * _up(cur, next8, 1) + w_ref[0:1, :] * _up(cur, next8, 2)


def _spec_cur(tr, tc, c0):
    return pl.BlockSpec((tr, tc), lambda j, i: (i, c0 + j))


def _spec_prev(tr, tc, c0):
    return pl.BlockSpec((SUB, tc), lambda j, i: (jnp.maximum(i * (tr // SUB) - 1, 0), c0 + j))


def _spec_next(tr, tc, c0, s):
    return pl.BlockSpec((SUB, tc), lambda j, i: (jnp.minimum((i + 1) * (tr // SUB), s // SUB - 1), c0 + j))


def _spec_w(tc, c0):
    return pl.BlockSpec((SUB, tc), lambda j, i: (0, c0 + j))


def _pad8(w):
    return jnp.pad(w, ((0, SUB - w.shape[0]), (0, 0)))


def _conv_mix_fwd(z_a, cw8, conv):
    s = z_a.shape[0]
    tr = _pick(s, ROW_TILE, 16)
    tc = _pick(conv, COL_TILE, LANE)
    nc = conv // tc

    def body(zb_ref, zc_ref, zv_ref, zcp_ref, zvp_ref, w_ref, p_ref):
        i = pl.program_id(1)
        cv = zc_ref[...] * zv_ref[...]
        cvp = jnp.where(i > 0, zcp_ref[...] * zvp_ref[...], 0.0)
        p_ref[...] = (zb_ref[...] * _conv3(w_ref, cv, cvp)).astype(BF)

    return pl.pallas_call(
        body, name="conv_mix_fwd", grid=(nc, s // tr),
        in_specs=[_spec_cur(tr, tc, 0), _spec_cur(tr, tc, nc), _spec_cur(tr, tc, 2 * nc),
                  _spec_prev(tr, tc, nc), _spec_prev(tr, tc, 2 * nc), _spec_w(tc, 0)],
        out_specs=_spec_cur(tr, tc, 0),
        out_shape=jax.ShapeDtypeStruct((s, conv), BF),
        compiler_params=_cp("parallel", "parallel"),
    )(z_a, z_a, z_a, z_a, z_a, cw8)


def _conv_mix_bwd(z_a, d_p, cw8, conv):
    s = z_a.shape[0]
    tr = _pick(s, ROW_TILE_BWD, 16)
    tc = _pick(conv, COL_TILE, LANE)
    nc = conv // tc
    nr = s // tr

    def body(zb_ref, zbn_ref, zc_ref, zcp_ref, zv_ref, zvp_ref, dp_ref, dpn_ref, w_ref,
             dzb_ref, dzc_ref, dzv_ref, dw_ref):
        i = pl.program_id(1)
        zc = zc_ref[...]
        zv = zv_ref[...]
        cv = zc * zv
        cvp = jnp.where(i > 0, zcp_ref[...] * zvp_ref[...], 0.0)
        cv1, cv2 = _lags(cv, cvp)
        dpv = dp_ref[...]
        dzb_ref[...] = (dpv * _conv3(w_ref, cv, cvp, (cv1, cv2))).astype(BF)
        dcc = dpv * zb_ref[...]
        dccn = jnp.where(i < nr - 1, dpn_ref[...] * zbn_ref[...], 0.0)
        dcv = _conv3_t(w_ref, dcc, dccn)
        dzc_ref[...] = (dcv * zv).astype(BF)
        dzv_ref[...] = (dcv * zc).astype(BF)

        @pl.when(i == 0)
        def _():
            dw_ref[...] = jnp.zeros_like(dw_ref)

        dw_ref[...] += _rows8([jnp.sum(dcc * cv2, axis=0, keepdims=True),
                               jnp.sum(dcc * cv1, axis=0, keepdims=True),
                               jnp.sum(dcc * cv, axis=0, keepdims=True)], tc)

    out = jax.ShapeDtypeStruct((s, conv), BF)
    return pl.pallas_call(
        body, name="conv_mix_bwd", grid=(nc, nr),
        in_specs=[_spec_cur(tr, tc, 0), _spec_next(tr, tc, 0, s),
                  _spec_cur(tr, tc, nc), _spec_prev(tr, tc, nc),
                  _spec_cur(tr, tc, 2 * nc), _spec_prev(tr, tc, 2 * nc),
                  _spec_cur(tr, tc, 0), _spec_next(tr, tc, 0, s), _spec_w(tc, 0)],
        out_specs=[_spec_cur(tr, tc, 0), _spec_cur(tr, tc, 0), _spec_cur(tr, tc, 0), _spec_w(tc, 0)],
        out_shape=[out, out, out, jax.ShapeDtypeStruct((SUB, conv), F32)],
        compiler_params=_cp("parallel", "arbitrary"),
    )(z_a, z_a, z_a, z_a, z_a, z_a, d_p, d_p, cw8)


def _silu_parts(ag):
    sg = jax.nn.sigmoid(ag)
    return ag * sg, sg


def _ffn_up_act(u2, w_up, cw8, cb, dff):
    s, d = u2.shape
    nb, _, nbw = w_up.shape
    half = nb // 2
    assert half * nbw == dff
    tm = _pick(s, ROW_TILE, 16)

    def body(u_ref, wg_ref, wu_ref, cg_ref, cu_ref, bg_ref, bu_ref, xg_ref, xu_ref, ag_ref, au_ref, f_ref,
             hist_g, hist_u):
        i = pl.program_id(1)

        @pl.when(i == 0)
        def _():
            hist_g[...] = jnp.zeros_like(hist_g)
            hist_u[...] = jnp.zeros_like(hist_u)

        u = u_ref[...]
        xg = jnp.dot(u, wg_ref[...], preferred_element_type=F32)
        xu = jnp.dot(u, wu_ref[...], preferred_element_type=F32)
        xg_ref[...] = xg.astype(BF)
        xu_ref[...] = xu.astype(BF)
        ag = _conv3(cg_ref, xg, hist_g[...]) + bg_ref[...]
        au = _conv3(cu_ref, xu, hist_u[...]) + bu_ref[...]
        ag_ref[...] = ag
        au_ref[...] = au
        f_ref[...] = (_silu_parts(ag)[0] * au).astype(BF)
        hist_g[...] = xg[tm - SUB:]
        hist_u[...] = xu[tm - SUB:]

    once = pl.Buffered(1)
    tile = pl.BlockSpec((tm, nbw), lambda j, i: (i, j))
    return pl.pallas_call(
        body, name="mm_ffn_up_act", grid=(half, s // tm),
        in_specs=[pl.BlockSpec((tm, d), lambda j, i: (i, 0)),
                  pl.BlockSpec((None, d, nbw), lambda j, i: (j, 0, 0), pipeline_mode=once),
                  pl.BlockSpec((None, d, nbw), lambda j, i: (half + j, 0, 0), pipeline_mode=once),
                  pl.BlockSpec((SUB, nbw), lambda j, i: (0, j)), pl.BlockSpec((SUB, nbw), lambda j, i: (0, half + j)),
                  pl.BlockSpec((1, nbw), lambda j, i: (0, j)), pl.BlockSpec((1, nbw), lambda j, i: (0, half + j))],
        out_specs=[tile] * 5,
        out_shape=[jax.ShapeDtypeStruct((s, dff), BF), jax.ShapeDtypeStruct((s, dff), BF),
                   jax.ShapeDtypeStruct((s, dff), F32), jax.ShapeDtypeStruct((s, dff), F32),
                   jax.ShapeDtypeStruct((s, dff), BF)],
        scratch_shapes=[pltpu.VMEM((SUB, nbw), F32), pltpu.VMEM((SUB, nbw), F32)],
        compiler_params=_cp("arbitrary", "arbitrary"),
    )(u2, w_up, w_up, cw8, cw8, cb, cb)


def _ffn_act_bwd(x_g, x_u, a_g, a_u, d_f, cw8, dff):
    s = a_g.shape[0]
    tr = _pick(s, ROW_TILE_BWD // 2, 16)
    tc = _pick(dff, FFN_COL_TILE, LANE)
    nc = dff // tc
    nr = s // tr

    def body(xg_ref, xu_ref, ag_ref, agn_ref, au_ref, aun_ref, df_ref, dfn_ref, wg_ref, wu_ref,
             dxg_ref, dxu_ref, dwg_ref, dwu_ref):
        i = pl.program_id(1)

        def d_act(ag, au, df):
            sil, sg = _silu_parts(ag)
            return df * au * (sg * (1.0 + ag * (1.0 - sg))), df * sil

        dag, dau = d_act(ag_ref[...], au_ref[...], df_ref[...])
        dagn, daun = d_act(agn_ref[...], aun_ref[...], jnp.where(i < nr - 1, dfn_ref[...], 0.0))

        @pl.when(i == 0)
        def _():
            dwg_ref[...] = jnp.zeros_like(dwg_ref)
            dwu_ref[...] = jnp.zeros_like(dwu_ref)

        def half_bwd(w_ref, da, dan, x, dx_ref, dw_ref):
            lead1, lead2 = _up(da, dan, 1), _up(da, dan, 2)
            dx_ref[...] = (w_ref[2:3, :] * da + w_ref[1:2, :] * lead1 + w_ref[0:1, :] * lead2).astype(BF)
            dw_ref[...] += _rows8([jnp.sum(lead2 * x, axis=0, keepdims=True),
                                   jnp.sum(lead1 * x, axis=0, keepdims=True),
                                   jnp.sum(da * x, axis=0, keepdims=True),
                                   jnp.sum(da, axis=0, keepdims=True)], tc)

        half_bwd(wg_ref, dag, dagn, xg_ref[...].astype(F32), dxg_ref, dwg_ref)
        half_bwd(wu_ref, dau, daun, xu_ref[...].astype(F32), dxu_ref, dwu_ref)

    half = jax.ShapeDtypeStruct((s, dff), BF)
    wsh = jax.ShapeDtypeStruct((SUB, dff), F32)
    return pl.pallas_call(
        body, name="ffn_act_bwd", grid=(nc, nr),
        in_specs=[_spec_cur(tr, tc, 0), _spec_cur(tr, tc, 0),
                  _spec_cur(tr, tc, 0), _spec_next(tr, tc, 0, s), _spec_cur(tr, tc, 0), _spec_next(tr, tc, 0, s),
                  _spec_cur(tr, tc, 0), _spec_next(tr, tc, 0, s), _spec_w(tc, 0), _spec_w(tc, nc)],
        out_specs=[_spec_cur(tr, tc, 0), _spec_cur(tr, tc, 0), _spec_w(tc, 0), _spec_w(tc, 0)],
        out_shape=[half, half, wsh, wsh],
        compiler_params=_cp("parallel", "arbitrary"),
    )(x_g, x_u, a_g, a_g, a_u, a_u, d_f, d_f, cw8, cw8)


def _residual_norm(mix, w_oo, x, g):
    m, k = mix.shape
    d = w_oo.shape[2]
    tm = _pick(m, ROW_TILE, 16)

    def body(a_ref, b_ref, x_ref, g_ref, h_ref, u_ref):
        h = jnp.dot(a_ref[...], b_ref[0], preferred_element_type=F32) + x_ref[...]
        h_ref[...] = h
        r = lax.rsqrt(jnp.mean(h * h, axis=-1, keepdims=True) + NORM_EPS)
        u_ref[...] = ((h * r) * g_ref[...]).astype(BF)

    row = pl.BlockSpec((tm, d), lambda i: (i, 0))
    return pl.pallas_call(
        body, name="mm_h1_norm", grid=(m // tm,),
        in_specs=[pl.BlockSpec((tm, k), lambda i: (i, 0)),
                  pl.BlockSpec((1, k, d), lambda i: (0, 0, 0), pipeline_mode=pl.Buffered(1)),
                  row, pl.BlockSpec((1, d), lambda i: (0, 0))],
        out_specs=[row, row],
        out_shape=[jax.ShapeDtypeStruct((m, d), F32), jax.ShapeDtypeStruct((m, d), BF)],
        compiler_params=_cp("parallel"),
    )(mix, w_oo, x, g)


def _mla_out_gate(o, w_mo, z_g, b_gate, yc):
    m, k = o.shape
    d = w_mo.shape[2]
    tm = _pick(m, MM_TM, 16)
    tn = _pick(d, MM_TM, LANE)
    nc = d // tn

    def body(a_ref, b_ref, za_ref, zb_ref, ba_ref, bb_ref, yc_ref, ym_ref, mix_ref):
        ym = jnp.dot(a_ref[...], b_ref[0], preferred_element_type=F32)
        ga = jax.nn.sigmoid(za_ref[...] + ba_ref[...])
        gb = jax.nn.sigmoid(zb_ref[...] + bb_ref[...])
        ym_ref[...] = ym.astype(BF)
        mix_ref[...] = (ga * yc_ref[...] + gb * ym).astype(BF)

    tile = pl.BlockSpec((tm, tn), lambda i, j: (i, j))
    out = jax.ShapeDtypeStruct((m, d), BF)
    return pl.pallas_call(
        body, name="mm_y_mla_gate", grid=(m // tm, nc),
        in_specs=[pl.BlockSpec((tm, k), lambda i, j: (i, 0)), pl.BlockSpec((1, k, tn), lambda i, j: (0, 0, j)),
                  tile, pl.BlockSpec((tm, tn), lambda i, j: (i, nc + j)),
                  pl.BlockSpec((1, tn), lambda i, j: (0, j)), pl.BlockSpec((1, tn), lambda i, j: (0, nc + j)), tile],
        out_specs=[tile, tile], out_shape=[out, out],
        compiler_params=_cp("parallel", "parallel"),
    )(o, w_mo, z_g, z_g, b_gate, b_gate, yc)


def _d_mix_gate(d_h1, w_oo, z_g, b_gate, yc, ym):
    m, n = d_h1.shape
    d = w_oo.shape[1]
    tm = _pick(m, ROW_TILE, 16)
    tn = _pick(d, MM_TM, LANE)
    nc = d // tn

    def body(a_ref, b_ref, za_ref, zb_ref, ba_ref, bb_ref, yc_ref, ym_ref,
             dza_ref, dzb_ref, dyc_ref, dym_ref, dba_ref, dbb_ref):
        i = pl.program_id(1)
        dm = lax.dot_general(a_ref[...], b_ref[0], (((1,), (1,)), ((), ())), preferred_element_type=F32)
        ga = jax.nn.sigmoid(za_ref[...] + ba_ref[...])
        gb = jax.nn.sigmoid(zb_ref[...] + bb_ref[...])
        dza = dm * yc_ref[...] * (ga * (1.0 - ga))
        dzb = dm * ym_ref[...] * (gb * (1.0 - gb))
        dza_ref[...] = dza.astype(BF)
        dzb_ref[...] = dzb.astype(BF)
        dyc_ref[...] = (dm * ga).astype(BF)
        dym_ref[...] = (dm * gb).astype(BF)

        @pl.when(i == 0)
        def _():
            dba_ref[...] = jnp.zeros_like(dba_ref)
            dbb_ref[...] = jnp.zeros_like(dbb_ref)

        dba_ref[...] += _rows8([jnp.sum(dza, axis=0, keepdims=True)], tn)
        dbb_ref[...] += _rows8([jnp.sum(dzb, axis=0, keepdims=True)], tn)

    tile = pl.BlockSpec((tm, tn), lambda j, i: (i, j))
    act = jax.ShapeDtypeStruct((m, d), BF)
    bsh = jax.ShapeDtypeStruct((SUB, d), F32)
    return pl.pallas_call(
        body, name="mm_d_mix_gate", grid=(nc, m // tm),
        in_specs=[pl.BlockSpec((tm, n), lambda j, i: (i, 0)), pl.BlockSpec((1, tn, n), lambda j, i: (0, j, 0)),
                  tile, pl.BlockSpec((tm, tn), lambda j, i: (i, nc + j)),
                  pl.BlockSpec((1, tn), lambda j, i: (0, j)), pl.BlockSpec((1, tn), lambda j, i: (0, nc + j)),
                  tile, tile],
        out_specs=[tile] * 4 + [pl.BlockSpec((SUB, tn), lambda j, i: (0, j))] * 2,
        out_shape=[act, act, act, act, bsh, bsh],
        compiler_params=_cp("parallel", "arbitrary"),
    )(d_h1, w_oo, z_g, z_g, b_gate, b_gate, yc, ym)


def _lay(v):
    z = jnp.zeros(v.shape[:-1] + (HALF,), v.dtype)
    return jnp.concatenate([v[..., :HALF], z, v[..., HALF:], z], axis=-1)


def _unlay(v):
    return jnp.concatenate([v[..., :HALF], v[..., 2 * HALF:3 * HALF]], axis=-1)


def _lay_rows(v):
    z = jnp.zeros((HALF,) + v.shape[1:], v.dtype)
    return jnp.concatenate([v[:HALF], z, v[HALF:], z], axis=0)


def _rope_tables(positions):
    s = positions.shape[0]
    tr = _pick(s, ROW_TILE, 8)
    inv_freq = ROPE_THETA ** (-jnp.arange(0, ROPE, 2, dtype=F32) / ROPE)
    consts = jnp.stack([_lay(jnp.concatenate([inv_freq, inv_freq])),
                        _lay(jnp.ones((ROPE,), F32)),
                        _lay(jnp.concatenate([-jnp.ones((HALF,), F32), jnp.ones((HALF,), F32)]))])
    consts = _pad8(consts)

    def body(p_ref, c_ref, cos_ref, sin_ref):
        ang = p_ref[...].astype(F32) * c_ref[0:1, :]
        cos_ref[...] = jnp.cos(ang) * c_ref[1:2, :]
        sin_ref[...] = jnp.sin(ang) * c_ref[2:3, :]

    tab = jax.ShapeDtypeStruct((s, LANE), F32)
    return pl.pallas_call(
        body, name="rope_tables", grid=(s // tr,),
        in_specs=[pl.BlockSpec((tr, 1), lambda i: (i, 0)), pl.BlockSpec((SUB, LANE), lambda i: (0, 0))],
        out_specs=[pl.BlockSpec((tr, LANE), lambda i: (i, 0))] * 2,
        out_shape=[tab, tab],
        compiler_params=_cp("parallel"),
    )(positions, consts)


def _lane_sum(p):
    return jnp.sum(p, axis=-1, keepdims=True)


def _rope(t, cos, sin):
    return t * cos + pltpu.roll(t, 2 * HALF, axis=1) * sin


def _rope_t(d, cos, sin):
    return d * cos + pltpu.roll(d * sin, 2 * HALF, axis=1)


def _head_fwd(q_raw, kv_raw, z_a, kr_blk, cos, sin, gains, heads):
    s = q_raw.shape[0]
    tr = _pick(s, HEAD_ROW_TILE, 16)
    hw = heads * LANE

    def body(q_ref, kv_ref, kr_ref, cos_ref, sin_ref, g_ref, qo_ref, ko_ref, vo_ref):
        cosv = cos_ref[...]
        sinv = sin_ref[...]
        krv = kr_ref[...]
        kr_sq = krv * krv
        for h in range(heads):
            lo = h * LANE
            qn = q_ref[:, lo:lo + LANE].astype(F32)
            qr = q_ref[:, hw + lo:hw + lo + LANE].astype(F32)
            r = lax.rsqrt(_lane_sum(qn * qn + qr * qr) / HEAD_QK + NORM_EPS)
            qo_ref[:, 2 * lo:2 * lo + LANE] = (((qn * r) * g_ref[0:1, :]) * (QK_SCALE * LOG2_E)).astype(BF)
            qo_ref[:, 2 * lo + LANE:2 * lo + 2 * LANE] = (
                _rope((qr * r) * g_ref[1:2, :], cosv, sinv) * (QK_SCALE * LOG2_E)).astype(BF)
            kn = kv_ref[:, 2 * lo:2 * lo + LANE].astype(F32)
            r = lax.rsqrt(_lane_sum(kn * kn + kr_sq) / HEAD_QK + NORM_EPS)
            ko_ref[:, 2 * lo:2 * lo + LANE] = ((kn * r) * g_ref[2:3, :]).astype(BF)
            ko_ref[:, 2 * lo + LANE:2 * lo + 2 * LANE] = _rope((krv * r) * g_ref[3:4, :], cosv, sinv).astype(BF)
            vo_ref[:, lo:lo + LANE] = kv_ref[:, 2 * lo + LANE:2 * lo + 2 * LANE].astype(BF)

    row = lambda w: pl.BlockSpec((tr, w), lambda i: (i, 0))
    return pl.pallas_call(
        body, name="head_fwd", grid=(s // tr,),
        in_specs=[row(2 * hw), row(2 * hw), pl.BlockSpec((tr, LANE), lambda i: (i, kr_blk)),
                  row(LANE), row(LANE), pl.BlockSpec((SUB, LANE), lambda i: (0, 0))],
        out_specs=[row(2 * hw), row(2 * hw), row(hw)],
        out_shape=[jax.ShapeDtypeStruct((s, 2 * hw), BF), jax.ShapeDtypeStruct((s, 2 * hw), BF),
                   jax.ShapeDtypeStruct((s, hw), BF)],
        compiler_params=_cp("parallel"),
    )(q_raw, kv_raw, z_a, cos, sin, gains)


def _head_bwd(q_raw, kv_raw, z_a, kr_blk, cos, sin, gains, dq_att, dk_att, dv, heads):
    s = q_raw.shape[0]
    tr = _pick(s, HEAD_ROW_TILE_BWD, 16)
    hw = heads * LANE

    def body(q_ref, kv_ref, kr_ref, cos_ref, sin_ref, g_ref, dq_ref, dk_ref, dv_ref,
             dqr_ref, dkv_ref, dkr_ref, dg_ref):
        i = pl.program_id(0)
        cosv = cos_ref[...]
        sinv = sin_ref[...]
        krv = kr_ref[...]
        kr_sq = krv * krv
        dkr = jnp.zeros((tr, LANE), F32)
        dgs = [jnp.zeros((1, LANE), F32) for _ in range(4)]

        def norm_bwd(xn, xr, sq, dn_out, dr_out, gn, gr):
            r = lax.rsqrt(_lane_sum(sq) / HEAD_QK + NORM_EPS)
            nn = xn * r
            nr = xr * r
            dt = _rope_t(dr_out, cosv, sinv)
            dnn = dn_out * gn
            dnr = dt * gr
            mean = _lane_sum(dnn * nn + dnr * nr) / HEAD_QK
            return (r * (dnn - nn * mean), r * (dnr - nr * mean),
                    jnp.sum(dn_out * nn, axis=0, keepdims=True), jnp.sum(dt * nr, axis=0, keepdims=True))

        for h in range(heads):
            lo = h * LANE
            qn = q_ref[:, lo:lo + LANE].astype(F32)
            qr = q_ref[:, hw + lo:hw + lo + LANE].astype(F32)
            dxn, dxr, g0, g1 = norm_bwd(qn, qr, qn * qn + qr * qr, dq_ref[:, 2 * lo:2 * lo + LANE] * QK_SCALE,
                                        dq_ref[:, 2 * lo + LANE:2 * lo + 2 * LANE] * QK_SCALE,
                                        g_ref[0:1, :], g_ref[1:2, :])
            dqr_ref[:, lo:lo + LANE] = dxn.astype(BF)
            dqr_ref[:, hw + lo:hw + lo + LANE] = dxr.astype(BF)
            kn = kv_ref[:, 2 * lo:2 * lo + LANE].astype(F32)
            dxn, dxr, g2, g3 = norm_bwd(kn, krv, kn * kn + kr_sq, dk_ref[:, 2 * lo:2 * lo + LANE],
                                        dk_ref[:, 2 * lo + LANE:2 * lo + 2 * LANE], g_ref[2:3, :], g_ref[3:4, :])
            dkv_ref[:, 2 * lo:2 * lo + LANE] = dxn.astype(BF)
            dkv_ref[:, 2 * lo + LANE:2 * lo + 2 * LANE] = dv_ref[:, lo:lo + LANE].astype(BF)
            dkr = dkr + dxr
            dgs = [a + b for a, b in zip(dgs, (g0, g1, g2, g3))]
        dkr_ref[...] = dkr

        @pl.when(i == 0)
        def _():
            dg_ref[...] = jnp.zeros_like(dg_ref)

        dg_ref[...] += _rows8(dgs, LANE)

    row = lambda w: pl.BlockSpec((tr, w), lambda i: (i, 0))
    return pl.pallas_call(
        body, name="head_bwd", grid=(s // tr,),
        in_specs=[row(2 * hw), row(2 * hw), pl.BlockSpec((tr, LANE), lambda i: (i, kr_blk)),
                  row(LANE), row(LANE), pl.BlockSpec((SUB, LANE), lambda i: (0, 0)),
                  row(2 * hw), row(2 * hw), row(hw)],
        out_specs=[row(2 * hw), row(2 * hw), row(LANE), pl.BlockSpec((SUB, LANE), lambda i: (0, 0))],
        out_shape=[jax.ShapeDtypeStruct((s, 2 * hw), BF), jax.ShapeDtypeStruct((s, 2 * hw), BF),
                   jax.ShapeDtypeStruct((s, LANE), F32), jax.ShapeDtypeStruct((SUB, LANE), F32)],
        compiler_params=_cp("arbitrary"),
    )(q_raw, kv_raw, z_a, cos, sin, gains, dq_att, dk_att, dv)


def _causal_mask(nrows, ncols, row0):
    rows = lax.broadcasted_iota(jnp.int32, (nrows, ncols), 0) + row0
    cols = lax.broadcasted_iota(jnp.int32, (nrows, ncols), 1)
    return cols <= rows


def _causal_steps(nt, q_major):
    pairs = ([(i, j) for i in range(nt) for j in range(i + 1)] if q_major
             else [(i, j) for j in range(nt) for i in range(j, nt)])
    return (jnp.array([p[0] for p in pairs], jnp.int32), jnp.array([p[1] for p in pairs], jnp.int32))


def _attn_fwd(q_att, k_att, v, heads):
    s = q_att.shape[0]
    t = _pick(s, ATTN_TILE_FWD, LANE)
    nt = s // t
    th = t // 2
    qi, kj = _causal_steps(nt, True)

    def body(qi_ref, kj_ref, q_ref, k_ref, v_ref, o_ref, ob_ref, lse_ref, m_s, l_s, acc_s):
        st = pl.program_id(1)
        i = qi_ref[st]
        j = kj_ref[st]

        @pl.when(j == 0)
        def _():
            m_s[...] = jnp.full_like(m_s, NEG_INF)
            l_s[...] = jnp.zeros_like(l_s)
            acc_s[...] = jnp.zeros_like(acc_s)

        def update(rows, ncol, masked):
            sc = lax.dot_general(q_ref[rows, :], k_ref[0:ncol, :], (((1,), (1,)), ((), ())),
                                 preferred_element_type=F32)
            if masked:
                sc = jnp.where(_causal_mask(rows.stop - rows.start, ncol, rows.start), sc, NEG_INF)
            m_prev = m_s[rows, :]
            m_new = jnp.maximum(m_prev, jnp.max(sc, axis=-1, keepdims=True))
            alpha = jnp.exp2(m_prev - m_new)
            p = jnp.exp2(sc - jnp.tile(m_new, (1, ncol // LANE)))
            l_s[rows, :] = alpha * l_s[rows, :] + jnp.sum(p, axis=-1, keepdims=True)
            acc_s[rows, :] = alpha * acc_s[rows, :] + jnp.dot(p.astype(BF), v_ref[0:ncol, :],
                                                              preferred_element_type=F32)
            m_s[rows, :] = m_new

        @pl.when(j < i)
        def _():
            update(slice(0, t), t, False)

        @pl.when(j == i)
        def _():
            update(slice(0, th), th, True)
            update(slice(th, t), t, True)
            o = acc_s[...] / l_s[...]
            o_ref[...] = o
            ob_ref[...] = o.astype(BF)
            lse_ref[...] = (m_s[...] + jnp.log2(l_s[...]))[:, 0:1]

    q_idx = lambda h, st, qi_r, kj_r: (qi_r[st], h)
    kv_idx = lambda h, st, qi_r, kj_r: (kj_r[st], h)
    return pl.pallas_call(
        body, name="attn_fwd",
        grid_spec=pltpu.PrefetchScalarGridSpec(
            num_scalar_prefetch=2, grid=(heads, qi.shape[0]),
            in_specs=[pl.BlockSpec((t, 2 * LANE), q_idx), pl.BlockSpec((t, 2 * LANE), kv_idx),
                      pl.BlockSpec((t, LANE), kv_idx)],
            out_specs=[pl.BlockSpec((t, LANE), q_idx), pl.BlockSpec((t, LANE), q_idx),
                       pl.BlockSpec((None, t, 1), lambda h, st, qi_r, kj_r: (h, qi_r[st], 0))],
            scratch_shapes=[pltpu.VMEM((t, LANE), F32), pltpu.VMEM((t, LANE), F32), pltpu.VMEM((t, LANE), F32)]),
        out_shape=[jax.ShapeDtypeStruct((s, heads * LANE), F32), jax.ShapeDtypeStruct((s, heads * LANE), BF),
                   jax.ShapeDtypeStruct((heads, s, 1), F32)],
        compiler_params=_cp("parallel", "arbitrary"),
    )(qi, kj, q_att, k_att, v)


def _attn_bwd(q_att, k_att, v, o, lse, d_o, heads, dep=None):
    s = q_att.shape[0]
    t = _pick(s, ATTN_TILE, LANE)
    nt = s // t
    th = t // 2
    qi, kj = _causal_steps(nt, False)

    def body(qi_ref, kj_ref, q_ref, k_ref, v_ref, do_ref, o_ref, lse_ref, *rest):
        dq_ref, dk_ref, dv_ref, dk_s, dv_s = rest[-5:]
        st = pl.program_id(1)
        i = qi_ref[st]
        j = kj_ref[st]

        @pl.when(st == 0)
        def _():
            dq_ref[...] = jnp.zeros_like(dq_ref)

        @pl.when(i == j)
        def _():
            dk_s[...] = jnp.zeros_like(dk_s)
            dv_s[...] = jnp.zeros_like(dv_s)

        def update(rows, ncol, masked):
            nrow = rows.stop - rows.start
            q = q_ref[rows, :]
            k = k_ref[0:ncol, :]
            do = do_ref[rows, :]
            sc = lax.dot_general(q, k, (((1,), (1,)), ((), ())), preferred_element_type=F32)
            if masked:
                sc = jnp.where(_causal_mask(nrow, ncol, rows.start), sc, NEG_INF)
            p = jnp.exp2(sc - lse_ref[rows, :])
            dp = lax.dot_general(do, v_ref[0:ncol, :], (((1,), (1,)), ((), ())), preferred_element_type=F32)
            delta = jnp.sum(do.astype(F32) * o_ref[rows, :], axis=-1, keepdims=True)
            ds = (p * (dp - delta)).astype(BF)
            dv_s[0:ncol, :] += lax.dot_general(p.astype(BF), do, (((0,), (0,)), ((), ())),
                                               preferred_element_type=F32)
            dk_s[0:ncol, :] += lax.dot_general(ds, q, (((0,), (0,)), ((), ())), preferred_element_type=F32)
            out_rows = pl.ds(pl.multiple_of(i * t + rows.start, nrow), nrow)
            dq_ref[out_rows, :] += jnp.dot(ds, k, preferred_element_type=F32)

        @pl.when(i > j)
        def _():
            update(slice(0, t), t, False)

        @pl.when(i == j)
        def _():
            update(slice(0, th), th, True)
            update(slice(th, t), t, True)

        @pl.when(i == nt - 1)
        def _():
            dk_ref[...] = (dk_s[...] * (1.0 / LOG2_E)).astype(BF)
            dv_ref[...] = dv_s[...].astype(BF)

    q_idx = lambda h, st, qi_r, kj_r: (qi_r[st], h)
    kv_idx = lambda h, st, qi_r, kj_r: (kj_r[st], h)
    in_specs = [pl.BlockSpec((t, 2 * LANE), q_idx), pl.BlockSpec((t, 2 * LANE), kv_idx),
                pl.BlockSpec((t, LANE), kv_idx), pl.BlockSpec((t, LANE), q_idx), pl.BlockSpec((t, LANE), q_idx),
                pl.BlockSpec((None, t, 1), lambda h, st, qi_r, kj_r: (h, qi_r[st], 0))]
    args = [q_att, k_att, v, d_o, o, lse]
    if dep is not None:
        in_specs.append(ANY)
        args.append(dep)
    return pl.pallas_call(
        body, name="attn_bwd",
        grid_spec=pltpu.PrefetchScalarGridSpec(
            num_scalar_prefetch=2, grid=(heads, qi.shape[0]),
            in_specs=in_specs,
            out_specs=[pl.BlockSpec((s, 2 * LANE), lambda h, st, qi_r, kj_r: (0, h)),
                       pl.BlockSpec((t, 2 * LANE), kv_idx), pl.BlockSpec((t, LANE), kv_idx)],
            scratch_shapes=[pltpu.VMEM((t, 2 * LANE), F32), pltpu.VMEM((t, LANE), F32)]),
        out_shape=[jax.ShapeDtypeStruct((s, heads * 2 * LANE), F32),
                   jax.ShapeDtypeStruct((s, heads * 2 * LANE), BF),
                   jax.ShapeDtypeStruct((s, heads * LANE), BF)],
        compiler_params=_cp("parallel", "arbitrary"),
    )(qi, kj, *args)


def _sum_parts(parts, name):
    n, r, c = parts.shape
    tr = _pick(r, 512, 8)

    def body(p_ref, o_ref):
        g = p_ref[0].astype(F32)
        for k in range(1, n):
            g = g + p_ref[k].astype(F32)
        o_ref[...] = g

    return pl.pallas_call(
        body, name=name, grid=(r // tr,),
        in_specs=[pl.BlockSpec((n, tr, c), lambda i: (0, i, 0))],
        out_specs=pl.BlockSpec((tr, c), lambda i: (i, 0)),
        out_shape=jax.ShapeDtypeStruct((r, c), F32),
        compiler_params=_cp("parallel"),
    )(parts)


def _adamw(parts, w, m, v, name, by_cols=False):
    n, rp, c = parts.shape
    r = w.shape[0]
    assert by_cols or rp == r
    tr, tc = (r, _pick(c, 256, LANE)) if by_cols else (_pick(r, 256, 16 if r % 16 == 0 else 8), c)

    def body(p_ref, w_ref, m_ref, v_ref, g_ref, d_ref, mo_ref, vo_ref):
        g = p_ref[0].astype(F32)
        for k in range(1, n):
            g = g + p_ref[k].astype(F32)
        g = g[:r] if by_cols else g
        m_new = ADAM_B1 * m_ref[...] + (1.0 - ADAM_B1) * g
        v_new = ADAM_B2 * v_ref[...] + (1.0 - ADAM_B2) * jnp.square(g)
        m_hat = m_new / (1.0 - ADAM_B1 ** ADAM_STEP)
        v_hat = v_new / (1.0 - ADAM_B2 ** ADAM_STEP)
        g_ref[...] = g
        d_ref[...] = -ADAM_LR * (m_hat / (jnp.sqrt(v_hat) + ADAM_EPS) + ADAM_WD * w_ref[...])
        mo_ref[...] = m_new
        vo_ref[...] = v_new

    idx = (lambda i: (0, i)) if by_cols else (lambda i: (i, 0))
    spec = pl.BlockSpec((tr, tc), idx)
    sh = jax.ShapeDtypeStruct((r, c), F32)
    return pl.pallas_call(
        body, name=name, grid=(c // tc if by_cols else r // tr,),
        in_specs=[pl.BlockSpec((n, rp if by_cols else tr, tc), lambda i: (0,) + idx(i)), spec, spec, spec],
        out_specs=[spec] * 4, out_shape=[sh] * 4,
        compiler_params=_cp("parallel"),
    )(parts, w, m, v)


def _place():
    x, y, c = lax.axis_index("x"), lax.axis_index("y"), lax.axis_index("c")
    chips = [(1 - x, y), (x, 1 - y), (1 - x, 1 - y)]
    return x, y, c, chips


def _all_gather(shards, name, dep=None):
    n = len(shards)
    deps = [] if dep is None else list(dep)

    def body(*refs):
        ins, outs = refs[:n], refs[n + len(deps):2 * n + len(deps)]
        send_sems, recv_sems, local_sems = refs[2 * n + len(deps):]
        x, y, c, chips = _place()
        me, sibling = (x, y, c), (x, y, 1 - c)

        def slot(w, p):
            return outs[w].at[4 * p[0] + 2 * p[1] + p[2]]

        def copy(w, k, block, to, src=None):
            return pltpu.make_async_remote_copy(
                src_ref=slot(w, block) if src is None else src, dst_ref=slot(w, block),
                send_sem=send_sems.at[w, k], recv_sem=recv_sems.at[w, k], device_id=to, device_id_type=MESH)

        first = []
        for w in range(n):
            first += [copy(w, 1 + j, me, (*chip, c), src=ins[w]) for j, chip in enumerate(chips)]
            first.append(copy(w, 0, me, sibling, src=ins[w]))
        for cp in first:
            cp.start()
        mine = [pltpu.make_async_copy(ins[w], slot(w, me), local_sems.at[w]) for w in range(n)]
        for cp in mine:
            cp.start()
        passed = []
        for w in range(n):
            for j, chip in enumerate(chips):
                copy(w, 1 + j, (*chip, c), me).wait_recv()
                cp = copy(w, 4 + j, (*chip, c), sibling)
                cp.start()
                passed.append(cp)
        for w in range(n):
            copy(w, 0, sibling, me).wait_recv()
            for j, chip in enumerate(chips):
                copy(w, 4 + j, (*chip, 1 - c), me).wait_recv()
        for cp in first + passed:
            cp.wait_send()
        for cp in mine:
            cp.wait()

    return pl.pallas_call(
        body, name=name,
        in_specs=[ANY] * (n + len(deps)), out_specs=[ANY] * n,
        out_shape=[jax.ShapeDtypeStruct((N_DEV,) + a.shape, a.dtype) for a in shards],
        scratch_shapes=[pltpu.SemaphoreType.DMA((n, 7)), pltpu.SemaphoreType.DMA((n, 7)),
                        pltpu.SemaphoreType.DMA((n,))],
    )(*shards, *deps)


HBM = pl.BlockSpec(memory_space=pltpu.HBM)
SEM = pl.BlockSpec(memory_space=pltpu.SEMAPHORE)
EFFECT = pltpu.SideEffectType.DATAFLOW_SIDE_EFFECTING
PEERS = [(dx, dy, dc) for dx in (1, 0) for dy in (1, 0) for dc in (0, 1) if (dx, dy, dc) != (0, 0, 0)]


def _peer(x, y, c, flip):
    dx, dy, dc = flip
    return (1 - x if dx else x, 1 - y if dy else y, 1 - c if dc else c)


def _exchange_copies(srcs, lands, send, recv, loc, gather):
    x, y, c, _ = _place()
    me = 4 * x + 2 * y + c
    remote, local = [], []
    for w in range(len(srcs)):
        for k, flip in enumerate(PEERS):
            px, py, pc = _peer(x, y, c, flip)
            src = srcs[w] if gather else srcs[w].at[4 * px + 2 * py + pc]
            remote.append(pltpu.make_async_remote_copy(
                src_ref=src, dst_ref=lands[w].at[me], send_sem=send[w].at[k], recv_sem=recv[w].at[k],
                device_id=(px, py, pc), device_id_type=MESH))
        local.append(pltpu.make_async_copy(srcs[w] if gather else srcs[w].at[me], lands[w].at[me], loc[w]))
    return remote, local


class _Exchange:
    def __init__(self, srcs, lands, send, recv, loc, token, gather):
        self.srcs, self.lands, self.send, self.recv, self.loc = srcs, lands, send, recv, loc
        self.token, self.gather = token, gather


def _exchange_start(srcs, gather, name, dep=None):
    n = len(srcs)
    deps = [] if dep is None else [dep]
    land_shapes = [((N_DEV,) + a.shape) if gather else a.shape for a in srcs]
    lands = [pltpu.with_memory_space_constraint(lax.empty(sh, a.dtype), pltpu.HBM) for sh, a in zip(land_shapes, srcs)]
    srcs = [pltpu.with_memory_space_constraint(a, pltpu.HBM) for a in srcs]

    def body(*refs):
        src_refs, land_refs = refs[:n], refs[n:2 * n]
        outs = refs[2 * n + len(deps):]
        send, recv, loc = outs[:n], outs[n:2 * n], outs[2 * n:3 * n]
        token = outs[-1]
        remote, local = _exchange_copies(src_refs, land_refs, send, recv, loc, gather)
        for cp in remote + local:
            cp.start()
        token[...] = jnp.zeros_like(token)

    out_shape = ([pltpu.SemaphoreType.DMA((len(PEERS),))] * (2 * n) + [pltpu.SemaphoreType.DMA(())] * n
                 + [pltpu.HBM(a.shape, a.dtype) for a in srcs] + [pltpu.HBM(a.shape, a.dtype) for a in lands]
                 + [jax.ShapeDtypeStruct((SUB, LANE), F32)])
    res = pl.pallas_call(
        body, name=name, out_shape=out_shape,
        in_specs=[HBM] * (2 * n) + [ANY] * len(deps),
        out_specs=[SEM] * (3 * n) + [HBM] * (2 * n) + [pl.BlockSpec(memory_space=pltpu.VMEM)],
        input_output_aliases={i: 3 * n + i for i in range(2 * n)},
        compiler_params=pltpu.CompilerParams(has_side_effects=EFFECT),
    )(*srcs, *lands, *deps)
    return _Exchange(res[3 * n:4 * n], res[4 * n:5 * n], res[:n], res[n:2 * n], res[2 * n:3 * n], res[-1], gather)


def _exchange_wait(ex, idxs, after, name):
    n = len(idxs)
    srcs = [ex.srcs[i] for i in idxs]
    lands = [ex.lands[i] for i in idxs]
    sems = [ex.send[i] for i in idxs] + [ex.recv[i] for i in idxs] + [ex.loc[i] for i in idxs]
    gather = ex.gather

    def body(*refs):
        src_refs, land_refs = refs[:n], refs[n:2 * n]
        send, recv, loc = refs[2 * n:3 * n], refs[3 * n:4 * n], refs[4 * n:5 * n]
        remote, local = _exchange_copies(src_refs, land_refs, send, recv, loc, gather)
        for cp in remote:
            cp.wait_send()
            cp.wait_recv()
        for cp in local:
            cp.wait()

    res = pl.pallas_call(
        body, name=name,
        out_shape=[pltpu.HBM(a.shape, a.dtype) for a in srcs] + [pltpu.HBM(a.shape, a.dtype) for a in lands],
        in_specs=[HBM] * (2 * n) + [SEM] * (3 * n) + [ANY],
        out_specs=[HBM] * (2 * n),
        input_output_aliases={i: i for i in range(2 * n)},
        compiler_params=pltpu.CompilerParams(has_side_effects=EFFECT),
    )(*srcs, *lands, *sems, after)
    return res[n:]


def _gather2_copies(srcs, lands, send, recv_ici, recv_sib, loc):
    x, y, c, chips = _place()
    me = 4 * x + 2 * y + c
    remote, local = [], []
    for w in range(len(srcs)):
        remote.append(pltpu.make_async_remote_copy(
            src_ref=srcs[w], dst_ref=lands[w].at[me], send_sem=send[w].at[0], recv_sem=recv_sib[w],
            device_id=(x, y, 1 - c), device_id_type=MESH))
        for j, chip in enumerate(chips):
            remote.append(pltpu.make_async_remote_copy(
                src_ref=srcs[w], dst_ref=lands[w].at[me], send_sem=send[w].at[1 + j], recv_sem=recv_ici[w].at[j],
                device_id=(*chip, c), device_id_type=MESH))
        local.append(pltpu.make_async_copy(srcs[w], lands[w].at[me], loc[w]))
    return remote, local


def _gather2_forwards(lands, fsend, frecv, arrived=None):
    x, y, c, chips = _place()
    cps = []
    for w in range(len(lands)):
        for j, chip in enumerate(chips):
            slot = lands[w].at[4 * chip[0] + 2 * chip[1] + c]
            cp = pltpu.make_async_remote_copy(
                src_ref=slot, dst_ref=slot, send_sem=fsend[w].at[j], recv_sem=frecv[w].at[j],
                device_id=(x, y, 1 - c), device_id_type=MESH)
            if arrived is not None:
                pltpu.make_async_remote_copy(
                    src_ref=slot, dst_ref=slot, send_sem=fsend[w].at[j], recv_sem=arrived[w].at[j],
                    device_id=(x, y, 1 - c), device_id_type=MESH).wait_recv()
            cps.append(cp)
    return cps


def _gather2(shards, between, name):
    n = len(shards)
    srcs = [pltpu.with_memory_space_constraint(a, pltpu.HBM) for a in shards]
    lands = [pltpu.with_memory_space_constraint(lax.empty((N_DEV,) + a.shape, a.dtype), pltpu.HBM) for a in shards]
    hbm_like = lambda arrs: [pltpu.HBM(a.shape, a.dtype) for a in arrs]
    tok = jax.ShapeDtypeStruct((SUB, LANE), F32)
    vmem = pl.BlockSpec(memory_space=pltpu.VMEM)
    side = pltpu.CompilerParams(has_side_effects=EFFECT)

    def start(*refs):
        src_refs, land_refs = refs[:n], refs[n:2 * n]
        outs = refs[2 * n:]
        send, recv_ici, recv_sib, loc = outs[:n], outs[n:2 * n], outs[2 * n:3 * n], outs[3 * n:4 * n]
        remote, local = _gather2_copies(src_refs, land_refs, send, recv_ici, recv_sib, loc)
        for cp in remote + local:
            cp.start()
        outs[-1][...] = jnp.zeros((SUB, LANE), F32)

    res = pl.pallas_call(
        start, name=name + "_start",
        out_shape=([pltpu.SemaphoreType.DMA((4,))] * n + [pltpu.SemaphoreType.DMA((3,))] * n
                   + [pltpu.SemaphoreType.DMA(())] * (2 * n) + hbm_like(srcs) + hbm_like(lands) + [tok]),
        in_specs=[HBM] * (2 * n), out_specs=[SEM] * (4 * n) + [HBM] * (2 * n) + [vmem],
        input_output_aliases={i: 4 * n + i for i in range(2 * n)}, compiler_params=side,
    )(*srcs, *lands)
    send, recv_ici, recv_sib, loc = res[:n], res[n:2 * n], res[2 * n:3 * n], res[3 * n:4 * n]
    srcs, lands, token = res[4 * n:5 * n], res[5 * n:6 * n], res[-1]

    done = between(token)
    after = jax.tree_util.tree_leaves(done)

    def forward(*refs):
        land_refs, arrived = refs[:n], refs[n:2 * n]
        outs = refs[2 * n + len(after):]
        fsend, frecv = outs[:n], outs[n:2 * n]
        for cp in _gather2_forwards(land_refs, fsend, frecv, arrived):
            cp.start()
        outs[-1][...] = jnp.zeros((SUB, LANE), F32)

    res = pl.pallas_call(
        forward, name=name + "_forward",
        out_shape=[pltpu.SemaphoreType.DMA((3,))] * (2 * n) + hbm_like(lands) + [tok],
        in_specs=[HBM] * n + [SEM] * n + [ANY] * len(after), out_specs=[SEM] * (2 * n) + [HBM] * n + [vmem],
        input_output_aliases={i: 2 * n + i for i in range(n)}, compiler_params=side,
    )(*lands, *recv_ici, *after)
    fsend, frecv, lands, token = res[:n], res[n:2 * n], res[2 * n:3 * n], res[-1]

    def wait(*refs):
        src_refs, land_refs = refs[:n], refs[n:2 * n]
        sems = refs[2 * n:7 * n]
        send, recv_sib, loc, fsend, frecv = (sems[k * n:(k + 1) * n] for k in range(5))
        remote, local = _gather2_copies(src_refs, land_refs, send, send, recv_sib, loc)
        for w in range(n):
            for cp in remote[4 * w:4 * w + 4]:
                cp.wait_send()
            remote[4 * w].wait_recv()
        for cp in local:
            cp.wait()
        for cp in _gather2_forwards(land_refs, fsend, frecv):
            cp.wait_send()
            cp.wait_recv()

    res = pl.pallas_call(
        wait, name=name + "_wait", out_shape=hbm_like(srcs) + hbm_like(lands),
        in_specs=[HBM] * (2 * n) + [SEM] * (5 * n) + [ANY], out_specs=[HBM] * (2 * n),
        input_output_aliases={i: i for i in range(2 * n)}, compiler_params=side,
    )(*srcs, *lands, *send, *recv_sib, *loc, *fsend, *frecv, token)
    return res[n:], done


def _after(token, a):
    return a + token[0:1, 0:1].astype(a.dtype)


def _unblock(w3):
    nb, k, nbw = w3.shape
    return w3.transpose(1, 0, 2).reshape(k, nb * nbw)


def _block(w, nb):
    k, n = w.shape
    return w.reshape(k, nb, n // nb).transpose(1, 0, 2)


def kernel(x, positions, ln1_g, w_in, b_gate, conv_w, w_conv_out, q_a_g, w_q_b, kv_a_g, w_kv_b, q_norm_g, k_norm_g, w_mla_out, w_o, ln2_g, w_ffn_up, ffn_conv_w, ffn_conv_b, w_ffn_down, loss_target, m_ln1_g, m_w_in, m_b_gate, m_conv_w, m_w_conv_out, m_q_a_g, m_w_q_b, m_kv_a_g, m_w_kv_b, m_q_norm_g, m_k_norm_g, m_w_mla_out, m_w_o, m_ln2_g, m_w_ffn_up, m_ffn_conv_w, m_ffn_conv_b, m_w_ffn_down, v_ln1_g, v_w_in, v_b_gate, v_conv_w, v_w_conv_out, v_q_a_g, v_w_q_b, v_kv_a_g, v_w_kv_b, v_q_norm_g, v_k_norm_g, v_w_mla_out, v_w_o, v_ln2_g, v_w_ffn_up, v_ffn_conv_w, v_ffn_conv_b, v_w_ffn_down):
    s, d = x.shape[1], x.shape[2]
    conv = conv_w.shape[2] * N_DEV
    ql, kvl = q_a_g.shape[1], kv_a_g.shape[1]
    heads = w_q_b.shape[2] * N_DEV // HEAD_QK
    dff = w_ffn_down.shape[1] * N_DEV
    hw = heads * LANE
    conv3 = 3 * conv
    kr_off = conv3 + ql
    kv_off = -(-(kr_off + LANE) // kvl) * kvl
    wa = kv_off + kvl
    assert conv3 % ql == 0 and kr_off % LANE == 0
    xs = x[0]
    tgt = loss_target[0]
    pos = positions.reshape(s, 1)

    nin = w_in.shape[2]
    big = dict(w_in=w_in[0].T, w_conv_out=w_conv_out[0], w_q_b=w_q_b[0], w_kv_b=w_kv_b[0],
               w_mla_out=w_mla_out[0], w_o=w_o[0], w_ffn_up=w_ffn_up[0], w_ffn_down=w_ffn_down[0])
    names = list(big)
    rest = names[1:]
    early = {}

    def while_w_in_travels(token):
        early["ag"] = _exchange_start([big[k].astype(BF) for k in rest], True, "gather_rest_start", dep=token)
        cos_sin = _rope_tables(pos)
        return cos_sin, _rms_fwd(xs, _after(early["ag"].token, ln1_g), d, 0, "rms1_fwd")

    first, ((cos, sin), u1) = _gather2([big["w_in"].astype(BF), _pad8(conv_w[0]), _pad8(ffn_conv_w[0])],
                                       while_w_in_travels, "gather_w_in")
    ag = early["ag"]
    cw8 = _unblock(first[1])
    fcw8 = _unblock(first[2])

    def landed(keys, after, name):
        return _exchange_wait(ag, [rest.index(k) for k in keys], after, name)

    w_in_t = first[0].reshape(N_DEV * nin, d)
    g_off = kr_off + kvl + ROPE
    w_a_t = jnp.concatenate([w_in_t[:kr_off], _lay_rows(w_in_t[kr_off + kvl:g_off]),
                             jnp.zeros((kv_off - kr_off - LANE, d), BF), w_in_t[kr_off:kr_off + kvl]], axis=0)[None]
    w_g_t = w_in_t[g_off:][None]
    gains = _pad8(jnp.concatenate([q_norm_g[:, :NOPE], _lay(q_norm_g[:, NOPE:]),
                                   k_norm_g[:, :NOPE], _lay(k_norm_g[:, NOPE:])], axis=0))
    kr_blk = kr_off // LANE

    z_a = _mm_nt(u1, w_a_t, "mm_z_a")
    z_g = _mm_nt(u1, w_g_t, "mm_z_g", out_dtype=BF)
    p = _conv_mix_fwd(z_a, cw8, conv)
    w_co, w_qb, w_kv = landed(["w_conv_out", "w_q_b", "w_kv_b"], p, "gather_wait_mixers")
    w_co = _unblock(w_co)[None]
    w_kv = _unblock(w_kv)[None]
    wq_full = _unblock(w_qb).reshape(ql, heads, HEAD_QK)
    w_q = jnp.concatenate([wq_full[:, :, :NOPE].reshape(ql, hw), _lay(wq_full[:, :, NOPE:]).reshape(ql, hw)],
                          axis=1)[None]
    yc = _mm_nn(p, w_co, "mm_y_conv", out_dtype=BF)
    qn, q_raw = _rms_mm_nn(z_a, q_a_g, conv3 // ql, w_q, "mm_q")
    kvn, kv_raw = _rms_mm_nn(z_a, kv_a_g, kv_off // kvl, w_kv, "mm_kv")
    q_att, k_att, v_bf = _head_fwd(q_raw, kv_raw, z_a, kr_blk, cos, sin, gains, heads)
    o, o_bf, lse = _attn_fwd(q_att, k_att, v_bf, heads)
    w_mo, w_oo = landed(["w_mla_out", "w_o"], lse, "gather_wait_outs")
    w_mo = w_mo.reshape(1, hw, d)
    w_oo = w_oo.reshape(1, d, d)
    ym, mix = _mla_out_gate(o_bf, w_mo, z_g, b_gate, yc)
    h1, u2 = _residual_norm(mix, w_oo, xs, ln2_g)
    w_up, = landed(["w_ffn_up"], u2, "gather_wait_ffn_up")
    x_g, x_u, a_g, a_u, f = _ffn_up_act(u2, w_up, fcw8, ffn_conv_b, dff)
    w_dn, = landed(["w_ffn_down"], f, "gather_wait_ffn_down")
    w_dn = w_dn.reshape(1, dff, d)
    dy, dy_bf, loss_part = _mm_nn_loss(f, w_dn, h1, tgt, "mm_ffn_down_loss")

    g_dn = _mm_tn(f, dy_bf, 1, "mm_g_ffn_down").reshape(N_DEV, dff // N_DEV, d)
    rs_dn = _exchange_start([g_dn], False, "reduce_ffn_down_start")
    d_f = _mm_nt(dy_bf, w_dn, "mm_d_f", dep=rs_dn.token)
    d_xg, d_xu, dfw_g, dfw_u = _ffn_act_bwd(x_g, x_u, a_g, a_u, d_f, fcw8, dff)
    half = N_DEV // 2
    g_up = _mm_tn(u2, d_xg, half, "mm_g_ffn_up_gate", into=lax.empty((N_DEV, d, 2 * dff // N_DEV), BF))
    g_up = _mm_tn(u2, d_xu, half, "mm_g_ffn_up_up", into=g_up, blk0=half)
    rs_up = _exchange_start([g_up], False, "reduce_ffn_up_start")
    d_u2 = _mm_nt([d_xg, d_xu], w_up, "mm_d_u2", out_dtype=BF, dep=rs_up.token)
    d_h1, d_h1_bf, dg_ln2 = _rms_bwd(h1, d_u2, ln2_g, d, 0, "rms2_bwd", extra=dy, also_bf16=True)
    g_oo = _mm_tn(mix, d_h1_bf, 1, "mm_g_w_o").reshape(N_DEV, d // N_DEV, d)
    d_zga, d_zgb, d_yc, d_ym, dba, dbb = _d_mix_gate(d_h1_bf, w_oo, z_g, b_gate, yc, ym)
    g_co = _block(_mm_tn(p, d_yc, 1, "mm_g_conv_out")[0], N_DEV)
    g_mo = _mm_tn(o_bf, d_ym, 1, "mm_g_mla_out").reshape(N_DEV, hw // N_DEV, d)
    rs_mix = _exchange_start([g_oo, g_co, g_mo], False, "reduce_mixers_start")
    d_p = _mm_nt(d_yc, w_co, "mm_d_p", dep=rs_mix.token)
    d_o = _mm_nt(d_ym, w_mo, "mm_d_o", out_dtype=BF)
    d_zb, d_zc, d_zv, dcw = _conv_mix_bwd(z_a, d_p, cw8, conv)
    dq_att, dk_att, dv = _attn_bwd(q_att, k_att, v_bf, o, lse, d_o, heads, dep=rs_mix.token)
    d_q_raw, d_kv_raw, d_kr, dgains = _head_bwd(q_raw, kv_raw, z_a, kr_blk, cos, sin, gains, dq_att, dk_att, dv, heads)
    g_q2 = _mm_tn(qn, d_q_raw, 1, "mm_g_q")[0]
    g_qb = _block(jnp.concatenate([g_q2[:, :hw].reshape(ql, heads, NOPE),
                                   _unlay(g_q2[:, hw:].reshape(ql, heads, LANE))], axis=2).reshape(ql, heads * HEAD_QK), N_DEV)
    g_kv = _block(_mm_tn(kvn, d_kv_raw, 1, "mm_g_kv")[0], N_DEV)
    rs_qkv = _exchange_start([g_qb, g_kv], False, "reduce_qkv_start")
    d_ql, dg_qa = _mm_nt_rms_bwd(d_q_raw, w_q, z_a, q_a_g, conv3 // ql, "mm_d_q_lat", dep=rs_qkv.token)
    d_kvl, dg_kva = _mm_nt_rms_bwd(d_kv_raw, w_kv, z_a, kv_a_g, kv_off // kvl, "mm_d_kv_lat")
    d_z_a = jnp.concatenate([d_zb, d_zc, d_zv, d_ql, d_kr.astype(BF), jnp.zeros((s, kv_off - kr_off - LANE), BF),
                             d_kvl], axis=1)
    g_a = _mm_tn(d_z_a, u1, 1, "mm_g_w_a")[0]
    g_ga = _mm_tn(d_zga, u1, 1, "mm_g_w_ga")[0]
    g_gb = _mm_tn(d_zgb, u1, 1, "mm_g_w_gb")[0]
    g_in = jnp.concatenate([g_a[:kr_off], g_a[kv_off:kv_off + kvl], g_a[kr_off:kr_off + HALF],
                            g_a[kr_off + 2 * HALF:kr_off + 3 * HALF], g_ga, g_gb], axis=0).reshape(N_DEV, nin, d)
    rs_in = _exchange_start([g_in], False, "reduce_w_in_start")
    d_u1 = _mm_nn(d_z_a, w_a_t, "mm_d_u1_a", dep=rs_in.token)
    d_u1 = _mm_nn([d_zga, d_zgb], w_g_t, "mm_d_u1_g", add=d_u1)
    grad_x, dg_ln1 = _rms_bwd(xs, d_u1, ln1_g, d, 0, "rms1_bwd", extra=d_h1)

    summed = {}
    summed["w_ffn_down"], = _exchange_wait(rs_dn, [0], grad_x, "reduce_ffn_down_wait")
    summed["w_ffn_up"], = _exchange_wait(rs_up, [0], grad_x, "reduce_ffn_up_wait")
    summed["w_o"], summed["w_conv_out"], summed["w_mla_out"] = _exchange_wait(rs_mix, [0, 1, 2], grad_x, "reduce_mixers_wait")
    summed["w_q_b"], summed["w_kv_b"] = _exchange_wait(rs_qkv, [0, 1], grad_x, "reduce_qkv_wait")
    loc = locals()
    out = {}
    for k in rest:
        out[k] = _adamw(summed[k], big[k], loc["m_" + k][0], loc["v_" + k][0], "adamw_" + k)

    small = dict(ln1_g=dg_ln1[0:1], b_gate=jnp.concatenate([dba[0:1], dbb[0:1]], axis=1), q_a_g=dg_qa[0:1],
                 kv_a_g=dg_kva[0:1],
                 q_norm_g=jnp.concatenate([dgains[0:1], _unlay(dgains[1:2])], axis=1),
                 k_norm_g=jnp.concatenate([dgains[2:3], _unlay(dgains[3:4])], axis=1),
                 ln2_g=dg_ln2[0:1], ffn_conv_b=jnp.concatenate([dfw_g[3:4], dfw_u[3:4]], axis=1))
    small_names = list(small)
    extra = [dcw[0:3].reshape(1, -1), jnp.concatenate([dfw_g[0:3], dfw_u[0:3]], axis=1).reshape(1, -1),
             loss_part[0:1, 0:1]]
    flat = jnp.concatenate([small[k] for k in small_names] + extra, axis=1)
    n_flat = flat.shape[1]
    rows = -(-n_flat // (SUB * LANE)) * SUB
    flat = jnp.pad(flat, ((0, 0), (0, rows * LANE - n_flat))).reshape(rows, LANE)
    total = _sum_parts(_all_gather([flat], "gather_small", dep=[out[k][0] for k in rest])[0], "sum_small").reshape(1, rows * LANE)
    off = 0
    small_g = {}
    for k in small_names:
        small_g[k] = total[:, off:off + small[k].shape[1]]
        off += small[k].shape[1]
    me = 4 * lax.axis_index("x") + 2 * lax.axis_index("y") + lax.axis_index("c")
    cwn, fcwn = conv // N_DEV, 2 * dff // N_DEV
    g_cw = lax.dynamic_slice_in_dim(total[:, off:off + 3 * conv].reshape(3, conv), me * cwn, cwn, axis=1)
    off += 3 * conv
    g_fcw = lax.dynamic_slice_in_dim(total[:, off:off + 6 * dff].reshape(3, 2 * dff), me * fcwn, fcwn, axis=1)
    off += 6 * dff
    loss = total[0, off]

    summed["w_in"], = _exchange_wait(rs_in, [0], total, "reduce_w_in_wait")
    out["w_in"] = [r.T for r in _adamw(summed["w_in"], big["w_in"], m_w_in[0].T, v_w_in[0].T, "adamw_w_in",
                                       by_cols=True)]
    small_w = dict(ln1_g=ln1_g, b_gate=b_gate, q_a_g=q_a_g, kv_a_g=kv_a_g, q_norm_g=q_norm_g, k_norm_g=k_norm_g,
                   ln2_g=ln2_g, ffn_conv_b=ffn_conv_b, conv_w=conv_w[0].reshape(1, -1),
                   ffn_conv_w=ffn_conv_w[0].reshape(1, -1))
    small_g["conv_w"] = g_cw.reshape(1, -1)
    small_g["ffn_conv_w"] = g_fcw.reshape(1, -1)
    packed_names = list(small_w)

    def pack(get):
        vflat = jnp.concatenate([get(k).reshape(1, -1) for k in packed_names], axis=1)
        nr = -(-vflat.shape[1] // (SUB * LANE)) * SUB
        return jnp.pad(vflat, ((0, 0), (0, nr * LANE - vflat.shape[1])), constant_values=1.0).reshape(nr, LANE)

    res = _adamw(pack(lambda k: small_g[k])[None], pack(lambda k: small_w[k]), pack(lambda k: loc["m_" + k]),
                 pack(lambda k: loc["v_" + k]), "adamw_small")
    res = [r.reshape(1, -1) for r in res]
    off = 0
    for k in packed_names:
        shape = loc[k].shape
        size = small_w[k].shape[1]
        out[k] = [r[:, off:off + size].reshape(shape) for r in res]
        off += size
    for k in names:
        out[k] = [r[None] for r in out[k]]

    order = ["ln1_g", "w_in", "b_gate", "conv_w", "w_conv_out", "q_a_g", "w_q_b", "kv_a_g", "w_kv_b", "q_norm_g",
             "k_norm_g", "w_mla_out", "w_o", "ln2_g", "w_ffn_up", "ffn_conv_w", "ffn_conv_b", "w_ffn_down"]
    return (loss, grad_x[None], *[out[k][0] for k in order], *[out[k][1] for k in order],
            *[out[k][2] for k in order], *[out[k][3] for k in order])
```
